```python
import jax, jax.numpy as jnp
from jax import lax
import numpy as np

D_MODEL = 1024
BATCH = 8
SEQ = 2048
DEPTH = 2

MEM_LEN = 256
POOL_GROUPS = 4
POOL_GROUP_DIM = D_MODEL // 16
POOL_WIDTH = POOL_GROUPS * POOL_GROUP_DIM
POOL_WINDOWS = (2, 4, 8, 16)
FOX_HEADS = 8
FOX_HEAD_DIM = 64
FOX_WIDTH = FOX_HEADS * FOX_HEAD_DIM
Q_BLOCK = 128
SGU_GROUPS = 4
SGU_GROUP_DIM = D_MODEL // 16
SGU_WIDTH = SGU_GROUPS * SGU_GROUP_DIM
SGU_CHUNK = 128
N_BRANCH = 3
OFF_A = 0
OFF_Q = OFF_A + POOL_WIDTH
OFF_K = OFF_Q + FOX_WIDTH
OFF_V = OFF_K + FOX_WIDTH
OFF_F = OFF_V + FOX_WIDTH
OFF_C = OFF_F + FOX_HEADS
OFF_G = OFF_C + 2 * SGU_WIDTH
N_IN = OFF_G + N_BRANCH * D_MODEL
XATTN_HEADS = 4
XATTN_HEAD_DIM = D_MODEL // XATTN_HEADS
D_FF = 4 * D_MODEL
EPS = 1e-6
NEG = -1e30

kernel_name = "hybrid_pool_fox_sgu_gated_block"


def rmsnorm(x, g):
    xf = x.astype(jnp.float32)
    y = xf * lax.rsqrt(jnp.mean(xf * xf, axis=-1, keepdims=True) + EPS)
    return (y * g.astype(jnp.float32)).astype(x.dtype)


def pool_mixer(a, w, scale):
    B, S, _ = a.shape
    af = a.astype(jnp.float32)
    c = jnp.pad(jnp.cumsum(af, axis=1), ((0, 0), (1, 0), (0, 0)))
    t = jnp.arange(S)
    outs = []
    for gi, win in enumerate(POOL_WINDOWS):
        sl = slice(gi * POOL_GROUP_DIM, (gi + 1) * POOL_GROUP_DIM)
        cg = c[..., sl]
        lo = jnp.take(cg, jnp.maximum(t + 1 - win, 0), axis=1)
        cnt = jnp.minimum(t + 1, win).astype(jnp.float32)[None, :, None]
        outs.append((cg[:, 1:] - lo) / cnt - af[..., sl])
    d = jnp.stack(outs, axis=2).astype(a.dtype)
    y = jnp.einsum('bsgc,gcd->bsgd', d, w).reshape(B, S, POOL_WIDTH)
    return y * scale


def forgetting_attention(q, k, v, logf):
    S = q.shape[1]
    F = jnp.cumsum(logf, axis=1).transpose(0, 2, 1)
    scale = FOX_HEAD_DIM ** -0.5
    outs = []
    for i in range(S // Q_BLOCK):
        q0 = i * Q_BLOCK
        kend = q0 + Q_BLOCK
        s = jnp.einsum('bqhd,bkhd->bhqk', q[:, q0:kend], k[:, :kend]).astype(jnp.float32) * scale
        s = s + F[:, :, q0:kend, None] - F[:, :, None, :kend]
        mask = (q0 + jnp.arange(Q_BLOCK))[:, None] >= jnp.arange(kend)[None, :]
        s = jnp.where(mask, s, NEG)
        p = jax.nn.softmax(s, axis=-1).astype(v.dtype)
        outs.append(jnp.einsum('bhqk,bkhd->bqhd', p, v[:, :kend]))
    return jnp.concatenate(outs, axis=1)


def spatial_gating(z, norm_g, ws, b):
    B, S, _ = z.shape
    u, v = z[..., :SGU_WIDTH], z[..., SGU_WIDTH:]
    v = rmsnorm(v, norm_g)
    vc = v.reshape(B, S // SGU_CHUNK, SGU_CHUNK, SGU_GROUPS, SGU_GROUP_DIM)
    causal = jnp.tril(jnp.ones((SGU_CHUNK, SGU_CHUNK), dtype=ws.dtype))
    w = ws * causal[None]
    mixed = jnp.einsum('gts,bcsgd->bctgd', w, vc) + b.T[None, None, :, :, None]
    return u * mixed.reshape(B, S, SGU_WIDTH)


def _fwd_setup_inputs(seed: int = 0) -> dict:
    key = jax.random.key(seed)
    ks = jax.random.split(key, 24)
    L, D = DEPTH, D_MODEL
    nrm = lambda k, shape, fan_in: jax.random.normal(k, shape, jnp.float32) * (fan_in ** -0.5)
    gain = lambda k, shape: 1.0 + 0.05 * jax.random.normal(k, shape, jnp.float32)
    b_forget = jnp.linspace(1.0, 6.0, FOX_HEADS, dtype=jnp.float32)[None, :] + 0.1 * jax.random.normal(ks[3], (L, FOX_HEADS), jnp.float32)
    return {
        "x": jax.random.normal(ks[0], (BATCH, SEQ, D), jnp.float32),
        "mem": jax.random.normal(ks[1], (BATCH, MEM_LEN, D), jnp.float32),
        "norm_mix_g": gain(ks[2], (L, D)),
        "w_in": nrm(ks[4], (L, D, N_IN), D),
        "b_forget": b_forget,
        "pool_w": nrm(ks[5], (L, POOL_GROUPS, POOL_GROUP_DIM, POOL_GROUP_DIM), POOL_GROUP_DIM),
        "pool_scale": gain(ks[6], (L, POOL_WIDTH)),
        "sgu_norm_g": gain(ks[7], (L, SGU_WIDTH)),
        "sgu_w": nrm(ks[8], (L, SGU_GROUPS, SGU_CHUNK, SGU_CHUNK), SGU_CHUNK),
        "sgu_b": gain(ks[9], (L, SGU_GROUPS, SGU_CHUNK)),
        "w_branch_a": nrm(ks[10], (L, POOL_WIDTH, D), POOL_WIDTH),
        "w_branch_b": nrm(ks[11], (L, FOX_WIDTH, D), FOX_WIDTH),
        "w_branch_c": nrm(ks[12], (L, SGU_WIDTH, D), SGU_WIDTH),
        "b_gate": 0.01 * jax.random.normal(ks[13], (L, N_BRANCH * D), jnp.float32),
        "w_out": nrm(ks[14], (L, D, D), D),
        "norm_xattn_g": gain(ks[15], (L, D)),
        "norm_mem_g": gain(ks[16], (L, D)),
        "w_xq": nrm(ks[17], (L, D, D), D),
        "w_xkv": nrm(ks[18], (L, D, 2 * D), D),
        "w_xo": nrm(ks[19], (L, D, D), D),
        "norm_ffn_g": gain(ks[20], (L, D)),
        "w_ff1": nrm(ks[21], (L, D, D_FF), D),
        "w_ff2": nrm(ks[22], (L, D_FF, D), D_FF),
        "final_norm_g": gain(ks[23], (D,)),
    }


def _fwd_reference(x, mem, norm_mix_g, w_in, b_forget, pool_w, pool_scale, sgu_norm_g, sgu_w, sgu_b,
              w_branch_a, w_branch_b, w_branch_c, b_gate, w_out, norm_xattn_g, norm_mem_g,
              w_xq, w_xkv, w_xo, norm_ffn_g, w_ff1, w_ff2, final_norm_g):
    B, S, D = x.shape
    M = mem.shape[1]
    for l in range(DEPTH):
        h = rmsnorm(x, norm_mix_g[l])
        proj = h @ w_in[l]
        a = proj[..., OFF_A:OFF_Q]
        q = proj[..., OFF_Q:OFF_K].reshape(B, S, FOX_HEADS, FOX_HEAD_DIM)
        k = proj[..., OFF_K:OFF_V].reshape(B, S, FOX_HEADS, FOX_HEAD_DIM)
        v = proj[..., OFF_V:OFF_F].reshape(B, S, FOX_HEADS, FOX_HEAD_DIM)
        logf = jax.nn.log_sigmoid(proj[..., OFF_F:OFF_C].astype(jnp.float32) + b_forget[l].astype(jnp.float32))
        zc = jax.nn.gelu(proj[..., OFF_C:OFF_G])
        gates = jax.nn.sigmoid(proj[..., OFF_G:] + b_gate[l])

        y_a = pool_mixer(a, pool_w[l], pool_scale[l]) @ w_branch_a[l]
        y_b = forgetting_attention(q, k, v, logf).reshape(B, S, FOX_WIDTH) @ w_branch_b[l]
        y_c = spatial_gating(zc, sgu_norm_g[l], sgu_w[l], sgu_b[l]) @ w_branch_c[l]
        merged = gates[..., :D] * y_a + gates[..., D:2 * D] * y_b + gates[..., 2 * D:] * y_c
        x = x + merged @ w_out[l]

        hx = rmsnorm(x, norm_xattn_g[l])
        hm = rmsnorm(mem, norm_mem_g[l])
        xq = (hx @ w_xq[l]).reshape(B, S, XATTN_HEADS, XATTN_HEAD_DIM)
        kv = hm @ w_xkv[l]
        xk = kv[..., :D].reshape(B, M, XATTN_HEADS, XATTN_HEAD_DIM)
        xv = kv[..., D:].reshape(B, M, XATTN_HEADS, XATTN_HEAD_DIM)
        s = jnp.einsum('bqhd,bkhd->bhqk', xq, xk).astype(jnp.float32) * (XATTN_HEAD_DIM ** -0.5)
        p = jax.nn.softmax(s, axis=-1).astype(xv.dtype)
        o = jnp.einsum('bhqk,bkhd->bqhd', p, xv).reshape(B, S, D)
        x = x + o @ w_xo[l]

        hf = rmsnorm(x, norm_ffn_g[l])
        x = x + jnp.square(jax.nn.relu(hf @ w_ff1[l])) @ w_ff2[l]
    return rmsnorm(x, final_norm_g)


import jax as _jax
import jax.numpy as _jnp

TWIN_FORMAT = 'train_step'
FWD_PARAMS = ['x', 'mem', 'norm_mix_g', 'w_in', 'b_forget', 'pool_w', 'pool_scale', 'sgu_norm_g', 'sgu_w', 'sgu_b', 'w_branch_a', 'w_branch_b', 'w_branch_c', 'b_gate', 'w_out', 'norm_xattn_g', 'norm_mem_g', 'w_xq', 'w_xkv', 'w_xo', 'norm_ffn_g', 'w_ff1', 'w_ff2', 'final_norm_g']
TWIN_WEIGHTS = ['norm_mix_g', 'w_in', 'b_forget', 'pool_w', 'pool_scale', 'sgu_norm_g', 'sgu_w', 'sgu_b', 'w_branch_a', 'w_branch_b', 'w_branch_c', 'b_gate', 'w_out', 'norm_xattn_g', 'norm_mem_g', 'w_xq', 'w_xkv', 'w_xo', 'norm_ffn_g', 'w_ff1', 'w_ff2', 'final_norm_g']
TWIN_DIFF_INPUT = 'x'
TWIN_INPUTS = ['x', 'mem', 'norm_mix_g', 'w_in', 'b_forget', 'pool_w', 'pool_scale', 'sgu_norm_g', 'sgu_w', 'sgu_b', 'w_branch_a', 'w_branch_b', 'w_branch_c', 'b_gate', 'w_out', 'norm_xattn_g', 'norm_mem_g', 'w_xq', 'w_xkv', 'w_xo', 'norm_ffn_g', 'w_ff1', 'w_ff2', 'final_norm_g', 'loss_target', 'm_norm_mix_g', 'm_w_in', 'm_b_forget', 'm_pool_w', 'm_pool_scale', 'm_sgu_norm_g', 'm_sgu_w', 'm_sgu_b', 'm_w_branch_a', 'm_w_branch_b', 'm_w_branch_c', 'm_b_gate', 'm_w_out', 'm_norm_xattn_g', 'm_norm_mem_g', 'm_w_xq', 'm_w_xkv', 'm_w_xo', 'm_norm_ffn_g', 'm_w_ff1', 'm_w_ff2', 'm_final_norm_g', 'v_norm_mix_g', 'v_w_in', 'v_b_forget', 'v_pool_w', 'v_pool_scale', 'v_sgu_norm_g', 'v_sgu_w', 'v_sgu_b', 'v_w_branch_a', 'v_w_branch_b', 'v_w_branch_c', 'v_b_gate', 'v_w_out', 'v_norm_xattn_g', 'v_norm_mem_g', 'v_w_xq', 'v_w_xkv', 'v_w_xo', 'v_norm_ffn_g', 'v_w_ff1', 'v_w_ff2', 'v_final_norm_g']
TWIN_OUTPUTS = ['loss', 'grad_x', 'grad_norm_mix_g', 'grad_w_in', 'grad_b_forget', 'grad_pool_w', 'grad_pool_scale', 'grad_sgu_norm_g', 'grad_sgu_w', 'grad_sgu_b', 'grad_w_branch_a', 'grad_w_branch_b', 'grad_w_branch_c', 'grad_b_gate', 'grad_w_out', 'grad_norm_xattn_g', 'grad_norm_mem_g', 'grad_w_xq', 'grad_w_xkv', 'grad_w_xo', 'grad_norm_ffn_g', 'grad_w_ff1', 'grad_w_ff2', 'grad_final_norm_g', 'delta_norm_mix_g', 'delta_w_in', 'delta_b_forget', 'delta_pool_w', 'delta_pool_scale', 'delta_sgu_norm_g', 'delta_sgu_w', 'delta_sgu_b', 'delta_w_branch_a', 'delta_w_branch_b', 'delta_w_branch_c', 'delta_b_gate', 'delta_w_out', 'delta_norm_xattn_g', 'delta_norm_mem_g', 'delta_w_xq', 'delta_w_xkv', 'delta_w_xo', 'delta_norm_ffn_g', 'delta_w_ff1', 'delta_w_ff2', 'delta_final_norm_g', 'new_m_norm_mix_g', 'new_m_w_in', 'new_m_b_forget', 'new_m_pool_w', 'new_m_pool_scale', 'new_m_sgu_norm_g', 'new_m_sgu_w', 'new_m_sgu_b', 'new_m_w_branch_a', 'new_m_w_branch_b', 'new_m_w_branch_c', 'new_m_b_gate', 'new_m_w_out', 'new_m_norm_xattn_g', 'new_m_norm_mem_g', 'new_m_w_xq', 'new_m_w_xkv', 'new_m_w_xo', 'new_m_norm_ffn_g', 'new_m_w_ff1', 'new_m_w_ff2', 'new_m_final_norm_g', 'new_v_norm_mix_g', 'new_v_w_in', 'new_v_b_forget', 'new_v_pool_w', 'new_v_pool_scale', 'new_v_sgu_norm_g', 'new_v_sgu_w', 'new_v_sgu_b', 'new_v_w_branch_a', 'new_v_w_branch_b', 'new_v_w_branch_c', 'new_v_b_gate', 'new_v_w_out', 'new_v_norm_xattn_g', 'new_v_norm_mem_g', 'new_v_w_xq', 'new_v_w_xkv', 'new_v_w_xo', 'new_v_norm_ffn_g', 'new_v_w_ff1', 'new_v_w_ff2', 'new_v_final_norm_g']
TWIN_LEAF_KINDS = {'loss': 'loss', 'grad_x': 'grad_x', 'grad_norm_mix_g': 'grad_w', 'grad_w_in': 'grad_w', 'grad_b_forget': 'grad_w', 'grad_pool_w': 'grad_w', 'grad_pool_scale': 'grad_w', 'grad_sgu_norm_g': 'grad_w', 'grad_sgu_w': 'grad_w', 'grad_sgu_b': 'grad_w', 'grad_w_branch_a': 'grad_w', 'grad_w_branch_b': 'grad_w', 'grad_w_branch_c': 'grad_w', 'grad_b_gate': 'grad_w', 'grad_w_out': 'grad_w', 'grad_norm_xattn_g': 'grad_w', 'grad_norm_mem_g': 'grad_w', 'grad_w_xq': 'grad_w', 'grad_w_xkv': 'grad_w', 'grad_w_xo': 'grad_w', 'grad_norm_ffn_g': 'grad_w', 'grad_w_ff1': 'grad_w', 'grad_w_ff2': 'grad_w', 'grad_final_norm_g': 'grad_w', 'delta_norm_mix_g': 'delta_w', 'delta_w_in': 'delta_w', 'delta_b_forget': 'delta_w', 'delta_pool_w': 'delta_w', 'delta_pool_scale': 'delta_w', 'delta_sgu_norm_g': 'delta_w', 'delta_sgu_w': 'delta_w', 'delta_sgu_b': 'delta_w', 'delta_w_branch_a': 'delta_w', 'delta_w_branch_b': 'delta_w', 'delta_w_branch_c': 'delta_w', 'delta_b_gate': 'delta_w', 'delta_w_out': 'delta_w', 'delta_norm_xattn_g': 'delta_w', 'delta_norm_mem_g': 'delta_w', 'delta_w_xq': 'delta_w', 'delta_w_xkv': 'delta_w', 'delta_w_xo': 'delta_w', 'delta_norm_ffn_g': 'delta_w', 'delta_w_ff1': 'delta_w', 'delta_w_ff2': 'delta_w', 'delta_final_norm_g': 'delta_w', 'new_m_norm_mix_g': 'new_m', 'new_m_w_in': 'new_m', 'new_m_b_forget': 'new_m', 'new_m_pool_w': 'new_m', 'new_m_pool_scale': 'new_m', 'new_m_sgu_norm_g': 'new_m', 'new_m_sgu_w': 'new_m', 'new_m_sgu_b': 'new_m', 'new_m_w_branch_a': 'new_m', 'new_m_w_branch_b': 'new_m', 'new_m_w_branch_c': 'new_m', 'new_m_b_gate': 'new_m', 'new_m_w_out': 'new_m', 'new_m_norm_xattn_g': 'new_m', 'new_m_norm_mem_g': 'new_m', 'new_m_w_xq': 'new_m', 'new_m_w_xkv': 'new_m', 'new_m_w_xo': 'new_m', 'new_m_norm_ffn_g': 'new_m', 'new_m_w_ff1': 'new_m', 'new_m_w_ff2': 'new_m', 'new_m_final_norm_g': 'new_m', 'new_v_norm_mix_g': 'new_v', 'new_v_w_in': 'new_v', 'new_v_b_forget': 'new_v', 'new_v_pool_w': 'new_v', 'new_v_pool_scale': 'new_v', 'new_v_sgu_norm_g': 'new_v', 'new_v_sgu_w': 'new_v', 'new_v_sgu_b': 'new_v', 'new_v_w_branch_a': 'new_v', 'new_v_w_branch_b': 'new_v', 'new_v_w_branch_c': 'new_v', 'new_v_b_gate': 'new_v', 'new_v_w_out': 'new_v', 'new_v_norm_xattn_g': 'new_v', 'new_v_norm_mem_g': 'new_v', 'new_v_w_xq': 'new_v', 'new_v_w_xkv': 'new_v', 'new_v_w_xo': 'new_v', 'new_v_norm_ffn_g': 'new_v', 'new_v_w_ff1': 'new_v', 'new_v_w_ff2': 'new_v', 'new_v_final_norm_g': 'new_v'}


def _forward(args):
    return _fwd_reference(*[args[k] for k in FWD_PARAMS])


def _output_shape():
    out = _jax.eval_shape(lambda: _forward(_fwd_setup_inputs(0)))
    return out.shape, out.dtype

N_MICROBATCH = 1
ADAM_LR = 0.001
ADAM_B1 = 0.9
ADAM_B2 = 0.999
ADAM_EPS = 1e-08
ADAM_WD = 0.01
ADAM_STEP = 10
PER_EXAMPLE_BATCH_AXIS = {'x': 0, 'mem': 0, 'loss_target': 0}
SHARED_INPUTS = []
_WEIGHT_DTYPES = {'norm_mix_g': _jnp.float32, 'w_in': _jnp.float32, 'b_forget': _jnp.float32, 'pool_w': _jnp.float32, 'pool_scale': _jnp.float32, 'sgu_norm_g': _jnp.float32, 'sgu_w': _jnp.float32, 'sgu_b': _jnp.float32, 'w_branch_a': _jnp.float32, 'w_branch_b': _jnp.float32, 'w_branch_c': _jnp.float32, 'b_gate': _jnp.float32, 'w_out': _jnp.float32, 'norm_xattn_g': _jnp.float32, 'norm_mem_g': _jnp.float32, 'w_xq': _jnp.float32, 'w_xkv': _jnp.float32, 'w_xo': _jnp.float32, 'norm_ffn_g': _jnp.float32, 'w_ff1': _jnp.float32, 'w_ff2': _jnp.float32, 'final_norm_g': _jnp.float32}
MOMENT_SCALE = {'norm_mix_g': 8.374147e-02, 'w_in': 3.569164e-02, 'b_forget': 8.865622e-02, 'pool_w': 9.429928e-02, 'pool_scale': 9.970543e-02, 'sgu_norm_g': 5.133015e-02, 'sgu_w': 3.498872e-02, 'sgu_b': 5.059290e-02, 'w_branch_a': 4.805185e-02, 'w_branch_b': 2.718456e-02, 'w_branch_c': 5.403841e-02, 'b_gate': 1.733680e-02, 'w_out': 7.797234e-02, 'norm_xattn_g': 1.032084e-02, 'norm_mem_g': 1.809263e-02, 'w_xq': 1.067545e-02, 'w_xkv': 1.175844e-02, 'w_xo': 1.281520e-02, 'norm_ffn_g': 1.013795e-01, 'w_ff1': 5.258082e-02, 'w_ff2': 1.380494e-01, 'final_norm_g': 1.626690e+01}


def _to_microbatches(a, axis):
    t = _jnp.moveaxis(a, axis, 0)
    t = t.reshape((N_MICROBATCH, t.shape[0] // N_MICROBATCH) + t.shape[1:])
    return _jnp.moveaxis(t, 1, axis + 1)


def setup_inputs(seed: int = 0) -> dict:
    inp = _fwd_setup_inputs(seed)
    key = _jax.random.fold_in(_jax.random.key(seed), 7919)
    shape, _ = _output_shape()
    out = dict(inp)
    out["loss_target"] = _jax.random.normal(_jax.random.fold_in(key, 0), shape, _jnp.float32)
    for i, name in enumerate(TWIN_WEIGHTS):
        w = inp[name].astype(_jnp.float32)
        if MOMENT_SCALE is None:
            s = _jnp.sqrt(_jnp.mean(_jnp.square(w)) + 1e-30)
        else:
            s = MOMENT_SCALE[name]
        km, kv = _jax.random.split(_jax.random.fold_in(key, i + 1))
        out[name] = w
        out["m_" + name] = s * _jax.random.normal(km, w.shape, _jnp.float32)
        out["v_" + name] = (s * s) * _jax.random.uniform(kv, w.shape, _jnp.float32, 0.5, 1.5)
    if N_MICROBATCH > 1:
        for name, axis in PER_EXAMPLE_BATCH_AXIS.items():
            out[name] = _to_microbatches(out[name], axis)
    return {'x': out['x'], 'mem': out['mem'], 'norm_mix_g': out['norm_mix_g'], 'w_in': out['w_in'], 'b_forget': out['b_forget'], 'pool_w': out['pool_w'], 'pool_scale': out['pool_scale'], 'sgu_norm_g': out['sgu_norm_g'], 'sgu_w': out['sgu_w'], 'sgu_b': out['sgu_b'], 'w_branch_a': out['w_branch_a'], 'w_branch_b': out['w_branch_b'], 'w_branch_c': out['w_branch_c'], 'b_gate': out['b_gate'], 'w_out': out['w_out'], 'norm_xattn_g': out['norm_xattn_g'], 'norm_mem_g': out['norm_mem_g'], 'w_xq': out['w_xq'], 'w_xkv': out['w_xkv'], 'w_xo': out['w_xo'], 'norm_ffn_g': out['norm_ffn_g'], 'w_ff1': out['w_ff1'], 'w_ff2': out['w_ff2'], 'final_norm_g': out['final_norm_g'], 'loss_target': out['loss_target'], 'm_norm_mix_g': out['m_norm_mix_g'], 'm_w_in': out['m_w_in'], 'm_b_forget': out['m_b_forget'], 'm_pool_w': out['m_pool_w'], 'm_pool_scale': out['m_pool_scale'], 'm_sgu_norm_g': out['m_sgu_norm_g'], 'm_sgu_w': out['m_sgu_w'], 'm_sgu_b': out['m_sgu_b'], 'm_w_branch_a': out['m_w_branch_a'], 'm_w_branch_b': out['m_w_branch_b'], 'm_w_branch_c': out['m_w_branch_c'], 'm_b_gate': out['m_b_gate'], 'm_w_out': out['m_w_out'], 'm_norm_xattn_g': out['m_norm_xattn_g'], 'm_norm_mem_g': out['m_norm_mem_g'], 'm_w_xq': out['m_w_xq'], 'm_w_xkv': out['m_w_xkv'], 'm_w_xo': out['m_w_xo'], 'm_norm_ffn_g': out['m_norm_ffn_g'], 'm_w_ff1': out['m_w_ff1'], 'm_w_ff2': out['m_w_ff2'], 'm_final_norm_g': out['m_final_norm_g'], 'v_norm_mix_g': out['v_norm_mix_g'], 'v_w_in': out['v_w_in'], 'v_b_forget': out['v_b_forget'], 'v_pool_w': out['v_pool_w'], 'v_pool_scale': out['v_pool_scale'], 'v_sgu_norm_g': out['v_sgu_norm_g'], 'v_sgu_w': out['v_sgu_w'], 'v_sgu_b': out['v_sgu_b'], 'v_w_branch_a': out['v_w_branch_a'], 'v_w_branch_b': out['v_w_branch_b'], 'v_w_branch_c': out['v_w_branch_c'], 'v_b_gate': out['v_b_gate'], 'v_w_out': out['v_w_out'], 'v_norm_xattn_g': out['v_norm_xattn_g'], 'v_norm_mem_g': out['v_norm_mem_g'], 'v_w_xq': out['v_w_xq'], 'v_w_xkv': out['v_w_xkv'], 'v_w_xo': out['v_w_xo'], 'v_norm_ffn_g': out['v_norm_ffn_g'], 'v_w_ff1': out['v_w_ff1'], 'v_w_ff2': out['v_w_ff2'], 'v_final_norm_g': out['v_final_norm_g']}


def _loss(weights, diff, rest, loss_target):
    with _jax.named_scope("forward"):
        args = {**rest, TWIN_DIFF_INPUT: diff, **{k: w.astype(_WEIGHT_DTYPES[k]) for k, w in weights.items()}}
        y = _forward(args)
    with _jax.named_scope("loss_head"):
        err = _jnp.square(y.astype(_jnp.float32) - loss_target)
        return 0.5 * _jnp.sum(_jnp.mean(err, axis=-1)) if err.ndim else 0.5 * err


def _adamw(w, g, m, v):
    m = ADAM_B1 * m + (1.0 - ADAM_B1) * g
    v = ADAM_B2 * v + (1.0 - ADAM_B2) * _jnp.square(g)
    m_hat = m / (1.0 - ADAM_B1 ** ADAM_STEP)
    v_hat = v / (1.0 - ADAM_B2 ** ADAM_STEP)
    delta = -ADAM_LR * (m_hat / (_jnp.sqrt(v_hat) + ADAM_EPS) + ADAM_WD * w)
    return delta, m, v


def reference(x, mem, norm_mix_g, w_in, b_forget, pool_w, pool_scale, sgu_norm_g, sgu_w, sgu_b, w_branch_a, w_branch_b, w_branch_c, b_gate, w_out, norm_xattn_g, norm_mem_g, w_xq, w_xkv, w_xo, norm_ffn_g, w_ff1, w_ff2, final_norm_g, loss_target, m_norm_mix_g, m_w_in, m_b_forget, m_pool_w, m_pool_scale, m_sgu_norm_g, m_sgu_w, m_sgu_b, m_w_branch_a, m_w_branch_b, m_w_branch_c, m_b_gate, m_w_out, m_norm_xattn_g, m_norm_mem_g, m_w_xq, m_w_xkv, m_w_xo, m_norm_ffn_g, m_w_ff1, m_w_ff2, m_final_norm_g, v_norm_mix_g, v_w_in, v_b_forget, v_pool_w, v_pool_scale, v_sgu_norm_g, v_sgu_w, v_sgu_b, v_w_branch_a, v_w_branch_b, v_w_branch_c, v_b_gate, v_w_out, v_norm_xattn_g, v_norm_mem_g, v_w_xq, v_w_xkv, v_w_xo, v_norm_ffn_g, v_w_ff1, v_w_ff2, v_final_norm_g):
    given = dict(x=x, mem=mem, norm_mix_g=norm_mix_g, w_in=w_in, b_forget=b_forget, pool_w=pool_w, pool_scale=pool_scale, sgu_norm_g=sgu_norm_g, sgu_w=sgu_w, sgu_b=sgu_b, w_branch_a=w_branch_a, w_branch_b=w_branch_b, w_branch_c=w_branch_c, b_gate=b_gate, w_out=w_out, norm_xattn_g=norm_xattn_g, norm_mem_g=norm_mem_g, w_xq=w_xq, w_xkv=w_xkv, w_xo=w_xo, norm_ffn_g=norm_ffn_g, w_ff1=w_ff1, w_ff2=w_ff2, final_norm_g=final_norm_g, loss_target=loss_target, m_norm_mix_g=m_norm_mix_g, m_w_in=m_w_in, m_b_forget=m_b_forget, m_pool_w=m_pool_w, m_pool_scale=m_pool_scale, m_sgu_norm_g=m_sgu_norm_g, m_sgu_w=m_sgu_w, m_sgu_b=m_sgu_b, m_w_branch_a=m_w_branch_a, m_w_branch_b=m_w_branch_b, m_w_branch_c=m_w_branch_c, m_b_gate=m_b_gate, m_w_out=m_w_out, m_norm_xattn_g=m_norm_xattn_g, m_norm_mem_g=m_norm_mem_g, m_w_xq=m_w_xq, m_w_xkv=m_w_xkv, m_w_xo=m_w_xo, m_norm_ffn_g=m_norm_ffn_g, m_w_ff1=m_w_ff1, m_w_ff2=m_w_ff2, m_final_norm_g=m_final_norm_g, v_norm_mix_g=v_norm_mix_g, v_w_in=v_w_in, v_b_forget=v_b_forget, v_pool_w=v_pool_w, v_pool_scale=v_pool_scale, v_sgu_norm_g=v_sgu_norm_g, v_sgu_w=v_sgu_w, v_sgu_b=v_sgu_b, v_w_branch_a=v_w_branch_a, v_w_branch_b=v_w_branch_b, v_w_branch_c=v_w_branch_c, v_b_gate=v_b_gate, v_w_out=v_w_out, v_norm_xattn_g=v_norm_xattn_g, v_norm_mem_g=v_norm_mem_g, v_w_xq=v_w_xq, v_w_xkv=v_w_xkv, v_w_xo=v_w_xo, v_norm_ffn_g=v_norm_ffn_g, v_w_ff1=v_w_ff1, v_w_ff2=v_w_ff2, v_final_norm_g=v_final_norm_g)
    weights = {n: given[n] for n in TWIN_WEIGHTS}
    shared = {n: given[n] for n in SHARED_INPUTS}
    per_example = {n: given[n] for n in ['x', 'mem']}
    grad_fn = _jax.value_and_grad(_loss, argnums=(0, 1))

    def one_microbatch(ex, loss_target):
        ex = dict(ex)
        diff = ex.pop(TWIN_DIFF_INPUT)
        return grad_fn(weights, diff, {**shared, **ex}, loss_target)

    if N_MICROBATCH == 1:
        loss, (grad_w, grad_x) = one_microbatch(per_example, given["loss_target"])
    else:
        def body(carry, xs):
            loss_sum, grad_sum = carry
            l_k, (gw_k, gx_k) = one_microbatch(xs[0], xs[1])
            with _jax.named_scope("update"):
                return (loss_sum + l_k, _jax.tree.map(_jnp.add, grad_sum, gw_k)), gx_k

        init = (_jnp.zeros((), _jnp.float32), _jax.tree.map(_jnp.zeros_like, weights))
        (loss, grad_w), grad_x = _jax.lax.scan(body, init, (per_example, given["loss_target"]))
    with _jax.named_scope("update"):
        delta_w, new_m, new_v = {}, {}, {}
        for n in TWIN_WEIGHTS:
            delta_w[n], new_m[n], new_v[n] = _adamw(weights[n], grad_w[n], given["m_" + n], given["v_" + n])
    return (loss, grad_x, *[grad_w[n] for n in TWIN_WEIGHTS], *[delta_w[n] for n in TWIN_WEIGHTS],
            *[new_m[n] for n in TWIN_WEIGHTS], *[new_v[n] for n in TWIN_WEIGHTS])
```

```python
import functools
import math

import jax
import jax.numpy as jnp
from jax import lax
from jax.experimental import pallas as pl
from jax.experimental.pallas import tpu as pltpu

F32 = jnp.float32
BF16 = jnp.bfloat16
MESH = pl.DeviceIdType.MESH

N_DEV = 8
D = 1024
DEPTH = 2
EPS = 1e-6
NEG = -1e30
POOL_W = 256
FOX_H = 8
FOX_DH = 64
FOX_W = 512
SGU_W = 256
SGU_CHUNK = 128
XH = 4
XDH = 256
D_FF = 4096
N_IN = 5384
R_OFF_C = 1800
OFF_A, OFF_Q, OFF_K, OFF_V, OFF_F, OFF_C, OFF_G, NP = 0, 256, 768, 1280, 1792, 2048, 2560, 5632
F_PAD = OFF_C - OFF_F

ADAM_LR = 0.001
ADAM_B1 = 0.9
ADAM_B2 = 0.999
ADAM_EPS = 1e-08
ADAM_WD = 0.01
ADAM_STEP = 10

VMEM_LIMIT = 48 * 1024 * 1024


def _tile(n, pref):
    t = min(n, pref)
    while n % t:
        t -= 128
    assert t > 0, (n, pref)
    return t


def _params(sem=None):
    return pltpu.CompilerParams(dimension_semantics=sem, vmem_limit_bytes=VMEM_LIMIT)


def _dot(a, b, ca, cb):
    return lax.dot_general(a, b, (((ca,), (cb,)), ((), ())), preferred_element_type=F32)


def _sigmoid(z):
    return 1.0 / (1.0 + jnp.exp(-z))


_GELU_K = math.sqrt(2.0 / math.pi)
_GELU_C = 0.044715


def _gelu(x):
    return 0.5 * x * (1.0 + jnp.tanh(_GELU_K * (x + _GELU_C * x * x * x)))


def _gelu_grad(x):
    t = jnp.tanh(_GELU_K * (x + _GELU_C * x * x * x))
    return 0.5 * (1.0 + t) + 0.5 * x * (1.0 - t * t) * _GELU_K * (1.0 + 3.0 * _GELU_C * x * x)


def _mm(a, b, *, ta=False, tb=False, out_dtype=F32, tm=512, tn=512, tk=1024, name):
    M, K = (a.shape[1], a.shape[0]) if ta else a.shape
    N = b.shape[0] if tb else b.shape[1]
    assert (b.shape[1] if tb else b.shape[0]) == K
    tm, tn, tk = _tile(M, tm), _tile(N, tn), _tile(K, tk)
    nk = K // tk
    ca, cb = (0 if ta else 1), (1 if tb else 0)

    def body(a_ref, b_ref, o_ref, acc_ref):
        part = _dot(a_ref[...].astype(BF16), b_ref[...].astype(BF16), ca, cb)
        if nk == 1:
            o_ref[...] = part.astype(out_dtype)
        else:
            k = pl.program_id(2)

            @pl.when(k == 0)
            def _():
                acc_ref[...] = part

            @pl.when(k > 0)
            def _():
                acc_ref[...] += part

            @pl.when(k == nk - 1)
            def _():
                o_ref[...] = acc_ref[...].astype(out_dtype)

    a_spec = pl.BlockSpec((tk, tm), lambda i, j, k: (k, i)) if ta else pl.BlockSpec((tm, tk), lambda i, j, k: (i, k))
    b_spec = pl.BlockSpec((tn, tk), lambda i, j, k: (j, k)) if tb else pl.BlockSpec((tk, tn), lambda i, j, k: (k, j))
    return pl.pallas_call(
        body,
        name=name,
        grid=(M // tm, N // tn, nk),
        in_specs=[a_spec, b_spec],
        out_specs=pl.BlockSpec((tm, tn), lambda i, j, k: (i, j)),
        out_shape=jax.ShapeDtypeStruct((M, N), out_dtype),
        scratch_shapes=[pltpu.VMEM((tm, tn), F32)],
        compiler_params=_params(("parallel", "parallel", "arbitrary")),
    )(a, b)


def _rms_fwd(x, g, *, name):
    R, C = x.shape
    tm = _tile(R, 256)

    def body(x_ref, g_ref, h_ref):
        xv = x_ref[...]
        r = lax.rsqrt(jnp.mean(xv * xv, axis=-1, keepdims=True) + EPS)
        h_ref[...] = (xv * r * g_ref[...]).astype(BF16)

    return pl.pallas_call(
        body,
        name=name,
        grid=(R // tm,),
        in_specs=[pl.BlockSpec((tm, C), lambda i: (i, 0)), pl.BlockSpec((1, C), lambda i: (0, 0))],
        out_specs=pl.BlockSpec((tm, C), lambda i: (i, 0)),
        out_shape=jax.ShapeDtypeStruct((R, C), BF16),
        compiler_params=_params(("parallel",)),
    )(x, g.reshape(1, C))


def _rms_bwd(x, g, dh, dres, *, name):
    R, C = x.shape
    tm = _tile(R, 256)

    def body(x_ref, g_ref, dh_ref, dres_ref, dx_ref, dg_ref):
        xv = x_ref[...]
        r = lax.rsqrt(jnp.mean(xv * xv, axis=-1, keepdims=True) + EPS)
        xn = xv * r
        dh_v = dh_ref[...].astype(F32)
        dxn = dh_v * g_ref[...]
        dx_ref[...] = r * (dxn - xn * jnp.mean(dxn * xn, axis=-1, keepdims=True)) + dres_ref[...]
        part = jnp.sum(dh_v * xn, axis=0, keepdims=True)

        @pl.when(pl.program_id(0) == 0)
        def _():
            dg_ref[...] = part

        @pl.when(pl.program_id(0) > 0)
        def _():
            dg_ref[...] += part

    row = pl.BlockSpec((tm, C), lambda i: (i, 0))
    vec = pl.BlockSpec((1, C), lambda i: (0, 0))
    dx, dg = pl.pallas_call(
        body,
        name=name,
        grid=(R // tm,),
        in_specs=[row, vec, row, row],
        out_specs=[row, vec],
        out_shape=[jax.ShapeDtypeStruct((R, C), F32), jax.ShapeDtypeStruct((1, C), F32)],
        compiler_params=_params(("arbitrary",)),
    )(x, g.reshape(1, C), dh, dres)
    return dx, dg.reshape(C)


def _final_loss(x, g, target, *, name):
    R, C = x.shape
    tm = _tile(R, 256)

    def body(x_ref, g_ref, t_ref, loss_ref, dx_ref, dg_ref):
        xv = x_ref[...]
        r = lax.rsqrt(jnp.mean(xv * xv, axis=-1, keepdims=True) + EPS)
        xn = xv * r
        gv = g_ref[...]
        err = xn * gv - t_ref[...]
        lpart = (0.5 / C) * jnp.sum(jnp.sum(err * err, axis=1, keepdims=True), axis=0, keepdims=True)
        dy = err * (1.0 / C)
        dxn = dy * gv
        dx_ref[...] = r * (dxn - xn * jnp.mean(dxn * xn, axis=-1, keepdims=True))
        gpart = jnp.sum(dy * xn, axis=0, keepdims=True)

        @pl.when(pl.program_id(0) == 0)
        def _():
            loss_ref[...] = lpart
            dg_ref[...] = gpart

        @pl.when(pl.program_id(0) > 0)
        def _():
            loss_ref[...] += lpart
            dg_ref[...] += gpart

    row = pl.BlockSpec((tm, C), lambda i: (i, 0))
    vec = pl.BlockSpec((1, C), lambda i: (0, 0))
    loss, dx, dg = pl.pallas_call(
        body,
        name=name,
        grid=(R // tm,),
        in_specs=[row, vec, row],
        out_specs=[pl.BlockSpec((1, 1), lambda i: (0, 0)), row, vec],
        out_shape=[jax.ShapeDtypeStruct((1, 1), F32), jax.ShapeDtypeStruct((R, C), F32), jax.ShapeDtypeStruct((1, C), F32)],
        compiler_params=_params(("arbitrary",)),
    )(x, g.reshape(1, C), target)
    return loss, dx, dg.reshape(C)


def _rows(shape):
    return lax.broadcasted_iota(jnp.int32, shape, 0)


def _lanes(shape):
    return lax.broadcasted_iota(jnp.int32, shape, 1)


def _pool_select(lane, vals):
    out = vals[3]
    for gi in (2, 1, 0):
        out = jnp.where(lane < 64 * (gi + 1), vals[gi], out)
    return out


def _pool_diff(a):
    S = a.shape[0]
    row, lane = _rows(a.shape), _lanes(a.shape)

    def down(v, k):
        return jnp.where(row >= k, pltpu.roll(v, k, 0), 0.0)

    s2 = a + down(a, 1)
    s4 = s2 + down(s2, 2)
    s8 = s4 + down(s4, 4)
    s16 = s8 + down(s8, 8)
    wsum = _pool_select(lane, (s2, s4, s8, s16))
    win = _pool_select(lane, (2, 4, 8, 16))
    cnt = jnp.minimum(row + 1, win).astype(F32)
    return wsum / cnt - a, cnt


def _pool_diff_t(dd, cnt):
    S = dd.shape[0]
    row, lane = _rows(dd.shape), _lanes(dd.shape)

    def up(v, k):
        return jnp.where(row < S - k, pltpu.roll(v, S - k, 0), 0.0)

    e = dd / cnt
    s2 = e + up(e, 1)
    s4 = s2 + up(s2, 2)
    s8 = s4 + up(s4, 4)
    s16 = s8 + up(s8, 8)
    return _pool_select(lane, (s2, s4, s8, s16)) - dd


def _pool_fwd(proj, wbd, scale, *, name):
    S = proj.shape[0]

    def body(a_ref, w_ref, s_ref, o_ref):
        d, _ = _pool_diff(a_ref[...])
        yp = _dot(d.astype(BF16), w_ref[...], 1, 0)
        o_ref[...] = (yp * s_ref[...]).astype(BF16)

    return pl.pallas_call(
        body,
        name=name,
        grid=(1,),
        in_specs=[
            pl.BlockSpec((S, POOL_W), lambda i: (0, OFF_A // POOL_W)),
            pl.BlockSpec((POOL_W, POOL_W), lambda i: (0, 0)),
            pl.BlockSpec((1, POOL_W), lambda i: (0, 0)),
        ],
        out_specs=pl.BlockSpec((S, POOL_W), lambda i: (0, 0)),
        out_shape=jax.ShapeDtypeStruct((S, POOL_W), BF16),
        compiler_params=_params(("arbitrary",)),
    )(proj, wbd, scale.reshape(1, POOL_W))


def _pool_bwd(proj, wbd, wbd_t, scale, dpa, *, name):
    S = proj.shape[0]

    def body(a_ref, w_ref, wt_ref, s_ref, dpa_ref, da_ref, dw_ref, ds_ref):
        d, cnt = _pool_diff(a_ref[...])
        db = d.astype(BF16)
        yp = _dot(db, w_ref[...], 1, 0)
        dpa_v = dpa_ref[...]
        ds_ref[...] = jnp.sum(dpa_v * yp, axis=0, keepdims=True)
        dyp = (dpa_v * s_ref[...]).astype(BF16)
        dw_ref[...] = _dot(db, dyp, 0, 0)
        dd = _dot(dyp, wt_ref[...], 1, 0)
        da_ref[...] = _pool_diff_t(dd, cnt).astype(BF16)

    full = pl.BlockSpec((S, POOL_W), lambda i: (0, 0))
    sq = pl.BlockSpec((POOL_W, POOL_W), lambda i: (0, 0))
    vec = pl.BlockSpec((1, POOL_W), lambda i: (0, 0))
    return pl.pallas_call(
        body,
        name=name,
        grid=(1,),
        in_specs=[pl.BlockSpec((S, POOL_W), lambda i: (0, OFF_A // POOL_W)), sq, sq, vec, full],
        out_specs=[full, sq, vec],
        out_shape=[
            jax.ShapeDtypeStruct((S, POOL_W), BF16),
            jax.ShapeDtypeStruct((POOL_W, POOL_W), F32),
            jax.ShapeDtypeStruct((1, POOL_W), F32),
        ],
        compiler_params=_params(("arbitrary",)),
    )(proj, wbd, wbd_t, scale.reshape(1, POOL_W), dpa)


def _log_sigmoid(z):
    return jnp.minimum(z, 0.0) - jnp.log(1.0 + jnp.exp(-jnp.abs(z)))


def _fox_prep(proj, bpad, *, name):
    S = proj.shape[0]

    def body(f_ref, b_ref, o_ref):
        acc = _log_sigmoid(f_ref[...] + b_ref[...])
        row = _rows(acc.shape)
        k = 1
        while k < S:
            acc = acc + jnp.where(row >= k, pltpu.roll(acc, k, 0), 0.0)
            k *= 2
        o_ref[...] = acc

    return pl.pallas_call(
        body,
        name=name,
        grid=(1,),
        in_specs=[pl.BlockSpec((S, F_PAD), lambda i: (0, OFF_F // F_PAD)), pl.BlockSpec((1, F_PAD), lambda i: (0, 0))],
        out_specs=pl.BlockSpec((S, F_PAD), lambda i: (0, 0)),
        out_shape=jax.ShapeDtypeStruct((S, F_PAD), F32),
        compiler_params=_params(("arbitrary",)),
    )(proj, bpad)


def _fox_post(proj, bpad, dcum, *, name):
    S = proj.shape[0]

    def body(f_ref, b_ref, d_ref, df_ref, db_ref):
        acc = d_ref[...]
        row = _rows(acc.shape)
        k = 1
        while k < S:
            acc = acc + jnp.where(row < S - k, pltpu.roll(acc, S - k, 0), 0.0)
            k *= 2
        df = acc * (1.0 - _sigmoid(f_ref[...] + b_ref[...]))
        df_ref[...] = df.astype(BF16)
        db_ref[...] = jnp.sum(df, axis=0, keepdims=True)

    full = pl.BlockSpec((S, F_PAD), lambda i: (0, 0))
    vec = pl.BlockSpec((1, F_PAD), lambda i: (0, 0))
    return pl.pallas_call(
        body,
        name=name,
        grid=(1,),
        in_specs=[pl.BlockSpec((S, F_PAD), lambda i: (0, OFF_F // F_PAD)), vec, full],
        out_specs=[full, vec],
        out_shape=[jax.ShapeDtypeStruct((S, F_PAD), BF16), jax.ShapeDtypeStruct((1, F_PAD), F32)],
        compiler_params=_params(("arbitrary",)),
    )(proj, bpad, dcum)


_FOX_SCALE = FOX_DH ** -0.5


def _fox_scores(qb, kb, fq, fk, q0, k0):
    s = _dot(qb, kb, 1, 1) * _FOX_SCALE + fq - fk
    row = q0 + _rows(s.shape)
    col = k0 + _lanes(s.shape)
    return jnp.where(row >= col, s, NEG)


def _fox_fwd(q, k, v, fq, fk, *, name):
    H, S, Dh = q.shape
    t = fk.shape[-1]
    nq = S // t

    def body(q_ref, k_ref, v_ref, fq_ref, fk_ref, o_ref, lse_ref):
        i = pl.program_id(1)
        qb = q_ref[...]
        fqv = fq_ref[...]

        def step(j, carry):
            m, l, acc = carry
            k0 = pl.multiple_of(j * t, t)
            kb = k_ref[pl.ds(k0, t), :]
            vb = v_ref[pl.ds(k0, t), :]
            s = _fox_scores(qb, kb, fqv, fk_ref[j], i * t, k0)
            m_new = jnp.maximum(m, jnp.max(s, axis=-1, keepdims=True))
            p = jnp.exp(s - m_new)
            alpha = jnp.exp(m - m_new)
            l = alpha * l + jnp.sum(p, axis=-1, keepdims=True)
            acc = alpha * acc + _dot(p.astype(BF16), vb, 1, 0)
            return m_new, l, acc

        init = (jnp.full((t, 1), NEG, F32), jnp.zeros((t, 1), F32), jnp.zeros((t, Dh), F32))
        m, l, acc = lax.fori_loop(0, i + 1, step, init)
        o_ref[...] = acc / l
        lse_ref[...] = m + jnp.log(l)

    qspec = pl.BlockSpec((None, t, Dh), lambda h, i: (h, i, 0))
    full = pl.BlockSpec((None, S, Dh), lambda h, i: (h, 0, 0))
    col = pl.BlockSpec((None, t, 1), lambda h, i: (h, i, 0))
    return pl.pallas_call(
        body,
        name=name,
        grid=(H, nq),
        in_specs=[qspec, full, full, col, pl.BlockSpec((None, nq, 1, t), lambda h, i: (h, 0, 0, 0))],
        out_specs=[qspec, col],
        out_shape=[jax.ShapeDtypeStruct((H, S, Dh), F32), jax.ShapeDtypeStruct((H, S, 1), F32)],
        compiler_params=_params(("parallel", "arbitrary")),
    )(q, k, v, fq, fk)


def _fox_bwd_q(q, k, v, fq, fk, o, do, lse, *, name):
    H, S, Dh = q.shape
    t = fk.shape[-1]
    nq = S // t

    def body(q_ref, k_ref, v_ref, fq_ref, fk_ref, o_ref, do_ref, lse_ref, dq_ref, delta_ref, dfq_ref):
        i = pl.program_id(1)
        qb = q_ref[...]
        fqv = fq_ref[...]
        dob = do_ref[...].astype(BF16)
        delta = jnp.sum(dob.astype(F32) * o_ref[...], axis=-1, keepdims=True)
        lse_v = lse_ref[...]

        def step(j, carry):
            dq, dfq = carry
            k0 = pl.multiple_of(j * t, t)
            kb = k_ref[pl.ds(k0, t), :]
            vb = v_ref[pl.ds(k0, t), :]
            s = _fox_scores(qb, kb, fqv, fk_ref[j], i * t, k0)
            p = jnp.exp(s - lse_v)
            dp = _dot(dob, vb, 1, 1)
            ds = p * (dp - delta)
            return dq + _dot(ds.astype(BF16), kb, 1, 0), dfq + jnp.sum(ds, axis=-1, keepdims=True)

        dq, dfq = lax.fori_loop(0, i + 1, step, (jnp.zeros((t, Dh), F32), jnp.zeros((t, 1), F32)))
        dq_ref[...] = dq * _FOX_SCALE
        delta_ref[...] = delta
        dfq_ref[...] = dfq

    qspec = pl.BlockSpec((None, t, Dh), lambda h, i: (h, i, 0))
    full = pl.BlockSpec((None, S, Dh), lambda h, i: (h, 0, 0))
    col = pl.BlockSpec((None, t, 1), lambda h, i: (h, i, 0))
    return pl.pallas_call(
        body,
        name=name,
        grid=(H, nq),
        in_specs=[qspec, full, full, col, pl.BlockSpec((None, nq, 1, t), lambda h, i: (h, 0, 0, 0)), qspec, qspec, col],
        out_specs=[qspec, col, col],
        out_shape=[jax.ShapeDtypeStruct((H, S, Dh), F32), jax.ShapeDtypeStruct((H, S, 1), F32), jax.ShapeDtypeStruct((H, S, 1), F32)],
        compiler_params=_params(("parallel", "arbitrary")),
    )(q, k, v, fq, fk, o, do, lse)


def _fox_bwd_kv(q, k, v, fq4, fk, do, lse4, delta4, *, name):
    H, S, Dh = q.shape
    t = fk.shape[-1]
    nq = S // t

    def body(q_ref, k_ref, v_ref, fq_ref, fk_ref, do_ref, lse_ref, delta_ref, dk_ref, dv_ref, dfk_ref):
        j = pl.program_id(1)
        kb = k_ref[...]
        vb = v_ref[...]
        fkv = fk_ref[...]

        def step(i, carry):
            dk, dv, dfk = carry
            q0 = pl.multiple_of(i * t, t)
            qb = q_ref[pl.ds(q0, t), :]
            dob = do_ref[pl.ds(q0, t), :].astype(BF16)
            s = _fox_scores(qb, kb, fq_ref[i], fkv, q0, j * t)
            p = jnp.exp(s - lse_ref[i])
            dv = dv + _dot(p.astype(BF16), dob, 0, 0)
            dp = _dot(dob, vb, 1, 1)
            ds = p * (dp - delta_ref[i])
            dk = dk + _dot(ds.astype(BF16), qb, 0, 0)
            dfk = dfk - jnp.sum(ds, axis=0, keepdims=True)
            return dk, dv, dfk

        init = (jnp.zeros((t, Dh), F32), jnp.zeros((t, Dh), F32), jnp.zeros((1, t), F32))
        dk, dv, dfk = lax.fori_loop(j, nq, step, init)
        dk_ref[...] = dk * _FOX_SCALE
        dv_ref[...] = dv
        dfk_ref[...] = dfk

    kspec = pl.BlockSpec((None, t, Dh), lambda h, j: (h, j, 0))
    full = pl.BlockSpec((None, S, Dh), lambda h, j: (h, 0, 0))
    col4 = pl.BlockSpec((None, nq, t, 1), lambda h, j: (h, 0, 0, 0))
    rowk = pl.BlockSpec((None, None, 1, t), lambda h, j: (h, j, 0, 0))
    return pl.pallas_call(
        body,
        name=name,
        grid=(H, nq),
        in_specs=[full, kspec, kspec, col4, rowk, full, col4, col4],
        out_specs=[kspec, kspec, rowk],
        out_shape=[
            jax.ShapeDtypeStruct((H, S, Dh), F32),
            jax.ShapeDtypeStruct((H, S, Dh), F32),
            jax.ShapeDtypeStruct((H, nq, 1, t), F32),
        ],
        compiler_params=_params(("parallel", "arbitrary")),
    )(q, k, v, fq4, fk, do, lse4, delta4)


def _group_mask(lane, gi):
    return (lane >= 64 * gi) & (lane < 64 * (gi + 1))


def _sgu_fwd(proj, gn, wm, bias, *, name):
    S = proj.shape[0]
    ts = _tile(S, 512)
    nc = ts // SGU_CHUNK

    def body(u_ref, v_ref, g_ref, w_ref, b_ref, o_ref):
        zv = _gelu(v_ref[...])
        vn = zv * lax.rsqrt(jnp.mean(zv * zv, axis=-1, keepdims=True) + EPS) * g_ref[...]
        lane = _lanes((SGU_CHUNK, SGU_W))
        for c in range(nc):
            rows = slice(c * SGU_CHUNK, (c + 1) * SGU_CHUNK)
            vcb = vn[rows].astype(BF16)
            mixed = b_ref[...]
            for gi in range(4):
                mixed = mixed + jnp.where(_group_mask(lane, gi), _dot(w_ref[gi], vcb, 1, 0), 0.0)
            o_ref[rows, :] = (_gelu(u_ref[rows, :]) * mixed).astype(BF16)

    return pl.pallas_call(
        body,
        name=name,
        grid=(S // ts,),
        in_specs=[
            pl.BlockSpec((ts, SGU_W), lambda i: (i, OFF_C // SGU_W)),
            pl.BlockSpec((ts, SGU_W), lambda i: (i, OFF_C // SGU_W + 1)),
            pl.BlockSpec((1, SGU_W), lambda i: (0, 0)),
            pl.BlockSpec((4, SGU_CHUNK, SGU_CHUNK), lambda i: (0, 0, 0)),
            pl.BlockSpec((SGU_CHUNK, SGU_W), lambda i: (0, 0)),
        ],
        out_specs=pl.BlockSpec((ts, SGU_W), lambda i: (i, 0)),
        out_shape=jax.ShapeDtypeStruct((S, SGU_W), BF16),
        compiler_params=_params(("parallel",)),
    )(proj, proj, gn.reshape(1, SGU_W), wm, bias)


def _sgu_bwd(proj, gn, wm, wm_t, bias, dsg, *, name):
    S = proj.shape[0]
    ts = _tile(S, 512)
    nc = ts // SGU_CHUNK

    def body(u_ref, v_ref, g_ref, w_ref, wt_ref, b_ref, dsg_ref, du_ref, dv_ref, dw_ref, db_ref, dg_ref):
        first = pl.program_id(0) == 0

        @pl.when(first)
        def _():
            dw_ref[...] = jnp.zeros_like(dw_ref)
            db_ref[...] = jnp.zeros_like(db_ref)
            dg_ref[...] = jnp.zeros_like(dg_ref)

        gv = g_ref[...]
        lane = _lanes((SGU_CHUNK, SGU_W))
        for c in range(nc):
            rows = slice(c * SGU_CHUNK, (c + 1) * SGU_CHUNK)
            vpre = v_ref[rows, :]
            upre = u_ref[rows, :]
            zv = _gelu(vpre)
            r = lax.rsqrt(jnp.mean(zv * zv, axis=-1, keepdims=True) + EPS)
            zn = zv * r
            vcb = (zn * gv).astype(BF16)
            mixed = b_ref[...]
            for gi in range(4):
                mixed = mixed + jnp.where(_group_mask(lane, gi), _dot(w_ref[gi], vcb, 1, 0), 0.0)
            zu = _gelu(upre)
            dsg_v = dsg_ref[rows, :]
            du_ref[rows, :] = (dsg_v * mixed * _gelu_grad(upre)).astype(BF16)
            dmixed = dsg_v * zu
            db_ref[...] += dmixed
            dvn = jnp.zeros((SGU_CHUNK, SGU_W), F32)
            for gi in range(4):
                dmg = jnp.where(_group_mask(lane, gi), dmixed, 0.0).astype(BF16)
                dw_ref[gi] += _dot(dmg, vcb, 1, 1)
                dvn = dvn + _dot(wt_ref[gi], dmg, 1, 0)
            dg_ref[...] += jnp.sum(dvn * zn, axis=0, keepdims=True)
            dzn = dvn * gv
            dzv = r * (dzn - zn * jnp.mean(dzn * zn, axis=-1, keepdims=True))
            dv_ref[rows, :] = (dzv * _gelu_grad(vpre)).astype(BF16)

    blk = pl.BlockSpec((ts, SGU_W), lambda i: (i, 0))
    vec = pl.BlockSpec((1, SGU_W), lambda i: (0, 0))
    w3 = pl.BlockSpec((4, SGU_CHUNK, SGU_CHUNK), lambda i: (0, 0, 0))
    bsp = pl.BlockSpec((SGU_CHUNK, SGU_W), lambda i: (0, 0))
    return pl.pallas_call(
        body,
        name=name,
        grid=(S // ts,),
        in_specs=[
            pl.BlockSpec((ts, SGU_W), lambda i: (i, OFF_C // SGU_W)),
            pl.BlockSpec((ts, SGU_W), lambda i: (i, OFF_C // SGU_W + 1)),
            vec, w3, w3, bsp, blk,
        ],
        out_specs=[blk, blk, w3, bsp, vec],
        out_shape=[
            jax.ShapeDtypeStruct((S, SGU_W), BF16),
            jax.ShapeDtypeStruct((S, SGU_W), BF16),
            jax.ShapeDtypeStruct((4, SGU_CHUNK, SGU_CHUNK), F32),
            jax.ShapeDtypeStruct((SGU_CHUNK, SGU_W), F32),
            jax.ShapeDtypeStruct((1, SGU_W), F32),
        ],
        compiler_params=_params(("arbitrary",)),
    )(proj, proj, gn.reshape(1, SGU_W), wm, wm_t, bias, dsg)


_GT = 512
_G0 = OFF_G // _GT


def _gate_specs(tm, col_of):
    specs = [pl.BlockSpec((tm, _GT), functools.partial(lambda k, *ids: (col_of(*ids)[0], _G0 + 2 * k + col_of(*ids)[1]), k)) for k in range(3)]
    specs += [pl.BlockSpec((1, _GT), functools.partial(lambda k, *ids: (0, 2 * k + col_of(*ids)[1]), k)) for k in range(3)]
    return specs


def _merge_fwd(proj, bg, ya, yb, yc, *, name):
    S = proj.shape[0]
    tm = _tile(S, 512)

    def body(g1, g2, g3, b1, b2, b3, ya_ref, yb_ref, yc_ref, o_ref):
        acc = _sigmoid(g1[...] + b1[...]) * ya_ref[...]
        acc = acc + _sigmoid(g2[...] + b2[...]) * yb_ref[...]
        acc = acc + _sigmoid(g3[...] + b3[...]) * yc_ref[...]
        o_ref[...] = acc.astype(BF16)

    blk = pl.BlockSpec((tm, _GT), lambda i, j: (i, j))
    return pl.pallas_call(
        body,
        name=name,
        grid=(S // tm, D // _GT),
        in_specs=_gate_specs(tm, lambda i, j: (i, j)) + [blk, blk, blk],
        out_specs=blk,
        out_shape=jax.ShapeDtypeStruct((S, D), BF16),
        compiler_params=_params(("parallel", "parallel")),
    )(proj, proj, proj, bg, bg, bg, ya, yb, yc)


def _merge_bwd(proj, bg, ya, yb, yc, dm, *, name):
    S = proj.shape[0]
    tm = _tile(S, 512)

    def body(g1, g2, g3, b1, b2, b3, ya_ref, yb_ref, yc_ref, dm_ref, dya, dyb, dyc, dg1, dg2, dg3, db1, db2, db3):
        first = pl.program_id(1) == 0
        dmv = dm_ref[...]
        for g_ref, b_ref, y_ref, dy_ref, dg_ref, db_ref in (
            (g1, b1, ya_ref, dya, dg1, db1), (g2, b2, yb_ref, dyb, dg2, db2), (g3, b3, yc_ref, dyc, dg3, db3)):
            gate = _sigmoid(g_ref[...] + b_ref[...])
            dy_ref[...] = (dmv * gate).astype(BF16)
            dpre = dmv * y_ref[...] * gate * (1.0 - gate)
            dg_ref[...] = dpre.astype(BF16)
            part = jnp.sum(dpre, axis=0, keepdims=True)

            @pl.when(first)
            def _():
                db_ref[...] = part

            @pl.when(jnp.logical_not(first))
            def _():
                db_ref[...] += part

    blk = pl.BlockSpec((tm, _GT), lambda j, i: (i, j))
    vec = pl.BlockSpec((1, _GT), lambda j, i: (0, j))
    big = jax.ShapeDtypeStruct((S, D), BF16)
    small = jax.ShapeDtypeStruct((1, D), F32)
    return pl.pallas_call(
        body,
        name=name,
        grid=(D // _GT, S // tm),
        in_specs=_gate_specs(tm, lambda j, i: (i, j)) + [blk, blk, blk, blk],
        out_specs=[blk] * 6 + [vec] * 3,
        out_shape=[big] * 6 + [small] * 3,
        compiler_params=_params(("parallel", "arbitrary")),
    )(proj, proj, proj, bg, bg, bg, ya, yb, yc, dm)


_X_SCALE = XDH ** -0.5


def _xattn_fwd(xq, kv, *, name):
    S = xq.shape[0]
    M = kv.shape[0]
    tq = _tile(S, 512)

    def body(q_ref, k_ref, v_ref, o_ref):
        s = _dot(q_ref[...], k_ref[...], 1, 1) * _X_SCALE
        e = jnp.exp(s - jnp.max(s, axis=-1, keepdims=True))
        p = e / jnp.sum(e, axis=-1, keepdims=True)
        o_ref[...] = _dot(p.astype(BF16), v_ref[...], 1, 0).astype(BF16)

    return pl.pallas_call(
        body,
        name=name,
        grid=(S // tq, XH),
        in_specs=[
            pl.BlockSpec((tq, XDH), lambda i, h: (i, h)),
            pl.BlockSpec((M, XDH), lambda i, h: (0, h)),
            pl.BlockSpec((M, XDH), lambda i, h: (0, XH + h)),
        ],
        out_specs=pl.BlockSpec((tq, XDH), lambda i, h: (i, h)),
        out_shape=jax.ShapeDtypeStruct((S, D), BF16),
        compiler_params=_params(("parallel", "parallel")),
    )(xq, kv, kv)


def _xattn_bwd(xq, kv, do, *, name):
    S = xq.shape[0]
    M = kv.shape[0]
    tq = _tile(S, 512)

    def body(q_ref, k_ref, v_ref, do_ref, dq_ref, dk_ref, dv_ref):
        qb = q_ref[...]
        kb = k_ref[...]
        dob = do_ref[...]
        s = _dot(qb, kb, 1, 1) * _X_SCALE
        e = jnp.exp(s - jnp.max(s, axis=-1, keepdims=True))
        p = e / jnp.sum(e, axis=-1, keepdims=True)
        dp = _dot(dob, v_ref[...], 1, 1)
        ds = (p * (dp - jnp.sum(p * dp, axis=-1, keepdims=True)) * _X_SCALE).astype(BF16)
        dq_ref[...] = _dot(ds, kb, 1, 0).astype(BF16)
        dk_part = _dot(ds, qb, 0, 0)
        dv_part = _dot(p.astype(BF16), dob, 0, 0)

        @pl.when(pl.program_id(1) == 0)
        def _():
            dk_ref[...] = dk_part
            dv_ref[...] = dv_part

        @pl.when(pl.program_id(1) > 0)
        def _():
            dk_ref[...] += dk_part
            dv_ref[...] += dv_part

    qspec = pl.BlockSpec((tq, XDH), lambda h, i: (i, h))
    kspec = pl.BlockSpec((M, XDH), lambda h, i: (0, h))
    return pl.pallas_call(
        body,
        name=name,
        grid=(XH, S // tq),
        in_specs=[qspec, kspec, pl.BlockSpec((M, XDH), lambda h, i: (0, XH + h)), qspec],
        out_specs=[qspec, kspec, kspec],
        out_shape=[jax.ShapeDtypeStruct((S, D), BF16), jax.ShapeDtypeStruct((M, D), F32), jax.ShapeDtypeStruct((M, D), F32)],
        compiler_params=_params(("parallel", "arbitrary")),
    )(xq, kv, kv, do)


def _pointwise(fn, ins, out_dtypes, *, name, tm=256, tc=2048):
    R, C = ins[0].shape
    tm, tc = _tile(R, tm), _tile(C, tc)
    n = len(ins)

    def body(*refs):
        outs = fn(*[r[...] for r in refs[:n]])
        for o_ref, val in zip(refs[n:], outs):
            o_ref[...] = val.astype(o_ref.dtype)

    blk = pl.BlockSpec((tm, tc), lambda i, j: (i, j))
    return pl.pallas_call(
        body,
        name=name,
        grid=(R // tm, C // tc),
        in_specs=[blk] * n,
        out_specs=[blk] * len(out_dtypes),
        out_shape=[jax.ShapeDtypeStruct((R, C), dt) for dt in out_dtypes],
        compiler_params=_params(("parallel", "parallel")),
    )(*ins)


def _adam_math(w, g, m, v):
    m = ADAM_B1 * m + (1.0 - ADAM_B1) * g
    v = ADAM_B2 * v + (1.0 - ADAM_B2) * (g * g)
    m_hat = m / (1.0 - ADAM_B1 ** ADAM_STEP)
    v_hat = v / (1.0 - ADAM_B2 ** ADAM_STEP)
    delta = -ADAM_LR * (m_hat / (jnp.sqrt(v_hat) + ADAM_EPS) + ADAM_WD * w)
    return delta, m, v


def _adamw_sharded(parts, w, m, v, *, name):
    R, C = w.shape
    tm = _tile(R, 256)

    def body(p_ref, w_ref, m_ref, v_ref, g_ref, d_ref, mo_ref, vo_ref):
        g = p_ref[0].astype(F32)
        for dev in range(1, N_DEV):
            g = g + p_ref[dev].astype(F32)
        delta, mn, vn = _adam_math(w_ref[...], g, m_ref[...], v_ref[...])
        g_ref[...] = g
        d_ref[...] = delta
        mo_ref[...] = mn
        vo_ref[...] = vn

    blk = pl.BlockSpec((tm, C), lambda i: (i, 0))
    sds = jax.ShapeDtypeStruct((R, C), F32)
    return pl.pallas_call(
        body,
        name=name,
        grid=(R // tm,),
        in_specs=[pl.BlockSpec((N_DEV, tm, C), lambda i: (0, i, 0)), blk, blk, blk],
        out_specs=[blk] * 4,
        out_shape=[sds] * 4,
        compiler_params=_params(("parallel",)),
    )(parts, w, m, v)


def _position():
    return lax.axis_index("x"), lax.axis_index("y"), lax.axis_index("c")


def _dev_index(px, py, pc):
    return 4 * px + 2 * py + pc


_ANY = pl.BlockSpec(memory_space=pl.ANY)


def _all_gather(shards, *, name):
    n = len(shards)

    def body(*refs):
        ins, outs = refs[:n], refs[n:2 * n]
        send_sems, recv_sems, local_sems = refs[2 * n:]
        x, y, c = _position()
        me, sibling = (x, y, c), (x, y, 1 - c)
        chips = [(1 - x, y), (x, 1 - y), (1 - x, 1 - y)]

        def copy(t, k, block, to, src=None):
            dst = outs[t].at[_dev_index(*block)]
            return pltpu.make_async_remote_copy(
                src_ref=dst if src is None else src, dst_ref=dst, send_sem=send_sems.at[t, k], recv_sem=recv_sems.at[t, k],
                device_id=to, device_id_type=MESH)

        mine = [pltpu.make_async_copy(ins[t], outs[t].at[_dev_index(*me)], local_sems.at[t]) for t in range(n)]
        for cp in mine:
            cp.start()
        started = []
        for j, chip in enumerate(chips):
            for t in range(n):
                started.append(copy(t, 1 + j, me, (*chip, c), src=ins[t]))
                started[-1].start()
        for t in range(n):
            started.append(copy(t, 0, me, sibling, src=ins[t]))
            started[-1].start()
        for j, chip in enumerate(chips):
            for t in range(n):
                copy(t, 1 + j, (*chip, c), me).wait_recv()
                started.append(copy(t, 4 + j, (*chip, c), sibling))
                started[-1].start()
        for t in range(n):
            copy(t, 0, sibling, me).wait_recv()
        for j, chip in enumerate(chips):
            for t in range(n):
                copy(t, 4 + j, (*chip, 1 - c), me).wait_recv()
        for cp in started:
            cp.wait_send()
        for cp in mine:
            cp.wait()

    return pl.pallas_call(
        body,
        name=name,
        in_specs=[_ANY] * n,
        out_specs=[_ANY] * n,
        out_shape=[jax.ShapeDtypeStruct((N_DEV, *s.shape), s.dtype) for s in shards],
        scratch_shapes=[pltpu.SemaphoreType.DMA((n, 7)), pltpu.SemaphoreType.DMA((n, 7)), pltpu.SemaphoreType.DMA((n,))],
        compiler_params=pltpu.CompilerParams(has_side_effects=True),
    )(*shards)


def _peers(x, y, c):
    out = []
    for mask in range(1, N_DEV):
        fx, fy, fc = (mask >> 2) & 1, (mask >> 1) & 1, mask & 1
        out.append((1 - x if fx else x, 1 - y if fy else y, 1 - c if fc else c))
    return out


def _exchange(parts, *, name):
    n = len(parts)

    def body(*refs):
        ins, outs = refs[:n], refs[n:2 * n]
        send_sems, recv_sems, local_sems = refs[2 * n:]
        x, y, c = _position()
        me = _dev_index(x, y, c)
        peers = _peers(x, y, c)

        def copy(t, k):
            peer = peers[k]
            return pltpu.make_async_remote_copy(
                src_ref=ins[t].at[_dev_index(*peer)], dst_ref=outs[t].at[me], send_sem=send_sems.at[t, k],
                recv_sem=recv_sems.at[t, k], device_id=peer, device_id_type=MESH)

        def arrival(t, k):
            peer = peers[k]
            dst = outs[t].at[_dev_index(*peer)]
            return pltpu.make_async_remote_copy(
                src_ref=dst, dst_ref=dst, send_sem=send_sems.at[t, k], recv_sem=recv_sems.at[t, k],
                device_id=peer, device_id_type=MESH)

        mine = [pltpu.make_async_copy(ins[t].at[me], outs[t].at[me], local_sems.at[t]) for t in range(n)]
        for cp in mine:
            cp.start()
        started = [copy(t, k) for k in range(N_DEV - 1) for t in range(n)]
        for cp in started:
            cp.start()
        for k in range(N_DEV - 1):
            for t in range(n):
                arrival(t, k).wait_recv()
        for cp in started:
            cp.wait_send()
        for cp in mine:
            cp.wait()

    return pl.pallas_call(
        body,
        name=name,
        in_specs=[_ANY] * n,
        out_specs=[_ANY] * n,
        out_shape=[jax.ShapeDtypeStruct(p.shape, p.dtype) for p in parts],
        scratch_shapes=[pltpu.SemaphoreType.DMA((n, 7)), pltpu.SemaphoreType.DMA((n, 7)), pltpu.SemaphoreType.DMA((n,))],
        compiler_params=pltpu.CompilerParams(has_side_effects=True),
    )(*parts)


def _adamw_replicated(g_local, w, m, v, *, name):
    R, C = g_local.shape

    def body(g_ref, w_ref, m_ref, v_ref, go_ref, d_ref, mo_ref, vo_ref, buf, send_sems, recv_sems):
        x, y, c = _position()
        me = _dev_index(x, y, c)
        peers = _peers(x, y, c)
        copies = []
        for k, peer in enumerate(peers):
            copies.append(pltpu.make_async_remote_copy(
                src_ref=g_ref, dst_ref=buf.at[me], send_sem=send_sems.at[k], recv_sem=recv_sems.at[k],
                device_id=peer, device_id_type=MESH))
            copies[-1].start()
        buf[me] = g_ref[...]
        for k, peer in enumerate(peers):
            dst = buf.at[_dev_index(*peer)]
            pltpu.make_async_remote_copy(
                src_ref=dst, dst_ref=dst, send_sem=send_sems.at[k], recv_sem=recv_sems.at[k],
                device_id=peer, device_id_type=MESH).wait_recv()
        for cp in copies:
            cp.wait_send()
        g = buf[0]
        for dev in range(1, N_DEV):
            g = g + buf[dev]
        delta, mn, vn = _adam_math(w_ref[...], g, m_ref[...], v_ref[...])
        go_ref[...] = g
        d_ref[...] = delta
        mo_ref[...] = mn
        vo_ref[...] = vn

    vm = pl.BlockSpec(memory_space=pltpu.VMEM)
    sds = jax.ShapeDtypeStruct((R, C), F32)
    return pl.pallas_call(
        body,
        name=name,
        in_specs=[vm] * 4,
        out_specs=[vm] * 4,
        out_shape=[sds] * 4,
        scratch_shapes=[pltpu.VMEM((N_DEV, R, C), F32), pltpu.SemaphoreType.DMA((7,)), pltpu.SemaphoreType.DMA((7,))],
        compiler_params=pltpu.CompilerParams(has_side_effects=True, vmem_limit_bytes=VMEM_LIMIT),
    )(g_local, w, m, v)


def _block_diag(w):
    out = jnp.zeros((POOL_W, POOL_W), w.dtype)
    for gi in range(4):
        out = out.at[64 * gi:64 * (gi + 1), 64 * gi:64 * (gi + 1)].set(w[gi])
    return out


def _heads_first(a):
    return a.reshape(a.shape[0], FOX_H, FOX_DH).transpose(1, 0, 2)


def _heads_last(a):
    return a.transpose(1, 0, 2).reshape(a.shape[1], FOX_W)


def _layer_consts(sp, l):
    causal = jnp.tril(jnp.ones((SGU_CHUNK, SGU_CHUNK), F32))
    wm = (sp["sgu_w"][l] * causal[None]).astype(BF16)
    wbd = _block_diag(sp["pool_w"][l]).astype(BF16)
    return dict(
        wbd=wbd, wbd_t=wbd.T, wm=wm, wm_t=wm.transpose(0, 2, 1),
        sgu_bias=jnp.repeat(sp["sgu_b"][l].T, 64, axis=1),
        bpad=jnp.pad(sp["b_forget"][l], (0, F_PAD - FOX_H)).reshape(1, F_PAD),
        bg=sp["b_gate"][l].reshape(1, 3 * D),
    )


def _layer_fwd(l, x, hm_in, W, sp):
    S = x.shape[0]
    t = _tile(S, 256)
    c = _layer_consts(sp, l)
    n = f"l{l}_"
    h = _rms_fwd(x, sp["norm_mix_g"][l], name=n + "norm_mix")
    proj = _mm(h, W["cat"], name=n + "proj")
    pa = _pool_fwd(proj, c["wbd"], sp["pool_scale"][l], name=n + "pool")
    cum = _fox_prep(proj, c["bpad"], name=n + "fox_prep")
    cum_h = cum[:, :FOX_H].T
    fq = cum_h.reshape(FOX_H, S, 1)
    fk = cum_h.reshape(FOX_H, S // t, 1, t)
    qkv = proj[:, OFF_Q:OFF_F].astype(BF16)
    q, k, v = (_heads_first(qkv[:, FOX_W * i:FOX_W * (i + 1)]) for i in range(3))
    o, lse = _fox_fwd(q, k, v, fq, fk, name=n + "fox")
    sg = _sgu_fwd(proj, sp["sgu_norm_g"][l], c["wm"], c["sgu_bias"], name=n + "sgu")
    ob = _heads_last(o).astype(BF16)
    ya = _mm(pa, W["ba"], name=n + "branch_a")
    yb = _mm(ob, W["bb"], name=n + "branch_b")
    yc = _mm(sg, W["bc"], name=n + "branch_c")
    merged = _merge_fwd(proj, c["bg"], ya, yb, yc, name=n + "merge")
    x1 = _pointwise(lambda a, b: (a + b,), [x, _mm(merged, W["out"], name=n + "out")], [F32], name=n + "res1")[0]
    hx = _rms_fwd(x1, sp["norm_xattn_g"][l], name=n + "norm_xattn")
    hm = _rms_fwd(hm_in, sp["norm_mem_g"][l], name=n + "norm_mem")
    xq = _mm(hx, W["xq"], out_dtype=BF16, name=n + "xq")
    kv = _mm(hm, W["xkv"], out_dtype=BF16, tm=256, name=n + "xkv")
    o2 = _xattn_fwd(xq, kv, name=n + "xattn")
    x2 = _pointwise(lambda a, b: (a + b,), [x1, _mm(o2, W["xo"], name=n + "xo")], [F32], name=n + "res2")[0]
    hf = _rms_fwd(x2, sp["norm_ffn_g"][l], name=n + "norm_ffn")
    z = _mm(hf, W["ff1"], name=n + "ff1")
    act = _pointwise(lambda a: (jnp.square(jnp.maximum(a, 0.0)),), [z], [BF16], name=n + "relu2")[0]
    x3 = _pointwise(lambda a, b: (a + b,), [x2, _mm(act, W["ff2"], name=n + "ff2")], [F32], name=n + "res3")[0]
    saved = dict(x=x, h=h, proj=proj, pa=pa, q=q, k=k, v=v, fq=fq, fk=fk, o=o, lse=lse, sg=sg, ob=ob, ya=ya, yb=yb, yc=yc,
                 merged=merged, x1=x1, hx=hx, hm=hm, xq=xq, kv=kv, o2=o2, x2=x2, hf=hf, z=z, act=act, c=c)
    return x3, saved


def _layer_bwd(l, dx3, sv, mem, W, sp):
    S = dx3.shape[0]
    t = sv["fk"].shape[-1]
    c = sv["c"]
    n = f"l{l}b_"
    gw, gs = {}, {}
    gw["ff2"] = _mm(sv["act"], dx3, ta=True, name=n + "dw_ff2")
    dact = _mm(dx3, W["ff2"], tb=True, name=n + "dact")
    dz = _pointwise(lambda a, b: (a * 2.0 * jnp.maximum(b, 0.0),), [dact, sv["z"]], [BF16], name=n + "drelu2")[0]
    gw["ff1"] = _mm(sv["hf"], dz, ta=True, name=n + "dw_ff1")
    dhf = _mm(dz, W["ff1"], tb=True, name=n + "dhf")
    dx2, gs["norm_ffn_g"] = _rms_bwd(sv["x2"], sp["norm_ffn_g"][l], dhf, dx3, name=n + "dnorm_ffn")
    gw["xo"] = _mm(sv["o2"], dx2, ta=True, name=n + "dw_xo")
    do2 = _mm(dx2, W["xo"], tb=True, out_dtype=BF16, name=n + "do2")
    dxq, dxk, dxv = _xattn_bwd(sv["xq"], sv["kv"], do2, name=n + "dxattn")
    dkv = jnp.concatenate([dxk, dxv], axis=1)
    gw["xq"] = _mm(sv["hx"], dxq, ta=True, name=n + "dw_xq")
    gw["xkv"] = _mm(sv["hm"], dkv, ta=True, tk=256, name=n + "dw_xkv")
    dhm = _mm(dkv, W["xkv"], tb=True, tm=256, name=n + "dhm")
    _, gs["norm_mem_g"] = _rms_bwd(mem, sp["norm_mem_g"][l], dhm, jnp.zeros_like(mem), name=n + "dnorm_mem")
    dhx = _mm(dxq, W["xq"], tb=True, name=n + "dhx")
    dx1, gs["norm_xattn_g"] = _rms_bwd(sv["x1"], sp["norm_xattn_g"][l], dhx, dx2, name=n + "dnorm_xattn")
    gw["out"] = _mm(sv["merged"], dx1, ta=True, name=n + "dw_out")
    dm = _mm(dx1, W["out"], tb=True, name=n + "dmerged")
    dya, dyb, dyc, dg1, dg2, dg3, db1, db2, db3 = _merge_bwd(sv["proj"], c["bg"], sv["ya"], sv["yb"], sv["yc"], dm, name=n + "dmerge")
    gs["b_gate"] = jnp.concatenate([db1, db2, db3], axis=1).reshape(3 * D)
    gw["ba"] = _mm(sv["pa"], dya, ta=True, name=n + "dw_ba")
    gw["bb"] = _mm(sv["ob"], dyb, ta=True, name=n + "dw_bb")
    gw["bc"] = _mm(sv["sg"], dyc, ta=True, name=n + "dw_bc")
    dpa = _mm(dya, W["ba"], tb=True, name=n + "dpa")
    do = _mm(dyb, W["bb"], tb=True, name=n + "do")
    dsg = _mm(dyc, W["bc"], tb=True, name=n + "dsg")
    da, dwbd, dscale = _pool_bwd(sv["proj"], c["wbd"], c["wbd_t"], sp["pool_scale"][l], dpa, name=n + "dpool")
    gs["pool_w"] = jnp.stack([dwbd[64 * gi:64 * (gi + 1), 64 * gi:64 * (gi + 1)] for gi in range(4)])
    gs["pool_scale"] = dscale.reshape(POOL_W)
    doh = _heads_first(do)
    dq, delta, dfq = _fox_bwd_q(sv["q"], sv["k"], sv["v"], sv["fq"], sv["fk"], sv["o"], doh, sv["lse"], name=n + "dfox_q")
    r4 = lambda a: a.reshape(FOX_H, S // t, t, 1)
    dk, dv, dfk = _fox_bwd_kv(sv["q"], sv["k"], sv["v"], r4(sv["fq"]), sv["fk"], doh, r4(sv["lse"]), r4(delta), name=n + "dfox_kv")
    dcum = jnp.pad((dfk.reshape(FOX_H, S) + dfq.reshape(FOX_H, S)).T, ((0, 0), (0, F_PAD - FOX_H)))
    df, dbf = _fox_post(sv["proj"], c["bpad"], dcum, name=n + "dfox_post")
    gs["b_forget"] = dbf[0, :FOX_H]
    du, dvv, dwm, dbias, dgn = _sgu_bwd(sv["proj"], sp["sgu_norm_g"][l], c["wm"], c["wm_t"], c["sgu_bias"], dsg, name=n + "dsgu")
    gs["sgu_w"] = dwm * jnp.tril(jnp.ones((SGU_CHUNK, SGU_CHUNK), F32))[None]
    gs["sgu_b"] = dbias.reshape(SGU_CHUNK, 4, 64).sum(axis=2).T
    gs["sgu_norm_g"] = dgn.reshape(SGU_W)
    dproj = jnp.concatenate(
        [da, _heads_last(dq).astype(BF16), _heads_last(dk).astype(BF16), _heads_last(dv).astype(BF16), df, du, dvv, dg1, dg2, dg3], axis=1)
    gw["cat"] = _mm(sv["h"], dproj, ta=True, name=n + "dw_in")
    dh = _mm(dproj, W["cat"], tb=True, name=n + "dh")
    dx, gs["norm_mix_g"] = _rms_bwd(sv["x"], sp["norm_mix_g"][l], dh, dx1, name=n + "dnorm_mix")
    return dx, gw, gs


def _local_step(x, mem, target, Ws, sp):
    saved = []
    for l in range(DEPTH):
        x, sv = _layer_fwd(l, x, mem, Ws[l], sp)
        saved.append(sv)
    loss, dx, dgf = _final_loss(x, sp["final_norm_g"], target, name="final_loss")
    gws, gss = [None] * DEPTH, [None] * DEPTH
    for l in reversed(range(DEPTH)):
        dx, gws[l], gss[l] = _layer_bwd(l, dx, saved[l], mem, Ws[l], sp)
    small = {k: jnp.stack([gss[l][k] for l in range(DEPTH)]) for k in gss[0]}
    small["final_norm_g"] = dgf
    return loss, dx, gws, small


_SMALL = ["norm_mix_g", "b_forget", "pool_w", "pool_scale", "sgu_norm_g", "sgu_w", "sgu_b", "b_gate", "norm_xattn_g",
          "norm_mem_g", "norm_ffn_g", "final_norm_g"]
_COL = {"w_branch_a": "ba", "w_branch_b": "bb", "w_branch_c": "bc", "w_xkv": "xkv", "w_ff1": "ff1"}
_ROW = {"w_out": "out", "w_xq": "xq", "w_xo": "xo", "w_ff2": "ff2"}
_BIG = ["w_in", "w_branch_a", "w_branch_b", "w_branch_c", "w_out", "w_xq", "w_xkv", "w_xo", "w_ff1", "w_ff2"]
_PACK_LANES = 128


def _pack(tensors):
    rows = []
    for a in tensors:
        flat = a.reshape(-1)
        flat = jnp.pad(flat, (0, (-flat.shape[0]) % _PACK_LANES))
        rows.append(flat.reshape(-1, _PACK_LANES))
    out = jnp.concatenate(rows, axis=0)
    return jnp.pad(out, ((0, (-out.shape[0]) % 8), (0, 0)))


def _unpack(packed, like):
    out, r = [], 0
    for a in like:
        size = math.prod(a.shape)
        nr = -(-size // _PACK_LANES)
        out.append(packed[r:r + nr].reshape(-1)[:size].reshape(a.shape))
        r += nr
    return out


def _cols_full(g):
    return g.transpose(1, 0, 2).reshape(g.shape[1], -1)


def _cols_split(a):
    return a.reshape(a.shape[0], N_DEV, -1).transpose(1, 0, 2)


def _full_weights(gathered, l):
    W = {}
    w_in = _cols_full(gathered["w_in"][:, l])
    W["cat"] = jnp.concatenate([w_in[:, :R_OFF_C], jnp.zeros((D, OFF_C - R_OFF_C), BF16), w_in[:, R_OFF_C:]], axis=1)
    for name, key in _COL.items():
        W[key] = _cols_full(gathered[name][:, l])
    for name, key in _ROW.items():
        g = gathered[name][:, l]
        W[key] = g.reshape(-1, g.shape[-1])
    return W


def _shard_parts(gws):
    parts = {}
    cat = [jnp.concatenate([g["cat"][:, :R_OFF_C], g["cat"][:, OFF_C:]], axis=1) for g in gws]
    parts["w_in"] = jnp.stack([_cols_split(a.astype(BF16)) for a in cat], axis=1)
    for name, key in _COL.items():
        parts[name] = jnp.stack([_cols_split(g[key].astype(BF16)) for g in gws], axis=1)
    for name, key in _ROW.items():
        parts[name] = jnp.stack([g[key].astype(BF16).reshape(N_DEV, -1, g[key].shape[-1]) for g in gws], axis=1)
    return parts


def kernel(x, mem, norm_mix_g, w_in, b_forget, pool_w, pool_scale, sgu_norm_g, sgu_w, sgu_b, w_branch_a, w_branch_b, w_branch_c, b_gate, w_out, norm_xattn_g, norm_mem_g, w_xq, w_xkv, w_xo, norm_ffn_g, w_ff1, w_ff2, final_norm_g, loss_target, m_norm_mix_g, m_w_in, m_b_forget, m_pool_w, m_pool_scale, m_sgu_norm_g, m_sgu_w, m_sgu_b, m_w_branch_a, m_w_branch_b, m_w_branch_c, m_b_gate, m_w_out, m_norm_xattn_g, m_norm_mem_g, m_w_xq, m_w_xkv, m_w_xo, m_norm_ffn_g, m_w_ff1, m_w_ff2, m_final_norm_g, v_norm_mix_g, v_w_in, v_b_forget, v_pool_w, v_pool_scale, v_sgu_norm_g, v_sgu_w, v_sgu_b, v_w_branch_a, v_w_branch_b, v_w_branch_c, v_b_gate, v_w_out, v_norm_xattn_g, v_norm_mem_g, v_w_xq, v_w_xkv, v_w_xo, v_norm_ffn_g, v_w_ff1, v_w_ff2, v_final_norm_g):
    names = ["norm_mix_g", "w_in", "b_forget", "pool_w", "pool_scale", "sgu_norm_g", "sgu_w", "sgu_b", "w_branch_a", "w_branch_b",
             "w_branch_c", "b_gate", "w_out", "norm_xattn_g", "norm_mem_g", "w_xq", "w_xkv", "w_xo", "norm_ffn_g", "w_ff1", "w_ff2",
             "final_norm_g"]
    w = dict(zip(names, [norm_mix_g, w_in, b_forget, pool_w, pool_scale, sgu_norm_g, sgu_w, sgu_b, w_branch_a, w_branch_b, w_branch_c,
                         b_gate, w_out, norm_xattn_g, norm_mem_g, w_xq, w_xkv, w_xo, norm_ffn_g, w_ff1, w_ff2, final_norm_g]))
    m = dict(zip(names, [m_norm_mix_g, m_w_in, m_b_forget, m_pool_w, m_pool_scale, m_sgu_norm_g, m_sgu_w, m_sgu_b, m_w_branch_a,
                         m_w_branch_b, m_w_branch_c, m_b_gate, m_w_out, m_norm_xattn_g, m_norm_mem_g, m_w_xq, m_w_xkv, m_w_xo,
                         m_norm_ffn_g, m_w_ff1, m_w_ff2, m_final_norm_g]))
    v = dict(zip(names, [v_norm_mix_g, v_w_in, v_b_forget, v_pool_w, v_pool_scale, v_sgu_norm_g, v_sgu_w, v_sgu_b, v_w_branch_a,
                         v_w_branch_b, v_w_branch_c, v_b_gate, v_w_out, v_norm_xattn_g, v_norm_mem_g, v_w_xq, v_w_xkv, v_w_xo,
                         v_norm_ffn_g, v_w_ff1, v_w_ff2, v_final_norm_g]))

    gathered = dict(zip(_BIG, _all_gather([w[k].astype(BF16) for k in _BIG], name="gather_weights")))
    Ws = [_full_weights(gathered, l) for l in range(DEPTH)]
    sp = {k: w[k] for k in _SMALL}
    loss, dx, gws, small = _local_step(x[0], mem[0], loss_target[0], Ws, sp)
    loss = lax.psum(loss[0, 0], ("x", "y", "c"))

    parts = _shard_parts(gws)
    received = dict(zip(_BIG, _exchange([parts[k] for k in _BIG], name="exchange_grads")))

    grads, deltas, new_m, new_v = {}, {}, {}, {}
    for k in _BIG:
        shape = w[k].shape
        flat = lambda a: a.reshape(-1, shape[-1])
        r = received[k]
        outs = _adamw_sharded(r.reshape(N_DEV, -1, shape[-1]), flat(w[k]), flat(m[k]), flat(v[k]), name="adamw_" + k)
        grads[k], deltas[k], new_m[k], new_v[k] = (a.reshape(shape) for a in outs)
    like = [w[k] for k in _SMALL]
    outs = _adamw_replicated(_pack([small[k] for k in _SMALL]), _pack(like), _pack([m[k] for k in _SMALL]),
                             _pack([v[k] for k in _SMALL]), name="adamw_replicated")
    for dst, packed in zip((grads, deltas, new_m, new_v), outs):
        dst.update(zip(_SMALL, _unpack(packed, like)))

    return (loss, dx[None], *[grads[k] for k in names], *[deltas[k] for k in names], *[new_m[k] for k in names],
            *[new_v[k] for k in names])
```

```python
import functools
import math

import jax
import jax.numpy as jnp
from jax import lax
from jax.experimental import pallas as pl
from jax.experimental.pallas import tpu as pltpu

F32 = jnp.float32
BF16 = jnp.bfloat16
MESH = pl.DeviceIdType.MESH

N_DEV = 8
D = 1024
DEPTH = 2
EPS = 1e-6
NEG = -1e30
POOL_W = 256
FOX_H = 8
FOX_DH = 64
FOX_W = 512
SGU_W = 256
SGU_CHUNK = 128
XH = 4
XDH = 256
N_IN = 5384
R_OFF_Q, R_OFF_F, R_OFF_C = 256, 1792, 1800
QKV_W = 3 * FOX_W
OFF_A, OFF_F, OFF_C, OFF_G, REST_W = 0, 256, 512, 1024, 4096
F_LANES = 128

ADAM_LR = 0.001
ADAM_B1 = 0.9
ADAM_B2 = 0.999
ADAM_EPS = 1e-08
ADAM_WD = 0.01
ADAM_STEP = 10

VMEM_LIMIT = 56 * 1024 * 1024


def _tile(n, pref):
    t = min(n, pref)
    while n % t:
        t -= 128
    assert t > 0, (n, pref)
    return t


def _params(sem=None):
    return pltpu.CompilerParams(dimension_semantics=sem, vmem_limit_bytes=VMEM_LIMIT)


def _dot(a, b, ca, cb):
    return lax.dot_general(a, b, (((ca,), (cb,)), ((), ())), preferred_element_type=F32)


def _sigmoid(z):
    return 1.0 / (1.0 + jnp.exp(-z))


_GELU_K = math.sqrt(2.0 / math.pi)
_GELU_C = 0.044715


def _gelu(x):
    return 0.5 * x * (1.0 + jnp.tanh(_GELU_K * (x + _GELU_C * x * x * x)))


def _gelu_grad(x):
    t = jnp.tanh(_GELU_K * (x + _GELU_C * x * x * x))
    return 0.5 * (1.0 + t) + 0.5 * x * (1.0 - t * t) * _GELU_K * (1.0 + 3.0 * _GELU_C * x * x)


def _rows(shape):
    return lax.broadcasted_iota(jnp.int32, shape, 0)


def _lanes(shape):
    return lax.broadcasted_iota(jnp.int32, shape, 1)


class _Gathered:
    def __init__(self, arr, kind, layer):
        self.arr, self.kind, self.layer = arr, kind, layer
        r, c = arr.shape[2:]
        self.shape = (r, N_DEV * c) if kind == "col" else (N_DEV * r, c)


def _mm(a, b, *, ta=False, tb=False, extras=(), epilogue=None, out_dtypes=(F32,), shard_out=False, tm=None, tn=512, tk=None, name):
    M, K = (a.shape[1], a.shape[0]) if ta else a.shape
    N, Kb = b.shape if tb else b.shape[::-1]
    assert Kb == K, (a.shape, b.shape, ta, tb)
    gathered = isinstance(b, _Gathered)
    if gathered:
        r, c = b.arr.shape[2:]
        layer = b.layer
        if (b.kind == "col") != tb:
            tn = c if b.kind == "col" else r
        else:
            tk = c if b.kind == "col" else r
    if shard_out:
        tn = N // N_DEV
    tm = _tile(M, tm or (1024 if ta else 2048))
    tn = _tile(N, tn)
    tk = _tile(K, tk or (2048 if ta else 1024))
    nk = K // tk
    ca, cb = (0 if ta else 1), (1 if tb else 0)
    n_ex, n_out = len(extras), len(out_dtypes)
    if epilogue is None:
        epilogue = lambda acc: (acc,)

    def body(*refs):
        a_ref, b_ref = refs[:2]
        ex_refs = refs[2:2 + n_ex]
        o_refs = refs[2 + n_ex:2 + n_ex + n_out]
        part = _dot(a_ref[...].astype(BF16), b_ref[...].astype(BF16), ca, cb)

        def finish(acc):
            for o_ref, val in zip(o_refs, epilogue(acc, *[e[...] for e in ex_refs])):
                o_ref[...] = val.astype(o_ref.dtype)

        if nk == 1:
            finish(part)
        else:
            acc_ref = refs[-1]
            k = pl.program_id(2)

            @pl.when(k == 0)
            def _():
                acc_ref[...] = part

            @pl.when(k > 0)
            def _():
                acc_ref[...] += part

            @pl.when(k == nk - 1)
            def _():
                finish(acc_ref[...])

    a_spec = pl.BlockSpec((tk, tm), lambda i, j, k: (k, i)) if ta else pl.BlockSpec((tm, tk), lambda i, j, k: (i, k))
    if not gathered:
        b_arr = b
        b_spec = pl.BlockSpec((tn, tk), lambda i, j, k: (j, k)) if tb else pl.BlockSpec((tk, tn), lambda i, j, k: (k, j))
    else:
        b_arr = b.arr
        if b.kind == "col" and not tb:
            b_spec = pl.BlockSpec((None, None, tk, tn), lambda i, j, k: (j, layer, k, 0))
        elif b.kind == "col":
            b_spec = pl.BlockSpec((None, None, tn, tk), lambda i, j, k: (k, layer, j, 0))
        elif not tb:
            b_spec = pl.BlockSpec((None, None, tk, tn), lambda i, j, k: (k, layer, 0, j))
        else:
            b_spec = pl.BlockSpec((None, None, tn, tk), lambda i, j, k: (j, layer, 0, k))
    tile = pl.BlockSpec((tm, tn), lambda i, j, k: (i, j))
    if shard_out:
        out_specs = [pl.BlockSpec((None, tm, tn), lambda i, j, k: (j, i, 0))] * n_out
        out_shape = [jax.ShapeDtypeStruct((N_DEV, M, tn), dt) for dt in out_dtypes]
    else:
        out_specs = [tile] * n_out
        out_shape = [jax.ShapeDtypeStruct((M, N), dt) for dt in out_dtypes]
    size = lambda dt: jnp.dtype(dt).itemsize
    vmem = 2 * (tm * tk * size(a.dtype) + tk * tn * size(b_arr.dtype)
                + tm * tn * (sum(size(e.dtype) for e in extras) + sum(map(size, out_dtypes))))
    vmem += tm * tn * 4 * (nk > 1)
    assert vmem <= VMEM_LIMIT - (4 << 20), (name, vmem)
    outs = pl.pallas_call(
        body,
        name=name,
        grid=(M // tm, N // tn, nk),
        in_specs=[a_spec, b_spec] + [tile] * n_ex,
        out_specs=out_specs,
        out_shape=out_shape,
        scratch_shapes=[pltpu.VMEM((tm, tn), F32)] if nk > 1 else [],
        compiler_params=_params(("parallel", "parallel", "arbitrary")),
    )(a, b_arr, *extras)
    return outs[0] if n_out == 1 else outs


def _add(acc, res):
    return (acc + res,)


def _rms_fwd(x, g, *, name):
    R, C = x.shape
    tm = _tile(R, 256)

    def body(x_ref, g_ref, h_ref):
        xv = x_ref[...]
        r = lax.rsqrt(jnp.mean(xv * xv, axis=-1, keepdims=True) + EPS)
        h_ref[...] = (xv * r * g_ref[...]).astype(BF16)

    return pl.pallas_call(
        body,
        name=name,
        grid=(R // tm,),
        in_specs=[pl.BlockSpec((tm, C), lambda i: (i, 0)), pl.BlockSpec((1, C), lambda i: (0, 0))],
        out_specs=pl.BlockSpec((tm, C), lambda i: (i, 0)),
        out_shape=jax.ShapeDtypeStruct((R, C), BF16),
        compiler_params=_params(("parallel",)),
    )(x, g.reshape(1, C))


def _rms_bwd(x, g, dh, dres, *, name):
    R, C = x.shape
    tm = _tile(R, 256)

    def body(x_ref, g_ref, dh_ref, dres_ref, dx_ref, dg_ref):
        xv = x_ref[...]
        r = lax.rsqrt(jnp.mean(xv * xv, axis=-1, keepdims=True) + EPS)
        xn = xv * r
        dh_v = dh_ref[...].astype(F32)
        dxn = dh_v * g_ref[...]
        dx_ref[...] = r * (dxn - xn * jnp.mean(dxn * xn, axis=-1, keepdims=True)) + dres_ref[...]
        part = jnp.sum(dh_v * xn, axis=0, keepdims=True)

        @pl.when(pl.program_id(0) == 0)
        def _():
            dg_ref[...] = part

        @pl.when(pl.program_id(0) > 0)
        def _():
            dg_ref[...] += part

    row = pl.BlockSpec((tm, C), lambda i: (i, 0))
    vec = pl.BlockSpec((1, C), lambda i: (0, 0))
    dx, dg = pl.pallas_call(
        body,
        name=name,
        grid=(R // tm,),
        in_specs=[row, vec, row, row],
        out_specs=[row, vec],
        out_shape=[jax.ShapeDtypeStruct((R, C), F32), jax.ShapeDtypeStruct((1, C), F32)],
        compiler_params=_params(("arbitrary",)),
    )(x, g.reshape(1, C), dh, dres)
    return dx, dg.reshape(C)


def _final_loss(x, g, target, *, name):
    R, C = x.shape
    tm = _tile(R, 256)

    def body(x_ref, g_ref, t_ref, loss_ref, dx_ref, dg_ref):
        xv = x_ref[...]
        r = lax.rsqrt(jnp.mean(xv * xv, axis=-1, keepdims=True) + EPS)
        xn = xv * r
        gv = g_ref[...]
        err = xn * gv - t_ref[...]
        lpart = (0.5 / C) * jnp.sum(jnp.sum(err * err, axis=1, keepdims=True), axis=0, keepdims=True)
        dy = err * (1.0 / C)
        dxn = dy * gv
        dx_ref[...] = r * (dxn - xn * jnp.mean(dxn * xn, axis=-1, keepdims=True))
        gpart = jnp.sum(dy * xn, axis=0, keepdims=True)

        @pl.when(pl.program_id(0) == 0)
        def _():
            loss_ref[...] = lpart
            dg_ref[...] = gpart

        @pl.when(pl.program_id(0) > 0)
        def _():
            loss_ref[...] += lpart
            dg_ref[...] += gpart

    row = pl.BlockSpec((tm, C), lambda i: (i, 0))
    vec = pl.BlockSpec((1, C), lambda i: (0, 0))
    loss, dx, dg = pl.pallas_call(
        body,
        name=name,
        grid=(R // tm,),
        in_specs=[row, vec, row],
        out_specs=[pl.BlockSpec((1, 1), lambda i: (0, 0)), row, vec],
        out_shape=[jax.ShapeDtypeStruct((1, 1), F32), jax.ShapeDtypeStruct((R, C), F32), jax.ShapeDtypeStruct((1, C), F32)],
        compiler_params=_params(("arbitrary",)),
    )(x, g.reshape(1, C), target)
    return loss, dx, dg.reshape(C)


def _pool_select(lane, vals):
    out = vals[3]
    for gi in (2, 1, 0):
        out = jnp.where(lane < 64 * (gi + 1), vals[gi], out)
    return out


def _pool_diff(a):
    row, lane = _rows(a.shape), _lanes(a.shape)

    def down(v, k):
        return jnp.where(row >= k, pltpu.roll(v, k, 0), 0.0)

    s2 = a + down(a, 1)
    s4 = s2 + down(s2, 2)
    s8 = s4 + down(s4, 4)
    s16 = s8 + down(s8, 8)
    wsum = _pool_select(lane, (s2, s4, s8, s16))
    win = _pool_select(lane, (2, 4, 8, 16))
    cnt = jnp.minimum(row + 1, win).astype(F32)
    return wsum / cnt - a, cnt


def _pool_diff_t(dd, cnt):
    S = dd.shape[0]
    row, lane = _rows(dd.shape), _lanes(dd.shape)

    def up(v, k):
        return jnp.where(row < S - k, pltpu.roll(v, S - k, 0), 0.0)

    e = dd / cnt
    s2 = e + up(e, 1)
    s4 = s2 + up(s2, 2)
    s8 = s4 + up(s4, 4)
    s16 = s8 + up(s8, 8)
    return _pool_select(lane, (s2, s4, s8, s16)) - dd


def _pool_fwd(rest, wbd, scale, *, name):
    S = rest.shape[0]

    def body(a_ref, w_ref, s_ref, o_ref):
        d, _ = _pool_diff(a_ref[...])
        yp = _dot(d.astype(BF16), w_ref[...], 1, 0)
        o_ref[...] = (yp * s_ref[...]).astype(BF16)

    return pl.pallas_call(
        body,
        name=name,
        grid=(1,),
        in_specs=[
            pl.BlockSpec((S, POOL_W), lambda i: (0, OFF_A // POOL_W)),
            pl.BlockSpec((POOL_W, POOL_W), lambda i: (0, 0)),
            pl.BlockSpec((1, POOL_W), lambda i: (0, 0)),
        ],
        out_specs=pl.BlockSpec((S, POOL_W), lambda i: (0, 0)),
        out_shape=jax.ShapeDtypeStruct((S, POOL_W), BF16),
        compiler_params=_params(("arbitrary",)),
    )(rest, wbd, scale.reshape(1, POOL_W))


def _pool_bwd(rest, wbd, wbd_t, scale, dpa, *, name):
    S = rest.shape[0]

    def body(a_ref, w_ref, wt_ref, s_ref, dpa_ref, da_ref, dw_ref, ds_ref):
        d, cnt = _pool_diff(a_ref[...])
        db = d.astype(BF16)
        yp = _dot(db, w_ref[...], 1, 0)
        dpa_v = dpa_ref[...]
        ds_ref[...] = jnp.sum(dpa_v * yp, axis=0, keepdims=True)
        dyp = (dpa_v * s_ref[...]).astype(BF16)
        dw_ref[...] = _dot(db, dyp, 0, 0)
        dd = _dot(dyp, wt_ref[...], 1, 0)
        da_ref[...] = _pool_diff_t(dd, cnt).astype(BF16)

    full = pl.BlockSpec((S, POOL_W), lambda i: (0, 0))
    sq = pl.BlockSpec((POOL_W, POOL_W), lambda i: (0, 0))
    vec = pl.BlockSpec((1, POOL_W), lambda i: (0, 0))
    return pl.pallas_call(
        body,
        name=name,
        grid=(1,),
        in_specs=[pl.BlockSpec((S, POOL_W), lambda i: (0, OFF_A // POOL_W)), sq, sq, vec, full],
        out_specs=[full, sq, vec],
        out_shape=[
            jax.ShapeDtypeStruct((S, POOL_W), BF16),
            jax.ShapeDtypeStruct((POOL_W, POOL_W), F32),
            jax.ShapeDtypeStruct((1, POOL_W), F32),
        ],
        compiler_params=_params(("arbitrary",)),
    )(rest, wbd, wbd_t, scale.reshape(1, POOL_W), dpa)


def _log_sigmoid(z):
    return jnp.minimum(z, 0.0) - jnp.log(1.0 + jnp.exp(-jnp.abs(z)))


_F_SPEC_COL = OFF_F // F_LANES


def _fox_prep(rest, bpad, *, name):
    S = rest.shape[0]

    def body(f_ref, b_ref, o_ref, ot_ref):
        acc = _log_sigmoid(f_ref[...] + b_ref[...])
        row = _rows(acc.shape)
        k = 1
        while k < S:
            acc = acc + jnp.where(row >= k, pltpu.roll(acc, k, 0), 0.0)
            k *= 2
        o_ref[...] = acc
        ot_ref[...] = acc.T

    return pl.pallas_call(
        body,
        name=name,
        grid=(1,),
        in_specs=[pl.BlockSpec((S, F_LANES), lambda i: (0, _F_SPEC_COL)), pl.BlockSpec((1, F_LANES), lambda i: (0, 0))],
        out_specs=[pl.BlockSpec((S, F_LANES), lambda i: (0, 0)), pl.BlockSpec((F_LANES, S), lambda i: (0, 0))],
        out_shape=[jax.ShapeDtypeStruct((S, F_LANES), F32), jax.ShapeDtypeStruct((F_LANES, S), F32)],
        compiler_params=_params(("arbitrary",)),
    )(rest, bpad)


def _fox_post(rest, bpad, dcum, *, name):
    S = rest.shape[0]

    def body(f_ref, b_ref, d_ref, df_ref, db_ref):
        acc = d_ref[...]
        row = _rows(acc.shape)
        k = 1
        while k < S:
            acc = acc + jnp.where(row < S - k, pltpu.roll(acc, S - k, 0), 0.0)
            k *= 2
        df = acc * (1.0 - _sigmoid(f_ref[...] + b_ref[...]))
        df_ref[...] = df.astype(BF16)
        db_ref[...] = jnp.sum(df, axis=0, keepdims=True)

    full = pl.BlockSpec((S, F_LANES), lambda i: (0, 0))
    vec = pl.BlockSpec((1, F_LANES), lambda i: (0, 0))
    return pl.pallas_call(
        body,
        name=name,
        grid=(1,),
        in_specs=[pl.BlockSpec((S, F_LANES), lambda i: (0, _F_SPEC_COL)), vec, full],
        out_specs=[full, vec],
        out_shape=[jax.ShapeDtypeStruct((S, F_LANES), BF16), jax.ShapeDtypeStruct((1, F_LANES), F32)],
        compiler_params=_params(("arbitrary",)),
    )(rest, bpad, dcum)


_FOX_SCALE = FOX_DH ** -0.5
_PAIRS = FOX_H // 2


def _scaled(v):
    return (v.astype(F32) * _FOX_SCALE).astype(BF16)


def _head_lane(cum, h):
    return jnp.sum(jnp.where(_lanes(cum.shape) == h, cum, 0.0), axis=-1, keepdims=True)


def _diag_mask(s):
    return jnp.where(_rows(s.shape) >= _lanes(s.shape), s, NEG)


def _fox_fwd(qkv, cum, fk3, *, name):
    S = qkv.shape[0]
    nk, t = fk3.shape[1:]

    def body(q_ref, k_ref, v_ref, cum_ref, fk_ref, o_ref, lse_ref, m_sc, l_sc, acc_sc):
        hp, i = pl.program_id(0), pl.program_id(1)
        lane = _lanes((t, 128))
        lo = lane < FOX_DH
        qs = _scaled(q_ref[...])
        zero = jnp.zeros_like(qs)
        qm = (jnp.where(lo, qs, zero), jnp.where(lo, zero, qs))
        cumv = cum_ref[...]
        fq = [_head_lane(cumv, 2 * hp + e) for e in range(2)]
        m_sc[...] = jnp.full(m_sc.shape, NEG, F32)
        l_sc[...] = jnp.zeros(l_sc.shape, F32)
        acc_sc[...] = jnp.zeros(acc_sc.shape, F32)

        def tile(j, masked):
            k0 = pl.multiple_of(j * t, t)
            kb = k_ref[pl.ds(k0, t), :]
            vb = v_ref[pl.ds(k0, t), :]
            alphas, pvs = [], []
            for e in range(2):
                s = _dot(qm[e], kb, 1, 1) + fq[e] - fk_ref[2 * hp + e, pl.ds(j, 1), :]
                if masked:
                    s = _diag_mask(s)
                m_old = m_sc[e]
                m_new = jnp.maximum(m_old, jnp.max(s, axis=-1, keepdims=True))
                p = jnp.exp(s - m_new)
                alpha = jnp.exp(m_old - m_new)
                l_sc[e] = alpha * l_sc[e] + jnp.sum(p, axis=-1, keepdims=True)
                m_sc[e] = m_new
                alphas.append(alpha)
                pvs.append(_dot(p.astype(BF16), vb, 1, 0))
            acc_sc[...] = jnp.where(lo, alphas[0], alphas[1]) * acc_sc[...] + jnp.where(lo, pvs[0], pvs[1])

        def step(j, carry):
            tile(j, False)
            return carry

        lax.fori_loop(0, i, step, 0)
        tile(i, True)
        o_ref[...] = acc_sc[...] / jnp.where(lo, l_sc[0], l_sc[1])
        lse = [m_sc[e] + jnp.log(l_sc[e]) for e in range(2)]
        lse_ref[...] = jnp.where(lane == 0, lse[0], jnp.where(lane == 1, lse[1], 0.0))

    return pl.pallas_call(
        body,
        name=name,
        grid=(_PAIRS, S // t),
        in_specs=[
            pl.BlockSpec((t, 128), lambda hp, i: (i, hp)),
            pl.BlockSpec((S, 128), lambda hp, i: (0, _PAIRS + hp)),
            pl.BlockSpec((S, 128), lambda hp, i: (0, 2 * _PAIRS + hp)),
            pl.BlockSpec((t, F_LANES), lambda hp, i: (i, 0)),
            pl.BlockSpec((FOX_H, nk, t), lambda hp, i: (0, 0, 0)),
        ],
        out_specs=[pl.BlockSpec((t, 128), lambda hp, i: (i, hp)), pl.BlockSpec((None, t, 128), lambda hp, i: (hp, i, 0))],
        out_shape=[jax.ShapeDtypeStruct((S, FOX_W), F32), jax.ShapeDtypeStruct((_PAIRS, S, 128), F32)],
        scratch_shapes=[pltpu.VMEM((2, t, 1), F32), pltpu.VMEM((2, t, 1), F32), pltpu.VMEM((t, 128), F32)],
        compiler_params=_params(("parallel", "arbitrary")),
    )(qkv, qkv, qkv, cum, fk3)


def _fox_bwd(qkv, cum, fk3, o, do, lse, *, name):
    S = qkv.shape[0]
    nk, t = fk3.shape[1:]

    def body(q_ref, k_ref, v_ref, cum_ref, fk_ref, o_ref, do_ref, lse_ref, dq_ref, dk_ref, dv_ref, dfq_ref, dfk_ref,
             qm_sc, km_sc, dom_sc, delta_sc, fq_sc, dfq_sc, dq_sc):
        hp = pl.program_id(0)
        lane = _lanes((t, 128))
        lo = lane < FOX_DH

        def prep(i, carry):
            r = pl.ds(pl.multiple_of(i * t, t), t)
            qs, ks, dob = _scaled(q_ref[r, :]), _scaled(k_ref[r, :]), do_ref[r, :]
            prod = dob.astype(F32) * o_ref[r, :]
            cumv = cum_ref[r, :]
            zero = jnp.zeros_like(qs)
            for e in range(2):
                mine = lo if e == 0 else jnp.logical_not(lo)
                qm_sc[e, r, :] = jnp.where(mine, qs, zero)
                km_sc[e, r, :] = jnp.where(mine, ks, zero)
                dom_sc[e, r, :] = jnp.where(mine, dob, zero)
                delta_sc[e, r, :] = jnp.sum(jnp.where(mine, prod, 0.0), axis=-1, keepdims=True)
                fq_sc[e, r, :] = _head_lane(cumv, 2 * hp + e)
                dfq_sc[e, r, :] = jnp.zeros((t, 1), F32)
            dq_sc[r, :] = jnp.zeros((t, 128), F32)
            return carry

        lax.fori_loop(0, nk, prep, 0)

        def kv_tile(j, carry):
            kr = pl.ds(pl.multiple_of(j * t, t), t)
            kb, vb = k_ref[kr, :], v_ref[kr, :]
            fks = [fk_ref[2 * hp + e, pl.ds(j, 1), :] for e in range(2)]

            def q_tile(i, acc, masked):
                dk, dv, dfk0, dfk1 = acc
                dfk = [dfk0, dfk1]
                qr = pl.ds(pl.multiple_of(i * t, t), t)
                dq_t = jnp.zeros((t, 128), F32)
                for e in range(2):
                    qe, doe = qm_sc[e, qr, :], dom_sc[e, qr, :]
                    s = _dot(qe, kb, 1, 1) + fq_sc[e, qr, :] - fks[e]
                    if masked:
                        s = _diag_mask(s)
                    p = jnp.exp(s - lse_ref[qr, e:e + 1])
                    dv = dv + _dot(p.astype(BF16), doe, 0, 0)
                    dp = _dot(doe, vb, 1, 1)
                    ds = p * (dp - delta_sc[e, qr, :])
                    dsb = ds.astype(BF16)
                    dk = dk + _dot(dsb, qe, 0, 0)
                    dq_t = dq_t + _dot(dsb, km_sc[e, kr, :], 1, 0)
                    dfq_sc[e, qr, :] += jnp.sum(ds, axis=-1, keepdims=True)
                    dfk[e] = dfk[e] - jnp.sum(ds, axis=0, keepdims=True)
                dq_sc[qr, :] += dq_t
                return dk, dv, dfk[0], dfk[1]

            init = (jnp.zeros((t, 128), F32), jnp.zeros((t, 128), F32), jnp.zeros((1, t), F32), jnp.zeros((1, t), F32))
            acc = q_tile(j, init, True)
            dk, dv, dfk0, dfk1 = lax.fori_loop(j + 1, nk, functools.partial(q_tile, masked=False), acc)
            dk_ref[kr, :] = dk.astype(BF16)
            dv_ref[kr, :] = dv.astype(BF16)
            dfk_ref[0, pl.ds(j, 1), :] = dfk0
            dfk_ref[1, pl.ds(j, 1), :] = dfk1
            return carry

        lax.fori_loop(0, nk, kv_tile, 0)
        dq_ref[...] = dq_sc[...].astype(BF16)
        lane_s = _lanes((S, 128))
        dfq_ref[...] = jnp.where(lane_s == 0, dfq_sc[0], jnp.where(lane_s == 1, dfq_sc[1], 0.0))

    col = lambda c0: pl.BlockSpec((S, 128), lambda hp: (0, c0 + hp))
    pair = pl.BlockSpec((S, 128), lambda hp: (0, hp))
    lanes3 = pl.BlockSpec((None, S, 128), lambda hp: (hp, 0, 0))
    big = jax.ShapeDtypeStruct((S, FOX_W), BF16)
    masked_bf16 = pltpu.VMEM((2, S, 128), BF16)
    column = pltpu.VMEM((2, S, 1), F32)
    return pl.pallas_call(
        body,
        name=name,
        grid=(_PAIRS,),
        in_specs=[
            col(0), col(_PAIRS), col(2 * _PAIRS),
            pl.BlockSpec((S, F_LANES), lambda hp: (0, 0)),
            pl.BlockSpec((FOX_H, nk, t), lambda hp: (0, 0, 0)),
            pair, pair, lanes3,
        ],
        out_specs=[pair, pair, pair, lanes3, pl.BlockSpec((None, 2, nk, t), lambda hp: (hp, 0, 0, 0))],
        out_shape=[big, big, big, jax.ShapeDtypeStruct((_PAIRS, S, 128), F32), jax.ShapeDtypeStruct((_PAIRS, 2, nk, t), F32)],
        scratch_shapes=[masked_bf16, masked_bf16, masked_bf16, column, column, column, pltpu.VMEM((S, 128), F32)],
        compiler_params=_params(("parallel",)),
    )(qkv, qkv, qkv, cum, fk3, o, do, lse)


def _group_mask(lane, gi):
    return (lane >= 64 * gi) & (lane < 64 * (gi + 1))


_U_COL = OFF_C // SGU_W


def _sgu_fwd(rest, gn, wm, bias, *, name):
    S = rest.shape[0]
    ts = _tile(S, 512)
    nc = ts // SGU_CHUNK

    def body(u_ref, v_ref, g_ref, w_ref, b_ref, o_ref):
        zv = _gelu(v_ref[...])
        vn = zv * lax.rsqrt(jnp.mean(zv * zv, axis=-1, keepdims=True) + EPS) * g_ref[...]
        lane = _lanes((SGU_CHUNK, SGU_W))
        for c in range(nc):
            rows = slice(c * SGU_CHUNK, (c + 1) * SGU_CHUNK)
            vcb = vn[rows].astype(BF16)
            mixed = b_ref[...]
            for gi in range(4):
                mixed = mixed + jnp.where(_group_mask(lane, gi), _dot(w_ref[gi], vcb, 1, 0), 0.0)
            o_ref[rows, :] = (_gelu(u_ref[rows, :]) * mixed).astype(BF16)

    return pl.pallas_call(
        body,
        name=name,
        grid=(S // ts,),
        in_specs=[
            pl.BlockSpec((ts, SGU_W), lambda i: (i, _U_COL)),
            pl.BlockSpec((ts, SGU_W), lambda i: (i, _U_COL + 1)),
            pl.BlockSpec((1, SGU_W), lambda i: (0, 0)),
            pl.BlockSpec((4, SGU_CHUNK, SGU_CHUNK), lambda i: (0, 0, 0)),
            pl.BlockSpec((SGU_CHUNK, SGU_W), lambda i: (0, 0)),
        ],
        out_specs=pl.BlockSpec((ts, SGU_W), lambda i: (i, 0)),
        out_shape=jax.ShapeDtypeStruct((S, SGU_W), BF16),
        compiler_params=_params(("parallel",)),
    )(rest, rest, gn.reshape(1, SGU_W), wm, bias)


def _sgu_bwd(rest, gn, wm, wm_t, bias, dsg, *, name):
    S = rest.shape[0]
    ts = _tile(S, 512)
    nc = ts // SGU_CHUNK

    def body(u_ref, v_ref, g_ref, w_ref, wt_ref, b_ref, dsg_ref, dc_ref, dw_ref, db_ref, dg_ref):
        first = pl.program_id(0) == 0

        @pl.when(first)
        def _():
            dw_ref[...] = jnp.zeros_like(dw_ref)
            db_ref[...] = jnp.zeros_like(db_ref)
            dg_ref[...] = jnp.zeros_like(dg_ref)

        gv = g_ref[...]
        lane = _lanes((SGU_CHUNK, SGU_W))
        for c in range(nc):
            rows = slice(c * SGU_CHUNK, (c + 1) * SGU_CHUNK)
            vpre = v_ref[rows, :]
            upre = u_ref[rows, :]
            zv = _gelu(vpre)
            r = lax.rsqrt(jnp.mean(zv * zv, axis=-1, keepdims=True) + EPS)
            zn = zv * r
            vcb = (zn * gv).astype(BF16)
            mixed = b_ref[...]
            for gi in range(4):
                mixed = mixed + jnp.where(_group_mask(lane, gi), _dot(w_ref[gi], vcb, 1, 0), 0.0)
            zu = _gelu(upre)
            dsg_v = dsg_ref[rows, :]
            dc_ref[rows, :SGU_W] = (dsg_v * mixed * _gelu_grad(upre)).astype(BF16)
            dmixed = dsg_v * zu
            db_ref[...] += dmixed
            dvn = jnp.zeros((SGU_CHUNK, SGU_W), F32)
            for gi in range(4):
                dmg = jnp.where(_group_mask(lane, gi), dmixed, 0.0).astype(BF16)
                dw_ref[gi] += _dot(dmg, vcb, 1, 1)
                dvn = dvn + _dot(wt_ref[gi], dmg, 1, 0)
            dg_ref[...] += jnp.sum(dvn * zn, axis=0, keepdims=True)
            dzn = dvn * gv
            dzv = r * (dzn - zn * jnp.mean(dzn * zn, axis=-1, keepdims=True))
            dc_ref[rows, SGU_W:] = (dzv * _gelu_grad(vpre)).astype(BF16)

    blk = pl.BlockSpec((ts, SGU_W), lambda i: (i, 0))
    vec = pl.BlockSpec((1, SGU_W), lambda i: (0, 0))
    w3 = pl.BlockSpec((4, SGU_CHUNK, SGU_CHUNK), lambda i: (0, 0, 0))
    bsp = pl.BlockSpec((SGU_CHUNK, SGU_W), lambda i: (0, 0))
    return pl.pallas_call(
        body,
        name=name,
        grid=(S // ts,),
        in_specs=[
            pl.BlockSpec((ts, SGU_W), lambda i: (i, _U_COL)),
            pl.BlockSpec((ts, SGU_W), lambda i: (i, _U_COL + 1)),
            vec, w3, w3, bsp, blk,
        ],
        out_specs=[pl.BlockSpec((ts, 2 * SGU_W), lambda i: (i, 0)), w3, bsp, vec],
        out_shape=[
            jax.ShapeDtypeStruct((S, 2 * SGU_W), BF16),
            jax.ShapeDtypeStruct((4, SGU_CHUNK, SGU_CHUNK), F32),
            jax.ShapeDtypeStruct((SGU_CHUNK, SGU_W), F32),
            jax.ShapeDtypeStruct((1, SGU_W), F32),
        ],
        compiler_params=_params(("arbitrary",)),
    )(rest, rest, gn.reshape(1, SGU_W), wm, wm_t, bias, dsg)


_GT = 512
_G0 = OFF_G // _GT


def _gate_specs(tm, col_of):
    specs = [pl.BlockSpec((tm, _GT), functools.partial(lambda k, *ids: (col_of(*ids)[0], _G0 + 2 * k + col_of(*ids)[1]), k)) for k in range(3)]
    specs += [pl.BlockSpec((1, _GT), functools.partial(lambda k, *ids: (0, 2 * k + col_of(*ids)[1]), k)) for k in range(3)]
    return specs


def _merge_fwd(rest, bg, ya, yb, yc, *, name):
    S = rest.shape[0]
    tm = _tile(S, 512)

    def body(g1, g2, g3, b1, b2, b3, ya_ref, yb_ref, yc_ref, o_ref):
        acc = _sigmoid(g1[...] + b1[...]) * ya_ref[...]
        acc = acc + _sigmoid(g2[...] + b2[...]) * yb_ref[...]
        acc = acc + _sigmoid(g3[...] + b3[...]) * yc_ref[...]
        o_ref[...] = acc.astype(BF16)

    blk = pl.BlockSpec((tm, _GT), lambda i, j: (i, j))
    return pl.pallas_call(
        body,
        name=name,
        grid=(S // tm, D // _GT),
        in_specs=_gate_specs(tm, lambda i, j: (i, j)) + [blk, blk, blk],
        out_specs=blk,
        out_shape=jax.ShapeDtypeStruct((S, D), BF16),
        compiler_params=_params(("parallel", "parallel")),
    )(rest, rest, rest, bg, bg, bg, ya, yb, yc)


def _merge_bwd(rest, bg, ya, yb, yc, dm, *, name):
    S = rest.shape[0]
    tm = _tile(S, 512)

    def body(g1, g2, g3, b1, b2, b3, ya_ref, yb_ref, yc_ref, dm_ref, dya, dyb, dyc, dg1, dg2, dg3, db1, db2, db3):
        first = pl.program_id(1) == 0
        dmv = dm_ref[...]
        for g_ref, b_ref, y_ref, dy_ref, dg_ref, db_ref in (
            (g1, b1, ya_ref, dya, dg1, db1), (g2, b2, yb_ref, dyb, dg2, db2), (g3, b3, yc_ref, dyc, dg3, db3)):
            gate = _sigmoid(g_ref[...] + b_ref[...])
            dy_ref[...] = (dmv * gate).astype(BF16)
            dpre = dmv * y_ref[...] * gate * (1.0 - gate)
            dg_ref[...] = dpre.astype(BF16)
            part = jnp.sum(dpre, axis=0, keepdims=True)

            @pl.when(first)
            def _():
                db_ref[...] = part

            @pl.when(jnp.logical_not(first))
            def _():
                db_ref[...] += part

    blk = pl.BlockSpec((tm, _GT), lambda j, i: (i, j))
    vec = pl.BlockSpec((1, _GT), lambda j, i: (0, j))
    big = jax.ShapeDtypeStruct((S, D), BF16)
    small = jax.ShapeDtypeStruct((1, D), F32)
    return pl.pallas_call(
        body,
        name=name,
        grid=(D // _GT, S // tm),
        in_specs=_gate_specs(tm, lambda j, i: (i, j)) + [blk, blk, blk, blk],
        out_specs=[blk] * 6 + [vec] * 3,
        out_shape=[big] * 6 + [small] * 3,
        compiler_params=_params(("parallel", "arbitrary")),
    )(rest, rest, rest, bg, bg, bg, ya, yb, yc, dm)


_X_SCALE = XDH ** -0.5


def _xattn_fwd(xq, kv, *, name):
    S = xq.shape[0]
    M = kv.shape[0]
    tq = _tile(S, 512)

    def body(q_ref, k_ref, v_ref, o_ref):
        s = _dot(q_ref[...], k_ref[...], 1, 1) * _X_SCALE
        e = jnp.exp(s - jnp.max(s, axis=-1, keepdims=True))
        p = e / jnp.sum(e, axis=-1, keepdims=True)
        o_ref[...] = _dot(p.astype(BF16), v_ref[...], 1, 0).astype(BF16)

    return pl.pallas_call(
        body,
        name=name,
        grid=(S // tq, XH),
        in_specs=[
            pl.BlockSpec((tq, XDH), lambda i, h: (i, h)),
            pl.BlockSpec((M, XDH), lambda i, h: (0, h)),
            pl.BlockSpec((M, XDH), lambda i, h: (0, XH + h)),
        ],
        out_specs=pl.BlockSpec((tq, XDH), lambda i, h: (i, h)),
        out_shape=jax.ShapeDtypeStruct((S, D), BF16),
        compiler_params=_params(("parallel", "parallel")),
    )(xq, kv, kv)


def _xattn_bwd(xq, kv, do, *, name):
    S = xq.shape[0]
    M = kv.shape[0]
    tq = _tile(S, 512)

    def body(q_ref, k_ref, v_ref, do_ref, dq_ref, dk_ref, dv_ref):
        qb = q_ref[...]
        kb = k_ref[...]
        dob = do_ref[...]
        s = _dot(qb, kb, 1, 1) * _X_SCALE
        e = jnp.exp(s - jnp.max(s, axis=-1, keepdims=True))
        p = e / jnp.sum(e, axis=-1, keepdims=True)
        dp = _dot(dob, v_ref[...], 1, 1)
        ds = (p * (dp - jnp.sum(p * dp, axis=-1, keepdims=True)) * _X_SCALE).astype(BF16)
        dq_ref[...] = _dot(ds, kb, 1, 0).astype(BF16)
        dk_part = _dot(ds, qb, 0, 0)
        dv_part = _dot(p.astype(BF16), dob, 0, 0)

        @pl.when(pl.program_id(1) == 0)
        def _():
            dk_ref[...] = dk_part
            dv_ref[...] = dv_part

        @pl.when(pl.program_id(1) > 0)
        def _():
            dk_ref[...] += dk_part
            dv_ref[...] += dv_part

    qspec = pl.BlockSpec((tq, XDH), lambda h, i: (i, h))
    kspec = pl.BlockSpec((M, XDH), lambda h, i: (0, h))
    dxq, dxk, dxv = pl.pallas_call(
        body,
        name=name,
        grid=(XH, S // tq),
        in_specs=[qspec, kspec, pl.BlockSpec((M, XDH), lambda h, i: (0, XH + h)), qspec],
        out_specs=[qspec, kspec, kspec],
        out_shape=[jax.ShapeDtypeStruct((S, D), BF16), jax.ShapeDtypeStruct((M, D), F32), jax.ShapeDtypeStruct((M, D), F32)],
        compiler_params=_params(("parallel", "arbitrary")),
    )(xq, kv, kv, do)
    return dxq, jnp.concatenate([dxk, dxv], axis=1)


def _adam_math(w, g, m, v):
    m = ADAM_B1 * m + (1.0 - ADAM_B1) * g
    v = ADAM_B2 * v + (1.0 - ADAM_B2) * (g * g)
    m_hat = m / (1.0 - ADAM_B1 ** ADAM_STEP)
    v_hat = v / (1.0 - ADAM_B2 ** ADAM_STEP)
    delta = -ADAM_LR * (m_hat / (jnp.sqrt(v_hat) + ADAM_EPS) + ADAM_WD * w)
    return delta, m, v


def _adamw_sharded(parts, w, m, v, *, name):
    _, R, C = w.shape
    tm = _tile(R, 256)
    nr = R // tm

    def body(p0_ref, p1_ref, w_ref, m_ref, v_ref, g_ref, d_ref, mo_ref, vo_ref):
        def update(p_ref):
            g = p_ref[0].astype(F32)
            for dev in range(1, N_DEV):
                g = g + p_ref[dev].astype(F32)
            delta, mn, vn = _adam_math(w_ref[...], g, m_ref[...], v_ref[...])
            g_ref[...] = g
            d_ref[...] = delta
            mo_ref[...] = mn
            vo_ref[...] = vn

        @pl.when(pl.program_id(0) == 0)
        def _():
            update(p0_ref)

        @pl.when(pl.program_id(0) == 1)
        def _():
            update(p1_ref)

    p0 = pl.BlockSpec((N_DEV, tm, C), lambda l, i: (0, i * (1 - l) + (nr - 1) * l, 0))
    p1 = pl.BlockSpec((N_DEV, tm, C), lambda l, i: (0, i * l, 0))
    blk = pl.BlockSpec((None, tm, C), lambda l, i: (l, i, 0))
    sds = jax.ShapeDtypeStruct(w.shape, F32)
    return pl.pallas_call(
        body,
        name=name,
        grid=(DEPTH, nr),
        in_specs=[p0, p1, blk, blk, blk],
        out_specs=[blk] * 4,
        out_shape=[sds] * 4,
        compiler_params=_params(("arbitrary", "arbitrary")),
    )(parts[0], parts[1], w, m, v)


def _adamw_small(g, w, m, v, *, name):
    n = len(g)

    def body(*refs):
        g_refs, w_refs, m_refs, v_refs = (refs[k * n:(k + 1) * n] for k in range(4))
        d_out, m_out, v_out = (refs[(4 + k) * n:(5 + k) * n] for k in range(3))
        for t in range(n):
            delta, mn, vn = _adam_math(w_refs[t][...], g_refs[t][...], m_refs[t][...], v_refs[t][...])
            d_out[t][...] = delta
            m_out[t][...] = mn
            v_out[t][...] = vn

    vm = pl.BlockSpec(memory_space=pltpu.VMEM)
    shapes = [jax.ShapeDtypeStruct(a.shape, F32) for a in w]
    outs = pl.pallas_call(
        body,
        name=name,
        in_specs=[vm] * (4 * n),
        out_specs=[vm] * (3 * n),
        out_shape=shapes * 3,
        compiler_params=pltpu.CompilerParams(vmem_limit_bytes=VMEM_LIMIT),
    )(*g, *w, *m, *v)
    return outs[:n], outs[n:2 * n], outs[2 * n:]


def _position():
    return lax.axis_index("x"), lax.axis_index("y"), lax.axis_index("c")


def _dev_index(px, py, pc):
    return 4 * px + 2 * py + pc


_ANY = pl.BlockSpec(memory_space=pl.ANY)


def _all_gather(shards, *, name):
    n = len(shards)

    def body(*refs):
        ins, outs = refs[:n], refs[n:2 * n]
        send_sems, recv_sems, local_sems = refs[2 * n:]
        x, y, c = _position()
        me, sibling = (x, y, c), (x, y, 1 - c)
        chips = [(1 - x, y), (x, 1 - y), (1 - x, 1 - y)]

        def copy(t, k, block, to, src=None):
            dst = outs[t].at[_dev_index(*block)]
            return pltpu.make_async_remote_copy(
                src_ref=dst if src is None else src, dst_ref=dst, send_sem=send_sems.at[t, k], recv_sem=recv_sems.at[t, k],
                device_id=to, device_id_type=MESH)

        mine = [pltpu.make_async_copy(ins[t], outs[t].at[_dev_index(*me)], local_sems.at[t]) for t in range(n)]
        for cp in mine:
            cp.start()
        started = []
        for j, chip in enumerate(chips):
            for t in range(n):
                started.append(copy(t, 1 + j, me, (*chip, c), src=ins[t]))
                started[-1].start()
        for t in range(n):
            started.append(copy(t, 0, me, sibling, src=ins[t]))
            started[-1].start()
        for j, chip in enumerate(chips):
            for t in range(n):
                copy(t, 1 + j, (*chip, c), me).wait_recv()
                started.append(copy(t, 4 + j, (*chip, c), sibling))
                started[-1].start()
        for t in range(n):
            copy(t, 0, sibling, me).wait_recv()
        for j, chip in enumerate(chips):
            for t in range(n):
                copy(t, 4 + j, (*chip, 1 - c), me).wait_recv()
        for cp in started:
            cp.wait_send()
        for cp in mine:
            cp.wait()

    return pl.pallas_call(
        body,
        name=name,
        in_specs=[_ANY] * n,
        out_specs=[_ANY] * n,
        out_shape=[jax.ShapeDtypeStruct((N_DEV, *s.shape), s.dtype) for s in shards],
        scratch_shapes=[pltpu.SemaphoreType.DMA((n, 7)), pltpu.SemaphoreType.DMA((n, 7)), pltpu.SemaphoreType.DMA((n,))],
        compiler_params=pltpu.CompilerParams(has_side_effects=True),
    )(*shards)


def _peers(x, y, c):
    out = []
    for mask in range(1, N_DEV):
        fx, fy, fc = (mask >> 2) & 1, (mask >> 1) & 1, mask & 1
        out.append((1 - x if fx else x, 1 - y if fy else y, 1 - c if fc else c))
    return out


def _exchange(parts, *, name):
    n = len(parts)

    def body(*refs):
        ins, outs = refs[:n], refs[n:2 * n]
        send_sems, recv_sems, local_sems = refs[2 * n:]
        x, y, c = _position()
        me = _dev_index(x, y, c)
        peers = _peers(x, y, c)

        def copy(t, k):
            peer = peers[k]
            return pltpu.make_async_remote_copy(
                src_ref=ins[t].at[_dev_index(*peer)], dst_ref=outs[t].at[me], send_sem=send_sems.at[t, k],
                recv_sem=recv_sems.at[t, k], device_id=peer, device_id_type=MESH)

        def arrival(t, k):
            peer = peers[k]
            dst = outs[t].at[_dev_index(*peer)]
            return pltpu.make_async_remote_copy(
                src_ref=dst, dst_ref=dst, send_sem=send_sems.at[t, k], recv_sem=recv_sems.at[t, k],
                device_id=peer, device_id_type=MESH)

        mine = [pltpu.make_async_copy(ins[t].at[me], outs[t].at[me], local_sems.at[t]) for t in range(n)]
        for cp in mine:
            cp.start()
        started = [copy(t, k) for k in range(N_DEV - 1) for t in range(n)]
        for cp in started:
            cp.start()
        for k in range(N_DEV - 1):
            for t in range(n):
                arrival(t, k).wait_recv()
        for cp in started:
            cp.wait_send()
        for cp in mine:
            cp.wait()

    return pl.pallas_call(
        body,
        name=name,
        in_specs=[_ANY] * n,
        out_specs=[_ANY] * n,
        out_shape=[jax.ShapeDtypeStruct(p.shape, p.dtype) for p in parts],
        scratch_shapes=[pltpu.SemaphoreType.DMA((n, 7)), pltpu.SemaphoreType.DMA((n, 7)), pltpu.SemaphoreType.DMA((n,))],
        compiler_params=pltpu.CompilerParams(has_side_effects=True),
    )(*parts)


def _all_reduce(g_local, *, name):
    R, C = g_local.shape

    def body(g_ref, o_ref, buf, send_sems, recv_sems):
        x, y, c = _position()
        me = _dev_index(x, y, c)
        peers = _peers(x, y, c)
        copies = []
        for k, peer in enumerate(peers):
            copies.append(pltpu.make_async_remote_copy(
                src_ref=g_ref, dst_ref=buf.at[me], send_sem=send_sems.at[k], recv_sem=recv_sems.at[k],
                device_id=peer, device_id_type=MESH))
            copies[-1].start()
        buf[me] = g_ref[...]
        for k, peer in enumerate(peers):
            dst = buf.at[_dev_index(*peer)]
            pltpu.make_async_remote_copy(
                src_ref=dst, dst_ref=dst, send_sem=send_sems.at[k], recv_sem=recv_sems.at[k],
                device_id=peer, device_id_type=MESH).wait_recv()
        for cp in copies:
            cp.wait_send()
        g = buf[0]
        for dev in range(1, N_DEV):
            g = g + buf[dev]
        o_ref[...] = g

    vm = pl.BlockSpec(memory_space=pltpu.VMEM)
    return pl.pallas_call(
        body,
        name=name,
        in_specs=[vm],
        out_specs=vm,
        out_shape=jax.ShapeDtypeStruct((R, C), F32),
        scratch_shapes=[pltpu.VMEM((N_DEV, R, C), F32), pltpu.SemaphoreType.DMA((7,)), pltpu.SemaphoreType.DMA((7,))],
        compiler_params=pltpu.CompilerParams(has_side_effects=True, vmem_limit_bytes=VMEM_LIMIT),
    )(g_local)


def _block_diag(w):
    out = jnp.zeros((POOL_W, POOL_W), w.dtype)
    for gi in range(4):
        out = out.at[64 * gi:64 * (gi + 1), 64 * gi:64 * (gi + 1)].set(w[gi])
    return out


def _layer_consts(sp, l):
    causal = jnp.tril(jnp.ones((SGU_CHUNK, SGU_CHUNK), F32))
    wm = (sp["sgu_w"][l] * causal[None]).astype(BF16)
    wbd = _block_diag(sp["pool_w"][l]).astype(BF16)
    return dict(
        wbd=wbd, wbd_t=wbd.T, wm=wm, wm_t=wm.transpose(0, 2, 1),
        sgu_bias=jnp.repeat(sp["sgu_b"][l].T, 64, axis=1),
        bpad=jnp.pad(sp["b_forget"][l], (0, F_LANES - FOX_H)).reshape(1, F_LANES),
        bg=sp["b_gate"][l].reshape(1, 3 * D),
    )


def _relu2(acc):
    return acc, jnp.square(jnp.maximum(acc, 0.0))


def _relu2_grad(acc, z):
    return (acc * 2.0 * jnp.maximum(z, 0.0),)


def _layer_fwd(l, x, mem, W, sp):
    S = x.shape[0]
    t = _tile(S, 256)
    c = _layer_consts(sp, l)
    n = f"l{l}_"
    h = _rms_fwd(x, sp["norm_mix_g"][l], name=n + "norm_mix")
    qkv = _mm(h, W["qkv"], out_dtypes=(BF16,), name=n + "qkv")
    rest = _mm(h, W["rest"], name=n + "rest")
    pa = _pool_fwd(rest, c["wbd"], sp["pool_scale"][l], name=n + "pool")
    cum, cum_t = _fox_prep(rest, c["bpad"], name=n + "fox_prep")
    fk3 = cum_t[:FOX_H].reshape(FOX_H, S // t, t)
    o, lse = _fox_fwd(qkv, cum, fk3, name=n + "fox")
    sg = _sgu_fwd(rest, sp["sgu_norm_g"][l], c["wm"], c["sgu_bias"], name=n + "sgu")
    ya = _mm(pa, W["ba"], name=n + "branch_a")
    yb = _mm(o, W["bb"], name=n + "branch_b")
    yc = _mm(sg, W["bc"], name=n + "branch_c")
    merged = _merge_fwd(rest, c["bg"], ya, yb, yc, name=n + "merge")
    x1 = _mm(merged, W["out"], extras=(x,), epilogue=_add, name=n + "out")
    hx = _rms_fwd(x1, sp["norm_xattn_g"][l], name=n + "norm_xattn")
    hm = _rms_fwd(mem, sp["norm_mem_g"][l], name=n + "norm_mem")
    xq = _mm(hx, W["xq"], out_dtypes=(BF16,), name=n + "xq")
    kv = _mm(hm, W["xkv"], out_dtypes=(BF16,), name=n + "xkv")
    o2 = _xattn_fwd(xq, kv, name=n + "xattn")
    x2 = _mm(o2, W["xo"], extras=(x1,), epilogue=_add, name=n + "xo")
    hf = _rms_fwd(x2, sp["norm_ffn_g"][l], name=n + "norm_ffn")
    z, act = _mm(hf, W["ff1"], epilogue=_relu2, out_dtypes=(F32, BF16), name=n + "ff1")
    x3 = _mm(act, W["ff2"], extras=(x2,), epilogue=_add, name=n + "ff2")
    saved = dict(x=x, h=h, qkv=qkv, rest=rest, pa=pa, cum=cum, fk3=fk3, o=o, lse=lse, sg=sg, ya=ya, yb=yb, yc=yc,
                 merged=merged, x1=x1, hx=hx, hm=hm, xq=xq, kv=kv, o2=o2, x2=x2, hf=hf, z=z, act=act, c=c)
    return x3, saved


def _layer_bwd(l, dx3, sv, mem, W, sp):
    S = dx3.shape[0]
    c = sv["c"]
    n = f"l{l}b_"
    bf = dict(out_dtypes=(BF16,))
    gw, gs = {}, {}
    gw["ff2"] = _mm(sv["act"], dx3, ta=True, name=n + "dw_ff2", **bf)
    dz = _mm(dx3, W["ff2"], tb=True, extras=(sv["z"],), epilogue=_relu2_grad, name=n + "dz", **bf)
    gw["ff1"] = _mm(sv["hf"], dz, ta=True, shard_out=True, name=n + "dw_ff1", **bf)
    dhf = _mm(dz, W["ff1"], tb=True, name=n + "dhf")
    dx2, gs["norm_ffn_g"] = _rms_bwd(sv["x2"], sp["norm_ffn_g"][l], dhf, dx3, name=n + "dnorm_ffn")
    gw["xo"] = _mm(sv["o2"], dx2, ta=True, name=n + "dw_xo", **bf)
    do2 = _mm(dx2, W["xo"], tb=True, name=n + "do2", **bf)
    dxq, dkv = _xattn_bwd(sv["xq"], sv["kv"], do2, name=n + "dxattn")
    gw["xq"] = _mm(sv["hx"], dxq, ta=True, name=n + "dw_xq", **bf)
    gw["xkv"] = _mm(sv["hm"], dkv, ta=True, shard_out=True, name=n + "dw_xkv", **bf)
    dhm = _mm(dkv, W["xkv"], tb=True, name=n + "dhm")
    _, gs["norm_mem_g"] = _rms_bwd(mem, sp["norm_mem_g"][l], dhm, jnp.zeros_like(mem), name=n + "dnorm_mem")
    dhx = _mm(dxq, W["xq"], tb=True, name=n + "dhx")
    dx1, gs["norm_xattn_g"] = _rms_bwd(sv["x1"], sp["norm_xattn_g"][l], dhx, dx2, name=n + "dnorm_xattn")
    gw["out"] = _mm(sv["merged"], dx1, ta=True, name=n + "dw_out", **bf)
    dm = _mm(dx1, W["out"], tb=True, name=n + "dmerged")
    dya, dyb, dyc, dg1, dg2, dg3, db1, db2, db3 = _merge_bwd(sv["rest"], c["bg"], sv["ya"], sv["yb"], sv["yc"], dm, name=n + "dmerge")
    gs["b_gate"] = jnp.concatenate([db1, db2, db3], axis=1).reshape(3 * D)
    gw["ba"] = _mm(sv["pa"], dya, ta=True, shard_out=True, name=n + "dw_ba", **bf)
    gw["bb"] = _mm(sv["o"], dyb, ta=True, shard_out=True, name=n + "dw_bb", **bf)
    gw["bc"] = _mm(sv["sg"], dyc, ta=True, shard_out=True, name=n + "dw_bc", **bf)
    dpa = _mm(dya, W["ba"], tb=True, name=n + "dpa")
    do = _mm(dyb, W["bb"], tb=True, name=n + "do", **bf)
    dsg = _mm(dyc, W["bc"], tb=True, name=n + "dsg")
    da, dwbd, dscale = _pool_bwd(sv["rest"], c["wbd"], c["wbd_t"], sp["pool_scale"][l], dpa, name=n + "dpool")
    gs["pool_w"] = jnp.stack([dwbd[64 * gi:64 * (gi + 1), 64 * gi:64 * (gi + 1)] for gi in range(4)])
    gs["pool_scale"] = dscale.reshape(POOL_W)
    dq, dk, dv, dfq, dfk = _fox_bwd(sv["qkv"], sv["cum"], sv["fk3"], sv["o"], do, sv["lse"], name=n + "dfox")
    dcum = dfq[:, :, :2].transpose(1, 0, 2).reshape(S, FOX_H) + dfk.reshape(FOX_H, S).T
    df, dbf = _fox_post(sv["rest"], c["bpad"], jnp.pad(dcum, ((0, 0), (0, F_LANES - FOX_H))), name=n + "dfox_post")
    gs["b_forget"] = dbf[0, :FOX_H]
    dc, dwm, dbias, dgn = _sgu_bwd(sv["rest"], sp["sgu_norm_g"][l], c["wm"], c["wm_t"], c["sgu_bias"], dsg, name=n + "dsgu")
    gs["sgu_w"] = dwm * jnp.tril(jnp.ones((SGU_CHUNK, SGU_CHUNK), F32))[None]
    gs["sgu_b"] = dbias.reshape(SGU_CHUNK, 4, 64).sum(axis=2).T
    gs["sgu_norm_g"] = dgn.reshape(SGU_W)
    dqkv = jnp.concatenate([dq, dk, dv], axis=1)
    drest = jnp.concatenate([da, df, jnp.zeros((S, OFF_C - OFF_F - F_LANES), BF16), dc, dg1, dg2, dg3], axis=1)
    gw["qkv"] = _mm(sv["h"], dqkv, ta=True, name=n + "dw_qkv", **bf)
    gw["rest"] = _mm(sv["h"], drest, ta=True, name=n + "dw_rest", **bf)
    dh = _mm(dqkv, W["qkv"], tb=True, name=n + "dh_qkv")
    dh = _mm(drest, W["rest"], tb=True, extras=(dh,), epilogue=_add, name=n + "dh")
    dx, gs["norm_mix_g"] = _rms_bwd(sv["x"], sp["norm_mix_g"][l], dh, dx1, name=n + "dnorm_mix")
    return dx, gw, gs


def _local_step(x, mem, target, Ws, sp):
    saved = []
    for l in range(DEPTH):
        x, sv = _layer_fwd(l, x, mem, Ws[l], sp)
        saved.append(sv)
    loss, dx, dgf = _final_loss(x, sp["final_norm_g"], target, name="final_loss")
    gws, gss = [None] * DEPTH, [None] * DEPTH
    for l in reversed(range(DEPTH)):
        dx, gws[l], gss[l] = _layer_bwd(l, dx, saved[l], mem, Ws[l], sp)
    small = {k: jnp.stack([gss[l][k] for l in range(DEPTH)]) for k in gss[0]}
    small["final_norm_g"] = dgf
    return loss, dx, gws, small


_SMALL = ["norm_mix_g", "b_forget", "pool_w", "pool_scale", "sgu_norm_g", "sgu_w", "sgu_b", "b_gate", "norm_xattn_g",
          "norm_mem_g", "norm_ffn_g", "final_norm_g"]
_COL = {"w_branch_a": "ba", "w_branch_b": "bb", "w_branch_c": "bc", "w_xkv": "xkv", "w_ff1": "ff1"}
_ROW = {"w_out": "out", "w_xq": "xq", "w_xo": "xo", "w_ff2": "ff2"}
_BIG = ["w_in", "w_branch_a", "w_branch_b", "w_branch_c", "w_out", "w_xq", "w_xkv", "w_xo", "w_ff1", "w_ff2"]
_PACK_LANES = 128


def _as_rows(a):
    return a.reshape(-1, a.shape[-1])


def _pack(tensors):
    rows = []
    for a in tensors:
        flat = a.reshape(-1)
        flat = jnp.pad(flat, (0, (-flat.shape[0]) % (8 * _PACK_LANES)))
        rows.append(flat.reshape(-1, _PACK_LANES))
    return jnp.concatenate(rows, axis=0)


def _unpack(packed, like):
    out, r = [], 0
    for a in like:
        size = math.prod(a.shape)
        nr = 8 * (-(-size // (8 * _PACK_LANES)))
        out.append(packed[r:r + nr].reshape(-1)[:size].reshape(a.shape))
        r += nr
    return out


def _split_w_in(w_in):
    K = w_in.shape[0]
    pad = lambda n: jnp.zeros((K, n), w_in.dtype)
    rest = jnp.concatenate(
        [w_in[:, :R_OFF_Q], w_in[:, R_OFF_F:R_OFF_C], pad(OFF_C - OFF_F - FOX_H), w_in[:, R_OFF_C:]], axis=1)
    return w_in[:, R_OFF_Q:R_OFF_F], rest


def _join_w_in(qkv, rest):
    return jnp.concatenate([rest[:, :R_OFF_Q], qkv, rest[:, OFF_F:OFF_F + FOX_H], rest[:, OFF_C:]], axis=1)


def _layer_weights(gathered, l):
    w_in = gathered["w_in"][:, l].transpose(1, 0, 2).reshape(D, N_IN)
    W = dict(zip(("qkv", "rest"), _split_w_in(w_in)))
    for name, key in _COL.items():
        W[key] = _Gathered(gathered[name], "col", l)
    for name, key in _ROW.items():
        W[key] = _Gathered(gathered[name], "row", l)
    return W


def _grad_blocks(gw):
    parts = {"w_in": _join_w_in(gw["qkv"], gw["rest"]).reshape(D, N_DEV, -1).transpose(1, 0, 2)}
    for name, key in _COL.items():
        parts[name] = gw[key]
    for name, key in _ROW.items():
        parts[name] = gw[key].reshape(N_DEV, -1, gw[key].shape[-1])
    return parts


def kernel(x, mem, norm_mix_g, w_in, b_forget, pool_w, pool_scale, sgu_norm_g, sgu_w, sgu_b, w_branch_a, w_branch_b, w_branch_c, b_gate, w_out, norm_xattn_g, norm_mem_g, w_xq, w_xkv, w_xo, norm_ffn_g, w_ff1, w_ff2, final_norm_g, loss_target, m_norm_mix_g, m_w_in, m_b_forget, m_pool_w, m_pool_scale, m_sgu_norm_g, m_sgu_w, m_sgu_b, m_w_branch_a, m_w_branch_b, m_w_branch_c, m_b_gate, m_w_out, m_norm_xattn_g, m_norm_mem_g, m_w_xq, m_w_xkv, m_w_xo, m_norm_ffn_g, m_w_ff1, m_w_ff2, m_final_norm_g, v_norm_mix_g, v_w_in, v_b_forget, v_pool_w, v_pool_scale, v_sgu_norm_g, v_sgu_w, v_sgu_b, v_w_branch_a, v_w_branch_b, v_w_branch_c, v_b_gate, v_w_out, v_norm_xattn_g, v_norm_mem_g, v_w_xq, v_w_xkv, v_w_xo, v_norm_ffn_g, v_w_ff1, v_w_ff2, v_final_norm_g):
    names = ["norm_mix_g", "w_in", "b_forget", "pool_w", "pool_scale", "sgu_norm_g", "sgu_w", "sgu_b", "w_branch_a", "w_branch_b",
             "w_branch_c", "b_gate", "w_out", "norm_xattn_g", "norm_mem_g", "w_xq", "w_xkv", "w_xo", "norm_ffn_g", "w_ff1", "w_ff2",
             "final_norm_g"]
    w = dict(zip(names, [norm_mix_g, w_in, b_forget, pool_w, pool_scale, sgu_norm_g, sgu_w, sgu_b, w_branch_a, w_branch_b, w_branch_c,
                         b_gate, w_out, norm_xattn_g, norm_mem_g, w_xq, w_xkv, w_xo, norm_ffn_g, w_ff1, w_ff2, final_norm_g]))
    m = dict(zip(names, [m_norm_mix_g, m_w_in, m_b_forget, m_pool_w, m_pool_scale, m_sgu_norm_g, m_sgu_w, m_sgu_b, m_w_branch_a,
                         m_w_branch_b, m_w_branch_c, m_b_gate, m_w_out, m_norm_xattn_g, m_norm_mem_g, m_w_xq, m_w_xkv, m_w_xo,
                         m_norm_ffn_g, m_w_ff1, m_w_ff2, m_final_norm_g]))
    v = dict(zip(names, [v_norm_mix_g, v_w_in, v_b_forget, v_pool_w, v_pool_scale, v_sgu_norm_g, v_sgu_w, v_sgu_b, v_w_branch_a,
                         v_w_branch_b, v_w_branch_c, v_b_gate, v_w_out, v_norm_xattn_g, v_norm_mem_g, v_w_xq, v_w_xkv, v_w_xo,
                         v_norm_ffn_g, v_w_ff1, v_w_ff2, v_final_norm_g]))

    gathered = dict(zip(_BIG, _all_gather([w[k].astype(BF16) for k in _BIG], name="gather_weights")))
    Ws = [_layer_weights(gathered, l) for l in range(DEPTH)]
    sp = {k: w[k] for k in _SMALL}
    loss, dx, gws, small = _local_step(x[0], mem[0], loss_target[0], Ws, sp)
    loss = lax.psum(loss[0, 0], ("x", "y", "c"))

    received = []
    for l in range(DEPTH):
        blocks = _grad_blocks(gws[l])
        received.append(dict(zip(_BIG, _exchange([blocks[k] for k in _BIG], name=f"exchange_grads_l{l}"))))

    grads, deltas, new_m, new_v = {}, {}, {}, {}
    for k in _BIG:
        outs = _adamw_sharded([received[l][k] for l in range(DEPTH)], w[k], m[k], v[k], name="adamw_" + k)
        grads[k], deltas[k], new_m[k], new_v[k] = outs
    like = [w[k] for k in _SMALL]
    g_small = _unpack(_all_reduce(_pack([small[k] for k in _SMALL]), name="all_reduce_small"), like)
    rows = lambda d: [_as_rows(d[k]) for k in _SMALL]
    outs = _adamw_small([_as_rows(g) for g in g_small], rows(w), rows(m), rows(v), name="adamw_small")
    grads.update(zip(_SMALL, g_small))
    for dst, vals in zip((deltas, new_m, new_v), outs):
        dst.update({k: a.reshape(w[k].shape) for k, a in zip(_SMALL, vals)})

    return (loss, dx[None], *[grads[k] for k in names], *[deltas[k] for k in names], *[new_m[k] for k in names],
            *[new_v[k] for k in names])
```

```python
import functools
import math

import jax
import jax.numpy as jnp
from jax import lax
from jax.experimental import pallas as pl
from jax.experimental.pallas import tpu as pltpu

F32 = jnp.float32
BF16 = jnp.bfloat16
MESH = pl.DeviceIdType.MESH

N_DEV = 8
D = 1024
DEPTH = 2
EPS = 1e-6
NEG = -1e30
POOL_W = 256
FOX_H = 8
FOX_DH = 64
FOX_W = 512
SGU_W = 256
SGU_CHUNK = 128
XH = 4
XDH = 256
N_IN = 5384
R_OFF_Q, R_OFF_F, R_OFF_C = 256, 1792, 1800
QKV_W = 3 * FOX_W
OFF_A, OFF_F, OFF_C, OFF_G, REST_W = 0, 256, 512, 1024, 4096
F_LANES = 128

ADAM_LR = 0.001
ADAM_B1 = 0.9
ADAM_B2 = 0.999
ADAM_EPS = 1e-08
ADAM_WD = 0.01
ADAM_STEP = 10

VMEM_LIMIT = 56 * 1024 * 1024


def _tile(n, pref):
    t = min(n, pref)
    while n % t:
        t -= 128
    assert t > 0, (n, pref)
    return t


def _params(sem=None):
    return pltpu.CompilerParams(dimension_semantics=sem, vmem_limit_bytes=VMEM_LIMIT)


def _dot(a, b, ca, cb):
    return lax.dot_general(a, b, (((ca,), (cb,)), ((), ())), preferred_element_type=F32)


def _sigmoid(z):
    return 1.0 / (1.0 + jnp.exp(-z))


_GELU_K = math.sqrt(2.0 / math.pi)
_GELU_C = 0.044715


def _gelu(x):
    return 0.5 * x * (1.0 + jnp.tanh(_GELU_K * (x + _GELU_C * x * x * x)))


def _gelu_grad(x):
    t = jnp.tanh(_GELU_K * (x + _GELU_C * x * x * x))
    return 0.5 * (1.0 + t) + 0.5 * x * (1.0 - t * t) * _GELU_K * (1.0 + 3.0 * _GELU_C * x * x)


def _rows(shape):
    return lax.broadcasted_iota(jnp.int32, shape, 0)


def _lanes(shape):
    return lax.broadcasted_iota(jnp.int32, shape, 1)


class _Gathered:
    def __init__(self, arr):
        self.arr = arr
        self.shape = (arr.shape[1], N_DEV * arr.shape[2])


_TOKEN = (8, 128)


def _mm(a, b, *, ta=False, tb=False, extras=(), epilogue=None, out_dtypes=(F32,), shard_out=False, after=None, tm=None, tn=512, tk=None,
        name):
    M, K = (a.shape[1], a.shape[0]) if ta else a.shape
    N, Kb = b.shape if tb else b.shape[::-1]
    assert Kb == K, (a.shape, b.shape, ta, tb)
    gathered = isinstance(b, _Gathered)
    if gathered:
        if tb:
            tk = b.arr.shape[2]
        else:
            tn = b.arr.shape[2]
    if shard_out:
        tn = N // N_DEV
    tm = _tile(M, tm or (1024 if ta else 2048))
    tn = _tile(N, tn)
    tk = _tile(K, tk or (2048 if ta else 1024))
    nk = K // tk
    ca, cb = (0 if ta else 1), (1 if tb else 0)
    n_ex, n_out = len(extras), len(out_dtypes)
    tokens = [] if after is None else [after]
    n_in = 2 + n_ex + len(tokens)
    if epilogue is None:
        epilogue = lambda acc: (acc,)

    def body(*refs):
        a_ref, b_ref = refs[:2]
        ex_refs = refs[2:2 + n_ex]
        o_refs = refs[n_in:n_in + n_out]
        part = _dot(a_ref[...].astype(BF16), b_ref[...].astype(BF16), ca, cb)

        def finish(acc):
            for o_ref, val in zip(o_refs, epilogue(acc, *[e[...] for e in ex_refs])):
                o_ref[...] = val.astype(o_ref.dtype)

        if nk == 1:
            finish(part)
        else:
            acc_ref = refs[-1]
            k = pl.program_id(2)

            @pl.when(k == 0)
            def _():
                acc_ref[...] = part

            @pl.when(k > 0)
            def _():
                acc_ref[...] += part

            @pl.when(k == nk - 1)
            def _():
                finish(acc_ref[...])

    a_spec = pl.BlockSpec((tk, tm), lambda i, j, k: (k, i)) if ta else pl.BlockSpec((tm, tk), lambda i, j, k: (i, k))
    if not gathered:
        b_arr = b
        b_spec = pl.BlockSpec((tn, tk), lambda i, j, k: (j, k)) if tb else pl.BlockSpec((tk, tn), lambda i, j, k: (k, j))
    else:
        b_arr = b.arr
        if tb:
            b_spec = pl.BlockSpec((None, tn, tk), lambda i, j, k: (k, j, 0))
        else:
            b_spec = pl.BlockSpec((None, tk, tn), lambda i, j, k: (j, k, 0))
    tile = pl.BlockSpec((tm, tn), lambda i, j, k: (i, j))
    if shard_out:
        out_specs = [pl.BlockSpec((None, tm, tn), lambda i, j, k: (j, i, 0))] * n_out
        out_shape = [jax.ShapeDtypeStruct((N_DEV, M, tn), dt) for dt in out_dtypes]
    else:
        out_specs = [tile] * n_out
        out_shape = [jax.ShapeDtypeStruct((M, N), dt) for dt in out_dtypes]
    size = lambda dt: jnp.dtype(dt).itemsize
    vmem = 2 * (tm * tk * size(a.dtype) + tk * tn * size(b_arr.dtype)
                + tm * tn * (sum(size(e.dtype) for e in extras) + sum(map(size, out_dtypes))))
    vmem += tm * tn * 4 * (nk > 1)
    assert vmem <= VMEM_LIMIT - (4 << 20), (name, vmem)
    outs = pl.pallas_call(
        body,
        name=name,
        grid=(M // tm, N // tn, nk),
        in_specs=[a_spec, b_spec] + [tile] * n_ex + [pl.BlockSpec(_TOKEN, lambda i, j, k: (0, 0))] * len(tokens),
        out_specs=out_specs,
        out_shape=out_shape,
        scratch_shapes=[pltpu.VMEM((tm, tn), F32)] if nk > 1 else [],
        compiler_params=_params(("parallel", "parallel", "arbitrary")),
    )(a, b_arr, *extras, *tokens)
    return outs[0] if n_out == 1 else outs


def _add(acc, res):
    return (acc + res,)


def _rms_fwd(x, g, *, after=None, name):
    R, C = x.shape
    tm = _tile(R, 256)
    tokens = [] if after is None else [after]

    def body(x_ref, g_ref, *rest):
        xv = x_ref[...]
        r = lax.rsqrt(jnp.mean(xv * xv, axis=-1, keepdims=True) + EPS)
        rest[-1][...] = (xv * r * g_ref[...]).astype(BF16)

    return pl.pallas_call(
        body,
        name=name,
        grid=(R // tm,),
        in_specs=[pl.BlockSpec((tm, C), lambda i: (i, 0)), pl.BlockSpec((1, C), lambda i: (0, 0))]
        + [pl.BlockSpec(_TOKEN, lambda i: (0, 0))] * len(tokens),
        out_specs=pl.BlockSpec((tm, C), lambda i: (i, 0)),
        out_shape=jax.ShapeDtypeStruct((R, C), BF16),
        compiler_params=_params(("parallel",)),
    )(x, g.reshape(1, C), *tokens)


def _rms_bwd(x, g, dh, dres, *, name):
    R, C = x.shape
    tm = _tile(R, 256)

    def body(x_ref, g_ref, dh_ref, dres_ref, dx_ref, dg_ref):
        xv = x_ref[...]
        r = lax.rsqrt(jnp.mean(xv * xv, axis=-1, keepdims=True) + EPS)
        xn = xv * r
        dh_v = dh_ref[...].astype(F32)
        dxn = dh_v * g_ref[...]
        dx_ref[...] = r * (dxn - xn * jnp.mean(dxn * xn, axis=-1, keepdims=True)) + dres_ref[...]
        part = jnp.sum(dh_v * xn, axis=0, keepdims=True)

        @pl.when(pl.program_id(0) == 0)
        def _():
            dg_ref[...] = part

        @pl.when(pl.program_id(0) > 0)
        def _():
            dg_ref[...] += part

    row = pl.BlockSpec((tm, C), lambda i: (i, 0))
    vec = pl.BlockSpec((1, C), lambda i: (0, 0))
    dx, dg = pl.pallas_call(
        body,
        name=name,
        grid=(R // tm,),
        in_specs=[row, vec, row, row],
        out_specs=[row, vec],
        out_shape=[jax.ShapeDtypeStruct((R, C), F32), jax.ShapeDtypeStruct((1, C), F32)],
        compiler_params=_params(("arbitrary",)),
    )(x, g.reshape(1, C), dh, dres)
    return dx, dg.reshape(C)


def _final_loss(x, g, target, *, name):
    R, C = x.shape
    tm = _tile(R, 256)

    def body(x_ref, g_ref, t_ref, loss_ref, dx_ref, dg_ref):
        xv = x_ref[...]
        r = lax.rsqrt(jnp.mean(xv * xv, axis=-1, keepdims=True) + EPS)
        xn = xv * r
        gv = g_ref[...]
        err = xn * gv - t_ref[...]
        lpart = (0.5 / C) * jnp.sum(jnp.sum(err * err, axis=1, keepdims=True), axis=0, keepdims=True)
        dy = err * (1.0 / C)
        dxn = dy * gv
        dx_ref[...] = r * (dxn - xn * jnp.mean(dxn * xn, axis=-1, keepdims=True))
        gpart = jnp.sum(dy * xn, axis=0, keepdims=True)

        @pl.when(pl.program_id(0) == 0)
        def _():
            loss_ref[...] = lpart
            dg_ref[...] = gpart

        @pl.when(pl.program_id(0) > 0)
        def _():
            loss_ref[...] += lpart
            dg_ref[...] += gpart

    row = pl.BlockSpec((tm, C), lambda i: (i, 0))
    vec = pl.BlockSpec((1, C), lambda i: (0, 0))
    loss, dx, dg = pl.pallas_call(
        body,
        name=name,
        grid=(R // tm,),
        in_specs=[row, vec, row],
        out_specs=[pl.BlockSpec((1, 1), lambda i: (0, 0)), row, vec],
        out_shape=[jax.ShapeDtypeStruct((1, 1), F32), jax.ShapeDtypeStruct((R, C), F32), jax.ShapeDtypeStruct((1, C), F32)],
        compiler_params=_params(("arbitrary",)),
    )(x, g.reshape(1, C), target)
    return loss, dx, dg.reshape(C)


def _pool_select(lane, vals):
    out = vals[3]
    for gi in (2, 1, 0):
        out = jnp.where(lane < 64 * (gi + 1), vals[gi], out)
    return out


def _pool_diff(a):
    row, lane = _rows(a.shape), _lanes(a.shape)

    def down(v, k):
        return jnp.where(row >= k, pltpu.roll(v, k, 0), 0.0)

    s2 = a + down(a, 1)
    s4 = s2 + down(s2, 2)
    s8 = s4 + down(s4, 4)
    s16 = s8 + down(s8, 8)
    wsum = _pool_select(lane, (s2, s4, s8, s16))
    win = _pool_select(lane, (2, 4, 8, 16))
    cnt = jnp.minimum(row + 1, win).astype(F32)
    return wsum / cnt - a, cnt


def _pool_diff_t(dd, cnt):
    S = dd.shape[0]
    row, lane = _rows(dd.shape), _lanes(dd.shape)

    def up(v, k):
        return jnp.where(row < S - k, pltpu.roll(v, S - k, 0), 0.0)

    e = dd / cnt
    s2 = e + up(e, 1)
    s4 = s2 + up(s2, 2)
    s8 = s4 + up(s4, 4)
    s16 = s8 + up(s8, 8)
    return _pool_select(lane, (s2, s4, s8, s16)) - dd


def _pool_fwd(rest, wbd, scale, *, name):
    S = rest.shape[0]

    def body(a_ref, w_ref, s_ref, o_ref):
        d, _ = _pool_diff(a_ref[...])
        yp = _dot(d.astype(BF16), w_ref[...], 1, 0)
        o_ref[...] = (yp * s_ref[...]).astype(BF16)

    return pl.pallas_call(
        body,
        name=name,
        grid=(1,),
        in_specs=[
            pl.BlockSpec((S, POOL_W), lambda i: (0, OFF_A // POOL_W)),
            pl.BlockSpec((POOL_W, POOL_W), lambda i: (0, 0)),
            pl.BlockSpec((1, POOL_W), lambda i: (0, 0)),
        ],
        out_specs=pl.BlockSpec((S, POOL_W), lambda i: (0, 0)),
        out_shape=jax.ShapeDtypeStruct((S, POOL_W), BF16),
        compiler_params=_params(("arbitrary",)),
    )(rest, wbd, scale.reshape(1, POOL_W))


def _pool_bwd(rest, wbd, wbd_t, scale, dpa, *, name):
    S = rest.shape[0]

    def body(a_ref, w_ref, wt_ref, s_ref, dpa_ref, da_ref, dw_ref, ds_ref):
        d, cnt = _pool_diff(a_ref[...])
        db = d.astype(BF16)
        yp = _dot(db, w_ref[...], 1, 0)
        dpa_v = dpa_ref[...]
        ds_ref[...] = jnp.sum(dpa_v * yp, axis=0, keepdims=True)
        dyp = (dpa_v * s_ref[...]).astype(BF16)
        dw_ref[...] = _dot(db, dyp, 0, 0)
        dd = _dot(dyp, wt_ref[...], 1, 0)
        da_ref[...] = _pool_diff_t(dd, cnt).astype(BF16)

    full = pl.BlockSpec((S, POOL_W), lambda i: (0, 0))
    sq = pl.BlockSpec((POOL_W, POOL_W), lambda i: (0, 0))
    vec = pl.BlockSpec((1, POOL_W), lambda i: (0, 0))
    return pl.pallas_call(
        body,
        name=name,
        grid=(1,),
        in_specs=[pl.BlockSpec((S, POOL_W), lambda i: (0, OFF_A // POOL_W)), sq, sq, vec, full],
        out_specs=[full, sq, vec],
        out_shape=[
            jax.ShapeDtypeStruct((S, POOL_W), BF16),
            jax.ShapeDtypeStruct((POOL_W, POOL_W), F32),
            jax.ShapeDtypeStruct((1, POOL_W), F32),
        ],
        compiler_params=_params(("arbitrary",)),
    )(rest, wbd, wbd_t, scale.reshape(1, POOL_W), dpa)


def _log_sigmoid(z):
    return jnp.minimum(z, 0.0) - jnp.log(1.0 + jnp.exp(-jnp.abs(z)))


_F_SPEC_COL = OFF_F // F_LANES


def _fox_prep(rest, bpad, *, name):
    S = rest.shape[0]

    def body(f_ref, b_ref, o_ref, ot_ref):
        acc = _log_sigmoid(f_ref[...] + b_ref[...])
        row = _rows(acc.shape)
        k = 1
        while k < S:
            acc = acc + jnp.where(row >= k, pltpu.roll(acc, k, 0), 0.0)
            k *= 2
        o_ref[...] = acc
        ot_ref[...] = acc.T

    return pl.pallas_call(
        body,
        name=name,
        grid=(1,),
        in_specs=[pl.BlockSpec((S, F_LANES), lambda i: (0, _F_SPEC_COL)), pl.BlockSpec((1, F_LANES), lambda i: (0, 0))],
        out_specs=[pl.BlockSpec((S, F_LANES), lambda i: (0, 0)), pl.BlockSpec((F_LANES, S), lambda i: (0, 0))],
        out_shape=[jax.ShapeDtypeStruct((S, F_LANES), F32), jax.ShapeDtypeStruct((F_LANES, S), F32)],
        compiler_params=_params(("arbitrary",)),
    )(rest, bpad)


def _fox_post(rest, bpad, dcum, *, name):
    S = rest.shape[0]

    def body(f_ref, b_ref, d_ref, df_ref, db_ref):
        acc = d_ref[...]
        row = _rows(acc.shape)
        k = 1
        while k < S:
            acc = acc + jnp.where(row < S - k, pltpu.roll(acc, S - k, 0), 0.0)
            k *= 2
        df = acc * (1.0 - _sigmoid(f_ref[...] + b_ref[...]))
        df_ref[...] = df.astype(BF16)
        db_ref[...] = jnp.sum(df, axis=0, keepdims=True)

    full = pl.BlockSpec((S, F_LANES), lambda i: (0, 0))
    vec = pl.BlockSpec((1, F_LANES), lambda i: (0, 0))
    return pl.pallas_call(
        body,
        name=name,
        grid=(1,),
        in_specs=[pl.BlockSpec((S, F_LANES), lambda i: (0, _F_SPEC_COL)), vec, full],
        out_specs=[full, vec],
        out_shape=[jax.ShapeDtypeStruct((S, F_LANES), BF16), jax.ShapeDtypeStruct((1, F_LANES), F32)],
        compiler_params=_params(("arbitrary",)),
    )(rest, bpad, dcum)


_FOX_SCALE = FOX_DH ** -0.5
_PAIRS = FOX_H // 2


def _scaled(v):
    return (v.astype(F32) * _FOX_SCALE).astype(BF16)


def _head_lane(cum, h):
    return jnp.sum(jnp.where(_lanes(cum.shape) == h, cum, 0.0), axis=-1, keepdims=True)


def _diag_mask(s):
    return jnp.where(_rows(s.shape) >= _lanes(s.shape), s, NEG)


def _fox_fwd(qkv, cum, fk3, *, name):
    S = qkv.shape[0]
    nk, t = fk3.shape[1:]

    def body(q_ref, k_ref, v_ref, cum_ref, fk_ref, o_ref, lse_ref, m_sc, l_sc, acc_sc):
        hp, i = pl.program_id(0), pl.program_id(1)
        lane = _lanes((t, 128))
        lo = lane < FOX_DH
        qs = _scaled(q_ref[...])
        zero = jnp.zeros_like(qs)
        qm = (jnp.where(lo, qs, zero), jnp.where(lo, zero, qs))
        cumv = cum_ref[...]
        fq = [_head_lane(cumv, 2 * hp + e) for e in range(2)]
        m_sc[...] = jnp.full(m_sc.shape, NEG, F32)
        l_sc[...] = jnp.zeros(l_sc.shape, F32)
        acc_sc[...] = jnp.zeros(acc_sc.shape, F32)

        def tile(j, masked):
            k0 = pl.multiple_of(j * t, t)
            kb = k_ref[pl.ds(k0, t), :]
            vb = v_ref[pl.ds(k0, t), :]
            alphas, pvs = [], []
            for e in range(2):
                s = _dot(qm[e], kb, 1, 1) + fq[e] - fk_ref[2 * hp + e, pl.ds(j, 1), :]
                if masked:
                    s = _diag_mask(s)
                m_old = m_sc[e]
                m_new = jnp.maximum(m_old, jnp.max(s, axis=-1, keepdims=True))
                p = jnp.exp(s - m_new)
                alpha = jnp.exp(m_old - m_new)
                l_sc[e] = alpha * l_sc[e] + jnp.sum(p, axis=-1, keepdims=True)
                m_sc[e] = m_new
                alphas.append(alpha)
                pvs.append(_dot(p.astype(BF16), vb, 1, 0))
            acc_sc[...] = jnp.where(lo, alphas[0], alphas[1]) * acc_sc[...] + jnp.where(lo, pvs[0], pvs[1])

        def step(j, carry):
            tile(j, False)
            return carry

        lax.fori_loop(0, i, step, 0)
        tile(i, True)
        o_ref[...] = acc_sc[...] / jnp.where(lo, l_sc[0], l_sc[1])
        lse = [m_sc[e] + jnp.log(l_sc[e]) for e in range(2)]
        lse_ref[...] = jnp.where(lane == 0, lse[0], jnp.where(lane == 1, lse[1], 0.0))

    return pl.pallas_call(
        body,
        name=name,
        grid=(_PAIRS, S // t),
        in_specs=[
            pl.BlockSpec((t, 128), lambda hp, i: (i, hp)),
            pl.BlockSpec((S, 128), lambda hp, i: (0, _PAIRS + hp)),
            pl.BlockSpec((S, 128), lambda hp, i: (0, 2 * _PAIRS + hp)),
            pl.BlockSpec((t, F_LANES), lambda hp, i: (i, 0)),
            pl.BlockSpec((FOX_H, nk, t), lambda hp, i: (0, 0, 0)),
        ],
        out_specs=[pl.BlockSpec((t, 128), lambda hp, i: (i, hp)), pl.BlockSpec((None, t, 128), lambda hp, i: (hp, i, 0))],
        out_shape=[jax.ShapeDtypeStruct((S, FOX_W), F32), jax.ShapeDtypeStruct((_PAIRS, S, 128), F32)],
        scratch_shapes=[pltpu.VMEM((2, t, 1), F32), pltpu.VMEM((2, t, 1), F32), pltpu.VMEM((t, 128), F32)],
        compiler_params=_params(("parallel", "arbitrary")),
    )(qkv, qkv, qkv, cum, fk3)


def _fox_bwd(qkv, cum, fk3, o, do, lse, *, name):
    S = qkv.shape[0]
    nk, t = fk3.shape[1:]

    def body(q_ref, k_ref, v_ref, cum_ref, fk_ref, o_ref, do_ref, lse_ref, dq_ref, dk_ref, dv_ref, dfq_ref, dfk_ref,
             qm_sc, km_sc, dom_sc, delta_sc, fq_sc, dfq_sc, dq_sc):
        hp = pl.program_id(0)
        lane = _lanes((t, 128))
        lo = lane < FOX_DH

        def prep(i, carry):
            r = pl.ds(pl.multiple_of(i * t, t), t)
            qs, ks, dob = _scaled(q_ref[r, :]), _scaled(k_ref[r, :]), do_ref[r, :]
            prod = dob.astype(F32) * o_ref[r, :]
            cumv = cum_ref[r, :]
            zero = jnp.zeros_like(qs)
            for e in range(2):
                mine = lo if e == 0 else jnp.logical_not(lo)
                qm_sc[e, r, :] = jnp.where(mine, qs, zero)
                km_sc[e, r, :] = jnp.where(mine, ks, zero)
                dom_sc[e, r, :] = jnp.where(mine, dob, zero)
                delta_sc[e, r, :] = jnp.sum(jnp.where(mine, prod, 0.0), axis=-1, keepdims=True)
                fq_sc[e, r, :] = _head_lane(cumv, 2 * hp + e)
                dfq_sc[e, r, :] = jnp.zeros((t, 1), F32)
            dq_sc[r, :] = jnp.zeros((t, 128), F32)
            return carry

        lax.fori_loop(0, nk, prep, 0)

        def kv_tile(j, carry):
            kr = pl.ds(pl.multiple_of(j * t, t), t)
            kb, vb = k_ref[kr, :], v_ref[kr, :]
            fks = [fk_ref[2 * hp + e, pl.ds(j, 1), :] for e in range(2)]

            def q_tile(i, acc, masked):
                dk, dv, dfk0, dfk1 = acc
                dfk = [dfk0, dfk1]
                qr = pl.ds(pl.multiple_of(i * t, t), t)
                dq_t = jnp.zeros((t, 128), F32)
                for e in range(2):
                    qe, doe = qm_sc[e, qr, :], dom_sc[e, qr, :]
                    s = _dot(qe, kb, 1, 1) + fq_sc[e, qr, :] - fks[e]
                    if masked:
                        s = _diag_mask(s)
                    p = jnp.exp(s - lse_ref[qr, e:e + 1])
                    dv = dv + _dot(p.astype(BF16), doe, 0, 0)
                    dp = _dot(doe, vb, 1, 1)
                    ds = p * (dp - delta_sc[e, qr, :])
                    dsb = ds.astype(BF16)
                    dk = dk + _dot(dsb, qe, 0, 0)
                    dq_t = dq_t + _dot(dsb, km_sc[e, kr, :], 1, 0)
                    dfq_sc[e, qr, :] += jnp.sum(ds, axis=-1, keepdims=True)
                    dfk[e] = dfk[e] - jnp.sum(ds, axis=0, keepdims=True)
                dq_sc[qr, :] += dq_t
                return dk, dv, dfk[0], dfk[1]

            init = (jnp.zeros((t, 128), F32), jnp.zeros((t, 128), F32), jnp.zeros((1, t), F32), jnp.zeros((1, t), F32))
            acc = q_tile(j, init, True)
            dk, dv, dfk0, dfk1 = lax.fori_loop(j + 1, nk, functools.partial(q_tile, masked=False), acc)
            dk_ref[kr, :] = dk.astype(BF16)
            dv_ref[kr, :] = dv.astype(BF16)
            dfk_ref[0, pl.ds(j, 1), :] = dfk0
            dfk_ref[1, pl.ds(j, 1), :] = dfk1
            return carry

        lax.fori_loop(0, nk, kv_tile, 0)
        dq_ref[...] = dq_sc[...].astype(BF16)
        lane_s = _lanes((S, 128))
        dfq_ref[...] = jnp.where(lane_s == 0, dfq_sc[0], jnp.where(lane_s == 1, dfq_sc[1], 0.0))

    col = lambda c0: pl.BlockSpec((S, 128), lambda hp: (0, c0 + hp))
    pair = pl.BlockSpec((S, 128), lambda hp: (0, hp))
    lanes3 = pl.BlockSpec((None, S, 128), lambda hp: (hp, 0, 0))
    big = jax.ShapeDtypeStruct((S, FOX_W), BF16)
    masked_bf16 = pltpu.VMEM((2, S, 128), BF16)
    column = pltpu.VMEM((2, S, 1), F32)
    return pl.pallas_call(
        body,
        name=name,
        grid=(_PAIRS,),
        in_specs=[
            col(0), col(_PAIRS), col(2 * _PAIRS),
            pl.BlockSpec((S, F_LANES), lambda hp: (0, 0)),
            pl.BlockSpec((FOX_H, nk, t), lambda hp: (0, 0, 0)),
            pair, pair, lanes3,
        ],
        out_specs=[pair, pair, pair, lanes3, pl.BlockSpec((None, 2, nk, t), lambda hp: (hp, 0, 0, 0))],
        out_shape=[big, big, big, jax.ShapeDtypeStruct((_PAIRS, S, 128), F32), jax.ShapeDtypeStruct((_PAIRS, 2, nk, t), F32)],
        scratch_shapes=[masked_bf16, masked_bf16, masked_bf16, column, column, column, pltpu.VMEM((S, 128), F32)],
        compiler_params=_params(("parallel",)),
    )(qkv, qkv, qkv, cum, fk3, o, do, lse)


def _group_mask(lane, gi):
    return (lane >= 64 * gi) & (lane < 64 * (gi + 1))


_U_COL = OFF_C // SGU_W


def _sgu_fwd(rest, gn, wm, bias, *, name):
    S = rest.shape[0]
    ts = _tile(S, 512)
    nc = ts // SGU_CHUNK

    def body(u_ref, v_ref, g_ref, w_ref, b_ref, o_ref):
        zv = _gelu(v_ref[...])
        vn = zv * lax.rsqrt(jnp.mean(zv * zv, axis=-1, keepdims=True) + EPS) * g_ref[...]
        lane = _lanes((SGU_CHUNK, SGU_W))
        for c in range(nc):
            rows = slice(c * SGU_CHUNK, (c + 1) * SGU_CHUNK)
            vcb = vn[rows].astype(BF16)
            mixed = b_ref[...]
            for gi in range(4):
                mixed = mixed + jnp.where(_group_mask(lane, gi), _dot(w_ref[gi], vcb, 1, 0), 0.0)
            o_ref[rows, :] = (_gelu(u_ref[rows, :]) * mixed).astype(BF16)

    return pl.pallas_call(
        body,
        name=name,
        grid=(S // ts,),
        in_specs=[
            pl.BlockSpec((ts, SGU_W), lambda i: (i, _U_COL)),
            pl.BlockSpec((ts, SGU_W), lambda i: (i, _U_COL + 1)),
            pl.BlockSpec((1, SGU_W), lambda i: (0, 0)),
            pl.BlockSpec((4, SGU_CHUNK, SGU_CHUNK), lambda i: (0, 0, 0)),
            pl.BlockSpec((SGU_CHUNK, SGU_W), lambda i: (0, 0)),
        ],
        out_specs=pl.BlockSpec((ts, SGU_W), lambda i: (i, 0)),
        out_shape=jax.ShapeDtypeStruct((S, SGU_W), BF16),
        compiler_params=_params(("parallel",)),
    )(rest, rest, gn.reshape(1, SGU_W), wm, bias)


def _sgu_bwd(rest, gn, wm, wm_t, bias, dsg, *, name):
    S = rest.shape[0]
    ts = _tile(S, 512)
    nc = ts // SGU_CHUNK

    def body(u_ref, v_ref, g_ref, w_ref, wt_ref, b_ref, dsg_ref, dc_ref, dw_ref, db_ref, dg_ref):
        first = pl.program_id(0) == 0

        @pl.when(first)
        def _():
            dw_ref[...] = jnp.zeros_like(dw_ref)
            db_ref[...] = jnp.zeros_like(db_ref)
            dg_ref[...] = jnp.zeros_like(dg_ref)

        gv = g_ref[...]
        lane = _lanes((SGU_CHUNK, SGU_W))
        for c in range(nc):
            rows = slice(c * SGU_CHUNK, (c + 1) * SGU_CHUNK)
            vpre = v_ref[rows, :]
            upre = u_ref[rows, :]
            zv = _gelu(vpre)
            r = lax.rsqrt(jnp.mean(zv * zv, axis=-1, keepdims=True) + EPS)
            zn = zv * r
            vcb = (zn * gv).astype(BF16)
            mixed = b_ref[...]
            for gi in range(4):
                mixed = mixed + jnp.where(_group_mask(lane, gi), _dot(w_ref[gi], vcb, 1, 0), 0.0)
            zu = _gelu(upre)
            dsg_v = dsg_ref[rows, :]
            dc_ref[rows, :SGU_W] = (dsg_v * mixed * _gelu_grad(upre)).astype(BF16)
            dmixed = dsg_v * zu
            db_ref[...] += dmixed
            dvn = jnp.zeros((SGU_CHUNK, SGU_W), F32)
            for gi in range(4):
                dmg = jnp.where(_group_mask(lane, gi), dmixed, 0.0).astype(BF16)
                dw_ref[gi] += _dot(dmg, vcb, 1, 1)
                dvn = dvn + _dot(wt_ref[gi], dmg, 1, 0)
            dg_ref[...] += jnp.sum(dvn * zn, axis=0, keepdims=True)
            dzn = dvn * gv
            dzv = r * (dzn - zn * jnp.mean(dzn * zn, axis=-1, keepdims=True))
            dc_ref[rows, SGU_W:] = (dzv * _gelu_grad(vpre)).astype(BF16)

    blk = pl.BlockSpec((ts, SGU_W), lambda i: (i, 0))
    vec = pl.BlockSpec((1, SGU_W), lambda i: (0, 0))
    w3 = pl.BlockSpec((4, SGU_CHUNK, SGU_CHUNK), lambda i: (0, 0, 0))
    bsp = pl.BlockSpec((SGU_CHUNK, SGU_W), lambda i: (0, 0))
    return pl.pallas_call(
        body,
        name=name,
        grid=(S // ts,),
        in_specs=[
            pl.BlockSpec((ts, SGU_W), lambda i: (i, _U_COL)),
            pl.BlockSpec((ts, SGU_W), lambda i: (i, _U_COL + 1)),
            vec, w3, w3, bsp, blk,
        ],
        out_specs=[pl.BlockSpec((ts, 2 * SGU_W), lambda i: (i, 0)), w3, bsp, vec],
        out_shape=[
            jax.ShapeDtypeStruct((S, 2 * SGU_W), BF16),
            jax.ShapeDtypeStruct((4, SGU_CHUNK, SGU_CHUNK), F32),
            jax.ShapeDtypeStruct((SGU_CHUNK, SGU_W), F32),
            jax.ShapeDtypeStruct((1, SGU_W), F32),
        ],
        compiler_params=_params(("arbitrary",)),
    )(rest, rest, gn.reshape(1, SGU_W), wm, wm_t, bias, dsg)


_GT = 512
_G0 = OFF_G // _GT


def _gate_specs(tm, col_of):
    specs = [pl.BlockSpec((tm, _GT), functools.partial(lambda k, *ids: (col_of(*ids)[0], _G0 + 2 * k + col_of(*ids)[1]), k)) for k in range(3)]
    specs += [pl.BlockSpec((1, _GT), functools.partial(lambda k, *ids: (0, 2 * k + col_of(*ids)[1]), k)) for k in range(3)]
    return specs


def _merge_fwd(rest, bg, ya, yb, yc, *, name):
    S = rest.shape[0]
    tm = _tile(S, 512)

    def body(g1, g2, g3, b1, b2, b3, ya_ref, yb_ref, yc_ref, o_ref):
        acc = _sigmoid(g1[...] + b1[...]) * ya_ref[...]
        acc = acc + _sigmoid(g2[...] + b2[...]) * yb_ref[...]
        acc = acc + _sigmoid(g3[...] + b3[...]) * yc_ref[...]
        o_ref[...] = acc.astype(BF16)

    blk = pl.BlockSpec((tm, _GT), lambda i, j: (i, j))
    return pl.pallas_call(
        body,
        name=name,
        grid=(S // tm, D // _GT),
        in_specs=_gate_specs(tm, lambda i, j: (i, j)) + [blk, blk, blk],
        out_specs=blk,
        out_shape=jax.ShapeDtypeStruct((S, D), BF16),
        compiler_params=_params(("parallel", "parallel")),
    )(rest, rest, rest, bg, bg, bg, ya, yb, yc)


def _merge_bwd(rest, bg, ya, yb, yc, dm, *, name):
    S = rest.shape[0]
    tm = _tile(S, 512)

    def body(g1, g2, g3, b1, b2, b3, ya_ref, yb_ref, yc_ref, dm_ref, dya, dyb, dyc, dg1, dg2, dg3, db1, db2, db3):
        first = pl.program_id(1) == 0
        dmv = dm_ref[...]
        for g_ref, b_ref, y_ref, dy_ref, dg_ref, db_ref in (
            (g1, b1, ya_ref, dya, dg1, db1), (g2, b2, yb_ref, dyb, dg2, db2), (g3, b3, yc_ref, dyc, dg3, db3)):
            gate = _sigmoid(g_ref[...] + b_ref[...])
            dy_ref[...] = (dmv * gate).astype(BF16)
            dpre = dmv * y_ref[...] * gate * (1.0 - gate)
            dg_ref[...] = dpre.astype(BF16)
            part = jnp.sum(dpre, axis=0, keepdims=True)

            @pl.when(first)
            def _():
                db_ref[...] = part

            @pl.when(jnp.logical_not(first))
            def _():
                db_ref[...] += part

    blk = pl.BlockSpec((tm, _GT), lambda j, i: (i, j))
    vec = pl.BlockSpec((1, _GT), lambda j, i: (0, j))
    big = jax.ShapeDtypeStruct((S, D), BF16)
    small = jax.ShapeDtypeStruct((1, D), F32)
    return pl.pallas_call(
        body,
        name=name,
        grid=(D // _GT, S // tm),
        in_specs=_gate_specs(tm, lambda j, i: (i, j)) + [blk, blk, blk, blk],
        out_specs=[blk] * 6 + [vec] * 3,
        out_shape=[big] * 6 + [small] * 3,
        compiler_params=_params(("parallel", "arbitrary")),
    )(rest, rest, rest, bg, bg, bg, ya, yb, yc, dm)


_X_SCALE = XDH ** -0.5


def _xattn_fwd(xq, kv, *, name):
    S = xq.shape[0]
    M = kv.shape[0]
    tq = _tile(S, 512)

    def body(q_ref, k_ref, v_ref, o_ref):
        s = _dot(q_ref[...], k_ref[...], 1, 1) * _X_SCALE
        e = jnp.exp(s - jnp.max(s, axis=-1, keepdims=True))
        p = e / jnp.sum(e, axis=-1, keepdims=True)
        o_ref[...] = _dot(p.astype(BF16), v_ref[...], 1, 0).astype(BF16)

    return pl.pallas_call(
        body,
        name=name,
        grid=(S // tq, XH),
        in_specs=[
            pl.BlockSpec((tq, XDH), lambda i, h: (i, h)),
            pl.BlockSpec((M, XDH), lambda i, h: (0, h)),
            pl.BlockSpec((M, XDH), lambda i, h: (0, XH + h)),
        ],
        out_specs=pl.BlockSpec((tq, XDH), lambda i, h: (i, h)),
        out_shape=jax.ShapeDtypeStruct((S, D), BF16),
        compiler_params=_params(("parallel", "parallel")),
    )(xq, kv, kv)


def _xattn_bwd(xq, kv, do, *, name):
    S = xq.shape[0]
    M = kv.shape[0]
    tq = _tile(S, 512)

    def body(q_ref, k_ref, v_ref, do_ref, dq_ref, dk_ref, dv_ref):
        qb = q_ref[...]
        kb = k_ref[...]
        dob = do_ref[...]
        s = _dot(qb, kb, 1, 1) * _X_SCALE
        e = jnp.exp(s - jnp.max(s, axis=-1, keepdims=True))
        p = e / jnp.sum(e, axis=-1, keepdims=True)
        dp = _dot(dob, v_ref[...], 1, 1)
        ds = (p * (dp - jnp.sum(p * dp, axis=-1, keepdims=True)) * _X_SCALE).astype(BF16)
        dq_ref[...] = _dot(ds, kb, 1, 0).astype(BF16)
        dk_part = _dot(ds, qb, 0, 0)
        dv_part = _dot(p.astype(BF16), dob, 0, 0)

        @pl.when(pl.program_id(1) == 0)
        def _():
            dk_ref[...] = dk_part
            dv_ref[...] = dv_part

        @pl.when(pl.program_id(1) > 0)
        def _():
            dk_ref[...] += dk_part
            dv_ref[...] += dv_part

    qspec = pl.BlockSpec((tq, XDH), lambda h, i: (i, h))
    kspec = pl.BlockSpec((M, XDH), lambda h, i: (0, h))
    dxq, dxk, dxv = pl.pallas_call(
        body,
        name=name,
        grid=(XH, S // tq),
        in_specs=[qspec, kspec, pl.BlockSpec((M, XDH), lambda h, i: (0, XH + h)), qspec],
        out_specs=[qspec, kspec, kspec],
        out_shape=[jax.ShapeDtypeStruct((S, D), BF16), jax.ShapeDtypeStruct((M, D), F32), jax.ShapeDtypeStruct((M, D), F32)],
        compiler_params=_params(("parallel", "arbitrary")),
    )(xq, kv, kv, do)
    return dxq, jnp.concatenate([dxk, dxv], axis=1)


def _adam_math(w, g, m, v):
    m = ADAM_B1 * m + (1.0 - ADAM_B1) * g
    v = ADAM_B2 * v + (1.0 - ADAM_B2) * (g * g)
    m_hat = m / (1.0 - ADAM_B1 ** ADAM_STEP)
    v_hat = v / (1.0 - ADAM_B2 ** ADAM_STEP)
    delta = -ADAM_LR * (m_hat / (jnp.sqrt(v_hat) + ADAM_EPS) + ADAM_WD * w)
    return delta, m, v


def _adamw_sharded(parts, w, m, v, *, name):
    _, R, C = w.shape
    tm = _tile(R, 256)
    nr = R // tm

    def body(p0_ref, p1_ref, w_ref, m_ref, v_ref, g_ref, d_ref, mo_ref, vo_ref):
        def update(p_ref):
            g = p_ref[0].astype(F32)
            for dev in range(1, N_DEV):
                g = g + p_ref[dev].astype(F32)
            delta, mn, vn = _adam_math(w_ref[...], g, m_ref[...], v_ref[...])
            g_ref[...] = g
            d_ref[...] = delta
            mo_ref[...] = mn
            vo_ref[...] = vn

        @pl.when(pl.program_id(0) == 0)
        def _():
            update(p0_ref)

        @pl.when(pl.program_id(0) == 1)
        def _():
            update(p1_ref)

    p0 = pl.BlockSpec((N_DEV, tm, C), lambda l, i: (0, i * (1 - l) + (nr - 1) * l, 0))
    p1 = pl.BlockSpec((N_DEV, tm, C), lambda l, i: (0, i * l, 0))
    blk = pl.BlockSpec((None, tm, C), lambda l, i: (l, i, 0))
    sds = jax.ShapeDtypeStruct(w.shape, F32)
    return pl.pallas_call(
        body,
        name=name,
        grid=(DEPTH, nr),
        in_specs=[p0, p1, blk, blk, blk],
        out_specs=[blk] * 4,
        out_shape=[sds] * 4,
        compiler_params=_params(("arbitrary", "arbitrary")),
    )(parts[0], parts[1], w, m, v)


def _adamw_small(g, w, m, v, *, name):
    n = len(g)

    def body(*refs):
        g_refs, w_refs, m_refs, v_refs = (refs[k * n:(k + 1) * n] for k in range(4))
        d_out, m_out, v_out = (refs[(4 + k) * n:(5 + k) * n] for k in range(3))
        for t in range(n):
            delta, mn, vn = _adam_math(w_refs[t][...], g_refs[t][...], m_refs[t][...], v_refs[t][...])
            d_out[t][...] = delta
            m_out[t][...] = mn
            v_out[t][...] = vn

    vm = pl.BlockSpec(memory_space=pltpu.VMEM)
    shapes = [jax.ShapeDtypeStruct(a.shape, F32) for a in w]
    outs = pl.pallas_call(
        body,
        name=name,
        in_specs=[vm] * (4 * n),
        out_specs=[vm] * (3 * n),
        out_shape=shapes * 3,
        compiler_params=pltpu.CompilerParams(vmem_limit_bytes=VMEM_LIMIT),
    )(*g, *w, *m, *v)
    return outs[:n], outs[n:2 * n], outs[2 * n:]


def _position():
    return lax.axis_index("x"), lax.axis_index("y"), lax.axis_index("c")


def _dev_index(px, py, pc):
    return 4 * px + 2 * py + pc


_ANY = pl.BlockSpec(memory_space=pl.ANY)


def _all_gather(shards, *, name):
    n = len(shards)

    def body(*refs):
        ins, outs = refs[:n], refs[n:2 * n]
        send_sems, recv_sems, local_sems = refs[2 * n:]
        x, y, c = _position()
        me, sibling = (x, y, c), (x, y, 1 - c)
        chips = [(1 - x, y), (x, 1 - y), (1 - x, 1 - y)]

        def copy(t, k, block, to, src=None):
            dst = outs[t].at[_dev_index(*block)]
            return pltpu.make_async_remote_copy(
                src_ref=dst if src is None else src, dst_ref=dst, send_sem=send_sems.at[t, k], recv_sem=recv_sems.at[t, k],
                device_id=to, device_id_type=MESH)

        mine = [pltpu.make_async_copy(ins[t], outs[t].at[_dev_index(*me)], local_sems.at[t]) for t in range(n)]
        for cp in mine:
            cp.start()
        started = []
        for j, chip in enumerate(chips):
            for t in range(n):
                started.append(copy(t, 1 + j, me, (*chip, c), src=ins[t]))
                started[-1].start()
        for t in range(n):
            started.append(copy(t, 0, me, sibling, src=ins[t]))
            started[-1].start()
        for j, chip in enumerate(chips):
            for t in range(n):
                copy(t, 1 + j, (*chip, c), me).wait_recv()
                started.append(copy(t, 4 + j, (*chip, c), sibling))
                started[-1].start()
        for t in range(n):
            copy(t, 0, sibling, me).wait_recv()
        for j, chip in enumerate(chips):
            for t in range(n):
                copy(t, 4 + j, (*chip, 1 - c), me).wait_recv()
        for cp in started:
            cp.wait_send()
        for cp in mine:
            cp.wait()

    return pl.pallas_call(
        body,
        name=name,
        in_specs=[_ANY] * n,
        out_specs=[_ANY] * n,
        out_shape=[jax.ShapeDtypeStruct((N_DEV, *s.shape), s.dtype) for s in shards],
        scratch_shapes=[pltpu.SemaphoreType.DMA((n, 7)), pltpu.SemaphoreType.DMA((n, 7)), pltpu.SemaphoreType.DMA((n,))],
        compiler_params=pltpu.CompilerParams(has_side_effects=True),
    )(*shards)


def _peers(x, y, c):
    out = []
    for mask in range(1, N_DEV):
        fx, fy, fc = (mask >> 2) & 1, (mask >> 1) & 1, mask & 1
        out.append((1 - x if fx else x, 1 - y if fy else y, 1 - c if fc else c))
    return out


def _exchange(parts, *, name):
    n = len(parts)

    def body(*refs):
        ins, outs = refs[:n], refs[n:2 * n]
        send_sems, recv_sems, local_sems = refs[2 * n:]
        x, y, c = _position()
        me = _dev_index(x, y, c)
        peers = _peers(x, y, c)

        def copy(t, k):
            peer = peers[k]
            return pltpu.make_async_remote_copy(
                src_ref=ins[t].at[_dev_index(*peer)], dst_ref=outs[t].at[me], send_sem=send_sems.at[t, k],
                recv_sem=recv_sems.at[t, k], device_id=peer, device_id_type=MESH)

        def arrival(t, k):
            peer = peers[k]
            dst = outs[t].at[_dev_index(*peer)]
            return pltpu.make_async_remote_copy(
                src_ref=dst, dst_ref=dst, send_sem=send_sems.at[t, k], recv_sem=recv_sems.at[t, k],
                device_id=peer, device_id_type=MESH)

        mine = [pltpu.make_async_copy(ins[t].at[me], outs[t].at[me], local_sems.at[t]) for t in range(n)]
        for cp in mine:
            cp.start()
        started = [copy(t, k) for k in range(N_DEV - 1) for t in range(n)]
        for cp in started:
            cp.start()
        for k in range(N_DEV - 1):
            for t in range(n):
                arrival(t, k).wait_recv()
        for cp in started:
            cp.wait_send()
        for cp in mine:
            cp.wait()

    return pl.pallas_call(
        body,
        name=name,
        in_specs=[_ANY] * n,
        out_specs=[_ANY] * n,
        out_shape=[jax.ShapeDtypeStruct(p.shape, p.dtype) for p in parts],
        scratch_shapes=[pltpu.SemaphoreType.DMA((n, 7)), pltpu.SemaphoreType.DMA((n, 7)), pltpu.SemaphoreType.DMA((n,))],
        compiler_params=pltpu.CompilerParams(has_side_effects=True),
    )(*parts)


_HBM = pl.BlockSpec(memory_space=pltpu.HBM)
_SEM = pl.BlockSpec(memory_space=pltpu.SEMAPHORE)


def _own_block_placed(block, like):
    x, y, c = _position()
    return lax.dynamic_update_index_in_dim(lax.empty(like.shape, like.dtype), block, _dev_index(x, y, c), 0)


def _direct_copies(srcs, lands, send_sems, recv_sems, blocked, arrivals):
    x, y, c = _position()
    me = _dev_index(x, y, c)
    copies = []
    for k, peer in enumerate(_peers(x, y, c)):
        p = _dev_index(*peer)
        for t in range(len(srcs)):
            sems = dict(send_sem=send_sems.at[7 * t + k], recv_sem=recv_sems.at[7 * t + k], device_id=peer, device_id_type=MESH)
            if arrivals:
                copies.append(pltpu.make_async_remote_copy(src_ref=lands[t].at[p], dst_ref=lands[t].at[p], **sems))
            else:
                copies.append(pltpu.make_async_remote_copy(
                    src_ref=srcs[t].at[p] if blocked else srcs[t], dst_ref=lands[t].at[me], **sems))
    return copies


def _exchange_start(srcs, own_blocks, *, blocked, name):
    n = len(srcs)
    shape_of = lambda s: s.shape if blocked else (N_DEV, *s.shape)
    lands = [_own_block_placed(own, jax.ShapeDtypeStruct(shape_of(s), s.dtype)) for s, own in zip(srcs, own_blocks)]

    def body(*refs):
        src_refs, land_refs = refs[:n], refs[n:2 * n]
        send_sems, recv_sems = refs[2 * n:2 * n + 2]
        token = refs[-1]
        for cp in _direct_copies(src_refs, land_refs, send_sems, recv_sems, blocked, arrivals=False):
            cp.start()
        token[...] = jnp.zeros_like(token)

    hbm = lambda a: pltpu.HBM(a.shape, a.dtype)
    outs = pl.pallas_call(
        body,
        name=name,
        in_specs=[_HBM] * (2 * n),
        out_specs=[_SEM, _SEM] + [_HBM] * (2 * n) + [pl.BlockSpec(memory_space=pltpu.VMEM)],
        out_shape=[pltpu.SemaphoreType.DMA((7 * n,)), pltpu.SemaphoreType.DMA((7 * n,))] + [hbm(a) for a in srcs] + [hbm(a) for a in lands]
        + [jax.ShapeDtypeStruct(_TOKEN, F32)],
        input_output_aliases={i: 2 + i for i in range(2 * n)},
        compiler_params=pltpu.CompilerParams(has_side_effects=pltpu.SideEffectType.DATAFLOW_SIDE_EFFECTING),
    )(*[pltpu.with_memory_space_constraint(a, pltpu.HBM) for a in (*srcs, *lands)])
    return (outs[0], outs[1], outs[2:2 + n], outs[2 + n:2 + 2 * n], blocked), outs[-1]


def _exchange_wait(state, after, *, name):
    send_sems, recv_sems, srcs, lands, blocked = state
    n = len(srcs)

    def body(*refs):
        src_refs, land_refs = refs[:n], refs[n:2 * n]
        send_refs, recv_refs = refs[2 * n:2 * n + 2]
        for cp in _direct_copies(src_refs, land_refs, send_refs, recv_refs, blocked, arrivals=False):
            cp.wait_send()
        for cp in _direct_copies(src_refs, land_refs, send_refs, recv_refs, blocked, arrivals=True):
            cp.wait_recv()

    hbm = lambda a: pltpu.HBM(a.shape, a.dtype)
    outs = pl.pallas_call(
        body,
        name=name,
        in_specs=[_HBM] * (2 * n) + [_SEM, _SEM, _ANY],
        out_specs=[_HBM] * (2 * n),
        out_shape=[hbm(a) for a in srcs] + [hbm(a) for a in lands],
        input_output_aliases={i: i for i in range(2 * n)},
        compiler_params=pltpu.CompilerParams(has_side_effects=pltpu.SideEffectType.DATAFLOW_SIDE_EFFECTING),
    )(*srcs, *lands, send_sems, recv_sems, after)
    return outs[n:]


def _all_reduce(g_local, *, name):
    R, C = g_local.shape

    def body(g_ref, o_ref, buf, send_sems, recv_sems):
        x, y, c = _position()
        me = _dev_index(x, y, c)
        peers = _peers(x, y, c)
        copies = []
        for k, peer in enumerate(peers):
            copies.append(pltpu.make_async_remote_copy(
                src_ref=g_ref, dst_ref=buf.at[me], send_sem=send_sems.at[k], recv_sem=recv_sems.at[k],
                device_id=peer, device_id_type=MESH))
            copies[-1].start()
        buf[me] = g_ref[...]
        for k, peer in enumerate(peers):
            dst = buf.at[_dev_index(*peer)]
            pltpu.make_async_remote_copy(
                src_ref=dst, dst_ref=dst, send_sem=send_sems.at[k], recv_sem=recv_sems.at[k],
                device_id=peer, device_id_type=MESH).wait_recv()
        for cp in copies:
            cp.wait_send()
        g = buf[0]
        for dev in range(1, N_DEV):
            g = g + buf[dev]
        o_ref[...] = g

    vm = pl.BlockSpec(memory_space=pltpu.VMEM)
    return pl.pallas_call(
        body,
        name=name,
        in_specs=[vm],
        out_specs=vm,
        out_shape=jax.ShapeDtypeStruct((R, C), F32),
        scratch_shapes=[pltpu.VMEM((N_DEV, R, C), F32), pltpu.SemaphoreType.DMA((7,)), pltpu.SemaphoreType.DMA((7,))],
        compiler_params=pltpu.CompilerParams(has_side_effects=True, vmem_limit_bytes=VMEM_LIMIT),
    )(g_local)


def _block_diag(w):
    out = jnp.zeros((POOL_W, POOL_W), w.dtype)
    for gi in range(4):
        out = out.at[64 * gi:64 * (gi + 1), 64 * gi:64 * (gi + 1)].set(w[gi])
    return out


def _layer_consts(sp, l):
    causal = jnp.tril(jnp.ones((SGU_CHUNK, SGU_CHUNK), F32))
    wm = (sp["sgu_w"][l] * causal[None]).astype(BF16)
    wbd = _block_diag(sp["pool_w"][l]).astype(BF16)
    return dict(
        wbd=wbd, wbd_t=wbd.T, wm=wm, wm_t=wm.transpose(0, 2, 1),
        sgu_bias=jnp.repeat(sp["sgu_b"][l].T, 64, axis=1),
        bpad=jnp.pad(sp["b_forget"][l], (0, F_LANES - FOX_H)).reshape(1, F_LANES),
        bg=sp["b_gate"][l].reshape(1, 3 * D),
    )


def _relu2(acc):
    return acc, jnp.square(jnp.maximum(acc, 0.0))


def _relu2_grad(acc, z):
    return (acc * 2.0 * jnp.maximum(z, 0.0),)


def _layer_fwd(l, x, mem, W, sp, after):
    S = x.shape[0]
    t = _tile(S, 256)
    c = _layer_consts(sp, l)
    n = f"l{l}_"
    h = _rms_fwd(x, sp["norm_mix_g"][l], after=after, name=n + "norm_mix")
    qkv = _mm(h, W["qkv"], out_dtypes=(BF16,), name=n + "qkv")
    rest = _mm(h, W["rest"], name=n + "rest")
    pa = _pool_fwd(rest, c["wbd"], sp["pool_scale"][l], name=n + "pool")
    cum, cum_t = _fox_prep(rest, c["bpad"], name=n + "fox_prep")
    fk3 = cum_t[:FOX_H].reshape(FOX_H, S // t, t)
    o, lse = _fox_fwd(qkv, cum, fk3, name=n + "fox")
    sg = _sgu_fwd(rest, sp["sgu_norm_g"][l], c["wm"], c["sgu_bias"], name=n + "sgu")
    ya = _mm(pa, W["ba"], name=n + "branch_a")
    yb = _mm(o, W["bb"], name=n + "branch_b")
    yc = _mm(sg, W["bc"], name=n + "branch_c")
    merged = _merge_fwd(rest, c["bg"], ya, yb, yc, name=n + "merge")
    x1 = _mm(merged, W["out"], extras=(x,), epilogue=_add, name=n + "out")
    hx = _rms_fwd(x1, sp["norm_xattn_g"][l], name=n + "norm_xattn")
    hm = _rms_fwd(mem, sp["norm_mem_g"][l], name=n + "norm_mem")
    xq = _mm(hx, W["xq"], out_dtypes=(BF16,), name=n + "xq")
    kv = _mm(hm, W["xkv"], out_dtypes=(BF16,), name=n + "xkv")
    o2 = _xattn_fwd(xq, kv, name=n + "xattn")
    x2 = _mm(o2, W["xo"], extras=(x1,), epilogue=_add, name=n + "xo")
    hf = _rms_fwd(x2, sp["norm_ffn_g"][l], name=n + "norm_ffn")
    z, act = _mm(hf, W["ff1"], epilogue=_relu2, out_dtypes=(F32, BF16), name=n + "ff1")
    x3 = _mm(act, W["ff2"], extras=(x2,), epilogue=_add, name=n + "ff2")
    saved = dict(x=x, h=h, qkv=qkv, rest=rest, pa=pa, cum=cum, fk3=fk3, o=o, lse=lse, sg=sg, ya=ya, yb=yb, yc=yc,
                 merged=merged, x1=x1, hx=hx, hm=hm, xq=xq, kv=kv, o2=o2, x2=x2, hf=hf, z=z, act=act, c=c)
    return x3, saved


def _layer_bwd(l, dx3, sv, mem, W, sp, after):
    S = dx3.shape[0]
    c = sv["c"]
    n = f"l{l}b_"
    bf = dict(out_dtypes=(BF16,))
    gw, gs = {}, {}
    gw["ff2"] = _mm(sv["act"], dx3, ta=True, after=after, name=n + "dw_ff2", **bf)
    dz = _mm(dx3, W["ff2"], tb=True, extras=(sv["z"],), epilogue=_relu2_grad, name=n + "dz", **bf)
    gw["ff1"] = _mm(sv["hf"], dz, ta=True, shard_out=True, name=n + "dw_ff1", **bf)
    dhf = _mm(dz, W["ff1"], tb=True, name=n + "dhf")
    dx2, gs["norm_ffn_g"] = _rms_bwd(sv["x2"], sp["norm_ffn_g"][l], dhf, dx3, name=n + "dnorm_ffn")
    gw["xo"] = _mm(sv["o2"], dx2, ta=True, name=n + "dw_xo", **bf)
    do2 = _mm(dx2, W["xo"], tb=True, name=n + "do2", **bf)
    dxq, dkv = _xattn_bwd(sv["xq"], sv["kv"], do2, name=n + "dxattn")
    gw["xq"] = _mm(sv["hx"], dxq, ta=True, name=n + "dw_xq", **bf)
    gw["xkv"] = _mm(sv["hm"], dkv, ta=True, shard_out=True, name=n + "dw_xkv", **bf)
    dhm = _mm(dkv, W["xkv"], tb=True, name=n + "dhm")
    _, gs["norm_mem_g"] = _rms_bwd(mem, sp["norm_mem_g"][l], dhm, jnp.zeros_like(mem), name=n + "dnorm_mem")
    dhx = _mm(dxq, W["xq"], tb=True, name=n + "dhx")
    dx1, gs["norm_xattn_g"] = _rms_bwd(sv["x1"], sp["norm_xattn_g"][l], dhx, dx2, name=n + "dnorm_xattn")
    gw["out"] = _mm(sv["merged"], dx1, ta=True, name=n + "dw_out", **bf)
    dm = _mm(dx1, W["out"], tb=True, name=n + "dmerged")
    dya, dyb, dyc, dg1, dg2, dg3, db1, db2, db3 = _merge_bwd(sv["rest"], c["bg"], sv["ya"], sv["yb"], sv["yc"], dm, name=n + "dmerge")
    gs["b_gate"] = jnp.concatenate([db1, db2, db3], axis=1).reshape(3 * D)
    gw["ba"] = _mm(sv["pa"], dya, ta=True, shard_out=True, name=n + "dw_ba", **bf)
    gw["bb"] = _mm(sv["o"], dyb, ta=True, shard_out=True, name=n + "dw_bb", **bf)
    gw["bc"] = _mm(sv["sg"], dyc, ta=True, shard_out=True, name=n + "dw_bc", **bf)
    dpa = _mm(dya, W["ba"], tb=True, name=n + "dpa")
    do = _mm(dyb, W["bb"], tb=True, name=n + "do", **bf)
    dsg = _mm(dyc, W["bc"], tb=True, name=n + "dsg")
    da, dwbd, dscale = _pool_bwd(sv["rest"], c["wbd"], c["wbd_t"], sp["pool_scale"][l], dpa, name=n + "dpool")
    gs["pool_w"] = jnp.stack([dwbd[64 * gi:64 * (gi + 1), 64 * gi:64 * (gi + 1)] for gi in range(4)])
    gs["pool_scale"] = dscale.reshape(POOL_W)
    dq, dk, dv, dfq, dfk = _fox_bwd(sv["qkv"], sv["cum"], sv["fk3"], sv["o"], do, sv["lse"], name=n + "dfox")
    dcum = dfq[:, :, :2].transpose(1, 0, 2).reshape(S, FOX_H) + dfk.reshape(FOX_H, S).T
    df, dbf = _fox_post(sv["rest"], c["bpad"], jnp.pad(dcum, ((0, 0), (0, F_LANES - FOX_H))), name=n + "dfox_post")
    gs["b_forget"] = dbf[0, :FOX_H]
    dc, dwm, dbias, dgn = _sgu_bwd(sv["rest"], sp["sgu_norm_g"][l], c["wm"], c["wm_t"], c["sgu_bias"], dsg, name=n + "dsgu")
    gs["sgu_w"] = dwm * jnp.tril(jnp.ones((SGU_CHUNK, SGU_CHUNK), F32))[None]
    gs["sgu_b"] = dbias.reshape(SGU_CHUNK, 4, 64).sum(axis=2).T
    gs["sgu_norm_g"] = dgn.reshape(SGU_W)
    dqkv = jnp.concatenate([dq, dk, dv], axis=1)
    drest = jnp.concatenate([da, df, jnp.zeros((S, OFF_C - OFF_F - F_LANES), BF16), dc, dg1, dg2, dg3], axis=1)
    gw["qkv"] = _mm(sv["h"], dqkv, ta=True, name=n + "dw_qkv", **bf)
    gw["rest"] = _mm(sv["h"], drest, ta=True, name=n + "dw_rest", **bf)
    dh = _mm(dqkv, W["qkv"], tb=True, name=n + "dh_qkv")
    dh = _mm(drest, W["rest"], tb=True, extras=(dh,), epilogue=_add, name=n + "dh")
    dx, gs["norm_mix_g"] = _rms_bwd(sv["x"], sp["norm_mix_g"][l], dh, dx1, name=n + "dnorm_mix")
    return dx, gw, gs


def _local_step(x, mem, target, sp, weights, grads_done):
    saved, Ws = [], []
    for l in range(DEPTH):
        W, after = weights(l, x)
        Ws.append(W)
        x, sv = _layer_fwd(l, x, mem, W, sp, after)
        saved.append(sv)
    loss, dx, dgf = _final_loss(x, sp["final_norm_g"], target, name="final_loss")
    gss, after = [None] * DEPTH, None
    for l in reversed(range(DEPTH)):
        dx, gw, gss[l] = _layer_bwd(l, dx, saved[l], mem, Ws[l], sp, after)
        after = grads_done(l, gw)
    small = {k: jnp.stack([gss[l][k] for l in range(DEPTH)]) for k in gss[0]}
    small["final_norm_g"] = dgf
    return loss, dx, small


_SMALL = ["norm_mix_g", "b_forget", "pool_w", "pool_scale", "sgu_norm_g", "sgu_w", "sgu_b", "b_gate", "norm_xattn_g",
          "norm_mem_g", "norm_ffn_g", "final_norm_g"]
_COL = {"w_branch_a": "ba", "w_branch_b": "bb", "w_branch_c": "bc", "w_xkv": "xkv", "w_ff1": "ff1"}
_ROW = {"w_out": "out", "w_xq": "xq", "w_xo": "xo", "w_ff2": "ff2"}
_BIG = ["w_in", "w_branch_a", "w_branch_b", "w_branch_c", "w_out", "w_xq", "w_xkv", "w_xo", "w_ff1", "w_ff2"]
_PACK_LANES = 128


def _as_rows(a):
    return a.reshape(-1, a.shape[-1])


def _pack(tensors):
    rows = []
    for a in tensors:
        flat = a.reshape(-1)
        flat = jnp.pad(flat, (0, (-flat.shape[0]) % (8 * _PACK_LANES)))
        rows.append(flat.reshape(-1, _PACK_LANES))
    return jnp.concatenate(rows, axis=0)


def _unpack(packed, like):
    out, r = [], 0
    for a in like:
        size = math.prod(a.shape)
        nr = 8 * (-(-size // (8 * _PACK_LANES)))
        out.append(packed[r:r + nr].reshape(-1)[:size].reshape(a.shape))
        r += nr
    return out


def _split_w_in(w_in):
    K = w_in.shape[0]
    pad = lambda n: jnp.zeros((K, n), w_in.dtype)
    rest = jnp.concatenate(
        [w_in[:, :R_OFF_Q], w_in[:, R_OFF_F:R_OFF_C], pad(OFF_C - OFF_F - FOX_H), w_in[:, R_OFF_C:]], axis=1)
    return w_in[:, R_OFF_Q:R_OFF_F], rest


def _join_w_in(qkv, rest):
    return jnp.concatenate([rest[:, :R_OFF_Q], qkv, rest[:, OFF_F:OFF_F + FOX_H], rest[:, OFF_C:]], axis=1)


def _layer_weights(gathered):
    w_in = gathered["w_in"].transpose(1, 0, 2).reshape(D, N_IN)
    W = dict(zip(("qkv", "rest"), _split_w_in(w_in)))
    for name, key in _COL.items():
        W[key] = _Gathered(gathered[name])
    for name, key in _ROW.items():
        W[key] = gathered[name].reshape(-1, gathered[name].shape[-1])
    return W


def _grad_blocks(gw):
    parts = {"w_in": _join_w_in(gw["qkv"], gw["rest"]).reshape(D, N_DEV, -1).transpose(1, 0, 2)}
    for name, key in _COL.items():
        parts[name] = gw[key]
    for name, key in _ROW.items():
        parts[name] = gw[key].reshape(N_DEV, -1, gw[key].shape[-1])
    return parts


def kernel(x, mem, norm_mix_g, w_in, b_forget, pool_w, pool_scale, sgu_norm_g, sgu_w, sgu_b, w_branch_a, w_branch_b, w_branch_c, b_gate, w_out, norm_xattn_g, norm_mem_g, w_xq, w_xkv, w_xo, norm_ffn_g, w_ff1, w_ff2, final_norm_g, loss_target, m_norm_mix_g, m_w_in, m_b_forget, m_pool_w, m_pool_scale, m_sgu_norm_g, m_sgu_w, m_sgu_b, m_w_branch_a, m_w_branch_b, m_w_branch_c, m_b_gate, m_w_out, m_norm_xattn_g, m_norm_mem_g, m_w_xq, m_w_xkv, m_w_xo, m_norm_ffn_g, m_w_ff1, m_w_ff2, m_final_norm_g, v_norm_mix_g, v_w_in, v_b_forget, v_pool_w, v_pool_scale, v_sgu_norm_g, v_sgu_w, v_sgu_b, v_w_branch_a, v_w_branch_b, v_w_branch_c, v_b_gate, v_w_out, v_norm_xattn_g, v_norm_mem_g, v_w_xq, v_w_xkv, v_w_xo, v_norm_ffn_g, v_w_ff1, v_w_ff2, v_final_norm_g):
    names = ["norm_mix_g", "w_in", "b_forget", "pool_w", "pool_scale", "sgu_norm_g", "sgu_w", "sgu_b", "w_branch_a", "w_branch_b",
             "w_branch_c", "b_gate", "w_out", "norm_xattn_g", "norm_mem_g", "w_xq", "w_xkv", "w_xo", "norm_ffn_g", "w_ff1", "w_ff2",
             "final_norm_g"]
    w = dict(zip(names, [norm_mix_g, w_in, b_forget, pool_w, pool_scale, sgu_norm_g, sgu_w, sgu_b, w_branch_a, w_branch_b, w_branch_c,
                         b_gate, w_out, norm_xattn_g, norm_mem_g, w_xq, w_xkv, w_xo, norm_ffn_g, w_ff1, w_ff2, final_norm_g]))
    m = dict(zip(names, [m_norm_mix_g, m_w_in, m_b_forget, m_pool_w, m_pool_scale, m_sgu_norm_g, m_sgu_w, m_sgu_b, m_w_branch_a,
                         m_w_branch_b, m_w_branch_c, m_b_gate, m_w_out, m_norm_xattn_g, m_norm_mem_g, m_w_xq, m_w_xkv, m_w_xo,
                         m_norm_ffn_g, m_w_ff1, m_w_ff2, m_final_norm_g]))
    v = dict(zip(names, [v_norm_mix_g, v_w_in, v_b_forget, v_pool_w, v_pool_scale, v_sgu_norm_g, v_sgu_w, v_sgu_b, v_w_branch_a,
                         v_w_branch_b, v_w_branch_c, v_b_gate, v_w_out, v_norm_xattn_g, v_norm_mem_g, v_w_xq, v_w_xkv, v_w_xo,
                         v_norm_ffn_g, v_w_ff1, v_w_ff2, v_final_norm_g]))

    sp = {k: w[k] for k in _SMALL}
    shards = [[w[k][l].astype(BF16) for k in _BIG] for l in range(DEPTH)]
    me = _dev_index(*_position())
    pending = {}

    def weights(l, x_in):
        if l == 0:
            gathered = _all_gather(shards[0], name="gather_l0")
            pending["gather"], token = _exchange_start(shards[1], shards[1], blocked=False, name="gather_l1_start")
        else:
            gathered, token = _exchange_wait(pending.pop("gather"), x_in, name="gather_l1_wait"), None
        return _layer_weights(dict(zip(_BIG, gathered))), token

    def grads_done(l, gw):
        blocks = _grad_blocks(gw)
        parts = [blocks[k] for k in _BIG]
        if l == 0:
            pending["grads_l0"] = _exchange(parts, name="exchange_grads_l0")
            return None
        own = [lax.dynamic_index_in_dim(p, me, 0, keepdims=False) for p in parts]
        pending["grads_l1"], token = _exchange_start(parts, own, blocked=True, name="exchange_grads_l1_start")
        return token

    loss, dx, small = _local_step(x[0], mem[0], loss_target[0], sp, weights, grads_done)
    loss = lax.psum(loss[0, 0], ("x", "y", "c"))
    received = [dict(zip(_BIG, pending["grads_l0"])),
                dict(zip(_BIG, _exchange_wait(pending["grads_l1"], dx, name="exchange_grads_l1_wait")))]

    grads, deltas, new_m, new_v = {}, {}, {}, {}
    for k in _BIG:
        outs = _adamw_sharded([received[l][k] for l in range(DEPTH)], w[k], m[k], v[k], name="adamw_" + k)
        grads[k], deltas[k], new_m[k], new_v[k] = outs
    like = [w[k] for k in _SMALL]
    g_small = _unpack(_all_reduce(_pack([small[k] for k in _SMALL]), name="all_reduce_small"), like)
    rows = lambda d: [_as_rows(d[k]) for k in _SMALL]
    outs = _adamw_small([_as_rows(g) for g in g_small], rows(w), rows(m), rows(v), name="adamw_small")
    grads.update(zip(_SMALL, g_small))
    for dst, vals in zip((deltas, new_m, new_v), outs):
        dst.update({k: a.reshape(w[k].shape) for k, a in zip(_SMALL, vals)})

    return (loss, dx[None], *[grads[k] for k in names], *[deltas[k] for k in names], *[new_m[k] for k in names],
            *[new_v[k] for k in names])
```

```python
import functools
import math

import jax
import jax.numpy as jnp
from jax import lax
from jax.experimental import pallas as pl
from jax.experimental.pallas import tpu as pltpu

F32 = jnp.float32
BF16 = jnp.bfloat16
MESH = pl.DeviceIdType.MESH

N_DEV = 8
D = 1024
DEPTH = 2
EPS = 1e-6
NEG = -1e30
POOL_W = 256
FOX_H = 8
FOX_DH = 64
FOX_W = 512
SGU_W = 256
SGU_CHUNK = 128
XH = 4
XDH = 256
N_IN = 5384
R_OFF_Q, R_OFF_F, R_OFF_C = 256, 1792, 1800
QKV_W = 3 * FOX_W
OFF_A, OFF_F, OFF_C, OFF_G, REST_W = 0, 256, 512, 1024, 4096
F_LANES = 128

ADAM_LR = 0.001
ADAM_B1 = 0.9
ADAM_B2 = 0.999
ADAM_EPS = 1e-08
ADAM_WD = 0.01
ADAM_STEP = 10

VMEM_LIMIT = 56 * 1024 * 1024


def _tile(n, pref):
    t = min(n, pref)
    while n % t:
        t -= 128
    assert t > 0, (n, pref)
    return t


def _params(sem=None):
    return pltpu.CompilerParams(dimension_semantics=sem, vmem_limit_bytes=VMEM_LIMIT)


def _dot(a, b, ca, cb):
    return lax.dot_general(a, b, (((ca,), (cb,)), ((), ())), preferred_element_type=F32)


def _sigmoid(z):
    return 1.0 / (1.0 + jnp.exp(-z))


_GELU_K = math.sqrt(2.0 / math.pi)
_GELU_C = 0.044715


def _gelu(x):
    return 0.5 * x * (1.0 + jnp.tanh(_GELU_K * (x + _GELU_C * x * x * x)))


def _gelu_grad(x):
    t = jnp.tanh(_GELU_K * (x + _GELU_C * x * x * x))
    return 0.5 * (1.0 + t) + 0.5 * x * (1.0 - t * t) * _GELU_K * (1.0 + 3.0 * _GELU_C * x * x)


def _rows(shape):
    return lax.broadcasted_iota(jnp.int32, shape, 0)


def _lanes(shape):
    return lax.broadcasted_iota(jnp.int32, shape, 1)


class _Gathered:
    def __init__(self, arr):
        self.arr = arr
        self.shape = (arr.shape[1], N_DEV * arr.shape[2])


_TOKEN = (8, 128)


def _mm(a, b, *, ta=False, tb=False, extras=(), epilogue=None, out_dtypes=(F32,), shard_out=False, after=None, tm=None, tn=512, tk=None,
        name):
    M, K = (a.shape[1], a.shape[0]) if ta else a.shape
    N, Kb = b.shape if tb else b.shape[::-1]
    assert Kb == K, (a.shape, b.shape, ta, tb)
    gathered = isinstance(b, _Gathered)
    if gathered:
        if tb:
            tk = b.arr.shape[2]
        else:
            tn = b.arr.shape[2]
    if shard_out:
        tn = N // N_DEV
    tm = _tile(M, tm or (1024 if ta else 2048))
    tn = _tile(N, tn)
    tk = _tile(K, tk or (2048 if ta else 1024))
    nk = K // tk
    ca, cb = (0 if ta else 1), (1 if tb else 0)
    n_ex, n_out = len(extras), len(out_dtypes)
    tokens = [] if after is None else [after]
    n_in = 2 + n_ex + len(tokens)
    if epilogue is None:
        epilogue = lambda acc: (acc,)

    def body(*refs):
        a_ref, b_ref = refs[:2]
        ex_refs = refs[2:2 + n_ex]
        o_refs = refs[n_in:n_in + n_out]
        part = _dot(a_ref[...].astype(BF16), b_ref[...].astype(BF16), ca, cb)

        def finish(acc):
            for o_ref, val in zip(o_refs, epilogue(acc, *[e[...] for e in ex_refs])):
                o_ref[...] = val.astype(o_ref.dtype)

        if nk == 1:
            finish(part)
        else:
            acc_ref = refs[-1]
            k = pl.program_id(2)

            @pl.when(k == 0)
            def _():
                acc_ref[...] = part

            @pl.when(k > 0)
            def _():
                acc_ref[...] += part

            @pl.when(k == nk - 1)
            def _():
                finish(acc_ref[...])

    a_spec = pl.BlockSpec((tk, tm), lambda i, j, k: (k, i)) if ta else pl.BlockSpec((tm, tk), lambda i, j, k: (i, k))
    if not gathered:
        b_arr = b
        b_spec = pl.BlockSpec((tn, tk), lambda i, j, k: (j, k)) if tb else pl.BlockSpec((tk, tn), lambda i, j, k: (k, j))
    else:
        b_arr = b.arr
        if tb:
            b_spec = pl.BlockSpec((None, tn, tk), lambda i, j, k: (k, j, 0))
        else:
            b_spec = pl.BlockSpec((None, tk, tn), lambda i, j, k: (j, k, 0))
    tile = pl.BlockSpec((tm, tn), lambda i, j, k: (i, j))
    if shard_out:
        out_specs = [pl.BlockSpec((None, tm, tn), lambda i, j, k: (j, i, 0))] * n_out
        out_shape = [jax.ShapeDtypeStruct((N_DEV, M, tn), dt) for dt in out_dtypes]
    else:
        out_specs = [tile] * n_out
        out_shape = [jax.ShapeDtypeStruct((M, N), dt) for dt in out_dtypes]
    size = lambda dt: jnp.dtype(dt).itemsize
    vmem = 2 * (tm * tk * size(a.dtype) + tk * tn * size(b_arr.dtype)
                + tm * tn * (sum(size(e.dtype) for e in extras) + sum(map(size, out_dtypes))))
    vmem += tm * tn * 4 * (nk > 1)
    assert vmem <= VMEM_LIMIT - (4 << 20), (name, vmem)
    outs = pl.pallas_call(
        body,
        name=name,
        grid=(M // tm, N // tn, nk),
        in_specs=[a_spec, b_spec] + [tile] * n_ex + [pl.BlockSpec(_TOKEN, lambda i, j, k: (0, 0))] * len(tokens),
        out_specs=out_specs,
        out_shape=out_shape,
        scratch_shapes=[pltpu.VMEM((tm, tn), F32)] if nk > 1 else [],
        compiler_params=_params(("parallel", "parallel", "arbitrary")),
    )(a, b_arr, *extras, *tokens)
    return outs[0] if n_out == 1 else outs


def _add(acc, res):
    return (acc + res,)


def _rms_fwd(x, g, *, after=None, name):
    R, C = x.shape
    tm = _tile(R, 256)
    tokens = [] if after is None else [after]

    def body(x_ref, g_ref, *rest):
        xv = x_ref[...]
        r = lax.rsqrt(jnp.mean(xv * xv, axis=-1, keepdims=True) + EPS)
        rest[-1][...] = (xv * r * g_ref[...]).astype(BF16)

    return pl.pallas_call(
        body,
        name=name,
        grid=(R // tm,),
        in_specs=[pl.BlockSpec((tm, C), lambda i: (i, 0)), pl.BlockSpec((1, C), lambda i: (0, 0))]
        + [pl.BlockSpec(_TOKEN, lambda i: (0, 0))] * len(tokens),
        out_specs=pl.BlockSpec((tm, C), lambda i: (i, 0)),
        out_shape=jax.ShapeDtypeStruct((R, C), BF16),
        compiler_params=_params(("parallel",)),
    )(x, g.reshape(1, C), *tokens)


def _rms_bwd(x, g, dh, dres, *, name):
    R, C = x.shape
    tm = _tile(R, 256)

    def body(x_ref, g_ref, dh_ref, dres_ref, dx_ref, dg_ref):
        xv = x_ref[...]
        r = lax.rsqrt(jnp.mean(xv * xv, axis=-1, keepdims=True) + EPS)
        xn = xv * r
        dh_v = dh_ref[...].astype(F32)
        dxn = dh_v * g_ref[...]
        dx_ref[...] = r * (dxn - xn * jnp.mean(dxn * xn, axis=-1, keepdims=True)) + dres_ref[...]
        part = jnp.sum(dh_v * xn, axis=0, keepdims=True)

        @pl.when(pl.program_id(0) == 0)
        def _():
            dg_ref[...] = part

        @pl.when(pl.program_id(0) > 0)
        def _():
            dg_ref[...] += part

    row = pl.BlockSpec((tm, C), lambda i: (i, 0))
    vec = pl.BlockSpec((1, C), lambda i: (0, 0))
    dx, dg = pl.pallas_call(
        body,
        name=name,
        grid=(R // tm,),
        in_specs=[row, vec, row, row],
        out_specs=[row, vec],
        out_shape=[jax.ShapeDtypeStruct((R, C), F32), jax.ShapeDtypeStruct((1, C), F32)],
        compiler_params=_params(("arbitrary",)),
    )(x, g.reshape(1, C), dh, dres)
    return dx, dg.reshape(C)


def _final_loss(x, g, target, *, name):
    R, C = x.shape
    tm = _tile(R, 256)

    def body(x_ref, g_ref, t_ref, loss_ref, dx_ref, dg_ref):
        xv = x_ref[...]
        r = lax.rsqrt(jnp.mean(xv * xv, axis=-1, keepdims=True) + EPS)
        xn = xv * r
        gv = g_ref[...]
        err = xn * gv - t_ref[...]
        lpart = (0.5 / C) * jnp.sum(jnp.sum(err * err, axis=1, keepdims=True), axis=0, keepdims=True)
        dy = err * (1.0 / C)
        dxn = dy * gv
        dx_ref[...] = r * (dxn - xn * jnp.mean(dxn * xn, axis=-1, keepdims=True))
        gpart = jnp.sum(dy * xn, axis=0, keepdims=True)

        @pl.when(pl.program_id(0) == 0)
        def _():
            loss_ref[...] = lpart
            dg_ref[...] = gpart

        @pl.when(pl.program_id(0) > 0)
        def _():
            loss_ref[...] += lpart
            dg_ref[...] += gpart

    row = pl.BlockSpec((tm, C), lambda i: (i, 0))
    vec = pl.BlockSpec((1, C), lambda i: (0, 0))
    loss, dx, dg = pl.pallas_call(
        body,
        name=name,
        grid=(R // tm,),
        in_specs=[row, vec, row],
        out_specs=[pl.BlockSpec((1, 1), lambda i: (0, 0)), row, vec],
        out_shape=[jax.ShapeDtypeStruct((1, 1), F32), jax.ShapeDtypeStruct((R, C), F32), jax.ShapeDtypeStruct((1, C), F32)],
        compiler_params=_params(("arbitrary",)),
    )(x, g.reshape(1, C), target)
    return loss, dx, dg.reshape(C)


def _pool_select(lane, vals):
    out = vals[3]
    for gi in (2, 1, 0):
        out = jnp.where(lane < 64 * (gi + 1), vals[gi], out)
    return out


def _pool_diff(a):
    row, lane = _rows(a.shape), _lanes(a.shape)

    def down(v, k):
        return jnp.where(row >= k, pltpu.roll(v, k, 0), 0.0)

    s2 = a + down(a, 1)
    s4 = s2 + down(s2, 2)
    s8 = s4 + down(s4, 4)
    s16 = s8 + down(s8, 8)
    wsum = _pool_select(lane, (s2, s4, s8, s16))
    win = _pool_select(lane, (2, 4, 8, 16))
    cnt = jnp.minimum(row + 1, win).astype(F32)
    return wsum / cnt - a, cnt


def _pool_diff_t(dd, cnt):
    S = dd.shape[0]
    row, lane = _rows(dd.shape), _lanes(dd.shape)

    def up(v, k):
        return jnp.where(row < S - k, pltpu.roll(v, S - k, 0), 0.0)

    e = dd / cnt
    s2 = e + up(e, 1)
    s4 = s2 + up(s2, 2)
    s8 = s4 + up(s4, 4)
    s16 = s8 + up(s8, 8)
    return _pool_select(lane, (s2, s4, s8, s16)) - dd


def _pool_fwd(rest, wbd, scale, *, name):
    S = rest.shape[0]

    def body(a_ref, w_ref, s_ref, o_ref):
        d, _ = _pool_diff(a_ref[...])
        yp = _dot(d.astype(BF16), w_ref[...], 1, 0)
        o_ref[...] = (yp * s_ref[...]).astype(BF16)

    return pl.pallas_call(
        body,
        name=name,
        grid=(1,),
        in_specs=[
            pl.BlockSpec((S, POOL_W), lambda i: (0, OFF_A // POOL_W)),
            pl.BlockSpec((POOL_W, POOL_W), lambda i: (0, 0)),
            pl.BlockSpec((1, POOL_W), lambda i: (0, 0)),
        ],
        out_specs=pl.BlockSpec((S, POOL_W), lambda i: (0, 0)),
        out_shape=jax.ShapeDtypeStruct((S, POOL_W), BF16),
        compiler_params=_params(("arbitrary",)),
    )(rest, wbd, scale.reshape(1, POOL_W))


def _pool_bwd(rest, wbd, wbd_t, scale, dpa, *, name):
    S = rest.shape[0]

    def body(a_ref, w_ref, wt_ref, s_ref, dpa_ref, da_ref, dw_ref, ds_ref):
        d, cnt = _pool_diff(a_ref[...])
        db = d.astype(BF16)
        yp = _dot(db, w_ref[...], 1, 0)
        dpa_v = dpa_ref[...]
        ds_ref[...] = jnp.sum(dpa_v * yp, axis=0, keepdims=True)
        dyp = (dpa_v * s_ref[...]).astype(BF16)
        dw_ref[...] = _dot(db, dyp, 0, 0)
        dd = _dot(dyp, wt_ref[...], 1, 0)
        da_ref[...] = _pool_diff_t(dd, cnt).astype(BF16)

    full = pl.BlockSpec((S, POOL_W), lambda i: (0, 0))
    sq = pl.BlockSpec((POOL_W, POOL_W), lambda i: (0, 0))
    vec = pl.BlockSpec((1, POOL_W), lambda i: (0, 0))
    return pl.pallas_call(
        body,
        name=name,
        grid=(1,),
        in_specs=[pl.BlockSpec((S, POOL_W), lambda i: (0, OFF_A // POOL_W)), sq, sq, vec, full],
        out_specs=[full, sq, vec],
        out_shape=[
            jax.ShapeDtypeStruct((S, POOL_W), BF16),
            jax.ShapeDtypeStruct((POOL_W, POOL_W), F32),
            jax.ShapeDtypeStruct((1, POOL_W), F32),
        ],
        compiler_params=_params(("arbitrary",)),
    )(rest, wbd, wbd_t, scale.reshape(1, POOL_W), dpa)


def _log_sigmoid(z):
    return jnp.minimum(z, 0.0) - jnp.log(1.0 + jnp.exp(-jnp.abs(z)))


_F_SPEC_COL = OFF_F // F_LANES


def _fox_prep(rest, bpad, *, name):
    S = rest.shape[0]

    def body(f_ref, b_ref, o_ref, ot_ref):
        acc = _log_sigmoid(f_ref[...] + b_ref[...])
        row = _rows(acc.shape)
        k = 1
        while k < S:
            acc = acc + jnp.where(row >= k, pltpu.roll(acc, k, 0), 0.0)
            k *= 2
        o_ref[...] = acc
        ot_ref[...] = acc.T

    return pl.pallas_call(
        body,
        name=name,
        grid=(1,),
        in_specs=[pl.BlockSpec((S, F_LANES), lambda i: (0, _F_SPEC_COL)), pl.BlockSpec((1, F_LANES), lambda i: (0, 0))],
        out_specs=[pl.BlockSpec((S, F_LANES), lambda i: (0, 0)), pl.BlockSpec((F_LANES, S), lambda i: (0, 0))],
        out_shape=[jax.ShapeDtypeStruct((S, F_LANES), F32), jax.ShapeDtypeStruct((F_LANES, S), F32)],
        compiler_params=_params(("arbitrary",)),
    )(rest, bpad)


def _fox_post(rest, bpad, dcum, *, name):
    S = rest.shape[0]

    def body(f_ref, b_ref, d_ref, df_ref, db_ref):
        acc = d_ref[...]
        row = _rows(acc.shape)
        k = 1
        while k < S:
            acc = acc + jnp.where(row < S - k, pltpu.roll(acc, S - k, 0), 0.0)
            k *= 2
        df = acc * (1.0 - _sigmoid(f_ref[...] + b_ref[...]))
        df_ref[...] = df.astype(BF16)
        db_ref[...] = jnp.sum(df, axis=0, keepdims=True)

    full = pl.BlockSpec((S, F_LANES), lambda i: (0, 0))
    vec = pl.BlockSpec((1, F_LANES), lambda i: (0, 0))
    return pl.pallas_call(
        body,
        name=name,
        grid=(1,),
        in_specs=[pl.BlockSpec((S, F_LANES), lambda i: (0, _F_SPEC_COL)), vec, full],
        out_specs=[full, vec],
        out_shape=[jax.ShapeDtypeStruct((S, F_LANES), BF16), jax.ShapeDtypeStruct((1, F_LANES), F32)],
        compiler_params=_params(("arbitrary",)),
    )(rest, bpad, dcum)


_FOX_SCALE = FOX_DH ** -0.5
_PAIRS = FOX_H // 2


def _scaled(v):
    return (v.astype(F32) * _FOX_SCALE).astype(BF16)


def _head_lane(cum, h):
    return jnp.sum(jnp.where(_lanes(cum.shape) == h, cum, 0.0), axis=-1, keepdims=True)


def _diag_mask(s):
    return jnp.where(_rows(s.shape) >= _lanes(s.shape), s, NEG)


def _fox_fwd(qkv, cum, fk3, *, name):
    S = qkv.shape[0]
    nk, t = fk3.shape[1:]

    def body(q_ref, k_ref, v_ref, cum_ref, fk_ref, o_ref, lse_ref, m_sc, l_sc, acc_sc):
        hp, i = pl.program_id(0), pl.program_id(1)
        lane = _lanes((t, 128))
        lo = lane < FOX_DH
        qs = _scaled(q_ref[...])
        zero = jnp.zeros_like(qs)
        qm = (jnp.where(lo, qs, zero), jnp.where(lo, zero, qs))
        cumv = cum_ref[...]
        fq = [_head_lane(cumv, 2 * hp + e) for e in range(2)]
        m_sc[...] = jnp.full(m_sc.shape, NEG, F32)
        l_sc[...] = jnp.zeros(l_sc.shape, F32)
        acc_sc[...] = jnp.zeros(acc_sc.shape, F32)

        def tile(j, masked):
            k0 = pl.multiple_of(j * t, t)
            kb = k_ref[pl.ds(k0, t), :]
            vb = v_ref[pl.ds(k0, t), :]
            alphas, pvs = [], []
            for e in range(2):
                s = _dot(qm[e], kb, 1, 1) + fq[e] - fk_ref[2 * hp + e, pl.ds(j, 1), :]
                if masked:
                    s = _diag_mask(s)
                m_old = m_sc[e]
                m_new = jnp.maximum(m_old, jnp.max(s, axis=-1, keepdims=True))
                p = jnp.exp(s - m_new)
                alpha = jnp.exp(m_old - m_new)
                l_sc[e] = alpha * l_sc[e] + jnp.sum(p, axis=-1, keepdims=True)
                m_sc[e] = m_new
                alphas.append(alpha)
                pvs.append(_dot(p.astype(BF16), vb, 1, 0))
            acc_sc[...] = jnp.where(lo, alphas[0], alphas[1]) * acc_sc[...] + jnp.where(lo, pvs[0], pvs[1])

        def step(j, carry):
            tile(j, False)
            return carry

        lax.fori_loop(0, i, step, 0)
        tile(i, True)
        o_ref[...] = acc_sc[...] / jnp.where(lo, l_sc[0], l_sc[1])
        lse = [m_sc[e] + jnp.log(l_sc[e]) for e in range(2)]
        lse_ref[...] = jnp.where(lane == 0, lse[0], jnp.where(lane == 1, lse[1], 0.0))

    return pl.pallas_call(
        body,
        name=name,
        grid=(_PAIRS, S // t),
        in_specs=[
            pl.BlockSpec((t, 128), lambda hp, i: (i, hp)),
            pl.BlockSpec((S, 128), lambda hp, i: (0, _PAIRS + hp)),
            pl.BlockSpec((S, 128), lambda hp, i: (0, 2 * _PAIRS + hp)),
            pl.BlockSpec((t, F_LANES), lambda hp, i: (i, 0)),
            pl.BlockSpec((FOX_H, nk, t), lambda hp, i: (0, 0, 0)),
        ],
        out_specs=[pl.BlockSpec((t, 128), lambda hp, i: (i, hp)), pl.BlockSpec((None, t, 128), lambda hp, i: (hp, i, 0))],
        out_shape=[jax.ShapeDtypeStruct((S, FOX_W), F32), jax.ShapeDtypeStruct((_PAIRS, S, 128), F32)],
        scratch_shapes=[pltpu.VMEM((2, t, 1), F32), pltpu.VMEM((2, t, 1), F32), pltpu.VMEM((t, 128), F32)],
        compiler_params=_params(("parallel", "arbitrary")),
    )(qkv, qkv, qkv, cum, fk3)


def _fox_bwd(qkv, cum, fk3, o, do, lse, *, name):
    S = qkv.shape[0]
    nk, t = fk3.shape[1:]

    def body(q_ref, k_ref, v_ref, cum_ref, fk_ref, o_ref, do_ref, lse_ref, dq_ref, dk_ref, dv_ref, dfq_ref, dfk_ref,
             qm_sc, km_sc, dom_sc, delta_sc, fq_sc, dfq_sc, dq_sc):
        hp = pl.program_id(0)
        lane = _lanes((t, 128))
        lo = lane < FOX_DH

        def prep(i, carry):
            r = pl.ds(pl.multiple_of(i * t, t), t)
            qs, ks, dob = _scaled(q_ref[r, :]), _scaled(k_ref[r, :]), do_ref[r, :]
            prod = dob.astype(F32) * o_ref[r, :]
            cumv = cum_ref[r, :]
            zero = jnp.zeros_like(qs)
            for e in range(2):
                mine = lo if e == 0 else jnp.logical_not(lo)
                qm_sc[e, r, :] = jnp.where(mine, qs, zero)
                km_sc[e, r, :] = jnp.where(mine, ks, zero)
                dom_sc[e, r, :] = jnp.where(mine, dob, zero)
                delta_sc[e, r, :] = jnp.sum(jnp.where(mine, prod, 0.0), axis=-1, keepdims=True)
                fq_sc[e, r, :] = _head_lane(cumv, 2 * hp + e)
                dfq_sc[e, r, :] = jnp.zeros((t, 1), F32)
            dq_sc[r, :] = jnp.zeros((t, 128), F32)
            return carry

        lax.fori_loop(0, nk, prep, 0)

        def kv_tile(j, carry):
            kr = pl.ds(pl.multiple_of(j * t, t), t)
            kb, vb = k_ref[kr, :], v_ref[kr, :]
            fks = [fk_ref[2 * hp + e, pl.ds(j, 1), :] for e in range(2)]

            def q_tile(i, acc, masked):
                dk, dv, dfk0, dfk1 = acc
                dfk = [dfk0, dfk1]
                qr = pl.ds(pl.multiple_of(i * t, t), t)
                dq_t = jnp.zeros((t, 128), F32)
                for e in range(2):
                    qe, doe = qm_sc[e, qr, :], dom_sc[e, qr, :]
                    s = _dot(qe, kb, 1, 1) + fq_sc[e, qr, :] - fks[e]
                    if masked:
                        s = _diag_mask(s)
                    p = jnp.exp(s - lse_ref[qr, e:e + 1])
                    dv = dv + _dot(p.astype(BF16), doe, 0, 0)
                    dp = _dot(doe, vb, 1, 1)
                    ds = p * (dp - delta_sc[e, qr, :])
                    dsb = ds.astype(BF16)
                    dk = dk + _dot(dsb, qe, 0, 0)
                    dq_t = dq_t + _dot(dsb, km_sc[e, kr, :], 1, 0)
                    dfq_sc[e, qr, :] += jnp.sum(ds, axis=-1, keepdims=True)
                    dfk[e] = dfk[e] - jnp.sum(ds, axis=0, keepdims=True)
                dq_sc[qr, :] += dq_t
                return dk, dv, dfk[0], dfk[1]

            init = (jnp.zeros((t, 128), F32), jnp.zeros((t, 128), F32), jnp.zeros((1, t), F32), jnp.zeros((1, t), F32))
            acc = q_tile(j, init, True)
            dk, dv, dfk0, dfk1 = lax.fori_loop(j + 1, nk, functools.partial(q_tile, masked=False), acc)
            dk_ref[kr, :] = dk.astype(BF16)
            dv_ref[kr, :] = dv.astype(BF16)
            dfk_ref[0, pl.ds(j, 1), :] = dfk0
            dfk_ref[1, pl.ds(j, 1), :] = dfk1
            return carry

        lax.fori_loop(0, nk, kv_tile, 0)
        dq_ref[...] = dq_sc[...].astype(BF16)
        lane_s = _lanes((S, 128))
        dfq_ref[...] = jnp.where(lane_s == 0, dfq_sc[0], jnp.where(lane_s == 1, dfq_sc[1], 0.0))

    col = lambda c0: pl.BlockSpec((S, 128), lambda hp: (0, c0 + hp))
    pair = pl.BlockSpec((S, 128), lambda hp: (0, hp))
    lanes3 = pl.BlockSpec((None, S, 128), lambda hp: (hp, 0, 0))
    big = jax.ShapeDtypeStruct((S, FOX_W), BF16)
    masked_bf16 = pltpu.VMEM((2, S, 128), BF16)
    column = pltpu.VMEM((2, S, 1), F32)
    return pl.pallas_call(
        body,
        name=name,
        grid=(_PAIRS,),
        in_specs=[
            col(0), col(_PAIRS), col(2 * _PAIRS),
            pl.BlockSpec((S, F_LANES), lambda hp: (0, 0)),
            pl.BlockSpec((FOX_H, nk, t), lambda hp: (0, 0, 0)),
            pair, pair, lanes3,
        ],
        out_specs=[pair, pair, pair, lanes3, pl.BlockSpec((None, 2, nk, t), lambda hp: (hp, 0, 0, 0))],
        out_shape=[big, big, big, jax.ShapeDtypeStruct((_PAIRS, S, 128), F32), jax.ShapeDtypeStruct((_PAIRS, 2, nk, t), F32)],
        scratch_shapes=[masked_bf16, masked_bf16, masked_bf16, column, column, column, pltpu.VMEM((S, 128), F32)],
        compiler_params=_params(("parallel",)),
    )(qkv, qkv, qkv, cum, fk3, o, do, lse)


def _group_mask(lane, gi):
    return (lane >= 64 * gi) & (lane < 64 * (gi + 1))


_U_COL = OFF_C // SGU_W


def _sgu_fwd(rest, gn, wm, bias, *, name):
    S = rest.shape[0]
    ts = _tile(S, 512)
    nc = ts // SGU_CHUNK

    def body(u_ref, v_ref, g_ref, w_ref, b_ref, o_ref):
        zv = _gelu(v_ref[...])
        vn = zv * lax.rsqrt(jnp.mean(zv * zv, axis=-1, keepdims=True) + EPS) * g_ref[...]
        lane = _lanes((SGU_CHUNK, SGU_W))
        for c in range(nc):
            rows = slice(c * SGU_CHUNK, (c + 1) * SGU_CHUNK)
            vcb = vn[rows].astype(BF16)
            mixed = b_ref[...]
            for gi in range(4):
                mixed = mixed + jnp.where(_group_mask(lane, gi), _dot(w_ref[gi], vcb, 1, 0), 0.0)
            o_ref[rows, :] = (_gelu(u_ref[rows, :]) * mixed).astype(BF16)

    return pl.pallas_call(
        body,
        name=name,
        grid=(S // ts,),
        in_specs=[
            pl.BlockSpec((ts, SGU_W), lambda i: (i, _U_COL)),
            pl.BlockSpec((ts, SGU_W), lambda i: (i, _U_COL + 1)),
            pl.BlockSpec((1, SGU_W), lambda i: (0, 0)),
            pl.BlockSpec((4, SGU_CHUNK, SGU_CHUNK), lambda i: (0, 0, 0)),
            pl.BlockSpec((SGU_CHUNK, SGU_W), lambda i: (0, 0)),
        ],
        out_specs=pl.BlockSpec((ts, SGU_W), lambda i: (i, 0)),
        out_shape=jax.ShapeDtypeStruct((S, SGU_W), BF16),
        compiler_params=_params(("parallel",)),
    )(rest, rest, gn.reshape(1, SGU_W), wm, bias)


def _sgu_bwd(rest, gn, wm, wm_t, bias, dsg, *, name):
    S = rest.shape[0]
    ts = _tile(S, 512)
    nc = ts // SGU_CHUNK

    def body(u_ref, v_ref, g_ref, w_ref, wt_ref, b_ref, dsg_ref, dc_ref, dw_ref, db_ref, dg_ref):
        first = pl.program_id(0) == 0

        @pl.when(first)
        def _():
            dw_ref[...] = jnp.zeros_like(dw_ref)
            db_ref[...] = jnp.zeros_like(db_ref)
            dg_ref[...] = jnp.zeros_like(dg_ref)

        gv = g_ref[...]
        lane = _lanes((SGU_CHUNK, SGU_W))
        for c in range(nc):
            rows = slice(c * SGU_CHUNK, (c + 1) * SGU_CHUNK)
            vpre = v_ref[rows, :]
            upre = u_ref[rows, :]
            zv = _gelu(vpre)
            r = lax.rsqrt(jnp.mean(zv * zv, axis=-1, keepdims=True) + EPS)
            zn = zv * r
            vcb = (zn * gv).astype(BF16)
            mixed = b_ref[...]
            for gi in range(4):
                mixed = mixed + jnp.where(_group_mask(lane, gi), _dot(w_ref[gi], vcb, 1, 0), 0.0)
            zu = _gelu(upre)
            dsg_v = dsg_ref[rows, :]
            dc_ref[rows, :SGU_W] = (dsg_v * mixed * _gelu_grad(upre)).astype(BF16)
            dmixed = dsg_v * zu
            db_ref[...] += dmixed
            dvn = jnp.zeros((SGU_CHUNK, SGU_W), F32)
            for gi in range(4):
                dmg = jnp.where(_group_mask(lane, gi), dmixed, 0.0).astype(BF16)
                dw_ref[gi] += _dot(dmg, vcb, 1, 1)
                dvn = dvn + _dot(wt_ref[gi], dmg, 1, 0)
            dg_ref[...] += jnp.sum(dvn * zn, axis=0, keepdims=True)
            dzn = dvn * gv
            dzv = r * (dzn - zn * jnp.mean(dzn * zn, axis=-1, keepdims=True))
            dc_ref[rows, SGU_W:] = (dzv * _gelu_grad(vpre)).astype(BF16)

    blk = pl.BlockSpec((ts, SGU_W), lambda i: (i, 0))
    vec = pl.BlockSpec((1, SGU_W), lambda i: (0, 0))
    w3 = pl.BlockSpec((4, SGU_CHUNK, SGU_CHUNK), lambda i: (0, 0, 0))
    bsp = pl.BlockSpec((SGU_CHUNK, SGU_W), lambda i: (0, 0))
    return pl.pallas_call(
        body,
        name=name,
        grid=(S // ts,),
        in_specs=[
            pl.BlockSpec((ts, SGU_W), lambda i: (i, _U_COL)),
            pl.BlockSpec((ts, SGU_W), lambda i: (i, _U_COL + 1)),
            vec, w3, w3, bsp, blk,
        ],
        out_specs=[pl.BlockSpec((ts, 2 * SGU_W), lambda i: (i, 0)), w3, bsp, vec],
        out_shape=[
            jax.ShapeDtypeStruct((S, 2 * SGU_W), BF16),
            jax.ShapeDtypeStruct((4, SGU_CHUNK, SGU_CHUNK), F32),
            jax.ShapeDtypeStruct((SGU_CHUNK, SGU_W), F32),
            jax.ShapeDtypeStruct((1, SGU_W), F32),
        ],
        compiler_params=_params(("arbitrary",)),
    )(rest, rest, gn.reshape(1, SGU_W), wm, wm_t, bias, dsg)


_GT = 512
_G0 = OFF_G // _GT


def _gate_specs(tm, col_of):
    specs = [pl.BlockSpec((tm, _GT), functools.partial(lambda k, *ids: (col_of(*ids)[0], _G0 + 2 * k + col_of(*ids)[1]), k)) for k in range(3)]
    specs += [pl.BlockSpec((1, _GT), functools.partial(lambda k, *ids: (0, 2 * k + col_of(*ids)[1]), k)) for k in range(3)]
    return specs


def _merge_fwd(rest, bg, ya, yb, yc, *, name):
    S = rest.shape[0]
    tm = _tile(S, 512)

    def body(g1, g2, g3, b1, b2, b3, ya_ref, yb_ref, yc_ref, o_ref):
        acc = _sigmoid(g1[...] + b1[...]) * ya_ref[...]
        acc = acc + _sigmoid(g2[...] + b2[...]) * yb_ref[...]
        acc = acc + _sigmoid(g3[...] + b3[...]) * yc_ref[...]
        o_ref[...] = acc.astype(BF16)

    blk = pl.BlockSpec((tm, _GT), lambda i, j: (i, j))
    return pl.pallas_call(
        body,
        name=name,
        grid=(S // tm, D // _GT),
        in_specs=_gate_specs(tm, lambda i, j: (i, j)) + [blk, blk, blk],
        out_specs=blk,
        out_shape=jax.ShapeDtypeStruct((S, D), BF16),
        compiler_params=_params(("parallel", "parallel")),
    )(rest, rest, rest, bg, bg, bg, ya, yb, yc)


def _merge_bwd(rest, bg, ya, yb, yc, dm, *, name):
    S = rest.shape[0]
    tm = _tile(S, 512)

    def body(g1, g2, g3, b1, b2, b3, ya_ref, yb_ref, yc_ref, dm_ref, dya, dyb, dyc, dg1, dg2, dg3, db1, db2, db3):
        first = pl.program_id(1) == 0
        dmv = dm_ref[...]
        for g_ref, b_ref, y_ref, dy_ref, dg_ref, db_ref in (
            (g1, b1, ya_ref, dya, dg1, db1), (g2, b2, yb_ref, dyb, dg2, db2), (g3, b3, yc_ref, dyc, dg3, db3)):
            gate = _sigmoid(g_ref[...] + b_ref[...])
            dy_ref[...] = (dmv * gate).astype(BF16)
            dpre = dmv * y_ref[...] * gate * (1.0 - gate)
            dg_ref[...] = dpre.astype(BF16)
            part = jnp.sum(dpre, axis=0, keepdims=True)

            @pl.when(first)
            def _():
                db_ref[...] = part

            @pl.when(jnp.logical_not(first))
            def _():
                db_ref[...] += part

    blk = pl.BlockSpec((tm, _GT), lambda j, i: (i, j))
    vec = pl.BlockSpec((1, _GT), lambda j, i: (0, j))
    big = jax.ShapeDtypeStruct((S, D), BF16)
    small = jax.ShapeDtypeStruct((1, D), F32)
    return pl.pallas_call(
        body,
        name=name,
        grid=(D // _GT, S // tm),
        in_specs=_gate_specs(tm, lambda j, i: (i, j)) + [blk, blk, blk, blk],
        out_specs=[blk] * 6 + [vec] * 3,
        out_shape=[big] * 6 + [small] * 3,
        compiler_params=_params(("parallel", "arbitrary")),
    )(rest, rest, rest, bg, bg, bg, ya, yb, yc, dm)


_X_SCALE = XDH ** -0.5


def _xattn_fwd(xq, kv, *, name):
    S = xq.shape[0]
    M = kv.shape[0]
    tq = _tile(S, 512)

    def body(q_ref, k_ref, v_ref, o_ref):
        s = _dot(q_ref[...], k_ref[...], 1, 1) * _X_SCALE
        e = jnp.exp(s - jnp.max(s, axis=-1, keepdims=True))
        p = e / jnp.sum(e, axis=-1, keepdims=True)
        o_ref[...] = _dot(p.astype(BF16), v_ref[...], 1, 0).astype(BF16)

    return pl.pallas_call(
        body,
        name=name,
        grid=(S // tq, XH),
        in_specs=[
            pl.BlockSpec((tq, XDH), lambda i, h: (i, h)),
            pl.BlockSpec((M, XDH), lambda i, h: (0, h)),
            pl.BlockSpec((M, XDH), lambda i, h: (0, XH + h)),
        ],
        out_specs=pl.BlockSpec((tq, XDH), lambda i, h: (i, h)),
        out_shape=jax.ShapeDtypeStruct((S, D), BF16),
        compiler_params=_params(("parallel", "parallel")),
    )(xq, kv, kv)


def _xattn_bwd(xq, kv, do, *, name):
    S = xq.shape[0]
    M = kv.shape[0]
    tq = _tile(S, 512)

    def body(q_ref, k_ref, v_ref, do_ref, dq_ref, dk_ref, dv_ref):
        qb = q_ref[...]
        kb = k_ref[...]
        dob = do_ref[...]
        s = _dot(qb, kb, 1, 1) * _X_SCALE
        e = jnp.exp(s - jnp.max(s, axis=-1, keepdims=True))
        p = e / jnp.sum(e, axis=-1, keepdims=True)
        dp = _dot(dob, v_ref[...], 1, 1)
        ds = (p * (dp - jnp.sum(p * dp, axis=-1, keepdims=True)) * _X_SCALE).astype(BF16)
        dq_ref[...] = _dot(ds, kb, 1, 0).astype(BF16)
        dk_part = _dot(ds, qb, 0, 0)
        dv_part = _dot(p.astype(BF16), dob, 0, 0)

        @pl.when(pl.program_id(1) == 0)
        def _():
            dk_ref[...] = dk_part
            dv_ref[...] = dv_part

        @pl.when(pl.program_id(1) > 0)
        def _():
            dk_ref[...] += dk_part
            dv_ref[...] += dv_part

    qspec = pl.BlockSpec((tq, XDH), lambda h, i: (i, h))
    kspec = pl.BlockSpec((M, XDH), lambda h, i: (0, h))
    dxq, dxk, dxv = pl.pallas_call(
        body,
        name=name,
        grid=(XH, S // tq),
        in_specs=[qspec, kspec, pl.BlockSpec((M, XDH), lambda h, i: (0, XH + h)), qspec],
        out_specs=[qspec, kspec, kspec],
        out_shape=[jax.ShapeDtypeStruct((S, D), BF16), jax.ShapeDtypeStruct((M, D), F32), jax.ShapeDtypeStruct((M, D), F32)],
        compiler_params=_params(("parallel", "arbitrary")),
    )(xq, kv, kv, do)
    return dxq, jnp.concatenate([dxk, dxv], axis=1)


def _adam_math(w, g, m, v):
    m = ADAM_B1 * m + (1.0 - ADAM_B1) * g
    v = ADAM_B2 * v + (1.0 - ADAM_B2) * (g * g)
    m_hat = m / (1.0 - ADAM_B1 ** ADAM_STEP)
    v_hat = v / (1.0 - ADAM_B2 ** ADAM_STEP)
    delta = -ADAM_LR * (m_hat / (jnp.sqrt(v_hat) + ADAM_EPS) + ADAM_WD * w)
    return delta, m, v


def _adamw_sharded(parts, w, m, v, *, name):
    _, R, C = w.shape
    tm = _tile(R, 256)
    nr = R // tm

    def body(p0_ref, p1_ref, w_ref, m_ref, v_ref, g_ref, d_ref, mo_ref, vo_ref):
        def update(p_ref):
            g = p_ref[0].astype(F32)
            for dev in range(1, N_DEV):
                g = g + p_ref[dev].astype(F32)
            delta, mn, vn = _adam_math(w_ref[...], g, m_ref[...], v_ref[...])
            g_ref[...] = g
            d_ref[...] = delta
            mo_ref[...] = mn
            vo_ref[...] = vn

        @pl.when(pl.program_id(0) == 0)
        def _():
            update(p0_ref)

        @pl.when(pl.program_id(0) == 1)
        def _():
            update(p1_ref)

    p0 = pl.BlockSpec((N_DEV, tm, C), lambda l, i: (0, i * (1 - l) + (nr - 1) * l, 0))
    p1 = pl.BlockSpec((N_DEV, tm, C), lambda l, i: (0, i * l, 0))
    blk = pl.BlockSpec((None, tm, C), lambda l, i: (l, i, 0))
    sds = jax.ShapeDtypeStruct(w.shape, F32)
    return pl.pallas_call(
        body,
        name=name,
        grid=(DEPTH, nr),
        in_specs=[p0, p1, blk, blk, blk],
        out_specs=[blk] * 4,
        out_shape=[sds] * 4,
        compiler_params=_params(("arbitrary", "arbitrary")),
    )(parts[0], parts[1], w, m, v)


def _adamw_small(g, w, m, v, *, name):
    n = len(g)

    def body(*refs):
        g_refs, w_refs, m_refs, v_refs = (refs[k * n:(k + 1) * n] for k in range(4))
        d_out, m_out, v_out = (refs[(4 + k) * n:(5 + k) * n] for k in range(3))
        for t in range(n):
            delta, mn, vn = _adam_math(w_refs[t][...], g_refs[t][...], m_refs[t][...], v_refs[t][...])
            d_out[t][...] = delta
            m_out[t][...] = mn
            v_out[t][...] = vn

    vm = pl.BlockSpec(memory_space=pltpu.VMEM)
    shapes = [jax.ShapeDtypeStruct(a.shape, F32) for a in w]
    outs = pl.pallas_call(
        body,
        name=name,
        in_specs=[vm] * (4 * n),
        out_specs=[vm] * (3 * n),
        out_shape=shapes * 3,
        compiler_params=pltpu.CompilerParams(vmem_limit_bytes=VMEM_LIMIT),
    )(*g, *w, *m, *v)
    return outs[:n], outs[n:2 * n], outs[2 * n:]


def _position():
    return lax.axis_index("x"), lax.axis_index("y"), lax.axis_index("c")


def _dev_index(px, py, pc):
    return 4 * px + 2 * py + pc


_ANY = pl.BlockSpec(memory_space=pl.ANY)


def _all_gather(shards, *, name):
    n = len(shards)

    def body(*refs):
        ins, outs = refs[:n], refs[n:2 * n]
        send_sems, recv_sems, local_sems = refs[2 * n:]
        x, y, c = _position()
        me, sibling = (x, y, c), (x, y, 1 - c)
        chips = [(1 - x, y), (x, 1 - y), (1 - x, 1 - y)]

        def copy(t, k, block, to, src=None):
            dst = outs[t].at[_dev_index(*block)]
            return pltpu.make_async_remote_copy(
                src_ref=dst if src is None else src, dst_ref=dst, send_sem=send_sems.at[t, k], recv_sem=recv_sems.at[t, k],
                device_id=to, device_id_type=MESH)

        mine = [pltpu.make_async_copy(ins[t], outs[t].at[_dev_index(*me)], local_sems.at[t]) for t in range(n)]
        for cp in mine:
            cp.start()
        started = []
        for j, chip in enumerate(chips):
            for t in range(n):
                started.append(copy(t, 1 + j, me, (*chip, c), src=ins[t]))
                started[-1].start()
        for t in range(n):
            started.append(copy(t, 0, me, sibling, src=ins[t]))
            started[-1].start()
        for j, chip in enumerate(chips):
            for t in range(n):
                copy(t, 1 + j, (*chip, c), me).wait_recv()
                started.append(copy(t, 4 + j, (*chip, c), sibling))
                started[-1].start()
        for t in range(n):
            copy(t, 0, sibling, me).wait_recv()
        for j, chip in enumerate(chips):
            for t in range(n):
                copy(t, 4 + j, (*chip, 1 - c), me).wait_recv()
        for cp in started:
            cp.wait_send()
        for cp in mine:
            cp.wait()

    return pl.pallas_call(
        body,
        name=name,
        in_specs=[_ANY] * n,
        out_specs=[_ANY] * n,
        out_shape=[jax.ShapeDtypeStruct((N_DEV, *s.shape), s.dtype) for s in shards],
        scratch_shapes=[pltpu.SemaphoreType.DMA((n, 7)), pltpu.SemaphoreType.DMA((n, 7)), pltpu.SemaphoreType.DMA((n,))],
        compiler_params=pltpu.CompilerParams(has_side_effects=True),
    )(*shards)


def _peers(x, y, c):
    out = []
    for mask in range(1, N_DEV):
        fx, fy, fc = (mask >> 2) & 1, (mask >> 1) & 1, mask & 1
        out.append((1 - x if fx else x, 1 - y if fy else y, 1 - c if fc else c))
    return out


def _exchange(parts, *, name):
    n = len(parts)

    def body(*refs):
        ins, outs = refs[:n], refs[n:2 * n]
        send_sems, recv_sems, local_sems = refs[2 * n:]
        x, y, c = _position()
        me = _dev_index(x, y, c)
        peers = _peers(x, y, c)

        def copy(t, k):
            peer = peers[k]
            return pltpu.make_async_remote_copy(
                src_ref=ins[t].at[_dev_index(*peer)], dst_ref=outs[t].at[me], send_sem=send_sems.at[t, k],
                recv_sem=recv_sems.at[t, k], device_id=peer, device_id_type=MESH)

        def arrival(t, k):
            peer = peers[k]
            dst = outs[t].at[_dev_index(*peer)]
            return pltpu.make_async_remote_copy(
                src_ref=dst, dst_ref=dst, send_sem=send_sems.at[t, k], recv_sem=recv_sems.at[t, k],
                device_id=peer, device_id_type=MESH)

        mine = [pltpu.make_async_copy(ins[t].at[me], outs[t].at[me], local_sems.at[t]) for t in range(n)]
        for cp in mine:
            cp.start()
        started = [copy(t, k) for k in range(N_DEV - 1) for t in range(n)]
        for cp in started:
            cp.start()
        for k in range(N_DEV - 1):
            for t in range(n):
                arrival(t, k).wait_recv()
        for cp in started:
            cp.wait_send()
        for cp in mine:
            cp.wait()

    return pl.pallas_call(
        body,
        name=name,
        in_specs=[_ANY] * n,
        out_specs=[_ANY] * n,
        out_shape=[jax.ShapeDtypeStruct(p.shape, p.dtype) for p in parts],
        scratch_shapes=[pltpu.SemaphoreType.DMA((n, 7)), pltpu.SemaphoreType.DMA((n, 7)), pltpu.SemaphoreType.DMA((n,))],
        compiler_params=pltpu.CompilerParams(has_side_effects=True),
    )(*parts)


_HBM = pl.BlockSpec(memory_space=pltpu.HBM)
_SEM = pl.BlockSpec(memory_space=pltpu.SEMAPHORE)


def _own_block_placed(block, like):
    x, y, c = _position()
    return lax.dynamic_update_index_in_dim(lax.empty(like.shape, like.dtype), block, _dev_index(x, y, c), 0)


def _direct_copies(srcs, lands, send_sems, recv_sems, blocked, arrivals):
    x, y, c = _position()
    me = _dev_index(x, y, c)
    copies = []
    for k, peer in enumerate(_peers(x, y, c)):
        p = _dev_index(*peer)
        for t in range(len(srcs)):
            sems = dict(send_sem=send_sems.at[7 * t + k], recv_sem=recv_sems.at[7 * t + k], device_id=peer, device_id_type=MESH)
            if arrivals:
                copies.append(pltpu.make_async_remote_copy(src_ref=lands[t].at[p], dst_ref=lands[t].at[p], **sems))
            else:
                copies.append(pltpu.make_async_remote_copy(
                    src_ref=srcs[t].at[p] if blocked else srcs[t], dst_ref=lands[t].at[me], **sems))
    return copies


def _exchange_start(srcs, own_blocks, *, blocked, name):
    n = len(srcs)
    shape_of = lambda s: s.shape if blocked else (N_DEV, *s.shape)
    lands = [_own_block_placed(own, jax.ShapeDtypeStruct(shape_of(s), s.dtype)) for s, own in zip(srcs, own_blocks)]

    def body(*refs):
        src_refs, land_refs = refs[:n], refs[n:2 * n]
        send_sems, recv_sems = refs[2 * n:2 * n + 2]
        token = refs[-1]
        for cp in _direct_copies(src_refs, land_refs, send_sems, recv_sems, blocked, arrivals=False):
            cp.start()
        token[...] = jnp.zeros_like(token)

    hbm = lambda a: pltpu.HBM(a.shape, a.dtype)
    outs = pl.pallas_call(
        body,
        name=name,
        in_specs=[_HBM] * (2 * n),
        out_specs=[_SEM, _SEM] + [_HBM] * (2 * n) + [pl.BlockSpec(memory_space=pltpu.VMEM)],
        out_shape=[pltpu.SemaphoreType.DMA((7 * n,)), pltpu.SemaphoreType.DMA((7 * n,))] + [hbm(a) for a in srcs] + [hbm(a) for a in lands]
        + [jax.ShapeDtypeStruct(_TOKEN, F32)],
        input_output_aliases={i: 2 + i for i in range(2 * n)},
        compiler_params=pltpu.CompilerParams(has_side_effects=pltpu.SideEffectType.DATAFLOW_SIDE_EFFECTING),
    )(*[pltpu.with_memory_space_constraint(a, pltpu.HBM) for a in (*srcs, *lands)])
    return (outs[0], outs[1], outs[2:2 + n], outs[2 + n:2 + 2 * n], blocked), outs[-1]


def _exchange_wait(state, after, *, name):
    send_sems, recv_sems, srcs, lands, blocked = state
    n = len(srcs)

    def body(*refs):
        src_refs, land_refs = refs[:n], refs[n:2 * n]
        send_refs, recv_refs = refs[2 * n:2 * n + 2]
        for cp in _direct_copies(src_refs, land_refs, send_refs, recv_refs, blocked, arrivals=False):
            cp.wait_send()
        for cp in _direct_copies(src_refs, land_refs, send_refs, recv_refs, blocked, arrivals=True):
            cp.wait_recv()

    hbm = lambda a: pltpu.HBM(a.shape, a.dtype)
    outs = pl.pallas_call(
        body,
        name=name,
        in_specs=[_HBM] * (2 * n) + [_SEM, _SEM, _ANY],
        out_specs=[_HBM] * (2 * n),
        out_shape=[hbm(a) for a in srcs] + [hbm(a) for a in lands],
        input_output_aliases={i: i for i in range(2 * n)},
        compiler_params=pltpu.CompilerParams(has_side_effects=pltpu.SideEffectType.DATAFLOW_SIDE_EFFECTING),
    )(*srcs, *lands, send_sems, recv_sems, after)
    return outs[n:]


def _all_reduce(g_local, *, name):
    R, C = g_local.shape

    def body(g_ref, o_ref, buf, send_sems, recv_sems):
        x, y, c = _position()
        me = _dev_index(x, y, c)
        peers = _peers(x, y, c)
        copies = []
        for k, peer in enumerate(peers):
            copies.append(pltpu.make_async_remote_copy(
                src_ref=g_ref, dst_ref=buf.at[me], send_sem=send_sems.at[k], recv_sem=recv_sems.at[k],
                device_id=peer, device_id_type=MESH))
            copies[-1].start()
        buf[me] = g_ref[...]
        for k, peer in enumerate(peers):
            dst = buf.at[_dev_index(*peer)]
            pltpu.make_async_remote_copy(
                src_ref=dst, dst_ref=dst, send_sem=send_sems.at[k], recv_sem=recv_sems.at[k],
                device_id=peer, device_id_type=MESH).wait_recv()
        for cp in copies:
            cp.wait_send()
        g = buf[0]
        for dev in range(1, N_DEV):
            g = g + buf[dev]
        o_ref[...] = g

    vm = pl.BlockSpec(memory_space=pltpu.VMEM)
    return pl.pallas_call(
        body,
        name=name,
        in_specs=[vm],
        out_specs=vm,
        out_shape=jax.ShapeDtypeStruct((R, C), F32),
        scratch_shapes=[pltpu.VMEM((N_DEV, R, C), F32), pltpu.SemaphoreType.DMA((7,)), pltpu.SemaphoreType.DMA((7,))],
        compiler_params=pltpu.CompilerParams(has_side_effects=True, vmem_limit_bytes=VMEM_LIMIT),
    )(g_local)


def _block_diag(w):
    out = jnp.zeros((POOL_W, POOL_W), w.dtype)
    for gi in range(4):
        out = out.at[64 * gi:64 * (gi + 1), 64 * gi:64 * (gi + 1)].set(w[gi])
    return out


def _layer_consts(sp, l):
    causal = jnp.tril(jnp.ones((SGU_CHUNK, SGU_CHUNK), F32))
    wm = (sp["sgu_w"][l] * causal[None]).astype(BF16)
    wbd = _block_diag(sp["pool_w"][l]).astype(BF16)
    return dict(
        wbd=wbd, wbd_t=wbd.T, wm=wm, wm_t=wm.transpose(0, 2, 1),
        sgu_bias=jnp.repeat(sp["sgu_b"][l].T, 64, axis=1),
        bpad=jnp.pad(sp["b_forget"][l], (0, F_LANES - FOX_H)).reshape(1, F_LANES),
        bg=sp["b_gate"][l].reshape(1, 3 * D),
    )


def _relu2(acc):
    return acc, jnp.square(jnp.maximum(acc, 0.0))


def _relu2_grad(acc, z):
    return (acc * 2.0 * jnp.maximum(z, 0.0),)


def _layer_fwd(l, x, mem, W, sp, after):
    S = x.shape[0]
    t = _tile(S, 256)
    c = _layer_consts(sp, l)
    n = f"l{l}_"
    h = _rms_fwd(x, sp["norm_mix_g"][l], after=after, name=n + "norm_mix")
    qkv = _mm(h, W["qkv"], out_dtypes=(BF16,), name=n + "qkv")
    rest = _mm(h, W["rest"], name=n + "rest")
    pa = _pool_fwd(rest, c["wbd"], sp["pool_scale"][l], name=n + "pool")
    cum, cum_t = _fox_prep(rest, c["bpad"], name=n + "fox_prep")
    fk3 = cum_t[:FOX_H].reshape(FOX_H, S // t, t)
    o, lse = _fox_fwd(qkv, cum, fk3, name=n + "fox")
    sg = _sgu_fwd(rest, sp["sgu_norm_g"][l], c["wm"], c["sgu_bias"], name=n + "sgu")
    ya = _mm(pa, W["ba"], name=n + "branch_a")
    yb = _mm(o, W["bb"], name=n + "branch_b")
    yc = _mm(sg, W["bc"], name=n + "branch_c")
    merged = _merge_fwd(rest, c["bg"], ya, yb, yc, name=n + "merge")
    x1 = _mm(merged, W["out"], extras=(x,), epilogue=_add, name=n + "out")
    hx = _rms_fwd(x1, sp["norm_xattn_g"][l], name=n + "norm_xattn")
    hm = _rms_fwd(mem, sp["norm_mem_g"][l], name=n + "norm_mem")
    xq = _mm(hx, W["xq"], out_dtypes=(BF16,), name=n + "xq")
    kv = _mm(hm, W["xkv"], out_dtypes=(BF16,), name=n + "xkv")
    o2 = _xattn_fwd(xq, kv, name=n + "xattn")
    x2 = _mm(o2, W["xo"], extras=(x1,), epilogue=_add, name=n + "xo")
    hf = _rms_fwd(x2, sp["norm_ffn_g"][l], name=n + "norm_ffn")
    z, act = _mm(hf, W["ff1"], epilogue=_relu2, out_dtypes=(F32, BF16), name=n + "ff1")
    x3 = _mm(act, W["ff2"], extras=(x2,), epilogue=_add, name=n + "ff2")
    saved = dict(x=x, h=h, qkv=qkv, rest=rest, pa=pa, cum=cum, fk3=fk3, o=o, lse=lse, sg=sg, ya=ya, yb=yb, yc=yc,
                 merged=merged, x1=x1, hx=hx, hm=hm, xq=xq, kv=kv, o2=o2, x2=x2, hf=hf, z=z, act=act, c=c)
    return x3, saved


def _layer_bwd(l, dx3, sv, mem, W, sp, after, grads_done):
    S = dx3.shape[0]
    c = sv["c"]
    n = f"l{l}b_"
    bf = dict(out_dtypes=(BF16,))
    gw, gs = {}, {}
    gw["ff2"] = _mm(sv["act"], dx3, ta=True, name=n + "dw_ff2", **bf)
    dz = _mm(dx3, W["ff2"], tb=True, extras=(sv["z"],), epilogue=_relu2_grad, after=after, name=n + "dz", **bf)
    gw["ff1"] = _mm(sv["hf"], dz, ta=True, shard_out=True, name=n + "dw_ff1", **bf)
    dhf = _mm(dz, W["ff1"], tb=True, name=n + "dhf")
    dx2, gs["norm_ffn_g"] = _rms_bwd(sv["x2"], sp["norm_ffn_g"][l], dhf, dx3, name=n + "dnorm_ffn")
    gw["xo"] = _mm(sv["o2"], dx2, ta=True, name=n + "dw_xo", **bf)
    do2 = _mm(dx2, W["xo"], tb=True, name=n + "do2", **bf)
    dxq, dkv = _xattn_bwd(sv["xq"], sv["kv"], do2, name=n + "dxattn")
    gw["xq"] = _mm(sv["hx"], dxq, ta=True, name=n + "dw_xq", **bf)
    gw["xkv"] = _mm(sv["hm"], dkv, ta=True, shard_out=True, name=n + "dw_xkv", **bf)
    dhm = _mm(dkv, W["xkv"], tb=True, name=n + "dhm")
    _, gs["norm_mem_g"] = _rms_bwd(mem, sp["norm_mem_g"][l], dhm, jnp.zeros_like(mem), name=n + "dnorm_mem")
    dhx = _mm(dxq, W["xq"], tb=True, name=n + "dhx")
    dx1, gs["norm_xattn_g"] = _rms_bwd(sv["x1"], sp["norm_xattn_g"][l], dhx, dx2, name=n + "dnorm_xattn")
    after, gw = grads_done(l, gw), {}
    gw["out"] = _mm(sv["merged"], dx1, ta=True, name=n + "dw_out", **bf)
    dm = _mm(dx1, W["out"], tb=True, after=after, name=n + "dmerged")
    dya, dyb, dyc, dg1, dg2, dg3, db1, db2, db3 = _merge_bwd(sv["rest"], c["bg"], sv["ya"], sv["yb"], sv["yc"], dm, name=n + "dmerge")
    gs["b_gate"] = jnp.concatenate([db1, db2, db3], axis=1).reshape(3 * D)
    gw["ba"] = _mm(sv["pa"], dya, ta=True, shard_out=True, name=n + "dw_ba", **bf)
    gw["bb"] = _mm(sv["o"], dyb, ta=True, shard_out=True, name=n + "dw_bb", **bf)
    gw["bc"] = _mm(sv["sg"], dyc, ta=True, shard_out=True, name=n + "dw_bc", **bf)
    dpa = _mm(dya, W["ba"], tb=True, name=n + "dpa")
    do = _mm(dyb, W["bb"], tb=True, name=n + "do", **bf)
    dsg = _mm(dyc, W["bc"], tb=True, name=n + "dsg")
    da, dwbd, dscale = _pool_bwd(sv["rest"], c["wbd"], c["wbd_t"], sp["pool_scale"][l], dpa, name=n + "dpool")
    gs["pool_w"] = jnp.stack([dwbd[64 * gi:64 * (gi + 1), 64 * gi:64 * (gi + 1)] for gi in range(4)])
    gs["pool_scale"] = dscale.reshape(POOL_W)
    dq, dk, dv, dfq, dfk = _fox_bwd(sv["qkv"], sv["cum"], sv["fk3"], sv["o"], do, sv["lse"], name=n + "dfox")
    dcum = dfq[:, :, :2].transpose(1, 0, 2).reshape(S, FOX_H) + dfk.reshape(FOX_H, S).T
    df, dbf = _fox_post(sv["rest"], c["bpad"], jnp.pad(dcum, ((0, 0), (0, F_LANES - FOX_H))), name=n + "dfox_post")
    gs["b_forget"] = dbf[0, :FOX_H]
    dc, dwm, dbias, dgn = _sgu_bwd(sv["rest"], sp["sgu_norm_g"][l], c["wm"], c["wm_t"], c["sgu_bias"], dsg, name=n + "dsgu")
    gs["sgu_w"] = dwm * jnp.tril(jnp.ones((SGU_CHUNK, SGU_CHUNK), F32))[None]
    gs["sgu_b"] = dbias.reshape(SGU_CHUNK, 4, 64).sum(axis=2).T
    gs["sgu_norm_g"] = dgn.reshape(SGU_W)
    dqkv = jnp.concatenate([dq, dk, dv], axis=1)
    drest = jnp.concatenate([da, df, jnp.zeros((S, OFF_C - OFF_F - F_LANES), BF16), dc, dg1, dg2, dg3], axis=1)
    gw["qkv"] = _mm(sv["h"], dqkv, ta=True, name=n + "dw_qkv", **bf)
    gw["rest"] = _mm(sv["h"], drest, ta=True, name=n + "dw_rest", **bf)
    dh = _mm(dqkv, W["qkv"], tb=True, name=n + "dh_qkv")
    dh = _mm(drest, W["rest"], tb=True, extras=(dh,), epilogue=_add, name=n + "dh")
    dx, gs["norm_mix_g"] = _rms_bwd(sv["x"], sp["norm_mix_g"][l], dh, dx1, name=n + "dnorm_mix")
    return dx, gs, grads_done(l, gw)


def _local_step(x, mem, target, sp, weights, grads_done):
    saved, Ws = [], []
    for l in range(DEPTH):
        W, after = weights(l, x)
        Ws.append(W)
        x, sv = _layer_fwd(l, x, mem, W, sp, after)
        saved.append(sv)
    loss, dx, dgf = _final_loss(x, sp["final_norm_g"], target, name="final_loss")
    gss, after = [None] * DEPTH, None
    for l in reversed(range(DEPTH)):
        dx, gss[l], after = _layer_bwd(l, dx, saved[l], mem, Ws[l], sp, after, grads_done)
    small = {k: jnp.stack([gss[l][k] for l in range(DEPTH)]) for k in gss[0]}
    small["final_norm_g"] = dgf
    return loss, dx, small


_SMALL = ["norm_mix_g", "b_forget", "pool_w", "pool_scale", "sgu_norm_g", "sgu_w", "sgu_b", "b_gate", "norm_xattn_g",
          "norm_mem_g", "norm_ffn_g", "final_norm_g"]
_COL = {"w_branch_a": "ba", "w_branch_b": "bb", "w_branch_c": "bc", "w_xkv": "xkv", "w_ff1": "ff1"}
_ROW = {"w_out": "out", "w_xq": "xq", "w_xo": "xo", "w_ff2": "ff2"}
_BIG = ["w_in", "w_branch_a", "w_branch_b", "w_branch_c", "w_out", "w_xq", "w_xkv", "w_xo", "w_ff1", "w_ff2"]
_PACK_LANES = 128


def _as_rows(a):
    return a.reshape(-1, a.shape[-1])


def _pack(tensors):
    rows = []
    for a in tensors:
        flat = a.reshape(-1)
        flat = jnp.pad(flat, (0, (-flat.shape[0]) % (8 * _PACK_LANES)))
        rows.append(flat.reshape(-1, _PACK_LANES))
    return jnp.concatenate(rows, axis=0)


def _unpack(packed, like):
    out, r = [], 0
    for a in like:
        size = math.prod(a.shape)
        nr = 8 * (-(-size // (8 * _PACK_LANES)))
        out.append(packed[r:r + nr].reshape(-1)[:size].reshape(a.shape))
        r += nr
    return out


_SHARD_IN = N_IN // N_DEV


def _columns(pieces, start, stop):
    out, at = [], 0
    for p in pieces:
        lo, hi = max(start, at), min(stop, at + p.shape[1])
        if lo < hi:
            out.append(p[:, lo - at:hi - at])
        at += p.shape[1]
    return out


def _split_w_in(blocks):
    K = blocks[0].shape[0]
    pad = jnp.zeros((K, OFF_C - OFF_F - FOX_H), blocks[0].dtype)
    cols = functools.partial(_columns, blocks)
    rest = jnp.concatenate(cols(0, R_OFF_Q) + cols(R_OFF_F, R_OFF_C) + [pad] + cols(R_OFF_C, N_IN), axis=1)
    return jnp.concatenate(cols(R_OFF_Q, R_OFF_F), axis=1), rest


def _join_w_in(qkv, rest):
    in_order = [rest[:, :R_OFF_Q], qkv, rest[:, OFF_F:OFF_F + FOX_H], rest[:, OFF_C:]]
    return jnp.stack([jnp.concatenate(_columns(in_order, _SHARD_IN * d, _SHARD_IN * (d + 1)), axis=1) for d in range(N_DEV)])


def _layer_weights(gathered):
    W = dict(zip(("qkv", "rest"), _split_w_in([gathered["w_in"][d] for d in range(N_DEV)])))
    for name, key in _COL.items():
        W[key] = _Gathered(gathered[name])
    for name, key in _ROW.items():
        W[key] = gathered[name].reshape(-1, gathered[name].shape[-1])
    return W


def _grad_blocks(gw):
    parts = {}
    if "qkv" in gw:
        parts["w_in"] = _join_w_in(gw["qkv"], gw["rest"])
    for name, key in _COL.items():
        if key in gw:
            parts[name] = gw[key]
    for name, key in _ROW.items():
        if key in gw:
            parts[name] = gw[key].reshape(N_DEV, -1, gw[key].shape[-1])
    return parts


def kernel(x, mem, norm_mix_g, w_in, b_forget, pool_w, pool_scale, sgu_norm_g, sgu_w, sgu_b, w_branch_a, w_branch_b, w_branch_c, b_gate, w_out, norm_xattn_g, norm_mem_g, w_xq, w_xkv, w_xo, norm_ffn_g, w_ff1, w_ff2, final_norm_g, loss_target, m_norm_mix_g, m_w_in, m_b_forget, m_pool_w, m_pool_scale, m_sgu_norm_g, m_sgu_w, m_sgu_b, m_w_branch_a, m_w_branch_b, m_w_branch_c, m_b_gate, m_w_out, m_norm_xattn_g, m_norm_mem_g, m_w_xq, m_w_xkv, m_w_xo, m_norm_ffn_g, m_w_ff1, m_w_ff2, m_final_norm_g, v_norm_mix_g, v_w_in, v_b_forget, v_pool_w, v_pool_scale, v_sgu_norm_g, v_sgu_w, v_sgu_b, v_w_branch_a, v_w_branch_b, v_w_branch_c, v_b_gate, v_w_out, v_norm_xattn_g, v_norm_mem_g, v_w_xq, v_w_xkv, v_w_xo, v_norm_ffn_g, v_w_ff1, v_w_ff2, v_final_norm_g):
    names = ["norm_mix_g", "w_in", "b_forget", "pool_w", "pool_scale", "sgu_norm_g", "sgu_w", "sgu_b", "w_branch_a", "w_branch_b",
             "w_branch_c", "b_gate", "w_out", "norm_xattn_g", "norm_mem_g", "w_xq", "w_xkv", "w_xo", "norm_ffn_g", "w_ff1", "w_ff2",
             "final_norm_g"]
    w = dict(zip(names, [norm_mix_g, w_in, b_forget, pool_w, pool_scale, sgu_norm_g, sgu_w, sgu_b, w_branch_a, w_branch_b, w_branch_c,
                         b_gate, w_out, norm_xattn_g, norm_mem_g, w_xq, w_xkv, w_xo, norm_ffn_g, w_ff1, w_ff2, final_norm_g]))
    m = dict(zip(names, [m_norm_mix_g, m_w_in, m_b_forget, m_pool_w, m_pool_scale, m_sgu_norm_g, m_sgu_w, m_sgu_b, m_w_branch_a,
                         m_w_branch_b, m_w_branch_c, m_b_gate, m_w_out, m_norm_xattn_g, m_norm_mem_g, m_w_xq, m_w_xkv, m_w_xo,
                         m_norm_ffn_g, m_w_ff1, m_w_ff2, m_final_norm_g]))
    v = dict(zip(names, [v_norm_mix_g, v_w_in, v_b_forget, v_pool_w, v_pool_scale, v_sgu_norm_g, v_sgu_w, v_sgu_b, v_w_branch_a,
                         v_w_branch_b, v_w_branch_c, v_b_gate, v_w_out, v_norm_xattn_g, v_norm_mem_g, v_w_xq, v_w_xkv, v_w_xo,
                         v_norm_ffn_g, v_w_ff1, v_w_ff2, v_final_norm_g]))

    sp = {k: w[k] for k in _SMALL}
    shards = [[w[k][l].astype(BF16) for k in _BIG] for l in range(DEPTH)]
    me = _dev_index(*_position())
    pending = {}

    def weights(l, x_in):
        if l == 0:
            gathered = _all_gather(shards[0], name="gather_l0")
            pending["gather"], token = _exchange_start(shards[1], shards[1], blocked=False, name="gather_l1_start")
        else:
            gathered, token = _exchange_wait(pending.pop("gather"), x_in, name="gather_l1_wait"), None
        return _layer_weights(dict(zip(_BIG, gathered))), token

    received = [{} for _ in range(DEPTH)]
    travelling = []

    def grads_done(l, gw):
        blocks = _grad_blocks(gw)
        keys = [k for k in _BIG if k in blocks]
        parts = [blocks[k] for k in keys]
        group = f"exchange_grads_l{l}_" + ("rest" if "w_in" in blocks else "mlp")
        if l == 0 and "w_in" in blocks:
            received[l].update(zip(keys, _exchange(parts, name=group)))
            return None
        own = [lax.dynamic_index_in_dim(p, me, 0, keepdims=False) for p in parts]
        state, token = _exchange_start(parts, own, blocked=True, name=group + "_start")
        travelling.append((l, keys, state, group + "_wait"))
        return token

    loss, dx, small = _local_step(x[0], mem[0], loss_target[0], sp, weights, grads_done)
    loss = lax.psum(loss[0, 0], ("x", "y", "c"))
    for l, keys, state, wait_name in travelling:
        received[l].update(zip(keys, _exchange_wait(state, dx, name=wait_name)))

    grads, deltas, new_m, new_v = {}, {}, {}, {}
    for k in _BIG:
        outs = _adamw_sharded([received[l][k] for l in range(DEPTH)], w[k], m[k], v[k], name="adamw_" + k)
        grads[k], deltas[k], new_m[k], new_v[k] = outs
    like = [w[k] for k in _SMALL]
    g_small = _unpack(_all_reduce(_pack([small[k] for k in _SMALL]), name="all_reduce_small"), like)
    rows = lambda d: [_as_rows(d[k]) for k in _SMALL]
    outs = _adamw_small([_as_rows(g) for g in g_small], rows(w), rows(m), rows(v), name="adamw_small")
    grads.update(zip(_SMALL, g_small))
    for dst, vals in zip((deltas, new_m, new_v), outs):
        dst.update({k: a.reshape(w[k].shape) for k, a in zip(_SMALL, vals)})

    return (loss, dx[None], *[grads[k] for k in names], *[deltas[k] for k in names], *[new_m[k] for k in names],
            *[new_v[k] for k in names])
```

```python
import functools
import math

import jax
import jax.numpy as jnp
from jax import lax
from jax.experimental import pallas as pl
from jax.experimental.pallas import tpu as pltpu

F32 = jnp.float32
BF16 = jnp.bfloat16
MESH = pl.DeviceIdType.MESH

N_DEV = 8
D = 1024
DEPTH = 2
EPS = 1e-6
NEG = -1e30
POOL_W = 256
FOX_H = 8
FOX_DH = 64
FOX_W = 512
SGU_W = 256
SGU_CHUNK = 128
XH = 4
XDH = 256
N_IN = 5384
R_OFF_Q, R_OFF_F, R_OFF_C = 256, 1792, 1800
QKV_W = 3 * FOX_W
OFF_A, OFF_F, OFF_C, OFF_G, REST_W = 0, 256, 512, 1024, 4096
F_LANES = 128

ADAM_LR = 0.001
ADAM_B1 = 0.9
ADAM_B2 = 0.999
ADAM_EPS = 1e-08
ADAM_WD = 0.01
ADAM_STEP = 10

VMEM_LIMIT = 56 * 1024 * 1024


def _tile(n, pref):
    t = min(n, pref)
    while n % t:
        t -= 128
    assert t > 0, (n, pref)
    return t


def _params(sem=None):
    return pltpu.CompilerParams(dimension_semantics=sem, vmem_limit_bytes=VMEM_LIMIT)


def _dot(a, b, ca, cb):
    return lax.dot_general(a, b, (((ca,), (cb,)), ((), ())), preferred_element_type=F32)


def _sigmoid(z):
    return 1.0 / (1.0 + jnp.exp(-z))


_GELU_K = math.sqrt(2.0 / math.pi)
_GELU_C = 0.044715


def _gelu(x):
    return 0.5 * x * (1.0 + jnp.tanh(_GELU_K * (x + _GELU_C * x * x * x)))


def _gelu_grad(x):
    t = jnp.tanh(_GELU_K * (x + _GELU_C * x * x * x))
    return 0.5 * (1.0 + t) + 0.5 * x * (1.0 - t * t) * _GELU_K * (1.0 + 3.0 * _GELU_C * x * x)


def _rows(shape):
    return lax.broadcasted_iota(jnp.int32, shape, 0)


def _lanes(shape):
    return lax.broadcasted_iota(jnp.int32, shape, 1)


class _Gathered:
    def __init__(self, arr):
        self.arr = arr
        self.shape = (arr.shape[1], N_DEV * arr.shape[2])


_TOKEN = (8, 128)


def _mm(a, b, *, ta=False, tb=False, extras=(), epilogue=None, out_dtypes=(F32,), shard_out=False, after=None, tm=None, tn=512, tk=None,
        name):
    M, K = (a.shape[1], a.shape[0]) if ta else a.shape
    N, Kb = b.shape if tb else b.shape[::-1]
    assert Kb == K, (a.shape, b.shape, ta, tb)
    gathered = isinstance(b, _Gathered)
    if gathered:
        if tb:
            tk = b.arr.shape[2]
        else:
            tn = b.arr.shape[2]
    if shard_out:
        tn = N // N_DEV
    tm = _tile(M, tm or (1024 if ta else 2048))
    tn = _tile(N, tn)
    tk = _tile(K, tk or (2048 if ta else 1024))
    nk = K // tk
    ca, cb = (0 if ta else 1), (1 if tb else 0)
    n_ex, n_out = len(extras), len(out_dtypes)
    tokens = [] if after is None else [after]
    n_in = 2 + n_ex + len(tokens)
    if epilogue is None:
        epilogue = lambda acc: (acc,)

    def body(*refs):
        a_ref, b_ref = refs[:2]
        ex_refs = refs[2:2 + n_ex]
        o_refs = refs[n_in:n_in + n_out]
        part = _dot(a_ref[...].astype(BF16), b_ref[...].astype(BF16), ca, cb)

        def finish(acc):
            for o_ref, val in zip(o_refs, epilogue(acc, *[e[...] for e in ex_refs])):
                o_ref[...] = val.astype(o_ref.dtype)

        if nk == 1:
            finish(part)
        else:
            acc_ref = refs[-1]
            k = pl.program_id(2)

            @pl.when(k == 0)
            def _():
                acc_ref[...] = part

            @pl.when(k > 0)
            def _():
                acc_ref[...] += part

            @pl.when(k == nk - 1)
            def _():
                finish(acc_ref[...])

    a_spec = pl.BlockSpec((tk, tm), lambda i, j, k: (k, i)) if ta else pl.BlockSpec((tm, tk), lambda i, j, k: (i, k))
    if not gathered:
        b_arr = b
        b_spec = pl.BlockSpec((tn, tk), lambda i, j, k: (j, k)) if tb else pl.BlockSpec((tk, tn), lambda i, j, k: (k, j))
    else:
        b_arr = b.arr
        if tb:
            b_spec = pl.BlockSpec((None, tn, tk), lambda i, j, k: (k, j, 0))
        else:
            b_spec = pl.BlockSpec((None, tk, tn), lambda i, j, k: (j, k, 0))
    tile = pl.BlockSpec((tm, tn), lambda i, j, k: (i, j))
    if shard_out:
        out_specs = [pl.BlockSpec((None, tm, tn), lambda i, j, k: (j, i, 0))] * n_out
        out_shape = [jax.ShapeDtypeStruct((N_DEV, M, tn), dt) for dt in out_dtypes]
    else:
        out_specs = [tile] * n_out
        out_shape = [jax.ShapeDtypeStruct((M, N), dt) for dt in out_dtypes]
    size = lambda dt: jnp.dtype(dt).itemsize
    vmem = 2 * (tm * tk * size(a.dtype) + tk * tn * size(b_arr.dtype)
                + tm * tn * (sum(size(e.dtype) for e in extras) + sum(map(size, out_dtypes))))
    vmem += tm * tn * 4 * (nk > 1)
    assert vmem <= VMEM_LIMIT - (4 << 20), (name, vmem)
    outs = pl.pallas_call(
        body,
        name=name,
        grid=(M // tm, N // tn, nk),
        in_specs=[a_spec, b_spec] + [tile] * n_ex + [pl.BlockSpec(_TOKEN, lambda i, j, k: (0, 0))] * len(tokens),
        out_specs=out_specs,
        out_shape=out_shape,
        scratch_shapes=[pltpu.VMEM((tm, tn), F32)] if nk > 1 else [],
        compiler_params=_params(("parallel", "parallel", "arbitrary")),
    )(a, b_arr, *extras, *tokens)
    return outs[0] if n_out == 1 else outs


def _add(acc, res):
    return (acc + res,)


def _rms_fwd(x, g, *, after=None, name):
    R, C = x.shape
    tm = _tile(R, 256)
    tokens = [] if after is None else [after]

    def body(x_ref, g_ref, *rest):
        xv = x_ref[...]
        r = lax.rsqrt(jnp.mean(xv * xv, axis=-1, keepdims=True) + EPS)
        rest[-1][...] = (xv * r * g_ref[...]).astype(BF16)

    return pl.pallas_call(
        body,
        name=name,
        grid=(R // tm,),
        in_specs=[pl.BlockSpec((tm, C), lambda i: (i, 0)), pl.BlockSpec((1, C), lambda i: (0, 0))]
        + [pl.BlockSpec(_TOKEN, lambda i: (0, 0))] * len(tokens),
        out_specs=pl.BlockSpec((tm, C), lambda i: (i, 0)),
        out_shape=jax.ShapeDtypeStruct((R, C), BF16),
        compiler_params=_params(("parallel",)),
    )(x, g.reshape(1, C), *tokens)


def _rms_bwd(x, g, dh, dres, *, name):
    R, C = x.shape
    tm = _tile(R, 256)

    def body(x_ref, g_ref, dh_ref, dres_ref, dx_ref, dg_ref):
        xv = x_ref[...]
        r = lax.rsqrt(jnp.mean(xv * xv, axis=-1, keepdims=True) + EPS)
        xn = xv * r
        dh_v = dh_ref[...].astype(F32)
        dxn = dh_v * g_ref[...]
        dx_ref[...] = r * (dxn - xn * jnp.mean(dxn * xn, axis=-1, keepdims=True)) + dres_ref[...]
        part = jnp.sum(dh_v * xn, axis=0, keepdims=True)

        @pl.when(pl.program_id(0) == 0)
        def _():
            dg_ref[...] = part

        @pl.when(pl.program_id(0) > 0)
        def _():
            dg_ref[...] += part

    row = pl.BlockSpec((tm, C), lambda i: (i, 0))
    vec = pl.BlockSpec((1, C), lambda i: (0, 0))
    dx, dg = pl.pallas_call(
        body,
        name=name,
        grid=(R // tm,),
        in_specs=[row, vec, row, row],
        out_specs=[row, vec],
        out_shape=[jax.ShapeDtypeStruct((R, C), F32), jax.ShapeDtypeStruct((1, C), F32)],
        compiler_params=_params(("arbitrary",)),
    )(x, g.reshape(1, C), dh, dres)
    return dx, dg.reshape(C)


def _final_loss(x, g, target, *, name):
    R, C = x.shape
    tm = _tile(R, 256)

    def body(x_ref, g_ref, t_ref, loss_ref, dx_ref, dg_ref):
        xv = x_ref[...]
        r = lax.rsqrt(jnp.mean(xv * xv, axis=-1, keepdims=True) + EPS)
        xn = xv * r
        gv = g_ref[...]
        err = xn * gv - t_ref[...]
        lpart = (0.5 / C) * jnp.sum(jnp.sum(err * err, axis=1, keepdims=True), axis=0, keepdims=True)
        dy = err * (1.0 / C)
        dxn = dy * gv
        dx_ref[...] = r * (dxn - xn * jnp.mean(dxn * xn, axis=-1, keepdims=True))
        gpart = jnp.sum(dy * xn, axis=0, keepdims=True)

        @pl.when(pl.program_id(0) == 0)
        def _():
            loss_ref[...] = lpart
            dg_ref[...] = gpart

        @pl.when(pl.program_id(0) > 0)
        def _():
            loss_ref[...] += lpart
            dg_ref[...] += gpart

    row = pl.BlockSpec((tm, C), lambda i: (i, 0))
    vec = pl.BlockSpec((1, C), lambda i: (0, 0))
    loss, dx, dg = pl.pallas_call(
        body,
        name=name,
        grid=(R // tm,),
        in_specs=[row, vec, row],
        out_specs=[pl.BlockSpec((1, 1), lambda i: (0, 0)), row, vec],
        out_shape=[jax.ShapeDtypeStruct((1, 1), F32), jax.ShapeDtypeStruct((R, C), F32), jax.ShapeDtypeStruct((1, C), F32)],
        compiler_params=_params(("arbitrary",)),
    )(x, g.reshape(1, C), target)
    return loss, dx, dg.reshape(C)


def _pool_select(lane, vals):
    out = vals[3]
    for gi in (2, 1, 0):
        out = jnp.where(lane < 64 * (gi + 1), vals[gi], out)
    return out


def _pool_diff(a):
    row, lane = _rows(a.shape), _lanes(a.shape)

    def down(v, k):
        return jnp.where(row >= k, pltpu.roll(v, k, 0), 0.0)

    s2 = a + down(a, 1)
    s4 = s2 + down(s2, 2)
    s8 = s4 + down(s4, 4)
    s16 = s8 + down(s8, 8)
    wsum = _pool_select(lane, (s2, s4, s8, s16))
    win = _pool_select(lane, (2, 4, 8, 16))
    cnt = jnp.minimum(row + 1, win).astype(F32)
    return wsum / cnt - a, cnt


def _pool_diff_t(dd, cnt):
    S = dd.shape[0]
    row, lane = _rows(dd.shape), _lanes(dd.shape)

    def up(v, k):
        return jnp.where(row < S - k, pltpu.roll(v, S - k, 0), 0.0)

    e = dd / cnt
    s2 = e + up(e, 1)
    s4 = s2 + up(s2, 2)
    s8 = s4 + up(s4, 4)
    s16 = s8 + up(s8, 8)
    return _pool_select(lane, (s2, s4, s8, s16)) - dd


def _pool_fwd(rest, wbd, scale, *, name):
    S = rest.shape[0]

    def body(a_ref, w_ref, s_ref, o_ref):
        d, _ = _pool_diff(a_ref[...])
        yp = _dot(d.astype(BF16), w_ref[...], 1, 0)
        o_ref[...] = (yp * s_ref[...]).astype(BF16)

    return pl.pallas_call(
        body,
        name=name,
        grid=(1,),
        in_specs=[
            pl.BlockSpec((S, POOL_W), lambda i: (0, OFF_A // POOL_W)),
            pl.BlockSpec((POOL_W, POOL_W), lambda i: (0, 0)),
            pl.BlockSpec((1, POOL_W), lambda i: (0, 0)),
        ],
        out_specs=pl.BlockSpec((S, POOL_W), lambda i: (0, 0)),
        out_shape=jax.ShapeDtypeStruct((S, POOL_W), BF16),
        compiler_params=_params(("arbitrary",)),
    )(rest, wbd, scale.reshape(1, POOL_W))


def _pool_bwd(rest, wbd, wbd_t, scale, dpa, *, name):
    S = rest.shape[0]

    def body(a_ref, w_ref, wt_ref, s_ref, dpa_ref, da_ref, dw_ref, ds_ref):
        d, cnt = _pool_diff(a_ref[...])
        db = d.astype(BF16)
        yp = _dot(db, w_ref[...], 1, 0)
        dpa_v = dpa_ref[...]
        ds_ref[...] = jnp.sum(dpa_v * yp, axis=0, keepdims=True)
        dyp = (dpa_v * s_ref[...]).astype(BF16)
        dw_ref[...] = _dot(db, dyp, 0, 0)
        dd = _dot(dyp, wt_ref[...], 1, 0)
        da_ref[...] = _pool_diff_t(dd, cnt).astype(BF16)

    full = pl.BlockSpec((S, POOL_W), lambda i: (0, 0))
    sq = pl.BlockSpec((POOL_W, POOL_W), lambda i: (0, 0))
    vec = pl.BlockSpec((1, POOL_W), lambda i: (0, 0))
    return pl.pallas_call(
        body,
        name=name,
        grid=(1,),
        in_specs=[pl.BlockSpec((S, POOL_W), lambda i: (0, OFF_A // POOL_W)), sq, sq, vec, full],
        out_specs=[full, sq, vec],
        out_shape=[
            jax.ShapeDtypeStruct((S, POOL_W), BF16),
            jax.ShapeDtypeStruct((POOL_W, POOL_W), F32),
            jax.ShapeDtypeStruct((1, POOL_W), F32),
        ],
        compiler_params=_params(("arbitrary",)),
    )(rest, wbd, wbd_t, scale.reshape(1, POOL_W), dpa)


def _log_sigmoid(z):
    return jnp.minimum(z, 0.0) - jnp.log(1.0 + jnp.exp(-jnp.abs(z)))


_F_SPEC_COL = OFF_F // F_LANES


def _fox_prep(rest, bpad, *, name):
    S = rest.shape[0]

    def body(f_ref, b_ref, o_ref, ot_ref):
        acc = _log_sigmoid(f_ref[...] + b_ref[...])
        row = _rows(acc.shape)
        k = 1
        while k < S:
            acc = acc + jnp.where(row >= k, pltpu.roll(acc, k, 0), 0.0)
            k *= 2
        o_ref[...] = acc
        ot_ref[...] = acc.T

    return pl.pallas_call(
        body,
        name=name,
        grid=(1,),
        in_specs=[pl.BlockSpec((S, F_LANES), lambda i: (0, _F_SPEC_COL)), pl.BlockSpec((1, F_LANES), lambda i: (0, 0))],
        out_specs=[pl.BlockSpec((S, F_LANES), lambda i: (0, 0)), pl.BlockSpec((F_LANES, S), lambda i: (0, 0))],
        out_shape=[jax.ShapeDtypeStruct((S, F_LANES), F32), jax.ShapeDtypeStruct((F_LANES, S), F32)],
        compiler_params=_params(("arbitrary",)),
    )(rest, bpad)


def _fox_post(rest, bpad, dcum, *, name):
    S = rest.shape[0]

    def body(f_ref, b_ref, d_ref, df_ref, db_ref):
        acc = d_ref[...]
        row = _rows(acc.shape)
        k = 1
        while k < S:
            acc = acc + jnp.where(row < S - k, pltpu.roll(acc, S - k, 0), 0.0)
            k *= 2
        df = acc * (1.0 - _sigmoid(f_ref[...] + b_ref[...]))
        df_ref[...] = df.astype(BF16)
        db_ref[...] = jnp.sum(df, axis=0, keepdims=True)

    full = pl.BlockSpec((S, F_LANES), lambda i: (0, 0))
    vec = pl.BlockSpec((1, F_LANES), lambda i: (0, 0))
    return pl.pallas_call(
        body,
        name=name,
        grid=(1,),
        in_specs=[pl.BlockSpec((S, F_LANES), lambda i: (0, _F_SPEC_COL)), vec, full],
        out_specs=[full, vec],
        out_shape=[jax.ShapeDtypeStruct((S, F_LANES), BF16), jax.ShapeDtypeStruct((1, F_LANES), F32)],
        compiler_params=_params(("arbitrary",)),
    )(rest, bpad, dcum)


_FOX_SCALE = FOX_DH ** -0.5
_PAIRS = FOX_H // 2


def _scaled(v):
    return (v.astype(F32) * _FOX_SCALE).astype(BF16)


def _head_lane(cum, h):
    return jnp.sum(jnp.where(_lanes(cum.shape) == h, cum, 0.0), axis=-1, keepdims=True)


def _diag_mask(s):
    return jnp.where(_rows(s.shape) >= _lanes(s.shape), s, NEG)


def _fox_fwd(qkv, cum, fk3, *, name):
    S = qkv.shape[0]
    nk, t = fk3.shape[1:]

    def body(q_ref, k_ref, v_ref, cum_ref, fk_ref, o_ref, lse_ref, m_sc, l_sc, acc_sc):
        hp, i = pl.program_id(0), pl.program_id(1)
        lane = _lanes((t, 128))
        lo = lane < FOX_DH
        qs = _scaled(q_ref[...])
        zero = jnp.zeros_like(qs)
        qm = (jnp.where(lo, qs, zero), jnp.where(lo, zero, qs))
        cumv = cum_ref[...]
        fq = [_head_lane(cumv, 2 * hp + e) for e in range(2)]
        m_sc[...] = jnp.full(m_sc.shape, NEG, F32)
        l_sc[...] = jnp.zeros(l_sc.shape, F32)
        acc_sc[...] = jnp.zeros(acc_sc.shape, F32)

        def tile(j, masked):
            k0 = pl.multiple_of(j * t, t)
            kb = k_ref[pl.ds(k0, t), :]
            vb = v_ref[pl.ds(k0, t), :]
            alphas, pvs = [], []
            for e in range(2):
                s = _dot(qm[e], kb, 1, 1) + fq[e] - fk_ref[2 * hp + e, pl.ds(j, 1), :]
                if masked:
                    s = _diag_mask(s)
                m_old = m_sc[e]
                m_new = jnp.maximum(m_old, jnp.max(s, axis=-1, keepdims=True))
                p = jnp.exp(s - m_new)
                alpha = jnp.exp(m_old - m_new)
                l_sc[e] = alpha * l_sc[e] + jnp.sum(p, axis=-1, keepdims=True)
                m_sc[e] = m_new
                alphas.append(alpha)
                pvs.append(_dot(p.astype(BF16), vb, 1, 0))
            acc_sc[...] = jnp.where(lo, alphas[0], alphas[1]) * acc_sc[...] + jnp.where(lo, pvs[0], pvs[1])

        def step(j, carry):
            tile(j, False)
            return carry

        lax.fori_loop(0, i, step, 0)
        tile(i, True)
        o_ref[...] = acc_sc[...] / jnp.where(lo, l_sc[0], l_sc[1])
        lse = [m_sc[e] + jnp.log(l_sc[e]) for e in range(2)]
        lse_ref[...] = jnp.where(lane == 0, lse[0], jnp.where(lane == 1, lse[1], 0.0))

    return pl.pallas_call(
        body,
        name=name,
        grid=(_PAIRS, S // t),
        in_specs=[
            pl.BlockSpec((t, 128), lambda hp, i: (i, hp)),
            pl.BlockSpec((S, 128), lambda hp, i: (0, _PAIRS + hp)),
            pl.BlockSpec((S, 128), lambda hp, i: (0, 2 * _PAIRS + hp)),
            pl.BlockSpec((t, F_LANES), lambda hp, i: (i, 0)),
            pl.BlockSpec((FOX_H, nk, t), lambda hp, i: (0, 0, 0)),
        ],
        out_specs=[pl.BlockSpec((t, 128), lambda hp, i: (i, hp)), pl.BlockSpec((None, t, 128), lambda hp, i: (hp, i, 0))],
        out_shape=[jax.ShapeDtypeStruct((S, FOX_W), F32), jax.ShapeDtypeStruct((_PAIRS, S, 128), F32)],
        scratch_shapes=[pltpu.VMEM((2, t, 1), F32), pltpu.VMEM((2, t, 1), F32), pltpu.VMEM((t, 128), F32)],
        compiler_params=_params(("parallel", "arbitrary")),
    )(qkv, qkv, qkv, cum, fk3)


def _fox_bwd(qkv, cum, fk3, o, do, lse, *, name):
    S = qkv.shape[0]
    nk, t = fk3.shape[1:]

    def body(q_ref, k_ref, v_ref, cum_ref, fk_ref, o_ref, do_ref, lse_ref, dq_ref, dk_ref, dv_ref, dfq_ref, dfk_ref,
             qm_sc, km_sc, dom_sc, delta_sc, fq_sc, dfq_sc, dq_sc):
        hp = pl.program_id(0)
        lane = _lanes((t, 128))
        lo = lane < FOX_DH

        def prep(i, carry):
            r = pl.ds(pl.multiple_of(i * t, t), t)
            qs, ks, dob = _scaled(q_ref[r, :]), _scaled(k_ref[r, :]), do_ref[r, :]
            prod = dob.astype(F32) * o_ref[r, :]
            cumv = cum_ref[r, :]
            zero = jnp.zeros_like(qs)
            for e in range(2):
                mine = lo if e == 0 else jnp.logical_not(lo)
                qm_sc[e, r, :] = jnp.where(mine, qs, zero)
                km_sc[e, r, :] = jnp.where(mine, ks, zero)
                dom_sc[e, r, :] = jnp.where(mine, dob, zero)
                delta_sc[e, r, :] = jnp.sum(jnp.where(mine, prod, 0.0), axis=-1, keepdims=True)
                fq_sc[e, r, :] = _head_lane(cumv, 2 * hp + e)
                dfq_sc[e, r, :] = jnp.zeros((t, 1), F32)
            dq_sc[r, :] = jnp.zeros((t, 128), F32)
            return carry

        lax.fori_loop(0, nk, prep, 0)

        def kv_tile(j, carry):
            kr = pl.ds(pl.multiple_of(j * t, t), t)
            kb, vb = k_ref[kr, :], v_ref[kr, :]
            fks = [fk_ref[2 * hp + e, pl.ds(j, 1), :] for e in range(2)]

            def q_tile(i, acc, masked):
                dk, dv, dfk0, dfk1 = acc
                dfk = [dfk0, dfk1]
                qr = pl.ds(pl.multiple_of(i * t, t), t)
                dq_t = jnp.zeros((t, 128), F32)
                for e in range(2):
                    qe, doe = qm_sc[e, qr, :], dom_sc[e, qr, :]
                    s = _dot(qe, kb, 1, 1) + fq_sc[e, qr, :] - fks[e]
                    if masked:
                        s = _diag_mask(s)
                    p = jnp.exp(s - lse_ref[qr, e:e + 1])
                    dv = dv + _dot(p.astype(BF16), doe, 0, 0)
                    dp = _dot(doe, vb, 1, 1)
                    ds = p * (dp - delta_sc[e, qr, :])
                    dsb = ds.astype(BF16)
                    dk = dk + _dot(dsb, qe, 0, 0)
                    dq_t = dq_t + _dot(dsb, km_sc[e, kr, :], 1, 0)
                    dfq_sc[e, qr, :] += jnp.sum(ds, axis=-1, keepdims=True)
                    dfk[e] = dfk[e] - jnp.sum(ds, axis=0, keepdims=True)
                dq_sc[qr, :] += dq_t
                return dk, dv, dfk[0], dfk[1]

            init = (jnp.zeros((t, 128), F32), jnp.zeros((t, 128), F32), jnp.zeros((1, t), F32), jnp.zeros((1, t), F32))
            acc = q_tile(j, init, True)
            dk, dv, dfk0, dfk1 = lax.fori_loop(j + 1, nk, functools.partial(q_tile, masked=False), acc)
            dk_ref[kr, :] = dk.astype(BF16)
            dv_ref[kr, :] = dv.astype(BF16)
            dfk_ref[0, pl.ds(j, 1), :] = dfk0
            dfk_ref[1, pl.ds(j, 1), :] = dfk1
            return carry

        lax.fori_loop(0, nk, kv_tile, 0)
        dq_ref[...] = dq_sc[...].astype(BF16)
        lane_s = _lanes((S, 128))
        dfq_ref[...] = jnp.where(lane_s == 0, dfq_sc[0], jnp.where(lane_s == 1, dfq_sc[1], 0.0))

    col = lambda c0: pl.BlockSpec((S, 128), lambda hp: (0, c0 + hp))
    pair = pl.BlockSpec((S, 128), lambda hp: (0, hp))
    lanes3 = pl.BlockSpec((None, S, 128), lambda hp: (hp, 0, 0))
    big = jax.ShapeDtypeStruct((S, FOX_W), BF16)
    masked_bf16 = pltpu.VMEM((2, S, 128), BF16)
    column = pltpu.VMEM((2, S, 1), F32)
    return pl.pallas_call(
        body,
        name=name,
        grid=(_PAIRS,),
        in_specs=[
            col(0), col(_PAIRS), col(2 * _PAIRS),
            pl.BlockSpec((S, F_LANES), lambda hp: (0, 0)),
            pl.BlockSpec((FOX_H, nk, t), lambda hp: (0, 0, 0)),
            pair, pair, lanes3,
        ],
        out_specs=[pair, pair, pair, lanes3, pl.BlockSpec((None, 2, nk, t), lambda hp: (hp, 0, 0, 0))],
        out_shape=[big, big, big, jax.ShapeDtypeStruct((_PAIRS, S, 128), F32), jax.ShapeDtypeStruct((_PAIRS, 2, nk, t), F32)],
        scratch_shapes=[masked_bf16, masked_bf16, masked_bf16, column, column, column, pltpu.VMEM((S, 128), F32)],
        compiler_params=_params(("parallel",)),
    )(qkv, qkv, qkv, cum, fk3, o, do, lse)


def _group_mask(lane, gi):
    return (lane >= 64 * gi) & (lane < 64 * (gi + 1))


_U_COL = OFF_C // SGU_W


def _sgu_fwd(rest, gn, wm, bias, *, name):
    S = rest.shape[0]
    ts = _tile(S, 512)
    nc = ts // SGU_CHUNK

    def body(u_ref, v_ref, g_ref, w_ref, b_ref, o_ref):
        zv = _gelu(v_ref[...])
        vn = zv * lax.rsqrt(jnp.mean(zv * zv, axis=-1, keepdims=True) + EPS) * g_ref[...]
        lane = _lanes((SGU_CHUNK, SGU_W))
        for c in range(nc):
            rows = slice(c * SGU_CHUNK, (c + 1) * SGU_CHUNK)
            vcb = vn[rows].astype(BF16)
            mixed = b_ref[...]
            for gi in range(4):
                mixed = mixed + jnp.where(_group_mask(lane, gi), _dot(w_ref[gi], vcb, 1, 0), 0.0)
            o_ref[rows, :] = (_gelu(u_ref[rows, :]) * mixed).astype(BF16)

    return pl.pallas_call(
        body,
        name=name,
        grid=(S // ts,),
        in_specs=[
            pl.BlockSpec((ts, SGU_W), lambda i: (i, _U_COL)),
            pl.BlockSpec((ts, SGU_W), lambda i: (i, _U_COL + 1)),
            pl.BlockSpec((1, SGU_W), lambda i: (0, 0)),
            pl.BlockSpec((4, SGU_CHUNK, SGU_CHUNK), lambda i: (0, 0, 0)),
            pl.BlockSpec((SGU_CHUNK, SGU_W), lambda i: (0, 0)),
        ],
        out_specs=pl.BlockSpec((ts, SGU_W), lambda i: (i, 0)),
        out_shape=jax.ShapeDtypeStruct((S, SGU_W), BF16),
        compiler_params=_params(("parallel",)),
    )(rest, rest, gn.reshape(1, SGU_W), wm, bias)


def _sgu_bwd(rest, gn, wm, wm_t, bias, dsg, *, name):
    S = rest.shape[0]
    ts = _tile(S, 512)
    nc = ts // SGU_CHUNK

    def body(u_ref, v_ref, g_ref, w_ref, wt_ref, b_ref, dsg_ref, dc_ref, dw_ref, db_ref, dg_ref):
        first = pl.program_id(0) == 0

        @pl.when(first)
        def _():
            dw_ref[...] = jnp.zeros_like(dw_ref)
            db_ref[...] = jnp.zeros_like(db_ref)
            dg_ref[...] = jnp.zeros_like(dg_ref)

        gv = g_ref[...]
        lane = _lanes((SGU_CHUNK, SGU_W))
        for c in range(nc):
            rows = slice(c * SGU_CHUNK, (c + 1) * SGU_CHUNK)
            vpre = v_ref[rows, :]
            upre = u_ref[rows, :]
            zv = _gelu(vpre)
            r = lax.rsqrt(jnp.mean(zv * zv, axis=-1, keepdims=True) + EPS)
            zn = zv * r
            vcb = (zn * gv).astype(BF16)
            mixed = b_ref[...]
            for gi in range(4):
                mixed = mixed + jnp.where(_group_mask(lane, gi), _dot(w_ref[gi], vcb, 1, 0), 0.0)
            zu = _gelu(upre)
            dsg_v = dsg_ref[rows, :]
            dc_ref[rows, :SGU_W] = (dsg_v * mixed * _gelu_grad(upre)).astype(BF16)
            dmixed = dsg_v * zu
            db_ref[...] += dmixed
            dvn = jnp.zeros((SGU_CHUNK, SGU_W), F32)
            for gi in range(4):
                dmg = jnp.where(_group_mask(lane, gi), dmixed, 0.0).astype(BF16)
                dw_ref[gi] += _dot(dmg, vcb, 1, 1)
                dvn = dvn + _dot(wt_ref[gi], dmg, 1, 0)
            dg_ref[...] += jnp.sum(dvn * zn, axis=0, keepdims=True)
            dzn = dvn * gv
            dzv = r * (dzn - zn * jnp.mean(dzn * zn, axis=-1, keepdims=True))
            dc_ref[rows, SGU_W:] = (dzv * _gelu_grad(vpre)).astype(BF16)

    blk = pl.BlockSpec((ts, SGU_W), lambda i: (i, 0))
    vec = pl.BlockSpec((1, SGU_W), lambda i: (0, 0))
    w3 = pl.BlockSpec((4, SGU_CHUNK, SGU_CHUNK), lambda i: (0, 0, 0))
    bsp = pl.BlockSpec((SGU_CHUNK, SGU_W), lambda i: (0, 0))
    return pl.pallas_call(
        body,
        name=name,
        grid=(S // ts,),
        in_specs=[
            pl.BlockSpec((ts, SGU_W), lambda i: (i, _U_COL)),
            pl.BlockSpec((ts, SGU_W), lambda i: (i, _U_COL + 1)),
            vec, w3, w3, bsp, blk,
        ],
        out_specs=[pl.BlockSpec((ts, 2 * SGU_W), lambda i: (i, 0)), w3, bsp, vec],
        out_shape=[
            jax.ShapeDtypeStruct((S, 2 * SGU_W), BF16),
            jax.ShapeDtypeStruct((4, SGU_CHUNK, SGU_CHUNK), F32),
            jax.ShapeDtypeStruct((SGU_CHUNK, SGU_W), F32),
            jax.ShapeDtypeStruct((1, SGU_W), F32),
        ],
        compiler_params=_params(("arbitrary",)),
    )(rest, rest, gn.reshape(1, SGU_W), wm, wm_t, bias, dsg)


_GT = 512
_G0 = OFF_G // _GT


def _gate_specs(tm, col_of):
    specs = [pl.BlockSpec((tm, _GT), functools.partial(lambda k, *ids: (col_of(*ids)[0], _G0 + 2 * k + col_of(*ids)[1]), k)) for k in range(3)]
    specs += [pl.BlockSpec((1, _GT), functools.partial(lambda k, *ids: (0, 2 * k + col_of(*ids)[1]), k)) for k in range(3)]
    return specs


def _merge_fwd(rest, bg, ya, yb, yc, *, name):
    S = rest.shape[0]
    tm = _tile(S, 512)

    def body(g1, g2, g3, b1, b2, b3, ya_ref, yb_ref, yc_ref, o_ref):
        acc = _sigmoid(g1[...] + b1[...]) * ya_ref[...]
        acc = acc + _sigmoid(g2[...] + b2[...]) * yb_ref[...]
        acc = acc + _sigmoid(g3[...] + b3[...]) * yc_ref[...]
        o_ref[...] = acc.astype(BF16)

    blk = pl.BlockSpec((tm, _GT), lambda i, j: (i, j))
    return pl.pallas_call(
        body,
        name=name,
        grid=(S // tm, D // _GT),
        in_specs=_gate_specs(tm, lambda i, j: (i, j)) + [blk, blk, blk],
        out_specs=blk,
        out_shape=jax.ShapeDtypeStruct((S, D), BF16),
        compiler_params=_params(("parallel", "parallel")),
    )(rest, rest, rest, bg, bg, bg, ya, yb, yc)


def _merge_bwd(rest, bg, ya, yb, yc, dm, *, name):
    S = rest.shape[0]
    tm = _tile(S, 512)

    def body(g1, g2, g3, b1, b2, b3, ya_ref, yb_ref, yc_ref, dm_ref, dya, dyb, dyc, dg1, dg2, dg3, db1, db2, db3):
        first = pl.program_id(1) == 0
        dmv = dm_ref[...]
        for g_ref, b_ref, y_ref, dy_ref, dg_ref, db_ref in (
            (g1, b1, ya_ref, dya, dg1, db1), (g2, b2, yb_ref, dyb, dg2, db2), (g3, b3, yc_ref, dyc, dg3, db3)):
            gate = _sigmoid(g_ref[...] + b_ref[...])
            dy_ref[...] = (dmv * gate).astype(BF16)
            dpre = dmv * y_ref[...] * gate * (1.0 - gate)
            dg_ref[...] = dpre.astype(BF16)
            part = jnp.sum(dpre, axis=0, keepdims=True)

            @pl.when(first)
            def _():
                db_ref[...] = part

            @pl.when(jnp.logical_not(first))
            def _():
                db_ref[...] += part

    blk = pl.BlockSpec((tm, _GT), lambda j, i: (i, j))
    vec = pl.BlockSpec((1, _GT), lambda j, i: (0, j))
    big = jax.ShapeDtypeStruct((S, D), BF16)
    small = jax.ShapeDtypeStruct((1, D), F32)
    return pl.pallas_call(
        body,
        name=name,
        grid=(D // _GT, S // tm),
        in_specs=_gate_specs(tm, lambda j, i: (i, j)) + [blk, blk, blk, blk],
        out_specs=[blk] * 6 + [vec] * 3,
        out_shape=[big] * 6 + [small] * 3,
        compiler_params=_params(("parallel", "arbitrary")),
    )(rest, rest, rest, bg, bg, bg, ya, yb, yc, dm)


_X_SCALE = XDH ** -0.5


def _xattn_fwd(xq, kv, *, name):
    S = xq.shape[0]
    M = kv.shape[0]
    tq = _tile(S, 512)

    def body(q_ref, k_ref, v_ref, o_ref):
        s = _dot(q_ref[...], k_ref[...], 1, 1) * _X_SCALE
        e = jnp.exp(s - jnp.max(s, axis=-1, keepdims=True))
        p = e / jnp.sum(e, axis=-1, keepdims=True)
        o_ref[...] = _dot(p.astype(BF16), v_ref[...], 1, 0).astype(BF16)

    return pl.pallas_call(
        body,
        name=name,
        grid=(S // tq, XH),
        in_specs=[
            pl.BlockSpec((tq, XDH), lambda i, h: (i, h)),
            pl.BlockSpec((M, XDH), lambda i, h: (0, h)),
            pl.BlockSpec((M, XDH), lambda i, h: (0, XH + h)),
        ],
        out_specs=pl.BlockSpec((tq, XDH), lambda i, h: (i, h)),
        out_shape=jax.ShapeDtypeStruct((S, D), BF16),
        compiler_params=_params(("parallel", "parallel")),
    )(xq, kv, kv)


def _xattn_bwd(xq, kv, do, *, name):
    S = xq.shape[0]
    M = kv.shape[0]
    tq = _tile(S, 512)

    def body(q_ref, k_ref, v_ref, do_ref, dq_ref, dk_ref, dv_ref):
        qb = q_ref[...]
        kb = k_ref[...]
        dob = do_ref[...]
        s = _dot(qb, kb, 1, 1) * _X_SCALE
        e = jnp.exp(s - jnp.max(s, axis=-1, keepdims=True))
        p = e / jnp.sum(e, axis=-1, keepdims=True)
        dp = _dot(dob, v_ref[...], 1, 1)
        ds = (p * (dp - jnp.sum(p * dp, axis=-1, keepdims=True)) * _X_SCALE).astype(BF16)
        dq_ref[...] = _dot(ds, kb, 1, 0).astype(BF16)
        dk_part = _dot(ds, qb, 0, 0)
        dv_part = _dot(p.astype(BF16), dob, 0, 0)

        @pl.when(pl.program_id(1) == 0)
        def _():
            dk_ref[...] = dk_part
            dv_ref[...] = dv_part

        @pl.when(pl.program_id(1) > 0)
        def _():
            dk_ref[...] += dk_part
            dv_ref[...] += dv_part

    qspec = pl.BlockSpec((tq, XDH), lambda h, i: (i, h))
    kspec = pl.BlockSpec((M, XDH), lambda h, i: (0, h))
    dxq, dxk, dxv = pl.pallas_call(
        body,
        name=name,
        grid=(XH, S // tq),
        in_specs=[qspec, kspec, pl.BlockSpec((M, XDH), lambda h, i: (0, XH + h)), qspec],
        out_specs=[qspec, kspec, kspec],
        out_shape=[jax.ShapeDtypeStruct((S, D), BF16), jax.ShapeDtypeStruct((M, D), F32), jax.ShapeDtypeStruct((M, D), F32)],
        compiler_params=_params(("parallel", "arbitrary")),
    )(xq, kv, kv, do)
    return dxq, jnp.concatenate([dxk, dxv], axis=1)


def _adam_math(w, g, m, v):
    m = ADAM_B1 * m + (1.0 - ADAM_B1) * g
    v = ADAM_B2 * v + (1.0 - ADAM_B2) * (g * g)
    m_hat = m / (1.0 - ADAM_B1 ** ADAM_STEP)
    v_hat = v / (1.0 - ADAM_B2 ** ADAM_STEP)
    delta = -ADAM_LR * (m_hat / (jnp.sqrt(v_hat) + ADAM_EPS) + ADAM_WD * w)
    return delta, m, v


def _adamw_sharded(parts, w, m, v, *, name):
    _, R, C = w.shape
    tm = _tile(R, 256)
    nr = R // tm

    def body(p0_ref, p1_ref, w_ref, m_ref, v_ref, g_ref, d_ref, mo_ref, vo_ref):
        def update(p_ref):
            g = p_ref[0].astype(F32)
            for dev in range(1, N_DEV):
                g = g + p_ref[dev].astype(F32)
            delta, mn, vn = _adam_math(w_ref[...], g, m_ref[...], v_ref[...])
            g_ref[...] = g
            d_ref[...] = delta
            mo_ref[...] = mn
            vo_ref[...] = vn

        @pl.when(pl.program_id(0) == 0)
        def _():
            update(p0_ref)

        @pl.when(pl.program_id(0) == 1)
        def _():
            update(p1_ref)

    p0 = pl.BlockSpec((N_DEV, tm, C), lambda l, i: (0, i * (1 - l) + (nr - 1) * l, 0))
    p1 = pl.BlockSpec((N_DEV, tm, C), lambda l, i: (0, i * l, 0))
    blk = pl.BlockSpec((None, tm, C), lambda l, i: (l, i, 0))
    sds = jax.ShapeDtypeStruct(w.shape, F32)
    return pl.pallas_call(
        body,
        name=name,
        grid=(DEPTH, nr),
        in_specs=[p0, p1, blk, blk, blk],
        out_specs=[blk] * 4,
        out_shape=[sds] * 4,
        compiler_params=_params(("arbitrary", "arbitrary")),
    )(parts[0], parts[1], w, m, v)


def _adamw_small(g, w, m, v, *, name):
    n = len(g)

    def body(*refs):
        g_refs, w_refs, m_refs, v_refs = (refs[k * n:(k + 1) * n] for k in range(4))
        d_out, m_out, v_out = (refs[(4 + k) * n:(5 + k) * n] for k in range(3))
        for t in range(n):
            delta, mn, vn = _adam_math(w_refs[t][...], g_refs[t][...], m_refs[t][...], v_refs[t][...])
            d_out[t][...] = delta
            m_out[t][...] = mn
            v_out[t][...] = vn

    vm = pl.BlockSpec(memory_space=pltpu.VMEM)
    shapes = [jax.ShapeDtypeStruct(a.shape, F32) for a in w]
    outs = pl.pallas_call(
        body,
        name=name,
        in_specs=[vm] * (4 * n),
        out_specs=[vm] * (3 * n),
        out_shape=shapes * 3,
        compiler_params=pltpu.CompilerParams(vmem_limit_bytes=VMEM_LIMIT),
    )(*g, *w, *m, *v)
    return outs[:n], outs[n:2 * n], outs[2 * n:]


def _position():
    return lax.axis_index("x"), lax.axis_index("y"), lax.axis_index("c")


def _dev_index(px, py, pc):
    return 4 * px + 2 * py + pc


_ANY = pl.BlockSpec(memory_space=pl.ANY)


def _all_gather(shards, *, name):
    n = len(shards)

    def body(*refs):
        ins, outs = refs[:n], refs[n:2 * n]
        send_sems, recv_sems, local_sems = refs[2 * n:]
        x, y, c = _position()
        me, sibling = (x, y, c), (x, y, 1 - c)
        chips = [(1 - x, y), (x, 1 - y), (1 - x, 1 - y)]

        def copy(t, k, block, to, src=None):
            dst = outs[t].at[_dev_index(*block)]
            return pltpu.make_async_remote_copy(
                src_ref=dst if src is None else src, dst_ref=dst, send_sem=send_sems.at[t, k], recv_sem=recv_sems.at[t, k],
                device_id=to, device_id_type=MESH)

        mine = [pltpu.make_async_copy(ins[t], outs[t].at[_dev_index(*me)], local_sems.at[t]) for t in range(n)]
        for cp in mine:
            cp.start()
        started = []
        for j, chip in enumerate(chips):
            for t in range(n):
                started.append(copy(t, 1 + j, me, (*chip, c), src=ins[t]))
                started[-1].start()
        for t in range(n):
            started.append(copy(t, 0, me, sibling, src=ins[t]))
            started[-1].start()
        for j, chip in enumerate(chips):
            for t in range(n):
                copy(t, 1 + j, (*chip, c), me).wait_recv()
                started.append(copy(t, 4 + j, (*chip, c), sibling))
                started[-1].start()
        for t in range(n):
            copy(t, 0, sibling, me).wait_recv()
        for j, chip in enumerate(chips):
            for t in range(n):
                copy(t, 4 + j, (*chip, 1 - c), me).wait_recv()
        for cp in started:
            cp.wait_send()
        for cp in mine:
            cp.wait()

    return pl.pallas_call(
        body,
        name=name,
        in_specs=[_ANY] * n,
        out_specs=[_ANY] * n,
        out_shape=[jax.ShapeDtypeStruct((N_DEV, *s.shape), s.dtype) for s in shards],
        scratch_shapes=[pltpu.SemaphoreType.DMA((n, 7)), pltpu.SemaphoreType.DMA((n, 7)), pltpu.SemaphoreType.DMA((n,))],
        compiler_params=pltpu.CompilerParams(has_side_effects=True),
    )(*shards)


def _peers(x, y, c):
    out = []
    for mask in range(1, N_DEV):
        fx, fy, fc = (mask >> 2) & 1, (mask >> 1) & 1, mask & 1
        out.append((1 - x if fx else x, 1 - y if fy else y, 1 - c if fc else c))
    return out


_HBM = pl.BlockSpec(memory_space=pltpu.HBM)
_SEM = pl.BlockSpec(memory_space=pltpu.SEMAPHORE)


def _own_block_placed(block, like):
    x, y, c = _position()
    return lax.dynamic_update_index_in_dim(lax.empty(like.shape, like.dtype), block, _dev_index(x, y, c), 0)


def _copies(per_array):
    def mark(fn):
        fn.per_array = per_array
        return fn
    return mark


@_copies(N_DEV - 1)
def _plan_exchange(srcs, lands, send_sems, recv_sems, arrivals):
    x, y, c = _position()
    me = _dev_index(x, y, c)
    out = []
    for k, peer in enumerate(_peers(x, y, c)):
        p = _dev_index(*peer)
        for t in range(len(lands)):
            sems = dict(send_sem=send_sems.at[7 * t + k], recv_sem=recv_sems.at[7 * t + k], device_id=peer, device_id_type=MESH)
            src, dst = (lands[t].at[p], lands[t].at[p]) if arrivals else (srcs[t].at[p], lands[t].at[me])
            out.append(pltpu.make_async_remote_copy(src_ref=src, dst_ref=dst, **sems))
    return out


@_copies(4)
def _plan_gather_out(srcs, lands, send_sems, recv_sems, arrivals):
    x, y, c = _position()
    me = _dev_index(x, y, c)
    out = []
    for k, peer in enumerate([(x, y, 1 - c), (1 - x, y, c), (x, 1 - y, c), (1 - x, 1 - y, c)]):
        p = _dev_index(*peer)
        for t in range(len(lands)):
            sems = dict(send_sem=send_sems.at[4 * t + k], recv_sem=recv_sems.at[4 * t + k], device_id=peer, device_id_type=MESH)
            src, dst = (lands[t].at[p], lands[t].at[p]) if arrivals else (srcs[t], lands[t].at[me])
            out.append(pltpu.make_async_remote_copy(src_ref=src, dst_ref=dst, **sems))
    return out


@_copies(3)
def _plan_gather_pass(srcs, lands, send_sems, recv_sems, arrivals):
    x, y, c = _position()
    sibling = (x, y, 1 - c)
    out = []
    for k, chip in enumerate([(1 - x, y), (x, 1 - y), (1 - x, 1 - y)]):
        p = _dev_index(*chip, 1 - c) if arrivals else _dev_index(*chip, c)
        for t in range(len(lands)):
            sems = dict(send_sem=send_sems.at[3 * t + k], recv_sem=recv_sems.at[3 * t + k], device_id=sibling, device_id_type=MESH)
            out.append(pltpu.make_async_remote_copy(src_ref=lands[t].at[p], dst_ref=lands[t].at[p], **sems))
    return out


def _split_start(plan, srcs, lands, *, name):
    n_src, n = len(srcs), len(srcs) + len(lands)
    n_sem = plan.per_array * len(lands)

    def body(*refs):
        send_sems, recv_sems = refs[n:n + 2]
        token = refs[-1]
        for cp in plan(refs[:n_src], refs[n_src:n], send_sems, recv_sems, arrivals=False):
            cp.start()
        token[...] = jnp.zeros_like(token)

    hbm = lambda a: pltpu.HBM(a.shape, a.dtype)
    outs = pl.pallas_call(
        body,
        name=name,
        in_specs=[_HBM] * n,
        out_specs=[_SEM, _SEM] + [_HBM] * n + [pl.BlockSpec(memory_space=pltpu.VMEM)],
        out_shape=[pltpu.SemaphoreType.DMA((n_sem,)), pltpu.SemaphoreType.DMA((n_sem,))] + [hbm(a) for a in (*srcs, *lands)]
        + [jax.ShapeDtypeStruct(_TOKEN, F32)],
        input_output_aliases={i: 2 + i for i in range(n)},
        compiler_params=pltpu.CompilerParams(has_side_effects=pltpu.SideEffectType.DATAFLOW_SIDE_EFFECTING),
    )(*[pltpu.with_memory_space_constraint(a, pltpu.HBM) for a in (*srcs, *lands)])
    return (outs[0], outs[1], outs[2:2 + n_src], outs[2 + n_src:2 + n]), outs[-1]


def _split_wait(plan, state, after, *, name):
    send_sems, recv_sems, srcs, lands = state
    n_src, n = len(srcs), len(srcs) + len(lands)

    def body(*refs):
        send_refs, recv_refs = refs[n:n + 2]
        for cp in plan(refs[:n_src], refs[n_src:n], send_refs, recv_refs, arrivals=False):
            cp.wait_send()
        for cp in plan(refs[:n_src], refs[n_src:n], send_refs, recv_refs, arrivals=True):
            cp.wait_recv()

    hbm = lambda a: pltpu.HBM(a.shape, a.dtype)
    outs = pl.pallas_call(
        body,
        name=name,
        in_specs=[_HBM] * n + [_SEM, _SEM, _ANY],
        out_specs=[_HBM] * n,
        out_shape=[hbm(a) for a in (*srcs, *lands)],
        input_output_aliases={i: i for i in range(n)},
        compiler_params=pltpu.CompilerParams(has_side_effects=pltpu.SideEffectType.DATAFLOW_SIDE_EFFECTING),
    )(*srcs, *lands, send_sems, recv_sems, after)
    return outs[n_src:]


def _all_reduce(g_local, *, name):
    R, C = g_local.shape

    def body(g_ref, o_ref, buf, send_sems, recv_sems):
        x, y, c = _position()
        me = _dev_index(x, y, c)
        peers = _peers(x, y, c)
        copies = []
        for k, peer in enumerate(peers):
            copies.append(pltpu.make_async_remote_copy(
                src_ref=g_ref, dst_ref=buf.at[me], send_sem=send_sems.at[k], recv_sem=recv_sems.at[k],
                device_id=peer, device_id_type=MESH))
            copies[-1].start()
        buf[me] = g_ref[...]
        for k, peer in enumerate(peers):
            dst = buf.at[_dev_index(*peer)]
            pltpu.make_async_remote_copy(
                src_ref=dst, dst_ref=dst, send_sem=send_sems.at[k], recv_sem=recv_sems.at[k],
                device_id=peer, device_id_type=MESH).wait_recv()
        for cp in copies:
            cp.wait_send()
        g = buf[0]
        for dev in range(1, N_DEV):
            g = g + buf[dev]
        o_ref[...] = g

    vm = pl.BlockSpec(memory_space=pltpu.VMEM)
    return pl.pallas_call(
        body,
        name=name,
        in_specs=[vm],
        out_specs=vm,
        out_shape=jax.ShapeDtypeStruct((R, C), F32),
        scratch_shapes=[pltpu.VMEM((N_DEV, R, C), F32), pltpu.SemaphoreType.DMA((7,)), pltpu.SemaphoreType.DMA((7,))],
        compiler_params=pltpu.CompilerParams(has_side_effects=True, vmem_limit_bytes=VMEM_LIMIT),
    )(g_local)


def _block_diag(w):
    out = jnp.zeros((POOL_W, POOL_W), w.dtype)
    for gi in range(4):
        out = out.at[64 * gi:64 * (gi + 1), 64 * gi:64 * (gi + 1)].set(w[gi])
    return out


def _layer_consts(sp, l):
    causal = jnp.tril(jnp.ones((SGU_CHUNK, SGU_CHUNK), F32))
    wm = (sp["sgu_w"][l] * causal[None]).astype(BF16)
    wbd = _block_diag(sp["pool_w"][l]).astype(BF16)
    return dict(
        wbd=wbd, wbd_t=wbd.T, wm=wm, wm_t=wm.transpose(0, 2, 1),
        sgu_bias=jnp.repeat(sp["sgu_b"][l].T, 64, axis=1),
        bpad=jnp.pad(sp["b_forget"][l], (0, F_LANES - FOX_H)).reshape(1, F_LANES),
        bg=sp["b_gate"][l].reshape(1, 3 * D),
    )


def _relu2(acc):
    return acc, jnp.square(jnp.maximum(acc, 0.0))


def _relu2_grad(acc, z):
    return (acc * 2.0 * jnp.maximum(z, 0.0),)


def _layer_fwd(l, x, mem, source, sp):
    S = x.shape[0]
    t = _tile(S, 256)
    c = _layer_consts(sp, l)
    n = f"l{l}_"
    W, after = source(l, "begin", x)
    h = _rms_fwd(x, sp["norm_mix_g"][l], after=after, name=n + "norm_mix")
    qkv = _mm(h, W["qkv"], out_dtypes=(BF16,), name=n + "qkv")
    rest = _mm(h, W["rest"], name=n + "rest")
    pa = _pool_fwd(rest, c["wbd"], sp["pool_scale"][l], name=n + "pool")
    cum, cum_t = _fox_prep(rest, c["bpad"], name=n + "fox_prep")
    fk3 = cum_t[:FOX_H].reshape(FOX_H, S // t, t)
    o, lse = _fox_fwd(qkv, cum, fk3, name=n + "fox")
    more, _ = source(l, "attended", o)
    W.update(more)
    sg = _sgu_fwd(rest, sp["sgu_norm_g"][l], c["wm"], c["sgu_bias"], name=n + "sgu")
    more, after = source(l, "mixed", sg)
    W.update(more)
    ya = _mm(pa, W["ba"], after=after, name=n + "branch_a")
    yb = _mm(o, W["bb"], name=n + "branch_b")
    yc = _mm(sg, W["bc"], name=n + "branch_c")
    merged = _merge_fwd(rest, c["bg"], ya, yb, yc, name=n + "merge")
    x1 = _mm(merged, W["out"], extras=(x,), epilogue=_add, name=n + "out")
    hx = _rms_fwd(x1, sp["norm_xattn_g"][l], name=n + "norm_xattn")
    hm = _rms_fwd(mem, sp["norm_mem_g"][l], name=n + "norm_mem")
    xq = _mm(hx, W["xq"], out_dtypes=(BF16,), name=n + "xq")
    kv = _mm(hm, W["xkv"], out_dtypes=(BF16,), name=n + "xkv")
    o2 = _xattn_fwd(xq, kv, name=n + "xattn")
    x2 = _mm(o2, W["xo"], extras=(x1,), epilogue=_add, name=n + "xo")
    hf = _rms_fwd(x2, sp["norm_ffn_g"][l], name=n + "norm_ffn")
    z, act = _mm(hf, W["ff1"], epilogue=_relu2, out_dtypes=(F32, BF16), name=n + "ff1")
    _, after = source(l, "expanded", act)
    x3 = _mm(act, W["ff2"], extras=(x2,), epilogue=_add, after=after, name=n + "ff2")
    saved = dict(x=x, h=h, qkv=qkv, rest=rest, pa=pa, cum=cum, fk3=fk3, o=o, lse=lse, sg=sg, ya=ya, yb=yb, yc=yc,
                 merged=merged, x1=x1, hx=hx, hm=hm, xq=xq, kv=kv, o2=o2, x2=x2, hf=hf, z=z, act=act, c=c)
    return x3, saved, W


def _layer_bwd(l, dx3, sv, mem, W, sp, grads_done):
    S = dx3.shape[0]
    c = sv["c"]
    n = f"l{l}b_"
    bf = dict(out_dtypes=(BF16,))
    gw, gs = {}, {}
    gw["ff2"] = _mm(sv["act"], dx3, ta=True, name=n + "dw_ff2", **bf)
    dz = _mm(dx3, W["ff2"], tb=True, extras=(sv["z"],), epilogue=_relu2_grad, name=n + "dz", **bf)
    gw["ff1"] = _mm(sv["hf"], dz, ta=True, shard_out=True, name=n + "dw_ff1", **bf)
    dhf = _mm(dz, W["ff1"], tb=True, name=n + "dhf")
    dx2, gs["norm_ffn_g"] = _rms_bwd(sv["x2"], sp["norm_ffn_g"][l], dhf, dx3, name=n + "dnorm_ffn")
    gw["xo"] = _mm(sv["o2"], dx2, ta=True, name=n + "dw_xo", **bf)
    do2 = _mm(dx2, W["xo"], tb=True, name=n + "do2", **bf)
    dxq, dkv = _xattn_bwd(sv["xq"], sv["kv"], do2, name=n + "dxattn")
    gw["xq"] = _mm(sv["hx"], dxq, ta=True, name=n + "dw_xq", **bf)
    gw["xkv"] = _mm(sv["hm"], dkv, ta=True, shard_out=True, name=n + "dw_xkv", **bf)
    dhm = _mm(dkv, W["xkv"], tb=True, name=n + "dhm")
    _, gs["norm_mem_g"] = _rms_bwd(mem, sp["norm_mem_g"][l], dhm, jnp.zeros_like(mem), name=n + "dnorm_mem")
    dhx = _mm(dxq, W["xq"], tb=True, name=n + "dhx")
    dx1, gs["norm_xattn_g"] = _rms_bwd(sv["x1"], sp["norm_xattn_g"][l], dhx, dx2, name=n + "dnorm_xattn")
    after, gw = grads_done(l, gw), {}
    gw["out"] = _mm(sv["merged"], dx1, ta=True, name=n + "dw_out", **bf)
    dm = _mm(dx1, W["out"], tb=True, after=after, name=n + "dmerged")
    dya, dyb, dyc, dg1, dg2, dg3, db1, db2, db3 = _merge_bwd(sv["rest"], c["bg"], sv["ya"], sv["yb"], sv["yc"], dm, name=n + "dmerge")
    gs["b_gate"] = jnp.concatenate([db1, db2, db3], axis=1).reshape(3 * D)
    gw["ba"] = _mm(sv["pa"], dya, ta=True, shard_out=True, name=n + "dw_ba", **bf)
    gw["bb"] = _mm(sv["o"], dyb, ta=True, shard_out=True, name=n + "dw_bb", **bf)
    gw["bc"] = _mm(sv["sg"], dyc, ta=True, shard_out=True, name=n + "dw_bc", **bf)
    after, gw = grads_done(l, gw), {}
    dpa = _mm(dya, W["ba"], tb=True, name=n + "dpa")
    do = _mm(dyb, W["bb"], tb=True, after=after, name=n + "do", **bf)
    dsg = _mm(dyc, W["bc"], tb=True, name=n + "dsg")
    da, dwbd, dscale = _pool_bwd(sv["rest"], c["wbd"], c["wbd_t"], sp["pool_scale"][l], dpa, name=n + "dpool")
    gs["pool_w"] = jnp.stack([dwbd[64 * gi:64 * (gi + 1), 64 * gi:64 * (gi + 1)] for gi in range(4)])
    gs["pool_scale"] = dscale.reshape(POOL_W)
    dq, dk, dv, dfq, dfk = _fox_bwd(sv["qkv"], sv["cum"], sv["fk3"], sv["o"], do, sv["lse"], name=n + "dfox")
    dcum = dfq[:, :, :2].transpose(1, 0, 2).reshape(S, FOX_H) + dfk.reshape(FOX_H, S).T
    df, dbf = _fox_post(sv["rest"], c["bpad"], jnp.pad(dcum, ((0, 0), (0, F_LANES - FOX_H))), name=n + "dfox_post")
    gs["b_forget"] = dbf[0, :FOX_H]
    dc, dwm, dbias, dgn = _sgu_bwd(sv["rest"], sp["sgu_norm_g"][l], c["wm"], c["wm_t"], c["sgu_bias"], dsg, name=n + "dsgu")
    gs["sgu_w"] = dwm * jnp.tril(jnp.ones((SGU_CHUNK, SGU_CHUNK), F32))[None]
    gs["sgu_b"] = dbias.reshape(SGU_CHUNK, 4, 64).sum(axis=2).T
    gs["sgu_norm_g"] = dgn.reshape(SGU_W)
    dqkv = jnp.concatenate([dq, dk, dv], axis=1)
    drest = jnp.concatenate([da, df, jnp.zeros((S, OFF_C - OFF_F - F_LANES), BF16), dc, dg1, dg2, dg3], axis=1)
    gw["qkv"] = _mm(sv["h"], dqkv, ta=True, name=n + "dw_qkv", **bf)
    gw["rest"] = _mm(sv["h"], drest, ta=True, name=n + "dw_rest", **bf)
    after = grads_done(l, gw)
    dh = _mm(dqkv, W["qkv"], tb=True, after=after, name=n + "dh_qkv")
    dh = _mm(drest, W["rest"], tb=True, extras=(dh,), epilogue=_add, name=n + "dh")
    dx, gs["norm_mix_g"] = _rms_bwd(sv["x"], sp["norm_mix_g"][l], dh, dx1, name=n + "dnorm_mix")
    return dx, gs


def _local_step(x, mem, target, sp, source, grads_done):
    saved, Ws = [], []
    for l in range(DEPTH):
        x, sv, W = _layer_fwd(l, x, mem, source, sp)
        saved.append(sv)
        Ws.append(W)
    loss, dx, dgf = _final_loss(x, sp["final_norm_g"], target, name="final_loss")
    gss = [None] * DEPTH
    for l in reversed(range(DEPTH)):
        dx, gss[l] = _layer_bwd(l, dx, saved[l], mem, Ws[l], sp, grads_done)
    small = {k: jnp.stack([gss[l][k] for l in range(DEPTH)]) for k in gss[0]}
    small["final_norm_g"] = dgf
    return loss, dx, small


_SMALL = ["norm_mix_g", "b_forget", "pool_w", "pool_scale", "sgu_norm_g", "sgu_w", "sgu_b", "b_gate", "norm_xattn_g",
          "norm_mem_g", "norm_ffn_g", "final_norm_g"]
_COL = {"w_branch_a": "ba", "w_branch_b": "bb", "w_branch_c": "bc", "w_xkv": "xkv", "w_ff1": "ff1"}
_ROW = {"w_out": "out", "w_xq": "xq", "w_xo": "xo", "w_ff2": "ff2"}
_BIG = ["w_in", "w_branch_a", "w_branch_b", "w_branch_c", "w_out", "w_xq", "w_xkv", "w_xo", "w_ff1", "w_ff2"]
_PACK_LANES = 128


def _as_rows(a):
    return a.reshape(-1, a.shape[-1])


def _pack(tensors):
    rows = []
    for a in tensors:
        flat = a.reshape(-1)
        flat = jnp.pad(flat, (0, (-flat.shape[0]) % (8 * _PACK_LANES)))
        rows.append(flat.reshape(-1, _PACK_LANES))
    return jnp.concatenate(rows, axis=0)


def _unpack(packed, like):
    out, r = [], 0
    for a in like:
        size = math.prod(a.shape)
        nr = 8 * (-(-size // (8 * _PACK_LANES)))
        out.append(packed[r:r + nr].reshape(-1)[:size].reshape(a.shape))
        r += nr
    return out


_SHARD_IN = N_IN // N_DEV


def _columns(pieces, start, stop):
    out, at = [], 0
    for p in pieces:
        lo, hi = max(start, at), min(stop, at + p.shape[1])
        if lo < hi:
            out.append(p[:, lo - at:hi - at])
        at += p.shape[1]
    return out


def _split_w_in(blocks):
    K = blocks[0].shape[0]
    pad = jnp.zeros((K, OFF_C - OFF_F - FOX_H), blocks[0].dtype)
    cols = functools.partial(_columns, blocks)
    rest = jnp.concatenate(cols(0, R_OFF_Q) + cols(R_OFF_F, R_OFF_C) + [pad] + cols(R_OFF_C, N_IN), axis=1)
    return jnp.concatenate(cols(R_OFF_Q, R_OFF_F), axis=1), rest


def _join_w_in(qkv, rest):
    in_order = [rest[:, :R_OFF_Q], qkv, rest[:, OFF_F:OFF_F + FOX_H], rest[:, OFF_C:]]
    return jnp.stack([jnp.concatenate(_columns(in_order, _SHARD_IN * d, _SHARD_IN * (d + 1)), axis=1) for d in range(N_DEV)])


_FIRST = ["w_in"]
_LATER = [k for k in _BIG if k not in _FIRST]


def _layer_weights(gathered):
    W = {}
    if "w_in" in gathered:
        W.update(zip(("qkv", "rest"), _split_w_in([gathered["w_in"][d] for d in range(N_DEV)])))
    for name, key in _COL.items():
        if name in gathered:
            W[key] = _Gathered(gathered[name])
    for name, key in _ROW.items():
        if name in gathered:
            W[key] = gathered[name].reshape(-1, gathered[name].shape[-1])
    return W


def _grad_blocks(gw):
    parts = {}
    if "qkv" in gw:
        parts["w_in"] = _join_w_in(gw["qkv"], gw["rest"])
    for name, key in _COL.items():
        if key in gw:
            parts[name] = gw[key]
    for name, key in _ROW.items():
        if key in gw:
            parts[name] = gw[key].reshape(N_DEV, -1, gw[key].shape[-1])
    return parts


def kernel(x, mem, norm_mix_g, w_in, b_forget, pool_w, pool_scale, sgu_norm_g, sgu_w, sgu_b, w_branch_a, w_branch_b, w_branch_c, b_gate, w_out, norm_xattn_g, norm_mem_g, w_xq, w_xkv, w_xo, norm_ffn_g, w_ff1, w_ff2, final_norm_g, loss_target, m_norm_mix_g, m_w_in, m_b_forget, m_pool_w, m_pool_scale, m_sgu_norm_g, m_sgu_w, m_sgu_b, m_w_branch_a, m_w_branch_b, m_w_branch_c, m_b_gate, m_w_out, m_norm_xattn_g, m_norm_mem_g, m_w_xq, m_w_xkv, m_w_xo, m_norm_ffn_g, m_w_ff1, m_w_ff2, m_final_norm_g, v_norm_mix_g, v_w_in, v_b_forget, v_pool_w, v_pool_scale, v_sgu_norm_g, v_sgu_w, v_sgu_b, v_w_branch_a, v_w_branch_b, v_w_branch_c, v_b_gate, v_w_out, v_norm_xattn_g, v_norm_mem_g, v_w_xq, v_w_xkv, v_w_xo, v_norm_ffn_g, v_w_ff1, v_w_ff2, v_final_norm_g):
    names = ["norm_mix_g", "w_in", "b_forget", "pool_w", "pool_scale", "sgu_norm_g", "sgu_w", "sgu_b", "w_branch_a", "w_branch_b",
             "w_branch_c", "b_gate", "w_out", "norm_xattn_g", "norm_mem_g", "w_xq", "w_xkv", "w_xo", "norm_ffn_g", "w_ff1", "w_ff2",
             "final_norm_g"]
    w = dict(zip(names, [norm_mix_g, w_in, b_forget, pool_w, pool_scale, sgu_norm_g, sgu_w, sgu_b, w_branch_a, w_branch_b, w_branch_c,
                         b_gate, w_out, norm_xattn_g, norm_mem_g, w_xq, w_xkv, w_xo, norm_ffn_g, w_ff1, w_ff2, final_norm_g]))
    m = dict(zip(names, [m_norm_mix_g, m_w_in, m_b_forget, m_pool_w, m_pool_scale, m_sgu_norm_g, m_sgu_w, m_sgu_b, m_w_branch_a,
                         m_w_branch_b, m_w_branch_c, m_b_gate, m_w_out, m_norm_xattn_g, m_norm_mem_g, m_w_xq, m_w_xkv, m_w_xo,
                         m_norm_ffn_g, m_w_ff1, m_w_ff2, m_final_norm_g]))
    v = dict(zip(names, [v_norm_mix_g, v_w_in, v_b_forget, v_pool_w, v_pool_scale, v_sgu_norm_g, v_sgu_w, v_sgu_b, v_w_branch_a,
                         v_w_branch_b, v_w_branch_c, v_b_gate, v_w_out, v_norm_xattn_g, v_norm_mem_g, v_w_xq, v_w_xkv, v_w_xo,
                         v_norm_ffn_g, v_w_ff1, v_w_ff2, v_final_norm_g]))

    sp = {k: w[k] for k in _SMALL}
    shards = [{k: w[k][l].astype(BF16) for k in _BIG} for l in range(DEPTH)]
    me = _dev_index(*_position())

    def gather_out(l, keys, name):
        srcs = [shards[l][k] for k in keys]
        lands = [_own_block_placed(a, jax.ShapeDtypeStruct((N_DEV, *a.shape), a.dtype)) for a in srcs]
        state, token = _split_start(_plan_gather_out, srcs, lands, name=name + "_out_start")
        return (keys, name, state), token

    def gather_pass(job, value):
        keys, name, state = job
        lands = _split_wait(_plan_gather_out, state, value, name=name + "_out_wait")
        state, token = _split_start(_plan_gather_pass, [], lands, name=name + "_pass_start")
        return (keys, name, state), token

    def gather_end(job, value):
        keys, name, state = job
        return _layer_weights(dict(zip(keys, _split_wait(_plan_gather_pass, state, value, name=name + "_pass_wait"))))

    jobs = {}

    def source(l, point, value):
        if (l, point) == (0, "begin"):
            first = _all_gather([shards[0][k] for k in _FIRST], name="gather_l0_first")
            jobs["l0"], token = gather_out(0, _LATER, "gather_l0")
            return _layer_weights(dict(zip(_FIRST, first))), token
        if (l, point) == (0, "attended"):
            jobs["l0"], _ = gather_pass(jobs["l0"], value)
            return {}, None
        if (l, point) == (0, "mixed"):
            W = gather_end(jobs.pop("l0"), value)
            jobs["l1"], token = gather_out(1, _BIG, "gather_l1")
            return W, token
        if (l, point) == (0, "expanded"):
            jobs["l1"], token = gather_pass(jobs["l1"], value)
            return {}, token
        if (l, point) == (1, "begin"):
            return gather_end(jobs.pop("l1"), value), None
        return {}, None

    received = [{} for _ in range(DEPTH)]
    travelling = []

    def grads_done(l, gw):
        blocks = _grad_blocks(gw)
        keys = [k for k in _BIG if k in blocks]
        parts = [blocks[k] for k in keys]
        group = f"exchange_grads_l{l}_" + ("in" if "w_in" in blocks else "merge" if "w_out" in blocks else "mlp")
        lands = [_own_block_placed(lax.dynamic_index_in_dim(p, me, 0, keepdims=False), p) for p in parts]
        state, token = _split_start(_plan_exchange, parts, lands, name=group + "_start")
        travelling.append((l, keys, state, group + "_wait"))
        return token

    loss, dx, small = _local_step(x[0], mem[0], loss_target[0], sp, source, grads_done)
    loss = lax.psum(loss[0, 0], ("x", "y", "c"))
    for l, keys, state, wait_name in travelling:
        received[l].update(zip(keys, _split_wait(_plan_exchange, state, dx, name=wait_name)))

    grads, deltas, new_m, new_v = {}, {}, {}, {}
    for k in _BIG:
        outs = _adamw_sharded([received[l][k] for l in range(DEPTH)], w[k], m[k], v[k], name="adamw_" + k)
        grads[k], deltas[k], new_m[k], new_v[k] = outs
    like = [w[k] for k in _SMALL]
    g_small = _unpack(_all_reduce(_pack([small[k] for k in _SMALL]), name="all_reduce_small"), like)
    rows = lambda d: [_as_rows(d[k]) for k in _SMALL]
    outs = _adamw_small([_as_rows(g) for g in g_small], rows(w), rows(m), rows(v), name="adamw_small")
    grads.update(zip(_SMALL, g_small))
    for dst, vals in zip((deltas, new_m, new_v), outs):
        dst.update({k: a.reshape(w[k].shape) for k, a in zip(_SMALL, vals)})

    return (loss, dx[None], *[grads[k] for k in names], *[deltas[k] for k in names], *[new_m[k] for k in names],
            *[new_v[k] for k in names])
```

```python
import functools
import math

import jax
import jax.numpy as jnp
from jax import lax
from jax.experimental import pallas as pl
from jax.experimental.pallas import tpu as pltpu

F32 = jnp.float32
BF16 = jnp.bfloat16
MESH = pl.DeviceIdType.MESH

N_DEV = 8
D = 1024
DEPTH = 2
EPS = 1e-6
NEG = -1e30
POOL_W = 256
FOX_H = 8
FOX_DH = 64
FOX_W = 512
SGU_W = 256
SGU_CHUNK = 128
XH = 4
XDH = 256
N_IN = 5384
R_OFF_Q, R_OFF_F, R_OFF_C = 256, 1792, 1800
QKV_W = 3 * FOX_W
OFF_A, OFF_F, OFF_C, OFF_G, REST_W = 0, 256, 512, 1024, 4096
F_LANES = 128

ADAM_LR = 0.001
ADAM_B1 = 0.9
ADAM_B2 = 0.999
ADAM_EPS = 1e-08
ADAM_WD = 0.01
ADAM_STEP = 10

VMEM_LIMIT = 56 * 1024 * 1024


def _tile(n, pref):
    t = min(n, pref)
    while n % t:
        t -= 128
    assert t > 0, (n, pref)
    return t


def _params(sem=None):
    return pltpu.CompilerParams(dimension_semantics=sem, vmem_limit_bytes=VMEM_LIMIT)


def _dot(a, b, ca, cb):
    return lax.dot_general(a, b, (((ca,), (cb,)), ((), ())), preferred_element_type=F32)


def _sigmoid(z):
    return 1.0 / (1.0 + jnp.exp(-z))


_GELU_K = math.sqrt(2.0 / math.pi)
_GELU_C = 0.044715


def _gelu(x):
    return 0.5 * x * (1.0 + jnp.tanh(_GELU_K * (x + _GELU_C * x * x * x)))


def _gelu_grad(x):
    t = jnp.tanh(_GELU_K * (x + _GELU_C * x * x * x))
    return 0.5 * (1.0 + t) + 0.5 * x * (1.0 - t * t) * _GELU_K * (1.0 + 3.0 * _GELU_C * x * x)


def _rows(shape):
    return lax.broadcasted_iota(jnp.int32, shape, 0)


def _lanes(shape):
    return lax.broadcasted_iota(jnp.int32, shape, 1)


class _Gathered:
    def __init__(self, arr):
        self.arr = arr
        self.shape = (arr.shape[1], N_DEV * arr.shape[2])


_TOKEN = (8, 128)


def _mm(a, b, *, ta=False, tb=False, extras=(), epilogue=None, out_dtypes=(F32,), shard_out=False, after=None, tm=None, tn=512, tk=None,
        name):
    M, K = (a.shape[1], a.shape[0]) if ta else a.shape
    N, Kb = b.shape if tb else b.shape[::-1]
    assert Kb == K, (a.shape, b.shape, ta, tb)
    gathered = isinstance(b, _Gathered)
    if gathered:
        if tb:
            tk = b.arr.shape[2]
        else:
            tn = b.arr.shape[2]
    if shard_out:
        tn = N // N_DEV
    tm = _tile(M, tm or (1024 if ta else 2048))
    tn = _tile(N, tn)
    tk = _tile(K, tk or (2048 if ta else 1024))
    nk = K // tk
    ca, cb = (0 if ta else 1), (1 if tb else 0)
    n_ex, n_out = len(extras), len(out_dtypes)
    tokens = [] if after is None else [after]
    n_in = 2 + n_ex + len(tokens)
    if epilogue is None:
        epilogue = lambda acc: (acc,)

    def body(*refs):
        a_ref, b_ref = refs[:2]
        ex_refs = refs[2:2 + n_ex]
        o_refs = refs[n_in:n_in + n_out]
        part = _dot(a_ref[...].astype(BF16), b_ref[...].astype(BF16), ca, cb)

        def finish(acc):
            for o_ref, val in zip(o_refs, epilogue(acc, *[e[...] for e in ex_refs])):
                o_ref[...] = val.astype(o_ref.dtype)

        if nk == 1:
            finish(part)
        else:
            acc_ref = refs[-1]
            k = pl.program_id(2)

            @pl.when(k == 0)
            def _():
                acc_ref[...] = part

            @pl.when(k > 0)
            def _():
                acc_ref[...] += part

            @pl.when(k == nk - 1)
            def _():
                finish(acc_ref[...])

    a_spec = pl.BlockSpec((tk, tm), lambda i, j, k: (k, i)) if ta else pl.BlockSpec((tm, tk), lambda i, j, k: (i, k))
    if not gathered:
        b_arr = b
        b_spec = pl.BlockSpec((tn, tk), lambda i, j, k: (j, k)) if tb else pl.BlockSpec((tk, tn), lambda i, j, k: (k, j))
    else:
        b_arr = b.arr
        if tb:
            b_spec = pl.BlockSpec((None, tn, tk), lambda i, j, k: (k, j, 0))
        else:
            b_spec = pl.BlockSpec((None, tk, tn), lambda i, j, k: (j, k, 0))
    tile = pl.BlockSpec((tm, tn), lambda i, j, k: (i, j))
    if shard_out:
        out_specs = [pl.BlockSpec((None, tm, tn), lambda i, j, k: (j, i, 0))] * n_out
        out_shape = [jax.ShapeDtypeStruct((N_DEV, M, tn), dt) for dt in out_dtypes]
    else:
        out_specs = [tile] * n_out
        out_shape = [jax.ShapeDtypeStruct((M, N), dt) for dt in out_dtypes]
    size = lambda dt: jnp.dtype(dt).itemsize
    vmem = 2 * (tm * tk * size(a.dtype) + tk * tn * size(b_arr.dtype)
                + tm * tn * (sum(size(e.dtype) for e in extras) + sum(map(size, out_dtypes))))
    vmem += tm * tn * 4 * (nk > 1)
    assert vmem <= VMEM_LIMIT - (4 << 20), (name, vmem)
    outs = pl.pallas_call(
        body,
        name=name,
        grid=(M // tm, N // tn, nk),
        in_specs=[a_spec, b_spec] + [tile] * n_ex + [pl.BlockSpec(_TOKEN, lambda i, j, k: (0, 0))] * len(tokens),
        out_specs=out_specs,
        out_shape=out_shape,
        scratch_shapes=[pltpu.VMEM((tm, tn), F32)] if nk > 1 else [],
        compiler_params=_params(("parallel", "parallel", "arbitrary")),
    )(a, b_arr, *extras, *tokens)
    return outs[0] if n_out == 1 else outs


def _add(acc, res):
    return (acc + res,)


def _rms_fwd(x, g, *, after=None, name):
    R, C = x.shape
    tm = _tile(R, 256)
    tokens = [] if after is None else [after]

    def body(x_ref, g_ref, *rest):
        xv = x_ref[...]
        r = lax.rsqrt(jnp.mean(xv * xv, axis=-1, keepdims=True) + EPS)
        rest[-1][...] = (xv * r * g_ref[...]).astype(BF16)

    return pl.pallas_call(
        body,
        name=name,
        grid=(R // tm,),
        in_specs=[pl.BlockSpec((tm, C), lambda i: (i, 0)), pl.BlockSpec((1, C), lambda i: (0, 0))]
        + [pl.BlockSpec(_TOKEN, lambda i: (0, 0))] * len(tokens),
        out_specs=pl.BlockSpec((tm, C), lambda i: (i, 0)),
        out_shape=jax.ShapeDtypeStruct((R, C), BF16),
        compiler_params=_params(("parallel",)),
    )(x, g.reshape(1, C), *tokens)


def _rms_bwd(x, g, dh, dres, *, name):
    R, C = x.shape
    tm = _tile(R, 256)

    def body(x_ref, g_ref, dh_ref, dres_ref, dx_ref, dg_ref):
        xv = x_ref[...]
        r = lax.rsqrt(jnp.mean(xv * xv, axis=-1, keepdims=True) + EPS)
        xn = xv * r
        dh_v = dh_ref[...].astype(F32)
        dxn = dh_v * g_ref[...]
        dx_ref[...] = r * (dxn - xn * jnp.mean(dxn * xn, axis=-1, keepdims=True)) + dres_ref[...]
        part = jnp.sum(dh_v * xn, axis=0, keepdims=True)

        @pl.when(pl.program_id(0) == 0)
        def _():
            dg_ref[...] = part

        @pl.when(pl.program_id(0) > 0)
        def _():
            dg_ref[...] += part

    row = pl.BlockSpec((tm, C), lambda i: (i, 0))
    vec = pl.BlockSpec((1, C), lambda i: (0, 0))
    dx, dg = pl.pallas_call(
        body,
        name=name,
        grid=(R // tm,),
        in_specs=[row, vec, row, row],
        out_specs=[row, vec],
        out_shape=[jax.ShapeDtypeStruct((R, C), F32), jax.ShapeDtypeStruct((1, C), F32)],
        compiler_params=_params(("arbitrary",)),
    )(x, g.reshape(1, C), dh, dres)
    return dx, dg.reshape(C)


def _final_loss(x, g, target, *, name):
    R, C = x.shape
    tm = _tile(R, 256)

    def body(x_ref, g_ref, t_ref, loss_ref, dx_ref, dg_ref):
        xv = x_ref[...]
        r = lax.rsqrt(jnp.mean(xv * xv, axis=-1, keepdims=True) + EPS)
        xn = xv * r
        gv = g_ref[...]
        err = xn * gv - t_ref[...]
        lpart = (0.5 / C) * jnp.sum(jnp.sum(err * err, axis=1, keepdims=True), axis=0, keepdims=True)
        dy = err * (1.0 / C)
        dxn = dy * gv
        dx_ref[...] = r * (dxn - xn * jnp.mean(dxn * xn, axis=-1, keepdims=True))
        gpart = jnp.sum(dy * xn, axis=0, keepdims=True)

        @pl.when(pl.program_id(0) == 0)
        def _():
            loss_ref[...] = lpart
            dg_ref[...] = gpart

        @pl.when(pl.program_id(0) > 0)
        def _():
            loss_ref[...] += lpart
            dg_ref[...] += gpart

    row = pl.BlockSpec((tm, C), lambda i: (i, 0))
    vec = pl.BlockSpec((1, C), lambda i: (0, 0))
    loss, dx, dg = pl.pallas_call(
        body,
        name=name,
        grid=(R // tm,),
        in_specs=[row, vec, row],
        out_specs=[pl.BlockSpec((1, 1), lambda i: (0, 0)), row, vec],
        out_shape=[jax.ShapeDtypeStruct((1, 1), F32), jax.ShapeDtypeStruct((R, C), F32), jax.ShapeDtypeStruct((1, C), F32)],
        compiler_params=_params(("arbitrary",)),
    )(x, g.reshape(1, C), target)
    return loss, dx, dg.reshape(C)


def _pool_select(lane, vals):
    out = vals[3]
    for gi in (2, 1, 0):
        out = jnp.where(lane < 64 * (gi + 1), vals[gi], out)
    return out


def _pool_diff(a):
    row, lane = _rows(a.shape), _lanes(a.shape)

    def down(v, k):
        return jnp.where(row >= k, pltpu.roll(v, k, 0), 0.0)

    s2 = a + down(a, 1)
    s4 = s2 + down(s2, 2)
    s8 = s4 + down(s4, 4)
    s16 = s8 + down(s8, 8)
    wsum = _pool_select(lane, (s2, s4, s8, s16))
    win = _pool_select(lane, (2, 4, 8, 16))
    cnt = jnp.minimum(row + 1, win).astype(F32)
    return wsum / cnt - a, cnt


def _pool_diff_t(dd, cnt):
    S = dd.shape[0]
    row, lane = _rows(dd.shape), _lanes(dd.shape)

    def up(v, k):
        return jnp.where(row < S - k, pltpu.roll(v, S - k, 0), 0.0)

    e = dd / cnt
    s2 = e + up(e, 1)
    s4 = s2 + up(s2, 2)
    s8 = s4 + up(s4, 4)
    s16 = s8 + up(s8, 8)
    return _pool_select(lane, (s2, s4, s8, s16)) - dd


def _pool_fwd(rest, wbd, scale, *, name):
    S = rest.shape[0]

    def body(a_ref, w_ref, s_ref, o_ref):
        d, _ = _pool_diff(a_ref[...])
        yp = _dot(d.astype(BF16), w_ref[...], 1, 0)
        o_ref[...] = (yp * s_ref[...]).astype(BF16)

    return pl.pallas_call(
        body,
        name=name,
        grid=(1,),
        in_specs=[
            pl.BlockSpec((S, POOL_W), lambda i: (0, OFF_A // POOL_W)),
            pl.BlockSpec((POOL_W, POOL_W), lambda i: (0, 0)),
            pl.BlockSpec((1, POOL_W), lambda i: (0, 0)),
        ],
        out_specs=pl.BlockSpec((S, POOL_W), lambda i: (0, 0)),
        out_shape=jax.ShapeDtypeStruct((S, POOL_W), BF16),
        compiler_params=_params(("arbitrary",)),
    )(rest, wbd, scale.reshape(1, POOL_W))


def _pool_bwd(rest, wbd, wbd_t, scale, dpa, *, name):
    S = rest.shape[0]

    def body(a_ref, w_ref, wt_ref, s_ref, dpa_ref, da_ref, dw_ref, ds_ref):
        d, cnt = _pool_diff(a_ref[...])
        db = d.astype(BF16)
        yp = _dot(db, w_ref[...], 1, 0)
        dpa_v = dpa_ref[...]
        ds_ref[...] = jnp.sum(dpa_v * yp, axis=0, keepdims=True)
        dyp = (dpa_v * s_ref[...]).astype(BF16)
        dw_ref[...] = _dot(db, dyp, 0, 0)
        dd = _dot(dyp, wt_ref[...], 1, 0)
        da_ref[...] = _pool_diff_t(dd, cnt).astype(BF16)

    full = pl.BlockSpec((S, POOL_W), lambda i: (0, 0))
    sq = pl.BlockSpec((POOL_W, POOL_W), lambda i: (0, 0))
    vec = pl.BlockSpec((1, POOL_W), lambda i: (0, 0))
    return pl.pallas_call(
        body,
        name=name,
        grid=(1,),
        in_specs=[pl.BlockSpec((S, POOL_W), lambda i: (0, OFF_A // POOL_W)), sq, sq, vec, full],
        out_specs=[full, sq, vec],
        out_shape=[
            jax.ShapeDtypeStruct((S, POOL_W), BF16),
            jax.ShapeDtypeStruct((POOL_W, POOL_W), F32),
            jax.ShapeDtypeStruct((1, POOL_W), F32),
        ],
        compiler_params=_params(("arbitrary",)),
    )(rest, wbd, wbd_t, scale.reshape(1, POOL_W), dpa)


def _log_sigmoid(z):
    return jnp.minimum(z, 0.0) - jnp.log(1.0 + jnp.exp(-jnp.abs(z)))


_F_SPEC_COL = OFF_F // F_LANES


def _fox_prep(rest, bpad, *, name):
    S = rest.shape[0]

    def body(f_ref, b_ref, o_ref, ot_ref):
        acc = _log_sigmoid(f_ref[...] + b_ref[...])
        row = _rows(acc.shape)
        k = 1
        while k < S:
            acc = acc + jnp.where(row >= k, pltpu.roll(acc, k, 0), 0.0)
            k *= 2
        o_ref[...] = acc
        ot_ref[...] = acc.T

    return pl.pallas_call(
        body,
        name=name,
        grid=(1,),
        in_specs=[pl.BlockSpec((S, F_LANES), lambda i: (0, _F_SPEC_COL)), pl.BlockSpec((1, F_LANES), lambda i: (0, 0))],
        out_specs=[pl.BlockSpec((S, F_LANES), lambda i: (0, 0)), pl.BlockSpec((F_LANES, S), lambda i: (0, 0))],
        out_shape=[jax.ShapeDtypeStruct((S, F_LANES), F32), jax.ShapeDtypeStruct((F_LANES, S), F32)],
        compiler_params=_params(("arbitrary",)),
    )(rest, bpad)


def _fox_post(rest, bpad, dcum, *, name):
    S = rest.shape[0]

    def body(f_ref, b_ref, d_ref, df_ref, db_ref):
        acc = d_ref[...]
        row = _rows(acc.shape)
        k = 1
        while k < S:
            acc = acc + jnp.where(row < S - k, pltpu.roll(acc, S - k, 0), 0.0)
            k *= 2
        df = acc * (1.0 - _sigmoid(f_ref[...] + b_ref[...]))
        df_ref[...] = df.astype(BF16)
        db_ref[...] = jnp.sum(df, axis=0, keepdims=True)

    full = pl.BlockSpec((S, F_LANES), lambda i: (0, 0))
    vec = pl.BlockSpec((1, F_LANES), lambda i: (0, 0))
    return pl.pallas_call(
        body,
        name=name,
        grid=(1,),
        in_specs=[pl.BlockSpec((S, F_LANES), lambda i: (0, _F_SPEC_COL)), vec, full],
        out_specs=[full, vec],
        out_shape=[jax.ShapeDtypeStruct((S, F_LANES), BF16), jax.ShapeDtypeStruct((1, F_LANES), F32)],
        compiler_params=_params(("arbitrary",)),
    )(rest, bpad, dcum)


_FOX_SCALE = FOX_DH ** -0.5
_PAIRS = FOX_H // 2


def _scaled(v):
    return (v.astype(F32) * _FOX_SCALE).astype(BF16)


def _head_lane(cum, h):
    return jnp.sum(jnp.where(_lanes(cum.shape) == h, cum, 0.0), axis=-1, keepdims=True)


def _diag_mask(s):
    return jnp.where(_rows(s.shape) >= _lanes(s.shape), s, NEG)


def _fox_fwd(qkv, cum, fk3, *, name):
    S = qkv.shape[0]
    nk, t = fk3.shape[1:]

    def body(q_ref, k_ref, v_ref, cum_ref, fk_ref, o_ref, lse_ref, m_sc, l_sc, acc_sc):
        hp, i = pl.program_id(0), pl.program_id(1)
        lane = _lanes((t, 128))
        lo = lane < FOX_DH
        qs = _scaled(q_ref[...])
        zero = jnp.zeros_like(qs)
        qm = (jnp.where(lo, qs, zero), jnp.where(lo, zero, qs))
        cumv = cum_ref[...]
        fq = [_head_lane(cumv, 2 * hp + e) for e in range(2)]
        m_sc[...] = jnp.full(m_sc.shape, NEG, F32)
        l_sc[...] = jnp.zeros(l_sc.shape, F32)
        acc_sc[...] = jnp.zeros(acc_sc.shape, F32)

        def tile(j, masked):
            k0 = pl.multiple_of(j * t, t)
            kb = k_ref[pl.ds(k0, t), :]
            vb = v_ref[pl.ds(k0, t), :]
            alphas, pvs = [], []
            for e in range(2):
                s = _dot(qm[e], kb, 1, 1) + fq[e] - fk_ref[2 * hp + e, pl.ds(j, 1), :]
                if masked:
                    s = _diag_mask(s)
                m_old = m_sc[e]
                m_new = jnp.maximum(m_old, jnp.max(s, axis=-1, keepdims=True))
                p = jnp.exp(s - m_new)
                alpha = jnp.exp(m_old - m_new)
                l_sc[e] = alpha * l_sc[e] + jnp.sum(p, axis=-1, keepdims=True)
                m_sc[e] = m_new
                alphas.append(alpha)
                pvs.append(_dot(p.astype(BF16), vb, 1, 0))
            acc_sc[...] = jnp.where(lo, alphas[0], alphas[1]) * acc_sc[...] + jnp.where(lo, pvs[0], pvs[1])

        def step(j, carry):
            tile(j, False)
            return carry

        lax.fori_loop(0, i, step, 0)
        tile(i, True)
        o_ref[...] = acc_sc[...] / jnp.where(lo, l_sc[0], l_sc[1])
        lse = [m_sc[e] + jnp.log(l_sc[e]) for e in range(2)]
        lse_ref[...] = jnp.where(lane == 0, lse[0], jnp.where(lane == 1, lse[1], 0.0))

    return pl.pallas_call(
        body,
        name=name,
        grid=(_PAIRS, S // t),
        in_specs=[
            pl.BlockSpec((t, 128), lambda hp, i: (i, hp)),
            pl.BlockSpec((S, 128), lambda hp, i: (0, _PAIRS + hp)),
            pl.BlockSpec((S, 128), lambda hp, i: (0, 2 * _PAIRS + hp)),
            pl.BlockSpec((t, F_LANES), lambda hp, i: (i, 0)),
            pl.BlockSpec((FOX_H, nk, t), lambda hp, i: (0, 0, 0)),
        ],
        out_specs=[pl.BlockSpec((t, 128), lambda hp, i: (i, hp)), pl.BlockSpec((None, t, 128), lambda hp, i: (hp, i, 0))],
        out_shape=[jax.ShapeDtypeStruct((S, FOX_W), F32), jax.ShapeDtypeStruct((_PAIRS, S, 128), F32)],
        scratch_shapes=[pltpu.VMEM((2, t, 1), F32), pltpu.VMEM((2, t, 1), F32), pltpu.VMEM((t, 128), F32)],
        compiler_params=_params(("parallel", "arbitrary")),
    )(qkv, qkv, qkv, cum, fk3)


def _fox_bwd(qkv, cum, fk3, o, do, lse, *, name):
    S = qkv.shape[0]
    nk, t = fk3.shape[1:]

    def body(q_ref, k_ref, v_ref, cum_ref, fk_ref, o_ref, do_ref, lse_ref, dq_ref, dk_ref, dv_ref, dfq_ref, dfk_ref,
             qm_sc, km_sc, dom_sc, delta_sc, fq_sc, dfq_sc, dq_sc):
        hp = pl.program_id(0)
        lane = _lanes((t, 128))
        lo = lane < FOX_DH

        def prep(i, carry):
            r = pl.ds(pl.multiple_of(i * t, t), t)
            qs, ks, dob = _scaled(q_ref[r, :]), _scaled(k_ref[r, :]), do_ref[r, :]
            prod = dob.astype(F32) * o_ref[r, :]
            cumv = cum_ref[r, :]
            zero = jnp.zeros_like(qs)
            for e in range(2):
                mine = lo if e == 0 else jnp.logical_not(lo)
                qm_sc[e, r, :] = jnp.where(mine, qs, zero)
                km_sc[e, r, :] = jnp.where(mine, ks, zero)
                dom_sc[e, r, :] = jnp.where(mine, dob, zero)
                delta_sc[e, r, :] = jnp.sum(jnp.where(mine, prod, 0.0), axis=-1, keepdims=True)
                fq_sc[e, r, :] = _head_lane(cumv, 2 * hp + e)
                dfq_sc[e, r, :] = jnp.zeros((t, 1), F32)
            dq_sc[r, :] = jnp.zeros((t, 128), F32)
            return carry

        lax.fori_loop(0, nk, prep, 0)

        def kv_tile(j, carry):
            kr = pl.ds(pl.multiple_of(j * t, t), t)
            kb, vb = k_ref[kr, :], v_ref[kr, :]
            fks = [fk_ref[2 * hp + e, pl.ds(j, 1), :] for e in range(2)]

            def q_tile(i, acc, masked):
                dk, dv, dfk0, dfk1 = acc
                dfk = [dfk0, dfk1]
                qr = pl.ds(pl.multiple_of(i * t, t), t)
                dq_t = jnp.zeros((t, 128), F32)
                for e in range(2):
                    qe, doe = qm_sc[e, qr, :], dom_sc[e, qr, :]
                    s = _dot(qe, kb, 1, 1) + fq_sc[e, qr, :] - fks[e]
                    if masked:
                        s = _diag_mask(s)
                    p = jnp.exp(s - lse_ref[qr, e:e + 1])
                    dv = dv + _dot(p.astype(BF16), doe, 0, 0)
                    dp = _dot(doe, vb, 1, 1)
                    ds = p * (dp - delta_sc[e, qr, :])
                    dsb = ds.astype(BF16)
                    dk = dk + _dot(dsb, qe, 0, 0)
                    dq_t = dq_t + _dot(dsb, km_sc[e, kr, :], 1, 0)
                    dfq_sc[e, qr, :] += jnp.sum(ds, axis=-1, keepdims=True)
                    dfk[e] = dfk[e] - jnp.sum(ds, axis=0, keepdims=True)
                dq_sc[qr, :] += dq_t
                return dk, dv, dfk[0], dfk[1]

            init = (jnp.zeros((t, 128), F32), jnp.zeros((t, 128), F32), jnp.zeros((1, t), F32), jnp.zeros((1, t), F32))
            acc = q_tile(j, init, True)
            dk, dv, dfk0, dfk1 = lax.fori_loop(j + 1, nk, functools.partial(q_tile, masked=False), acc)
            dk_ref[kr, :] = dk.astype(BF16)
            dv_ref[kr, :] = dv.astype(BF16)
            dfk_ref[0, pl.ds(j, 1), :] = dfk0
            dfk_ref[1, pl.ds(j, 1), :] = dfk1
            return carry

        lax.fori_loop(0, nk, kv_tile, 0)
        dq_ref[...] = dq_sc[...].astype(BF16)
        lane_s = _lanes((S, 128))
        dfq_ref[...] = jnp.where(lane_s == 0, dfq_sc[0], jnp.where(lane_s == 1, dfq_sc[1], 0.0))

    col = lambda c0: pl.BlockSpec((S, 128), lambda hp: (0, c0 + hp))
    pair = pl.BlockSpec((S, 128), lambda hp: (0, hp))
    lanes3 = pl.BlockSpec((None, S, 128), lambda hp: (hp, 0, 0))
    big = jax.ShapeDtypeStruct((S, FOX_W), BF16)
    masked_bf16 = pltpu.VMEM((2, S, 128), BF16)
    column = pltpu.VMEM((2, S, 1), F32)
    return pl.pallas_call(
        body,
        name=name,
        grid=(_PAIRS,),
        in_specs=[
            col(0), col(_PAIRS), col(2 * _PAIRS),
            pl.BlockSpec((S, F_LANES), lambda hp: (0, 0)),
            pl.BlockSpec((FOX_H, nk, t), lambda hp: (0, 0, 0)),
            pair, pair, lanes3,
        ],
        out_specs=[pair, pair, pair, lanes3, pl.BlockSpec((None, 2, nk, t), lambda hp: (hp, 0, 0, 0))],
        out_shape=[big, big, big, jax.ShapeDtypeStruct((_PAIRS, S, 128), F32), jax.ShapeDtypeStruct((_PAIRS, 2, nk, t), F32)],
        scratch_shapes=[masked_bf16, masked_bf16, masked_bf16, column, column, column, pltpu.VMEM((S, 128), F32)],
        compiler_params=_params(("parallel",)),
    )(qkv, qkv, qkv, cum, fk3, o, do, lse)


def _group_mask(lane, gi):
    return (lane >= 64 * gi) & (lane < 64 * (gi + 1))


_U_COL = OFF_C // SGU_W


def _sgu_fwd(rest, gn, wm, bias, *, name):
    S = rest.shape[0]
    ts = _tile(S, 512)
    nc = ts // SGU_CHUNK

    def body(u_ref, v_ref, g_ref, w_ref, b_ref, o_ref):
        zv = _gelu(v_ref[...])
        vn = zv * lax.rsqrt(jnp.mean(zv * zv, axis=-1, keepdims=True) + EPS) * g_ref[...]
        lane = _lanes((SGU_CHUNK, SGU_W))
        for c in range(nc):
            rows = slice(c * SGU_CHUNK, (c + 1) * SGU_CHUNK)
            vcb = vn[rows].astype(BF16)
            mixed = b_ref[...]
            for gi in range(4):
                mixed = mixed + jnp.where(_group_mask(lane, gi), _dot(w_ref[gi], vcb, 1, 0), 0.0)
            o_ref[rows, :] = (_gelu(u_ref[rows, :]) * mixed).astype(BF16)

    return pl.pallas_call(
        body,
        name=name,
        grid=(S // ts,),
        in_specs=[
            pl.BlockSpec((ts, SGU_W), lambda i: (i, _U_COL)),
            pl.BlockSpec((ts, SGU_W), lambda i: (i, _U_COL + 1)),
            pl.BlockSpec((1, SGU_W), lambda i: (0, 0)),
            pl.BlockSpec((4, SGU_CHUNK, SGU_CHUNK), lambda i: (0, 0, 0)),
            pl.BlockSpec((SGU_CHUNK, SGU_W), lambda i: (0, 0)),
        ],
        out_specs=pl.BlockSpec((ts, SGU_W), lambda i: (i, 0)),
        out_shape=jax.ShapeDtypeStruct((S, SGU_W), BF16),
        compiler_params=_params(("parallel",)),
    )(rest, rest, gn.reshape(1, SGU_W), wm, bias)


def _sgu_bwd(rest, gn, wm, wm_t, bias, dsg, *, name):
    S = rest.shape[0]
    ts = _tile(S, 512)
    nc = ts // SGU_CHUNK

    def body(u_ref, v_ref, g_ref, w_ref, wt_ref, b_ref, dsg_ref, dc_ref, dw_ref, db_ref, dg_ref):
        first = pl.program_id(0) == 0

        @pl.when(first)
        def _():
            dw_ref[...] = jnp.zeros_like(dw_ref)
            db_ref[...] = jnp.zeros_like(db_ref)
            dg_ref[...] = jnp.zeros_like(dg_ref)

        gv = g_ref[...]
        lane = _lanes((SGU_CHUNK, SGU_W))
        for c in range(nc):
            rows = slice(c * SGU_CHUNK, (c + 1) * SGU_CHUNK)
            vpre = v_ref[rows, :]
            upre = u_ref[rows, :]
            zv = _gelu(vpre)
            r = lax.rsqrt(jnp.mean(zv * zv, axis=-1, keepdims=True) + EPS)
            zn = zv * r
            vcb = (zn * gv).astype(BF16)
            mixed = b_ref[...]
            for gi in range(4):
                mixed = mixed + jnp.where(_group_mask(lane, gi), _dot(w_ref[gi], vcb, 1, 0), 0.0)
            zu = _gelu(upre)
            dsg_v = dsg_ref[rows, :]
            dc_ref[rows, :SGU_W] = (dsg_v * mixed * _gelu_grad(upre)).astype(BF16)
            dmixed = dsg_v * zu
            db_ref[...] += dmixed
            dvn = jnp.zeros((SGU_CHUNK, SGU_W), F32)
            for gi in range(4):
                dmg = jnp.where(_group_mask(lane, gi), dmixed, 0.0).astype(BF16)
                dw_ref[gi] += _dot(dmg, vcb, 1, 1)
                dvn = dvn + _dot(wt_ref[gi], dmg, 1, 0)
            dg_ref[...] += jnp.sum(dvn * zn, axis=0, keepdims=True)
            dzn = dvn * gv
            dzv = r * (dzn - zn * jnp.mean(dzn * zn, axis=-1, keepdims=True))
            dc_ref[rows, SGU_W:] = (dzv * _gelu_grad(vpre)).astype(BF16)

    blk = pl.BlockSpec((ts, SGU_W), lambda i: (i, 0))
    vec = pl.BlockSpec((1, SGU_W), lambda i: (0, 0))
    w3 = pl.BlockSpec((4, SGU_CHUNK, SGU_CHUNK), lambda i: (0, 0, 0))
    bsp = pl.BlockSpec((SGU_CHUNK, SGU_W), lambda i: (0, 0))
    return pl.pallas_call(
        body,
        name=name,
        grid=(S // ts,),
        in_specs=[
            pl.BlockSpec((ts, SGU_W), lambda i: (i, _U_COL)),
            pl.BlockSpec((ts, SGU_W), lambda i: (i, _U_COL + 1)),
            vec, w3, w3, bsp, blk,
        ],
        out_specs=[pl.BlockSpec((ts, 2 * SGU_W), lambda i: (i, 0)), w3, bsp, vec],
        out_shape=[
            jax.ShapeDtypeStruct((S, 2 * SGU_W), BF16),
            jax.ShapeDtypeStruct((4, SGU_CHUNK, SGU_CHUNK), F32),
            jax.ShapeDtypeStruct((SGU_CHUNK, SGU_W), F32),
            jax.ShapeDtypeStruct((1, SGU_W), F32),
        ],
        compiler_params=_params(("arbitrary",)),
    )(rest, rest, gn.reshape(1, SGU_W), wm, wm_t, bias, dsg)


_GT = 512
_G0 = OFF_G // _GT


def _gate_specs(tm, col_of):
    specs = [pl.BlockSpec((tm, _GT), functools.partial(lambda k, *ids: (col_of(*ids)[0], _G0 + 2 * k + col_of(*ids)[1]), k)) for k in range(3)]
    specs += [pl.BlockSpec((1, _GT), functools.partial(lambda k, *ids: (0, 2 * k + col_of(*ids)[1]), k)) for k in range(3)]
    return specs


def _merge_fwd(rest, bg, ya, yb, yc, *, name):
    S = rest.shape[0]
    tm = _tile(S, 512)

    def body(g1, g2, g3, b1, b2, b3, ya_ref, yb_ref, yc_ref, o_ref):
        acc = _sigmoid(g1[...] + b1[...]) * ya_ref[...]
        acc = acc + _sigmoid(g2[...] + b2[...]) * yb_ref[...]
        acc = acc + _sigmoid(g3[...] + b3[...]) * yc_ref[...]
        o_ref[...] = acc.astype(BF16)

    blk = pl.BlockSpec((tm, _GT), lambda i, j: (i, j))
    return pl.pallas_call(
        body,
        name=name,
        grid=(S // tm, D // _GT),
        in_specs=_gate_specs(tm, lambda i, j: (i, j)) + [blk, blk, blk],
        out_specs=blk,
        out_shape=jax.ShapeDtypeStruct((S, D), BF16),
        compiler_params=_params(("parallel", "parallel")),
    )(rest, rest, rest, bg, bg, bg, ya, yb, yc)


def _merge_bwd(rest, bg, ya, yb, yc, dm, *, name):
    S = rest.shape[0]
    tm = _tile(S, 512)

    def body(g1, g2, g3, b1, b2, b3, ya_ref, yb_ref, yc_ref, dm_ref, dya, dyb, dyc, dg1, dg2, dg3, db1, db2, db3):
        first = pl.program_id(1) == 0
        dmv = dm_ref[...]
        for g_ref, b_ref, y_ref, dy_ref, dg_ref, db_ref in (
            (g1, b1, ya_ref, dya, dg1, db1), (g2, b2, yb_ref, dyb, dg2, db2), (g3, b3, yc_ref, dyc, dg3, db3)):
            gate = _sigmoid(g_ref[...] + b_ref[...])
            dy_ref[...] = (dmv * gate).astype(BF16)
            dpre = dmv * y_ref[...] * gate * (1.0 - gate)
            dg_ref[...] = dpre.astype(BF16)
            part = jnp.sum(dpre, axis=0, keepdims=True)

            @pl.when(first)
            def _():
                db_ref[...] = part

            @pl.when(jnp.logical_not(first))
            def _():
                db_ref[...] += part

    blk = pl.BlockSpec((tm, _GT), lambda j, i: (i, j))
    vec = pl.BlockSpec((1, _GT), lambda j, i: (0, j))
    big = jax.ShapeDtypeStruct((S, D), BF16)
    small = jax.ShapeDtypeStruct((1, D), F32)
    return pl.pallas_call(
        body,
        name=name,
        grid=(D // _GT, S // tm),
        in_specs=_gate_specs(tm, lambda j, i: (i, j)) + [blk, blk, blk, blk],
        out_specs=[blk] * 6 + [vec] * 3,
        out_shape=[big] * 6 + [small] * 3,
        compiler_params=_params(("parallel", "arbitrary")),
    )(rest, rest, rest, bg, bg, bg, ya, yb, yc, dm)


_X_SCALE = XDH ** -0.5


def _xattn_fwd(xq, kv, *, name):
    S = xq.shape[0]
    M = kv.shape[0]
    tq = _tile(S, 512)

    def body(q_ref, k_ref, v_ref, o_ref):
        s = _dot(q_ref[...], k_ref[...], 1, 1) * _X_SCALE
        e = jnp.exp(s - jnp.max(s, axis=-1, keepdims=True))
        p = e / jnp.sum(e, axis=-1, keepdims=True)
        o_ref[...] = _dot(p.astype(BF16), v_ref[...], 1, 0).astype(BF16)

    return pl.pallas_call(
        body,
        name=name,
        grid=(S // tq, XH),
        in_specs=[
            pl.BlockSpec((tq, XDH), lambda i, h: (i, h)),
            pl.BlockSpec((M, XDH), lambda i, h: (0, h)),
            pl.BlockSpec((M, XDH), lambda i, h: (0, XH + h)),
        ],
        out_specs=pl.BlockSpec((tq, XDH), lambda i, h: (i, h)),
        out_shape=jax.ShapeDtypeStruct((S, D), BF16),
        compiler_params=_params(("parallel", "parallel")),
    )(xq, kv, kv)


def _xattn_bwd(xq, kv, do, *, name):
    S = xq.shape[0]
    M = kv.shape[0]
    tq = _tile(S, 512)

    def body(q_ref, k_ref, v_ref, do_ref, dq_ref, dk_ref, dv_ref):
        qb = q_ref[...]
        kb = k_ref[...]
        dob = do_ref[...]
        s = _dot(qb, kb, 1, 1) * _X_SCALE
        e = jnp.exp(s - jnp.max(s, axis=-1, keepdims=True))
        p = e / jnp.sum(e, axis=-1, keepdims=True)
        dp = _dot(dob, v_ref[...], 1, 1)
        ds = (p * (dp - jnp.sum(p * dp, axis=-1, keepdims=True)) * _X_SCALE).astype(BF16)
        dq_ref[...] = _dot(ds, kb, 1, 0).astype(BF16)
        dk_part = _dot(ds, qb, 0, 0)
        dv_part = _dot(p.astype(BF16), dob, 0, 0)

        @pl.when(pl.program_id(1) == 0)
        def _():
            dk_ref[...] = dk_part
            dv_ref[...] = dv_part

        @pl.when(pl.program_id(1) > 0)
        def _():
            dk_ref[...] += dk_part
            dv_ref[...] += dv_part

    qspec = pl.BlockSpec((tq, XDH), lambda h, i: (i, h))
    kspec = pl.BlockSpec((M, XDH), lambda h, i: (0, h))
    dxq, dxk, dxv = pl.pallas_call(
        body,
        name=name,
        grid=(XH, S // tq),
        in_specs=[qspec, kspec, pl.BlockSpec((M, XDH), lambda h, i: (0, XH + h)), qspec],
        out_specs=[qspec, kspec, kspec],
        out_shape=[jax.ShapeDtypeStruct((S, D), BF16), jax.ShapeDtypeStruct((M, D), F32), jax.ShapeDtypeStruct((M, D), F32)],
        compiler_params=_params(("parallel", "arbitrary")),
    )(xq, kv, kv, do)
    return dxq, jnp.concatenate([dxk, dxv], axis=1)


def _adam_math(w, g, m, v):
    m = ADAM_B1 * m + (1.0 - ADAM_B1) * g
    v = ADAM_B2 * v + (1.0 - ADAM_B2) * (g * g)
    m_hat = m / (1.0 - ADAM_B1 ** ADAM_STEP)
    v_hat = v / (1.0 - ADAM_B2 ** ADAM_STEP)
    delta = -ADAM_LR * (m_hat / (jnp.sqrt(v_hat) + ADAM_EPS) + ADAM_WD * w)
    return delta, m, v


def _adamw_sharded(parts, w, m, v, *, name):
    _, R, C = w.shape
    Cp = parts[0].shape[2]
    tm = _tile(R, 256)
    nr = R // tm

    def body(p0_ref, p1_ref, w_ref, m_ref, v_ref, g_ref, d_ref, mo_ref, vo_ref):
        def update(p_ref):
            g = p_ref[0][:, :C].astype(F32)
            for dev in range(1, N_DEV):
                g = g + p_ref[dev][:, :C].astype(F32)
            delta, mn, vn = _adam_math(w_ref[...], g, m_ref[...], v_ref[...])
            g_ref[...] = g
            d_ref[...] = delta
            mo_ref[...] = mn
            vo_ref[...] = vn

        @pl.when(pl.program_id(0) == 0)
        def _():
            update(p0_ref)

        @pl.when(pl.program_id(0) == 1)
        def _():
            update(p1_ref)

    p0 = pl.BlockSpec((N_DEV, tm, Cp), lambda l, i: (0, i * (1 - l) + (nr - 1) * l, 0))
    p1 = pl.BlockSpec((N_DEV, tm, Cp), lambda l, i: (0, i * l, 0))
    blk = pl.BlockSpec((None, tm, C), lambda l, i: (l, i, 0))
    sds = jax.ShapeDtypeStruct(w.shape, F32)
    return pl.pallas_call(
        body,
        name=name,
        grid=(DEPTH, nr),
        in_specs=[p0, p1, blk, blk, blk],
        out_specs=[blk] * 4,
        out_shape=[sds] * 4,
        compiler_params=_params(("arbitrary", "arbitrary")),
    )(parts[0], parts[1], w, m, v)


def _adamw_small(g, w, m, v, *, name):
    n = len(g)

    def body(*refs):
        g_refs, w_refs, m_refs, v_refs = (refs[k * n:(k + 1) * n] for k in range(4))
        d_out, m_out, v_out = (refs[(4 + k) * n:(5 + k) * n] for k in range(3))
        for t in range(n):
            delta, mn, vn = _adam_math(w_refs[t][...], g_refs[t][...], m_refs[t][...], v_refs[t][...])
            d_out[t][...] = delta
            m_out[t][...] = mn
            v_out[t][...] = vn

    vm = pl.BlockSpec(memory_space=pltpu.VMEM)
    shapes = [jax.ShapeDtypeStruct(a.shape, F32) for a in w]
    outs = pl.pallas_call(
        body,
        name=name,
        in_specs=[vm] * (4 * n),
        out_specs=[vm] * (3 * n),
        out_shape=shapes * 3,
        compiler_params=pltpu.CompilerParams(vmem_limit_bytes=VMEM_LIMIT),
    )(*g, *w, *m, *v)
    return outs[:n], outs[n:2 * n], outs[2 * n:]


def _position():
    return lax.axis_index("x"), lax.axis_index("y"), lax.axis_index("c")


def _dev_index(px, py, pc):
    return 4 * px + 2 * py + pc


_ANY = pl.BlockSpec(memory_space=pl.ANY)


def _all_gather(shards, *, name):
    n = len(shards)

    def body(*refs):
        ins, outs = refs[:n], refs[n:2 * n]
        send_sems, recv_sems, local_sems = refs[2 * n:]
        x, y, c = _position()
        me, sibling = (x, y, c), (x, y, 1 - c)
        chips = [(1 - x, y), (x, 1 - y), (1 - x, 1 - y)]

        def copy(t, k, block, to, src=None):
            dst = outs[t].at[_dev_index(*block)]
            return pltpu.make_async_remote_copy(
                src_ref=dst if src is None else src, dst_ref=dst, send_sem=send_sems.at[t, k], recv_sem=recv_sems.at[t, k],
                device_id=to, device_id_type=MESH)

        mine = [pltpu.make_async_copy(ins[t], outs[t].at[_dev_index(*me)], local_sems.at[t]) for t in range(n)]
        for cp in mine:
            cp.start()
        started = []
        for j, chip in enumerate(chips):
            for t in range(n):
                started.append(copy(t, 1 + j, me, (*chip, c), src=ins[t]))
                started[-1].start()
        for t in range(n):
            started.append(copy(t, 0, me, sibling, src=ins[t]))
            started[-1].start()
        for j, chip in enumerate(chips):
            for t in range(n):
                copy(t, 1 + j, (*chip, c), me).wait_recv()
                started.append(copy(t, 4 + j, (*chip, c), sibling))
                started[-1].start()
        for t in range(n):
            copy(t, 0, sibling, me).wait_recv()
        for j, chip in enumerate(chips):
            for t in range(n):
                copy(t, 4 + j, (*chip, 1 - c), me).wait_recv()
        for cp in started:
            cp.wait_send()
        for cp in mine:
            cp.wait()

    return pl.pallas_call(
        body,
        name=name,
        in_specs=[_ANY] * n,
        out_specs=[_ANY] * n,
        out_shape=[jax.ShapeDtypeStruct((N_DEV, *s.shape), s.dtype) for s in shards],
        scratch_shapes=[pltpu.SemaphoreType.DMA((n, 7)), pltpu.SemaphoreType.DMA((n, 7)), pltpu.SemaphoreType.DMA((n,))],
        compiler_params=pltpu.CompilerParams(has_side_effects=True),
    )(*shards)


def _peers(x, y, c):
    out = []
    for mask in range(1, N_DEV):
        fx, fy, fc = (mask >> 2) & 1, (mask >> 1) & 1, mask & 1
        out.append((1 - x if fx else x, 1 - y if fy else y, 1 - c if fc else c))
    return out


_HBM = pl.BlockSpec(memory_space=pltpu.HBM)
_SEM = pl.BlockSpec(memory_space=pltpu.SEMAPHORE)


def _own_block_placed(block, like):
    x, y, c = _position()
    return lax.dynamic_update_index_in_dim(lax.empty(like.shape, like.dtype), block, _dev_index(x, y, c), 0)


def _copies(per_array):
    def mark(fn):
        fn.per_array = per_array
        return fn
    return mark


@_copies(N_DEV - 1)
def _plan_exchange(srcs, lands, send_sems, recv_sems, arrivals):
    x, y, c = _position()
    me = _dev_index(x, y, c)
    out = []
    for k, peer in enumerate(_peers(x, y, c)):
        p = _dev_index(*peer)
        for t in range(len(lands)):
            sems = dict(send_sem=send_sems.at[7 * t + k], recv_sem=recv_sems.at[7 * t + k], device_id=peer, device_id_type=MESH)
            src, dst = (lands[t].at[p], lands[t].at[p]) if arrivals else (srcs[t].at[p], lands[t].at[me])
            out.append(pltpu.make_async_remote_copy(src_ref=src, dst_ref=dst, **sems))
    return out


@_copies(4)
def _plan_gather_out(srcs, lands, send_sems, recv_sems, arrivals):
    x, y, c = _position()
    me = _dev_index(x, y, c)
    out = []
    for k, peer in enumerate([(x, y, 1 - c), (1 - x, y, c), (x, 1 - y, c), (1 - x, 1 - y, c)]):
        p = _dev_index(*peer)
        for t in range(len(lands)):
            sems = dict(send_sem=send_sems.at[4 * t + k], recv_sem=recv_sems.at[4 * t + k], device_id=peer, device_id_type=MESH)
            src, dst = (lands[t].at[p], lands[t].at[p]) if arrivals else (srcs[t], lands[t].at[me])
            out.append(pltpu.make_async_remote_copy(src_ref=src, dst_ref=dst, **sems))
    return out


@_copies(3)
def _plan_gather_pass(srcs, lands, send_sems, recv_sems, arrivals):
    x, y, c = _position()
    sibling = (x, y, 1 - c)
    out = []
    for k, chip in enumerate([(1 - x, y), (x, 1 - y), (1 - x, 1 - y)]):
        p = _dev_index(*chip, 1 - c) if arrivals else _dev_index(*chip, c)
        for t in range(len(lands)):
            sems = dict(send_sem=send_sems.at[3 * t + k], recv_sem=recv_sems.at[3 * t + k], device_id=sibling, device_id_type=MESH)
            out.append(pltpu.make_async_remote_copy(src_ref=lands[t].at[p], dst_ref=lands[t].at[p], **sems))
    return out


def _split_start(plan, srcs, lands, *, after=None, name):
    n_src, n = len(srcs), len(srcs) + len(lands)
    n_sem = plan.per_array * len(lands)
    order = [] if after is None else [after]

    def body(*refs):
        send_sems, recv_sems = refs[n + len(order):n + len(order) + 2]
        token = refs[-1]
        for cp in plan(refs[:n_src], refs[n_src:n], send_sems, recv_sems, arrivals=False):
            cp.start()
        token[...] = jnp.zeros_like(token)

    hbm = lambda a: pltpu.HBM(a.shape, a.dtype)
    outs = pl.pallas_call(
        body,
        name=name,
        in_specs=[_HBM] * n + [_ANY] * len(order),
        out_specs=[_SEM, _SEM] + [_HBM] * n + [pl.BlockSpec(memory_space=pltpu.VMEM)],
        out_shape=[pltpu.SemaphoreType.DMA((n_sem,)), pltpu.SemaphoreType.DMA((n_sem,))] + [hbm(a) for a in (*srcs, *lands)]
        + [jax.ShapeDtypeStruct(_TOKEN, F32)],
        input_output_aliases={i: 2 + i for i in range(n)},
        compiler_params=pltpu.CompilerParams(has_side_effects=pltpu.SideEffectType.DATAFLOW_SIDE_EFFECTING),
    )(*[pltpu.with_memory_space_constraint(a, pltpu.HBM) for a in (*srcs, *lands)], *order)
    return (outs[0], outs[1], outs[2:2 + n_src], outs[2 + n_src:2 + n]), outs[-1]


def _split_wait(plan, state, after, *, name):
    send_sems, recv_sems, srcs, lands = state
    n_src, n = len(srcs), len(srcs) + len(lands)

    def body(*refs):
        send_refs, recv_refs = refs[n:n + 2]
        for cp in plan(refs[:n_src], refs[n_src:n], send_refs, recv_refs, arrivals=False):
            cp.wait_send()
        for cp in plan(refs[:n_src], refs[n_src:n], send_refs, recv_refs, arrivals=True):
            cp.wait_recv()

    hbm = lambda a: pltpu.HBM(a.shape, a.dtype)
    outs = pl.pallas_call(
        body,
        name=name,
        in_specs=[_HBM] * n + [_SEM, _SEM, _ANY],
        out_specs=[_HBM] * n,
        out_shape=[hbm(a) for a in (*srcs, *lands)],
        input_output_aliases={i: i for i in range(n)},
        compiler_params=pltpu.CompilerParams(has_side_effects=pltpu.SideEffectType.DATAFLOW_SIDE_EFFECTING),
    )(*srcs, *lands, send_sems, recv_sems, after)
    return outs[n_src:]


def _all_reduce(g_local, *, name):
    R, C = g_local.shape

    def body(g_ref, o_ref, buf, send_sems, recv_sems):
        x, y, c = _position()
        me = _dev_index(x, y, c)
        peers = _peers(x, y, c)
        copies = []
        for k, peer in enumerate(peers):
            copies.append(pltpu.make_async_remote_copy(
                src_ref=g_ref, dst_ref=buf.at[me], send_sem=send_sems.at[k], recv_sem=recv_sems.at[k],
                device_id=peer, device_id_type=MESH))
            copies[-1].start()
        buf[me] = g_ref[...]
        for k, peer in enumerate(peers):
            dst = buf.at[_dev_index(*peer)]
            pltpu.make_async_remote_copy(
                src_ref=dst, dst_ref=dst, send_sem=send_sems.at[k], recv_sem=recv_sems.at[k],
                device_id=peer, device_id_type=MESH).wait_recv()
        for cp in copies:
            cp.wait_send()
        g = buf[0]
        for dev in range(1, N_DEV):
            g = g + buf[dev]
        o_ref[...] = g

    vm = pl.BlockSpec(memory_space=pltpu.VMEM)
    return pl.pallas_call(
        body,
        name=name,
        in_specs=[vm],
        out_specs=vm,
        out_shape=jax.ShapeDtypeStruct((R, C), F32),
        scratch_shapes=[pltpu.VMEM((N_DEV, R, C), F32), pltpu.SemaphoreType.DMA((7,)), pltpu.SemaphoreType.DMA((7,))],
        compiler_params=pltpu.CompilerParams(has_side_effects=True, vmem_limit_bytes=VMEM_LIMIT),
    )(g_local)


def _block_diag(w):
    out = jnp.zeros((POOL_W, POOL_W), w.dtype)
    for gi in range(4):
        out = out.at[64 * gi:64 * (gi + 1), 64 * gi:64 * (gi + 1)].set(w[gi])
    return out


def _layer_consts(sp, l):
    causal = jnp.tril(jnp.ones((SGU_CHUNK, SGU_CHUNK), F32))
    wm = (sp["sgu_w"][l] * causal[None]).astype(BF16)
    wbd = _block_diag(sp["pool_w"][l]).astype(BF16)
    return dict(
        wbd=wbd, wbd_t=wbd.T, wm=wm, wm_t=wm.transpose(0, 2, 1),
        sgu_bias=jnp.repeat(sp["sgu_b"][l].T, 64, axis=1),
        bpad=jnp.pad(sp["b_forget"][l], (0, F_LANES - FOX_H)).reshape(1, F_LANES),
        bg=sp["b_gate"][l].reshape(1, 3 * D),
    )


def _relu2(acc):
    return acc, jnp.square(jnp.maximum(acc, 0.0))


def _relu2_grad(acc, z):
    return (acc * 2.0 * jnp.maximum(z, 0.0),)


def _layer_fwd(l, x, mem, source, sp):
    S = x.shape[0]
    t = _tile(S, 256)
    c = _layer_consts(sp, l)
    n = f"l{l}_"
    W, after = source(l, "begin", x)
    h = _rms_fwd(x, sp["norm_mix_g"][l], after=after, name=n + "norm_mix")
    qkv = _mm(h, W["qkv"], out_dtypes=(BF16,), name=n + "qkv")
    rest = _mm(h, W["rest"], name=n + "rest")
    pa = _pool_fwd(rest, c["wbd"], sp["pool_scale"][l], name=n + "pool")
    cum, cum_t = _fox_prep(rest, c["bpad"], name=n + "fox_prep")
    fk3 = cum_t[:FOX_H].reshape(FOX_H, S // t, t)
    o, lse = _fox_fwd(qkv, cum, fk3, name=n + "fox")
    more, _ = source(l, "attended", o)
    W.update(more)
    sg = _sgu_fwd(rest, sp["sgu_norm_g"][l], c["wm"], c["sgu_bias"], name=n + "sgu")
    more, after = source(l, "mixed", sg)
    W.update(more)
    ya = _mm(pa, W["ba"], after=after, name=n + "branch_a")
    yb = _mm(o, W["bb"], name=n + "branch_b")
    yc = _mm(sg, W["bc"], name=n + "branch_c")
    merged = _merge_fwd(rest, c["bg"], ya, yb, yc, name=n + "merge")
    x1 = _mm(merged, W["out"], extras=(x,), epilogue=_add, name=n + "out")
    hx = _rms_fwd(x1, sp["norm_xattn_g"][l], name=n + "norm_xattn")
    hm = _rms_fwd(mem, sp["norm_mem_g"][l], name=n + "norm_mem")
    xq = _mm(hx, W["xq"], out_dtypes=(BF16,), name=n + "xq")
    kv = _mm(hm, W["xkv"], out_dtypes=(BF16,), name=n + "xkv")
    o2 = _xattn_fwd(xq, kv, name=n + "xattn")
    x2 = _mm(o2, W["xo"], extras=(x1,), epilogue=_add, name=n + "xo")
    hf = _rms_fwd(x2, sp["norm_ffn_g"][l], name=n + "norm_ffn")
    z, act = _mm(hf, W["ff1"], epilogue=_relu2, out_dtypes=(F32, BF16), name=n + "ff1")
    _, after = source(l, "expanded", act)
    x3 = _mm(act, W["ff2"], extras=(x2,), epilogue=_add, after=after, name=n + "ff2")
    saved = dict(x=x, h=h, qkv=qkv, rest=rest, pa=pa, cum=cum, fk3=fk3, o=o, lse=lse, sg=sg, ya=ya, yb=yb, yc=yc,
                 merged=merged, x1=x1, hx=hx, hm=hm, xq=xq, kv=kv, o2=o2, x2=x2, hf=hf, z=z, act=act, c=c)
    return x3, saved, W


def _layer_bwd(l, dx3, sv, mem, W, sp, grads_done):
    S = dx3.shape[0]
    c = sv["c"]
    n = f"l{l}b_"
    bf = dict(out_dtypes=(BF16,))
    gw, gs = {}, {}
    gw["ff2"] = _mm(sv["act"], dx3, ta=True, name=n + "dw_ff2", **bf)
    dz = _mm(dx3, W["ff2"], tb=True, extras=(sv["z"],), epilogue=_relu2_grad, name=n + "dz", **bf)
    gw["ff1"] = _mm(sv["hf"], dz, ta=True, shard_out=True, name=n + "dw_ff1", **bf)
    dhf = _mm(dz, W["ff1"], tb=True, name=n + "dhf")
    dx2, gs["norm_ffn_g"] = _rms_bwd(sv["x2"], sp["norm_ffn_g"][l], dhf, dx3, name=n + "dnorm_ffn")
    gw["xo"] = _mm(sv["o2"], dx2, ta=True, name=n + "dw_xo", **bf)
    do2 = _mm(dx2, W["xo"], tb=True, name=n + "do2", **bf)
    dxq, dkv = _xattn_bwd(sv["xq"], sv["kv"], do2, name=n + "dxattn")
    gw["xq"] = _mm(sv["hx"], dxq, ta=True, name=n + "dw_xq", **bf)
    gw["xkv"] = _mm(sv["hm"], dkv, ta=True, shard_out=True, name=n + "dw_xkv", **bf)
    dhm = _mm(dkv, W["xkv"], tb=True, name=n + "dhm")
    _, gs["norm_mem_g"] = _rms_bwd(mem, sp["norm_mem_g"][l], dhm, jnp.zeros_like(mem), name=n + "dnorm_mem")
    dhx = _mm(dxq, W["xq"], tb=True, name=n + "dhx")
    dx1, gs["norm_xattn_g"] = _rms_bwd(sv["x1"], sp["norm_xattn_g"][l], dhx, dx2, name=n + "dnorm_xattn")
    after, gw = grads_done(l, gw), {}
    gw["out"] = _mm(sv["merged"], dx1, ta=True, name=n + "dw_out", **bf)
    dm = _mm(dx1, W["out"], tb=True, after=after, name=n + "dmerged")
    dya, dyb, dyc, dg1, dg2, dg3, db1, db2, db3 = _merge_bwd(sv["rest"], c["bg"], sv["ya"], sv["yb"], sv["yc"], dm, name=n + "dmerge")
    gs["b_gate"] = jnp.concatenate([db1, db2, db3], axis=1).reshape(3 * D)
    gw["ba"] = _mm(sv["pa"], dya, ta=True, shard_out=True, name=n + "dw_ba", **bf)
    gw["bb"] = _mm(sv["o"], dyb, ta=True, shard_out=True, name=n + "dw_bb", **bf)
    gw["bc"] = _mm(sv["sg"], dyc, ta=True, shard_out=True, name=n + "dw_bc", **bf)
    after, gw = grads_done(l, gw), {}
    dpa = _mm(dya, W["ba"], tb=True, name=n + "dpa")
    do = _mm(dyb, W["bb"], tb=True, after=after, name=n + "do", **bf)
    dsg = _mm(dyc, W["bc"], tb=True, name=n + "dsg")
    da, dwbd, dscale = _pool_bwd(sv["rest"], c["wbd"], c["wbd_t"], sp["pool_scale"][l], dpa, name=n + "dpool")
    gs["pool_w"] = jnp.stack([dwbd[64 * gi:64 * (gi + 1), 64 * gi:64 * (gi + 1)] for gi in range(4)])
    gs["pool_scale"] = dscale.reshape(POOL_W)
    dq, dk, dv, dfq, dfk = _fox_bwd(sv["qkv"], sv["cum"], sv["fk3"], sv["o"], do, sv["lse"], name=n + "dfox")
    dcum = dfq[:, :, :2].transpose(1, 0, 2).reshape(S, FOX_H) + dfk.reshape(FOX_H, S).T
    df, dbf = _fox_post(sv["rest"], c["bpad"], jnp.pad(dcum, ((0, 0), (0, F_LANES - FOX_H))), name=n + "dfox_post")
    gs["b_forget"] = dbf[0, :FOX_H]
    dc, dwm, dbias, dgn = _sgu_bwd(sv["rest"], sp["sgu_norm_g"][l], c["wm"], c["wm_t"], c["sgu_bias"], dsg, name=n + "dsgu")
    gs["sgu_w"] = dwm * jnp.tril(jnp.ones((SGU_CHUNK, SGU_CHUNK), F32))[None]
    gs["sgu_b"] = dbias.reshape(SGU_CHUNK, 4, 64).sum(axis=2).T
    gs["sgu_norm_g"] = dgn.reshape(SGU_W)
    dqkv = jnp.concatenate([dq, dk, dv], axis=1)
    drest = jnp.concatenate([da, df, jnp.zeros((S, OFF_C - OFF_F - F_LANES), BF16), dc, dg1, dg2, dg3], axis=1)
    gw["qkv"] = _mm(sv["h"], dqkv, ta=True, name=n + "dw_qkv", **bf)
    gw["rest"] = _mm(sv["h"], drest, ta=True, name=n + "dw_rest", **bf)
    after = grads_done(l, gw)
    dh = _mm(dqkv, W["qkv"], tb=True, after=after, name=n + "dh_qkv")
    dh = _mm(drest, W["rest"], tb=True, extras=(dh,), epilogue=_add, name=n + "dh")
    dx, gs["norm_mix_g"] = _rms_bwd(sv["x"], sp["norm_mix_g"][l], dh, dx1, name=n + "dnorm_mix")
    return dx, gs


def _local_step(x, mem, target, sp, source, grads_done):
    saved, Ws = [], []
    for l in range(DEPTH):
        x, sv, W = _layer_fwd(l, x, mem, source, sp)
        saved.append(sv)
        Ws.append(W)
    loss, dx, dgf = _final_loss(x, sp["final_norm_g"], target, name="final_loss")
    gss = [None] * DEPTH
    for l in reversed(range(DEPTH)):
        dx, gss[l] = _layer_bwd(l, dx, saved[l], mem, Ws[l], sp, grads_done)
    small = {k: jnp.stack([gss[l][k] for l in range(DEPTH)]) for k in gss[0]}
    small["final_norm_g"] = dgf
    return loss, dx, small


_SMALL = ["norm_mix_g", "b_forget", "pool_w", "pool_scale", "sgu_norm_g", "sgu_w", "sgu_b", "b_gate", "norm_xattn_g",
          "norm_mem_g", "norm_ffn_g", "final_norm_g"]
_COL = {"w_branch_a": "ba", "w_branch_b": "bb", "w_branch_c": "bc", "w_xkv": "xkv", "w_ff1": "ff1"}
_ROW = {"w_out": "out", "w_xq": "xq", "w_xo": "xo", "w_ff2": "ff2"}
_BIG = ["w_in", "w_branch_a", "w_branch_b", "w_branch_c", "w_out", "w_xq", "w_xkv", "w_xo", "w_ff1", "w_ff2"]
_PACK_LANES = 128


def _as_rows(a):
    return a.reshape(-1, a.shape[-1])


def _pack(tensors):
    rows = []
    for a in tensors:
        flat = a.reshape(-1)
        flat = jnp.pad(flat, (0, (-flat.shape[0]) % (8 * _PACK_LANES)))
        rows.append(flat.reshape(-1, _PACK_LANES))
    return jnp.concatenate(rows, axis=0)


def _unpack(packed, like):
    out, r = [], 0
    for a in like:
        size = math.prod(a.shape)
        nr = 8 * (-(-size // (8 * _PACK_LANES)))
        out.append(packed[r:r + nr].reshape(-1)[:size].reshape(a.shape))
        r += nr
    return out


_SHARD_IN = N_IN // N_DEV
_SHARD_IN_PAD = -(-_SHARD_IN // 128) * 128


def _columns(pieces, start, stop):
    out, at = [], 0
    for p in pieces:
        lo, hi = max(start, at), min(stop, at + p.shape[1])
        if lo < hi:
            out.append(p[:, lo - at:hi - at])
        at += p.shape[1]
    return out


def _split_w_in(blocks):
    K = blocks[0].shape[0]
    pad = jnp.zeros((K, OFF_C - OFF_F - FOX_H), blocks[0].dtype)
    cols = functools.partial(_columns, blocks)
    rest = jnp.concatenate(cols(0, R_OFF_Q) + cols(R_OFF_F, R_OFF_C) + [pad] + cols(R_OFF_C, N_IN), axis=1)
    return jnp.concatenate(cols(R_OFF_Q, R_OFF_F), axis=1), rest


def _join_w_in(qkv, rest):
    in_order = [rest[:, :R_OFF_Q], qkv, rest[:, OFF_F:OFF_F + FOX_H], rest[:, OFF_C:]]
    pad = jnp.zeros((qkv.shape[0], _SHARD_IN_PAD - _SHARD_IN), qkv.dtype)
    return jnp.stack([jnp.concatenate(_columns(in_order, _SHARD_IN * d, _SHARD_IN * (d + 1)) + [pad], axis=1) for d in range(N_DEV)])


_FIRST = ["w_in"]
_LATER = [k for k in _BIG if k not in _FIRST]


def _layer_weights(gathered):
    W = {}
    if "w_in" in gathered:
        W.update(zip(("qkv", "rest"), _split_w_in([gathered["w_in"][d][:, :_SHARD_IN] for d in range(N_DEV)])))
    for name, key in _COL.items():
        if name in gathered:
            W[key] = _Gathered(gathered[name])
    for name, key in _ROW.items():
        if name in gathered:
            W[key] = gathered[name].reshape(-1, gathered[name].shape[-1])
    return W


def _grad_blocks(gw):
    parts = {}
    if "qkv" in gw:
        parts["w_in"] = _join_w_in(gw["qkv"], gw["rest"])
    for name, key in _COL.items():
        if key in gw:
            parts[name] = gw[key]
    for name, key in _ROW.items():
        if key in gw:
            parts[name] = gw[key].reshape(N_DEV, -1, gw[key].shape[-1])
    return parts


def kernel(x, mem, norm_mix_g, w_in, b_forget, pool_w, pool_scale, sgu_norm_g, sgu_w, sgu_b, w_branch_a, w_branch_b, w_branch_c, b_gate, w_out, norm_xattn_g, norm_mem_g, w_xq, w_xkv, w_xo, norm_ffn_g, w_ff1, w_ff2, final_norm_g, loss_target, m_norm_mix_g, m_w_in, m_b_forget, m_pool_w, m_pool_scale, m_sgu_norm_g, m_sgu_w, m_sgu_b, m_w_branch_a, m_w_branch_b, m_w_branch_c, m_b_gate, m_w_out, m_norm_xattn_g, m_norm_mem_g, m_w_xq, m_w_xkv, m_w_xo, m_norm_ffn_g, m_w_ff1, m_w_ff2, m_final_norm_g, v_norm_mix_g, v_w_in, v_b_forget, v_pool_w, v_pool_scale, v_sgu_norm_g, v_sgu_w, v_sgu_b, v_w_branch_a, v_w_branch_b, v_w_branch_c, v_b_gate, v_w_out, v_norm_xattn_g, v_norm_mem_g, v_w_xq, v_w_xkv, v_w_xo, v_norm_ffn_g, v_w_ff1, v_w_ff2, v_final_norm_g):
    names = ["norm_mix_g", "w_in", "b_forget", "pool_w", "pool_scale", "sgu_norm_g", "sgu_w", "sgu_b", "w_branch_a", "w_branch_b",
             "w_branch_c", "b_gate", "w_out", "norm_xattn_g", "norm_mem_g", "w_xq", "w_xkv", "w_xo", "norm_ffn_g", "w_ff1", "w_ff2",
             "final_norm_g"]
    w = dict(zip(names, [norm_mix_g, w_in, b_forget, pool_w, pool_scale, sgu_norm_g, sgu_w, sgu_b, w_branch_a, w_branch_b, w_branch_c,
                         b_gate, w_out, norm_xattn_g, norm_mem_g, w_xq, w_xkv, w_xo, norm_ffn_g, w_ff1, w_ff2, final_norm_g]))
    m = dict(zip(names, [m_norm_mix_g, m_w_in, m_b_forget, m_pool_w, m_pool_scale, m_sgu_norm_g, m_sgu_w, m_sgu_b, m_w_branch_a,
                         m_w_branch_b, m_w_branch_c, m_b_gate, m_w_out, m_norm_xattn_g, m_norm_mem_g, m_w_xq, m_w_xkv, m_w_xo,
                         m_norm_ffn_g, m_w_ff1, m_w_ff2, m_final_norm_g]))
    v = dict(zip(names, [v_norm_mix_g, v_w_in, v_b_forget, v_pool_w, v_pool_scale, v_sgu_norm_g, v_sgu_w, v_sgu_b, v_w_branch_a,
                         v_w_branch_b, v_w_branch_c, v_b_gate, v_w_out, v_norm_xattn_g, v_norm_mem_g, v_w_xq, v_w_xkv, v_w_xo,
                         v_norm_ffn_g, v_w_ff1, v_w_ff2, v_final_norm_g]))

    sp = {k: w[k] for k in _SMALL}
    shards = [{k: w[k][l].astype(BF16) for k in _BIG} for l in range(DEPTH)]
    for sh in shards:
        sh["w_in"] = jnp.pad(sh["w_in"], ((0, 0), (0, _SHARD_IN_PAD - _SHARD_IN)))
    me = _dev_index(*_position())

    def gather_out(l, keys, name, after=None):
        srcs = [shards[l][k] for k in keys]
        lands = [_own_block_placed(a, jax.ShapeDtypeStruct((N_DEV, *a.shape), a.dtype)) for a in srcs]
        state, token = _split_start(_plan_gather_out, srcs, lands, after=after, name=name + "_out_start")
        return (keys, name, state), token

    def gather_pass(job, value):
        keys, name, state = job
        lands = _split_wait(_plan_gather_out, state, value, name=name + "_out_wait")
        state, token = _split_start(_plan_gather_pass, [], lands, name=name + "_pass_start")
        return (keys, name, state), token, lands[0]

    def gather_end(job, value):
        keys, name, state = job
        return _layer_weights(dict(zip(keys, _split_wait(_plan_gather_pass, state, value, name=name + "_pass_wait"))))

    jobs = {}

    def source(l, point, value):
        if (l, point) == (0, "begin"):
            first = _all_gather([shards[0][k] for k in _FIRST], name="gather_l0_first")
            jobs["l0"], token = gather_out(0, _LATER, "gather_l0")
            return _layer_weights(dict(zip(_FIRST, first))), token
        if (l, point) == (0, "attended"):
            jobs["l0"], _, arrived = gather_pass(jobs["l0"], value)
            jobs["l1"], jobs["token"] = gather_out(1, _BIG, "gather_l1", after=arrived)
            return {}, None
        if (l, point) == (0, "mixed"):
            return gather_end(jobs.pop("l0"), value), jobs.pop("token")
        if (l, point) == (0, "expanded"):
            jobs["l1"], token, _ = gather_pass(jobs["l1"], value)
            return {}, token
        if (l, point) == (1, "begin"):
            return gather_end(jobs.pop("l1"), value), None
        return {}, None

    received = [{} for _ in range(DEPTH)]
    travelling = []

    def grads_done(l, gw):
        blocks = _grad_blocks(gw)
        keys = [k for k in _BIG if k in blocks]
        parts = [blocks[k] for k in keys]
        group = f"exchange_grads_l{l}_" + ("in" if "w_in" in blocks else "merge" if "w_out" in blocks else "mlp")
        lands = [_own_block_placed(lax.dynamic_index_in_dim(p, me, 0, keepdims=False), p) for p in parts]
        state, token = _split_start(_plan_exchange, parts, lands, name=group + "_start")
        travelling.append((l, keys, state, group + "_wait"))
        return token

    loss, dx, small = _local_step(x[0], mem[0], loss_target[0], sp, source, grads_done)
    loss = lax.psum(loss[0, 0], ("x", "y", "c"))
    grads, deltas, new_m, new_v = {}, {}, {}, {}
    done = dx
    for group_keys in dict.fromkeys(tuple(keys) for _, keys, _, _ in travelling):
        for l, keys, state, wait_name in travelling:
            if tuple(keys) == group_keys:
                received[l].update(zip(keys, _split_wait(_plan_exchange, state, done, name=wait_name)))
        for k in group_keys:
            outs = _adamw_sharded([received[l][k] for l in range(DEPTH)], w[k], m[k], v[k], name="adamw_" + k)
            grads[k], deltas[k], new_m[k], new_v[k] = outs
        done = grads[group_keys[-1]]
    like = [w[k] for k in _SMALL]
    g_small = _unpack(_all_reduce(_pack([small[k] for k in _SMALL]), name="all_reduce_small"), like)
    rows = lambda d: [_as_rows(d[k]) for k in _SMALL]
    outs = _adamw_small([_as_rows(g) for g in g_small], rows(w), rows(m), rows(v), name="adamw_small")
    grads.update(zip(_SMALL, g_small))
    for dst, vals in zip((deltas, new_m, new_v), outs):
        dst.update({k: a.reshape(w[k].shape) for k, a in zip(_SMALL, vals)})

    return (loss, dx[None], *[grads[k] for k in names], *[deltas[k] for k in names], *[new_m[k] for k in names],
            *[new_v[k] for k in names])
```

```python
import functools
import math

import jax
import jax.numpy as jnp
from jax import lax
from jax.experimental import pallas as pl
from jax.experimental.pallas import tpu as pltpu

F32 = jnp.float32
BF16 = jnp.bfloat16
MESH = pl.DeviceIdType.MESH

N_DEV = 8
D = 1024
DEPTH = 2
EPS = 1e-6
NEG = -1e30
POOL_W = 256
FOX_H = 8
FOX_DH = 64
FOX_W = 512
SGU_W = 256
SGU_CHUNK = 128
XH = 4
XDH = 256
N_IN = 5384
R_OFF_Q, R_OFF_F, R_OFF_C = 256, 1792, 1800
QKV_W = 3 * FOX_W
OFF_A, OFF_F, OFF_C, OFF_G, REST_W = 0, 256, 512, 1024, 4096
F_LANES = 128

ADAM_LR = 0.001
ADAM_B1 = 0.9
ADAM_B2 = 0.999
ADAM_EPS = 1e-08
ADAM_WD = 0.01
ADAM_STEP = 10

VMEM_LIMIT = 56 * 1024 * 1024


def _tile(n, pref):
    t = min(n, pref)
    while n % t:
        t -= 128
    assert t > 0, (n, pref)
    return t


def _params(sem=None):
    return pltpu.CompilerParams(dimension_semantics=sem, vmem_limit_bytes=VMEM_LIMIT)


def _dot(a, b, ca, cb):
    return lax.dot_general(a, b, (((ca,), (cb,)), ((), ())), preferred_element_type=F32)


def _sigmoid(z):
    return 1.0 / (1.0 + jnp.exp(-z))


_GELU_K = math.sqrt(2.0 / math.pi)
_GELU_C = 0.044715


def _gelu(x):
    return 0.5 * x * (1.0 + jnp.tanh(_GELU_K * (x + _GELU_C * x * x * x)))


def _gelu_grad(x):
    t = jnp.tanh(_GELU_K * (x + _GELU_C * x * x * x))
    return 0.5 * (1.0 + t) + 0.5 * x * (1.0 - t * t) * _GELU_K * (1.0 + 3.0 * _GELU_C * x * x)


def _rows(shape):
    return lax.broadcasted_iota(jnp.int32, shape, 0)


def _lanes(shape):
    return lax.broadcasted_iota(jnp.int32, shape, 1)


class _Gathered:
    def __init__(self, arr):
        self.arr = arr
        self.shape = (arr.shape[1], N_DEV * arr.shape[2])


_TOKEN = (8, 128)


def _mm(a, b, *, ta=False, tb=False, extras=(), epilogue=None, out_dtypes=(F32,), shard_out=False, after=None, tm=None, tn=512, tk=None,
        name):
    M, K = (a.shape[1], a.shape[0]) if ta else a.shape
    N, Kb = b.shape if tb else b.shape[::-1]
    assert Kb == K, (a.shape, b.shape, ta, tb)
    gathered = isinstance(b, _Gathered)
    if gathered:
        if tb:
            tk = b.arr.shape[2]
        else:
            tn = b.arr.shape[2]
    if shard_out:
        tn = N // N_DEV
    tm = _tile(M, tm or (1024 if ta else 2048))
    tn = _tile(N, tn)
    tk = _tile(K, tk or (2048 if ta else 1024))
    nk = K // tk
    ca, cb = (0 if ta else 1), (1 if tb else 0)
    n_ex, n_out = len(extras), len(out_dtypes)
    tokens = [] if after is None else [after]
    n_in = 2 + n_ex + len(tokens)
    if epilogue is None:
        epilogue = lambda acc: (acc,)

    def body(*refs):
        a_ref, b_ref = refs[:2]
        ex_refs = refs[2:2 + n_ex]
        o_refs = refs[n_in:n_in + n_out]
        part = _dot(a_ref[...].astype(BF16), b_ref[...].astype(BF16), ca, cb)

        def finish(acc):
            for o_ref, val in zip(o_refs, epilogue(acc, *[e[...] for e in ex_refs])):
                o_ref[...] = val.astype(o_ref.dtype)

        if nk == 1:
            finish(part)
        else:
            acc_ref = refs[-1]
            k = pl.program_id(2)

            @pl.when(k == 0)
            def _():
                acc_ref[...] = part

            @pl.when(k > 0)
            def _():
                acc_ref[...] += part

            @pl.when(k == nk - 1)
            def _():
                finish(acc_ref[...])

    a_spec = pl.BlockSpec((tk, tm), lambda i, j, k: (k, i)) if ta else pl.BlockSpec((tm, tk), lambda i, j, k: (i, k))
    if not gathered:
        b_arr = b
        b_spec = pl.BlockSpec((tn, tk), lambda i, j, k: (j, k)) if tb else pl.BlockSpec((tk, tn), lambda i, j, k: (k, j))
    else:
        b_arr = b.arr
        if tb:
            b_spec = pl.BlockSpec((None, tn, tk), lambda i, j, k: (k, j, 0))
        else:
            b_spec = pl.BlockSpec((None, tk, tn), lambda i, j, k: (j, k, 0))
    tile = pl.BlockSpec((tm, tn), lambda i, j, k: (i, j))
    if shard_out:
        out_specs = [pl.BlockSpec((None, tm, tn), lambda i, j, k: (j, i, 0))] * n_out
        out_shape = [jax.ShapeDtypeStruct((N_DEV, M, tn), dt) for dt in out_dtypes]
    else:
        out_specs = [tile] * n_out
        out_shape = [jax.ShapeDtypeStruct((M, N), dt) for dt in out_dtypes]
    size = lambda dt: jnp.dtype(dt).itemsize
    vmem = 2 * (tm * tk * size(a.dtype) + tk * tn * size(b_arr.dtype)
                + tm * tn * (sum(size(e.dtype) for e in extras) + sum(map(size, out_dtypes))))
    vmem += tm * tn * 4 * (nk > 1)
    assert vmem <= VMEM_LIMIT - (4 << 20), (name, vmem)
    outs = pl.pallas_call(
        body,
        name=name,
        grid=(M // tm, N // tn, nk),
        in_specs=[a_spec, b_spec] + [tile] * n_ex + [pl.BlockSpec(_TOKEN, lambda i, j, k: (0, 0))] * len(tokens),
        out_specs=out_specs,
        out_shape=out_shape,
        scratch_shapes=[pltpu.VMEM((tm, tn), F32)] if nk > 1 else [],
        compiler_params=_params(("parallel", "parallel", "arbitrary")),
    )(a, b_arr, *extras, *tokens)
    return outs[0] if n_out == 1 else outs


def _add(acc, res):
    return (acc + res,)


def _rms_fwd(x, g, *, after=None, name):
    R, C = x.shape
    tm = _tile(R, 256)
    tokens = [] if after is None else [after]

    def body(x_ref, g_ref, *rest):
        xv = x_ref[...]
        r = lax.rsqrt(jnp.mean(xv * xv, axis=-1, keepdims=True) + EPS)
        rest[-1][...] = (xv * r * g_ref[...]).astype(BF16)

    return pl.pallas_call(
        body,
        name=name,
        grid=(R // tm,),
        in_specs=[pl.BlockSpec((tm, C), lambda i: (i, 0)), pl.BlockSpec((1, C), lambda i: (0, 0))]
        + [pl.BlockSpec(_TOKEN, lambda i: (0, 0))] * len(tokens),
        out_specs=pl.BlockSpec((tm, C), lambda i: (i, 0)),
        out_shape=jax.ShapeDtypeStruct((R, C), BF16),
        compiler_params=_params(("parallel",)),
    )(x, g.reshape(1, C), *tokens)


def _rms_bwd(x, g, dh, dres, *, name):
    R, C = x.shape
    tm = _tile(R, 256)

    def body(x_ref, g_ref, dh_ref, dres_ref, dx_ref, dg_ref):
        xv = x_ref[...]
        r = lax.rsqrt(jnp.mean(xv * xv, axis=-1, keepdims=True) + EPS)
        xn = xv * r
        dh_v = dh_ref[...].astype(F32)
        dxn = dh_v * g_ref[...]
        dx_ref[...] = r * (dxn - xn * jnp.mean(dxn * xn, axis=-1, keepdims=True)) + dres_ref[...]
        part = jnp.sum(dh_v * xn, axis=0, keepdims=True)

        @pl.when(pl.program_id(0) == 0)
        def _():
            dg_ref[...] = part

        @pl.when(pl.program_id(0) > 0)
        def _():
            dg_ref[...] += part

    row = pl.BlockSpec((tm, C), lambda i: (i, 0))
    vec = pl.BlockSpec((1, C), lambda i: (0, 0))
    dx, dg = pl.pallas_call(
        body,
        name=name,
        grid=(R // tm,),
        in_specs=[row, vec, row, row],
        out_specs=[row, vec],
        out_shape=[jax.ShapeDtypeStruct((R, C), F32), jax.ShapeDtypeStruct((1, C), F32)],
        compiler_params=_params(("arbitrary",)),
    )(x, g.reshape(1, C), dh, dres)
    return dx, dg.reshape(C)


def _final_loss(x, g, target, *, name):
    R, C = x.shape
    tm = _tile(R, 256)

    def body(x_ref, g_ref, t_ref, loss_ref, dx_ref, dg_ref):
        xv = x_ref[...]
        r = lax.rsqrt(jnp.mean(xv * xv, axis=-1, keepdims=True) + EPS)
        xn = xv * r
        gv = g_ref[...]
        err = xn * gv - t_ref[...]
        lpart = (0.5 / C) * jnp.sum(jnp.sum(err * err, axis=1, keepdims=True), axis=0, keepdims=True)
        dy = err * (1.0 / C)
        dxn = dy * gv
        dx_ref[...] = r * (dxn - xn * jnp.mean(dxn * xn, axis=-1, keepdims=True))
        gpart = jnp.sum(dy * xn, axis=0, keepdims=True)

        @pl.when(pl.program_id(0) == 0)
        def _():
            loss_ref[...] = lpart
            dg_ref[...] = gpart

        @pl.when(pl.program_id(0) > 0)
        def _():
            loss_ref[...] += lpart
            dg_ref[...] += gpart

    row = pl.BlockSpec((tm, C), lambda i: (i, 0))
    vec = pl.BlockSpec((1, C), lambda i: (0, 0))
    loss, dx, dg = pl.pallas_call(
        body,
        name=name,
        grid=(R // tm,),
        in_specs=[row, vec, row],
        out_specs=[pl.BlockSpec((1, 1), lambda i: (0, 0)), row, vec],
        out_shape=[jax.ShapeDtypeStruct((1, 1), F32), jax.ShapeDtypeStruct((R, C), F32), jax.ShapeDtypeStruct((1, C), F32)],
        compiler_params=_params(("arbitrary",)),
    )(x, g.reshape(1, C), target)
    return loss, dx, dg.reshape(C)


def _pool_select(lane, vals):
    out = vals[3]
    for gi in (2, 1, 0):
        out = jnp.where(lane < 64 * (gi + 1), vals[gi], out)
    return out


def _pool_diff(a):
    row, lane = _rows(a.shape), _lanes(a.shape)

    def down(v, k):
        return jnp.where(row >= k, pltpu.roll(v, k, 0), 0.0)

    s2 = a + down(a, 1)
    s4 = s2 + down(s2, 2)
    s8 = s4 + down(s4, 4)
    s16 = s8 + down(s8, 8)
    wsum = _pool_select(lane, (s2, s4, s8, s16))
    win = _pool_select(lane, (2, 4, 8, 16))
    cnt = jnp.minimum(row + 1, win).astype(F32)
    return wsum / cnt - a, cnt


def _pool_diff_t(dd, cnt):
    S = dd.shape[0]
    row, lane = _rows(dd.shape), _lanes(dd.shape)

    def up(v, k):
        return jnp.where(row < S - k, pltpu.roll(v, S - k, 0), 0.0)

    e = dd / cnt
    s2 = e + up(e, 1)
    s4 = s2 + up(s2, 2)
    s8 = s4 + up(s4, 4)
    s16 = s8 + up(s8, 8)
    return _pool_select(lane, (s2, s4, s8, s16)) - dd


def _pool_fwd(rest, wbd, scale, *, name):
    S = rest.shape[0]

    def body(a_ref, w_ref, s_ref, o_ref):
        d, _ = _pool_diff(a_ref[...])
        yp = _dot(d.astype(BF16), w_ref[...], 1, 0)
        o_ref[...] = (yp * s_ref[...]).astype(BF16)

    return pl.pallas_call(
        body,
        name=name,
        grid=(1,),
        in_specs=[
            pl.BlockSpec((S, POOL_W), lambda i: (0, OFF_A // POOL_W)),
            pl.BlockSpec((POOL_W, POOL_W), lambda i: (0, 0)),
            pl.BlockSpec((1, POOL_W), lambda i: (0, 0)),
        ],
        out_specs=pl.BlockSpec((S, POOL_W), lambda i: (0, 0)),
        out_shape=jax.ShapeDtypeStruct((S, POOL_W), BF16),
        compiler_params=_params(("arbitrary",)),
    )(rest, wbd, scale.reshape(1, POOL_W))


def _pool_bwd(rest, wbd, wbd_t, scale, dpa, *, name):
    S = rest.shape[0]

    def body(a_ref, w_ref, wt_ref, s_ref, dpa_ref, da_ref, dw_ref, ds_ref):
        d, cnt = _pool_diff(a_ref[...])
        db = d.astype(BF16)
        yp = _dot(db, w_ref[...], 1, 0)
        dpa_v = dpa_ref[...]
        ds_ref[...] = jnp.sum(dpa_v * yp, axis=0, keepdims=True)
        dyp = (dpa_v * s_ref[...]).astype(BF16)
        dw_ref[...] = _dot(db, dyp, 0, 0)
        dd = _dot(dyp, wt_ref[...], 1, 0)
        da_ref[...] = _pool_diff_t(dd, cnt).astype(BF16)

    full = pl.BlockSpec((S, POOL_W), lambda i: (0, 0))
    sq = pl.BlockSpec((POOL_W, POOL_W), lambda i: (0, 0))
    vec = pl.BlockSpec((1, POOL_W), lambda i: (0, 0))
    return pl.pallas_call(
        body,
        name=name,
        grid=(1,),
        in_specs=[pl.BlockSpec((S, POOL_W), lambda i: (0, OFF_A // POOL_W)), sq, sq, vec, full],
        out_specs=[full, sq, vec],
        out_shape=[
            jax.ShapeDtypeStruct((S, POOL_W), BF16),
            jax.ShapeDtypeStruct((POOL_W, POOL_W), F32),
            jax.ShapeDtypeStruct((1, POOL_W), F32),
        ],
        compiler_params=_params(("arbitrary",)),
    )(rest, wbd, wbd_t, scale.reshape(1, POOL_W), dpa)


def _log_sigmoid(z):
    return jnp.minimum(z, 0.0) - jnp.log(1.0 + jnp.exp(-jnp.abs(z)))


_F_SPEC_COL = OFF_F // F_LANES


def _fox_prep(rest, bpad, *, name):
    S = rest.shape[0]

    def body(f_ref, b_ref, o_ref, ot_ref):
        acc = _log_sigmoid(f_ref[...] + b_ref[...])
        row = _rows(acc.shape)
        k = 1
        while k < S:
            acc = acc + jnp.where(row >= k, pltpu.roll(acc, k, 0), 0.0)
            k *= 2
        o_ref[...] = acc
        ot_ref[...] = acc.T

    return pl.pallas_call(
        body,
        name=name,
        grid=(1,),
        in_specs=[pl.BlockSpec((S, F_LANES), lambda i: (0, _F_SPEC_COL)), pl.BlockSpec((1, F_LANES), lambda i: (0, 0))],
        out_specs=[pl.BlockSpec((S, F_LANES), lambda i: (0, 0)), pl.BlockSpec((F_LANES, S), lambda i: (0, 0))],
        out_shape=[jax.ShapeDtypeStruct((S, F_LANES), F32), jax.ShapeDtypeStruct((F_LANES, S), F32)],
        compiler_params=_params(("arbitrary",)),
    )(rest, bpad)


def _fox_post(rest, bpad, dcum, *, name):
    S = rest.shape[0]

    def body(f_ref, b_ref, d_ref, df_ref, db_ref):
        acc = d_ref[...]
        row = _rows(acc.shape)
        k = 1
        while k < S:
            acc = acc + jnp.where(row < S - k, pltpu.roll(acc, S - k, 0), 0.0)
            k *= 2
        df = acc * (1.0 - _sigmoid(f_ref[...] + b_ref[...]))
        df_ref[...] = df.astype(BF16)
        db_ref[...] = jnp.sum(df, axis=0, keepdims=True)

    full = pl.BlockSpec((S, F_LANES), lambda i: (0, 0))
    vec = pl.BlockSpec((1, F_LANES), lambda i: (0, 0))
    return pl.pallas_call(
        body,
        name=name,
        grid=(1,),
        in_specs=[pl.BlockSpec((S, F_LANES), lambda i: (0, _F_SPEC_COL)), vec, full],
        out_specs=[full, vec],
        out_shape=[jax.ShapeDtypeStruct((S, F_LANES), BF16), jax.ShapeDtypeStruct((1, F_LANES), F32)],
        compiler_params=_params(("arbitrary",)),
    )(rest, bpad, dcum)


_FOX_SCALE = FOX_DH ** -0.5
_PAIRS = FOX_H // 2


def _scaled(v):
    return (v.astype(F32) * _FOX_SCALE).astype(BF16)


def _head_lane(cum, h):
    return jnp.sum(jnp.where(_lanes(cum.shape) == h, cum, 0.0), axis=-1, keepdims=True)


def _diag_mask(s):
    return jnp.where(_rows(s.shape) >= _lanes(s.shape), s, NEG)


def _fox_fwd(qkv, cum, fk3, *, name):
    S = qkv.shape[0]
    nk, t = fk3.shape[1:]

    def body(q_ref, k_ref, v_ref, cum_ref, fk_ref, o_ref, lse_ref, m_sc, l_sc, acc_sc):
        hp, i = pl.program_id(0), pl.program_id(1)
        lane = _lanes((t, 128))
        lo = lane < FOX_DH
        qs = _scaled(q_ref[...])
        zero = jnp.zeros_like(qs)
        qm = (jnp.where(lo, qs, zero), jnp.where(lo, zero, qs))
        cumv = cum_ref[...]
        fq = [_head_lane(cumv, 2 * hp + e) for e in range(2)]
        m_sc[...] = jnp.full(m_sc.shape, NEG, F32)
        l_sc[...] = jnp.zeros(l_sc.shape, F32)
        acc_sc[...] = jnp.zeros(acc_sc.shape, F32)

        def tile(j, masked):
            k0 = pl.multiple_of(j * t, t)
            kb = k_ref[pl.ds(k0, t), :]
            vb = v_ref[pl.ds(k0, t), :]
            alphas, pvs = [], []
            for e in range(2):
                s = _dot(qm[e], kb, 1, 1) + fq[e] - fk_ref[2 * hp + e, pl.ds(j, 1), :]
                if masked:
                    s = _diag_mask(s)
                m_old = m_sc[e]
                m_new = jnp.maximum(m_old, jnp.max(s, axis=-1, keepdims=True))
                p = jnp.exp(s - m_new)
                alpha = jnp.exp(m_old - m_new)
                l_sc[e] = alpha * l_sc[e] + jnp.sum(p, axis=-1, keepdims=True)
                m_sc[e] = m_new
                alphas.append(alpha)
                pvs.append(_dot(p.astype(BF16), vb, 1, 0))
            acc_sc[...] = jnp.where(lo, alphas[0], alphas[1]) * acc_sc[...] + jnp.where(lo, pvs[0], pvs[1])

        def step(j, carry):
            tile(j, False)
            return carry

        lax.fori_loop(0, i, step, 0)
        tile(i, True)
        o_ref[...] = acc_sc[...] / jnp.where(lo, l_sc[0], l_sc[1])
        lse = [m_sc[e] + jnp.log(l_sc[e]) for e in range(2)]
        lse_ref[...] = jnp.where(lane == 0, lse[0], jnp.where(lane == 1, lse[1], 0.0))

    return pl.pallas_call(
        body,
        name=name,
        grid=(_PAIRS, S // t),
        in_specs=[
            pl.BlockSpec((t, 128), lambda hp, i: (i, hp)),
            pl.BlockSpec((S, 128), lambda hp, i: (0, _PAIRS + hp)),
            pl.BlockSpec((S, 128), lambda hp, i: (0, 2 * _PAIRS + hp)),
            pl.BlockSpec((t, F_LANES), lambda hp, i: (i, 0)),
            pl.BlockSpec((FOX_H, nk, t), lambda hp, i: (0, 0, 0)),
        ],
        out_specs=[pl.BlockSpec((t, 128), lambda hp, i: (i, hp)), pl.BlockSpec((None, t, 128), lambda hp, i: (hp, i, 0))],
        out_shape=[jax.ShapeDtypeStruct((S, FOX_W), F32), jax.ShapeDtypeStruct((_PAIRS, S, 128), F32)],
        scratch_shapes=[pltpu.VMEM((2, t, 1), F32), pltpu.VMEM((2, t, 1), F32), pltpu.VMEM((t, 128), F32)],
        compiler_params=_params(("parallel", "arbitrary")),
    )(qkv, qkv, qkv, cum, fk3)


def _fox_bwd(qkv, cum, fk3, o, do, lse, *, name):
    S = qkv.shape[0]
    nk, t = fk3.shape[1:]

    def body(q_ref, k_ref, v_ref, cum_ref, fk_ref, o_ref, do_ref, lse_ref, dq_ref, dk_ref, dv_ref, dfq_ref, dfk_ref,
             qm_sc, km_sc, dom_sc, delta_sc, fq_sc, dfq_sc, dq_sc):
        hp = pl.program_id(0)
        lane = _lanes((t, 128))
        lo = lane < FOX_DH

        def prep(i, carry):
            r = pl.ds(pl.multiple_of(i * t, t), t)
            qs, ks, dob = _scaled(q_ref[r, :]), _scaled(k_ref[r, :]), do_ref[r, :]
            prod = dob.astype(F32) * o_ref[r, :]
            cumv = cum_ref[r, :]
            zero = jnp.zeros_like(qs)
            for e in range(2):
                mine = lo if e == 0 else jnp.logical_not(lo)
                qm_sc[e, r, :] = jnp.where(mine, qs, zero)
                km_sc[e, r, :] = jnp.where(mine, ks, zero)
                dom_sc[e, r, :] = jnp.where(mine, dob, zero)
                delta_sc[e, r, :] = jnp.sum(jnp.where(mine, prod, 0.0), axis=-1, keepdims=True)
                fq_sc[e, r, :] = _head_lane(cumv, 2 * hp + e)
                dfq_sc[e, r, :] = jnp.zeros((t, 1), F32)
            dq_sc[r, :] = jnp.zeros((t, 128), F32)
            return carry

        lax.fori_loop(0, nk, prep, 0)

        def kv_tile(j, carry):
            kr = pl.ds(pl.multiple_of(j * t, t), t)
            kb, vb = k_ref[kr, :], v_ref[kr, :]
            fks = [fk_ref[2 * hp + e, pl.ds(j, 1), :] for e in range(2)]

            def q_tile(i, acc, masked):
                dk, dv, dfk0, dfk1 = acc
                dfk = [dfk0, dfk1]
                qr = pl.ds(pl.multiple_of(i * t, t), t)
                dq_t = jnp.zeros((t, 128), F32)
                for e in range(2):
                    qe, doe = qm_sc[e, qr, :], dom_sc[e, qr, :]
                    s = _dot(qe, kb, 1, 1) + fq_sc[e, qr, :] - fks[e]
                    if masked:
                        s = _diag_mask(s)
                    p = jnp.exp(s - lse_ref[qr, e:e + 1])
                    dv = dv + _dot(p.astype(BF16), doe, 0, 0)
                    dp = _dot(doe, vb, 1, 1)
                    ds = p * (dp - delta_sc[e, qr, :])
                    dsb = ds.astype(BF16)
                    dk = dk + _dot(dsb, qe, 0, 0)
                    dq_t = dq_t + _dot(dsb, km_sc[e, kr, :], 1, 0)
                    dfq_sc[e, qr, :] += jnp.sum(ds, axis=-1, keepdims=True)
                    dfk[e] = dfk[e] - jnp.sum(ds, axis=0, keepdims=True)
                dq_sc[qr, :] += dq_t
                return dk, dv, dfk[0], dfk[1]

            init = (jnp.zeros((t, 128), F32), jnp.zeros((t, 128), F32), jnp.zeros((1, t), F32), jnp.zeros((1, t), F32))
            acc = q_tile(j, init, True)
            dk, dv, dfk0, dfk1 = lax.fori_loop(j + 1, nk, functools.partial(q_tile, masked=False), acc)
            dk_ref[kr, :] = dk.astype(BF16)
            dv_ref[kr, :] = dv.astype(BF16)
            dfk_ref[0, pl.ds(j, 1), :] = dfk0
            dfk_ref[1, pl.ds(j, 1), :] = dfk1
            return carry

        lax.fori_loop(0, nk, kv_tile, 0)
        dq_ref[...] = dq_sc[...].astype(BF16)
        lane_s = _lanes((S, 128))
        dfq_ref[...] = jnp.where(lane_s == 0, dfq_sc[0], jnp.where(lane_s == 1, dfq_sc[1], 0.0))

    col = lambda c0: pl.BlockSpec((S, 128), lambda hp: (0, c0 + hp))
    pair = pl.BlockSpec((S, 128), lambda hp: (0, hp))
    lanes3 = pl.BlockSpec((None, S, 128), lambda hp: (hp, 0, 0))
    big = jax.ShapeDtypeStruct((S, FOX_W), BF16)
    masked_bf16 = pltpu.VMEM((2, S, 128), BF16)
    column = pltpu.VMEM((2, S, 1), F32)
    return pl.pallas_call(
        body,
        name=name,
        grid=(_PAIRS,),
        in_specs=[
            col(0), col(_PAIRS), col(2 * _PAIRS),
            pl.BlockSpec((S, F_LANES), lambda hp: (0, 0)),
            pl.BlockSpec((FOX_H, nk, t), lambda hp: (0, 0, 0)),
            pair, pair, lanes3,
        ],
        out_specs=[pair, pair, pair, lanes3, pl.BlockSpec((None, 2, nk, t), lambda hp: (hp, 0, 0, 0))],
        out_shape=[big, big, big, jax.ShapeDtypeStruct((_PAIRS, S, 128), F32), jax.ShapeDtypeStruct((_PAIRS, 2, nk, t), F32)],
        scratch_shapes=[masked_bf16, masked_bf16, masked_bf16, column, column, column, pltpu.VMEM((S, 128), F32)],
        compiler_params=_params(("parallel",)),
    )(qkv, qkv, qkv, cum, fk3, o, do, lse)


def _group_mask(lane, gi):
    return (lane >= 64 * gi) & (lane < 64 * (gi + 1))


_U_COL = OFF_C // SGU_W


def _sgu_fwd(rest, gn, wm, bias, *, name):
    S = rest.shape[0]
    ts = _tile(S, 512)
    nc = ts // SGU_CHUNK

    def body(u_ref, v_ref, g_ref, w_ref, b_ref, o_ref):
        zv = _gelu(v_ref[...])
        vn = zv * lax.rsqrt(jnp.mean(zv * zv, axis=-1, keepdims=True) + EPS) * g_ref[...]
        lane = _lanes((SGU_CHUNK, SGU_W))
        for c in range(nc):
            rows = slice(c * SGU_CHUNK, (c + 1) * SGU_CHUNK)
            vcb = vn[rows].astype(BF16)
            mixed = b_ref[...]
            for gi in range(4):
                mixed = mixed + jnp.where(_group_mask(lane, gi), _dot(w_ref[gi], vcb, 1, 0), 0.0)
            o_ref[rows, :] = (_gelu(u_ref[rows, :]) * mixed).astype(BF16)

    return pl.pallas_call(
        body,
        name=name,
        grid=(S // ts,),
        in_specs=[
            pl.BlockSpec((ts, SGU_W), lambda i: (i, _U_COL)),
            pl.BlockSpec((ts, SGU_W), lambda i: (i, _U_COL + 1)),
            pl.BlockSpec((1, SGU_W), lambda i: (0, 0)),
            pl.BlockSpec((4, SGU_CHUNK, SGU_CHUNK), lambda i: (0, 0, 0)),
            pl.BlockSpec((SGU_CHUNK, SGU_W), lambda i: (0, 0)),
        ],
        out_specs=pl.BlockSpec((ts, SGU_W), lambda i: (i, 0)),
        out_shape=jax.ShapeDtypeStruct((S, SGU_W), BF16),
        compiler_params=_params(("parallel",)),
    )(rest, rest, gn.reshape(1, SGU_W), wm, bias)


def _sgu_bwd(rest, gn, wm, wm_t, bias, dsg, *, name):
    S = rest.shape[0]
    ts = _tile(S, 512)
    nc = ts // SGU_CHUNK

    def body(u_ref, v_ref, g_ref, w_ref, wt_ref, b_ref, dsg_ref, dc_ref, dw_ref, db_ref, dg_ref):
        first = pl.program_id(0) == 0

        @pl.when(first)
        def _():
            dw_ref[...] = jnp.zeros_like(dw_ref)
            db_ref[...] = jnp.zeros_like(db_ref)
            dg_ref[...] = jnp.zeros_like(dg_ref)

        gv = g_ref[...]
        lane = _lanes((SGU_CHUNK, SGU_W))
        for c in range(nc):
            rows = slice(c * SGU_CHUNK, (c + 1) * SGU_CHUNK)
            vpre = v_ref[rows, :]
            upre = u_ref[rows, :]
            zv = _gelu(vpre)
            r = lax.rsqrt(jnp.mean(zv * zv, axis=-1, keepdims=True) + EPS)
            zn = zv * r
            vcb = (zn * gv).astype(BF16)
            mixed = b_ref[...]
            for gi in range(4):
                mixed = mixed + jnp.where(_group_mask(lane, gi), _dot(w_ref[gi], vcb, 1, 0), 0.0)
            zu = _gelu(upre)
            dsg_v = dsg_ref[rows, :]
            dc_ref[rows, :SGU_W] = (dsg_v * mixed * _gelu_grad(upre)).astype(BF16)
            dmixed = dsg_v * zu
            db_ref[...] += dmixed
            dvn = jnp.zeros((SGU_CHUNK, SGU_W), F32)
            for gi in range(4):
                dmg = jnp.where(_group_mask(lane, gi), dmixed, 0.0).astype(BF16)
                dw_ref[gi] += _dot(dmg, vcb, 1, 1)
                dvn = dvn + _dot(wt_ref[gi], dmg, 1, 0)
            dg_ref[...] += jnp.sum(dvn * zn, axis=0, keepdims=True)
            dzn = dvn * gv
            dzv = r * (dzn - zn * jnp.mean(dzn * zn, axis=-1, keepdims=True))
            dc_ref[rows, SGU_W:] = (dzv * _gelu_grad(vpre)).astype(BF16)

    blk = pl.BlockSpec((ts, SGU_W), lambda i: (i, 0))
    vec = pl.BlockSpec((1, SGU_W), lambda i: (0, 0))
    w3 = pl.BlockSpec((4, SGU_CHUNK, SGU_CHUNK), lambda i: (0, 0, 0))
    bsp = pl.BlockSpec((SGU_CHUNK, SGU_W), lambda i: (0, 0))
    return pl.pallas_call(
        body,
        name=name,
        grid=(S // ts,),
        in_specs=[
            pl.BlockSpec((ts, SGU_W), lambda i: (i, _U_COL)),
            pl.BlockSpec((ts, SGU_W), lambda i: (i, _U_COL + 1)),
            vec, w3, w3, bsp, blk,
        ],
        out_specs=[pl.BlockSpec((ts, 2 * SGU_W), lambda i: (i, 0)), w3, bsp, vec],
        out_shape=[
            jax.ShapeDtypeStruct((S, 2 * SGU_W), BF16),
            jax.ShapeDtypeStruct((4, SGU_CHUNK, SGU_CHUNK), F32),
            jax.ShapeDtypeStruct((SGU_CHUNK, SGU_W), F32),
            jax.ShapeDtypeStruct((1, SGU_W), F32),
        ],
        compiler_params=_params(("arbitrary",)),
    )(rest, rest, gn.reshape(1, SGU_W), wm, wm_t, bias, dsg)


_GT = 512
_G0 = OFF_G // _GT


def _gate_specs(tm, col_of):
    specs = [pl.BlockSpec((tm, _GT), functools.partial(lambda k, *ids: (col_of(*ids)[0], _G0 + 2 * k + col_of(*ids)[1]), k)) for k in range(3)]
    specs += [pl.BlockSpec((1, _GT), functools.partial(lambda k, *ids: (0, 2 * k + col_of(*ids)[1]), k)) for k in range(3)]
    return specs


def _merge_fwd(rest, bg, ya, yb, yc, *, name):
    S = rest.shape[0]
    tm = _tile(S, 512)

    def body(g1, g2, g3, b1, b2, b3, ya_ref, yb_ref, yc_ref, o_ref):
        acc = _sigmoid(g1[...] + b1[...]) * ya_ref[...]
        acc = acc + _sigmoid(g2[...] + b2[...]) * yb_ref[...]
        acc = acc + _sigmoid(g3[...] + b3[...]) * yc_ref[...]
        o_ref[...] = acc.astype(BF16)

    blk = pl.BlockSpec((tm, _GT), lambda i, j: (i, j))
    return pl.pallas_call(
        body,
        name=name,
        grid=(S // tm, D // _GT),
        in_specs=_gate_specs(tm, lambda i, j: (i, j)) + [blk, blk, blk],
        out_specs=blk,
        out_shape=jax.ShapeDtypeStruct((S, D), BF16),
        compiler_params=_params(("parallel", "parallel")),
    )(rest, rest, rest, bg, bg, bg, ya, yb, yc)


def _merge_bwd(rest, bg, ya, yb, yc, dm, *, name):
    S = rest.shape[0]
    tm = _tile(S, 512)

    def body(g1, g2, g3, b1, b2, b3, ya_ref, yb_ref, yc_ref, dm_ref, dya, dyb, dyc, dg1, dg2, dg3, db1, db2, db3):
        first = pl.program_id(1) == 0
        dmv = dm_ref[...]
        for g_ref, b_ref, y_ref, dy_ref, dg_ref, db_ref in (
            (g1, b1, ya_ref, dya, dg1, db1), (g2, b2, yb_ref, dyb, dg2, db2), (g3, b3, yc_ref, dyc, dg3, db3)):
            gate = _sigmoid(g_ref[...] + b_ref[...])
            dy_ref[...] = (dmv * gate).astype(BF16)
            dpre = dmv * y_ref[...] * gate * (1.0 - gate)
            dg_ref[...] = dpre.astype(BF16)
            part = jnp.sum(dpre, axis=0, keepdims=True)

            @pl.when(first)
            def _():
                db_ref[...] = part

            @pl.when(jnp.logical_not(first))
            def _():
                db_ref[...] += part

    blk = pl.BlockSpec((tm, _GT), lambda j, i: (i, j))
    vec = pl.BlockSpec((1, _GT), lambda j, i: (0, j))
    big = jax.ShapeDtypeStruct((S, D), BF16)
    small = jax.ShapeDtypeStruct((1, D), F32)
    return pl.pallas_call(
        body,
        name=name,
        grid=(D // _GT, S // tm),
        in_specs=_gate_specs(tm, lambda j, i: (i, j)) + [blk, blk, blk, blk],
        out_specs=[blk] * 6 + [vec] * 3,
        out_shape=[big] * 6 + [small] * 3,
        compiler_params=_params(("parallel", "arbitrary")),
    )(rest, rest, rest, bg, bg, bg, ya, yb, yc, dm)


_X_SCALE = XDH ** -0.5


def _xattn_fwd(xq, kv, *, name):
    S = xq.shape[0]
    M = kv.shape[0]
    tq = _tile(S, 512)

    def body(q_ref, k_ref, v_ref, o_ref):
        s = _dot(q_ref[...], k_ref[...], 1, 1) * _X_SCALE
        e = jnp.exp(s - jnp.max(s, axis=-1, keepdims=True))
        p = e / jnp.sum(e, axis=-1, keepdims=True)
        o_ref[...] = _dot(p.astype(BF16), v_ref[...], 1, 0).astype(BF16)

    return pl.pallas_call(
        body,
        name=name,
        grid=(S // tq, XH),
        in_specs=[
            pl.BlockSpec((tq, XDH), lambda i, h: (i, h)),
            pl.BlockSpec((M, XDH), lambda i, h: (0, h)),
            pl.BlockSpec((M, XDH), lambda i, h: (0, XH + h)),
        ],
        out_specs=pl.BlockSpec((tq, XDH), lambda i, h: (i, h)),
        out_shape=jax.ShapeDtypeStruct((S, D), BF16),
        compiler_params=_params(("parallel", "parallel")),
    )(xq, kv, kv)


def _xattn_bwd(xq, kv, do, *, name):
    S = xq.shape[0]
    M = kv.shape[0]
    tq = _tile(S, 512)

    def body(q_ref, k_ref, v_ref, do_ref, dq_ref, dk_ref, dv_ref):
        qb = q_ref[...]
        kb = k_ref[...]
        dob = do_ref[...]
        s = _dot(qb, kb, 1, 1) * _X_SCALE
        e = jnp.exp(s - jnp.max(s, axis=-1, keepdims=True))
        p = e / jnp.sum(e, axis=-1, keepdims=True)
        dp = _dot(dob, v_ref[...], 1, 1)
        ds = (p * (dp - jnp.sum(p * dp, axis=-1, keepdims=True)) * _X_SCALE).astype(BF16)
        dq_ref[...] = _dot(ds, kb, 1, 0).astype(BF16)
        dk_part = _dot(ds, qb, 0, 0)
        dv_part = _dot(p.astype(BF16), dob, 0, 0)

        @pl.when(pl.program_id(1) == 0)
        def _():
            dk_ref[...] = dk_part
            dv_ref[...] = dv_part

        @pl.when(pl.program_id(1) > 0)
        def _():
            dk_ref[...] += dk_part
            dv_ref[...] += dv_part

    qspec = pl.BlockSpec((tq, XDH), lambda h, i: (i, h))
    kspec = pl.BlockSpec((M, XDH), lambda h, i: (0, h))
    dxq, dxk, dxv = pl.pallas_call(
        body,
        name=name,
        grid=(XH, S // tq),
        in_specs=[qspec, kspec, pl.BlockSpec((M, XDH), lambda h, i: (0, XH + h)), qspec],
        out_specs=[qspec, kspec, kspec],
        out_shape=[jax.ShapeDtypeStruct((S, D), BF16), jax.ShapeDtypeStruct((M, D), F32), jax.ShapeDtypeStruct((M, D), F32)],
        compiler_params=_params(("parallel", "arbitrary")),
    )(xq, kv, kv, do)
    return dxq, jnp.concatenate([dxk, dxv], axis=1)


def _adam_math(w, g, m, v):
    m = ADAM_B1 * m + (1.0 - ADAM_B1) * g
    v = ADAM_B2 * v + (1.0 - ADAM_B2) * (g * g)
    m_hat = m / (1.0 - ADAM_B1 ** ADAM_STEP)
    v_hat = v / (1.0 - ADAM_B2 ** ADAM_STEP)
    delta = -ADAM_LR * (m_hat / (jnp.sqrt(v_hat) + ADAM_EPS) + ADAM_WD * w)
    return delta, m, v


def _adamw_sharded(parts, w, m, v, *, name):
    _, R, C = w.shape
    Cp = parts[0].shape[2]
    tm = _tile(R, 256)
    nr = R // tm

    def body(p0_ref, p1_ref, w_ref, m_ref, v_ref, g_ref, d_ref, mo_ref, vo_ref):
        def update(p_ref):
            g = p_ref[0][:, :C].astype(F32)
            for dev in range(1, N_DEV):
                g = g + p_ref[dev][:, :C].astype(F32)
            delta, mn, vn = _adam_math(w_ref[...], g, m_ref[...], v_ref[...])
            g_ref[...] = g
            d_ref[...] = delta
            mo_ref[...] = mn
            vo_ref[...] = vn

        @pl.when(pl.program_id(0) == 0)
        def _():
            update(p0_ref)

        @pl.when(pl.program_id(0) == 1)
        def _():
            update(p1_ref)

    p0 = pl.BlockSpec((N_DEV, tm, Cp), lambda l, i: (0, i * (1 - l) + (nr - 1) * l, 0))
    p1 = pl.BlockSpec((N_DEV, tm, Cp), lambda l, i: (0, i * l, 0))
    blk = pl.BlockSpec((None, tm, C), lambda l, i: (l, i, 0))
    sds = jax.ShapeDtypeStruct(w.shape, F32)
    return pl.pallas_call(
        body,
        name=name,
        grid=(DEPTH, nr),
        in_specs=[p0, p1, blk, blk, blk],
        out_specs=[blk] * 4,
        out_shape=[sds] * 4,
        compiler_params=_params(("arbitrary", "arbitrary")),
    )(parts[0], parts[1], w, m, v)


def _adamw_small(g, w, m, v, *, name):
    n = len(g)

    def body(*refs):
        g_refs, w_refs, m_refs, v_refs = (refs[k * n:(k + 1) * n] for k in range(4))
        d_out, m_out, v_out = (refs[(4 + k) * n:(5 + k) * n] for k in range(3))
        for t in range(n):
            delta, mn, vn = _adam_math(w_refs[t][...], g_refs[t][...], m_refs[t][...], v_refs[t][...])
            d_out[t][...] = delta
            m_out[t][...] = mn
            v_out[t][...] = vn

    vm = pl.BlockSpec(memory_space=pltpu.VMEM)
    shapes = [jax.ShapeDtypeStruct(a.shape, F32) for a in w]
    outs = pl.pallas_call(
        body,
        name=name,
        in_specs=[vm] * (4 * n),
        out_specs=[vm] * (3 * n),
        out_shape=shapes * 3,
        compiler_params=pltpu.CompilerParams(vmem_limit_bytes=VMEM_LIMIT),
    )(*g, *w, *m, *v)
    return outs[:n], outs[n:2 * n], outs[2 * n:]


def _position():
    return lax.axis_index("x"), lax.axis_index("y"), lax.axis_index("c")


def _dev_index(px, py, pc):
    return 4 * px + 2 * py + pc


_ANY = pl.BlockSpec(memory_space=pl.ANY)


def _all_gather(shards, *, name):
    n = len(shards)
    out_shape = [jax.ShapeDtypeStruct((N_DEV, *s.shape), s.dtype) for s in shards]
    n_pieces = len(_pieces(out_shape))

    def body(*refs):
        ins, outs = refs[:n], refs[n:2 * n]
        send_sems, recv_sems, local_sems = refs[2 * n:]
        x, y, c = _position()
        me, sibling = (x, y, c), (x, y, 1 - c)
        chips = [(1 - x, y), (x, 1 - y), (1 - x, 1 - y)]
        pieces = _pieces(outs)

        def copy(i, k, block, to, from_input=False):
            t, rows = pieces[i]
            dst = _cut(outs[t].at[_dev_index(*block)], rows)
            return pltpu.make_async_remote_copy(
                src_ref=_cut(ins[t], rows) if from_input else dst, dst_ref=dst, send_sem=send_sems.at[i, k],
                recv_sem=recv_sems.at[i, k], device_id=to, device_id_type=MESH)

        mine = [pltpu.make_async_copy(_cut(ins[t], rows), _cut(outs[t].at[_dev_index(*me)], rows), local_sems.at[i])
                for i, (t, rows) in enumerate(pieces)]
        for cp in mine:
            cp.start()
        started = []
        for j, chip in enumerate(chips):
            for i in range(n_pieces):
                started.append(copy(i, 1 + j, me, (*chip, c), from_input=True))
                started[-1].start()
        for i in range(n_pieces):
            started.append(copy(i, 0, me, sibling, from_input=True))
            started[-1].start()
        for j, chip in enumerate(chips):
            for i in range(n_pieces):
                copy(i, 1 + j, (*chip, c), me).wait_recv()
                started.append(copy(i, 4 + j, (*chip, c), sibling))
                started[-1].start()
        for i in range(n_pieces):
            copy(i, 0, sibling, me).wait_recv()
        for j, chip in enumerate(chips):
            for i in range(n_pieces):
                copy(i, 4 + j, (*chip, 1 - c), me).wait_recv()
        for cp in started:
            cp.wait_send()
        for cp in mine:
            cp.wait()

    return pl.pallas_call(
        body,
        name=name,
        in_specs=[_ANY] * n,
        out_specs=[_ANY] * n,
        out_shape=out_shape,
        scratch_shapes=[pltpu.SemaphoreType.DMA((n_pieces, 7)), pltpu.SemaphoreType.DMA((n_pieces, 7)),
                        pltpu.SemaphoreType.DMA((n_pieces,))],
        compiler_params=pltpu.CompilerParams(has_side_effects=True),
    )(*shards)


def _peers(x, y, c):
    out = []
    for mask in range(1, N_DEV):
        fx, fy, fc = (mask >> 2) & 1, (mask >> 1) & 1, mask & 1
        out.append((1 - x if fx else x, 1 - y if fy else y, 1 - c if fc else c))
    return out


_HBM = pl.BlockSpec(memory_space=pltpu.HBM)
_SEM = pl.BlockSpec(memory_space=pltpu.SEMAPHORE)


def _own_block_placed(block, like):
    x, y, c = _position()
    return lax.dynamic_update_index_in_dim(lax.empty(like.shape, like.dtype), block, _dev_index(x, y, c), 0)


_COPY_BYTES = 256 << 10
_MAX_PIECES = 8


def _pieces(blocks):
    out = []
    for t, b in enumerate(blocks):
        R, C = b.shape[-2:]
        n = max(1, min(_MAX_PIECES, R * C * jnp.dtype(b.dtype).itemsize // _COPY_BYTES))
        while R % (16 * n):
            n //= 2
        out += [(t, pl.ds(j * (R // n), R // n) if n > 1 else None) for j in range(n)]
    return out


def _cut(block, rows):
    return block if rows is None else block.at[rows]


def _copies(per_piece):
    def mark(fn):
        fn.per_piece = per_piece
        return fn
    return mark


@_copies(N_DEV - 1)
def _plan_exchange(srcs, lands, send_sems, recv_sems, arrivals):
    x, y, c = _position()
    me = _dev_index(x, y, c)
    out = []
    for k, peer in enumerate(_peers(x, y, c)):
        p = _dev_index(*peer)
        for i, (t, rows) in enumerate(_pieces(lands)):
            sems = dict(send_sem=send_sems.at[7 * i + k], recv_sem=recv_sems.at[7 * i + k], device_id=peer, device_id_type=MESH)
            src, dst = (lands[t].at[p], lands[t].at[p]) if arrivals else (srcs[t].at[p], lands[t].at[me])
            out.append(pltpu.make_async_remote_copy(src_ref=_cut(src, rows), dst_ref=_cut(dst, rows), **sems))
    return out


@_copies(4)
def _plan_gather_out(srcs, lands, send_sems, recv_sems, arrivals):
    x, y, c = _position()
    me = _dev_index(x, y, c)
    out = []
    for k, peer in enumerate([(x, y, 1 - c), (1 - x, y, c), (x, 1 - y, c), (1 - x, 1 - y, c)]):
        p = _dev_index(*peer)
        for i, (t, rows) in enumerate(_pieces(lands)):
            sems = dict(send_sem=send_sems.at[4 * i + k], recv_sem=recv_sems.at[4 * i + k], device_id=peer, device_id_type=MESH)
            src, dst = (lands[t].at[p], lands[t].at[p]) if arrivals else (srcs[t], lands[t].at[me])
            out.append(pltpu.make_async_remote_copy(src_ref=_cut(src, rows), dst_ref=_cut(dst, rows), **sems))
    return out


@_copies(3)
def _plan_gather_pass(srcs, lands, send_sems, recv_sems, arrivals):
    x, y, c = _position()
    sibling = (x, y, 1 - c)
    out = []
    for k, chip in enumerate([(1 - x, y), (x, 1 - y), (1 - x, 1 - y)]):
        p = _dev_index(*chip, 1 - c) if arrivals else _dev_index(*chip, c)
        for i, (t, rows) in enumerate(_pieces(lands)):
            sems = dict(send_sem=send_sems.at[3 * i + k], recv_sem=recv_sems.at[3 * i + k], device_id=sibling, device_id_type=MESH)
            block = _cut(lands[t].at[p], rows)
            out.append(pltpu.make_async_remote_copy(src_ref=block, dst_ref=block, **sems))
    return out


def _split_start(plan, srcs, lands, *, after=None, name):
    n_src, n = len(srcs), len(srcs) + len(lands)
    n_sem = plan.per_piece * len(_pieces(lands))
    order = [] if after is None else [after]

    def body(*refs):
        send_sems, recv_sems = refs[n + len(order):n + len(order) + 2]
        token = refs[-1]
        for cp in plan(refs[:n_src], refs[n_src:n], send_sems, recv_sems, arrivals=False):
            cp.start()
        token[...] = jnp.zeros_like(token)

    hbm = lambda a: pltpu.HBM(a.shape, a.dtype)
    outs = pl.pallas_call(
        body,
        name=name,
        in_specs=[_HBM] * n + [_ANY] * len(order),
        out_specs=[_SEM, _SEM] + [_HBM] * n + [pl.BlockSpec(memory_space=pltpu.VMEM)],
        out_shape=[pltpu.SemaphoreType.DMA((n_sem,)), pltpu.SemaphoreType.DMA((n_sem,))] + [hbm(a) for a in (*srcs, *lands)]
        + [jax.ShapeDtypeStruct(_TOKEN, F32)],
        input_output_aliases={i: 2 + i for i in range(n)},
        compiler_params=pltpu.CompilerParams(has_side_effects=pltpu.SideEffectType.DATAFLOW_SIDE_EFFECTING),
    )(*[pltpu.with_memory_space_constraint(a, pltpu.HBM) for a in (*srcs, *lands)], *order)
    return (outs[0], outs[1], outs[2:2 + n_src], outs[2 + n_src:2 + n]), outs[-1]


def _split_wait(plan, state, after, *, name):
    send_sems, recv_sems, srcs, lands = state
    n_src, n = len(srcs), len(srcs) + len(lands)

    def body(*refs):
        send_refs, recv_refs = refs[n:n + 2]
        for cp in plan(refs[:n_src], refs[n_src:n], send_refs, recv_refs, arrivals=False):
            cp.wait_send()
        for cp in plan(refs[:n_src], refs[n_src:n], send_refs, recv_refs, arrivals=True):
            cp.wait_recv()

    hbm = lambda a: pltpu.HBM(a.shape, a.dtype)
    outs = pl.pallas_call(
        body,
        name=name,
        in_specs=[_HBM] * n + [_SEM, _SEM, _ANY],
        out_specs=[_HBM] * n,
        out_shape=[hbm(a) for a in (*srcs, *lands)],
        input_output_aliases={i: i for i in range(n)},
        compiler_params=pltpu.CompilerParams(has_side_effects=pltpu.SideEffectType.DATAFLOW_SIDE_EFFECTING),
    )(*srcs, *lands, send_sems, recv_sems, after)
    return outs[n_src:]


def _all_reduce(g_local, *, name):
    R, C = g_local.shape
    nc = next(n for n in (4, 3, 2, 1) if R % (8 * n) == 0)
    chunks = [pl.ds(j * (R // nc), R // nc) for j in range(nc)]

    def body(g_ref, o_ref, buf, send_sems, recv_sems):
        x, y, c = _position()
        me = _dev_index(x, y, c)
        peers = _peers(x, y, c)
        copies = []
        for k, peer in enumerate(peers):
            for j, rows in enumerate(chunks):
                copies.append(pltpu.make_async_remote_copy(
                    src_ref=g_ref.at[rows], dst_ref=buf.at[me, rows], send_sem=send_sems.at[nc * k + j],
                    recv_sem=recv_sems.at[nc * k + j], device_id=peer, device_id_type=MESH))
                copies[-1].start()
        buf[me] = g_ref[...]
        for k, peer in enumerate(peers):
            for j, rows in enumerate(chunks):
                dst = buf.at[_dev_index(*peer), rows]
                pltpu.make_async_remote_copy(
                    src_ref=dst, dst_ref=dst, send_sem=send_sems.at[nc * k + j], recv_sem=recv_sems.at[nc * k + j],
                    device_id=peer, device_id_type=MESH).wait_recv()
        for cp in copies:
            cp.wait_send()
        g = buf[0]
        for dev in range(1, N_DEV):
            g = g + buf[dev]
        o_ref[...] = g

    vm = pl.BlockSpec(memory_space=pltpu.VMEM)
    return pl.pallas_call(
        body,
        name=name,
        in_specs=[vm],
        out_specs=vm,
        out_shape=jax.ShapeDtypeStruct((R, C), F32),
        scratch_shapes=[pltpu.VMEM((N_DEV, R, C), F32), pltpu.SemaphoreType.DMA((7 * nc,)), pltpu.SemaphoreType.DMA((7 * nc,))],
        compiler_params=pltpu.CompilerParams(has_side_effects=True, vmem_limit_bytes=VMEM_LIMIT),
    )(g_local)


def _block_diag(w):
    out = jnp.zeros((POOL_W, POOL_W), w.dtype)
    for gi in range(4):
        out = out.at[64 * gi:64 * (gi + 1), 64 * gi:64 * (gi + 1)].set(w[gi])
    return out


def _layer_consts(sp, l):
    causal = jnp.tril(jnp.ones((SGU_CHUNK, SGU_CHUNK), F32))
    wm = (sp["sgu_w"][l] * causal[None]).astype(BF16)
    wbd = _block_diag(sp["pool_w"][l]).astype(BF16)
    return dict(
        wbd=wbd, wbd_t=wbd.T, wm=wm, wm_t=wm.transpose(0, 2, 1),
        sgu_bias=jnp.repeat(sp["sgu_b"][l].T, 64, axis=1),
        bpad=jnp.pad(sp["b_forget"][l], (0, F_LANES - FOX_H)).reshape(1, F_LANES),
        bg=sp["b_gate"][l].reshape(1, 3 * D),
    )


def _relu2(acc):
    return acc, jnp.square(jnp.maximum(acc, 0.0))


def _relu2_grad(acc, z):
    return (acc * 2.0 * jnp.maximum(z, 0.0),)


def _layer_fwd(l, x, mem, source, sp):
    S = x.shape[0]
    t = _tile(S, 256)
    c = _layer_consts(sp, l)
    n = f"l{l}_"
    W, after = source(l, "begin", x)
    h = _rms_fwd(x, sp["norm_mix_g"][l], after=after, name=n + "norm_mix")
    qkv = _mm(h, W["qkv"], out_dtypes=(BF16,), name=n + "qkv")
    rest = _mm(h, W["rest"], name=n + "rest")
    pa = _pool_fwd(rest, c["wbd"], sp["pool_scale"][l], name=n + "pool")
    cum, cum_t = _fox_prep(rest, c["bpad"], name=n + "fox_prep")
    fk3 = cum_t[:FOX_H].reshape(FOX_H, S // t, t)
    o, lse = _fox_fwd(qkv, cum, fk3, name=n + "fox")
    more, _ = source(l, "attended", o)
    W.update(more)
    sg = _sgu_fwd(rest, sp["sgu_norm_g"][l], c["wm"], c["sgu_bias"], name=n + "sgu")
    more, after = source(l, "mixed", sg)
    W.update(more)
    ya = _mm(pa, W["ba"], after=after, name=n + "branch_a")
    yb = _mm(o, W["bb"], name=n + "branch_b")
    yc = _mm(sg, W["bc"], name=n + "branch_c")
    merged = _merge_fwd(rest, c["bg"], ya, yb, yc, name=n + "merge")
    x1 = _mm(merged, W["out"], extras=(x,), epilogue=_add, name=n + "out")
    hx = _rms_fwd(x1, sp["norm_xattn_g"][l], name=n + "norm_xattn")
    hm = _rms_fwd(mem, sp["norm_mem_g"][l], name=n + "norm_mem")
    xq = _mm(hx, W["xq"], out_dtypes=(BF16,), name=n + "xq")
    kv = _mm(hm, W["xkv"], out_dtypes=(BF16,), name=n + "xkv")
    o2 = _xattn_fwd(xq, kv, name=n + "xattn")
    x2 = _mm(o2, W["xo"], extras=(x1,), epilogue=_add, name=n + "xo")
    hf = _rms_fwd(x2, sp["norm_ffn_g"][l], name=n + "norm_ffn")
    z, act = _mm(hf, W["ff1"], epilogue=_relu2, out_dtypes=(F32, BF16), name=n + "ff1")
    _, after = source(l, "expanded", act)
    x3 = _mm(act, W["ff2"], extras=(x2,), epilogue=_add, after=after, name=n + "ff2")
    saved = dict(x=x, h=h, qkv=qkv, rest=rest, pa=pa, cum=cum, fk3=fk3, o=o, lse=lse, sg=sg, ya=ya, yb=yb, yc=yc,
                 merged=merged, x1=x1, hx=hx, hm=hm, xq=xq, kv=kv, o2=o2, x2=x2, hf=hf, z=z, act=act, c=c)
    return x3, saved, W


def _layer_bwd(l, dx3, sv, mem, W, sp, grads_done):
    S = dx3.shape[0]
    c = sv["c"]
    n = f"l{l}b_"
    bf = dict(out_dtypes=(BF16,))
    gw, gs = {}, {}
    gw["ff2"] = _mm(sv["act"], dx3, ta=True, name=n + "dw_ff2", **bf)
    dz = _mm(dx3, W["ff2"], tb=True, extras=(sv["z"],), epilogue=_relu2_grad, name=n + "dz", **bf)
    gw["ff1"] = _mm(sv["hf"], dz, ta=True, shard_out=True, name=n + "dw_ff1", **bf)
    dhf = _mm(dz, W["ff1"], tb=True, name=n + "dhf")
    dx2, gs["norm_ffn_g"] = _rms_bwd(sv["x2"], sp["norm_ffn_g"][l], dhf, dx3, name=n + "dnorm_ffn")
    gw["xo"] = _mm(sv["o2"], dx2, ta=True, name=n + "dw_xo", **bf)
    do2 = _mm(dx2, W["xo"], tb=True, name=n + "do2", **bf)
    dxq, dkv = _xattn_bwd(sv["xq"], sv["kv"], do2, name=n + "dxattn")
    gw["xq"] = _mm(sv["hx"], dxq, ta=True, name=n + "dw_xq", **bf)
    gw["xkv"] = _mm(sv["hm"], dkv, ta=True, shard_out=True, name=n + "dw_xkv", **bf)
    dhm = _mm(dkv, W["xkv"], tb=True, name=n + "dhm")
    _, gs["norm_mem_g"] = _rms_bwd(mem, sp["norm_mem_g"][l], dhm, jnp.zeros_like(mem), name=n + "dnorm_mem")
    dhx = _mm(dxq, W["xq"], tb=True, name=n + "dhx")
    dx1, gs["norm_xattn_g"] = _rms_bwd(sv["x1"], sp["norm_xattn_g"][l], dhx, dx2, name=n + "dnorm_xattn")
    after, gw = grads_done(l, gw), {}
    gw["out"] = _mm(sv["merged"], dx1, ta=True, name=n + "dw_out", **bf)
    dm = _mm(dx1, W["out"], tb=True, after=after, name=n + "dmerged")
    dya, dyb, dyc, dg1, dg2, dg3, db1, db2, db3 = _merge_bwd(sv["rest"], c["bg"], sv["ya"], sv["yb"], sv["yc"], dm, name=n + "dmerge")
    gs["b_gate"] = jnp.concatenate([db1, db2, db3], axis=1).reshape(3 * D)
    gw["ba"] = _mm(sv["pa"], dya, ta=True, shard_out=True, name=n + "dw_ba", **bf)
    gw["bb"] = _mm(sv["o"], dyb, ta=True, shard_out=True, name=n + "dw_bb", **bf)
    gw["bc"] = _mm(sv["sg"], dyc, ta=True, shard_out=True, name=n + "dw_bc", **bf)
    after, gw = grads_done(l, gw), {}
    dpa = _mm(dya, W["ba"], tb=True, name=n + "dpa")
    do = _mm(dyb, W["bb"], tb=True, after=after, name=n + "do", **bf)
    dsg = _mm(dyc, W["bc"], tb=True, name=n + "dsg")
    da, dwbd, dscale = _pool_bwd(sv["rest"], c["wbd"], c["wbd_t"], sp["pool_scale"][l], dpa, name=n + "dpool")
    gs["pool_w"] = jnp.stack([dwbd[64 * gi:64 * (gi + 1), 64 * gi:64 * (gi + 1)] for gi in range(4)])
    gs["pool_scale"] = dscale.reshape(POOL_W)
    dq, dk, dv, dfq, dfk = _fox_bwd(sv["qkv"], sv["cum"], sv["fk3"], sv["o"], do, sv["lse"], name=n + "dfox")
    dcum = dfq[:, :, :2].transpose(1, 0, 2).reshape(S, FOX_H) + dfk.reshape(FOX_H, S).T
    df, dbf = _fox_post(sv["rest"], c["bpad"], jnp.pad(dcum, ((0, 0), (0, F_LANES - FOX_H))), name=n + "dfox_post")
    gs["b_forget"] = dbf[0, :FOX_H]
    dc, dwm, dbias, dgn = _sgu_bwd(sv["rest"], sp["sgu_norm_g"][l], c["wm"], c["wm_t"], c["sgu_bias"], dsg, name=n + "dsgu")
    gs["sgu_w"] = dwm * jnp.tril(jnp.ones((SGU_CHUNK, SGU_CHUNK), F32))[None]
    gs["sgu_b"] = dbias.reshape(SGU_CHUNK, 4, 64).sum(axis=2).T
    gs["sgu_norm_g"] = dgn.reshape(SGU_W)
    dqkv = jnp.concatenate([dq, dk, dv], axis=1)
    drest = jnp.concatenate([da, df, jnp.zeros((S, OFF_C - OFF_F - F_LANES), BF16), dc, dg1, dg2, dg3], axis=1)
    gw["qkv"] = _mm(sv["h"], dqkv, ta=True, name=n + "dw_qkv", **bf)
    gw["rest"] = _mm(sv["h"], drest, ta=True, name=n + "dw_rest", **bf)
    after = grads_done(l, gw)
    dh = _mm(dqkv, W["qkv"], tb=True, after=after, name=n + "dh_qkv")
    dh = _mm(drest, W["rest"], tb=True, extras=(dh,), epilogue=_add, name=n + "dh")
    dx, gs["norm_mix_g"] = _rms_bwd(sv["x"], sp["norm_mix_g"][l], dh, dx1, name=n + "dnorm_mix")
    return dx, gs


def _local_step(x, mem, target, sp, source, grads_done):
    saved, Ws = [], []
    for l in range(DEPTH):
        x, sv, W = _layer_fwd(l, x, mem, source, sp)
        saved.append(sv)
        Ws.append(W)
    loss, dx, dgf = _final_loss(x, sp["final_norm_g"], target, name="final_loss")
    gss = [None] * DEPTH
    for l in reversed(range(DEPTH)):
        dx, gss[l] = _layer_bwd(l, dx, saved[l], mem, Ws[l], sp, grads_done)
    small = {k: jnp.stack([gss[l][k] for l in range(DEPTH)]) for k in gss[0]}
    small["final_norm_g"] = dgf
    return loss, dx, small


_SMALL = ["norm_mix_g", "b_forget", "pool_w", "pool_scale", "sgu_norm_g", "sgu_w", "sgu_b", "b_gate", "norm_xattn_g",
          "norm_mem_g", "norm_ffn_g", "final_norm_g"]
_COL = {"w_branch_a": "ba", "w_branch_b": "bb", "w_branch_c": "bc", "w_xkv": "xkv", "w_ff1": "ff1"}
_ROW = {"w_out": "out", "w_xq": "xq", "w_xo": "xo", "w_ff2": "ff2"}
_BIG = ["w_in", "w_branch_a", "w_branch_b", "w_branch_c", "w_out", "w_xq", "w_xkv", "w_xo", "w_ff1", "w_ff2"]
_PACK_LANES = 128


def _as_rows(a):
    return a.reshape(-1, a.shape[-1])


def _pack(tensors):
    rows = []
    for a in tensors:
        flat = a.reshape(-1)
        flat = jnp.pad(flat, (0, (-flat.shape[0]) % (8 * _PACK_LANES)))
        rows.append(flat.reshape(-1, _PACK_LANES))
    return jnp.concatenate(rows, axis=0)


def _unpack(packed, like):
    out, r = [], 0
    for a in like:
        size = math.prod(a.shape)
        nr = 8 * (-(-size // (8 * _PACK_LANES)))
        out.append(packed[r:r + nr].reshape(-1)[:size].reshape(a.shape))
        r += nr
    return out


_SHARD_IN = N_IN // N_DEV
_SHARD_IN_PAD = -(-_SHARD_IN // 128) * 128


def _columns(pieces, start, stop):
    out, at = [], 0
    for p in pieces:
        lo, hi = max(start, at), min(stop, at + p.shape[1])
        if lo < hi:
            out.append(p[:, lo - at:hi - at])
        at += p.shape[1]
    return out


def _split_w_in(blocks):
    K = blocks[0].shape[0]
    pad = jnp.zeros((K, OFF_C - OFF_F - FOX_H), blocks[0].dtype)
    cols = functools.partial(_columns, blocks)
    rest = jnp.concatenate(cols(0, R_OFF_Q) + cols(R_OFF_F, R_OFF_C) + [pad] + cols(R_OFF_C, N_IN), axis=1)
    return jnp.concatenate(cols(R_OFF_Q, R_OFF_F), axis=1), rest


def _join_w_in(qkv, rest):
    in_order = [rest[:, :R_OFF_Q], qkv, rest[:, OFF_F:OFF_F + FOX_H], rest[:, OFF_C:]]
    pad = jnp.zeros((qkv.shape[0], _SHARD_IN_PAD - _SHARD_IN), qkv.dtype)
    return jnp.stack([jnp.concatenate(_columns(in_order, _SHARD_IN * d, _SHARD_IN * (d + 1)) + [pad], axis=1) for d in range(N_DEV)])


_FIRST = ["w_in"]
_LATER = [k for k in _BIG if k not in _FIRST]


def _layer_weights(gathered):
    W = {}
    if "w_in" in gathered:
        W.update(zip(("qkv", "rest"), _split_w_in([gathered["w_in"][d][:, :_SHARD_IN] for d in range(N_DEV)])))
    for name, key in _COL.items():
        if name in gathered:
            W[key] = _Gathered(gathered[name])
    for name, key in _ROW.items():
        if name in gathered:
            W[key] = gathered[name].reshape(-1, gathered[name].shape[-1])
    return W


def _grad_blocks(gw):
    parts = {}
    if "qkv" in gw:
        parts["w_in"] = _join_w_in(gw["qkv"], gw["rest"])
    for name, key in _COL.items():
        if key in gw:
            parts[name] = gw[key]
    for name, key in _ROW.items():
        if key in gw:
            parts[name] = gw[key].reshape(N_DEV, -1, gw[key].shape[-1])
    return parts


def kernel(x, mem, norm_mix_g, w_in, b_forget, pool_w, pool_scale, sgu_norm_g, sgu_w, sgu_b, w_branch_a, w_branch_b, w_branch_c, b_gate, w_out, norm_xattn_g, norm_mem_g, w_xq, w_xkv, w_xo, norm_ffn_g, w_ff1, w_ff2, final_norm_g, loss_target, m_norm_mix_g, m_w_in, m_b_forget, m_pool_w, m_pool_scale, m_sgu_norm_g, m_sgu_w, m_sgu_b, m_w_branch_a, m_w_branch_b, m_w_branch_c, m_b_gate, m_w_out, m_norm_xattn_g, m_norm_mem_g, m_w_xq, m_w_xkv, m_w_xo, m_norm_ffn_g, m_w_ff1, m_w_ff2, m_final_norm_g, v_norm_mix_g, v_w_in, v_b_forget, v_pool_w, v_pool_scale, v_sgu_norm_g, v_sgu_w, v_sgu_b, v_w_branch_a, v_w_branch_b, v_w_branch_c, v_b_gate, v_w_out, v_norm_xattn_g, v_norm_mem_g, v_w_xq, v_w_xkv, v_w_xo, v_norm_ffn_g, v_w_ff1, v_w_ff2, v_final_norm_g):
    names = ["norm_mix_g", "w_in", "b_forget", "pool_w", "pool_scale", "sgu_norm_g", "sgu_w", "sgu_b", "w_branch_a", "w_branch_b",
             "w_branch_c", "b_gate", "w_out", "norm_xattn_g", "norm_mem_g", "w_xq", "w_xkv", "w_xo", "norm_ffn_g", "w_ff1", "w_ff2",
             "final_norm_g"]
    w = dict(zip(names, [norm_mix_g, w_in, b_forget, pool_w, pool_scale, sgu_norm_g, sgu_w, sgu_b, w_branch_a, w_branch_b, w_branch_c,
                         b_gate, w_out, norm_xattn_g, norm_mem_g, w_xq, w_xkv, w_xo, norm_ffn_g, w_ff1, w_ff2, final_norm_g]))
    m = dict(zip(names, [m_norm_mix_g, m_w_in, m_b_forget, m_pool_w, m_pool_scale, m_sgu_norm_g, m_sgu_w, m_sgu_b, m_w_branch_a,
                         m_w_branch_b, m_w_branch_c, m_b_gate, m_w_out, m_norm_xattn_g, m_norm_mem_g, m_w_xq, m_w_xkv, m_w_xo,
                         m_norm_ffn_g, m_w_ff1, m_w_ff2, m_final_norm_g]))
    v = dict(zip(names, [v_norm_mix_g, v_w_in, v_b_forget, v_pool_w, v_pool_scale, v_sgu_norm_g, v_sgu_w, v_sgu_b, v_w_branch_a,
                         v_w_branch_b, v_w_branch_c, v_b_gate, v_w_out, v_norm_xattn_g, v_norm_mem_g, v_w_xq, v_w_xkv, v_w_xo,
                         v_norm_ffn_g, v_w_ff1, v_w_ff2, v_final_norm_g]))

    sp = {k: w[k] for k in _SMALL}
    shards = [{k: w[k][l].astype(BF16) for k in _BIG} for l in range(DEPTH)]
    for sh in shards:
        sh["w_in"] = jnp.pad(sh["w_in"], ((0, 0), (0, _SHARD_IN_PAD - _SHARD_IN)))
    me = _dev_index(*_position())

    def gather_out(l, keys, name, after=None):
        srcs = [shards[l][k] for k in keys]
        lands = [_own_block_placed(a, jax.ShapeDtypeStruct((N_DEV, *a.shape), a.dtype)) for a in srcs]
        state, token = _split_start(_plan_gather_out, srcs, lands, after=after, name=name + "_out_start")
        return (keys, name, state), token

    def gather_pass(job, value):
        keys, name, state = job
        lands = _split_wait(_plan_gather_out, state, value, name=name + "_out_wait")
        state, token = _split_start(_plan_gather_pass, [], lands, name=name + "_pass_start")
        return (keys, name, state), token, lands[0]

    def gather_end(job, value):
        keys, name, state = job
        return _layer_weights(dict(zip(keys, _split_wait(_plan_gather_pass, state, value, name=name + "_pass_wait"))))

    jobs = {}

    def source(l, point, value):
        if (l, point) == (0, "begin"):
            first = _all_gather([shards[0][k] for k in _FIRST], name="gather_l0_first")
            jobs["l0"], token = gather_out(0, _LATER, "gather_l0")
            return _layer_weights(dict(zip(_FIRST, first))), token
        if (l, point) == (0, "attended"):
            jobs["l0"], _, arrived = gather_pass(jobs["l0"], value)
            jobs["l1"], jobs["token"] = gather_out(1, _BIG, "gather_l1", after=arrived)
            return {}, None
        if (l, point) == (0, "mixed"):
            return gather_end(jobs.pop("l0"), value), jobs.pop("token")
        if (l, point) == (0, "expanded"):
            jobs["l1"], token, _ = gather_pass(jobs["l1"], value)
            return {}, token
        if (l, point) == (1, "begin"):
            return gather_end(jobs.pop("l1"), value), None
        return {}, None

    received = [{} for _ in range(DEPTH)]
    travelling = []

    def grads_done(l, gw):
        blocks = _grad_blocks(gw)
        keys = [k for k in _BIG if k in blocks]
        parts = [blocks[k] for k in keys]
        group = f"exchange_grads_l{l}_" + ("in" if "w_in" in blocks else "merge" if "w_out" in blocks else "mlp")
        lands = [_own_block_placed(lax.dynamic_index_in_dim(p, me, 0, keepdims=False), p) for p in parts]
        state, token = _split_start(_plan_exchange, parts, lands, name=group + "_start")
        travelling.append((l, keys, state, group + "_wait"))
        return token

    loss, dx, small = _local_step(x[0], mem[0], loss_target[0], sp, source, grads_done)
    loss = lax.psum(loss[0, 0], ("x", "y", "c"))
    grads, deltas, new_m, new_v = {}, {}, {}, {}
    done = dx
    for group_keys in dict.fromkeys(tuple(keys) for _, keys, _, _ in travelling):
        for l, keys, state, wait_name in travelling:
            if tuple(keys) == group_keys:
                received[l].update(zip(keys, _split_wait(_plan_exchange, state, done, name=wait_name)))
        for k in group_keys:
            outs = _adamw_sharded([received[l][k] for l in range(DEPTH)], w[k], m[k], v[k], name="adamw_" + k)
            grads[k], deltas[k], new_m[k], new_v[k] = outs
        done = grads[group_keys[-1]]
    like = [w[k] for k in _SMALL]
    g_small = _unpack(_all_reduce(_pack([small[k] for k in _SMALL]), name="all_reduce_small"), like)
    rows = lambda d: [_as_rows(d[k]) for k in _SMALL]
    outs = _adamw_small([_as_rows(g) for g in g_small], rows(w), rows(m), rows(v), name="adamw_small")
    grads.update(zip(_SMALL, g_small))
    for dst, vals in zip((deltas, new_m, new_v), outs):
        dst.update({k: a.reshape(w[k].shape) for k, a in zip(_SMALL, vals)})

    return (loss, dx[None], *[grads[k] for k in names], *[deltas[k] for k in names], *[new_m[k] for k in names],
            *[new_v[k] for k in names])
```

```python
import functools
import math

import jax
import jax.numpy as jnp
from jax import lax
from jax.experimental import pallas as pl
from jax.experimental.pallas import tpu as pltpu

F32 = jnp.float32
BF16 = jnp.bfloat16
MESH = pl.DeviceIdType.MESH

N_DEV = 8
D = 1024
DEPTH = 2
EPS = 1e-6
NEG = -1e30
POOL_W = 256
FOX_H = 8
FOX_DH = 64
FOX_W = 512
SGU_W = 256
SGU_CHUNK = 128
XH = 4
XDH = 256
N_IN = 5384
R_OFF_Q, R_OFF_F, R_OFF_C = 256, 1792, 1800
QKV_W = 3 * FOX_W
OFF_A, OFF_F, OFF_C, OFF_G, REST_W = 0, 256, 512, 1024, 4096
F_LANES = 128

ADAM_LR = 0.001
ADAM_B1 = 0.9
ADAM_B2 = 0.999
ADAM_EPS = 1e-08
ADAM_WD = 0.01
ADAM_STEP = 10

VMEM_LIMIT = 56 * 1024 * 1024


def _tile(n, pref):
    t = min(n, pref)
    while n % t:
        t -= 128
    assert t > 0, (n, pref)
    return t


def _params(sem=None):
    return pltpu.CompilerParams(dimension_semantics=sem, vmem_limit_bytes=VMEM_LIMIT)


def _dot(a, b, ca, cb):
    return lax.dot_general(a, b, (((ca,), (cb,)), ((), ())), preferred_element_type=F32)


def _sigmoid(z):
    return 1.0 / (1.0 + jnp.exp(-z))


_GELU_K = math.sqrt(2.0 / math.pi)
_GELU_C = 0.044715


def _gelu(x):
    return 0.5 * x * (1.0 + jnp.tanh(_GELU_K * (x + _GELU_C * x * x * x)))


def _gelu_grad(x):
    t = jnp.tanh(_GELU_K * (x + _GELU_C * x * x * x))
    return 0.5 * (1.0 + t) + 0.5 * x * (1.0 - t * t) * _GELU_K * (1.0 + 3.0 * _GELU_C * x * x)


def _rows(shape):
    return lax.broadcasted_iota(jnp.int32, shape, 0)


def _lanes(shape):
    return lax.broadcasted_iota(jnp.int32, shape, 1)


class _Gathered:
    def __init__(self, arr):
        self.arr = arr
        self.shape = (arr.shape[1], N_DEV * arr.shape[2])


_TOKEN = (8, 128)


def _mm(a, b, *, ta=False, tb=False, extras=(), epilogue=None, out_dtypes=(F32,), shard_out=False, after=None, tm=None, tn=512, tk=None,
        name):
    M, K = (a.shape[1], a.shape[0]) if ta else a.shape
    N, Kb = b.shape if tb else b.shape[::-1]
    assert Kb == K, (a.shape, b.shape, ta, tb)
    gathered = isinstance(b, _Gathered)
    if gathered:
        if tb:
            tk = b.arr.shape[2]
        else:
            tn = b.arr.shape[2]
    if shard_out:
        tn = N // N_DEV
    tm = _tile(M, tm or (1024 if ta else 2048))
    tn = _tile(N, tn)
    tk = _tile(K, tk or (2048 if ta else 1024))
    nk = K // tk
    ca, cb = (0 if ta else 1), (1 if tb else 0)
    n_ex, n_out = len(extras), len(out_dtypes)
    tokens = [] if after is None else [after]
    n_in = 2 + n_ex + len(tokens)
    if epilogue is None:
        epilogue = lambda acc: (acc,)

    def body(*refs):
        a_ref, b_ref = refs[:2]
        ex_refs = refs[2:2 + n_ex]
        o_refs = refs[n_in:n_in + n_out]
        part = _dot(a_ref[...].astype(BF16), b_ref[...].astype(BF16), ca, cb)

        def finish(acc):
            for o_ref, val in zip(o_refs, epilogue(acc, *[e[...] for e in ex_refs])):
                o_ref[...] = val.astype(o_ref.dtype)

        if nk == 1:
            finish(part)
        else:
            acc_ref = refs[-1]
            k = pl.program_id(2)

            @pl.when(k == 0)
            def _():
                acc_ref[...] = part

            @pl.when(k > 0)
            def _():
                acc_ref[...] += part

            @pl.when(k == nk - 1)
            def _():
                finish(acc_ref[...])

    a_spec = pl.BlockSpec((tk, tm), lambda i, j, k: (k, i)) if ta else pl.BlockSpec((tm, tk), lambda i, j, k: (i, k))
    if not gathered:
        b_arr = b
        b_spec = pl.BlockSpec((tn, tk), lambda i, j, k: (j, k)) if tb else pl.BlockSpec((tk, tn), lambda i, j, k: (k, j))
    else:
        b_arr = b.arr
        if tb:
            b_spec = pl.BlockSpec((None, tn, tk), lambda i, j, k: (k, j, 0))
        else:
            b_spec = pl.BlockSpec((None, tk, tn), lambda i, j, k: (j, k, 0))
    tile = pl.BlockSpec((tm, tn), lambda i, j, k: (i, j))
    if shard_out:
        out_specs = [pl.BlockSpec((None, tm, tn), lambda i, j, k: (j, i, 0))] * n_out
        out_shape = [jax.ShapeDtypeStruct((N_DEV, M, tn), dt) for dt in out_dtypes]
    else:
        out_specs = [tile] * n_out
        out_shape = [jax.ShapeDtypeStruct((M, N), dt) for dt in out_dtypes]
    size = lambda dt: jnp.dtype(dt).itemsize
    vmem = 2 * (tm * tk * size(a.dtype) + tk * tn * size(b_arr.dtype)
                + tm * tn * (sum(size(e.dtype) for e in extras) + sum(map(size, out_dtypes))))
    vmem += tm * tn * 4 * (nk > 1)
    assert vmem <= VMEM_LIMIT - (4 << 20), (name, vmem)
    outs = pl.pallas_call(
        body,
        name=name,
        grid=(M // tm, N // tn, nk),
        in_specs=[a_spec, b_spec] + [tile] * n_ex + [pl.BlockSpec(_TOKEN, lambda i, j, k: (0, 0))] * len(tokens),
        out_specs=out_specs,
        out_shape=out_shape,
        scratch_shapes=[pltpu.VMEM((tm, tn), F32)] if nk > 1 else [],
        compiler_params=_params(("parallel", "parallel", "arbitrary")),
    )(a, b_arr, *extras, *tokens)
    return outs[0] if n_out == 1 else outs


def _add(acc, res):
    return (acc + res,)


def _rms_fwd(x, g, *, after=None, name):
    R, C = x.shape
    tm = _tile(R, 256)
    tokens = [] if after is None else [after]

    def body(x_ref, g_ref, *rest):
        xv = x_ref[...]
        r = lax.rsqrt(jnp.mean(xv * xv, axis=-1, keepdims=True) + EPS)
        rest[-1][...] = (xv * r * g_ref[...]).astype(BF16)

    return pl.pallas_call(
        body,
        name=name,
        grid=(R // tm,),
        in_specs=[pl.BlockSpec((tm, C), lambda i: (i, 0)), pl.BlockSpec((1, C), lambda i: (0, 0))]
        + [pl.BlockSpec(_TOKEN, lambda i: (0, 0))] * len(tokens),
        out_specs=pl.BlockSpec((tm, C), lambda i: (i, 0)),
        out_shape=jax.ShapeDtypeStruct((R, C), BF16),
        compiler_params=_params(("parallel",)),
    )(x, g.reshape(1, C), *tokens)


def _rms_bwd(x, g, dh, dres, *, name):
    R, C = x.shape
    tm = _tile(R, 256)

    def body(x_ref, g_ref, dh_ref, dres_ref, dx_ref, dg_ref):
        xv = x_ref[...]
        r = lax.rsqrt(jnp.mean(xv * xv, axis=-1, keepdims=True) + EPS)
        xn = xv * r
        dh_v = dh_ref[...].astype(F32)
        dxn = dh_v * g_ref[...]
        dx_ref[...] = r * (dxn - xn * jnp.mean(dxn * xn, axis=-1, keepdims=True)) + dres_ref[...]
        part = jnp.sum(dh_v * xn, axis=0, keepdims=True)

        @pl.when(pl.program_id(0) == 0)
        def _():
            dg_ref[...] = part

        @pl.when(pl.program_id(0) > 0)
        def _():
            dg_ref[...] += part

    row = pl.BlockSpec((tm, C), lambda i: (i, 0))
    vec = pl.BlockSpec((1, C), lambda i: (0, 0))
    dx, dg = pl.pallas_call(
        body,
        name=name,
        grid=(R // tm,),
        in_specs=[row, vec, row, row],
        out_specs=[row, vec],
        out_shape=[jax.ShapeDtypeStruct((R, C), F32), jax.ShapeDtypeStruct((1, C), F32)],
        compiler_params=_params(("arbitrary",)),
    )(x, g.reshape(1, C), dh, dres)
    return dx, dg.reshape(C)


def _final_loss(x, g, target, *, name):
    R, C = x.shape
    tm = _tile(R, 256)

    def body(x_ref, g_ref, t_ref, loss_ref, dx_ref, dg_ref):
        xv = x_ref[...]
        r = lax.rsqrt(jnp.mean(xv * xv, axis=-1, keepdims=True) + EPS)
        xn = xv * r
        gv = g_ref[...]
        err = xn * gv - t_ref[...]
        lpart = (0.5 / C) * jnp.sum(jnp.sum(err * err, axis=1, keepdims=True), axis=0, keepdims=True)
        dy = err * (1.0 / C)
        dxn = dy * gv
        dx_ref[...] = r * (dxn - xn * jnp.mean(dxn * xn, axis=-1, keepdims=True))
        gpart = jnp.sum(dy * xn, axis=0, keepdims=True)

        @pl.when(pl.program_id(0) == 0)
        def _():
            loss_ref[...] = lpart
            dg_ref[...] = gpart

        @pl.when(pl.program_id(0) > 0)
        def _():
            loss_ref[...] += lpart
            dg_ref[...] += gpart

    row = pl.BlockSpec((tm, C), lambda i: (i, 0))
    vec = pl.BlockSpec((1, C), lambda i: (0, 0))
    loss, dx, dg = pl.pallas_call(
        body,
        name=name,
        grid=(R // tm,),
        in_specs=[row, vec, row],
        out_specs=[pl.BlockSpec((1, 1), lambda i: (0, 0)), row, vec],
        out_shape=[jax.ShapeDtypeStruct((1, 1), F32), jax.ShapeDtypeStruct((R, C), F32), jax.ShapeDtypeStruct((1, C), F32)],
        compiler_params=_params(("arbitrary",)),
    )(x, g.reshape(1, C), target)
    return loss, dx, dg.reshape(C)


def _pool_select(lane, vals):
    out = vals[3]
    for gi in (2, 1, 0):
        out = jnp.where(lane < 64 * (gi + 1), vals[gi], out)
    return out


def _pool_diff(a):
    row, lane = _rows(a.shape), _lanes(a.shape)

    def down(v, k):
        return jnp.where(row >= k, pltpu.roll(v, k, 0), 0.0)

    s2 = a + down(a, 1)
    s4 = s2 + down(s2, 2)
    s8 = s4 + down(s4, 4)
    s16 = s8 + down(s8, 8)
    wsum = _pool_select(lane, (s2, s4, s8, s16))
    win = _pool_select(lane, (2, 4, 8, 16))
    cnt = jnp.minimum(row + 1, win).astype(F32)
    return wsum / cnt - a, cnt


def _pool_diff_t(dd, cnt):
    S = dd.shape[0]
    row, lane = _rows(dd.shape), _lanes(dd.shape)

    def up(v, k):
        return jnp.where(row < S - k, pltpu.roll(v, S - k, 0), 0.0)

    e = dd / cnt
    s2 = e + up(e, 1)
    s4 = s2 + up(s2, 2)
    s8 = s4 + up(s4, 4)
    s16 = s8 + up(s8, 8)
    return _pool_select(lane, (s2, s4, s8, s16)) - dd


def _pool_fwd(rest, wbd, scale, *, name):
    S = rest.shape[0]

    def body(a_ref, w_ref, s_ref, o_ref):
        d, _ = _pool_diff(a_ref[...])
        yp = _dot(d.astype(BF16), w_ref[...], 1, 0)
        o_ref[...] = (yp * s_ref[...]).astype(BF16)

    return pl.pallas_call(
        body,
        name=name,
        grid=(1,),
        in_specs=[
            pl.BlockSpec((S, POOL_W), lambda i: (0, OFF_A // POOL_W)),
            pl.BlockSpec((POOL_W, POOL_W), lambda i: (0, 0)),
            pl.BlockSpec((1, POOL_W), lambda i: (0, 0)),
        ],
        out_specs=pl.BlockSpec((S, POOL_W), lambda i: (0, 0)),
        out_shape=jax.ShapeDtypeStruct((S, POOL_W), BF16),
        compiler_params=_params(("arbitrary",)),
    )(rest, wbd, scale.reshape(1, POOL_W))


def _pool_bwd(rest, wbd, wbd_t, scale, dpa, *, name):
    S = rest.shape[0]

    def body(a_ref, w_ref, wt_ref, s_ref, dpa_ref, da_ref, dw_ref, ds_ref):
        d, cnt = _pool_diff(a_ref[...])
        db = d.astype(BF16)
        yp = _dot(db, w_ref[...], 1, 0)
        dpa_v = dpa_ref[...]
        ds_ref[...] = jnp.sum(dpa_v * yp, axis=0, keepdims=True)
        dyp = (dpa_v * s_ref[...]).astype(BF16)
        dw_ref[...] = _dot(db, dyp, 0, 0)
        dd = _dot(dyp, wt_ref[...], 1, 0)
        da_ref[...] = _pool_diff_t(dd, cnt).astype(BF16)

    full = pl.BlockSpec((S, POOL_W), lambda i: (0, 0))
    sq = pl.BlockSpec((POOL_W, POOL_W), lambda i: (0, 0))
    vec = pl.BlockSpec((1, POOL_W), lambda i: (0, 0))
    return pl.pallas_call(
        body,
        name=name,
        grid=(1,),
        in_specs=[pl.BlockSpec((S, POOL_W), lambda i: (0, OFF_A // POOL_W)), sq, sq, vec, full],
        out_specs=[full, sq, vec],
        out_shape=[
            jax.ShapeDtypeStruct((S, POOL_W), BF16),
            jax.ShapeDtypeStruct((POOL_W, POOL_W), F32),
            jax.ShapeDtypeStruct((1, POOL_W), F32),
        ],
        compiler_params=_params(("arbitrary",)),
    )(rest, wbd, wbd_t, scale.reshape(1, POOL_W), dpa)


def _log_sigmoid(z):
    return jnp.minimum(z, 0.0) - jnp.log(1.0 + jnp.exp(-jnp.abs(z)))


_F_SPEC_COL = OFF_F // F_LANES


def _fox_prep(rest, bpad, *, name):
    S = rest.shape[0]

    def body(f_ref, b_ref, o_ref, ot_ref):
        acc = _log_sigmoid(f_ref[...] + b_ref[...])
        row = _rows(acc.shape)
        k = 1
        while k < S:
            acc = acc + jnp.where(row >= k, pltpu.roll(acc, k, 0), 0.0)
            k *= 2
        o_ref[...] = acc
        ot_ref[...] = acc.T

    return pl.pallas_call(
        body,
        name=name,
        grid=(1,),
        in_specs=[pl.BlockSpec((S, F_LANES), lambda i: (0, _F_SPEC_COL)), pl.BlockSpec((1, F_LANES), lambda i: (0, 0))],
        out_specs=[pl.BlockSpec((S, F_LANES), lambda i: (0, 0)), pl.BlockSpec((F_LANES, S), lambda i: (0, 0))],
        out_shape=[jax.ShapeDtypeStruct((S, F_LANES), F32), jax.ShapeDtypeStruct((F_LANES, S), F32)],
        compiler_params=_params(("arbitrary",)),
    )(rest, bpad)


def _fox_post(rest, bpad, dcum, *, name):
    S = rest.shape[0]

    def body(f_ref, b_ref, d_ref, df_ref, db_ref):
        acc = d_ref[...]
        row = _rows(acc.shape)
        k = 1
        while k < S:
            acc = acc + jnp.where(row < S - k, pltpu.roll(acc, S - k, 0), 0.0)
            k *= 2
        df = acc * (1.0 - _sigmoid(f_ref[...] + b_ref[...]))
        df_ref[...] = df.astype(BF16)
        db_ref[...] = jnp.sum(df, axis=0, keepdims=True)

    full = pl.BlockSpec((S, F_LANES), lambda i: (0, 0))
    vec = pl.BlockSpec((1, F_LANES), lambda i: (0, 0))
    return pl.pallas_call(
        body,
        name=name,
        grid=(1,),
        in_specs=[pl.BlockSpec((S, F_LANES), lambda i: (0, _F_SPEC_COL)), vec, full],
        out_specs=[full, vec],
        out_shape=[jax.ShapeDtypeStruct((S, F_LANES), BF16), jax.ShapeDtypeStruct((1, F_LANES), F32)],
        compiler_params=_params(("arbitrary",)),
    )(rest, bpad, dcum)


_FOX_SCALE = FOX_DH ** -0.5
_PAIRS = FOX_H // 2


def _scaled(v):
    return (v.astype(F32) * _FOX_SCALE).astype(BF16)


def _head_lane(cum, h):
    return jnp.sum(jnp.where(_lanes(cum.shape) == h, cum, 0.0), axis=-1, keepdims=True)


def _diag_mask(s):
    return jnp.where(_rows(s.shape) >= _lanes(s.shape), s, NEG)


def _fox_fwd(qkv, cum, fk3, *, name):
    S = qkv.shape[0]
    nk, t = fk3.shape[1:]

    def body(q_ref, k_ref, v_ref, cum_ref, fk_ref, o_ref, lse_ref, m_sc, l_sc, acc_sc):
        hp, i = pl.program_id(0), pl.program_id(1)
        lane = _lanes((t, 128))
        lo = lane < FOX_DH
        qs = _scaled(q_ref[...])
        zero = jnp.zeros_like(qs)
        qm = (jnp.where(lo, qs, zero), jnp.where(lo, zero, qs))
        cumv = cum_ref[...]
        fq = [_head_lane(cumv, 2 * hp + e) for e in range(2)]
        m_sc[...] = jnp.full(m_sc.shape, NEG, F32)
        l_sc[...] = jnp.zeros(l_sc.shape, F32)
        acc_sc[...] = jnp.zeros(acc_sc.shape, F32)

        def tile(j, masked):
            k0 = pl.multiple_of(j * t, t)
            kb = k_ref[pl.ds(k0, t), :]
            vb = v_ref[pl.ds(k0, t), :]
            alphas, pvs = [], []
            for e in range(2):
                s = _dot(qm[e], kb, 1, 1) + fq[e] - fk_ref[2 * hp + e, pl.ds(j, 1), :]
                if masked:
                    s = _diag_mask(s)
                m_old = m_sc[e]
                m_new = jnp.maximum(m_old, jnp.max(s, axis=-1, keepdims=True))
                p = jnp.exp(s - m_new)
                alpha = jnp.exp(m_old - m_new)
                l_sc[e] = alpha * l_sc[e] + jnp.sum(p, axis=-1, keepdims=True)
                m_sc[e] = m_new
                alphas.append(alpha)
                pvs.append(_dot(p.astype(BF16), vb, 1, 0))
            acc_sc[...] = jnp.where(lo, alphas[0], alphas[1]) * acc_sc[...] + jnp.where(lo, pvs[0], pvs[1])

        def step(j, carry):
            tile(j, False)
            return carry

        lax.fori_loop(0, i, step, 0)
        tile(i, True)
        o_ref[...] = acc_sc[...] / jnp.where(lo, l_sc[0], l_sc[1])
        lse = [m_sc[e] + jnp.log(l_sc[e]) for e in range(2)]
        lse_ref[...] = jnp.where(lane == 0, lse[0], jnp.where(lane == 1, lse[1], 0.0))

    return pl.pallas_call(
        body,
        name=name,
        grid=(_PAIRS, S // t),
        in_specs=[
            pl.BlockSpec((t, 128), lambda hp, i: (i, hp)),
            pl.BlockSpec((S, 128), lambda hp, i: (0, _PAIRS + hp)),
            pl.BlockSpec((S, 128), lambda hp, i: (0, 2 * _PAIRS + hp)),
            pl.BlockSpec((t, F_LANES), lambda hp, i: (i, 0)),
            pl.BlockSpec((FOX_H, nk, t), lambda hp, i: (0, 0, 0)),
        ],
        out_specs=[pl.BlockSpec((t, 128), lambda hp, i: (i, hp)), pl.BlockSpec((None, t, 128), lambda hp, i: (hp, i, 0))],
        out_shape=[jax.ShapeDtypeStruct((S, FOX_W), F32), jax.ShapeDtypeStruct((_PAIRS, S, 128), F32)],
        scratch_shapes=[pltpu.VMEM((2, t, 1), F32), pltpu.VMEM((2, t, 1), F32), pltpu.VMEM((t, 128), F32)],
        compiler_params=_params(("parallel", "arbitrary")),
    )(qkv, qkv, qkv, cum, fk3)


def _fox_bwd(qkv, cum, fk3, o, do, lse, *, name):
    S = qkv.shape[0]
    nk, t = fk3.shape[1:]

    def body(q_ref, k_ref, v_ref, cum_ref, fk_ref, o_ref, do_ref, lse_ref, dq_ref, dk_ref, dv_ref, dfq_ref, dfk_ref,
             qm_sc, km_sc, dom_sc, delta_sc, fq_sc, dfq_sc, dq_sc):
        hp = pl.program_id(0)
        lane = _lanes((t, 128))
        lo = lane < FOX_DH

        def prep(i, carry):
            r = pl.ds(pl.multiple_of(i * t, t), t)
            qs, ks, dob = _scaled(q_ref[r, :]), _scaled(k_ref[r, :]), do_ref[r, :]
            prod = dob.astype(F32) * o_ref[r, :]
            cumv = cum_ref[r, :]
            zero = jnp.zeros_like(qs)
            for e in range(2):
                mine = lo if e == 0 else jnp.logical_not(lo)
                qm_sc[e, r, :] = jnp.where(mine, qs, zero)
                km_sc[e, r, :] = jnp.where(mine, ks, zero)
                dom_sc[e, r, :] = jnp.where(mine, dob, zero)
                delta_sc[e, r, :] = jnp.sum(jnp.where(mine, prod, 0.0), axis=-1, keepdims=True)
                fq_sc[e, r, :] = _head_lane(cumv, 2 * hp + e)
                dfq_sc[e, r, :] = jnp.zeros((t, 1), F32)
            dq_sc[r, :] = jnp.zeros((t, 128), F32)
            return carry

        lax.fori_loop(0, nk, prep, 0)

        def kv_tile(j, carry):
            kr = pl.ds(pl.multiple_of(j * t, t), t)
            kb, vb = k_ref[kr, :], v_ref[kr, :]
            fks = [fk_ref[2 * hp + e, pl.ds(j, 1), :] for e in range(2)]

            def q_tile(i, acc, masked):
                dk, dv, dfk0, dfk1 = acc
                dfk = [dfk0, dfk1]
                qr = pl.ds(pl.multiple_of(i * t, t), t)
                dq_t = jnp.zeros((t, 128), F32)
                for e in range(2):
                    qe, doe = qm_sc[e, qr, :], dom_sc[e, qr, :]
                    s = _dot(qe, kb, 1, 1) + fq_sc[e, qr, :] - fks[e]
                    if masked:
                        s = _diag_mask(s)
                    p = jnp.exp(s - lse_ref[qr, e:e + 1])
                    dv = dv + _dot(p.astype(BF16), doe, 0, 0)
                    dp = _dot(doe, vb, 1, 1)
                    ds = p * (dp - delta_sc[e, qr, :])
                    dsb = ds.astype(BF16)
                    dk = dk + _dot(dsb, qe, 0, 0)
                    dq_t = dq_t + _dot(dsb, km_sc[e, kr, :], 1, 0)
                    dfq_sc[e, qr, :] += jnp.sum(ds, axis=-1, keepdims=True)
                    dfk[e] = dfk[e] - jnp.sum(ds, axis=0, keepdims=True)
                dq_sc[qr, :] += dq_t
                return dk, dv, dfk[0], dfk[1]

            init = (jnp.zeros((t, 128), F32), jnp.zeros((t, 128), F32), jnp.zeros((1, t), F32), jnp.zeros((1, t), F32))
            acc = q_tile(j, init, True)
            dk, dv, dfk0, dfk1 = lax.fori_loop(j + 1, nk, functools.partial(q_tile, masked=False), acc)
            dk_ref[kr, :] = dk.astype(BF16)
            dv_ref[kr, :] = dv.astype(BF16)
            dfk_ref[0, pl.ds(j, 1), :] = dfk0
            dfk_ref[1, pl.ds(j, 1), :] = dfk1
            return carry

        lax.fori_loop(0, nk, kv_tile, 0)
        dq_ref[...] = dq_sc[...].astype(BF16)
        lane_s = _lanes((S, 128))
        dfq_ref[...] = jnp.where(lane_s == 0, dfq_sc[0], jnp.where(lane_s == 1, dfq_sc[1], 0.0))

    col = lambda c0: pl.BlockSpec((S, 128), lambda hp: (0, c0 + hp))
    pair = pl.BlockSpec((S, 128), lambda hp: (0, hp))
    lanes3 = pl.BlockSpec((None, S, 128), lambda hp: (hp, 0, 0))
    big = jax.ShapeDtypeStruct((S, FOX_W), BF16)
    masked_bf16 = pltpu.VMEM((2, S, 128), BF16)
    column = pltpu.VMEM((2, S, 1), F32)
    return pl.pallas_call(
        body,
        name=name,
        grid=(_PAIRS,),
        in_specs=[
            col(0), col(_PAIRS), col(2 * _PAIRS),
            pl.BlockSpec((S, F_LANES), lambda hp: (0, 0)),
            pl.BlockSpec((FOX_H, nk, t), lambda hp: (0, 0, 0)),
            pair, pair, lanes3,
        ],
        out_specs=[pair, pair, pair, lanes3, pl.BlockSpec((None, 2, nk, t), lambda hp: (hp, 0, 0, 0))],
        out_shape=[big, big, big, jax.ShapeDtypeStruct((_PAIRS, S, 128), F32), jax.ShapeDtypeStruct((_PAIRS, 2, nk, t), F32)],
        scratch_shapes=[masked_bf16, masked_bf16, masked_bf16, column, column, column, pltpu.VMEM((S, 128), F32)],
        compiler_params=_params(("parallel",)),
    )(qkv, qkv, qkv, cum, fk3, o, do, lse)


def _group_mask(lane, gi):
    return (lane >= 64 * gi) & (lane < 64 * (gi + 1))


_U_COL = OFF_C // SGU_W


def _sgu_fwd(rest, gn, wm, bias, *, name):
    S = rest.shape[0]
    ts = _tile(S, 512)
    nc = ts // SGU_CHUNK

    def body(u_ref, v_ref, g_ref, w_ref, b_ref, o_ref):
        zv = _gelu(v_ref[...])
        vn = zv * lax.rsqrt(jnp.mean(zv * zv, axis=-1, keepdims=True) + EPS) * g_ref[...]
        lane = _lanes((SGU_CHUNK, SGU_W))
        for c in range(nc):
            rows = slice(c * SGU_CHUNK, (c + 1) * SGU_CHUNK)
            vcb = vn[rows].astype(BF16)
            mixed = b_ref[...]
            for gi in range(4):
                mixed = mixed + jnp.where(_group_mask(lane, gi), _dot(w_ref[gi], vcb, 1, 0), 0.0)
            o_ref[rows, :] = (_gelu(u_ref[rows, :]) * mixed).astype(BF16)

    return pl.pallas_call(
        body,
        name=name,
        grid=(S // ts,),
        in_specs=[
            pl.BlockSpec((ts, SGU_W), lambda i: (i, _U_COL)),
            pl.BlockSpec((ts, SGU_W), lambda i: (i, _U_COL + 1)),
            pl.BlockSpec((1, SGU_W), lambda i: (0, 0)),
            pl.BlockSpec((4, SGU_CHUNK, SGU_CHUNK), lambda i: (0, 0, 0)),
            pl.BlockSpec((SGU_CHUNK, SGU_W), lambda i: (0, 0)),
        ],
        out_specs=pl.BlockSpec((ts, SGU_W), lambda i: (i, 0)),
        out_shape=jax.ShapeDtypeStruct((S, SGU_W), BF16),
        compiler_params=_params(("parallel",)),
    )(rest, rest, gn.reshape(1, SGU_W), wm, bias)


def _sgu_bwd(rest, gn, wm, wm_t, bias, dsg, *, name):
    S = rest.shape[0]
    ts = _tile(S, 512)
    nc = ts // SGU_CHUNK

    def body(u_ref, v_ref, g_ref, w_ref, wt_ref, b_ref, dsg_ref, dc_ref, dw_ref, db_ref, dg_ref):
        first = pl.program_id(0) == 0

        @pl.when(first)
        def _():
            dw_ref[...] = jnp.zeros_like(dw_ref)
            db_ref[...] = jnp.zeros_like(db_ref)
            dg_ref[...] = jnp.zeros_like(dg_ref)

        gv = g_ref[...]
        lane = _lanes((SGU_CHUNK, SGU_W))
        for c in range(nc):
            rows = slice(c * SGU_CHUNK, (c + 1) * SGU_CHUNK)
            vpre = v_ref[rows, :]
            upre = u_ref[rows, :]
            zv = _gelu(vpre)
            r = lax.rsqrt(jnp.mean(zv * zv, axis=-1, keepdims=True) + EPS)
            zn = zv * r
            vcb = (zn * gv).astype(BF16)
            mixed = b_ref[...]
            for gi in range(4):
                mixed = mixed + jnp.where(_group_mask(lane, gi), _dot(w_ref[gi], vcb, 1, 0), 0.0)
            zu = _gelu(upre)
            dsg_v = dsg_ref[rows, :]
            dc_ref[rows, :SGU_W] = (dsg_v * mixed * _gelu_grad(upre)).astype(BF16)
            dmixed = dsg_v * zu
            db_ref[...] += dmixed
            dvn = jnp.zeros((SGU_CHUNK, SGU_W), F32)
            for gi in range(4):
                dmg = jnp.where(_group_mask(lane, gi), dmixed, 0.0).astype(BF16)
                dw_ref[gi] += _dot(dmg, vcb, 1, 1)
                dvn = dvn + _dot(wt_ref[gi], dmg, 1, 0)
            dg_ref[...] += jnp.sum(dvn * zn, axis=0, keepdims=True)
            dzn = dvn * gv
            dzv = r * (dzn - zn * jnp.mean(dzn * zn, axis=-1, keepdims=True))
            dc_ref[rows, SGU_W:] = (dzv * _gelu_grad(vpre)).astype(BF16)

    blk = pl.BlockSpec((ts, SGU_W), lambda i: (i, 0))
    vec = pl.BlockSpec((1, SGU_W), lambda i: (0, 0))
    w3 = pl.BlockSpec((4, SGU_CHUNK, SGU_CHUNK), lambda i: (0, 0, 0))
    bsp = pl.BlockSpec((SGU_CHUNK, SGU_W), lambda i: (0, 0))
    return pl.pallas_call(
        body,
        name=name,
        grid=(S // ts,),
        in_specs=[
            pl.BlockSpec((ts, SGU_W), lambda i: (i, _U_COL)),
            pl.BlockSpec((ts, SGU_W), lambda i: (i, _U_COL + 1)),
            vec, w3, w3, bsp, blk,
        ],
        out_specs=[pl.BlockSpec((ts, 2 * SGU_W), lambda i: (i, 0)), w3, bsp, vec],
        out_shape=[
            jax.ShapeDtypeStruct((S, 2 * SGU_W), BF16),
            jax.ShapeDtypeStruct((4, SGU_CHUNK, SGU_CHUNK), F32),
            jax.ShapeDtypeStruct((SGU_CHUNK, SGU_W), F32),
            jax.ShapeDtypeStruct((1, SGU_W), F32),
        ],
        compiler_params=_params(("arbitrary",)),
    )(rest, rest, gn.reshape(1, SGU_W), wm, wm_t, bias, dsg)


_GT = 512
_G0 = OFF_G // _GT


def _gate_specs(tm, col_of):
    specs = [pl.BlockSpec((tm, _GT), functools.partial(lambda k, *ids: (col_of(*ids)[0], _G0 + 2 * k + col_of(*ids)[1]), k)) for k in range(3)]
    specs += [pl.BlockSpec((1, _GT), functools.partial(lambda k, *ids: (0, 2 * k + col_of(*ids)[1]), k)) for k in range(3)]
    return specs


def _merge_fwd(rest, bg, ya, yb, yc, *, name):
    S = rest.shape[0]
    tm = _tile(S, 512)

    def body(g1, g2, g3, b1, b2, b3, ya_ref, yb_ref, yc_ref, o_ref):
        acc = _sigmoid(g1[...] + b1[...]) * ya_ref[...]
        acc = acc + _sigmoid(g2[...] + b2[...]) * yb_ref[...]
        acc = acc + _sigmoid(g3[...] + b3[...]) * yc_ref[...]
        o_ref[...] = acc.astype(BF16)

    blk = pl.BlockSpec((tm, _GT), lambda i, j: (i, j))
    return pl.pallas_call(
        body,
        name=name,
        grid=(S // tm, D // _GT),
        in_specs=_gate_specs(tm, lambda i, j: (i, j)) + [blk, blk, blk],
        out_specs=blk,
        out_shape=jax.ShapeDtypeStruct((S, D), BF16),
        compiler_params=_params(("parallel", "parallel")),
    )(rest, rest, rest, bg, bg, bg, ya, yb, yc)


def _merge_bwd(rest, bg, ya, yb, yc, dm, *, name):
    S = rest.shape[0]
    tm = _tile(S, 512)

    def body(g1, g2, g3, b1, b2, b3, ya_ref, yb_ref, yc_ref, dm_ref, dya, dyb, dyc, dg1, dg2, dg3, db1, db2, db3):
        first = pl.program_id(1) == 0
        dmv = dm_ref[...]
        for g_ref, b_ref, y_ref, dy_ref, dg_ref, db_ref in (
            (g1, b1, ya_ref, dya, dg1, db1), (g2, b2, yb_ref, dyb, dg2, db2), (g3, b3, yc_ref, dyc, dg3, db3)):
            gate = _sigmoid(g_ref[...] + b_ref[...])
            dy_ref[...] = (dmv * gate).astype(BF16)
            dpre = dmv * y_ref[...] * gate * (1.0 - gate)
            dg_ref[...] = dpre.astype(BF16)
            part = jnp.sum(dpre, axis=0, keepdims=True)

            @pl.when(first)
            def _():
                db_ref[...] = part

            @pl.when(jnp.logical_not(first))
            def _():
                db_ref[...] += part

    blk = pl.BlockSpec((tm, _GT), lambda j, i: (i, j))
    vec = pl.BlockSpec((1, _GT), lambda j, i: (0, j))
    big = jax.ShapeDtypeStruct((S, D), BF16)
    small = jax.ShapeDtypeStruct((1, D), F32)
    return pl.pallas_call(
        body,
        name=name,
        grid=(D // _GT, S // tm),
        in_specs=_gate_specs(tm, lambda j, i: (i, j)) + [blk, blk, blk, blk],
        out_specs=[blk] * 6 + [vec] * 3,
        out_shape=[big] * 6 + [small] * 3,
        compiler_params=_params(("parallel", "arbitrary")),
    )(rest, rest, rest, bg, bg, bg, ya, yb, yc, dm)


_X_SCALE = XDH ** -0.5


def _xattn_fwd(xq, kv, *, name):
    S = xq.shape[0]
    M = kv.shape[0]
    tq = _tile(S, 512)

    def body(q_ref, k_ref, v_ref, o_ref):
        s = _dot(q_ref[...], k_ref[...], 1, 1) * _X_SCALE
        e = jnp.exp(s - jnp.max(s, axis=-1, keepdims=True))
        p = e / jnp.sum(e, axis=-1, keepdims=True)
        o_ref[...] = _dot(p.astype(BF16), v_ref[...], 1, 0).astype(BF16)

    return pl.pallas_call(
        body,
        name=name,
        grid=(S // tq, XH),
        in_specs=[
            pl.BlockSpec((tq, XDH), lambda i, h: (i, h)),
            pl.BlockSpec((M, XDH), lambda i, h: (0, h)),
            pl.BlockSpec((M, XDH), lambda i, h: (0, XH + h)),
        ],
        out_specs=pl.BlockSpec((tq, XDH), lambda i, h: (i, h)),
        out_shape=jax.ShapeDtypeStruct((S, D), BF16),
        compiler_params=_params(("parallel", "parallel")),
    )(xq, kv, kv)


def _xattn_bwd(xq, kv, do, *, name):
    S = xq.shape[0]
    M = kv.shape[0]
    tq = _tile(S, 512)

    def body(q_ref, k_ref, v_ref, do_ref, dq_ref, dk_ref, dv_ref):
        qb = q_ref[...]
        kb = k_ref[...]
        dob = do_ref[...]
        s = _dot(qb, kb, 1, 1) * _X_SCALE
        e = jnp.exp(s - jnp.max(s, axis=-1, keepdims=True))
        p = e / jnp.sum(e, axis=-1, keepdims=True)
        dp = _dot(dob, v_ref[...], 1, 1)
        ds = (p * (dp - jnp.sum(p * dp, axis=-1, keepdims=True)) * _X_SCALE).astype(BF16)
        dq_ref[...] = _dot(ds, kb, 1, 0).astype(BF16)
        dk_part = _dot(ds, qb, 0, 0)
        dv_part = _dot(p.astype(BF16), dob, 0, 0)

        @pl.when(pl.program_id(1) == 0)
        def _():
            dk_ref[...] = dk_part
            dv_ref[...] = dv_part

        @pl.when(pl.program_id(1) > 0)
        def _():
            dk_ref[...] += dk_part
            dv_ref[...] += dv_part

    qspec = pl.BlockSpec((tq, XDH), lambda h, i: (i, h))
    kspec = pl.BlockSpec((M, XDH), lambda h, i: (0, h))
    dxq, dxk, dxv = pl.pallas_call(
        body,
        name=name,
        grid=(XH, S // tq),
        in_specs=[qspec, kspec, pl.BlockSpec((M, XDH), lambda h, i: (0, XH + h)), qspec],
        out_specs=[qspec, kspec, kspec],
        out_shape=[jax.ShapeDtypeStruct((S, D), BF16), jax.ShapeDtypeStruct((M, D), F32), jax.ShapeDtypeStruct((M, D), F32)],
        compiler_params=_params(("parallel", "arbitrary")),
    )(xq, kv, kv, do)
    return dxq, jnp.concatenate([dxk, dxv], axis=1)


def _adam_math(w, g, m, v):
    m = ADAM_B1 * m + (1.0 - ADAM_B1) * g
    v = ADAM_B2 * v + (1.0 - ADAM_B2) * (g * g)
    m_hat = m / (1.0 - ADAM_B1 ** ADAM_STEP)
    v_hat = v / (1.0 - ADAM_B2 ** ADAM_STEP)
    delta = -ADAM_LR * (m_hat / (jnp.sqrt(v_hat) + ADAM_EPS) + ADAM_WD * w)
    return delta, m, v


def _adamw_sharded(parts, w, m, v, *, name):
    _, R, C = w.shape
    Cp = parts[0].shape[2]
    tm = _tile(R, 256)
    nr = R // tm

    def body(p0_ref, p1_ref, w_ref, m_ref, v_ref, g_ref, d_ref, mo_ref, vo_ref):
        def update(p_ref):
            g = p_ref[0][:, :C].astype(F32)
            for dev in range(1, N_DEV):
                g = g + p_ref[dev][:, :C].astype(F32)
            delta, mn, vn = _adam_math(w_ref[...], g, m_ref[...], v_ref[...])
            g_ref[...] = g
            d_ref[...] = delta
            mo_ref[...] = mn
            vo_ref[...] = vn

        @pl.when(pl.program_id(0) == 0)
        def _():
            update(p0_ref)

        @pl.when(pl.program_id(0) == 1)
        def _():
            update(p1_ref)

    p0 = pl.BlockSpec((N_DEV, tm, Cp), lambda l, i: (0, i * (1 - l) + (nr - 1) * l, 0))
    p1 = pl.BlockSpec((N_DEV, tm, Cp), lambda l, i: (0, i * l, 0))
    blk = pl.BlockSpec((None, tm, C), lambda l, i: (l, i, 0))
    sds = jax.ShapeDtypeStruct(w.shape, F32)
    return pl.pallas_call(
        body,
        name=name,
        grid=(DEPTH, nr),
        in_specs=[p0, p1, blk, blk, blk],
        out_specs=[blk] * 4,
        out_shape=[sds] * 4,
        compiler_params=_params(("arbitrary", "arbitrary")),
    )(parts[0], parts[1], w, m, v)


def _adamw_small(g, w, m, v, *, name):
    n = len(g)

    def body(*refs):
        g_refs, w_refs, m_refs, v_refs = (refs[k * n:(k + 1) * n] for k in range(4))
        d_out, m_out, v_out = (refs[(4 + k) * n:(5 + k) * n] for k in range(3))
        for t in range(n):
            delta, mn, vn = _adam_math(w_refs[t][...], g_refs[t][...], m_refs[t][...], v_refs[t][...])
            d_out[t][...] = delta
            m_out[t][...] = mn
            v_out[t][...] = vn

    vm = pl.BlockSpec(memory_space=pltpu.VMEM)
    shapes = [jax.ShapeDtypeStruct(a.shape, F32) for a in w]
    outs = pl.pallas_call(
        body,
        name=name,
        in_specs=[vm] * (4 * n),
        out_specs=[vm] * (3 * n),
        out_shape=shapes * 3,
        compiler_params=pltpu.CompilerParams(vmem_limit_bytes=VMEM_LIMIT),
    )(*g, *w, *m, *v)
    return outs[:n], outs[n:2 * n], outs[2 * n:]


def _position():
    return lax.axis_index("x"), lax.axis_index("y"), lax.axis_index("c")


def _dev_index(px, py, pc):
    return 4 * px + 2 * py + pc


_ANY = pl.BlockSpec(memory_space=pl.ANY)


def _all_gather(shards, *, name):
    n = len(shards)
    out_shape = [jax.ShapeDtypeStruct((N_DEV, *s.shape), s.dtype) for s in shards]
    n_pieces = len(_pieces(out_shape))

    def body(*refs):
        ins, outs = refs[:n], refs[n:2 * n]
        send_sems, recv_sems, local_sems = refs[2 * n:]
        x, y, c = _position()
        me, sibling = (x, y, c), (x, y, 1 - c)
        chips = [(1 - x, y), (x, 1 - y), (1 - x, 1 - y)]
        pieces = _pieces(outs)

        def copy(i, k, block, to, from_input=False):
            t, rows = pieces[i]
            dst = _cut(outs[t].at[_dev_index(*block)], rows)
            return pltpu.make_async_remote_copy(
                src_ref=_cut(ins[t], rows) if from_input else dst, dst_ref=dst, send_sem=send_sems.at[i, k],
                recv_sem=recv_sems.at[i, k], device_id=to, device_id_type=MESH)

        mine = [pltpu.make_async_copy(_cut(ins[t], rows), _cut(outs[t].at[_dev_index(*me)], rows), local_sems.at[i])
                for i, (t, rows) in enumerate(pieces)]
        for cp in mine:
            cp.start()
        started = []
        for j, chip in enumerate(chips):
            for i in range(n_pieces):
                started.append(copy(i, 1 + j, me, (*chip, c), from_input=True))
                started[-1].start()
        for i in range(n_pieces):
            started.append(copy(i, 0, me, sibling, from_input=True))
            started[-1].start()
        for j, chip in enumerate(chips):
            for i in range(n_pieces):
                copy(i, 1 + j, (*chip, c), me).wait_recv()
                started.append(copy(i, 4 + j, (*chip, c), sibling))
                started[-1].start()
        for i in range(n_pieces):
            copy(i, 0, sibling, me).wait_recv()
        for j, chip in enumerate(chips):
            for i in range(n_pieces):
                copy(i, 4 + j, (*chip, 1 - c), me).wait_recv()
        for cp in started:
            cp.wait_send()
        for cp in mine:
            cp.wait()

    return pl.pallas_call(
        body,
        name=name,
        in_specs=[_ANY] * n,
        out_specs=[_ANY] * n,
        out_shape=out_shape,
        scratch_shapes=[pltpu.SemaphoreType.DMA((n_pieces, 7)), pltpu.SemaphoreType.DMA((n_pieces, 7)),
                        pltpu.SemaphoreType.DMA((n_pieces,))],
        compiler_params=pltpu.CompilerParams(has_side_effects=True),
    )(*shards)


def _peers(x, y, c):
    out = []
    for mask in range(1, N_DEV):
        fx, fy, fc = (mask >> 2) & 1, (mask >> 1) & 1, mask & 1
        out.append((1 - x if fx else x, 1 - y if fy else y, 1 - c if fc else c))
    return out


_HBM = pl.BlockSpec(memory_space=pltpu.HBM)
_SEM = pl.BlockSpec(memory_space=pltpu.SEMAPHORE)


def _own_block_placed(block, like):
    x, y, c = _position()
    return lax.dynamic_update_index_in_dim(lax.empty(like.shape, like.dtype), block, _dev_index(x, y, c), 0)


_COPY_BYTES = 256 << 10
_MAX_PIECES = 8


def _pieces(blocks):
    out = []
    for t, b in enumerate(blocks):
        R, C = b.shape[-2:]
        n = max(1, min(_MAX_PIECES, R * C * jnp.dtype(b.dtype).itemsize // _COPY_BYTES))
        while R % (16 * n):
            n //= 2
        out += [(t, pl.ds(j * (R // n), R // n) if n > 1 else None) for j in range(n)]
    return out


def _cut(block, rows):
    return block if rows is None else block.at[rows]


def _copies(per_piece):
    def mark(fn):
        fn.per_piece = per_piece
        return fn
    return mark


@_copies(N_DEV - 1)
def _plan_exchange(srcs, lands, send_sems, recv_sems, arrivals):
    x, y, c = _position()
    me = _dev_index(x, y, c)
    out = []
    for k, peer in enumerate(_peers(x, y, c)):
        p = _dev_index(*peer)
        for i, (t, rows) in enumerate(_pieces(lands)):
            sems = dict(send_sem=send_sems.at[7 * i + k], recv_sem=recv_sems.at[7 * i + k], device_id=peer, device_id_type=MESH)
            src, dst = (lands[t].at[p], lands[t].at[p]) if arrivals else (srcs[t].at[p], lands[t].at[me])
            out.append(pltpu.make_async_remote_copy(src_ref=_cut(src, rows), dst_ref=_cut(dst, rows), **sems))
    return out


@_copies(4)
def _plan_gather_out(srcs, lands, send_sems, recv_sems, arrivals):
    x, y, c = _position()
    me = _dev_index(x, y, c)
    out = []
    for k, peer in enumerate([(x, y, 1 - c), (1 - x, y, c), (x, 1 - y, c), (1 - x, 1 - y, c)]):
        p = _dev_index(*peer)
        for i, (t, rows) in enumerate(_pieces(lands)):
            sems = dict(send_sem=send_sems.at[4 * i + k], recv_sem=recv_sems.at[4 * i + k], device_id=peer, device_id_type=MESH)
            src, dst = (lands[t].at[p], lands[t].at[p]) if arrivals else (srcs[t], lands[t].at[me])
            out.append(pltpu.make_async_remote_copy(src_ref=_cut(src, rows), dst_ref=_cut(dst, rows), **sems))
    return out


@_copies(3)
def _plan_gather_pass(srcs, lands, send_sems, recv_sems, arrivals):
    x, y, c = _position()
    sibling = (x, y, 1 - c)
    out = []
    for k, chip in enumerate([(1 - x, y), (x, 1 - y), (1 - x, 1 - y)]):
        p = _dev_index(*chip, 1 - c) if arrivals else _dev_index(*chip, c)
        for i, (t, rows) in enumerate(_pieces(lands)):
            sems = dict(send_sem=send_sems.at[3 * i + k], recv_sem=recv_sems.at[3 * i + k], device_id=sibling, device_id_type=MESH)
            block = _cut(lands[t].at[p], rows)
            out.append(pltpu.make_async_remote_copy(src_ref=block, dst_ref=block, **sems))
    return out


def _split_start(plan, srcs, lands, *, after=None, name):
    n_src, n = len(srcs), len(srcs) + len(lands)
    n_sem = plan.per_piece * len(_pieces(lands))
    order = [] if after is None else [after]

    def body(*refs):
        send_sems, recv_sems = refs[n + len(order):n + len(order) + 2]
        token = refs[-1]
        for cp in plan(refs[:n_src], refs[n_src:n], send_sems, recv_sems, arrivals=False):
            cp.start()
        token[...] = jnp.zeros_like(token)

    hbm = lambda a: pltpu.HBM(a.shape, a.dtype)
    outs = pl.pallas_call(
        body,
        name=name,
        in_specs=[_HBM] * n + [_ANY] * len(order),
        out_specs=[_SEM, _SEM] + [_HBM] * n + [pl.BlockSpec(memory_space=pltpu.VMEM)],
        out_shape=[pltpu.SemaphoreType.DMA((n_sem,)), pltpu.SemaphoreType.DMA((n_sem,))] + [hbm(a) for a in (*srcs, *lands)]
        + [jax.ShapeDtypeStruct(_TOKEN, F32)],
        input_output_aliases={i: 2 + i for i in range(n)},
        compiler_params=pltpu.CompilerParams(has_side_effects=pltpu.SideEffectType.DATAFLOW_SIDE_EFFECTING),
    )(*[pltpu.with_memory_space_constraint(a, pltpu.HBM) for a in (*srcs, *lands)], *order)
    return (outs[0], outs[1], outs[2:2 + n_src], outs[2 + n_src:2 + n]), outs[-1]


def _split_wait(plan, state, after, *, name):
    send_sems, recv_sems, srcs, lands = state
    n_src, n = len(srcs), len(srcs) + len(lands)

    def body(*refs):
        send_refs, recv_refs = refs[n:n + 2]
        for cp in plan(refs[:n_src], refs[n_src:n], send_refs, recv_refs, arrivals=False):
            cp.wait_send()
        for cp in plan(refs[:n_src], refs[n_src:n], send_refs, recv_refs, arrivals=True):
            cp.wait_recv()

    hbm = lambda a: pltpu.HBM(a.shape, a.dtype)
    outs = pl.pallas_call(
        body,
        name=name,
        in_specs=[_HBM] * n + [_SEM, _SEM, _ANY],
        out_specs=[_HBM] * n,
        out_shape=[hbm(a) for a in (*srcs, *lands)],
        input_output_aliases={i: i for i in range(n)},
        compiler_params=pltpu.CompilerParams(has_side_effects=pltpu.SideEffectType.DATAFLOW_SIDE_EFFECTING),
    )(*srcs, *lands, send_sems, recv_sems, after)
    return outs[n_src:]


def _all_reduce(g_local, after, *, name):
    R, C = g_local.shape
    nc = next(n for n in (4, 3, 2, 1) if R % (8 * n) == 0)
    chunks = [pl.ds(j * (R // nc), R // nc) for j in range(nc)]

    def body(g_ref, after_ref, o_ref, buf, send_sems, recv_sems):
        x, y, c = _position()
        me = _dev_index(x, y, c)
        peers = _peers(x, y, c)
        copies = []
        for k, peer in enumerate(peers):
            for j, rows in enumerate(chunks):
                copies.append(pltpu.make_async_remote_copy(
                    src_ref=g_ref.at[rows], dst_ref=buf.at[me, rows], send_sem=send_sems.at[nc * k + j],
                    recv_sem=recv_sems.at[nc * k + j], device_id=peer, device_id_type=MESH))
                copies[-1].start()
        buf[me] = g_ref[...]
        for k, peer in enumerate(peers):
            for j, rows in enumerate(chunks):
                dst = buf.at[_dev_index(*peer), rows]
                pltpu.make_async_remote_copy(
                    src_ref=dst, dst_ref=dst, send_sem=send_sems.at[nc * k + j], recv_sem=recv_sems.at[nc * k + j],
                    device_id=peer, device_id_type=MESH).wait_recv()
        for cp in copies:
            cp.wait_send()
        g = buf[0]
        for dev in range(1, N_DEV):
            g = g + buf[dev]
        o_ref[...] = g

    vm = pl.BlockSpec(memory_space=pltpu.VMEM)
    return pl.pallas_call(
        body,
        name=name,
        in_specs=[vm, _ANY],
        out_specs=vm,
        out_shape=jax.ShapeDtypeStruct((R, C), F32),
        scratch_shapes=[pltpu.VMEM((N_DEV, R, C), F32), pltpu.SemaphoreType.DMA((7 * nc,)), pltpu.SemaphoreType.DMA((7 * nc,))],
        compiler_params=pltpu.CompilerParams(has_side_effects=True, vmem_limit_bytes=VMEM_LIMIT),
    )(g_local, after)


def _block_diag(w):
    out = jnp.zeros((POOL_W, POOL_W), w.dtype)
    for gi in range(4):
        out = out.at[64 * gi:64 * (gi + 1), 64 * gi:64 * (gi + 1)].set(w[gi])
    return out


def _layer_consts(sp, l):
    causal = jnp.tril(jnp.ones((SGU_CHUNK, SGU_CHUNK), F32))
    wm = (sp["sgu_w"][l] * causal[None]).astype(BF16)
    wbd = _block_diag(sp["pool_w"][l]).astype(BF16)
    return dict(
        wbd=wbd, wbd_t=wbd.T, wm=wm, wm_t=wm.transpose(0, 2, 1),
        sgu_bias=jnp.repeat(sp["sgu_b"][l].T, 64, axis=1),
        bpad=jnp.pad(sp["b_forget"][l], (0, F_LANES - FOX_H)).reshape(1, F_LANES),
        bg=sp["b_gate"][l].reshape(1, 3 * D),
    )


def _relu2(acc):
    return acc, jnp.square(jnp.maximum(acc, 0.0))


def _relu2_grad(acc, z):
    return (acc * 2.0 * jnp.maximum(z, 0.0),)


def _layer_fwd(l, x, mem, source, sp):
    S = x.shape[0]
    t = _tile(S, 256)
    c = _layer_consts(sp, l)
    n = f"l{l}_"
    W, after = source(l, "begin", x)
    h = _rms_fwd(x, sp["norm_mix_g"][l], after=after, name=n + "norm_mix")
    qkv = _mm(h, W["qkv"], out_dtypes=(BF16,), name=n + "qkv")
    rest = _mm(h, W["rest"], name=n + "rest")
    pa = _pool_fwd(rest, c["wbd"], sp["pool_scale"][l], name=n + "pool")
    cum, cum_t = _fox_prep(rest, c["bpad"], name=n + "fox_prep")
    fk3 = cum_t[:FOX_H].reshape(FOX_H, S // t, t)
    o, lse = _fox_fwd(qkv, cum, fk3, name=n + "fox")
    more, _ = source(l, "attended", o)
    W.update(more)
    sg = _sgu_fwd(rest, sp["sgu_norm_g"][l], c["wm"], c["sgu_bias"], name=n + "sgu")
    more, after = source(l, "mixed", sg)
    W.update(more)
    ya = _mm(pa, W["ba"], after=after, name=n + "branch_a")
    yb = _mm(o, W["bb"], name=n + "branch_b")
    yc = _mm(sg, W["bc"], name=n + "branch_c")
    merged = _merge_fwd(rest, c["bg"], ya, yb, yc, name=n + "merge")
    x1 = _mm(merged, W["out"], extras=(x,), epilogue=_add, name=n + "out")
    hx = _rms_fwd(x1, sp["norm_xattn_g"][l], name=n + "norm_xattn")
    hm = _rms_fwd(mem, sp["norm_mem_g"][l], name=n + "norm_mem")
    xq = _mm(hx, W["xq"], out_dtypes=(BF16,), name=n + "xq")
    kv = _mm(hm, W["xkv"], out_dtypes=(BF16,), name=n + "xkv")
    o2 = _xattn_fwd(xq, kv, name=n + "xattn")
    x2 = _mm(o2, W["xo"], extras=(x1,), epilogue=_add, name=n + "xo")
    hf = _rms_fwd(x2, sp["norm_ffn_g"][l], name=n + "norm_ffn")
    z, act = _mm(hf, W["ff1"], epilogue=_relu2, out_dtypes=(F32, BF16), name=n + "ff1")
    _, after = source(l, "expanded", act)
    x3 = _mm(act, W["ff2"], extras=(x2,), epilogue=_add, after=after, name=n + "ff2")
    saved = dict(x=x, h=h, qkv=qkv, rest=rest, pa=pa, cum=cum, fk3=fk3, o=o, lse=lse, sg=sg, ya=ya, yb=yb, yc=yc,
                 merged=merged, x1=x1, hx=hx, hm=hm, xq=xq, kv=kv, o2=o2, x2=x2, hf=hf, z=z, act=act, c=c)
    return x3, saved, W


def _layer_bwd(l, dx3, sv, mem, W, sp, grads_done):
    S = dx3.shape[0]
    c = sv["c"]
    n = f"l{l}b_"
    bf = dict(out_dtypes=(BF16,))
    gw, gs = {}, {}
    gw["ff2"] = _mm(sv["act"], dx3, ta=True, name=n + "dw_ff2", **bf)
    dz = _mm(dx3, W["ff2"], tb=True, extras=(sv["z"],), epilogue=_relu2_grad, name=n + "dz", **bf)
    gw["ff1"] = _mm(sv["hf"], dz, ta=True, shard_out=True, name=n + "dw_ff1", **bf)
    dhf = _mm(dz, W["ff1"], tb=True, name=n + "dhf")
    dx2, gs["norm_ffn_g"] = _rms_bwd(sv["x2"], sp["norm_ffn_g"][l], dhf, dx3, name=n + "dnorm_ffn")
    gw["xo"] = _mm(sv["o2"], dx2, ta=True, name=n + "dw_xo", **bf)
    do2 = _mm(dx2, W["xo"], tb=True, name=n + "do2", **bf)
    dxq, dkv = _xattn_bwd(sv["xq"], sv["kv"], do2, name=n + "dxattn")
    gw["xq"] = _mm(sv["hx"], dxq, ta=True, name=n + "dw_xq", **bf)
    gw["xkv"] = _mm(sv["hm"], dkv, ta=True, shard_out=True, name=n + "dw_xkv", **bf)
    dhm = _mm(dkv, W["xkv"], tb=True, name=n + "dhm")
    _, gs["norm_mem_g"] = _rms_bwd(mem, sp["norm_mem_g"][l], dhm, jnp.zeros_like(mem), name=n + "dnorm_mem")
    dhx = _mm(dxq, W["xq"], tb=True, name=n + "dhx")
    dx1, gs["norm_xattn_g"] = _rms_bwd(sv["x1"], sp["norm_xattn_g"][l], dhx, dx2, name=n + "dnorm_xattn")
    after, gw = grads_done(l, gw), {}
    gw["out"] = _mm(sv["merged"], dx1, ta=True, name=n + "dw_out", **bf)
    dm = _mm(dx1, W["out"], tb=True, after=after, name=n + "dmerged")
    dya, dyb, dyc, dg1, dg2, dg3, db1, db2, db3 = _merge_bwd(sv["rest"], c["bg"], sv["ya"], sv["yb"], sv["yc"], dm, name=n + "dmerge")
    gs["b_gate"] = jnp.concatenate([db1, db2, db3], axis=1).reshape(3 * D)
    gw["ba"] = _mm(sv["pa"], dya, ta=True, shard_out=True, name=n + "dw_ba", **bf)
    gw["bb"] = _mm(sv["o"], dyb, ta=True, shard_out=True, name=n + "dw_bb", **bf)
    gw["bc"] = _mm(sv["sg"], dyc, ta=True, shard_out=True, name=n + "dw_bc", **bf)
    after, gw = grads_done(l, gw), {}
    dpa = _mm(dya, W["ba"], tb=True, name=n + "dpa")
    do = _mm(dyb, W["bb"], tb=True, after=after, name=n + "do", **bf)
    dsg = _mm(dyc, W["bc"], tb=True, name=n + "dsg")
    da, dwbd, dscale = _pool_bwd(sv["rest"], c["wbd"], c["wbd_t"], sp["pool_scale"][l], dpa, name=n + "dpool")
    gs["pool_w"] = jnp.stack([dwbd[64 * gi:64 * (gi + 1), 64 * gi:64 * (gi + 1)] for gi in range(4)])
    gs["pool_scale"] = dscale.reshape(POOL_W)
    dq, dk, dv, dfq, dfk = _fox_bwd(sv["qkv"], sv["cum"], sv["fk3"], sv["o"], do, sv["lse"], name=n + "dfox")
    dcum = dfq[:, :, :2].transpose(1, 0, 2).reshape(S, FOX_H) + dfk.reshape(FOX_H, S).T
    df, dbf = _fox_post(sv["rest"], c["bpad"], jnp.pad(dcum, ((0, 0), (0, F_LANES - FOX_H))), name=n + "dfox_post")
    gs["b_forget"] = dbf[0, :FOX_H]
    dc, dwm, dbias, dgn = _sgu_bwd(sv["rest"], sp["sgu_norm_g"][l], c["wm"], c["wm_t"], c["sgu_bias"], dsg, name=n + "dsgu")
    gs["sgu_w"] = dwm * jnp.tril(jnp.ones((SGU_CHUNK, SGU_CHUNK), F32))[None]
    gs["sgu_b"] = dbias.reshape(SGU_CHUNK, 4, 64).sum(axis=2).T
    gs["sgu_norm_g"] = dgn.reshape(SGU_W)
    dqkv = jnp.concatenate([dq, dk, dv], axis=1)
    drest = jnp.concatenate([da, df, jnp.zeros((S, OFF_C - OFF_F - F_LANES), BF16), dc, dg1, dg2, dg3], axis=1)
    gw["qkv"] = _mm(sv["h"], dqkv, ta=True, name=n + "dw_qkv", **bf)
    gw["rest"] = _mm(sv["h"], drest, ta=True, name=n + "dw_rest", **bf)
    after = grads_done(l, gw)
    dh = _mm(dqkv, W["qkv"], tb=True, after=after, name=n + "dh_qkv")
    dh = _mm(drest, W["rest"], tb=True, extras=(dh,), epilogue=_add, name=n + "dh")
    dx, gs["norm_mix_g"] = _rms_bwd(sv["x"], sp["norm_mix_g"][l], dh, dx1, name=n + "dnorm_mix")
    return dx, gs


def _local_step(x, mem, target, sp, source, grads_done):
    saved, Ws = [], []
    for l in range(DEPTH):
        x, sv, W = _layer_fwd(l, x, mem, source, sp)
        saved.append(sv)
        Ws.append(W)
    loss, dx, dgf = _final_loss(x, sp["final_norm_g"], target, name="final_loss")
    gss = [None] * DEPTH
    for l in reversed(range(DEPTH)):
        dx, gss[l] = _layer_bwd(l, dx, saved[l], mem, Ws[l], sp, grads_done)
    small = {k: jnp.stack([gss[l][k] for l in range(DEPTH)]) for k in gss[0]}
    small["final_norm_g"] = dgf
    return loss, dx, small


_SMALL = ["norm_mix_g", "b_forget", "pool_w", "pool_scale", "sgu_norm_g", "sgu_w", "sgu_b", "b_gate", "norm_xattn_g",
          "norm_mem_g", "norm_ffn_g", "final_norm_g"]
_COL = {"w_branch_a": "ba", "w_branch_b": "bb", "w_branch_c": "bc", "w_xkv": "xkv", "w_ff1": "ff1"}
_ROW = {"w_out": "out", "w_xq": "xq", "w_xo": "xo", "w_ff2": "ff2"}
_BIG = ["w_in", "w_branch_a", "w_branch_b", "w_branch_c", "w_out", "w_xq", "w_xkv", "w_xo", "w_ff1", "w_ff2"]
_PACK_LANES = 128


def _as_rows(a):
    return a.reshape(-1, a.shape[-1])


def _pack(tensors):
    rows = []
    for a in tensors:
        flat = a.reshape(-1)
        flat = jnp.pad(flat, (0, (-flat.shape[0]) % (8 * _PACK_LANES)))
        rows.append(flat.reshape(-1, _PACK_LANES))
    return jnp.concatenate(rows, axis=0)


def _unpack(packed, like):
    out, r = [], 0
    for a in like:
        size = math.prod(a.shape)
        nr = 8 * (-(-size // (8 * _PACK_LANES)))
        out.append(packed[r:r + nr].reshape(-1)[:size].reshape(a.shape))
        r += nr
    return out


_SHARD_IN = N_IN // N_DEV
_SHARD_IN_PAD = -(-_SHARD_IN // 128) * 128


def _columns(pieces, start, stop):
    out, at = [], 0
    for p in pieces:
        lo, hi = max(start, at), min(stop, at + p.shape[1])
        if lo < hi:
            out.append(p[:, lo - at:hi - at])
        at += p.shape[1]
    return out


def _split_w_in(blocks):
    K = blocks[0].shape[0]
    pad = jnp.zeros((K, OFF_C - OFF_F - FOX_H), blocks[0].dtype)
    cols = functools.partial(_columns, blocks)
    rest = jnp.concatenate(cols(0, R_OFF_Q) + cols(R_OFF_F, R_OFF_C) + [pad] + cols(R_OFF_C, N_IN), axis=1)
    return jnp.concatenate(cols(R_OFF_Q, R_OFF_F), axis=1), rest


def _join_w_in(qkv, rest):
    in_order = [rest[:, :R_OFF_Q], qkv, rest[:, OFF_F:OFF_F + FOX_H], rest[:, OFF_C:]]
    pad = jnp.zeros((qkv.shape[0], _SHARD_IN_PAD - _SHARD_IN), qkv.dtype)
    return jnp.stack([jnp.concatenate(_columns(in_order, _SHARD_IN * d, _SHARD_IN * (d + 1)) + [pad], axis=1) for d in range(N_DEV)])


_FIRST = ["w_in"]
_LATER = [k for k in _BIG if k not in _FIRST]


def _layer_weights(gathered):
    W = {}
    if "w_in" in gathered:
        W.update(zip(("qkv", "rest"), _split_w_in([gathered["w_in"][d][:, :_SHARD_IN] for d in range(N_DEV)])))
    for name, key in _COL.items():
        if name in gathered:
            W[key] = _Gathered(gathered[name])
    for name, key in _ROW.items():
        if name in gathered:
            W[key] = gathered[name].reshape(-1, gathered[name].shape[-1])
    return W


def _grad_blocks(gw):
    parts = {}
    if "qkv" in gw:
        parts["w_in"] = _join_w_in(gw["qkv"], gw["rest"])
    for name, key in _COL.items():
        if key in gw:
            parts[name] = gw[key]
    for name, key in _ROW.items():
        if key in gw:
            parts[name] = gw[key].reshape(N_DEV, -1, gw[key].shape[-1])
    return parts


def kernel(x, mem, norm_mix_g, w_in, b_forget, pool_w, pool_scale, sgu_norm_g, sgu_w, sgu_b, w_branch_a, w_branch_b, w_branch_c, b_gate, w_out, norm_xattn_g, norm_mem_g, w_xq, w_xkv, w_xo, norm_ffn_g, w_ff1, w_ff2, final_norm_g, loss_target, m_norm_mix_g, m_w_in, m_b_forget, m_pool_w, m_pool_scale, m_sgu_norm_g, m_sgu_w, m_sgu_b, m_w_branch_a, m_w_branch_b, m_w_branch_c, m_b_gate, m_w_out, m_norm_xattn_g, m_norm_mem_g, m_w_xq, m_w_xkv, m_w_xo, m_norm_ffn_g, m_w_ff1, m_w_ff2, m_final_norm_g, v_norm_mix_g, v_w_in, v_b_forget, v_pool_w, v_pool_scale, v_sgu_norm_g, v_sgu_w, v_sgu_b, v_w_branch_a, v_w_branch_b, v_w_branch_c, v_b_gate, v_w_out, v_norm_xattn_g, v_norm_mem_g, v_w_xq, v_w_xkv, v_w_xo, v_norm_ffn_g, v_w_ff1, v_w_ff2, v_final_norm_g):
    names = ["norm_mix_g", "w_in", "b_forget", "pool_w", "pool_scale", "sgu_norm_g", "sgu_w", "sgu_b", "w_branch_a", "w_branch_b",
             "w_branch_c", "b_gate", "w_out", "norm_xattn_g", "norm_mem_g", "w_xq", "w_xkv", "w_xo", "norm_ffn_g", "w_ff1", "w_ff2",
             "final_norm_g"]
    w = dict(zip(names, [norm_mix_g, w_in, b_forget, pool_w, pool_scale, sgu_norm_g, sgu_w, sgu_b, w_branch_a, w_branch_b, w_branch_c,
                         b_gate, w_out, norm_xattn_g, norm_mem_g, w_xq, w_xkv, w_xo, norm_ffn_g, w_ff1, w_ff2, final_norm_g]))
    m = dict(zip(names, [m_norm_mix_g, m_w_in, m_b_forget, m_pool_w, m_pool_scale, m_sgu_norm_g, m_sgu_w, m_sgu_b, m_w_branch_a,
                         m_w_branch_b, m_w_branch_c, m_b_gate, m_w_out, m_norm_xattn_g, m_norm_mem_g, m_w_xq, m_w_xkv, m_w_xo,
                         m_norm_ffn_g, m_w_ff1, m_w_ff2, m_final_norm_g]))
    v = dict(zip(names, [v_norm_mix_g, v_w_in, v_b_forget, v_pool_w, v_pool_scale, v_sgu_norm_g, v_sgu_w, v_sgu_b, v_w_branch_a,
                         v_w_branch_b, v_w_branch_c, v_b_gate, v_w_out, v_norm_xattn_g, v_norm_mem_g, v_w_xq, v_w_xkv, v_w_xo,
                         v_norm_ffn_g, v_w_ff1, v_w_ff2, v_final_norm_g]))

    sp = {k: w[k] for k in _SMALL}
    shards = [{k: w[k][l].astype(BF16) for k in _BIG} for l in range(DEPTH)]
    for sh in shards:
        sh["w_in"] = jnp.pad(sh["w_in"], ((0, 0), (0, _SHARD_IN_PAD - _SHARD_IN)))
    me = _dev_index(*_position())

    def gather_out(l, keys, name, after=None):
        srcs = [shards[l][k] for k in keys]
        lands = [_own_block_placed(a, jax.ShapeDtypeStruct((N_DEV, *a.shape), a.dtype)) for a in srcs]
        state, token = _split_start(_plan_gather_out, srcs, lands, after=after, name=name + "_out_start")
        return (keys, name, state), token

    def gather_pass(job, value):
        keys, name, state = job
        lands = _split_wait(_plan_gather_out, state, value, name=name + "_out_wait")
        state, token = _split_start(_plan_gather_pass, [], lands, name=name + "_pass_start")
        return (keys, name, state), token, lands[0]

    def gather_end(job, value):
        keys, name, state = job
        return _layer_weights(dict(zip(keys, _split_wait(_plan_gather_pass, state, value, name=name + "_pass_wait"))))

    jobs = {}

    def source(l, point, value):
        if (l, point) == (0, "begin"):
            first = _all_gather([shards[0][k] for k in _FIRST], name="gather_l0_first")
            jobs["l0"], token = gather_out(0, _LATER, "gather_l0", after=first[0])
            return _layer_weights(dict(zip(_FIRST, first))), token
        if (l, point) == (0, "attended"):
            jobs["l0"], _, arrived = gather_pass(jobs["l0"], value)
            jobs["l1"], jobs["token"] = gather_out(1, _BIG, "gather_l1", after=arrived)
            return {}, None
        if (l, point) == (0, "mixed"):
            return gather_end(jobs.pop("l0"), value), jobs.pop("token")
        if (l, point) == (0, "expanded"):
            jobs["l1"], token, _ = gather_pass(jobs["l1"], value)
            return {}, token
        if (l, point) == (1, "begin"):
            return gather_end(jobs.pop("l1"), value), None
        return {}, None

    received = [{} for _ in range(DEPTH)]
    travelling = []

    def grads_done(l, gw):
        blocks = _grad_blocks(gw)
        keys = [k for k in _BIG if k in blocks]
        parts = [blocks[k] for k in keys]
        group = f"exchange_grads_l{l}_" + ("in" if "w_in" in blocks else "merge" if "w_out" in blocks else "mlp")
        lands = [_own_block_placed(lax.dynamic_index_in_dim(p, me, 0, keepdims=False), p) for p in parts]
        state, token = _split_start(_plan_exchange, parts, lands, name=group + "_start")
        travelling.append((l, keys, state, group + "_wait"))
        return token

    loss, dx, small = _local_step(x[0], mem[0], loss_target[0], sp, source, grads_done)
    loss = lax.psum(loss[0, 0], ("x", "y", "c"))
    grads, deltas, new_m, new_v = {}, {}, {}, {}

    def update_small(after):
        like = [w[k] for k in _SMALL]
        g_small = _unpack(_all_reduce(_pack([small[k] for k in _SMALL]), after, name="all_reduce_small"), like)
        rows = lambda d: [_as_rows(d[k]) for k in _SMALL]
        outs = _adamw_small([_as_rows(g) for g in g_small], rows(w), rows(m), rows(v), name="adamw_small")
        grads.update(zip(_SMALL, g_small))
        for dst, vals in zip((deltas, new_m, new_v), outs):
            dst.update({k: a.reshape(w[k].shape) for k, a in zip(_SMALL, vals)})
        return outs[0][0]

    done = dx
    groups = list(dict.fromkeys(tuple(keys) for _, keys, _, _ in travelling))
    for group_keys in groups:
        if group_keys == groups[-1]:
            done = update_small(done)
        for l, keys, state, wait_name in travelling:
            if tuple(keys) == group_keys:
                received[l].update(zip(keys, _split_wait(_plan_exchange, state, done, name=wait_name)))
        for k in group_keys:
            outs = _adamw_sharded([received[l][k] for l in range(DEPTH)], w[k], m[k], v[k], name="adamw_" + k)
            grads[k], deltas[k], new_m[k], new_v[k] = outs
        done = grads[group_keys[-1]]

    return (loss, dx[None], *[grads[k] for k in names], *[deltas[k] for k in names], *[new_m[k] for k in names],
            *[new_v[k] for k in names])
```

```python
import functools
import math

import jax
import jax.numpy as jnp
from jax import lax
from jax.experimental import pallas as pl
from jax.experimental.pallas import tpu as pltpu

F32 = jnp.float32
BF16 = jnp.bfloat16
MESH = pl.DeviceIdType.MESH

N_DEV = 8
D = 1024
DEPTH = 2
EPS = 1e-6
NEG = -1e30
POOL_W = 256
FOX_H = 8
FOX_DH = 64
FOX_W = 512
SGU_W = 256
SGU_CHUNK = 128
XH = 4
XDH = 256
N_IN = 5384
R_OFF_Q, R_OFF_F, R_OFF_C = 256, 1792, 1800
QKV_W = 3 * FOX_W
OFF_A, OFF_F, OFF_C, OFF_G, REST_W = 0, 256, 512, 1024, 4096
F_LANES = 128

ADAM_LR = 0.001
ADAM_B1 = 0.9
ADAM_B2 = 0.999
ADAM_EPS = 1e-08
ADAM_WD = 0.01
ADAM_STEP = 10

VMEM_LIMIT = 56 * 1024 * 1024


def _tile(n, pref):
    t = min(n, pref)
    while n % t:
        t -= 128
    assert t > 0, (n, pref)
    return t


def _params(sem=None):
    return pltpu.CompilerParams(dimension_semantics=sem, vmem_limit_bytes=VMEM_LIMIT)


def _dot(a, b, ca, cb):
    return lax.dot_general(a, b, (((ca,), (cb,)), ((), ())), preferred_element_type=F32)


def _sigmoid(z):
    return 1.0 / (1.0 + jnp.exp(-z))


_GELU_K = math.sqrt(2.0 / math.pi)
_GELU_C = 0.044715


def _gelu(x):
    return 0.5 * x * (1.0 + jnp.tanh(_GELU_K * (x + _GELU_C * x * x * x)))


def _gelu_grad(x):
    t = jnp.tanh(_GELU_K * (x + _GELU_C * x * x * x))
    return 0.5 * (1.0 + t) + 0.5 * x * (1.0 - t * t) * _GELU_K * (1.0 + 3.0 * _GELU_C * x * x)


def _rows(shape):
    return lax.broadcasted_iota(jnp.int32, shape, 0)


def _lanes(shape):
    return lax.broadcasted_iota(jnp.int32, shape, 1)


class _Gathered:
    def __init__(self, arr):
        self.arr = arr
        self.shape = (arr.shape[1], N_DEV * arr.shape[2])


_TOKEN = (8, 128)


def _mm(a, b, *, ta=False, tb=False, extras=(), epilogue=None, out_dtypes=(F32,), shard_out=False, after=None, tm=None, tn=512, tk=None,
        name):
    M, K = (a.shape[1], a.shape[0]) if ta else a.shape
    N, Kb = b.shape if tb else b.shape[::-1]
    assert Kb == K, (a.shape, b.shape, ta, tb)
    gathered = isinstance(b, _Gathered)
    if gathered:
        if tb:
            tk = b.arr.shape[2]
        else:
            tn = b.arr.shape[2]
    if shard_out:
        tn = N // N_DEV
    tm = _tile(M, tm or (1024 if ta else 2048))
    tn = _tile(N, tn)
    tk = _tile(K, tk or (2048 if ta else 1024))
    nk = K // tk
    ca, cb = (0 if ta else 1), (1 if tb else 0)
    n_ex, n_out = len(extras), len(out_dtypes)
    tokens = [] if after is None else [after]
    n_in = 2 + n_ex + len(tokens)
    if epilogue is None:
        epilogue = lambda acc: (acc,)

    def body(*refs):
        a_ref, b_ref = refs[:2]
        ex_refs = refs[2:2 + n_ex]
        o_refs = refs[n_in:n_in + n_out]
        part = _dot(a_ref[...].astype(BF16), b_ref[...].astype(BF16), ca, cb)

        def finish(acc):
            for o_ref, val in zip(o_refs, epilogue(acc, *[e[...] for e in ex_refs])):
                o_ref[...] = val.astype(o_ref.dtype)

        if nk == 1:
            finish(part)
        else:
            acc_ref = refs[-1]
            k = pl.program_id(2)

            @pl.when(k == 0)
            def _():
                acc_ref[...] = part

            @pl.when(k > 0)
            def _():
                acc_ref[...] += part

            @pl.when(k == nk - 1)
            def _():
                finish(acc_ref[...])

    a_spec = pl.BlockSpec((tk, tm), lambda i, j, k: (k, i)) if ta else pl.BlockSpec((tm, tk), lambda i, j, k: (i, k))
    if not gathered:
        b_arr = b
        b_spec = pl.BlockSpec((tn, tk), lambda i, j, k: (j, k)) if tb else pl.BlockSpec((tk, tn), lambda i, j, k: (k, j))
    else:
        b_arr = b.arr
        if tb:
            b_spec = pl.BlockSpec((None, tn, tk), lambda i, j, k: (k, j, 0))
        else:
            b_spec = pl.BlockSpec((None, tk, tn), lambda i, j, k: (j, k, 0))
    tile = pl.BlockSpec((tm, tn), lambda i, j, k: (i, j))
    if shard_out:
        out_specs = [pl.BlockSpec((None, tm, tn), lambda i, j, k: (j, i, 0))] * n_out
        out_shape = [jax.ShapeDtypeStruct((N_DEV, M, tn), dt) for dt in out_dtypes]
    else:
        out_specs = [tile] * n_out
        out_shape = [jax.ShapeDtypeStruct((M, N), dt) for dt in out_dtypes]
    size = lambda dt: jnp.dtype(dt).itemsize
    vmem = 2 * (tm * tk * size(a.dtype) + tk * tn * size(b_arr.dtype)
                + tm * tn * (sum(size(e.dtype) for e in extras) + sum(map(size, out_dtypes))))
    vmem += tm * tn * 4 * (nk > 1)
    assert vmem <= VMEM_LIMIT - (4 << 20), (name, vmem)
    outs = pl.pallas_call(
        body,
        name=name,
        grid=(M // tm, N // tn, nk),
        in_specs=[a_spec, b_spec] + [tile] * n_ex + [pl.BlockSpec(_TOKEN, lambda i, j, k: (0, 0))] * len(tokens),
        out_specs=out_specs,
        out_shape=out_shape,
        scratch_shapes=[pltpu.VMEM((tm, tn), F32)] if nk > 1 else [],
        compiler_params=_params(("parallel", "parallel", "arbitrary")),
    )(a, b_arr, *extras, *tokens)
    return outs[0] if n_out == 1 else outs


def _add(acc, res):
    return (acc + res,)


def _rms_fwd(x, g, *, after=None, name):
    R, C = x.shape
    tm = _tile(R, 256)
    tokens = [] if after is None else [after]

    def body(x_ref, g_ref, *rest):
        xv = x_ref[...]
        r = lax.rsqrt(jnp.mean(xv * xv, axis=-1, keepdims=True) + EPS)
        rest[-1][...] = (xv * r * g_ref[...]).astype(BF16)

    return pl.pallas_call(
        body,
        name=name,
        grid=(R // tm,),
        in_specs=[pl.BlockSpec((tm, C), lambda i: (i, 0)), pl.BlockSpec((1, C), lambda i: (0, 0))]
        + [pl.BlockSpec(_TOKEN, lambda i: (0, 0))] * len(tokens),
        out_specs=pl.BlockSpec((tm, C), lambda i: (i, 0)),
        out_shape=jax.ShapeDtypeStruct((R, C), BF16),
        compiler_params=_params(("parallel",)),
    )(x, g.reshape(1, C), *tokens)


def _rms_bwd(x, g, dh, dres, *, name):
    R, C = x.shape
    tm = _tile(R, 256)

    def body(x_ref, g_ref, dh_ref, dres_ref, dx_ref, dg_ref):
        xv = x_ref[...]
        r = lax.rsqrt(jnp.mean(xv * xv, axis=-1, keepdims=True) + EPS)
        xn = xv * r
        dh_v = dh_ref[...].astype(F32)
        dxn = dh_v * g_ref[...]
        dx_ref[...] = r * (dxn - xn * jnp.mean(dxn * xn, axis=-1, keepdims=True)) + dres_ref[...]
        part = jnp.sum(dh_v * xn, axis=0, keepdims=True)

        @pl.when(pl.program_id(0) == 0)
        def _():
            dg_ref[...] = part

        @pl.when(pl.program_id(0) > 0)
        def _():
            dg_ref[...] += part

    row = pl.BlockSpec((tm, C), lambda i: (i, 0))
    vec = pl.BlockSpec((1, C), lambda i: (0, 0))
    dx, dg = pl.pallas_call(
        body,
        name=name,
        grid=(R // tm,),
        in_specs=[row, vec, row, row],
        out_specs=[row, vec],
        out_shape=[jax.ShapeDtypeStruct((R, C), F32), jax.ShapeDtypeStruct((1, C), F32)],
        compiler_params=_params(("arbitrary",)),
    )(x, g.reshape(1, C), dh, dres)
    return dx, dg.reshape(C)


def _final_loss(x, g, target, *, name):
    R, C = x.shape
    tm = _tile(R, 256)

    def body(x_ref, g_ref, t_ref, loss_ref, dx_ref, dg_ref):
        xv = x_ref[...]
        r = lax.rsqrt(jnp.mean(xv * xv, axis=-1, keepdims=True) + EPS)
        xn = xv * r
        gv = g_ref[...]
        err = xn * gv - t_ref[...]
        lpart = (0.5 / C) * jnp.sum(jnp.sum(err * err, axis=1, keepdims=True), axis=0, keepdims=True)
        dy = err * (1.0 / C)
        dxn = dy * gv
        dx_ref[...] = r * (dxn - xn * jnp.mean(dxn * xn, axis=-1, keepdims=True))
        gpart = jnp.sum(dy * xn, axis=0, keepdims=True)

        @pl.when(pl.program_id(0) == 0)
        def _():
            loss_ref[...] = lpart
            dg_ref[...] = gpart

        @pl.when(pl.program_id(0) > 0)
        def _():
            loss_ref[...] += lpart
            dg_ref[...] += gpart

    row = pl.BlockSpec((tm, C), lambda i: (i, 0))
    vec = pl.BlockSpec((1, C), lambda i: (0, 0))
    loss, dx, dg = pl.pallas_call(
        body,
        name=name,
        grid=(R // tm,),
        in_specs=[row, vec, row],
        out_specs=[pl.BlockSpec((1, 1), lambda i: (0, 0)), row, vec],
        out_shape=[jax.ShapeDtypeStruct((1, 1), F32), jax.ShapeDtypeStruct((R, C), F32), jax.ShapeDtypeStruct((1, C), F32)],
        compiler_params=_params(("arbitrary",)),
    )(x, g.reshape(1, C), target)
    return loss, dx, dg.reshape(C)


def _pool_select(lane, vals):
    out = vals[3]
    for gi in (2, 1, 0):
        out = jnp.where(lane < 64 * (gi + 1), vals[gi], out)
    return out


def _pool_diff(a):
    row, lane = _rows(a.shape), _lanes(a.shape)

    def down(v, k):
        return jnp.where(row >= k, pltpu.roll(v, k, 0), 0.0)

    s2 = a + down(a, 1)
    s4 = s2 + down(s2, 2)
    s8 = s4 + down(s4, 4)
    s16 = s8 + down(s8, 8)
    wsum = _pool_select(lane, (s2, s4, s8, s16))
    win = _pool_select(lane, (2, 4, 8, 16))
    cnt = jnp.minimum(row + 1, win).astype(F32)
    return wsum / cnt - a, cnt


def _pool_diff_t(dd, cnt):
    S = dd.shape[0]
    row, lane = _rows(dd.shape), _lanes(dd.shape)

    def up(v, k):
        return jnp.where(row < S - k, pltpu.roll(v, S - k, 0), 0.0)

    e = dd / cnt
    s2 = e + up(e, 1)
    s4 = s2 + up(s2, 2)
    s8 = s4 + up(s4, 4)
    s16 = s8 + up(s8, 8)
    return _pool_select(lane, (s2, s4, s8, s16)) - dd


def _pool_fwd(rest, wbd, scale, *, name):
    S = rest.shape[0]

    def body(a_ref, w_ref, s_ref, o_ref):
        d, _ = _pool_diff(a_ref[...])
        yp = _dot(d.astype(BF16), w_ref[...], 1, 0)
        o_ref[...] = (yp * s_ref[...]).astype(BF16)

    return pl.pallas_call(
        body,
        name=name,
        grid=(1,),
        in_specs=[
            pl.BlockSpec((S, POOL_W), lambda i: (0, OFF_A // POOL_W)),
            pl.BlockSpec((POOL_W, POOL_W), lambda i: (0, 0)),
            pl.BlockSpec((1, POOL_W), lambda i: (0, 0)),
        ],
        out_specs=pl.BlockSpec((S, POOL_W), lambda i: (0, 0)),
        out_shape=jax.ShapeDtypeStruct((S, POOL_W), BF16),
        compiler_params=_params(("arbitrary",)),
    )(rest, wbd, scale.reshape(1, POOL_W))


def _pool_bwd(rest, wbd, wbd_t, scale, dpa, *, name):
    S = rest.shape[0]

    def body(a_ref, w_ref, wt_ref, s_ref, dpa_ref, da_ref, dw_ref, ds_ref):
        d, cnt = _pool_diff(a_ref[...])
        db = d.astype(BF16)
        yp = _dot(db, w_ref[...], 1, 0)
        dpa_v = dpa_ref[...]
        ds_ref[...] = jnp.sum(dpa_v * yp, axis=0, keepdims=True)
        dyp = (dpa_v * s_ref[...]).astype(BF16)
        dw_ref[...] = _dot(db, dyp, 0, 0)
        dd = _dot(dyp, wt_ref[...], 1, 0)
        da_ref[...] = _pool_diff_t(dd, cnt).astype(BF16)

    full = pl.BlockSpec((S, POOL_W), lambda i: (0, 0))
    sq = pl.BlockSpec((POOL_W, POOL_W), lambda i: (0, 0))
    vec = pl.BlockSpec((1, POOL_W), lambda i: (0, 0))
    return pl.pallas_call(
        body,
        name=name,
        grid=(1,),
        in_specs=[pl.BlockSpec((S, POOL_W), lambda i: (0, OFF_A // POOL_W)), sq, sq, vec, full],
        out_specs=[full, sq, vec],
        out_shape=[
            jax.ShapeDtypeStruct((S, POOL_W), BF16),
            jax.ShapeDtypeStruct((POOL_W, POOL_W), F32),
            jax.ShapeDtypeStruct((1, POOL_W), F32),
        ],
        compiler_params=_params(("arbitrary",)),
    )(rest, wbd, wbd_t, scale.reshape(1, POOL_W), dpa)


def _log_sigmoid(z):
    return jnp.minimum(z, 0.0) - jnp.log(1.0 + jnp.exp(-jnp.abs(z)))


_F_SPEC_COL = OFF_F // F_LANES


def _fox_prep(rest, bpad, *, name):
    S = rest.shape[0]

    def body(f_ref, b_ref, o_ref, ot_ref):
        acc = _log_sigmoid(f_ref[...] + b_ref[...])
        row = _rows(acc.shape)
        k = 1
        while k < S:
            acc = acc + jnp.where(row >= k, pltpu.roll(acc, k, 0), 0.0)
            k *= 2
        o_ref[...] = acc
        ot_ref[...] = acc.T

    return pl.pallas_call(
        body,
        name=name,
        grid=(1,),
        in_specs=[pl.BlockSpec((S, F_LANES), lambda i: (0, _F_SPEC_COL)), pl.BlockSpec((1, F_LANES), lambda i: (0, 0))],
        out_specs=[pl.BlockSpec((S, F_LANES), lambda i: (0, 0)), pl.BlockSpec((F_LANES, S), lambda i: (0, 0))],
        out_shape=[jax.ShapeDtypeStruct((S, F_LANES), F32), jax.ShapeDtypeStruct((F_LANES, S), F32)],
        compiler_params=_params(("arbitrary",)),
    )(rest, bpad)


def _fox_post(rest, bpad, dcum, *, name):
    S = rest.shape[0]

    def body(f_ref, b_ref, d_ref, df_ref, db_ref):
        acc = d_ref[...]
        row = _rows(acc.shape)
        k = 1
        while k < S:
            acc = acc + jnp.where(row < S - k, pltpu.roll(acc, S - k, 0), 0.0)
            k *= 2
        df = acc * (1.0 - _sigmoid(f_ref[...] + b_ref[...]))
        df_ref[...] = df.astype(BF16)
        db_ref[...] = jnp.sum(df, axis=0, keepdims=True)

    full = pl.BlockSpec((S, F_LANES), lambda i: (0, 0))
    vec = pl.BlockSpec((1, F_LANES), lambda i: (0, 0))
    return pl.pallas_call(
        body,
        name=name,
        grid=(1,),
        in_specs=[pl.BlockSpec((S, F_LANES), lambda i: (0, _F_SPEC_COL)), vec, full],
        out_specs=[full, vec],
        out_shape=[jax.ShapeDtypeStruct((S, F_LANES), BF16), jax.ShapeDtypeStruct((1, F_LANES), F32)],
        compiler_params=_params(("arbitrary",)),
    )(rest, bpad, dcum)


_FOX_SCALE = FOX_DH ** -0.5
_PAIRS = FOX_H // 2


def _scaled(v):
    return (v.astype(F32) * _FOX_SCALE).astype(BF16)


def _diag_mask(s):
    return jnp.where(_rows(s.shape) >= _lanes(s.shape), s, NEG)


def _fox_fwd(qkv, cum, fk3, *, name):
    S = qkv.shape[0]
    nk, t = fk3.shape[1:]

    def body(q_ref, k_ref, v_ref, cum_ref, fk_ref, o_ref, lse_ref, m_sc, l_sc, acc_sc):
        i = pl.program_id(0)
        lane = _lanes((t, 128))
        lo = lane < FOX_DH
        cumv = cum_ref[...]
        qm, fq = [], []
        for h in range(FOX_H):
            qs = _scaled(q_ref[:, 128 * (h // 2):128 * (h // 2 + 1)])
            zero = jnp.zeros_like(qs)
            qm.append(jnp.where(lo, qs, zero) if h % 2 == 0 else jnp.where(lo, zero, qs))
            fq.append(cumv[:, h:h + 1])
        m_sc[...] = jnp.full(m_sc.shape, NEG, F32)
        l_sc[...] = jnp.zeros(l_sc.shape, F32)
        acc_sc[...] = jnp.zeros(acc_sc.shape, F32)

        def tile(j, masked):
            k0 = pl.multiple_of(j * t, t)
            for hp in range(_PAIRS):
                cols = slice(128 * hp, 128 * (hp + 1))
                kb = k_ref[pl.ds(k0, t), cols]
                vb = v_ref[pl.ds(k0, t), cols]
                alphas, pvs = [], []
                for h in (2 * hp, 2 * hp + 1):
                    s = _dot(qm[h], kb, 1, 1) + fq[h] - fk_ref[h, pl.ds(j, 1), :]
                    if masked:
                        s = _diag_mask(s)
                    m_old = m_sc[h]
                    m_new = jnp.maximum(m_old, jnp.max(s, axis=-1, keepdims=True))
                    p = jnp.exp(s - m_new)
                    alpha = jnp.exp(m_old - m_new)
                    l_sc[h] = alpha * l_sc[h] + jnp.sum(p, axis=-1, keepdims=True)
                    m_sc[h] = m_new
                    alphas.append(alpha)
                    pvs.append(_dot(p.astype(BF16), vb, 1, 0))
                acc_sc[:, cols] = jnp.where(lo, alphas[0], alphas[1]) * acc_sc[:, cols] + jnp.where(lo, pvs[0], pvs[1])

        def step(j, carry):
            tile(j, False)
            return carry

        lax.fori_loop(0, i, step, 0)
        tile(i, True)
        for hp in range(_PAIRS):
            cols = slice(128 * hp, 128 * (hp + 1))
            o_ref[:, cols] = acc_sc[:, cols] / jnp.where(lo, l_sc[2 * hp], l_sc[2 * hp + 1])
            lse = [m_sc[h] + jnp.log(l_sc[h]) for h in (2 * hp, 2 * hp + 1)]
            lse_ref[hp] = jnp.where(lane == 0, lse[0], jnp.where(lane == 1, lse[1], 0.0))

    whole = lambda col: pl.BlockSpec((S, FOX_W), lambda i: (0, col))
    return pl.pallas_call(
        body,
        name=name,
        grid=(S // t,),
        in_specs=[
            pl.BlockSpec((t, FOX_W), lambda i: (i, 0)), whole(1), whole(2),
            pl.BlockSpec((t, F_LANES), lambda i: (i, 0)),
            pl.BlockSpec((FOX_H, nk, t), lambda i: (0, 0, 0)),
        ],
        out_specs=[pl.BlockSpec((t, FOX_W), lambda i: (i, 0)), pl.BlockSpec((_PAIRS, t, 128), lambda i: (0, i, 0))],
        out_shape=[jax.ShapeDtypeStruct((S, FOX_W), F32), jax.ShapeDtypeStruct((_PAIRS, S, 128), F32)],
        scratch_shapes=[pltpu.VMEM((FOX_H, t, 1), F32), pltpu.VMEM((FOX_H, t, 1), F32), pltpu.VMEM((t, FOX_W), F32)],
        compiler_params=_params(("arbitrary",)),
    )(qkv, qkv, qkv, cum, fk3)


def _fox_bwd(qkv, cum, fk3, o, do, lse, *, name):
    S = qkv.shape[0]
    nk, t = fk3.shape[1:]
    q_at, k_at, v_at = 0, FOX_W, 2 * FOX_W

    def body(qkv_ref, cum_ref, fk_ref, o_ref, do_ref, lse_ref, dq_ref, dk_ref, dv_ref, dfq_ref, dfk_ref,
             qs_sc, ks_sc, delta_sc, dq_sc):
        lane = _lanes((t, 128))
        lo = lane < FOX_DH
        mine = lambda h: lo if h % 2 == 0 else jnp.logical_not(lo)

        def by_head(tile, values):
            for h, val in enumerate(values):
                tile = jnp.where(lane == h, val, tile)
            return tile

        def prep(i, carry):
            r = pl.ds(pl.multiple_of(i * t, t), t)
            qs_sc[r, :] = _scaled(qkv_ref[r, q_at:q_at + FOX_W])
            ks_sc[r, :] = _scaled(qkv_ref[r, k_at:k_at + FOX_W])
            sums = []
            for hp in range(_PAIRS):
                cols = slice(128 * hp, 128 * (hp + 1))
                prod = do_ref[r, cols].astype(F32) * o_ref[r, cols]
                sums += [jnp.sum(jnp.where(mine(h), prod, 0.0), axis=-1, keepdims=True) for h in (2 * hp, 2 * hp + 1)]
            delta_sc[r, :] = by_head(jnp.zeros((t, 128), F32), sums)
            dfq_ref[r, :] = jnp.zeros((t, 128), F32)
            dq_sc[r, :] = jnp.zeros((t, FOX_W), F32)
            return carry

        lax.fori_loop(0, nk, prep, 0)

        def kv_tile(j, carry):
            kr = pl.ds(pl.multiple_of(j * t, t), t)

            def q_tile(i, acc, masked):
                dk, dv, dfk = list(acc[:_PAIRS]), list(acc[_PAIRS:2 * _PAIRS]), list(acc[2 * _PAIRS:])
                qr = pl.ds(pl.multiple_of(i * t, t), t)
                delta_t, cum_t = delta_sc[qr, :], cum_ref[qr, :]
                row_sums = []
                for hp in range(_PAIRS):
                    cols = slice(128 * hp, 128 * (hp + 1))
                    kb = qkv_ref[kr, k_at + 128 * hp:k_at + 128 * (hp + 1)]
                    vb = qkv_ref[kr, v_at + 128 * hp:v_at + 128 * (hp + 1)]
                    ksb, qsb, dob = ks_sc[kr, cols], qs_sc[qr, cols], do_ref[qr, cols]
                    zero = jnp.zeros_like(qsb)
                    dq_t = jnp.zeros((t, 128), F32)
                    for h in (2 * hp, 2 * hp + 1):
                        qe, doe, ke = (jnp.where(mine(h), a, zero) for a in (qsb, dob, ksb))
                        s = _dot(qe, kb, 1, 1) + cum_t[:, h:h + 1] - fk_ref[h, pl.ds(j, 1), :]
                        if masked:
                            s = _diag_mask(s)
                        p = jnp.exp(s - lse_ref[hp, qr, h % 2:h % 2 + 1])
                        dv[hp] = dv[hp] + _dot(p.astype(BF16), doe, 0, 0)
                        dp = _dot(doe, vb, 1, 1)
                        ds = p * (dp - delta_t[:, h:h + 1])
                        dsb = ds.astype(BF16)
                        dk[hp] = dk[hp] + _dot(dsb, qe, 0, 0)
                        dq_t = dq_t + _dot(dsb, ke, 1, 0)
                        row_sums.append(jnp.sum(ds, axis=-1, keepdims=True))
                        dfk[h] = dfk[h] - jnp.sum(ds, axis=0, keepdims=True)
                    dq_sc[qr, cols] += dq_t
                dfq_ref[qr, :] += by_head(jnp.zeros((t, 128), F32), row_sums)
                return (*dk, *dv, *dfk)

            init = tuple([jnp.zeros((t, 128), F32)] * (2 * _PAIRS) + [jnp.zeros((1, t), F32)] * FOX_H)
            acc = q_tile(j, init, True)
            acc = lax.fori_loop(j + 1, nk, functools.partial(q_tile, masked=False), acc)
            for hp in range(_PAIRS):
                cols = slice(128 * hp, 128 * (hp + 1))
                dk_ref[kr, cols] = acc[hp].astype(BF16)
                dv_ref[kr, cols] = acc[_PAIRS + hp].astype(BF16)
            for h in range(FOX_H):
                dfk_ref[h, pl.ds(j, 1), :] = acc[2 * _PAIRS + h]
            return carry

        lax.fori_loop(0, nk, kv_tile, 0)
        dq_ref[...] = dq_sc[...].astype(BF16)

    vm = pl.BlockSpec(memory_space=pltpu.VMEM)
    big = jax.ShapeDtypeStruct((S, FOX_W), BF16)
    return pl.pallas_call(
        body,
        name=name,
        in_specs=[vm] * 6,
        out_specs=[vm] * 5,
        out_shape=[big, big, big, jax.ShapeDtypeStruct((S, 128), F32), jax.ShapeDtypeStruct((FOX_H, nk, t), F32)],
        scratch_shapes=[pltpu.VMEM((S, FOX_W), BF16), pltpu.VMEM((S, FOX_W), BF16), pltpu.VMEM((S, 128), F32),
                        pltpu.VMEM((S, FOX_W), F32)],
        compiler_params=pltpu.CompilerParams(vmem_limit_bytes=VMEM_LIMIT),
    )(qkv, cum, fk3, o, do, lse)


def _group_mask(lane, gi):
    return (lane >= 64 * gi) & (lane < 64 * (gi + 1))


_U_COL = OFF_C // SGU_W


def _sgu_fwd(rest, gn, wm, bias, *, name):
    S = rest.shape[0]
    ts = _tile(S, 512)
    nc = ts // SGU_CHUNK

    def body(u_ref, v_ref, g_ref, w_ref, b_ref, o_ref):
        zv = _gelu(v_ref[...])
        vn = zv * lax.rsqrt(jnp.mean(zv * zv, axis=-1, keepdims=True) + EPS) * g_ref[...]
        lane = _lanes((SGU_CHUNK, SGU_W))
        for c in range(nc):
            rows = slice(c * SGU_CHUNK, (c + 1) * SGU_CHUNK)
            vcb = vn[rows].astype(BF16)
            mixed = b_ref[...]
            for gi in range(4):
                mixed = mixed + jnp.where(_group_mask(lane, gi), _dot(w_ref[gi], vcb, 1, 0), 0.0)
            o_ref[rows, :] = (_gelu(u_ref[rows, :]) * mixed).astype(BF16)

    return pl.pallas_call(
        body,
        name=name,
        grid=(S // ts,),
        in_specs=[
            pl.BlockSpec((ts, SGU_W), lambda i: (i, _U_COL)),
            pl.BlockSpec((ts, SGU_W), lambda i: (i, _U_COL + 1)),
            pl.BlockSpec((1, SGU_W), lambda i: (0, 0)),
            pl.BlockSpec((4, SGU_CHUNK, SGU_CHUNK), lambda i: (0, 0, 0)),
            pl.BlockSpec((SGU_CHUNK, SGU_W), lambda i: (0, 0)),
        ],
        out_specs=pl.BlockSpec((ts, SGU_W), lambda i: (i, 0)),
        out_shape=jax.ShapeDtypeStruct((S, SGU_W), BF16),
        compiler_params=_params(("parallel",)),
    )(rest, rest, gn.reshape(1, SGU_W), wm, bias)


def _sgu_bwd(rest, gn, wm, wm_t, bias, dsg, *, name):
    S = rest.shape[0]
    ts = _tile(S, 512)
    nc = ts // SGU_CHUNK

    def body(u_ref, v_ref, g_ref, w_ref, wt_ref, b_ref, dsg_ref, dc_ref, dw_ref, db_ref, dg_ref):
        first = pl.program_id(0) == 0

        @pl.when(first)
        def _():
            dw_ref[...] = jnp.zeros_like(dw_ref)
            db_ref[...] = jnp.zeros_like(db_ref)
            dg_ref[...] = jnp.zeros_like(dg_ref)

        gv = g_ref[...]
        lane = _lanes((SGU_CHUNK, SGU_W))
        for c in range(nc):
            rows = slice(c * SGU_CHUNK, (c + 1) * SGU_CHUNK)
            vpre = v_ref[rows, :]
            upre = u_ref[rows, :]
            zv = _gelu(vpre)
            r = lax.rsqrt(jnp.mean(zv * zv, axis=-1, keepdims=True) + EPS)
            zn = zv * r
            vcb = (zn * gv).astype(BF16)
            mixed = b_ref[...]
            for gi in range(4):
                mixed = mixed + jnp.where(_group_mask(lane, gi), _dot(w_ref[gi], vcb, 1, 0), 0.0)
            zu = _gelu(upre)
            dsg_v = dsg_ref[rows, :]
            dc_ref[rows, :SGU_W] = (dsg_v * mixed * _gelu_grad(upre)).astype(BF16)
            dmixed = dsg_v * zu
            db_ref[...] += dmixed
            dvn = jnp.zeros((SGU_CHUNK, SGU_W), F32)
            for gi in range(4):
                dmg = jnp.where(_group_mask(lane, gi), dmixed, 0.0).astype(BF16)
                dw_ref[gi] += _dot(dmg, vcb, 1, 1)
                dvn = dvn + _dot(wt_ref[gi], dmg, 1, 0)
            dg_ref[...] += jnp.sum(dvn * zn, axis=0, keepdims=True)
            dzn = dvn * gv
            dzv = r * (dzn - zn * jnp.mean(dzn * zn, axis=-1, keepdims=True))
            dc_ref[rows, SGU_W:] = (dzv * _gelu_grad(vpre)).astype(BF16)

    blk = pl.BlockSpec((ts, SGU_W), lambda i: (i, 0))
    vec = pl.BlockSpec((1, SGU_W), lambda i: (0, 0))
    w3 = pl.BlockSpec((4, SGU_CHUNK, SGU_CHUNK), lambda i: (0, 0, 0))
    bsp = pl.BlockSpec((SGU_CHUNK, SGU_W), lambda i: (0, 0))
    return pl.pallas_call(
        body,
        name=name,
        grid=(S // ts,),
        in_specs=[
            pl.BlockSpec((ts, SGU_W), lambda i: (i, _U_COL)),
            pl.BlockSpec((ts, SGU_W), lambda i: (i, _U_COL + 1)),
            vec, w3, w3, bsp, blk,
        ],
        out_specs=[pl.BlockSpec((ts, 2 * SGU_W), lambda i: (i, 0)), w3, bsp, vec],
        out_shape=[
            jax.ShapeDtypeStruct((S, 2 * SGU_W), BF16),
            jax.ShapeDtypeStruct((4, SGU_CHUNK, SGU_CHUNK), F32),
            jax.ShapeDtypeStruct((SGU_CHUNK, SGU_W), F32),
            jax.ShapeDtypeStruct((1, SGU_W), F32),
        ],
        compiler_params=_params(("arbitrary",)),
    )(rest, rest, gn.reshape(1, SGU_W), wm, wm_t, bias, dsg)


_GT = 512
_G0 = OFF_G // _GT


def _gate_specs(tm, col_of):
    specs = [pl.BlockSpec((tm, _GT), functools.partial(lambda k, *ids: (col_of(*ids)[0], _G0 + 2 * k + col_of(*ids)[1]), k)) for k in range(3)]
    specs += [pl.BlockSpec((1, _GT), functools.partial(lambda k, *ids: (0, 2 * k + col_of(*ids)[1]), k)) for k in range(3)]
    return specs


def _merge_fwd(rest, bg, ya, yb, yc, *, name):
    S = rest.shape[0]
    tm = _tile(S, 512)

    def body(g1, g2, g3, b1, b2, b3, ya_ref, yb_ref, yc_ref, o_ref):
        acc = _sigmoid(g1[...] + b1[...]) * ya_ref[...]
        acc = acc + _sigmoid(g2[...] + b2[...]) * yb_ref[...]
        acc = acc + _sigmoid(g3[...] + b3[...]) * yc_ref[...]
        o_ref[...] = acc.astype(BF16)

    blk = pl.BlockSpec((tm, _GT), lambda i, j: (i, j))
    return pl.pallas_call(
        body,
        name=name,
        grid=(S // tm, D // _GT),
        in_specs=_gate_specs(tm, lambda i, j: (i, j)) + [blk, blk, blk],
        out_specs=blk,
        out_shape=jax.ShapeDtypeStruct((S, D), BF16),
        compiler_params=_params(("parallel", "parallel")),
    )(rest, rest, rest, bg, bg, bg, ya, yb, yc)


def _merge_bwd(rest, bg, ya, yb, yc, dm, *, name):
    S = rest.shape[0]
    tm = _tile(S, 512)

    def body(g1, g2, g3, b1, b2, b3, ya_ref, yb_ref, yc_ref, dm_ref, dya, dyb, dyc, dg1, dg2, dg3, db1, db2, db3):
        first = pl.program_id(1) == 0
        dmv = dm_ref[...]
        for g_ref, b_ref, y_ref, dy_ref, dg_ref, db_ref in (
            (g1, b1, ya_ref, dya, dg1, db1), (g2, b2, yb_ref, dyb, dg2, db2), (g3, b3, yc_ref, dyc, dg3, db3)):
            gate = _sigmoid(g_ref[...] + b_ref[...])
            dy_ref[...] = (dmv * gate).astype(BF16)
            dpre = dmv * y_ref[...] * gate * (1.0 - gate)
            dg_ref[...] = dpre.astype(BF16)
            part = jnp.sum(dpre, axis=0, keepdims=True)

            @pl.when(first)
            def _():
                db_ref[...] = part

            @pl.when(jnp.logical_not(first))
            def _():
                db_ref[...] += part

    blk = pl.BlockSpec((tm, _GT), lambda j, i: (i, j))
    vec = pl.BlockSpec((1, _GT), lambda j, i: (0, j))
    big = jax.ShapeDtypeStruct((S, D), BF16)
    small = jax.ShapeDtypeStruct((1, D), F32)
    return pl.pallas_call(
        body,
        name=name,
        grid=(D // _GT, S // tm),
        in_specs=_gate_specs(tm, lambda j, i: (i, j)) + [blk, blk, blk, blk],
        out_specs=[blk] * 6 + [vec] * 3,
        out_shape=[big] * 6 + [small] * 3,
        compiler_params=_params(("parallel", "arbitrary")),
    )(rest, rest, rest, bg, bg, bg, ya, yb, yc, dm)


_X_SCALE = XDH ** -0.5


def _xattn_fwd(xq, kv, *, name):
    S = xq.shape[0]
    M = kv.shape[0]
    tq = _tile(S, 512)

    def body(q_ref, k_ref, v_ref, o_ref):
        s = _dot(q_ref[...], k_ref[...], 1, 1) * _X_SCALE
        e = jnp.exp(s - jnp.max(s, axis=-1, keepdims=True))
        p = e / jnp.sum(e, axis=-1, keepdims=True)
        o_ref[...] = _dot(p.astype(BF16), v_ref[...], 1, 0).astype(BF16)

    return pl.pallas_call(
        body,
        name=name,
        grid=(S // tq, XH),
        in_specs=[
            pl.BlockSpec((tq, XDH), lambda i, h: (i, h)),
            pl.BlockSpec((M, XDH), lambda i, h: (0, h)),
            pl.BlockSpec((M, XDH), lambda i, h: (0, XH + h)),
        ],
        out_specs=pl.BlockSpec((tq, XDH), lambda i, h: (i, h)),
        out_shape=jax.ShapeDtypeStruct((S, D), BF16),
        compiler_params=_params(("parallel", "parallel")),
    )(xq, kv, kv)


def _xattn_bwd(xq, kv, do, *, name):
    S = xq.shape[0]
    M = kv.shape[0]
    tq = _tile(S, 512)

    def body(q_ref, k_ref, v_ref, do_ref, dq_ref, dk_ref, dv_ref):
        qb = q_ref[...]
        kb = k_ref[...]
        dob = do_ref[...]
        s = _dot(qb, kb, 1, 1) * _X_SCALE
        e = jnp.exp(s - jnp.max(s, axis=-1, keepdims=True))
        p = e / jnp.sum(e, axis=-1, keepdims=True)
        dp = _dot(dob, v_ref[...], 1, 1)
        ds = (p * (dp - jnp.sum(p * dp, axis=-1, keepdims=True)) * _X_SCALE).astype(BF16)
        dq_ref[...] = _dot(ds, kb, 1, 0).astype(BF16)
        dk_part = _dot(ds, qb, 0, 0)
        dv_part = _dot(p.astype(BF16), dob, 0, 0)

        @pl.when(pl.program_id(1) == 0)
        def _():
            dk_ref[...] = dk_part
            dv_ref[...] = dv_part

        @pl.when(pl.program_id(1) > 0)
        def _():
            dk_ref[...] += dk_part
            dv_ref[...] += dv_part

    qspec = pl.BlockSpec((tq, XDH), lambda h, i: (i, h))
    kspec = pl.BlockSpec((M, XDH), lambda h, i: (0, h))
    dxq, dxk, dxv = pl.pallas_call(
        body,
        name=name,
        grid=(XH, S // tq),
        in_specs=[qspec, kspec, pl.BlockSpec((M, XDH), lambda h, i: (0, XH + h)), qspec],
        out_specs=[qspec, kspec, kspec],
        out_shape=[jax.ShapeDtypeStruct((S, D), BF16), jax.ShapeDtypeStruct((M, D), F32), jax.ShapeDtypeStruct((M, D), F32)],
        compiler_params=_params(("parallel", "arbitrary")),
    )(xq, kv, kv, do)
    return dxq, jnp.concatenate([dxk, dxv], axis=1)


def _adam_math(w, g, m, v):
    m = ADAM_B1 * m + (1.0 - ADAM_B1) * g
    v = ADAM_B2 * v + (1.0 - ADAM_B2) * (g * g)
    m_hat = m / (1.0 - ADAM_B1 ** ADAM_STEP)
    v_hat = v / (1.0 - ADAM_B2 ** ADAM_STEP)
    delta = -ADAM_LR * (m_hat / (jnp.sqrt(v_hat) + ADAM_EPS) + ADAM_WD * w)
    return delta, m, v


def _adamw_sharded(parts, w, m, v, *, name):
    _, R, C = w.shape
    Cp = parts[0].shape[2]
    tm = _tile(R, 256)
    nr = R // tm

    def body(p0_ref, p1_ref, w_ref, m_ref, v_ref, g_ref, d_ref, mo_ref, vo_ref):
        def update(p_ref):
            g = p_ref[0][:, :C].astype(F32)
            for dev in range(1, N_DEV):
                g = g + p_ref[dev][:, :C].astype(F32)
            delta, mn, vn = _adam_math(w_ref[...], g, m_ref[...], v_ref[...])
            g_ref[...] = g
            d_ref[...] = delta
            mo_ref[...] = mn
            vo_ref[...] = vn

        @pl.when(pl.program_id(0) == 0)
        def _():
            update(p0_ref)

        @pl.when(pl.program_id(0) == 1)
        def _():
            update(p1_ref)

    p0 = pl.BlockSpec((N_DEV, tm, Cp), lambda l, i: (0, i * (1 - l) + (nr - 1) * l, 0))
    p1 = pl.BlockSpec((N_DEV, tm, Cp), lambda l, i: (0, i * l, 0))
    blk = pl.BlockSpec((None, tm, C), lambda l, i: (l, i, 0))
    sds = jax.ShapeDtypeStruct(w.shape, F32)
    return pl.pallas_call(
        body,
        name=name,
        grid=(DEPTH, nr),
        in_specs=[p0, p1, blk, blk, blk],
        out_specs=[blk] * 4,
        out_shape=[sds] * 4,
        compiler_params=_params(("arbitrary", "arbitrary")),
    )(parts[0], parts[1], w, m, v)


def _adamw_small(g, w, m, v, *, name):
    n = len(g)

    def body(*refs):
        g_refs, w_refs, m_refs, v_refs = (refs[k * n:(k + 1) * n] for k in range(4))
        d_out, m_out, v_out = (refs[(4 + k) * n:(5 + k) * n] for k in range(3))
        for t in range(n):
            delta, mn, vn = _adam_math(w_refs[t][...], g_refs[t][...], m_refs[t][...], v_refs[t][...])
            d_out[t][...] = delta
            m_out[t][...] = mn
            v_out[t][...] = vn

    vm = pl.BlockSpec(memory_space=pltpu.VMEM)
    shapes = [jax.ShapeDtypeStruct(a.shape, F32) for a in w]
    outs = pl.pallas_call(
        body,
        name=name,
        in_specs=[vm] * (4 * n),
        out_specs=[vm] * (3 * n),
        out_shape=shapes * 3,
        compiler_params=pltpu.CompilerParams(vmem_limit_bytes=VMEM_LIMIT),
    )(*g, *w, *m, *v)
    return outs[:n], outs[n:2 * n], outs[2 * n:]


def _position():
    return lax.axis_index("x"), lax.axis_index("y"), lax.axis_index("c")


def _dev_index(px, py, pc):
    return 4 * px + 2 * py + pc


_ANY = pl.BlockSpec(memory_space=pl.ANY)


def _all_gather(shards, *, name):
    n = len(shards)
    out_shape = [jax.ShapeDtypeStruct((N_DEV, *s.shape), s.dtype) for s in shards]
    n_pieces = len(_pieces(out_shape))

    def body(*refs):
        ins, outs = refs[:n], refs[n:2 * n]
        send_sems, recv_sems, local_sems = refs[2 * n:]
        x, y, c = _position()
        me, sibling = (x, y, c), (x, y, 1 - c)
        chips = [(1 - x, y), (x, 1 - y), (1 - x, 1 - y)]
        pieces = _pieces(outs)

        def copy(i, k, block, to, from_input=False):
            t, rows = pieces[i]
            dst = _cut(outs[t].at[_dev_index(*block)], rows)
            return pltpu.make_async_remote_copy(
                src_ref=_cut(ins[t], rows) if from_input else dst, dst_ref=dst, send_sem=send_sems.at[i, k],
                recv_sem=recv_sems.at[i, k], device_id=to, device_id_type=MESH)

        mine = [pltpu.make_async_copy(_cut(ins[t], rows), _cut(outs[t].at[_dev_index(*me)], rows), local_sems.at[i])
                for i, (t, rows) in enumerate(pieces)]
        for cp in mine:
            cp.start()
        started = []
        for j, chip in enumerate(chips):
            for i in range(n_pieces):
                started.append(copy(i, 1 + j, me, (*chip, c), from_input=True))
                started[-1].start()
        for i in range(n_pieces):
            started.append(copy(i, 0, me, sibling, from_input=True))
            started[-1].start()
        for j, chip in enumerate(chips):
            for i in range(n_pieces):
                copy(i, 1 + j, (*chip, c), me).wait_recv()
                started.append(copy(i, 4 + j, (*chip, c), sibling))
                started[-1].start()
        for i in range(n_pieces):
            copy(i, 0, sibling, me).wait_recv()
        for j, chip in enumerate(chips):
            for i in range(n_pieces):
                copy(i, 4 + j, (*chip, 1 - c), me).wait_recv()
        for cp in started:
            cp.wait_send()
        for cp in mine:
            cp.wait()

    return pl.pallas_call(
        body,
        name=name,
        in_specs=[_ANY] * n,
        out_specs=[_ANY] * n,
        out_shape=out_shape,
        scratch_shapes=[pltpu.SemaphoreType.DMA((n_pieces, 7)), pltpu.SemaphoreType.DMA((n_pieces, 7)),
                        pltpu.SemaphoreType.DMA((n_pieces,))],
        compiler_params=pltpu.CompilerParams(has_side_effects=True),
    )(*shards)


def _peers(x, y, c):
    out = []
    for mask in range(1, N_DEV):
        fx, fy, fc = (mask >> 2) & 1, (mask >> 1) & 1, mask & 1
        out.append((1 - x if fx else x, 1 - y if fy else y, 1 - c if fc else c))
    return out


_HBM = pl.BlockSpec(memory_space=pltpu.HBM)
_SEM = pl.BlockSpec(memory_space=pltpu.SEMAPHORE)


def _own_block_placed(block, like):
    x, y, c = _position()
    return lax.dynamic_update_index_in_dim(lax.empty(like.shape, like.dtype), block, _dev_index(x, y, c), 0)


_COPY_BYTES = 256 << 10
_MAX_PIECES = 8


def _pieces(blocks):
    out = []
    for t, b in enumerate(blocks):
        R, C = b.shape[-2:]
        n = max(1, min(_MAX_PIECES, R * C * jnp.dtype(b.dtype).itemsize // _COPY_BYTES))
        while R % (16 * n):
            n //= 2
        out += [(t, pl.ds(j * (R // n), R // n) if n > 1 else None) for j in range(n)]
    return out


def _cut(block, rows):
    return block if rows is None else block.at[rows]


def _copies(per_piece):
    def mark(fn):
        fn.per_piece = per_piece
        return fn
    return mark


@_copies(N_DEV - 1)
def _plan_exchange(srcs, lands, send_sems, recv_sems, arrivals):
    x, y, c = _position()
    me = _dev_index(x, y, c)
    out = []
    for k, peer in enumerate(_peers(x, y, c)):
        p = _dev_index(*peer)
        for i, (t, rows) in enumerate(_pieces(lands)):
            sems = dict(send_sem=send_sems.at[7 * i + k], recv_sem=recv_sems.at[7 * i + k], device_id=peer, device_id_type=MESH)
            src, dst = (lands[t].at[p], lands[t].at[p]) if arrivals else (srcs[t].at[p], lands[t].at[me])
            out.append(pltpu.make_async_remote_copy(src_ref=_cut(src, rows), dst_ref=_cut(dst, rows), **sems))
    return out


@_copies(4)
def _plan_gather_out(srcs, lands, send_sems, recv_sems, arrivals):
    x, y, c = _position()
    me = _dev_index(x, y, c)
    out = []
    for k, peer in enumerate([(x, y, 1 - c), (1 - x, y, c), (x, 1 - y, c), (1 - x, 1 - y, c)]):
        p = _dev_index(*peer)
        for i, (t, rows) in enumerate(_pieces(lands)):
            sems = dict(send_sem=send_sems.at[4 * i + k], recv_sem=recv_sems.at[4 * i + k], device_id=peer, device_id_type=MESH)
            src, dst = (lands[t].at[p], lands[t].at[p]) if arrivals else (srcs[t], lands[t].at[me])
            out.append(pltpu.make_async_remote_copy(src_ref=_cut(src, rows), dst_ref=_cut(dst, rows), **sems))
    return out


@_copies(3)
def _plan_gather_pass(srcs, lands, send_sems, recv_sems, arrivals):
    x, y, c = _position()
    sibling = (x, y, 1 - c)
    out = []
    for k, chip in enumerate([(1 - x, y), (x, 1 - y), (1 - x, 1 - y)]):
        p = _dev_index(*chip, 1 - c) if arrivals else _dev_index(*chip, c)
        for i, (t, rows) in enumerate(_pieces(lands)):
            sems = dict(send_sem=send_sems.at[3 * i + k], recv_sem=recv_sems.at[3 * i + k], device_id=sibling, device_id_type=MESH)
            block = _cut(lands[t].at[p], rows)
            out.append(pltpu.make_async_remote_copy(src_ref=block, dst_ref=block, **sems))
    return out


def _split_start(plan, srcs, lands, *, after=None, name):
    n_src, n = len(srcs), len(srcs) + len(lands)
    n_sem = plan.per_piece * len(_pieces(lands))
    order = [] if after is None else [after]

    def body(*refs):
        send_sems, recv_sems = refs[n + len(order):n + len(order) + 2]
        token = refs[-1]
        for cp in plan(refs[:n_src], refs[n_src:n], send_sems, recv_sems, arrivals=False):
            cp.start()
        token[...] = jnp.zeros_like(token)

    hbm = lambda a: pltpu.HBM(a.shape, a.dtype)
    outs = pl.pallas_call(
        body,
        name=name,
        in_specs=[_HBM] * n + [_ANY] * len(order),
        out_specs=[_SEM, _SEM] + [_HBM] * n + [pl.BlockSpec(memory_space=pltpu.VMEM)],
        out_shape=[pltpu.SemaphoreType.DMA((n_sem,)), pltpu.SemaphoreType.DMA((n_sem,))] + [hbm(a) for a in (*srcs, *lands)]
        + [jax.ShapeDtypeStruct(_TOKEN, F32)],
        input_output_aliases={i: 2 + i for i in range(n)},
        compiler_params=pltpu.CompilerParams(has_side_effects=pltpu.SideEffectType.DATAFLOW_SIDE_EFFECTING),
    )(*[pltpu.with_memory_space_constraint(a, pltpu.HBM) for a in (*srcs, *lands)], *order)
    return (outs[0], outs[1], outs[2:2 + n_src], outs[2 + n_src:2 + n]), outs[-1]


def _split_wait(plan, state, after, *, name):
    send_sems, recv_sems, srcs, lands = state
    n_src, n = len(srcs), len(srcs) + len(lands)

    def body(*refs):
        send_refs, recv_refs = refs[n:n + 2]
        for cp in plan(refs[:n_src], refs[n_src:n], send_refs, recv_refs, arrivals=False):
            cp.wait_send()
        for cp in plan(refs[:n_src], refs[n_src:n], send_refs, recv_refs, arrivals=True):
            cp.wait_recv()

    hbm = lambda a: pltpu.HBM(a.shape, a.dtype)
    outs = pl.pallas_call(
        body,
        name=name,
        in_specs=[_HBM] * n + [_SEM, _SEM, _ANY],
        out_specs=[_HBM] * n,
        out_shape=[hbm(a) for a in (*srcs, *lands)],
        input_output_aliases={i: i for i in range(n)},
        compiler_params=pltpu.CompilerParams(has_side_effects=pltpu.SideEffectType.DATAFLOW_SIDE_EFFECTING),
    )(*srcs, *lands, send_sems, recv_sems, after)
    return outs[n_src:]


def _all_reduce(g_local, after, *, name):
    R, C = g_local.shape
    nc = next(n for n in (4, 3, 2, 1) if R % (8 * n) == 0)
    chunks = [pl.ds(j * (R // nc), R // nc) for j in range(nc)]

    def body(g_ref, after_ref, o_ref, buf, send_sems, recv_sems):
        x, y, c = _position()
        me = _dev_index(x, y, c)
        peers = _peers(x, y, c)
        copies = []
        for k, peer in enumerate(peers):
            for j, rows in enumerate(chunks):
                copies.append(pltpu.make_async_remote_copy(
                    src_ref=g_ref.at[rows], dst_ref=buf.at[me, rows], send_sem=send_sems.at[nc * k + j],
                    recv_sem=recv_sems.at[nc * k + j], device_id=peer, device_id_type=MESH))
                copies[-1].start()
        buf[me] = g_ref[...]
        for k, peer in enumerate(peers):
            for j, rows in enumerate(chunks):
                dst = buf.at[_dev_index(*peer), rows]
                pltpu.make_async_remote_copy(
                    src_ref=dst, dst_ref=dst, send_sem=send_sems.at[nc * k + j], recv_sem=recv_sems.at[nc * k + j],
                    device_id=peer, device_id_type=MESH).wait_recv()
        for cp in copies:
            cp.wait_send()
        g = buf[0]
        for dev in range(1, N_DEV):
            g = g + buf[dev]
        o_ref[...] = g

    vm = pl.BlockSpec(memory_space=pltpu.VMEM)
    return pl.pallas_call(
        body,
        name=name,
        in_specs=[vm, _ANY],
        out_specs=vm,
        out_shape=jax.ShapeDtypeStruct((R, C), F32),
        scratch_shapes=[pltpu.VMEM((N_DEV, R, C), F32), pltpu.SemaphoreType.DMA((7 * nc,)), pltpu.SemaphoreType.DMA((7 * nc,))],
        compiler_params=pltpu.CompilerParams(has_side_effects=True, vmem_limit_bytes=VMEM_LIMIT),
    )(g_local, after)


def _block_diag(w):
    out = jnp.zeros((POOL_W, POOL_W), w.dtype)
    for gi in range(4):
        out = out.at[64 * gi:64 * (gi + 1), 64 * gi:64 * (gi + 1)].set(w[gi])
    return out


def _layer_consts(sp, l):
    causal = jnp.tril(jnp.ones((SGU_CHUNK, SGU_CHUNK), F32))
    wm = (sp["sgu_w"][l] * causal[None]).astype(BF16)
    wbd = _block_diag(sp["pool_w"][l]).astype(BF16)
    return dict(
        wbd=wbd, wbd_t=wbd.T, wm=wm, wm_t=wm.transpose(0, 2, 1),
        sgu_bias=jnp.repeat(sp["sgu_b"][l].T, 64, axis=1),
        bpad=jnp.pad(sp["b_forget"][l], (0, F_LANES - FOX_H)).reshape(1, F_LANES),
        bg=sp["b_gate"][l].reshape(1, 3 * D),
    )


def _relu2(acc):
    return acc, jnp.square(jnp.maximum(acc, 0.0))


def _relu2_grad(acc, z):
    return (acc * 2.0 * jnp.maximum(z, 0.0),)


def _layer_fwd(l, x, mem, source, sp):
    S = x.shape[0]
    t = _tile(S, 256)
    c = _layer_consts(sp, l)
    n = f"l{l}_"
    W, after = source(l, "begin", x)
    h = _rms_fwd(x, sp["norm_mix_g"][l], after=after, name=n + "norm_mix")
    qkv = _mm(h, W["qkv"], out_dtypes=(BF16,), name=n + "qkv")
    rest = _mm(h, W["rest"], name=n + "rest")
    pa = _pool_fwd(rest, c["wbd"], sp["pool_scale"][l], name=n + "pool")
    cum, cum_t = _fox_prep(rest, c["bpad"], name=n + "fox_prep")
    fk3 = cum_t[:FOX_H].reshape(FOX_H, S // t, t)
    o, lse = _fox_fwd(qkv, cum, fk3, name=n + "fox")
    more, _ = source(l, "attended", o)
    W.update(more)
    sg = _sgu_fwd(rest, sp["sgu_norm_g"][l], c["wm"], c["sgu_bias"], name=n + "sgu")
    more, after = source(l, "mixed", sg)
    W.update(more)
    ya = _mm(pa, W["ba"], after=after, name=n + "branch_a")
    yb = _mm(o, W["bb"], name=n + "branch_b")
    yc = _mm(sg, W["bc"], name=n + "branch_c")
    merged = _merge_fwd(rest, c["bg"], ya, yb, yc, name=n + "merge")
    x1 = _mm(merged, W["out"], extras=(x,), epilogue=_add, name=n + "out")
    hx = _rms_fwd(x1, sp["norm_xattn_g"][l], name=n + "norm_xattn")
    hm = _rms_fwd(mem, sp["norm_mem_g"][l], name=n + "norm_mem")
    xq = _mm(hx, W["xq"], out_dtypes=(BF16,), name=n + "xq")
    kv = _mm(hm, W["xkv"], out_dtypes=(BF16,), name=n + "xkv")
    o2 = _xattn_fwd(xq, kv, name=n + "xattn")
    x2 = _mm(o2, W["xo"], extras=(x1,), epilogue=_add, name=n + "xo")
    hf = _rms_fwd(x2, sp["norm_ffn_g"][l], name=n + "norm_ffn")
    z, act = _mm(hf, W["ff1"], epilogue=_relu2, out_dtypes=(F32, BF16), name=n + "ff1")
    _, after = source(l, "expanded", act)
    x3 = _mm(act, W["ff2"], extras=(x2,), epilogue=_add, after=after, name=n + "ff2")
    saved = dict(x=x, h=h, qkv=qkv, rest=rest, pa=pa, cum=cum, fk3=fk3, o=o, lse=lse, sg=sg, ya=ya, yb=yb, yc=yc,
                 merged=merged, x1=x1, hx=hx, hm=hm, xq=xq, kv=kv, o2=o2, x2=x2, hf=hf, z=z, act=act, c=c)
    return x3, saved, W


def _layer_bwd(l, dx3, sv, mem, W, sp, grads_done):
    S = dx3.shape[0]
    c = sv["c"]
    n = f"l{l}b_"
    bf = dict(out_dtypes=(BF16,))
    gw, gs = {}, {}
    gw["ff2"] = _mm(sv["act"], dx3, ta=True, name=n + "dw_ff2", **bf)
    dz = _mm(dx3, W["ff2"], tb=True, extras=(sv["z"],), epilogue=_relu2_grad, name=n + "dz", **bf)
    gw["ff1"] = _mm(sv["hf"], dz, ta=True, shard_out=True, name=n + "dw_ff1", **bf)
    dhf = _mm(dz, W["ff1"], tb=True, name=n + "dhf")
    dx2, gs["norm_ffn_g"] = _rms_bwd(sv["x2"], sp["norm_ffn_g"][l], dhf, dx3, name=n + "dnorm_ffn")
    gw["xo"] = _mm(sv["o2"], dx2, ta=True, name=n + "dw_xo", **bf)
    do2 = _mm(dx2, W["xo"], tb=True, name=n + "do2", **bf)
    dxq, dkv = _xattn_bwd(sv["xq"], sv["kv"], do2, name=n + "dxattn")
    gw["xq"] = _mm(sv["hx"], dxq, ta=True, name=n + "dw_xq", **bf)
    gw["xkv"] = _mm(sv["hm"], dkv, ta=True, shard_out=True, name=n + "dw_xkv", **bf)
    dhm = _mm(dkv, W["xkv"], tb=True, name=n + "dhm")
    _, gs["norm_mem_g"] = _rms_bwd(mem, sp["norm_mem_g"][l], dhm, jnp.zeros_like(mem), name=n + "dnorm_mem")
    dhx = _mm(dxq, W["xq"], tb=True, name=n + "dhx")
    dx1, gs["norm_xattn_g"] = _rms_bwd(sv["x1"], sp["norm_xattn_g"][l], dhx, dx2, name=n + "dnorm_xattn")
    after, gw = grads_done(l, gw), {}
    gw["out"] = _mm(sv["merged"], dx1, ta=True, name=n + "dw_out", **bf)
    dm = _mm(dx1, W["out"], tb=True, after=after, name=n + "dmerged")
    dya, dyb, dyc, dg1, dg2, dg3, db1, db2, db3 = _merge_bwd(sv["rest"], c["bg"], sv["ya"], sv["yb"], sv["yc"], dm, name=n + "dmerge")
    gs["b_gate"] = jnp.concatenate([db1, db2, db3], axis=1).reshape(3 * D)
    gw["ba"] = _mm(sv["pa"], dya, ta=True, shard_out=True, name=n + "dw_ba", **bf)
    gw["bb"] = _mm(sv["o"], dyb, ta=True, shard_out=True, name=n + "dw_bb", **bf)
    gw["bc"] = _mm(sv["sg"], dyc, ta=True, shard_out=True, name=n + "dw_bc", **bf)
    after, gw = grads_done(l, gw), {}
    dpa = _mm(dya, W["ba"], tb=True, name=n + "dpa")
    do = _mm(dyb, W["bb"], tb=True, after=after, name=n + "do", **bf)
    dsg = _mm(dyc, W["bc"], tb=True, name=n + "dsg")
    da, dwbd, dscale = _pool_bwd(sv["rest"], c["wbd"], c["wbd_t"], sp["pool_scale"][l], dpa, name=n + "dpool")
    gs["pool_w"] = jnp.stack([dwbd[64 * gi:64 * (gi + 1), 64 * gi:64 * (gi + 1)] for gi in range(4)])
    gs["pool_scale"] = dscale.reshape(POOL_W)
    dq, dk, dv, dfq, dfk = _fox_bwd(sv["qkv"], sv["cum"], sv["fk3"], sv["o"], do, sv["lse"], name=n + "dfox")
    dcum = dfq + jnp.pad(dfk.reshape(FOX_H, S).T, ((0, 0), (0, F_LANES - FOX_H)))
    df, dbf = _fox_post(sv["rest"], c["bpad"], dcum, name=n + "dfox_post")
    gs["b_forget"] = dbf[0, :FOX_H]
    dc, dwm, dbias, dgn = _sgu_bwd(sv["rest"], sp["sgu_norm_g"][l], c["wm"], c["wm_t"], c["sgu_bias"], dsg, name=n + "dsgu")
    gs["sgu_w"] = dwm * jnp.tril(jnp.ones((SGU_CHUNK, SGU_CHUNK), F32))[None]
    gs["sgu_b"] = dbias.reshape(SGU_CHUNK, 4, 64).sum(axis=2).T
    gs["sgu_norm_g"] = dgn.reshape(SGU_W)
    dqkv = jnp.concatenate([dq, dk, dv], axis=1)
    drest = jnp.concatenate([da, df, jnp.zeros((S, OFF_C - OFF_F - F_LANES), BF16), dc, dg1, dg2, dg3], axis=1)
    gw["qkv"] = _mm(sv["h"], dqkv, ta=True, name=n + "dw_qkv", **bf)
    gw["rest"] = _mm(sv["h"], drest, ta=True, name=n + "dw_rest", **bf)
    after = grads_done(l, gw)
    dh = _mm(dqkv, W["qkv"], tb=True, after=after, name=n + "dh_qkv")
    dh = _mm(drest, W["rest"], tb=True, extras=(dh,), epilogue=_add, name=n + "dh")
    dx, gs["norm_mix_g"] = _rms_bwd(sv["x"], sp["norm_mix_g"][l], dh, dx1, name=n + "dnorm_mix")
    return dx, gs


def _local_step(x, mem, target, sp, source, grads_done):
    saved, Ws = [], []
    for l in range(DEPTH):
        x, sv, W = _layer_fwd(l, x, mem, source, sp)
        saved.append(sv)
        Ws.append(W)
    loss, dx, dgf = _final_loss(x, sp["final_norm_g"], target, name="final_loss")
    gss = [None] * DEPTH
    for l in reversed(range(DEPTH)):
        dx, gss[l] = _layer_bwd(l, dx, saved[l], mem, Ws[l], sp, grads_done)
    small = {k: jnp.stack([gss[l][k] for l in range(DEPTH)]) for k in gss[0]}
    small["final_norm_g"] = dgf
    return loss, dx, small


_SMALL = ["norm_mix_g", "b_forget", "pool_w", "pool_scale", "sgu_norm_g", "sgu_w", "sgu_b", "b_gate", "norm_xattn_g",
          "norm_mem_g", "norm_ffn_g", "final_norm_g"]
_COL = {"w_branch_a": "ba", "w_branch_b": "bb", "w_branch_c": "bc", "w_xkv": "xkv", "w_ff1": "ff1"}
_ROW = {"w_out": "out", "w_xq": "xq", "w_xo": "xo", "w_ff2": "ff2"}
_BIG = ["w_in", "w_branch_a", "w_branch_b", "w_branch_c", "w_out", "w_xq", "w_xkv", "w_xo", "w_ff1", "w_ff2"]
_PACK_LANES = 128


def _as_rows(a):
    return a.reshape(-1, a.shape[-1])


def _pack(tensors):
    rows = []
    for a in tensors:
        flat = a.reshape(-1)
        flat = jnp.pad(flat, (0, (-flat.shape[0]) % (8 * _PACK_LANES)))
        rows.append(flat.reshape(-1, _PACK_LANES))
    return jnp.concatenate(rows, axis=0)


def _unpack(packed, like):
    out, r = [], 0
    for a in like:
        size = math.prod(a.shape)
        nr = 8 * (-(-size // (8 * _PACK_LANES)))
        out.append(packed[r:r + nr].reshape(-1)[:size].reshape(a.shape))
        r += nr
    return out


_SHARD_IN = N_IN // N_DEV
_SHARD_IN_PAD = -(-_SHARD_IN // 128) * 128


def _columns(pieces, start, stop):
    out, at = [], 0
    for p in pieces:
        lo, hi = max(start, at), min(stop, at + p.shape[1])
        if lo < hi:
            out.append(p[:, lo - at:hi - at])
        at += p.shape[1]
    return out


def _split_w_in(blocks):
    K = blocks[0].shape[0]
    pad = jnp.zeros((K, OFF_C - OFF_F - FOX_H), blocks[0].dtype)
    cols = functools.partial(_columns, blocks)
    rest = jnp.concatenate(cols(0, R_OFF_Q) + cols(R_OFF_F, R_OFF_C) + [pad] + cols(R_OFF_C, N_IN), axis=1)
    return jnp.concatenate(cols(R_OFF_Q, R_OFF_F), axis=1), rest


def _join_w_in(qkv, rest):
    in_order = [rest[:, :R_OFF_Q], qkv, rest[:, OFF_F:OFF_F + FOX_H], rest[:, OFF_C:]]
    pad = jnp.zeros((qkv.shape[0], _SHARD_IN_PAD - _SHARD_IN), qkv.dtype)
    return jnp.stack([jnp.concatenate(_columns(in_order, _SHARD_IN * d, _SHARD_IN * (d + 1)) + [pad], axis=1) for d in range(N_DEV)])


_FIRST = ["w_in"]
_LATER = [k for k in _BIG if k not in _FIRST]


def _layer_weights(gathered):
    W = {}
    if "w_in" in gathered:
        W.update(zip(("qkv", "rest"), _split_w_in([gathered["w_in"][d][:, :_SHARD_IN] for d in range(N_DEV)])))
    for name, key in _COL.items():
        if name in gathered:
            W[key] = _Gathered(gathered[name])
    for name, key in _ROW.items():
        if name in gathered:
            W[key] = gathered[name].reshape(-1, gathered[name].shape[-1])
    return W


def _grad_blocks(gw):
    parts = {}
    if "qkv" in gw:
        parts["w_in"] = _join_w_in(gw["qkv"], gw["rest"])
    for name, key in _COL.items():
        if key in gw:
            parts[name] = gw[key]
    for name, key in _ROW.items():
        if key in gw:
            parts[name] = gw[key].reshape(N_DEV, -1, gw[key].shape[-1])
    return parts


def kernel(x, mem, norm_mix_g, w_in, b_forget, pool_w, pool_scale, sgu_norm_g, sgu_w, sgu_b, w_branch_a, w_branch_b, w_branch_c, b_gate, w_out, norm_xattn_g, norm_mem_g, w_xq, w_xkv, w_xo, norm_ffn_g, w_ff1, w_ff2, final_norm_g, loss_target, m_norm_mix_g, m_w_in, m_b_forget, m_pool_w, m_pool_scale, m_sgu_norm_g, m_sgu_w, m_sgu_b, m_w_branch_a, m_w_branch_b, m_w_branch_c, m_b_gate, m_w_out, m_norm_xattn_g, m_norm_mem_g, m_w_xq, m_w_xkv, m_w_xo, m_norm_ffn_g, m_w_ff1, m_w_ff2, m_final_norm_g, v_norm_mix_g, v_w_in, v_b_forget, v_pool_w, v_pool_scale, v_sgu_norm_g, v_sgu_w, v_sgu_b, v_w_branch_a, v_w_branch_b, v_w_branch_c, v_b_gate, v_w_out, v_norm_xattn_g, v_norm_mem_g, v_w_xq, v_w_xkv, v_w_xo, v_norm_ffn_g, v_w_ff1, v_w_ff2, v_final_norm_g):
    names = ["norm_mix_g", "w_in", "b_forget", "pool_w", "pool_scale", "sgu_norm_g", "sgu_w", "sgu_b", "w_branch_a", "w_branch_b",
             "w_branch_c", "b_gate", "w_out", "norm_xattn_g", "norm_mem_g", "w_xq", "w_xkv", "w_xo", "norm_ffn_g", "w_ff1", "w_ff2",
             "final_norm_g"]
    w = dict(zip(names, [norm_mix_g, w_in, b_forget, pool_w, pool_scale, sgu_norm_g, sgu_w, sgu_b, w_branch_a, w_branch_b, w_branch_c,
                         b_gate, w_out, norm_xattn_g, norm_mem_g, w_xq, w_xkv, w_xo, norm_ffn_g, w_ff1, w_ff2, final_norm_g]))
    m = dict(zip(names, [m_norm_mix_g, m_w_in, m_b_forget, m_pool_w, m_pool_scale, m_sgu_norm_g, m_sgu_w, m_sgu_b, m_w_branch_a,
                         m_w_branch_b, m_w_branch_c, m_b_gate, m_w_out, m_norm_xattn_g, m_norm_mem_g, m_w_xq, m_w_xkv, m_w_xo,
                         m_norm_ffn_g, m_w_ff1, m_w_ff2, m_final_norm_g]))
    v = dict(zip(names, [v_norm_mix_g, v_w_in, v_b_forget, v_pool_w, v_pool_scale, v_sgu_norm_g, v_sgu_w, v_sgu_b, v_w_branch_a,
                         v_w_branch_b, v_w_branch_c, v_b_gate, v_w_out, v_norm_xattn_g, v_norm_mem_g, v_w_xq, v_w_xkv, v_w_xo,
                         v_norm_ffn_g, v_w_ff1, v_w_ff2, v_final_norm_g]))

    sp = {k: w[k] for k in _SMALL}
    shards = [{k: w[k][l].astype(BF16) for k in _BIG} for l in range(DEPTH)]
    for sh in shards:
        sh["w_in"] = jnp.pad(sh["w_in"], ((0, 0), (0, _SHARD_IN_PAD - _SHARD_IN)))
    me = _dev_index(*_position())

    def gather_out(l, keys, name, after=None):
        srcs = [shards[l][k] for k in keys]
        lands = [_own_block_placed(a, jax.ShapeDtypeStruct((N_DEV, *a.shape), a.dtype)) for a in srcs]
        state, token = _split_start(_plan_gather_out, srcs, lands, after=after, name=name + "_out_start")
        return (keys, name, state), token

    def gather_pass(job, value):
        keys, name, state = job
        lands = _split_wait(_plan_gather_out, state, value, name=name + "_out_wait")
        state, token = _split_start(_plan_gather_pass, [], lands, name=name + "_pass_start")
        return (keys, name, state), token, lands[0]

    def gather_end(job, value):
        keys, name, state = job
        return _layer_weights(dict(zip(keys, _split_wait(_plan_gather_pass, state, value, name=name + "_pass_wait"))))

    jobs = {}

    def source(l, point, value):
        if (l, point) == (0, "begin"):
            first = _all_gather([shards[0][k] for k in _FIRST], name="gather_l0_first")
            jobs["l0"], token = gather_out(0, _LATER, "gather_l0", after=first[0])
            return _layer_weights(dict(zip(_FIRST, first))), token
        if (l, point) == (0, "attended"):
            jobs["l0"], _, arrived = gather_pass(jobs["l0"], value)
            jobs["l1"], jobs["token"] = gather_out(1, _BIG, "gather_l1", after=arrived)
            return {}, None
        if (l, point) == (0, "mixed"):
            return gather_end(jobs.pop("l0"), value), jobs.pop("token")
        if (l, point) == (0, "expanded"):
            jobs["l1"], token, _ = gather_pass(jobs["l1"], value)
            return {}, token
        if (l, point) == (1, "begin"):
            return gather_end(jobs.pop("l1"), value), None
        return {}, None

    received = [{} for _ in range(DEPTH)]
    travelling = []

    def grads_done(l, gw):
        blocks = _grad_blocks(gw)
        keys = [k for k in _BIG if k in blocks]
        parts = [blocks[k] for k in keys]
        group = f"exchange_grads_l{l}_" + ("in" if "w_in" in blocks else "merge" if "w_out" in blocks else "mlp")
        lands = [_own_block_placed(lax.dynamic_index_in_dim(p, me, 0, keepdims=False), p) for p in parts]
        state, token = _split_start(_plan_exchange, parts, lands, name=group + "_start")
        travelling.append((l, keys, state, group + "_wait"))
        return token

    loss, dx, small = _local_step(x[0], mem[0], loss_target[0], sp, source, grads_done)
    loss = lax.psum(loss[0, 0], ("x", "y", "c"))
    grads, deltas, new_m, new_v = {}, {}, {}, {}

    def update_small(after):
        like = [w[k] for k in _SMALL]
        g_small = _unpack(_all_reduce(_pack([small[k] for k in _SMALL]), after, name="all_reduce_small"), like)
        rows = lambda d: [_as_rows(d[k]) for k in _SMALL]
        outs = _adamw_small([_as_rows(g) for g in g_small], rows(w), rows(m), rows(v), name="adamw_small")
        grads.update(zip(_SMALL, g_small))
        for dst, vals in zip((deltas, new_m, new_v), outs):
            dst.update({k: a.reshape(w[k].shape) for k, a in zip(_SMALL, vals)})
        return outs[0][0]

    done = dx
    groups = list(dict.fromkeys(tuple(keys) for _, keys, _, _ in travelling))
    for group_keys in groups:
        if group_keys == groups[-1]:
            done = update_small(done)
        for l, keys, state, wait_name in travelling:
            if tuple(keys) == group_keys:
                received[l].update(zip(keys, _split_wait(_plan_exchange, state, done, name=wait_name)))
        for k in group_keys:
            outs = _adamw_sharded([received[l][k] for l in range(DEPTH)], w[k], m[k], v[k], name="adamw_" + k)
            grads[k], deltas[k], new_m[k], new_v[k] = outs
        done = grads[group_keys[-1]]

    return (loss, dx[None], *[grads[k] for k in names], *[deltas[k] for k in names], *[new_m[k] for k in names],
            *[new_v[k] for k in names])
```

```python
import functools
import math

import jax
import jax.numpy as jnp
from jax import lax
from jax.experimental import pallas as pl
from jax.experimental.pallas import tpu as pltpu

F32 = jnp.float32
BF16 = jnp.bfloat16
MESH = pl.DeviceIdType.MESH

N_DEV = 8
D = 1024
DEPTH = 2
EPS = 1e-6
NEG = -1e30
POOL_W = 256
FOX_H = 8
FOX_DH = 64
FOX_W = 512
SGU_W = 256
SGU_CHUNK = 128
XH = 4
XDH = 256
N_IN = 5384
R_OFF_Q, R_OFF_F, R_OFF_C = 256, 1792, 1800
QKV_W = 3 * FOX_W
OFF_A, OFF_F, OFF_C, OFF_G, REST_W = 0, 256, 512, 1024, 4096
F_LANES = 128

ADAM_LR = 0.001
ADAM_B1 = 0.9
ADAM_B2 = 0.999
ADAM_EPS = 1e-08
ADAM_WD = 0.01
ADAM_STEP = 10

VMEM_LIMIT = 56 * 1024 * 1024


def _tile(n, pref):
    t = min(n, pref)
    while n % t:
        t -= 128
    assert t > 0, (n, pref)
    return t


def _params(sem=None):
    return pltpu.CompilerParams(dimension_semantics=sem, vmem_limit_bytes=VMEM_LIMIT)


def _dot(a, b, ca, cb):
    return lax.dot_general(a, b, (((ca,), (cb,)), ((), ())), preferred_element_type=F32)


def _sigmoid(z):
    return 1.0 / (1.0 + jnp.exp(-z))


_GELU_K = math.sqrt(2.0 / math.pi)
_GELU_C = 0.044715


def _gelu(x):
    return 0.5 * x * (1.0 + jnp.tanh(_GELU_K * (x + _GELU_C * x * x * x)))


def _gelu_grad(x):
    t = jnp.tanh(_GELU_K * (x + _GELU_C * x * x * x))
    return 0.5 * (1.0 + t) + 0.5 * x * (1.0 - t * t) * _GELU_K * (1.0 + 3.0 * _GELU_C * x * x)


def _rows(shape):
    return lax.broadcasted_iota(jnp.int32, shape, 0)


def _lanes(shape):
    return lax.broadcasted_iota(jnp.int32, shape, 1)


class _Gathered:
    def __init__(self, arr):
        self.arr = arr
        self.shape = (arr.shape[1], N_DEV * arr.shape[2])


_TOKEN = (8, 128)


def _mm(a, b, *, ta=False, tb=False, extras=(), epilogue=None, out_dtypes=(F32,), shard_out=False, after=None, tm=None, tn=512, tk=None,
        name):
    M, K = (a.shape[1], a.shape[0]) if ta else a.shape
    N, Kb = b.shape if tb else b.shape[::-1]
    assert Kb == K, (a.shape, b.shape, ta, tb)
    gathered = isinstance(b, _Gathered)
    if gathered:
        if tb:
            tk = b.arr.shape[2]
        else:
            tn = b.arr.shape[2]
    if shard_out:
        tn = N // N_DEV
    tm = _tile(M, tm or (1024 if ta else 2048))
    tn = _tile(N, tn)
    tk = _tile(K, tk or (2048 if ta else 1024))
    nk = K // tk
    ca, cb = (0 if ta else 1), (1 if tb else 0)
    n_ex, n_out = len(extras), len(out_dtypes)
    tokens = [] if after is None else [after]
    n_in = 2 + n_ex + len(tokens)
    if epilogue is None:
        epilogue = lambda acc: (acc,)

    def body(*refs):
        a_ref, b_ref = refs[:2]
        ex_refs = refs[2:2 + n_ex]
        o_refs = refs[n_in:n_in + n_out]
        part = _dot(a_ref[...].astype(BF16), b_ref[...].astype(BF16), ca, cb)

        def finish(acc):
            for o_ref, val in zip(o_refs, epilogue(acc, *[e[...] for e in ex_refs])):
                o_ref[...] = val.astype(o_ref.dtype)

        if nk == 1:
            finish(part)
        else:
            acc_ref = refs[-1]
            k = pl.program_id(2)

            @pl.when(k == 0)
            def _():
                acc_ref[...] = part

            @pl.when(k > 0)
            def _():
                acc_ref[...] += part

            @pl.when(k == nk - 1)
            def _():
                finish(acc_ref[...])

    a_spec = pl.BlockSpec((tk, tm), lambda i, j, k: (k, i)) if ta else pl.BlockSpec((tm, tk), lambda i, j, k: (i, k))
    if not gathered:
        b_arr = b
        b_spec = pl.BlockSpec((tn, tk), lambda i, j, k: (j, k)) if tb else pl.BlockSpec((tk, tn), lambda i, j, k: (k, j))
    else:
        b_arr = b.arr
        if tb:
            b_spec = pl.BlockSpec((None, tn, tk), lambda i, j, k: (k, j, 0))
        else:
            b_spec = pl.BlockSpec((None, tk, tn), lambda i, j, k: (j, k, 0))
    tile = pl.BlockSpec((tm, tn), lambda i, j, k: (i, j))
    if shard_out:
        out_specs = [pl.BlockSpec((None, tm, tn), lambda i, j, k: (j, i, 0))] * n_out
        out_shape = [jax.ShapeDtypeStruct((N_DEV, M, tn), dt) for dt in out_dtypes]
    else:
        out_specs = [tile] * n_out
        out_shape = [jax.ShapeDtypeStruct((M, N), dt) for dt in out_dtypes]
    size = lambda dt: jnp.dtype(dt).itemsize
    vmem = 2 * (tm * tk * size(a.dtype) + tk * tn * size(b_arr.dtype)
                + tm * tn * (sum(size(e.dtype) for e in extras) + sum(map(size, out_dtypes))))
    vmem += tm * tn * 4 * (nk > 1)
    assert vmem <= VMEM_LIMIT - (4 << 20), (name, vmem)
    outs = pl.pallas_call(
        body,
        name=name,
        grid=(M // tm, N // tn, nk),
        in_specs=[a_spec, b_spec] + [tile] * n_ex + [pl.BlockSpec(_TOKEN, lambda i, j, k: (0, 0))] * len(tokens),
        out_specs=out_specs,
        out_shape=out_shape,
        scratch_shapes=[pltpu.VMEM((tm, tn), F32)] if nk > 1 else [],
        compiler_params=_params(("parallel", "parallel", "arbitrary")),
    )(a, b_arr, *extras, *tokens)
    return outs[0] if n_out == 1 else outs


def _add(acc, res):
    return (acc + res,)


def _rms_fwd(x, g, *, after=None, name):
    R, C = x.shape
    tm = _tile(R, 256)
    tokens = [] if after is None else [after]

    def body(x_ref, g_ref, *rest):
        xv = x_ref[...]
        r = lax.rsqrt(jnp.mean(xv * xv, axis=-1, keepdims=True) + EPS)
        rest[-1][...] = (xv * r * g_ref[...]).astype(BF16)

    return pl.pallas_call(
        body,
        name=name,
        grid=(R // tm,),
        in_specs=[pl.BlockSpec((tm, C), lambda i: (i, 0)), pl.BlockSpec((1, C), lambda i: (0, 0))]
        + [pl.BlockSpec(_TOKEN, lambda i: (0, 0))] * len(tokens),
        out_specs=pl.BlockSpec((tm, C), lambda i: (i, 0)),
        out_shape=jax.ShapeDtypeStruct((R, C), BF16),
        compiler_params=_params(("parallel",)),
    )(x, g.reshape(1, C), *tokens)


def _rms_bwd(x, g, dh, dres, *, name):
    R, C = x.shape
    tm = _tile(R, 256)

    def body(x_ref, g_ref, dh_ref, dres_ref, dx_ref, dg_ref):
        xv = x_ref[...]
        r = lax.rsqrt(jnp.mean(xv * xv, axis=-1, keepdims=True) + EPS)
        xn = xv * r
        dh_v = dh_ref[...].astype(F32)
        dxn = dh_v * g_ref[...]
        dx_ref[...] = r * (dxn - xn * jnp.mean(dxn * xn, axis=-1, keepdims=True)) + dres_ref[...]
        part = jnp.sum(dh_v * xn, axis=0, keepdims=True)

        @pl.when(pl.program_id(0) == 0)
        def _():
            dg_ref[...] = part

        @pl.when(pl.program_id(0) > 0)
        def _():
            dg_ref[...] += part

    row = pl.BlockSpec((tm, C), lambda i: (i, 0))
    vec = pl.BlockSpec((1, C), lambda i: (0, 0))
    dx, dg = pl.pallas_call(
        body,
        name=name,
        grid=(R // tm,),
        in_specs=[row, vec, row, row],
        out_specs=[row, vec],
        out_shape=[jax.ShapeDtypeStruct((R, C), F32), jax.ShapeDtypeStruct((1, C), F32)],
        compiler_params=_params(("arbitrary",)),
    )(x, g.reshape(1, C), dh, dres)
    return dx, dg.reshape(C)


def _final_loss(x, g, target, *, name):
    R, C = x.shape
    tm = _tile(R, 256)

    def body(x_ref, g_ref, t_ref, loss_ref, dx_ref, dg_ref):
        xv = x_ref[...]
        r = lax.rsqrt(jnp.mean(xv * xv, axis=-1, keepdims=True) + EPS)
        xn = xv * r
        gv = g_ref[...]
        err = xn * gv - t_ref[...]
        lpart = (0.5 / C) * jnp.sum(jnp.sum(err * err, axis=1, keepdims=True), axis=0, keepdims=True)
        dy = err * (1.0 / C)
        dxn = dy * gv
        dx_ref[...] = r * (dxn - xn * jnp.mean(dxn * xn, axis=-1, keepdims=True))
        gpart = jnp.sum(dy * xn, axis=0, keepdims=True)

        @pl.when(pl.program_id(0) == 0)
        def _():
            loss_ref[...] = lpart
            dg_ref[...] = gpart

        @pl.when(pl.program_id(0) > 0)
        def _():
            loss_ref[...] += lpart
            dg_ref[...] += gpart

    row = pl.BlockSpec((tm, C), lambda i: (i, 0))
    vec = pl.BlockSpec((1, C), lambda i: (0, 0))
    loss, dx, dg = pl.pallas_call(
        body,
        name=name,
        grid=(R // tm,),
        in_specs=[row, vec, row],
        out_specs=[pl.BlockSpec((1, 1), lambda i: (0, 0)), row, vec],
        out_shape=[jax.ShapeDtypeStruct((1, 1), F32), jax.ShapeDtypeStruct((R, C), F32), jax.ShapeDtypeStruct((1, C), F32)],
        compiler_params=_params(("arbitrary",)),
    )(x, g.reshape(1, C), target)
    return loss, dx, dg.reshape(C)


def _pool_select(lane, vals):
    out = vals[3]
    for gi in (2, 1, 0):
        out = jnp.where(lane < 64 * (gi + 1), vals[gi], out)
    return out


def _pool_diff(a):
    row, lane = _rows(a.shape), _lanes(a.shape)

    def down(v, k):
        return jnp.where(row >= k, pltpu.roll(v, k, 0), 0.0)

    s2 = a + down(a, 1)
    s4 = s2 + down(s2, 2)
    s8 = s4 + down(s4, 4)
    s16 = s8 + down(s8, 8)
    wsum = _pool_select(lane, (s2, s4, s8, s16))
    win = _pool_select(lane, (2, 4, 8, 16))
    cnt = jnp.minimum(row + 1, win).astype(F32)
    return wsum / cnt - a, cnt


def _pool_diff_t(dd, cnt):
    S = dd.shape[0]
    row, lane = _rows(dd.shape), _lanes(dd.shape)

    def up(v, k):
        return jnp.where(row < S - k, pltpu.roll(v, S - k, 0), 0.0)

    e = dd / cnt
    s2 = e + up(e, 1)
    s4 = s2 + up(s2, 2)
    s8 = s4 + up(s4, 4)
    s16 = s8 + up(s8, 8)
    return _pool_select(lane, (s2, s4, s8, s16)) - dd


def _pool_fwd(rest, wbd, scale, *, name):
    S = rest.shape[0]

    def body(a_ref, w_ref, s_ref, o_ref):
        d, _ = _pool_diff(a_ref[...])
        yp = _dot(d.astype(BF16), w_ref[...], 1, 0)
        o_ref[...] = (yp * s_ref[...]).astype(BF16)

    return pl.pallas_call(
        body,
        name=name,
        grid=(1,),
        in_specs=[
            pl.BlockSpec((S, POOL_W), lambda i: (0, OFF_A // POOL_W)),
            pl.BlockSpec((POOL_W, POOL_W), lambda i: (0, 0)),
            pl.BlockSpec((1, POOL_W), lambda i: (0, 0)),
        ],
        out_specs=pl.BlockSpec((S, POOL_W), lambda i: (0, 0)),
        out_shape=jax.ShapeDtypeStruct((S, POOL_W), BF16),
        compiler_params=_params(("arbitrary",)),
    )(rest, wbd, scale.reshape(1, POOL_W))


def _pool_bwd(rest, wbd, wbd_t, scale, dpa, *, name):
    S = rest.shape[0]

    def body(a_ref, w_ref, wt_ref, s_ref, dpa_ref, da_ref, dw_ref, ds_ref):
        d, cnt = _pool_diff(a_ref[...])
        db = d.astype(BF16)
        yp = _dot(db, w_ref[...], 1, 0)
        dpa_v = dpa_ref[...]
        ds_ref[...] = jnp.sum(dpa_v * yp, axis=0, keepdims=True)
        dyp = (dpa_v * s_ref[...]).astype(BF16)
        dw_ref[...] = _dot(db, dyp, 0, 0)
        dd = _dot(dyp, wt_ref[...], 1, 0)
        da_ref[...] = _pool_diff_t(dd, cnt).astype(BF16)

    full = pl.BlockSpec((S, POOL_W), lambda i: (0, 0))
    sq = pl.BlockSpec((POOL_W, POOL_W), lambda i: (0, 0))
    vec = pl.BlockSpec((1, POOL_W), lambda i: (0, 0))
    return pl.pallas_call(
        body,
        name=name,
        grid=(1,),
        in_specs=[pl.BlockSpec((S, POOL_W), lambda i: (0, OFF_A // POOL_W)), sq, sq, vec, full],
        out_specs=[full, sq, vec],
        out_shape=[
            jax.ShapeDtypeStruct((S, POOL_W), BF16),
            jax.ShapeDtypeStruct((POOL_W, POOL_W), F32),
            jax.ShapeDtypeStruct((1, POOL_W), F32),
        ],
        compiler_params=_params(("arbitrary",)),
    )(rest, wbd, wbd_t, scale.reshape(1, POOL_W), dpa)


def _log_sigmoid(z):
    return jnp.minimum(z, 0.0) - jnp.log(1.0 + jnp.exp(-jnp.abs(z)))


_F_SPEC_COL = OFF_F // F_LANES


def _fox_prep(rest, bpad, *, name):
    S = rest.shape[0]

    def body(f_ref, b_ref, o_ref, ot_ref):
        acc = _log_sigmoid(f_ref[...] + b_ref[...])
        row = _rows(acc.shape)
        k = 1
        while k < S:
            acc = acc + jnp.where(row >= k, pltpu.roll(acc, k, 0), 0.0)
            k *= 2
        o_ref[...] = acc
        ot_ref[...] = acc.T

    return pl.pallas_call(
        body,
        name=name,
        grid=(1,),
        in_specs=[pl.BlockSpec((S, F_LANES), lambda i: (0, _F_SPEC_COL)), pl.BlockSpec((1, F_LANES), lambda i: (0, 0))],
        out_specs=[pl.BlockSpec((S, F_LANES), lambda i: (0, 0)), pl.BlockSpec((F_LANES, S), lambda i: (0, 0))],
        out_shape=[jax.ShapeDtypeStruct((S, F_LANES), F32), jax.ShapeDtypeStruct((F_LANES, S), F32)],
        compiler_params=_params(("arbitrary",)),
    )(rest, bpad)


def _fox_post(rest, bpad, dcum, *, name):
    S = rest.shape[0]

    def body(f_ref, b_ref, d_ref, df_ref, db_ref):
        acc = d_ref[...]
        row = _rows(acc.shape)
        k = 1
        while k < S:
            acc = acc + jnp.where(row < S - k, pltpu.roll(acc, S - k, 0), 0.0)
            k *= 2
        df = acc * (1.0 - _sigmoid(f_ref[...] + b_ref[...]))
        df_ref[...] = df.astype(BF16)
        db_ref[...] = jnp.sum(df, axis=0, keepdims=True)

    full = pl.BlockSpec((S, F_LANES), lambda i: (0, 0))
    vec = pl.BlockSpec((1, F_LANES), lambda i: (0, 0))
    return pl.pallas_call(
        body,
        name=name,
        grid=(1,),
        in_specs=[pl.BlockSpec((S, F_LANES), lambda i: (0, _F_SPEC_COL)), vec, full],
        out_specs=[full, vec],
        out_shape=[jax.ShapeDtypeStruct((S, F_LANES), BF16), jax.ShapeDtypeStruct((1, F_LANES), F32)],
        compiler_params=_params(("arbitrary",)),
    )(rest, bpad, dcum)


_FOX_SCALE = FOX_DH ** -0.5
_PAIRS = FOX_H // 2


def _scaled(v):
    return (v.astype(F32) * _FOX_SCALE).astype(BF16)


def _diag_mask(s):
    return jnp.where(_rows(s.shape) >= _lanes(s.shape), s, NEG)


def _fox_fwd(qkv, cum, fk3, *, name):
    S = qkv.shape[0]
    nk, t = fk3.shape[1:]

    def body(q_ref, k_ref, v_ref, cum_ref, fk_ref, o_ref, lse_ref):
        i = pl.program_id(0)
        lane = _lanes((t, 128))
        lo = lane < FOX_DH
        cumv = cum_ref[...]
        qm, fq = [], []
        for h in range(FOX_H):
            qs = _scaled(q_ref[:, 128 * (h // 2):128 * (h // 2 + 1)])
            zero = jnp.zeros_like(qs)
            qm.append(jnp.where(lo, qs, zero) if h % 2 == 0 else jnp.where(lo, zero, qs))
            fq.append(cumv[:, h:h + 1])

        def tile(j, state, masked):
            m, l, acc = (list(part) for part in state)
            k0 = pl.multiple_of(j * t, t)
            for hp in range(_PAIRS):
                cols = slice(128 * hp, 128 * (hp + 1))
                kb = k_ref[pl.ds(k0, t), cols]
                vb = v_ref[pl.ds(k0, t), cols]
                alphas, pvs = [], []
                for h in (2 * hp, 2 * hp + 1):
                    s = _dot(qm[h], kb, 1, 1) + fq[h] - fk_ref[h, pl.ds(j, 1), :]
                    if masked:
                        s = _diag_mask(s)
                    m_new = jnp.maximum(m[h], jnp.max(s, axis=-1, keepdims=True))
                    p = jnp.exp(s - m_new)
                    alpha = jnp.exp(m[h] - m_new)
                    l[h] = alpha * l[h] + jnp.sum(p, axis=-1, keepdims=True)
                    m[h] = m_new
                    alphas.append(alpha)
                    pvs.append(_dot(p.astype(BF16), vb, 1, 0))
                acc[hp] = jnp.where(lo, alphas[0], alphas[1]) * acc[hp] + jnp.where(lo, pvs[0], pvs[1])
            return tuple(m), tuple(l), tuple(acc)

        init = ((jnp.full((t, 1), NEG, F32),) * FOX_H, (jnp.zeros((t, 1), F32),) * FOX_H, (jnp.zeros((t, 128), F32),) * _PAIRS)
        state = lax.fori_loop(0, i, functools.partial(tile, masked=False), init)
        m, l, acc = tile(i, state, True)
        for hp in range(_PAIRS):
            o_ref[:, 128 * hp:128 * (hp + 1)] = acc[hp] / jnp.where(lo, l[2 * hp], l[2 * hp + 1])
            lse = [m[h] + jnp.log(l[h]) for h in (2 * hp, 2 * hp + 1)]
            lse_ref[hp] = jnp.where(lane == 0, lse[0], jnp.where(lane == 1, lse[1], 0.0))

    whole = lambda col: pl.BlockSpec((S, FOX_W), lambda i: (0, col))
    return pl.pallas_call(
        body,
        name=name,
        grid=(S // t,),
        in_specs=[
            pl.BlockSpec((t, FOX_W), lambda i: (i, 0)), whole(1), whole(2),
            pl.BlockSpec((t, F_LANES), lambda i: (i, 0)),
            pl.BlockSpec((FOX_H, nk, t), lambda i: (0, 0, 0)),
        ],
        out_specs=[pl.BlockSpec((t, FOX_W), lambda i: (i, 0)), pl.BlockSpec((_PAIRS, t, 128), lambda i: (0, i, 0))],
        out_shape=[jax.ShapeDtypeStruct((S, FOX_W), F32), jax.ShapeDtypeStruct((_PAIRS, S, 128), F32)],
        compiler_params=_params(("arbitrary",)),
    )(qkv, qkv, qkv, cum, fk3)


def _fox_bwd(qkv, cum, fk3, o, do, lse, *, name):
    S = qkv.shape[0]
    nk, t = fk3.shape[1:]
    q_at, k_at, v_at = 0, FOX_W, 2 * FOX_W

    def body(qkv_ref, cum_ref, fk_ref, o_ref, do_ref, lse_ref, dq_ref, dk_ref, dv_ref, dfq_ref, dfk_ref,
             qs_sc, ks_sc, delta_sc, dq_sc):
        lane = _lanes((t, 128))
        lo = lane < FOX_DH
        mine = lambda h: lo if h % 2 == 0 else jnp.logical_not(lo)

        def by_head(tile, values):
            for h, val in enumerate(values):
                tile = jnp.where(lane == h, val, tile)
            return tile

        def prep(i, carry):
            r = pl.ds(pl.multiple_of(i * t, t), t)
            qs_sc[r, :] = _scaled(qkv_ref[r, q_at:q_at + FOX_W])
            ks_sc[r, :] = _scaled(qkv_ref[r, k_at:k_at + FOX_W])
            sums = []
            for hp in range(_PAIRS):
                cols = slice(128 * hp, 128 * (hp + 1))
                prod = do_ref[r, cols].astype(F32) * o_ref[r, cols]
                sums += [jnp.sum(jnp.where(mine(h), prod, 0.0), axis=-1, keepdims=True) for h in (2 * hp, 2 * hp + 1)]
            delta_sc[r, :] = by_head(jnp.zeros((t, 128), F32), sums)
            dfq_ref[r, :] = jnp.zeros((t, 128), F32)
            dq_sc[r, :] = jnp.zeros((t, FOX_W), F32)
            return carry

        lax.fori_loop(0, nk, prep, 0)

        def kv_tile(j, carry):
            kr = pl.ds(pl.multiple_of(j * t, t), t)

            def q_tile(i, acc, masked):
                dk, dv, dfk = list(acc[:_PAIRS]), list(acc[_PAIRS:2 * _PAIRS]), list(acc[2 * _PAIRS:])
                qr = pl.ds(pl.multiple_of(i * t, t), t)
                delta_t, cum_t, dq_old, dfq_old = delta_sc[qr, :], cum_ref[qr, :], dq_sc[qr, :], dfq_ref[qr, :]
                row_sums, dq_new = [], []
                for hp in range(_PAIRS):
                    cols = slice(128 * hp, 128 * (hp + 1))
                    kb = qkv_ref[kr, k_at + 128 * hp:k_at + 128 * (hp + 1)]
                    vb = qkv_ref[kr, v_at + 128 * hp:v_at + 128 * (hp + 1)]
                    ksb, qsb, dob = ks_sc[kr, cols], qs_sc[qr, cols], do_ref[qr, cols]
                    zero = jnp.zeros_like(qsb)
                    dq_t = jnp.zeros((t, 128), F32)
                    for h in (2 * hp, 2 * hp + 1):
                        qe, doe, ke = (jnp.where(mine(h), a, zero) for a in (qsb, dob, ksb))
                        s = _dot(qe, kb, 1, 1) + cum_t[:, h:h + 1] - fk_ref[h, pl.ds(j, 1), :]
                        if masked:
                            s = _diag_mask(s)
                        p = jnp.exp(s - lse_ref[hp, qr, h % 2:h % 2 + 1])
                        dv[hp] = dv[hp] + _dot(p.astype(BF16), doe, 0, 0)
                        dp = _dot(doe, vb, 1, 1)
                        ds = p * (dp - delta_t[:, h:h + 1])
                        dsb = ds.astype(BF16)
                        dk[hp] = dk[hp] + _dot(dsb, qe, 0, 0)
                        dq_t = dq_t + _dot(dsb, ke, 1, 0)
                        row_sums.append(jnp.sum(ds, axis=-1, keepdims=True))
                        dfk[h] = dfk[h] - jnp.sum(ds, axis=0, keepdims=True)
                    dq_new.append(dq_old[:, cols] + dq_t)
                for hp in range(_PAIRS):
                    dq_sc[qr, 128 * hp:128 * (hp + 1)] = dq_new[hp]
                dfq_ref[qr, :] = dfq_old + by_head(jnp.zeros((t, 128), F32), row_sums)
                return (*dk, *dv, *dfk)

            init = tuple([jnp.zeros((t, 128), F32)] * (2 * _PAIRS) + [jnp.zeros((1, t), F32)] * FOX_H)
            acc = q_tile(j, init, True)
            acc = lax.fori_loop(j + 1, nk, functools.partial(q_tile, masked=False), acc)
            for hp in range(_PAIRS):
                cols = slice(128 * hp, 128 * (hp + 1))
                dk_ref[kr, cols] = acc[hp].astype(BF16)
                dv_ref[kr, cols] = acc[_PAIRS + hp].astype(BF16)
            for h in range(FOX_H):
                dfk_ref[h, pl.ds(j, 1), :] = acc[2 * _PAIRS + h]
            return carry

        lax.fori_loop(0, nk, kv_tile, 0)
        dq_ref[...] = dq_sc[...].astype(BF16)

    vm = pl.BlockSpec(memory_space=pltpu.VMEM)
    big = jax.ShapeDtypeStruct((S, FOX_W), BF16)
    return pl.pallas_call(
        body,
        name=name,
        in_specs=[vm] * 6,
        out_specs=[vm] * 5,
        out_shape=[big, big, big, jax.ShapeDtypeStruct((S, 128), F32), jax.ShapeDtypeStruct((FOX_H, nk, t), F32)],
        scratch_shapes=[pltpu.VMEM((S, FOX_W), BF16), pltpu.VMEM((S, FOX_W), BF16), pltpu.VMEM((S, 128), F32),
                        pltpu.VMEM((S, FOX_W), F32)],
        compiler_params=pltpu.CompilerParams(vmem_limit_bytes=VMEM_LIMIT),
    )(qkv, cum, fk3, o, do, lse)


def _group_mask(lane, gi):
    return (lane >= 64 * gi) & (lane < 64 * (gi + 1))


_U_COL = OFF_C // SGU_W


def _sgu_fwd(rest, gn, wm, bias, *, name):
    S = rest.shape[0]
    ts = _tile(S, 512)
    nc = ts // SGU_CHUNK

    def body(u_ref, v_ref, g_ref, w_ref, b_ref, o_ref):
        zv = _gelu(v_ref[...])
        vn = zv * lax.rsqrt(jnp.mean(zv * zv, axis=-1, keepdims=True) + EPS) * g_ref[...]
        lane = _lanes((SGU_CHUNK, SGU_W))
        for c in range(nc):
            rows = slice(c * SGU_CHUNK, (c + 1) * SGU_CHUNK)
            vcb = vn[rows].astype(BF16)
            mixed = b_ref[...]
            for gi in range(4):
                mixed = mixed + jnp.where(_group_mask(lane, gi), _dot(w_ref[gi], vcb, 1, 0), 0.0)
            o_ref[rows, :] = (_gelu(u_ref[rows, :]) * mixed).astype(BF16)

    return pl.pallas_call(
        body,
        name=name,
        grid=(S // ts,),
        in_specs=[
            pl.BlockSpec((ts, SGU_W), lambda i: (i, _U_COL)),
            pl.BlockSpec((ts, SGU_W), lambda i: (i, _U_COL + 1)),
            pl.BlockSpec((1, SGU_W), lambda i: (0, 0)),
            pl.BlockSpec((4, SGU_CHUNK, SGU_CHUNK), lambda i: (0, 0, 0)),
            pl.BlockSpec((SGU_CHUNK, SGU_W), lambda i: (0, 0)),
        ],
        out_specs=pl.BlockSpec((ts, SGU_W), lambda i: (i, 0)),
        out_shape=jax.ShapeDtypeStruct((S, SGU_W), BF16),
        compiler_params=_params(("parallel",)),
    )(rest, rest, gn.reshape(1, SGU_W), wm, bias)


def _sgu_bwd(rest, gn, wm, wm_t, bias, dsg, *, name):
    S = rest.shape[0]
    ts = _tile(S, 512)
    nc = ts // SGU_CHUNK

    def body(u_ref, v_ref, g_ref, w_ref, wt_ref, b_ref, dsg_ref, dc_ref, dw_ref, db_ref, dg_ref):
        first = pl.program_id(0) == 0

        @pl.when(first)
        def _():
            dw_ref[...] = jnp.zeros_like(dw_ref)
            db_ref[...] = jnp.zeros_like(db_ref)
            dg_ref[...] = jnp.zeros_like(dg_ref)

        gv = g_ref[...]
        lane = _lanes((SGU_CHUNK, SGU_W))
        for c in range(nc):
            rows = slice(c * SGU_CHUNK, (c + 1) * SGU_CHUNK)
            vpre = v_ref[rows, :]
            upre = u_ref[rows, :]
            zv = _gelu(vpre)
            r = lax.rsqrt(jnp.mean(zv * zv, axis=-1, keepdims=True) + EPS)
            zn = zv * r
            vcb = (zn * gv).astype(BF16)
            mixed = b_ref[...]
            for gi in range(4):
                mixed = mixed + jnp.where(_group_mask(lane, gi), _dot(w_ref[gi], vcb, 1, 0), 0.0)
            zu = _gelu(upre)
            dsg_v = dsg_ref[rows, :]
            dc_ref[rows, :SGU_W] = (dsg_v * mixed * _gelu_grad(upre)).astype(BF16)
            dmixed = dsg_v * zu
            db_ref[...] += dmixed
            dvn = jnp.zeros((SGU_CHUNK, SGU_W), F32)
            for gi in range(4):
                dmg = jnp.where(_group_mask(lane, gi), dmixed, 0.0).astype(BF16)
                dw_ref[gi] += _dot(dmg, vcb, 1, 1)
                dvn = dvn + _dot(wt_ref[gi], dmg, 1, 0)
            dg_ref[...] += jnp.sum(dvn * zn, axis=0, keepdims=True)
            dzn = dvn * gv
            dzv = r * (dzn - zn * jnp.mean(dzn * zn, axis=-1, keepdims=True))
            dc_ref[rows, SGU_W:] = (dzv * _gelu_grad(vpre)).astype(BF16)

    blk = pl.BlockSpec((ts, SGU_W), lambda i: (i, 0))
    vec = pl.BlockSpec((1, SGU_W), lambda i: (0, 0))
    w3 = pl.BlockSpec((4, SGU_CHUNK, SGU_CHUNK), lambda i: (0, 0, 0))
    bsp = pl.BlockSpec((SGU_CHUNK, SGU_W), lambda i: (0, 0))
    return pl.pallas_call(
        body,
        name=name,
        grid=(S // ts,),
        in_specs=[
            pl.BlockSpec((ts, SGU_W), lambda i: (i, _U_COL)),
            pl.BlockSpec((ts, SGU_W), lambda i: (i, _U_COL + 1)),
            vec, w3, w3, bsp, blk,
        ],
        out_specs=[pl.BlockSpec((ts, 2 * SGU_W), lambda i: (i, 0)), w3, bsp, vec],
        out_shape=[
            jax.ShapeDtypeStruct((S, 2 * SGU_W), BF16),
            jax.ShapeDtypeStruct((4, SGU_CHUNK, SGU_CHUNK), F32),
            jax.ShapeDtypeStruct((SGU_CHUNK, SGU_W), F32),
            jax.ShapeDtypeStruct((1, SGU_W), F32),
        ],
        compiler_params=_params(("arbitrary",)),
    )(rest, rest, gn.reshape(1, SGU_W), wm, wm_t, bias, dsg)


_GT = 512
_G0 = OFF_G // _GT


def _gate_specs(tm, col_of):
    specs = [pl.BlockSpec((tm, _GT), functools.partial(lambda k, *ids: (col_of(*ids)[0], _G0 + 2 * k + col_of(*ids)[1]), k)) for k in range(3)]
    specs += [pl.BlockSpec((1, _GT), functools.partial(lambda k, *ids: (0, 2 * k + col_of(*ids)[1]), k)) for k in range(3)]
    return specs


def _merge_fwd(rest, bg, ya, yb, yc, *, name):
    S = rest.shape[0]
    tm = _tile(S, 512)

    def body(g1, g2, g3, b1, b2, b3, ya_ref, yb_ref, yc_ref, o_ref):
        acc = _sigmoid(g1[...] + b1[...]) * ya_ref[...]
        acc = acc + _sigmoid(g2[...] + b2[...]) * yb_ref[...]
        acc = acc + _sigmoid(g3[...] + b3[...]) * yc_ref[...]
        o_ref[...] = acc.astype(BF16)

    blk = pl.BlockSpec((tm, _GT), lambda i, j: (i, j))
    return pl.pallas_call(
        body,
        name=name,
        grid=(S // tm, D // _GT),
        in_specs=_gate_specs(tm, lambda i, j: (i, j)) + [blk, blk, blk],
        out_specs=blk,
        out_shape=jax.ShapeDtypeStruct((S, D), BF16),
        compiler_params=_params(("parallel", "parallel")),
    )(rest, rest, rest, bg, bg, bg, ya, yb, yc)


def _merge_bwd(rest, bg, ya, yb, yc, dm, *, name):
    S = rest.shape[0]
    tm = _tile(S, 512)

    def body(g1, g2, g3, b1, b2, b3, ya_ref, yb_ref, yc_ref, dm_ref, dya, dyb, dyc, dg1, dg2, dg3, db1, db2, db3):
        first = pl.program_id(1) == 0
        dmv = dm_ref[...]
        for g_ref, b_ref, y_ref, dy_ref, dg_ref, db_ref in (
            (g1, b1, ya_ref, dya, dg1, db1), (g2, b2, yb_ref, dyb, dg2, db2), (g3, b3, yc_ref, dyc, dg3, db3)):
            gate = _sigmoid(g_ref[...] + b_ref[...])
            dy_ref[...] = (dmv * gate).astype(BF16)
            dpre = dmv * y_ref[...] * gate * (1.0 - gate)
            dg_ref[...] = dpre.astype(BF16)
            part = jnp.sum(dpre, axis=0, keepdims=True)

            @pl.when(first)
            def _():
                db_ref[...] = part

            @pl.when(jnp.logical_not(first))
            def _():
                db_ref[...] += part

    blk = pl.BlockSpec((tm, _GT), lambda j, i: (i, j))
    vec = pl.BlockSpec((1, _GT), lambda j, i: (0, j))
    big = jax.ShapeDtypeStruct((S, D), BF16)
    small = jax.ShapeDtypeStruct((1, D), F32)
    return pl.pallas_call(
        body,
        name=name,
        grid=(D // _GT, S // tm),
        in_specs=_gate_specs(tm, lambda j, i: (i, j)) + [blk, blk, blk, blk],
        out_specs=[blk] * 6 + [vec] * 3,
        out_shape=[big] * 6 + [small] * 3,
        compiler_params=_params(("parallel", "arbitrary")),
    )(rest, rest, rest, bg, bg, bg, ya, yb, yc, dm)


_X_SCALE = XDH ** -0.5


def _xattn_fwd(xq, kv, *, name):
    S = xq.shape[0]
    M = kv.shape[0]
    tq = _tile(S, 512)

    def body(q_ref, k_ref, v_ref, o_ref):
        s = _dot(q_ref[...], k_ref[...], 1, 1) * _X_SCALE
        e = jnp.exp(s - jnp.max(s, axis=-1, keepdims=True))
        p = e / jnp.sum(e, axis=-1, keepdims=True)
        o_ref[...] = _dot(p.astype(BF16), v_ref[...], 1, 0).astype(BF16)

    return pl.pallas_call(
        body,
        name=name,
        grid=(S // tq, XH),
        in_specs=[
            pl.BlockSpec((tq, XDH), lambda i, h: (i, h)),
            pl.BlockSpec((M, XDH), lambda i, h: (0, h)),
            pl.BlockSpec((M, XDH), lambda i, h: (0, XH + h)),
        ],
        out_specs=pl.BlockSpec((tq, XDH), lambda i, h: (i, h)),
        out_shape=jax.ShapeDtypeStruct((S, D), BF16),
        compiler_params=_params(("parallel", "parallel")),
    )(xq, kv, kv)


def _xattn_bwd(xq, kv, do, *, name):
    S = xq.shape[0]
    M = kv.shape[0]
    tq = _tile(S, 512)

    def body(q_ref, k_ref, v_ref, do_ref, dq_ref, dk_ref, dv_ref):
        qb = q_ref[...]
        kb = k_ref[...]
        dob = do_ref[...]
        s = _dot(qb, kb, 1, 1) * _X_SCALE
        e = jnp.exp(s - jnp.max(s, axis=-1, keepdims=True))
        p = e / jnp.sum(e, axis=-1, keepdims=True)
        dp = _dot(dob, v_ref[...], 1, 1)
        ds = (p * (dp - jnp.sum(p * dp, axis=-1, keepdims=True)) * _X_SCALE).astype(BF16)
        dq_ref[...] = _dot(ds, kb, 1, 0).astype(BF16)
        dk_part = _dot(ds, qb, 0, 0)
        dv_part = _dot(p.astype(BF16), dob, 0, 0)

        @pl.when(pl.program_id(1) == 0)
        def _():
            dk_ref[...] = dk_part
            dv_ref[...] = dv_part

        @pl.when(pl.program_id(1) > 0)
        def _():
            dk_ref[...] += dk_part
            dv_ref[...] += dv_part

    qspec = pl.BlockSpec((tq, XDH), lambda h, i: (i, h))
    kspec = pl.BlockSpec((M, XDH), lambda h, i: (0, h))
    dxq, dxk, dxv = pl.pallas_call(
        body,
        name=name,
        grid=(XH, S // tq),
        in_specs=[qspec, kspec, pl.BlockSpec((M, XDH), lambda h, i: (0, XH + h)), qspec],
        out_specs=[qspec, kspec, kspec],
        out_shape=[jax.ShapeDtypeStruct((S, D), BF16), jax.ShapeDtypeStruct((M, D), F32), jax.ShapeDtypeStruct((M, D), F32)],
        compiler_params=_params(("parallel", "arbitrary")),
    )(xq, kv, kv, do)
    return dxq, jnp.concatenate([dxk, dxv], axis=1)


def _adam_math(w, g, m, v):
    m = ADAM_B1 * m + (1.0 - ADAM_B1) * g
    v = ADAM_B2 * v + (1.0 - ADAM_B2) * (g * g)
    m_hat = m / (1.0 - ADAM_B1 ** ADAM_STEP)
    v_hat = v / (1.0 - ADAM_B2 ** ADAM_STEP)
    delta = -ADAM_LR * (m_hat / (jnp.sqrt(v_hat) + ADAM_EPS) + ADAM_WD * w)
    return delta, m, v


def _adamw_sharded(parts, w, m, v, *, name):
    _, R, C = w.shape
    Cp = parts[0].shape[2]
    tm = _tile(R, 256)
    nr = R // tm

    def body(p0_ref, p1_ref, w_ref, m_ref, v_ref, g_ref, d_ref, mo_ref, vo_ref):
        def update(p_ref):
            g = p_ref[0][:, :C].astype(F32)
            for dev in range(1, N_DEV):
                g = g + p_ref[dev][:, :C].astype(F32)
            delta, mn, vn = _adam_math(w_ref[...], g, m_ref[...], v_ref[...])
            g_ref[...] = g
            d_ref[...] = delta
            mo_ref[...] = mn
            vo_ref[...] = vn

        @pl.when(pl.program_id(0) == 0)
        def _():
            update(p0_ref)

        @pl.when(pl.program_id(0) == 1)
        def _():
            update(p1_ref)

    p0 = pl.BlockSpec((N_DEV, tm, Cp), lambda l, i: (0, i * (1 - l) + (nr - 1) * l, 0))
    p1 = pl.BlockSpec((N_DEV, tm, Cp), lambda l, i: (0, i * l, 0))
    blk = pl.BlockSpec((None, tm, C), lambda l, i: (l, i, 0))
    sds = jax.ShapeDtypeStruct(w.shape, F32)
    return pl.pallas_call(
        body,
        name=name,
        grid=(DEPTH, nr),
        in_specs=[p0, p1, blk, blk, blk],
        out_specs=[blk] * 4,
        out_shape=[sds] * 4,
        compiler_params=_params(("arbitrary", "arbitrary")),
    )(parts[0], parts[1], w, m, v)


def _adamw_small(g, w, m, v, *, name):
    n = len(g)

    def body(*refs):
        g_refs, w_refs, m_refs, v_refs = (refs[k * n:(k + 1) * n] for k in range(4))
        d_out, m_out, v_out = (refs[(4 + k) * n:(5 + k) * n] for k in range(3))
        for t in range(n):
            delta, mn, vn = _adam_math(w_refs[t][...], g_refs[t][...], m_refs[t][...], v_refs[t][...])
            d_out[t][...] = delta
            m_out[t][...] = mn
            v_out[t][...] = vn

    vm = pl.BlockSpec(memory_space=pltpu.VMEM)
    shapes = [jax.ShapeDtypeStruct(a.shape, F32) for a in w]
    outs = pl.pallas_call(
        body,
        name=name,
        in_specs=[vm] * (4 * n),
        out_specs=[vm] * (3 * n),
        out_shape=shapes * 3,
        compiler_params=pltpu.CompilerParams(vmem_limit_bytes=VMEM_LIMIT),
    )(*g, *w, *m, *v)
    return outs[:n], outs[n:2 * n], outs[2 * n:]


def _position():
    return lax.axis_index("x"), lax.axis_index("y"), lax.axis_index("c")


def _dev_index(px, py, pc):
    return 4 * px + 2 * py + pc


_ANY = pl.BlockSpec(memory_space=pl.ANY)


def _all_gather(shards, *, name):
    n = len(shards)
    out_shape = [jax.ShapeDtypeStruct((N_DEV, *s.shape), s.dtype) for s in shards]
    n_pieces = len(_pieces(out_shape))

    def body(*refs):
        ins, outs = refs[:n], refs[n:2 * n]
        send_sems, recv_sems, local_sems = refs[2 * n:]
        x, y, c = _position()
        me, sibling = (x, y, c), (x, y, 1 - c)
        chips = [(1 - x, y), (x, 1 - y), (1 - x, 1 - y)]
        pieces = _pieces(outs)

        def copy(i, k, block, to, from_input=False):
            t, rows = pieces[i]
            dst = _cut(outs[t].at[_dev_index(*block)], rows)
            return pltpu.make_async_remote_copy(
                src_ref=_cut(ins[t], rows) if from_input else dst, dst_ref=dst, send_sem=send_sems.at[i, k],
                recv_sem=recv_sems.at[i, k], device_id=to, device_id_type=MESH)

        mine = [pltpu.make_async_copy(_cut(ins[t], rows), _cut(outs[t].at[_dev_index(*me)], rows), local_sems.at[i])
                for i, (t, rows) in enumerate(pieces)]
        for cp in mine:
            cp.start()
        started = []
        for j, chip in enumerate(chips):
            for i in range(n_pieces):
                started.append(copy(i, 1 + j, me, (*chip, c), from_input=True))
                started[-1].start()
        for i in range(n_pieces):
            started.append(copy(i, 0, me, sibling, from_input=True))
            started[-1].start()
        for j, chip in enumerate(chips):
            for i in range(n_pieces):
                copy(i, 1 + j, (*chip, c), me).wait_recv()
                started.append(copy(i, 4 + j, (*chip, c), sibling))
                started[-1].start()
        for i in range(n_pieces):
            copy(i, 0, sibling, me).wait_recv()
        for j, chip in enumerate(chips):
            for i in range(n_pieces):
                copy(i, 4 + j, (*chip, 1 - c), me).wait_recv()
        for cp in started:
            cp.wait_send()
        for cp in mine:
            cp.wait()

    return pl.pallas_call(
        body,
        name=name,
        in_specs=[_ANY] * n,
        out_specs=[_ANY] * n,
        out_shape=out_shape,
        scratch_shapes=[pltpu.SemaphoreType.DMA((n_pieces, 7)), pltpu.SemaphoreType.DMA((n_pieces, 7)),
                        pltpu.SemaphoreType.DMA((n_pieces,))],
        compiler_params=pltpu.CompilerParams(has_side_effects=True),
    )(*shards)


def _peers(x, y, c):
    out = []
    for mask in range(1, N_DEV):
        fx, fy, fc = (mask >> 2) & 1, (mask >> 1) & 1, mask & 1
        out.append((1 - x if fx else x, 1 - y if fy else y, 1 - c if fc else c))
    return out


_HBM = pl.BlockSpec(memory_space=pltpu.HBM)
_SEM = pl.BlockSpec(memory_space=pltpu.SEMAPHORE)


def _own_block_placed(block, like):
    x, y, c = _position()
    return lax.dynamic_update_index_in_dim(lax.empty(like.shape, like.dtype), block, _dev_index(x, y, c), 0)


_COPY_BYTES = 256 << 10
_MAX_PIECES = 8


def _pieces(blocks):
    out = []
    for t, b in enumerate(blocks):
        R, C = b.shape[-2:]
        n = max(1, min(_MAX_PIECES, R * C * jnp.dtype(b.dtype).itemsize // _COPY_BYTES))
        while R % (16 * n):
            n //= 2
        out += [(t, pl.ds(j * (R // n), R // n) if n > 1 else None) for j in range(n)]
    return out


def _cut(block, rows):
    return block if rows is None else block.at[rows]


def _copies(per_piece):
    def mark(fn):
        fn.per_piece = per_piece
        return fn
    return mark


@_copies(N_DEV - 1)
def _plan_exchange(srcs, lands, send_sems, recv_sems, arrivals):
    x, y, c = _position()
    me = _dev_index(x, y, c)
    out = []
    for k, peer in enumerate(_peers(x, y, c)):
        p = _dev_index(*peer)
        for i, (t, rows) in enumerate(_pieces(lands)):
            sems = dict(send_sem=send_sems.at[7 * i + k], recv_sem=recv_sems.at[7 * i + k], device_id=peer, device_id_type=MESH)
            src, dst = (lands[t].at[p], lands[t].at[p]) if arrivals else (srcs[t].at[p], lands[t].at[me])
            out.append(pltpu.make_async_remote_copy(src_ref=_cut(src, rows), dst_ref=_cut(dst, rows), **sems))
    return out


@_copies(4)
def _plan_gather_out(srcs, lands, send_sems, recv_sems, arrivals):
    x, y, c = _position()
    me = _dev_index(x, y, c)
    out = []
    for k, peer in enumerate([(x, y, 1 - c), (1 - x, y, c), (x, 1 - y, c), (1 - x, 1 - y, c)]):
        p = _dev_index(*peer)
        for i, (t, rows) in enumerate(_pieces(lands)):
            sems = dict(send_sem=send_sems.at[4 * i + k], recv_sem=recv_sems.at[4 * i + k], device_id=peer, device_id_type=MESH)
            src, dst = (lands[t].at[p], lands[t].at[p]) if arrivals else (srcs[t], lands[t].at[me])
            out.append(pltpu.make_async_remote_copy(src_ref=_cut(src, rows), dst_ref=_cut(dst, rows), **sems))
    return out


@_copies(3)
def _plan_gather_pass(srcs, lands, send_sems, recv_sems, arrivals):
    x, y, c = _position()
    sibling = (x, y, 1 - c)
    out = []
    for k, chip in enumerate([(1 - x, y), (x, 1 - y), (1 - x, 1 - y)]):
        p = _dev_index(*chip, 1 - c) if arrivals else _dev_index(*chip, c)
        for i, (t, rows) in enumerate(_pieces(lands)):
            sems = dict(send_sem=send_sems.at[3 * i + k], recv_sem=recv_sems.at[3 * i + k], device_id=sibling, device_id_type=MESH)
            block = _cut(lands[t].at[p], rows)
            out.append(pltpu.make_async_remote_copy(src_ref=block, dst_ref=block, **sems))
    return out


def _split_start(plan, srcs, lands, *, after=None, name):
    n_src, n = len(srcs), len(srcs) + len(lands)
    n_sem = plan.per_piece * len(_pieces(lands))
    order = [] if after is None else [after]

    def body(*refs):
        send_sems, recv_sems = refs[n + len(order):n + len(order) + 2]
        token = refs[-1]
        for cp in plan(refs[:n_src], refs[n_src:n], send_sems, recv_sems, arrivals=False):
            cp.start()
        token[...] = jnp.zeros_like(token)

    hbm = lambda a: pltpu.HBM(a.shape, a.dtype)
    outs = pl.pallas_call(
        body,
        name=name,
        in_specs=[_HBM] * n + [_ANY] * len(order),
        out_specs=[_SEM, _SEM] + [_HBM] * n + [pl.BlockSpec(memory_space=pltpu.VMEM)],
        out_shape=[pltpu.SemaphoreType.DMA((n_sem,)), pltpu.SemaphoreType.DMA((n_sem,))] + [hbm(a) for a in (*srcs, *lands)]
        + [jax.ShapeDtypeStruct(_TOKEN, F32)],
        input_output_aliases={i: 2 + i for i in range(n)},
        compiler_params=pltpu.CompilerParams(has_side_effects=pltpu.SideEffectType.DATAFLOW_SIDE_EFFECTING),
    )(*[pltpu.with_memory_space_constraint(a, pltpu.HBM) for a in (*srcs, *lands)], *order)
    return (outs[0], outs[1], outs[2:2 + n_src], outs[2 + n_src:2 + n]), outs[-1]


def _split_wait(plan, state, after, *, name):
    send_sems, recv_sems, srcs, lands = state
    n_src, n = len(srcs), len(srcs) + len(lands)

    def body(*refs):
        send_refs, recv_refs = refs[n:n + 2]
        for cp in plan(refs[:n_src], refs[n_src:n], send_refs, recv_refs, arrivals=False):
            cp.wait_send()
        for cp in plan(refs[:n_src], refs[n_src:n], send_refs, recv_refs, arrivals=True):
            cp.wait_recv()

    hbm = lambda a: pltpu.HBM(a.shape, a.dtype)
    outs = pl.pallas_call(
        body,
        name=name,
        in_specs=[_HBM] * n + [_SEM, _SEM, _ANY],
        out_specs=[_HBM] * n,
        out_shape=[hbm(a) for a in (*srcs, *lands)],
        input_output_aliases={i: i for i in range(n)},
        compiler_params=pltpu.CompilerParams(has_side_effects=pltpu.SideEffectType.DATAFLOW_SIDE_EFFECTING),
    )(*srcs, *lands, send_sems, recv_sems, after)
    return outs[n_src:]


def _all_reduce(g_local, after, *, name):
    R, C = g_local.shape
    nc = next(n for n in (4, 3, 2, 1) if R % (8 * n) == 0)
    chunks = [pl.ds(j * (R // nc), R // nc) for j in range(nc)]

    def body(g_ref, after_ref, o_ref, buf, send_sems, recv_sems):
        x, y, c = _position()
        me = _dev_index(x, y, c)
        peers = _peers(x, y, c)
        copies = []
        for k, peer in enumerate(peers):
            for j, rows in enumerate(chunks):
                copies.append(pltpu.make_async_remote_copy(
                    src_ref=g_ref.at[rows], dst_ref=buf.at[me, rows], send_sem=send_sems.at[nc * k + j],
                    recv_sem=recv_sems.at[nc * k + j], device_id=peer, device_id_type=MESH))
                copies[-1].start()
        buf[me] = g_ref[...]
        for k, peer in enumerate(peers):
            for j, rows in enumerate(chunks):
                dst = buf.at[_dev_index(*peer), rows]
                pltpu.make_async_remote_copy(
                    src_ref=dst, dst_ref=dst, send_sem=send_sems.at[nc * k + j], recv_sem=recv_sems.at[nc * k + j],
                    device_id=peer, device_id_type=MESH).wait_recv()
        for cp in copies:
            cp.wait_send()
        g = buf[0]
        for dev in range(1, N_DEV):
            g = g + buf[dev]
        o_ref[...] = g

    vm = pl.BlockSpec(memory_space=pltpu.VMEM)
    return pl.pallas_call(
        body,
        name=name,
        in_specs=[vm, _ANY],
        out_specs=vm,
        out_shape=jax.ShapeDtypeStruct((R, C), F32),
        scratch_shapes=[pltpu.VMEM((N_DEV, R, C), F32), pltpu.SemaphoreType.DMA((7 * nc,)), pltpu.SemaphoreType.DMA((7 * nc,))],
        compiler_params=pltpu.CompilerParams(has_side_effects=True, vmem_limit_bytes=VMEM_LIMIT),
    )(g_local, after)


def _block_diag(w):
    out = jnp.zeros((POOL_W, POOL_W), w.dtype)
    for gi in range(4):
        out = out.at[64 * gi:64 * (gi + 1), 64 * gi:64 * (gi + 1)].set(w[gi])
    return out


def _layer_consts(sp, l):
    causal = jnp.tril(jnp.ones((SGU_CHUNK, SGU_CHUNK), F32))
    wm = (sp["sgu_w"][l] * causal[None]).astype(BF16)
    wbd = _block_diag(sp["pool_w"][l]).astype(BF16)
    return dict(
        wbd=wbd, wbd_t=wbd.T, wm=wm, wm_t=wm.transpose(0, 2, 1),
        sgu_bias=jnp.repeat(sp["sgu_b"][l].T, 64, axis=1),
        bpad=jnp.pad(sp["b_forget"][l], (0, F_LANES - FOX_H)).reshape(1, F_LANES),
        bg=sp["b_gate"][l].reshape(1, 3 * D),
    )


def _relu2(acc):
    return acc, jnp.square(jnp.maximum(acc, 0.0))


def _relu2_grad(acc, z):
    return (acc * 2.0 * jnp.maximum(z, 0.0),)


def _layer_fwd(l, x, mem, source, sp):
    S = x.shape[0]
    t = _tile(S, 256)
    c = _layer_consts(sp, l)
    n = f"l{l}_"
    W, after = source(l, "begin", x)
    h = _rms_fwd(x, sp["norm_mix_g"][l], after=after, name=n + "norm_mix")
    qkv = _mm(h, W["qkv"], out_dtypes=(BF16,), name=n + "qkv")
    rest = _mm(h, W["rest"], name=n + "rest")
    pa = _pool_fwd(rest, c["wbd"], sp["pool_scale"][l], name=n + "pool")
    cum, cum_t = _fox_prep(rest, c["bpad"], name=n + "fox_prep")
    fk3 = cum_t[:FOX_H].reshape(FOX_H, S // t, t)
    o, lse = _fox_fwd(qkv, cum, fk3, name=n + "fox")
    more, _ = source(l, "attended", o)
    W.update(more)
    sg = _sgu_fwd(rest, sp["sgu_norm_g"][l], c["wm"], c["sgu_bias"], name=n + "sgu")
    more, after = source(l, "mixed", sg)
    W.update(more)
    ya = _mm(pa, W["ba"], after=after, name=n + "branch_a")
    yb = _mm(o, W["bb"], name=n + "branch_b")
    yc = _mm(sg, W["bc"], name=n + "branch_c")
    merged = _merge_fwd(rest, c["bg"], ya, yb, yc, name=n + "merge")
    x1 = _mm(merged, W["out"], extras=(x,), epilogue=_add, name=n + "out")
    hx = _rms_fwd(x1, sp["norm_xattn_g"][l], name=n + "norm_xattn")
    hm = _rms_fwd(mem, sp["norm_mem_g"][l], name=n + "norm_mem")
    xq = _mm(hx, W["xq"], out_dtypes=(BF16,), name=n + "xq")
    kv = _mm(hm, W["xkv"], out_dtypes=(BF16,), name=n + "xkv")
    o2 = _xattn_fwd(xq, kv, name=n + "xattn")
    x2 = _mm(o2, W["xo"], extras=(x1,), epilogue=_add, name=n + "xo")
    hf = _rms_fwd(x2, sp["norm_ffn_g"][l], name=n + "norm_ffn")
    z, act = _mm(hf, W["ff1"], epilogue=_relu2, out_dtypes=(F32, BF16), name=n + "ff1")
    _, after = source(l, "expanded", act)
    x3 = _mm(act, W["ff2"], extras=(x2,), epilogue=_add, after=after, name=n + "ff2")
    saved = dict(x=x, h=h, qkv=qkv, rest=rest, pa=pa, cum=cum, fk3=fk3, o=o, lse=lse, sg=sg, ya=ya, yb=yb, yc=yc,
                 merged=merged, x1=x1, hx=hx, hm=hm, xq=xq, kv=kv, o2=o2, x2=x2, hf=hf, z=z, act=act, c=c)
    return x3, saved, W


def _layer_bwd(l, dx3, sv, mem, W, sp, grads_done):
    S = dx3.shape[0]
    c = sv["c"]
    n = f"l{l}b_"
    bf = dict(out_dtypes=(BF16,))
    gw, gs = {}, {}
    gw["ff2"] = _mm(sv["act"], dx3, ta=True, name=n + "dw_ff2", **bf)
    dz = _mm(dx3, W["ff2"], tb=True, extras=(sv["z"],), epilogue=_relu2_grad, name=n + "dz", **bf)
    gw["ff1"] = _mm(sv["hf"], dz, ta=True, shard_out=True, name=n + "dw_ff1", **bf)
    dhf = _mm(dz, W["ff1"], tb=True, name=n + "dhf")
    dx2, gs["norm_ffn_g"] = _rms_bwd(sv["x2"], sp["norm_ffn_g"][l], dhf, dx3, name=n + "dnorm_ffn")
    gw["xo"] = _mm(sv["o2"], dx2, ta=True, name=n + "dw_xo", **bf)
    do2 = _mm(dx2, W["xo"], tb=True, name=n + "do2", **bf)
    dxq, dkv = _xattn_bwd(sv["xq"], sv["kv"], do2, name=n + "dxattn")
    gw["xq"] = _mm(sv["hx"], dxq, ta=True, name=n + "dw_xq", **bf)
    gw["xkv"] = _mm(sv["hm"], dkv, ta=True, shard_out=True, name=n + "dw_xkv", **bf)
    dhm = _mm(dkv, W["xkv"], tb=True, name=n + "dhm")
    _, gs["norm_mem_g"] = _rms_bwd(mem, sp["norm_mem_g"][l], dhm, jnp.zeros_like(mem), name=n + "dnorm_mem")
    dhx = _mm(dxq, W["xq"], tb=True, name=n + "dhx")
    dx1, gs["norm_xattn_g"] = _rms_bwd(sv["x1"], sp["norm_xattn_g"][l], dhx, dx2, name=n + "dnorm_xattn")
    after, gw = grads_done(l, gw), {}
    gw["out"] = _mm(sv["merged"], dx1, ta=True, name=n + "dw_out", **bf)
    dm = _mm(dx1, W["out"], tb=True, after=after, name=n + "dmerged")
    dya, dyb, dyc, dg1, dg2, dg3, db1, db2, db3 = _merge_bwd(sv["rest"], c["bg"], sv["ya"], sv["yb"], sv["yc"], dm, name=n + "dmerge")
    gs["b_gate"] = jnp.concatenate([db1, db2, db3], axis=1).reshape(3 * D)
    gw["ba"] = _mm(sv["pa"], dya, ta=True, shard_out=True, name=n + "dw_ba", **bf)
    gw["bb"] = _mm(sv["o"], dyb, ta=True, shard_out=True, name=n + "dw_bb", **bf)
    gw["bc"] = _mm(sv["sg"], dyc, ta=True, shard_out=True, name=n + "dw_bc", **bf)
    after, gw = grads_done(l, gw), {}
    dpa = _mm(dya, W["ba"], tb=True, name=n + "dpa")
    do = _mm(dyb, W["bb"], tb=True, after=after, name=n + "do", **bf)
    dsg = _mm(dyc, W["bc"], tb=True, name=n + "dsg")
    da, dwbd, dscale = _pool_bwd(sv["rest"], c["wbd"], c["wbd_t"], sp["pool_scale"][l], dpa, name=n + "dpool")
    gs["pool_w"] = jnp.stack([dwbd[64 * gi:64 * (gi + 1), 64 * gi:64 * (gi + 1)] for gi in range(4)])
    gs["pool_scale"] = dscale.reshape(POOL_W)
    dq, dk, dv, dfq, dfk = _fox_bwd(sv["qkv"], sv["cum"], sv["fk3"], sv["o"], do, sv["lse"], name=n + "dfox")
    dcum = dfq + jnp.pad(dfk.reshape(FOX_H, S).T, ((0, 0), (0, F_LANES - FOX_H)))
    df, dbf = _fox_post(sv["rest"], c["bpad"], dcum, name=n + "dfox_post")
    gs["b_forget"] = dbf[0, :FOX_H]
    dc, dwm, dbias, dgn = _sgu_bwd(sv["rest"], sp["sgu_norm_g"][l], c["wm"], c["wm_t"], c["sgu_bias"], dsg, name=n + "dsgu")
    gs["sgu_w"] = dwm * jnp.tril(jnp.ones((SGU_CHUNK, SGU_CHUNK), F32))[None]
    gs["sgu_b"] = dbias.reshape(SGU_CHUNK, 4, 64).sum(axis=2).T
    gs["sgu_norm_g"] = dgn.reshape(SGU_W)
    dqkv = jnp.concatenate([dq, dk, dv], axis=1)
    drest = jnp.concatenate([da, df, jnp.zeros((S, OFF_C - OFF_F - F_LANES), BF16), dc, dg1, dg2, dg3], axis=1)
    gw["qkv"] = _mm(sv["h"], dqkv, ta=True, name=n + "dw_qkv", **bf)
    gw["rest"] = _mm(sv["h"], drest, ta=True, name=n + "dw_rest", **bf)
    after = grads_done(l, gw)
    dh = _mm(dqkv, W["qkv"], tb=True, after=after, name=n + "dh_qkv")
    dh = _mm(drest, W["rest"], tb=True, extras=(dh,), epilogue=_add, name=n + "dh")
    dx, gs["norm_mix_g"] = _rms_bwd(sv["x"], sp["norm_mix_g"][l], dh, dx1, name=n + "dnorm_mix")
    return dx, gs


def _local_step(x, mem, target, sp, source, grads_done):
    saved, Ws = [], []
    for l in range(DEPTH):
        x, sv, W = _layer_fwd(l, x, mem, source, sp)
        saved.append(sv)
        Ws.append(W)
    loss, dx, dgf = _final_loss(x, sp["final_norm_g"], target, name="final_loss")
    gss = [None] * DEPTH
    for l in reversed(range(DEPTH)):
        dx, gss[l] = _layer_bwd(l, dx, saved[l], mem, Ws[l], sp, grads_done)
    small = {k: jnp.stack([gss[l][k] for l in range(DEPTH)]) for k in gss[0]}
    small["final_norm_g"] = dgf
    return loss, dx, small


_SMALL = ["norm_mix_g", "b_forget", "pool_w", "pool_scale", "sgu_norm_g", "sgu_w", "sgu_b", "b_gate", "norm_xattn_g",
          "norm_mem_g", "norm_ffn_g", "final_norm_g"]
_COL = {"w_branch_a": "ba", "w_branch_b": "bb", "w_branch_c": "bc", "w_xkv": "xkv", "w_ff1": "ff1"}
_ROW = {"w_out": "out", "w_xq": "xq", "w_xo": "xo", "w_ff2": "ff2"}
_BIG = ["w_in", "w_branch_a", "w_branch_b", "w_branch_c", "w_out", "w_xq", "w_xkv", "w_xo", "w_ff1", "w_ff2"]
_PACK_LANES = 128


def _as_rows(a):
    return a.reshape(-1, a.shape[-1])


def _pack(tensors):
    rows = []
    for a in tensors:
        flat = a.reshape(-1)
        flat = jnp.pad(flat, (0, (-flat.shape[0]) % (8 * _PACK_LANES)))
        rows.append(flat.reshape(-1, _PACK_LANES))
    return jnp.concatenate(rows, axis=0)


def _unpack(packed, like):
    out, r = [], 0
    for a in like:
        size = math.prod(a.shape)
        nr = 8 * (-(-size // (8 * _PACK_LANES)))
        out.append(packed[r:r + nr].reshape(-1)[:size].reshape(a.shape))
        r += nr
    return out


_SHARD_IN = N_IN // N_DEV
_SHARD_IN_PAD = -(-_SHARD_IN // 128) * 128


def _columns(pieces, start, stop):
    out, at = [], 0
    for p in pieces:
        lo, hi = max(start, at), min(stop, at + p.shape[1])
        if lo < hi:
            out.append(p[:, lo - at:hi - at])
        at += p.shape[1]
    return out


def _split_w_in(blocks):
    K = blocks[0].shape[0]
    pad = jnp.zeros((K, OFF_C - OFF_F - FOX_H), blocks[0].dtype)
    cols = functools.partial(_columns, blocks)
    rest = jnp.concatenate(cols(0, R_OFF_Q) + cols(R_OFF_F, R_OFF_C) + [pad] + cols(R_OFF_C, N_IN), axis=1)
    return jnp.concatenate(cols(R_OFF_Q, R_OFF_F), axis=1), rest


def _join_w_in(qkv, rest):
    in_order = [rest[:, :R_OFF_Q], qkv, rest[:, OFF_F:OFF_F + FOX_H], rest[:, OFF_C:]]
    pad = jnp.zeros((qkv.shape[0], _SHARD_IN_PAD - _SHARD_IN), qkv.dtype)
    return jnp.stack([jnp.concatenate(_columns(in_order, _SHARD_IN * d, _SHARD_IN * (d + 1)) + [pad], axis=1) for d in range(N_DEV)])


_FIRST = ["w_in"]
_LATER = [k for k in _BIG if k not in _FIRST]


def _layer_weights(gathered):
    W = {}
    if "w_in" in gathered:
        W.update(zip(("qkv", "rest"), _split_w_in([gathered["w_in"][d][:, :_SHARD_IN] for d in range(N_DEV)])))
    for name, key in _COL.items():
        if name in gathered:
            W[key] = _Gathered(gathered[name])
    for name, key in _ROW.items():
        if name in gathered:
            W[key] = gathered[name].reshape(-1, gathered[name].shape[-1])
    return W


def _grad_blocks(gw):
    parts = {}
    if "qkv" in gw:
        parts["w_in"] = _join_w_in(gw["qkv"], gw["rest"])
    for name, key in _COL.items():
        if key in gw:
            parts[name] = gw[key]
    for name, key in _ROW.items():
        if key in gw:
            parts[name] = gw[key].reshape(N_DEV, -1, gw[key].shape[-1])
    return parts


def kernel(x, mem, norm_mix_g, w_in, b_forget, pool_w, pool_scale, sgu_norm_g, sgu_w, sgu_b, w_branch_a, w_branch_b, w_branch_c, b_gate, w_out, norm_xattn_g, norm_mem_g, w_xq, w_xkv, w_xo, norm_ffn_g, w_ff1, w_ff2, final_norm_g, loss_target, m_norm_mix_g, m_w_in, m_b_forget, m_pool_w, m_pool_scale, m_sgu_norm_g, m_sgu_w, m_sgu_b, m_w_branch_a, m_w_branch_b, m_w_branch_c, m_b_gate, m_w_out, m_norm_xattn_g, m_norm_mem_g, m_w_xq, m_w_xkv, m_w_xo, m_norm_ffn_g, m_w_ff1, m_w_ff2, m_final_norm_g, v_norm_mix_g, v_w_in, v_b_forget, v_pool_w, v_pool_scale, v_sgu_norm_g, v_sgu_w, v_sgu_b, v_w_branch_a, v_w_branch_b, v_w_branch_c, v_b_gate, v_w_out, v_norm_xattn_g, v_norm_mem_g, v_w_xq, v_w_xkv, v_w_xo, v_norm_ffn_g, v_w_ff1, v_w_ff2, v_final_norm_g):
    names = ["norm_mix_g", "w_in", "b_forget", "pool_w", "pool_scale", "sgu_norm_g", "sgu_w", "sgu_b", "w_branch_a", "w_branch_b",
             "w_branch_c", "b_gate", "w_out", "norm_xattn_g", "norm_mem_g", "w_xq", "w_xkv", "w_xo", "norm_ffn_g", "w_ff1", "w_ff2",
             "final_norm_g"]
    w = dict(zip(names, [norm_mix_g, w_in, b_forget, pool_w, pool_scale, sgu_norm_g, sgu_w, sgu_b, w_branch_a, w_branch_b, w_branch_c,
                         b_gate, w_out, norm_xattn_g, norm_mem_g, w_xq, w_xkv, w_xo, norm_ffn_g, w_ff1, w_ff2, final_norm_g]))
    m = dict(zip(names, [m_norm_mix_g, m_w_in, m_b_forget, m_pool_w, m_pool_scale, m_sgu_norm_g, m_sgu_w, m_sgu_b, m_w_branch_a,
                         m_w_branch_b, m_w_branch_c, m_b_gate, m_w_out, m_norm_xattn_g, m_norm_mem_g, m_w_xq, m_w_xkv, m_w_xo,
                         m_norm_ffn_g, m_w_ff1, m_w_ff2, m_final_norm_g]))
    v = dict(zip(names, [v_norm_mix_g, v_w_in, v_b_forget, v_pool_w, v_pool_scale, v_sgu_norm_g, v_sgu_w, v_sgu_b, v_w_branch_a,
                         v_w_branch_b, v_w_branch_c, v_b_gate, v_w_out, v_norm_xattn_g, v_norm_mem_g, v_w_xq, v_w_xkv, v_w_xo,
                         v_norm_ffn_g, v_w_ff1, v_w_ff2, v_final_norm_g]))

    sp = {k: w[k] for k in _SMALL}
    shards = [{k: w[k][l].astype(BF16) for k in _BIG} for l in range(DEPTH)]
    for sh in shards:
        sh["w_in"] = jnp.pad(sh["w_in"], ((0, 0), (0, _SHARD_IN_PAD - _SHARD_IN)))
    me = _dev_index(*_position())

    def gather_out(l, keys, name, after=None):
        srcs = [shards[l][k] for k in keys]
        lands = [_own_block_placed(a, jax.ShapeDtypeStruct((N_DEV, *a.shape), a.dtype)) for a in srcs]
        state, token = _split_start(_plan_gather_out, srcs, lands, after=after, name=name + "_out_start")
        return (keys, name, state), token

    def gather_pass(job, value):
        keys, name, state = job
        lands = _split_wait(_plan_gather_out, state, value, name=name + "_out_wait")
        state, token = _split_start(_plan_gather_pass, [], lands, name=name + "_pass_start")
        return (keys, name, state), token, lands[0]

    def gather_end(job, value):
        keys, name, state = job
        return _layer_weights(dict(zip(keys, _split_wait(_plan_gather_pass, state, value, name=name + "_pass_wait"))))

    jobs = {}

    def source(l, point, value):
        if (l, point) == (0, "begin"):
            first = _all_gather([shards[0][k] for k in _FIRST], name="gather_l0_first")
            jobs["l0"], token = gather_out(0, _LATER, "gather_l0", after=first[0])
            return _layer_weights(dict(zip(_FIRST, first))), token
        if (l, point) == (0, "attended"):
            jobs["l0"], _, arrived = gather_pass(jobs["l0"], value)
            jobs["l1"], jobs["token"] = gather_out(1, _BIG, "gather_l1", after=arrived)
            return {}, None
        if (l, point) == (0, "mixed"):
            return gather_end(jobs.pop("l0"), value), jobs.pop("token")
        if (l, point) == (0, "expanded"):
            jobs["l1"], token, _ = gather_pass(jobs["l1"], value)
            return {}, token
        if (l, point) == (1, "begin"):
            return gather_end(jobs.pop("l1"), value), None
        return {}, None

    received = [{} for _ in range(DEPTH)]
    travelling = []

    def grads_done(l, gw):
        blocks = _grad_blocks(gw)
        keys = [k for k in _BIG if k in blocks]
        parts = [blocks[k] for k in keys]
        group = f"exchange_grads_l{l}_" + ("in" if "w_in" in blocks else "merge" if "w_out" in blocks else "mlp")
        lands = [_own_block_placed(lax.dynamic_index_in_dim(p, me, 0, keepdims=False), p) for p in parts]
        state, token = _split_start(_plan_exchange, parts, lands, name=group + "_start")
        travelling.append((l, keys, state, group + "_wait"))
        return token

    loss, dx, small = _local_step(x[0], mem[0], loss_target[0], sp, source, grads_done)
    loss = lax.psum(loss[0, 0], ("x", "y", "c"))
    grads, deltas, new_m, new_v = {}, {}, {}, {}

    def update_small(after):
        like = [w[k] for k in _SMALL]
        g_small = _unpack(_all_reduce(_pack([small[k] for k in _SMALL]), after, name="all_reduce_small"), like)
        rows = lambda d: [_as_rows(d[k]) for k in _SMALL]
        outs = _adamw_small([_as_rows(g) for g in g_small], rows(w), rows(m), rows(v), name="adamw_small")
        grads.update(zip(_SMALL, g_small))
        for dst, vals in zip((deltas, new_m, new_v), outs):
            dst.update({k: a.reshape(w[k].shape) for k, a in zip(_SMALL, vals)})
        return outs[0][0]

    done = dx
    groups = list(dict.fromkeys(tuple(keys) for _, keys, _, _ in travelling))
    for group_keys in groups:
        if group_keys == groups[-1]:
            done = update_small(done)
        for l, keys, state, wait_name in travelling:
            if tuple(keys) == group_keys:
                received[l].update(zip(keys, _split_wait(_plan_exchange, state, done, name=wait_name)))
        for k in group_keys:
            outs = _adamw_sharded([received[l][k] for l in range(DEPTH)], w[k], m[k], v[k], name="adamw_" + k)
            grads[k], deltas[k], new_m[k], new_v[k] = outs
        done = grads[group_keys[-1]]

    return (loss, dx[None], *[grads[k] for k in names], *[deltas[k] for k in names], *[new_m[k] for k in names],
            *[new_v[k] for k in names])
```

```python
import functools
import math

import jax
import jax.numpy as jnp
from jax import lax
from jax.experimental import pallas as pl
from jax.experimental.pallas import tpu as pltpu

F32 = jnp.float32
BF16 = jnp.bfloat16
MESH = pl.DeviceIdType.MESH

N_DEV = 8
D = 1024
DEPTH = 2
EPS = 1e-6
NEG = -1e30
POOL_W = 256
FOX_H = 8
FOX_DH = 64
FOX_W = 512
SGU_W = 256
SGU_CHUNK = 128
XH = 4
XDH = 256
N_IN = 5384
R_OFF_Q, R_OFF_F, R_OFF_C = 256, 1792, 1800
QKV_W = 3 * FOX_W
OFF_A, OFF_F, OFF_C, OFF_G, REST_W = 0, 256, 512, 1024, 4096
F_LANES = 128

ADAM_LR = 0.001
ADAM_B1 = 0.9
ADAM_B2 = 0.999
ADAM_EPS = 1e-08
ADAM_WD = 0.01
ADAM_STEP = 10

VMEM_LIMIT = 56 * 1024 * 1024


def _tile(n, pref):
    t = min(n, pref)
    while n % t:
        t -= 128
    assert t > 0, (n, pref)
    return t


def _params(sem=None):
    return pltpu.CompilerParams(dimension_semantics=sem, vmem_limit_bytes=VMEM_LIMIT)


def _dot(a, b, ca, cb):
    return lax.dot_general(a, b, (((ca,), (cb,)), ((), ())), preferred_element_type=F32)


def _sigmoid(z):
    return 1.0 / (1.0 + jnp.exp(-z))


_GELU_K = math.sqrt(2.0 / math.pi)
_GELU_C = 0.044715


def _gelu(x):
    return 0.5 * x * (1.0 + jnp.tanh(_GELU_K * (x + _GELU_C * x * x * x)))


def _gelu_grad(x):
    t = jnp.tanh(_GELU_K * (x + _GELU_C * x * x * x))
    return 0.5 * (1.0 + t) + 0.5 * x * (1.0 - t * t) * _GELU_K * (1.0 + 3.0 * _GELU_C * x * x)


def _rows(shape):
    return lax.broadcasted_iota(jnp.int32, shape, 0)


def _lanes(shape):
    return lax.broadcasted_iota(jnp.int32, shape, 1)


class _Gathered:
    def __init__(self, arr):
        self.arr = arr
        self.shape = (arr.shape[1], N_DEV * arr.shape[2])


_TOKEN = (8, 128)


def _mm(a, b, *, ta=False, tb=False, extras=(), epilogue=None, out_dtypes=(F32,), shard_out=False, after=None, tm=None, tn=512, tk=None,
        name):
    M, K = (a.shape[1], a.shape[0]) if ta else a.shape
    N, Kb = b.shape if tb else b.shape[::-1]
    assert Kb == K, (a.shape, b.shape, ta, tb)
    gathered = isinstance(b, _Gathered)
    if gathered:
        if tb:
            tk = b.arr.shape[2]
        else:
            tn = b.arr.shape[2]
    if shard_out:
        tn = N // N_DEV
    tm = _tile(M, tm or (1024 if ta else 2048))
    tn = _tile(N, tn)
    tk = _tile(K, tk or (2048 if ta else 1024))
    nk = K // tk
    ca, cb = (0 if ta else 1), (1 if tb else 0)
    n_ex, n_out = len(extras), len(out_dtypes)
    tokens = [] if after is None else [after]
    n_in = 2 + n_ex + len(tokens)
    if epilogue is None:
        epilogue = lambda acc: (acc,)

    def body(*refs):
        a_ref, b_ref = refs[:2]
        ex_refs = refs[2:2 + n_ex]
        o_refs = refs[n_in:n_in + n_out]
        part = _dot(a_ref[...].astype(BF16), b_ref[...].astype(BF16), ca, cb)

        def finish(acc):
            for o_ref, val in zip(o_refs, epilogue(acc, *[e[...] for e in ex_refs])):
                o_ref[...] = val.astype(o_ref.dtype)

        if nk == 1:
            finish(part)
        else:
            acc_ref = refs[-1]
            k = pl.program_id(2)

            @pl.when(k == 0)
            def _():
                acc_ref[...] = part

            @pl.when(k > 0)
            def _():
                acc_ref[...] += part

            @pl.when(k == nk - 1)
            def _():
                finish(acc_ref[...])

    a_spec = pl.BlockSpec((tk, tm), lambda i, j, k: (k, i)) if ta else pl.BlockSpec((tm, tk), lambda i, j, k: (i, k))
    if not gathered:
        b_arr = b
        b_spec = pl.BlockSpec((tn, tk), lambda i, j, k: (j, k)) if tb else pl.BlockSpec((tk, tn), lambda i, j, k: (k, j))
    else:
        b_arr = b.arr
        if tb:
            b_spec = pl.BlockSpec((None, tn, tk), lambda i, j, k: (k, j, 0))
        else:
            b_spec = pl.BlockSpec((None, tk, tn), lambda i, j, k: (j, k, 0))
    tile = pl.BlockSpec((tm, tn), lambda i, j, k: (i, j))
    if shard_out:
        out_specs = [pl.BlockSpec((None, tm, tn), lambda i, j, k: (j, i, 0))] * n_out
        out_shape = [jax.ShapeDtypeStruct((N_DEV, M, tn), dt) for dt in out_dtypes]
    else:
        out_specs = [tile] * n_out
        out_shape = [jax.ShapeDtypeStruct((M, N), dt) for dt in out_dtypes]
    size = lambda dt: jnp.dtype(dt).itemsize
    vmem = 2 * (tm * tk * size(a.dtype) + tk * tn * size(b_arr.dtype)
                + tm * tn * (sum(size(e.dtype) for e in extras) + sum(map(size, out_dtypes))))
    vmem += tm * tn * 4 * (nk > 1)
    assert vmem <= VMEM_LIMIT - (4 << 20), (name, vmem)
    outs = pl.pallas_call(
        body,
        name=name,
        grid=(M // tm, N // tn, nk),
        in_specs=[a_spec, b_spec] + [tile] * n_ex + [pl.BlockSpec(_TOKEN, lambda i, j, k: (0, 0))] * len(tokens),
        out_specs=out_specs,
        out_shape=out_shape,
        scratch_shapes=[pltpu.VMEM((tm, tn), F32)] if nk > 1 else [],
        compiler_params=_params(("parallel", "parallel", "arbitrary")),
    )(a, b_arr, *extras, *tokens)
    return outs[0] if n_out == 1 else outs


def _add(acc, res):
    return (acc + res,)


def _rms_fwd(x, g, *, after=None, name):
    R, C = x.shape
    tm = _tile(R, 256)
    tokens = [] if after is None else [after]

    def body(x_ref, g_ref, *rest):
        xv = x_ref[...]
        r = lax.rsqrt(jnp.mean(xv * xv, axis=-1, keepdims=True) + EPS)
        rest[-1][...] = (xv * r * g_ref[...]).astype(BF16)

    return pl.pallas_call(
        body,
        name=name,
        grid=(R // tm,),
        in_specs=[pl.BlockSpec((tm, C), lambda i: (i, 0)), pl.BlockSpec((1, C), lambda i: (0, 0))]
        + [pl.BlockSpec(_TOKEN, lambda i: (0, 0))] * len(tokens),
        out_specs=pl.BlockSpec((tm, C), lambda i: (i, 0)),
        out_shape=jax.ShapeDtypeStruct((R, C), BF16),
        compiler_params=_params(("parallel",)),
    )(x, g.reshape(1, C), *tokens)


def _rms_bwd(x, g, dh, dres, *, name):
    R, C = x.shape
    tm = _tile(R, 256)

    def body(x_ref, g_ref, dh_ref, dres_ref, dx_ref, dg_ref):
        xv = x_ref[...]
        r = lax.rsqrt(jnp.mean(xv * xv, axis=-1, keepdims=True) + EPS)
        xn = xv * r
        dh_v = dh_ref[...].astype(F32)
        dxn = dh_v * g_ref[...]
        dx_ref[...] = r * (dxn - xn * jnp.mean(dxn * xn, axis=-1, keepdims=True)) + dres_ref[...]
        part = jnp.sum(dh_v * xn, axis=0, keepdims=True)

        @pl.when(pl.program_id(0) == 0)
        def _():
            dg_ref[...] = part

        @pl.when(pl.program_id(0) > 0)
        def _():
            dg_ref[...] += part

    row = pl.BlockSpec((tm, C), lambda i: (i, 0))
    vec = pl.BlockSpec((1, C), lambda i: (0, 0))
    dx, dg = pl.pallas_call(
        body,
        name=name,
        grid=(R // tm,),
        in_specs=[row, vec, row, row],
        out_specs=[row, vec],
        out_shape=[jax.ShapeDtypeStruct((R, C), F32), jax.ShapeDtypeStruct((1, C), F32)],
        compiler_params=_params(("arbitrary",)),
    )(x, g.reshape(1, C), dh, dres)
    return dx, dg.reshape(C)


def _final_loss(x, g, target, *, name):
    R, C = x.shape
    tm = _tile(R, 256)

    def body(x_ref, g_ref, t_ref, loss_ref, dx_ref, dg_ref):
        xv = x_ref[...]
        r = lax.rsqrt(jnp.mean(xv * xv, axis=-1, keepdims=True) + EPS)
        xn = xv * r
        gv = g_ref[...]
        err = xn * gv - t_ref[...]
        lpart = (0.5 / C) * jnp.sum(jnp.sum(err * err, axis=1, keepdims=True), axis=0, keepdims=True)
        dy = err * (1.0 / C)
        dxn = dy * gv
        dx_ref[...] = r * (dxn - xn * jnp.mean(dxn * xn, axis=-1, keepdims=True))
        gpart = jnp.sum(dy * xn, axis=0, keepdims=True)

        @pl.when(pl.program_id(0) == 0)
        def _():
            loss_ref[...] = lpart
            dg_ref[...] = gpart

        @pl.when(pl.program_id(0) > 0)
        def _():
            loss_ref[...] += lpart
            dg_ref[...] += gpart

    row = pl.BlockSpec((tm, C), lambda i: (i, 0))
    vec = pl.BlockSpec((1, C), lambda i: (0, 0))
    loss, dx, dg = pl.pallas_call(
        body,
        name=name,
        grid=(R // tm,),
        in_specs=[row, vec, row],
        out_specs=[pl.BlockSpec((1, 1), lambda i: (0, 0)), row, vec],
        out_shape=[jax.ShapeDtypeStruct((1, 1), F32), jax.ShapeDtypeStruct((R, C), F32), jax.ShapeDtypeStruct((1, C), F32)],
        compiler_params=_params(("arbitrary",)),
    )(x, g.reshape(1, C), target)
    return loss, dx, dg.reshape(C)


def _pool_select(lane, vals):
    out = vals[3]
    for gi in (2, 1, 0):
        out = jnp.where(lane < 64 * (gi + 1), vals[gi], out)
    return out


def _pool_diff(a):
    row, lane = _rows(a.shape), _lanes(a.shape)

    def down(v, k):
        return jnp.where(row >= k, pltpu.roll(v, k, 0), 0.0)

    s2 = a + down(a, 1)
    s4 = s2 + down(s2, 2)
    s8 = s4 + down(s4, 4)
    s16 = s8 + down(s8, 8)
    wsum = _pool_select(lane, (s2, s4, s8, s16))
    win = _pool_select(lane, (2, 4, 8, 16))
    cnt = jnp.minimum(row + 1, win).astype(F32)
    return wsum / cnt - a, cnt


def _pool_diff_t(dd, cnt):
    S = dd.shape[0]
    row, lane = _rows(dd.shape), _lanes(dd.shape)

    def up(v, k):
        return jnp.where(row < S - k, pltpu.roll(v, S - k, 0), 0.0)

    e = dd / cnt
    s2 = e + up(e, 1)
    s4 = s2 + up(s2, 2)
    s8 = s4 + up(s4, 4)
    s16 = s8 + up(s8, 8)
    return _pool_select(lane, (s2, s4, s8, s16)) - dd


def _pool_fwd(rest, wbd, scale, *, name):
    S = rest.shape[0]

    def body(a_ref, w_ref, s_ref, o_ref):
        d, _ = _pool_diff(a_ref[...])
        yp = _dot(d.astype(BF16), w_ref[...], 1, 0)
        o_ref[...] = (yp * s_ref[...]).astype(BF16)

    return pl.pallas_call(
        body,
        name=name,
        grid=(1,),
        in_specs=[
            pl.BlockSpec((S, POOL_W), lambda i: (0, OFF_A // POOL_W)),
            pl.BlockSpec((POOL_W, POOL_W), lambda i: (0, 0)),
            pl.BlockSpec((1, POOL_W), lambda i: (0, 0)),
        ],
        out_specs=pl.BlockSpec((S, POOL_W), lambda i: (0, 0)),
        out_shape=jax.ShapeDtypeStruct((S, POOL_W), BF16),
        compiler_params=_params(("arbitrary",)),
    )(rest, wbd, scale.reshape(1, POOL_W))


def _pool_bwd(rest, wbd, wbd_t, scale, dpa, *, name):
    S = rest.shape[0]

    def body(a_ref, w_ref, wt_ref, s_ref, dpa_ref, da_ref, dw_ref, ds_ref):
        d, cnt = _pool_diff(a_ref[...])
        db = d.astype(BF16)
        yp = _dot(db, w_ref[...], 1, 0)
        dpa_v = dpa_ref[...]
        ds_ref[...] = jnp.sum(dpa_v * yp, axis=0, keepdims=True)
        dyp = (dpa_v * s_ref[...]).astype(BF16)
        dw_ref[...] = _dot(db, dyp, 0, 0)
        dd = _dot(dyp, wt_ref[...], 1, 0)
        da_ref[...] = _pool_diff_t(dd, cnt).astype(BF16)

    full = pl.BlockSpec((S, POOL_W), lambda i: (0, 0))
    sq = pl.BlockSpec((POOL_W, POOL_W), lambda i: (0, 0))
    vec = pl.BlockSpec((1, POOL_W), lambda i: (0, 0))
    return pl.pallas_call(
        body,
        name=name,
        grid=(1,),
        in_specs=[pl.BlockSpec((S, POOL_W), lambda i: (0, OFF_A // POOL_W)), sq, sq, vec, full],
        out_specs=[full, sq, vec],
        out_shape=[
            jax.ShapeDtypeStruct((S, POOL_W), BF16),
            jax.ShapeDtypeStruct((POOL_W, POOL_W), F32),
            jax.ShapeDtypeStruct((1, POOL_W), F32),
        ],
        compiler_params=_params(("arbitrary",)),
    )(rest, wbd, wbd_t, scale.reshape(1, POOL_W), dpa)


def _log_sigmoid(z):
    return jnp.minimum(z, 0.0) - jnp.log(1.0 + jnp.exp(-jnp.abs(z)))


_F_SPEC_COL = OFF_F // F_LANES


def _fox_prep(rest, bpad, *, name):
    S = rest.shape[0]

    def body(f_ref, b_ref, o_ref, ot_ref):
        acc = _log_sigmoid(f_ref[...] + b_ref[...])
        row = _rows(acc.shape)
        k = 1
        while k < S:
            acc = acc + jnp.where(row >= k, pltpu.roll(acc, k, 0), 0.0)
            k *= 2
        o_ref[...] = acc
        ot_ref[...] = acc.T

    return pl.pallas_call(
        body,
        name=name,
        grid=(1,),
        in_specs=[pl.BlockSpec((S, F_LANES), lambda i: (0, _F_SPEC_COL)), pl.BlockSpec((1, F_LANES), lambda i: (0, 0))],
        out_specs=[pl.BlockSpec((S, F_LANES), lambda i: (0, 0)), pl.BlockSpec((F_LANES, S), lambda i: (0, 0))],
        out_shape=[jax.ShapeDtypeStruct((S, F_LANES), F32), jax.ShapeDtypeStruct((F_LANES, S), F32)],
        compiler_params=_params(("arbitrary",)),
    )(rest, bpad)


def _fox_post(rest, bpad, dcum, *, name):
    S = rest.shape[0]

    def body(f_ref, b_ref, d_ref, df_ref, db_ref):
        acc = d_ref[...]
        row = _rows(acc.shape)
        k = 1
        while k < S:
            acc = acc + jnp.where(row < S - k, pltpu.roll(acc, S - k, 0), 0.0)
            k *= 2
        df = acc * (1.0 - _sigmoid(f_ref[...] + b_ref[...]))
        df_ref[...] = df.astype(BF16)
        db_ref[...] = jnp.sum(df, axis=0, keepdims=True)

    full = pl.BlockSpec((S, F_LANES), lambda i: (0, 0))
    vec = pl.BlockSpec((1, F_LANES), lambda i: (0, 0))
    return pl.pallas_call(
        body,
        name=name,
        grid=(1,),
        in_specs=[pl.BlockSpec((S, F_LANES), lambda i: (0, _F_SPEC_COL)), vec, full],
        out_specs=[full, vec],
        out_shape=[jax.ShapeDtypeStruct((S, F_LANES), BF16), jax.ShapeDtypeStruct((1, F_LANES), F32)],
        compiler_params=_params(("arbitrary",)),
    )(rest, bpad, dcum)


_FOX_SCALE = FOX_DH ** -0.5
_PAIRS = FOX_H // 2


def _scaled(v):
    return (v.astype(F32) * _FOX_SCALE).astype(BF16)


def _diag_mask(s):
    return jnp.where(_rows(s.shape) >= _lanes(s.shape), s, NEG)


def _fox_fwd(qkv, cum, fk3, *, name):
    S = qkv.shape[0]
    nk, t = fk3.shape[1:]

    def body(q_ref, k_ref, v_ref, cum_ref, fk_ref, o_ref, lse_ref):
        i = pl.program_id(0)
        lane = _lanes((t, 128))
        lo = lane < FOX_DH
        cumv = cum_ref[...]
        qm, fq = [], []
        for h in range(FOX_H):
            qs = _scaled(q_ref[:, 128 * (h // 2):128 * (h // 2 + 1)])
            zero = jnp.zeros_like(qs)
            qm.append(jnp.where(lo, qs, zero) if h % 2 == 0 else jnp.where(lo, zero, qs))
            fq.append(cumv[:, h:h + 1])

        def tile(j, state, masked):
            m, l, acc = (list(part) for part in state)
            k0 = pl.multiple_of(j * t, t)
            for hp in range(_PAIRS):
                cols = slice(128 * hp, 128 * (hp + 1))
                kb = k_ref[pl.ds(k0, t), cols]
                vb = v_ref[pl.ds(k0, t), cols]
                alphas, pvs = [], []
                for h in (2 * hp, 2 * hp + 1):
                    s = _dot(qm[h], kb, 1, 1) + fq[h] - fk_ref[h, pl.ds(j, 1), :]
                    if masked:
                        s = _diag_mask(s)
                    m_new = jnp.maximum(m[h], jnp.max(s, axis=-1, keepdims=True))
                    p = jnp.exp(s - m_new)
                    alpha = jnp.exp(m[h] - m_new)
                    l[h] = alpha * l[h] + jnp.sum(p, axis=-1, keepdims=True)
                    m[h] = m_new
                    alphas.append(alpha)
                    pvs.append(_dot(p.astype(BF16), vb, 1, 0))
                acc[hp] = jnp.where(lo, alphas[0], alphas[1]) * acc[hp] + jnp.where(lo, pvs[0], pvs[1])
            return tuple(m), tuple(l), tuple(acc)

        init = ((jnp.full((t, 1), NEG, F32),) * FOX_H, (jnp.zeros((t, 1), F32),) * FOX_H, (jnp.zeros((t, 128), F32),) * _PAIRS)
        state = lax.fori_loop(0, i, functools.partial(tile, masked=False), init)
        m, l, acc = tile(i, state, True)
        for hp in range(_PAIRS):
            o_ref[:, 128 * hp:128 * (hp + 1)] = acc[hp] / jnp.where(lo, l[2 * hp], l[2 * hp + 1])
            lse = [m[h] + jnp.log(l[h]) for h in (2 * hp, 2 * hp + 1)]
            lse_ref[hp] = jnp.where(lane == 0, lse[0], jnp.where(lane == 1, lse[1], 0.0))

    whole = lambda col: pl.BlockSpec((S, FOX_W), lambda i: (0, col))
    return pl.pallas_call(
        body,
        name=name,
        grid=(S // t,),
        in_specs=[
            pl.BlockSpec((t, FOX_W), lambda i: (i, 0)), whole(1), whole(2),
            pl.BlockSpec((t, F_LANES), lambda i: (i, 0)),
            pl.BlockSpec((FOX_H, nk, t), lambda i: (0, 0, 0)),
        ],
        out_specs=[pl.BlockSpec((t, FOX_W), lambda i: (i, 0)), pl.BlockSpec((_PAIRS, t, 128), lambda i: (0, i, 0))],
        out_shape=[jax.ShapeDtypeStruct((S, FOX_W), F32), jax.ShapeDtypeStruct((_PAIRS, S, 128), F32)],
        compiler_params=_params(("arbitrary",)),
    )(qkv, qkv, qkv, cum, fk3)


def _fox_bwd(qkv, cum, fk3, o, do, lse, *, name):
    S = qkv.shape[0]
    nk, t = fk3.shape[1:]
    q_at, k_at, v_at = 0, FOX_W, 2 * FOX_W

    def body(qkv_ref, cum_ref, fk_ref, o_ref, do_ref, lse_ref, dq_ref, dk_ref, dv_ref, dfq_ref, dfk_ref,
             qs_sc, ks_sc, delta_sc, dq_sc):
        lane = _lanes((t, 128))
        lo = lane < FOX_DH
        mine = lambda h: lo if h % 2 == 0 else jnp.logical_not(lo)

        def by_head(tile, values):
            for h, val in enumerate(values):
                tile = jnp.where(lane == h, val, tile)
            return tile

        def prep(i, carry):
            r = pl.ds(pl.multiple_of(i * t, t), t)
            qs_sc[r, :] = _scaled(qkv_ref[r, q_at:q_at + FOX_W])
            ks_sc[r, :] = _scaled(qkv_ref[r, k_at:k_at + FOX_W])
            sums = []
            for hp in range(_PAIRS):
                cols = slice(128 * hp, 128 * (hp + 1))
                prod = do_ref[r, cols].astype(F32) * o_ref[r, cols]
                sums += [jnp.sum(jnp.where(mine(h), prod, 0.0), axis=-1, keepdims=True) for h in (2 * hp, 2 * hp + 1)]
            delta_sc[r, :] = by_head(jnp.zeros((t, 128), F32), sums)
            dfq_ref[r, :] = jnp.zeros((t, 128), F32)
            dq_sc[r, :] = jnp.zeros((t, FOX_W), F32)
            return carry

        lax.fori_loop(0, nk, prep, 0)

        def kv_tile(j, carry):
            kr = pl.ds(pl.multiple_of(j * t, t), t)

            def q_tile(i, acc, masked):
                dk, dv, dfk = list(acc[:_PAIRS]), list(acc[_PAIRS:2 * _PAIRS]), list(acc[2 * _PAIRS:])
                qr = pl.ds(pl.multiple_of(i * t, t), t)
                delta_t, cum_t, dq_old, dfq_old = delta_sc[qr, :], cum_ref[qr, :], dq_sc[qr, :], dfq_ref[qr, :]
                row_sums, dq_new = [], []
                for hp in range(_PAIRS):
                    cols = slice(128 * hp, 128 * (hp + 1))
                    kb = qkv_ref[kr, k_at + 128 * hp:k_at + 128 * (hp + 1)]
                    vb = qkv_ref[kr, v_at + 128 * hp:v_at + 128 * (hp + 1)]
                    ksb, qsb, dob = ks_sc[kr, cols], qs_sc[qr, cols], do_ref[qr, cols]
                    zero = jnp.zeros_like(qsb)
                    dq_t = jnp.zeros((t, 128), F32)
                    for h in (2 * hp, 2 * hp + 1):
                        qe, doe, ke = (jnp.where(mine(h), a, zero) for a in (qsb, dob, ksb))
                        s = _dot(qe, kb, 1, 1) + cum_t[:, h:h + 1] - fk_ref[h, pl.ds(j, 1), :]
                        if masked:
                            s = _diag_mask(s)
                        p = jnp.exp(s - lse_ref[hp, qr, h % 2:h % 2 + 1])
                        dv[hp] = dv[hp] + _dot(p.astype(BF16), doe, 0, 0)
                        dp = _dot(doe, vb, 1, 1)
                        ds = p * (dp - delta_t[:, h:h + 1])
                        dsb = ds.astype(BF16)
                        dk[hp] = dk[hp] + _dot(dsb, qe, 0, 0)
                        dq_t = dq_t + _dot(dsb, ke, 1, 0)
                        row_sums.append(jnp.sum(ds, axis=-1, keepdims=True))
                        dfk[h] = dfk[h] - jnp.sum(ds, axis=0, keepdims=True)
                    dq_new.append(dq_old[:, cols] + dq_t)
                for hp in range(_PAIRS):
                    dq_sc[qr, 128 * hp:128 * (hp + 1)] = dq_new[hp]
                dfq_ref[qr, :] = dfq_old + by_head(jnp.zeros((t, 128), F32), row_sums)
                return (*dk, *dv, *dfk)

            init = tuple([jnp.zeros((t, 128), F32)] * (2 * _PAIRS) + [jnp.zeros((1, t), F32)] * FOX_H)
            acc = q_tile(j, init, True)
            acc = lax.fori_loop(j + 1, nk, functools.partial(q_tile, masked=False), acc)
            for hp in range(_PAIRS):
                cols = slice(128 * hp, 128 * (hp + 1))
                dk_ref[kr, cols] = acc[hp].astype(BF16)
                dv_ref[kr, cols] = acc[_PAIRS + hp].astype(BF16)
            for h in range(FOX_H):
                dfk_ref[h, pl.ds(j, 1), :] = acc[2 * _PAIRS + h]
            return carry

        lax.fori_loop(0, nk, kv_tile, 0)
        dq_ref[...] = dq_sc[...].astype(BF16)

    vm = pl.BlockSpec(memory_space=pltpu.VMEM)
    big = jax.ShapeDtypeStruct((S, FOX_W), BF16)
    return pl.pallas_call(
        body,
        name=name,
        in_specs=[vm] * 6,
        out_specs=[vm] * 5,
        out_shape=[big, big, big, jax.ShapeDtypeStruct((S, 128), F32), jax.ShapeDtypeStruct((FOX_H, nk, t), F32)],
        scratch_shapes=[pltpu.VMEM((S, FOX_W), BF16), pltpu.VMEM((S, FOX_W), BF16), pltpu.VMEM((S, 128), F32),
                        pltpu.VMEM((S, FOX_W), F32)],
        compiler_params=pltpu.CompilerParams(vmem_limit_bytes=VMEM_LIMIT),
    )(qkv, cum, fk3, o, do, lse)


def _group_mask(lane, gi):
    return (lane >= 64 * gi) & (lane < 64 * (gi + 1))


_U_COL = OFF_C // SGU_W


def _sgu_fwd(rest, gn, wm, bias, *, name):
    S = rest.shape[0]
    ts = _tile(S, 512)
    nc = ts // SGU_CHUNK

    def body(u_ref, v_ref, g_ref, w_ref, b_ref, o_ref):
        zv = _gelu(v_ref[...])
        vn = zv * lax.rsqrt(jnp.mean(zv * zv, axis=-1, keepdims=True) + EPS) * g_ref[...]
        lane = _lanes((SGU_CHUNK, SGU_W))
        for c in range(nc):
            rows = slice(c * SGU_CHUNK, (c + 1) * SGU_CHUNK)
            vcb = vn[rows].astype(BF16)
            mixed = b_ref[...]
            for gi in range(4):
                mixed = mixed + jnp.where(_group_mask(lane, gi), _dot(w_ref[gi], vcb, 1, 0), 0.0)
            o_ref[rows, :] = (_gelu(u_ref[rows, :]) * mixed).astype(BF16)

    return pl.pallas_call(
        body,
        name=name,
        grid=(S // ts,),
        in_specs=[
            pl.BlockSpec((ts, SGU_W), lambda i: (i, _U_COL)),
            pl.BlockSpec((ts, SGU_W), lambda i: (i, _U_COL + 1)),
            pl.BlockSpec((1, SGU_W), lambda i: (0, 0)),
            pl.BlockSpec((4, SGU_CHUNK, SGU_CHUNK), lambda i: (0, 0, 0)),
            pl.BlockSpec((SGU_CHUNK, SGU_W), lambda i: (0, 0)),
        ],
        out_specs=pl.BlockSpec((ts, SGU_W), lambda i: (i, 0)),
        out_shape=jax.ShapeDtypeStruct((S, SGU_W), BF16),
        compiler_params=_params(("parallel",)),
    )(rest, rest, gn.reshape(1, SGU_W), wm, bias)


def _sgu_bwd(rest, gn, wm, wm_t, bias, dsg, *, name):
    S = rest.shape[0]
    ts = _tile(S, 512)
    nc = ts // SGU_CHUNK

    def body(u_ref, v_ref, g_ref, w_ref, wt_ref, b_ref, dsg_ref, dc_ref, dw_ref, db_ref, dg_ref):
        first = pl.program_id(0) == 0

        @pl.when(first)
        def _():
            dw_ref[...] = jnp.zeros_like(dw_ref)
            db_ref[...] = jnp.zeros_like(db_ref)
            dg_ref[...] = jnp.zeros_like(dg_ref)

        gv = g_ref[...]
        lane = _lanes((SGU_CHUNK, SGU_W))
        for c in range(nc):
            rows = slice(c * SGU_CHUNK, (c + 1) * SGU_CHUNK)
            vpre = v_ref[rows, :]
            upre = u_ref[rows, :]
            zv = _gelu(vpre)
            r = lax.rsqrt(jnp.mean(zv * zv, axis=-1, keepdims=True) + EPS)
            zn = zv * r
            vcb = (zn * gv).astype(BF16)
            mixed = b_ref[...]
            for gi in range(4):
                mixed = mixed + jnp.where(_group_mask(lane, gi), _dot(w_ref[gi], vcb, 1, 0), 0.0)
            zu = _gelu(upre)
            dsg_v = dsg_ref[rows, :]
            dc_ref[rows, :SGU_W] = (dsg_v * mixed * _gelu_grad(upre)).astype(BF16)
            dmixed = dsg_v * zu
            db_ref[...] += dmixed
            dvn = jnp.zeros((SGU_CHUNK, SGU_W), F32)
            for gi in range(4):
                dmg = jnp.where(_group_mask(lane, gi), dmixed, 0.0).astype(BF16)
                dw_ref[gi] += _dot(dmg, vcb, 1, 1)
                dvn = dvn + _dot(wt_ref[gi], dmg, 1, 0)
            dg_ref[...] += jnp.sum(dvn * zn, axis=0, keepdims=True)
            dzn = dvn * gv
            dzv = r * (dzn - zn * jnp.mean(dzn * zn, axis=-1, keepdims=True))
            dc_ref[rows, SGU_W:] = (dzv * _gelu_grad(vpre)).astype(BF16)

    blk = pl.BlockSpec((ts, SGU_W), lambda i: (i, 0))
    vec = pl.BlockSpec((1, SGU_W), lambda i: (0, 0))
    w3 = pl.BlockSpec((4, SGU_CHUNK, SGU_CHUNK), lambda i: (0, 0, 0))
    bsp = pl.BlockSpec((SGU_CHUNK, SGU_W), lambda i: (0, 0))
    return pl.pallas_call(
        body,
        name=name,
        grid=(S // ts,),
        in_specs=[
            pl.BlockSpec((ts, SGU_W), lambda i: (i, _U_COL)),
            pl.BlockSpec((ts, SGU_W), lambda i: (i, _U_COL + 1)),
            vec, w3, w3, bsp, blk,
        ],
        out_specs=[pl.BlockSpec((ts, 2 * SGU_W), lambda i: (i, 0)), w3, bsp, vec],
        out_shape=[
            jax.ShapeDtypeStruct((S, 2 * SGU_W), BF16),
            jax.ShapeDtypeStruct((4, SGU_CHUNK, SGU_CHUNK), F32),
            jax.ShapeDtypeStruct((SGU_CHUNK, SGU_W), F32),
            jax.ShapeDtypeStruct((1, SGU_W), F32),
        ],
        compiler_params=_params(("arbitrary",)),
    )(rest, rest, gn.reshape(1, SGU_W), wm, wm_t, bias, dsg)


_GT = 512
_G0 = OFF_G // _GT


def _gate_specs(tm, col_of):
    specs = [pl.BlockSpec((tm, _GT), functools.partial(lambda k, *ids: (col_of(*ids)[0], _G0 + 2 * k + col_of(*ids)[1]), k)) for k in range(3)]
    specs += [pl.BlockSpec((1, _GT), functools.partial(lambda k, *ids: (0, 2 * k + col_of(*ids)[1]), k)) for k in range(3)]
    return specs


def _merge_fwd(rest, bg, ya, yb, yc, *, name):
    S = rest.shape[0]
    tm = _tile(S, 512)

    def body(g1, g2, g3, b1, b2, b3, ya_ref, yb_ref, yc_ref, o_ref):
        acc = _sigmoid(g1[...] + b1[...]) * ya_ref[...]
        acc = acc + _sigmoid(g2[...] + b2[...]) * yb_ref[...]
        acc = acc + _sigmoid(g3[...] + b3[...]) * yc_ref[...]
        o_ref[...] = acc.astype(BF16)

    blk = pl.BlockSpec((tm, _GT), lambda i, j: (i, j))
    return pl.pallas_call(
        body,
        name=name,
        grid=(S // tm, D // _GT),
        in_specs=_gate_specs(tm, lambda i, j: (i, j)) + [blk, blk, blk],
        out_specs=blk,
        out_shape=jax.ShapeDtypeStruct((S, D), BF16),
        compiler_params=_params(("parallel", "parallel")),
    )(rest, rest, rest, bg, bg, bg, ya, yb, yc)


def _merge_bwd(rest, bg, ya, yb, yc, dm, *, name):
    S = rest.shape[0]
    tm = _tile(S, 512)

    def body(g1, g2, g3, b1, b2, b3, ya_ref, yb_ref, yc_ref, dm_ref, dya, dyb, dyc, dg1, dg2, dg3, db1, db2, db3):
        first = pl.program_id(1) == 0
        dmv = dm_ref[...]
        for g_ref, b_ref, y_ref, dy_ref, dg_ref, db_ref in (
            (g1, b1, ya_ref, dya, dg1, db1), (g2, b2, yb_ref, dyb, dg2, db2), (g3, b3, yc_ref, dyc, dg3, db3)):
            gate = _sigmoid(g_ref[...] + b_ref[...])
            dy_ref[...] = (dmv * gate).astype(BF16)
            dpre = dmv * y_ref[...] * gate * (1.0 - gate)
            dg_ref[...] = dpre.astype(BF16)
            part = jnp.sum(dpre, axis=0, keepdims=True)

            @pl.when(first)
            def _():
                db_ref[...] = part

            @pl.when(jnp.logical_not(first))
            def _():
                db_ref[...] += part

    blk = pl.BlockSpec((tm, _GT), lambda j, i: (i, j))
    vec = pl.BlockSpec((1, _GT), lambda j, i: (0, j))
    big = jax.ShapeDtypeStruct((S, D), BF16)
    small = jax.ShapeDtypeStruct((1, D), F32)
    return pl.pallas_call(
        body,
        name=name,
        grid=(D // _GT, S // tm),
        in_specs=_gate_specs(tm, lambda j, i: (i, j)) + [blk, blk, blk, blk],
        out_specs=[blk] * 6 + [vec] * 3,
        out_shape=[big] * 6 + [small] * 3,
        compiler_params=_params(("parallel", "arbitrary")),
    )(rest, rest, rest, bg, bg, bg, ya, yb, yc, dm)


_X_SCALE = XDH ** -0.5


def _xattn_fwd(xq, kv, *, name):
    S = xq.shape[0]
    M = kv.shape[0]
    tq = _tile(S, 512)

    def body(q_ref, k_ref, v_ref, o_ref):
        s = _dot(q_ref[...], k_ref[...], 1, 1) * _X_SCALE
        e = jnp.exp(s - jnp.max(s, axis=-1, keepdims=True))
        p = e / jnp.sum(e, axis=-1, keepdims=True)
        o_ref[...] = _dot(p.astype(BF16), v_ref[...], 1, 0).astype(BF16)

    return pl.pallas_call(
        body,
        name=name,
        grid=(S // tq, XH),
        in_specs=[
            pl.BlockSpec((tq, XDH), lambda i, h: (i, h)),
            pl.BlockSpec((M, XDH), lambda i, h: (0, h)),
            pl.BlockSpec((M, XDH), lambda i, h: (0, XH + h)),
        ],
        out_specs=pl.BlockSpec((tq, XDH), lambda i, h: (i, h)),
        out_shape=jax.ShapeDtypeStruct((S, D), BF16),
        compiler_params=_params(("parallel", "parallel")),
    )(xq, kv, kv)


def _xattn_bwd(xq, kv, do, *, name):
    S = xq.shape[0]
    M = kv.shape[0]
    tq = _tile(S, 512)

    def body(q_ref, k_ref, v_ref, do_ref, dq_ref, dk_ref, dv_ref):
        qb = q_ref[...]
        kb = k_ref[...]
        dob = do_ref[...]
        s = _dot(qb, kb, 1, 1) * _X_SCALE
        e = jnp.exp(s - jnp.max(s, axis=-1, keepdims=True))
        p = e / jnp.sum(e, axis=-1, keepdims=True)
        dp = _dot(dob, v_ref[...], 1, 1)
        ds = (p * (dp - jnp.sum(p * dp, axis=-1, keepdims=True)) * _X_SCALE).astype(BF16)
        dq_ref[...] = _dot(ds, kb, 1, 0).astype(BF16)
        dk_part = _dot(ds, qb, 0, 0)
        dv_part = _dot(p.astype(BF16), dob, 0, 0)

        @pl.when(pl.program_id(1) == 0)
        def _():
            dk_ref[...] = dk_part
            dv_ref[...] = dv_part

        @pl.when(pl.program_id(1) > 0)
        def _():
            dk_ref[...] += dk_part
            dv_ref[...] += dv_part

    qspec = pl.BlockSpec((tq, XDH), lambda h, i: (i, h))
    kspec = pl.BlockSpec((M, XDH), lambda h, i: (0, h))
    dxq, dxk, dxv = pl.pallas_call(
        body,
        name=name,
        grid=(XH, S // tq),
        in_specs=[qspec, kspec, pl.BlockSpec((M, XDH), lambda h, i: (0, XH + h)), qspec],
        out_specs=[qspec, kspec, kspec],
        out_shape=[jax.ShapeDtypeStruct((S, D), BF16), jax.ShapeDtypeStruct((M, D), F32), jax.ShapeDtypeStruct((M, D), F32)],
        compiler_params=_params(("parallel", "arbitrary")),
    )(xq, kv, kv, do)
    return dxq, jnp.concatenate([dxk, dxv], axis=1)


def _adam_math(w, g, m, v):
    m = ADAM_B1 * m + (1.0 - ADAM_B1) * g
    v = ADAM_B2 * v + (1.0 - ADAM_B2) * (g * g)
    m_hat = m / (1.0 - ADAM_B1 ** ADAM_STEP)
    v_hat = v / (1.0 - ADAM_B2 ** ADAM_STEP)
    delta = -ADAM_LR * (m_hat / (jnp.sqrt(v_hat) + ADAM_EPS) + ADAM_WD * w)
    return delta, m, v


def _adamw_sharded(parts, w, m, v, *, name):
    _, R, C = w.shape
    Cp = parts[0].shape[2]
    tm = _tile(R, 256)
    nr = R // tm

    def body(p0_ref, p1_ref, w_ref, m_ref, v_ref, g_ref, d_ref, mo_ref, vo_ref):
        def update(p_ref):
            g = p_ref[0][:, :C].astype(F32)
            for dev in range(1, N_DEV):
                g = g + p_ref[dev][:, :C].astype(F32)
            delta, mn, vn = _adam_math(w_ref[...], g, m_ref[...], v_ref[...])
            g_ref[...] = g
            d_ref[...] = delta
            mo_ref[...] = mn
            vo_ref[...] = vn

        @pl.when(pl.program_id(0) == 0)
        def _():
            update(p0_ref)

        @pl.when(pl.program_id(0) == 1)
        def _():
            update(p1_ref)

    p0 = pl.BlockSpec((N_DEV, tm, Cp), lambda l, i: (0, i * (1 - l) + (nr - 1) * l, 0))
    p1 = pl.BlockSpec((N_DEV, tm, Cp), lambda l, i: (0, i * l, 0))
    blk = pl.BlockSpec((None, tm, C), lambda l, i: (l, i, 0))
    sds = jax.ShapeDtypeStruct(w.shape, F32)
    return pl.pallas_call(
        body,
        name=name,
        grid=(DEPTH, nr),
        in_specs=[p0, p1, blk, blk, blk],
        out_specs=[blk] * 4,
        out_shape=[sds] * 4,
        compiler_params=_params(("arbitrary", "arbitrary")),
    )(parts[0], parts[1], w, m, v)


def _adamw_small(g, w, m, v, *, name):
    n = len(g)

    def body(*refs):
        g_refs, w_refs, m_refs, v_refs = (refs[k * n:(k + 1) * n] for k in range(4))
        d_out, m_out, v_out = (refs[(4 + k) * n:(5 + k) * n] for k in range(3))
        for t in range(n):
            delta, mn, vn = _adam_math(w_refs[t][...], g_refs[t][...], m_refs[t][...], v_refs[t][...])
            d_out[t][...] = delta
            m_out[t][...] = mn
            v_out[t][...] = vn

    vm = pl.BlockSpec(memory_space=pltpu.VMEM)
    shapes = [jax.ShapeDtypeStruct(a.shape, F32) for a in w]
    outs = pl.pallas_call(
        body,
        name=name,
        in_specs=[vm] * (4 * n),
        out_specs=[vm] * (3 * n),
        out_shape=shapes * 3,
        compiler_params=pltpu.CompilerParams(vmem_limit_bytes=VMEM_LIMIT),
    )(*g, *w, *m, *v)
    return outs[:n], outs[n:2 * n], outs[2 * n:]


def _position():
    return lax.axis_index("x"), lax.axis_index("y"), lax.axis_index("c")


def _dev_index(px, py, pc):
    return 4 * px + 2 * py + pc


_ANY = pl.BlockSpec(memory_space=pl.ANY)


def _all_gather(shards, *, name):
    n = len(shards)
    out_shape = [jax.ShapeDtypeStruct((N_DEV, *s.shape), s.dtype) for s in shards]
    n_pieces = len(_pieces(out_shape))

    def body(*refs):
        ins, outs = refs[:n], refs[n:2 * n]
        send_sems, recv_sems, local_sems = refs[2 * n:]
        x, y, c = _position()
        me, sibling = (x, y, c), (x, y, 1 - c)
        chips = [(1 - x, y), (x, 1 - y), (1 - x, 1 - y)]
        pieces = _pieces(outs)

        def copy(i, k, block, to, from_input=False):
            t, rows = pieces[i]
            dst = _cut(outs[t].at[_dev_index(*block)], rows)
            return pltpu.make_async_remote_copy(
                src_ref=_cut(ins[t], rows) if from_input else dst, dst_ref=dst, send_sem=send_sems.at[i, k],
                recv_sem=recv_sems.at[i, k], device_id=to, device_id_type=MESH)

        mine = [pltpu.make_async_copy(_cut(ins[t], rows), _cut(outs[t].at[_dev_index(*me)], rows), local_sems.at[i])
                for i, (t, rows) in enumerate(pieces)]
        for cp in mine:
            cp.start()
        started = []
        for j, chip in enumerate(chips):
            for i in range(n_pieces):
                started.append(copy(i, 1 + j, me, (*chip, c), from_input=True))
                started[-1].start()
        for i in range(n_pieces):
            started.append(copy(i, 0, me, sibling, from_input=True))
            started[-1].start()
        for j, chip in enumerate(chips):
            for i in range(n_pieces):
                copy(i, 1 + j, (*chip, c), me).wait_recv()
                started.append(copy(i, 4 + j, (*chip, c), sibling))
                started[-1].start()
        for i in range(n_pieces):
            copy(i, 0, sibling, me).wait_recv()
        for j, chip in enumerate(chips):
            for i in range(n_pieces):
                copy(i, 4 + j, (*chip, 1 - c), me).wait_recv()
        for cp in started:
            cp.wait_send()
        for cp in mine:
            cp.wait()

    return pl.pallas_call(
        body,
        name=name,
        in_specs=[_ANY] * n,
        out_specs=[_ANY] * n,
        out_shape=out_shape,
        scratch_shapes=[pltpu.SemaphoreType.DMA((n_pieces, 7)), pltpu.SemaphoreType.DMA((n_pieces, 7)),
                        pltpu.SemaphoreType.DMA((n_pieces,))],
        compiler_params=pltpu.CompilerParams(has_side_effects=True),
    )(*shards)


def _peers(x, y, c):
    out = []
    for mask in range(1, N_DEV):
        fx, fy, fc = (mask >> 2) & 1, (mask >> 1) & 1, mask & 1
        out.append((1 - x if fx else x, 1 - y if fy else y, 1 - c if fc else c))
    return out


_HBM = pl.BlockSpec(memory_space=pltpu.HBM)
_SEM = pl.BlockSpec(memory_space=pltpu.SEMAPHORE)


def _own_block_placed(block, like):
    x, y, c = _position()
    return lax.dynamic_update_index_in_dim(lax.empty(like.shape, like.dtype), block, _dev_index(x, y, c), 0)


_COPY_BYTES = 256 << 10
_MAX_PIECES = 8


def _pieces(blocks):
    out = []
    for t, b in enumerate(blocks):
        R, C = b.shape[-2:]
        n = max(1, min(_MAX_PIECES, R * C * jnp.dtype(b.dtype).itemsize // _COPY_BYTES))
        while n > 1 and R % (16 * n):
            n -= 1
        out += [(t, pl.ds(j * (R // n), R // n) if n > 1 else None) for j in range(n)]
    return out


def _cut(block, rows):
    return block if rows is None else block.at[rows]


def _copies(per_piece):
    def mark(fn):
        fn.per_piece = per_piece
        return fn
    return mark


@_copies(N_DEV - 1)
def _plan_exchange(srcs, lands, send_sems, recv_sems, arrivals):
    x, y, c = _position()
    me = _dev_index(x, y, c)
    out = []
    for k, peer in enumerate(_peers(x, y, c)):
        p = _dev_index(*peer)
        for i, (t, rows) in enumerate(_pieces(lands)):
            sems = dict(send_sem=send_sems.at[7 * i + k], recv_sem=recv_sems.at[7 * i + k], device_id=peer, device_id_type=MESH)
            src, dst = (lands[t].at[p], lands[t].at[p]) if arrivals else (srcs[t].at[p], lands[t].at[me])
            out.append(pltpu.make_async_remote_copy(src_ref=_cut(src, rows), dst_ref=_cut(dst, rows), **sems))
    return out


@_copies(N_DEV - 1)
def _plan_broadcast(srcs, lands, send_sems, recv_sems, arrivals):
    x, y, c = _position()
    me = _dev_index(x, y, c)
    out = []
    for k, peer in enumerate(_peers(x, y, c)):
        p = _dev_index(*peer)
        for i, (t, rows) in enumerate(_pieces(lands)):
            sems = dict(send_sem=send_sems.at[7 * i + k], recv_sem=recv_sems.at[7 * i + k], device_id=peer, device_id_type=MESH)
            src, dst = (lands[t].at[p], lands[t].at[p]) if arrivals else (srcs[t], lands[t].at[me])
            out.append(pltpu.make_async_remote_copy(src_ref=_cut(src, rows), dst_ref=_cut(dst, rows), **sems))
    return out


@_copies(4)
def _plan_gather_out(srcs, lands, send_sems, recv_sems, arrivals):
    x, y, c = _position()
    me = _dev_index(x, y, c)
    out = []
    for k, peer in enumerate([(x, y, 1 - c), (1 - x, y, c), (x, 1 - y, c), (1 - x, 1 - y, c)]):
        p = _dev_index(*peer)
        for i, (t, rows) in enumerate(_pieces(lands)):
            sems = dict(send_sem=send_sems.at[4 * i + k], recv_sem=recv_sems.at[4 * i + k], device_id=peer, device_id_type=MESH)
            src, dst = (lands[t].at[p], lands[t].at[p]) if arrivals else (srcs[t], lands[t].at[me])
            out.append(pltpu.make_async_remote_copy(src_ref=_cut(src, rows), dst_ref=_cut(dst, rows), **sems))
    return out


@_copies(3)
def _plan_gather_pass(srcs, lands, send_sems, recv_sems, arrivals):
    x, y, c = _position()
    sibling = (x, y, 1 - c)
    out = []
    for k, chip in enumerate([(1 - x, y), (x, 1 - y), (1 - x, 1 - y)]):
        p = _dev_index(*chip, 1 - c) if arrivals else _dev_index(*chip, c)
        for i, (t, rows) in enumerate(_pieces(lands)):
            sems = dict(send_sem=send_sems.at[3 * i + k], recv_sem=recv_sems.at[3 * i + k], device_id=sibling, device_id_type=MESH)
            block = _cut(lands[t].at[p], rows)
            out.append(pltpu.make_async_remote_copy(src_ref=block, dst_ref=block, **sems))
    return out


def _split_start(plan, srcs, lands, *, after=None, name):
    n_src, n = len(srcs), len(srcs) + len(lands)
    n_sem = plan.per_piece * len(_pieces(lands))
    order = [] if after is None else [after]

    def body(*refs):
        send_sems, recv_sems = refs[n + len(order):n + len(order) + 2]
        token = refs[-1]
        for cp in plan(refs[:n_src], refs[n_src:n], send_sems, recv_sems, arrivals=False):
            cp.start()
        token[...] = jnp.zeros_like(token)

    hbm = lambda a: pltpu.HBM(a.shape, a.dtype)
    outs = pl.pallas_call(
        body,
        name=name,
        in_specs=[_HBM] * n + [_ANY] * len(order),
        out_specs=[_SEM, _SEM] + [_HBM] * n + [pl.BlockSpec(memory_space=pltpu.VMEM)],
        out_shape=[pltpu.SemaphoreType.DMA((n_sem,)), pltpu.SemaphoreType.DMA((n_sem,))] + [hbm(a) for a in (*srcs, *lands)]
        + [jax.ShapeDtypeStruct(_TOKEN, F32)],
        input_output_aliases={i: 2 + i for i in range(n)},
        compiler_params=pltpu.CompilerParams(has_side_effects=pltpu.SideEffectType.DATAFLOW_SIDE_EFFECTING),
    )(*[pltpu.with_memory_space_constraint(a, pltpu.HBM) for a in (*srcs, *lands)], *order)
    return (outs[0], outs[1], outs[2:2 + n_src], outs[2 + n_src:2 + n]), outs[-1]


def _split_wait(plan, state, after, *, name):
    send_sems, recv_sems, srcs, lands = state
    n_src, n = len(srcs), len(srcs) + len(lands)

    def body(*refs):
        send_refs, recv_refs = refs[n:n + 2]
        for cp in plan(refs[:n_src], refs[n_src:n], send_refs, recv_refs, arrivals=False):
            cp.wait_send()
        for cp in plan(refs[:n_src], refs[n_src:n], send_refs, recv_refs, arrivals=True):
            cp.wait_recv()

    hbm = lambda a: pltpu.HBM(a.shape, a.dtype)
    outs = pl.pallas_call(
        body,
        name=name,
        in_specs=[_HBM] * n + [_SEM, _SEM, _ANY],
        out_specs=[_HBM] * n,
        out_shape=[hbm(a) for a in (*srcs, *lands)],
        input_output_aliases={i: i for i in range(n)},
        compiler_params=pltpu.CompilerParams(has_side_effects=pltpu.SideEffectType.DATAFLOW_SIDE_EFFECTING),
    )(*srcs, *lands, send_sems, recv_sems, after)
    return outs[n_src:]


def _sum_blocks(blocks, *, name):
    _, R, C = blocks.shape
    tm = next(R // n for n in (4, 3, 2, 1) if R % (8 * n) == 0)

    def body(b_ref, o_ref):
        g = b_ref[0]
        for dev in range(1, N_DEV):
            g = g + b_ref[dev]
        o_ref[...] = g

    return pl.pallas_call(
        body,
        name=name,
        grid=(R // tm,),
        in_specs=[pl.BlockSpec((N_DEV, tm, C), lambda i: (0, i, 0))],
        out_specs=pl.BlockSpec((tm, C), lambda i: (i, 0)),
        out_shape=jax.ShapeDtypeStruct((R, C), F32),
        compiler_params=_params(("parallel",)),
    )(blocks)


def _block_diag(w):
    out = jnp.zeros((POOL_W, POOL_W), w.dtype)
    for gi in range(4):
        out = out.at[64 * gi:64 * (gi + 1), 64 * gi:64 * (gi + 1)].set(w[gi])
    return out


def _layer_consts(sp, l):
    causal = jnp.tril(jnp.ones((SGU_CHUNK, SGU_CHUNK), F32))
    wm = (sp["sgu_w"][l] * causal[None]).astype(BF16)
    wbd = _block_diag(sp["pool_w"][l]).astype(BF16)
    return dict(
        wbd=wbd, wbd_t=wbd.T, wm=wm, wm_t=wm.transpose(0, 2, 1),
        sgu_bias=jnp.repeat(sp["sgu_b"][l].T, 64, axis=1),
        bpad=jnp.pad(sp["b_forget"][l], (0, F_LANES - FOX_H)).reshape(1, F_LANES),
        bg=sp["b_gate"][l].reshape(1, 3 * D),
    )


def _relu2(acc):
    return acc, jnp.square(jnp.maximum(acc, 0.0))


def _relu2_grad(acc, z):
    return (acc * 2.0 * jnp.maximum(z, 0.0),)


def _layer_fwd(l, x, mem, source, sp):
    S = x.shape[0]
    t = _tile(S, 256)
    c = _layer_consts(sp, l)
    n = f"l{l}_"
    W, after = source(l, "begin", x)
    h = _rms_fwd(x, sp["norm_mix_g"][l], after=after, name=n + "norm_mix")
    qkv = _mm(h, W["qkv"], out_dtypes=(BF16,), name=n + "qkv")
    rest = _mm(h, W["rest"], name=n + "rest")
    pa = _pool_fwd(rest, c["wbd"], sp["pool_scale"][l], name=n + "pool")
    cum, cum_t = _fox_prep(rest, c["bpad"], name=n + "fox_prep")
    fk3 = cum_t[:FOX_H].reshape(FOX_H, S // t, t)
    o, lse = _fox_fwd(qkv, cum, fk3, name=n + "fox")
    more, _ = source(l, "attended", o)
    W.update(more)
    sg = _sgu_fwd(rest, sp["sgu_norm_g"][l], c["wm"], c["sgu_bias"], name=n + "sgu")
    more, after = source(l, "mixed", sg)
    W.update(more)
    ya = _mm(pa, W["ba"], after=after, name=n + "branch_a")
    yb = _mm(o, W["bb"], name=n + "branch_b")
    yc = _mm(sg, W["bc"], name=n + "branch_c")
    merged = _merge_fwd(rest, c["bg"], ya, yb, yc, name=n + "merge")
    x1 = _mm(merged, W["out"], extras=(x,), epilogue=_add, name=n + "out")
    hx = _rms_fwd(x1, sp["norm_xattn_g"][l], name=n + "norm_xattn")
    hm = _rms_fwd(mem, sp["norm_mem_g"][l], name=n + "norm_mem")
    xq = _mm(hx, W["xq"], out_dtypes=(BF16,), name=n + "xq")
    kv = _mm(hm, W["xkv"], out_dtypes=(BF16,), name=n + "xkv")
    o2 = _xattn_fwd(xq, kv, name=n + "xattn")
    x2 = _mm(o2, W["xo"], extras=(x1,), epilogue=_add, name=n + "xo")
    hf = _rms_fwd(x2, sp["norm_ffn_g"][l], name=n + "norm_ffn")
    z, act = _mm(hf, W["ff1"], epilogue=_relu2, out_dtypes=(F32, BF16), name=n + "ff1")
    _, after = source(l, "expanded", act)
    x3 = _mm(act, W["ff2"], extras=(x2,), epilogue=_add, after=after, name=n + "ff2")
    saved = dict(x=x, h=h, qkv=qkv, rest=rest, pa=pa, cum=cum, fk3=fk3, o=o, lse=lse, sg=sg, ya=ya, yb=yb, yc=yc,
                 merged=merged, x1=x1, hx=hx, hm=hm, xq=xq, kv=kv, o2=o2, x2=x2, hf=hf, z=z, act=act, c=c)
    return x3, saved, W


def _layer_bwd(l, dx3, sv, mem, W, sp, grads_done):
    S = dx3.shape[0]
    c = sv["c"]
    n = f"l{l}b_"
    bf = dict(out_dtypes=(BF16,))
    gw, gs = {}, {}
    gw["ff2"] = _mm(sv["act"], dx3, ta=True, name=n + "dw_ff2", **bf)
    dz = _mm(dx3, W["ff2"], tb=True, extras=(sv["z"],), epilogue=_relu2_grad, name=n + "dz", **bf)
    gw["ff1"] = _mm(sv["hf"], dz, ta=True, shard_out=True, name=n + "dw_ff1", **bf)
    dhf = _mm(dz, W["ff1"], tb=True, name=n + "dhf")
    dx2, gs["norm_ffn_g"] = _rms_bwd(sv["x2"], sp["norm_ffn_g"][l], dhf, dx3, name=n + "dnorm_ffn")
    gw["xo"] = _mm(sv["o2"], dx2, ta=True, name=n + "dw_xo", **bf)
    do2 = _mm(dx2, W["xo"], tb=True, name=n + "do2", **bf)
    dxq, dkv = _xattn_bwd(sv["xq"], sv["kv"], do2, name=n + "dxattn")
    gw["xq"] = _mm(sv["hx"], dxq, ta=True, name=n + "dw_xq", **bf)
    gw["xkv"] = _mm(sv["hm"], dkv, ta=True, shard_out=True, name=n + "dw_xkv", **bf)
    dhm = _mm(dkv, W["xkv"], tb=True, name=n + "dhm")
    _, gs["norm_mem_g"] = _rms_bwd(mem, sp["norm_mem_g"][l], dhm, jnp.zeros_like(mem), name=n + "dnorm_mem")
    dhx = _mm(dxq, W["xq"], tb=True, name=n + "dhx")
    dx1, gs["norm_xattn_g"] = _rms_bwd(sv["x1"], sp["norm_xattn_g"][l], dhx, dx2, name=n + "dnorm_xattn")
    after, gw = grads_done(l, gw), {}
    gw["out"] = _mm(sv["merged"], dx1, ta=True, name=n + "dw_out", **bf)
    dm = _mm(dx1, W["out"], tb=True, after=after, name=n + "dmerged")
    dya, dyb, dyc, dg1, dg2, dg3, db1, db2, db3 = _merge_bwd(sv["rest"], c["bg"], sv["ya"], sv["yb"], sv["yc"], dm, name=n + "dmerge")
    gs["b_gate"] = jnp.concatenate([db1, db2, db3], axis=1).reshape(3 * D)
    gw["ba"] = _mm(sv["pa"], dya, ta=True, shard_out=True, name=n + "dw_ba", **bf)
    gw["bb"] = _mm(sv["o"], dyb, ta=True, shard_out=True, name=n + "dw_bb", **bf)
    gw["bc"] = _mm(sv["sg"], dyc, ta=True, shard_out=True, name=n + "dw_bc", **bf)
    after, gw = grads_done(l, gw), {}
    dpa = _mm(dya, W["ba"], tb=True, name=n + "dpa")
    do = _mm(dyb, W["bb"], tb=True, after=after, name=n + "do", **bf)
    dsg = _mm(dyc, W["bc"], tb=True, name=n + "dsg")
    da, dwbd, dscale = _pool_bwd(sv["rest"], c["wbd"], c["wbd_t"], sp["pool_scale"][l], dpa, name=n + "dpool")
    gs["pool_w"] = jnp.stack([dwbd[64 * gi:64 * (gi + 1), 64 * gi:64 * (gi + 1)] for gi in range(4)])
    gs["pool_scale"] = dscale.reshape(POOL_W)
    dq, dk, dv, dfq, dfk = _fox_bwd(sv["qkv"], sv["cum"], sv["fk3"], sv["o"], do, sv["lse"], name=n + "dfox")
    dcum = dfq + jnp.pad(dfk.reshape(FOX_H, S).T, ((0, 0), (0, F_LANES - FOX_H)))
    df, dbf = _fox_post(sv["rest"], c["bpad"], dcum, name=n + "dfox_post")
    gs["b_forget"] = dbf[0, :FOX_H]
    dc, dwm, dbias, dgn = _sgu_bwd(sv["rest"], sp["sgu_norm_g"][l], c["wm"], c["wm_t"], c["sgu_bias"], dsg, name=n + "dsgu")
    gs["sgu_w"] = dwm * jnp.tril(jnp.ones((SGU_CHUNK, SGU_CHUNK), F32))[None]
    gs["sgu_b"] = dbias.reshape(SGU_CHUNK, 4, 64).sum(axis=2).T
    gs["sgu_norm_g"] = dgn.reshape(SGU_W)
    dqkv = jnp.concatenate([dq, dk, dv], axis=1)
    drest = jnp.concatenate([da, df, jnp.zeros((S, OFF_C - OFF_F - F_LANES), BF16), dc, dg1, dg2, dg3], axis=1)
    gw["qkv"] = _mm(sv["h"], dqkv, ta=True, name=n + "dw_qkv", **bf)
    gw["rest"] = _mm(sv["h"], drest, ta=True, name=n + "dw_rest", **bf)
    after = grads_done(l, gw)
    dh = _mm(dqkv, W["qkv"], tb=True, after=after, name=n + "dh_qkv")
    dh = _mm(drest, W["rest"], tb=True, extras=(dh,), epilogue=_add, name=n + "dh")
    dx, gs["norm_mix_g"] = _rms_bwd(sv["x"], sp["norm_mix_g"][l], dh, dx1, name=n + "dnorm_mix")
    return dx, gs


def _local_step(x, mem, target, sp, source, grads_done):
    saved, Ws = [], []
    for l in range(DEPTH):
        x, sv, W = _layer_fwd(l, x, mem, source, sp)
        saved.append(sv)
        Ws.append(W)
    loss, dx, dgf = _final_loss(x, sp["final_norm_g"], target, name="final_loss")
    gss = [None] * DEPTH
    for l in reversed(range(DEPTH)):
        dx, gss[l] = _layer_bwd(l, dx, saved[l], mem, Ws[l], sp, grads_done)
    small = {k: jnp.stack([gss[l][k] for l in range(DEPTH)]) for k in gss[0]}
    small["final_norm_g"] = dgf
    return loss, dx, small


_SMALL = ["norm_mix_g", "b_forget", "pool_w", "pool_scale", "sgu_norm_g", "sgu_w", "sgu_b", "b_gate", "norm_xattn_g",
          "norm_mem_g", "norm_ffn_g", "final_norm_g"]
_COL = {"w_branch_a": "ba", "w_branch_b": "bb", "w_branch_c": "bc", "w_xkv": "xkv", "w_ff1": "ff1"}
_ROW = {"w_out": "out", "w_xq": "xq", "w_xo": "xo", "w_ff2": "ff2"}
_BIG = ["w_in", "w_branch_a", "w_branch_b", "w_branch_c", "w_out", "w_xq", "w_xkv", "w_xo", "w_ff1", "w_ff2"]
_PACK_LANES = 128


def _as_rows(a):
    return a.reshape(-1, a.shape[-1])


def _pack(tensors):
    rows = []
    for a in tensors:
        flat = a.reshape(-1)
        flat = jnp.pad(flat, (0, (-flat.shape[0]) % (8 * _PACK_LANES)))
        rows.append(flat.reshape(-1, _PACK_LANES))
    return jnp.concatenate(rows, axis=0)


def _unpack(packed, like):
    out, r = [], 0
    for a in like:
        size = math.prod(a.shape)
        nr = 8 * (-(-size // (8 * _PACK_LANES)))
        out.append(packed[r:r + nr].reshape(-1)[:size].reshape(a.shape))
        r += nr
    return out


_SHARD_IN = N_IN // N_DEV
_SHARD_IN_PAD = -(-_SHARD_IN // 128) * 128


def _columns(pieces, start, stop):
    out, at = [], 0
    for p in pieces:
        lo, hi = max(start, at), min(stop, at + p.shape[1])
        if lo < hi:
            out.append(p[:, lo - at:hi - at])
        at += p.shape[1]
    return out


def _split_w_in(blocks):
    K = blocks[0].shape[0]
    pad = jnp.zeros((K, OFF_C - OFF_F - FOX_H), blocks[0].dtype)
    cols = functools.partial(_columns, blocks)
    rest = jnp.concatenate(cols(0, R_OFF_Q) + cols(R_OFF_F, R_OFF_C) + [pad] + cols(R_OFF_C, N_IN), axis=1)
    return jnp.concatenate(cols(R_OFF_Q, R_OFF_F), axis=1), rest


def _join_w_in(qkv, rest):
    in_order = [rest[:, :R_OFF_Q], qkv, rest[:, OFF_F:OFF_F + FOX_H], rest[:, OFF_C:]]
    pad = jnp.zeros((qkv.shape[0], _SHARD_IN_PAD - _SHARD_IN), qkv.dtype)
    return jnp.stack([jnp.concatenate(_columns(in_order, _SHARD_IN * d, _SHARD_IN * (d + 1)) + [pad], axis=1) for d in range(N_DEV)])


_FIRST = ["w_in"]
_LATER = [k for k in _BIG if k not in _FIRST]


def _layer_weights(gathered):
    W = {}
    if "w_in" in gathered:
        W.update(zip(("qkv", "rest"), _split_w_in([gathered["w_in"][d][:, :_SHARD_IN] for d in range(N_DEV)])))
    for name, key in _COL.items():
        if name in gathered:
            W[key] = _Gathered(gathered[name])
    for name, key in _ROW.items():
        if name in gathered:
            W[key] = gathered[name].reshape(-1, gathered[name].shape[-1])
    return W


def _grad_blocks(gw):
    parts = {}
    if "qkv" in gw:
        parts["w_in"] = _join_w_in(gw["qkv"], gw["rest"])
    for name, key in _COL.items():
        if key in gw:
            parts[name] = gw[key]
    for name, key in _ROW.items():
        if key in gw:
            parts[name] = gw[key].reshape(N_DEV, -1, gw[key].shape[-1])
    return parts


def kernel(x, mem, norm_mix_g, w_in, b_forget, pool_w, pool_scale, sgu_norm_g, sgu_w, sgu_b, w_branch_a, w_branch_b, w_branch_c, b_gate, w_out, norm_xattn_g, norm_mem_g, w_xq, w_xkv, w_xo, norm_ffn_g, w_ff1, w_ff2, final_norm_g, loss_target, m_norm_mix_g, m_w_in, m_b_forget, m_pool_w, m_pool_scale, m_sgu_norm_g, m_sgu_w, m_sgu_b, m_w_branch_a, m_w_branch_b, m_w_branch_c, m_b_gate, m_w_out, m_norm_xattn_g, m_norm_mem_g, m_w_xq, m_w_xkv, m_w_xo, m_norm_ffn_g, m_w_ff1, m_w_ff2, m_final_norm_g, v_norm_mix_g, v_w_in, v_b_forget, v_pool_w, v_pool_scale, v_sgu_norm_g, v_sgu_w, v_sgu_b, v_w_branch_a, v_w_branch_b, v_w_branch_c, v_b_gate, v_w_out, v_norm_xattn_g, v_norm_mem_g, v_w_xq, v_w_xkv, v_w_xo, v_norm_ffn_g, v_w_ff1, v_w_ff2, v_final_norm_g):
    names = ["norm_mix_g", "w_in", "b_forget", "pool_w", "pool_scale", "sgu_norm_g", "sgu_w", "sgu_b", "w_branch_a", "w_branch_b",
             "w_branch_c", "b_gate", "w_out", "norm_xattn_g", "norm_mem_g", "w_xq", "w_xkv", "w_xo", "norm_ffn_g", "w_ff1", "w_ff2",
             "final_norm_g"]
    w = dict(zip(names, [norm_mix_g, w_in, b_forget, pool_w, pool_scale, sgu_norm_g, sgu_w, sgu_b, w_branch_a, w_branch_b, w_branch_c,
                         b_gate, w_out, norm_xattn_g, norm_mem_g, w_xq, w_xkv, w_xo, norm_ffn_g, w_ff1, w_ff2, final_norm_g]))
    m = dict(zip(names, [m_norm_mix_g, m_w_in, m_b_forget, m_pool_w, m_pool_scale, m_sgu_norm_g, m_sgu_w, m_sgu_b, m_w_branch_a,
                         m_w_branch_b, m_w_branch_c, m_b_gate, m_w_out, m_norm_xattn_g, m_norm_mem_g, m_w_xq, m_w_xkv, m_w_xo,
                         m_norm_ffn_g, m_w_ff1, m_w_ff2, m_final_norm_g]))
    v = dict(zip(names, [v_norm_mix_g, v_w_in, v_b_forget, v_pool_w, v_pool_scale, v_sgu_norm_g, v_sgu_w, v_sgu_b, v_w_branch_a,
                         v_w_branch_b, v_w_branch_c, v_b_gate, v_w_out, v_norm_xattn_g, v_norm_mem_g, v_w_xq, v_w_xkv, v_w_xo,
                         v_norm_ffn_g, v_w_ff1, v_w_ff2, v_final_norm_g]))

    sp = {k: w[k] for k in _SMALL}
    shards = [{k: w[k][l].astype(BF16) for k in _BIG} for l in range(DEPTH)]
    for sh in shards:
        sh["w_in"] = jnp.pad(sh["w_in"], ((0, 0), (0, _SHARD_IN_PAD - _SHARD_IN)))
    me = _dev_index(*_position())

    def gather_out(l, keys, name, after=None):
        srcs = [shards[l][k] for k in keys]
        lands = [_own_block_placed(a, jax.ShapeDtypeStruct((N_DEV, *a.shape), a.dtype)) for a in srcs]
        state, token = _split_start(_plan_gather_out, srcs, lands, after=after, name=name + "_out_start")
        return (keys, name, state), token

    def gather_pass(job, value):
        keys, name, state = job
        lands = _split_wait(_plan_gather_out, state, value, name=name + "_out_wait")
        state, token = _split_start(_plan_gather_pass, [], lands, name=name + "_pass_start")
        return (keys, name, state), token, lands[0]

    def gather_end(job, value):
        keys, name, state = job
        return _layer_weights(dict(zip(keys, _split_wait(_plan_gather_pass, state, value, name=name + "_pass_wait"))))

    jobs = {}

    def source(l, point, value):
        if (l, point) == (0, "begin"):
            first = _all_gather([shards[0][k] for k in _FIRST], name="gather_l0_first")
            jobs["l0"], token = gather_out(0, _LATER, "gather_l0", after=first[0])
            return _layer_weights(dict(zip(_FIRST, first))), token
        if (l, point) == (0, "attended"):
            jobs["l0"], _, arrived = gather_pass(jobs["l0"], value)
            jobs["l1"], jobs["token"] = gather_out(1, _BIG, "gather_l1", after=arrived)
            return {}, None
        if (l, point) == (0, "mixed"):
            return gather_end(jobs.pop("l0"), value), jobs.pop("token")
        if (l, point) == (0, "expanded"):
            jobs["l1"], token, _ = gather_pass(jobs["l1"], value)
            return {}, token
        if (l, point) == (1, "begin"):
            return gather_end(jobs.pop("l1"), value), None
        return {}, None

    received = [{} for _ in range(DEPTH)]
    travelling = []

    def grads_done(l, gw):
        blocks = _grad_blocks(gw)
        keys = [k for k in _BIG if k in blocks]
        parts = [blocks[k] for k in keys]
        group = f"exchange_grads_l{l}_" + ("in" if "w_in" in blocks else "merge" if "w_out" in blocks else "mlp")
        lands = [_own_block_placed(lax.dynamic_index_in_dim(p, me, 0, keepdims=False), p) for p in parts]
        state, token = _split_start(_plan_exchange, parts, lands, name=group + "_start")
        travelling.append((l, keys, state, group + "_wait"))
        return token

    loss, dx, small = _local_step(x[0], mem[0], loss_target[0], sp, source, grads_done)
    loss = lax.psum(loss[0, 0], ("x", "y", "c"))
    grads, deltas, new_m, new_v = {}, {}, {}, {}
    packed = _pack([small[k] for k in _SMALL])
    small_state, done = _split_start(_plan_broadcast, [packed], [_own_block_placed(packed, jax.ShapeDtypeStruct((N_DEV, *packed.shape), F32))],
                                     after=dx, name="small_grads_start")

    def update_small(after):
        like = [w[k] for k in _SMALL]
        blocks = _split_wait(_plan_broadcast, small_state, after, name="small_grads_wait")[0]
        g_small = _unpack(_sum_blocks(blocks, name="small_grads_sum"), like)
        rows = lambda d: [_as_rows(d[k]) for k in _SMALL]
        outs = _adamw_small([_as_rows(g) for g in g_small], rows(w), rows(m), rows(v), name="adamw_small")
        grads.update(zip(_SMALL, g_small))
        for dst, vals in zip((deltas, new_m, new_v), outs):
            dst.update({k: a.reshape(w[k].shape) for k, a in zip(_SMALL, vals)})
        return outs[0][0]

    groups = list(dict.fromkeys(tuple(keys) for _, keys, _, _ in travelling))
    for group_keys in groups:
        if group_keys == groups[-1]:
            done = update_small(done)
        for l, keys, state, wait_name in travelling:
            if tuple(keys) == group_keys:
                received[l].update(zip(keys, _split_wait(_plan_exchange, state, done, name=wait_name)))
        for k in group_keys:
            outs = _adamw_sharded([received[l][k] for l in range(DEPTH)], w[k], m[k], v[k], name="adamw_" + k)
            grads[k], deltas[k], new_m[k], new_v[k] = outs
        done = grads[group_keys[-1]]

    return (loss, dx[None], *[grads[k] for k in names], *[deltas[k] for k in names], *[new_m[k] for k in names],
            *[new_v[k] for k in names])
```

```python
import functools
import math

import jax
import jax.numpy as jnp
from jax import lax
from jax.experimental import pallas as pl
from jax.experimental.pallas import tpu as pltpu

F32 = jnp.float32
BF16 = jnp.bfloat16
MESH = pl.DeviceIdType.MESH

N_DEV = 8
D = 1024
DEPTH = 2
EPS = 1e-6
NEG = -1e30
POOL_W = 256
FOX_H = 8
FOX_DH = 64
FOX_W = 512
SGU_W = 256
SGU_CHUNK = 128
XH = 4
XDH = 256
N_IN = 5384
R_OFF_Q, R_OFF_F, R_OFF_C = 256, 1792, 1800
QKV_W = 3 * FOX_W
OFF_A, OFF_F, OFF_C, OFF_G, REST_W = 0, 256, 512, 1024, 4096
F_LANES = 128

ADAM_LR = 0.001
ADAM_B1 = 0.9
ADAM_B2 = 0.999
ADAM_EPS = 1e-08
ADAM_WD = 0.01
ADAM_STEP = 10

VMEM_LIMIT = 56 * 1024 * 1024


def _tile(n, pref):
    t = min(n, pref)
    while n % t:
        t -= 128
    assert t > 0, (n, pref)
    return t


def _params(sem=None):
    return pltpu.CompilerParams(dimension_semantics=sem, vmem_limit_bytes=VMEM_LIMIT)


def _dot(a, b, ca, cb):
    return lax.dot_general(a, b, (((ca,), (cb,)), ((), ())), preferred_element_type=F32)


def _sigmoid(z):
    return 1.0 / (1.0 + jnp.exp(-z))


_GELU_K = math.sqrt(2.0 / math.pi)
_GELU_C = 0.044715


def _gelu(x):
    return 0.5 * x * (1.0 + jnp.tanh(_GELU_K * (x + _GELU_C * x * x * x)))


def _gelu_grad(x):
    t = jnp.tanh(_GELU_K * (x + _GELU_C * x * x * x))
    return 0.5 * (1.0 + t) + 0.5 * x * (1.0 - t * t) * _GELU_K * (1.0 + 3.0 * _GELU_C * x * x)


def _rows(shape):
    return lax.broadcasted_iota(jnp.int32, shape, 0)


def _lanes(shape):
    return lax.broadcasted_iota(jnp.int32, shape, 1)


class _Gathered:
    def __init__(self, arr):
        self.arr = arr
        self.shape = (arr.shape[1], N_DEV * arr.shape[2])


_TOKEN = (8, 128)


def _mm(a, b, *, ta=False, tb=False, extras=(), epilogue=None, out_dtypes=(F32,), shard_out=False, after=None, tm=None, tn=512, tk=None,
        name):
    M, K = (a.shape[1], a.shape[0]) if ta else a.shape
    N, Kb = b.shape if tb else b.shape[::-1]
    assert Kb == K, (a.shape, b.shape, ta, tb)
    gathered = isinstance(b, _Gathered)
    if gathered:
        if tb:
            tk = b.arr.shape[2]
        else:
            tn = b.arr.shape[2]
    if shard_out:
        tn = N // N_DEV
    tm = _tile(M, tm or (1024 if ta else 2048))
    tn = _tile(N, tn)
    size = lambda dt: jnp.dtype(dt).itemsize
    row_bytes = tm * size(a.dtype) + tn * size(b.arr.dtype if gathered else b.dtype)
    tile_bytes = tm * tn * (sum(size(e.dtype) for e in extras) + sum(map(size, out_dtypes)))

    def vmem_bytes(k_tile):
        return 2 * (k_tile * row_bytes + tile_bytes) + tm * tn * 4 * (K > k_tile)

    if tk is None:
        tk = next(c for c in (_tile(K, 2048), _tile(K, 1024), _tile(K, 512), _tile(K, 256)) if vmem_bytes(c) <= VMEM_LIMIT - (4 << 20))
    tk = _tile(K, tk)
    nk = K // tk
    ca, cb = (0 if ta else 1), (1 if tb else 0)
    n_ex, n_out = len(extras), len(out_dtypes)
    tokens = [] if after is None else [after]
    n_in = 2 + n_ex + len(tokens)
    if epilogue is None:
        epilogue = lambda acc: (acc,)

    def body(*refs):
        a_ref, b_ref = refs[:2]
        ex_refs = refs[2:2 + n_ex]
        o_refs = refs[n_in:n_in + n_out]
        part = _dot(a_ref[...].astype(BF16), b_ref[...].astype(BF16), ca, cb)

        def finish(acc):
            for o_ref, val in zip(o_refs, epilogue(acc, *[e[...] for e in ex_refs])):
                o_ref[...] = val.astype(o_ref.dtype)

        if nk == 1:
            finish(part)
        else:
            acc_ref = refs[-1]
            k = pl.program_id(2)

            @pl.when(k == 0)
            def _():
                acc_ref[...] = part

            @pl.when(k > 0)
            def _():
                acc_ref[...] += part

            @pl.when(k == nk - 1)
            def _():
                finish(acc_ref[...])

    a_spec = pl.BlockSpec((tk, tm), lambda i, j, k: (k, i)) if ta else pl.BlockSpec((tm, tk), lambda i, j, k: (i, k))
    if not gathered:
        b_arr = b
        b_spec = pl.BlockSpec((tn, tk), lambda i, j, k: (j, k)) if tb else pl.BlockSpec((tk, tn), lambda i, j, k: (k, j))
    else:
        b_arr = b.arr
        if tb:
            b_spec = pl.BlockSpec((None, tn, tk), lambda i, j, k: (k, j, 0))
        else:
            b_spec = pl.BlockSpec((None, tk, tn), lambda i, j, k: (j, k, 0))
    tile = pl.BlockSpec((tm, tn), lambda i, j, k: (i, j))
    if shard_out:
        out_specs = [pl.BlockSpec((None, tm, tn), lambda i, j, k: (j, i, 0))] * n_out
        out_shape = [jax.ShapeDtypeStruct((N_DEV, M, tn), dt) for dt in out_dtypes]
    else:
        out_specs = [tile] * n_out
        out_shape = [jax.ShapeDtypeStruct((M, N), dt) for dt in out_dtypes]
    assert vmem_bytes(tk) <= VMEM_LIMIT - (4 << 20), (name, vmem_bytes(tk))
    outs = pl.pallas_call(
        body,
        name=name,
        grid=(M // tm, N // tn, nk),
        in_specs=[a_spec, b_spec] + [tile] * n_ex + [pl.BlockSpec(_TOKEN, lambda i, j, k: (0, 0))] * len(tokens),
        out_specs=out_specs,
        out_shape=out_shape,
        scratch_shapes=[pltpu.VMEM((tm, tn), F32)] if nk > 1 else [],
        compiler_params=_params(("parallel", "parallel", "arbitrary")),
    )(a, b_arr, *extras, *tokens)
    return outs[0] if n_out == 1 else outs


def _add(acc, res):
    return (acc + res,)


def _rms_fwd(x, g, *, after=None, name):
    R, C = x.shape
    tm = _tile(R, 256)
    tokens = [] if after is None else [after]

    def body(x_ref, g_ref, *rest):
        xv = x_ref[...]
        r = lax.rsqrt(jnp.mean(xv * xv, axis=-1, keepdims=True) + EPS)
        rest[-1][...] = (xv * r * g_ref[...]).astype(BF16)

    return pl.pallas_call(
        body,
        name=name,
        grid=(R // tm,),
        in_specs=[pl.BlockSpec((tm, C), lambda i: (i, 0)), pl.BlockSpec((1, C), lambda i: (0, 0))]
        + [pl.BlockSpec(_TOKEN, lambda i: (0, 0))] * len(tokens),
        out_specs=pl.BlockSpec((tm, C), lambda i: (i, 0)),
        out_shape=jax.ShapeDtypeStruct((R, C), BF16),
        compiler_params=_params(("parallel",)),
    )(x, g.reshape(1, C), *tokens)


def _rms_bwd(x, g, dh, dres, *, name):
    R, C = x.shape
    tm = _tile(R, 256)

    def body(x_ref, g_ref, dh_ref, dres_ref, dx_ref, dg_ref):
        xv = x_ref[...]
        r = lax.rsqrt(jnp.mean(xv * xv, axis=-1, keepdims=True) + EPS)
        xn = xv * r
        dh_v = dh_ref[...].astype(F32)
        dxn = dh_v * g_ref[...]
        dx_ref[...] = r * (dxn - xn * jnp.mean(dxn * xn, axis=-1, keepdims=True)) + dres_ref[...]
        part = jnp.sum(dh_v * xn, axis=0, keepdims=True)

        @pl.when(pl.program_id(0) == 0)
        def _():
            dg_ref[...] = part

        @pl.when(pl.program_id(0) > 0)
        def _():
            dg_ref[...] += part

    row = pl.BlockSpec((tm, C), lambda i: (i, 0))
    vec = pl.BlockSpec((1, C), lambda i: (0, 0))
    dx, dg = pl.pallas_call(
        body,
        name=name,
        grid=(R // tm,),
        in_specs=[row, vec, row, row],
        out_specs=[row, vec],
        out_shape=[jax.ShapeDtypeStruct((R, C), F32), jax.ShapeDtypeStruct((1, C), F32)],
        compiler_params=_params(("arbitrary",)),
    )(x, g.reshape(1, C), dh, dres)
    return dx, dg.reshape(C)


def _final_loss(x, g, target, *, name):
    R, C = x.shape
    tm = _tile(R, 256)

    def body(x_ref, g_ref, t_ref, loss_ref, dx_ref, dg_ref):
        xv = x_ref[...]
        r = lax.rsqrt(jnp.mean(xv * xv, axis=-1, keepdims=True) + EPS)
        xn = xv * r
        gv = g_ref[...]
        err = xn * gv - t_ref[...]
        lpart = (0.5 / C) * jnp.sum(jnp.sum(err * err, axis=1, keepdims=True), axis=0, keepdims=True)
        dy = err * (1.0 / C)
        dxn = dy * gv
        dx_ref[...] = r * (dxn - xn * jnp.mean(dxn * xn, axis=-1, keepdims=True))
        gpart = jnp.sum(dy * xn, axis=0, keepdims=True)

        @pl.when(pl.program_id(0) == 0)
        def _():
            loss_ref[...] = lpart
            dg_ref[...] = gpart

        @pl.when(pl.program_id(0) > 0)
        def _():
            loss_ref[...] += lpart
            dg_ref[...] += gpart

    row = pl.BlockSpec((tm, C), lambda i: (i, 0))
    vec = pl.BlockSpec((1, C), lambda i: (0, 0))
    loss, dx, dg = pl.pallas_call(
        body,
        name=name,
        grid=(R // tm,),
        in_specs=[row, vec, row],
        out_specs=[pl.BlockSpec((1, 1), lambda i: (0, 0)), row, vec],
        out_shape=[jax.ShapeDtypeStruct((1, 1), F32), jax.ShapeDtypeStruct((R, C), F32), jax.ShapeDtypeStruct((1, C), F32)],
        compiler_params=_params(("arbitrary",)),
    )(x, g.reshape(1, C), target)
    return loss, dx, dg.reshape(C)


def _pool_select(lane, vals):
    out = vals[3]
    for gi in (2, 1, 0):
        out = jnp.where(lane < 64 * (gi + 1), vals[gi], out)
    return out


def _pool_diff(a):
    row, lane = _rows(a.shape), _lanes(a.shape)

    def down(v, k):
        return jnp.where(row >= k, pltpu.roll(v, k, 0), 0.0)

    s2 = a + down(a, 1)
    s4 = s2 + down(s2, 2)
    s8 = s4 + down(s4, 4)
    s16 = s8 + down(s8, 8)
    wsum = _pool_select(lane, (s2, s4, s8, s16))
    win = _pool_select(lane, (2, 4, 8, 16))
    cnt = jnp.minimum(row + 1, win).astype(F32)
    return wsum / cnt - a, cnt


def _pool_diff_t(dd, cnt):
    S = dd.shape[0]
    row, lane = _rows(dd.shape), _lanes(dd.shape)

    def up(v, k):
        return jnp.where(row < S - k, pltpu.roll(v, S - k, 0), 0.0)

    e = dd / cnt
    s2 = e + up(e, 1)
    s4 = s2 + up(s2, 2)
    s8 = s4 + up(s4, 4)
    s16 = s8 + up(s8, 8)
    return _pool_select(lane, (s2, s4, s8, s16)) - dd


def _pool_fwd(rest, wbd, scale, *, name):
    S = rest.shape[0]

    def body(a_ref, w_ref, s_ref, o_ref):
        d, _ = _pool_diff(a_ref[...])
        yp = _dot(d.astype(BF16), w_ref[...], 1, 0)
        o_ref[...] = (yp * s_ref[...]).astype(BF16)

    return pl.pallas_call(
        body,
        name=name,
        grid=(1,),
        in_specs=[
            pl.BlockSpec((S, POOL_W), lambda i: (0, OFF_A // POOL_W)),
            pl.BlockSpec((POOL_W, POOL_W), lambda i: (0, 0)),
            pl.BlockSpec((1, POOL_W), lambda i: (0, 0)),
        ],
        out_specs=pl.BlockSpec((S, POOL_W), lambda i: (0, 0)),
        out_shape=jax.ShapeDtypeStruct((S, POOL_W), BF16),
        compiler_params=_params(("arbitrary",)),
    )(rest, wbd, scale.reshape(1, POOL_W))


def _pool_bwd(rest, wbd, wbd_t, scale, dpa, *, name):
    S = rest.shape[0]

    def body(a_ref, w_ref, wt_ref, s_ref, dpa_ref, da_ref, dw_ref, ds_ref):
        d, cnt = _pool_diff(a_ref[...])
        db = d.astype(BF16)
        yp = _dot(db, w_ref[...], 1, 0)
        dpa_v = dpa_ref[...]
        ds_ref[...] = jnp.sum(dpa_v * yp, axis=0, keepdims=True)
        dyp = (dpa_v * s_ref[...]).astype(BF16)
        dw_ref[...] = _dot(db, dyp, 0, 0)
        dd = _dot(dyp, wt_ref[...], 1, 0)
        da_ref[...] = _pool_diff_t(dd, cnt).astype(BF16)

    full = pl.BlockSpec((S, POOL_W), lambda i: (0, 0))
    sq = pl.BlockSpec((POOL_W, POOL_W), lambda i: (0, 0))
    vec = pl.BlockSpec((1, POOL_W), lambda i: (0, 0))
    return pl.pallas_call(
        body,
        name=name,
        grid=(1,),
        in_specs=[pl.BlockSpec((S, POOL_W), lambda i: (0, OFF_A // POOL_W)), sq, sq, vec, full],
        out_specs=[full, sq, vec],
        out_shape=[
            jax.ShapeDtypeStruct((S, POOL_W), BF16),
            jax.ShapeDtypeStruct((POOL_W, POOL_W), F32),
            jax.ShapeDtypeStruct((1, POOL_W), F32),
        ],
        compiler_params=_params(("arbitrary",)),
    )(rest, wbd, wbd_t, scale.reshape(1, POOL_W), dpa)


def _log_sigmoid(z):
    return jnp.minimum(z, 0.0) - jnp.log(1.0 + jnp.exp(-jnp.abs(z)))


_F_SPEC_COL = OFF_F // F_LANES


def _fox_prep(rest, bpad, *, name):
    S = rest.shape[0]

    def body(f_ref, b_ref, o_ref, ot_ref):
        acc = _log_sigmoid(f_ref[...] + b_ref[...])
        row = _rows(acc.shape)
        k = 1
        while k < S:
            acc = acc + jnp.where(row >= k, pltpu.roll(acc, k, 0), 0.0)
            k *= 2
        o_ref[...] = acc
        ot_ref[...] = acc.T

    return pl.pallas_call(
        body,
        name=name,
        grid=(1,),
        in_specs=[pl.BlockSpec((S, F_LANES), lambda i: (0, _F_SPEC_COL)), pl.BlockSpec((1, F_LANES), lambda i: (0, 0))],
        out_specs=[pl.BlockSpec((S, F_LANES), lambda i: (0, 0)), pl.BlockSpec((F_LANES, S), lambda i: (0, 0))],
        out_shape=[jax.ShapeDtypeStruct((S, F_LANES), F32), jax.ShapeDtypeStruct((F_LANES, S), F32)],
        compiler_params=_params(("arbitrary",)),
    )(rest, bpad)


def _fox_post(rest, bpad, dcum, *, name):
    S = rest.shape[0]

    def body(f_ref, b_ref, d_ref, df_ref, db_ref):
        acc = d_ref[...]
        row = _rows(acc.shape)
        k = 1
        while k < S:
            acc = acc + jnp.where(row < S - k, pltpu.roll(acc, S - k, 0), 0.0)
            k *= 2
        df = acc * (1.0 - _sigmoid(f_ref[...] + b_ref[...]))
        df_ref[...] = df.astype(BF16)
        db_ref[...] = jnp.sum(df, axis=0, keepdims=True)

    full = pl.BlockSpec((S, F_LANES), lambda i: (0, 0))
    vec = pl.BlockSpec((1, F_LANES), lambda i: (0, 0))
    return pl.pallas_call(
        body,
        name=name,
        grid=(1,),
        in_specs=[pl.BlockSpec((S, F_LANES), lambda i: (0, _F_SPEC_COL)), vec, full],
        out_specs=[full, vec],
        out_shape=[jax.ShapeDtypeStruct((S, F_LANES), BF16), jax.ShapeDtypeStruct((1, F_LANES), F32)],
        compiler_params=_params(("arbitrary",)),
    )(rest, bpad, dcum)


_FOX_SCALE = FOX_DH ** -0.5
_PAIRS = FOX_H // 2


def _scaled(v):
    return (v.astype(F32) * _FOX_SCALE).astype(BF16)


def _diag_mask(s):
    return jnp.where(_rows(s.shape) >= _lanes(s.shape), s, NEG)


def _fox_fwd(qkv, cum, fk3, *, name):
    S = qkv.shape[0]
    nk, t = fk3.shape[1:]

    def body(q_ref, k_ref, v_ref, cum_ref, fk_ref, o_ref, lse_ref):
        i = pl.program_id(0)
        lane = _lanes((t, 128))
        lo = lane < FOX_DH
        cumv = cum_ref[...]
        qm, fq = [], []
        for h in range(FOX_H):
            qs = _scaled(q_ref[:, 128 * (h // 2):128 * (h // 2 + 1)])
            zero = jnp.zeros_like(qs)
            qm.append(jnp.where(lo, qs, zero) if h % 2 == 0 else jnp.where(lo, zero, qs))
            fq.append(cumv[:, h:h + 1])

        def tile(j, state, masked):
            m, l, acc = (list(part) for part in state)
            k0 = pl.multiple_of(j * t, t)
            for hp in range(_PAIRS):
                cols = slice(128 * hp, 128 * (hp + 1))
                kb = k_ref[pl.ds(k0, t), cols]
                vb = v_ref[pl.ds(k0, t), cols]
                alphas, pvs = [], []
                for h in (2 * hp, 2 * hp + 1):
                    s = _dot(qm[h], kb, 1, 1) + fq[h] - fk_ref[h, pl.ds(j, 1), :]
                    if masked:
                        s = _diag_mask(s)
                    m_new = jnp.maximum(m[h], jnp.max(s, axis=-1, keepdims=True))
                    p = jnp.exp(s - m_new)
                    alpha = jnp.exp(m[h] - m_new)
                    l[h] = alpha * l[h] + jnp.sum(p, axis=-1, keepdims=True)
                    m[h] = m_new
                    alphas.append(alpha)
                    pvs.append(_dot(p.astype(BF16), vb, 1, 0))
                acc[hp] = jnp.where(lo, alphas[0], alphas[1]) * acc[hp] + jnp.where(lo, pvs[0], pvs[1])
            return tuple(m), tuple(l), tuple(acc)

        init = ((jnp.full((t, 1), NEG, F32),) * FOX_H, (jnp.zeros((t, 1), F32),) * FOX_H, (jnp.zeros((t, 128), F32),) * _PAIRS)
        state = lax.fori_loop(0, i, functools.partial(tile, masked=False), init)
        m, l, acc = tile(i, state, True)
        for hp in range(_PAIRS):
            o_ref[:, 128 * hp:128 * (hp + 1)] = acc[hp] / jnp.where(lo, l[2 * hp], l[2 * hp + 1])
            lse = [m[h] + jnp.log(l[h]) for h in (2 * hp, 2 * hp + 1)]
            lse_ref[hp] = jnp.where(lane == 0, lse[0], jnp.where(lane == 1, lse[1], 0.0))

    whole = lambda col: pl.BlockSpec((S, FOX_W), lambda i: (0, col))
    return pl.pallas_call(
        body,
        name=name,
        grid=(S // t,),
        in_specs=[
            pl.BlockSpec((t, FOX_W), lambda i: (i, 0)), whole(1), whole(2),
            pl.BlockSpec((t, F_LANES), lambda i: (i, 0)),
            pl.BlockSpec((FOX_H, nk, t), lambda i: (0, 0, 0)),
        ],
        out_specs=[pl.BlockSpec((t, FOX_W), lambda i: (i, 0)), pl.BlockSpec((_PAIRS, t, 128), lambda i: (0, i, 0))],
        out_shape=[jax.ShapeDtypeStruct((S, FOX_W), F32), jax.ShapeDtypeStruct((_PAIRS, S, 128), F32)],
        compiler_params=_params(("arbitrary",)),
    )(qkv, qkv, qkv, cum, fk3)


def _fox_bwd(qkv, cum, fk3, o, do, lse, *, name):
    S = qkv.shape[0]
    nk, t = fk3.shape[1:]
    q_at, k_at, v_at = 0, FOX_W, 2 * FOX_W

    def body(qkv_ref, cum_ref, fk_ref, o_ref, do_ref, lse_ref, dq_ref, dk_ref, dv_ref, dfq_ref, dfk_ref,
             qs_sc, ks_sc, delta_sc, dq_sc):
        lane = _lanes((t, 128))
        lo = lane < FOX_DH
        mine = lambda h: lo if h % 2 == 0 else jnp.logical_not(lo)

        def by_head(tile, values):
            for h, val in enumerate(values):
                tile = jnp.where(lane == h, val, tile)
            return tile

        def prep(i, carry):
            r = pl.ds(pl.multiple_of(i * t, t), t)
            qs_sc[r, :] = _scaled(qkv_ref[r, q_at:q_at + FOX_W])
            ks_sc[r, :] = _scaled(qkv_ref[r, k_at:k_at + FOX_W])
            sums = []
            for hp in range(_PAIRS):
                cols = slice(128 * hp, 128 * (hp + 1))
                prod = do_ref[r, cols].astype(F32) * o_ref[r, cols]
                sums += [jnp.sum(jnp.where(mine(h), prod, 0.0), axis=-1, keepdims=True) for h in (2 * hp, 2 * hp + 1)]
            delta_sc[r, :] = by_head(jnp.zeros((t, 128), F32), sums)
            dfq_ref[r, :] = jnp.zeros((t, 128), F32)
            dq_sc[r, :] = jnp.zeros((t, FOX_W), F32)
            return carry

        lax.fori_loop(0, nk, prep, 0)

        def kv_tile(j, carry):
            kr = pl.ds(pl.multiple_of(j * t, t), t)

            def q_tile(i, acc, masked):
                dk, dv, dfk = list(acc[:_PAIRS]), list(acc[_PAIRS:2 * _PAIRS]), list(acc[2 * _PAIRS:])
                qr = pl.ds(pl.multiple_of(i * t, t), t)
                delta_t, cum_t, dq_old, dfq_old = delta_sc[qr, :], cum_ref[qr, :], dq_sc[qr, :], dfq_ref[qr, :]
                row_sums, dq_new = [], []
                for hp in range(_PAIRS):
                    cols = slice(128 * hp, 128 * (hp + 1))
                    kb = qkv_ref[kr, k_at + 128 * hp:k_at + 128 * (hp + 1)]
                    vb = qkv_ref[kr, v_at + 128 * hp:v_at + 128 * (hp + 1)]
                    ksb, qsb, dob = ks_sc[kr, cols], qs_sc[qr, cols], do_ref[qr, cols]
                    zero = jnp.zeros_like(qsb)
                    dq_t = jnp.zeros((t, 128), F32)
                    for h in (2 * hp, 2 * hp + 1):
                        qe, doe, ke = (jnp.where(mine(h), a, zero) for a in (qsb, dob, ksb))
                        s = _dot(qe, kb, 1, 1) + cum_t[:, h:h + 1] - fk_ref[h, pl.ds(j, 1), :]
                        if masked:
                            s = _diag_mask(s)
                        p = jnp.exp(s - lse_ref[hp, qr, h % 2:h % 2 + 1])
                        dv[hp] = dv[hp] + _dot(p.astype(BF16), doe, 0, 0)
                        dp = _dot(doe, vb, 1, 1)
                        ds = p * (dp - delta_t[:, h:h + 1])
                        dsb = ds.astype(BF16)
                        dk[hp] = dk[hp] + _dot(dsb, qe, 0, 0)
                        dq_t = dq_t + _dot(dsb, ke, 1, 0)
                        row_sums.append(jnp.sum(ds, axis=-1, keepdims=True))
                        dfk[h] = dfk[h] - jnp.sum(ds, axis=0, keepdims=True)
                    dq_new.append(dq_old[:, cols] + dq_t)
                for hp in range(_PAIRS):
                    dq_sc[qr, 128 * hp:128 * (hp + 1)] = dq_new[hp]
                dfq_ref[qr, :] = dfq_old + by_head(jnp.zeros((t, 128), F32), row_sums)
                return (*dk, *dv, *dfk)

            init = tuple([jnp.zeros((t, 128), F32)] * (2 * _PAIRS) + [jnp.zeros((1, t), F32)] * FOX_H)
            acc = q_tile(j, init, True)
            acc = lax.fori_loop(j + 1, nk, functools.partial(q_tile, masked=False), acc)
            for hp in range(_PAIRS):
                cols = slice(128 * hp, 128 * (hp + 1))
                dk_ref[kr, cols] = acc[hp].astype(BF16)
                dv_ref[kr, cols] = acc[_PAIRS + hp].astype(BF16)
            for h in range(FOX_H):
                dfk_ref[h, pl.ds(j, 1), :] = acc[2 * _PAIRS + h]
            return carry

        lax.fori_loop(0, nk, kv_tile, 0)
        dq_ref[...] = dq_sc[...].astype(BF16)

    vm = pl.BlockSpec(memory_space=pltpu.VMEM)
    big = jax.ShapeDtypeStruct((S, FOX_W), BF16)
    return pl.pallas_call(
        body,
        name=name,
        in_specs=[vm] * 6,
        out_specs=[vm] * 5,
        out_shape=[big, big, big, jax.ShapeDtypeStruct((S, 128), F32), jax.ShapeDtypeStruct((FOX_H, nk, t), F32)],
        scratch_shapes=[pltpu.VMEM((S, FOX_W), BF16), pltpu.VMEM((S, FOX_W), BF16), pltpu.VMEM((S, 128), F32),
                        pltpu.VMEM((S, FOX_W), F32)],
        compiler_params=pltpu.CompilerParams(vmem_limit_bytes=VMEM_LIMIT),
    )(qkv, cum, fk3, o, do, lse)


def _group_mask(lane, gi):
    return (lane >= 64 * gi) & (lane < 64 * (gi + 1))


_U_COL = OFF_C // SGU_W


def _sgu_fwd(rest, gn, wm, bias, *, name):
    S = rest.shape[0]
    ts = _tile(S, 512)
    nc = ts // SGU_CHUNK

    def body(u_ref, v_ref, g_ref, w_ref, b_ref, o_ref):
        zv = _gelu(v_ref[...])
        vn = zv * lax.rsqrt(jnp.mean(zv * zv, axis=-1, keepdims=True) + EPS) * g_ref[...]
        lane = _lanes((SGU_CHUNK, SGU_W))
        for c in range(nc):
            rows = slice(c * SGU_CHUNK, (c + 1) * SGU_CHUNK)
            vcb = vn[rows].astype(BF16)
            mixed = b_ref[...]
            for gi in range(4):
                mixed = mixed + jnp.where(_group_mask(lane, gi), _dot(w_ref[gi], vcb, 1, 0), 0.0)
            o_ref[rows, :] = (_gelu(u_ref[rows, :]) * mixed).astype(BF16)

    return pl.pallas_call(
        body,
        name=name,
        grid=(S // ts,),
        in_specs=[
            pl.BlockSpec((ts, SGU_W), lambda i: (i, _U_COL)),
            pl.BlockSpec((ts, SGU_W), lambda i: (i, _U_COL + 1)),
            pl.BlockSpec((1, SGU_W), lambda i: (0, 0)),
            pl.BlockSpec((4, SGU_CHUNK, SGU_CHUNK), lambda i: (0, 0, 0)),
            pl.BlockSpec((SGU_CHUNK, SGU_W), lambda i: (0, 0)),
        ],
        out_specs=pl.BlockSpec((ts, SGU_W), lambda i: (i, 0)),
        out_shape=jax.ShapeDtypeStruct((S, SGU_W), BF16),
        compiler_params=_params(("parallel",)),
    )(rest, rest, gn.reshape(1, SGU_W), wm, bias)


def _sgu_bwd(rest, gn, wm, wm_t, bias, dsg, *, name):
    S = rest.shape[0]
    ts = _tile(S, 512)
    nc = ts // SGU_CHUNK

    def body(u_ref, v_ref, g_ref, w_ref, wt_ref, b_ref, dsg_ref, dc_ref, dw_ref, db_ref, dg_ref):
        first = pl.program_id(0) == 0

        @pl.when(first)
        def _():
            dw_ref[...] = jnp.zeros_like(dw_ref)
            db_ref[...] = jnp.zeros_like(db_ref)
            dg_ref[...] = jnp.zeros_like(dg_ref)

        gv = g_ref[...]
        lane = _lanes((SGU_CHUNK, SGU_W))
        for c in range(nc):
            rows = slice(c * SGU_CHUNK, (c + 1) * SGU_CHUNK)
            vpre = v_ref[rows, :]
            upre = u_ref[rows, :]
            zv = _gelu(vpre)
            r = lax.rsqrt(jnp.mean(zv * zv, axis=-1, keepdims=True) + EPS)
            zn = zv * r
            vcb = (zn * gv).astype(BF16)
            mixed = b_ref[...]
            for gi in range(4):
                mixed = mixed + jnp.where(_group_mask(lane, gi), _dot(w_ref[gi], vcb, 1, 0), 0.0)
            zu = _gelu(upre)
            dsg_v = dsg_ref[rows, :]
            dc_ref[rows, :SGU_W] = (dsg_v * mixed * _gelu_grad(upre)).astype(BF16)
            dmixed = dsg_v * zu
            db_ref[...] += dmixed
            dvn = jnp.zeros((SGU_CHUNK, SGU_W), F32)
            for gi in range(4):
                dmg = jnp.where(_group_mask(lane, gi), dmixed, 0.0).astype(BF16)
                dw_ref[gi] += _dot(dmg, vcb, 1, 1)
                dvn = dvn + _dot(wt_ref[gi], dmg, 1, 0)
            dg_ref[...] += jnp.sum(dvn * zn, axis=0, keepdims=True)
            dzn = dvn * gv
            dzv = r * (dzn - zn * jnp.mean(dzn * zn, axis=-1, keepdims=True))
            dc_ref[rows, SGU_W:] = (dzv * _gelu_grad(vpre)).astype(BF16)

    blk = pl.BlockSpec((ts, SGU_W), lambda i: (i, 0))
    vec = pl.BlockSpec((1, SGU_W), lambda i: (0, 0))
    w3 = pl.BlockSpec((4, SGU_CHUNK, SGU_CHUNK), lambda i: (0, 0, 0))
    bsp = pl.BlockSpec((SGU_CHUNK, SGU_W), lambda i: (0, 0))
    return pl.pallas_call(
        body,
        name=name,
        grid=(S // ts,),
        in_specs=[
            pl.BlockSpec((ts, SGU_W), lambda i: (i, _U_COL)),
            pl.BlockSpec((ts, SGU_W), lambda i: (i, _U_COL + 1)),
            vec, w3, w3, bsp, blk,
        ],
        out_specs=[pl.BlockSpec((ts, 2 * SGU_W), lambda i: (i, 0)), w3, bsp, vec],
        out_shape=[
            jax.ShapeDtypeStruct((S, 2 * SGU_W), BF16),
            jax.ShapeDtypeStruct((4, SGU_CHUNK, SGU_CHUNK), F32),
            jax.ShapeDtypeStruct((SGU_CHUNK, SGU_W), F32),
            jax.ShapeDtypeStruct((1, SGU_W), F32),
        ],
        compiler_params=_params(("arbitrary",)),
    )(rest, rest, gn.reshape(1, SGU_W), wm, wm_t, bias, dsg)


_GT = 512
_G0 = OFF_G // _GT


def _gate_specs(tm, col_of):
    specs = [pl.BlockSpec((tm, _GT), functools.partial(lambda k, *ids: (col_of(*ids)[0], _G0 + 2 * k + col_of(*ids)[1]), k)) for k in range(3)]
    specs += [pl.BlockSpec((1, _GT), functools.partial(lambda k, *ids: (0, 2 * k + col_of(*ids)[1]), k)) for k in range(3)]
    return specs


def _merge_fwd(rest, bg, ya, yb, yc, *, name):
    S = rest.shape[0]
    tm = _tile(S, 512)

    def body(g1, g2, g3, b1, b2, b3, ya_ref, yb_ref, yc_ref, o_ref):
        acc = _sigmoid(g1[...] + b1[...]) * ya_ref[...]
        acc = acc + _sigmoid(g2[...] + b2[...]) * yb_ref[...]
        acc = acc + _sigmoid(g3[...] + b3[...]) * yc_ref[...]
        o_ref[...] = acc.astype(BF16)

    blk = pl.BlockSpec((tm, _GT), lambda i, j: (i, j))
    return pl.pallas_call(
        body,
        name=name,
        grid=(S // tm, D // _GT),
        in_specs=_gate_specs(tm, lambda i, j: (i, j)) + [blk, blk, blk],
        out_specs=blk,
        out_shape=jax.ShapeDtypeStruct((S, D), BF16),
        compiler_params=_params(("parallel", "parallel")),
    )(rest, rest, rest, bg, bg, bg, ya, yb, yc)


def _merge_bwd(rest, bg, ya, yb, yc, dm, *, name):
    S = rest.shape[0]
    tm = _tile(S, 512)

    def body(g1, g2, g3, b1, b2, b3, ya_ref, yb_ref, yc_ref, dm_ref, dya, dyb, dyc, dg1, dg2, dg3, db1, db2, db3):
        first = pl.program_id(1) == 0
        dmv = dm_ref[...]
        for g_ref, b_ref, y_ref, dy_ref, dg_ref, db_ref in (
            (g1, b1, ya_ref, dya, dg1, db1), (g2, b2, yb_ref, dyb, dg2, db2), (g3, b3, yc_ref, dyc, dg3, db3)):
            gate = _sigmoid(g_ref[...] + b_ref[...])
            dy_ref[...] = (dmv * gate).astype(BF16)
            dpre = dmv * y_ref[...] * gate * (1.0 - gate)
            dg_ref[...] = dpre.astype(BF16)
            part = jnp.sum(dpre, axis=0, keepdims=True)

            @pl.when(first)
            def _():
                db_ref[...] = part

            @pl.when(jnp.logical_not(first))
            def _():
                db_ref[...] += part

    blk = pl.BlockSpec((tm, _GT), lambda j, i: (i, j))
    vec = pl.BlockSpec((1, _GT), lambda j, i: (0, j))
    big = jax.ShapeDtypeStruct((S, D), BF16)
    small = jax.ShapeDtypeStruct((1, D), F32)
    return pl.pallas_call(
        body,
        name=name,
        grid=(D // _GT, S // tm),
        in_specs=_gate_specs(tm, lambda j, i: (i, j)) + [blk, blk, blk, blk],
        out_specs=[blk] * 6 + [vec] * 3,
        out_shape=[big] * 6 + [small] * 3,
        compiler_params=_params(("parallel", "arbitrary")),
    )(rest, rest, rest, bg, bg, bg, ya, yb, yc, dm)


_X_SCALE = XDH ** -0.5


def _xattn_fwd(xq, kv, *, name):
    S = xq.shape[0]
    M = kv.shape[0]
    tq = _tile(S, 512)

    def body(q_ref, k_ref, v_ref, o_ref):
        s = _dot(q_ref[...], k_ref[...], 1, 1) * _X_SCALE
        e = jnp.exp(s - jnp.max(s, axis=-1, keepdims=True))
        p = e / jnp.sum(e, axis=-1, keepdims=True)
        o_ref[...] = _dot(p.astype(BF16), v_ref[...], 1, 0).astype(BF16)

    return pl.pallas_call(
        body,
        name=name,
        grid=(S // tq, XH),
        in_specs=[
            pl.BlockSpec((tq, XDH), lambda i, h: (i, h)),
            pl.BlockSpec((M, XDH), lambda i, h: (0, h)),
            pl.BlockSpec((M, XDH), lambda i, h: (0, XH + h)),
        ],
        out_specs=pl.BlockSpec((tq, XDH), lambda i, h: (i, h)),
        out_shape=jax.ShapeDtypeStruct((S, D), BF16),
        compiler_params=_params(("parallel", "parallel")),
    )(xq, kv, kv)


def _xattn_bwd(xq, kv, do, *, name):
    S = xq.shape[0]
    M = kv.shape[0]
    tq = _tile(S, 512)

    def body(q_ref, k_ref, v_ref, do_ref, dq_ref, dk_ref, dv_ref):
        qb = q_ref[...]
        kb = k_ref[...]
        dob = do_ref[...]
        s = _dot(qb, kb, 1, 1) * _X_SCALE
        e = jnp.exp(s - jnp.max(s, axis=-1, keepdims=True))
        p = e / jnp.sum(e, axis=-1, keepdims=True)
        dp = _dot(dob, v_ref[...], 1, 1)
        ds = (p * (dp - jnp.sum(p * dp, axis=-1, keepdims=True)) * _X_SCALE).astype(BF16)
        dq_ref[...] = _dot(ds, kb, 1, 0).astype(BF16)
        dk_part = _dot(ds, qb, 0, 0)
        dv_part = _dot(p.astype(BF16), dob, 0, 0)

        @pl.when(pl.program_id(1) == 0)
        def _():
            dk_ref[...] = dk_part
            dv_ref[...] = dv_part

        @pl.when(pl.program_id(1) > 0)
        def _():
            dk_ref[...] += dk_part
            dv_ref[...] += dv_part

    qspec = pl.BlockSpec((tq, XDH), lambda h, i: (i, h))
    kspec = pl.BlockSpec((M, XDH), lambda h, i: (0, h))
    dxq, dxk, dxv = pl.pallas_call(
        body,
        name=name,
        grid=(XH, S // tq),
        in_specs=[qspec, kspec, pl.BlockSpec((M, XDH), lambda h, i: (0, XH + h)), qspec],
        out_specs=[qspec, kspec, kspec],
        out_shape=[jax.ShapeDtypeStruct((S, D), BF16), jax.ShapeDtypeStruct((M, D), F32), jax.ShapeDtypeStruct((M, D), F32)],
        compiler_params=_params(("parallel", "arbitrary")),
    )(xq, kv, kv, do)
    return dxq, jnp.concatenate([dxk, dxv], axis=1)


def _adam_math(w, g, m, v):
    m = ADAM_B1 * m + (1.0 - ADAM_B1) * g
    v = ADAM_B2 * v + (1.0 - ADAM_B2) * (g * g)
    m_hat = m / (1.0 - ADAM_B1 ** ADAM_STEP)
    v_hat = v / (1.0 - ADAM_B2 ** ADAM_STEP)
    delta = -ADAM_LR * (m_hat / (jnp.sqrt(v_hat) + ADAM_EPS) + ADAM_WD * w)
    return delta, m, v


def _adamw_sharded(parts, w, m, v, *, name):
    _, R, C = w.shape
    Cp = parts[0].shape[2]
    tm = _tile(R, 256)
    nr = R // tm

    def body(p0_ref, p1_ref, w_ref, m_ref, v_ref, g_ref, d_ref, mo_ref, vo_ref):
        def update(p_ref):
            g = p_ref[0][:, :C].astype(F32)
            for dev in range(1, N_DEV):
                g = g + p_ref[dev][:, :C].astype(F32)
            delta, mn, vn = _adam_math(w_ref[...], g, m_ref[...], v_ref[...])
            g_ref[...] = g
            d_ref[...] = delta
            mo_ref[...] = mn
            vo_ref[...] = vn

        @pl.when(pl.program_id(0) == 0)
        def _():
            update(p0_ref)

        @pl.when(pl.program_id(0) == 1)
        def _():
            update(p1_ref)

    p0 = pl.BlockSpec((N_DEV, tm, Cp), lambda l, i: (0, i * (1 - l) + (nr - 1) * l, 0))
    p1 = pl.BlockSpec((N_DEV, tm, Cp), lambda l, i: (0, i * l, 0))
    blk = pl.BlockSpec((None, tm, C), lambda l, i: (l, i, 0))
    sds = jax.ShapeDtypeStruct(w.shape, F32)
    return pl.pallas_call(
        body,
        name=name,
        grid=(DEPTH, nr),
        in_specs=[p0, p1, blk, blk, blk],
        out_specs=[blk] * 4,
        out_shape=[sds] * 4,
        compiler_params=_params(("arbitrary", "arbitrary")),
    )(parts[0], parts[1], w, m, v)


def _adamw_small(g, w, m, v, *, name):
    n = len(g)

    def body(*refs):
        g_refs, w_refs, m_refs, v_refs = (refs[k * n:(k + 1) * n] for k in range(4))
        d_out, m_out, v_out = (refs[(4 + k) * n:(5 + k) * n] for k in range(3))
        for t in range(n):
            delta, mn, vn = _adam_math(w_refs[t][...], g_refs[t][...], m_refs[t][...], v_refs[t][...])
            d_out[t][...] = delta
            m_out[t][...] = mn
            v_out[t][...] = vn

    vm = pl.BlockSpec(memory_space=pltpu.VMEM)
    shapes = [jax.ShapeDtypeStruct(a.shape, F32) for a in w]
    outs = pl.pallas_call(
        body,
        name=name,
        in_specs=[vm] * (4 * n),
        out_specs=[vm] * (3 * n),
        out_shape=shapes * 3,
        compiler_params=pltpu.CompilerParams(vmem_limit_bytes=VMEM_LIMIT),
    )(*g, *w, *m, *v)
    return outs[:n], outs[n:2 * n], outs[2 * n:]


def _position():
    return lax.axis_index("x"), lax.axis_index("y"), lax.axis_index("c")


def _dev_index(px, py, pc):
    return 4 * px + 2 * py + pc


_ANY = pl.BlockSpec(memory_space=pl.ANY)


def _all_gather(shards, *, name):
    n = len(shards)
    out_shape = [jax.ShapeDtypeStruct((N_DEV, *s.shape), s.dtype) for s in shards]
    n_pieces = len(_pieces(out_shape))

    def body(*refs):
        ins, outs = refs[:n], refs[n:2 * n]
        send_sems, recv_sems, local_sems = refs[2 * n:]
        x, y, c = _position()
        me, sibling = (x, y, c), (x, y, 1 - c)
        chips = [(1 - x, y), (x, 1 - y), (1 - x, 1 - y)]
        pieces = _pieces(outs)

        def copy(i, k, block, to, from_input=False):
            t, rows = pieces[i]
            dst = _cut(outs[t].at[_dev_index(*block)], rows)
            return pltpu.make_async_remote_copy(
                src_ref=_cut(ins[t], rows) if from_input else dst, dst_ref=dst, send_sem=send_sems.at[i, k],
                recv_sem=recv_sems.at[i, k], device_id=to, device_id_type=MESH)

        mine = [pltpu.make_async_copy(_cut(ins[t], rows), _cut(outs[t].at[_dev_index(*me)], rows), local_sems.at[i])
                for i, (t, rows) in enumerate(pieces)]
        for cp in mine:
            cp.start()
        started = []
        for j, chip in enumerate(chips):
            for i in range(n_pieces):
                started.append(copy(i, 1 + j, me, (*chip, c), from_input=True))
                started[-1].start()
        for i in range(n_pieces):
            started.append(copy(i, 0, me, sibling, from_input=True))
            started[-1].start()
        for j, chip in enumerate(chips):
            for i in range(n_pieces):
                copy(i, 1 + j, (*chip, c), me).wait_recv()
                started.append(copy(i, 4 + j, (*chip, c), sibling))
                started[-1].start()
        for i in range(n_pieces):
            copy(i, 0, sibling, me).wait_recv()
        for j, chip in enumerate(chips):
            for i in range(n_pieces):
                copy(i, 4 + j, (*chip, 1 - c), me).wait_recv()
        for cp in started:
            cp.wait_send()
        for cp in mine:
            cp.wait()

    return pl.pallas_call(
        body,
        name=name,
        in_specs=[_ANY] * n,
        out_specs=[_ANY] * n,
        out_shape=out_shape,
        scratch_shapes=[pltpu.SemaphoreType.DMA((n_pieces, 7)), pltpu.SemaphoreType.DMA((n_pieces, 7)),
                        pltpu.SemaphoreType.DMA((n_pieces,))],
        compiler_params=pltpu.CompilerParams(has_side_effects=True),
    )(*shards)


def _peers(x, y, c):
    out = []
    for mask in range(1, N_DEV):
        fx, fy, fc = (mask >> 2) & 1, (mask >> 1) & 1, mask & 1
        out.append((1 - x if fx else x, 1 - y if fy else y, 1 - c if fc else c))
    return out


_HBM = pl.BlockSpec(memory_space=pltpu.HBM)
_SEM = pl.BlockSpec(memory_space=pltpu.SEMAPHORE)


def _own_block_placed(block, like):
    x, y, c = _position()
    return lax.dynamic_update_index_in_dim(lax.empty(like.shape, like.dtype), block, _dev_index(x, y, c), 0)


_COPY_BYTES = 256 << 10
_MAX_PIECES = 8


def _pieces(blocks):
    out = []
    for t, b in enumerate(blocks):
        R, C = b.shape[-2:]
        n = max(1, min(_MAX_PIECES, R * C * jnp.dtype(b.dtype).itemsize // _COPY_BYTES))
        while n > 1 and R % (16 * n):
            n -= 1
        out += [(t, pl.ds(j * (R // n), R // n) if n > 1 else None) for j in range(n)]
    return out


def _cut(block, rows):
    return block if rows is None else block.at[rows]


def _copies(per_piece):
    def mark(fn):
        fn.per_piece = per_piece
        return fn
    return mark


@_copies(N_DEV - 1)
def _plan_exchange(srcs, lands, send_sems, recv_sems, arrivals):
    x, y, c = _position()
    me = _dev_index(x, y, c)
    out = []
    for k, peer in enumerate(_peers(x, y, c)):
        p = _dev_index(*peer)
        for i, (t, rows) in enumerate(_pieces(lands)):
            sems = dict(send_sem=send_sems.at[7 * i + k], recv_sem=recv_sems.at[7 * i + k], device_id=peer, device_id_type=MESH)
            src, dst = (lands[t].at[p], lands[t].at[p]) if arrivals else (srcs[t].at[p], lands[t].at[me])
            out.append(pltpu.make_async_remote_copy(src_ref=_cut(src, rows), dst_ref=_cut(dst, rows), **sems))
    return out


@_copies(N_DEV - 1)
def _plan_broadcast(srcs, lands, send_sems, recv_sems, arrivals):
    x, y, c = _position()
    me = _dev_index(x, y, c)
    out = []
    for k, peer in enumerate(_peers(x, y, c)):
        p = _dev_index(*peer)
        for i, (t, rows) in enumerate(_pieces(lands)):
            sems = dict(send_sem=send_sems.at[7 * i + k], recv_sem=recv_sems.at[7 * i + k], device_id=peer, device_id_type=MESH)
            src, dst = (lands[t].at[p], lands[t].at[p]) if arrivals else (srcs[t], lands[t].at[me])
            out.append(pltpu.make_async_remote_copy(src_ref=_cut(src, rows), dst_ref=_cut(dst, rows), **sems))
    return out


@_copies(4)
def _plan_gather_out(srcs, lands, send_sems, recv_sems, arrivals):
    x, y, c = _position()
    me = _dev_index(x, y, c)
    out = []
    for k, peer in enumerate([(x, y, 1 - c), (1 - x, y, c), (x, 1 - y, c), (1 - x, 1 - y, c)]):
        p = _dev_index(*peer)
        for i, (t, rows) in enumerate(_pieces(lands)):
            sems = dict(send_sem=send_sems.at[4 * i + k], recv_sem=recv_sems.at[4 * i + k], device_id=peer, device_id_type=MESH)
            src, dst = (lands[t].at[p], lands[t].at[p]) if arrivals else (srcs[t], lands[t].at[me])
            out.append(pltpu.make_async_remote_copy(src_ref=_cut(src, rows), dst_ref=_cut(dst, rows), **sems))
    return out


@_copies(3)
def _plan_gather_pass(srcs, lands, send_sems, recv_sems, arrivals):
    x, y, c = _position()
    sibling = (x, y, 1 - c)
    out = []
    for k, chip in enumerate([(1 - x, y), (x, 1 - y), (1 - x, 1 - y)]):
        p = _dev_index(*chip, 1 - c) if arrivals else _dev_index(*chip, c)
        for i, (t, rows) in enumerate(_pieces(lands)):
            sems = dict(send_sem=send_sems.at[3 * i + k], recv_sem=recv_sems.at[3 * i + k], device_id=sibling, device_id_type=MESH)
            block = _cut(lands[t].at[p], rows)
            out.append(pltpu.make_async_remote_copy(src_ref=block, dst_ref=block, **sems))
    return out


def _split_start(plan, srcs, lands, *, after=None, name):
    n_src, n = len(srcs), len(srcs) + len(lands)
    n_sem = plan.per_piece * len(_pieces(lands))
    order = [] if after is None else [after]

    def body(*refs):
        send_sems, recv_sems = refs[n + len(order):n + len(order) + 2]
        token = refs[-1]
        for cp in plan(refs[:n_src], refs[n_src:n], send_sems, recv_sems, arrivals=False):
            cp.start()
        token[...] = jnp.zeros_like(token)

    hbm = lambda a: pltpu.HBM(a.shape, a.dtype)
    outs = pl.pallas_call(
        body,
        name=name,
        in_specs=[_HBM] * n + [_ANY] * len(order),
        out_specs=[_SEM, _SEM] + [_HBM] * n + [pl.BlockSpec(memory_space=pltpu.VMEM)],
        out_shape=[pltpu.SemaphoreType.DMA((n_sem,)), pltpu.SemaphoreType.DMA((n_sem,))] + [hbm(a) for a in (*srcs, *lands)]
        + [jax.ShapeDtypeStruct(_TOKEN, F32)],
        input_output_aliases={i: 2 + i for i in range(n)},
        compiler_params=pltpu.CompilerParams(has_side_effects=pltpu.SideEffectType.DATAFLOW_SIDE_EFFECTING),
    )(*[pltpu.with_memory_space_constraint(a, pltpu.HBM) for a in (*srcs, *lands)], *order)
    return (outs[0], outs[1], outs[2:2 + n_src], outs[2 + n_src:2 + n]), outs[-1]


def _split_wait(plan, state, after, *, name):
    send_sems, recv_sems, srcs, lands = state
    n_src, n = len(srcs), len(srcs) + len(lands)

    def body(*refs):
        send_refs, recv_refs = refs[n:n + 2]
        for cp in plan(refs[:n_src], refs[n_src:n], send_refs, recv_refs, arrivals=False):
            cp.wait_send()
        for cp in plan(refs[:n_src], refs[n_src:n], send_refs, recv_refs, arrivals=True):
            cp.wait_recv()

    hbm = lambda a: pltpu.HBM(a.shape, a.dtype)
    outs = pl.pallas_call(
        body,
        name=name,
        in_specs=[_HBM] * n + [_SEM, _SEM, _ANY],
        out_specs=[_HBM] * n,
        out_shape=[hbm(a) for a in (*srcs, *lands)],
        input_output_aliases={i: i for i in range(n)},
        compiler_params=pltpu.CompilerParams(has_side_effects=pltpu.SideEffectType.DATAFLOW_SIDE_EFFECTING),
    )(*srcs, *lands, send_sems, recv_sems, after)
    return outs[n_src:]


def _sum_blocks(blocks, *, name):
    _, R, C = blocks.shape
    tm = next(R // n for n in (4, 3, 2, 1) if R % (8 * n) == 0)

    def body(b_ref, o_ref):
        g = b_ref[0]
        for dev in range(1, N_DEV):
            g = g + b_ref[dev]
        o_ref[...] = g

    return pl.pallas_call(
        body,
        name=name,
        grid=(R // tm,),
        in_specs=[pl.BlockSpec((N_DEV, tm, C), lambda i: (0, i, 0))],
        out_specs=pl.BlockSpec((tm, C), lambda i: (i, 0)),
        out_shape=jax.ShapeDtypeStruct((R, C), F32),
        compiler_params=_params(("parallel",)),
    )(blocks)


def _block_diag(w):
    out = jnp.zeros((POOL_W, POOL_W), w.dtype)
    for gi in range(4):
        out = out.at[64 * gi:64 * (gi + 1), 64 * gi:64 * (gi + 1)].set(w[gi])
    return out


def _layer_consts(sp, l):
    causal = jnp.tril(jnp.ones((SGU_CHUNK, SGU_CHUNK), F32))
    wm = (sp["sgu_w"][l] * causal[None]).astype(BF16)
    wbd = _block_diag(sp["pool_w"][l]).astype(BF16)
    return dict(
        wbd=wbd, wbd_t=wbd.T, wm=wm, wm_t=wm.transpose(0, 2, 1),
        sgu_bias=jnp.repeat(sp["sgu_b"][l].T, 64, axis=1),
        bpad=jnp.pad(sp["b_forget"][l], (0, F_LANES - FOX_H)).reshape(1, F_LANES),
        bg=sp["b_gate"][l].reshape(1, 3 * D),
    )


def _relu2(acc):
    return acc, jnp.square(jnp.maximum(acc, 0.0))


def _relu2_grad(acc, z):
    return (acc * 2.0 * jnp.maximum(z, 0.0),)


def _layer_fwd(l, x, mem, source, sp):
    S = x.shape[0]
    t = _tile(S, 256)
    c = _layer_consts(sp, l)
    n = f"l{l}_"
    W, after = source(l, "begin", x)
    h = _rms_fwd(x, sp["norm_mix_g"][l], after=after, name=n + "norm_mix")
    qkv = _mm(h, W["qkv"], out_dtypes=(BF16,), name=n + "qkv")
    rest = _mm(h, W["rest"], name=n + "rest")
    pa = _pool_fwd(rest, c["wbd"], sp["pool_scale"][l], name=n + "pool")
    cum, cum_t = _fox_prep(rest, c["bpad"], name=n + "fox_prep")
    fk3 = cum_t[:FOX_H].reshape(FOX_H, S // t, t)
    o, lse = _fox_fwd(qkv, cum, fk3, name=n + "fox")
    more, _ = source(l, "attended", o)
    W.update(more)
    sg = _sgu_fwd(rest, sp["sgu_norm_g"][l], c["wm"], c["sgu_bias"], name=n + "sgu")
    more, after = source(l, "mixed", sg)
    W.update(more)
    ya = _mm(pa, W["ba"], after=after, name=n + "branch_a")
    yb = _mm(o, W["bb"], name=n + "branch_b")
    yc = _mm(sg, W["bc"], name=n + "branch_c")
    merged = _merge_fwd(rest, c["bg"], ya, yb, yc, name=n + "merge")
    x1 = _mm(merged, W["out"], extras=(x,), epilogue=_add, name=n + "out")
    hx = _rms_fwd(x1, sp["norm_xattn_g"][l], name=n + "norm_xattn")
    hm = _rms_fwd(mem, sp["norm_mem_g"][l], name=n + "norm_mem")
    xq = _mm(hx, W["xq"], out_dtypes=(BF16,), name=n + "xq")
    kv = _mm(hm, W["xkv"], out_dtypes=(BF16,), name=n + "xkv")
    o2 = _xattn_fwd(xq, kv, name=n + "xattn")
    x2 = _mm(o2, W["xo"], extras=(x1,), epilogue=_add, name=n + "xo")
    hf = _rms_fwd(x2, sp["norm_ffn_g"][l], name=n + "norm_ffn")
    z, act = _mm(hf, W["ff1"], epilogue=_relu2, out_dtypes=(F32, BF16), name=n + "ff1")
    _, after = source(l, "expanded", act)
    x3 = _mm(act, W["ff2"], extras=(x2,), epilogue=_add, after=after, name=n + "ff2")
    saved = dict(x=x, h=h, qkv=qkv, rest=rest, pa=pa, cum=cum, fk3=fk3, o=o, lse=lse, sg=sg, ya=ya, yb=yb, yc=yc,
                 merged=merged, x1=x1, hx=hx, hm=hm, xq=xq, kv=kv, o2=o2, x2=x2, hf=hf, z=z, act=act, c=c)
    return x3, saved, W


def _layer_bwd(l, dx3, sv, mem, W, sp, grads_done):
    S = dx3.shape[0]
    c = sv["c"]
    n = f"l{l}b_"
    bf = dict(out_dtypes=(BF16,))
    gw, gs = {}, {}
    gw["ff2"] = _mm(sv["act"], dx3, ta=True, name=n + "dw_ff2", **bf)
    dz = _mm(dx3, W["ff2"], tb=True, extras=(sv["z"],), epilogue=_relu2_grad, name=n + "dz", **bf)
    gw["ff1"] = _mm(sv["hf"], dz, ta=True, shard_out=True, name=n + "dw_ff1", **bf)
    dhf = _mm(dz, W["ff1"], tb=True, name=n + "dhf")
    dx2, gs["norm_ffn_g"] = _rms_bwd(sv["x2"], sp["norm_ffn_g"][l], dhf, dx3, name=n + "dnorm_ffn")
    gw["xo"] = _mm(sv["o2"], dx2, ta=True, name=n + "dw_xo", **bf)
    do2 = _mm(dx2, W["xo"], tb=True, name=n + "do2", **bf)
    dxq, dkv = _xattn_bwd(sv["xq"], sv["kv"], do2, name=n + "dxattn")
    gw["xq"] = _mm(sv["hx"], dxq, ta=True, name=n + "dw_xq", **bf)
    gw["xkv"] = _mm(sv["hm"], dkv, ta=True, shard_out=True, name=n + "dw_xkv", **bf)
    dhm = _mm(dkv, W["xkv"], tb=True, name=n + "dhm")
    _, gs["norm_mem_g"] = _rms_bwd(mem, sp["norm_mem_g"][l], dhm, jnp.zeros_like(mem), name=n + "dnorm_mem")
    dhx = _mm(dxq, W["xq"], tb=True, name=n + "dhx")
    dx1, gs["norm_xattn_g"] = _rms_bwd(sv["x1"], sp["norm_xattn_g"][l], dhx, dx2, name=n + "dnorm_xattn")
    after, gw = grads_done(l, gw), {}
    gw["out"] = _mm(sv["merged"], dx1, ta=True, name=n + "dw_out", **bf)
    dm = _mm(dx1, W["out"], tb=True, after=after, name=n + "dmerged")
    dya, dyb, dyc, dg1, dg2, dg3, db1, db2, db3 = _merge_bwd(sv["rest"], c["bg"], sv["ya"], sv["yb"], sv["yc"], dm, name=n + "dmerge")
    gs["b_gate"] = jnp.concatenate([db1, db2, db3], axis=1).reshape(3 * D)
    gw["ba"] = _mm(sv["pa"], dya, ta=True, shard_out=True, name=n + "dw_ba", **bf)
    gw["bb"] = _mm(sv["o"], dyb, ta=True, shard_out=True, name=n + "dw_bb", **bf)
    gw["bc"] = _mm(sv["sg"], dyc, ta=True, shard_out=True, name=n + "dw_bc", **bf)
    after, gw = grads_done(l, gw), {}
    dpa = _mm(dya, W["ba"], tb=True, name=n + "dpa")
    do = _mm(dyb, W["bb"], tb=True, after=after, name=n + "do", **bf)
    dsg = _mm(dyc, W["bc"], tb=True, name=n + "dsg")
    da, dwbd, dscale = _pool_bwd(sv["rest"], c["wbd"], c["wbd_t"], sp["pool_scale"][l], dpa, name=n + "dpool")
    gs["pool_w"] = jnp.stack([dwbd[64 * gi:64 * (gi + 1), 64 * gi:64 * (gi + 1)] for gi in range(4)])
    gs["pool_scale"] = dscale.reshape(POOL_W)
    dq, dk, dv, dfq, dfk = _fox_bwd(sv["qkv"], sv["cum"], sv["fk3"], sv["o"], do, sv["lse"], name=n + "dfox")
    dcum = dfq + jnp.pad(dfk.reshape(FOX_H, S).T, ((0, 0), (0, F_LANES - FOX_H)))
    df, dbf = _fox_post(sv["rest"], c["bpad"], dcum, name=n + "dfox_post")
    gs["b_forget"] = dbf[0, :FOX_H]
    dc, dwm, dbias, dgn = _sgu_bwd(sv["rest"], sp["sgu_norm_g"][l], c["wm"], c["wm_t"], c["sgu_bias"], dsg, name=n + "dsgu")
    gs["sgu_w"] = dwm * jnp.tril(jnp.ones((SGU_CHUNK, SGU_CHUNK), F32))[None]
    gs["sgu_b"] = dbias.reshape(SGU_CHUNK, 4, 64).sum(axis=2).T
    gs["sgu_norm_g"] = dgn.reshape(SGU_W)
    dqkv = jnp.concatenate([dq, dk, dv], axis=1)
    drest = jnp.concatenate([da, df, jnp.zeros((S, OFF_C - OFF_F - F_LANES), BF16), dc, dg1, dg2, dg3], axis=1)
    gw["qkv"] = _mm(sv["h"], dqkv, ta=True, name=n + "dw_qkv", **bf)
    gw["rest"] = _mm(sv["h"], drest, ta=True, name=n + "dw_rest", **bf)
    after = grads_done(l, gw)
    dh = _mm(dqkv, W["qkv"], tb=True, after=after, name=n + "dh_qkv")
    dh = _mm(drest, W["rest"], tb=True, extras=(dh,), epilogue=_add, name=n + "dh")
    dx, gs["norm_mix_g"] = _rms_bwd(sv["x"], sp["norm_mix_g"][l], dh, dx1, name=n + "dnorm_mix")
    return dx, gs


def _local_step(x, mem, target, sp, source, grads_done):
    saved, Ws = [], []
    for l in range(DEPTH):
        x, sv, W = _layer_fwd(l, x, mem, source, sp)
        saved.append(sv)
        Ws.append(W)
    loss, dx, dgf = _final_loss(x, sp["final_norm_g"], target, name="final_loss")
    gss = [None] * DEPTH
    for l in reversed(range(DEPTH)):
        dx, gss[l] = _layer_bwd(l, dx, saved[l], mem, Ws[l], sp, grads_done)
    small = {k: jnp.stack([gss[l][k] for l in range(DEPTH)]) for k in gss[0]}
    small["final_norm_g"] = dgf
    return loss, dx, small


_SMALL = ["norm_mix_g", "b_forget", "pool_w", "pool_scale", "sgu_norm_g", "sgu_w", "sgu_b", "b_gate", "norm_xattn_g",
          "norm_mem_g", "norm_ffn_g", "final_norm_g"]
_COL = {"w_branch_a": "ba", "w_branch_b": "bb", "w_branch_c": "bc", "w_xkv": "xkv", "w_ff1": "ff1"}
_ROW = {"w_out": "out", "w_xq": "xq", "w_xo": "xo", "w_ff2": "ff2"}
_BIG = ["w_in", "w_branch_a", "w_branch_b", "w_branch_c", "w_out", "w_xq", "w_xkv", "w_xo", "w_ff1", "w_ff2"]
_PACK_LANES = 128


def _as_rows(a):
    return a.reshape(-1, a.shape[-1])


def _pack(tensors):
    rows = []
    for a in tensors:
        flat = a.reshape(-1)
        flat = jnp.pad(flat, (0, (-flat.shape[0]) % (8 * _PACK_LANES)))
        rows.append(flat.reshape(-1, _PACK_LANES))
    n_rows = sum(r.shape[0] for r in rows)
    rows.append(jnp.zeros(((-n_rows) % (8 * N_DEV), _PACK_LANES), F32))
    return jnp.concatenate(rows, axis=0)


def _unpack(packed, like):
    out, r = [], 0
    for a in like:
        size = math.prod(a.shape)
        nr = 8 * (-(-size // (8 * _PACK_LANES)))
        out.append(packed[r:r + nr].reshape(-1)[:size].reshape(a.shape))
        r += nr
    return out


_SHARD_IN = N_IN // N_DEV
_SHARD_IN_PAD = -(-_SHARD_IN // 128) * 128


def _columns(pieces, start, stop):
    out, at = [], 0
    for p in pieces:
        lo, hi = max(start, at), min(stop, at + p.shape[1])
        if lo < hi:
            out.append(p[:, lo - at:hi - at])
        at += p.shape[1]
    return out


def _split_w_in(blocks):
    K = blocks[0].shape[0]
    pad = jnp.zeros((K, OFF_C - OFF_F - FOX_H), blocks[0].dtype)
    cols = functools.partial(_columns, blocks)
    rest = jnp.concatenate(cols(0, R_OFF_Q) + cols(R_OFF_F, R_OFF_C) + [pad] + cols(R_OFF_C, N_IN), axis=1)
    return jnp.concatenate(cols(R_OFF_Q, R_OFF_F), axis=1), rest


def _join_w_in(qkv, rest):
    in_order = [rest[:, :R_OFF_Q], qkv, rest[:, OFF_F:OFF_F + FOX_H], rest[:, OFF_C:]]
    pad = jnp.zeros((qkv.shape[0], _SHARD_IN_PAD - _SHARD_IN), qkv.dtype)
    return jnp.stack([jnp.concatenate(_columns(in_order, _SHARD_IN * d, _SHARD_IN * (d + 1)) + [pad], axis=1) for d in range(N_DEV)])


_FIRST = ["w_in"]
_LATER = [k for k in _BIG if k not in _FIRST]


def _layer_weights(gathered):
    W = {}
    if "w_in" in gathered:
        W.update(zip(("qkv", "rest"), _split_w_in([gathered["w_in"][d][:, :_SHARD_IN] for d in range(N_DEV)])))
    for name, key in _COL.items():
        if name in gathered:
            W[key] = _Gathered(gathered[name])
    for name, key in _ROW.items():
        if name in gathered:
            W[key] = gathered[name].reshape(-1, gathered[name].shape[-1])
    return W


def _grad_blocks(gw):
    parts = {}
    if "qkv" in gw:
        parts["w_in"] = _join_w_in(gw["qkv"], gw["rest"])
    for name, key in _COL.items():
        if key in gw:
            parts[name] = gw[key]
    for name, key in _ROW.items():
        if key in gw:
            parts[name] = gw[key].reshape(N_DEV, -1, gw[key].shape[-1])
    return parts


def kernel(x, mem, norm_mix_g, w_in, b_forget, pool_w, pool_scale, sgu_norm_g, sgu_w, sgu_b, w_branch_a, w_branch_b, w_branch_c, b_gate, w_out, norm_xattn_g, norm_mem_g, w_xq, w_xkv, w_xo, norm_ffn_g, w_ff1, w_ff2, final_norm_g, loss_target, m_norm_mix_g, m_w_in, m_b_forget, m_pool_w, m_pool_scale, m_sgu_norm_g, m_sgu_w, m_sgu_b, m_w_branch_a, m_w_branch_b, m_w_branch_c, m_b_gate, m_w_out, m_norm_xattn_g, m_norm_mem_g, m_w_xq, m_w_xkv, m_w_xo, m_norm_ffn_g, m_w_ff1, m_w_ff2, m_final_norm_g, v_norm_mix_g, v_w_in, v_b_forget, v_pool_w, v_pool_scale, v_sgu_norm_g, v_sgu_w, v_sgu_b, v_w_branch_a, v_w_branch_b, v_w_branch_c, v_b_gate, v_w_out, v_norm_xattn_g, v_norm_mem_g, v_w_xq, v_w_xkv, v_w_xo, v_norm_ffn_g, v_w_ff1, v_w_ff2, v_final_norm_g):
    names = ["norm_mix_g", "w_in", "b_forget", "pool_w", "pool_scale", "sgu_norm_g", "sgu_w", "sgu_b", "w_branch_a", "w_branch_b",
             "w_branch_c", "b_gate", "w_out", "norm_xattn_g", "norm_mem_g", "w_xq", "w_xkv", "w_xo", "norm_ffn_g", "w_ff1", "w_ff2",
             "final_norm_g"]
    w = dict(zip(names, [norm_mix_g, w_in, b_forget, pool_w, pool_scale, sgu_norm_g, sgu_w, sgu_b, w_branch_a, w_branch_b, w_branch_c,
                         b_gate, w_out, norm_xattn_g, norm_mem_g, w_xq, w_xkv, w_xo, norm_ffn_g, w_ff1, w_ff2, final_norm_g]))
    m = dict(zip(names, [m_norm_mix_g, m_w_in, m_b_forget, m_pool_w, m_pool_scale, m_sgu_norm_g, m_sgu_w, m_sgu_b, m_w_branch_a,
                         m_w_branch_b, m_w_branch_c, m_b_gate, m_w_out, m_norm_xattn_g, m_norm_mem_g, m_w_xq, m_w_xkv, m_w_xo,
                         m_norm_ffn_g, m_w_ff1, m_w_ff2, m_final_norm_g]))
    v = dict(zip(names, [v_norm_mix_g, v_w_in, v_b_forget, v_pool_w, v_pool_scale, v_sgu_norm_g, v_sgu_w, v_sgu_b, v_w_branch_a,
                         v_w_branch_b, v_w_branch_c, v_b_gate, v_w_out, v_norm_xattn_g, v_norm_mem_g, v_w_xq, v_w_xkv, v_w_xo,
                         v_norm_ffn_g, v_w_ff1, v_w_ff2, v_final_norm_g]))

    sp = {k: w[k] for k in _SMALL}
    shards = [{k: w[k][l].astype(BF16) for k in _BIG} for l in range(DEPTH)]
    for sh in shards:
        sh["w_in"] = jnp.pad(sh["w_in"], ((0, 0), (0, _SHARD_IN_PAD - _SHARD_IN)))
    me = _dev_index(*_position())

    def gather_out(l, keys, name, after=None):
        srcs = [shards[l][k] for k in keys]
        lands = [_own_block_placed(a, jax.ShapeDtypeStruct((N_DEV, *a.shape), a.dtype)) for a in srcs]
        state, token = _split_start(_plan_gather_out, srcs, lands, after=after, name=name + "_out_start")
        return (keys, name, state), token

    def gather_pass(job, value):
        keys, name, state = job
        lands = _split_wait(_plan_gather_out, state, value, name=name + "_out_wait")
        state, token = _split_start(_plan_gather_pass, [], lands, name=name + "_pass_start")
        return (keys, name, state), token, lands[0]

    def gather_end(job, value):
        keys, name, state = job
        return _layer_weights(dict(zip(keys, _split_wait(_plan_gather_pass, state, value, name=name + "_pass_wait"))))

    jobs = {}

    def source(l, point, value):
        if (l, point) == (0, "begin"):
            first = _all_gather([shards[0][k] for k in _FIRST], name="gather_l0_first")
            jobs["l0"], token = gather_out(0, _LATER, "gather_l0", after=first[0])
            return _layer_weights(dict(zip(_FIRST, first))), token
        if (l, point) == (0, "attended"):
            jobs["l0"], _, arrived = gather_pass(jobs["l0"], value)
            jobs["l1"], jobs["token"] = gather_out(1, _BIG, "gather_l1", after=arrived)
            return {}, None
        if (l, point) == (0, "mixed"):
            return gather_end(jobs.pop("l0"), value), jobs.pop("token")
        if (l, point) == (0, "expanded"):
            jobs["l1"], token, _ = gather_pass(jobs["l1"], value)
            return {}, token
        if (l, point) == (1, "begin"):
            return gather_end(jobs.pop("l1"), value), None
        return {}, None

    received = [{} for _ in range(DEPTH)]
    travelling = []

    def grads_done(l, gw):
        blocks = _grad_blocks(gw)
        keys = [k for k in _BIG if k in blocks]
        parts = [blocks[k] for k in keys]
        group = f"exchange_grads_l{l}_" + ("in" if "w_in" in blocks else "merge" if "w_out" in blocks else "mlp")
        lands = [_own_block_placed(lax.dynamic_index_in_dim(p, me, 0, keepdims=False), p) for p in parts]
        state, token = _split_start(_plan_exchange, parts, lands, name=group + "_start")
        travelling.append((l, keys, state, group + "_wait"))
        return token

    loss, dx, small = _local_step(x[0], mem[0], loss_target[0], sp, source, grads_done)
    grads, deltas, new_m, new_v = {}, {}, {}, {}
    like = [loss] + [w[k] for k in _SMALL]
    packed = _pack([loss] + [small[k] for k in _SMALL])
    eighths = packed.reshape(N_DEV, -1, _PACK_LANES)
    own = lambda a: _own_block_placed(lax.dynamic_index_in_dim(a, me, 0, keepdims=False) if a.ndim == 3 else a, eighths)
    scatter, done = _split_start(_plan_exchange, [eighths], [own(eighths)], after=dx, name="small_grads_scatter_start")

    def reduce_small(after):
        mine = _sum_blocks(_split_wait(_plan_exchange, scatter, after, name="small_grads_scatter_wait")[0], name="small_grads_sum")
        return _split_start(_plan_broadcast, [mine], [own(mine)], name="small_grads_gather_start")

    def update_small(state, after):
        total = _split_wait(_plan_broadcast, state, after, name="small_grads_gather_wait")[0].reshape(packed.shape)
        loss_sum, *g_small = _unpack(total, like)
        rows = lambda d: [_as_rows(d[k]) for k in _SMALL]
        outs = _adamw_small([_as_rows(g) for g in g_small], rows(w), rows(m), rows(v), name="adamw_small")
        grads.update(zip(_SMALL, g_small))
        for dst, vals in zip((deltas, new_m, new_v), outs):
            dst.update({k: a.reshape(w[k].shape) for k, a in zip(_SMALL, vals)})
        return loss_sum[0, 0], outs[0][0]

    groups = list(dict.fromkeys(tuple(keys) for _, keys, _, _ in travelling))
    for n_done, group_keys in enumerate(groups):
        if n_done == 1:
            gather, _ = reduce_small(done)
        if n_done == len(groups) - 1:
            loss, done = update_small(gather, done)
        for l, keys, state, wait_name in travelling:
            if tuple(keys) == group_keys:
                received[l].update(zip(keys, _split_wait(_plan_exchange, state, done, name=wait_name)))
        for k in group_keys:
            outs = _adamw_sharded([received[l][k] for l in range(DEPTH)], w[k], m[k], v[k], name="adamw_" + k)
            grads[k], deltas[k], new_m[k], new_v[k] = outs
        done = grads[group_keys[-1]]

    return (loss, dx[None], *[grads[k] for k in names], *[deltas[k] for k in names], *[new_m[k] for k in names],
            *[new_v[k] for k in names])
```

```python
import functools
import math

import jax
import jax.numpy as jnp
from jax import lax
from jax.experimental import pallas as pl
from jax.experimental.pallas import tpu as pltpu

F32 = jnp.float32
BF16 = jnp.bfloat16
MESH = pl.DeviceIdType.MESH

N_DEV = 8
D = 1024
DEPTH = 2
EPS = 1e-6
NEG = -1e30
POOL_W = 256
FOX_H = 8
FOX_DH = 64
FOX_W = 512
SGU_W = 256
SGU_CHUNK = 128
XH = 4
XDH = 256
N_IN = 5384
R_OFF_Q, R_OFF_F, R_OFF_C = 256, 1792, 1800
QKV_W = 3 * FOX_W
OFF_A, OFF_F, OFF_C, OFF_G, REST_W = 0, 256, 512, 1024, 4096
F_LANES = 128

ADAM_LR = 0.001
ADAM_B1 = 0.9
ADAM_B2 = 0.999
ADAM_EPS = 1e-08
ADAM_WD = 0.01
ADAM_STEP = 10

VMEM_LIMIT = 56 * 1024 * 1024


def _tile(n, pref):
    t = min(n, pref)
    while n % t:
        t -= 128
    assert t > 0, (n, pref)
    return t


def _params(sem=None):
    return pltpu.CompilerParams(dimension_semantics=sem, vmem_limit_bytes=VMEM_LIMIT)


def _dot(a, b, ca, cb):
    return lax.dot_general(a, b, (((ca,), (cb,)), ((), ())), preferred_element_type=F32)


def _sigmoid(z):
    return 1.0 / (1.0 + jnp.exp(-z))


_GELU_K = math.sqrt(2.0 / math.pi)
_GELU_C = 0.044715


def _gelu(x):
    return 0.5 * x * (1.0 + jnp.tanh(_GELU_K * (x + _GELU_C * x * x * x)))


def _gelu_grad(x):
    t = jnp.tanh(_GELU_K * (x + _GELU_C * x * x * x))
    return 0.5 * (1.0 + t) + 0.5 * x * (1.0 - t * t) * _GELU_K * (1.0 + 3.0 * _GELU_C * x * x)


def _rows(shape):
    return lax.broadcasted_iota(jnp.int32, shape, 0)


def _lanes(shape):
    return lax.broadcasted_iota(jnp.int32, shape, 1)


class _Gathered:
    def __init__(self, arr):
        self.arr = arr
        self.shape = (arr.shape[1], N_DEV * arr.shape[2])


_TOKEN = (8, 128)


def _mm(a, b, *, ta=False, tb=False, extras=(), epilogue=None, out_dtypes=(F32,), shard_out=False, after=None, tm=None, tn=512, tk=None,
        name):
    a_parts = list(a) if isinstance(a, (list, tuple)) else [a]
    b_parts = list(b) if isinstance(b, (list, tuple)) else [b]
    gathered = isinstance(b, _Gathered)
    assert (len(a_parts) == 1 or not ta) and (len(b_parts) == 1 or not tb) and min(len(a_parts), len(b_parts)) == 1
    a0, b0 = a_parts[0], b_parts[0]
    M, K = (a0.shape[1], a0.shape[0]) if ta else (a0.shape[0], a0.shape[1] * len(a_parts))
    N, Kb = b0.shape if tb else (b0.shape[1] * len(b_parts), b0.shape[0])
    assert Kb == K, (a0.shape, b0.shape, ta, tb)
    if gathered:
        if tb:
            tk = b.arr.shape[2]
        else:
            tn = b.arr.shape[2]
    if len(a_parts) > 1:
        tk = a0.shape[1]
    if shard_out:
        tn = N // N_DEV
    tm = _tile(M, tm or (1024 if ta else 2048))
    tn = _tile(b0.shape[1] if len(b_parts) > 1 else N, tn)
    per_piece = b0.shape[1] // tn
    size = lambda dt: jnp.dtype(dt).itemsize
    row_bytes = len(a_parts) * tm * size(a0.dtype) + len(b_parts) * tn * size(b.arr.dtype if gathered else b0.dtype)
    tile_bytes = tm * tn * (sum(size(e.dtype) for e in extras) + sum(map(size, out_dtypes)))

    def vmem_bytes(k_tile):
        return 2 * (k_tile * row_bytes + tile_bytes) + tm * tn * 4 * (K > k_tile)

    if tk is None:
        tk = next(c for c in (_tile(K, 2048), _tile(K, 1024), _tile(K, 512), _tile(K, 256)) if vmem_bytes(c) <= VMEM_LIMIT - (4 << 20))
    tk = _tile(K, tk)
    nk = K // tk
    ca, cb = (0 if ta else 1), (1 if tb else 0)
    n_a, n_b, n_ex, n_out = len(a_parts), len(b_parts), len(extras), len(out_dtypes)
    tokens = [] if after is None else [after]
    n_in = n_a + n_b + n_ex + len(tokens)
    if epilogue is None:
        epilogue = lambda acc: (acc,)

    def body(*refs):
        a_refs, b_refs = refs[:n_a], refs[n_a:n_a + n_b]
        ex_refs = refs[n_a + n_b:n_a + n_b + n_ex]
        o_refs = refs[n_in:n_in + n_out]
        j, k = pl.program_id(1), pl.program_id(2)

        def finish(acc):
            for o_ref, val in zip(o_refs, epilogue(acc, *[e[...] for e in ex_refs])):
                o_ref[...] = val.astype(o_ref.dtype)

        def step(a_ref, b_ref):
            part = _dot(a_ref[...].astype(BF16), b_ref[...].astype(BF16), ca, cb)
            if nk == 1:
                finish(part)
            else:
                acc_ref = refs[-1]

                @pl.when(k == 0)
                def _():
                    acc_ref[...] = part

                @pl.when(k > 0)
                def _():
                    acc_ref[...] += part

                @pl.when(k == nk - 1)
                def _():
                    finish(acc_ref[...])

        if n_a > 1:
            for p in range(n_a):
                pl.when(k == p)(functools.partial(step, a_refs[p], b_refs[0]))
        elif n_b > 1:
            for p in range(n_b):
                pl.when(j // per_piece == p)(functools.partial(step, a_refs[0], b_refs[p]))
        else:
            step(a_refs[0], b_refs[0])

    if n_a > 1:
        a_specs = [pl.BlockSpec((tm, tk), lambda i, j, k: (i, 0))] * n_a
    else:
        a_specs = [pl.BlockSpec((tk, tm), lambda i, j, k: (k, i)) if ta else pl.BlockSpec((tm, tk), lambda i, j, k: (i, k))]
    if gathered:
        b_arrs = [b.arr]
        b_specs = [pl.BlockSpec((None, tn, tk), lambda i, j, k: (k, j, 0)) if tb else pl.BlockSpec((None, tk, tn), lambda i, j, k: (j, k, 0))]
    elif n_b > 1:
        b_arrs = b_parts
        b_specs = [pl.BlockSpec((tk, tn), functools.partial(lambda p, i, j, k: (k, jnp.clip(j - p * per_piece, 0, per_piece - 1)), p))
                   for p in range(n_b)]
    else:
        b_arrs = b_parts
        b_specs = [pl.BlockSpec((tn, tk), lambda i, j, k: (j, k)) if tb else pl.BlockSpec((tk, tn), lambda i, j, k: (k, j))]
    tile = pl.BlockSpec((tm, tn), lambda i, j, k: (i, j))
    if shard_out:
        out_specs = [pl.BlockSpec((None, tm, tn), lambda i, j, k: (j, i, 0))] * n_out
        out_shape = [jax.ShapeDtypeStruct((N_DEV, M, tn), dt) for dt in out_dtypes]
    else:
        out_specs = [tile] * n_out
        out_shape = [jax.ShapeDtypeStruct((M, N), dt) for dt in out_dtypes]
    assert vmem_bytes(tk) <= VMEM_LIMIT - (4 << 20), (name, vmem_bytes(tk))
    outs = pl.pallas_call(
        body,
        name=name,
        grid=(M // tm, N // tn, nk),
        in_specs=a_specs + b_specs + [tile] * n_ex + [pl.BlockSpec(_TOKEN, lambda i, j, k: (0, 0))] * len(tokens),
        out_specs=out_specs,
        out_shape=out_shape,
        scratch_shapes=[pltpu.VMEM((tm, tn), F32)] if nk > 1 else [],
        compiler_params=_params(("parallel", "parallel", "arbitrary")),
    )(*a_parts, *b_arrs, *extras, *tokens)
    return outs[0] if n_out == 1 else outs


def _add(acc, res):
    return (acc + res,)


def _rms_fwd(x, g, *, after=None, name):
    R, C = x.shape
    tm = _tile(R, 256)
    tokens = [] if after is None else [after]

    def body(x_ref, g_ref, *rest):
        xv = x_ref[...]
        r = lax.rsqrt(jnp.mean(xv * xv, axis=-1, keepdims=True) + EPS)
        rest[-1][...] = (xv * r * g_ref[...]).astype(BF16)

    return pl.pallas_call(
        body,
        name=name,
        grid=(R // tm,),
        in_specs=[pl.BlockSpec((tm, C), lambda i: (i, 0)), pl.BlockSpec((1, C), lambda i: (0, 0))]
        + [pl.BlockSpec(_TOKEN, lambda i: (0, 0))] * len(tokens),
        out_specs=pl.BlockSpec((tm, C), lambda i: (i, 0)),
        out_shape=jax.ShapeDtypeStruct((R, C), BF16),
        compiler_params=_params(("parallel",)),
    )(x, g.reshape(1, C), *tokens)


def _rms_bwd(x, g, dh, dres, *, name):
    R, C = x.shape
    tm = _tile(R, 256)

    def body(x_ref, g_ref, dh_ref, dres_ref, dx_ref, dg_ref):
        xv = x_ref[...]
        r = lax.rsqrt(jnp.mean(xv * xv, axis=-1, keepdims=True) + EPS)
        xn = xv * r
        dh_v = dh_ref[...].astype(F32)
        dxn = dh_v * g_ref[...]
        dx_ref[...] = r * (dxn - xn * jnp.mean(dxn * xn, axis=-1, keepdims=True)) + dres_ref[...]
        part = jnp.sum(dh_v * xn, axis=0, keepdims=True)

        @pl.when(pl.program_id(0) == 0)
        def _():
            dg_ref[...] = part

        @pl.when(pl.program_id(0) > 0)
        def _():
            dg_ref[...] += part

    row = pl.BlockSpec((tm, C), lambda i: (i, 0))
    vec = pl.BlockSpec((1, C), lambda i: (0, 0))
    dx, dg = pl.pallas_call(
        body,
        name=name,
        grid=(R // tm,),
        in_specs=[row, vec, row, row],
        out_specs=[row, vec],
        out_shape=[jax.ShapeDtypeStruct((R, C), F32), jax.ShapeDtypeStruct((1, C), F32)],
        compiler_params=_params(("arbitrary",)),
    )(x, g.reshape(1, C), dh, dres)
    return dx, dg.reshape(C)


def _final_loss(x, g, target, *, name):
    R, C = x.shape
    tm = _tile(R, 256)

    def body(x_ref, g_ref, t_ref, loss_ref, dx_ref, dg_ref):
        xv = x_ref[...]
        r = lax.rsqrt(jnp.mean(xv * xv, axis=-1, keepdims=True) + EPS)
        xn = xv * r
        gv = g_ref[...]
        err = xn * gv - t_ref[...]
        lpart = (0.5 / C) * jnp.sum(jnp.sum(err * err, axis=1, keepdims=True), axis=0, keepdims=True)
        dy = err * (1.0 / C)
        dxn = dy * gv
        dx_ref[...] = r * (dxn - xn * jnp.mean(dxn * xn, axis=-1, keepdims=True))
        gpart = jnp.sum(dy * xn, axis=0, keepdims=True)

        @pl.when(pl.program_id(0) == 0)
        def _():
            loss_ref[...] = lpart
            dg_ref[...] = gpart

        @pl.when(pl.program_id(0) > 0)
        def _():
            loss_ref[...] += lpart
            dg_ref[...] += gpart

    row = pl.BlockSpec((tm, C), lambda i: (i, 0))
    vec = pl.BlockSpec((1, C), lambda i: (0, 0))
    loss, dx, dg = pl.pallas_call(
        body,
        name=name,
        grid=(R // tm,),
        in_specs=[row, vec, row],
        out_specs=[pl.BlockSpec((1, 1), lambda i: (0, 0)), row, vec],
        out_shape=[jax.ShapeDtypeStruct((1, 1), F32), jax.ShapeDtypeStruct((R, C), F32), jax.ShapeDtypeStruct((1, C), F32)],
        compiler_params=_params(("arbitrary",)),
    )(x, g.reshape(1, C), target)
    return loss, dx, dg.reshape(C)


def _pool_select(lane, vals):
    out = vals[3]
    for gi in (2, 1, 0):
        out = jnp.where(lane < 64 * (gi + 1), vals[gi], out)
    return out


def _pool_diff(a):
    row, lane = _rows(a.shape), _lanes(a.shape)

    def down(v, k):
        return jnp.where(row >= k, pltpu.roll(v, k, 0), 0.0)

    s2 = a + down(a, 1)
    s4 = s2 + down(s2, 2)
    s8 = s4 + down(s4, 4)
    s16 = s8 + down(s8, 8)
    wsum = _pool_select(lane, (s2, s4, s8, s16))
    win = _pool_select(lane, (2, 4, 8, 16))
    cnt = jnp.minimum(row + 1, win).astype(F32)
    return wsum / cnt - a, cnt


def _pool_diff_t(dd, cnt):
    S = dd.shape[0]
    row, lane = _rows(dd.shape), _lanes(dd.shape)

    def up(v, k):
        return jnp.where(row < S - k, pltpu.roll(v, S - k, 0), 0.0)

    e = dd / cnt
    s2 = e + up(e, 1)
    s4 = s2 + up(s2, 2)
    s8 = s4 + up(s4, 4)
    s16 = s8 + up(s8, 8)
    return _pool_select(lane, (s2, s4, s8, s16)) - dd


def _pool_fwd(rest, wbd, scale, *, name):
    S = rest.shape[0]

    def body(a_ref, w_ref, s_ref, o_ref):
        d, _ = _pool_diff(a_ref[...])
        yp = _dot(d.astype(BF16), w_ref[...], 1, 0)
        o_ref[...] = (yp * s_ref[...]).astype(BF16)

    return pl.pallas_call(
        body,
        name=name,
        grid=(1,),
        in_specs=[
            pl.BlockSpec((S, POOL_W), lambda i: (0, OFF_A // POOL_W)),
            pl.BlockSpec((POOL_W, POOL_W), lambda i: (0, 0)),
            pl.BlockSpec((1, POOL_W), lambda i: (0, 0)),
        ],
        out_specs=pl.BlockSpec((S, POOL_W), lambda i: (0, 0)),
        out_shape=jax.ShapeDtypeStruct((S, POOL_W), BF16),
        compiler_params=_params(("arbitrary",)),
    )(rest, wbd, scale.reshape(1, POOL_W))


def _pool_bwd(rest, wbd, wbd_t, scale, dpa, *, name):
    S = rest.shape[0]

    def body(a_ref, w_ref, wt_ref, s_ref, dpa_ref, da_ref, dw_ref, ds_ref):
        d, cnt = _pool_diff(a_ref[...])
        db = d.astype(BF16)
        yp = _dot(db, w_ref[...], 1, 0)
        dpa_v = dpa_ref[...]
        ds_ref[...] = jnp.sum(dpa_v * yp, axis=0, keepdims=True)
        dyp = (dpa_v * s_ref[...]).astype(BF16)
        dw_ref[...] = _dot(db, dyp, 0, 0)
        dd = _dot(dyp, wt_ref[...], 1, 0)
        da_ref[...] = _pool_diff_t(dd, cnt).astype(BF16)

    full = pl.BlockSpec((S, POOL_W), lambda i: (0, 0))
    sq = pl.BlockSpec((POOL_W, POOL_W), lambda i: (0, 0))
    vec = pl.BlockSpec((1, POOL_W), lambda i: (0, 0))
    return pl.pallas_call(
        body,
        name=name,
        grid=(1,),
        in_specs=[pl.BlockSpec((S, POOL_W), lambda i: (0, OFF_A // POOL_W)), sq, sq, vec, full],
        out_specs=[full, sq, vec],
        out_shape=[
            jax.ShapeDtypeStruct((S, POOL_W), BF16),
            jax.ShapeDtypeStruct((POOL_W, POOL_W), F32),
            jax.ShapeDtypeStruct((1, POOL_W), F32),
        ],
        compiler_params=_params(("arbitrary",)),
    )(rest, wbd, wbd_t, scale.reshape(1, POOL_W), dpa)


def _log_sigmoid(z):
    return jnp.minimum(z, 0.0) - jnp.log(1.0 + jnp.exp(-jnp.abs(z)))


_F_SPEC_COL = OFF_F // F_LANES


def _fox_prep(rest, bpad, *, name):
    S = rest.shape[0]

    def body(f_ref, b_ref, o_ref, ot_ref):
        acc = _log_sigmoid(f_ref[...] + b_ref[...])
        row = _rows(acc.shape)
        k = 1
        while k < S:
            acc = acc + jnp.where(row >= k, pltpu.roll(acc, k, 0), 0.0)
            k *= 2
        o_ref[...] = acc
        ot_ref[...] = acc.T

    return pl.pallas_call(
        body,
        name=name,
        grid=(1,),
        in_specs=[pl.BlockSpec((S, F_LANES), lambda i: (0, _F_SPEC_COL)), pl.BlockSpec((1, F_LANES), lambda i: (0, 0))],
        out_specs=[pl.BlockSpec((S, F_LANES), lambda i: (0, 0)), pl.BlockSpec((F_LANES, S), lambda i: (0, 0))],
        out_shape=[jax.ShapeDtypeStruct((S, F_LANES), F32), jax.ShapeDtypeStruct((F_LANES, S), F32)],
        compiler_params=_params(("arbitrary",)),
    )(rest, bpad)


def _fox_post(rest, bpad, dcum, *, name):
    S = rest.shape[0]

    def body(f_ref, b_ref, d_ref, df_ref, db_ref):
        acc = d_ref[...]
        row = _rows(acc.shape)
        k = 1
        while k < S:
            acc = acc + jnp.where(row < S - k, pltpu.roll(acc, S - k, 0), 0.0)
            k *= 2
        df = acc * (1.0 - _sigmoid(f_ref[...] + b_ref[...]))
        df_ref[...] = df.astype(BF16)
        db_ref[...] = jnp.sum(df, axis=0, keepdims=True)

    full = pl.BlockSpec((S, F_LANES), lambda i: (0, 0))
    vec = pl.BlockSpec((1, F_LANES), lambda i: (0, 0))
    return pl.pallas_call(
        body,
        name=name,
        grid=(1,),
        in_specs=[pl.BlockSpec((S, F_LANES), lambda i: (0, _F_SPEC_COL)), vec, full],
        out_specs=[full, vec],
        out_shape=[jax.ShapeDtypeStruct((S, F_LANES), BF16), jax.ShapeDtypeStruct((1, F_LANES), F32)],
        compiler_params=_params(("arbitrary",)),
    )(rest, bpad, dcum)


_FOX_SCALE = FOX_DH ** -0.5
_PAIRS = FOX_H // 2


def _scaled(v):
    return (v.astype(F32) * _FOX_SCALE).astype(BF16)


def _diag_mask(s):
    return jnp.where(_rows(s.shape) >= _lanes(s.shape), s, NEG)


def _fox_fwd(qkv, cum, fk3, *, name):
    S = qkv.shape[0]
    nk, t = fk3.shape[1:]

    def body(q_ref, k_ref, v_ref, cum_ref, fk_ref, o_ref, lse_ref):
        i = pl.program_id(0)
        lane = _lanes((t, 128))
        lo = lane < FOX_DH
        cumv = cum_ref[...]
        qm, fq = [], []
        for h in range(FOX_H):
            qs = _scaled(q_ref[:, 128 * (h // 2):128 * (h // 2 + 1)])
            zero = jnp.zeros_like(qs)
            qm.append(jnp.where(lo, qs, zero) if h % 2 == 0 else jnp.where(lo, zero, qs))
            fq.append(cumv[:, h:h + 1])

        def tile(j, state, masked):
            m, l, acc = (list(part) for part in state)
            k0 = pl.multiple_of(j * t, t)
            for hp in range(_PAIRS):
                cols = slice(128 * hp, 128 * (hp + 1))
                kb = k_ref[pl.ds(k0, t), cols]
                vb = v_ref[pl.ds(k0, t), cols]
                alphas, pvs = [], []
                for h in (2 * hp, 2 * hp + 1):
                    s = _dot(qm[h], kb, 1, 1) + fq[h] - fk_ref[h, pl.ds(j, 1), :]
                    if masked:
                        s = _diag_mask(s)
                    m_new = jnp.maximum(m[h], jnp.max(s, axis=-1, keepdims=True))
                    p = jnp.exp(s - m_new)
                    alpha = jnp.exp(m[h] - m_new)
                    l[h] = alpha * l[h] + jnp.sum(p, axis=-1, keepdims=True)
                    m[h] = m_new
                    alphas.append(alpha)
                    pvs.append(_dot(p.astype(BF16), vb, 1, 0))
                acc[hp] = jnp.where(lo, alphas[0], alphas[1]) * acc[hp] + jnp.where(lo, pvs[0], pvs[1])
            return tuple(m), tuple(l), tuple(acc)

        init = ((jnp.full((t, 1), NEG, F32),) * FOX_H, (jnp.zeros((t, 1), F32),) * FOX_H, (jnp.zeros((t, 128), F32),) * _PAIRS)
        state = lax.fori_loop(0, i, functools.partial(tile, masked=False), init)
        m, l, acc = tile(i, state, True)
        for hp in range(_PAIRS):
            o_ref[:, 128 * hp:128 * (hp + 1)] = acc[hp] / jnp.where(lo, l[2 * hp], l[2 * hp + 1])
            lse = [m[h] + jnp.log(l[h]) for h in (2 * hp, 2 * hp + 1)]
            lse_ref[hp] = jnp.where(lane == 0, lse[0], jnp.where(lane == 1, lse[1], 0.0))

    whole = lambda col: pl.BlockSpec((S, FOX_W), lambda i: (0, col))
    return pl.pallas_call(
        body,
        name=name,
        grid=(S // t,),
        in_specs=[
            pl.BlockSpec((t, FOX_W), lambda i: (i, 0)), whole(1), whole(2),
            pl.BlockSpec((t, F_LANES), lambda i: (i, 0)),
            pl.BlockSpec((FOX_H, nk, t), lambda i: (0, 0, 0)),
        ],
        out_specs=[pl.BlockSpec((t, FOX_W), lambda i: (i, 0)), pl.BlockSpec((_PAIRS, t, 128), lambda i: (0, i, 0))],
        out_shape=[jax.ShapeDtypeStruct((S, FOX_W), F32), jax.ShapeDtypeStruct((_PAIRS, S, 128), F32)],
        compiler_params=_params(("arbitrary",)),
    )(qkv, qkv, qkv, cum, fk3)


def _fox_bwd(qkv, cum, fk3, o, do, lse, *, name):
    S = qkv.shape[0]
    nk, t = fk3.shape[1:]
    q_at, k_at, v_at = 0, FOX_W, 2 * FOX_W

    def body(qkv_ref, cum_ref, fk_ref, o_ref, do_ref, lse_ref, dq_ref, dk_ref, dv_ref, dfq_ref, dfk_ref,
             qs_sc, ks_sc, delta_sc, dq_sc):
        lane = _lanes((t, 128))
        lo = lane < FOX_DH
        mine = lambda h: lo if h % 2 == 0 else jnp.logical_not(lo)

        def by_head(tile, values):
            for h, val in enumerate(values):
                tile = jnp.where(lane == h, val, tile)
            return tile

        def prep(i, carry):
            r = pl.ds(pl.multiple_of(i * t, t), t)
            qs_sc[r, :] = _scaled(qkv_ref[r, q_at:q_at + FOX_W])
            ks_sc[r, :] = _scaled(qkv_ref[r, k_at:k_at + FOX_W])
            sums = []
            for hp in range(_PAIRS):
                cols = slice(128 * hp, 128 * (hp + 1))
                prod = do_ref[r, cols].astype(F32) * o_ref[r, cols]
                sums += [jnp.sum(jnp.where(mine(h), prod, 0.0), axis=-1, keepdims=True) for h in (2 * hp, 2 * hp + 1)]
            delta_sc[r, :] = by_head(jnp.zeros((t, 128), F32), sums)
            dfq_ref[r, :] = jnp.zeros((t, 128), F32)
            dq_sc[r, :] = jnp.zeros((t, FOX_W), F32)
            return carry

        lax.fori_loop(0, nk, prep, 0)

        def kv_tile(j, carry):
            kr = pl.ds(pl.multiple_of(j * t, t), t)

            def q_tile(i, acc, masked):
                dk, dv, dfk = list(acc[:_PAIRS]), list(acc[_PAIRS:2 * _PAIRS]), list(acc[2 * _PAIRS:])
                qr = pl.ds(pl.multiple_of(i * t, t), t)
                delta_t, cum_t, dq_old, dfq_old = delta_sc[qr, :], cum_ref[qr, :], dq_sc[qr, :], dfq_ref[qr, :]
                row_sums, dq_new = [], []
                for hp in range(_PAIRS):
                    cols = slice(128 * hp, 128 * (hp + 1))
                    kb = qkv_ref[kr, k_at + 128 * hp:k_at + 128 * (hp + 1)]
                    vb = qkv_ref[kr, v_at + 128 * hp:v_at + 128 * (hp + 1)]
                    ksb, qsb, dob = ks_sc[kr, cols], qs_sc[qr, cols], do_ref[qr, cols]
                    zero = jnp.zeros_like(qsb)
                    dq_t = jnp.zeros((t, 128), F32)
                    for h in (2 * hp, 2 * hp + 1):
                        qe, doe, ke = (jnp.where(mine(h), a, zero) for a in (qsb, dob, ksb))
                        s = _dot(qe, kb, 1, 1) + cum_t[:, h:h + 1] - fk_ref[h, pl.ds(j, 1), :]
                        if masked:
                            s = _diag_mask(s)
                        p = jnp.exp(s - lse_ref[hp, qr, h % 2:h % 2 + 1])
                        dv[hp] = dv[hp] + _dot(p.astype(BF16), doe, 0, 0)
                        dp = _dot(doe, vb, 1, 1)
                        ds = p * (dp - delta_t[:, h:h + 1])
                        dsb = ds.astype(BF16)
                        dk[hp] = dk[hp] + _dot(dsb, qe, 0, 0)
                        dq_t = dq_t + _dot(dsb, ke, 1, 0)
                        row_sums.append(jnp.sum(ds, axis=-1, keepdims=True))
                        dfk[h] = dfk[h] - jnp.sum(ds, axis=0, keepdims=True)
                    dq_new.append(dq_old[:, cols] + dq_t)
                for hp in range(_PAIRS):
                    dq_sc[qr, 128 * hp:128 * (hp + 1)] = dq_new[hp]
                dfq_ref[qr, :] = dfq_old + by_head(jnp.zeros((t, 128), F32), row_sums)
                return (*dk, *dv, *dfk)

            init = tuple([jnp.zeros((t, 128), F32)] * (2 * _PAIRS) + [jnp.zeros((1, t), F32)] * FOX_H)
            acc = q_tile(j, init, True)
            acc = lax.fori_loop(j + 1, nk, functools.partial(q_tile, masked=False), acc)
            for hp in range(_PAIRS):
                cols = slice(128 * hp, 128 * (hp + 1))
                dk_ref[kr, cols] = acc[hp].astype(BF16)
                dv_ref[kr, cols] = acc[_PAIRS + hp].astype(BF16)
            for h in range(FOX_H):
                dfk_ref[h, pl.ds(j, 1), :] = acc[2 * _PAIRS + h]
            return carry

        lax.fori_loop(0, nk, kv_tile, 0)
        dq_ref[...] = dq_sc[...].astype(BF16)

    vm = pl.BlockSpec(memory_space=pltpu.VMEM)
    big = jax.ShapeDtypeStruct((S, FOX_W), BF16)
    return pl.pallas_call(
        body,
        name=name,
        in_specs=[vm] * 6,
        out_specs=[vm] * 5,
        out_shape=[big, big, big, jax.ShapeDtypeStruct((S, 128), F32), jax.ShapeDtypeStruct((FOX_H, nk, t), F32)],
        scratch_shapes=[pltpu.VMEM((S, FOX_W), BF16), pltpu.VMEM((S, FOX_W), BF16), pltpu.VMEM((S, 128), F32),
                        pltpu.VMEM((S, FOX_W), F32)],
        compiler_params=pltpu.CompilerParams(vmem_limit_bytes=VMEM_LIMIT),
    )(qkv, cum, fk3, o, do, lse)


def _group_mask(lane, gi):
    return (lane >= 64 * gi) & (lane < 64 * (gi + 1))


_U_COL = OFF_C // SGU_W


def _sgu_fwd(rest, gn, wm, bias, *, name):
    S = rest.shape[0]
    ts = _tile(S, 512)
    nc = ts // SGU_CHUNK

    def body(u_ref, v_ref, g_ref, w_ref, b_ref, o_ref):
        zv = _gelu(v_ref[...])
        vn = zv * lax.rsqrt(jnp.mean(zv * zv, axis=-1, keepdims=True) + EPS) * g_ref[...]
        lane = _lanes((SGU_CHUNK, SGU_W))
        for c in range(nc):
            rows = slice(c * SGU_CHUNK, (c + 1) * SGU_CHUNK)
            vcb = vn[rows].astype(BF16)
            mixed = b_ref[...]
            for gi in range(4):
                mixed = mixed + jnp.where(_group_mask(lane, gi), _dot(w_ref[gi], vcb, 1, 0), 0.0)
            o_ref[rows, :] = (_gelu(u_ref[rows, :]) * mixed).astype(BF16)

    return pl.pallas_call(
        body,
        name=name,
        grid=(S // ts,),
        in_specs=[
            pl.BlockSpec((ts, SGU_W), lambda i: (i, _U_COL)),
            pl.BlockSpec((ts, SGU_W), lambda i: (i, _U_COL + 1)),
            pl.BlockSpec((1, SGU_W), lambda i: (0, 0)),
            pl.BlockSpec((4, SGU_CHUNK, SGU_CHUNK), lambda i: (0, 0, 0)),
            pl.BlockSpec((SGU_CHUNK, SGU_W), lambda i: (0, 0)),
        ],
        out_specs=pl.BlockSpec((ts, SGU_W), lambda i: (i, 0)),
        out_shape=jax.ShapeDtypeStruct((S, SGU_W), BF16),
        compiler_params=_params(("parallel",)),
    )(rest, rest, gn.reshape(1, SGU_W), wm, bias)


def _sgu_bwd(rest, gn, wm, wm_t, bias, dsg, *, name):
    S = rest.shape[0]
    ts = _tile(S, 512)
    nc = ts // SGU_CHUNK

    def body(u_ref, v_ref, g_ref, w_ref, wt_ref, b_ref, dsg_ref, dc_ref, dw_ref, db_ref, dg_ref):
        first = pl.program_id(0) == 0

        @pl.when(first)
        def _():
            dw_ref[...] = jnp.zeros_like(dw_ref)
            db_ref[...] = jnp.zeros_like(db_ref)
            dg_ref[...] = jnp.zeros_like(dg_ref)

        gv = g_ref[...]
        lane = _lanes((SGU_CHUNK, SGU_W))
        for c in range(nc):
            rows = slice(c * SGU_CHUNK, (c + 1) * SGU_CHUNK)
            vpre = v_ref[rows, :]
            upre = u_ref[rows, :]
            zv = _gelu(vpre)
            r = lax.rsqrt(jnp.mean(zv * zv, axis=-1, keepdims=True) + EPS)
            zn = zv * r
            vcb = (zn * gv).astype(BF16)
            mixed = b_ref[...]
            for gi in range(4):
                mixed = mixed + jnp.where(_group_mask(lane, gi), _dot(w_ref[gi], vcb, 1, 0), 0.0)
            zu = _gelu(upre)
            dsg_v = dsg_ref[rows, :]
            dc_ref[rows, :SGU_W] = (dsg_v * mixed * _gelu_grad(upre)).astype(BF16)
            dmixed = dsg_v * zu
            db_ref[...] += dmixed
            dvn = jnp.zeros((SGU_CHUNK, SGU_W), F32)
            for gi in range(4):
                dmg = jnp.where(_group_mask(lane, gi), dmixed, 0.0).astype(BF16)
                dw_ref[gi] += _dot(dmg, vcb, 1, 1)
                dvn = dvn + _dot(wt_ref[gi], dmg, 1, 0)
            dg_ref[...] += jnp.sum(dvn * zn, axis=0, keepdims=True)
            dzn = dvn * gv
            dzv = r * (dzn - zn * jnp.mean(dzn * zn, axis=-1, keepdims=True))
            dc_ref[rows, SGU_W:] = (dzv * _gelu_grad(vpre)).astype(BF16)

    blk = pl.BlockSpec((ts, SGU_W), lambda i: (i, 0))
    vec = pl.BlockSpec((1, SGU_W), lambda i: (0, 0))
    w3 = pl.BlockSpec((4, SGU_CHUNK, SGU_CHUNK), lambda i: (0, 0, 0))
    bsp = pl.BlockSpec((SGU_CHUNK, SGU_W), lambda i: (0, 0))
    return pl.pallas_call(
        body,
        name=name,
        grid=(S // ts,),
        in_specs=[
            pl.BlockSpec((ts, SGU_W), lambda i: (i, _U_COL)),
            pl.BlockSpec((ts, SGU_W), lambda i: (i, _U_COL + 1)),
            vec, w3, w3, bsp, blk,
        ],
        out_specs=[pl.BlockSpec((ts, 2 * SGU_W), lambda i: (i, 0)), w3, bsp, vec],
        out_shape=[
            jax.ShapeDtypeStruct((S, 2 * SGU_W), BF16),
            jax.ShapeDtypeStruct((4, SGU_CHUNK, SGU_CHUNK), F32),
            jax.ShapeDtypeStruct((SGU_CHUNK, SGU_W), F32),
            jax.ShapeDtypeStruct((1, SGU_W), F32),
        ],
        compiler_params=_params(("arbitrary",)),
    )(rest, rest, gn.reshape(1, SGU_W), wm, wm_t, bias, dsg)


_GT = 512
_G0 = OFF_G // _GT


def _gate_specs(tm, col_of):
    specs = [pl.BlockSpec((tm, _GT), functools.partial(lambda k, *ids: (col_of(*ids)[0], _G0 + 2 * k + col_of(*ids)[1]), k)) for k in range(3)]
    specs += [pl.BlockSpec((1, _GT), functools.partial(lambda k, *ids: (0, 2 * k + col_of(*ids)[1]), k)) for k in range(3)]
    return specs


def _merge_fwd(rest, bg, ya, yb, yc, *, name):
    S = rest.shape[0]
    tm = _tile(S, 512)

    def body(g1, g2, g3, b1, b2, b3, ya_ref, yb_ref, yc_ref, o_ref):
        acc = _sigmoid(g1[...] + b1[...]) * ya_ref[...]
        acc = acc + _sigmoid(g2[...] + b2[...]) * yb_ref[...]
        acc = acc + _sigmoid(g3[...] + b3[...]) * yc_ref[...]
        o_ref[...] = acc.astype(BF16)

    blk = pl.BlockSpec((tm, _GT), lambda i, j: (i, j))
    return pl.pallas_call(
        body,
        name=name,
        grid=(S // tm, D // _GT),
        in_specs=_gate_specs(tm, lambda i, j: (i, j)) + [blk, blk, blk],
        out_specs=blk,
        out_shape=jax.ShapeDtypeStruct((S, D), BF16),
        compiler_params=_params(("parallel", "parallel")),
    )(rest, rest, rest, bg, bg, bg, ya, yb, yc)


def _merge_bwd(rest, bg, ya, yb, yc, dm, *, name):
    S = rest.shape[0]
    tm = _tile(S, 512)

    def body(g1, g2, g3, b1, b2, b3, ya_ref, yb_ref, yc_ref, dm_ref, dya, dyb, dyc, dg1, dg2, dg3, db1, db2, db3):
        first = pl.program_id(1) == 0
        dmv = dm_ref[...]
        for g_ref, b_ref, y_ref, dy_ref, dg_ref, db_ref in (
            (g1, b1, ya_ref, dya, dg1, db1), (g2, b2, yb_ref, dyb, dg2, db2), (g3, b3, yc_ref, dyc, dg3, db3)):
            gate = _sigmoid(g_ref[...] + b_ref[...])
            dy_ref[...] = (dmv * gate).astype(BF16)
            dpre = dmv * y_ref[...] * gate * (1.0 - gate)
            dg_ref[...] = dpre.astype(BF16)
            part = jnp.sum(dpre, axis=0, keepdims=True)

            @pl.when(first)
            def _():
                db_ref[...] = part

            @pl.when(jnp.logical_not(first))
            def _():
                db_ref[...] += part

    blk = pl.BlockSpec((tm, _GT), lambda j, i: (i, j))
    vec = pl.BlockSpec((1, _GT), lambda j, i: (0, j))
    big = jax.ShapeDtypeStruct((S, D), BF16)
    small = jax.ShapeDtypeStruct((1, D), F32)
    return pl.pallas_call(
        body,
        name=name,
        grid=(D // _GT, S // tm),
        in_specs=_gate_specs(tm, lambda j, i: (i, j)) + [blk, blk, blk, blk],
        out_specs=[blk] * 6 + [vec] * 3,
        out_shape=[big] * 6 + [small] * 3,
        compiler_params=_params(("parallel", "arbitrary")),
    )(rest, rest, rest, bg, bg, bg, ya, yb, yc, dm)


_X_SCALE = XDH ** -0.5


def _xattn_fwd(xq, kv, *, name):
    S = xq.shape[0]
    M = kv.shape[0]
    tq = _tile(S, 512)

    def body(q_ref, k_ref, v_ref, o_ref):
        s = _dot(q_ref[...], k_ref[...], 1, 1) * _X_SCALE
        e = jnp.exp(s - jnp.max(s, axis=-1, keepdims=True))
        p = e / jnp.sum(e, axis=-1, keepdims=True)
        o_ref[...] = _dot(p.astype(BF16), v_ref[...], 1, 0).astype(BF16)

    return pl.pallas_call(
        body,
        name=name,
        grid=(S // tq, XH),
        in_specs=[
            pl.BlockSpec((tq, XDH), lambda i, h: (i, h)),
            pl.BlockSpec((M, XDH), lambda i, h: (0, h)),
            pl.BlockSpec((M, XDH), lambda i, h: (0, XH + h)),
        ],
        out_specs=pl.BlockSpec((tq, XDH), lambda i, h: (i, h)),
        out_shape=jax.ShapeDtypeStruct((S, D), BF16),
        compiler_params=_params(("parallel", "parallel")),
    )(xq, kv, kv)


def _xattn_bwd(xq, kv, do, *, name):
    S = xq.shape[0]
    M = kv.shape[0]
    tq = _tile(S, 512)

    def body(q_ref, k_ref, v_ref, do_ref, dq_ref, dk_ref, dv_ref):
        qb = q_ref[...]
        kb = k_ref[...]
        dob = do_ref[...]
        s = _dot(qb, kb, 1, 1) * _X_SCALE
        e = jnp.exp(s - jnp.max(s, axis=-1, keepdims=True))
        p = e / jnp.sum(e, axis=-1, keepdims=True)
        dp = _dot(dob, v_ref[...], 1, 1)
        ds = (p * (dp - jnp.sum(p * dp, axis=-1, keepdims=True)) * _X_SCALE).astype(BF16)
        dq_ref[...] = _dot(ds, kb, 1, 0).astype(BF16)
        dk_part = _dot(ds, qb, 0, 0)
        dv_part = _dot(p.astype(BF16), dob, 0, 0)

        @pl.when(pl.program_id(1) == 0)
        def _():
            dk_ref[...] = dk_part
            dv_ref[...] = dv_part

        @pl.when(pl.program_id(1) > 0)
        def _():
            dk_ref[...] += dk_part
            dv_ref[...] += dv_part

    qspec = pl.BlockSpec((tq, XDH), lambda h, i: (i, h))
    kspec = pl.BlockSpec((M, XDH), lambda h, i: (0, h))
    dxq, dxk, dxv = pl.pallas_call(
        body,
        name=name,
        grid=(XH, S // tq),
        in_specs=[qspec, kspec, pl.BlockSpec((M, XDH), lambda h, i: (0, XH + h)), qspec],
        out_specs=[qspec, kspec, kspec],
        out_shape=[jax.ShapeDtypeStruct((S, D), BF16), jax.ShapeDtypeStruct((M, D), F32), jax.ShapeDtypeStruct((M, D), F32)],
        compiler_params=_params(("parallel", "arbitrary")),
    )(xq, kv, kv, do)
    return dxq, jnp.concatenate([dxk, dxv], axis=1)


def _adam_math(w, g, m, v):
    m = ADAM_B1 * m + (1.0 - ADAM_B1) * g
    v = ADAM_B2 * v + (1.0 - ADAM_B2) * (g * g)
    m_hat = m / (1.0 - ADAM_B1 ** ADAM_STEP)
    v_hat = v / (1.0 - ADAM_B2 ** ADAM_STEP)
    delta = -ADAM_LR * (m_hat / (jnp.sqrt(v_hat) + ADAM_EPS) + ADAM_WD * w)
    return delta, m, v


def _adamw_sharded(parts, w, m, v, *, name):
    _, R, C = w.shape
    Cp = parts[0].shape[2]
    tm = _tile(R, 256)
    nr = R // tm

    def body(p0_ref, p1_ref, w_ref, m_ref, v_ref, g_ref, d_ref, mo_ref, vo_ref):
        def update(p_ref):
            g = p_ref[0][:, :C].astype(F32)
            for dev in range(1, N_DEV):
                g = g + p_ref[dev][:, :C].astype(F32)
            delta, mn, vn = _adam_math(w_ref[...], g, m_ref[...], v_ref[...])
            g_ref[...] = g
            d_ref[...] = delta
            mo_ref[...] = mn
            vo_ref[...] = vn

        @pl.when(pl.program_id(0) == 0)
        def _():
            update(p0_ref)

        @pl.when(pl.program_id(0) == 1)
        def _():
            update(p1_ref)

    p0 = pl.BlockSpec((N_DEV, tm, Cp), lambda l, i: (0, i * (1 - l) + (nr - 1) * l, 0))
    p1 = pl.BlockSpec((N_DEV, tm, Cp), lambda l, i: (0, i * l, 0))
    blk = pl.BlockSpec((None, tm, C), lambda l, i: (l, i, 0))
    sds = jax.ShapeDtypeStruct(w.shape, F32)
    return pl.pallas_call(
        body,
        name=name,
        grid=(DEPTH, nr),
        in_specs=[p0, p1, blk, blk, blk],
        out_specs=[blk] * 4,
        out_shape=[sds] * 4,
        compiler_params=_params(("arbitrary", "arbitrary")),
    )(parts[0], parts[1], w, m, v)


def _adamw_small(g, w, m, v, *, name):
    n = len(g)

    def body(*refs):
        g_refs, w_refs, m_refs, v_refs = (refs[k * n:(k + 1) * n] for k in range(4))
        d_out, m_out, v_out = (refs[(4 + k) * n:(5 + k) * n] for k in range(3))
        for t in range(n):
            delta, mn, vn = _adam_math(w_refs[t][...], g_refs[t][...], m_refs[t][...], v_refs[t][...])
            d_out[t][...] = delta
            m_out[t][...] = mn
            v_out[t][...] = vn

    vm = pl.BlockSpec(memory_space=pltpu.VMEM)
    shapes = [jax.ShapeDtypeStruct(a.shape, F32) for a in w]
    outs = pl.pallas_call(
        body,
        name=name,
        in_specs=[vm] * (4 * n),
        out_specs=[vm] * (3 * n),
        out_shape=shapes * 3,
        compiler_params=pltpu.CompilerParams(vmem_limit_bytes=VMEM_LIMIT),
    )(*g, *w, *m, *v)
    return outs[:n], outs[n:2 * n], outs[2 * n:]


def _position():
    return lax.axis_index("x"), lax.axis_index("y"), lax.axis_index("c")


def _dev_index(px, py, pc):
    return 4 * px + 2 * py + pc


_ANY = pl.BlockSpec(memory_space=pl.ANY)


def _all_gather(shards, *, name):
    n = len(shards)
    out_shape = [jax.ShapeDtypeStruct((N_DEV, *s.shape), s.dtype) for s in shards]
    n_pieces = len(_pieces(out_shape))

    def body(*refs):
        ins, outs = refs[:n], refs[n:2 * n]
        send_sems, recv_sems, local_sems = refs[2 * n:]
        x, y, c = _position()
        me, sibling = (x, y, c), (x, y, 1 - c)
        chips = [(1 - x, y), (x, 1 - y), (1 - x, 1 - y)]
        pieces = _pieces(outs)

        def copy(i, k, block, to, from_input=False):
            t, rows = pieces[i]
            dst = _cut(outs[t].at[_dev_index(*block)], rows)
            return pltpu.make_async_remote_copy(
                src_ref=_cut(ins[t], rows) if from_input else dst, dst_ref=dst, send_sem=send_sems.at[i, k],
                recv_sem=recv_sems.at[i, k], device_id=to, device_id_type=MESH)

        mine = [pltpu.make_async_copy(_cut(ins[t], rows), _cut(outs[t].at[_dev_index(*me)], rows), local_sems.at[i])
                for i, (t, rows) in enumerate(pieces)]
        for cp in mine:
            cp.start()
        started = []
        for j, chip in enumerate(chips):
            for i in range(n_pieces):
                started.append(copy(i, 1 + j, me, (*chip, c), from_input=True))
                started[-1].start()
        for i in range(n_pieces):
            started.append(copy(i, 0, me, sibling, from_input=True))
            started[-1].start()
        for j, chip in enumerate(chips):
            for i in range(n_pieces):
                copy(i, 1 + j, (*chip, c), me).wait_recv()
                started.append(copy(i, 4 + j, (*chip, c), sibling))
                started[-1].start()
        for i in range(n_pieces):
            copy(i, 0, sibling, me).wait_recv()
        for j, chip in enumerate(chips):
            for i in range(n_pieces):
                copy(i, 4 + j, (*chip, 1 - c), me).wait_recv()
        for cp in started:
            cp.wait_send()
        for cp in mine:
            cp.wait()

    return pl.pallas_call(
        body,
        name=name,
        in_specs=[_ANY] * n,
        out_specs=[_ANY] * n,
        out_shape=out_shape,
        scratch_shapes=[pltpu.SemaphoreType.DMA((n_pieces, 7)), pltpu.SemaphoreType.DMA((n_pieces, 7)),
                        pltpu.SemaphoreType.DMA((n_pieces,))],
        compiler_params=pltpu.CompilerParams(has_side_effects=True),
    )(*shards)


def _peers(x, y, c):
    out = []
    for mask in range(1, N_DEV):
        fx, fy, fc = (mask >> 2) & 1, (mask >> 1) & 1, mask & 1
        out.append((1 - x if fx else x, 1 - y if fy else y, 1 - c if fc else c))
    return out


_HBM = pl.BlockSpec(memory_space=pltpu.HBM)
_SEM = pl.BlockSpec(memory_space=pltpu.SEMAPHORE)


def _own_block_placed(block, like):
    x, y, c = _position()
    return lax.dynamic_update_index_in_dim(lax.empty(like.shape, like.dtype), block, _dev_index(x, y, c), 0)


_COPY_BYTES = 256 << 10
_MAX_PIECES = 8


def _pieces(blocks):
    out = []
    for t, b in enumerate(blocks):
        R, C = b.shape[-2:]
        n = max(1, min(_MAX_PIECES, R * C * jnp.dtype(b.dtype).itemsize // _COPY_BYTES))
        while n > 1 and R % (16 * n):
            n -= 1
        out += [(t, pl.ds(j * (R // n), R // n) if n > 1 else None) for j in range(n)]
    return out


def _cut(block, rows):
    return block if rows is None else block.at[rows]


def _copies(per_piece):
    def mark(fn):
        fn.per_piece = per_piece
        return fn
    return mark


@_copies(N_DEV - 1)
def _plan_exchange(srcs, lands, send_sems, recv_sems, arrivals):
    x, y, c = _position()
    me = _dev_index(x, y, c)
    out = []
    for k, peer in enumerate(_peers(x, y, c)):
        p = _dev_index(*peer)
        for i, (t, rows) in enumerate(_pieces(lands)):
            sems = dict(send_sem=send_sems.at[7 * i + k], recv_sem=recv_sems.at[7 * i + k], device_id=peer, device_id_type=MESH)
            src, dst = (lands[t].at[p], lands[t].at[p]) if arrivals else (srcs[t].at[p], lands[t].at[me])
            out.append(pltpu.make_async_remote_copy(src_ref=_cut(src, rows), dst_ref=_cut(dst, rows), **sems))
    return out


@_copies(N_DEV - 1)
def _plan_broadcast(srcs, lands, send_sems, recv_sems, arrivals):
    x, y, c = _position()
    me = _dev_index(x, y, c)
    out = []
    for k, peer in enumerate(_peers(x, y, c)):
        p = _dev_index(*peer)
        for i, (t, rows) in enumerate(_pieces(lands)):
            sems = dict(send_sem=send_sems.at[7 * i + k], recv_sem=recv_sems.at[7 * i + k], device_id=peer, device_id_type=MESH)
            src, dst = (lands[t].at[p], lands[t].at[p]) if arrivals else (srcs[t], lands[t].at[me])
            out.append(pltpu.make_async_remote_copy(src_ref=_cut(src, rows), dst_ref=_cut(dst, rows), **sems))
    return out


@_copies(4)
def _plan_gather_out(srcs, lands, send_sems, recv_sems, arrivals):
    x, y, c = _position()
    me = _dev_index(x, y, c)
    out = []
    for k, peer in enumerate([(x, y, 1 - c), (1 - x, y, c), (x, 1 - y, c), (1 - x, 1 - y, c)]):
        p = _dev_index(*peer)
        for i, (t, rows) in enumerate(_pieces(lands)):
            sems = dict(send_sem=send_sems.at[4 * i + k], recv_sem=recv_sems.at[4 * i + k], device_id=peer, device_id_type=MESH)
            src, dst = (lands[t].at[p], lands[t].at[p]) if arrivals else (srcs[t], lands[t].at[me])
            out.append(pltpu.make_async_remote_copy(src_ref=_cut(src, rows), dst_ref=_cut(dst, rows), **sems))
    return out


@_copies(3)
def _plan_gather_pass(srcs, lands, send_sems, recv_sems, arrivals):
    x, y, c = _position()
    sibling = (x, y, 1 - c)
    out = []
    for k, chip in enumerate([(1 - x, y), (x, 1 - y), (1 - x, 1 - y)]):
        p = _dev_index(*chip, 1 - c) if arrivals else _dev_index(*chip, c)
        for i, (t, rows) in enumerate(_pieces(lands)):
            sems = dict(send_sem=send_sems.at[3 * i + k], recv_sem=recv_sems.at[3 * i + k], device_id=sibling, device_id_type=MESH)
            block = _cut(lands[t].at[p], rows)
            out.append(pltpu.make_async_remote_copy(src_ref=block, dst_ref=block, **sems))
    return out


def _split_start(plan, srcs, lands, *, after=None, name):
    n_src, n = len(srcs), len(srcs) + len(lands)
    n_sem = plan.per_piece * len(_pieces(lands))
    order = [] if after is None else [after]

    def body(*refs):
        send_sems, recv_sems = refs[n + len(order):n + len(order) + 2]
        token = refs[-1]
        for cp in plan(refs[:n_src], refs[n_src:n], send_sems, recv_sems, arrivals=False):
            cp.start()
        token[...] = jnp.zeros_like(token)

    hbm = lambda a: pltpu.HBM(a.shape, a.dtype)
    outs = pl.pallas_call(
        body,
        name=name,
        in_specs=[_HBM] * n + [_ANY] * len(order),
        out_specs=[_SEM, _SEM] + [_HBM] * n + [pl.BlockSpec(memory_space=pltpu.VMEM)],
        out_shape=[pltpu.SemaphoreType.DMA((n_sem,)), pltpu.SemaphoreType.DMA((n_sem,))] + [hbm(a) for a in (*srcs, *lands)]
        + [jax.ShapeDtypeStruct(_TOKEN, F32)],
        input_output_aliases={i: 2 + i for i in range(n)},
        compiler_params=pltpu.CompilerParams(has_side_effects=pltpu.SideEffectType.DATAFLOW_SIDE_EFFECTING),
    )(*[pltpu.with_memory_space_constraint(a, pltpu.HBM) for a in (*srcs, *lands)], *order)
    return (outs[0], outs[1], outs[2:2 + n_src], outs[2 + n_src:2 + n]), outs[-1]


def _split_wait(plan, state, after, *, name):
    send_sems, recv_sems, srcs, lands = state
    n_src, n = len(srcs), len(srcs) + len(lands)

    def body(*refs):
        send_refs, recv_refs = refs[n:n + 2]
        for cp in plan(refs[:n_src], refs[n_src:n], send_refs, recv_refs, arrivals=False):
            cp.wait_send()
        for cp in plan(refs[:n_src], refs[n_src:n], send_refs, recv_refs, arrivals=True):
            cp.wait_recv()

    hbm = lambda a: pltpu.HBM(a.shape, a.dtype)
    outs = pl.pallas_call(
        body,
        name=name,
        in_specs=[_HBM] * n + [_SEM, _SEM, _ANY],
        out_specs=[_HBM] * n,
        out_shape=[hbm(a) for a in (*srcs, *lands)],
        input_output_aliases={i: i for i in range(n)},
        compiler_params=pltpu.CompilerParams(has_side_effects=pltpu.SideEffectType.DATAFLOW_SIDE_EFFECTING),
    )(*srcs, *lands, send_sems, recv_sems, after)
    return outs[n_src:]


def _sum_blocks(blocks, *, name):
    _, R, C = blocks.shape
    tm = next(R // n for n in (4, 3, 2, 1) if R % (8 * n) == 0)

    def body(b_ref, o_ref):
        g = b_ref[0]
        for dev in range(1, N_DEV):
            g = g + b_ref[dev]
        o_ref[...] = g

    return pl.pallas_call(
        body,
        name=name,
        grid=(R // tm,),
        in_specs=[pl.BlockSpec((N_DEV, tm, C), lambda i: (0, i, 0))],
        out_specs=pl.BlockSpec((tm, C), lambda i: (i, 0)),
        out_shape=jax.ShapeDtypeStruct((R, C), F32),
        compiler_params=_params(("parallel",)),
    )(blocks)


def _block_diag(w):
    out = jnp.zeros((POOL_W, POOL_W), w.dtype)
    for gi in range(4):
        out = out.at[64 * gi:64 * (gi + 1), 64 * gi:64 * (gi + 1)].set(w[gi])
    return out


def _layer_consts(sp, l):
    causal = jnp.tril(jnp.ones((SGU_CHUNK, SGU_CHUNK), F32))
    wm = (sp["sgu_w"][l] * causal[None]).astype(BF16)
    wbd = _block_diag(sp["pool_w"][l]).astype(BF16)
    return dict(
        wbd=wbd, wbd_t=wbd.T, wm=wm, wm_t=wm.transpose(0, 2, 1),
        sgu_bias=jnp.repeat(sp["sgu_b"][l].T, 64, axis=1),
        bpad=jnp.pad(sp["b_forget"][l], (0, F_LANES - FOX_H)).reshape(1, F_LANES),
        bg=sp["b_gate"][l].reshape(1, 3 * D),
    )


def _relu2(acc):
    return acc, jnp.square(jnp.maximum(acc, 0.0))


def _relu2_grad(acc, z):
    return (acc * 2.0 * jnp.maximum(z, 0.0),)


def _layer_fwd(l, x, mem, source, sp):
    S = x.shape[0]
    t = _tile(S, 256)
    c = _layer_consts(sp, l)
    n = f"l{l}_"
    W, after = source(l, "begin", x)
    h = _rms_fwd(x, sp["norm_mix_g"][l], after=after, name=n + "norm_mix")
    qkv = _mm(h, W["qkv"], out_dtypes=(BF16,), name=n + "qkv")
    rest = _mm(h, W["rest"], name=n + "rest")
    pa = _pool_fwd(rest, c["wbd"], sp["pool_scale"][l], name=n + "pool")
    cum, cum_t = _fox_prep(rest, c["bpad"], name=n + "fox_prep")
    fk3 = cum_t[:FOX_H].reshape(FOX_H, S // t, t)
    o, lse = _fox_fwd(qkv, cum, fk3, name=n + "fox")
    more, _ = source(l, "attended", o)
    W.update(more)
    sg = _sgu_fwd(rest, sp["sgu_norm_g"][l], c["wm"], c["sgu_bias"], name=n + "sgu")
    more, after = source(l, "mixed", sg)
    W.update(more)
    ya = _mm(pa, W["ba"], after=after, name=n + "branch_a")
    yb = _mm(o, W["bb"], name=n + "branch_b")
    yc = _mm(sg, W["bc"], name=n + "branch_c")
    merged = _merge_fwd(rest, c["bg"], ya, yb, yc, name=n + "merge")
    x1 = _mm(merged, W["out"], extras=(x,), epilogue=_add, name=n + "out")
    hx = _rms_fwd(x1, sp["norm_xattn_g"][l], name=n + "norm_xattn")
    hm = _rms_fwd(mem, sp["norm_mem_g"][l], name=n + "norm_mem")
    xq = _mm(hx, W["xq"], out_dtypes=(BF16,), name=n + "xq")
    kv = _mm(hm, W["xkv"], out_dtypes=(BF16,), name=n + "xkv")
    o2 = _xattn_fwd(xq, kv, name=n + "xattn")
    x2 = _mm(o2, W["xo"], extras=(x1,), epilogue=_add, name=n + "xo")
    hf = _rms_fwd(x2, sp["norm_ffn_g"][l], name=n + "norm_ffn")
    z, act = _mm(hf, W["ff1"], epilogue=_relu2, out_dtypes=(F32, BF16), name=n + "ff1")
    _, after = source(l, "expanded", act)
    x3 = _mm(act, W["ff2"], extras=(x2,), epilogue=_add, after=after, name=n + "ff2")
    saved = dict(x=x, h=h, qkv=qkv, rest=rest, pa=pa, cum=cum, fk3=fk3, o=o, lse=lse, sg=sg, ya=ya, yb=yb, yc=yc,
                 merged=merged, x1=x1, hx=hx, hm=hm, xq=xq, kv=kv, o2=o2, x2=x2, hf=hf, z=z, act=act, c=c)
    return x3, saved, W


def _layer_bwd(l, dx3, sv, mem, W, sp, grads_done):
    S = dx3.shape[0]
    c = sv["c"]
    n = f"l{l}b_"
    bf = dict(out_dtypes=(BF16,))
    gw, gs = {}, {}
    gw["ff2"] = _mm(sv["act"], dx3, ta=True, name=n + "dw_ff2", **bf)
    dz = _mm(dx3, W["ff2"], tb=True, extras=(sv["z"],), epilogue=_relu2_grad, name=n + "dz", **bf)
    gw["ff1"] = _mm(sv["hf"], dz, ta=True, shard_out=True, name=n + "dw_ff1", **bf)
    dhf = _mm(dz, W["ff1"], tb=True, name=n + "dhf")
    dx2, gs["norm_ffn_g"] = _rms_bwd(sv["x2"], sp["norm_ffn_g"][l], dhf, dx3, name=n + "dnorm_ffn")
    gw["xo"] = _mm(sv["o2"], dx2, ta=True, name=n + "dw_xo", **bf)
    do2 = _mm(dx2, W["xo"], tb=True, name=n + "do2", **bf)
    dxq, dkv = _xattn_bwd(sv["xq"], sv["kv"], do2, name=n + "dxattn")
    gw["xq"] = _mm(sv["hx"], dxq, ta=True, name=n + "dw_xq", **bf)
    gw["xkv"] = _mm(sv["hm"], dkv, ta=True, shard_out=True, name=n + "dw_xkv", **bf)
    dhm = _mm(dkv, W["xkv"], tb=True, name=n + "dhm")
    _, gs["norm_mem_g"] = _rms_bwd(mem, sp["norm_mem_g"][l], dhm, jnp.zeros_like(mem), name=n + "dnorm_mem")
    dhx = _mm(dxq, W["xq"], tb=True, name=n + "dhx")
    dx1, gs["norm_xattn_g"] = _rms_bwd(sv["x1"], sp["norm_xattn_g"][l], dhx, dx2, name=n + "dnorm_xattn")
    after, gw = grads_done(l, gw), {}
    gw["out"] = _mm(sv["merged"], dx1, ta=True, name=n + "dw_out", **bf)
    dm = _mm(dx1, W["out"], tb=True, after=after, name=n + "dmerged")
    dya, dyb, dyc, dg1, dg2, dg3, db1, db2, db3 = _merge_bwd(sv["rest"], c["bg"], sv["ya"], sv["yb"], sv["yc"], dm, name=n + "dmerge")
    gs["b_gate"] = jnp.concatenate([db1, db2, db3], axis=1).reshape(3 * D)
    gw["ba"] = _mm(sv["pa"], dya, ta=True, shard_out=True, name=n + "dw_ba", **bf)
    gw["bb"] = _mm(sv["o"], dyb, ta=True, shard_out=True, name=n + "dw_bb", **bf)
    gw["bc"] = _mm(sv["sg"], dyc, ta=True, shard_out=True, name=n + "dw_bc", **bf)
    after, gw = grads_done(l, gw), {}
    dpa = _mm(dya, W["ba"], tb=True, name=n + "dpa")
    do = _mm(dyb, W["bb"], tb=True, after=after, name=n + "do", **bf)
    dsg = _mm(dyc, W["bc"], tb=True, name=n + "dsg")
    da, dwbd, dscale = _pool_bwd(sv["rest"], c["wbd"], c["wbd_t"], sp["pool_scale"][l], dpa, name=n + "dpool")
    gs["pool_w"] = jnp.stack([dwbd[64 * gi:64 * (gi + 1), 64 * gi:64 * (gi + 1)] for gi in range(4)])
    gs["pool_scale"] = dscale.reshape(POOL_W)
    dq, dk, dv, dfq, dfk = _fox_bwd(sv["qkv"], sv["cum"], sv["fk3"], sv["o"], do, sv["lse"], name=n + "dfox")
    dcum = dfq + jnp.pad(dfk.reshape(FOX_H, S).T, ((0, 0), (0, F_LANES - FOX_H)))
    df, dbf = _fox_post(sv["rest"], c["bpad"], dcum, name=n + "dfox_post")
    gs["b_forget"] = dbf[0, :FOX_H]
    dc, dwm, dbias, dgn = _sgu_bwd(sv["rest"], sp["sgu_norm_g"][l], c["wm"], c["wm_t"], c["sgu_bias"], dsg, name=n + "dsgu")
    gs["sgu_w"] = dwm * jnp.tril(jnp.ones((SGU_CHUNK, SGU_CHUNK), F32))[None]
    gs["sgu_b"] = dbias.reshape(SGU_CHUNK, 4, 64).sum(axis=2).T
    gs["sgu_norm_g"] = dgn.reshape(SGU_W)
    dqkv = [dq, dk, dv]
    drest = [jnp.concatenate([da, df, jnp.zeros((S, OFF_C - OFF_F - F_LANES), BF16), dc], axis=1), dg1, dg2, dg3]
    gw["qkv"] = _mm(sv["h"], dqkv, ta=True, name=n + "dw_qkv", **bf)
    gw["rest"] = _mm(sv["h"], drest, ta=True, name=n + "dw_rest", **bf)
    after = grads_done(l, gw)
    dh = _mm(dqkv, W["qkv"], tb=True, after=after, name=n + "dh_qkv")
    dh = _mm(drest, W["rest"], tb=True, extras=(dh,), epilogue=_add, tm=1024, name=n + "dh")
    dx, gs["norm_mix_g"] = _rms_bwd(sv["x"], sp["norm_mix_g"][l], dh, dx1, name=n + "dnorm_mix")
    return dx, gs


def _local_step(x, mem, target, sp, source, grads_done):
    saved, Ws = [], []
    for l in range(DEPTH):
        x, sv, W = _layer_fwd(l, x, mem, source, sp)
        saved.append(sv)
        Ws.append(W)
    loss, dx, dgf = _final_loss(x, sp["final_norm_g"], target, name="final_loss")
    gss = [None] * DEPTH
    for l in reversed(range(DEPTH)):
        dx, gss[l] = _layer_bwd(l, dx, saved[l], mem, Ws[l], sp, grads_done)
    small = {k: jnp.stack([gss[l][k] for l in range(DEPTH)]) for k in gss[0]}
    small["final_norm_g"] = dgf
    return loss, dx, small


_SMALL = ["norm_mix_g", "b_forget", "pool_w", "pool_scale", "sgu_norm_g", "sgu_w", "sgu_b", "b_gate", "norm_xattn_g",
          "norm_mem_g", "norm_ffn_g", "final_norm_g"]
_COL = {"w_branch_a": "ba", "w_branch_b": "bb", "w_branch_c": "bc", "w_xkv": "xkv", "w_ff1": "ff1"}
_ROW = {"w_out": "out", "w_xq": "xq", "w_xo": "xo", "w_ff2": "ff2"}
_BIG = ["w_in", "w_branch_a", "w_branch_b", "w_branch_c", "w_out", "w_xq", "w_xkv", "w_xo", "w_ff1", "w_ff2"]
_PACK_LANES = 128


def _as_rows(a):
    return a.reshape(-1, a.shape[-1])


def _pack(tensors):
    rows = []
    for a in tensors:
        flat = a.reshape(-1)
        flat = jnp.pad(flat, (0, (-flat.shape[0]) % (8 * _PACK_LANES)))
        rows.append(flat.reshape(-1, _PACK_LANES))
    n_rows = sum(r.shape[0] for r in rows)
    rows.append(jnp.zeros(((-n_rows) % (8 * N_DEV), _PACK_LANES), F32))
    return jnp.concatenate(rows, axis=0)


def _unpack(packed, like):
    out, r = [], 0
    for a in like:
        size = math.prod(a.shape)
        nr = 8 * (-(-size // (8 * _PACK_LANES)))
        out.append(packed[r:r + nr].reshape(-1)[:size].reshape(a.shape))
        r += nr
    return out


_SHARD_IN = N_IN // N_DEV
_SHARD_IN_PAD = -(-_SHARD_IN // 128) * 128


def _columns(pieces, start, stop):
    out, at = [], 0
    for p in pieces:
        lo, hi = max(start, at), min(stop, at + p.shape[1])
        if lo < hi:
            out.append(p[:, lo - at:hi - at])
        at += p.shape[1]
    return out


def _split_w_in(blocks):
    K = blocks[0].shape[0]
    pad = jnp.zeros((K, OFF_C - OFF_F - FOX_H), blocks[0].dtype)
    cols = functools.partial(_columns, blocks)
    rest = jnp.concatenate(cols(0, R_OFF_Q) + cols(R_OFF_F, R_OFF_C) + [pad] + cols(R_OFF_C, N_IN), axis=1)
    return jnp.concatenate(cols(R_OFF_Q, R_OFF_F), axis=1), rest


def _join_w_in(qkv, rest):
    in_order = [rest[:, :R_OFF_Q], qkv, rest[:, OFF_F:OFF_F + FOX_H], rest[:, OFF_C:]]
    pad = jnp.zeros((qkv.shape[0], _SHARD_IN_PAD - _SHARD_IN), qkv.dtype)
    return jnp.stack([jnp.concatenate(_columns(in_order, _SHARD_IN * d, _SHARD_IN * (d + 1)) + [pad], axis=1) for d in range(N_DEV)])


_FIRST = ["w_in"]
_LATER = [k for k in _BIG if k not in _FIRST]


def _layer_weights(gathered):
    W = {}
    if "w_in" in gathered:
        W.update(zip(("qkv", "rest"), _split_w_in([gathered["w_in"][d][:, :_SHARD_IN] for d in range(N_DEV)])))
    for name, key in _COL.items():
        if name in gathered:
            W[key] = _Gathered(gathered[name])
    for name, key in _ROW.items():
        if name in gathered:
            W[key] = gathered[name].reshape(-1, gathered[name].shape[-1])
    return W


def _grad_blocks(gw):
    parts = {}
    if "qkv" in gw:
        parts["w_in"] = _join_w_in(gw["qkv"], gw["rest"])
    for name, key in _COL.items():
        if key in gw:
            parts[name] = gw[key]
    for name, key in _ROW.items():
        if key in gw:
            parts[name] = gw[key].reshape(N_DEV, -1, gw[key].shape[-1])
    return parts


def kernel(x, mem, norm_mix_g, w_in, b_forget, pool_w, pool_scale, sgu_norm_g, sgu_w, sgu_b, w_branch_a, w_branch_b, w_branch_c, b_gate, w_out, norm_xattn_g, norm_mem_g, w_xq, w_xkv, w_xo, norm_ffn_g, w_ff1, w_ff2, final_norm_g, loss_target, m_norm_mix_g, m_w_in, m_b_forget, m_pool_w, m_pool_scale, m_sgu_norm_g, m_sgu_w, m_sgu_b, m_w_branch_a, m_w_branch_b, m_w_branch_c, m_b_gate, m_w_out, m_norm_xattn_g, m_norm_mem_g, m_w_xq, m_w_xkv, m_w_xo, m_norm_ffn_g, m_w_ff1, m_w_ff2, m_final_norm_g, v_norm_mix_g, v_w_in, v_b_forget, v_pool_w, v_pool_scale, v_sgu_norm_g, v_sgu_w, v_sgu_b, v_w_branch_a, v_w_branch_b, v_w_branch_c, v_b_gate, v_w_out, v_norm_xattn_g, v_norm_mem_g, v_w_xq, v_w_xkv, v_w_xo, v_norm_ffn_g, v_w_ff1, v_w_ff2, v_final_norm_g):
    names = ["norm_mix_g", "w_in", "b_forget", "pool_w", "pool_scale", "sgu_norm_g", "sgu_w", "sgu_b", "w_branch_a", "w_branch_b",
             "w_branch_c", "b_gate", "w_out", "norm_xattn_g", "norm_mem_g", "w_xq", "w_xkv", "w_xo", "norm_ffn_g", "w_ff1", "w_ff2",
             "final_norm_g"]
    w = dict(zip(names, [norm_mix_g, w_in, b_forget, pool_w, pool_scale, sgu_norm_g, sgu_w, sgu_b, w_branch_a, w_branch_b, w_branch_c,
                         b_gate, w_out, norm_xattn_g, norm_mem_g, w_xq, w_xkv, w_xo, norm_ffn_g, w_ff1, w_ff2, final_norm_g]))
    m = dict(zip(names, [m_norm_mix_g, m_w_in, m_b_forget, m_pool_w, m_pool_scale, m_sgu_norm_g, m_sgu_w, m_sgu_b, m_w_branch_a,
                         m_w_branch_b, m_w_branch_c, m_b_gate, m_w_out, m_norm_xattn_g, m_norm_mem_g, m_w_xq, m_w_xkv, m_w_xo,
                         m_norm_ffn_g, m_w_ff1, m_w_ff2, m_final_norm_g]))
    v = dict(zip(names, [v_norm_mix_g, v_w_in, v_b_forget, v_pool_w, v_pool_scale, v_sgu_norm_g, v_sgu_w, v_sgu_b, v_w_branch_a,
                         v_w_branch_b, v_w_branch_c, v_b_gate, v_w_out, v_norm_xattn_g, v_norm_mem_g, v_w_xq, v_w_xkv, v_w_xo,
                         v_norm_ffn_g, v_w_ff1, v_w_ff2, v_final_norm_g]))

    sp = {k: w[k] for k in _SMALL}
    shards = [{k: w[k][l].astype(BF16) for k in _BIG} for l in range(DEPTH)]
    for sh in shards:
        sh["w_in"] = jnp.pad(sh["w_in"], ((0, 0), (0, _SHARD_IN_PAD - _SHARD_IN)))
    me = _dev_index(*_position())

    def gather_out(l, keys, name, after=None):
        srcs = [shards[l][k] for k in keys]
        lands = [_own_block_placed(a, jax.ShapeDtypeStruct((N_DEV, *a.shape), a.dtype)) for a in srcs]
        state, token = _split_start(_plan_gather_out, srcs, lands, after=after, name=name + "_out_start")
        return (keys, name, state), token

    def gather_pass(job, value):
        keys, name, state = job
        lands = _split_wait(_plan_gather_out, state, value, name=name + "_out_wait")
        state, token = _split_start(_plan_gather_pass, [], lands, name=name + "_pass_start")
        return (keys, name, state), token, lands[0]

    def gather_end(job, value):
        keys, name, state = job
        return _layer_weights(dict(zip(keys, _split_wait(_plan_gather_pass, state, value, name=name + "_pass_wait"))))

    jobs = {}

    def source(l, point, value):
        if (l, point) == (0, "begin"):
            first = _all_gather([shards[0][k] for k in _FIRST], name="gather_l0_first")
            jobs["l0"], token = gather_out(0, _LATER, "gather_l0", after=first[0])
            return _layer_weights(dict(zip(_FIRST, first))), token
        if (l, point) == (0, "attended"):
            jobs["l0"], _, arrived = gather_pass(jobs["l0"], value)
            jobs["l1"], jobs["token"] = gather_out(1, _BIG, "gather_l1", after=arrived)
            return {}, None
        if (l, point) == (0, "mixed"):
            return gather_end(jobs.pop("l0"), value), jobs.pop("token")
        if (l, point) == (0, "expanded"):
            jobs["l1"], token, _ = gather_pass(jobs["l1"], value)
            return {}, token
        if (l, point) == (1, "begin"):
            return gather_end(jobs.pop("l1"), value), None
        return {}, None

    received = [{} for _ in range(DEPTH)]
    travelling = []

    def grads_done(l, gw):
        blocks = _grad_blocks(gw)
        keys = [k for k in _BIG if k in blocks]
        parts = [blocks[k] for k in keys]
        group = f"exchange_grads_l{l}_" + ("in" if "w_in" in blocks else "merge" if "w_out" in blocks else "mlp")
        lands = [_own_block_placed(lax.dynamic_index_in_dim(p, me, 0, keepdims=False), p) for p in parts]
        state, token = _split_start(_plan_exchange, parts, lands, name=group + "_start")
        travelling.append((l, keys, state, group + "_wait"))
        return token

    loss, dx, small = _local_step(x[0], mem[0], loss_target[0], sp, source, grads_done)
    grads, deltas, new_m, new_v = {}, {}, {}, {}
    like = [loss] + [w[k] for k in _SMALL]
    packed = _pack([loss] + [small[k] for k in _SMALL])
    eighths = packed.reshape(N_DEV, -1, _PACK_LANES)
    own = lambda a: _own_block_placed(lax.dynamic_index_in_dim(a, me, 0, keepdims=False) if a.ndim == 3 else a, eighths)
    scatter, done = _split_start(_plan_exchange, [eighths], [own(eighths)], after=dx, name="small_grads_scatter_start")

    def reduce_small(after):
        mine = _sum_blocks(_split_wait(_plan_exchange, scatter, after, name="small_grads_scatter_wait")[0], name="small_grads_sum")
        return _split_start(_plan_broadcast, [mine], [own(mine)], name="small_grads_gather_start")

    def update_small(state, after):
        total = _split_wait(_plan_broadcast, state, after, name="small_grads_gather_wait")[0].reshape(packed.shape)
        loss_sum, *g_small = _unpack(total, like)
        rows = lambda d: [_as_rows(d[k]) for k in _SMALL]
        outs = _adamw_small([_as_rows(g) for g in g_small], rows(w), rows(m), rows(v), name="adamw_small")
        grads.update(zip(_SMALL, g_small))
        for dst, vals in zip((deltas, new_m, new_v), outs):
            dst.update({k: a.reshape(w[k].shape) for k, a in zip(_SMALL, vals)})
        return loss_sum[0, 0], outs[0][0]

    groups = list(dict.fromkeys(tuple(keys) for _, keys, _, _ in travelling))
    for n_done, group_keys in enumerate(groups):
        if n_done == 1:
            gather, _ = reduce_small(done)
        if n_done == len(groups) - 1:
            loss, done = update_small(gather, done)
        for l, keys, state, wait_name in travelling:
            if tuple(keys) == group_keys:
                received[l].update(zip(keys, _split_wait(_plan_exchange, state, done, name=wait_name)))
        for k in group_keys:
            outs = _adamw_sharded([received[l][k] for l in range(DEPTH)], w[k], m[k], v[k], name="adamw_" + k)
            grads[k], deltas[k], new_m[k], new_v[k] = outs
        done = grads[group_keys[-1]]

    return (loss, dx[None], *[grads[k] for k in names], *[deltas[k] for k in names], *[new_m[k] for k in names],
            *[new_v[k] for k in names])
```

```python
import functools
import math

import jax
import jax.numpy as jnp
from jax import lax
from jax.experimental import pallas as pl
from jax.experimental.pallas import tpu as pltpu

F32 = jnp.float32
BF16 = jnp.bfloat16
MESH = pl.DeviceIdType.MESH

N_DEV = 8
D = 1024
DEPTH = 2
EPS = 1e-6
NEG = -1e30
POOL_W = 256
FOX_H = 8
FOX_DH = 64
FOX_W = 512
SGU_W = 256
SGU_CHUNK = 128
XH = 4
XDH = 256
N_IN = 5384
R_OFF_Q, R_OFF_F, R_OFF_C = 256, 1792, 1800
QKV_W = 3 * FOX_W
OFF_A, OFF_F, OFF_C, OFF_G, REST_W = 0, 256, 512, 1024, 4096
F_LANES = 128

ADAM_LR = 0.001
ADAM_B1 = 0.9
ADAM_B2 = 0.999
ADAM_EPS = 1e-08
ADAM_WD = 0.01
ADAM_STEP = 10

VMEM_LIMIT = 56 * 1024 * 1024


def _tile(n, pref):
    t = min(n, pref)
    while n % t:
        t -= 128
    assert t > 0, (n, pref)
    return t


def _params(sem=None):
    return pltpu.CompilerParams(dimension_semantics=sem, vmem_limit_bytes=VMEM_LIMIT)


def _dot(a, b, ca, cb):
    return lax.dot_general(a, b, (((ca,), (cb,)), ((), ())), preferred_element_type=F32)


def _sigmoid(z):
    return 1.0 / (1.0 + jnp.exp(-z))


_GELU_K = math.sqrt(2.0 / math.pi)
_GELU_C = 0.044715


def _gelu(x):
    return 0.5 * x * (1.0 + jnp.tanh(_GELU_K * (x + _GELU_C * x * x * x)))


def _gelu_grad(x):
    t = jnp.tanh(_GELU_K * (x + _GELU_C * x * x * x))
    return 0.5 * (1.0 + t) + 0.5 * x * (1.0 - t * t) * _GELU_K * (1.0 + 3.0 * _GELU_C * x * x)


def _rows(shape):
    return lax.broadcasted_iota(jnp.int32, shape, 0)


def _lanes(shape):
    return lax.broadcasted_iota(jnp.int32, shape, 1)


class _Gathered:
    def __init__(self, arr):
        self.arr = arr
        self.shape = (arr.shape[1], N_DEV * arr.shape[2])


_TOKEN = (8, 128)


def _mm(a, b, *, ta=False, tb=False, extras=(), epilogue=None, out_dtypes=(F32,), shard_out=False, after=None, tm=None, tn=512, tk=None,
        name):
    a_parts = list(a) if isinstance(a, (list, tuple)) else [a]
    b_parts = list(b) if isinstance(b, (list, tuple)) else [b]
    gathered = isinstance(b, _Gathered)
    assert (len(a_parts) == 1 or not ta) and (len(b_parts) == 1 or not tb) and min(len(a_parts), len(b_parts)) == 1
    a0, b0 = a_parts[0], b_parts[0]
    M, K = (a0.shape[1], a0.shape[0]) if ta else (a0.shape[0], a0.shape[1] * len(a_parts))
    N, Kb = b0.shape if tb else (b0.shape[1] * len(b_parts), b0.shape[0])
    assert Kb == K, (a0.shape, b0.shape, ta, tb)
    if gathered:
        if tb:
            tk = b.arr.shape[2]
        else:
            tn = b.arr.shape[2]
    if len(a_parts) > 1:
        tk = a0.shape[1]
    if shard_out:
        tn = N // N_DEV
    tm = _tile(M, tm or (1024 if ta else 2048))
    tn = _tile(b0.shape[1] if len(b_parts) > 1 else N, tn)
    per_piece = b0.shape[1] // tn
    size = lambda dt: jnp.dtype(dt).itemsize
    row_bytes = len(a_parts) * tm * size(a0.dtype) + len(b_parts) * tn * size(b.arr.dtype if gathered else b0.dtype)
    tile_bytes = tm * tn * (sum(size(e.dtype) for e in extras) + sum(map(size, out_dtypes)))

    def vmem_bytes(k_tile):
        return 2 * (k_tile * row_bytes + tile_bytes) + tm * tn * 4 * (K > k_tile)

    if tk is None:
        tk = next(c for c in (_tile(K, 2048), _tile(K, 1024), _tile(K, 512), _tile(K, 256)) if vmem_bytes(c) <= VMEM_LIMIT - (4 << 20))
    tk = _tile(K, tk)
    nk = K // tk
    ca, cb = (0 if ta else 1), (1 if tb else 0)
    n_a, n_b, n_ex, n_out = len(a_parts), len(b_parts), len(extras), len(out_dtypes)
    tokens = [] if after is None else [after]
    n_in = n_a + n_b + n_ex + len(tokens)
    if epilogue is None:
        epilogue = lambda acc: (acc,)

    def body(*refs):
        a_refs, b_refs = refs[:n_a], refs[n_a:n_a + n_b]
        ex_refs = refs[n_a + n_b:n_a + n_b + n_ex]
        o_refs = refs[n_in:n_in + n_out]
        j, k = pl.program_id(1), pl.program_id(2)

        def finish(acc):
            for o_ref, val in zip(o_refs, epilogue(acc, *[e[...] for e in ex_refs])):
                o_ref[...] = val.astype(o_ref.dtype)

        def step(a_ref, b_ref):
            part = _dot(a_ref[...].astype(BF16), b_ref[...].astype(BF16), ca, cb)
            if nk == 1:
                finish(part)
            else:
                acc_ref = refs[-1]

                @pl.when(k == 0)
                def _():
                    acc_ref[...] = part

                @pl.when(k > 0)
                def _():
                    acc_ref[...] += part

                @pl.when(k == nk - 1)
                def _():
                    finish(acc_ref[...])

        if n_a > 1:
            for p in range(n_a):
                pl.when(k == p)(functools.partial(step, a_refs[p], b_refs[0]))
        elif n_b > 1:
            for p in range(n_b):
                pl.when(j // per_piece == p)(functools.partial(step, a_refs[0], b_refs[p]))
        else:
            step(a_refs[0], b_refs[0])

    if n_a > 1:
        a_specs = [pl.BlockSpec((tm, tk), lambda i, j, k: (i, 0))] * n_a
    else:
        a_specs = [pl.BlockSpec((tk, tm), lambda i, j, k: (k, i)) if ta else pl.BlockSpec((tm, tk), lambda i, j, k: (i, k))]
    if gathered:
        b_arrs = [b.arr]
        b_specs = [pl.BlockSpec((None, tn, tk), lambda i, j, k: (k, j, 0)) if tb else pl.BlockSpec((None, tk, tn), lambda i, j, k: (j, k, 0))]
    elif n_b > 1:
        b_arrs = b_parts
        b_specs = [pl.BlockSpec((tk, tn), functools.partial(lambda p, i, j, k: (k, jnp.clip(j - p * per_piece, 0, per_piece - 1)), p))
                   for p in range(n_b)]
    else:
        b_arrs = b_parts
        b_specs = [pl.BlockSpec((tn, tk), lambda i, j, k: (j, k)) if tb else pl.BlockSpec((tk, tn), lambda i, j, k: (k, j))]
    tile = pl.BlockSpec((tm, tn), lambda i, j, k: (i, j))
    if shard_out:
        out_specs = [pl.BlockSpec((None, tm, tn), lambda i, j, k: (j, i, 0))] * n_out
        out_shape = [jax.ShapeDtypeStruct((N_DEV, M, tn), dt) for dt in out_dtypes]
    else:
        out_specs = [tile] * n_out
        out_shape = [jax.ShapeDtypeStruct((M, N), dt) for dt in out_dtypes]
    assert vmem_bytes(tk) <= VMEM_LIMIT - (4 << 20), (name, vmem_bytes(tk))
    outs = pl.pallas_call(
        body,
        name=name,
        grid=(M // tm, N // tn, nk),
        in_specs=a_specs + b_specs + [tile] * n_ex + [pl.BlockSpec(_TOKEN, lambda i, j, k: (0, 0))] * len(tokens),
        out_specs=out_specs,
        out_shape=out_shape,
        scratch_shapes=[pltpu.VMEM((tm, tn), F32)] if nk > 1 else [],
        compiler_params=_params(("parallel", "parallel", "arbitrary")),
    )(*a_parts, *b_arrs, *extras, *tokens)
    return outs[0] if n_out == 1 else outs


def _add(acc, res):
    return (acc + res,)


def _rms_fwd(x, g, *, after=None, name):
    R, C = x.shape
    tm = _tile(R, 256)
    tokens = [] if after is None else [after]

    def body(x_ref, g_ref, *rest):
        xv = x_ref[...]
        r = lax.rsqrt(jnp.mean(xv * xv, axis=-1, keepdims=True) + EPS)
        rest[-1][...] = (xv * r * g_ref[...]).astype(BF16)

    return pl.pallas_call(
        body,
        name=name,
        grid=(R // tm,),
        in_specs=[pl.BlockSpec((tm, C), lambda i: (i, 0)), pl.BlockSpec((1, C), lambda i: (0, 0))]
        + [pl.BlockSpec(_TOKEN, lambda i: (0, 0))] * len(tokens),
        out_specs=pl.BlockSpec((tm, C), lambda i: (i, 0)),
        out_shape=jax.ShapeDtypeStruct((R, C), BF16),
        compiler_params=_params(("parallel",)),
    )(x, g.reshape(1, C), *tokens)


def _rms_bwd(x, g, dh, dres, *, name):
    R, C = x.shape
    tm = _tile(R, 256)

    def body(x_ref, g_ref, dh_ref, dres_ref, dx_ref, dg_ref):
        xv = x_ref[...]
        r = lax.rsqrt(jnp.mean(xv * xv, axis=-1, keepdims=True) + EPS)
        xn = xv * r
        dh_v = dh_ref[...].astype(F32)
        dxn = dh_v * g_ref[...]
        dx_ref[...] = r * (dxn - xn * jnp.mean(dxn * xn, axis=-1, keepdims=True)) + dres_ref[...]
        part = jnp.sum(dh_v * xn, axis=0, keepdims=True)

        @pl.when(pl.program_id(0) == 0)
        def _():
            dg_ref[...] = part

        @pl.when(pl.program_id(0) > 0)
        def _():
            dg_ref[...] += part

    row = pl.BlockSpec((tm, C), lambda i: (i, 0))
    vec = pl.BlockSpec((1, C), lambda i: (0, 0))
    dx, dg = pl.pallas_call(
        body,
        name=name,
        grid=(R // tm,),
        in_specs=[row, vec, row, row],
        out_specs=[row, vec],
        out_shape=[jax.ShapeDtypeStruct((R, C), F32), jax.ShapeDtypeStruct((1, C), F32)],
        compiler_params=_params(("arbitrary",)),
    )(x, g.reshape(1, C), dh, dres)
    return dx, dg.reshape(C)


def _final_loss(x, g, target, *, name):
    R, C = x.shape
    tm = _tile(R, 256)

    def body(x_ref, g_ref, t_ref, loss_ref, dx_ref, dg_ref):
        xv = x_ref[...]
        r = lax.rsqrt(jnp.mean(xv * xv, axis=-1, keepdims=True) + EPS)
        xn = xv * r
        gv = g_ref[...]
        err = xn * gv - t_ref[...]
        lpart = (0.5 / C) * jnp.sum(jnp.sum(err * err, axis=1, keepdims=True), axis=0, keepdims=True)
        dy = err * (1.0 / C)
        dxn = dy * gv
        dx_ref[...] = r * (dxn - xn * jnp.mean(dxn * xn, axis=-1, keepdims=True))
        gpart = jnp.sum(dy * xn, axis=0, keepdims=True)

        @pl.when(pl.program_id(0) == 0)
        def _():
            loss_ref[...] = lpart
            dg_ref[...] = gpart

        @pl.when(pl.program_id(0) > 0)
        def _():
            loss_ref[...] += lpart
            dg_ref[...] += gpart

    row = pl.BlockSpec((tm, C), lambda i: (i, 0))
    vec = pl.BlockSpec((1, C), lambda i: (0, 0))
    loss, dx, dg = pl.pallas_call(
        body,
        name=name,
        grid=(R // tm,),
        in_specs=[row, vec, row],
        out_specs=[pl.BlockSpec((1, 1), lambda i: (0, 0)), row, vec],
        out_shape=[jax.ShapeDtypeStruct((1, 1), F32), jax.ShapeDtypeStruct((R, C), F32), jax.ShapeDtypeStruct((1, C), F32)],
        compiler_params=_params(("arbitrary",)),
    )(x, g.reshape(1, C), target)
    return loss, dx, dg.reshape(C)


def _pool_select(lane, vals):
    out = vals[3]
    for gi in (2, 1, 0):
        out = jnp.where(lane < 64 * (gi + 1), vals[gi], out)
    return out


def _pool_diff(a):
    row, lane = _rows(a.shape), _lanes(a.shape)

    def down(v, k):
        return jnp.where(row >= k, pltpu.roll(v, k, 0), 0.0)

    s2 = a + down(a, 1)
    s4 = s2 + down(s2, 2)
    s8 = s4 + down(s4, 4)
    s16 = s8 + down(s8, 8)
    wsum = _pool_select(lane, (s2, s4, s8, s16))
    win = _pool_select(lane, (2, 4, 8, 16))
    cnt = jnp.minimum(row + 1, win).astype(F32)
    return wsum / cnt - a, cnt


def _pool_diff_t(dd, cnt):
    S = dd.shape[0]
    row, lane = _rows(dd.shape), _lanes(dd.shape)

    def up(v, k):
        return jnp.where(row < S - k, pltpu.roll(v, S - k, 0), 0.0)

    e = dd / cnt
    s2 = e + up(e, 1)
    s4 = s2 + up(s2, 2)
    s8 = s4 + up(s4, 4)
    s16 = s8 + up(s8, 8)
    return _pool_select(lane, (s2, s4, s8, s16)) - dd


def _pool_fwd(rest, wbd, scale, *, name):
    S = rest.shape[0]

    def body(a_ref, w_ref, s_ref, o_ref):
        d, _ = _pool_diff(a_ref[...])
        yp = _dot(d.astype(BF16), w_ref[...], 1, 0)
        o_ref[...] = (yp * s_ref[...]).astype(BF16)

    return pl.pallas_call(
        body,
        name=name,
        grid=(1,),
        in_specs=[
            pl.BlockSpec((S, POOL_W), lambda i: (0, OFF_A // POOL_W)),
            pl.BlockSpec((POOL_W, POOL_W), lambda i: (0, 0)),
            pl.BlockSpec((1, POOL_W), lambda i: (0, 0)),
        ],
        out_specs=pl.BlockSpec((S, POOL_W), lambda i: (0, 0)),
        out_shape=jax.ShapeDtypeStruct((S, POOL_W), BF16),
        compiler_params=_params(("arbitrary",)),
    )(rest, wbd, scale.reshape(1, POOL_W))


def _pool_bwd(rest, wbd, wbd_t, scale, dpa, *, name):
    S = rest.shape[0]

    def body(a_ref, w_ref, wt_ref, s_ref, dpa_ref, da_ref, dw_ref, ds_ref):
        d, cnt = _pool_diff(a_ref[...])
        db = d.astype(BF16)
        yp = _dot(db, w_ref[...], 1, 0)
        dpa_v = dpa_ref[...]
        ds_ref[...] = jnp.sum(dpa_v * yp, axis=0, keepdims=True)
        dyp = (dpa_v * s_ref[...]).astype(BF16)
        dw_ref[...] = _dot(db, dyp, 0, 0)
        dd = _dot(dyp, wt_ref[...], 1, 0)
        da_ref[...] = _pool_diff_t(dd, cnt).astype(BF16)

    full = pl.BlockSpec((S, POOL_W), lambda i: (0, 0))
    sq = pl.BlockSpec((POOL_W, POOL_W), lambda i: (0, 0))
    vec = pl.BlockSpec((1, POOL_W), lambda i: (0, 0))
    return pl.pallas_call(
        body,
        name=name,
        grid=(1,),
        in_specs=[pl.BlockSpec((S, POOL_W), lambda i: (0, OFF_A // POOL_W)), sq, sq, vec, full],
        out_specs=[full, sq, vec],
        out_shape=[
            jax.ShapeDtypeStruct((S, POOL_W), BF16),
            jax.ShapeDtypeStruct((POOL_W, POOL_W), F32),
            jax.ShapeDtypeStruct((1, POOL_W), F32),
        ],
        compiler_params=_params(("arbitrary",)),
    )(rest, wbd, wbd_t, scale.reshape(1, POOL_W), dpa)


def _log_sigmoid(z):
    return jnp.minimum(z, 0.0) - jnp.log(1.0 + jnp.exp(-jnp.abs(z)))


_F_SPEC_COL = OFF_F // F_LANES


def _fox_prep(rest, bpad, *, name):
    S = rest.shape[0]

    def body(f_ref, b_ref, o_ref, ot_ref):
        acc = _log_sigmoid(f_ref[...] + b_ref[...])
        row = _rows(acc.shape)
        k = 1
        while k < S:
            acc = acc + jnp.where(row >= k, pltpu.roll(acc, k, 0), 0.0)
            k *= 2
        o_ref[...] = acc
        ot_ref[...] = acc.T

    return pl.pallas_call(
        body,
        name=name,
        grid=(1,),
        in_specs=[pl.BlockSpec((S, F_LANES), lambda i: (0, _F_SPEC_COL)), pl.BlockSpec((1, F_LANES), lambda i: (0, 0))],
        out_specs=[pl.BlockSpec((S, F_LANES), lambda i: (0, 0)), pl.BlockSpec((F_LANES, S), lambda i: (0, 0))],
        out_shape=[jax.ShapeDtypeStruct((S, F_LANES), F32), jax.ShapeDtypeStruct((F_LANES, S), F32)],
        compiler_params=_params(("arbitrary",)),
    )(rest, bpad)


def _fox_post(rest, bpad, dcum, *, name):
    S = rest.shape[0]

    def body(f_ref, b_ref, d_ref, df_ref, db_ref):
        acc = d_ref[...]
        row = _rows(acc.shape)
        k = 1
        while k < S:
            acc = acc + jnp.where(row < S - k, pltpu.roll(acc, S - k, 0), 0.0)
            k *= 2
        df = acc * (1.0 - _sigmoid(f_ref[...] + b_ref[...]))
        df_ref[...] = df.astype(BF16)
        db_ref[...] = jnp.sum(df, axis=0, keepdims=True)

    full = pl.BlockSpec((S, F_LANES), lambda i: (0, 0))
    vec = pl.BlockSpec((1, F_LANES), lambda i: (0, 0))
    return pl.pallas_call(
        body,
        name=name,
        grid=(1,),
        in_specs=[pl.BlockSpec((S, F_LANES), lambda i: (0, _F_SPEC_COL)), vec, full],
        out_specs=[full, vec],
        out_shape=[jax.ShapeDtypeStruct((S, F_LANES), BF16), jax.ShapeDtypeStruct((1, F_LANES), F32)],
        compiler_params=_params(("arbitrary",)),
    )(rest, bpad, dcum)


_FOX_SCALE = FOX_DH ** -0.5
_PAIRS = FOX_H // 2


def _scaled(v):
    return (v.astype(F32) * _FOX_SCALE).astype(BF16)


def _diag_mask(s):
    return jnp.where(_rows(s.shape) >= _lanes(s.shape), s, NEG)


def _fox_fwd(qkv, cum, fk3, *, name):
    S = qkv.shape[0]
    nk, t = fk3.shape[1:]

    def body(q_ref, k_ref, v_ref, cum_ref, fk_ref, o_ref, lse_ref):
        i = pl.program_id(0)
        lane = _lanes((t, 128))
        lo = lane < FOX_DH
        cumv = cum_ref[...]
        qm, fq = [], []
        for h in range(FOX_H):
            qs = _scaled(q_ref[:, 128 * (h // 2):128 * (h // 2 + 1)])
            zero = jnp.zeros_like(qs)
            qm.append(jnp.where(lo, qs, zero) if h % 2 == 0 else jnp.where(lo, zero, qs))
            fq.append(cumv[:, h:h + 1])

        def tile(j, state, masked):
            m, acc, lsum = (list(part) for part in state)
            k0 = pl.multiple_of(j * t, t)
            for hp in range(_PAIRS):
                cols = slice(128 * hp, 128 * (hp + 1))
                kb = k_ref[pl.ds(k0, t), cols]
                vb = v_ref[pl.ds(k0, t), cols]
                one = jnp.ones_like(vb)
                alphas, pvs = [], []
                for h in (2 * hp, 2 * hp + 1):
                    s = _dot(qm[h], kb, 1, 1) + fq[h] - fk_ref[h, pl.ds(j, 1), :]
                    if masked:
                        s = _diag_mask(s)
                    m_new = jnp.maximum(m[h], jnp.max(s, axis=-1, keepdims=True))
                    p = jnp.exp(s - m_new)
                    alphas.append(jnp.exp(m[h] - m_new))
                    m[h] = m_new
                    pvs.append(_dot(p.astype(BF16), jnp.where(lo, vb, one) if h % 2 == 0 else jnp.where(lo, one, vb), 1, 0))
                acc[hp] = jnp.where(lo, alphas[0], alphas[1]) * acc[hp] + jnp.where(lo, pvs[0], pvs[1])
                lsum[hp] = jnp.where(lo, alphas[1], alphas[0]) * lsum[hp] + jnp.where(lo, pvs[1], pvs[0])
            return tuple(m), tuple(acc), tuple(lsum)

        zeros = (jnp.zeros((t, 128), F32),) * _PAIRS
        state = lax.fori_loop(0, i, functools.partial(tile, masked=False), ((jnp.full((t, 1), NEG, F32),) * FOX_H, zeros, zeros))
        m, acc, lsum = tile(i, state, True)
        for hp in range(_PAIRS):
            o_ref[:, 128 * hp:128 * (hp + 1)] = acc[hp] / pltpu.roll(lsum[hp], FOX_DH, 1)
            lse = [m[2 * hp] + jnp.log(lsum[hp][:, FOX_DH:FOX_DH + 1]), m[2 * hp + 1] + jnp.log(lsum[hp][:, 0:1])]
            lse_ref[hp] = jnp.where(lane == 0, lse[0], jnp.where(lane == 1, lse[1], 0.0))

    whole = lambda col: pl.BlockSpec((S, FOX_W), lambda i: (0, col))
    return pl.pallas_call(
        body,
        name=name,
        grid=(S // t,),
        in_specs=[
            pl.BlockSpec((t, FOX_W), lambda i: (i, 0)), whole(1), whole(2),
            pl.BlockSpec((t, F_LANES), lambda i: (i, 0)),
            pl.BlockSpec((FOX_H, nk, t), lambda i: (0, 0, 0)),
        ],
        out_specs=[pl.BlockSpec((t, FOX_W), lambda i: (i, 0)), pl.BlockSpec((_PAIRS, t, 128), lambda i: (0, i, 0))],
        out_shape=[jax.ShapeDtypeStruct((S, FOX_W), F32), jax.ShapeDtypeStruct((_PAIRS, S, 128), F32)],
        compiler_params=_params(("arbitrary",)),
    )(qkv, qkv, qkv, cum, fk3)


def _fox_bwd(qkv, cum, fk3, o, do, lse, *, name):
    S = qkv.shape[0]
    nk, t = fk3.shape[1:]
    q_at, k_at, v_at = 0, FOX_W, 2 * FOX_W

    def body(qkv_ref, cum_ref, fk_ref, o_ref, do_ref, lse_ref, dq_ref, dk_ref, dv_ref, dfq_ref, dfk_ref,
             qs_sc, ks_sc, delta_sc, dq_sc):
        lane = _lanes((t, 128))
        lo = lane < FOX_DH
        mine = lambda h: lo if h % 2 == 0 else jnp.logical_not(lo)

        def by_head(tile, values):
            for h, val in enumerate(values):
                tile = jnp.where(lane == h, val, tile)
            return tile

        def prep(i, carry):
            r = pl.ds(pl.multiple_of(i * t, t), t)
            qs_sc[r, :] = _scaled(qkv_ref[r, q_at:q_at + FOX_W])
            ks_sc[r, :] = _scaled(qkv_ref[r, k_at:k_at + FOX_W])
            sums = []
            for hp in range(_PAIRS):
                cols = slice(128 * hp, 128 * (hp + 1))
                prod = do_ref[r, cols].astype(F32) * o_ref[r, cols]
                sums += [jnp.sum(jnp.where(mine(h), prod, 0.0), axis=-1, keepdims=True) for h in (2 * hp, 2 * hp + 1)]
            delta_sc[r, :] = by_head(jnp.zeros((t, 128), F32), sums)
            dfq_ref[r, :] = jnp.zeros((t, 128), F32)
            dq_sc[r, :] = jnp.zeros((t, FOX_W), F32)
            return carry

        lax.fori_loop(0, nk, prep, 0)

        def kv_tile(j, carry):
            kr = pl.ds(pl.multiple_of(j * t, t), t)

            def q_tile(i, acc, masked):
                dk, dv, dfk = list(acc[:_PAIRS]), list(acc[_PAIRS:2 * _PAIRS]), list(acc[2 * _PAIRS:])
                qr = pl.ds(pl.multiple_of(i * t, t), t)
                delta_t, cum_t, dq_old, dfq_old = delta_sc[qr, :], cum_ref[qr, :], dq_sc[qr, :], dfq_ref[qr, :]
                row_sums, dq_new = [], []
                for hp in range(_PAIRS):
                    cols = slice(128 * hp, 128 * (hp + 1))
                    kb = qkv_ref[kr, k_at + 128 * hp:k_at + 128 * (hp + 1)]
                    vb = qkv_ref[kr, v_at + 128 * hp:v_at + 128 * (hp + 1)]
                    ksb, qsb, dob = ks_sc[kr, cols], qs_sc[qr, cols], do_ref[qr, cols]
                    zero = jnp.zeros_like(qsb)
                    dq_t = jnp.zeros((t, 128), F32)
                    for h in (2 * hp, 2 * hp + 1):
                        qe, doe, ke = (jnp.where(mine(h), a, zero) for a in (qsb, dob, ksb))
                        s = _dot(qe, kb, 1, 1) + cum_t[:, h:h + 1] - fk_ref[h, pl.ds(j, 1), :]
                        if masked:
                            s = _diag_mask(s)
                        p = jnp.exp(s - lse_ref[hp, qr, h % 2:h % 2 + 1])
                        dv[hp] = dv[hp] + _dot(p.astype(BF16), doe, 0, 0)
                        dp = _dot(doe, vb, 1, 1)
                        ds = p * (dp - delta_t[:, h:h + 1])
                        dsb = ds.astype(BF16)
                        dk[hp] = dk[hp] + _dot(dsb, qe, 0, 0)
                        dq_t = dq_t + _dot(dsb, ke, 1, 0)
                        row_sums.append(jnp.sum(ds, axis=-1, keepdims=True))
                        dfk[h] = dfk[h] - jnp.sum(ds, axis=0, keepdims=True)
                    dq_new.append(dq_old[:, cols] + dq_t)
                for hp in range(_PAIRS):
                    dq_sc[qr, 128 * hp:128 * (hp + 1)] = dq_new[hp]
                dfq_ref[qr, :] = dfq_old + by_head(jnp.zeros((t, 128), F32), row_sums)
                return (*dk, *dv, *dfk)

            init = tuple([jnp.zeros((t, 128), F32)] * (2 * _PAIRS) + [jnp.zeros((1, t), F32)] * FOX_H)
            acc = q_tile(j, init, True)
            acc = lax.fori_loop(j + 1, nk, functools.partial(q_tile, masked=False), acc)
            for hp in range(_PAIRS):
                cols = slice(128 * hp, 128 * (hp + 1))
                dk_ref[kr, cols] = acc[hp].astype(BF16)
                dv_ref[kr, cols] = acc[_PAIRS + hp].astype(BF16)
            for h in range(FOX_H):
                dfk_ref[h, pl.ds(j, 1), :] = acc[2 * _PAIRS + h]
            return carry

        lax.fori_loop(0, nk, kv_tile, 0)
        dq_ref[...] = dq_sc[...].astype(BF16)

    vm = pl.BlockSpec(memory_space=pltpu.VMEM)
    big = jax.ShapeDtypeStruct((S, FOX_W), BF16)
    return pl.pallas_call(
        body,
        name=name,
        in_specs=[vm] * 6,
        out_specs=[vm] * 5,
        out_shape=[big, big, big, jax.ShapeDtypeStruct((S, 128), F32), jax.ShapeDtypeStruct((FOX_H, nk, t), F32)],
        scratch_shapes=[pltpu.VMEM((S, FOX_W), BF16), pltpu.VMEM((S, FOX_W), BF16), pltpu.VMEM((S, 128), F32),
                        pltpu.VMEM((S, FOX_W), F32)],
        compiler_params=pltpu.CompilerParams(vmem_limit_bytes=VMEM_LIMIT),
    )(qkv, cum, fk3, o, do, lse)


def _group_mask(lane, gi):
    return (lane >= 64 * gi) & (lane < 64 * (gi + 1))


_U_COL = OFF_C // SGU_W


def _sgu_fwd(rest, gn, wm, bias, *, name):
    S = rest.shape[0]
    ts = _tile(S, 512)
    nc = ts // SGU_CHUNK

    def body(u_ref, v_ref, g_ref, w_ref, b_ref, o_ref):
        zv = _gelu(v_ref[...])
        vn = zv * lax.rsqrt(jnp.mean(zv * zv, axis=-1, keepdims=True) + EPS) * g_ref[...]
        lane = _lanes((SGU_CHUNK, SGU_W))
        for c in range(nc):
            rows = slice(c * SGU_CHUNK, (c + 1) * SGU_CHUNK)
            vcb = vn[rows].astype(BF16)
            mixed = b_ref[...]
            for gi in range(4):
                mixed = mixed + jnp.where(_group_mask(lane, gi), _dot(w_ref[gi], vcb, 1, 0), 0.0)
            o_ref[rows, :] = (_gelu(u_ref[rows, :]) * mixed).astype(BF16)

    return pl.pallas_call(
        body,
        name=name,
        grid=(S // ts,),
        in_specs=[
            pl.BlockSpec((ts, SGU_W), lambda i: (i, _U_COL)),
            pl.BlockSpec((ts, SGU_W), lambda i: (i, _U_COL + 1)),
            pl.BlockSpec((1, SGU_W), lambda i: (0, 0)),
            pl.BlockSpec((4, SGU_CHUNK, SGU_CHUNK), lambda i: (0, 0, 0)),
            pl.BlockSpec((SGU_CHUNK, SGU_W), lambda i: (0, 0)),
        ],
        out_specs=pl.BlockSpec((ts, SGU_W), lambda i: (i, 0)),
        out_shape=jax.ShapeDtypeStruct((S, SGU_W), BF16),
        compiler_params=_params(("parallel",)),
    )(rest, rest, gn.reshape(1, SGU_W), wm, bias)


def _sgu_bwd(rest, gn, wm, wm_t, bias, dsg, *, name):
    S = rest.shape[0]
    ts = _tile(S, 512)
    nc = ts // SGU_CHUNK

    def body(u_ref, v_ref, g_ref, w_ref, wt_ref, b_ref, dsg_ref, dc_ref, dw_ref, db_ref, dg_ref):
        first = pl.program_id(0) == 0

        @pl.when(first)
        def _():
            dw_ref[...] = jnp.zeros_like(dw_ref)
            db_ref[...] = jnp.zeros_like(db_ref)
            dg_ref[...] = jnp.zeros_like(dg_ref)

        gv = g_ref[...]
        lane = _lanes((SGU_CHUNK, SGU_W))
        for c in range(nc):
            rows = slice(c * SGU_CHUNK, (c + 1) * SGU_CHUNK)
            vpre = v_ref[rows, :]
            upre = u_ref[rows, :]
            zv = _gelu(vpre)
            r = lax.rsqrt(jnp.mean(zv * zv, axis=-1, keepdims=True) + EPS)
            zn = zv * r
            vcb = (zn * gv).astype(BF16)
            mixed = b_ref[...]
            for gi in range(4):
                mixed = mixed + jnp.where(_group_mask(lane, gi), _dot(w_ref[gi], vcb, 1, 0), 0.0)
            zu = _gelu(upre)
            dsg_v = dsg_ref[rows, :]
            dc_ref[rows, :SGU_W] = (dsg_v * mixed * _gelu_grad(upre)).astype(BF16)
            dmixed = dsg_v * zu
            db_ref[...] += dmixed
            dvn = jnp.zeros((SGU_CHUNK, SGU_W), F32)
            for gi in range(4):
                dmg = jnp.where(_group_mask(lane, gi), dmixed, 0.0).astype(BF16)
                dw_ref[gi] += _dot(dmg, vcb, 1, 1)
                dvn = dvn + _dot(wt_ref[gi], dmg, 1, 0)
            dg_ref[...] += jnp.sum(dvn * zn, axis=0, keepdims=True)
            dzn = dvn * gv
            dzv = r * (dzn - zn * jnp.mean(dzn * zn, axis=-1, keepdims=True))
            dc_ref[rows, SGU_W:] = (dzv * _gelu_grad(vpre)).astype(BF16)

    blk = pl.BlockSpec((ts, SGU_W), lambda i: (i, 0))
    vec = pl.BlockSpec((1, SGU_W), lambda i: (0, 0))
    w3 = pl.BlockSpec((4, SGU_CHUNK, SGU_CHUNK), lambda i: (0, 0, 0))
    bsp = pl.BlockSpec((SGU_CHUNK, SGU_W), lambda i: (0, 0))
    return pl.pallas_call(
        body,
        name=name,
        grid=(S // ts,),
        in_specs=[
            pl.BlockSpec((ts, SGU_W), lambda i: (i, _U_COL)),
            pl.BlockSpec((ts, SGU_W), lambda i: (i, _U_COL + 1)),
            vec, w3, w3, bsp, blk,
        ],
        out_specs=[pl.BlockSpec((ts, 2 * SGU_W), lambda i: (i, 0)), w3, bsp, vec],
        out_shape=[
            jax.ShapeDtypeStruct((S, 2 * SGU_W), BF16),
            jax.ShapeDtypeStruct((4, SGU_CHUNK, SGU_CHUNK), F32),
            jax.ShapeDtypeStruct((SGU_CHUNK, SGU_W), F32),
            jax.ShapeDtypeStruct((1, SGU_W), F32),
        ],
        compiler_params=_params(("arbitrary",)),
    )(rest, rest, gn.reshape(1, SGU_W), wm, wm_t, bias, dsg)


_GT = 512
_G0 = OFF_G // _GT


def _gate_specs(tm, col_of):
    specs = [pl.BlockSpec((tm, _GT), functools.partial(lambda k, *ids: (col_of(*ids)[0], _G0 + 2 * k + col_of(*ids)[1]), k)) for k in range(3)]
    specs += [pl.BlockSpec((1, _GT), functools.partial(lambda k, *ids: (0, 2 * k + col_of(*ids)[1]), k)) for k in range(3)]
    return specs


def _merge_fwd(rest, bg, ya, yb, yc, *, name):
    S = rest.shape[0]
    tm = _tile(S, 512)

    def body(g1, g2, g3, b1, b2, b3, ya_ref, yb_ref, yc_ref, o_ref):
        acc = _sigmoid(g1[...] + b1[...]) * ya_ref[...]
        acc = acc + _sigmoid(g2[...] + b2[...]) * yb_ref[...]
        acc = acc + _sigmoid(g3[...] + b3[...]) * yc_ref[...]
        o_ref[...] = acc.astype(BF16)

    blk = pl.BlockSpec((tm, _GT), lambda i, j: (i, j))
    return pl.pallas_call(
        body,
        name=name,
        grid=(S // tm, D // _GT),
        in_specs=_gate_specs(tm, lambda i, j: (i, j)) + [blk, blk, blk],
        out_specs=blk,
        out_shape=jax.ShapeDtypeStruct((S, D), BF16),
        compiler_params=_params(("parallel", "parallel")),
    )(rest, rest, rest, bg, bg, bg, ya, yb, yc)


def _merge_bwd(rest, bg, ya, yb, yc, dm, *, name):
    S = rest.shape[0]
    tm = _tile(S, 512)

    def body(g1, g2, g3, b1, b2, b3, ya_ref, yb_ref, yc_ref, dm_ref, dya, dyb, dyc, dg1, dg2, dg3, db1, db2, db3):
        first = pl.program_id(1) == 0
        dmv = dm_ref[...]
        for g_ref, b_ref, y_ref, dy_ref, dg_ref, db_ref in (
            (g1, b1, ya_ref, dya, dg1, db1), (g2, b2, yb_ref, dyb, dg2, db2), (g3, b3, yc_ref, dyc, dg3, db3)):
            gate = _sigmoid(g_ref[...] + b_ref[...])
            dy_ref[...] = (dmv * gate).astype(BF16)
            dpre = dmv * y_ref[...] * gate * (1.0 - gate)
            dg_ref[...] = dpre.astype(BF16)
            part = jnp.sum(dpre, axis=0, keepdims=True)

            @pl.when(first)
            def _():
                db_ref[...] = part

            @pl.when(jnp.logical_not(first))
            def _():
                db_ref[...] += part

    blk = pl.BlockSpec((tm, _GT), lambda j, i: (i, j))
    vec = pl.BlockSpec((1, _GT), lambda j, i: (0, j))
    big = jax.ShapeDtypeStruct((S, D), BF16)
    small = jax.ShapeDtypeStruct((1, D), F32)
    return pl.pallas_call(
        body,
        name=name,
        grid=(D // _GT, S // tm),
        in_specs=_gate_specs(tm, lambda j, i: (i, j)) + [blk, blk, blk, blk],
        out_specs=[blk] * 6 + [vec] * 3,
        out_shape=[big] * 6 + [small] * 3,
        compiler_params=_params(("parallel", "arbitrary")),
    )(rest, rest, rest, bg, bg, bg, ya, yb, yc, dm)


_X_SCALE = XDH ** -0.5


def _xattn_fwd(xq, kv, *, name):
    S = xq.shape[0]
    M = kv.shape[0]
    tq = _tile(S, 512)

    def body(q_ref, k_ref, v_ref, o_ref):
        s = _dot(q_ref[...], k_ref[...], 1, 1) * _X_SCALE
        e = jnp.exp(s - jnp.max(s, axis=-1, keepdims=True))
        p = e / jnp.sum(e, axis=-1, keepdims=True)
        o_ref[...] = _dot(p.astype(BF16), v_ref[...], 1, 0).astype(BF16)

    return pl.pallas_call(
        body,
        name=name,
        grid=(S // tq, XH),
        in_specs=[
            pl.BlockSpec((tq, XDH), lambda i, h: (i, h)),
            pl.BlockSpec((M, XDH), lambda i, h: (0, h)),
            pl.BlockSpec((M, XDH), lambda i, h: (0, XH + h)),
        ],
        out_specs=pl.BlockSpec((tq, XDH), lambda i, h: (i, h)),
        out_shape=jax.ShapeDtypeStruct((S, D), BF16),
        compiler_params=_params(("parallel", "parallel")),
    )(xq, kv, kv)


def _xattn_bwd(xq, kv, do, *, name):
    S = xq.shape[0]
    M = kv.shape[0]
    tq = _tile(S, 512)

    def body(q_ref, k_ref, v_ref, do_ref, dq_ref, dk_ref, dv_ref):
        qb = q_ref[...]
        kb = k_ref[...]
        dob = do_ref[...]
        s = _dot(qb, kb, 1, 1) * _X_SCALE
        e = jnp.exp(s - jnp.max(s, axis=-1, keepdims=True))
        p = e / jnp.sum(e, axis=-1, keepdims=True)
        dp = _dot(dob, v_ref[...], 1, 1)
        ds = (p * (dp - jnp.sum(p * dp, axis=-1, keepdims=True)) * _X_SCALE).astype(BF16)
        dq_ref[...] = _dot(ds, kb, 1, 0).astype(BF16)
        dk_part = _dot(ds, qb, 0, 0)
        dv_part = _dot(p.astype(BF16), dob, 0, 0)

        @pl.when(pl.program_id(1) == 0)
        def _():
            dk_ref[...] = dk_part
            dv_ref[...] = dv_part

        @pl.when(pl.program_id(1) > 0)
        def _():
            dk_ref[...] += dk_part
            dv_ref[...] += dv_part

    qspec = pl.BlockSpec((tq, XDH), lambda h, i: (i, h))
    kspec = pl.BlockSpec((M, XDH), lambda h, i: (0, h))
    dxq, dxk, dxv = pl.pallas_call(
        body,
        name=name,
        grid=(XH, S // tq),
        in_specs=[qspec, kspec, pl.BlockSpec((M, XDH), lambda h, i: (0, XH + h)), qspec],
        out_specs=[qspec, kspec, kspec],
        out_shape=[jax.ShapeDtypeStruct((S, D), BF16), jax.ShapeDtypeStruct((M, D), F32), jax.ShapeDtypeStruct((M, D), F32)],
        compiler_params=_params(("parallel", "arbitrary")),
    )(xq, kv, kv, do)
    return dxq, jnp.concatenate([dxk, dxv], axis=1)


def _adam_math(w, g, m, v):
    m = ADAM_B1 * m + (1.0 - ADAM_B1) * g
    v = ADAM_B2 * v + (1.0 - ADAM_B2) * (g * g)
    m_hat = m / (1.0 - ADAM_B1 ** ADAM_STEP)
    v_hat = v / (1.0 - ADAM_B2 ** ADAM_STEP)
    delta = -ADAM_LR * (m_hat / (jnp.sqrt(v_hat) + ADAM_EPS) + ADAM_WD * w)
    return delta, m, v


def _adamw_sharded(parts, w, m, v, *, name):
    _, R, C = w.shape
    Cp = parts[0].shape[2]
    tm = _tile(R, 256)
    nr = R // tm

    def body(p0_ref, p1_ref, w_ref, m_ref, v_ref, g_ref, d_ref, mo_ref, vo_ref):
        def update(p_ref):
            g = p_ref[0][:, :C].astype(F32)
            for dev in range(1, N_DEV):
                g = g + p_ref[dev][:, :C].astype(F32)
            delta, mn, vn = _adam_math(w_ref[...], g, m_ref[...], v_ref[...])
            g_ref[...] = g
            d_ref[...] = delta
            mo_ref[...] = mn
            vo_ref[...] = vn

        @pl.when(pl.program_id(0) == 0)
        def _():
            update(p0_ref)

        @pl.when(pl.program_id(0) == 1)
        def _():
            update(p1_ref)

    p0 = pl.BlockSpec((N_DEV, tm, Cp), lambda l, i: (0, i * (1 - l) + (nr - 1) * l, 0))
    p1 = pl.BlockSpec((N_DEV, tm, Cp), lambda l, i: (0, i * l, 0))
    blk = pl.BlockSpec((None, tm, C), lambda l, i: (l, i, 0))
    sds = jax.ShapeDtypeStruct(w.shape, F32)
    return pl.pallas_call(
        body,
        name=name,
        grid=(DEPTH, nr),
        in_specs=[p0, p1, blk, blk, blk],
        out_specs=[blk] * 4,
        out_shape=[sds] * 4,
        compiler_params=_params(("arbitrary", "arbitrary")),
    )(parts[0], parts[1], w, m, v)


def _adamw_small(g, w, m, v, *, name):
    n = len(g)

    def body(*refs):
        g_refs, w_refs, m_refs, v_refs = (refs[k * n:(k + 1) * n] for k in range(4))
        d_out, m_out, v_out = (refs[(4 + k) * n:(5 + k) * n] for k in range(3))
        for t in range(n):
            delta, mn, vn = _adam_math(w_refs[t][...], g_refs[t][...], m_refs[t][...], v_refs[t][...])
            d_out[t][...] = delta
            m_out[t][...] = mn
            v_out[t][...] = vn

    vm = pl.BlockSpec(memory_space=pltpu.VMEM)
    shapes = [jax.ShapeDtypeStruct(a.shape, F32) for a in w]
    outs = pl.pallas_call(
        body,
        name=name,
        in_specs=[vm] * (4 * n),
        out_specs=[vm] * (3 * n),
        out_shape=shapes * 3,
        compiler_params=pltpu.CompilerParams(vmem_limit_bytes=VMEM_LIMIT),
    )(*g, *w, *m, *v)
    return outs[:n], outs[n:2 * n], outs[2 * n:]


def _position():
    return lax.axis_index("x"), lax.axis_index("y"), lax.axis_index("c")


def _dev_index(px, py, pc):
    return 4 * px + 2 * py + pc


_ANY = pl.BlockSpec(memory_space=pl.ANY)


def _all_gather(shards, *, name):
    n = len(shards)
    out_shape = [jax.ShapeDtypeStruct((N_DEV, *s.shape), s.dtype) for s in shards]
    n_pieces = len(_pieces(out_shape))

    def body(*refs):
        ins, outs = refs[:n], refs[n:2 * n]
        send_sems, recv_sems, local_sems = refs[2 * n:]
        x, y, c = _position()
        me, sibling = (x, y, c), (x, y, 1 - c)
        chips = [(1 - x, y), (x, 1 - y), (1 - x, 1 - y)]
        pieces = _pieces(outs)

        def copy(i, k, block, to, from_input=False):
            t, rows = pieces[i]
            dst = _cut(outs[t].at[_dev_index(*block)], rows)
            return pltpu.make_async_remote_copy(
                src_ref=_cut(ins[t], rows) if from_input else dst, dst_ref=dst, send_sem=send_sems.at[i, k],
                recv_sem=recv_sems.at[i, k], device_id=to, device_id_type=MESH)

        mine = [pltpu.make_async_copy(_cut(ins[t], rows), _cut(outs[t].at[_dev_index(*me)], rows), local_sems.at[i])
                for i, (t, rows) in enumerate(pieces)]
        for cp in mine:
            cp.start()
        started = []
        for j, chip in enumerate(chips):
            for i in range(n_pieces):
                started.append(copy(i, 1 + j, me, (*chip, c), from_input=True))
                started[-1].start()
        for i in range(n_pieces):
            started.append(copy(i, 0, me, sibling, from_input=True))
            started[-1].start()
        for j, chip in enumerate(chips):
            for i in range(n_pieces):
                copy(i, 1 + j, (*chip, c), me).wait_recv()
                started.append(copy(i, 4 + j, (*chip, c), sibling))
                started[-1].start()
        for i in range(n_pieces):
            copy(i, 0, sibling, me).wait_recv()
        for j, chip in enumerate(chips):
            for i in range(n_pieces):
                copy(i, 4 + j, (*chip, 1 - c), me).wait_recv()
        for cp in started:
            cp.wait_send()
        for cp in mine:
            cp.wait()

    return pl.pallas_call(
        body,
        name=name,
        in_specs=[_ANY] * n,
        out_specs=[_ANY] * n,
        out_shape=out_shape,
        scratch_shapes=[pltpu.SemaphoreType.DMA((n_pieces, 7)), pltpu.SemaphoreType.DMA((n_pieces, 7)),
                        pltpu.SemaphoreType.DMA((n_pieces,))],
        compiler_params=pltpu.CompilerParams(has_side_effects=True),
    )(*shards)


def _peers(x, y, c):
    out = []
    for mask in range(1, N_DEV):
        fx, fy, fc = (mask >> 2) & 1, (mask >> 1) & 1, mask & 1
        out.append((1 - x if fx else x, 1 - y if fy else y, 1 - c if fc else c))
    return out


_HBM = pl.BlockSpec(memory_space=pltpu.HBM)
_SEM = pl.BlockSpec(memory_space=pltpu.SEMAPHORE)


def _own_block_placed(block, like):
    x, y, c = _position()
    return lax.dynamic_update_index_in_dim(lax.empty(like.shape, like.dtype), block, _dev_index(x, y, c), 0)


_COPY_BYTES = 256 << 10
_MAX_PIECES = 8


def _pieces(blocks):
    out = []
    for t, b in enumerate(blocks):
        R, C = b.shape[-2:]
        n = max(1, min(_MAX_PIECES, R * C * jnp.dtype(b.dtype).itemsize // _COPY_BYTES))
        while n > 1 and R % (16 * n):
            n -= 1
        out += [(t, pl.ds(j * (R // n), R // n) if n > 1 else None) for j in range(n)]
    return out


def _cut(block, rows):
    return block if rows is None else block.at[rows]


def _copies(per_piece):
    def mark(fn):
        fn.per_piece = per_piece
        return fn
    return mark


@_copies(N_DEV - 1)
def _plan_exchange(srcs, lands, send_sems, recv_sems, arrivals):
    x, y, c = _position()
    me = _dev_index(x, y, c)
    out = []
    for k, peer in enumerate(_peers(x, y, c)):
        p = _dev_index(*peer)
        for i, (t, rows) in enumerate(_pieces(lands)):
            sems = dict(send_sem=send_sems.at[7 * i + k], recv_sem=recv_sems.at[7 * i + k], device_id=peer, device_id_type=MESH)
            src, dst = (lands[t].at[p], lands[t].at[p]) if arrivals else (srcs[t].at[p], lands[t].at[me])
            out.append(pltpu.make_async_remote_copy(src_ref=_cut(src, rows), dst_ref=_cut(dst, rows), **sems))
    return out


@_copies(N_DEV - 1)
def _plan_broadcast(srcs, lands, send_sems, recv_sems, arrivals):
    x, y, c = _position()
    me = _dev_index(x, y, c)
    out = []
    for k, peer in enumerate(_peers(x, y, c)):
        p = _dev_index(*peer)
        for i, (t, rows) in enumerate(_pieces(lands)):
            sems = dict(send_sem=send_sems.at[7 * i + k], recv_sem=recv_sems.at[7 * i + k], device_id=peer, device_id_type=MESH)
            src, dst = (lands[t].at[p], lands[t].at[p]) if arrivals else (srcs[t], lands[t].at[me])
            out.append(pltpu.make_async_remote_copy(src_ref=_cut(src, rows), dst_ref=_cut(dst, rows), **sems))
    return out


@_copies(4)
def _plan_gather_out(srcs, lands, send_sems, recv_sems, arrivals):
    x, y, c = _position()
    me = _dev_index(x, y, c)
    out = []
    for k, peer in enumerate([(x, y, 1 - c), (1 - x, y, c), (x, 1 - y, c), (1 - x, 1 - y, c)]):
        p = _dev_index(*peer)
        for i, (t, rows) in enumerate(_pieces(lands)):
            sems = dict(send_sem=send_sems.at[4 * i + k], recv_sem=recv_sems.at[4 * i + k], device_id=peer, device_id_type=MESH)
            src, dst = (lands[t].at[p], lands[t].at[p]) if arrivals else (srcs[t], lands[t].at[me])
            out.append(pltpu.make_async_remote_copy(src_ref=_cut(src, rows), dst_ref=_cut(dst, rows), **sems))
    return out


@_copies(3)
def _plan_gather_pass(srcs, lands, send_sems, recv_sems, arrivals):
    x, y, c = _position()
    sibling = (x, y, 1 - c)
    out = []
    for k, chip in enumerate([(1 - x, y), (x, 1 - y), (1 - x, 1 - y)]):
        p = _dev_index(*chip, 1 - c) if arrivals else _dev_index(*chip, c)
        for i, (t, rows) in enumerate(_pieces(lands)):
            sems = dict(send_sem=send_sems.at[3 * i + k], recv_sem=recv_sems.at[3 * i + k], device_id=sibling, device_id_type=MESH)
            block = _cut(lands[t].at[p], rows)
            out.append(pltpu.make_async_remote_copy(src_ref=block, dst_ref=block, **sems))
    return out


def _split_start(plan, srcs, lands, *, after=None, name):
    n_src, n = len(srcs), len(srcs) + len(lands)
    n_sem = plan.per_piece * len(_pieces(lands))
    order = [] if after is None else [after]

    def body(*refs):
        send_sems, recv_sems = refs[n + len(order):n + len(order) + 2]
        token = refs[-1]
        for cp in plan(refs[:n_src], refs[n_src:n], send_sems, recv_sems, arrivals=False):
            cp.start()
        token[...] = jnp.zeros_like(token)

    hbm = lambda a: pltpu.HBM(a.shape, a.dtype)
    outs = pl.pallas_call(
        body,
        name=name,
        in_specs=[_HBM] * n + [_ANY] * len(order),
        out_specs=[_SEM, _SEM] + [_HBM] * n + [pl.BlockSpec(memory_space=pltpu.VMEM)],
        out_shape=[pltpu.SemaphoreType.DMA((n_sem,)), pltpu.SemaphoreType.DMA((n_sem,))] + [hbm(a) for a in (*srcs, *lands)]
        + [jax.ShapeDtypeStruct(_TOKEN, F32)],
        input_output_aliases={i: 2 + i for i in range(n)},
        compiler_params=pltpu.CompilerParams(has_side_effects=pltpu.SideEffectType.DATAFLOW_SIDE_EFFECTING),
    )(*[pltpu.with_memory_space_constraint(a, pltpu.HBM) for a in (*srcs, *lands)], *order)
    return (outs[0], outs[1], outs[2:2 + n_src], outs[2 + n_src:2 + n]), outs[-1]


def _split_wait(plan, state, after, *, name):
    send_sems, recv_sems, srcs, lands = state
    n_src, n = len(srcs), len(srcs) + len(lands)

    def body(*refs):
        send_refs, recv_refs = refs[n:n + 2]
        for cp in plan(refs[:n_src], refs[n_src:n], send_refs, recv_refs, arrivals=False):
            cp.wait_send()
        for cp in plan(refs[:n_src], refs[n_src:n], send_refs, recv_refs, arrivals=True):
            cp.wait_recv()

    hbm = lambda a: pltpu.HBM(a.shape, a.dtype)
    outs = pl.pallas_call(
        body,
        name=name,
        in_specs=[_HBM] * n + [_SEM, _SEM, _ANY],
        out_specs=[_HBM] * n,
        out_shape=[hbm(a) for a in (*srcs, *lands)],
        input_output_aliases={i: i for i in range(n)},
        compiler_params=pltpu.CompilerParams(has_side_effects=pltpu.SideEffectType.DATAFLOW_SIDE_EFFECTING),
    )(*srcs, *lands, send_sems, recv_sems, after)
    return outs[n_src:]


def _sum_blocks(blocks, *, name):
    _, R, C = blocks.shape
    tm = next(R // n for n in (4, 3, 2, 1) if R % (8 * n) == 0)

    def body(b_ref, o_ref):
        g = b_ref[0]
        for dev in range(1, N_DEV):
            g = g + b_ref[dev]
        o_ref[...] = g

    return pl.pallas_call(
        body,
        name=name,
        grid=(R // tm,),
        in_specs=[pl.BlockSpec((N_DEV, tm, C), lambda i: (0, i, 0))],
        out_specs=pl.BlockSpec((tm, C), lambda i: (i, 0)),
        out_shape=jax.ShapeDtypeStruct((R, C), F32),
        compiler_params=_params(("parallel",)),
    )(blocks)


def _block_diag(w):
    out = jnp.zeros((POOL_W, POOL_W), w.dtype)
    for gi in range(4):
        out = out.at[64 * gi:64 * (gi + 1), 64 * gi:64 * (gi + 1)].set(w[gi])
    return out


def _layer_consts(sp, l):
    causal = jnp.tril(jnp.ones((SGU_CHUNK, SGU_CHUNK), F32))
    wm = (sp["sgu_w"][l] * causal[None]).astype(BF16)
    wbd = _block_diag(sp["pool_w"][l]).astype(BF16)
    return dict(
        wbd=wbd, wbd_t=wbd.T, wm=wm, wm_t=wm.transpose(0, 2, 1),
        sgu_bias=jnp.repeat(sp["sgu_b"][l].T, 64, axis=1),
        bpad=jnp.pad(sp["b_forget"][l], (0, F_LANES - FOX_H)).reshape(1, F_LANES),
        bg=sp["b_gate"][l].reshape(1, 3 * D),
    )


def _relu2(acc):
    return acc, jnp.square(jnp.maximum(acc, 0.0))


def _relu2_grad(acc, z):
    return (acc * 2.0 * jnp.maximum(z, 0.0),)


def _layer_fwd(l, x, mem, source, sp):
    S = x.shape[0]
    t = _tile(S, 256)
    c = _layer_consts(sp, l)
    n = f"l{l}_"
    W, after = source(l, "begin", x)
    h = _rms_fwd(x, sp["norm_mix_g"][l], after=after, name=n + "norm_mix")
    qkv = _mm(h, W["qkv"], out_dtypes=(BF16,), name=n + "qkv")
    rest = _mm(h, W["rest"], name=n + "rest")
    pa = _pool_fwd(rest, c["wbd"], sp["pool_scale"][l], name=n + "pool")
    cum, cum_t = _fox_prep(rest, c["bpad"], name=n + "fox_prep")
    fk3 = cum_t[:FOX_H].reshape(FOX_H, S // t, t)
    o, lse = _fox_fwd(qkv, cum, fk3, name=n + "fox")
    more, _ = source(l, "attended", o)
    W.update(more)
    sg = _sgu_fwd(rest, sp["sgu_norm_g"][l], c["wm"], c["sgu_bias"], name=n + "sgu")
    more, after = source(l, "mixed", sg)
    W.update(more)
    ya = _mm(pa, W["ba"], after=after, name=n + "branch_a")
    yb = _mm(o, W["bb"], name=n + "branch_b")
    yc = _mm(sg, W["bc"], name=n + "branch_c")
    merged = _merge_fwd(rest, c["bg"], ya, yb, yc, name=n + "merge")
    x1 = _mm(merged, W["out"], extras=(x,), epilogue=_add, name=n + "out")
    hx = _rms_fwd(x1, sp["norm_xattn_g"][l], name=n + "norm_xattn")
    hm = _rms_fwd(mem, sp["norm_mem_g"][l], name=n + "norm_mem")
    xq = _mm(hx, W["xq"], out_dtypes=(BF16,), name=n + "xq")
    kv = _mm(hm, W["xkv"], out_dtypes=(BF16,), name=n + "xkv")
    o2 = _xattn_fwd(xq, kv, name=n + "xattn")
    x2 = _mm(o2, W["xo"], extras=(x1,), epilogue=_add, name=n + "xo")
    hf = _rms_fwd(x2, sp["norm_ffn_g"][l], name=n + "norm_ffn")
    z, act = _mm(hf, W["ff1"], epilogue=_relu2, out_dtypes=(F32, BF16), name=n + "ff1")
    _, after = source(l, "expanded", act)
    x3 = _mm(act, W["ff2"], extras=(x2,), epilogue=_add, after=after, name=n + "ff2")
    saved = dict(x=x, h=h, qkv=qkv, rest=rest, pa=pa, cum=cum, fk3=fk3, o=o, lse=lse, sg=sg, ya=ya, yb=yb, yc=yc,
                 merged=merged, x1=x1, hx=hx, hm=hm, xq=xq, kv=kv, o2=o2, x2=x2, hf=hf, z=z, act=act, c=c)
    return x3, saved, W


def _layer_bwd(l, dx3, sv, mem, W, sp, grads_done):
    S = dx3.shape[0]
    c = sv["c"]
    n = f"l{l}b_"
    bf = dict(out_dtypes=(BF16,))
    gw, gs = {}, {}
    gw["ff2"] = _mm(sv["act"], dx3, ta=True, name=n + "dw_ff2", **bf)
    dz = _mm(dx3, W["ff2"], tb=True, extras=(sv["z"],), epilogue=_relu2_grad, name=n + "dz", **bf)
    gw["ff1"] = _mm(sv["hf"], dz, ta=True, shard_out=True, name=n + "dw_ff1", **bf)
    dhf = _mm(dz, W["ff1"], tb=True, name=n + "dhf")
    dx2, gs["norm_ffn_g"] = _rms_bwd(sv["x2"], sp["norm_ffn_g"][l], dhf, dx3, name=n + "dnorm_ffn")
    gw["xo"] = _mm(sv["o2"], dx2, ta=True, name=n + "dw_xo", **bf)
    do2 = _mm(dx2, W["xo"], tb=True, name=n + "do2", **bf)
    dxq, dkv = _xattn_bwd(sv["xq"], sv["kv"], do2, name=n + "dxattn")
    gw["xq"] = _mm(sv["hx"], dxq, ta=True, name=n + "dw_xq", **bf)
    gw["xkv"] = _mm(sv["hm"], dkv, ta=True, shard_out=True, name=n + "dw_xkv", **bf)
    dhm = _mm(dkv, W["xkv"], tb=True, name=n + "dhm")
    _, gs["norm_mem_g"] = _rms_bwd(mem, sp["norm_mem_g"][l], dhm, jnp.zeros_like(mem), name=n + "dnorm_mem")
    dhx = _mm(dxq, W["xq"], tb=True, name=n + "dhx")
    dx1, gs["norm_xattn_g"] = _rms_bwd(sv["x1"], sp["norm_xattn_g"][l], dhx, dx2, name=n + "dnorm_xattn")
    after, gw = grads_done(l, gw), {}
    gw["out"] = _mm(sv["merged"], dx1, ta=True, name=n + "dw_out", **bf)
    dm = _mm(dx1, W["out"], tb=True, after=after, name=n + "dmerged")
    dya, dyb, dyc, dg1, dg2, dg3, db1, db2, db3 = _merge_bwd(sv["rest"], c["bg"], sv["ya"], sv["yb"], sv["yc"], dm, name=n + "dmerge")
    gs["b_gate"] = jnp.concatenate([db1, db2, db3], axis=1).reshape(3 * D)
    gw["ba"] = _mm(sv["pa"], dya, ta=True, shard_out=True, name=n + "dw_ba", **bf)
    gw["bb"] = _mm(sv["o"], dyb, ta=True, shard_out=True, name=n + "dw_bb", **bf)
    gw["bc"] = _mm(sv["sg"], dyc, ta=True, shard_out=True, name=n + "dw_bc", **bf)
    after, gw = grads_done(l, gw), {}
    dpa = _mm(dya, W["ba"], tb=True, name=n + "dpa")
    do = _mm(dyb, W["bb"], tb=True, after=after, name=n + "do", **bf)
    dsg = _mm(dyc, W["bc"], tb=True, name=n + "dsg")
    da, dwbd, dscale = _pool_bwd(sv["rest"], c["wbd"], c["wbd_t"], sp["pool_scale"][l], dpa, name=n + "dpool")
    gs["pool_w"] = jnp.stack([dwbd[64 * gi:64 * (gi + 1), 64 * gi:64 * (gi + 1)] for gi in range(4)])
    gs["pool_scale"] = dscale.reshape(POOL_W)
    dq, dk, dv, dfq, dfk = _fox_bwd(sv["qkv"], sv["cum"], sv["fk3"], sv["o"], do, sv["lse"], name=n + "dfox")
    dcum = dfq + jnp.pad(dfk.reshape(FOX_H, S).T, ((0, 0), (0, F_LANES - FOX_H)))
    df, dbf = _fox_post(sv["rest"], c["bpad"], dcum, name=n + "dfox_post")
    gs["b_forget"] = dbf[0, :FOX_H]
    dc, dwm, dbias, dgn = _sgu_bwd(sv["rest"], sp["sgu_norm_g"][l], c["wm"], c["wm_t"], c["sgu_bias"], dsg, name=n + "dsgu")
    gs["sgu_w"] = dwm * jnp.tril(jnp.ones((SGU_CHUNK, SGU_CHUNK), F32))[None]
    gs["sgu_b"] = dbias.reshape(SGU_CHUNK, 4, 64).sum(axis=2).T
    gs["sgu_norm_g"] = dgn.reshape(SGU_W)
    dqkv = [dq, dk, dv]
    drest = [jnp.concatenate([da, df, jnp.zeros((S, OFF_C - OFF_F - F_LANES), BF16), dc], axis=1), dg1, dg2, dg3]
    gw["qkv"] = _mm(sv["h"], dqkv, ta=True, name=n + "dw_qkv", **bf)
    gw["rest"] = _mm(sv["h"], drest, ta=True, name=n + "dw_rest", **bf)
    after = grads_done(l, gw)
    dh = _mm(dqkv, W["qkv"], tb=True, after=after, name=n + "dh_qkv")
    dh = _mm(drest, W["rest"], tb=True, extras=(dh,), epilogue=_add, tm=1024, name=n + "dh")
    dx, gs["norm_mix_g"] = _rms_bwd(sv["x"], sp["norm_mix_g"][l], dh, dx1, name=n + "dnorm_mix")
    return dx, gs


def _local_step(x, mem, target, sp, source, grads_done):
    saved, Ws = [], []
    for l in range(DEPTH):
        x, sv, W = _layer_fwd(l, x, mem, source, sp)
        saved.append(sv)
        Ws.append(W)
    loss, dx, dgf = _final_loss(x, sp["final_norm_g"], target, name="final_loss")
    gss = [None] * DEPTH
    for l in reversed(range(DEPTH)):
        dx, gss[l] = _layer_bwd(l, dx, saved[l], mem, Ws[l], sp, grads_done)
    small = {k: jnp.stack([gss[l][k] for l in range(DEPTH)]) for k in gss[0]}
    small["final_norm_g"] = dgf
    return loss, dx, small


_SMALL = ["norm_mix_g", "b_forget", "pool_w", "pool_scale", "sgu_norm_g", "sgu_w", "sgu_b", "b_gate", "norm_xattn_g",
          "norm_mem_g", "norm_ffn_g", "final_norm_g"]
_COL = {"w_branch_a": "ba", "w_branch_b": "bb", "w_branch_c": "bc", "w_xkv": "xkv", "w_ff1": "ff1"}
_ROW = {"w_out": "out", "w_xq": "xq", "w_xo": "xo", "w_ff2": "ff2"}
_BIG = ["w_in", "w_branch_a", "w_branch_b", "w_branch_c", "w_out", "w_xq", "w_xkv", "w_xo", "w_ff1", "w_ff2"]
_PACK_LANES = 128


def _as_rows(a):
    return a.reshape(-1, a.shape[-1])


def _pack(tensors):
    rows = []
    for a in tensors:
        flat = a.reshape(-1)
        flat = jnp.pad(flat, (0, (-flat.shape[0]) % (8 * _PACK_LANES)))
        rows.append(flat.reshape(-1, _PACK_LANES))
    n_rows = sum(r.shape[0] for r in rows)
    rows.append(jnp.zeros(((-n_rows) % (8 * N_DEV), _PACK_LANES), F32))
    return jnp.concatenate(rows, axis=0)


def _unpack(packed, like):
    out, r = [], 0
    for a in like:
        size = math.prod(a.shape)
        nr = 8 * (-(-size // (8 * _PACK_LANES)))
        out.append(packed[r:r + nr].reshape(-1)[:size].reshape(a.shape))
        r += nr
    return out


_SHARD_IN = N_IN // N_DEV
_SHARD_IN_PAD = -(-_SHARD_IN // 128) * 128


def _columns(pieces, start, stop):
    out, at = [], 0
    for p in pieces:
        lo, hi = max(start, at), min(stop, at + p.shape[1])
        if lo < hi:
            out.append(p[:, lo - at:hi - at])
        at += p.shape[1]
    return out


def _split_w_in(blocks):
    K = blocks[0].shape[0]
    pad = jnp.zeros((K, OFF_C - OFF_F - FOX_H), blocks[0].dtype)
    cols = functools.partial(_columns, blocks)
    rest = jnp.concatenate(cols(0, R_OFF_Q) + cols(R_OFF_F, R_OFF_C) + [pad] + cols(R_OFF_C, N_IN), axis=1)
    return jnp.concatenate(cols(R_OFF_Q, R_OFF_F), axis=1), rest


def _join_w_in(qkv, rest):
    in_order = [rest[:, :R_OFF_Q], qkv, rest[:, OFF_F:OFF_F + FOX_H], rest[:, OFF_C:]]
    pad = jnp.zeros((qkv.shape[0], _SHARD_IN_PAD - _SHARD_IN), qkv.dtype)
    return jnp.stack([jnp.concatenate(_columns(in_order, _SHARD_IN * d, _SHARD_IN * (d + 1)) + [pad], axis=1) for d in range(N_DEV)])


_FIRST = ["w_in"]
_LATER = [k for k in _BIG if k not in _FIRST]


def _layer_weights(gathered):
    W = {}
    if "w_in" in gathered:
        W.update(zip(("qkv", "rest"), _split_w_in([gathered["w_in"][d][:, :_SHARD_IN] for d in range(N_DEV)])))
    for name, key in _COL.items():
        if name in gathered:
            W[key] = _Gathered(gathered[name])
    for name, key in _ROW.items():
        if name in gathered:
            W[key] = gathered[name].reshape(-1, gathered[name].shape[-1])
    return W


def _grad_blocks(gw):
    parts = {}
    if "qkv" in gw:
        parts["w_in"] = _join_w_in(gw["qkv"], gw["rest"])
    for name, key in _COL.items():
        if key in gw:
            parts[name] = gw[key]
    for name, key in _ROW.items():
        if key in gw:
            parts[name] = gw[key].reshape(N_DEV, -1, gw[key].shape[-1])
    return parts


def kernel(x, mem, norm_mix_g, w_in, b_forget, pool_w, pool_scale, sgu_norm_g, sgu_w, sgu_b, w_branch_a, w_branch_b, w_branch_c, b_gate, w_out, norm_xattn_g, norm_mem_g, w_xq, w_xkv, w_xo, norm_ffn_g, w_ff1, w_ff2, final_norm_g, loss_target, m_norm_mix_g, m_w_in, m_b_forget, m_pool_w, m_pool_scale, m_sgu_norm_g, m_sgu_w, m_sgu_b, m_w_branch_a, m_w_branch_b, m_w_branch_c, m_b_gate, m_w_out, m_norm_xattn_g, m_norm_mem_g, m_w_xq, m_w_xkv, m_w_xo, m_norm_ffn_g, m_w_ff1, m_w_ff2, m_final_norm_g, v_norm_mix_g, v_w_in, v_b_forget, v_pool_w, v_pool_scale, v_sgu_norm_g, v_sgu_w, v_sgu_b, v_w_branch_a, v_w_branch_b, v_w_branch_c, v_b_gate, v_w_out, v_norm_xattn_g, v_norm_mem_g, v_w_xq, v_w_xkv, v_w_xo, v_norm_ffn_g, v_w_ff1, v_w_ff2, v_final_norm_g):
    names = ["norm_mix_g", "w_in", "b_forget", "pool_w", "pool_scale", "sgu_norm_g", "sgu_w", "sgu_b", "w_branch_a", "w_branch_b",
             "w_branch_c", "b_gate", "w_out", "norm_xattn_g", "norm_mem_g", "w_xq", "w_xkv", "w_xo", "norm_ffn_g", "w_ff1", "w_ff2",
             "final_norm_g"]
    w = dict(zip(names, [norm_mix_g, w_in, b_forget, pool_w, pool_scale, sgu_norm_g, sgu_w, sgu_b, w_branch_a, w_branch_b, w_branch_c,
                         b_gate, w_out, norm_xattn_g, norm_mem_g, w_xq, w_xkv, w_xo, norm_ffn_g, w_ff1, w_ff2, final_norm_g]))
    m = dict(zip(names, [m_norm_mix_g, m_w_in, m_b_forget, m_pool_w, m_pool_scale, m_sgu_norm_g, m_sgu_w, m_sgu_b, m_w_branch_a,
                         m_w_branch_b, m_w_branch_c, m_b_gate, m_w_out, m_norm_xattn_g, m_norm_mem_g, m_w_xq, m_w_xkv, m_w_xo,
                         m_norm_ffn_g, m_w_ff1, m_w_ff2, m_final_norm_g]))
    v = dict(zip(names, [v_norm_mix_g, v_w_in, v_b_forget, v_pool_w, v_pool_scale, v_sgu_norm_g, v_sgu_w, v_sgu_b, v_w_branch_a,
                         v_w_branch_b, v_w_branch_c, v_b_gate, v_w_out, v_norm_xattn_g, v_norm_mem_g, v_w_xq, v_w_xkv, v_w_xo,
                         v_norm_ffn_g, v_w_ff1, v_w_ff2, v_final_norm_g]))

    sp = {k: w[k] for k in _SMALL}
    shards = [{k: w[k][l].astype(BF16) for k in _BIG} for l in range(DEPTH)]
    for sh in shards:
        sh["w_in"] = jnp.pad(sh["w_in"], ((0, 0), (0, _SHARD_IN_PAD - _SHARD_IN)))
    me = _dev_index(*_position())

    def gather_out(l, keys, name, after=None):
        srcs = [shards[l][k] for k in keys]
        lands = [_own_block_placed(a, jax.ShapeDtypeStruct((N_DEV, *a.shape), a.dtype)) for a in srcs]
        state, token = _split_start(_plan_gather_out, srcs, lands, after=after, name=name + "_out_start")
        return (keys, name, state), token

    def gather_pass(job, value):
        keys, name, state = job
        lands = _split_wait(_plan_gather_out, state, value, name=name + "_out_wait")
        state, token = _split_start(_plan_gather_pass, [], lands, name=name + "_pass_start")
        return (keys, name, state), token, lands[0]

    def gather_end(job, value):
        keys, name, state = job
        return _layer_weights(dict(zip(keys, _split_wait(_plan_gather_pass, state, value, name=name + "_pass_wait"))))

    jobs = {}

    def source(l, point, value):
        if (l, point) == (0, "begin"):
            first = _all_gather([shards[0][k] for k in _FIRST], name="gather_l0_first")
            jobs["l0"], token = gather_out(0, _LATER, "gather_l0", after=first[0])
            return _layer_weights(dict(zip(_FIRST, first))), token
        if (l, point) == (0, "attended"):
            jobs["l0"], _, arrived = gather_pass(jobs["l0"], value)
            jobs["l1"], jobs["token"] = gather_out(1, _BIG, "gather_l1", after=arrived)
            return {}, None
        if (l, point) == (0, "mixed"):
            return gather_end(jobs.pop("l0"), value), jobs.pop("token")
        if (l, point) == (0, "expanded"):
            jobs["l1"], token, _ = gather_pass(jobs["l1"], value)
            return {}, token
        if (l, point) == (1, "begin"):
            return gather_end(jobs.pop("l1"), value), None
        return {}, None

    received = [{} for _ in range(DEPTH)]
    travelling = []

    def grads_done(l, gw):
        blocks = _grad_blocks(gw)
        keys = [k for k in _BIG if k in blocks]
        parts = [blocks[k] for k in keys]
        group = f"exchange_grads_l{l}_" + ("in" if "w_in" in blocks else "merge" if "w_out" in blocks else "mlp")
        lands = [_own_block_placed(lax.dynamic_index_in_dim(p, me, 0, keepdims=False), p) for p in parts]
        state, token = _split_start(_plan_exchange, parts, lands, name=group + "_start")
        travelling.append((l, keys, state, group + "_wait"))
        return token

    loss, dx, small = _local_step(x[0], mem[0], loss_target[0], sp, source, grads_done)
    grads, deltas, new_m, new_v = {}, {}, {}, {}
    like = [loss] + [w[k] for k in _SMALL]
    packed = _pack([loss] + [small[k] for k in _SMALL])
    eighths = packed.reshape(N_DEV, -1, _PACK_LANES)
    own = lambda a: _own_block_placed(lax.dynamic_index_in_dim(a, me, 0, keepdims=False) if a.ndim == 3 else a, eighths)
    scatter, done = _split_start(_plan_exchange, [eighths], [own(eighths)], after=dx, name="small_grads_scatter_start")

    def reduce_small(after):
        mine = _sum_blocks(_split_wait(_plan_exchange, scatter, after, name="small_grads_scatter_wait")[0], name="small_grads_sum")
        return _split_start(_plan_broadcast, [mine], [own(mine)], name="small_grads_gather_start")

    def update_small(state, after):
        total = _split_wait(_plan_broadcast, state, after, name="small_grads_gather_wait")[0].reshape(packed.shape)
        loss_sum, *g_small = _unpack(total, like)
        rows = lambda d: [_as_rows(d[k]) for k in _SMALL]
        outs = _adamw_small([_as_rows(g) for g in g_small], rows(w), rows(m), rows(v), name="adamw_small")
        grads.update(zip(_SMALL, g_small))
        for dst, vals in zip((deltas, new_m, new_v), outs):
            dst.update({k: a.reshape(w[k].shape) for k, a in zip(_SMALL, vals)})
        return loss_sum[0, 0], outs[0][0]

    groups = list(dict.fromkeys(tuple(keys) for _, keys, _, _ in travelling))
    for n_done, group_keys in enumerate(groups):
        if n_done == 1:
            gather, _ = reduce_small(done)
        if n_done == len(groups) - 1:
            loss, done = update_small(gather, done)
        for l, keys, state, wait_name in travelling:
            if tuple(keys) == group_keys:
                received[l].update(zip(keys, _split_wait(_plan_exchange, state, done, name=wait_name)))
        for k in group_keys:
            outs = _adamw_sharded([received[l][k] for l in range(DEPTH)], w[k], m[k], v[k], name="adamw_" + k)
            grads[k], deltas[k], new_m[k], new_v[k] = outs
        done = grads[group_keys[-1]]

    return (loss, dx[None], *[grads[k] for k in names], *[deltas[k] for k in names], *[new_m[k] for k in names],
            *[new_v[k] for k in names])
```

```python
import functools
import math

import jax
import jax.numpy as jnp
from jax import lax
from jax.experimental import pallas as pl
from jax.experimental.pallas import tpu as pltpu

F32 = jnp.float32
BF16 = jnp.bfloat16
MESH = pl.DeviceIdType.MESH

N_DEV = 8
D = 1024
DEPTH = 2
EPS = 1e-6
NEG = -1e30
POOL_W = 256
FOX_H = 8
FOX_DH = 64
FOX_W = 512
SGU_W = 256
SGU_CHUNK = 128
XH = 4
XDH = 256
N_IN = 5384
R_OFF_Q, R_OFF_F, R_OFF_C = 256, 1792, 1800
QKV_W = 3 * FOX_W
OFF_A, OFF_F, OFF_C, OFF_G, REST_W = 0, 256, 512, 1024, 4096
F_LANES = 128

ADAM_LR = 0.001
ADAM_B1 = 0.9
ADAM_B2 = 0.999
ADAM_EPS = 1e-08
ADAM_WD = 0.01
ADAM_STEP = 10

VMEM_LIMIT = 56 * 1024 * 1024


def _tile(n, pref):
    t = min(n, pref)
    while n % t:
        t -= 128
    assert t > 0, (n, pref)
    return t


def _params(sem=None):
    return pltpu.CompilerParams(dimension_semantics=sem, vmem_limit_bytes=VMEM_LIMIT)


def _dot(a, b, ca, cb):
    return lax.dot_general(a, b, (((ca,), (cb,)), ((), ())), preferred_element_type=F32)


def _sigmoid(z):
    return 1.0 / (1.0 + jnp.exp(-z))


_GELU_K = math.sqrt(2.0 / math.pi)
_GELU_C = 0.044715


def _gelu(x):
    return 0.5 * x * (1.0 + jnp.tanh(_GELU_K * (x + _GELU_C * x * x * x)))


def _gelu_grad(x):
    t = jnp.tanh(_GELU_K * (x + _GELU_C * x * x * x))
    return 0.5 * (1.0 + t) + 0.5 * x * (1.0 - t * t) * _GELU_K * (1.0 + 3.0 * _GELU_C * x * x)


def _rows(shape):
    return lax.broadcasted_iota(jnp.int32, shape, 0)


def _lanes(shape):
    return lax.broadcasted_iota(jnp.int32, shape, 1)


class _Gathered:
    def __init__(self, arr):
        self.arr = arr
        self.shape = (arr.shape[1], N_DEV * arr.shape[2])


_TOKEN = (8, 128)


def _mm(a, b, *, ta=False, tb=False, extras=(), epilogue=None, out_dtypes=(F32,), shard_out=False, after=None, tm=None, tn=512, tk=None,
        name):
    a_parts = list(a) if isinstance(a, (list, tuple)) else [a]
    b_parts = list(b) if isinstance(b, (list, tuple)) else [b]
    gathered = isinstance(b, _Gathered)
    assert (len(a_parts) == 1 or not ta) and (len(b_parts) == 1 or not tb) and min(len(a_parts), len(b_parts)) == 1
    a0, b0 = a_parts[0], b_parts[0]
    M, K = (a0.shape[1], a0.shape[0]) if ta else (a0.shape[0], a0.shape[1] * len(a_parts))
    N, Kb = b0.shape if tb else (b0.shape[1] * len(b_parts), b0.shape[0])
    assert Kb == K, (a0.shape, b0.shape, ta, tb)
    if gathered:
        if tb:
            tk = b.arr.shape[2]
        else:
            tn = b.arr.shape[2]
    if len(a_parts) > 1:
        tk = a0.shape[1]
    if shard_out:
        tn = N // N_DEV
    tm = _tile(M, tm or (1024 if ta else 2048))
    tn = _tile(b0.shape[1] if len(b_parts) > 1 else N, tn)
    per_piece = b0.shape[1] // tn
    size = lambda dt: jnp.dtype(dt).itemsize
    row_bytes = len(a_parts) * tm * size(a0.dtype) + len(b_parts) * tn * size(b.arr.dtype if gathered else b0.dtype)
    tile_bytes = tm * tn * (sum(size(e.dtype) for e in extras) + sum(map(size, out_dtypes)))

    def vmem_bytes(k_tile):
        return 2 * (k_tile * row_bytes + tile_bytes) + tm * tn * 4 * (K > k_tile)

    if tk is None:
        tk = next(c for c in (_tile(K, 2048), _tile(K, 1024), _tile(K, 512), _tile(K, 256)) if vmem_bytes(c) <= VMEM_LIMIT - (4 << 20))
    tk = _tile(K, tk)
    nk = K // tk
    ca, cb = (0 if ta else 1), (1 if tb else 0)
    n_a, n_b, n_ex, n_out = len(a_parts), len(b_parts), len(extras), len(out_dtypes)
    tokens = [] if after is None else [after]
    n_in = n_a + n_b + n_ex + len(tokens)
    if epilogue is None:
        epilogue = lambda acc: (acc,)

    def body(*refs):
        a_refs, b_refs = refs[:n_a], refs[n_a:n_a + n_b]
        ex_refs = refs[n_a + n_b:n_a + n_b + n_ex]
        o_refs = refs[n_in:n_in + n_out]
        j, k = pl.program_id(1), pl.program_id(2)

        def finish(acc):
            for o_ref, val in zip(o_refs, epilogue(acc, *[e[...] for e in ex_refs])):
                o_ref[...] = val.astype(o_ref.dtype)

        def step(a_ref, b_ref):
            part = _dot(a_ref[...].astype(BF16), b_ref[...].astype(BF16), ca, cb)
            if nk == 1:
                finish(part)
            else:
                acc_ref = refs[-1]

                @pl.when(k == 0)
                def _():
                    acc_ref[...] = part

                @pl.when(k > 0)
                def _():
                    acc_ref[...] += part

                @pl.when(k == nk - 1)
                def _():
                    finish(acc_ref[...])

        if n_a > 1:
            for p in range(n_a):
                pl.when(k == p)(functools.partial(step, a_refs[p], b_refs[0]))
        elif n_b > 1:
            for p in range(n_b):
                pl.when(j // per_piece == p)(functools.partial(step, a_refs[0], b_refs[p]))
        else:
            step(a_refs[0], b_refs[0])

    if n_a > 1:
        a_specs = [pl.BlockSpec((tm, tk), lambda i, j, k: (i, 0))] * n_a
    else:
        a_specs = [pl.BlockSpec((tk, tm), lambda i, j, k: (k, i)) if ta else pl.BlockSpec((tm, tk), lambda i, j, k: (i, k))]
    if gathered:
        b_arrs = [b.arr]
        b_specs = [pl.BlockSpec((None, tn, tk), lambda i, j, k: (k, j, 0)) if tb else pl.BlockSpec((None, tk, tn), lambda i, j, k: (j, k, 0))]
    elif n_b > 1:
        b_arrs = b_parts
        b_specs = [pl.BlockSpec((tk, tn), functools.partial(lambda p, i, j, k: (k, jnp.clip(j - p * per_piece, 0, per_piece - 1)), p))
                   for p in range(n_b)]
    else:
        b_arrs = b_parts
        b_specs = [pl.BlockSpec((tn, tk), lambda i, j, k: (j, k)) if tb else pl.BlockSpec((tk, tn), lambda i, j, k: (k, j))]
    tile = pl.BlockSpec((tm, tn), lambda i, j, k: (i, j))
    if shard_out:
        out_specs = [pl.BlockSpec((None, tm, tn), lambda i, j, k: (j, i, 0))] * n_out
        out_shape = [jax.ShapeDtypeStruct((N_DEV, M, tn), dt) for dt in out_dtypes]
    else:
        out_specs = [tile] * n_out
        out_shape = [jax.ShapeDtypeStruct((M, N), dt) for dt in out_dtypes]
    assert vmem_bytes(tk) <= VMEM_LIMIT - (4 << 20), (name, vmem_bytes(tk))
    outs = pl.pallas_call(
        body,
        name=name,
        grid=(M // tm, N // tn, nk),
        in_specs=a_specs + b_specs + [tile] * n_ex + [pl.BlockSpec(_TOKEN, lambda i, j, k: (0, 0))] * len(tokens),
        out_specs=out_specs,
        out_shape=out_shape,
        scratch_shapes=[pltpu.VMEM((tm, tn), F32)] if nk > 1 else [],
        compiler_params=_params(("parallel", "parallel", "arbitrary")),
    )(*a_parts, *b_arrs, *extras, *tokens)
    return outs[0] if n_out == 1 else outs


def _add(acc, res):
    return (acc + res,)


def _rms_fwd(x, g, *, after=None, name):
    R, C = x.shape
    tm = _tile(R, 256)
    tokens = [] if after is None else [after]

    def body(x_ref, g_ref, *rest):
        xv = x_ref[...]
        r = lax.rsqrt(jnp.mean(xv * xv, axis=-1, keepdims=True) + EPS)
        rest[-1][...] = (xv * r * g_ref[...]).astype(BF16)

    return pl.pallas_call(
        body,
        name=name,
        grid=(R // tm,),
        in_specs=[pl.BlockSpec((tm, C), lambda i: (i, 0)), pl.BlockSpec((1, C), lambda i: (0, 0))]
        + [pl.BlockSpec(_TOKEN, lambda i: (0, 0))] * len(tokens),
        out_specs=pl.BlockSpec((tm, C), lambda i: (i, 0)),
        out_shape=jax.ShapeDtypeStruct((R, C), BF16),
        compiler_params=_params(("parallel",)),
    )(x, g.reshape(1, C), *tokens)


def _rms_bwd(x, g, dh, dres, *, name):
    R, C = x.shape
    tm = _tile(R, 256)

    def body(x_ref, g_ref, dh_ref, dres_ref, dx_ref, dg_ref):
        xv = x_ref[...]
        r = lax.rsqrt(jnp.mean(xv * xv, axis=-1, keepdims=True) + EPS)
        xn = xv * r
        dh_v = dh_ref[...].astype(F32)
        dxn = dh_v * g_ref[...]
        dx_ref[...] = r * (dxn - xn * jnp.mean(dxn * xn, axis=-1, keepdims=True)) + dres_ref[...]
        part = jnp.sum(dh_v * xn, axis=0, keepdims=True)

        @pl.when(pl.program_id(0) == 0)
        def _():
            dg_ref[...] = part

        @pl.when(pl.program_id(0) > 0)
        def _():
            dg_ref[...] += part

    row = pl.BlockSpec((tm, C), lambda i: (i, 0))
    vec = pl.BlockSpec((1, C), lambda i: (0, 0))
    dx, dg = pl.pallas_call(
        body,
        name=name,
        grid=(R // tm,),
        in_specs=[row, vec, row, row],
        out_specs=[row, vec],
        out_shape=[jax.ShapeDtypeStruct((R, C), F32), jax.ShapeDtypeStruct((1, C), F32)],
        compiler_params=_params(("arbitrary",)),
    )(x, g.reshape(1, C), dh, dres)
    return dx, dg.reshape(C)


def _final_loss(x, g, target, *, name):
    R, C = x.shape
    tm = _tile(R, 256)

    def body(x_ref, g_ref, t_ref, loss_ref, dx_ref, dg_ref):
        xv = x_ref[...]
        r = lax.rsqrt(jnp.mean(xv * xv, axis=-1, keepdims=True) + EPS)
        xn = xv * r
        gv = g_ref[...]
        err = xn * gv - t_ref[...]
        lpart = (0.5 / C) * jnp.sum(jnp.sum(err * err, axis=1, keepdims=True), axis=0, keepdims=True)
        dy = err * (1.0 / C)
        dxn = dy * gv
        dx_ref[...] = r * (dxn - xn * jnp.mean(dxn * xn, axis=-1, keepdims=True))
        gpart = jnp.sum(dy * xn, axis=0, keepdims=True)

        @pl.when(pl.program_id(0) == 0)
        def _():
            loss_ref[...] = lpart
            dg_ref[...] = gpart

        @pl.when(pl.program_id(0) > 0)
        def _():
            loss_ref[...] += lpart
            dg_ref[...] += gpart

    row = pl.BlockSpec((tm, C), lambda i: (i, 0))
    vec = pl.BlockSpec((1, C), lambda i: (0, 0))
    loss, dx, dg = pl.pallas_call(
        body,
        name=name,
        grid=(R // tm,),
        in_specs=[row, vec, row],
        out_specs=[pl.BlockSpec((1, 1), lambda i: (0, 0)), row, vec],
        out_shape=[jax.ShapeDtypeStruct((1, 1), F32), jax.ShapeDtypeStruct((R, C), F32), jax.ShapeDtypeStruct((1, C), F32)],
        compiler_params=_params(("arbitrary",)),
    )(x, g.reshape(1, C), target)
    return loss, dx, dg.reshape(C)


def _pool_select(lane, vals):
    out = vals[3]
    for gi in (2, 1, 0):
        out = jnp.where(lane < 64 * (gi + 1), vals[gi], out)
    return out


def _pool_diff(a):
    row, lane = _rows(a.shape), _lanes(a.shape)

    def down(v, k):
        return jnp.where(row >= k, pltpu.roll(v, k, 0), 0.0)

    s2 = a + down(a, 1)
    s4 = s2 + down(s2, 2)
    s8 = s4 + down(s4, 4)
    s16 = s8 + down(s8, 8)
    wsum = _pool_select(lane, (s2, s4, s8, s16))
    win = _pool_select(lane, (2, 4, 8, 16))
    cnt = jnp.minimum(row + 1, win).astype(F32)
    return wsum / cnt - a, cnt


def _pool_diff_t(dd, cnt):
    S = dd.shape[0]
    row, lane = _rows(dd.shape), _lanes(dd.shape)

    def up(v, k):
        return jnp.where(row < S - k, pltpu.roll(v, S - k, 0), 0.0)

    e = dd / cnt
    s2 = e + up(e, 1)
    s4 = s2 + up(s2, 2)
    s8 = s4 + up(s4, 4)
    s16 = s8 + up(s8, 8)
    return _pool_select(lane, (s2, s4, s8, s16)) - dd


def _pool_fwd(rest, wbd, scale, *, name):
    S = rest.shape[0]

    def body(a_ref, w_ref, s_ref, o_ref):
        d, _ = _pool_diff(a_ref[...])
        yp = _dot(d.astype(BF16), w_ref[...], 1, 0)
        o_ref[...] = (yp * s_ref[...]).astype(BF16)

    return pl.pallas_call(
        body,
        name=name,
        grid=(1,),
        in_specs=[
            pl.BlockSpec((S, POOL_W), lambda i: (0, OFF_A // POOL_W)),
            pl.BlockSpec((POOL_W, POOL_W), lambda i: (0, 0)),
            pl.BlockSpec((1, POOL_W), lambda i: (0, 0)),
        ],
        out_specs=pl.BlockSpec((S, POOL_W), lambda i: (0, 0)),
        out_shape=jax.ShapeDtypeStruct((S, POOL_W), BF16),
        compiler_params=_params(("arbitrary",)),
    )(rest, wbd, scale.reshape(1, POOL_W))


def _pool_bwd(rest, wbd, wbd_t, scale, dpa, *, name):
    S = rest.shape[0]

    def body(a_ref, w_ref, wt_ref, s_ref, dpa_ref, da_ref, dw_ref, ds_ref):
        d, cnt = _pool_diff(a_ref[...])
        db = d.astype(BF16)
        yp = _dot(db, w_ref[...], 1, 0)
        dpa_v = dpa_ref[...]
        ds_ref[...] = jnp.sum(dpa_v * yp, axis=0, keepdims=True)
        dyp = (dpa_v * s_ref[...]).astype(BF16)
        dw_ref[...] = _dot(db, dyp, 0, 0)
        dd = _dot(dyp, wt_ref[...], 1, 0)
        da_ref[...] = _pool_diff_t(dd, cnt).astype(BF16)

    full = pl.BlockSpec((S, POOL_W), lambda i: (0, 0))
    sq = pl.BlockSpec((POOL_W, POOL_W), lambda i: (0, 0))
    vec = pl.BlockSpec((1, POOL_W), lambda i: (0, 0))
    return pl.pallas_call(
        body,
        name=name,
        grid=(1,),
        in_specs=[pl.BlockSpec((S, POOL_W), lambda i: (0, OFF_A // POOL_W)), sq, sq, vec, full],
        out_specs=[full, sq, vec],
        out_shape=[
            jax.ShapeDtypeStruct((S, POOL_W), BF16),
            jax.ShapeDtypeStruct((POOL_W, POOL_W), F32),
            jax.ShapeDtypeStruct((1, POOL_W), F32),
        ],
        compiler_params=_params(("arbitrary",)),
    )(rest, wbd, wbd_t, scale.reshape(1, POOL_W), dpa)


def _log_sigmoid(z):
    return jnp.minimum(z, 0.0) - jnp.log(1.0 + jnp.exp(-jnp.abs(z)))


_F_SPEC_COL = OFF_F // F_LANES


def _fox_prep(rest, bpad, *, name):
    S = rest.shape[0]

    def body(f_ref, b_ref, o_ref, ot_ref):
        acc = _log_sigmoid(f_ref[...] + b_ref[...])
        row = _rows(acc.shape)
        k = 1
        while k < S:
            acc = acc + jnp.where(row >= k, pltpu.roll(acc, k, 0), 0.0)
            k *= 2
        o_ref[...] = acc
        ot_ref[...] = acc.T

    return pl.pallas_call(
        body,
        name=name,
        grid=(1,),
        in_specs=[pl.BlockSpec((S, F_LANES), lambda i: (0, _F_SPEC_COL)), pl.BlockSpec((1, F_LANES), lambda i: (0, 0))],
        out_specs=[pl.BlockSpec((S, F_LANES), lambda i: (0, 0)), pl.BlockSpec((F_LANES, S), lambda i: (0, 0))],
        out_shape=[jax.ShapeDtypeStruct((S, F_LANES), F32), jax.ShapeDtypeStruct((F_LANES, S), F32)],
        compiler_params=_params(("arbitrary",)),
    )(rest, bpad)


def _fox_post(rest, bpad, dcum, *, name):
    S = rest.shape[0]

    def body(f_ref, b_ref, d_ref, df_ref, db_ref):
        acc = d_ref[...]
        row = _rows(acc.shape)
        k = 1
        while k < S:
            acc = acc + jnp.where(row < S - k, pltpu.roll(acc, S - k, 0), 0.0)
            k *= 2
        df = acc * (1.0 - _sigmoid(f_ref[...] + b_ref[...]))
        df_ref[...] = df.astype(BF16)
        db_ref[...] = jnp.sum(df, axis=0, keepdims=True)

    full = pl.BlockSpec((S, F_LANES), lambda i: (0, 0))
    vec = pl.BlockSpec((1, F_LANES), lambda i: (0, 0))
    return pl.pallas_call(
        body,
        name=name,
        grid=(1,),
        in_specs=[pl.BlockSpec((S, F_LANES), lambda i: (0, _F_SPEC_COL)), vec, full],
        out_specs=[full, vec],
        out_shape=[jax.ShapeDtypeStruct((S, F_LANES), BF16), jax.ShapeDtypeStruct((1, F_LANES), F32)],
        compiler_params=_params(("arbitrary",)),
    )(rest, bpad, dcum)


_FOX_SCALE = FOX_DH ** -0.5
_PAIRS = FOX_H // 2


def _scaled(v):
    return (v.astype(F32) * _FOX_SCALE).astype(BF16)


def _diag_mask(s):
    return jnp.where(_rows(s.shape) >= _lanes(s.shape), s, NEG)


def _fox_fwd(qkv, cum, fk3, *, name):
    S = qkv.shape[0]
    nk, t = fk3.shape[1:]

    def body(q_ref, k_ref, v_ref, cum_ref, fk_ref, o_ref, lse_ref):
        i = pl.program_id(0)
        lane = _lanes((t, 128))
        lo = lane < FOX_DH
        cumv = cum_ref[...]
        qm, fq = [], []
        for h in range(FOX_H):
            qs = _scaled(q_ref[:, 128 * (h // 2):128 * (h // 2 + 1)])
            zero = jnp.zeros_like(qs)
            qm.append(jnp.where(lo, qs, zero) if h % 2 == 0 else jnp.where(lo, zero, qs))
            fq.append(cumv[:, h:h + 1])

        def tile(j, state, masked):
            m, acc, lsum = (list(part) for part in state)
            k0 = pl.multiple_of(j * t, t)
            for hp in range(_PAIRS):
                cols = slice(128 * hp, 128 * (hp + 1))
                kb = k_ref[pl.ds(k0, t), cols]
                vb = v_ref[pl.ds(k0, t), cols]
                one = jnp.ones_like(vb)
                alphas, pvs = [], []
                for h in (2 * hp, 2 * hp + 1):
                    s = _dot(qm[h], kb, 1, 1) + fq[h] - fk_ref[h, pl.ds(j, 1), :]
                    if masked:
                        s = _diag_mask(s)
                    m_new = jnp.maximum(m[h], jnp.max(s, axis=-1, keepdims=True))
                    p = jnp.exp(s - m_new)
                    alphas.append(jnp.exp(m[h] - m_new))
                    m[h] = m_new
                    pvs.append(_dot(p.astype(BF16), jnp.where(lo, vb, one) if h % 2 == 0 else jnp.where(lo, one, vb), 1, 0))
                acc[hp] = jnp.where(lo, alphas[0], alphas[1]) * acc[hp] + jnp.where(lo, pvs[0], pvs[1])
                lsum[hp] = jnp.where(lo, alphas[1], alphas[0]) * lsum[hp] + jnp.where(lo, pvs[1], pvs[0])
            return tuple(m), tuple(acc), tuple(lsum)

        zeros = (jnp.zeros((t, 128), F32),) * _PAIRS
        state = lax.fori_loop(0, i, functools.partial(tile, masked=False), ((jnp.full((t, 1), NEG, F32),) * FOX_H, zeros, zeros))
        m, acc, lsum = tile(i, state, True)
        for hp in range(_PAIRS):
            o_ref[:, 128 * hp:128 * (hp + 1)] = acc[hp] / pltpu.roll(lsum[hp], FOX_DH, 1)
            lse = [m[2 * hp] + jnp.log(lsum[hp][:, FOX_DH:FOX_DH + 1]), m[2 * hp + 1] + jnp.log(lsum[hp][:, 0:1])]
            lse_ref[hp] = jnp.where(lane == 0, lse[0], jnp.where(lane == 1, lse[1], 0.0))

    whole = lambda col: pl.BlockSpec((S, FOX_W), lambda i: (0, col))
    return pl.pallas_call(
        body,
        name=name,
        grid=(S // t,),
        in_specs=[
            pl.BlockSpec((t, FOX_W), lambda i: (i, 0)), whole(1), whole(2),
            pl.BlockSpec((t, F_LANES), lambda i: (i, 0)),
            pl.BlockSpec((FOX_H, nk, t), lambda i: (0, 0, 0)),
        ],
        out_specs=[pl.BlockSpec((t, FOX_W), lambda i: (i, 0)), pl.BlockSpec((_PAIRS, t, 128), lambda i: (0, i, 0))],
        out_shape=[jax.ShapeDtypeStruct((S, FOX_W), F32), jax.ShapeDtypeStruct((_PAIRS, S, 128), F32)],
        compiler_params=_params(("arbitrary",)),
    )(qkv, qkv, qkv, cum, fk3)


def _fox_bwd(qkv, cum, fk3, o, do, lse, *, name):
    S = qkv.shape[0]
    nk, t = fk3.shape[1:]
    q_at, k_at, v_at = 0, FOX_W, 2 * FOX_W

    def body(qkv_ref, cum_ref, fk_ref, o_ref, do_ref, lse_ref, dq_ref, dk_ref, dv_ref, dfq_ref, dfk_ref,
             qs_sc, ks_sc, delta_sc, dq_sc):
        lane = _lanes((t, 128))
        lo = lane < FOX_DH
        mine = lambda h: lo if h % 2 == 0 else jnp.logical_not(lo)

        def by_head(tile, values):
            for h, val in enumerate(values):
                tile = jnp.where(lane == h, val, tile)
            return tile

        def prep(i, carry):
            r = pl.ds(pl.multiple_of(i * t, t), t)
            qs_sc[r, :] = _scaled(qkv_ref[r, q_at:q_at + FOX_W])
            ks_sc[r, :] = _scaled(qkv_ref[r, k_at:k_at + FOX_W])
            sums = []
            for hp in range(_PAIRS):
                cols = slice(128 * hp, 128 * (hp + 1))
                prod = do_ref[r, cols].astype(F32) * o_ref[r, cols]
                sums += [jnp.sum(jnp.where(mine(h), prod, 0.0), axis=-1, keepdims=True) for h in (2 * hp, 2 * hp + 1)]
            delta_sc[r, :] = by_head(jnp.zeros((t, 128), F32), sums)
            dfq_ref[r, :] = jnp.zeros((t, 128), F32)
            dq_sc[r, :] = jnp.zeros((t, FOX_W), F32)
            return carry

        lax.fori_loop(0, nk, prep, 0)

        def kv_tile(j, carry):
            kr = pl.ds(pl.multiple_of(j * t, t), t)

            def q_tile(i, acc, masked):
                dk, dv, dfk = list(acc[:_PAIRS]), list(acc[_PAIRS:2 * _PAIRS]), list(acc[2 * _PAIRS:])
                qr = pl.ds(pl.multiple_of(i * t, t), t)
                delta_t, cum_t, dq_old, dfq_old = delta_sc[qr, :], cum_ref[qr, :], dq_sc[qr, :], dfq_ref[qr, :]
                row_sums, dq_new = [], []
                for hp in range(_PAIRS):
                    cols = slice(128 * hp, 128 * (hp + 1))
                    kb = qkv_ref[kr, k_at + 128 * hp:k_at + 128 * (hp + 1)]
                    vb = qkv_ref[kr, v_at + 128 * hp:v_at + 128 * (hp + 1)]
                    ksb, qsb, dob = ks_sc[kr, cols], qs_sc[qr, cols], do_ref[qr, cols]
                    zero = jnp.zeros_like(qsb)
                    dq_t = jnp.zeros((t, 128), F32)
                    for h in (2 * hp, 2 * hp + 1):
                        qe, doe, ke = (jnp.where(mine(h), a, zero) for a in (qsb, dob, ksb))
                        s = _dot(qe, kb, 1, 1) + cum_t[:, h:h + 1] - fk_ref[h, pl.ds(j, 1), :]
                        if masked:
                            s = _diag_mask(s)
                        p = jnp.exp(s - lse_ref[hp, qr, h % 2:h % 2 + 1])
                        dv[hp] = dv[hp] + _dot(p.astype(BF16), doe, 0, 0)
                        dp = _dot(doe, vb, 1, 1)
                        ds = p * (dp - delta_t[:, h:h + 1])
                        dsb = ds.astype(BF16)
                        dk[hp] = dk[hp] + _dot(dsb, qe, 0, 0)
                        dq_t = dq_t + _dot(dsb, ke, 1, 0)
                        row_sums.append(jnp.sum(ds, axis=-1, keepdims=True))
                        dfk[h] = dfk[h] - jnp.sum(ds, axis=0, keepdims=True)
                    dq_new.append(dq_old[:, cols] + dq_t)
                for hp in range(_PAIRS):
                    dq_sc[qr, 128 * hp:128 * (hp + 1)] = dq_new[hp]
                dfq_ref[qr, :] = dfq_old + by_head(jnp.zeros((t, 128), F32), row_sums)
                return (*dk, *dv, *dfk)

            init = tuple([jnp.zeros((t, 128), F32)] * (2 * _PAIRS) + [jnp.zeros((1, t), F32)] * FOX_H)
            acc = q_tile(j, init, True)
            acc = lax.fori_loop(j + 1, nk, functools.partial(q_tile, masked=False), acc)
            for hp in range(_PAIRS):
                cols = slice(128 * hp, 128 * (hp + 1))
                dk_ref[kr, cols] = acc[hp].astype(BF16)
                dv_ref[kr, cols] = acc[_PAIRS + hp].astype(BF16)
            for h in range(FOX_H):
                dfk_ref[h, pl.ds(j, 1), :] = acc[2 * _PAIRS + h]
            return carry

        lax.fori_loop(0, nk, kv_tile, 0)
        dq_ref[...] = dq_sc[...].astype(BF16)

    vm = pl.BlockSpec(memory_space=pltpu.VMEM)
    big = jax.ShapeDtypeStruct((S, FOX_W), BF16)
    return pl.pallas_call(
        body,
        name=name,
        in_specs=[vm] * 6,
        out_specs=[vm] * 5,
        out_shape=[big, big, big, jax.ShapeDtypeStruct((S, 128), F32), jax.ShapeDtypeStruct((FOX_H, nk, t), F32)],
        scratch_shapes=[pltpu.VMEM((S, FOX_W), BF16), pltpu.VMEM((S, FOX_W), BF16), pltpu.VMEM((S, 128), F32),
                        pltpu.VMEM((S, FOX_W), F32)],
        compiler_params=pltpu.CompilerParams(vmem_limit_bytes=VMEM_LIMIT),
    )(qkv, cum, fk3, o, do, lse)


def _group_mask(lane, gi):
    return (lane >= 64 * gi) & (lane < 64 * (gi + 1))


_U_COL = OFF_C // SGU_W


def _sgu_fwd(rest, gn, wm, bias, *, name):
    S = rest.shape[0]
    ts = _tile(S, 512)
    nc = ts // SGU_CHUNK

    def body(u_ref, v_ref, g_ref, w_ref, b_ref, o_ref):
        zv = _gelu(v_ref[...])
        vn = zv * lax.rsqrt(jnp.mean(zv * zv, axis=-1, keepdims=True) + EPS) * g_ref[...]
        lane = _lanes((SGU_CHUNK, SGU_W))
        for c in range(nc):
            rows = slice(c * SGU_CHUNK, (c + 1) * SGU_CHUNK)
            vcb = vn[rows].astype(BF16)
            mixed = b_ref[...]
            for gi in range(4):
                mixed = mixed + jnp.where(_group_mask(lane, gi), _dot(w_ref[gi], vcb, 1, 0), 0.0)
            o_ref[rows, :] = (_gelu(u_ref[rows, :]) * mixed).astype(BF16)

    return pl.pallas_call(
        body,
        name=name,
        grid=(S // ts,),
        in_specs=[
            pl.BlockSpec((ts, SGU_W), lambda i: (i, _U_COL)),
            pl.BlockSpec((ts, SGU_W), lambda i: (i, _U_COL + 1)),
            pl.BlockSpec((1, SGU_W), lambda i: (0, 0)),
            pl.BlockSpec((4, SGU_CHUNK, SGU_CHUNK), lambda i: (0, 0, 0)),
            pl.BlockSpec((SGU_CHUNK, SGU_W), lambda i: (0, 0)),
        ],
        out_specs=pl.BlockSpec((ts, SGU_W), lambda i: (i, 0)),
        out_shape=jax.ShapeDtypeStruct((S, SGU_W), BF16),
        compiler_params=_params(("parallel",)),
    )(rest, rest, gn.reshape(1, SGU_W), wm, bias)


def _sgu_bwd(rest, gn, wm, wm_t, bias, dsg, *, name):
    S = rest.shape[0]
    ts = _tile(S, 512)
    nc = ts // SGU_CHUNK

    def body(u_ref, v_ref, g_ref, w_ref, wt_ref, b_ref, dsg_ref, dc_ref, dw_ref, db_ref, dg_ref):
        first = pl.program_id(0) == 0

        @pl.when(first)
        def _():
            dw_ref[...] = jnp.zeros_like(dw_ref)
            db_ref[...] = jnp.zeros_like(db_ref)
            dg_ref[...] = jnp.zeros_like(dg_ref)

        gv = g_ref[...]
        lane = _lanes((SGU_CHUNK, SGU_W))
        for c in range(nc):
            rows = slice(c * SGU_CHUNK, (c + 1) * SGU_CHUNK)
            vpre = v_ref[rows, :]
            upre = u_ref[rows, :]
            zv = _gelu(vpre)
            r = lax.rsqrt(jnp.mean(zv * zv, axis=-1, keepdims=True) + EPS)
            zn = zv * r
            vcb = (zn * gv).astype(BF16)
            mixed = b_ref[...]
            for gi in range(4):
                mixed = mixed + jnp.where(_group_mask(lane, gi), _dot(w_ref[gi], vcb, 1, 0), 0.0)
            zu = _gelu(upre)
            dsg_v = dsg_ref[rows, :]
            dc_ref[rows, :SGU_W] = (dsg_v * mixed * _gelu_grad(upre)).astype(BF16)
            dmixed = dsg_v * zu
            db_ref[...] += dmixed
            dvn = jnp.zeros((SGU_CHUNK, SGU_W), F32)
            for gi in range(4):
                dmg = jnp.where(_group_mask(lane, gi), dmixed, 0.0).astype(BF16)
                dw_ref[gi] += _dot(dmg, vcb, 1, 1)
                dvn = dvn + _dot(wt_ref[gi], dmg, 1, 0)
            dg_ref[...] += jnp.sum(dvn * zn, axis=0, keepdims=True)
            dzn = dvn * gv
            dzv = r * (dzn - zn * jnp.mean(dzn * zn, axis=-1, keepdims=True))
            dc_ref[rows, SGU_W:] = (dzv * _gelu_grad(vpre)).astype(BF16)

    blk = pl.BlockSpec((ts, SGU_W), lambda i: (i, 0))
    vec = pl.BlockSpec((1, SGU_W), lambda i: (0, 0))
    w3 = pl.BlockSpec((4, SGU_CHUNK, SGU_CHUNK), lambda i: (0, 0, 0))
    bsp = pl.BlockSpec((SGU_CHUNK, SGU_W), lambda i: (0, 0))
    return pl.pallas_call(
        body,
        name=name,
        grid=(S // ts,),
        in_specs=[
            pl.BlockSpec((ts, SGU_W), lambda i: (i, _U_COL)),
            pl.BlockSpec((ts, SGU_W), lambda i: (i, _U_COL + 1)),
            vec, w3, w3, bsp, blk,
        ],
        out_specs=[pl.BlockSpec((ts, 2 * SGU_W), lambda i: (i, 0)), w3, bsp, vec],
        out_shape=[
            jax.ShapeDtypeStruct((S, 2 * SGU_W), BF16),
            jax.ShapeDtypeStruct((4, SGU_CHUNK, SGU_CHUNK), F32),
            jax.ShapeDtypeStruct((SGU_CHUNK, SGU_W), F32),
            jax.ShapeDtypeStruct((1, SGU_W), F32),
        ],
        compiler_params=_params(("arbitrary",)),
    )(rest, rest, gn.reshape(1, SGU_W), wm, wm_t, bias, dsg)


_GT = 512
_G0 = OFF_G // _GT


def _gate_specs(tm, col_of):
    specs = [pl.BlockSpec((tm, _GT), functools.partial(lambda k, *ids: (col_of(*ids)[0], _G0 + 2 * k + col_of(*ids)[1]), k)) for k in range(3)]
    specs += [pl.BlockSpec((1, _GT), functools.partial(lambda k, *ids: (0, 2 * k + col_of(*ids)[1]), k)) for k in range(3)]
    return specs


def _merge_fwd(rest, bg, ya, yb, yc, *, name):
    S = rest.shape[0]
    tm = _tile(S, 512)

    def body(g1, g2, g3, b1, b2, b3, ya_ref, yb_ref, yc_ref, o_ref):
        acc = _sigmoid(g1[...] + b1[...]) * ya_ref[...]
        acc = acc + _sigmoid(g2[...] + b2[...]) * yb_ref[...]
        acc = acc + _sigmoid(g3[...] + b3[...]) * yc_ref[...]
        o_ref[...] = acc.astype(BF16)

    blk = pl.BlockSpec((tm, _GT), lambda i, j: (i, j))
    return pl.pallas_call(
        body,
        name=name,
        grid=(S // tm, D // _GT),
        in_specs=_gate_specs(tm, lambda i, j: (i, j)) + [blk, blk, blk],
        out_specs=blk,
        out_shape=jax.ShapeDtypeStruct((S, D), BF16),
        compiler_params=_params(("parallel", "parallel")),
    )(rest, rest, rest, bg, bg, bg, ya, yb, yc)


def _merge_bwd(rest, bg, ya, yb, yc, dm, *, name):
    S = rest.shape[0]
    tm = _tile(S, 512)

    def body(g1, g2, g3, b1, b2, b3, ya_ref, yb_ref, yc_ref, dm_ref, dya, dyb, dyc, dg1, dg2, dg3, db1, db2, db3):
        first = pl.program_id(1) == 0
        dmv = dm_ref[...]
        for g_ref, b_ref, y_ref, dy_ref, dg_ref, db_ref in (
            (g1, b1, ya_ref, dya, dg1, db1), (g2, b2, yb_ref, dyb, dg2, db2), (g3, b3, yc_ref, dyc, dg3, db3)):
            gate = _sigmoid(g_ref[...] + b_ref[...])
            dy_ref[...] = (dmv * gate).astype(BF16)
            dpre = dmv * y_ref[...] * gate * (1.0 - gate)
            dg_ref[...] = dpre.astype(BF16)
            part = jnp.sum(dpre, axis=0, keepdims=True)

            @pl.when(first)
            def _():
                db_ref[...] = part

            @pl.when(jnp.logical_not(first))
            def _():
                db_ref[...] += part

    blk = pl.BlockSpec((tm, _GT), lambda j, i: (i, j))
    vec = pl.BlockSpec((1, _GT), lambda j, i: (0, j))
    big = jax.ShapeDtypeStruct((S, D), BF16)
    small = jax.ShapeDtypeStruct((1, D), F32)
    return pl.pallas_call(
        body,
        name=name,
        grid=(D // _GT, S // tm),
        in_specs=_gate_specs(tm, lambda j, i: (i, j)) + [blk, blk, blk, blk],
        out_specs=[blk] * 6 + [vec] * 3,
        out_shape=[big] * 6 + [small] * 3,
        compiler_params=_params(("parallel", "arbitrary")),
    )(rest, rest, rest, bg, bg, bg, ya, yb, yc, dm)


_X_SCALE = XDH ** -0.5


def _xattn_fwd(xq, kv, *, name):
    S = xq.shape[0]
    M = kv.shape[0]
    tq = _tile(S, 512)

    def body(q_ref, k_ref, v_ref, o_ref):
        s = _dot(q_ref[...], k_ref[...], 1, 1) * _X_SCALE
        e = jnp.exp(s - jnp.max(s, axis=-1, keepdims=True))
        p = e / jnp.sum(e, axis=-1, keepdims=True)
        o_ref[...] = _dot(p.astype(BF16), v_ref[...], 1, 0).astype(BF16)

    return pl.pallas_call(
        body,
        name=name,
        grid=(S // tq, XH),
        in_specs=[
            pl.BlockSpec((tq, XDH), lambda i, h: (i, h)),
            pl.BlockSpec((M, XDH), lambda i, h: (0, h)),
            pl.BlockSpec((M, XDH), lambda i, h: (0, XH + h)),
        ],
        out_specs=pl.BlockSpec((tq, XDH), lambda i, h: (i, h)),
        out_shape=jax.ShapeDtypeStruct((S, D), BF16),
        compiler_params=_params(("parallel", "parallel")),
    )(xq, kv, kv)


def _xattn_bwd(xq, kv, do, *, name):
    S = xq.shape[0]
    M = kv.shape[0]
    tq = _tile(S, 512)

    def body(q_ref, k_ref, v_ref, do_ref, dq_ref, dk_ref, dv_ref):
        qb = q_ref[...]
        kb = k_ref[...]
        dob = do_ref[...]
        s = _dot(qb, kb, 1, 1) * _X_SCALE
        e = jnp.exp(s - jnp.max(s, axis=-1, keepdims=True))
        p = e / jnp.sum(e, axis=-1, keepdims=True)
        dp = _dot(dob, v_ref[...], 1, 1)
        ds = (p * (dp - jnp.sum(p * dp, axis=-1, keepdims=True)) * _X_SCALE).astype(BF16)
        dq_ref[...] = _dot(ds, kb, 1, 0).astype(BF16)
        dk_part = _dot(ds, qb, 0, 0)
        dv_part = _dot(p.astype(BF16), dob, 0, 0)

        @pl.when(pl.program_id(1) == 0)
        def _():
            dk_ref[...] = dk_part
            dv_ref[...] = dv_part

        @pl.when(pl.program_id(1) > 0)
        def _():
            dk_ref[...] += dk_part
            dv_ref[...] += dv_part

    qspec = pl.BlockSpec((tq, XDH), lambda h, i: (i, h))
    kspec = pl.BlockSpec((M, XDH), lambda h, i: (0, h))
    dxq, dxk, dxv = pl.pallas_call(
        body,
        name=name,
        grid=(XH, S // tq),
        in_specs=[qspec, kspec, pl.BlockSpec((M, XDH), lambda h, i: (0, XH + h)), qspec],
        out_specs=[qspec, kspec, kspec],
        out_shape=[jax.ShapeDtypeStruct((S, D), BF16), jax.ShapeDtypeStruct((M, D), F32), jax.ShapeDtypeStruct((M, D), F32)],
        compiler_params=_params(("parallel", "arbitrary")),
    )(xq, kv, kv, do)
    return dxq, jnp.concatenate([dxk, dxv], axis=1)


def _adam_math(w, g, m, v):
    m = ADAM_B1 * m + (1.0 - ADAM_B1) * g
    v = ADAM_B2 * v + (1.0 - ADAM_B2) * (g * g)
    m_hat = m / (1.0 - ADAM_B1 ** ADAM_STEP)
    v_hat = v / (1.0 - ADAM_B2 ** ADAM_STEP)
    delta = -ADAM_LR * (m_hat / (jnp.sqrt(v_hat) + ADAM_EPS) + ADAM_WD * w)
    return delta, m, v


def _adamw_sharded(parts, w, m, v, *, name):
    _, R, C = w.shape
    Cp = parts[0].shape[2]
    tm = _tile(R, 256)
    nr = R // tm

    def body(p0_ref, p1_ref, w_ref, m_ref, v_ref, g_ref, d_ref, mo_ref, vo_ref):
        def update(p_ref):
            g = p_ref[0][:, :C].astype(F32)
            for dev in range(1, N_DEV):
                g = g + p_ref[dev][:, :C].astype(F32)
            delta, mn, vn = _adam_math(w_ref[...], g, m_ref[...], v_ref[...])
            g_ref[...] = g
            d_ref[...] = delta
            mo_ref[...] = mn
            vo_ref[...] = vn

        @pl.when(pl.program_id(0) == 0)
        def _():
            update(p0_ref)

        @pl.when(pl.program_id(0) == 1)
        def _():
            update(p1_ref)

    p0 = pl.BlockSpec((N_DEV, tm, Cp), lambda l, i: (0, i * (1 - l) + (nr - 1) * l, 0))
    p1 = pl.BlockSpec((N_DEV, tm, Cp), lambda l, i: (0, i * l, 0))
    blk = pl.BlockSpec((None, tm, C), lambda l, i: (l, i, 0))
    sds = jax.ShapeDtypeStruct(w.shape, F32)
    return pl.pallas_call(
        body,
        name=name,
        grid=(DEPTH, nr),
        in_specs=[p0, p1, blk, blk, blk],
        out_specs=[blk] * 4,
        out_shape=[sds] * 4,
        compiler_params=_params(("arbitrary", "arbitrary")),
    )(parts[0], parts[1], w, m, v)


def _adamw_small(g, w, m, v, *, name):
    n = len(g)

    def body(*refs):
        g_refs, w_refs, m_refs, v_refs = (refs[k * n:(k + 1) * n] for k in range(4))
        d_out, m_out, v_out = (refs[(4 + k) * n:(5 + k) * n] for k in range(3))
        for t in range(n):
            delta, mn, vn = _adam_math(w_refs[t][...], g_refs[t][...], m_refs[t][...], v_refs[t][...])
            d_out[t][...] = delta
            m_out[t][...] = mn
            v_out[t][...] = vn

    vm = pl.BlockSpec(memory_space=pltpu.VMEM)
    shapes = [jax.ShapeDtypeStruct(a.shape, F32) for a in w]
    outs = pl.pallas_call(
        body,
        name=name,
        in_specs=[vm] * (4 * n),
        out_specs=[vm] * (3 * n),
        out_shape=shapes * 3,
        compiler_params=pltpu.CompilerParams(vmem_limit_bytes=VMEM_LIMIT),
    )(*g, *w, *m, *v)
    return outs[:n], outs[n:2 * n], outs[2 * n:]


def _position():
    return lax.axis_index("x"), lax.axis_index("y"), lax.axis_index("c")


def _dev_index(px, py, pc):
    return 4 * px + 2 * py + pc


_ANY = pl.BlockSpec(memory_space=pl.ANY)


def _all_gather(shards, *, name):
    n = len(shards)
    out_shape = [jax.ShapeDtypeStruct((N_DEV, *s.shape), s.dtype) for s in shards]
    n_pieces = len(_pieces(out_shape))

    def body(*refs):
        ins, outs = refs[:n], refs[n:2 * n]
        send_sems, recv_sems, local_sems = refs[2 * n:]
        x, y, c = _position()
        me, sibling = (x, y, c), (x, y, 1 - c)
        chips = [(1 - x, y), (x, 1 - y), (1 - x, 1 - y)]
        pieces = _pieces(outs)

        def copy(i, k, block, to, from_input=False):
            t, rows = pieces[i]
            dst = _cut(outs[t].at[_dev_index(*block)], rows)
            return pltpu.make_async_remote_copy(
                src_ref=_cut(ins[t], rows) if from_input else dst, dst_ref=dst, send_sem=send_sems.at[i, k],
                recv_sem=recv_sems.at[i, k], device_id=to, device_id_type=MESH)

        mine = [pltpu.make_async_copy(_cut(ins[t], rows), _cut(outs[t].at[_dev_index(*me)], rows), local_sems.at[i])
                for i, (t, rows) in enumerate(pieces)]
        for cp in mine:
            cp.start()
        started = []
        for j, chip in enumerate(chips):
            for i in range(n_pieces):
                started.append(copy(i, 1 + j, me, (*chip, c), from_input=True))
                started[-1].start()
        for i in range(n_pieces):
            started.append(copy(i, 0, me, sibling, from_input=True))
            started[-1].start()
        for j, chip in enumerate(chips):
            for i in range(n_pieces):
                copy(i, 1 + j, (*chip, c), me).wait_recv()
                started.append(copy(i, 4 + j, (*chip, c), sibling))
                started[-1].start()
        for i in range(n_pieces):
            copy(i, 0, sibling, me).wait_recv()
        for j, chip in enumerate(chips):
            for i in range(n_pieces):
                copy(i, 4 + j, (*chip, 1 - c), me).wait_recv()
        for cp in started:
            cp.wait_send()
        for cp in mine:
            cp.wait()

    return pl.pallas_call(
        body,
        name=name,
        in_specs=[_ANY] * n,
        out_specs=[_ANY] * n,
        out_shape=out_shape,
        scratch_shapes=[pltpu.SemaphoreType.DMA((n_pieces, 7)), pltpu.SemaphoreType.DMA((n_pieces, 7)),
                        pltpu.SemaphoreType.DMA((n_pieces,))],
        compiler_params=pltpu.CompilerParams(has_side_effects=True),
    )(*shards)


def _peers(x, y, c):
    out = []
    for mask in range(1, N_DEV):
        fx, fy, fc = (mask >> 2) & 1, (mask >> 1) & 1, mask & 1
        out.append((1 - x if fx else x, 1 - y if fy else y, 1 - c if fc else c))
    return out


_HBM = pl.BlockSpec(memory_space=pltpu.HBM)
_SEM = pl.BlockSpec(memory_space=pltpu.SEMAPHORE)


def _own_block_placed(block, like):
    x, y, c = _position()
    return lax.dynamic_update_index_in_dim(lax.empty(like.shape, like.dtype), block, _dev_index(x, y, c), 0)


_COPY_BYTES = 256 << 10
_MAX_PIECES = 8


def _pieces(blocks):
    out = []
    for t, b in enumerate(blocks):
        R, C = b.shape[-2:]
        n = max(1, min(_MAX_PIECES, R * C * jnp.dtype(b.dtype).itemsize // _COPY_BYTES))
        while n > 1 and R % (16 * n):
            n -= 1
        out += [(t, pl.ds(j * (R // n), R // n) if n > 1 else None) for j in range(n)]
    return out


def _cut(block, rows):
    return block if rows is None else block.at[rows]


def _copies(per_piece):
    def mark(fn):
        fn.per_piece = per_piece
        return fn
    return mark


@_copies(N_DEV - 1)
def _plan_exchange(srcs, lands, send_sems, recv_sems, arrivals):
    x, y, c = _position()
    me = _dev_index(x, y, c)
    out = []
    for k, peer in enumerate(_peers(x, y, c)):
        p = _dev_index(*peer)
        for i, (t, rows) in enumerate(_pieces(lands)):
            sems = dict(send_sem=send_sems.at[7 * i + k], recv_sem=recv_sems.at[7 * i + k], device_id=peer, device_id_type=MESH)
            src, dst = (lands[t].at[p], lands[t].at[p]) if arrivals else (srcs[t].at[p], lands[t].at[me])
            out.append(pltpu.make_async_remote_copy(src_ref=_cut(src, rows), dst_ref=_cut(dst, rows), **sems))
    return out


@_copies(N_DEV - 1)
def _plan_broadcast(srcs, lands, send_sems, recv_sems, arrivals):
    x, y, c = _position()
    me = _dev_index(x, y, c)
    out = []
    for k, peer in enumerate(_peers(x, y, c)):
        p = _dev_index(*peer)
        for i, (t, rows) in enumerate(_pieces(lands)):
            sems = dict(send_sem=send_sems.at[7 * i + k], recv_sem=recv_sems.at[7 * i + k], device_id=peer, device_id_type=MESH)
            src, dst = (lands[t].at[p], lands[t].at[p]) if arrivals else (srcs[t], lands[t].at[me])
            out.append(pltpu.make_async_remote_copy(src_ref=_cut(src, rows), dst_ref=_cut(dst, rows), **sems))
    return out


@_copies(4)
def _plan_gather_out(srcs, lands, send_sems, recv_sems, arrivals):
    x, y, c = _position()
    me = _dev_index(x, y, c)
    out = []
    for k, peer in enumerate([(x, y, 1 - c), (1 - x, y, c), (x, 1 - y, c), (1 - x, 1 - y, c)]):
        p = _dev_index(*peer)
        for i, (t, rows) in enumerate(_pieces(lands)):
            sems = dict(send_sem=send_sems.at[4 * i + k], recv_sem=recv_sems.at[4 * i + k], device_id=peer, device_id_type=MESH)
            src, dst = (lands[t].at[p], lands[t].at[p]) if arrivals else (srcs[t], lands[t].at[me])
            out.append(pltpu.make_async_remote_copy(src_ref=_cut(src, rows), dst_ref=_cut(dst, rows), **sems))
    return out


@_copies(3)
def _plan_gather_pass(srcs, lands, send_sems, recv_sems, arrivals):
    x, y, c = _position()
    sibling = (x, y, 1 - c)
    out = []
    for k, chip in enumerate([(1 - x, y), (x, 1 - y), (1 - x, 1 - y)]):
        p = _dev_index(*chip, 1 - c) if arrivals else _dev_index(*chip, c)
        for i, (t, rows) in enumerate(_pieces(lands)):
            sems = dict(send_sem=send_sems.at[3 * i + k], recv_sem=recv_sems.at[3 * i + k], device_id=sibling, device_id_type=MESH)
            block = _cut(lands[t].at[p], rows)
            out.append(pltpu.make_async_remote_copy(src_ref=block, dst_ref=block, **sems))
    return out


def _split_start(plan, srcs, lands, *, after=None, name):
    n_src, n = len(srcs), len(srcs) + len(lands)
    n_sem = plan.per_piece * len(_pieces(lands))
    order = [] if after is None else [after]

    def body(*refs):
        send_sems, recv_sems = refs[n + len(order):n + len(order) + 2]
        token = refs[-1]
        for cp in plan(refs[:n_src], refs[n_src:n], send_sems, recv_sems, arrivals=False):
            cp.start()
        token[...] = jnp.zeros_like(token)

    hbm = lambda a: pltpu.HBM(a.shape, a.dtype)
    outs = pl.pallas_call(
        body,
        name=name,
        in_specs=[_HBM] * n + [_ANY] * len(order),
        out_specs=[_SEM, _SEM] + [_HBM] * n + [pl.BlockSpec(memory_space=pltpu.VMEM)],
        out_shape=[pltpu.SemaphoreType.DMA((n_sem,)), pltpu.SemaphoreType.DMA((n_sem,))] + [hbm(a) for a in (*srcs, *lands)]
        + [jax.ShapeDtypeStruct(_TOKEN, F32)],
        input_output_aliases={i: 2 + i for i in range(n)},
        compiler_params=pltpu.CompilerParams(has_side_effects=pltpu.SideEffectType.DATAFLOW_SIDE_EFFECTING),
    )(*[pltpu.with_memory_space_constraint(a, pltpu.HBM) for a in (*srcs, *lands)], *order)
    return (outs[0], outs[1], outs[2:2 + n_src], outs[2 + n_src:2 + n]), outs[-1]


def _split_wait(plan, state, after, *, name):
    send_sems, recv_sems, srcs, lands = state
    n_src, n = len(srcs), len(srcs) + len(lands)

    def body(*refs):
        send_refs, recv_refs = refs[n:n + 2]
        for cp in plan(refs[:n_src], refs[n_src:n], send_refs, recv_refs, arrivals=False):
            cp.wait_send()
        for cp in plan(refs[:n_src], refs[n_src:n], send_refs, recv_refs, arrivals=True):
            cp.wait_recv()

    hbm = lambda a: pltpu.HBM(a.shape, a.dtype)
    outs = pl.pallas_call(
        body,
        name=name,
        in_specs=[_HBM] * n + [_SEM, _SEM, _ANY],
        out_specs=[_HBM] * n,
        out_shape=[hbm(a) for a in (*srcs, *lands)],
        input_output_aliases={i: i for i in range(n)},
        compiler_params=pltpu.CompilerParams(has_side_effects=pltpu.SideEffectType.DATAFLOW_SIDE_EFFECTING),
    )(*srcs, *lands, send_sems, recv_sems, after)
    return outs[n_src:]


def _sum_blocks(blocks, *, name):
    _, R, C = blocks.shape
    tm = next(R // n for n in (4, 3, 2, 1) if R % (8 * n) == 0)

    def body(b_ref, o_ref):
        g = b_ref[0]
        for dev in range(1, N_DEV):
            g = g + b_ref[dev]
        o_ref[...] = g

    return pl.pallas_call(
        body,
        name=name,
        grid=(R // tm,),
        in_specs=[pl.BlockSpec((N_DEV, tm, C), lambda i: (0, i, 0))],
        out_specs=pl.BlockSpec((tm, C), lambda i: (i, 0)),
        out_shape=jax.ShapeDtypeStruct((R, C), F32),
        compiler_params=_params(("parallel",)),
    )(blocks)


def _block_diag(w):
    out = jnp.zeros((POOL_W, POOL_W), w.dtype)
    for gi in range(4):
        out = out.at[64 * gi:64 * (gi + 1), 64 * gi:64 * (gi + 1)].set(w[gi])
    return out


def _layer_consts(sp, l):
    causal = jnp.tril(jnp.ones((SGU_CHUNK, SGU_CHUNK), F32))
    wm = (sp["sgu_w"][l] * causal[None]).astype(BF16)
    wbd = _block_diag(sp["pool_w"][l]).astype(BF16)
    return dict(
        wbd=wbd, wbd_t=wbd.T, wm=wm, wm_t=wm.transpose(0, 2, 1),
        sgu_bias=jnp.repeat(sp["sgu_b"][l].T, 64, axis=1),
        bpad=jnp.pad(sp["b_forget"][l], (0, F_LANES - FOX_H)).reshape(1, F_LANES),
        bg=sp["b_gate"][l].reshape(1, 3 * D),
    )


def _relu2(acc):
    return acc, jnp.square(jnp.maximum(acc, 0.0))


def _relu2_grad(acc, z):
    return (acc * 2.0 * jnp.maximum(z, 0.0),)


def _layer_fwd(l, x, mem, source, sp):
    S = x.shape[0]
    t = _tile(S, 256)
    c = _layer_consts(sp, l)
    n = f"l{l}_"
    W, after = source(l, "begin", x)
    h = _rms_fwd(x, sp["norm_mix_g"][l], after=after, name=n + "norm_mix")
    qkv = _mm(h, W["qkv"], out_dtypes=(BF16,), name=n + "qkv")
    rest = _mm(h, W["rest"], name=n + "rest")
    pa = _pool_fwd(rest, c["wbd"], sp["pool_scale"][l], name=n + "pool")
    cum, cum_t = _fox_prep(rest, c["bpad"], name=n + "fox_prep")
    fk3 = cum_t[:FOX_H].reshape(FOX_H, S // t, t)
    o, lse = _fox_fwd(qkv, cum, fk3, name=n + "fox")
    more, _ = source(l, "attended", o)
    W.update(more)
    sg = _sgu_fwd(rest, sp["sgu_norm_g"][l], c["wm"], c["sgu_bias"], name=n + "sgu")
    more, after = source(l, "mixed", sg)
    W.update(more)
    ya = _mm(pa, W["ba"], out_dtypes=(BF16,), after=after, name=n + "branch_a")
    yb = _mm(o, W["bb"], out_dtypes=(BF16,), name=n + "branch_b")
    yc = _mm(sg, W["bc"], out_dtypes=(BF16,), name=n + "branch_c")
    merged = _merge_fwd(rest, c["bg"], ya, yb, yc, name=n + "merge")
    x1 = _mm(merged, W["out"], extras=(x,), epilogue=_add, name=n + "out")
    hx = _rms_fwd(x1, sp["norm_xattn_g"][l], name=n + "norm_xattn")
    hm = _rms_fwd(mem, sp["norm_mem_g"][l], name=n + "norm_mem")
    xq = _mm(hx, W["xq"], out_dtypes=(BF16,), name=n + "xq")
    kv = _mm(hm, W["xkv"], out_dtypes=(BF16,), name=n + "xkv")
    o2 = _xattn_fwd(xq, kv, name=n + "xattn")
    x2 = _mm(o2, W["xo"], extras=(x1,), epilogue=_add, name=n + "xo")
    hf = _rms_fwd(x2, sp["norm_ffn_g"][l], name=n + "norm_ffn")
    z, act = _mm(hf, W["ff1"], epilogue=_relu2, out_dtypes=(BF16, BF16), name=n + "ff1")
    _, after = source(l, "expanded", act)
    x3 = _mm(act, W["ff2"], extras=(x2,), epilogue=_add, after=after, name=n + "ff2")
    saved = dict(x=x, h=h, qkv=qkv, rest=rest, pa=pa, cum=cum, fk3=fk3, o=o, lse=lse, sg=sg, ya=ya, yb=yb, yc=yc,
                 merged=merged, x1=x1, hx=hx, hm=hm, xq=xq, kv=kv, o2=o2, x2=x2, hf=hf, z=z, act=act, c=c)
    return x3, saved, W


def _layer_bwd(l, dx3, sv, mem, W, sp, grads_done):
    S = dx3.shape[0]
    c = sv["c"]
    n = f"l{l}b_"
    bf = dict(out_dtypes=(BF16,))
    gw, gs = {}, {}
    gw["ff2"] = _mm(sv["act"], dx3, ta=True, name=n + "dw_ff2", **bf)
    dz = _mm(dx3, W["ff2"], tb=True, extras=(sv["z"],), epilogue=_relu2_grad, name=n + "dz", **bf)
    gw["ff1"] = _mm(sv["hf"], dz, ta=True, shard_out=True, name=n + "dw_ff1", **bf)
    dhf = _mm(dz, W["ff1"], tb=True, name=n + "dhf")
    dx2, gs["norm_ffn_g"] = _rms_bwd(sv["x2"], sp["norm_ffn_g"][l], dhf, dx3, name=n + "dnorm_ffn")
    gw["xo"] = _mm(sv["o2"], dx2, ta=True, name=n + "dw_xo", **bf)
    do2 = _mm(dx2, W["xo"], tb=True, name=n + "do2", **bf)
    dxq, dkv = _xattn_bwd(sv["xq"], sv["kv"], do2, name=n + "dxattn")
    gw["xq"] = _mm(sv["hx"], dxq, ta=True, name=n + "dw_xq", **bf)
    gw["xkv"] = _mm(sv["hm"], dkv, ta=True, shard_out=True, name=n + "dw_xkv", **bf)
    dhm = _mm(dkv, W["xkv"], tb=True, name=n + "dhm")
    _, gs["norm_mem_g"] = _rms_bwd(mem, sp["norm_mem_g"][l], dhm, jnp.zeros_like(mem), name=n + "dnorm_mem")
    dhx = _mm(dxq, W["xq"], tb=True, name=n + "dhx")
    dx1, gs["norm_xattn_g"] = _rms_bwd(sv["x1"], sp["norm_xattn_g"][l], dhx, dx2, name=n + "dnorm_xattn")
    after, gw = grads_done(l, gw), {}
    gw["out"] = _mm(sv["merged"], dx1, ta=True, name=n + "dw_out", **bf)
    dm = _mm(dx1, W["out"], tb=True, after=after, name=n + "dmerged")
    dya, dyb, dyc, dg1, dg2, dg3, db1, db2, db3 = _merge_bwd(sv["rest"], c["bg"], sv["ya"], sv["yb"], sv["yc"], dm, name=n + "dmerge")
    gs["b_gate"] = jnp.concatenate([db1, db2, db3], axis=1).reshape(3 * D)
    gw["ba"] = _mm(sv["pa"], dya, ta=True, shard_out=True, name=n + "dw_ba", **bf)
    gw["bb"] = _mm(sv["o"], dyb, ta=True, shard_out=True, name=n + "dw_bb", **bf)
    gw["bc"] = _mm(sv["sg"], dyc, ta=True, shard_out=True, name=n + "dw_bc", **bf)
    after, gw = grads_done(l, gw), {}
    dpa = _mm(dya, W["ba"], tb=True, name=n + "dpa")
    do = _mm(dyb, W["bb"], tb=True, after=after, name=n + "do", **bf)
    dsg = _mm(dyc, W["bc"], tb=True, name=n + "dsg")
    da, dwbd, dscale = _pool_bwd(sv["rest"], c["wbd"], c["wbd_t"], sp["pool_scale"][l], dpa, name=n + "dpool")
    gs["pool_w"] = jnp.stack([dwbd[64 * gi:64 * (gi + 1), 64 * gi:64 * (gi + 1)] for gi in range(4)])
    gs["pool_scale"] = dscale.reshape(POOL_W)
    dq, dk, dv, dfq, dfk = _fox_bwd(sv["qkv"], sv["cum"], sv["fk3"], sv["o"], do, sv["lse"], name=n + "dfox")
    dcum = dfq + jnp.pad(dfk.reshape(FOX_H, S).T, ((0, 0), (0, F_LANES - FOX_H)))
    df, dbf = _fox_post(sv["rest"], c["bpad"], dcum, name=n + "dfox_post")
    gs["b_forget"] = dbf[0, :FOX_H]
    dc, dwm, dbias, dgn = _sgu_bwd(sv["rest"], sp["sgu_norm_g"][l], c["wm"], c["wm_t"], c["sgu_bias"], dsg, name=n + "dsgu")
    gs["sgu_w"] = dwm * jnp.tril(jnp.ones((SGU_CHUNK, SGU_CHUNK), F32))[None]
    gs["sgu_b"] = dbias.reshape(SGU_CHUNK, 4, 64).sum(axis=2).T
    gs["sgu_norm_g"] = dgn.reshape(SGU_W)
    dqkv = [dq, dk, dv]
    drest = [jnp.concatenate([da, df, jnp.zeros((S, OFF_C - OFF_F - F_LANES), BF16), dc], axis=1), dg1, dg2, dg3]
    gw["qkv"] = _mm(sv["h"], dqkv, ta=True, name=n + "dw_qkv", **bf)
    gw["rest"] = _mm(sv["h"], drest, ta=True, name=n + "dw_rest", **bf)
    after = grads_done(l, gw)
    dh = _mm(dqkv, W["qkv"], tb=True, after=after, name=n + "dh_qkv")
    dh = _mm(drest, W["rest"], tb=True, extras=(dh,), epilogue=_add, tm=1024, name=n + "dh")
    dx, gs["norm_mix_g"] = _rms_bwd(sv["x"], sp["norm_mix_g"][l], dh, dx1, name=n + "dnorm_mix")
    return dx, gs


def _local_step(x, mem, target, sp, source, grads_done):
    saved, Ws = [], []
    for l in range(DEPTH):
        x, sv, W = _layer_fwd(l, x, mem, source, sp)
        saved.append(sv)
        Ws.append(W)
    loss, dx, dgf = _final_loss(x, sp["final_norm_g"], target, name="final_loss")
    gss = [None] * DEPTH
    for l in reversed(range(DEPTH)):
        dx, gss[l] = _layer_bwd(l, dx, saved[l], mem, Ws[l], sp, grads_done)
    small = {k: jnp.stack([gss[l][k] for l in range(DEPTH)]) for k in gss[0]}
    small["final_norm_g"] = dgf
    return loss, dx, small


_SMALL = ["norm_mix_g", "b_forget", "pool_w", "pool_scale", "sgu_norm_g", "sgu_w", "sgu_b", "b_gate", "norm_xattn_g",
          "norm_mem_g", "norm_ffn_g", "final_norm_g"]
_COL = {"w_branch_a": "ba", "w_branch_b": "bb", "w_branch_c": "bc", "w_xkv": "xkv", "w_ff1": "ff1"}
_ROW = {"w_out": "out", "w_xq": "xq", "w_xo": "xo", "w_ff2": "ff2"}
_BIG = ["w_in", "w_branch_a", "w_branch_b", "w_branch_c", "w_out", "w_xq", "w_xkv", "w_xo", "w_ff1", "w_ff2"]
_PACK_LANES = 128


def _as_rows(a):
    return a.reshape(-1, a.shape[-1])


def _pack(tensors):
    rows = []
    for a in tensors:
        flat = a.reshape(-1)
        flat = jnp.pad(flat, (0, (-flat.shape[0]) % (8 * _PACK_LANES)))
        rows.append(flat.reshape(-1, _PACK_LANES))
    n_rows = sum(r.shape[0] for r in rows)
    rows.append(jnp.zeros(((-n_rows) % (8 * N_DEV), _PACK_LANES), F32))
    return jnp.concatenate(rows, axis=0)


def _unpack(packed, like):
    out, r = [], 0
    for a in like:
        size = math.prod(a.shape)
        nr = 8 * (-(-size // (8 * _PACK_LANES)))
        out.append(packed[r:r + nr].reshape(-1)[:size].reshape(a.shape))
        r += nr
    return out


_SHARD_IN = N_IN // N_DEV
_SHARD_IN_PAD = -(-_SHARD_IN // 128) * 128


def _columns(pieces, start, stop):
    out, at = [], 0
    for p in pieces:
        lo, hi = max(start, at), min(stop, at + p.shape[1])
        if lo < hi:
            out.append(p[:, lo - at:hi - at])
        at += p.shape[1]
    return out


def _split_w_in(blocks):
    K = blocks[0].shape[0]
    pad = jnp.zeros((K, OFF_C - OFF_F - FOX_H), blocks[0].dtype)
    cols = functools.partial(_columns, blocks)
    rest = jnp.concatenate(cols(0, R_OFF_Q) + cols(R_OFF_F, R_OFF_C) + [pad] + cols(R_OFF_C, N_IN), axis=1)
    return jnp.concatenate(cols(R_OFF_Q, R_OFF_F), axis=1), rest


def _join_w_in(qkv, rest):
    in_order = [rest[:, :R_OFF_Q], qkv, rest[:, OFF_F:OFF_F + FOX_H], rest[:, OFF_C:]]
    pad = jnp.zeros((qkv.shape[0], _SHARD_IN_PAD - _SHARD_IN), qkv.dtype)
    return jnp.stack([jnp.concatenate(_columns(in_order, _SHARD_IN * d, _SHARD_IN * (d + 1)) + [pad], axis=1) for d in range(N_DEV)])


_FIRST = ["w_in"]
_LATER = [k for k in _BIG if k not in _FIRST]


def _layer_weights(gathered):
    W = {}
    if "w_in" in gathered:
        W.update(zip(("qkv", "rest"), _split_w_in([gathered["w_in"][d][:, :_SHARD_IN] for d in range(N_DEV)])))
    for name, key in _COL.items():
        if name in gathered:
            W[key] = _Gathered(gathered[name])
    for name, key in _ROW.items():
        if name in gathered:
            W[key] = gathered[name].reshape(-1, gathered[name].shape[-1])
    return W


def _grad_blocks(gw):
    parts = {}
    if "qkv" in gw:
        parts["w_in"] = _join_w_in(gw["qkv"], gw["rest"])
    for name, key in _COL.items():
        if key in gw:
            parts[name] = gw[key]
    for name, key in _ROW.items():
        if key in gw:
            parts[name] = gw[key].reshape(N_DEV, -1, gw[key].shape[-1])
    return parts


def kernel(x, mem, norm_mix_g, w_in, b_forget, pool_w, pool_scale, sgu_norm_g, sgu_w, sgu_b, w_branch_a, w_branch_b, w_branch_c, b_gate, w_out, norm_xattn_g, norm_mem_g, w_xq, w_xkv, w_xo, norm_ffn_g, w_ff1, w_ff2, final_norm_g, loss_target, m_norm_mix_g, m_w_in, m_b_forget, m_pool_w, m_pool_scale, m_sgu_norm_g, m_sgu_w, m_sgu_b, m_w_branch_a, m_w_branch_b, m_w_branch_c, m_b_gate, m_w_out, m_norm_xattn_g, m_norm_mem_g, m_w_xq, m_w_xkv, m_w_xo, m_norm_ffn_g, m_w_ff1, m_w_ff2, m_final_norm_g, v_norm_mix_g, v_w_in, v_b_forget, v_pool_w, v_pool_scale, v_sgu_norm_g, v_sgu_w, v_sgu_b, v_w_branch_a, v_w_branch_b, v_w_branch_c, v_b_gate, v_w_out, v_norm_xattn_g, v_norm_mem_g, v_w_xq, v_w_xkv, v_w_xo, v_norm_ffn_g, v_w_ff1, v_w_ff2, v_final_norm_g):
    names = ["norm_mix_g", "w_in", "b_forget", "pool_w", "pool_scale", "sgu_norm_g", "sgu_w", "sgu_b", "w_branch_a", "w_branch_b",
             "w_branch_c", "b_gate", "w_out", "norm_xattn_g", "norm_mem_g", "w_xq", "w_xkv", "w_xo", "norm_ffn_g", "w_ff1", "w_ff2",
             "final_norm_g"]
    w = dict(zip(names, [norm_mix_g, w_in, b_forget, pool_w, pool_scale, sgu_norm_g, sgu_w, sgu_b, w_branch_a, w_branch_b, w_branch_c,
                         b_gate, w_out, norm_xattn_g, norm_mem_g, w_xq, w_xkv, w_xo, norm_ffn_g, w_ff1, w_ff2, final_norm_g]))
    m = dict(zip(names, [m_norm_mix_g, m_w_in, m_b_forget, m_pool_w, m_pool_scale, m_sgu_norm_g, m_sgu_w, m_sgu_b, m_w_branch_a,
                         m_w_branch_b, m_w_branch_c, m_b_gate, m_w_out, m_norm_xattn_g, m_norm_mem_g, m_w_xq, m_w_xkv, m_w_xo,
                         m_norm_ffn_g, m_w_ff1, m_w_ff2, m_final_norm_g]))
    v = dict(zip(names, [v_norm_mix_g, v_w_in, v_b_forget, v_pool_w, v_pool_scale, v_sgu_norm_g, v_sgu_w, v_sgu_b, v_w_branch_a,
                         v_w_branch_b, v_w_branch_c, v_b_gate, v_w_out, v_norm_xattn_g, v_norm_mem_g, v_w_xq, v_w_xkv, v_w_xo,
                         v_norm_ffn_g, v_w_ff1, v_w_ff2, v_final_norm_g]))

    sp = {k: w[k] for k in _SMALL}
    shards = [{k: w[k][l].astype(BF16) for k in _BIG} for l in range(DEPTH)]
    for sh in shards:
        sh["w_in"] = jnp.pad(sh["w_in"], ((0, 0), (0, _SHARD_IN_PAD - _SHARD_IN)))
    me = _dev_index(*_position())

    def gather_out(l, keys, name, after=None):
        srcs = [shards[l][k] for k in keys]
        lands = [_own_block_placed(a, jax.ShapeDtypeStruct((N_DEV, *a.shape), a.dtype)) for a in srcs]
        state, token = _split_start(_plan_gather_out, srcs, lands, after=after, name=name + "_out_start")
        return (keys, name, state), token

    def gather_pass(job, value):
        keys, name, state = job
        lands = _split_wait(_plan_gather_out, state, value, name=name + "_out_wait")
        state, token = _split_start(_plan_gather_pass, [], lands, name=name + "_pass_start")
        return (keys, name, state), token, lands[0]

    def gather_end(job, value):
        keys, name, state = job
        return _layer_weights(dict(zip(keys, _split_wait(_plan_gather_pass, state, value, name=name + "_pass_wait"))))

    jobs = {}

    def source(l, point, value):
        if (l, point) == (0, "begin"):
            first = _all_gather([shards[0][k] for k in _FIRST], name="gather_l0_first")
            jobs["l0"], token = gather_out(0, _LATER, "gather_l0", after=first[0])
            return _layer_weights(dict(zip(_FIRST, first))), token
        if (l, point) == (0, "attended"):
            jobs["l0"], _, arrived = gather_pass(jobs["l0"], value)
            jobs["l1"], jobs["token"] = gather_out(1, _BIG, "gather_l1", after=arrived)
            return {}, None
        if (l, point) == (0, "mixed"):
            return gather_end(jobs.pop("l0"), value), jobs.pop("token")
        if (l, point) == (0, "expanded"):
            jobs["l1"], token, _ = gather_pass(jobs["l1"], value)
            return {}, token
        if (l, point) == (1, "begin"):
            return gather_end(jobs.pop("l1"), value), None
        return {}, None

    received = [{} for _ in range(DEPTH)]
    travelling = []

    def grads_done(l, gw):
        blocks = _grad_blocks(gw)
        keys = [k for k in _BIG if k in blocks]
        parts = [blocks[k] for k in keys]
        group = f"exchange_grads_l{l}_" + ("in" if "w_in" in blocks else "merge" if "w_out" in blocks else "mlp")
        lands = [_own_block_placed(lax.dynamic_index_in_dim(p, me, 0, keepdims=False), p) for p in parts]
        state, token = _split_start(_plan_exchange, parts, lands, name=group + "_start")
        travelling.append((l, keys, state, group + "_wait"))
        return token

    loss, dx, small = _local_step(x[0], mem[0], loss_target[0], sp, source, grads_done)
    grads, deltas, new_m, new_v = {}, {}, {}, {}
    like = [loss] + [w[k] for k in _SMALL]
    packed = _pack([loss] + [small[k] for k in _SMALL])
    eighths = packed.reshape(N_DEV, -1, _PACK_LANES)
    own = lambda a: _own_block_placed(lax.dynamic_index_in_dim(a, me, 0, keepdims=False) if a.ndim == 3 else a, eighths)
    scatter, done = _split_start(_plan_exchange, [eighths], [own(eighths)], after=dx, name="small_grads_scatter_start")

    def reduce_small(after):
        mine = _sum_blocks(_split_wait(_plan_exchange, scatter, after, name="small_grads_scatter_wait")[0], name="small_grads_sum")
        return _split_start(_plan_broadcast, [mine], [own(mine)], name="small_grads_gather_start")

    def update_small(state, after):
        total = _split_wait(_plan_broadcast, state, after, name="small_grads_gather_wait")[0].reshape(packed.shape)
        loss_sum, *g_small = _unpack(total, like)
        rows = lambda d: [_as_rows(d[k]) for k in _SMALL]
        outs = _adamw_small([_as_rows(g) for g in g_small], rows(w), rows(m), rows(v), name="adamw_small")
        grads.update(zip(_SMALL, g_small))
        for dst, vals in zip((deltas, new_m, new_v), outs):
            dst.update({k: a.reshape(w[k].shape) for k, a in zip(_SMALL, vals)})
        return loss_sum[0, 0], outs[0][0]

    groups = list(dict.fromkeys(tuple(keys) for _, keys, _, _ in travelling))
    for n_done, group_keys in enumerate(groups):
        if n_done == 1:
            gather, _ = reduce_small(done)
        if n_done == len(groups) - 1:
            loss, done = update_small(gather, done)
        for l, keys, state, wait_name in travelling:
            if tuple(keys) == group_keys:
                received[l].update(zip(keys, _split_wait(_plan_exchange, state, done, name=wait_name)))
        for k in group_keys:
            outs = _adamw_sharded([received[l][k] for l in range(DEPTH)], w[k], m[k], v[k], name="adamw_" + k)
            grads[k], deltas[k], new_m[k], new_v[k] = outs
        done = grads[group_keys[-1]]

    return (loss, dx[None], *[grads[k] for k in names], *[deltas[k] for k in names], *[new_m[k] for k in names],
            *[new_v[k] for k in names])
```

```python
import functools
import math

import jax
import jax.numpy as jnp
from jax import lax
from jax.experimental import pallas as pl
from jax.experimental.pallas import tpu as pltpu

F32 = jnp.float32
BF16 = jnp.bfloat16
MESH = pl.DeviceIdType.MESH

N_DEV = 8
D = 1024
DEPTH = 2
EPS = 1e-6
NEG = -1e30
POOL_W = 256
FOX_H = 8
FOX_DH = 64
FOX_W = 512
SGU_W = 256
SGU_CHUNK = 128
XH = 4
XDH = 256
N_IN = 5384
R_OFF_Q, R_OFF_F, R_OFF_C = 256, 1792, 1800
QKV_W = 3 * FOX_W
OFF_A, OFF_F, OFF_C, OFF_G, REST_W = 0, 256, 512, 1024, 4096
F_LANES = 128

ADAM_LR = 0.001
ADAM_B1 = 0.9
ADAM_B2 = 0.999
ADAM_EPS = 1e-08
ADAM_WD = 0.01
ADAM_STEP = 10

VMEM_LIMIT = 56 * 1024 * 1024


def _tile(n, pref):
    t = min(n, pref)
    while n % t:
        t -= 128
    assert t > 0, (n, pref)
    return t


def _params(sem=None):
    return pltpu.CompilerParams(dimension_semantics=sem, vmem_limit_bytes=VMEM_LIMIT)


def _dot(a, b, ca, cb):
    return lax.dot_general(a, b, (((ca,), (cb,)), ((), ())), preferred_element_type=F32)


def _sigmoid(z):
    return 1.0 / (1.0 + jnp.exp(-z))


_GELU_K = math.sqrt(2.0 / math.pi)
_GELU_C = 0.044715


def _gelu(x):
    return 0.5 * x * (1.0 + jnp.tanh(_GELU_K * (x + _GELU_C * x * x * x)))


def _gelu_grad(x):
    t = jnp.tanh(_GELU_K * (x + _GELU_C * x * x * x))
    return 0.5 * (1.0 + t) + 0.5 * x * (1.0 - t * t) * _GELU_K * (1.0 + 3.0 * _GELU_C * x * x)


def _rows(shape):
    return lax.broadcasted_iota(jnp.int32, shape, 0)


def _lanes(shape):
    return lax.broadcasted_iota(jnp.int32, shape, 1)


class _Gathered:
    def __init__(self, arr):
        self.arr = arr
        self.shape = (arr.shape[1], N_DEV * arr.shape[2])


_TOKEN = (8, 128)


def _mm(a, b, *, ta=False, tb=False, extras=(), epilogue=None, out_dtypes=(F32,), shard_out=False, after=None, tm=None, tn=512, tk=None,
        name):
    a_parts = list(a) if isinstance(a, (list, tuple)) else [a]
    b_parts = list(b) if isinstance(b, (list, tuple)) else [b]
    gathered = isinstance(b, _Gathered)
    assert (len(a_parts) == 1 or not ta) and (len(b_parts) == 1 or not tb) and min(len(a_parts), len(b_parts)) == 1
    a0, b0 = a_parts[0], b_parts[0]
    M, K = (a0.shape[1], a0.shape[0]) if ta else (a0.shape[0], a0.shape[1] * len(a_parts))
    N, Kb = b0.shape if tb else (b0.shape[1] * len(b_parts), b0.shape[0])
    assert Kb == K, (a0.shape, b0.shape, ta, tb)
    if gathered:
        if tb:
            tk = b.arr.shape[2]
        else:
            tn = b.arr.shape[2]
    if len(a_parts) > 1:
        tk = a0.shape[1]
    if shard_out:
        tn = N // N_DEV
    tm = _tile(M, tm or (1024 if ta else 2048))
    tn = _tile(b0.shape[1] if len(b_parts) > 1 else N, tn)
    per_piece = b0.shape[1] // tn
    size = lambda dt: jnp.dtype(dt).itemsize
    row_bytes = len(a_parts) * tm * size(a0.dtype) + len(b_parts) * tn * size(b.arr.dtype if gathered else b0.dtype)
    tile_bytes = tm * tn * (sum(size(e.dtype) for e in extras) + sum(map(size, out_dtypes)))

    def vmem_bytes(k_tile):
        return 2 * (k_tile * row_bytes + tile_bytes) + tm * tn * 4 * (K > k_tile)

    if tk is None:
        tk = next(c for c in (_tile(K, 2048), _tile(K, 1024), _tile(K, 512), _tile(K, 256)) if vmem_bytes(c) <= VMEM_LIMIT - (4 << 20))
    tk = _tile(K, tk)
    nk = K // tk
    ca, cb = (0 if ta else 1), (1 if tb else 0)
    n_a, n_b, n_ex, n_out = len(a_parts), len(b_parts), len(extras), len(out_dtypes)
    tokens = [] if after is None else [after]
    n_in = n_a + n_b + n_ex + len(tokens)
    if epilogue is None:
        epilogue = lambda acc: (acc,)

    def body(*refs):
        a_refs, b_refs = refs[:n_a], refs[n_a:n_a + n_b]
        ex_refs = refs[n_a + n_b:n_a + n_b + n_ex]
        o_refs = refs[n_in:n_in + n_out]
        j, k = pl.program_id(1), pl.program_id(2)

        def finish(acc):
            for o_ref, val in zip(o_refs, epilogue(acc, *[e[...] for e in ex_refs])):
                o_ref[...] = val.astype(o_ref.dtype)

        def step(a_ref, b_ref):
            part = _dot(a_ref[...].astype(BF16), b_ref[...].astype(BF16), ca, cb)
            if nk == 1:
                finish(part)
            else:
                acc_ref = refs[-1]

                @pl.when(k == 0)
                def _():
                    acc_ref[...] = part

                @pl.when(k > 0)
                def _():
                    acc_ref[...] += part

                @pl.when(k == nk - 1)
                def _():
                    finish(acc_ref[...])

        if n_a > 1:
            for p in range(n_a):
                pl.when(k == p)(functools.partial(step, a_refs[p], b_refs[0]))
        elif n_b > 1:
            for p in range(n_b):
                pl.when(j // per_piece == p)(functools.partial(step, a_refs[0], b_refs[p]))
        else:
            step(a_refs[0], b_refs[0])

    if n_a > 1:
        a_specs = [pl.BlockSpec((tm, tk), lambda i, j, k: (i, 0))] * n_a
    else:
        a_specs = [pl.BlockSpec((tk, tm), lambda i, j, k: (k, i)) if ta else pl.BlockSpec((tm, tk), lambda i, j, k: (i, k))]
    if gathered:
        b_arrs = [b.arr]
        b_specs = [pl.BlockSpec((None, tn, tk), lambda i, j, k: (k, j, 0)) if tb else pl.BlockSpec((None, tk, tn), lambda i, j, k: (j, k, 0))]
    elif n_b > 1:
        b_arrs = b_parts
        b_specs = [pl.BlockSpec((tk, tn), functools.partial(lambda p, i, j, k: (k, jnp.clip(j - p * per_piece, 0, per_piece - 1)), p))
                   for p in range(n_b)]
    else:
        b_arrs = b_parts
        b_specs = [pl.BlockSpec((tn, tk), lambda i, j, k: (j, k)) if tb else pl.BlockSpec((tk, tn), lambda i, j, k: (k, j))]
    tile = pl.BlockSpec((tm, tn), lambda i, j, k: (i, j))
    if shard_out:
        out_specs = [pl.BlockSpec((None, tm, tn), lambda i, j, k: (j, i, 0))] * n_out
        out_shape = [jax.ShapeDtypeStruct((N_DEV, M, tn), dt) for dt in out_dtypes]
    else:
        out_specs = [tile] * n_out
        out_shape = [jax.ShapeDtypeStruct((M, N), dt) for dt in out_dtypes]
    assert vmem_bytes(tk) <= VMEM_LIMIT - (4 << 20), (name, vmem_bytes(tk))
    outs = pl.pallas_call(
        body,
        name=name,
        grid=(M // tm, N // tn, nk),
        in_specs=a_specs + b_specs + [tile] * n_ex + [pl.BlockSpec(_TOKEN, lambda i, j, k: (0, 0))] * len(tokens),
        out_specs=out_specs,
        out_shape=out_shape,
        scratch_shapes=[pltpu.VMEM((tm, tn), F32)] if nk > 1 else [],
        compiler_params=_params(("parallel", "parallel", "arbitrary")),
    )(*a_parts, *b_arrs, *extras, *tokens)
    return outs[0] if n_out == 1 else outs


def _add(acc, res):
    return (acc + res,)


def _rms_fwd(x, g, *, after=None, name):
    R, C = x.shape
    tm = _tile(R, 256)
    tokens = [] if after is None else [after]

    def body(x_ref, g_ref, *rest):
        xv = x_ref[...]
        r = lax.rsqrt(jnp.mean(xv * xv, axis=-1, keepdims=True) + EPS)
        rest[-1][...] = (xv * r * g_ref[...]).astype(BF16)

    return pl.pallas_call(
        body,
        name=name,
        grid=(R // tm,),
        in_specs=[pl.BlockSpec((tm, C), lambda i: (i, 0)), pl.BlockSpec((1, C), lambda i: (0, 0))]
        + [pl.BlockSpec(_TOKEN, lambda i: (0, 0))] * len(tokens),
        out_specs=pl.BlockSpec((tm, C), lambda i: (i, 0)),
        out_shape=jax.ShapeDtypeStruct((R, C), BF16),
        compiler_params=_params(("parallel",)),
    )(x, g.reshape(1, C), *tokens)


def _rms_bwd(x, g, dh, dres, *, name):
    R, C = x.shape
    tm = _tile(R, 256)

    def body(x_ref, g_ref, dh_ref, dres_ref, dx_ref, dg_ref):
        xv = x_ref[...]
        r = lax.rsqrt(jnp.mean(xv * xv, axis=-1, keepdims=True) + EPS)
        xn = xv * r
        dh_v = dh_ref[...].astype(F32)
        dxn = dh_v * g_ref[...]
        dx_ref[...] = r * (dxn - xn * jnp.mean(dxn * xn, axis=-1, keepdims=True)) + dres_ref[...]
        part = jnp.sum(dh_v * xn, axis=0, keepdims=True)

        @pl.when(pl.program_id(0) == 0)
        def _():
            dg_ref[...] = part

        @pl.when(pl.program_id(0) > 0)
        def _():
            dg_ref[...] += part

    row = pl.BlockSpec((tm, C), lambda i: (i, 0))
    vec = pl.BlockSpec((1, C), lambda i: (0, 0))
    dx, dg = pl.pallas_call(
        body,
        name=name,
        grid=(R // tm,),
        in_specs=[row, vec, row, row],
        out_specs=[row, vec],
        out_shape=[jax.ShapeDtypeStruct((R, C), F32), jax.ShapeDtypeStruct((1, C), F32)],
        compiler_params=_params(("arbitrary",)),
    )(x, g.reshape(1, C), dh, dres)
    return dx, dg.reshape(C)


def _final_loss(x, g, target, *, name):
    R, C = x.shape
    tm = _tile(R, 256)

    def body(x_ref, g_ref, t_ref, loss_ref, dx_ref, dg_ref):
        xv = x_ref[...]
        r = lax.rsqrt(jnp.mean(xv * xv, axis=-1, keepdims=True) + EPS)
        xn = xv * r
        gv = g_ref[...]
        err = xn * gv - t_ref[...]
        lpart = (0.5 / C) * jnp.sum(jnp.sum(err * err, axis=1, keepdims=True), axis=0, keepdims=True)
        dy = err * (1.0 / C)
        dxn = dy * gv
        dx_ref[...] = r * (dxn - xn * jnp.mean(dxn * xn, axis=-1, keepdims=True))
        gpart = jnp.sum(dy * xn, axis=0, keepdims=True)

        @pl.when(pl.program_id(0) == 0)
        def _():
            loss_ref[...] = lpart
            dg_ref[...] = gpart

        @pl.when(pl.program_id(0) > 0)
        def _():
            loss_ref[...] += lpart
            dg_ref[...] += gpart

    row = pl.BlockSpec((tm, C), lambda i: (i, 0))
    vec = pl.BlockSpec((1, C), lambda i: (0, 0))
    loss, dx, dg = pl.pallas_call(
        body,
        name=name,
        grid=(R // tm,),
        in_specs=[row, vec, row],
        out_specs=[pl.BlockSpec((1, 1), lambda i: (0, 0)), row, vec],
        out_shape=[jax.ShapeDtypeStruct((1, 1), F32), jax.ShapeDtypeStruct((R, C), F32), jax.ShapeDtypeStruct((1, C), F32)],
        compiler_params=_params(("arbitrary",)),
    )(x, g.reshape(1, C), target)
    return loss, dx, dg.reshape(C)


def _pool_select(lane, vals):
    out = vals[3]
    for gi in (2, 1, 0):
        out = jnp.where(lane < 64 * (gi + 1), vals[gi], out)
    return out


def _pool_diff(a):
    row, lane = _rows(a.shape), _lanes(a.shape)

    def down(v, k):
        return jnp.where(row >= k, pltpu.roll(v, k, 0), 0.0)

    s2 = a + down(a, 1)
    s4 = s2 + down(s2, 2)
    s8 = s4 + down(s4, 4)
    s16 = s8 + down(s8, 8)
    wsum = _pool_select(lane, (s2, s4, s8, s16))
    win = _pool_select(lane, (2, 4, 8, 16))
    cnt = jnp.minimum(row + 1, win).astype(F32)
    return wsum / cnt - a, cnt


def _pool_diff_t(dd, cnt):
    S = dd.shape[0]
    row, lane = _rows(dd.shape), _lanes(dd.shape)

    def up(v, k):
        return jnp.where(row < S - k, pltpu.roll(v, S - k, 0), 0.0)

    e = dd / cnt
    s2 = e + up(e, 1)
    s4 = s2 + up(s2, 2)
    s8 = s4 + up(s4, 4)
    s16 = s8 + up(s8, 8)
    return _pool_select(lane, (s2, s4, s8, s16)) - dd


def _pool_fwd(rest, wbd, scale, *, name):
    S = rest.shape[0]

    def body(a_ref, w_ref, s_ref, o_ref):
        d, _ = _pool_diff(a_ref[...])
        yp = _dot(d.astype(BF16), w_ref[...], 1, 0)
        o_ref[...] = (yp * s_ref[...]).astype(BF16)

    return pl.pallas_call(
        body,
        name=name,
        grid=(1,),
        in_specs=[
            pl.BlockSpec((S, POOL_W), lambda i: (0, OFF_A // POOL_W)),
            pl.BlockSpec((POOL_W, POOL_W), lambda i: (0, 0)),
            pl.BlockSpec((1, POOL_W), lambda i: (0, 0)),
        ],
        out_specs=pl.BlockSpec((S, POOL_W), lambda i: (0, 0)),
        out_shape=jax.ShapeDtypeStruct((S, POOL_W), BF16),
        compiler_params=_params(("arbitrary",)),
    )(rest, wbd, scale.reshape(1, POOL_W))


def _pool_bwd(rest, wbd, wbd_t, scale, dpa, *, name):
    S = rest.shape[0]

    def body(a_ref, w_ref, wt_ref, s_ref, dpa_ref, da_ref, dw_ref, ds_ref):
        d, cnt = _pool_diff(a_ref[...])
        db = d.astype(BF16)
        yp = _dot(db, w_ref[...], 1, 0)
        dpa_v = dpa_ref[...]
        ds_ref[...] = jnp.sum(dpa_v * yp, axis=0, keepdims=True)
        dyp = (dpa_v * s_ref[...]).astype(BF16)
        dw_ref[...] = _dot(db, dyp, 0, 0)
        dd = _dot(dyp, wt_ref[...], 1, 0)
        da_ref[...] = _pool_diff_t(dd, cnt).astype(BF16)

    full = pl.BlockSpec((S, POOL_W), lambda i: (0, 0))
    sq = pl.BlockSpec((POOL_W, POOL_W), lambda i: (0, 0))
    vec = pl.BlockSpec((1, POOL_W), lambda i: (0, 0))
    return pl.pallas_call(
        body,
        name=name,
        grid=(1,),
        in_specs=[pl.BlockSpec((S, POOL_W), lambda i: (0, OFF_A // POOL_W)), sq, sq, vec, full],
        out_specs=[full, sq, vec],
        out_shape=[
            jax.ShapeDtypeStruct((S, POOL_W), BF16),
            jax.ShapeDtypeStruct((POOL_W, POOL_W), F32),
            jax.ShapeDtypeStruct((1, POOL_W), F32),
        ],
        compiler_params=_params(("arbitrary",)),
    )(rest, wbd, wbd_t, scale.reshape(1, POOL_W), dpa)


def _log_sigmoid(z):
    return jnp.minimum(z, 0.0) - jnp.log(1.0 + jnp.exp(-jnp.abs(z)))


_F_SPEC_COL = OFF_F // F_LANES


def _fox_prep(rest, bpad, *, name):
    S = rest.shape[0]

    def body(f_ref, b_ref, o_ref, ot_ref):
        acc = _log_sigmoid(f_ref[...] + b_ref[...])
        row = _rows(acc.shape)
        k = 1
        while k < S:
            acc = acc + jnp.where(row >= k, pltpu.roll(acc, k, 0), 0.0)
            k *= 2
        o_ref[...] = acc
        ot_ref[...] = acc.T

    return pl.pallas_call(
        body,
        name=name,
        grid=(1,),
        in_specs=[pl.BlockSpec((S, F_LANES), lambda i: (0, _F_SPEC_COL)), pl.BlockSpec((1, F_LANES), lambda i: (0, 0))],
        out_specs=[pl.BlockSpec((S, F_LANES), lambda i: (0, 0)), pl.BlockSpec((F_LANES, S), lambda i: (0, 0))],
        out_shape=[jax.ShapeDtypeStruct((S, F_LANES), F32), jax.ShapeDtypeStruct((F_LANES, S), F32)],
        compiler_params=_params(("arbitrary",)),
    )(rest, bpad)


def _fox_post(rest, bpad, dcum, *, name):
    S = rest.shape[0]

    def body(f_ref, b_ref, d_ref, df_ref, db_ref):
        acc = d_ref[...]
        row = _rows(acc.shape)
        k = 1
        while k < S:
            acc = acc + jnp.where(row < S - k, pltpu.roll(acc, S - k, 0), 0.0)
            k *= 2
        df = acc * (1.0 - _sigmoid(f_ref[...] + b_ref[...]))
        df_ref[...] = df.astype(BF16)
        db_ref[...] = jnp.sum(df, axis=0, keepdims=True)

    full = pl.BlockSpec((S, F_LANES), lambda i: (0, 0))
    vec = pl.BlockSpec((1, F_LANES), lambda i: (0, 0))
    return pl.pallas_call(
        body,
        name=name,
        grid=(1,),
        in_specs=[pl.BlockSpec((S, F_LANES), lambda i: (0, _F_SPEC_COL)), vec, full],
        out_specs=[full, vec],
        out_shape=[jax.ShapeDtypeStruct((S, F_LANES), BF16), jax.ShapeDtypeStruct((1, F_LANES), F32)],
        compiler_params=_params(("arbitrary",)),
    )(rest, bpad, dcum)


_FOX_SCALE = FOX_DH ** -0.5
_PAIRS = FOX_H // 2


def _scaled(v):
    return (v.astype(F32) * _FOX_SCALE).astype(BF16)


def _diag_mask(s):
    return jnp.where(_rows(s.shape) >= _lanes(s.shape), s, NEG)


def _fox_fwd(qkv, cum, fk3, *, name):
    S = qkv.shape[0]
    nk, t = fk3.shape[1:]

    def body(q_ref, k_ref, v_ref, cum_ref, fk_ref, o_ref, lse_ref):
        i = pl.program_id(0)
        lane = _lanes((t, 128))
        lo = lane < FOX_DH
        cumv = cum_ref[...]
        qm, fq = [], []
        for h in range(FOX_H):
            qs = _scaled(q_ref[:, 128 * (h // 2):128 * (h // 2 + 1)])
            zero = jnp.zeros_like(qs)
            qm.append(jnp.where(lo, qs, zero) if h % 2 == 0 else jnp.where(lo, zero, qs))
            fq.append(cumv[:, h:h + 1])

        def tile(j, state, masked):
            m, acc, lsum = (list(part) for part in state)
            k0 = pl.multiple_of(j * t, t)
            for hp in range(_PAIRS):
                cols = slice(128 * hp, 128 * (hp + 1))
                kb = k_ref[pl.ds(k0, t), cols]
                vb = v_ref[pl.ds(k0, t), cols]
                one = jnp.ones_like(vb)
                alphas, pvs = [], []
                for h in (2 * hp, 2 * hp + 1):
                    s = _dot(qm[h], kb, 1, 1) + fq[h] - fk_ref[h, pl.ds(j, 1), :]
                    if masked:
                        s = _diag_mask(s)
                    m_new = jnp.maximum(m[h], jnp.max(s, axis=-1, keepdims=True))
                    p = jnp.exp(s - m_new)
                    alphas.append(jnp.exp(m[h] - m_new))
                    m[h] = m_new
                    pvs.append(_dot(p.astype(BF16), jnp.where(lo, vb, one) if h % 2 == 0 else jnp.where(lo, one, vb), 1, 0))
                acc[hp] = jnp.where(lo, alphas[0], alphas[1]) * acc[hp] + jnp.where(lo, pvs[0], pvs[1])
                lsum[hp] = jnp.where(lo, alphas[1], alphas[0]) * lsum[hp] + jnp.where(lo, pvs[1], pvs[0])
            return tuple(m), tuple(acc), tuple(lsum)

        zeros = (jnp.zeros((t, 128), F32),) * _PAIRS
        state = lax.fori_loop(0, i, functools.partial(tile, masked=False), ((jnp.full((t, 1), NEG, F32),) * FOX_H, zeros, zeros))
        m, acc, lsum = tile(i, state, True)
        for hp in range(_PAIRS):
            o_ref[:, 128 * hp:128 * (hp + 1)] = acc[hp] / pltpu.roll(lsum[hp], FOX_DH, 1)
            lse = [m[2 * hp] + jnp.log(lsum[hp][:, FOX_DH:FOX_DH + 1]), m[2 * hp + 1] + jnp.log(lsum[hp][:, 0:1])]
            lse_ref[hp] = jnp.where(lane == 0, lse[0], jnp.where(lane == 1, lse[1], 0.0))

    whole = lambda col: pl.BlockSpec((S, FOX_W), lambda i: (0, col))
    return pl.pallas_call(
        body,
        name=name,
        grid=(S // t,),
        in_specs=[
            pl.BlockSpec((t, FOX_W), lambda i: (i, 0)), whole(1), whole(2),
            pl.BlockSpec((t, F_LANES), lambda i: (i, 0)),
            pl.BlockSpec((FOX_H, nk, t), lambda i: (0, 0, 0)),
        ],
        out_specs=[pl.BlockSpec((t, FOX_W), lambda i: (i, 0)), pl.BlockSpec((_PAIRS, t, 128), lambda i: (0, i, 0))],
        out_shape=[jax.ShapeDtypeStruct((S, FOX_W), F32), jax.ShapeDtypeStruct((_PAIRS, S, 128), F32)],
        compiler_params=_params(("arbitrary",)),
    )(qkv, qkv, qkv, cum, fk3)


def _fox_bwd(qkv, cum, fk3, o, do, lse, *, name):
    S = qkv.shape[0]
    nk, t = fk3.shape[1:]
    q_at, k_at, v_at = 0, FOX_W, 2 * FOX_W

    def body(qkv_ref, cum_ref, fk_ref, o_ref, do_ref, lse_ref, dq_ref, dk_ref, dv_ref, dfq_ref, dfk_ref,
             qs_sc, ks_sc, delta_sc, dq_sc):
        lane = _lanes((t, 128))
        lo = lane < FOX_DH
        mine = lambda h: lo if h % 2 == 0 else jnp.logical_not(lo)

        def by_head(tile, values):
            for h, val in enumerate(values):
                tile = jnp.where(lane == h, val, tile)
            return tile

        def prep(i, carry):
            r = pl.ds(pl.multiple_of(i * t, t), t)
            qs_sc[r, :] = _scaled(qkv_ref[r, q_at:q_at + FOX_W])
            ks_sc[r, :] = _scaled(qkv_ref[r, k_at:k_at + FOX_W])
            sums = []
            for hp in range(_PAIRS):
                cols = slice(128 * hp, 128 * (hp + 1))
                prod = do_ref[r, cols].astype(F32) * o_ref[r, cols]
                sums += [jnp.sum(jnp.where(mine(h), prod, 0.0), axis=-1, keepdims=True) for h in (2 * hp, 2 * hp + 1)]
            delta_sc[r, :] = by_head(jnp.zeros((t, 128), F32), sums)
            dfq_ref[r, :] = jnp.zeros((t, 128), F32)
            dq_sc[r, :] = jnp.zeros((t, FOX_W), F32)
            return carry

        lax.fori_loop(0, nk, prep, 0)

        def kv_tile(j, carry):
            kr = pl.ds(pl.multiple_of(j * t, t), t)

            def q_tile(i, acc, masked):
                dk, dv, dfk = list(acc[:_PAIRS]), list(acc[_PAIRS:2 * _PAIRS]), list(acc[2 * _PAIRS:])
                qr = pl.ds(pl.multiple_of(i * t, t), t)
                delta_t, cum_t, dq_old, dfq_old = delta_sc[qr, :], cum_ref[qr, :], dq_sc[qr, :], dfq_ref[qr, :]
                row_sums, dq_new = [], []
                for hp in range(_PAIRS):
                    cols = slice(128 * hp, 128 * (hp + 1))
                    kb = qkv_ref[kr, k_at + 128 * hp:k_at + 128 * (hp + 1)]
                    vb = qkv_ref[kr, v_at + 128 * hp:v_at + 128 * (hp + 1)]
                    ksb, qsb, dob = ks_sc[kr, cols], qs_sc[qr, cols], do_ref[qr, cols]
                    zero = jnp.zeros_like(qsb)
                    dq_t = jnp.zeros((t, 128), F32)
                    for h in (2 * hp, 2 * hp + 1):
                        qe, doe, ke = (jnp.where(mine(h), a, zero) for a in (qsb, dob, ksb))
                        s = _dot(qe, kb, 1, 1) + cum_t[:, h:h + 1] - fk_ref[h, pl.ds(j, 1), :]
                        if masked:
                            s = _diag_mask(s)
                        p = jnp.exp(s - lse_ref[hp, qr, h % 2:h % 2 + 1])
                        dv[hp] = dv[hp] + _dot(p.astype(BF16), doe, 0, 0)
                        dp = _dot(doe, vb, 1, 1)
                        ds = p * (dp - delta_t[:, h:h + 1])
                        dsb = ds.astype(BF16)
                        dk[hp] = dk[hp] + _dot(dsb, qe, 0, 0)
                        dq_t = dq_t + _dot(dsb, ke, 1, 0)
                        row_sums.append(jnp.sum(ds, axis=-1, keepdims=True))
                        dfk[h] = dfk[h] - jnp.sum(ds, axis=0, keepdims=True)
                    dq_new.append(dq_old[:, cols] + dq_t)
                for hp in range(_PAIRS):
                    dq_sc[qr, 128 * hp:128 * (hp + 1)] = dq_new[hp]
                dfq_ref[qr, :] = dfq_old + by_head(jnp.zeros((t, 128), F32), row_sums)
                return (*dk, *dv, *dfk)

            init = tuple([jnp.zeros((t, 128), F32)] * (2 * _PAIRS) + [jnp.zeros((1, t), F32)] * FOX_H)
            acc = q_tile(j, init, True)
            acc = lax.fori_loop(j + 1, nk, functools.partial(q_tile, masked=False), acc)
            for hp in range(_PAIRS):
                cols = slice(128 * hp, 128 * (hp + 1))
                dk_ref[kr, cols] = acc[hp].astype(BF16)
                dv_ref[kr, cols] = acc[_PAIRS + hp].astype(BF16)
            for h in range(FOX_H):
                dfk_ref[h, pl.ds(j, 1), :] = acc[2 * _PAIRS + h]
            return carry

        lax.fori_loop(0, nk, kv_tile, 0)
        dq_ref[...] = dq_sc[...].astype(BF16)

    vm = pl.BlockSpec(memory_space=pltpu.VMEM)
    big = jax.ShapeDtypeStruct((S, FOX_W), BF16)
    return pl.pallas_call(
        body,
        name=name,
        in_specs=[vm] * 6,
        out_specs=[vm] * 5,
        out_shape=[big, big, big, jax.ShapeDtypeStruct((S, 128), F32), jax.ShapeDtypeStruct((FOX_H, nk, t), F32)],
        scratch_shapes=[pltpu.VMEM((S, FOX_W), BF16), pltpu.VMEM((S, FOX_W), BF16), pltpu.VMEM((S, 128), F32),
                        pltpu.VMEM((S, FOX_W), F32)],
        compiler_params=pltpu.CompilerParams(vmem_limit_bytes=VMEM_LIMIT),
    )(qkv, cum, fk3, o, do, lse)


def _group_mask(lane, gi):
    return (lane >= 64 * gi) & (lane < 64 * (gi + 1))


_U_COL = OFF_C // SGU_W


def _sgu_fwd(rest, gn, wm, bias, *, name):
    S = rest.shape[0]
    ts = _tile(S, 512)
    nc = ts // SGU_CHUNK

    def body(u_ref, v_ref, g_ref, w_ref, b_ref, o_ref):
        zv = _gelu(v_ref[...])
        vn = zv * lax.rsqrt(jnp.mean(zv * zv, axis=-1, keepdims=True) + EPS) * g_ref[...]
        lane = _lanes((SGU_CHUNK, SGU_W))
        for c in range(nc):
            rows = slice(c * SGU_CHUNK, (c + 1) * SGU_CHUNK)
            vcb = vn[rows].astype(BF16)
            mixed = b_ref[...]
            for gi in range(4):
                mixed = mixed + jnp.where(_group_mask(lane, gi), _dot(w_ref[gi], vcb, 1, 0), 0.0)
            o_ref[rows, :] = (_gelu(u_ref[rows, :]) * mixed).astype(BF16)

    return pl.pallas_call(
        body,
        name=name,
        grid=(S // ts,),
        in_specs=[
            pl.BlockSpec((ts, SGU_W), lambda i: (i, _U_COL)),
            pl.BlockSpec((ts, SGU_W), lambda i: (i, _U_COL + 1)),
            pl.BlockSpec((1, SGU_W), lambda i: (0, 0)),
            pl.BlockSpec((4, SGU_CHUNK, SGU_CHUNK), lambda i: (0, 0, 0)),
            pl.BlockSpec((SGU_CHUNK, SGU_W), lambda i: (0, 0)),
        ],
        out_specs=pl.BlockSpec((ts, SGU_W), lambda i: (i, 0)),
        out_shape=jax.ShapeDtypeStruct((S, SGU_W), BF16),
        compiler_params=_params(("parallel",)),
    )(rest, rest, gn.reshape(1, SGU_W), wm, bias)


def _sgu_bwd(rest, gn, wm, wm_t, bias, dsg, *, name):
    S = rest.shape[0]
    ts = _tile(S, 512)
    nc = ts // SGU_CHUNK

    def body(u_ref, v_ref, g_ref, w_ref, wt_ref, b_ref, dsg_ref, dc_ref, dw_ref, db_ref, dg_ref):
        first = pl.program_id(0) == 0

        @pl.when(first)
        def _():
            dw_ref[...] = jnp.zeros_like(dw_ref)
            db_ref[...] = jnp.zeros_like(db_ref)
            dg_ref[...] = jnp.zeros_like(dg_ref)

        gv = g_ref[...]
        lane = _lanes((SGU_CHUNK, SGU_W))
        for c in range(nc):
            rows = slice(c * SGU_CHUNK, (c + 1) * SGU_CHUNK)
            vpre = v_ref[rows, :]
            upre = u_ref[rows, :]
            zv = _gelu(vpre)
            r = lax.rsqrt(jnp.mean(zv * zv, axis=-1, keepdims=True) + EPS)
            zn = zv * r
            vcb = (zn * gv).astype(BF16)
            mixed = b_ref[...]
            for gi in range(4):
                mixed = mixed + jnp.where(_group_mask(lane, gi), _dot(w_ref[gi], vcb, 1, 0), 0.0)
            zu = _gelu(upre)
            dsg_v = dsg_ref[rows, :]
            dc_ref[rows, :SGU_W] = (dsg_v * mixed * _gelu_grad(upre)).astype(BF16)
            dmixed = dsg_v * zu
            db_ref[...] += dmixed
            dvn = jnp.zeros((SGU_CHUNK, SGU_W), F32)
            for gi in range(4):
                dmg = jnp.where(_group_mask(lane, gi), dmixed, 0.0).astype(BF16)
                dw_ref[gi] += _dot(dmg, vcb, 1, 1)
                dvn = dvn + _dot(wt_ref[gi], dmg, 1, 0)
            dg_ref[...] += jnp.sum(dvn * zn, axis=0, keepdims=True)
            dzn = dvn * gv
            dzv = r * (dzn - zn * jnp.mean(dzn * zn, axis=-1, keepdims=True))
            dc_ref[rows, SGU_W:] = (dzv * _gelu_grad(vpre)).astype(BF16)

    blk = pl.BlockSpec((ts, SGU_W), lambda i: (i, 0))
    vec = pl.BlockSpec((1, SGU_W), lambda i: (0, 0))
    w3 = pl.BlockSpec((4, SGU_CHUNK, SGU_CHUNK), lambda i: (0, 0, 0))
    bsp = pl.BlockSpec((SGU_CHUNK, SGU_W), lambda i: (0, 0))
    return pl.pallas_call(
        body,
        name=name,
        grid=(S // ts,),
        in_specs=[
            pl.BlockSpec((ts, SGU_W), lambda i: (i, _U_COL)),
            pl.BlockSpec((ts, SGU_W), lambda i: (i, _U_COL + 1)),
            vec, w3, w3, bsp, blk,
        ],
        out_specs=[pl.BlockSpec((ts, 2 * SGU_W), lambda i: (i, 0)), w3, bsp, vec],
        out_shape=[
            jax.ShapeDtypeStruct((S, 2 * SGU_W), BF16),
            jax.ShapeDtypeStruct((4, SGU_CHUNK, SGU_CHUNK), F32),
            jax.ShapeDtypeStruct((SGU_CHUNK, SGU_W), F32),
            jax.ShapeDtypeStruct((1, SGU_W), F32),
        ],
        compiler_params=_params(("arbitrary",)),
    )(rest, rest, gn.reshape(1, SGU_W), wm, wm_t, bias, dsg)


_GT = 512
_G0 = OFF_G // _GT


def _gate_specs(tm, col_of):
    specs = [pl.BlockSpec((tm, _GT), functools.partial(lambda k, *ids: (col_of(*ids)[0], _G0 + 2 * k + col_of(*ids)[1]), k)) for k in range(3)]
    specs += [pl.BlockSpec((1, _GT), functools.partial(lambda k, *ids: (0, 2 * k + col_of(*ids)[1]), k)) for k in range(3)]
    return specs


def _merge_fwd(rest, bg, ya, yb, yc, *, name):
    S = rest.shape[0]
    tm = _tile(S, 512)

    def body(g1, g2, g3, b1, b2, b3, ya_ref, yb_ref, yc_ref, o_ref):
        acc = _sigmoid(g1[...] + b1[...]) * ya_ref[...]
        acc = acc + _sigmoid(g2[...] + b2[...]) * yb_ref[...]
        acc = acc + _sigmoid(g3[...] + b3[...]) * yc_ref[...]
        o_ref[...] = acc.astype(BF16)

    blk = pl.BlockSpec((tm, _GT), lambda i, j: (i, j))
    return pl.pallas_call(
        body,
        name=name,
        grid=(S // tm, D // _GT),
        in_specs=_gate_specs(tm, lambda i, j: (i, j)) + [blk, blk, blk],
        out_specs=blk,
        out_shape=jax.ShapeDtypeStruct((S, D), BF16),
        compiler_params=_params(("parallel", "parallel")),
    )(rest, rest, rest, bg, bg, bg, ya, yb, yc)


def _merge_bwd(rest, bg, ya, yb, yc, dm, *, name):
    S = rest.shape[0]
    tm = _tile(S, 512)

    def body(g1, g2, g3, b1, b2, b3, ya_ref, yb_ref, yc_ref, dm_ref, dya, dyb, dyc, dg1, dg2, dg3, db1, db2, db3):
        first = pl.program_id(1) == 0
        dmv = dm_ref[...]
        for g_ref, b_ref, y_ref, dy_ref, dg_ref, db_ref in (
            (g1, b1, ya_ref, dya, dg1, db1), (g2, b2, yb_ref, dyb, dg2, db2), (g3, b3, yc_ref, dyc, dg3, db3)):
            gate = _sigmoid(g_ref[...] + b_ref[...])
            dy_ref[...] = (dmv * gate).astype(BF16)
            dpre = dmv * y_ref[...] * gate * (1.0 - gate)
            dg_ref[...] = dpre.astype(BF16)
            part = jnp.sum(dpre, axis=0, keepdims=True)

            @pl.when(first)
            def _():
                db_ref[...] = part

            @pl.when(jnp.logical_not(first))
            def _():
                db_ref[...] += part

    blk = pl.BlockSpec((tm, _GT), lambda j, i: (i, j))
    vec = pl.BlockSpec((1, _GT), lambda j, i: (0, j))
    big = jax.ShapeDtypeStruct((S, D), BF16)
    small = jax.ShapeDtypeStruct((1, D), F32)
    return pl.pallas_call(
        body,
        name=name,
        grid=(D // _GT, S // tm),
        in_specs=_gate_specs(tm, lambda j, i: (i, j)) + [blk, blk, blk, blk],
        out_specs=[blk] * 6 + [vec] * 3,
        out_shape=[big] * 6 + [small] * 3,
        compiler_params=_params(("parallel", "arbitrary")),
    )(rest, rest, rest, bg, bg, bg, ya, yb, yc, dm)


_X_SCALE = XDH ** -0.5


def _xattn_fwd(xq, kv, *, name):
    S = xq.shape[0]
    M = kv.shape[0]
    tq = _tile(S, 512)

    def body(q_ref, k_ref, v_ref, o_ref):
        s = _dot(q_ref[...], k_ref[...], 1, 1) * _X_SCALE
        e = jnp.exp(s - jnp.max(s, axis=-1, keepdims=True))
        p = e / jnp.sum(e, axis=-1, keepdims=True)
        o_ref[...] = _dot(p.astype(BF16), v_ref[...], 1, 0).astype(BF16)

    return pl.pallas_call(
        body,
        name=name,
        grid=(S // tq, XH),
        in_specs=[
            pl.BlockSpec((tq, XDH), lambda i, h: (i, h)),
            pl.BlockSpec((M, XDH), lambda i, h: (0, h)),
            pl.BlockSpec((M, XDH), lambda i, h: (0, XH + h)),
        ],
        out_specs=pl.BlockSpec((tq, XDH), lambda i, h: (i, h)),
        out_shape=jax.ShapeDtypeStruct((S, D), BF16),
        compiler_params=_params(("parallel", "parallel")),
    )(xq, kv, kv)


def _xattn_bwd(xq, kv, do, *, name):
    S = xq.shape[0]
    M = kv.shape[0]
    tq = _tile(S, 512)

    def body(q_ref, k_ref, v_ref, do_ref, dq_ref, dk_ref, dv_ref):
        qb = q_ref[...]
        kb = k_ref[...]
        dob = do_ref[...]
        s = _dot(qb, kb, 1, 1) * _X_SCALE
        e = jnp.exp(s - jnp.max(s, axis=-1, keepdims=True))
        p = e / jnp.sum(e, axis=-1, keepdims=True)
        dp = _dot(dob, v_ref[...], 1, 1)
        ds = (p * (dp - jnp.sum(p * dp, axis=-1, keepdims=True)) * _X_SCALE).astype(BF16)
        dq_ref[...] = _dot(ds, kb, 1, 0).astype(BF16)
        dk_part = _dot(ds, qb, 0, 0)
        dv_part = _dot(p.astype(BF16), dob, 0, 0)

        @pl.when(pl.program_id(1) == 0)
        def _():
            dk_ref[...] = dk_part
            dv_ref[...] = dv_part

        @pl.when(pl.program_id(1) > 0)
        def _():
            dk_ref[...] += dk_part
            dv_ref[...] += dv_part

    qspec = pl.BlockSpec((tq, XDH), lambda h, i: (i, h))
    kspec = pl.BlockSpec((M, XDH), lambda h, i: (0, h))
    dxq, dxk, dxv = pl.pallas_call(
        body,
        name=name,
        grid=(XH, S // tq),
        in_specs=[qspec, kspec, pl.BlockSpec((M, XDH), lambda h, i: (0, XH + h)), qspec],
        out_specs=[qspec, kspec, kspec],
        out_shape=[jax.ShapeDtypeStruct((S, D), BF16), jax.ShapeDtypeStruct((M, D), F32), jax.ShapeDtypeStruct((M, D), F32)],
        compiler_params=_params(("parallel", "arbitrary")),
    )(xq, kv, kv, do)
    return dxq, jnp.concatenate([dxk, dxv], axis=1)


def _adam_math(w, g, m, v):
    m = ADAM_B1 * m + (1.0 - ADAM_B1) * g
    v = ADAM_B2 * v + (1.0 - ADAM_B2) * (g * g)
    m_hat = m / (1.0 - ADAM_B1 ** ADAM_STEP)
    v_hat = v / (1.0 - ADAM_B2 ** ADAM_STEP)
    delta = -ADAM_LR * (m_hat / (jnp.sqrt(v_hat) + ADAM_EPS) + ADAM_WD * w)
    return delta, m, v


def _adamw_sharded(parts, w, m, v, *, name):
    _, R, C = w.shape
    Cp = parts[0].shape[2]
    tm = _tile(R, 256)
    nr = R // tm

    def body(p0_ref, p1_ref, w_ref, m_ref, v_ref, g_ref, d_ref, mo_ref, vo_ref):
        def update(p_ref):
            g = p_ref[0][:, :C].astype(F32)
            for dev in range(1, N_DEV):
                g = g + p_ref[dev][:, :C].astype(F32)
            delta, mn, vn = _adam_math(w_ref[...], g, m_ref[...], v_ref[...])
            g_ref[...] = g
            d_ref[...] = delta
            mo_ref[...] = mn
            vo_ref[...] = vn

        @pl.when(pl.program_id(0) == 0)
        def _():
            update(p0_ref)

        @pl.when(pl.program_id(0) == 1)
        def _():
            update(p1_ref)

    p0 = pl.BlockSpec((N_DEV, tm, Cp), lambda l, i: (0, i * (1 - l) + (nr - 1) * l, 0))
    p1 = pl.BlockSpec((N_DEV, tm, Cp), lambda l, i: (0, i * l, 0))
    blk = pl.BlockSpec((None, tm, C), lambda l, i: (l, i, 0))
    sds = jax.ShapeDtypeStruct(w.shape, F32)
    return pl.pallas_call(
        body,
        name=name,
        grid=(DEPTH, nr),
        in_specs=[p0, p1, blk, blk, blk],
        out_specs=[blk] * 4,
        out_shape=[sds] * 4,
        compiler_params=_params(("arbitrary", "arbitrary")),
    )(parts[0], parts[1], w, m, v)


def _adamw_small(g, w, m, v, *, name):
    n = len(g)

    def body(*refs):
        g_refs, w_refs, m_refs, v_refs = (refs[k * n:(k + 1) * n] for k in range(4))
        d_out, m_out, v_out = (refs[(4 + k) * n:(5 + k) * n] for k in range(3))
        for t in range(n):
            delta, mn, vn = _adam_math(w_refs[t][...], g_refs[t][...], m_refs[t][...], v_refs[t][...])
            d_out[t][...] = delta
            m_out[t][...] = mn
            v_out[t][...] = vn

    vm = pl.BlockSpec(memory_space=pltpu.VMEM)
    shapes = [jax.ShapeDtypeStruct(a.shape, F32) for a in w]
    outs = pl.pallas_call(
        body,
        name=name,
        in_specs=[vm] * (4 * n),
        out_specs=[vm] * (3 * n),
        out_shape=shapes * 3,
        compiler_params=pltpu.CompilerParams(vmem_limit_bytes=VMEM_LIMIT),
    )(*g, *w, *m, *v)
    return outs[:n], outs[n:2 * n], outs[2 * n:]


def _position():
    return lax.axis_index("x"), lax.axis_index("y"), lax.axis_index("c")


def _dev_index(px, py, pc):
    return 4 * px + 2 * py + pc


_ANY = pl.BlockSpec(memory_space=pl.ANY)


def _all_gather(shards, *, name):
    n = len(shards)
    out_shape = [jax.ShapeDtypeStruct((N_DEV, *s.shape), s.dtype) for s in shards]
    n_pieces = len(_pieces(out_shape))

    def body(*refs):
        ins, outs = refs[:n], refs[n:2 * n]
        send_sems, recv_sems, local_sems = refs[2 * n:]
        x, y, c = _position()
        me, sibling = (x, y, c), (x, y, 1 - c)
        chips = [(1 - x, y), (x, 1 - y), (1 - x, 1 - y)]
        pieces = _pieces(outs)

        def copy(i, k, block, to, from_input=False):
            t, rows = pieces[i]
            dst = _cut(outs[t].at[_dev_index(*block)], rows)
            return pltpu.make_async_remote_copy(
                src_ref=_cut(ins[t], rows) if from_input else dst, dst_ref=dst, send_sem=send_sems.at[i, k],
                recv_sem=recv_sems.at[i, k], device_id=to, device_id_type=MESH)

        mine = [pltpu.make_async_copy(_cut(ins[t], rows), _cut(outs[t].at[_dev_index(*me)], rows), local_sems.at[i])
                for i, (t, rows) in enumerate(pieces)]
        for cp in mine:
            cp.start()
        started = []
        for j, chip in enumerate(chips):
            for i in range(n_pieces):
                started.append(copy(i, 1 + j, me, (*chip, c), from_input=True))
                started[-1].start()
        for i in range(n_pieces):
            started.append(copy(i, 0, me, sibling, from_input=True))
            started[-1].start()
        for j, chip in enumerate(chips):
            for i in range(n_pieces):
                copy(i, 1 + j, (*chip, c), me).wait_recv()
                started.append(copy(i, 4 + j, (*chip, c), sibling))
                started[-1].start()
        for i in range(n_pieces):
            copy(i, 0, sibling, me).wait_recv()
        for j, chip in enumerate(chips):
            for i in range(n_pieces):
                copy(i, 4 + j, (*chip, 1 - c), me).wait_recv()
        for cp in started:
            cp.wait_send()
        for cp in mine:
            cp.wait()

    return pl.pallas_call(
        body,
        name=name,
        in_specs=[_ANY] * n,
        out_specs=[_ANY] * n,
        out_shape=out_shape,
        scratch_shapes=[pltpu.SemaphoreType.DMA((n_pieces, 7)), pltpu.SemaphoreType.DMA((n_pieces, 7)),
                        pltpu.SemaphoreType.DMA((n_pieces,))],
        compiler_params=pltpu.CompilerParams(has_side_effects=True),
    )(*shards)


def _peers(x, y, c):
    out = []
    for mask in range(1, N_DEV):
        fx, fy, fc = (mask >> 2) & 1, (mask >> 1) & 1, mask & 1
        out.append((1 - x if fx else x, 1 - y if fy else y, 1 - c if fc else c))
    return out


_HBM = pl.BlockSpec(memory_space=pltpu.HBM)
_SEM = pl.BlockSpec(memory_space=pltpu.SEMAPHORE)


def _own_block_placed(block, like):
    x, y, c = _position()
    return lax.dynamic_update_index_in_dim(lax.empty(like.shape, like.dtype), block, _dev_index(x, y, c), 0)


_COPY_BYTES = 256 << 10
_MAX_PIECES = 8


def _pieces(blocks):
    out = []
    for t, b in enumerate(blocks):
        R, C = b.shape[-2:]
        n = max(1, min(_MAX_PIECES, R * C * jnp.dtype(b.dtype).itemsize // _COPY_BYTES))
        while n > 1 and R % (16 * n):
            n -= 1
        out += [(t, pl.ds(j * (R // n), R // n) if n > 1 else None) for j in range(n)]
    return out


def _cut(block, rows):
    return block if rows is None else block.at[rows]


def _copies(per_piece):
    def mark(fn):
        fn.per_piece = per_piece
        return fn
    return mark


@_copies(N_DEV - 1)
def _plan_exchange(srcs, lands, send_sems, recv_sems, arrivals):
    x, y, c = _position()
    me = _dev_index(x, y, c)
    out = []
    for k, peer in enumerate(_peers(x, y, c)):
        p = _dev_index(*peer)
        for i, (t, rows) in enumerate(_pieces(lands)):
            sems = dict(send_sem=send_sems.at[7 * i + k], recv_sem=recv_sems.at[7 * i + k], device_id=peer, device_id_type=MESH)
            src, dst = (lands[t].at[p], lands[t].at[p]) if arrivals else (srcs[t].at[p], lands[t].at[me])
            out.append(pltpu.make_async_remote_copy(src_ref=_cut(src, rows), dst_ref=_cut(dst, rows), **sems))
    return out


@_copies(N_DEV - 1)
def _plan_broadcast(srcs, lands, send_sems, recv_sems, arrivals):
    x, y, c = _position()
    me = _dev_index(x, y, c)
    out = []
    for k, peer in enumerate(_peers(x, y, c)):
        p = _dev_index(*peer)
        for i, (t, rows) in enumerate(_pieces(lands)):
            sems = dict(send_sem=send_sems.at[7 * i + k], recv_sem=recv_sems.at[7 * i + k], device_id=peer, device_id_type=MESH)
            src, dst = (lands[t].at[p], lands[t].at[p]) if arrivals else (srcs[t], lands[t].at[me])
            out.append(pltpu.make_async_remote_copy(src_ref=_cut(src, rows), dst_ref=_cut(dst, rows), **sems))
    return out


@_copies(4)
def _plan_gather_out(srcs, lands, send_sems, recv_sems, arrivals):
    x, y, c = _position()
    me = _dev_index(x, y, c)
    out = []
    for k, peer in enumerate([(x, y, 1 - c), (1 - x, y, c), (x, 1 - y, c), (1 - x, 1 - y, c)]):
        p = _dev_index(*peer)
        for i, (t, rows) in enumerate(_pieces(lands)):
            sems = dict(send_sem=send_sems.at[4 * i + k], recv_sem=recv_sems.at[4 * i + k], device_id=peer, device_id_type=MESH)
            src, dst = (lands[t].at[p], lands[t].at[p]) if arrivals else (srcs[t], lands[t].at[me])
            out.append(pltpu.make_async_remote_copy(src_ref=_cut(src, rows), dst_ref=_cut(dst, rows), **sems))
    return out


@_copies(3)
def _plan_gather_pass(srcs, lands, send_sems, recv_sems, arrivals):
    x, y, c = _position()
    sibling = (x, y, 1 - c)
    out = []
    for k, chip in enumerate([(1 - x, y), (x, 1 - y), (1 - x, 1 - y)]):
        p = _dev_index(*chip, 1 - c) if arrivals else _dev_index(*chip, c)
        for i, (t, rows) in enumerate(_pieces(lands)):
            sems = dict(send_sem=send_sems.at[3 * i + k], recv_sem=recv_sems.at[3 * i + k], device_id=sibling, device_id_type=MESH)
            block = _cut(lands[t].at[p], rows)
            out.append(pltpu.make_async_remote_copy(src_ref=block, dst_ref=block, **sems))
    return out


def _split_start(plan, srcs, lands, *, after=None, name):
    n_src, n = len(srcs), len(srcs) + len(lands)
    n_sem = plan.per_piece * len(_pieces(lands))
    order = [] if after is None else [after]

    def body(*refs):
        send_sems, recv_sems = refs[n + len(order):n + len(order) + 2]
        token = refs[-1]
        for cp in plan(refs[:n_src], refs[n_src:n], send_sems, recv_sems, arrivals=False):
            cp.start()
        token[...] = jnp.zeros_like(token)

    hbm = lambda a: pltpu.HBM(a.shape, a.dtype)
    outs = pl.pallas_call(
        body,
        name=name,
        in_specs=[_HBM] * n + [_ANY] * len(order),
        out_specs=[_SEM, _SEM] + [_HBM] * n + [pl.BlockSpec(memory_space=pltpu.VMEM)],
        out_shape=[pltpu.SemaphoreType.DMA((n_sem,)), pltpu.SemaphoreType.DMA((n_sem,))] + [hbm(a) for a in (*srcs, *lands)]
        + [jax.ShapeDtypeStruct(_TOKEN, F32)],
        input_output_aliases={i: 2 + i for i in range(n)},
        compiler_params=pltpu.CompilerParams(has_side_effects=pltpu.SideEffectType.DATAFLOW_SIDE_EFFECTING),
    )(*[pltpu.with_memory_space_constraint(a, pltpu.HBM) for a in (*srcs, *lands)], *order)
    return (outs[0], outs[1], outs[2:2 + n_src], outs[2 + n_src:2 + n]), outs[-1]


def _split_wait(plan, state, after, *, name):
    send_sems, recv_sems, srcs, lands = state
    n_src, n = len(srcs), len(srcs) + len(lands)

    def body(*refs):
        send_refs, recv_refs = refs[n:n + 2]
        for cp in plan(refs[:n_src], refs[n_src:n], send_refs, recv_refs, arrivals=False):
            cp.wait_send()
        for cp in plan(refs[:n_src], refs[n_src:n], send_refs, recv_refs, arrivals=True):
            cp.wait_recv()

    hbm = lambda a: pltpu.HBM(a.shape, a.dtype)
    outs = pl.pallas_call(
        body,
        name=name,
        in_specs=[_HBM] * n + [_SEM, _SEM, _ANY],
        out_specs=[_HBM] * n,
        out_shape=[hbm(a) for a in (*srcs, *lands)],
        input_output_aliases={i: i for i in range(n)},
        compiler_params=pltpu.CompilerParams(has_side_effects=pltpu.SideEffectType.DATAFLOW_SIDE_EFFECTING),
    )(*srcs, *lands, send_sems, recv_sems, after)
    return outs[n_src:]


def _sum_blocks(blocks, *, name):
    _, R, C = blocks.shape
    tm = next(R // n for n in (4, 3, 2, 1) if R % (8 * n) == 0)

    def body(b_ref, o_ref):
        g = b_ref[0]
        for dev in range(1, N_DEV):
            g = g + b_ref[dev]
        o_ref[...] = g

    return pl.pallas_call(
        body,
        name=name,
        grid=(R // tm,),
        in_specs=[pl.BlockSpec((N_DEV, tm, C), lambda i: (0, i, 0))],
        out_specs=pl.BlockSpec((tm, C), lambda i: (i, 0)),
        out_shape=jax.ShapeDtypeStruct((R, C), F32),
        compiler_params=_params(("parallel",)),
    )(blocks)


def _block_diag(w):
    out = jnp.zeros((POOL_W, POOL_W), w.dtype)
    for gi in range(4):
        out = out.at[64 * gi:64 * (gi + 1), 64 * gi:64 * (gi + 1)].set(w[gi])
    return out


def _layer_consts(sp, l):
    causal = jnp.tril(jnp.ones((SGU_CHUNK, SGU_CHUNK), F32))
    wm = (sp["sgu_w"][l] * causal[None]).astype(BF16)
    wbd = _block_diag(sp["pool_w"][l]).astype(BF16)
    return dict(
        wbd=wbd, wbd_t=wbd.T, wm=wm, wm_t=wm.transpose(0, 2, 1),
        sgu_bias=jnp.repeat(sp["sgu_b"][l].T, 64, axis=1),
        bpad=jnp.pad(sp["b_forget"][l], (0, F_LANES - FOX_H)).reshape(1, F_LANES),
        bg=sp["b_gate"][l].reshape(1, 3 * D),
    )


def _relu2(acc):
    return acc, jnp.square(jnp.maximum(acc, 0.0))


def _relu2_grad(acc, z):
    return (acc * 2.0 * jnp.maximum(z, 0.0),)


def _layer_fwd(l, x, mem, source, sp):
    S = x.shape[0]
    t = _tile(S, 256)
    c = _layer_consts(sp, l)
    n = f"l{l}_"
    W, after = source(l, "begin", x)
    h = _rms_fwd(x, sp["norm_mix_g"][l], after=after, name=n + "norm_mix")
    qkv = _mm(h, W["qkv"], out_dtypes=(BF16,), name=n + "qkv")
    rest = _mm(h, W["rest"], name=n + "rest")
    pa = _pool_fwd(rest, c["wbd"], sp["pool_scale"][l], name=n + "pool")
    cum, cum_t = _fox_prep(rest, c["bpad"], name=n + "fox_prep")
    fk3 = cum_t[:FOX_H].reshape(FOX_H, S // t, t)
    o, lse = _fox_fwd(qkv, cum, fk3, name=n + "fox")
    more, _ = source(l, "attended", o)
    W.update(more)
    sg = _sgu_fwd(rest, sp["sgu_norm_g"][l], c["wm"], c["sgu_bias"], name=n + "sgu")
    more, after = source(l, "mixed", sg)
    W.update(more)
    ya = _mm(pa, W["ba"], out_dtypes=(BF16,), after=after, name=n + "branch_a")
    yb = _mm(o, W["bb"], out_dtypes=(BF16,), name=n + "branch_b")
    yc = _mm(sg, W["bc"], out_dtypes=(BF16,), name=n + "branch_c")
    merged = _merge_fwd(rest, c["bg"], ya, yb, yc, name=n + "merge")
    x1 = _mm(merged, W["out"], extras=(x,), epilogue=_add, name=n + "out")
    hx = _rms_fwd(x1, sp["norm_xattn_g"][l], name=n + "norm_xattn")
    hm = _rms_fwd(mem, sp["norm_mem_g"][l], name=n + "norm_mem")
    xq = _mm(hx, W["xq"], out_dtypes=(BF16,), name=n + "xq")
    kv = _mm(hm, W["xkv"], out_dtypes=(BF16,), name=n + "xkv")
    o2 = _xattn_fwd(xq, kv, name=n + "xattn")
    x2 = _mm(o2, W["xo"], extras=(x1,), epilogue=_add, name=n + "xo")
    hf = _rms_fwd(x2, sp["norm_ffn_g"][l], name=n + "norm_ffn")
    z, act = _mm(hf, W["ff1"], epilogue=_relu2, out_dtypes=(BF16, BF16), name=n + "ff1")
    _, after = source(l, "expanded", act)
    x3 = _mm(act, W["ff2"], extras=(x2,), epilogue=_add, after=after, name=n + "ff2")
    saved = dict(x=x, h=h, qkv=qkv, rest=rest, pa=pa, cum=cum, fk3=fk3, o=o, lse=lse, sg=sg, ya=ya, yb=yb, yc=yc,
                 merged=merged, x1=x1, hx=hx, hm=hm, xq=xq, kv=kv, o2=o2, x2=x2, hf=hf, z=z, act=act, c=c)
    return x3, saved, W


def _layer_bwd(l, dx3, sv, mem, W, sp, grads_done):
    S = dx3.shape[0]
    c = sv["c"]
    n = f"l{l}b_"
    bf = dict(out_dtypes=(BF16,))
    gw, gs = {}, {}
    gw["ff2"] = _mm(sv["act"], dx3, ta=True, name=n + "dw_ff2", **bf)
    dz = _mm(dx3, W["ff2"], tb=True, extras=(sv["z"],), epilogue=_relu2_grad, name=n + "dz", **bf)
    gw["ff1"] = _mm(sv["hf"], dz, ta=True, shard_out=True, name=n + "dw_ff1", **bf)
    dhf = _mm(dz, W["ff1"], tb=True, name=n + "dhf")
    dx2, gs["norm_ffn_g"] = _rms_bwd(sv["x2"], sp["norm_ffn_g"][l], dhf, dx3, name=n + "dnorm_ffn")
    gw["xo"] = _mm(sv["o2"], dx2, ta=True, name=n + "dw_xo", **bf)
    do2 = _mm(dx2, W["xo"], tb=True, name=n + "do2", **bf)
    dxq, dkv = _xattn_bwd(sv["xq"], sv["kv"], do2, name=n + "dxattn")
    gw["xq"] = _mm(sv["hx"], dxq, ta=True, name=n + "dw_xq", **bf)
    gw["xkv"] = _mm(sv["hm"], dkv, ta=True, shard_out=True, name=n + "dw_xkv", **bf)
    dhm = _mm(dkv, W["xkv"], tb=True, name=n + "dhm")
    _, gs["norm_mem_g"] = _rms_bwd(mem, sp["norm_mem_g"][l], dhm, jnp.zeros_like(mem), name=n + "dnorm_mem")
    dhx = _mm(dxq, W["xq"], tb=True, name=n + "dhx")
    dx1, gs["norm_xattn_g"] = _rms_bwd(sv["x1"], sp["norm_xattn_g"][l], dhx, dx2, name=n + "dnorm_xattn")
    after, gw = grads_done(l, gw), {}
    gw["out"] = _mm(sv["merged"], dx1, ta=True, name=n + "dw_out", **bf)
    dm = _mm(dx1, W["out"], tb=True, after=after, name=n + "dmerged")
    dya, dyb, dyc, dg1, dg2, dg3, db1, db2, db3 = _merge_bwd(sv["rest"], c["bg"], sv["ya"], sv["yb"], sv["yc"], dm, name=n + "dmerge")
    gs["b_gate"] = jnp.concatenate([db1, db2, db3], axis=1).reshape(3 * D)
    gw["ba"] = _mm(sv["pa"], dya, ta=True, shard_out=True, name=n + "dw_ba", **bf)
    gw["bb"] = _mm(sv["o"], dyb, ta=True, shard_out=True, name=n + "dw_bb", **bf)
    gw["bc"] = _mm(sv["sg"], dyc, ta=True, shard_out=True, name=n + "dw_bc", **bf)
    after, gw = grads_done(l, gw), {}
    dpa = _mm(dya, W["ba"], tb=True, name=n + "dpa")
    do = _mm(dyb, W["bb"], tb=True, after=after, name=n + "do", **bf)
    dsg = _mm(dyc, W["bc"], tb=True, name=n + "dsg")
    da, dwbd, dscale = _pool_bwd(sv["rest"], c["wbd"], c["wbd_t"], sp["pool_scale"][l], dpa, name=n + "dpool")
    gs["pool_w"] = jnp.stack([dwbd[64 * gi:64 * (gi + 1), 64 * gi:64 * (gi + 1)] for gi in range(4)])
    gs["pool_scale"] = dscale.reshape(POOL_W)
    dq, dk, dv, dfq, dfk = _fox_bwd(sv["qkv"], sv["cum"], sv["fk3"], sv["o"], do, sv["lse"], name=n + "dfox")
    dcum = dfq + jnp.pad(dfk.reshape(FOX_H, S).T, ((0, 0), (0, F_LANES - FOX_H)))
    df, dbf = _fox_post(sv["rest"], c["bpad"], dcum, name=n + "dfox_post")
    gs["b_forget"] = dbf[0, :FOX_H]
    dc, dwm, dbias, dgn = _sgu_bwd(sv["rest"], sp["sgu_norm_g"][l], c["wm"], c["wm_t"], c["sgu_bias"], dsg, name=n + "dsgu")
    gs["sgu_w"] = dwm * jnp.tril(jnp.ones((SGU_CHUNK, SGU_CHUNK), F32))[None]
    gs["sgu_b"] = dbias.reshape(SGU_CHUNK, 4, 64).sum(axis=2).T
    gs["sgu_norm_g"] = dgn.reshape(SGU_W)
    dqkv = [dq, dk, dv]
    drest = [jnp.concatenate([da, df, jnp.zeros((S, OFF_C - OFF_F - F_LANES), BF16), dc], axis=1), dg1, dg2, dg3]
    gw["qkv"] = _mm(sv["h"], dqkv, ta=True, name=n + "dw_qkv", **bf)
    gw["rest"] = _mm(sv["h"], drest, ta=True, name=n + "dw_rest", **bf)
    after = grads_done(l, gw)
    dh = _mm(dqkv, W["qkv"], tb=True, after=after, name=n + "dh_qkv")
    dh = _mm(drest, W["rest"], tb=True, extras=(dh,), epilogue=_add, tm=1024, name=n + "dh")
    dx, gs["norm_mix_g"] = _rms_bwd(sv["x"], sp["norm_mix_g"][l], dh, dx1, name=n + "dnorm_mix")
    return dx, gs


def _local_step(x, mem, target, sp, source, grads_done):
    saved, Ws = [], []
    for l in range(DEPTH):
        x, sv, W = _layer_fwd(l, x, mem, source, sp)
        saved.append(sv)
        Ws.append(W)
    loss, dx, dgf = _final_loss(x, sp["final_norm_g"], target, name="final_loss")
    gss = [None] * DEPTH
    for l in reversed(range(DEPTH)):
        dx, gss[l] = _layer_bwd(l, dx, saved[l], mem, Ws[l], sp, grads_done)
    small = {k: jnp.stack([gss[l][k] for l in range(DEPTH)]) for k in gss[0]}
    small["final_norm_g"] = dgf
    return loss, dx, small


_SMALL = ["norm_mix_g", "b_forget", "pool_w", "pool_scale", "sgu_norm_g", "sgu_w", "sgu_b", "b_gate", "norm_xattn_g",
          "norm_mem_g", "norm_ffn_g", "final_norm_g"]
_COL = {"w_branch_a": "ba", "w_branch_b": "bb", "w_branch_c": "bc", "w_xkv": "xkv", "w_ff1": "ff1"}
_ROW = {"w_out": "out", "w_xq": "xq", "w_xo": "xo", "w_ff2": "ff2"}
_BIG = ["w_in", "w_branch_a", "w_branch_b", "w_branch_c", "w_out", "w_xq", "w_xkv", "w_xo", "w_ff1", "w_ff2"]
_PACK_LANES = 128


def _as_rows(a):
    return a.reshape(-1, a.shape[-1])


def _pack(tensors):
    rows = []
    for a in tensors:
        flat = a.reshape(-1)
        flat = jnp.pad(flat, (0, (-flat.shape[0]) % (8 * _PACK_LANES)))
        rows.append(flat.reshape(-1, _PACK_LANES))
    n_rows = sum(r.shape[0] for r in rows)
    rows.append(jnp.zeros(((-n_rows) % (8 * N_DEV), _PACK_LANES), F32))
    return jnp.concatenate(rows, axis=0)


def _unpack(packed, like):
    out, r = [], 0
    for a in like:
        size = math.prod(a.shape)
        nr = 8 * (-(-size // (8 * _PACK_LANES)))
        out.append(packed[r:r + nr].reshape(-1)[:size].reshape(a.shape))
        r += nr
    return out


_SHARD_IN = N_IN // N_DEV
_SHARD_IN_PAD = -(-_SHARD_IN // 128) * 128


def _columns(pieces, start, stop):
    out, at = [], 0
    for p in pieces:
        lo, hi = max(start, at), min(stop, at + p.shape[1])
        if lo < hi:
            out.append(p[:, lo - at:hi - at])
        at += p.shape[1]
    return out


def _split_w_in(blocks):
    K = blocks[0].shape[0]
    pad = jnp.zeros((K, OFF_C - OFF_F - FOX_H), blocks[0].dtype)
    cols = functools.partial(_columns, blocks)
    rest = jnp.concatenate(cols(0, R_OFF_Q) + cols(R_OFF_F, R_OFF_C) + [pad] + cols(R_OFF_C, N_IN), axis=1)
    return jnp.concatenate(cols(R_OFF_Q, R_OFF_F), axis=1), rest


def _join_w_in(qkv, rest):
    in_order = [rest[:, :R_OFF_Q], qkv, rest[:, OFF_F:OFF_F + FOX_H], rest[:, OFF_C:]]
    pad = jnp.zeros((qkv.shape[0], _SHARD_IN_PAD - _SHARD_IN), qkv.dtype)
    return jnp.stack([jnp.concatenate(_columns(in_order, _SHARD_IN * d, _SHARD_IN * (d + 1)) + [pad], axis=1) for d in range(N_DEV)])


_FIRST = ["w_in"]
_LATER = [k for k in _BIG if k not in _FIRST]


def _layer_weights(gathered):
    W = {}
    if "w_in" in gathered:
        W.update(zip(("qkv", "rest"), _split_w_in([gathered["w_in"][d][:, :_SHARD_IN] for d in range(N_DEV)])))
    for name, key in _COL.items():
        if name in gathered:
            W[key] = _Gathered(gathered[name])
    for name, key in _ROW.items():
        if name in gathered:
            W[key] = gathered[name].reshape(-1, gathered[name].shape[-1])
    return W


def _grad_blocks(gw):
    parts = {}
    if "qkv" in gw:
        parts["w_in"] = _join_w_in(gw["qkv"], gw["rest"])
    for name, key in _COL.items():
        if key in gw:
            parts[name] = gw[key]
    for name, key in _ROW.items():
        if key in gw:
            parts[name] = gw[key].reshape(N_DEV, -1, gw[key].shape[-1])
    return parts


def kernel(x, mem, norm_mix_g, w_in, b_forget, pool_w, pool_scale, sgu_norm_g, sgu_w, sgu_b, w_branch_a, w_branch_b, w_branch_c, b_gate, w_out, norm_xattn_g, norm_mem_g, w_xq, w_xkv, w_xo, norm_ffn_g, w_ff1, w_ff2, final_norm_g, loss_target, m_norm_mix_g, m_w_in, m_b_forget, m_pool_w, m_pool_scale, m_sgu_norm_g, m_sgu_w, m_sgu_b, m_w_branch_a, m_w_branch_b, m_w_branch_c, m_b_gate, m_w_out, m_norm_xattn_g, m_norm_mem_g, m_w_xq, m_w_xkv, m_w_xo, m_norm_ffn_g, m_w_ff1, m_w_ff2, m_final_norm_g, v_norm_mix_g, v_w_in, v_b_forget, v_pool_w, v_pool_scale, v_sgu_norm_g, v_sgu_w, v_sgu_b, v_w_branch_a, v_w_branch_b, v_w_branch_c, v_b_gate, v_w_out, v_norm_xattn_g, v_norm_mem_g, v_w_xq, v_w_xkv, v_w_xo, v_norm_ffn_g, v_w_ff1, v_w_ff2, v_final_norm_g):
    names = ["norm_mix_g", "w_in", "b_forget", "pool_w", "pool_scale", "sgu_norm_g", "sgu_w", "sgu_b", "w_branch_a", "w_branch_b",
             "w_branch_c", "b_gate", "w_out", "norm_xattn_g", "norm_mem_g", "w_xq", "w_xkv", "w_xo", "norm_ffn_g", "w_ff1", "w_ff2",
             "final_norm_g"]
    w = dict(zip(names, [norm_mix_g, w_in, b_forget, pool_w, pool_scale, sgu_norm_g, sgu_w, sgu_b, w_branch_a, w_branch_b, w_branch_c,
                         b_gate, w_out, norm_xattn_g, norm_mem_g, w_xq, w_xkv, w_xo, norm_ffn_g, w_ff1, w_ff2, final_norm_g]))
    m = dict(zip(names, [m_norm_mix_g, m_w_in, m_b_forget, m_pool_w, m_pool_scale, m_sgu_norm_g, m_sgu_w, m_sgu_b, m_w_branch_a,
                         m_w_branch_b, m_w_branch_c, m_b_gate, m_w_out, m_norm_xattn_g, m_norm_mem_g, m_w_xq, m_w_xkv, m_w_xo,
                         m_norm_ffn_g, m_w_ff1, m_w_ff2, m_final_norm_g]))
    v = dict(zip(names, [v_norm_mix_g, v_w_in, v_b_forget, v_pool_w, v_pool_scale, v_sgu_norm_g, v_sgu_w, v_sgu_b, v_w_branch_a,
                         v_w_branch_b, v_w_branch_c, v_b_gate, v_w_out, v_norm_xattn_g, v_norm_mem_g, v_w_xq, v_w_xkv, v_w_xo,
                         v_norm_ffn_g, v_w_ff1, v_w_ff2, v_final_norm_g]))

    sp = {k: w[k] for k in _SMALL}
    shards = [{k: w[k][l].astype(BF16) for k in _BIG} for l in range(DEPTH)]
    for sh in shards:
        sh["w_in"] = jnp.pad(sh["w_in"], ((0, 0), (0, _SHARD_IN_PAD - _SHARD_IN)))
    me = _dev_index(*_position())

    def gather_out(l, keys, name, after=None):
        srcs = [shards[l][k] for k in keys]
        lands = [_own_block_placed(a, jax.ShapeDtypeStruct((N_DEV, *a.shape), a.dtype)) for a in srcs]
        state, token = _split_start(_plan_gather_out, srcs, lands, after=after, name=name + "_out_start")
        return (keys, name, state), token

    def gather_pass(job, value):
        keys, name, state = job
        lands = _split_wait(_plan_gather_out, state, value, name=name + "_out_wait")
        state, token = _split_start(_plan_gather_pass, [], lands, name=name + "_pass_start")
        return (keys, name, state), token, lands[0]

    def gather_end(job, value):
        keys, name, state = job
        return _layer_weights(dict(zip(keys, _split_wait(_plan_gather_pass, state, value, name=name + "_pass_wait"))))

    jobs = {}

    def source(l, point, value):
        if (l, point) == (0, "begin"):
            first = _all_gather([shards[0][k] for k in _FIRST], name="gather_l0_first")
            jobs["l0"], token = gather_out(0, _LATER, "gather_l0", after=first[0])
            return _layer_weights(dict(zip(_FIRST, first))), token
        if (l, point) == (0, "attended"):
            jobs["l0"], _, arrived = gather_pass(jobs["l0"], value)
            jobs["l1_first"], token = gather_out(1, _FIRST, "gather_l1_first", after=arrived)
            jobs["l1"], jobs["token"] = gather_out(1, _LATER, "gather_l1", after=token)
            return {}, None
        if (l, point) == (0, "mixed"):
            return gather_end(jobs.pop("l0"), value), jobs.pop("token")
        if (l, point) == (0, "expanded"):
            jobs["l1_first"], token, _ = gather_pass(jobs["l1_first"], value)
            return {}, token
        if (l, point) == (1, "begin"):
            W = gather_end(jobs.pop("l1_first"), value)
            jobs["l1"], token, _ = gather_pass(jobs["l1"], value)
            return W, token
        if (l, point) == (1, "mixed"):
            return gather_end(jobs.pop("l1"), value), None
        return {}, None

    received = [{} for _ in range(DEPTH)]
    travelling = []

    def grads_done(l, gw):
        blocks = _grad_blocks(gw)
        keys = [k for k in _BIG if k in blocks]
        parts = [blocks[k] for k in keys]
        group = f"exchange_grads_l{l}_" + ("in" if "w_in" in blocks else "merge" if "w_out" in blocks else "mlp")
        lands = [_own_block_placed(lax.dynamic_index_in_dim(p, me, 0, keepdims=False), p) for p in parts]
        state, token = _split_start(_plan_exchange, parts, lands, name=group + "_start")
        travelling.append((l, keys, state, group + "_wait"))
        return token

    loss, dx, small = _local_step(x[0], mem[0], loss_target[0], sp, source, grads_done)
    grads, deltas, new_m, new_v = {}, {}, {}, {}
    like = [loss] + [w[k] for k in _SMALL]
    packed = _pack([loss] + [small[k] for k in _SMALL])
    eighths = packed.reshape(N_DEV, -1, _PACK_LANES)
    own = lambda a: _own_block_placed(lax.dynamic_index_in_dim(a, me, 0, keepdims=False) if a.ndim == 3 else a, eighths)
    scatter, done = _split_start(_plan_exchange, [eighths], [own(eighths)], after=dx, name="small_grads_scatter_start")

    def reduce_small(after):
        mine = _sum_blocks(_split_wait(_plan_exchange, scatter, after, name="small_grads_scatter_wait")[0], name="small_grads_sum")
        return _split_start(_plan_broadcast, [mine], [own(mine)], name="small_grads_gather_start")

    def update_small(state, after):
        total = _split_wait(_plan_broadcast, state, after, name="small_grads_gather_wait")[0].reshape(packed.shape)
        loss_sum, *g_small = _unpack(total, like)
        rows = lambda d: [_as_rows(d[k]) for k in _SMALL]
        outs = _adamw_small([_as_rows(g) for g in g_small], rows(w), rows(m), rows(v), name="adamw_small")
        grads.update(zip(_SMALL, g_small))
        for dst, vals in zip((deltas, new_m, new_v), outs):
            dst.update({k: a.reshape(w[k].shape) for k, a in zip(_SMALL, vals)})
        return loss_sum[0, 0], outs[0][0]

    groups = list(dict.fromkeys(tuple(keys) for _, keys, _, _ in travelling))
    for n_done, group_keys in enumerate(groups):
        if n_done == 1:
            gather, _ = reduce_small(done)
        if n_done == len(groups) - 1:
            loss, done = update_small(gather, done)
        for l, keys, state, wait_name in travelling:
            if tuple(keys) == group_keys:
                received[l].update(zip(keys, _split_wait(_plan_exchange, state, done, name=wait_name)))
        for k in group_keys:
            outs = _adamw_sharded([received[l][k] for l in range(DEPTH)], w[k], m[k], v[k], name="adamw_" + k)
            grads[k], deltas[k], new_m[k], new_v[k] = outs
        done = grads[group_keys[-1]]

    return (loss, dx[None], *[grads[k] for k in names], *[deltas[k] for k in names], *[new_m[k] for k in names],
            *[new_v[k] for k in names])
```

```python
import functools
import math

import jax
import jax.numpy as jnp
from jax import lax
from jax.experimental import pallas as pl
from jax.experimental.pallas import tpu as pltpu

F32 = jnp.float32
BF16 = jnp.bfloat16
MESH = pl.DeviceIdType.MESH

N_DEV = 8
D = 1024
DEPTH = 2
EPS = 1e-6
NEG = -1e30
POOL_W = 256
FOX_H = 8
FOX_DH = 64
FOX_W = 512
SGU_W = 256
SGU_CHUNK = 128
XH = 4
XDH = 256
N_IN = 5384
R_OFF_Q, R_OFF_F, R_OFF_C = 256, 1792, 1800
QKV_W = 3 * FOX_W
OFF_A, OFF_F, OFF_C, OFF_G, REST_W = 0, 256, 512, 1024, 4096
F_LANES = 128

ADAM_LR = 0.001
ADAM_B1 = 0.9
ADAM_B2 = 0.999
ADAM_EPS = 1e-08
ADAM_WD = 0.01
ADAM_STEP = 10

VMEM_LIMIT = 56 * 1024 * 1024


def _tile(n, pref):
    t = min(n, pref)
    while n % t:
        t -= 128
    assert t > 0, (n, pref)
    return t


def _params(sem=None):
    return pltpu.CompilerParams(dimension_semantics=sem, vmem_limit_bytes=VMEM_LIMIT)


def _dot(a, b, ca, cb):
    return lax.dot_general(a, b, (((ca,), (cb,)), ((), ())), preferred_element_type=F32)


def _sigmoid(z):
    return 1.0 / (1.0 + jnp.exp(-z))


_GELU_K = math.sqrt(2.0 / math.pi)
_GELU_C = 0.044715


def _gelu(x):
    return 0.5 * x * (1.0 + jnp.tanh(_GELU_K * (x + _GELU_C * x * x * x)))


def _gelu_grad(x):
    t = jnp.tanh(_GELU_K * (x + _GELU_C * x * x * x))
    return 0.5 * (1.0 + t) + 0.5 * x * (1.0 - t * t) * _GELU_K * (1.0 + 3.0 * _GELU_C * x * x)


def _rows(shape):
    return lax.broadcasted_iota(jnp.int32, shape, 0)


def _lanes(shape):
    return lax.broadcasted_iota(jnp.int32, shape, 1)


class _Gathered:
    def __init__(self, arr):
        self.arr = arr
        self.shape = (arr.shape[1], N_DEV * arr.shape[2])


_TOKEN = (8, 128)


def _mm(a, b, *, ta=False, tb=False, extras=(), row_extras=(), epilogue=None, out_dtypes=(F32,), shard_out=False, after=None, tm=None,
        tn=512, tk=None, name):
    a_parts = list(a) if isinstance(a, (list, tuple)) else [a]
    b_parts = list(b) if isinstance(b, (list, tuple)) else [b]
    gathered = isinstance(b, _Gathered)
    assert (len(a_parts) == 1 or not ta) and (len(b_parts) == 1 or not tb) and min(len(a_parts), len(b_parts)) == 1
    a0, b0 = a_parts[0], b_parts[0]
    M, K = (a0.shape[1], a0.shape[0]) if ta else (a0.shape[0], a0.shape[1] * len(a_parts))
    N, Kb = b0.shape if tb else (b0.shape[1] * len(b_parts), b0.shape[0])
    assert Kb == K, (a0.shape, b0.shape, ta, tb)
    if gathered:
        if tb:
            tk = b.arr.shape[2]
        else:
            tn = b.arr.shape[2]
    if len(a_parts) > 1:
        tk = a0.shape[1]
    if shard_out:
        tn = N // N_DEV
    tm = _tile(M, tm or (1024 if ta else 2048))
    tn = _tile(b0.shape[1] if len(b_parts) > 1 else N, tn)
    per_piece = b0.shape[1] // tn
    size = lambda dt: jnp.dtype(dt).itemsize
    row_bytes = len(a_parts) * tm * size(a0.dtype) + len(b_parts) * tn * size(b.arr.dtype if gathered else b0.dtype)
    tile_bytes = tm * tn * (sum(size(e.dtype) for e in extras) + sum(map(size, out_dtypes)))

    def vmem_bytes(k_tile):
        return 2 * (k_tile * row_bytes + tile_bytes) + tm * tn * 4 * (K > k_tile)

    if tk is None:
        tk = next(c for c in (_tile(K, 2048), _tile(K, 1024), _tile(K, 512), _tile(K, 256)) if vmem_bytes(c) <= VMEM_LIMIT - (4 << 20))
    tk = _tile(K, tk)
    nk = K // tk
    ca, cb = (0 if ta else 1), (1 if tb else 0)
    n_a, n_b, n_ex, n_out = len(a_parts), len(b_parts), len(extras) + len(row_extras), len(out_dtypes)
    tokens = [] if after is None else [after]
    n_in = n_a + n_b + n_ex + len(tokens)
    if epilogue is None:
        epilogue = lambda acc: (acc,)

    def body(*refs):
        a_refs, b_refs = refs[:n_a], refs[n_a:n_a + n_b]
        ex_refs = refs[n_a + n_b:n_a + n_b + n_ex]
        o_refs = refs[n_in:n_in + n_out]
        j, k = pl.program_id(1), pl.program_id(2)

        def finish(acc):
            for o_ref, val in zip(o_refs, epilogue(acc, *[e[...] for e in ex_refs])):
                o_ref[...] = val.astype(o_ref.dtype)

        def step(a_ref, b_ref):
            part = _dot(a_ref[...].astype(BF16), b_ref[...].astype(BF16), ca, cb)
            if nk == 1:
                finish(part)
            else:
                acc_ref = refs[-1]

                @pl.when(k == 0)
                def _():
                    acc_ref[...] = part

                @pl.when(k > 0)
                def _():
                    acc_ref[...] += part

                @pl.when(k == nk - 1)
                def _():
                    finish(acc_ref[...])

        if n_a > 1:
            for p in range(n_a):
                pl.when(k == p)(functools.partial(step, a_refs[p], b_refs[0]))
        elif n_b > 1:
            for p in range(n_b):
                pl.when(j // per_piece == p)(functools.partial(step, a_refs[0], b_refs[p]))
        else:
            step(a_refs[0], b_refs[0])

    if n_a > 1:
        a_specs = [pl.BlockSpec((tm, tk), lambda i, j, k: (i, 0))] * n_a
    else:
        a_specs = [pl.BlockSpec((tk, tm), lambda i, j, k: (k, i)) if ta else pl.BlockSpec((tm, tk), lambda i, j, k: (i, k))]
    if gathered:
        b_arrs = [b.arr]
        b_specs = [pl.BlockSpec((None, tn, tk), lambda i, j, k: (k, j, 0)) if tb else pl.BlockSpec((None, tk, tn), lambda i, j, k: (j, k, 0))]
    elif n_b > 1:
        b_arrs = b_parts
        b_specs = [pl.BlockSpec((tk, tn), functools.partial(lambda p, i, j, k: (k, jnp.clip(j - p * per_piece, 0, per_piece - 1)), p))
                   for p in range(n_b)]
    else:
        b_arrs = b_parts
        b_specs = [pl.BlockSpec((tn, tk), lambda i, j, k: (j, k)) if tb else pl.BlockSpec((tk, tn), lambda i, j, k: (k, j))]
    tile = pl.BlockSpec((tm, tn), lambda i, j, k: (i, j))
    if shard_out:
        out_specs = [pl.BlockSpec((None, tm, tn), lambda i, j, k: (j, i, 0))] * n_out
        out_shape = [jax.ShapeDtypeStruct((N_DEV, M, tn), dt) for dt in out_dtypes]
    else:
        out_specs = [tile] * n_out
        out_shape = [jax.ShapeDtypeStruct((M, N), dt) for dt in out_dtypes]
    assert vmem_bytes(tk) <= VMEM_LIMIT - (4 << 20), (name, vmem_bytes(tk))
    outs = pl.pallas_call(
        body,
        name=name,
        grid=(M // tm, N // tn, nk),
        in_specs=a_specs + b_specs + [tile] * len(extras) + [pl.BlockSpec((1, tn), lambda i, j, k: (0, j))] * len(row_extras)
        + [pl.BlockSpec(_TOKEN, lambda i, j, k: (0, 0))] * len(tokens),
        out_specs=out_specs,
        out_shape=out_shape,
        scratch_shapes=[pltpu.VMEM((tm, tn), F32)] if nk > 1 else [],
        compiler_params=_params(("parallel", "parallel", "arbitrary")),
    )(*a_parts, *b_arrs, *extras, *row_extras, *tokens)
    return outs[0] if n_out == 1 else outs


def _add(acc, res):
    return (acc + res,)


def _add_norm(acc, res, g):
    x = acc + res
    return x, x * lax.rsqrt(jnp.mean(x * x, axis=-1, keepdims=True) + EPS) * g


def _rms_fwd(x, g, *, after=None, name):
    R, C = x.shape
    tm = _tile(R, 256)
    tokens = [] if after is None else [after]

    def body(x_ref, g_ref, *rest):
        xv = x_ref[...]
        r = lax.rsqrt(jnp.mean(xv * xv, axis=-1, keepdims=True) + EPS)
        rest[-1][...] = (xv * r * g_ref[...]).astype(BF16)

    return pl.pallas_call(
        body,
        name=name,
        grid=(R // tm,),
        in_specs=[pl.BlockSpec((tm, C), lambda i: (i, 0)), pl.BlockSpec((1, C), lambda i: (0, 0))]
        + [pl.BlockSpec(_TOKEN, lambda i: (0, 0))] * len(tokens),
        out_specs=pl.BlockSpec((tm, C), lambda i: (i, 0)),
        out_shape=jax.ShapeDtypeStruct((R, C), BF16),
        compiler_params=_params(("parallel",)),
    )(x, g.reshape(1, C), *tokens)


def _rms_bwd(x, g, dh, dres, *, name):
    R, C = x.shape
    tm = _tile(R, 256)

    def body(x_ref, g_ref, dh_ref, dres_ref, dx_ref, dg_ref):
        xv = x_ref[...]
        r = lax.rsqrt(jnp.mean(xv * xv, axis=-1, keepdims=True) + EPS)
        xn = xv * r
        dh_v = dh_ref[...].astype(F32)
        dxn = dh_v * g_ref[...]
        dx_ref[...] = r * (dxn - xn * jnp.mean(dxn * xn, axis=-1, keepdims=True)) + dres_ref[...]
        part = jnp.sum(dh_v * xn, axis=0, keepdims=True)

        @pl.when(pl.program_id(0) == 0)
        def _():
            dg_ref[...] = part

        @pl.when(pl.program_id(0) > 0)
        def _():
            dg_ref[...] += part

    row = pl.BlockSpec((tm, C), lambda i: (i, 0))
    vec = pl.BlockSpec((1, C), lambda i: (0, 0))
    dx, dg = pl.pallas_call(
        body,
        name=name,
        grid=(R // tm,),
        in_specs=[row, vec, row, row],
        out_specs=[row, vec],
        out_shape=[jax.ShapeDtypeStruct((R, C), F32), jax.ShapeDtypeStruct((1, C), F32)],
        compiler_params=_params(("arbitrary",)),
    )(x, g.reshape(1, C), dh, dres)
    return dx, dg.reshape(C)


def _final_loss(x, g, target, *, name):
    R, C = x.shape
    tm = _tile(R, 256)

    def body(x_ref, g_ref, t_ref, loss_ref, dx_ref, dg_ref):
        xv = x_ref[...]
        r = lax.rsqrt(jnp.mean(xv * xv, axis=-1, keepdims=True) + EPS)
        xn = xv * r
        gv = g_ref[...]
        err = xn * gv - t_ref[...]
        lpart = (0.5 / C) * jnp.sum(jnp.sum(err * err, axis=1, keepdims=True), axis=0, keepdims=True)
        dy = err * (1.0 / C)
        dxn = dy * gv
        dx_ref[...] = r * (dxn - xn * jnp.mean(dxn * xn, axis=-1, keepdims=True))
        gpart = jnp.sum(dy * xn, axis=0, keepdims=True)

        @pl.when(pl.program_id(0) == 0)
        def _():
            loss_ref[...] = lpart
            dg_ref[...] = gpart

        @pl.when(pl.program_id(0) > 0)
        def _():
            loss_ref[...] += lpart
            dg_ref[...] += gpart

    row = pl.BlockSpec((tm, C), lambda i: (i, 0))
    vec = pl.BlockSpec((1, C), lambda i: (0, 0))
    loss, dx, dg = pl.pallas_call(
        body,
        name=name,
        grid=(R // tm,),
        in_specs=[row, vec, row],
        out_specs=[pl.BlockSpec((1, 1), lambda i: (0, 0)), row, vec],
        out_shape=[jax.ShapeDtypeStruct((1, 1), F32), jax.ShapeDtypeStruct((R, C), F32), jax.ShapeDtypeStruct((1, C), F32)],
        compiler_params=_params(("arbitrary",)),
    )(x, g.reshape(1, C), target)
    return loss, dx, dg.reshape(C)


def _pool_select(lane, vals):
    out = vals[3]
    for gi in (2, 1, 0):
        out = jnp.where(lane < 64 * (gi + 1), vals[gi], out)
    return out


def _pool_diff(a):
    row, lane = _rows(a.shape), _lanes(a.shape)

    def down(v, k):
        return jnp.where(row >= k, pltpu.roll(v, k, 0), 0.0)

    s2 = a + down(a, 1)
    s4 = s2 + down(s2, 2)
    s8 = s4 + down(s4, 4)
    s16 = s8 + down(s8, 8)
    wsum = _pool_select(lane, (s2, s4, s8, s16))
    win = _pool_select(lane, (2, 4, 8, 16))
    cnt = jnp.minimum(row + 1, win).astype(F32)
    return wsum / cnt - a, cnt


def _pool_diff_t(dd, cnt):
    S = dd.shape[0]
    row, lane = _rows(dd.shape), _lanes(dd.shape)

    def up(v, k):
        return jnp.where(row < S - k, pltpu.roll(v, S - k, 0), 0.0)

    e = dd / cnt
    s2 = e + up(e, 1)
    s4 = s2 + up(s2, 2)
    s8 = s4 + up(s4, 4)
    s16 = s8 + up(s8, 8)
    return _pool_select(lane, (s2, s4, s8, s16)) - dd


def _pool_fwd(rest, wbd, scale, *, name):
    S = rest.shape[0]

    def body(a_ref, w_ref, s_ref, o_ref):
        d, _ = _pool_diff(a_ref[...])
        yp = _dot(d.astype(BF16), w_ref[...], 1, 0)
        o_ref[...] = (yp * s_ref[...]).astype(BF16)

    return pl.pallas_call(
        body,
        name=name,
        grid=(1,),
        in_specs=[
            pl.BlockSpec((S, POOL_W), lambda i: (0, OFF_A // POOL_W)),
            pl.BlockSpec((POOL_W, POOL_W), lambda i: (0, 0)),
            pl.BlockSpec((1, POOL_W), lambda i: (0, 0)),
        ],
        out_specs=pl.BlockSpec((S, POOL_W), lambda i: (0, 0)),
        out_shape=jax.ShapeDtypeStruct((S, POOL_W), BF16),
        compiler_params=_params(("arbitrary",)),
    )(rest, wbd, scale.reshape(1, POOL_W))


def _pool_bwd(rest, wbd, wbd_t, scale, dpa, *, name):
    S = rest.shape[0]

    def body(a_ref, w_ref, wt_ref, s_ref, dpa_ref, da_ref, dw_ref, ds_ref):
        d, cnt = _pool_diff(a_ref[...])
        db = d.astype(BF16)
        yp = _dot(db, w_ref[...], 1, 0)
        dpa_v = dpa_ref[...]
        ds_ref[...] = jnp.sum(dpa_v * yp, axis=0, keepdims=True)
        dyp = (dpa_v * s_ref[...]).astype(BF16)
        dw_ref[...] = _dot(db, dyp, 0, 0)
        dd = _dot(dyp, wt_ref[...], 1, 0)
        da_ref[...] = _pool_diff_t(dd, cnt).astype(BF16)

    full = pl.BlockSpec((S, POOL_W), lambda i: (0, 0))
    sq = pl.BlockSpec((POOL_W, POOL_W), lambda i: (0, 0))
    vec = pl.BlockSpec((1, POOL_W), lambda i: (0, 0))
    return pl.pallas_call(
        body,
        name=name,
        grid=(1,),
        in_specs=[pl.BlockSpec((S, POOL_W), lambda i: (0, OFF_A // POOL_W)), sq, sq, vec, full],
        out_specs=[full, sq, vec],
        out_shape=[
            jax.ShapeDtypeStruct((S, POOL_W), BF16),
            jax.ShapeDtypeStruct((POOL_W, POOL_W), F32),
            jax.ShapeDtypeStruct((1, POOL_W), F32),
        ],
        compiler_params=_params(("arbitrary",)),
    )(rest, wbd, wbd_t, scale.reshape(1, POOL_W), dpa)


def _log_sigmoid(z):
    return jnp.minimum(z, 0.0) - jnp.log(1.0 + jnp.exp(-jnp.abs(z)))


_F_SPEC_COL = OFF_F // F_LANES


def _fox_prep(rest, bpad, *, name):
    S = rest.shape[0]

    def body(f_ref, b_ref, o_ref, ot_ref):
        acc = _log_sigmoid(f_ref[...] + b_ref[...])
        row = _rows(acc.shape)
        k = 1
        while k < S:
            acc = acc + jnp.where(row >= k, pltpu.roll(acc, k, 0), 0.0)
            k *= 2
        o_ref[...] = acc
        ot_ref[...] = acc.T

    return pl.pallas_call(
        body,
        name=name,
        grid=(1,),
        in_specs=[pl.BlockSpec((S, F_LANES), lambda i: (0, _F_SPEC_COL)), pl.BlockSpec((1, F_LANES), lambda i: (0, 0))],
        out_specs=[pl.BlockSpec((S, F_LANES), lambda i: (0, 0)), pl.BlockSpec((F_LANES, S), lambda i: (0, 0))],
        out_shape=[jax.ShapeDtypeStruct((S, F_LANES), F32), jax.ShapeDtypeStruct((F_LANES, S), F32)],
        compiler_params=_params(("arbitrary",)),
    )(rest, bpad)


def _fox_post(rest, bpad, dcum, *, name):
    S = rest.shape[0]

    def body(f_ref, b_ref, d_ref, df_ref, db_ref):
        acc = d_ref[...]
        row = _rows(acc.shape)
        k = 1
        while k < S:
            acc = acc + jnp.where(row < S - k, pltpu.roll(acc, S - k, 0), 0.0)
            k *= 2
        df = acc * (1.0 - _sigmoid(f_ref[...] + b_ref[...]))
        df_ref[...] = df.astype(BF16)
        db_ref[...] = jnp.sum(df, axis=0, keepdims=True)

    full = pl.BlockSpec((S, F_LANES), lambda i: (0, 0))
    vec = pl.BlockSpec((1, F_LANES), lambda i: (0, 0))
    return pl.pallas_call(
        body,
        name=name,
        grid=(1,),
        in_specs=[pl.BlockSpec((S, F_LANES), lambda i: (0, _F_SPEC_COL)), vec, full],
        out_specs=[full, vec],
        out_shape=[jax.ShapeDtypeStruct((S, F_LANES), BF16), jax.ShapeDtypeStruct((1, F_LANES), F32)],
        compiler_params=_params(("arbitrary",)),
    )(rest, bpad, dcum)


_FOX_SCALE = FOX_DH ** -0.5
_PAIRS = FOX_H // 2


def _scaled(v):
    return (v.astype(F32) * _FOX_SCALE).astype(BF16)


def _diag_mask(s):
    return jnp.where(_rows(s.shape) >= _lanes(s.shape), s, NEG)


def _fox_fwd(qkv, cum, fk3, *, name):
    S = qkv.shape[0]
    nk, t = fk3.shape[1:]

    def body(q_ref, k_ref, v_ref, cum_ref, fk_ref, o_ref, lse_ref):
        i = pl.program_id(0)
        lane = _lanes((t, 128))
        lo = lane < FOX_DH
        cumv = cum_ref[...]
        qm, fq = [], []
        for h in range(FOX_H):
            qs = _scaled(q_ref[:, 128 * (h // 2):128 * (h // 2 + 1)])
            zero = jnp.zeros_like(qs)
            qm.append(jnp.where(lo, qs, zero) if h % 2 == 0 else jnp.where(lo, zero, qs))
            fq.append(cumv[:, h:h + 1])

        def tile(j, state, masked):
            m, acc, lsum = (list(part) for part in state)
            k0 = pl.multiple_of(j * t, t)
            for hp in range(_PAIRS):
                cols = slice(128 * hp, 128 * (hp + 1))
                kb = k_ref[pl.ds(k0, t), cols]
                vb = v_ref[pl.ds(k0, t), cols]
                one = jnp.ones_like(vb)
                alphas, pvs = [], []
                for h in (2 * hp, 2 * hp + 1):
                    s = _dot(qm[h], kb, 1, 1) + fq[h] - fk_ref[h, pl.ds(j, 1), :]
                    if masked:
                        s = _diag_mask(s)
                    m_new = jnp.maximum(m[h], jnp.max(s, axis=-1, keepdims=True))
                    p = jnp.exp(s - m_new)
                    alphas.append(jnp.exp(m[h] - m_new))
                    m[h] = m_new
                    pvs.append(_dot(p.astype(BF16), jnp.where(lo, vb, one) if h % 2 == 0 else jnp.where(lo, one, vb), 1, 0))
                acc[hp] = jnp.where(lo, alphas[0], alphas[1]) * acc[hp] + jnp.where(lo, pvs[0], pvs[1])
                lsum[hp] = jnp.where(lo, alphas[1], alphas[0]) * lsum[hp] + jnp.where(lo, pvs[1], pvs[0])
            return tuple(m), tuple(acc), tuple(lsum)

        zeros = (jnp.zeros((t, 128), F32),) * _PAIRS
        state = lax.fori_loop(0, i, functools.partial(tile, masked=False), ((jnp.full((t, 1), NEG, F32),) * FOX_H, zeros, zeros))
        m, acc, lsum = tile(i, state, True)
        for hp in range(_PAIRS):
            o_ref[:, 128 * hp:128 * (hp + 1)] = acc[hp] / pltpu.roll(lsum[hp], FOX_DH, 1)
            lse = [m[2 * hp] + jnp.log(lsum[hp][:, FOX_DH:FOX_DH + 1]), m[2 * hp + 1] + jnp.log(lsum[hp][:, 0:1])]
            lse_ref[hp] = jnp.where(lane == 0, lse[0], jnp.where(lane == 1, lse[1], 0.0))

    whole = lambda col: pl.BlockSpec((S, FOX_W), lambda i: (0, col))
    return pl.pallas_call(
        body,
        name=name,
        grid=(S // t,),
        in_specs=[
            pl.BlockSpec((t, FOX_W), lambda i: (i, 0)), whole(1), whole(2),
            pl.BlockSpec((t, F_LANES), lambda i: (i, 0)),
            pl.BlockSpec((FOX_H, nk, t), lambda i: (0, 0, 0)),
        ],
        out_specs=[pl.BlockSpec((t, FOX_W), lambda i: (i, 0)), pl.BlockSpec((_PAIRS, t, 128), lambda i: (0, i, 0))],
        out_shape=[jax.ShapeDtypeStruct((S, FOX_W), F32), jax.ShapeDtypeStruct((_PAIRS, S, 128), F32)],
        compiler_params=_params(("arbitrary",)),
    )(qkv, qkv, qkv, cum, fk3)


def _fox_bwd(qkv, cum, fk3, o, do, lse, *, name):
    S = qkv.shape[0]
    nk, t = fk3.shape[1:]
    q_at, k_at, v_at = 0, FOX_W, 2 * FOX_W

    def body(qkv_ref, cum_ref, fk_ref, o_ref, do_ref, lse_ref, dq_ref, dk_ref, dv_ref, dfq_ref, dfk_ref,
             qs_sc, ks_sc, delta_sc, dq_sc):
        lane = _lanes((t, 128))
        lo = lane < FOX_DH
        mine = lambda h: lo if h % 2 == 0 else jnp.logical_not(lo)

        def by_head(tile, values):
            for h, val in enumerate(values):
                tile = jnp.where(lane == h, val, tile)
            return tile

        def prep(i, carry):
            r = pl.ds(pl.multiple_of(i * t, t), t)
            qs_sc[r, :] = _scaled(qkv_ref[r, q_at:q_at + FOX_W])
            ks_sc[r, :] = _scaled(qkv_ref[r, k_at:k_at + FOX_W])
            sums = []
            for hp in range(_PAIRS):
                cols = slice(128 * hp, 128 * (hp + 1))
                prod = do_ref[r, cols].astype(F32) * o_ref[r, cols]
                sums += [jnp.sum(jnp.where(mine(h), prod, 0.0), axis=-1, keepdims=True) for h in (2 * hp, 2 * hp + 1)]
            delta_sc[r, :] = by_head(jnp.zeros((t, 128), F32), sums)
            dfq_ref[r, :] = jnp.zeros((t, 128), F32)
            dq_sc[r, :] = jnp.zeros((t, FOX_W), F32)
            return carry

        lax.fori_loop(0, nk, prep, 0)

        def kv_tile(j, carry):
            kr = pl.ds(pl.multiple_of(j * t, t), t)

            def q_tile(i, acc, masked):
                dk, dv, dfk = list(acc[:_PAIRS]), list(acc[_PAIRS:2 * _PAIRS]), list(acc[2 * _PAIRS:])
                qr = pl.ds(pl.multiple_of(i * t, t), t)
                delta_t, cum_t, dq_old, dfq_old = delta_sc[qr, :], cum_ref[qr, :], dq_sc[qr, :], dfq_ref[qr, :]
                row_sums, dq_new = [], []
                for hp in range(_PAIRS):
                    cols = slice(128 * hp, 128 * (hp + 1))
                    kb = qkv_ref[kr, k_at + 128 * hp:k_at + 128 * (hp + 1)]
                    vb = qkv_ref[kr, v_at + 128 * hp:v_at + 128 * (hp + 1)]
                    ksb, qsb, dob = ks_sc[kr, cols], qs_sc[qr, cols], do_ref[qr, cols]
                    zero = jnp.zeros_like(qsb)
                    dq_t = jnp.zeros((t, 128), F32)
                    for h in (2 * hp, 2 * hp + 1):
                        qe, doe, ke = (jnp.where(mine(h), a, zero) for a in (qsb, dob, ksb))
                        s = _dot(qe, kb, 1, 1) + cum_t[:, h:h + 1] - fk_ref[h, pl.ds(j, 1), :]
                        if masked:
                            s = _diag_mask(s)
                        p = jnp.exp(s - lse_ref[hp, qr, h % 2:h % 2 + 1])
                        dv[hp] = dv[hp] + _dot(p.astype(BF16), doe, 0, 0)
                        dp = _dot(doe, vb, 1, 1)
                        ds = p * (dp - delta_t[:, h:h + 1])
                        dsb = ds.astype(BF16)
                        dk[hp] = dk[hp] + _dot(dsb, qe, 0, 0)
                        dq_t = dq_t + _dot(dsb, ke, 1, 0)
                        row_sums.append(jnp.sum(ds, axis=-1, keepdims=True))
                        dfk[h] = dfk[h] - jnp.sum(ds, axis=0, keepdims=True)
                    dq_new.append(dq_old[:, cols] + dq_t)
                for hp in range(_PAIRS):
                    dq_sc[qr, 128 * hp:128 * (hp + 1)] = dq_new[hp]
                dfq_ref[qr, :] = dfq_old + by_head(jnp.zeros((t, 128), F32), row_sums)
                return (*dk, *dv, *dfk)

            init = tuple([jnp.zeros((t, 128), F32)] * (2 * _PAIRS) + [jnp.zeros((1, t), F32)] * FOX_H)
            acc = q_tile(j, init, True)
            acc = lax.fori_loop(j + 1, nk, functools.partial(q_tile, masked=False), acc)
            for hp in range(_PAIRS):
                cols = slice(128 * hp, 128 * (hp + 1))
                dk_ref[kr, cols] = acc[hp].astype(BF16)
                dv_ref[kr, cols] = acc[_PAIRS + hp].astype(BF16)
            for h in range(FOX_H):
                dfk_ref[h, pl.ds(j, 1), :] = acc[2 * _PAIRS + h]
            return carry

        lax.fori_loop(0, nk, kv_tile, 0)
        dq_ref[...] = dq_sc[...].astype(BF16)

    vm = pl.BlockSpec(memory_space=pltpu.VMEM)
    big = jax.ShapeDtypeStruct((S, FOX_W), BF16)
    return pl.pallas_call(
        body,
        name=name,
        in_specs=[vm] * 6,
        out_specs=[vm] * 5,
        out_shape=[big, big, big, jax.ShapeDtypeStruct((S, 128), F32), jax.ShapeDtypeStruct((FOX_H, nk, t), F32)],
        scratch_shapes=[pltpu.VMEM((S, FOX_W), BF16), pltpu.VMEM((S, FOX_W), BF16), pltpu.VMEM((S, 128), F32),
                        pltpu.VMEM((S, FOX_W), F32)],
        compiler_params=pltpu.CompilerParams(vmem_limit_bytes=VMEM_LIMIT),
    )(qkv, cum, fk3, o, do, lse)


def _group_mask(lane, gi):
    return (lane >= 64 * gi) & (lane < 64 * (gi + 1))


_U_COL = OFF_C // SGU_W


def _sgu_fwd(rest, gn, wm, bias, *, name):
    S = rest.shape[0]
    ts = _tile(S, 512)
    nc = ts // SGU_CHUNK

    def body(u_ref, v_ref, g_ref, w_ref, b_ref, o_ref):
        zv = _gelu(v_ref[...])
        vn = zv * lax.rsqrt(jnp.mean(zv * zv, axis=-1, keepdims=True) + EPS) * g_ref[...]
        lane = _lanes((SGU_CHUNK, SGU_W))
        for c in range(nc):
            rows = slice(c * SGU_CHUNK, (c + 1) * SGU_CHUNK)
            vcb = vn[rows].astype(BF16)
            mixed = b_ref[...]
            for gi in range(4):
                mixed = mixed + jnp.where(_group_mask(lane, gi), _dot(w_ref[gi], vcb, 1, 0), 0.0)
            o_ref[rows, :] = (_gelu(u_ref[rows, :]) * mixed).astype(BF16)

    return pl.pallas_call(
        body,
        name=name,
        grid=(S // ts,),
        in_specs=[
            pl.BlockSpec((ts, SGU_W), lambda i: (i, _U_COL)),
            pl.BlockSpec((ts, SGU_W), lambda i: (i, _U_COL + 1)),
            pl.BlockSpec((1, SGU_W), lambda i: (0, 0)),
            pl.BlockSpec((4, SGU_CHUNK, SGU_CHUNK), lambda i: (0, 0, 0)),
            pl.BlockSpec((SGU_CHUNK, SGU_W), lambda i: (0, 0)),
        ],
        out_specs=pl.BlockSpec((ts, SGU_W), lambda i: (i, 0)),
        out_shape=jax.ShapeDtypeStruct((S, SGU_W), BF16),
        compiler_params=_params(("parallel",)),
    )(rest, rest, gn.reshape(1, SGU_W), wm, bias)


def _sgu_bwd(rest, gn, wm, wm_t, bias, dsg, *, name):
    S = rest.shape[0]
    ts = _tile(S, 512)
    nc = ts // SGU_CHUNK

    def body(u_ref, v_ref, g_ref, w_ref, wt_ref, b_ref, dsg_ref, dc_ref, dw_ref, db_ref, dg_ref):
        first = pl.program_id(0) == 0

        @pl.when(first)
        def _():
            dw_ref[...] = jnp.zeros_like(dw_ref)
            db_ref[...] = jnp.zeros_like(db_ref)
            dg_ref[...] = jnp.zeros_like(dg_ref)

        gv = g_ref[...]
        lane = _lanes((SGU_CHUNK, SGU_W))
        for c in range(nc):
            rows = slice(c * SGU_CHUNK, (c + 1) * SGU_CHUNK)
            vpre = v_ref[rows, :]
            upre = u_ref[rows, :]
            zv = _gelu(vpre)
            r = lax.rsqrt(jnp.mean(zv * zv, axis=-1, keepdims=True) + EPS)
            zn = zv * r
            vcb = (zn * gv).astype(BF16)
            mixed = b_ref[...]
            for gi in range(4):
                mixed = mixed + jnp.where(_group_mask(lane, gi), _dot(w_ref[gi], vcb, 1, 0), 0.0)
            zu = _gelu(upre)
            dsg_v = dsg_ref[rows, :]
            dc_ref[rows, :SGU_W] = (dsg_v * mixed * _gelu_grad(upre)).astype(BF16)
            dmixed = dsg_v * zu
            db_ref[...] += dmixed
            dvn = jnp.zeros((SGU_CHUNK, SGU_W), F32)
            for gi in range(4):
                dmg = jnp.where(_group_mask(lane, gi), dmixed, 0.0).astype(BF16)
                dw_ref[gi] += _dot(dmg, vcb, 1, 1)
                dvn = dvn + _dot(wt_ref[gi], dmg, 1, 0)
            dg_ref[...] += jnp.sum(dvn * zn, axis=0, keepdims=True)
            dzn = dvn * gv
            dzv = r * (dzn - zn * jnp.mean(dzn * zn, axis=-1, keepdims=True))
            dc_ref[rows, SGU_W:] = (dzv * _gelu_grad(vpre)).astype(BF16)

    blk = pl.BlockSpec((ts, SGU_W), lambda i: (i, 0))
    vec = pl.BlockSpec((1, SGU_W), lambda i: (0, 0))
    w3 = pl.BlockSpec((4, SGU_CHUNK, SGU_CHUNK), lambda i: (0, 0, 0))
    bsp = pl.BlockSpec((SGU_CHUNK, SGU_W), lambda i: (0, 0))
    return pl.pallas_call(
        body,
        name=name,
        grid=(S // ts,),
        in_specs=[
            pl.BlockSpec((ts, SGU_W), lambda i: (i, _U_COL)),
            pl.BlockSpec((ts, SGU_W), lambda i: (i, _U_COL + 1)),
            vec, w3, w3, bsp, blk,
        ],
        out_specs=[pl.BlockSpec((ts, 2 * SGU_W), lambda i: (i, 0)), w3, bsp, vec],
        out_shape=[
            jax.ShapeDtypeStruct((S, 2 * SGU_W), BF16),
            jax.ShapeDtypeStruct((4, SGU_CHUNK, SGU_CHUNK), F32),
            jax.ShapeDtypeStruct((SGU_CHUNK, SGU_W), F32),
            jax.ShapeDtypeStruct((1, SGU_W), F32),
        ],
        compiler_params=_params(("arbitrary",)),
    )(rest, rest, gn.reshape(1, SGU_W), wm, wm_t, bias, dsg)


_GT = 512
_G0 = OFF_G // _GT


def _gate_specs(tm, col_of):
    specs = [pl.BlockSpec((tm, _GT), functools.partial(lambda k, *ids: (col_of(*ids)[0], _G0 + 2 * k + col_of(*ids)[1]), k)) for k in range(3)]
    specs += [pl.BlockSpec((1, _GT), functools.partial(lambda k, *ids: (0, 2 * k + col_of(*ids)[1]), k)) for k in range(3)]
    return specs


def _merge_fwd(rest, bg, ya, yb, yc, *, name):
    S = rest.shape[0]
    tm = _tile(S, 512)

    def body(g1, g2, g3, b1, b2, b3, ya_ref, yb_ref, yc_ref, o_ref):
        acc = _sigmoid(g1[...] + b1[...]) * ya_ref[...]
        acc = acc + _sigmoid(g2[...] + b2[...]) * yb_ref[...]
        acc = acc + _sigmoid(g3[...] + b3[...]) * yc_ref[...]
        o_ref[...] = acc.astype(BF16)

    blk = pl.BlockSpec((tm, _GT), lambda i, j: (i, j))
    return pl.pallas_call(
        body,
        name=name,
        grid=(S // tm, D // _GT),
        in_specs=_gate_specs(tm, lambda i, j: (i, j)) + [blk, blk, blk],
        out_specs=blk,
        out_shape=jax.ShapeDtypeStruct((S, D), BF16),
        compiler_params=_params(("parallel", "parallel")),
    )(rest, rest, rest, bg, bg, bg, ya, yb, yc)


def _merge_bwd(rest, bg, ya, yb, yc, dm, *, name):
    S = rest.shape[0]
    tm = _tile(S, 512)

    def body(g1, g2, g3, b1, b2, b3, ya_ref, yb_ref, yc_ref, dm_ref, dya, dyb, dyc, dg1, dg2, dg3, db1, db2, db3):
        first = pl.program_id(1) == 0
        dmv = dm_ref[...]
        for g_ref, b_ref, y_ref, dy_ref, dg_ref, db_ref in (
            (g1, b1, ya_ref, dya, dg1, db1), (g2, b2, yb_ref, dyb, dg2, db2), (g3, b3, yc_ref, dyc, dg3, db3)):
            gate = _sigmoid(g_ref[...] + b_ref[...])
            dy_ref[...] = (dmv * gate).astype(BF16)
            dpre = dmv * y_ref[...] * gate * (1.0 - gate)
            dg_ref[...] = dpre.astype(BF16)
            part = jnp.sum(dpre, axis=0, keepdims=True)

            @pl.when(first)
            def _():
                db_ref[...] = part

            @pl.when(jnp.logical_not(first))
            def _():
                db_ref[...] += part

    blk = pl.BlockSpec((tm, _GT), lambda j, i: (i, j))
    vec = pl.BlockSpec((1, _GT), lambda j, i: (0, j))
    big = jax.ShapeDtypeStruct((S, D), BF16)
    small = jax.ShapeDtypeStruct((1, D), F32)
    return pl.pallas_call(
        body,
        name=name,
        grid=(D // _GT, S // tm),
        in_specs=_gate_specs(tm, lambda j, i: (i, j)) + [blk, blk, blk, blk],
        out_specs=[blk] * 6 + [vec] * 3,
        out_shape=[big] * 6 + [small] * 3,
        compiler_params=_params(("parallel", "arbitrary")),
    )(rest, rest, rest, bg, bg, bg, ya, yb, yc, dm)


_X_SCALE = XDH ** -0.5


def _xattn_fwd(xq, kv, *, name):
    S = xq.shape[0]
    M = kv.shape[0]
    tq = _tile(S, 512)

    def body(q_ref, k_ref, v_ref, o_ref):
        s = _dot(q_ref[...], k_ref[...], 1, 1) * _X_SCALE
        e = jnp.exp(s - jnp.max(s, axis=-1, keepdims=True))
        p = e / jnp.sum(e, axis=-1, keepdims=True)
        o_ref[...] = _dot(p.astype(BF16), v_ref[...], 1, 0).astype(BF16)

    return pl.pallas_call(
        body,
        name=name,
        grid=(S // tq, XH),
        in_specs=[
            pl.BlockSpec((tq, XDH), lambda i, h: (i, h)),
            pl.BlockSpec((M, XDH), lambda i, h: (0, h)),
            pl.BlockSpec((M, XDH), lambda i, h: (0, XH + h)),
        ],
        out_specs=pl.BlockSpec((tq, XDH), lambda i, h: (i, h)),
        out_shape=jax.ShapeDtypeStruct((S, D), BF16),
        compiler_params=_params(("parallel", "parallel")),
    )(xq, kv, kv)


def _xattn_bwd(xq, kv, do, *, name):
    S = xq.shape[0]
    M = kv.shape[0]
    tq = _tile(S, 512)

    def body(q_ref, k_ref, v_ref, do_ref, dq_ref, dk_ref, dv_ref):
        qb = q_ref[...]
        kb = k_ref[...]
        dob = do_ref[...]
        s = _dot(qb, kb, 1, 1) * _X_SCALE
        e = jnp.exp(s - jnp.max(s, axis=-1, keepdims=True))
        p = e / jnp.sum(e, axis=-1, keepdims=True)
        dp = _dot(dob, v_ref[...], 1, 1)
        ds = (p * (dp - jnp.sum(p * dp, axis=-1, keepdims=True)) * _X_SCALE).astype(BF16)
        dq_ref[...] = _dot(ds, kb, 1, 0).astype(BF16)
        dk_part = _dot(ds, qb, 0, 0)
        dv_part = _dot(p.astype(BF16), dob, 0, 0)

        @pl.when(pl.program_id(1) == 0)
        def _():
            dk_ref[...] = dk_part
            dv_ref[...] = dv_part

        @pl.when(pl.program_id(1) > 0)
        def _():
            dk_ref[...] += dk_part
            dv_ref[...] += dv_part

    qspec = pl.BlockSpec((tq, XDH), lambda h, i: (i, h))
    kspec = pl.BlockSpec((M, XDH), lambda h, i: (0, h))
    dxq, dxk, dxv = pl.pallas_call(
        body,
        name=name,
        grid=(XH, S // tq),
        in_specs=[qspec, kspec, pl.BlockSpec((M, XDH), lambda h, i: (0, XH + h)), qspec],
        out_specs=[qspec, kspec, kspec],
        out_shape=[jax.ShapeDtypeStruct((S, D), BF16), jax.ShapeDtypeStruct((M, D), F32), jax.ShapeDtypeStruct((M, D), F32)],
        compiler_params=_params(("parallel", "arbitrary")),
    )(xq, kv, kv, do)
    return dxq, jnp.concatenate([dxk, dxv], axis=1)


def _adam_math(w, g, m, v):
    m = ADAM_B1 * m + (1.0 - ADAM_B1) * g
    v = ADAM_B2 * v + (1.0 - ADAM_B2) * (g * g)
    m_hat = m / (1.0 - ADAM_B1 ** ADAM_STEP)
    v_hat = v / (1.0 - ADAM_B2 ** ADAM_STEP)
    delta = -ADAM_LR * (m_hat / (jnp.sqrt(v_hat) + ADAM_EPS) + ADAM_WD * w)
    return delta, m, v


def _adamw_sharded(parts, w, m, v, *, name):
    _, R, C = w.shape
    Cp = parts[0].shape[2]
    tm = _tile(R, 256)
    nr = R // tm

    def body(p0_ref, p1_ref, w_ref, m_ref, v_ref, g_ref, d_ref, mo_ref, vo_ref):
        def update(p_ref):
            g = p_ref[0][:, :C].astype(F32)
            for dev in range(1, N_DEV):
                g = g + p_ref[dev][:, :C].astype(F32)
            delta, mn, vn = _adam_math(w_ref[...], g, m_ref[...], v_ref[...])
            g_ref[...] = g
            d_ref[...] = delta
            mo_ref[...] = mn
            vo_ref[...] = vn

        @pl.when(pl.program_id(0) == 0)
        def _():
            update(p0_ref)

        @pl.when(pl.program_id(0) == 1)
        def _():
            update(p1_ref)

    p0 = pl.BlockSpec((N_DEV, tm, Cp), lambda l, i: (0, i * (1 - l) + (nr - 1) * l, 0))
    p1 = pl.BlockSpec((N_DEV, tm, Cp), lambda l, i: (0, i * l, 0))
    blk = pl.BlockSpec((None, tm, C), lambda l, i: (l, i, 0))
    sds = jax.ShapeDtypeStruct(w.shape, F32)
    return pl.pallas_call(
        body,
        name=name,
        grid=(DEPTH, nr),
        in_specs=[p0, p1, blk, blk, blk],
        out_specs=[blk] * 4,
        out_shape=[sds] * 4,
        compiler_params=_params(("arbitrary", "arbitrary")),
    )(parts[0], parts[1], w, m, v)


def _adamw_small(g, w, m, v, *, name):
    n = len(g)

    def body(*refs):
        g_refs, w_refs, m_refs, v_refs = (refs[k * n:(k + 1) * n] for k in range(4))
        d_out, m_out, v_out = (refs[(4 + k) * n:(5 + k) * n] for k in range(3))
        for t in range(n):
            delta, mn, vn = _adam_math(w_refs[t][...], g_refs[t][...], m_refs[t][...], v_refs[t][...])
            d_out[t][...] = delta
            m_out[t][...] = mn
            v_out[t][...] = vn

    vm = pl.BlockSpec(memory_space=pltpu.VMEM)
    shapes = [jax.ShapeDtypeStruct(a.shape, F32) for a in w]
    outs = pl.pallas_call(
        body,
        name=name,
        in_specs=[vm] * (4 * n),
        out_specs=[vm] * (3 * n),
        out_shape=shapes * 3,
        compiler_params=pltpu.CompilerParams(vmem_limit_bytes=VMEM_LIMIT),
    )(*g, *w, *m, *v)
    return outs[:n], outs[n:2 * n], outs[2 * n:]


def _position():
    return lax.axis_index("x"), lax.axis_index("y"), lax.axis_index("c")


def _dev_index(px, py, pc):
    return 4 * px + 2 * py + pc


_ANY = pl.BlockSpec(memory_space=pl.ANY)


def _all_gather(shards, *, name):
    n = len(shards)
    out_shape = [jax.ShapeDtypeStruct((N_DEV, *s.shape), s.dtype) for s in shards]
    n_pieces = len(_pieces(out_shape))

    def body(*refs):
        ins, outs = refs[:n], refs[n:2 * n]
        send_sems, recv_sems, local_sems = refs[2 * n:]
        x, y, c = _position()
        me, sibling = (x, y, c), (x, y, 1 - c)
        chips = [(1 - x, y), (x, 1 - y), (1 - x, 1 - y)]
        pieces = _pieces(outs)

        def copy(i, k, block, to, from_input=False):
            t, rows = pieces[i]
            dst = _cut(outs[t].at[_dev_index(*block)], rows)
            return pltpu.make_async_remote_copy(
                src_ref=_cut(ins[t], rows) if from_input else dst, dst_ref=dst, send_sem=send_sems.at[i, k],
                recv_sem=recv_sems.at[i, k], device_id=to, device_id_type=MESH)

        mine = [pltpu.make_async_copy(_cut(ins[t], rows), _cut(outs[t].at[_dev_index(*me)], rows), local_sems.at[i])
                for i, (t, rows) in enumerate(pieces)]
        for cp in mine:
            cp.start()
        started = []
        for j, chip in enumerate(chips):
            for i in range(n_pieces):
                started.append(copy(i, 1 + j, me, (*chip, c), from_input=True))
                started[-1].start()
        for i in range(n_pieces):
            started.append(copy(i, 0, me, sibling, from_input=True))
            started[-1].start()
        for j, chip in enumerate(chips):
            for i in range(n_pieces):
                copy(i, 1 + j, (*chip, c), me).wait_recv()
                started.append(copy(i, 4 + j, (*chip, c), sibling))
                started[-1].start()
        for i in range(n_pieces):
            copy(i, 0, sibling, me).wait_recv()
        for j, chip in enumerate(chips):
            for i in range(n_pieces):
                copy(i, 4 + j, (*chip, 1 - c), me).wait_recv()
        for cp in started:
            cp.wait_send()
        for cp in mine:
            cp.wait()

    return pl.pallas_call(
        body,
        name=name,
        in_specs=[_ANY] * n,
        out_specs=[_ANY] * n,
        out_shape=out_shape,
        scratch_shapes=[pltpu.SemaphoreType.DMA((n_pieces, 7)), pltpu.SemaphoreType.DMA((n_pieces, 7)),
                        pltpu.SemaphoreType.DMA((n_pieces,))],
        compiler_params=pltpu.CompilerParams(has_side_effects=True),
    )(*shards)


def _peers(x, y, c):
    out = []
    for mask in range(1, N_DEV):
        fx, fy, fc = (mask >> 2) & 1, (mask >> 1) & 1, mask & 1
        out.append((1 - x if fx else x, 1 - y if fy else y, 1 - c if fc else c))
    return out


_HBM = pl.BlockSpec(memory_space=pltpu.HBM)
_SEM = pl.BlockSpec(memory_space=pltpu.SEMAPHORE)


def _own_block_placed(block, like):
    x, y, c = _position()
    return lax.dynamic_update_index_in_dim(lax.empty(like.shape, like.dtype), block, _dev_index(x, y, c), 0)


_COPY_BYTES = 256 << 10
_MAX_PIECES = 8


def _pieces(blocks):
    out = []
    for t, b in enumerate(blocks):
        R, C = b.shape[-2:]
        n = max(1, min(_MAX_PIECES, R * C * jnp.dtype(b.dtype).itemsize // _COPY_BYTES))
        while n > 1 and R % (16 * n):
            n -= 1
        out += [(t, pl.ds(j * (R // n), R // n) if n > 1 else None) for j in range(n)]
    return out


def _cut(block, rows):
    return block if rows is None else block.at[rows]


def _copies(per_piece):
    def mark(fn):
        fn.per_piece = per_piece
        return fn
    return mark


@_copies(N_DEV - 1)
def _plan_exchange(srcs, lands, send_sems, recv_sems, arrivals):
    x, y, c = _position()
    me = _dev_index(x, y, c)
    out = []
    for k, peer in enumerate(_peers(x, y, c)):
        p = _dev_index(*peer)
        for i, (t, rows) in enumerate(_pieces(lands)):
            sems = dict(send_sem=send_sems.at[7 * i + k], recv_sem=recv_sems.at[7 * i + k], device_id=peer, device_id_type=MESH)
            src, dst = (lands[t].at[p], lands[t].at[p]) if arrivals else (srcs[t].at[p], lands[t].at[me])
            out.append(pltpu.make_async_remote_copy(src_ref=_cut(src, rows), dst_ref=_cut(dst, rows), **sems))
    return out


@_copies(N_DEV - 1)
def _plan_broadcast(srcs, lands, send_sems, recv_sems, arrivals):
    x, y, c = _position()
    me = _dev_index(x, y, c)
    out = []
    for k, peer in enumerate(_peers(x, y, c)):
        p = _dev_index(*peer)
        for i, (t, rows) in enumerate(_pieces(lands)):
            sems = dict(send_sem=send_sems.at[7 * i + k], recv_sem=recv_sems.at[7 * i + k], device_id=peer, device_id_type=MESH)
            src, dst = (lands[t].at[p], lands[t].at[p]) if arrivals else (srcs[t], lands[t].at[me])
            out.append(pltpu.make_async_remote_copy(src_ref=_cut(src, rows), dst_ref=_cut(dst, rows), **sems))
    return out


@_copies(4)
def _plan_gather_out(srcs, lands, send_sems, recv_sems, arrivals):
    x, y, c = _position()
    me = _dev_index(x, y, c)
    out = []
    for k, peer in enumerate([(x, y, 1 - c), (1 - x, y, c), (x, 1 - y, c), (1 - x, 1 - y, c)]):
        p = _dev_index(*peer)
        for i, (t, rows) in enumerate(_pieces(lands)):
            sems = dict(send_sem=send_sems.at[4 * i + k], recv_sem=recv_sems.at[4 * i + k], device_id=peer, device_id_type=MESH)
            src, dst = (lands[t].at[p], lands[t].at[p]) if arrivals else (srcs[t], lands[t].at[me])
            out.append(pltpu.make_async_remote_copy(src_ref=_cut(src, rows), dst_ref=_cut(dst, rows), **sems))
    return out


@_copies(3)
def _plan_gather_pass(srcs, lands, send_sems, recv_sems, arrivals):
    x, y, c = _position()
    sibling = (x, y, 1 - c)
    out = []
    for k, chip in enumerate([(1 - x, y), (x, 1 - y), (1 - x, 1 - y)]):
        p = _dev_index(*chip, 1 - c) if arrivals else _dev_index(*chip, c)
        for i, (t, rows) in enumerate(_pieces(lands)):
            sems = dict(send_sem=send_sems.at[3 * i + k], recv_sem=recv_sems.at[3 * i + k], device_id=sibling, device_id_type=MESH)
            block = _cut(lands[t].at[p], rows)
            out.append(pltpu.make_async_remote_copy(src_ref=block, dst_ref=block, **sems))
    return out


def _split_start(plan, srcs, lands, *, after=None, name):
    n_src, n = len(srcs), len(srcs) + len(lands)
    n_sem = plan.per_piece * len(_pieces(lands))
    order = [] if after is None else [after]

    def body(*refs):
        send_sems, recv_sems = refs[n + len(order):n + len(order) + 2]
        token = refs[-1]
        for cp in plan(refs[:n_src], refs[n_src:n], send_sems, recv_sems, arrivals=False):
            cp.start()
        token[...] = jnp.zeros_like(token)

    hbm = lambda a: pltpu.HBM(a.shape, a.dtype)
    outs = pl.pallas_call(
        body,
        name=name,
        in_specs=[_HBM] * n + [_ANY] * len(order),
        out_specs=[_SEM, _SEM] + [_HBM] * n + [pl.BlockSpec(memory_space=pltpu.VMEM)],
        out_shape=[pltpu.SemaphoreType.DMA((n_sem,)), pltpu.SemaphoreType.DMA((n_sem,))] + [hbm(a) for a in (*srcs, *lands)]
        + [jax.ShapeDtypeStruct(_TOKEN, F32)],
        input_output_aliases={i: 2 + i for i in range(n)},
        compiler_params=pltpu.CompilerParams(has_side_effects=pltpu.SideEffectType.DATAFLOW_SIDE_EFFECTING),
    )(*[pltpu.with_memory_space_constraint(a, pltpu.HBM) for a in (*srcs, *lands)], *order)
    return (outs[0], outs[1], outs[2:2 + n_src], outs[2 + n_src:2 + n]), outs[-1]


def _split_wait(plan, state, after, *, name):
    send_sems, recv_sems, srcs, lands = state
    n_src, n = len(srcs), len(srcs) + len(lands)

    def body(*refs):
        send_refs, recv_refs = refs[n:n + 2]
        for cp in plan(refs[:n_src], refs[n_src:n], send_refs, recv_refs, arrivals=False):
            cp.wait_send()
        for cp in plan(refs[:n_src], refs[n_src:n], send_refs, recv_refs, arrivals=True):
            cp.wait_recv()

    hbm = lambda a: pltpu.HBM(a.shape, a.dtype)
    outs = pl.pallas_call(
        body,
        name=name,
        in_specs=[_HBM] * n + [_SEM, _SEM, _ANY],
        out_specs=[_HBM] * n,
        out_shape=[hbm(a) for a in (*srcs, *lands)],
        input_output_aliases={i: i for i in range(n)},
        compiler_params=pltpu.CompilerParams(has_side_effects=pltpu.SideEffectType.DATAFLOW_SIDE_EFFECTING),
    )(*srcs, *lands, send_sems, recv_sems, after)
    return outs[n_src:]


def _sum_blocks(blocks, *, name):
    _, R, C = blocks.shape
    tm = next(R // n for n in (4, 3, 2, 1) if R % (8 * n) == 0)

    def body(b_ref, o_ref):
        g = b_ref[0]
        for dev in range(1, N_DEV):
            g = g + b_ref[dev]
        o_ref[...] = g

    return pl.pallas_call(
        body,
        name=name,
        grid=(R // tm,),
        in_specs=[pl.BlockSpec((N_DEV, tm, C), lambda i: (0, i, 0))],
        out_specs=pl.BlockSpec((tm, C), lambda i: (i, 0)),
        out_shape=jax.ShapeDtypeStruct((R, C), F32),
        compiler_params=_params(("parallel",)),
    )(blocks)


def _block_diag(w):
    out = jnp.zeros((POOL_W, POOL_W), w.dtype)
    for gi in range(4):
        out = out.at[64 * gi:64 * (gi + 1), 64 * gi:64 * (gi + 1)].set(w[gi])
    return out


def _layer_consts(sp, l):
    causal = jnp.tril(jnp.ones((SGU_CHUNK, SGU_CHUNK), F32))
    wm = (sp["sgu_w"][l] * causal[None]).astype(BF16)
    wbd = _block_diag(sp["pool_w"][l]).astype(BF16)
    return dict(
        wbd=wbd, wbd_t=wbd.T, wm=wm, wm_t=wm.transpose(0, 2, 1),
        sgu_bias=jnp.repeat(sp["sgu_b"][l].T, 64, axis=1),
        bpad=jnp.pad(sp["b_forget"][l], (0, F_LANES - FOX_H)).reshape(1, F_LANES),
        bg=sp["b_gate"][l].reshape(1, 3 * D),
    )


def _relu2(acc):
    return acc, jnp.square(jnp.maximum(acc, 0.0))


def _relu2_grad(acc, z):
    return (acc * 2.0 * jnp.maximum(z, 0.0),)


def _layer_fwd(l, x, h, mem, source, sp):
    S = x.shape[0]
    t = _tile(S, 256)
    c = _layer_consts(sp, l)
    n = f"l{l}_"
    W, after = source(l, "begin", x)
    if h is None:
        h, after = _rms_fwd(x, sp["norm_mix_g"][l], after=after, name=n + "norm_mix"), None
    qkv = _mm(h, W["qkv"], out_dtypes=(BF16,), after=after, name=n + "qkv")
    rest = _mm(h, W["rest"], name=n + "rest")
    pa = _pool_fwd(rest, c["wbd"], sp["pool_scale"][l], name=n + "pool")
    cum, cum_t = _fox_prep(rest, c["bpad"], name=n + "fox_prep")
    fk3 = cum_t[:FOX_H].reshape(FOX_H, S // t, t)
    o, lse = _fox_fwd(qkv, cum, fk3, name=n + "fox")
    more, _ = source(l, "attended", o)
    W.update(more)
    sg = _sgu_fwd(rest, sp["sgu_norm_g"][l], c["wm"], c["sgu_bias"], name=n + "sgu")
    more, after = source(l, "mixed", sg)
    W.update(more)
    ya = _mm(pa, W["ba"], out_dtypes=(BF16,), after=after, name=n + "branch_a")
    yb = _mm(o, W["bb"], out_dtypes=(BF16,), name=n + "branch_b")
    yc = _mm(sg, W["bc"], out_dtypes=(BF16,), name=n + "branch_c")
    merged = _merge_fwd(rest, c["bg"], ya, yb, yc, name=n + "merge")
    whole_rows = dict(epilogue=_add_norm, out_dtypes=(F32, BF16), tm=1024, tn=D)
    x1, hx = _mm(merged, W["out"], extras=(x,), row_extras=(sp["norm_xattn_g"][l].reshape(1, D),), name=n + "out", **whole_rows)
    hm = _rms_fwd(mem, sp["norm_mem_g"][l], name=n + "norm_mem")
    xq = _mm(hx, W["xq"], out_dtypes=(BF16,), name=n + "xq")
    kv = _mm(hm, W["xkv"], out_dtypes=(BF16,), name=n + "xkv")
    o2 = _xattn_fwd(xq, kv, name=n + "xattn")
    x2, hf = _mm(o2, W["xo"], extras=(x1,), row_extras=(sp["norm_ffn_g"][l].reshape(1, D),), name=n + "xo", **whole_rows)
    z, act = _mm(hf, W["ff1"], epilogue=_relu2, out_dtypes=(BF16, BF16), name=n + "ff1")
    _, after = source(l, "expanded", act)
    if l + 1 < DEPTH:
        x3, h_next = _mm(act, W["ff2"], extras=(x2,), row_extras=(sp["norm_mix_g"][l + 1].reshape(1, D),), after=after, name=n + "ff2",
                         **whole_rows)
    else:
        x3, h_next = _mm(act, W["ff2"], extras=(x2,), epilogue=_add, after=after, name=n + "ff2"), None
    saved = dict(x=x, h=h, qkv=qkv, rest=rest, pa=pa, cum=cum, fk3=fk3, o=o, lse=lse, sg=sg, ya=ya, yb=yb, yc=yc,
                 merged=merged, x1=x1, hx=hx, hm=hm, xq=xq, kv=kv, o2=o2, x2=x2, hf=hf, z=z, act=act, c=c)
    return x3, h_next, saved, W


def _layer_bwd(l, dx3, sv, mem, W, sp, grads_done):
    S = dx3.shape[0]
    c = sv["c"]
    n = f"l{l}b_"
    bf = dict(out_dtypes=(BF16,))
    gw, gs = {}, {}
    gw["ff2"] = _mm(sv["act"], dx3, ta=True, name=n + "dw_ff2", **bf)
    dz = _mm(dx3, W["ff2"], tb=True, extras=(sv["z"],), epilogue=_relu2_grad, name=n + "dz", **bf)
    gw["ff1"] = _mm(sv["hf"], dz, ta=True, shard_out=True, name=n + "dw_ff1", **bf)
    dhf = _mm(dz, W["ff1"], tb=True, name=n + "dhf")
    dx2, gs["norm_ffn_g"] = _rms_bwd(sv["x2"], sp["norm_ffn_g"][l], dhf, dx3, name=n + "dnorm_ffn")
    gw["xo"] = _mm(sv["o2"], dx2, ta=True, name=n + "dw_xo", **bf)
    do2 = _mm(dx2, W["xo"], tb=True, name=n + "do2", **bf)
    dxq, dkv = _xattn_bwd(sv["xq"], sv["kv"], do2, name=n + "dxattn")
    gw["xq"] = _mm(sv["hx"], dxq, ta=True, name=n + "dw_xq", **bf)
    gw["xkv"] = _mm(sv["hm"], dkv, ta=True, shard_out=True, name=n + "dw_xkv", **bf)
    dhm = _mm(dkv, W["xkv"], tb=True, name=n + "dhm")
    _, gs["norm_mem_g"] = _rms_bwd(mem, sp["norm_mem_g"][l], dhm, jnp.zeros_like(mem), name=n + "dnorm_mem")
    dhx = _mm(dxq, W["xq"], tb=True, name=n + "dhx")
    dx1, gs["norm_xattn_g"] = _rms_bwd(sv["x1"], sp["norm_xattn_g"][l], dhx, dx2, name=n + "dnorm_xattn")
    after, gw = grads_done(l, gw), {}
    gw["out"] = _mm(sv["merged"], dx1, ta=True, name=n + "dw_out", **bf)
    dm = _mm(dx1, W["out"], tb=True, after=after, name=n + "dmerged")
    dya, dyb, dyc, dg1, dg2, dg3, db1, db2, db3 = _merge_bwd(sv["rest"], c["bg"], sv["ya"], sv["yb"], sv["yc"], dm, name=n + "dmerge")
    gs["b_gate"] = jnp.concatenate([db1, db2, db3], axis=1).reshape(3 * D)
    gw["ba"] = _mm(sv["pa"], dya, ta=True, shard_out=True, name=n + "dw_ba", **bf)
    gw["bb"] = _mm(sv["o"], dyb, ta=True, shard_out=True, name=n + "dw_bb", **bf)
    gw["bc"] = _mm(sv["sg"], dyc, ta=True, shard_out=True, name=n + "dw_bc", **bf)
    after, gw = grads_done(l, gw), {}
    dpa = _mm(dya, W["ba"], tb=True, name=n + "dpa")
    do = _mm(dyb, W["bb"], tb=True, after=after, name=n + "do", **bf)
    dsg = _mm(dyc, W["bc"], tb=True, name=n + "dsg")
    da, dwbd, dscale = _pool_bwd(sv["rest"], c["wbd"], c["wbd_t"], sp["pool_scale"][l], dpa, name=n + "dpool")
    gs["pool_w"] = jnp.stack([dwbd[64 * gi:64 * (gi + 1), 64 * gi:64 * (gi + 1)] for gi in range(4)])
    gs["pool_scale"] = dscale.reshape(POOL_W)
    dq, dk, dv, dfq, dfk = _fox_bwd(sv["qkv"], sv["cum"], sv["fk3"], sv["o"], do, sv["lse"], name=n + "dfox")
    dcum = dfq + jnp.pad(dfk.reshape(FOX_H, S).T, ((0, 0), (0, F_LANES - FOX_H)))
    df, dbf = _fox_post(sv["rest"], c["bpad"], dcum, name=n + "dfox_post")
    gs["b_forget"] = dbf[0, :FOX_H]
    dc, dwm, dbias, dgn = _sgu_bwd(sv["rest"], sp["sgu_norm_g"][l], c["wm"], c["wm_t"], c["sgu_bias"], dsg, name=n + "dsgu")
    gs["sgu_w"] = dwm * jnp.tril(jnp.ones((SGU_CHUNK, SGU_CHUNK), F32))[None]
    gs["sgu_b"] = dbias.reshape(SGU_CHUNK, 4, 64).sum(axis=2).T
    gs["sgu_norm_g"] = dgn.reshape(SGU_W)
    dqkv = [dq, dk, dv]
    drest = [jnp.concatenate([da, df, jnp.zeros((S, OFF_C - OFF_F - F_LANES), BF16), dc], axis=1), dg1, dg2, dg3]
    gw["qkv"] = _mm(sv["h"], dqkv, ta=True, name=n + "dw_qkv", **bf)
    gw["rest"] = _mm(sv["h"], drest, ta=True, name=n + "dw_rest", **bf)
    after = grads_done(l, gw)
    dh = _mm(dqkv, W["qkv"], tb=True, after=after, name=n + "dh_qkv")
    dh = _mm(drest, W["rest"], tb=True, extras=(dh,), epilogue=_add, tm=1024, name=n + "dh")
    dx, gs["norm_mix_g"] = _rms_bwd(sv["x"], sp["norm_mix_g"][l], dh, dx1, name=n + "dnorm_mix")
    return dx, gs


def _local_step(x, mem, target, sp, source, grads_done):
    saved, Ws, h = [], [], None
    for l in range(DEPTH):
        x, h, sv, W = _layer_fwd(l, x, h, mem, source, sp)
        saved.append(sv)
        Ws.append(W)
    loss, dx, dgf = _final_loss(x, sp["final_norm_g"], target, name="final_loss")
    gss = [None] * DEPTH
    for l in reversed(range(DEPTH)):
        dx, gss[l] = _layer_bwd(l, dx, saved[l], mem, Ws[l], sp, grads_done)
    small = {k: jnp.stack([gss[l][k] for l in range(DEPTH)]) for k in gss[0]}
    small["final_norm_g"] = dgf
    return loss, dx, small


_SMALL = ["norm_mix_g", "b_forget", "pool_w", "pool_scale", "sgu_norm_g", "sgu_w", "sgu_b", "b_gate", "norm_xattn_g",
          "norm_mem_g", "norm_ffn_g", "final_norm_g"]
_COL = {"w_branch_a": "ba", "w_branch_b": "bb", "w_branch_c": "bc", "w_xkv": "xkv", "w_ff1": "ff1"}
_ROW = {"w_out": "out", "w_xq": "xq", "w_xo": "xo", "w_ff2": "ff2"}
_BIG = ["w_in", "w_branch_a", "w_branch_b", "w_branch_c", "w_out", "w_xq", "w_xkv", "w_xo", "w_ff1", "w_ff2"]
_PACK_LANES = 128


def _as_rows(a):
    return a.reshape(-1, a.shape[-1])


def _pack(tensors):
    rows = []
    for a in tensors:
        flat = a.reshape(-1)
        flat = jnp.pad(flat, (0, (-flat.shape[0]) % (8 * _PACK_LANES)))
        rows.append(flat.reshape(-1, _PACK_LANES))
    n_rows = sum(r.shape[0] for r in rows)
    rows.append(jnp.zeros(((-n_rows) % (8 * N_DEV), _PACK_LANES), F32))
    return jnp.concatenate(rows, axis=0)


def _unpack(packed, like):
    out, r = [], 0
    for a in like:
        size = math.prod(a.shape)
        nr = 8 * (-(-size // (8 * _PACK_LANES)))
        out.append(packed[r:r + nr].reshape(-1)[:size].reshape(a.shape))
        r += nr
    return out


_SHARD_IN = N_IN // N_DEV
_SHARD_IN_PAD = -(-_SHARD_IN // 128) * 128


def _columns(pieces, start, stop):
    out, at = [], 0
    for p in pieces:
        lo, hi = max(start, at), min(stop, at + p.shape[1])
        if lo < hi:
            out.append(p[:, lo - at:hi - at])
        at += p.shape[1]
    return out


def _split_w_in(blocks):
    K = blocks[0].shape[0]
    pad = jnp.zeros((K, OFF_C - OFF_F - FOX_H), blocks[0].dtype)
    cols = functools.partial(_columns, blocks)
    rest = jnp.concatenate(cols(0, R_OFF_Q) + cols(R_OFF_F, R_OFF_C) + [pad] + cols(R_OFF_C, N_IN), axis=1)
    return jnp.concatenate(cols(R_OFF_Q, R_OFF_F), axis=1), rest


def _join_w_in(qkv, rest):
    in_order = [rest[:, :R_OFF_Q], qkv, rest[:, OFF_F:OFF_F + FOX_H], rest[:, OFF_C:]]
    pad = jnp.zeros((qkv.shape[0], _SHARD_IN_PAD - _SHARD_IN), qkv.dtype)
    return jnp.stack([jnp.concatenate(_columns(in_order, _SHARD_IN * d, _SHARD_IN * (d + 1)) + [pad], axis=1) for d in range(N_DEV)])


_FIRST = ["w_in"]
_LATER = [k for k in _BIG if k not in _FIRST]


def _layer_weights(gathered):
    W = {}
    if "w_in" in gathered:
        W.update(zip(("qkv", "rest"), _split_w_in([gathered["w_in"][d][:, :_SHARD_IN] for d in range(N_DEV)])))
    for name, key in _COL.items():
        if name in gathered:
            W[key] = _Gathered(gathered[name])
    for name, key in _ROW.items():
        if name in gathered:
            W[key] = gathered[name].reshape(-1, gathered[name].shape[-1])
    return W


def _grad_blocks(gw):
    parts = {}
    if "qkv" in gw:
        parts["w_in"] = _join_w_in(gw["qkv"], gw["rest"])
    for name, key in _COL.items():
        if key in gw:
            parts[name] = gw[key]
    for name, key in _ROW.items():
        if key in gw:
            parts[name] = gw[key].reshape(N_DEV, -1, gw[key].shape[-1])
    return parts


def kernel(x, mem, norm_mix_g, w_in, b_forget, pool_w, pool_scale, sgu_norm_g, sgu_w, sgu_b, w_branch_a, w_branch_b, w_branch_c, b_gate, w_out, norm_xattn_g, norm_mem_g, w_xq, w_xkv, w_xo, norm_ffn_g, w_ff1, w_ff2, final_norm_g, loss_target, m_norm_mix_g, m_w_in, m_b_forget, m_pool_w, m_pool_scale, m_sgu_norm_g, m_sgu_w, m_sgu_b, m_w_branch_a, m_w_branch_b, m_w_branch_c, m_b_gate, m_w_out, m_norm_xattn_g, m_norm_mem_g, m_w_xq, m_w_xkv, m_w_xo, m_norm_ffn_g, m_w_ff1, m_w_ff2, m_final_norm_g, v_norm_mix_g, v_w_in, v_b_forget, v_pool_w, v_pool_scale, v_sgu_norm_g, v_sgu_w, v_sgu_b, v_w_branch_a, v_w_branch_b, v_w_branch_c, v_b_gate, v_w_out, v_norm_xattn_g, v_norm_mem_g, v_w_xq, v_w_xkv, v_w_xo, v_norm_ffn_g, v_w_ff1, v_w_ff2, v_final_norm_g):
    names = ["norm_mix_g", "w_in", "b_forget", "pool_w", "pool_scale", "sgu_norm_g", "sgu_w", "sgu_b", "w_branch_a", "w_branch_b",
             "w_branch_c", "b_gate", "w_out", "norm_xattn_g", "norm_mem_g", "w_xq", "w_xkv", "w_xo", "norm_ffn_g", "w_ff1", "w_ff2",
             "final_norm_g"]
    w = dict(zip(names, [norm_mix_g, w_in, b_forget, pool_w, pool_scale, sgu_norm_g, sgu_w, sgu_b, w_branch_a, w_branch_b, w_branch_c,
                         b_gate, w_out, norm_xattn_g, norm_mem_g, w_xq, w_xkv, w_xo, norm_ffn_g, w_ff1, w_ff2, final_norm_g]))
    m = dict(zip(names, [m_norm_mix_g, m_w_in, m_b_forget, m_pool_w, m_pool_scale, m_sgu_norm_g, m_sgu_w, m_sgu_b, m_w_branch_a,
                         m_w_branch_b, m_w_branch_c, m_b_gate, m_w_out, m_norm_xattn_g, m_norm_mem_g, m_w_xq, m_w_xkv, m_w_xo,
                         m_norm_ffn_g, m_w_ff1, m_w_ff2, m_final_norm_g]))
    v = dict(zip(names, [v_norm_mix_g, v_w_in, v_b_forget, v_pool_w, v_pool_scale, v_sgu_norm_g, v_sgu_w, v_sgu_b, v_w_branch_a,
                         v_w_branch_b, v_w_branch_c, v_b_gate, v_w_out, v_norm_xattn_g, v_norm_mem_g, v_w_xq, v_w_xkv, v_w_xo,
                         v_norm_ffn_g, v_w_ff1, v_w_ff2, v_final_norm_g]))

    sp = {k: w[k] for k in _SMALL}
    shards = [{k: w[k][l].astype(BF16) for k in _BIG} for l in range(DEPTH)]
    for sh in shards:
        sh["w_in"] = jnp.pad(sh["w_in"], ((0, 0), (0, _SHARD_IN_PAD - _SHARD_IN)))
    me = _dev_index(*_position())

    def gather_out(l, keys, name, after=None):
        srcs = [shards[l][k] for k in keys]
        lands = [_own_block_placed(a, jax.ShapeDtypeStruct((N_DEV, *a.shape), a.dtype)) for a in srcs]
        state, token = _split_start(_plan_gather_out, srcs, lands, after=after, name=name + "_out_start")
        return (keys, name, state), token

    def gather_pass(job, value):
        keys, name, state = job
        lands = _split_wait(_plan_gather_out, state, value, name=name + "_out_wait")
        state, token = _split_start(_plan_gather_pass, [], lands, name=name + "_pass_start")
        return (keys, name, state), token, lands[0]

    def gather_end(job, value):
        keys, name, state = job
        return _layer_weights(dict(zip(keys, _split_wait(_plan_gather_pass, state, value, name=name + "_pass_wait"))))

    jobs = {}

    def source(l, point, value):
        if (l, point) == (0, "begin"):
            first = _all_gather([shards[0][k] for k in _FIRST], name="gather_l0_first")
            jobs["l0"], token = gather_out(0, _LATER, "gather_l0", after=first[0])
            return _layer_weights(dict(zip(_FIRST, first))), token
        if (l, point) == (0, "attended"):
            jobs["l0"], _, arrived = gather_pass(jobs["l0"], value)
            jobs["l1_first"], token = gather_out(1, _FIRST, "gather_l1_first", after=arrived)
            jobs["l1"], jobs["token"] = gather_out(1, _LATER, "gather_l1", after=token)
            return {}, None
        if (l, point) == (0, "mixed"):
            return gather_end(jobs.pop("l0"), value), jobs.pop("token")
        if (l, point) == (0, "expanded"):
            jobs["l1_first"], token, _ = gather_pass(jobs["l1_first"], value)
            return {}, token
        if (l, point) == (1, "begin"):
            W = gather_end(jobs.pop("l1_first"), value)
            jobs["l1"], token, _ = gather_pass(jobs["l1"], value)
            return W, token
        if (l, point) == (1, "mixed"):
            return gather_end(jobs.pop("l1"), value), None
        return {}, None

    received = [{} for _ in range(DEPTH)]
    travelling = []

    def grads_done(l, gw):
        blocks = _grad_blocks(gw)
        keys = [k for k in _BIG if k in blocks]
        parts = [blocks[k] for k in keys]
        group = f"exchange_grads_l{l}_" + ("in" if "w_in" in blocks else "merge" if "w_out" in blocks else "mlp")
        lands = [_own_block_placed(lax.dynamic_index_in_dim(p, me, 0, keepdims=False), p) for p in parts]
        state, token = _split_start(_plan_exchange, parts, lands, name=group + "_start")
        travelling.append((l, keys, state, group + "_wait"))
        return token

    loss, dx, small = _local_step(x[0], mem[0], loss_target[0], sp, source, grads_done)
    grads, deltas, new_m, new_v = {}, {}, {}, {}
    like = [loss] + [w[k] for k in _SMALL]
    packed = _pack([loss] + [small[k] for k in _SMALL])
    eighths = packed.reshape(N_DEV, -1, _PACK_LANES)
    own = lambda a: _own_block_placed(lax.dynamic_index_in_dim(a, me, 0, keepdims=False) if a.ndim == 3 else a, eighths)
    scatter, done = _split_start(_plan_exchange, [eighths], [own(eighths)], after=dx, name="small_grads_scatter_start")

    def reduce_small(after):
        mine = _sum_blocks(_split_wait(_plan_exchange, scatter, after, name="small_grads_scatter_wait")[0], name="small_grads_sum")
        return _split_start(_plan_broadcast, [mine], [own(mine)], name="small_grads_gather_start")

    def update_small(state, after):
        total = _split_wait(_plan_broadcast, state, after, name="small_grads_gather_wait")[0].reshape(packed.shape)
        loss_sum, *g_small = _unpack(total, like)
        rows = lambda d: [_as_rows(d[k]) for k in _SMALL]
        outs = _adamw_small([_as_rows(g) for g in g_small], rows(w), rows(m), rows(v), name="adamw_small")
        grads.update(zip(_SMALL, g_small))
        for dst, vals in zip((deltas, new_m, new_v), outs):
            dst.update({k: a.reshape(w[k].shape) for k, a in zip(_SMALL, vals)})
        return loss_sum[0, 0], outs[0][0]

    groups = list(dict.fromkeys(tuple(keys) for _, keys, _, _ in travelling))
    for n_done, group_keys in enumerate(groups):
        if n_done == 1:
            gather, _ = reduce_small(done)
        if n_done == len(groups) - 1:
            loss, done = update_small(gather, done)
        for l, keys, state, wait_name in travelling:
            if tuple(keys) == group_keys:
                received[l].update(zip(keys, _split_wait(_plan_exchange, state, done, name=wait_name)))
        for k in group_keys:
            outs = _adamw_sharded([received[l][k] for l in range(DEPTH)], w[k], m[k], v[k], name="adamw_" + k)
            grads[k], deltas[k], new_m[k], new_v[k] = outs
        done = grads[group_keys[-1]]

    return (loss, dx[None], *[grads[k] for k in names], *[deltas[k] for k in names], *[new_m[k] for k in names],
            *[new_v[k] for k in names])
```

```python
import functools
import math

import jax
import jax.numpy as jnp
from jax import lax
from jax.experimental import pallas as pl
from jax.experimental.pallas import tpu as pltpu

F32 = jnp.float32
BF16 = jnp.bfloat16
MESH = pl.DeviceIdType.MESH

N_DEV = 8
D = 1024
DEPTH = 2
EPS = 1e-6
NEG = -1e30
POOL_W = 256
FOX_H = 8
FOX_DH = 64
FOX_W = 512
SGU_W = 256
SGU_CHUNK = 128
XH = 4
XDH = 256
N_IN = 5384
R_OFF_Q, R_OFF_F, R_OFF_C = 256, 1792, 1800
QKV_W = 3 * FOX_W
OFF_A, OFF_F, OFF_C, OFF_G, REST_W = 0, 256, 512, 1024, 4096
F_LANES = 128

ADAM_LR = 0.001
ADAM_B1 = 0.9
ADAM_B2 = 0.999
ADAM_EPS = 1e-08
ADAM_WD = 0.01
ADAM_STEP = 10

VMEM_LIMIT = 56 * 1024 * 1024


def _tile(n, pref):
    t = min(n, pref)
    while n % t:
        t -= 128
    assert t > 0, (n, pref)
    return t


def _params(sem=None):
    return pltpu.CompilerParams(dimension_semantics=sem, vmem_limit_bytes=VMEM_LIMIT)


def _dot(a, b, ca, cb):
    return lax.dot_general(a, b, (((ca,), (cb,)), ((), ())), preferred_element_type=F32)


def _sigmoid(z):
    return 1.0 / (1.0 + jnp.exp(-z))


_GELU_K = math.sqrt(2.0 / math.pi)
_GELU_C = 0.044715


def _gelu(x):
    return 0.5 * x * (1.0 + jnp.tanh(_GELU_K * (x + _GELU_C * x * x * x)))


def _gelu_grad(x):
    t = jnp.tanh(_GELU_K * (x + _GELU_C * x * x * x))
    return 0.5 * (1.0 + t) + 0.5 * x * (1.0 - t * t) * _GELU_K * (1.0 + 3.0 * _GELU_C * x * x)


def _rows(shape):
    return lax.broadcasted_iota(jnp.int32, shape, 0)


def _lanes(shape):
    return lax.broadcasted_iota(jnp.int32, shape, 1)


class _Gathered:
    def __init__(self, arr):
        self.arr = arr
        self.shape = (arr.shape[1], N_DEV * arr.shape[2])


_TOKEN = (8, 128)


def _mm(a, b, *, ta=False, tb=False, extras=(), row_extras=(), epilogue=None, out_dtypes=(F32,), row_outs=0, shard_out=False, after=None,
        tm=None, tn=512, tk=None, name):
    a_parts = list(a) if isinstance(a, (list, tuple)) else [a]
    b_parts = list(b) if isinstance(b, (list, tuple)) else [b]
    gathered = isinstance(b, _Gathered)
    assert (len(a_parts) == 1 or not ta) and (len(b_parts) == 1 or not tb) and min(len(a_parts), len(b_parts)) == 1
    a0, b0 = a_parts[0], b_parts[0]
    M, K = (a0.shape[1], a0.shape[0]) if ta else (a0.shape[0], a0.shape[1] * len(a_parts))
    N, Kb = b0.shape if tb else (b0.shape[1] * len(b_parts), b0.shape[0])
    assert Kb == K, (a0.shape, b0.shape, ta, tb)
    if gathered:
        if tb:
            tk = b.arr.shape[2]
        else:
            tn = b.arr.shape[2]
    if len(a_parts) > 1:
        tk = a0.shape[1]
    if shard_out:
        tn = N // N_DEV
    tm = _tile(M, tm or (1024 if ta else 2048))
    tn = _tile(b0.shape[1] if len(b_parts) > 1 else N, tn)
    per_piece = b0.shape[1] // tn
    size = lambda dt: jnp.dtype(dt).itemsize
    row_bytes = len(a_parts) * tm * size(a0.dtype) + len(b_parts) * tn * size(b.arr.dtype if gathered else b0.dtype)
    tile_bytes = tm * tn * (sum(size(e.dtype) for e in extras) + sum(map(size, out_dtypes)))

    def vmem_bytes(k_tile):
        return 2 * (k_tile * row_bytes + tile_bytes) + tm * tn * 4 * (K > k_tile)

    if tk is None:
        tk = next(c for c in (_tile(K, 2048), _tile(K, 1024), _tile(K, 512), _tile(K, 256)) if vmem_bytes(c) <= VMEM_LIMIT - (4 << 20))
    tk = _tile(K, tk)
    nk = K // tk
    ca, cb = (0 if ta else 1), (1 if tb else 0)
    n_a, n_b, n_ex, n_out = len(a_parts), len(b_parts), len(extras) + len(row_extras), len(out_dtypes)
    tokens = [] if after is None else [after]
    n_in = n_a + n_b + n_ex + len(tokens)
    if epilogue is None:
        epilogue = lambda acc: (acc,)

    def body(*refs):
        a_refs, b_refs = refs[:n_a], refs[n_a:n_a + n_b]
        ex_refs = refs[n_a + n_b:n_a + n_b + n_ex]
        o_refs = refs[n_in:n_in + n_out]
        j, k = pl.program_id(1), pl.program_id(2)

        def finish(acc):
            vals = epilogue(acc, *[e[...] for e in ex_refs])
            for o_ref, val in zip(o_refs[:n_out - row_outs], vals):
                o_ref[...] = val.astype(o_ref.dtype)
            for o_ref, val in zip(o_refs[n_out - row_outs:], vals[n_out - row_outs:]):
                first = pl.program_id(0) == 0
                o_ref[...] = jnp.where(first, val, o_ref[...] + val)

        def step(a_ref, b_ref):
            part = _dot(a_ref[...].astype(BF16), b_ref[...].astype(BF16), ca, cb)
            if nk == 1:
                finish(part)
            else:
                acc_ref = refs[-1]

                @pl.when(k == 0)
                def _():
                    acc_ref[...] = part

                @pl.when(k > 0)
                def _():
                    acc_ref[...] += part

                @pl.when(k == nk - 1)
                def _():
                    finish(acc_ref[...])

        if n_a > 1:
            for p in range(n_a):
                pl.when(k == p)(functools.partial(step, a_refs[p], b_refs[0]))
        elif n_b > 1:
            for p in range(n_b):
                pl.when(j // per_piece == p)(functools.partial(step, a_refs[0], b_refs[p]))
        else:
            step(a_refs[0], b_refs[0])

    if n_a > 1:
        a_specs = [pl.BlockSpec((tm, tk), lambda i, j, k: (i, 0))] * n_a
    else:
        a_specs = [pl.BlockSpec((tk, tm), lambda i, j, k: (k, i)) if ta else pl.BlockSpec((tm, tk), lambda i, j, k: (i, k))]
    if gathered:
        b_arrs = [b.arr]
        b_specs = [pl.BlockSpec((None, tn, tk), lambda i, j, k: (k, j, 0)) if tb else pl.BlockSpec((None, tk, tn), lambda i, j, k: (j, k, 0))]
    elif n_b > 1:
        b_arrs = b_parts
        b_specs = [pl.BlockSpec((tk, tn), functools.partial(lambda p, i, j, k: (k, jnp.clip(j - p * per_piece, 0, per_piece - 1)), p))
                   for p in range(n_b)]
    else:
        b_arrs = b_parts
        b_specs = [pl.BlockSpec((tn, tk), lambda i, j, k: (j, k)) if tb else pl.BlockSpec((tk, tn), lambda i, j, k: (k, j))]
    tile = pl.BlockSpec((tm, tn), lambda i, j, k: (i, j))
    if shard_out:
        out_specs = [pl.BlockSpec((None, tm, tn), lambda i, j, k: (j, i, 0))] * n_out
        out_shape = [jax.ShapeDtypeStruct((N_DEV, M, tn), dt) for dt in out_dtypes]
    else:
        assert row_outs == 0 or tn == N
        out_specs = [tile] * (n_out - row_outs) + [pl.BlockSpec((1, tn), lambda i, j, k: (0, j))] * row_outs
        out_shape = [jax.ShapeDtypeStruct((1, N) if t >= n_out - row_outs else (M, N), dt) for t, dt in enumerate(out_dtypes)]
    assert vmem_bytes(tk) <= VMEM_LIMIT - (4 << 20), (name, vmem_bytes(tk))
    outs = pl.pallas_call(
        body,
        name=name,
        grid=(M // tm, N // tn, nk),
        in_specs=a_specs + b_specs + [tile] * len(extras) + [pl.BlockSpec((1, tn), lambda i, j, k: (0, j))] * len(row_extras)
        + [pl.BlockSpec(_TOKEN, lambda i, j, k: (0, 0))] * len(tokens),
        out_specs=out_specs,
        out_shape=out_shape,
        scratch_shapes=[pltpu.VMEM((tm, tn), F32)] if nk > 1 else [],
        compiler_params=_params(("arbitrary",) * 3 if row_outs else ("parallel", "parallel", "arbitrary")),
    )(*a_parts, *b_arrs, *extras, *row_extras, *tokens)
    return outs[0] if n_out == 1 else outs


def _add(acc, res):
    return (acc + res,)


def _norm_grad(dh, x, dres, g):
    r = lax.rsqrt(jnp.mean(x * x, axis=-1, keepdims=True) + EPS)
    xn = x * r
    dxn = dh * g
    return r * (dxn - xn * jnp.mean(dxn * xn, axis=-1, keepdims=True)) + dres, jnp.sum(dh * xn, axis=0, keepdims=True)


def _add_norm_grad(acc, more, x, dres, g):
    return _norm_grad(acc + more, x, dres, g)


def _add_norm(acc, res, g):
    x = acc + res
    return x, x * lax.rsqrt(jnp.mean(x * x, axis=-1, keepdims=True) + EPS) * g


def _rms_fwd(x, g, *, after=None, name):
    R, C = x.shape
    tm = _tile(R, 256)
    tokens = [] if after is None else [after]

    def body(x_ref, g_ref, *rest):
        xv = x_ref[...]
        r = lax.rsqrt(jnp.mean(xv * xv, axis=-1, keepdims=True) + EPS)
        rest[-1][...] = (xv * r * g_ref[...]).astype(BF16)

    return pl.pallas_call(
        body,
        name=name,
        grid=(R // tm,),
        in_specs=[pl.BlockSpec((tm, C), lambda i: (i, 0)), pl.BlockSpec((1, C), lambda i: (0, 0))]
        + [pl.BlockSpec(_TOKEN, lambda i: (0, 0))] * len(tokens),
        out_specs=pl.BlockSpec((tm, C), lambda i: (i, 0)),
        out_shape=jax.ShapeDtypeStruct((R, C), BF16),
        compiler_params=_params(("parallel",)),
    )(x, g.reshape(1, C), *tokens)


def _rms_bwd(x, g, dh, dres, *, name):
    R, C = x.shape
    tm = _tile(R, 256)

    def body(x_ref, g_ref, dh_ref, dres_ref, dx_ref, dg_ref):
        xv = x_ref[...]
        r = lax.rsqrt(jnp.mean(xv * xv, axis=-1, keepdims=True) + EPS)
        xn = xv * r
        dh_v = dh_ref[...].astype(F32)
        dxn = dh_v * g_ref[...]
        dx_ref[...] = r * (dxn - xn * jnp.mean(dxn * xn, axis=-1, keepdims=True)) + dres_ref[...]
        part = jnp.sum(dh_v * xn, axis=0, keepdims=True)

        @pl.when(pl.program_id(0) == 0)
        def _():
            dg_ref[...] = part

        @pl.when(pl.program_id(0) > 0)
        def _():
            dg_ref[...] += part

    row = pl.BlockSpec((tm, C), lambda i: (i, 0))
    vec = pl.BlockSpec((1, C), lambda i: (0, 0))
    dx, dg = pl.pallas_call(
        body,
        name=name,
        grid=(R // tm,),
        in_specs=[row, vec, row, row],
        out_specs=[row, vec],
        out_shape=[jax.ShapeDtypeStruct((R, C), F32), jax.ShapeDtypeStruct((1, C), F32)],
        compiler_params=_params(("arbitrary",)),
    )(x, g.reshape(1, C), dh, dres)
    return dx, dg.reshape(C)


def _final_loss(x, g, target, *, name):
    R, C = x.shape
    tm = _tile(R, 256)

    def body(x_ref, g_ref, t_ref, loss_ref, dx_ref, dg_ref):
        xv = x_ref[...]
        r = lax.rsqrt(jnp.mean(xv * xv, axis=-1, keepdims=True) + EPS)
        xn = xv * r
        gv = g_ref[...]
        err = xn * gv - t_ref[...]
        lpart = (0.5 / C) * jnp.sum(jnp.sum(err * err, axis=1, keepdims=True), axis=0, keepdims=True)
        dy = err * (1.0 / C)
        dxn = dy * gv
        dx_ref[...] = r * (dxn - xn * jnp.mean(dxn * xn, axis=-1, keepdims=True))
        gpart = jnp.sum(dy * xn, axis=0, keepdims=True)

        @pl.when(pl.program_id(0) == 0)
        def _():
            loss_ref[...] = lpart
            dg_ref[...] = gpart

        @pl.when(pl.program_id(0) > 0)
        def _():
            loss_ref[...] += lpart
            dg_ref[...] += gpart

    row = pl.BlockSpec((tm, C), lambda i: (i, 0))
    vec = pl.BlockSpec((1, C), lambda i: (0, 0))
    loss, dx, dg = pl.pallas_call(
        body,
        name=name,
        grid=(R // tm,),
        in_specs=[row, vec, row],
        out_specs=[pl.BlockSpec((1, 1), lambda i: (0, 0)), row, vec],
        out_shape=[jax.ShapeDtypeStruct((1, 1), F32), jax.ShapeDtypeStruct((R, C), F32), jax.ShapeDtypeStruct((1, C), F32)],
        compiler_params=_params(("arbitrary",)),
    )(x, g.reshape(1, C), target)
    return loss, dx, dg.reshape(C)


def _pool_select(lane, vals):
    out = vals[3]
    for gi in (2, 1, 0):
        out = jnp.where(lane < 64 * (gi + 1), vals[gi], out)
    return out


def _pool_diff(a):
    row, lane = _rows(a.shape), _lanes(a.shape)

    def down(v, k):
        return jnp.where(row >= k, pltpu.roll(v, k, 0), 0.0)

    s2 = a + down(a, 1)
    s4 = s2 + down(s2, 2)
    s8 = s4 + down(s4, 4)
    s16 = s8 + down(s8, 8)
    wsum = _pool_select(lane, (s2, s4, s8, s16))
    win = _pool_select(lane, (2, 4, 8, 16))
    cnt = jnp.minimum(row + 1, win).astype(F32)
    return wsum / cnt - a, cnt


def _pool_diff_t(dd, cnt):
    S = dd.shape[0]
    row, lane = _rows(dd.shape), _lanes(dd.shape)

    def up(v, k):
        return jnp.where(row < S - k, pltpu.roll(v, S - k, 0), 0.0)

    e = dd / cnt
    s2 = e + up(e, 1)
    s4 = s2 + up(s2, 2)
    s8 = s4 + up(s4, 4)
    s16 = s8 + up(s8, 8)
    return _pool_select(lane, (s2, s4, s8, s16)) - dd


def _pool_fwd(rest, wbd, scale, *, name):
    S = rest.shape[0]

    def body(a_ref, w_ref, s_ref, o_ref):
        d, _ = _pool_diff(a_ref[...])
        yp = _dot(d.astype(BF16), w_ref[...], 1, 0)
        o_ref[...] = (yp * s_ref[...]).astype(BF16)

    return pl.pallas_call(
        body,
        name=name,
        grid=(1,),
        in_specs=[
            pl.BlockSpec((S, POOL_W), lambda i: (0, OFF_A // POOL_W)),
            pl.BlockSpec((POOL_W, POOL_W), lambda i: (0, 0)),
            pl.BlockSpec((1, POOL_W), lambda i: (0, 0)),
        ],
        out_specs=pl.BlockSpec((S, POOL_W), lambda i: (0, 0)),
        out_shape=jax.ShapeDtypeStruct((S, POOL_W), BF16),
        compiler_params=_params(("arbitrary",)),
    )(rest, wbd, scale.reshape(1, POOL_W))


def _pool_bwd(rest, wbd, wbd_t, scale, dpa, *, name):
    S = rest.shape[0]

    def body(a_ref, w_ref, wt_ref, s_ref, dpa_ref, da_ref, dw_ref, ds_ref):
        d, cnt = _pool_diff(a_ref[...])
        db = d.astype(BF16)
        yp = _dot(db, w_ref[...], 1, 0)
        dpa_v = dpa_ref[...]
        ds_ref[...] = jnp.sum(dpa_v * yp, axis=0, keepdims=True)
        dyp = (dpa_v * s_ref[...]).astype(BF16)
        dw_ref[...] = _dot(db, dyp, 0, 0)
        dd = _dot(dyp, wt_ref[...], 1, 0)
        da_ref[...] = _pool_diff_t(dd, cnt).astype(BF16)

    full = pl.BlockSpec((S, POOL_W), lambda i: (0, 0))
    sq = pl.BlockSpec((POOL_W, POOL_W), lambda i: (0, 0))
    vec = pl.BlockSpec((1, POOL_W), lambda i: (0, 0))
    return pl.pallas_call(
        body,
        name=name,
        grid=(1,),
        in_specs=[pl.BlockSpec((S, POOL_W), lambda i: (0, OFF_A // POOL_W)), sq, sq, vec, full],
        out_specs=[full, sq, vec],
        out_shape=[
            jax.ShapeDtypeStruct((S, POOL_W), BF16),
            jax.ShapeDtypeStruct((POOL_W, POOL_W), F32),
            jax.ShapeDtypeStruct((1, POOL_W), F32),
        ],
        compiler_params=_params(("arbitrary",)),
    )(rest, wbd, wbd_t, scale.reshape(1, POOL_W), dpa)


def _log_sigmoid(z):
    return jnp.minimum(z, 0.0) - jnp.log(1.0 + jnp.exp(-jnp.abs(z)))


_F_SPEC_COL = OFF_F // F_LANES


def _fox_prep(rest, bpad, *, name):
    S = rest.shape[0]

    def body(f_ref, b_ref, o_ref, ot_ref):
        acc = _log_sigmoid(f_ref[...] + b_ref[...])
        row = _rows(acc.shape)
        k = 1
        while k < S:
            acc = acc + jnp.where(row >= k, pltpu.roll(acc, k, 0), 0.0)
            k *= 2
        o_ref[...] = acc
        ot_ref[...] = acc.T

    return pl.pallas_call(
        body,
        name=name,
        grid=(1,),
        in_specs=[pl.BlockSpec((S, F_LANES), lambda i: (0, _F_SPEC_COL)), pl.BlockSpec((1, F_LANES), lambda i: (0, 0))],
        out_specs=[pl.BlockSpec((S, F_LANES), lambda i: (0, 0)), pl.BlockSpec((F_LANES, S), lambda i: (0, 0))],
        out_shape=[jax.ShapeDtypeStruct((S, F_LANES), F32), jax.ShapeDtypeStruct((F_LANES, S), F32)],
        compiler_params=_params(("arbitrary",)),
    )(rest, bpad)


def _fox_post(rest, bpad, dcum, *, name):
    S = rest.shape[0]

    def body(f_ref, b_ref, d_ref, df_ref, db_ref):
        acc = d_ref[...]
        row = _rows(acc.shape)
        k = 1
        while k < S:
            acc = acc + jnp.where(row < S - k, pltpu.roll(acc, S - k, 0), 0.0)
            k *= 2
        df = acc * (1.0 - _sigmoid(f_ref[...] + b_ref[...]))
        df_ref[...] = df.astype(BF16)
        db_ref[...] = jnp.sum(df, axis=0, keepdims=True)

    full = pl.BlockSpec((S, F_LANES), lambda i: (0, 0))
    vec = pl.BlockSpec((1, F_LANES), lambda i: (0, 0))
    return pl.pallas_call(
        body,
        name=name,
        grid=(1,),
        in_specs=[pl.BlockSpec((S, F_LANES), lambda i: (0, _F_SPEC_COL)), vec, full],
        out_specs=[full, vec],
        out_shape=[jax.ShapeDtypeStruct((S, F_LANES), BF16), jax.ShapeDtypeStruct((1, F_LANES), F32)],
        compiler_params=_params(("arbitrary",)),
    )(rest, bpad, dcum)


_FOX_SCALE = FOX_DH ** -0.5
_PAIRS = FOX_H // 2


def _scaled(v):
    return (v.astype(F32) * _FOX_SCALE).astype(BF16)


def _diag_mask(s):
    return jnp.where(_rows(s.shape) >= _lanes(s.shape), s, NEG)


def _fox_fwd(qkv, cum, fk3, *, name):
    S = qkv.shape[0]
    nk, t = fk3.shape[1:]

    def body(q_ref, k_ref, v_ref, cum_ref, fk_ref, o_ref, lse_ref):
        i = pl.program_id(0)
        lane = _lanes((t, 128))
        lo = lane < FOX_DH
        cumv = cum_ref[...]
        qm, fq = [], []
        for h in range(FOX_H):
            qs = _scaled(q_ref[:, 128 * (h // 2):128 * (h // 2 + 1)])
            zero = jnp.zeros_like(qs)
            qm.append(jnp.where(lo, qs, zero) if h % 2 == 0 else jnp.where(lo, zero, qs))
            fq.append(cumv[:, h:h + 1])

        def tile(j, state, masked):
            m, acc, lsum = (list(part) for part in state)
            k0 = pl.multiple_of(j * t, t)
            for hp in range(_PAIRS):
                cols = slice(128 * hp, 128 * (hp + 1))
                kb = k_ref[pl.ds(k0, t), cols]
                vb = v_ref[pl.ds(k0, t), cols]
                one = jnp.ones_like(vb)
                alphas, pvs = [], []
                for h in (2 * hp, 2 * hp + 1):
                    s = _dot(qm[h], kb, 1, 1) + fq[h] - fk_ref[h, pl.ds(j, 1), :]
                    if masked:
                        s = _diag_mask(s)
                    m_new = jnp.maximum(m[h], jnp.max(s, axis=-1, keepdims=True))
                    p = jnp.exp(s - m_new)
                    alphas.append(jnp.exp(m[h] - m_new))
                    m[h] = m_new
                    pvs.append(_dot(p.astype(BF16), jnp.where(lo, vb, one) if h % 2 == 0 else jnp.where(lo, one, vb), 1, 0))
                acc[hp] = jnp.where(lo, alphas[0], alphas[1]) * acc[hp] + jnp.where(lo, pvs[0], pvs[1])
                lsum[hp] = jnp.where(lo, alphas[1], alphas[0]) * lsum[hp] + jnp.where(lo, pvs[1], pvs[0])
            return tuple(m), tuple(acc), tuple(lsum)

        zeros = (jnp.zeros((t, 128), F32),) * _PAIRS
        state = lax.fori_loop(0, i, functools.partial(tile, masked=False), ((jnp.full((t, 1), NEG, F32),) * FOX_H, zeros, zeros))
        m, acc, lsum = tile(i, state, True)
        for hp in range(_PAIRS):
            o_ref[:, 128 * hp:128 * (hp + 1)] = acc[hp] / pltpu.roll(lsum[hp], FOX_DH, 1)
            lse = [m[2 * hp] + jnp.log(lsum[hp][:, FOX_DH:FOX_DH + 1]), m[2 * hp + 1] + jnp.log(lsum[hp][:, 0:1])]
            lse_ref[hp] = jnp.where(lane == 0, lse[0], jnp.where(lane == 1, lse[1], 0.0))

    whole = lambda col: pl.BlockSpec((S, FOX_W), lambda i: (0, col))
    return pl.pallas_call(
        body,
        name=name,
        grid=(S // t,),
        in_specs=[
            pl.BlockSpec((t, FOX_W), lambda i: (i, 0)), whole(1), whole(2),
            pl.BlockSpec((t, F_LANES), lambda i: (i, 0)),
            pl.BlockSpec((FOX_H, nk, t), lambda i: (0, 0, 0)),
        ],
        out_specs=[pl.BlockSpec((t, FOX_W), lambda i: (i, 0)), pl.BlockSpec((_PAIRS, t, 128), lambda i: (0, i, 0))],
        out_shape=[jax.ShapeDtypeStruct((S, FOX_W), F32), jax.ShapeDtypeStruct((_PAIRS, S, 128), F32)],
        compiler_params=_params(("arbitrary",)),
    )(qkv, qkv, qkv, cum, fk3)


def _fox_bwd(qkv, cum, fk3, o, do, lse, *, name):
    S = qkv.shape[0]
    nk, t = fk3.shape[1:]
    q_at, k_at, v_at = 0, FOX_W, 2 * FOX_W

    def body(qkv_ref, cum_ref, fk_ref, o_ref, do_ref, lse_ref, dq_ref, dk_ref, dv_ref, dfq_ref, dfk_ref,
             qs_sc, ks_sc, delta_sc, dq_sc):
        lane = _lanes((t, 128))
        lo = lane < FOX_DH
        mine = lambda h: lo if h % 2 == 0 else jnp.logical_not(lo)

        def by_head(tile, values):
            for h, val in enumerate(values):
                tile = jnp.where(lane == h, val, tile)
            return tile

        def prep(i, carry):
            r = pl.ds(pl.multiple_of(i * t, t), t)
            qs_sc[r, :] = _scaled(qkv_ref[r, q_at:q_at + FOX_W])
            ks_sc[r, :] = _scaled(qkv_ref[r, k_at:k_at + FOX_W])
            sums = []
            for hp in range(_PAIRS):
                cols = slice(128 * hp, 128 * (hp + 1))
                prod = do_ref[r, cols].astype(F32) * o_ref[r, cols]
                sums += [jnp.sum(jnp.where(mine(h), prod, 0.0), axis=-1, keepdims=True) for h in (2 * hp, 2 * hp + 1)]
            delta_sc[r, :] = by_head(jnp.zeros((t, 128), F32), sums)
            dfq_ref[r, :] = jnp.zeros((t, 128), F32)
            dq_sc[r, :] = jnp.zeros((t, FOX_W), F32)
            return carry

        lax.fori_loop(0, nk, prep, 0)

        def kv_tile(j, carry):
            kr = pl.ds(pl.multiple_of(j * t, t), t)

            def q_tile(i, acc, masked):
                dk, dv, dfk = list(acc[:_PAIRS]), list(acc[_PAIRS:2 * _PAIRS]), list(acc[2 * _PAIRS:])
                qr = pl.ds(pl.multiple_of(i * t, t), t)
                delta_t, cum_t, dq_old, dfq_old = delta_sc[qr, :], cum_ref[qr, :], dq_sc[qr, :], dfq_ref[qr, :]
                row_sums, dq_new = [], []
                for hp in range(_PAIRS):
                    cols = slice(128 * hp, 128 * (hp + 1))
                    kb = qkv_ref[kr, k_at + 128 * hp:k_at + 128 * (hp + 1)]
                    vb = qkv_ref[kr, v_at + 128 * hp:v_at + 128 * (hp + 1)]
                    ksb, qsb, dob = ks_sc[kr, cols], qs_sc[qr, cols], do_ref[qr, cols]
                    zero = jnp.zeros_like(qsb)
                    dq_t = jnp.zeros((t, 128), F32)
                    for h in (2 * hp, 2 * hp + 1):
                        qe, doe, ke = (jnp.where(mine(h), a, zero) for a in (qsb, dob, ksb))
                        s = _dot(qe, kb, 1, 1) + cum_t[:, h:h + 1] - fk_ref[h, pl.ds(j, 1), :]
                        if masked:
                            s = _diag_mask(s)
                        p = jnp.exp(s - lse_ref[hp, qr, h % 2:h % 2 + 1])
                        dv[hp] = dv[hp] + _dot(p.astype(BF16), doe, 0, 0)
                        dp = _dot(doe, vb, 1, 1)
                        ds = p * (dp - delta_t[:, h:h + 1])
                        dsb = ds.astype(BF16)
                        dk[hp] = dk[hp] + _dot(dsb, qe, 0, 0)
                        dq_t = dq_t + _dot(dsb, ke, 1, 0)
                        row_sums.append(jnp.sum(ds, axis=-1, keepdims=True))
                        dfk[h] = dfk[h] - jnp.sum(ds, axis=0, keepdims=True)
                    dq_new.append(dq_old[:, cols] + dq_t)
                for hp in range(_PAIRS):
                    dq_sc[qr, 128 * hp:128 * (hp + 1)] = dq_new[hp]
                dfq_ref[qr, :] = dfq_old + by_head(jnp.zeros((t, 128), F32), row_sums)
                return (*dk, *dv, *dfk)

            init = tuple([jnp.zeros((t, 128), F32)] * (2 * _PAIRS) + [jnp.zeros((1, t), F32)] * FOX_H)
            acc = q_tile(j, init, True)
            acc = lax.fori_loop(j + 1, nk, functools.partial(q_tile, masked=False), acc)
            for hp in range(_PAIRS):
                cols = slice(128 * hp, 128 * (hp + 1))
                dk_ref[kr, cols] = acc[hp].astype(BF16)
                dv_ref[kr, cols] = acc[_PAIRS + hp].astype(BF16)
            for h in range(FOX_H):
                dfk_ref[h, pl.ds(j, 1), :] = acc[2 * _PAIRS + h]
            return carry

        lax.fori_loop(0, nk, kv_tile, 0)
        dq_ref[...] = dq_sc[...].astype(BF16)

    vm = pl.BlockSpec(memory_space=pltpu.VMEM)
    big = jax.ShapeDtypeStruct((S, FOX_W), BF16)
    return pl.pallas_call(
        body,
        name=name,
        in_specs=[vm] * 6,
        out_specs=[vm] * 5,
        out_shape=[big, big, big, jax.ShapeDtypeStruct((S, 128), F32), jax.ShapeDtypeStruct((FOX_H, nk, t), F32)],
        scratch_shapes=[pltpu.VMEM((S, FOX_W), BF16), pltpu.VMEM((S, FOX_W), BF16), pltpu.VMEM((S, 128), F32),
                        pltpu.VMEM((S, FOX_W), F32)],
        compiler_params=pltpu.CompilerParams(vmem_limit_bytes=VMEM_LIMIT),
    )(qkv, cum, fk3, o, do, lse)


def _group_mask(lane, gi):
    return (lane >= 64 * gi) & (lane < 64 * (gi + 1))


_U_COL = OFF_C // SGU_W


def _sgu_fwd(rest, gn, wm, bias, *, name):
    S = rest.shape[0]
    ts = _tile(S, 512)
    nc = ts // SGU_CHUNK

    def body(u_ref, v_ref, g_ref, w_ref, b_ref, o_ref):
        zv = _gelu(v_ref[...])
        vn = zv * lax.rsqrt(jnp.mean(zv * zv, axis=-1, keepdims=True) + EPS) * g_ref[...]
        lane = _lanes((SGU_CHUNK, SGU_W))
        for c in range(nc):
            rows = slice(c * SGU_CHUNK, (c + 1) * SGU_CHUNK)
            vcb = vn[rows].astype(BF16)
            mixed = b_ref[...]
            for gi in range(4):
                mixed = mixed + jnp.where(_group_mask(lane, gi), _dot(w_ref[gi], vcb, 1, 0), 0.0)
            o_ref[rows, :] = (_gelu(u_ref[rows, :]) * mixed).astype(BF16)

    return pl.pallas_call(
        body,
        name=name,
        grid=(S // ts,),
        in_specs=[
            pl.BlockSpec((ts, SGU_W), lambda i: (i, _U_COL)),
            pl.BlockSpec((ts, SGU_W), lambda i: (i, _U_COL + 1)),
            pl.BlockSpec((1, SGU_W), lambda i: (0, 0)),
            pl.BlockSpec((4, SGU_CHUNK, SGU_CHUNK), lambda i: (0, 0, 0)),
            pl.BlockSpec((SGU_CHUNK, SGU_W), lambda i: (0, 0)),
        ],
        out_specs=pl.BlockSpec((ts, SGU_W), lambda i: (i, 0)),
        out_shape=jax.ShapeDtypeStruct((S, SGU_W), BF16),
        compiler_params=_params(("parallel",)),
    )(rest, rest, gn.reshape(1, SGU_W), wm, bias)


def _sgu_bwd(rest, gn, wm, wm_t, bias, dsg, *, name):
    S = rest.shape[0]
    ts = _tile(S, 512)
    nc = ts // SGU_CHUNK

    def body(u_ref, v_ref, g_ref, w_ref, wt_ref, b_ref, dsg_ref, dc_ref, dw_ref, db_ref, dg_ref):
        first = pl.program_id(0) == 0

        @pl.when(first)
        def _():
            dw_ref[...] = jnp.zeros_like(dw_ref)
            db_ref[...] = jnp.zeros_like(db_ref)
            dg_ref[...] = jnp.zeros_like(dg_ref)

        gv = g_ref[...]
        lane = _lanes((SGU_CHUNK, SGU_W))
        for c in range(nc):
            rows = slice(c * SGU_CHUNK, (c + 1) * SGU_CHUNK)
            vpre = v_ref[rows, :]
            upre = u_ref[rows, :]
            zv = _gelu(vpre)
            r = lax.rsqrt(jnp.mean(zv * zv, axis=-1, keepdims=True) + EPS)
            zn = zv * r
            vcb = (zn * gv).astype(BF16)
            mixed = b_ref[...]
            for gi in range(4):
                mixed = mixed + jnp.where(_group_mask(lane, gi), _dot(w_ref[gi], vcb, 1, 0), 0.0)
            zu = _gelu(upre)
            dsg_v = dsg_ref[rows, :]
            dc_ref[rows, :SGU_W] = (dsg_v * mixed * _gelu_grad(upre)).astype(BF16)
            dmixed = dsg_v * zu
            db_ref[...] += dmixed
            dvn = jnp.zeros((SGU_CHUNK, SGU_W), F32)
            for gi in range(4):
                dmg = jnp.where(_group_mask(lane, gi), dmixed, 0.0).astype(BF16)
                dw_ref[gi] += _dot(dmg, vcb, 1, 1)
                dvn = dvn + _dot(wt_ref[gi], dmg, 1, 0)
            dg_ref[...] += jnp.sum(dvn * zn, axis=0, keepdims=True)
            dzn = dvn * gv
            dzv = r * (dzn - zn * jnp.mean(dzn * zn, axis=-1, keepdims=True))
            dc_ref[rows, SGU_W:] = (dzv * _gelu_grad(vpre)).astype(BF16)

    blk = pl.BlockSpec((ts, SGU_W), lambda i: (i, 0))
    vec = pl.BlockSpec((1, SGU_W), lambda i: (0, 0))
    w3 = pl.BlockSpec((4, SGU_CHUNK, SGU_CHUNK), lambda i: (0, 0, 0))
    bsp = pl.BlockSpec((SGU_CHUNK, SGU_W), lambda i: (0, 0))
    return pl.pallas_call(
        body,
        name=name,
        grid=(S // ts,),
        in_specs=[
            pl.BlockSpec((ts, SGU_W), lambda i: (i, _U_COL)),
            pl.BlockSpec((ts, SGU_W), lambda i: (i, _U_COL + 1)),
            vec, w3, w3, bsp, blk,
        ],
        out_specs=[pl.BlockSpec((ts, 2 * SGU_W), lambda i: (i, 0)), w3, bsp, vec],
        out_shape=[
            jax.ShapeDtypeStruct((S, 2 * SGU_W), BF16),
            jax.ShapeDtypeStruct((4, SGU_CHUNK, SGU_CHUNK), F32),
            jax.ShapeDtypeStruct((SGU_CHUNK, SGU_W), F32),
            jax.ShapeDtypeStruct((1, SGU_W), F32),
        ],
        compiler_params=_params(("arbitrary",)),
    )(rest, rest, gn.reshape(1, SGU_W), wm, wm_t, bias, dsg)


_GT = 512
_G0 = OFF_G // _GT


def _gate_specs(tm, col_of):
    specs = [pl.BlockSpec((tm, _GT), functools.partial(lambda k, *ids: (col_of(*ids)[0], _G0 + 2 * k + col_of(*ids)[1]), k)) for k in range(3)]
    specs += [pl.BlockSpec((1, _GT), functools.partial(lambda k, *ids: (0, 2 * k + col_of(*ids)[1]), k)) for k in range(3)]
    return specs


def _merge_fwd(rest, bg, ya, yb, yc, *, name):
    S = rest.shape[0]
    tm = _tile(S, 512)

    def body(g1, g2, g3, b1, b2, b3, ya_ref, yb_ref, yc_ref, o_ref):
        acc = _sigmoid(g1[...] + b1[...]) * ya_ref[...]
        acc = acc + _sigmoid(g2[...] + b2[...]) * yb_ref[...]
        acc = acc + _sigmoid(g3[...] + b3[...]) * yc_ref[...]
        o_ref[...] = acc.astype(BF16)

    blk = pl.BlockSpec((tm, _GT), lambda i, j: (i, j))
    return pl.pallas_call(
        body,
        name=name,
        grid=(S // tm, D // _GT),
        in_specs=_gate_specs(tm, lambda i, j: (i, j)) + [blk, blk, blk],
        out_specs=blk,
        out_shape=jax.ShapeDtypeStruct((S, D), BF16),
        compiler_params=_params(("parallel", "parallel")),
    )(rest, rest, rest, bg, bg, bg, ya, yb, yc)


def _merge_bwd(rest, bg, ya, yb, yc, dm, *, name):
    S = rest.shape[0]
    tm = _tile(S, 512)

    def body(g1, g2, g3, b1, b2, b3, ya_ref, yb_ref, yc_ref, dm_ref, dya, dyb, dyc, dg1, dg2, dg3, db1, db2, db3):
        first = pl.program_id(1) == 0
        dmv = dm_ref[...]
        for g_ref, b_ref, y_ref, dy_ref, dg_ref, db_ref in (
            (g1, b1, ya_ref, dya, dg1, db1), (g2, b2, yb_ref, dyb, dg2, db2), (g3, b3, yc_ref, dyc, dg3, db3)):
            gate = _sigmoid(g_ref[...] + b_ref[...])
            dy_ref[...] = (dmv * gate).astype(BF16)
            dpre = dmv * y_ref[...] * gate * (1.0 - gate)
            dg_ref[...] = dpre.astype(BF16)
            part = jnp.sum(dpre, axis=0, keepdims=True)

            @pl.when(first)
            def _():
                db_ref[...] = part

            @pl.when(jnp.logical_not(first))
            def _():
                db_ref[...] += part

    blk = pl.BlockSpec((tm, _GT), lambda j, i: (i, j))
    vec = pl.BlockSpec((1, _GT), lambda j, i: (0, j))
    big = jax.ShapeDtypeStruct((S, D), BF16)
    small = jax.ShapeDtypeStruct((1, D), F32)
    return pl.pallas_call(
        body,
        name=name,
        grid=(D // _GT, S // tm),
        in_specs=_gate_specs(tm, lambda j, i: (i, j)) + [blk, blk, blk, blk],
        out_specs=[blk] * 6 + [vec] * 3,
        out_shape=[big] * 6 + [small] * 3,
        compiler_params=_params(("parallel", "arbitrary")),
    )(rest, rest, rest, bg, bg, bg, ya, yb, yc, dm)


_X_SCALE = XDH ** -0.5


def _xattn_fwd(xq, kv, *, name):
    S = xq.shape[0]
    M = kv.shape[0]
    tq = _tile(S, 512)

    def body(q_ref, k_ref, v_ref, o_ref):
        s = _dot(q_ref[...], k_ref[...], 1, 1) * _X_SCALE
        e = jnp.exp(s - jnp.max(s, axis=-1, keepdims=True))
        p = e / jnp.sum(e, axis=-1, keepdims=True)
        o_ref[...] = _dot(p.astype(BF16), v_ref[...], 1, 0).astype(BF16)

    return pl.pallas_call(
        body,
        name=name,
        grid=(S // tq, XH),
        in_specs=[
            pl.BlockSpec((tq, XDH), lambda i, h: (i, h)),
            pl.BlockSpec((M, XDH), lambda i, h: (0, h)),
            pl.BlockSpec((M, XDH), lambda i, h: (0, XH + h)),
        ],
        out_specs=pl.BlockSpec((tq, XDH), lambda i, h: (i, h)),
        out_shape=jax.ShapeDtypeStruct((S, D), BF16),
        compiler_params=_params(("parallel", "parallel")),
    )(xq, kv, kv)


def _xattn_bwd(xq, kv, do, *, name):
    S = xq.shape[0]
    M = kv.shape[0]
    tq = _tile(S, 512)

    def body(q_ref, k_ref, v_ref, do_ref, dq_ref, dk_ref, dv_ref):
        qb = q_ref[...]
        kb = k_ref[...]
        dob = do_ref[...]
        s = _dot(qb, kb, 1, 1) * _X_SCALE
        e = jnp.exp(s - jnp.max(s, axis=-1, keepdims=True))
        p = e / jnp.sum(e, axis=-1, keepdims=True)
        dp = _dot(dob, v_ref[...], 1, 1)
        ds = (p * (dp - jnp.sum(p * dp, axis=-1, keepdims=True)) * _X_SCALE).astype(BF16)
        dq_ref[...] = _dot(ds, kb, 1, 0).astype(BF16)
        dk_part = _dot(ds, qb, 0, 0)
        dv_part = _dot(p.astype(BF16), dob, 0, 0)

        @pl.when(pl.program_id(1) == 0)
        def _():
            dk_ref[...] = dk_part
            dv_ref[...] = dv_part

        @pl.when(pl.program_id(1) > 0)
        def _():
            dk_ref[...] += dk_part
            dv_ref[...] += dv_part

    qspec = pl.BlockSpec((tq, XDH), lambda h, i: (i, h))
    kspec = pl.BlockSpec((M, XDH), lambda h, i: (0, h))
    dxq, dxk, dxv = pl.pallas_call(
        body,
        name=name,
        grid=(XH, S // tq),
        in_specs=[qspec, kspec, pl.BlockSpec((M, XDH), lambda h, i: (0, XH + h)), qspec],
        out_specs=[qspec, kspec, kspec],
        out_shape=[jax.ShapeDtypeStruct((S, D), BF16), jax.ShapeDtypeStruct((M, D), F32), jax.ShapeDtypeStruct((M, D), F32)],
        compiler_params=_params(("parallel", "arbitrary")),
    )(xq, kv, kv, do)
    return dxq, jnp.concatenate([dxk, dxv], axis=1)


def _adam_math(w, g, m, v):
    m = ADAM_B1 * m + (1.0 - ADAM_B1) * g
    v = ADAM_B2 * v + (1.0 - ADAM_B2) * (g * g)
    m_hat = m / (1.0 - ADAM_B1 ** ADAM_STEP)
    v_hat = v / (1.0 - ADAM_B2 ** ADAM_STEP)
    delta = -ADAM_LR * (m_hat / (jnp.sqrt(v_hat) + ADAM_EPS) + ADAM_WD * w)
    return delta, m, v


def _adamw_sharded(parts, w, m, v, *, name):
    _, R, C = w.shape
    Cp = parts[0].shape[2]
    tm = _tile(R, 256)
    nr = R // tm

    def body(p0_ref, p1_ref, w_ref, m_ref, v_ref, g_ref, d_ref, mo_ref, vo_ref):
        def update(p_ref):
            g = p_ref[0][:, :C].astype(F32)
            for dev in range(1, N_DEV):
                g = g + p_ref[dev][:, :C].astype(F32)
            delta, mn, vn = _adam_math(w_ref[...], g, m_ref[...], v_ref[...])
            g_ref[...] = g
            d_ref[...] = delta
            mo_ref[...] = mn
            vo_ref[...] = vn

        @pl.when(pl.program_id(0) == 0)
        def _():
            update(p0_ref)

        @pl.when(pl.program_id(0) == 1)
        def _():
            update(p1_ref)

    p0 = pl.BlockSpec((N_DEV, tm, Cp), lambda l, i: (0, i * (1 - l) + (nr - 1) * l, 0))
    p1 = pl.BlockSpec((N_DEV, tm, Cp), lambda l, i: (0, i * l, 0))
    blk = pl.BlockSpec((None, tm, C), lambda l, i: (l, i, 0))
    sds = jax.ShapeDtypeStruct(w.shape, F32)
    return pl.pallas_call(
        body,
        name=name,
        grid=(DEPTH, nr),
        in_specs=[p0, p1, blk, blk, blk],
        out_specs=[blk] * 4,
        out_shape=[sds] * 4,
        compiler_params=_params(("arbitrary", "arbitrary")),
    )(parts[0], parts[1], w, m, v)


def _adamw_small(g, w, m, v, *, name):
    n = len(g)

    def body(*refs):
        g_refs, w_refs, m_refs, v_refs = (refs[k * n:(k + 1) * n] for k in range(4))
        d_out, m_out, v_out = (refs[(4 + k) * n:(5 + k) * n] for k in range(3))
        for t in range(n):
            delta, mn, vn = _adam_math(w_refs[t][...], g_refs[t][...], m_refs[t][...], v_refs[t][...])
            d_out[t][...] = delta
            m_out[t][...] = mn
            v_out[t][...] = vn

    vm = pl.BlockSpec(memory_space=pltpu.VMEM)
    shapes = [jax.ShapeDtypeStruct(a.shape, F32) for a in w]
    outs = pl.pallas_call(
        body,
        name=name,
        in_specs=[vm] * (4 * n),
        out_specs=[vm] * (3 * n),
        out_shape=shapes * 3,
        compiler_params=pltpu.CompilerParams(vmem_limit_bytes=VMEM_LIMIT),
    )(*g, *w, *m, *v)
    return outs[:n], outs[n:2 * n], outs[2 * n:]


def _position():
    return lax.axis_index("x"), lax.axis_index("y"), lax.axis_index("c")


def _dev_index(px, py, pc):
    return 4 * px + 2 * py + pc


_ANY = pl.BlockSpec(memory_space=pl.ANY)


def _all_gather(shards, *, name):
    n = len(shards)
    out_shape = [jax.ShapeDtypeStruct((N_DEV, *s.shape), s.dtype) for s in shards]
    n_pieces = len(_pieces(out_shape))

    def body(*refs):
        ins, outs = refs[:n], refs[n:2 * n]
        send_sems, recv_sems, local_sems = refs[2 * n:]
        x, y, c = _position()
        me, sibling = (x, y, c), (x, y, 1 - c)
        chips = [(1 - x, y), (x, 1 - y), (1 - x, 1 - y)]
        pieces = _pieces(outs)

        def copy(i, k, block, to, from_input=False):
            t, rows = pieces[i]
            dst = _cut(outs[t].at[_dev_index(*block)], rows)
            return pltpu.make_async_remote_copy(
                src_ref=_cut(ins[t], rows) if from_input else dst, dst_ref=dst, send_sem=send_sems.at[i, k],
                recv_sem=recv_sems.at[i, k], device_id=to, device_id_type=MESH)

        mine = [pltpu.make_async_copy(_cut(ins[t], rows), _cut(outs[t].at[_dev_index(*me)], rows), local_sems.at[i])
                for i, (t, rows) in enumerate(pieces)]
        for cp in mine:
            cp.start()
        started = []
        for j, chip in enumerate(chips):
            for i in range(n_pieces):
                started.append(copy(i, 1 + j, me, (*chip, c), from_input=True))
                started[-1].start()
        for i in range(n_pieces):
            started.append(copy(i, 0, me, sibling, from_input=True))
            started[-1].start()
        for j, chip in enumerate(chips):
            for i in range(n_pieces):
                copy(i, 1 + j, (*chip, c), me).wait_recv()
                started.append(copy(i, 4 + j, (*chip, c), sibling))
                started[-1].start()
        for i in range(n_pieces):
            copy(i, 0, sibling, me).wait_recv()
        for j, chip in enumerate(chips):
            for i in range(n_pieces):
                copy(i, 4 + j, (*chip, 1 - c), me).wait_recv()
        for cp in started:
            cp.wait_send()
        for cp in mine:
            cp.wait()

    return pl.pallas_call(
        body,
        name=name,
        in_specs=[_ANY] * n,
        out_specs=[_ANY] * n,
        out_shape=out_shape,
        scratch_shapes=[pltpu.SemaphoreType.DMA((n_pieces, 7)), pltpu.SemaphoreType.DMA((n_pieces, 7)),
                        pltpu.SemaphoreType.DMA((n_pieces,))],
        compiler_params=pltpu.CompilerParams(has_side_effects=True),
    )(*shards)


def _peers(x, y, c):
    out = []
    for mask in range(1, N_DEV):
        fx, fy, fc = (mask >> 2) & 1, (mask >> 1) & 1, mask & 1
        out.append((1 - x if fx else x, 1 - y if fy else y, 1 - c if fc else c))
    return out


_HBM = pl.BlockSpec(memory_space=pltpu.HBM)
_SEM = pl.BlockSpec(memory_space=pltpu.SEMAPHORE)


def _own_block_placed(block, like):
    x, y, c = _position()
    return lax.dynamic_update_index_in_dim(lax.empty(like.shape, like.dtype), block, _dev_index(x, y, c), 0)


_COPY_BYTES = 256 << 10
_MAX_PIECES = 8


def _pieces(blocks):
    out = []
    for t, b in enumerate(blocks):
        R, C = b.shape[-2:]
        n = max(1, min(_MAX_PIECES, R * C * jnp.dtype(b.dtype).itemsize // _COPY_BYTES))
        while n > 1 and R % (16 * n):
            n -= 1
        out += [(t, pl.ds(j * (R // n), R // n) if n > 1 else None) for j in range(n)]
    return out


def _cut(block, rows):
    return block if rows is None else block.at[rows]


def _copies(per_piece):
    def mark(fn):
        fn.per_piece = per_piece
        return fn
    return mark


@_copies(N_DEV - 1)
def _plan_exchange(srcs, lands, send_sems, recv_sems, arrivals):
    x, y, c = _position()
    me = _dev_index(x, y, c)
    out = []
    for k, peer in enumerate(_peers(x, y, c)):
        p = _dev_index(*peer)
        for i, (t, rows) in enumerate(_pieces(lands)):
            sems = dict(send_sem=send_sems.at[7 * i + k], recv_sem=recv_sems.at[7 * i + k], device_id=peer, device_id_type=MESH)
            src, dst = (lands[t].at[p], lands[t].at[p]) if arrivals else (srcs[t].at[p], lands[t].at[me])
            out.append(pltpu.make_async_remote_copy(src_ref=_cut(src, rows), dst_ref=_cut(dst, rows), **sems))
    return out


@_copies(N_DEV - 1)
def _plan_broadcast(srcs, lands, send_sems, recv_sems, arrivals):
    x, y, c = _position()
    me = _dev_index(x, y, c)
    out = []
    for k, peer in enumerate(_peers(x, y, c)):
        p = _dev_index(*peer)
        for i, (t, rows) in enumerate(_pieces(lands)):
            sems = dict(send_sem=send_sems.at[7 * i + k], recv_sem=recv_sems.at[7 * i + k], device_id=peer, device_id_type=MESH)
            src, dst = (lands[t].at[p], lands[t].at[p]) if arrivals else (srcs[t], lands[t].at[me])
            out.append(pltpu.make_async_remote_copy(src_ref=_cut(src, rows), dst_ref=_cut(dst, rows), **sems))
    return out


@_copies(4)
def _plan_gather_out(srcs, lands, send_sems, recv_sems, arrivals):
    x, y, c = _position()
    me = _dev_index(x, y, c)
    out = []
    for k, peer in enumerate([(x, y, 1 - c), (1 - x, y, c), (x, 1 - y, c), (1 - x, 1 - y, c)]):
        p = _dev_index(*peer)
        for i, (t, rows) in enumerate(_pieces(lands)):
            sems = dict(send_sem=send_sems.at[4 * i + k], recv_sem=recv_sems.at[4 * i + k], device_id=peer, device_id_type=MESH)
            src, dst = (lands[t].at[p], lands[t].at[p]) if arrivals else (srcs[t], lands[t].at[me])
            out.append(pltpu.make_async_remote_copy(src_ref=_cut(src, rows), dst_ref=_cut(dst, rows), **sems))
    return out


@_copies(3)
def _plan_gather_pass(srcs, lands, send_sems, recv_sems, arrivals):
    x, y, c = _position()
    sibling = (x, y, 1 - c)
    out = []
    for k, chip in enumerate([(1 - x, y), (x, 1 - y), (1 - x, 1 - y)]):
        p = _dev_index(*chip, 1 - c) if arrivals else _dev_index(*chip, c)
        for i, (t, rows) in enumerate(_pieces(lands)):
            sems = dict(send_sem=send_sems.at[3 * i + k], recv_sem=recv_sems.at[3 * i + k], device_id=sibling, device_id_type=MESH)
            block = _cut(lands[t].at[p], rows)
            out.append(pltpu.make_async_remote_copy(src_ref=block, dst_ref=block, **sems))
    return out


def _split_start(plan, srcs, lands, *, after=None, name):
    n_src, n = len(srcs), len(srcs) + len(lands)
    n_sem = plan.per_piece * len(_pieces(lands))
    order = [] if after is None else [after]

    def body(*refs):
        send_sems, recv_sems = refs[n + len(order):n + len(order) + 2]
        token = refs[-1]
        for cp in plan(refs[:n_src], refs[n_src:n], send_sems, recv_sems, arrivals=False):
            cp.start()
        token[...] = jnp.zeros_like(token)

    hbm = lambda a: pltpu.HBM(a.shape, a.dtype)
    outs = pl.pallas_call(
        body,
        name=name,
        in_specs=[_HBM] * n + [_ANY] * len(order),
        out_specs=[_SEM, _SEM] + [_HBM] * n + [pl.BlockSpec(memory_space=pltpu.VMEM)],
        out_shape=[pltpu.SemaphoreType.DMA((n_sem,)), pltpu.SemaphoreType.DMA((n_sem,))] + [hbm(a) for a in (*srcs, *lands)]
        + [jax.ShapeDtypeStruct(_TOKEN, F32)],
        input_output_aliases={i: 2 + i for i in range(n)},
        compiler_params=pltpu.CompilerParams(has_side_effects=pltpu.SideEffectType.DATAFLOW_SIDE_EFFECTING),
    )(*[pltpu.with_memory_space_constraint(a, pltpu.HBM) for a in (*srcs, *lands)], *order)
    return (outs[0], outs[1], outs[2:2 + n_src], outs[2 + n_src:2 + n]), outs[-1]


def _split_wait(plan, state, after, *, name):
    send_sems, recv_sems, srcs, lands = state
    n_src, n = len(srcs), len(srcs) + len(lands)

    def body(*refs):
        send_refs, recv_refs = refs[n:n + 2]
        for cp in plan(refs[:n_src], refs[n_src:n], send_refs, recv_refs, arrivals=False):
            cp.wait_send()
        for cp in plan(refs[:n_src], refs[n_src:n], send_refs, recv_refs, arrivals=True):
            cp.wait_recv()

    hbm = lambda a: pltpu.HBM(a.shape, a.dtype)
    outs = pl.pallas_call(
        body,
        name=name,
        in_specs=[_HBM] * n + [_SEM, _SEM, _ANY],
        out_specs=[_HBM] * n,
        out_shape=[hbm(a) for a in (*srcs, *lands)],
        input_output_aliases={i: i for i in range(n)},
        compiler_params=pltpu.CompilerParams(has_side_effects=pltpu.SideEffectType.DATAFLOW_SIDE_EFFECTING),
    )(*srcs, *lands, send_sems, recv_sems, after)
    return outs[n_src:]


def _sum_blocks(blocks, *, name):
    _, R, C = blocks.shape
    tm = next(R // n for n in (4, 3, 2, 1) if R % (8 * n) == 0)

    def body(b_ref, o_ref):
        g = b_ref[0]
        for dev in range(1, N_DEV):
            g = g + b_ref[dev]
        o_ref[...] = g

    return pl.pallas_call(
        body,
        name=name,
        grid=(R // tm,),
        in_specs=[pl.BlockSpec((N_DEV, tm, C), lambda i: (0, i, 0))],
        out_specs=pl.BlockSpec((tm, C), lambda i: (i, 0)),
        out_shape=jax.ShapeDtypeStruct((R, C), F32),
        compiler_params=_params(("parallel",)),
    )(blocks)


def _block_diag(w):
    out = jnp.zeros((POOL_W, POOL_W), w.dtype)
    for gi in range(4):
        out = out.at[64 * gi:64 * (gi + 1), 64 * gi:64 * (gi + 1)].set(w[gi])
    return out


def _layer_consts(sp, l):
    causal = jnp.tril(jnp.ones((SGU_CHUNK, SGU_CHUNK), F32))
    wm = (sp["sgu_w"][l] * causal[None]).astype(BF16)
    wbd = _block_diag(sp["pool_w"][l]).astype(BF16)
    return dict(
        wbd=wbd, wbd_t=wbd.T, wm=wm, wm_t=wm.transpose(0, 2, 1),
        sgu_bias=jnp.repeat(sp["sgu_b"][l].T, 64, axis=1),
        bpad=jnp.pad(sp["b_forget"][l], (0, F_LANES - FOX_H)).reshape(1, F_LANES),
        bg=sp["b_gate"][l].reshape(1, 3 * D),
    )


def _relu2(acc):
    return acc, jnp.square(jnp.maximum(acc, 0.0))


def _relu2_grad(acc, z):
    return (acc * 2.0 * jnp.maximum(z, 0.0),)


def _layer_fwd(l, x, h, mem, source, sp):
    S = x.shape[0]
    t = _tile(S, 256)
    c = _layer_consts(sp, l)
    n = f"l{l}_"
    W, after = source(l, "begin", x)
    if h is None:
        h, after = _rms_fwd(x, sp["norm_mix_g"][l], after=after, name=n + "norm_mix"), None
    qkv = _mm(h, W["qkv"], out_dtypes=(BF16,), after=after, name=n + "qkv")
    rest = _mm(h, W["rest"], name=n + "rest")
    pa = _pool_fwd(rest, c["wbd"], sp["pool_scale"][l], name=n + "pool")
    cum, cum_t = _fox_prep(rest, c["bpad"], name=n + "fox_prep")
    fk3 = cum_t[:FOX_H].reshape(FOX_H, S // t, t)
    o, lse = _fox_fwd(qkv, cum, fk3, name=n + "fox")
    more, _ = source(l, "attended", o)
    W.update(more)
    sg = _sgu_fwd(rest, sp["sgu_norm_g"][l], c["wm"], c["sgu_bias"], name=n + "sgu")
    more, after = source(l, "mixed", sg)
    W.update(more)
    ya = _mm(pa, W["ba"], out_dtypes=(BF16,), after=after, name=n + "branch_a")
    yb = _mm(o, W["bb"], out_dtypes=(BF16,), name=n + "branch_b")
    yc = _mm(sg, W["bc"], out_dtypes=(BF16,), name=n + "branch_c")
    merged = _merge_fwd(rest, c["bg"], ya, yb, yc, name=n + "merge")
    whole_rows = dict(epilogue=_add_norm, out_dtypes=(F32, BF16), tm=1024, tn=D)
    x1, hx = _mm(merged, W["out"], extras=(x,), row_extras=(sp["norm_xattn_g"][l].reshape(1, D),), name=n + "out", **whole_rows)
    hm = _rms_fwd(mem, sp["norm_mem_g"][l], name=n + "norm_mem")
    xq = _mm(hx, W["xq"], out_dtypes=(BF16,), name=n + "xq")
    kv = _mm(hm, W["xkv"], out_dtypes=(BF16,), name=n + "xkv")
    o2 = _xattn_fwd(xq, kv, name=n + "xattn")
    x2, hf = _mm(o2, W["xo"], extras=(x1,), row_extras=(sp["norm_ffn_g"][l].reshape(1, D),), name=n + "xo", **whole_rows)
    z, act = _mm(hf, W["ff1"], epilogue=_relu2, out_dtypes=(BF16, BF16), name=n + "ff1")
    _, after = source(l, "expanded", act)
    if l + 1 < DEPTH:
        x3, h_next = _mm(act, W["ff2"], extras=(x2,), row_extras=(sp["norm_mix_g"][l + 1].reshape(1, D),), after=after, name=n + "ff2",
                         **whole_rows)
    else:
        x3, h_next = _mm(act, W["ff2"], extras=(x2,), epilogue=_add, after=after, name=n + "ff2"), None
    saved = dict(x=x, h=h, qkv=qkv, rest=rest, pa=pa, cum=cum, fk3=fk3, o=o, lse=lse, sg=sg, ya=ya, yb=yb, yc=yc,
                 merged=merged, x1=x1, hx=hx, hm=hm, xq=xq, kv=kv, o2=o2, x2=x2, hf=hf, z=z, act=act, c=c)
    return x3, h_next, saved, W


def _layer_bwd(l, dx3, sv, mem, W, sp, grads_done):
    S = dx3.shape[0]
    c = sv["c"]
    n = f"l{l}b_"
    bf = dict(out_dtypes=(BF16,))
    gw, gs = {}, {}
    gw["ff2"] = _mm(sv["act"], dx3, ta=True, name=n + "dw_ff2", **bf)
    dz = _mm(dx3, W["ff2"], tb=True, extras=(sv["z"],), epilogue=_relu2_grad, name=n + "dz", **bf)
    gw["ff1"] = _mm(sv["hf"], dz, ta=True, shard_out=True, name=n + "dw_ff1", **bf)
    whole_rows = dict(epilogue=_norm_grad, out_dtypes=(F32, F32), row_outs=1, tm=1024, tn=D)
    gain = lambda key: (sp[key][l].reshape(1, D),)
    dx2, dg = _mm(dz, W["ff1"], tb=True, extras=(sv["x2"], dx3), row_extras=gain("norm_ffn_g"), name=n + "dhf", **whole_rows)
    gs["norm_ffn_g"] = dg.reshape(D)
    gw["xo"] = _mm(sv["o2"], dx2, ta=True, name=n + "dw_xo", **bf)
    do2 = _mm(dx2, W["xo"], tb=True, name=n + "do2", **bf)
    dxq, dkv = _xattn_bwd(sv["xq"], sv["kv"], do2, name=n + "dxattn")
    gw["xq"] = _mm(sv["hx"], dxq, ta=True, name=n + "dw_xq", **bf)
    gw["xkv"] = _mm(sv["hm"], dkv, ta=True, shard_out=True, name=n + "dw_xkv", **bf)
    dhm = _mm(dkv, W["xkv"], tb=True, name=n + "dhm")
    _, gs["norm_mem_g"] = _rms_bwd(mem, sp["norm_mem_g"][l], dhm, jnp.zeros_like(mem), name=n + "dnorm_mem")
    dx1, dg = _mm(dxq, W["xq"], tb=True, extras=(sv["x1"], dx2), row_extras=gain("norm_xattn_g"), name=n + "dhx", **whole_rows)
    gs["norm_xattn_g"] = dg.reshape(D)
    after, gw = grads_done(l, gw), {}
    gw["out"] = _mm(sv["merged"], dx1, ta=True, name=n + "dw_out", **bf)
    dm = _mm(dx1, W["out"], tb=True, after=after, name=n + "dmerged")
    dya, dyb, dyc, dg1, dg2, dg3, db1, db2, db3 = _merge_bwd(sv["rest"], c["bg"], sv["ya"], sv["yb"], sv["yc"], dm, name=n + "dmerge")
    gs["b_gate"] = jnp.concatenate([db1, db2, db3], axis=1).reshape(3 * D)
    gw["ba"] = _mm(sv["pa"], dya, ta=True, shard_out=True, name=n + "dw_ba", **bf)
    gw["bb"] = _mm(sv["o"], dyb, ta=True, shard_out=True, name=n + "dw_bb", **bf)
    gw["bc"] = _mm(sv["sg"], dyc, ta=True, shard_out=True, name=n + "dw_bc", **bf)
    after, gw = grads_done(l, gw), {}
    dpa = _mm(dya, W["ba"], tb=True, name=n + "dpa")
    do = _mm(dyb, W["bb"], tb=True, after=after, name=n + "do", **bf)
    dsg = _mm(dyc, W["bc"], tb=True, name=n + "dsg")
    da, dwbd, dscale = _pool_bwd(sv["rest"], c["wbd"], c["wbd_t"], sp["pool_scale"][l], dpa, name=n + "dpool")
    gs["pool_w"] = jnp.stack([dwbd[64 * gi:64 * (gi + 1), 64 * gi:64 * (gi + 1)] for gi in range(4)])
    gs["pool_scale"] = dscale.reshape(POOL_W)
    dq, dk, dv, dfq, dfk = _fox_bwd(sv["qkv"], sv["cum"], sv["fk3"], sv["o"], do, sv["lse"], name=n + "dfox")
    dcum = dfq + jnp.pad(dfk.reshape(FOX_H, S).T, ((0, 0), (0, F_LANES - FOX_H)))
    df, dbf = _fox_post(sv["rest"], c["bpad"], dcum, name=n + "dfox_post")
    gs["b_forget"] = dbf[0, :FOX_H]
    dc, dwm, dbias, dgn = _sgu_bwd(sv["rest"], sp["sgu_norm_g"][l], c["wm"], c["wm_t"], c["sgu_bias"], dsg, name=n + "dsgu")
    gs["sgu_w"] = dwm * jnp.tril(jnp.ones((SGU_CHUNK, SGU_CHUNK), F32))[None]
    gs["sgu_b"] = dbias.reshape(SGU_CHUNK, 4, 64).sum(axis=2).T
    gs["sgu_norm_g"] = dgn.reshape(SGU_W)
    dqkv = [dq, dk, dv]
    drest = [jnp.concatenate([da, df, jnp.zeros((S, OFF_C - OFF_F - F_LANES), BF16), dc], axis=1), dg1, dg2, dg3]
    gw["qkv"] = _mm(sv["h"], dqkv, ta=True, name=n + "dw_qkv", **bf)
    gw["rest"] = _mm(sv["h"], drest, ta=True, name=n + "dw_rest", **bf)
    after = grads_done(l, gw)
    dh = _mm(dqkv, W["qkv"], tb=True, after=after, name=n + "dh_qkv")
    dx, dg = _mm(drest, W["rest"], tb=True, extras=(dh, sv["x"], dx1), row_extras=gain("norm_mix_g"), name=n + "dh",
                 **{**whole_rows, "epilogue": _add_norm_grad, "tm": 512})
    gs["norm_mix_g"] = dg.reshape(D)
    return dx, gs


def _local_step(x, mem, target, sp, source, grads_done):
    saved, Ws, h = [], [], None
    for l in range(DEPTH):
        x, h, sv, W = _layer_fwd(l, x, h, mem, source, sp)
        saved.append(sv)
        Ws.append(W)
    loss, dx, dgf = _final_loss(x, sp["final_norm_g"], target, name="final_loss")
    gss = [None] * DEPTH
    for l in reversed(range(DEPTH)):
        dx, gss[l] = _layer_bwd(l, dx, saved[l], mem, Ws[l], sp, grads_done)
    small = {k: jnp.stack([gss[l][k] for l in range(DEPTH)]) for k in gss[0]}
    small["final_norm_g"] = dgf
    return loss, dx, small


_SMALL = ["norm_mix_g", "b_forget", "pool_w", "pool_scale", "sgu_norm_g", "sgu_w", "sgu_b", "b_gate", "norm_xattn_g",
          "norm_mem_g", "norm_ffn_g", "final_norm_g"]
_COL = {"w_branch_a": "ba", "w_branch_b": "bb", "w_branch_c": "bc", "w_xkv": "xkv", "w_ff1": "ff1"}
_ROW = {"w_out": "out", "w_xq": "xq", "w_xo": "xo", "w_ff2": "ff2"}
_BIG = ["w_in", "w_branch_a", "w_branch_b", "w_branch_c", "w_out", "w_xq", "w_xkv", "w_xo", "w_ff1", "w_ff2"]
_PACK_LANES = 128


def _as_rows(a):
    return a.reshape(-1, a.shape[-1])


def _pack(tensors):
    rows = []
    for a in tensors:
        flat = a.reshape(-1)
        flat = jnp.pad(flat, (0, (-flat.shape[0]) % (8 * _PACK_LANES)))
        rows.append(flat.reshape(-1, _PACK_LANES))
    n_rows = sum(r.shape[0] for r in rows)
    rows.append(jnp.zeros(((-n_rows) % (8 * N_DEV), _PACK_LANES), F32))
    return jnp.concatenate(rows, axis=0)


def _unpack(packed, like):
    out, r = [], 0
    for a in like:
        size = math.prod(a.shape)
        nr = 8 * (-(-size // (8 * _PACK_LANES)))
        out.append(packed[r:r + nr].reshape(-1)[:size].reshape(a.shape))
        r += nr
    return out


_SHARD_IN = N_IN // N_DEV
_SHARD_IN_PAD = -(-_SHARD_IN // 128) * 128


def _columns(pieces, start, stop):
    out, at = [], 0
    for p in pieces:
        lo, hi = max(start, at), min(stop, at + p.shape[1])
        if lo < hi:
            out.append(p[:, lo - at:hi - at])
        at += p.shape[1]
    return out


def _split_w_in(blocks):
    K = blocks[0].shape[0]
    pad = jnp.zeros((K, OFF_C - OFF_F - FOX_H), blocks[0].dtype)
    cols = functools.partial(_columns, blocks)
    rest = jnp.concatenate(cols(0, R_OFF_Q) + cols(R_OFF_F, R_OFF_C) + [pad] + cols(R_OFF_C, N_IN), axis=1)
    return jnp.concatenate(cols(R_OFF_Q, R_OFF_F), axis=1), rest


def _join_w_in(qkv, rest):
    in_order = [rest[:, :R_OFF_Q], qkv, rest[:, OFF_F:OFF_F + FOX_H], rest[:, OFF_C:]]
    pad = jnp.zeros((qkv.shape[0], _SHARD_IN_PAD - _SHARD_IN), qkv.dtype)
    return jnp.stack([jnp.concatenate(_columns(in_order, _SHARD_IN * d, _SHARD_IN * (d + 1)) + [pad], axis=1) for d in range(N_DEV)])


_FIRST = ["w_in"]
_LATER = [k for k in _BIG if k not in _FIRST]


def _layer_weights(gathered):
    W = {}
    if "w_in" in gathered:
        W.update(zip(("qkv", "rest"), _split_w_in([gathered["w_in"][d][:, :_SHARD_IN] for d in range(N_DEV)])))
    for name, key in _COL.items():
        if name in gathered:
            W[key] = _Gathered(gathered[name])
    for name, key in _ROW.items():
        if name in gathered:
            W[key] = gathered[name].reshape(-1, gathered[name].shape[-1])
    return W


def _grad_blocks(gw):
    parts = {}
    if "qkv" in gw:
        parts["w_in"] = _join_w_in(gw["qkv"], gw["rest"])
    for name, key in _COL.items():
        if key in gw:
            parts[name] = gw[key]
    for name, key in _ROW.items():
        if key in gw:
            parts[name] = gw[key].reshape(N_DEV, -1, gw[key].shape[-1])
    return parts


def kernel(x, mem, norm_mix_g, w_in, b_forget, pool_w, pool_scale, sgu_norm_g, sgu_w, sgu_b, w_branch_a, w_branch_b, w_branch_c, b_gate, w_out, norm_xattn_g, norm_mem_g, w_xq, w_xkv, w_xo, norm_ffn_g, w_ff1, w_ff2, final_norm_g, loss_target, m_norm_mix_g, m_w_in, m_b_forget, m_pool_w, m_pool_scale, m_sgu_norm_g, m_sgu_w, m_sgu_b, m_w_branch_a, m_w_branch_b, m_w_branch_c, m_b_gate, m_w_out, m_norm_xattn_g, m_norm_mem_g, m_w_xq, m_w_xkv, m_w_xo, m_norm_ffn_g, m_w_ff1, m_w_ff2, m_final_norm_g, v_norm_mix_g, v_w_in, v_b_forget, v_pool_w, v_pool_scale, v_sgu_norm_g, v_sgu_w, v_sgu_b, v_w_branch_a, v_w_branch_b, v_w_branch_c, v_b_gate, v_w_out, v_norm_xattn_g, v_norm_mem_g, v_w_xq, v_w_xkv, v_w_xo, v_norm_ffn_g, v_w_ff1, v_w_ff2, v_final_norm_g):
    names = ["norm_mix_g", "w_in", "b_forget", "pool_w", "pool_scale", "sgu_norm_g", "sgu_w", "sgu_b", "w_branch_a", "w_branch_b",
             "w_branch_c", "b_gate", "w_out", "norm_xattn_g", "norm_mem_g", "w_xq", "w_xkv", "w_xo", "norm_ffn_g", "w_ff1", "w_ff2",
             "final_norm_g"]
    w = dict(zip(names, [norm_mix_g, w_in, b_forget, pool_w, pool_scale, sgu_norm_g, sgu_w, sgu_b, w_branch_a, w_branch_b, w_branch_c,
                         b_gate, w_out, norm_xattn_g, norm_mem_g, w_xq, w_xkv, w_xo, norm_ffn_g, w_ff1, w_ff2, final_norm_g]))
    m = dict(zip(names, [m_norm_mix_g, m_w_in, m_b_forget, m_pool_w, m_pool_scale, m_sgu_norm_g, m_sgu_w, m_sgu_b, m_w_branch_a,
                         m_w_branch_b, m_w_branch_c, m_b_gate, m_w_out, m_norm_xattn_g, m_norm_mem_g, m_w_xq, m_w_xkv, m_w_xo,
                         m_norm_ffn_g, m_w_ff1, m_w_ff2, m_final_norm_g]))
    v = dict(zip(names, [v_norm_mix_g, v_w_in, v_b_forget, v_pool_w, v_pool_scale, v_sgu_norm_g, v_sgu_w, v_sgu_b, v_w_branch_a,
                         v_w_branch_b, v_w_branch_c, v_b_gate, v_w_out, v_norm_xattn_g, v_norm_mem_g, v_w_xq, v_w_xkv, v_w_xo,
                         v_norm_ffn_g, v_w_ff1, v_w_ff2, v_final_norm_g]))

    sp = {k: w[k] for k in _SMALL}
    shards = [{k: w[k][l].astype(BF16) for k in _BIG} for l in range(DEPTH)]
    for sh in shards:
        sh["w_in"] = jnp.pad(sh["w_in"], ((0, 0), (0, _SHARD_IN_PAD - _SHARD_IN)))
    me = _dev_index(*_position())

    def gather_out(l, keys, name, after=None):
        srcs = [shards[l][k] for k in keys]
        lands = [_own_block_placed(a, jax.ShapeDtypeStruct((N_DEV, *a.shape), a.dtype)) for a in srcs]
        state, token = _split_start(_plan_gather_out, srcs, lands, after=after, name=name + "_out_start")
        return (keys, name, state), token

    def gather_pass(job, value):
        keys, name, state = job
        lands = _split_wait(_plan_gather_out, state, value, name=name + "_out_wait")
        state, token = _split_start(_plan_gather_pass, [], lands, name=name + "_pass_start")
        return (keys, name, state), token, lands[0]

    def gather_end(job, value):
        keys, name, state = job
        return _layer_weights(dict(zip(keys, _split_wait(_plan_gather_pass, state, value, name=name + "_pass_wait"))))

    jobs = {}

    def source(l, point, value):
        if (l, point) == (0, "begin"):
            first = _all_gather([shards[0][k] for k in _FIRST], name="gather_l0_first")
            jobs["l0"], token = gather_out(0, _LATER, "gather_l0", after=first[0])
            return _layer_weights(dict(zip(_FIRST, first))), token
        if (l, point) == (0, "attended"):
            jobs["l0"], _, arrived = gather_pass(jobs["l0"], value)
            jobs["l1_first"], token = gather_out(1, _FIRST, "gather_l1_first", after=arrived)
            jobs["l1"], jobs["token"] = gather_out(1, _LATER, "gather_l1", after=token)
            return {}, None
        if (l, point) == (0, "mixed"):
            return gather_end(jobs.pop("l0"), value), jobs.pop("token")
        if (l, point) == (0, "expanded"):
            jobs["l1_first"], token, _ = gather_pass(jobs["l1_first"], value)
            return {}, token
        if (l, point) == (1, "begin"):
            W = gather_end(jobs.pop("l1_first"), value)
            jobs["l1"], token, _ = gather_pass(jobs["l1"], value)
            return W, token
        if (l, point) == (1, "mixed"):
            return gather_end(jobs.pop("l1"), value), None
        return {}, None

    received = [{} for _ in range(DEPTH)]
    travelling = []

    def grads_done(l, gw):
        blocks = _grad_blocks(gw)
        keys = [k for k in _BIG if k in blocks]
        parts = [blocks[k] for k in keys]
        group = f"exchange_grads_l{l}_" + ("in" if "w_in" in blocks else "merge" if "w_out" in blocks else "mlp")
        lands = [_own_block_placed(lax.dynamic_index_in_dim(p, me, 0, keepdims=False), p) for p in parts]
        state, token = _split_start(_plan_exchange, parts, lands, name=group + "_start")
        travelling.append((l, keys, state, group + "_wait"))
        return token

    loss, dx, small = _local_step(x[0], mem[0], loss_target[0], sp, source, grads_done)
    grads, deltas, new_m, new_v = {}, {}, {}, {}
    like = [loss] + [w[k] for k in _SMALL]
    packed = _pack([loss] + [small[k] for k in _SMALL])
    eighths = packed.reshape(N_DEV, -1, _PACK_LANES)
    own = lambda a: _own_block_placed(lax.dynamic_index_in_dim(a, me, 0, keepdims=False) if a.ndim == 3 else a, eighths)
    scatter, done = _split_start(_plan_exchange, [eighths], [own(eighths)], after=dx, name="small_grads_scatter_start")

    def reduce_small(after):
        mine = _sum_blocks(_split_wait(_plan_exchange, scatter, after, name="small_grads_scatter_wait")[0], name="small_grads_sum")
        return _split_start(_plan_broadcast, [mine], [own(mine)], name="small_grads_gather_start")

    def update_small(state, after):
        total = _split_wait(_plan_broadcast, state, after, name="small_grads_gather_wait")[0].reshape(packed.shape)
        loss_sum, *g_small = _unpack(total, like)
        rows = lambda d: [_as_rows(d[k]) for k in _SMALL]
        outs = _adamw_small([_as_rows(g) for g in g_small], rows(w), rows(m), rows(v), name="adamw_small")
        grads.update(zip(_SMALL, g_small))
        for dst, vals in zip((deltas, new_m, new_v), outs):
            dst.update({k: a.reshape(w[k].shape) for k, a in zip(_SMALL, vals)})
        return loss_sum[0, 0], outs[0][0]

    groups = list(dict.fromkeys(tuple(keys) for _, keys, _, _ in travelling))
    for n_done, group_keys in enumerate(groups):
        if n_done == 1:
            gather, _ = reduce_small(done)
        if n_done == len(groups) - 1:
            loss, done = update_small(gather, done)
        for l, keys, state, wait_name in travelling:
            if tuple(keys) == group_keys:
                received[l].update(zip(keys, _split_wait(_plan_exchange, state, done, name=wait_name)))
        for k in group_keys:
            outs = _adamw_sharded([received[l][k] for l in range(DEPTH)], w[k], m[k], v[k], name="adamw_" + k)
            grads[k], deltas[k], new_m[k], new_v[k] = outs
        done = grads[group_keys[-1]]

    return (loss, dx[None], *[grads[k] for k in names], *[deltas[k] for k in names], *[new_m[k] for k in names],
            *[new_v[k] for k in names])
```

```python
import functools
import math

import jax
import jax.numpy as jnp
from jax import lax
from jax.experimental import pallas as pl
from jax.experimental.pallas import tpu as pltpu

F32 = jnp.float32
BF16 = jnp.bfloat16
MESH = pl.DeviceIdType.MESH

N_DEV = 8
D = 1024
DEPTH = 2
EPS = 1e-6
NEG = -1e30
POOL_W = 256
FOX_H = 8
FOX_DH = 64
FOX_W = 512
SGU_W = 256
SGU_CHUNK = 128
XH = 4
XDH = 256
N_IN = 5384
R_OFF_Q, R_OFF_F, R_OFF_C = 256, 1792, 1800
QKV_W = 3 * FOX_W
OFF_A, OFF_F, OFF_C, OFF_G, REST_W = 0, 256, 512, 1024, 4096
F_LANES = 128

ADAM_LR = 0.001
ADAM_B1 = 0.9
ADAM_B2 = 0.999
ADAM_EPS = 1e-08
ADAM_WD = 0.01
ADAM_STEP = 10

VMEM_LIMIT = 56 * 1024 * 1024


def _tile(n, pref):
    t = min(n, pref)
    while n % t:
        t -= 128
    assert t > 0, (n, pref)
    return t


def _params(sem=None):
    return pltpu.CompilerParams(dimension_semantics=sem, vmem_limit_bytes=VMEM_LIMIT)


def _dot(a, b, ca, cb):
    return lax.dot_general(a, b, (((ca,), (cb,)), ((), ())), preferred_element_type=F32)


def _sigmoid(z):
    return 1.0 / (1.0 + jnp.exp(-z))


_GELU_K = math.sqrt(2.0 / math.pi)
_GELU_C = 0.044715


def _gelu(x):
    return 0.5 * x * (1.0 + jnp.tanh(_GELU_K * (x + _GELU_C * x * x * x)))


def _gelu_grad(x):
    t = jnp.tanh(_GELU_K * (x + _GELU_C * x * x * x))
    return 0.5 * (1.0 + t) + 0.5 * x * (1.0 - t * t) * _GELU_K * (1.0 + 3.0 * _GELU_C * x * x)


def _rows(shape):
    return lax.broadcasted_iota(jnp.int32, shape, 0)


def _lanes(shape):
    return lax.broadcasted_iota(jnp.int32, shape, 1)


class _Gathered:
    def __init__(self, arr):
        self.arr = arr
        self.shape = (arr.shape[1], N_DEV * arr.shape[2])


_TOKEN = (8, 128)


def _mm(a, b, *, ta=False, tb=False, extras=(), row_extras=(), epilogue=None, out_dtypes=(F32,), row_outs=0, shard_out=False, after=None,
        tm=None, tn=512, tk=None, name):
    a_parts = list(a) if isinstance(a, (list, tuple)) else [a]
    b_parts = list(b) if isinstance(b, (list, tuple)) else [b]
    gathered = isinstance(b, _Gathered)
    assert (len(a_parts) == 1 or not ta) and (len(b_parts) == 1 or not tb) and min(len(a_parts), len(b_parts)) == 1
    a0, b0 = a_parts[0], b_parts[0]
    M, K = (a0.shape[1], a0.shape[0]) if ta else (a0.shape[0], a0.shape[1] * len(a_parts))
    N, Kb = b0.shape if tb else (b0.shape[1] * len(b_parts), b0.shape[0])
    assert Kb == K, (a0.shape, b0.shape, ta, tb)
    if gathered:
        if tb:
            tk = b.arr.shape[2]
        else:
            tn = b.arr.shape[2]
    if len(a_parts) > 1:
        tk = a0.shape[1]
    if shard_out:
        tn = N // N_DEV
    tm = _tile(M, tm or (1024 if ta else 2048))
    tn = _tile(b0.shape[1] if len(b_parts) > 1 else N, tn)
    per_piece = b0.shape[1] // tn
    size = lambda dt: jnp.dtype(dt).itemsize
    row_bytes = len(a_parts) * tm * size(a0.dtype) + len(b_parts) * tn * size(b.arr.dtype if gathered else b0.dtype)
    tile_bytes = tm * tn * (sum(size(e.dtype) for e in extras) + sum(map(size, out_dtypes)))

    def vmem_bytes(k_tile):
        return 2 * (k_tile * row_bytes + tile_bytes) + tm * tn * 4 * (K > k_tile)

    if tk is None:
        tk = next(c for c in (_tile(K, 2048), _tile(K, 1024), _tile(K, 512), _tile(K, 256)) if vmem_bytes(c) <= VMEM_LIMIT - (4 << 20))
    tk = _tile(K, tk)
    nk = K // tk
    ca, cb = (0 if ta else 1), (1 if tb else 0)
    n_a, n_b, n_ex, n_out = len(a_parts), len(b_parts), len(extras) + len(row_extras), len(out_dtypes)
    tokens = [] if after is None else [after]
    n_in = n_a + n_b + n_ex + len(tokens)
    if epilogue is None:
        epilogue = lambda acc: (acc,)

    def body(*refs):
        a_refs, b_refs = refs[:n_a], refs[n_a:n_a + n_b]
        ex_refs = refs[n_a + n_b:n_a + n_b + n_ex]
        o_refs = refs[n_in:n_in + n_out]
        j, k = pl.program_id(1), pl.program_id(2)

        def finish(acc):
            vals = epilogue(acc, *[e[...] for e in ex_refs])
            for o_ref, val in zip(o_refs[:n_out - row_outs], vals):
                o_ref[...] = val.astype(o_ref.dtype)
            for o_ref, val in zip(o_refs[n_out - row_outs:], vals[n_out - row_outs:]):
                first = pl.program_id(0) == 0
                o_ref[...] = jnp.where(first, val, o_ref[...] + val)

        def step(a_ref, b_ref):
            part = _dot(a_ref[...].astype(BF16), b_ref[...].astype(BF16), ca, cb)
            if nk == 1:
                finish(part)
            else:
                acc_ref = refs[-1]

                @pl.when(k == 0)
                def _():
                    acc_ref[...] = part

                @pl.when(k > 0)
                def _():
                    acc_ref[...] += part

                @pl.when(k == nk - 1)
                def _():
                    finish(acc_ref[...])

        if n_a > 1:
            for p in range(n_a):
                pl.when(k == p)(functools.partial(step, a_refs[p], b_refs[0]))
        elif n_b > 1:
            for p in range(n_b):
                pl.when(j // per_piece == p)(functools.partial(step, a_refs[0], b_refs[p]))
        else:
            step(a_refs[0], b_refs[0])

    if n_a > 1:
        a_specs = [pl.BlockSpec((tm, tk), lambda i, j, k: (i, 0))] * n_a
    else:
        a_specs = [pl.BlockSpec((tk, tm), lambda i, j, k: (k, i)) if ta else pl.BlockSpec((tm, tk), lambda i, j, k: (i, k))]
    if gathered:
        b_arrs = [b.arr]
        b_specs = [pl.BlockSpec((None, tn, tk), lambda i, j, k: (k, j, 0)) if tb else pl.BlockSpec((None, tk, tn), lambda i, j, k: (j, k, 0))]
    elif n_b > 1:
        b_arrs = b_parts
        b_specs = [pl.BlockSpec((tk, tn), functools.partial(lambda p, i, j, k: (k, jnp.clip(j - p * per_piece, 0, per_piece - 1)), p))
                   for p in range(n_b)]
    else:
        b_arrs = b_parts
        b_specs = [pl.BlockSpec((tn, tk), lambda i, j, k: (j, k)) if tb else pl.BlockSpec((tk, tn), lambda i, j, k: (k, j))]
    tile = pl.BlockSpec((tm, tn), lambda i, j, k: (i, j))
    if shard_out:
        out_specs = [pl.BlockSpec((None, tm, tn), lambda i, j, k: (j, i, 0))] * n_out
        out_shape = [jax.ShapeDtypeStruct((N_DEV, M, tn), dt) for dt in out_dtypes]
    else:
        assert row_outs == 0 or tn == N
        out_specs = [tile] * (n_out - row_outs) + [pl.BlockSpec((1, tn), lambda i, j, k: (0, j))] * row_outs
        out_shape = [jax.ShapeDtypeStruct((1, N) if t >= n_out - row_outs else (M, N), dt) for t, dt in enumerate(out_dtypes)]
    assert vmem_bytes(tk) <= VMEM_LIMIT - (4 << 20), (name, vmem_bytes(tk))
    outs = pl.pallas_call(
        body,
        name=name,
        grid=(M // tm, N // tn, nk),
        in_specs=a_specs + b_specs + [tile] * len(extras) + [pl.BlockSpec((1, tn), lambda i, j, k: (0, j))] * len(row_extras)
        + [pl.BlockSpec(_TOKEN, lambda i, j, k: (0, 0))] * len(tokens),
        out_specs=out_specs,
        out_shape=out_shape,
        scratch_shapes=[pltpu.VMEM((tm, tn), F32)] if nk > 1 else [],
        compiler_params=_params(("arbitrary",) * 3 if row_outs else ("parallel", "parallel", "arbitrary")),
    )(*a_parts, *b_arrs, *extras, *row_extras, *tokens)
    return outs[0] if n_out == 1 else outs


def _add(acc, res):
    return (acc + res,)


def _norm_grad(dh, x, dres, g):
    r = lax.rsqrt(jnp.mean(x * x, axis=-1, keepdims=True) + EPS)
    xn = x * r
    dxn = dh * g
    return r * (dxn - xn * jnp.mean(dxn * xn, axis=-1, keepdims=True)) + dres, jnp.sum(dh * xn, axis=0, keepdims=True)


def _add_norm_grad(acc, more, x, dres, g):
    return _norm_grad(acc + more, x, dres, g)


def _add_norm(acc, res, g):
    x = acc + res
    return x, x * lax.rsqrt(jnp.mean(x * x, axis=-1, keepdims=True) + EPS) * g


def _rms_fwd(x, g, *, after=None, name):
    R, C = x.shape
    tm = _tile(R, 256)
    tokens = [] if after is None else [after]

    def body(x_ref, g_ref, *rest):
        xv = x_ref[...]
        r = lax.rsqrt(jnp.mean(xv * xv, axis=-1, keepdims=True) + EPS)
        rest[-1][...] = (xv * r * g_ref[...]).astype(BF16)

    return pl.pallas_call(
        body,
        name=name,
        grid=(R // tm,),
        in_specs=[pl.BlockSpec((tm, C), lambda i: (i, 0)), pl.BlockSpec((1, C), lambda i: (0, 0))]
        + [pl.BlockSpec(_TOKEN, lambda i: (0, 0))] * len(tokens),
        out_specs=pl.BlockSpec((tm, C), lambda i: (i, 0)),
        out_shape=jax.ShapeDtypeStruct((R, C), BF16),
        compiler_params=_params(("parallel",)),
    )(x, g.reshape(1, C), *tokens)


def _rms_bwd(x, g, dh, dres, *, name):
    R, C = x.shape
    tm = _tile(R, 256)

    def body(x_ref, g_ref, dh_ref, dres_ref, dx_ref, dg_ref):
        xv = x_ref[...]
        r = lax.rsqrt(jnp.mean(xv * xv, axis=-1, keepdims=True) + EPS)
        xn = xv * r
        dh_v = dh_ref[...].astype(F32)
        dxn = dh_v * g_ref[...]
        dx_ref[...] = r * (dxn - xn * jnp.mean(dxn * xn, axis=-1, keepdims=True)) + dres_ref[...]
        part = jnp.sum(dh_v * xn, axis=0, keepdims=True)

        @pl.when(pl.program_id(0) == 0)
        def _():
            dg_ref[...] = part

        @pl.when(pl.program_id(0) > 0)
        def _():
            dg_ref[...] += part

    row = pl.BlockSpec((tm, C), lambda i: (i, 0))
    vec = pl.BlockSpec((1, C), lambda i: (0, 0))
    dx, dg = pl.pallas_call(
        body,
        name=name,
        grid=(R // tm,),
        in_specs=[row, vec, row, row],
        out_specs=[row, vec],
        out_shape=[jax.ShapeDtypeStruct((R, C), F32), jax.ShapeDtypeStruct((1, C), F32)],
        compiler_params=_params(("arbitrary",)),
    )(x, g.reshape(1, C), dh, dres)
    return dx, dg.reshape(C)


def _final_loss(x, g, target, *, name):
    R, C = x.shape
    tm = _tile(R, 256)

    def body(x_ref, g_ref, t_ref, loss_ref, dx_ref, dg_ref):
        xv = x_ref[...]
        r = lax.rsqrt(jnp.mean(xv * xv, axis=-1, keepdims=True) + EPS)
        xn = xv * r
        gv = g_ref[...]
        err = xn * gv - t_ref[...]
        lpart = (0.5 / C) * jnp.sum(jnp.sum(err * err, axis=1, keepdims=True), axis=0, keepdims=True)
        dy = err * (1.0 / C)
        dxn = dy * gv
        dx_ref[...] = r * (dxn - xn * jnp.mean(dxn * xn, axis=-1, keepdims=True))
        gpart = jnp.sum(dy * xn, axis=0, keepdims=True)

        @pl.when(pl.program_id(0) == 0)
        def _():
            loss_ref[...] = lpart
            dg_ref[...] = gpart

        @pl.when(pl.program_id(0) > 0)
        def _():
            loss_ref[...] += lpart
            dg_ref[...] += gpart

    row = pl.BlockSpec((tm, C), lambda i: (i, 0))
    vec = pl.BlockSpec((1, C), lambda i: (0, 0))
    loss, dx, dg = pl.pallas_call(
        body,
        name=name,
        grid=(R // tm,),
        in_specs=[row, vec, row],
        out_specs=[pl.BlockSpec((1, 1), lambda i: (0, 0)), row, vec],
        out_shape=[jax.ShapeDtypeStruct((1, 1), F32), jax.ShapeDtypeStruct((R, C), F32), jax.ShapeDtypeStruct((1, C), F32)],
        compiler_params=_params(("arbitrary",)),
    )(x, g.reshape(1, C), target)
    return loss, dx, dg.reshape(C)


def _pool_select(lane, vals):
    out = vals[3]
    for gi in (2, 1, 0):
        out = jnp.where(lane < 64 * (gi + 1), vals[gi], out)
    return out


def _pool_diff(a):
    row, lane = _rows(a.shape), _lanes(a.shape)

    def down(v, k):
        return jnp.where(row >= k, pltpu.roll(v, k, 0), 0.0)

    s2 = a + down(a, 1)
    s4 = s2 + down(s2, 2)
    s8 = s4 + down(s4, 4)
    s16 = s8 + down(s8, 8)
    wsum = _pool_select(lane, (s2, s4, s8, s16))
    win = _pool_select(lane, (2, 4, 8, 16))
    cnt = jnp.minimum(row + 1, win).astype(F32)
    return wsum / cnt - a, cnt


def _pool_diff_t(dd, cnt):
    S = dd.shape[0]
    row, lane = _rows(dd.shape), _lanes(dd.shape)

    def up(v, k):
        return jnp.where(row < S - k, pltpu.roll(v, S - k, 0), 0.0)

    e = dd / cnt
    s2 = e + up(e, 1)
    s4 = s2 + up(s2, 2)
    s8 = s4 + up(s4, 4)
    s16 = s8 + up(s8, 8)
    return _pool_select(lane, (s2, s4, s8, s16)) - dd


def _pool_fwd(rest, wbd, scale, *, name):
    S = rest.shape[0]

    def body(a_ref, w_ref, s_ref, o_ref):
        d, _ = _pool_diff(a_ref[...])
        yp = _dot(d.astype(BF16), w_ref[...], 1, 0)
        o_ref[...] = (yp * s_ref[...]).astype(BF16)

    return pl.pallas_call(
        body,
        name=name,
        grid=(1,),
        in_specs=[
            pl.BlockSpec((S, POOL_W), lambda i: (0, OFF_A // POOL_W)),
            pl.BlockSpec((POOL_W, POOL_W), lambda i: (0, 0)),
            pl.BlockSpec((1, POOL_W), lambda i: (0, 0)),
        ],
        out_specs=pl.BlockSpec((S, POOL_W), lambda i: (0, 0)),
        out_shape=jax.ShapeDtypeStruct((S, POOL_W), BF16),
        compiler_params=_params(("arbitrary",)),
    )(rest, wbd, scale.reshape(1, POOL_W))


def _pool_bwd(rest, wbd, wbd_t, scale, dpa, *, name):
    S = rest.shape[0]

    def body(a_ref, w_ref, wt_ref, s_ref, dpa_ref, da_ref, dw_ref, ds_ref):
        d, cnt = _pool_diff(a_ref[...])
        db = d.astype(BF16)
        yp = _dot(db, w_ref[...], 1, 0)
        dpa_v = dpa_ref[...]
        ds_ref[...] = jnp.sum(dpa_v * yp, axis=0, keepdims=True)
        dyp = (dpa_v * s_ref[...]).astype(BF16)
        dw_ref[...] = _dot(db, dyp, 0, 0)
        dd = _dot(dyp, wt_ref[...], 1, 0)
        da_ref[...] = _pool_diff_t(dd, cnt).astype(BF16)

    full = pl.BlockSpec((S, POOL_W), lambda i: (0, 0))
    sq = pl.BlockSpec((POOL_W, POOL_W), lambda i: (0, 0))
    vec = pl.BlockSpec((1, POOL_W), lambda i: (0, 0))
    return pl.pallas_call(
        body,
        name=name,
        grid=(1,),
        in_specs=[pl.BlockSpec((S, POOL_W), lambda i: (0, OFF_A // POOL_W)), sq, sq, vec, full],
        out_specs=[full, sq, vec],
        out_shape=[
            jax.ShapeDtypeStruct((S, POOL_W), BF16),
            jax.ShapeDtypeStruct((POOL_W, POOL_W), F32),
            jax.ShapeDtypeStruct((1, POOL_W), F32),
        ],
        compiler_params=_params(("arbitrary",)),
    )(rest, wbd, wbd_t, scale.reshape(1, POOL_W), dpa)


def _log_sigmoid(z):
    return jnp.minimum(z, 0.0) - jnp.log(1.0 + jnp.exp(-jnp.abs(z)))


_F_SPEC_COL = OFF_F // F_LANES


def _fox_prep(rest, bpad, *, name):
    S = rest.shape[0]

    def body(f_ref, b_ref, o_ref, ot_ref):
        acc = _log_sigmoid(f_ref[...] + b_ref[...])
        row = _rows(acc.shape)
        k = 1
        while k < S:
            acc = acc + jnp.where(row >= k, pltpu.roll(acc, k, 0), 0.0)
            k *= 2
        o_ref[...] = acc
        ot_ref[...] = acc.T

    return pl.pallas_call(
        body,
        name=name,
        grid=(1,),
        in_specs=[pl.BlockSpec((S, F_LANES), lambda i: (0, _F_SPEC_COL)), pl.BlockSpec((1, F_LANES), lambda i: (0, 0))],
        out_specs=[pl.BlockSpec((S, F_LANES), lambda i: (0, 0)), pl.BlockSpec((F_LANES, S), lambda i: (0, 0))],
        out_shape=[jax.ShapeDtypeStruct((S, F_LANES), F32), jax.ShapeDtypeStruct((F_LANES, S), F32)],
        compiler_params=_params(("arbitrary",)),
    )(rest, bpad)


def _fox_post(rest, bpad, dcum, *, name):
    S = rest.shape[0]

    def body(f_ref, b_ref, d_ref, df_ref, db_ref):
        acc = d_ref[...]
        row = _rows(acc.shape)
        k = 1
        while k < S:
            acc = acc + jnp.where(row < S - k, pltpu.roll(acc, S - k, 0), 0.0)
            k *= 2
        df = acc * (1.0 - _sigmoid(f_ref[...] + b_ref[...]))
        df_ref[...] = df.astype(BF16)
        db_ref[...] = jnp.sum(df, axis=0, keepdims=True)

    full = pl.BlockSpec((S, F_LANES), lambda i: (0, 0))
    vec = pl.BlockSpec((1, F_LANES), lambda i: (0, 0))
    return pl.pallas_call(
        body,
        name=name,
        grid=(1,),
        in_specs=[pl.BlockSpec((S, F_LANES), lambda i: (0, _F_SPEC_COL)), vec, full],
        out_specs=[full, vec],
        out_shape=[jax.ShapeDtypeStruct((S, F_LANES), BF16), jax.ShapeDtypeStruct((1, F_LANES), F32)],
        compiler_params=_params(("arbitrary",)),
    )(rest, bpad, dcum)


_FOX_SCALE = FOX_DH ** -0.5
_PAIRS = FOX_H // 2


def _scaled(v):
    return (v.astype(F32) * _FOX_SCALE).astype(BF16)


def _diag_mask(s):
    return jnp.where(_rows(s.shape) >= _lanes(s.shape), s, NEG)


def _fox_fwd(qkv, cum, fk3, *, name):
    S = qkv.shape[0]
    nk, t = fk3.shape[1:]

    def body(q_ref, k_ref, v_ref, cum_ref, fk_ref, o_ref, lse_ref):
        i = pl.program_id(0)
        lane = _lanes((t, 128))
        lo = lane < FOX_DH
        cumv = cum_ref[...]
        qm, fq = [], []
        for h in range(FOX_H):
            qs = _scaled(q_ref[:, 128 * (h // 2):128 * (h // 2 + 1)])
            zero = jnp.zeros_like(qs)
            qm.append(jnp.where(lo, qs, zero) if h % 2 == 0 else jnp.where(lo, zero, qs))
            fq.append(jnp.broadcast_to(cumv[:, h:h + 1], (t, 128)))

        def tile(j, state, masked):
            m, acc, lsum = (list(part) for part in state)
            k0 = pl.multiple_of(j * t, t)
            for hp in range(_PAIRS):
                cols = slice(128 * hp, 128 * (hp + 1))
                kb = k_ref[pl.ds(k0, t), cols]
                vb = v_ref[pl.ds(k0, t), cols]
                one = jnp.ones_like(vb)
                alphas, pvs = [], []
                for h in (2 * hp, 2 * hp + 1):
                    s = _dot(qm[h], kb, 1, 1) + jnp.concatenate([fq[h]] * (t // 128), axis=1) - fk_ref[h, pl.ds(j, 1), :]
                    if masked:
                        s = _diag_mask(s)
                    m_new = jnp.maximum(m[h], jnp.max(s, axis=-1, keepdims=True))
                    p = jnp.exp(s - m_new)
                    alphas.append(jnp.exp(m[h] - m_new))
                    m[h] = m_new
                    pvs.append(_dot(p.astype(BF16), jnp.where(lo, vb, one) if h % 2 == 0 else jnp.where(lo, one, vb), 1, 0))
                acc[hp] = jnp.where(lo, alphas[0], alphas[1]) * acc[hp] + jnp.where(lo, pvs[0], pvs[1])
                lsum[hp] = jnp.where(lo, alphas[1], alphas[0]) * lsum[hp] + jnp.where(lo, pvs[1], pvs[0])
            return tuple(m), tuple(acc), tuple(lsum)

        zeros = (jnp.zeros((t, 128), F32),) * _PAIRS
        state = lax.fori_loop(0, i, functools.partial(tile, masked=False), ((jnp.full((t, 1), NEG, F32),) * FOX_H, zeros, zeros))
        m, acc, lsum = tile(i, state, True)
        for hp in range(_PAIRS):
            o_ref[:, 128 * hp:128 * (hp + 1)] = acc[hp] / pltpu.roll(lsum[hp], FOX_DH, 1)
            lse = [m[2 * hp] + jnp.log(lsum[hp][:, FOX_DH:FOX_DH + 1]), m[2 * hp + 1] + jnp.log(lsum[hp][:, 0:1])]
            lse_ref[hp] = jnp.where(lane == 0, lse[0], jnp.where(lane == 1, lse[1], 0.0))

    whole = lambda col: pl.BlockSpec((S, FOX_W), lambda i: (0, col))
    return pl.pallas_call(
        body,
        name=name,
        grid=(S // t,),
        in_specs=[
            pl.BlockSpec((t, FOX_W), lambda i: (i, 0)), whole(1), whole(2),
            pl.BlockSpec((t, F_LANES), lambda i: (i, 0)),
            pl.BlockSpec((FOX_H, nk, t), lambda i: (0, 0, 0)),
        ],
        out_specs=[pl.BlockSpec((t, FOX_W), lambda i: (i, 0)), pl.BlockSpec((_PAIRS, t, 128), lambda i: (0, i, 0))],
        out_shape=[jax.ShapeDtypeStruct((S, FOX_W), F32), jax.ShapeDtypeStruct((_PAIRS, S, 128), F32)],
        compiler_params=_params(("arbitrary",)),
    )(qkv, qkv, qkv, cum, fk3)


def _fox_bwd(qkv, cum, fk3, o, do, lse, *, name):
    S = qkv.shape[0]
    nk, t = fk3.shape[1:]
    q_at, k_at, v_at = 0, FOX_W, 2 * FOX_W

    def body(qkv_ref, cum_ref, fk_ref, o_ref, do_ref, lse_ref, dq_ref, dk_ref, dv_ref, dfq_ref, dfk_ref,
             qs_sc, ks_sc, bias_sc, delta_sc, dq_sc):
        lane = _lanes((t, 128))
        lo = lane < FOX_DH
        mine = lambda h: lo if h % 2 == 0 else jnp.logical_not(lo)

        def by_head(tile, values):
            for h, val in enumerate(values):
                tile = jnp.where(lane == h, val, tile)
            return tile

        def prep(i, carry):
            r = pl.ds(pl.multiple_of(i * t, t), t)
            qs_sc[r, :] = _scaled(qkv_ref[r, q_at:q_at + FOX_W])
            ks_sc[r, :] = _scaled(qkv_ref[r, k_at:k_at + FOX_W])
            cum_t = cum_ref[r, :]
            for hp in range(_PAIRS):
                cols = slice(128 * hp, 128 * (hp + 1))
                prod = do_ref[r, cols].astype(F32) * o_ref[r, cols]
                for h in (2 * hp, 2 * hp + 1):
                    delta = jnp.sum(jnp.where(mine(h), prod, 0.0), axis=-1, keepdims=True)
                    delta_sc[h, r, :] = jnp.broadcast_to(delta, (t, 128))
                    bias_sc[h, r, :] = jnp.broadcast_to(cum_t[:, h:h + 1] - lse_ref[hp, r, h % 2:h % 2 + 1], (t, 128))
            dfq_ref[r, :] = jnp.zeros((t, 128), F32)
            dq_sc[r, :] = jnp.zeros((t, FOX_W), F32)
            return carry

        lax.fori_loop(0, nk, prep, 0)

        def kv_tile(j, carry):
            kr = pl.ds(pl.multiple_of(j * t, t), t)

            def q_tile(i, acc, masked):
                dk, dv, dfk = list(acc[:_PAIRS]), list(acc[_PAIRS:2 * _PAIRS]), list(acc[2 * _PAIRS:])
                qr = pl.ds(pl.multiple_of(i * t, t), t)
                dq_old, dfq_old = dq_sc[qr, :], dfq_ref[qr, :]
                wide = lambda a: jnp.concatenate([a] * (t // 128), axis=1)
                row_sums, dq_new = [], []
                for hp in range(_PAIRS):
                    cols = slice(128 * hp, 128 * (hp + 1))
                    kb = qkv_ref[kr, k_at + 128 * hp:k_at + 128 * (hp + 1)]
                    vb = qkv_ref[kr, v_at + 128 * hp:v_at + 128 * (hp + 1)]
                    ksb, qsb, dob = ks_sc[kr, cols], qs_sc[qr, cols], do_ref[qr, cols]
                    zero = jnp.zeros_like(qsb)
                    dq_t = jnp.zeros((t, 128), F32)
                    for h in (2 * hp, 2 * hp + 1):
                        qe, doe, ke = (jnp.where(mine(h), a, zero) for a in (qsb, dob, ksb))
                        s = _dot(qe, kb, 1, 1) + wide(bias_sc[h, qr, :]) - fk_ref[h, pl.ds(j, 1), :]
                        if masked:
                            s = _diag_mask(s)
                        p = jnp.exp(s)
                        dv[hp] = dv[hp] + _dot(p.astype(BF16), doe, 0, 0)
                        dp = _dot(doe, vb, 1, 1)
                        ds = p * (dp - wide(delta_sc[h, qr, :]))
                        dsb = ds.astype(BF16)
                        dk[hp] = dk[hp] + _dot(dsb, qe, 0, 0)
                        dq_t = dq_t + _dot(dsb, ke, 1, 0)
                        row_sums.append(jnp.sum(ds, axis=-1, keepdims=True))
                        dfk[h] = dfk[h] - jnp.sum(ds, axis=0, keepdims=True)
                    dq_new.append(dq_old[:, cols] + dq_t)
                for hp in range(_PAIRS):
                    dq_sc[qr, 128 * hp:128 * (hp + 1)] = dq_new[hp]
                dfq_ref[qr, :] = dfq_old + by_head(jnp.zeros((t, 128), F32), row_sums)
                return (*dk, *dv, *dfk)

            init = tuple([jnp.zeros((t, 128), F32)] * (2 * _PAIRS) + [jnp.zeros((1, t), F32)] * FOX_H)
            acc = q_tile(j, init, True)
            acc = lax.fori_loop(j + 1, nk, functools.partial(q_tile, masked=False), acc)
            for hp in range(_PAIRS):
                cols = slice(128 * hp, 128 * (hp + 1))
                dk_ref[kr, cols] = acc[hp].astype(BF16)
                dv_ref[kr, cols] = acc[_PAIRS + hp].astype(BF16)
            for h in range(FOX_H):
                dfk_ref[h, pl.ds(j, 1), :] = acc[2 * _PAIRS + h]
            return carry

        lax.fori_loop(0, nk, kv_tile, 0)
        dq_ref[...] = dq_sc[...].astype(BF16)

    vm = pl.BlockSpec(memory_space=pltpu.VMEM)
    big = jax.ShapeDtypeStruct((S, FOX_W), BF16)
    return pl.pallas_call(
        body,
        name=name,
        in_specs=[vm] * 6,
        out_specs=[vm] * 5,
        out_shape=[big, big, big, jax.ShapeDtypeStruct((S, 128), F32), jax.ShapeDtypeStruct((FOX_H, nk, t), F32)],
        scratch_shapes=[pltpu.VMEM((S, FOX_W), BF16), pltpu.VMEM((S, FOX_W), BF16), pltpu.VMEM((FOX_H, S, 128), F32),
                        pltpu.VMEM((FOX_H, S, 128), F32), pltpu.VMEM((S, FOX_W), F32)],
        compiler_params=pltpu.CompilerParams(vmem_limit_bytes=VMEM_LIMIT),
    )(qkv, cum, fk3, o, do, lse)


def _group_mask(lane, gi):
    return (lane >= 64 * gi) & (lane < 64 * (gi + 1))


_U_COL = OFF_C // SGU_W


def _sgu_fwd(rest, gn, wm, bias, *, name):
    S = rest.shape[0]
    ts = _tile(S, 512)
    nc = ts // SGU_CHUNK

    def body(u_ref, v_ref, g_ref, w_ref, b_ref, o_ref):
        zv = _gelu(v_ref[...])
        vn = zv * lax.rsqrt(jnp.mean(zv * zv, axis=-1, keepdims=True) + EPS) * g_ref[...]
        lane = _lanes((SGU_CHUNK, SGU_W))
        for c in range(nc):
            rows = slice(c * SGU_CHUNK, (c + 1) * SGU_CHUNK)
            vcb = vn[rows].astype(BF16)
            mixed = b_ref[...]
            for gi in range(4):
                mixed = mixed + jnp.where(_group_mask(lane, gi), _dot(w_ref[gi], vcb, 1, 0), 0.0)
            o_ref[rows, :] = (_gelu(u_ref[rows, :]) * mixed).astype(BF16)

    return pl.pallas_call(
        body,
        name=name,
        grid=(S // ts,),
        in_specs=[
            pl.BlockSpec((ts, SGU_W), lambda i: (i, _U_COL)),
            pl.BlockSpec((ts, SGU_W), lambda i: (i, _U_COL + 1)),
            pl.BlockSpec((1, SGU_W), lambda i: (0, 0)),
            pl.BlockSpec((4, SGU_CHUNK, SGU_CHUNK), lambda i: (0, 0, 0)),
            pl.BlockSpec((SGU_CHUNK, SGU_W), lambda i: (0, 0)),
        ],
        out_specs=pl.BlockSpec((ts, SGU_W), lambda i: (i, 0)),
        out_shape=jax.ShapeDtypeStruct((S, SGU_W), BF16),
        compiler_params=_params(("parallel",)),
    )(rest, rest, gn.reshape(1, SGU_W), wm, bias)


def _sgu_bwd(rest, gn, wm, wm_t, bias, dsg, *, name):
    S = rest.shape[0]
    ts = _tile(S, 512)
    nc = ts // SGU_CHUNK

    def body(u_ref, v_ref, g_ref, w_ref, wt_ref, b_ref, dsg_ref, dc_ref, dw_ref, db_ref, dg_ref):
        first = pl.program_id(0) == 0

        @pl.when(first)
        def _():
            dw_ref[...] = jnp.zeros_like(dw_ref)
            db_ref[...] = jnp.zeros_like(db_ref)
            dg_ref[...] = jnp.zeros_like(dg_ref)

        gv = g_ref[...]
        lane = _lanes((SGU_CHUNK, SGU_W))
        for c in range(nc):
            rows = slice(c * SGU_CHUNK, (c + 1) * SGU_CHUNK)
            vpre = v_ref[rows, :]
            upre = u_ref[rows, :]
            zv = _gelu(vpre)
            r = lax.rsqrt(jnp.mean(zv * zv, axis=-1, keepdims=True) + EPS)
            zn = zv * r
            vcb = (zn * gv).astype(BF16)
            mixed = b_ref[...]
            for gi in range(4):
                mixed = mixed + jnp.where(_group_mask(lane, gi), _dot(w_ref[gi], vcb, 1, 0), 0.0)
            zu = _gelu(upre)
            dsg_v = dsg_ref[rows, :]
            dc_ref[rows, :SGU_W] = (dsg_v * mixed * _gelu_grad(upre)).astype(BF16)
            dmixed = dsg_v * zu
            db_ref[...] += dmixed
            dvn = jnp.zeros((SGU_CHUNK, SGU_W), F32)
            for gi in range(4):
                dmg = jnp.where(_group_mask(lane, gi), dmixed, 0.0).astype(BF16)
                dw_ref[gi] += _dot(dmg, vcb, 1, 1)
                dvn = dvn + _dot(wt_ref[gi], dmg, 1, 0)
            dg_ref[...] += jnp.sum(dvn * zn, axis=0, keepdims=True)
            dzn = dvn * gv
            dzv = r * (dzn - zn * jnp.mean(dzn * zn, axis=-1, keepdims=True))
            dc_ref[rows, SGU_W:] = (dzv * _gelu_grad(vpre)).astype(BF16)

    blk = pl.BlockSpec((ts, SGU_W), lambda i: (i, 0))
    vec = pl.BlockSpec((1, SGU_W), lambda i: (0, 0))
    w3 = pl.BlockSpec((4, SGU_CHUNK, SGU_CHUNK), lambda i: (0, 0, 0))
    bsp = pl.BlockSpec((SGU_CHUNK, SGU_W), lambda i: (0, 0))
    return pl.pallas_call(
        body,
        name=name,
        grid=(S // ts,),
        in_specs=[
            pl.BlockSpec((ts, SGU_W), lambda i: (i, _U_COL)),
            pl.BlockSpec((ts, SGU_W), lambda i: (i, _U_COL + 1)),
            vec, w3, w3, bsp, blk,
        ],
        out_specs=[pl.BlockSpec((ts, 2 * SGU_W), lambda i: (i, 0)), w3, bsp, vec],
        out_shape=[
            jax.ShapeDtypeStruct((S, 2 * SGU_W), BF16),
            jax.ShapeDtypeStruct((4, SGU_CHUNK, SGU_CHUNK), F32),
            jax.ShapeDtypeStruct((SGU_CHUNK, SGU_W), F32),
            jax.ShapeDtypeStruct((1, SGU_W), F32),
        ],
        compiler_params=_params(("arbitrary",)),
    )(rest, rest, gn.reshape(1, SGU_W), wm, wm_t, bias, dsg)


_GT = 512
_G0 = OFF_G // _GT


def _gate_specs(tm, col_of):
    specs = [pl.BlockSpec((tm, _GT), functools.partial(lambda k, *ids: (col_of(*ids)[0], _G0 + 2 * k + col_of(*ids)[1]), k)) for k in range(3)]
    specs += [pl.BlockSpec((1, _GT), functools.partial(lambda k, *ids: (0, 2 * k + col_of(*ids)[1]), k)) for k in range(3)]
    return specs


def _merge_fwd(rest, bg, ya, yb, yc, *, name):
    S = rest.shape[0]
    tm = _tile(S, 512)

    def body(g1, g2, g3, b1, b2, b3, ya_ref, yb_ref, yc_ref, o_ref):
        acc = _sigmoid(g1[...] + b1[...]) * ya_ref[...]
        acc = acc + _sigmoid(g2[...] + b2[...]) * yb_ref[...]
        acc = acc + _sigmoid(g3[...] + b3[...]) * yc_ref[...]
        o_ref[...] = acc.astype(BF16)

    blk = pl.BlockSpec((tm, _GT), lambda i, j: (i, j))
    return pl.pallas_call(
        body,
        name=name,
        grid=(S // tm, D // _GT),
        in_specs=_gate_specs(tm, lambda i, j: (i, j)) + [blk, blk, blk],
        out_specs=blk,
        out_shape=jax.ShapeDtypeStruct((S, D), BF16),
        compiler_params=_params(("parallel", "parallel")),
    )(rest, rest, rest, bg, bg, bg, ya, yb, yc)


def _merge_bwd(rest, bg, ya, yb, yc, dm, *, name):
    S = rest.shape[0]
    tm = _tile(S, 512)

    def body(g1, g2, g3, b1, b2, b3, ya_ref, yb_ref, yc_ref, dm_ref, dya, dyb, dyc, dg1, dg2, dg3, db1, db2, db3):
        first = pl.program_id(1) == 0
        dmv = dm_ref[...]
        for g_ref, b_ref, y_ref, dy_ref, dg_ref, db_ref in (
            (g1, b1, ya_ref, dya, dg1, db1), (g2, b2, yb_ref, dyb, dg2, db2), (g3, b3, yc_ref, dyc, dg3, db3)):
            gate = _sigmoid(g_ref[...] + b_ref[...])
            dy_ref[...] = (dmv * gate).astype(BF16)
            dpre = dmv * y_ref[...] * gate * (1.0 - gate)
            dg_ref[...] = dpre.astype(BF16)
            part = jnp.sum(dpre, axis=0, keepdims=True)

            @pl.when(first)
            def _():
                db_ref[...] = part

            @pl.when(jnp.logical_not(first))
            def _():
                db_ref[...] += part

    blk = pl.BlockSpec((tm, _GT), lambda j, i: (i, j))
    vec = pl.BlockSpec((1, _GT), lambda j, i: (0, j))
    big = jax.ShapeDtypeStruct((S, D), BF16)
    small = jax.ShapeDtypeStruct((1, D), F32)
    return pl.pallas_call(
        body,
        name=name,
        grid=(D // _GT, S // tm),
        in_specs=_gate_specs(tm, lambda j, i: (i, j)) + [blk, blk, blk, blk],
        out_specs=[blk] * 6 + [vec] * 3,
        out_shape=[big] * 6 + [small] * 3,
        compiler_params=_params(("parallel", "arbitrary")),
    )(rest, rest, rest, bg, bg, bg, ya, yb, yc, dm)


_X_SCALE = XDH ** -0.5


def _xattn_fwd(xq, kv, *, name):
    S = xq.shape[0]
    M = kv.shape[0]
    tq = _tile(S, 512)

    def body(q_ref, k_ref, v_ref, o_ref):
        s = _dot(q_ref[...], k_ref[...], 1, 1) * _X_SCALE
        e = jnp.exp(s - jnp.max(s, axis=-1, keepdims=True))
        p = e / jnp.sum(e, axis=-1, keepdims=True)
        o_ref[...] = _dot(p.astype(BF16), v_ref[...], 1, 0).astype(BF16)

    return pl.pallas_call(
        body,
        name=name,
        grid=(S // tq, XH),
        in_specs=[
            pl.BlockSpec((tq, XDH), lambda i, h: (i, h)),
            pl.BlockSpec((M, XDH), lambda i, h: (0, h)),
            pl.BlockSpec((M, XDH), lambda i, h: (0, XH + h)),
        ],
        out_specs=pl.BlockSpec((tq, XDH), lambda i, h: (i, h)),
        out_shape=jax.ShapeDtypeStruct((S, D), BF16),
        compiler_params=_params(("parallel", "parallel")),
    )(xq, kv, kv)


def _xattn_bwd(xq, kv, do, *, name):
    S = xq.shape[0]
    M = kv.shape[0]
    tq = _tile(S, 512)

    def body(q_ref, k_ref, v_ref, do_ref, dq_ref, dk_ref, dv_ref):
        qb = q_ref[...]
        kb = k_ref[...]
        dob = do_ref[...]
        s = _dot(qb, kb, 1, 1) * _X_SCALE
        e = jnp.exp(s - jnp.max(s, axis=-1, keepdims=True))
        p = e / jnp.sum(e, axis=-1, keepdims=True)
        dp = _dot(dob, v_ref[...], 1, 1)
        ds = (p * (dp - jnp.sum(p * dp, axis=-1, keepdims=True)) * _X_SCALE).astype(BF16)
        dq_ref[...] = _dot(ds, kb, 1, 0).astype(BF16)
        dk_part = _dot(ds, qb, 0, 0)
        dv_part = _dot(p.astype(BF16), dob, 0, 0)

        @pl.when(pl.program_id(1) == 0)
        def _():
            dk_ref[...] = dk_part
            dv_ref[...] = dv_part

        @pl.when(pl.program_id(1) > 0)
        def _():
            dk_ref[...] += dk_part
            dv_ref[...] += dv_part

    qspec = pl.BlockSpec((tq, XDH), lambda h, i: (i, h))
    kspec = pl.BlockSpec((M, XDH), lambda h, i: (0, h))
    dxq, dxk, dxv = pl.pallas_call(
        body,
        name=name,
        grid=(XH, S // tq),
        in_specs=[qspec, kspec, pl.BlockSpec((M, XDH), lambda h, i: (0, XH + h)), qspec],
        out_specs=[qspec, kspec, kspec],
        out_shape=[jax.ShapeDtypeStruct((S, D), BF16), jax.ShapeDtypeStruct((M, D), F32), jax.ShapeDtypeStruct((M, D), F32)],
        compiler_params=_params(("parallel", "arbitrary")),
    )(xq, kv, kv, do)
    return dxq, jnp.concatenate([dxk, dxv], axis=1)


def _adam_math(w, g, m, v):
    m = ADAM_B1 * m + (1.0 - ADAM_B1) * g
    v = ADAM_B2 * v + (1.0 - ADAM_B2) * (g * g)
    m_hat = m / (1.0 - ADAM_B1 ** ADAM_STEP)
    v_hat = v / (1.0 - ADAM_B2 ** ADAM_STEP)
    delta = -ADAM_LR * (m_hat / (jnp.sqrt(v_hat) + ADAM_EPS) + ADAM_WD * w)
    return delta, m, v


def _adamw_sharded(parts, w, m, v, *, name):
    _, R, C = w.shape
    Cp = parts[0].shape[2]
    tm = _tile(R, 256)
    nr = R // tm

    def body(p0_ref, p1_ref, w_ref, m_ref, v_ref, g_ref, d_ref, mo_ref, vo_ref):
        def update(p_ref):
            g = p_ref[0][:, :C].astype(F32)
            for dev in range(1, N_DEV):
                g = g + p_ref[dev][:, :C].astype(F32)
            delta, mn, vn = _adam_math(w_ref[...], g, m_ref[...], v_ref[...])
            g_ref[...] = g
            d_ref[...] = delta
            mo_ref[...] = mn
            vo_ref[...] = vn

        @pl.when(pl.program_id(0) == 0)
        def _():
            update(p0_ref)

        @pl.when(pl.program_id(0) == 1)
        def _():
            update(p1_ref)

    p0 = pl.BlockSpec((N_DEV, tm, Cp), lambda l, i: (0, i * (1 - l) + (nr - 1) * l, 0))
    p1 = pl.BlockSpec((N_DEV, tm, Cp), lambda l, i: (0, i * l, 0))
    blk = pl.BlockSpec((None, tm, C), lambda l, i: (l, i, 0))
    sds = jax.ShapeDtypeStruct(w.shape, F32)
    return pl.pallas_call(
        body,
        name=name,
        grid=(DEPTH, nr),
        in_specs=[p0, p1, blk, blk, blk],
        out_specs=[blk] * 4,
        out_shape=[sds] * 4,
        compiler_params=_params(("arbitrary", "arbitrary")),
    )(parts[0], parts[1], w, m, v)


def _adamw_small(g, w, m, v, *, name):
    n = len(g)

    def body(*refs):
        g_refs, w_refs, m_refs, v_refs = (refs[k * n:(k + 1) * n] for k in range(4))
        d_out, m_out, v_out = (refs[(4 + k) * n:(5 + k) * n] for k in range(3))
        for t in range(n):
            delta, mn, vn = _adam_math(w_refs[t][...], g_refs[t][...], m_refs[t][...], v_refs[t][...])
            d_out[t][...] = delta
            m_out[t][...] = mn
            v_out[t][...] = vn

    vm = pl.BlockSpec(memory_space=pltpu.VMEM)
    shapes = [jax.ShapeDtypeStruct(a.shape, F32) for a in w]
    outs = pl.pallas_call(
        body,
        name=name,
        in_specs=[vm] * (4 * n),
        out_specs=[vm] * (3 * n),
        out_shape=shapes * 3,
        compiler_params=pltpu.CompilerParams(vmem_limit_bytes=VMEM_LIMIT),
    )(*g, *w, *m, *v)
    return outs[:n], outs[n:2 * n], outs[2 * n:]


def _position():
    return lax.axis_index("x"), lax.axis_index("y"), lax.axis_index("c")


def _dev_index(px, py, pc):
    return 4 * px + 2 * py + pc


_ANY = pl.BlockSpec(memory_space=pl.ANY)


def _all_gather(shards, *, name):
    n = len(shards)
    out_shape = [jax.ShapeDtypeStruct((N_DEV, *s.shape), s.dtype) for s in shards]
    n_pieces = len(_pieces(out_shape))

    def body(*refs):
        ins, outs = refs[:n], refs[n:2 * n]
        send_sems, recv_sems, local_sems = refs[2 * n:]
        x, y, c = _position()
        me, sibling = (x, y, c), (x, y, 1 - c)
        chips = [(1 - x, y), (x, 1 - y), (1 - x, 1 - y)]
        pieces = _pieces(outs)

        def copy(i, k, block, to, from_input=False):
            t, rows = pieces[i]
            dst = _cut(outs[t].at[_dev_index(*block)], rows)
            return pltpu.make_async_remote_copy(
                src_ref=_cut(ins[t], rows) if from_input else dst, dst_ref=dst, send_sem=send_sems.at[i, k],
                recv_sem=recv_sems.at[i, k], device_id=to, device_id_type=MESH)

        mine = [pltpu.make_async_copy(_cut(ins[t], rows), _cut(outs[t].at[_dev_index(*me)], rows), local_sems.at[i])
                for i, (t, rows) in enumerate(pieces)]
        for cp in mine:
            cp.start()
        started = []
        for j, chip in enumerate(chips):
            for i in range(n_pieces):
                started.append(copy(i, 1 + j, me, (*chip, c), from_input=True))
                started[-1].start()
        for i in range(n_pieces):
            started.append(copy(i, 0, me, sibling, from_input=True))
            started[-1].start()
        for j, chip in enumerate(chips):
            for i in range(n_pieces):
                copy(i, 1 + j, (*chip, c), me).wait_recv()
                started.append(copy(i, 4 + j, (*chip, c), sibling))
                started[-1].start()
        for i in range(n_pieces):
            copy(i, 0, sibling, me).wait_recv()
        for j, chip in enumerate(chips):
            for i in range(n_pieces):
                copy(i, 4 + j, (*chip, 1 - c), me).wait_recv()
        for cp in started:
            cp.wait_send()
        for cp in mine:
            cp.wait()

    return pl.pallas_call(
        body,
        name=name,
        in_specs=[_ANY] * n,
        out_specs=[_ANY] * n,
        out_shape=out_shape,
        scratch_shapes=[pltpu.SemaphoreType.DMA((n_pieces, 7)), pltpu.SemaphoreType.DMA((n_pieces, 7)),
                        pltpu.SemaphoreType.DMA((n_pieces,))],
        compiler_params=pltpu.CompilerParams(has_side_effects=True),
    )(*shards)


def _peers(x, y, c):
    out = []
    for mask in range(1, N_DEV):
        fx, fy, fc = (mask >> 2) & 1, (mask >> 1) & 1, mask & 1
        out.append((1 - x if fx else x, 1 - y if fy else y, 1 - c if fc else c))
    return out


_HBM = pl.BlockSpec(memory_space=pltpu.HBM)
_SEM = pl.BlockSpec(memory_space=pltpu.SEMAPHORE)


def _own_block_placed(block, like):
    x, y, c = _position()
    return lax.dynamic_update_index_in_dim(lax.empty(like.shape, like.dtype), block, _dev_index(x, y, c), 0)


_COPY_BYTES = 256 << 10
_MAX_PIECES = 8


def _pieces(blocks):
    out = []
    for t, b in enumerate(blocks):
        R, C = b.shape[-2:]
        n = max(1, min(_MAX_PIECES, R * C * jnp.dtype(b.dtype).itemsize // _COPY_BYTES))
        while n > 1 and R % (16 * n):
            n -= 1
        out += [(t, pl.ds(j * (R // n), R // n) if n > 1 else None) for j in range(n)]
    return out


def _cut(block, rows):
    return block if rows is None else block.at[rows]


def _copies(per_piece):
    def mark(fn):
        fn.per_piece = per_piece
        return fn
    return mark


@_copies(N_DEV - 1)
def _plan_exchange(srcs, lands, send_sems, recv_sems, arrivals):
    x, y, c = _position()
    me = _dev_index(x, y, c)
    out = []
    for k, peer in enumerate(_peers(x, y, c)):
        p = _dev_index(*peer)
        for i, (t, rows) in enumerate(_pieces(lands)):
            sems = dict(send_sem=send_sems.at[7 * i + k], recv_sem=recv_sems.at[7 * i + k], device_id=peer, device_id_type=MESH)
            src, dst = (lands[t].at[p], lands[t].at[p]) if arrivals else (srcs[t].at[p], lands[t].at[me])
            out.append(pltpu.make_async_remote_copy(src_ref=_cut(src, rows), dst_ref=_cut(dst, rows), **sems))
    return out


@_copies(N_DEV - 1)
def _plan_broadcast(srcs, lands, send_sems, recv_sems, arrivals):
    x, y, c = _position()
    me = _dev_index(x, y, c)
    out = []
    for k, peer in enumerate(_peers(x, y, c)):
        p = _dev_index(*peer)
        for i, (t, rows) in enumerate(_pieces(lands)):
            sems = dict(send_sem=send_sems.at[7 * i + k], recv_sem=recv_sems.at[7 * i + k], device_id=peer, device_id_type=MESH)
            src, dst = (lands[t].at[p], lands[t].at[p]) if arrivals else (srcs[t], lands[t].at[me])
            out.append(pltpu.make_async_remote_copy(src_ref=_cut(src, rows), dst_ref=_cut(dst, rows), **sems))
    return out


@_copies(4)
def _plan_gather_out(srcs, lands, send_sems, recv_sems, arrivals):
    x, y, c = _position()
    me = _dev_index(x, y, c)
    out = []
    for k, peer in enumerate([(x, y, 1 - c), (1 - x, y, c), (x, 1 - y, c), (1 - x, 1 - y, c)]):
        p = _dev_index(*peer)
        for i, (t, rows) in enumerate(_pieces(lands)):
            sems = dict(send_sem=send_sems.at[4 * i + k], recv_sem=recv_sems.at[4 * i + k], device_id=peer, device_id_type=MESH)
            src, dst = (lands[t].at[p], lands[t].at[p]) if arrivals else (srcs[t], lands[t].at[me])
            out.append(pltpu.make_async_remote_copy(src_ref=_cut(src, rows), dst_ref=_cut(dst, rows), **sems))
    return out


@_copies(3)
def _plan_gather_pass(srcs, lands, send_sems, recv_sems, arrivals):
    x, y, c = _position()
    sibling = (x, y, 1 - c)
    out = []
    for k, chip in enumerate([(1 - x, y), (x, 1 - y), (1 - x, 1 - y)]):
        p = _dev_index(*chip, 1 - c) if arrivals else _dev_index(*chip, c)
        for i, (t, rows) in enumerate(_pieces(lands)):
            sems = dict(send_sem=send_sems.at[3 * i + k], recv_sem=recv_sems.at[3 * i + k], device_id=sibling, device_id_type=MESH)
            block = _cut(lands[t].at[p], rows)
            out.append(pltpu.make_async_remote_copy(src_ref=block, dst_ref=block, **sems))
    return out


def _split_start(plan, srcs, lands, *, after=None, name):
    n_src, n = len(srcs), len(srcs) + len(lands)
    n_sem = plan.per_piece * len(_pieces(lands))
    order = [] if after is None else [after]

    def body(*refs):
        send_sems, recv_sems = refs[n + len(order):n + len(order) + 2]
        token = refs[-1]
        for cp in plan(refs[:n_src], refs[n_src:n], send_sems, recv_sems, arrivals=False):
            cp.start()
        token[...] = jnp.zeros_like(token)

    hbm = lambda a: pltpu.HBM(a.shape, a.dtype)
    outs = pl.pallas_call(
        body,
        name=name,
        in_specs=[_HBM] * n + [_ANY] * len(order),
        out_specs=[_SEM, _SEM] + [_HBM] * n + [pl.BlockSpec(memory_space=pltpu.VMEM)],
        out_shape=[pltpu.SemaphoreType.DMA((n_sem,)), pltpu.SemaphoreType.DMA((n_sem,))] + [hbm(a) for a in (*srcs, *lands)]
        + [jax.ShapeDtypeStruct(_TOKEN, F32)],
        input_output_aliases={i: 2 + i for i in range(n)},
        compiler_params=pltpu.CompilerParams(has_side_effects=pltpu.SideEffectType.DATAFLOW_SIDE_EFFECTING),
    )(*[pltpu.with_memory_space_constraint(a, pltpu.HBM) for a in (*srcs, *lands)], *order)
    return (outs[0], outs[1], outs[2:2 + n_src], outs[2 + n_src:2 + n]), outs[-1]


def _split_wait(plan, state, after, *, name):
    send_sems, recv_sems, srcs, lands = state
    n_src, n = len(srcs), len(srcs) + len(lands)

    def body(*refs):
        send_refs, recv_refs = refs[n:n + 2]
        for cp in plan(refs[:n_src], refs[n_src:n], send_refs, recv_refs, arrivals=False):
            cp.wait_send()
        for cp in plan(refs[:n_src], refs[n_src:n], send_refs, recv_refs, arrivals=True):
            cp.wait_recv()

    hbm = lambda a: pltpu.HBM(a.shape, a.dtype)
    outs = pl.pallas_call(
        body,
        name=name,
        in_specs=[_HBM] * n + [_SEM, _SEM, _ANY],
        out_specs=[_HBM] * n,
        out_shape=[hbm(a) for a in (*srcs, *lands)],
        input_output_aliases={i: i for i in range(n)},
        compiler_params=pltpu.CompilerParams(has_side_effects=pltpu.SideEffectType.DATAFLOW_SIDE_EFFECTING),
    )(*srcs, *lands, send_sems, recv_sems, after)
    return outs[n_src:]


def _sum_blocks(blocks, *, name):
    _, R, C = blocks.shape
    tm = next(R // n for n in (4, 3, 2, 1) if R % (8 * n) == 0)

    def body(b_ref, o_ref):
        g = b_ref[0]
        for dev in range(1, N_DEV):
            g = g + b_ref[dev]
        o_ref[...] = g

    return pl.pallas_call(
        body,
        name=name,
        grid=(R // tm,),
        in_specs=[pl.BlockSpec((N_DEV, tm, C), lambda i: (0, i, 0))],
        out_specs=pl.BlockSpec((tm, C), lambda i: (i, 0)),
        out_shape=jax.ShapeDtypeStruct((R, C), F32),
        compiler_params=_params(("parallel",)),
    )(blocks)


def _block_diag(w):
    out = jnp.zeros((POOL_W, POOL_W), w.dtype)
    for gi in range(4):
        out = out.at[64 * gi:64 * (gi + 1), 64 * gi:64 * (gi + 1)].set(w[gi])
    return out


def _layer_consts(sp, l):
    causal = jnp.tril(jnp.ones((SGU_CHUNK, SGU_CHUNK), F32))
    wm = (sp["sgu_w"][l] * causal[None]).astype(BF16)
    wbd = _block_diag(sp["pool_w"][l]).astype(BF16)
    return dict(
        wbd=wbd, wbd_t=wbd.T, wm=wm, wm_t=wm.transpose(0, 2, 1),
        sgu_bias=jnp.repeat(sp["sgu_b"][l].T, 64, axis=1),
        bpad=jnp.pad(sp["b_forget"][l], (0, F_LANES - FOX_H)).reshape(1, F_LANES),
        bg=sp["b_gate"][l].reshape(1, 3 * D),
    )


def _relu2(acc):
    return acc, jnp.square(jnp.maximum(acc, 0.0))


def _relu2_grad(acc, z):
    return (acc * 2.0 * jnp.maximum(z, 0.0),)


def _layer_fwd(l, x, h, mem, source, sp):
    S = x.shape[0]
    t = _tile(S, 256)
    c = _layer_consts(sp, l)
    n = f"l{l}_"
    W, after = source(l, "begin", x)
    if h is None:
        h, after = _rms_fwd(x, sp["norm_mix_g"][l], after=after, name=n + "norm_mix"), None
    qkv = _mm(h, W["qkv"], out_dtypes=(BF16,), after=after, name=n + "qkv")
    rest = _mm(h, W["rest"], name=n + "rest")
    pa = _pool_fwd(rest, c["wbd"], sp["pool_scale"][l], name=n + "pool")
    cum, cum_t = _fox_prep(rest, c["bpad"], name=n + "fox_prep")
    fk3 = cum_t[:FOX_H].reshape(FOX_H, S // t, t)
    o, lse = _fox_fwd(qkv, cum, fk3, name=n + "fox")
    more, _ = source(l, "attended", o)
    W.update(more)
    sg = _sgu_fwd(rest, sp["sgu_norm_g"][l], c["wm"], c["sgu_bias"], name=n + "sgu")
    more, after = source(l, "mixed", sg)
    W.update(more)
    ya = _mm(pa, W["ba"], out_dtypes=(BF16,), after=after, name=n + "branch_a")
    yb = _mm(o, W["bb"], out_dtypes=(BF16,), name=n + "branch_b")
    yc = _mm(sg, W["bc"], out_dtypes=(BF16,), name=n + "branch_c")
    merged = _merge_fwd(rest, c["bg"], ya, yb, yc, name=n + "merge")
    whole_rows = dict(epilogue=_add_norm, out_dtypes=(F32, BF16), tm=1024, tn=D)
    x1, hx = _mm(merged, W["out"], extras=(x,), row_extras=(sp["norm_xattn_g"][l].reshape(1, D),), name=n + "out", **whole_rows)
    hm = _rms_fwd(mem, sp["norm_mem_g"][l], name=n + "norm_mem")
    xq = _mm(hx, W["xq"], out_dtypes=(BF16,), name=n + "xq")
    kv = _mm(hm, W["xkv"], out_dtypes=(BF16,), name=n + "xkv")
    o2 = _xattn_fwd(xq, kv, name=n + "xattn")
    x2, hf = _mm(o2, W["xo"], extras=(x1,), row_extras=(sp["norm_ffn_g"][l].reshape(1, D),), name=n + "xo", **whole_rows)
    z, act = _mm(hf, W["ff1"], epilogue=_relu2, out_dtypes=(BF16, BF16), name=n + "ff1")
    _, after = source(l, "expanded", act)
    if l + 1 < DEPTH:
        x3, h_next = _mm(act, W["ff2"], extras=(x2,), row_extras=(sp["norm_mix_g"][l + 1].reshape(1, D),), after=after, name=n + "ff2",
                         **whole_rows)
    else:
        x3, h_next = _mm(act, W["ff2"], extras=(x2,), epilogue=_add, after=after, name=n + "ff2"), None
    saved = dict(x=x, h=h, qkv=qkv, rest=rest, pa=pa, cum=cum, fk3=fk3, o=o, lse=lse, sg=sg, ya=ya, yb=yb, yc=yc,
                 merged=merged, x1=x1, hx=hx, hm=hm, xq=xq, kv=kv, o2=o2, x2=x2, hf=hf, z=z, act=act, c=c)
    return x3, h_next, saved, W


def _layer_bwd(l, dx3, sv, mem, W, sp, grads_done):
    S = dx3.shape[0]
    c = sv["c"]
    n = f"l{l}b_"
    bf = dict(out_dtypes=(BF16,))
    gw, gs = {}, {}
    gw["ff2"] = _mm(sv["act"], dx3, ta=True, name=n + "dw_ff2", **bf)
    dz = _mm(dx3, W["ff2"], tb=True, extras=(sv["z"],), epilogue=_relu2_grad, name=n + "dz", **bf)
    gw["ff1"] = _mm(sv["hf"], dz, ta=True, shard_out=True, name=n + "dw_ff1", **bf)
    whole_rows = dict(epilogue=_norm_grad, out_dtypes=(F32, F32), row_outs=1, tm=1024, tn=D)
    gain = lambda key: (sp[key][l].reshape(1, D),)
    dx2, dg = _mm(dz, W["ff1"], tb=True, extras=(sv["x2"], dx3), row_extras=gain("norm_ffn_g"), name=n + "dhf", **whole_rows)
    gs["norm_ffn_g"] = dg.reshape(D)
    gw["xo"] = _mm(sv["o2"], dx2, ta=True, name=n + "dw_xo", **bf)
    do2 = _mm(dx2, W["xo"], tb=True, name=n + "do2", **bf)
    dxq, dkv = _xattn_bwd(sv["xq"], sv["kv"], do2, name=n + "dxattn")
    gw["xq"] = _mm(sv["hx"], dxq, ta=True, name=n + "dw_xq", **bf)
    gw["xkv"] = _mm(sv["hm"], dkv, ta=True, shard_out=True, name=n + "dw_xkv", **bf)
    dhm = _mm(dkv, W["xkv"], tb=True, name=n + "dhm")
    _, gs["norm_mem_g"] = _rms_bwd(mem, sp["norm_mem_g"][l], dhm, jnp.zeros_like(mem), name=n + "dnorm_mem")
    dx1, dg = _mm(dxq, W["xq"], tb=True, extras=(sv["x1"], dx2), row_extras=gain("norm_xattn_g"), name=n + "dhx", **whole_rows)
    gs["norm_xattn_g"] = dg.reshape(D)
    after, gw = grads_done(l, gw), {}
    gw["out"] = _mm(sv["merged"], dx1, ta=True, name=n + "dw_out", **bf)
    dm = _mm(dx1, W["out"], tb=True, after=after, name=n + "dmerged")
    dya, dyb, dyc, dg1, dg2, dg3, db1, db2, db3 = _merge_bwd(sv["rest"], c["bg"], sv["ya"], sv["yb"], sv["yc"], dm, name=n + "dmerge")
    gs["b_gate"] = jnp.concatenate([db1, db2, db3], axis=1).reshape(3 * D)
    gw["ba"] = _mm(sv["pa"], dya, ta=True, shard_out=True, name=n + "dw_ba", **bf)
    gw["bb"] = _mm(sv["o"], dyb, ta=True, shard_out=True, name=n + "dw_bb", **bf)
    gw["bc"] = _mm(sv["sg"], dyc, ta=True, shard_out=True, name=n + "dw_bc", **bf)
    after, gw = grads_done(l, gw), {}
    dpa = _mm(dya, W["ba"], tb=True, name=n + "dpa")
    do = _mm(dyb, W["bb"], tb=True, after=after, name=n + "do", **bf)
    dsg = _mm(dyc, W["bc"], tb=True, name=n + "dsg")
    da, dwbd, dscale = _pool_bwd(sv["rest"], c["wbd"], c["wbd_t"], sp["pool_scale"][l], dpa, name=n + "dpool")
    gs["pool_w"] = jnp.stack([dwbd[64 * gi:64 * (gi + 1), 64 * gi:64 * (gi + 1)] for gi in range(4)])
    gs["pool_scale"] = dscale.reshape(POOL_W)
    dq, dk, dv, dfq, dfk = _fox_bwd(sv["qkv"], sv["cum"], sv["fk3"], sv["o"], do, sv["lse"], name=n + "dfox")
    dcum = dfq + jnp.pad(dfk.reshape(FOX_H, S).T, ((0, 0), (0, F_LANES - FOX_H)))
    df, dbf = _fox_post(sv["rest"], c["bpad"], dcum, name=n + "dfox_post")
    gs["b_forget"] = dbf[0, :FOX_H]
    dc, dwm, dbias, dgn = _sgu_bwd(sv["rest"], sp["sgu_norm_g"][l], c["wm"], c["wm_t"], c["sgu_bias"], dsg, name=n + "dsgu")
    gs["sgu_w"] = dwm * jnp.tril(jnp.ones((SGU_CHUNK, SGU_CHUNK), F32))[None]
    gs["sgu_b"] = dbias.reshape(SGU_CHUNK, 4, 64).sum(axis=2).T
    gs["sgu_norm_g"] = dgn.reshape(SGU_W)
    dqkv = [dq, dk, dv]
    drest = [jnp.concatenate([da, df, jnp.zeros((S, OFF_C - OFF_F - F_LANES), BF16), dc], axis=1), dg1, dg2, dg3]
    gw["qkv"] = _mm(sv["h"], dqkv, ta=True, name=n + "dw_qkv", **bf)
    gw["rest"] = _mm(sv["h"], drest, ta=True, name=n + "dw_rest", **bf)
    after = grads_done(l, gw)
    dh = _mm(dqkv, W["qkv"], tb=True, after=after, name=n + "dh_qkv")
    dx, dg = _mm(drest, W["rest"], tb=True, extras=(dh, sv["x"], dx1), row_extras=gain("norm_mix_g"), name=n + "dh",
                 **{**whole_rows, "epilogue": _add_norm_grad, "tm": 512})
    gs["norm_mix_g"] = dg.reshape(D)
    return dx, gs


def _local_step(x, mem, target, sp, source, grads_done):
    saved, Ws, h = [], [], None
    for l in range(DEPTH):
        x, h, sv, W = _layer_fwd(l, x, h, mem, source, sp)
        saved.append(sv)
        Ws.append(W)
    loss, dx, dgf = _final_loss(x, sp["final_norm_g"], target, name="final_loss")
    gss = [None] * DEPTH
    for l in reversed(range(DEPTH)):
        dx, gss[l] = _layer_bwd(l, dx, saved[l], mem, Ws[l], sp, grads_done)
    small = {k: jnp.stack([gss[l][k] for l in range(DEPTH)]) for k in gss[0]}
    small["final_norm_g"] = dgf
    return loss, dx, small


_SMALL = ["norm_mix_g", "b_forget", "pool_w", "pool_scale", "sgu_norm_g", "sgu_w", "sgu_b", "b_gate", "norm_xattn_g",
          "norm_mem_g", "norm_ffn_g", "final_norm_g"]
_COL = {"w_branch_a": "ba", "w_branch_b": "bb", "w_branch_c": "bc", "w_xkv": "xkv", "w_ff1": "ff1"}
_ROW = {"w_out": "out", "w_xq": "xq", "w_xo": "xo", "w_ff2": "ff2"}
_BIG = ["w_in", "w_branch_a", "w_branch_b", "w_branch_c", "w_out", "w_xq", "w_xkv", "w_xo", "w_ff1", "w_ff2"]
_PACK_LANES = 128


def _as_rows(a):
    return a.reshape(-1, a.shape[-1])


def _pack(tensors):
    rows = []
    for a in tensors:
        flat = a.reshape(-1)
        flat = jnp.pad(flat, (0, (-flat.shape[0]) % (8 * _PACK_LANES)))
        rows.append(flat.reshape(-1, _PACK_LANES))
    n_rows = sum(r.shape[0] for r in rows)
    rows.append(jnp.zeros(((-n_rows) % (8 * N_DEV), _PACK_LANES), F32))
    return jnp.concatenate(rows, axis=0)


def _unpack(packed, like):
    out, r = [], 0
    for a in like:
        size = math.prod(a.shape)
        nr = 8 * (-(-size // (8 * _PACK_LANES)))
        out.append(packed[r:r + nr].reshape(-1)[:size].reshape(a.shape))
        r += nr
    return out


_SHARD_IN = N_IN // N_DEV
_SHARD_IN_PAD = -(-_SHARD_IN // 128) * 128


def _columns(pieces, start, stop):
    out, at = [], 0
    for p in pieces:
        lo, hi = max(start, at), min(stop, at + p.shape[1])
        if lo < hi:
            out.append(p[:, lo - at:hi - at])
        at += p.shape[1]
    return out


def _split_w_in(blocks):
    K = blocks[0].shape[0]
    pad = jnp.zeros((K, OFF_C - OFF_F - FOX_H), blocks[0].dtype)
    cols = functools.partial(_columns, blocks)
    rest = jnp.concatenate(cols(0, R_OFF_Q) + cols(R_OFF_F, R_OFF_C) + [pad] + cols(R_OFF_C, N_IN), axis=1)
    return jnp.concatenate(cols(R_OFF_Q, R_OFF_F), axis=1), rest


def _join_w_in(qkv, rest):
    in_order = [rest[:, :R_OFF_Q], qkv, rest[:, OFF_F:OFF_F + FOX_H], rest[:, OFF_C:]]
    pad = jnp.zeros((qkv.shape[0], _SHARD_IN_PAD - _SHARD_IN), qkv.dtype)
    return jnp.stack([jnp.concatenate(_columns(in_order, _SHARD_IN * d, _SHARD_IN * (d + 1)) + [pad], axis=1) for d in range(N_DEV)])


_FIRST = ["w_in"]
_LATER = [k for k in _BIG if k not in _FIRST]


def _layer_weights(gathered):
    W = {}
    if "w_in" in gathered:
        W.update(zip(("qkv", "rest"), _split_w_in([gathered["w_in"][d][:, :_SHARD_IN] for d in range(N_DEV)])))
    for name, key in _COL.items():
        if name in gathered:
            W[key] = _Gathered(gathered[name])
    for name, key in _ROW.items():
        if name in gathered:
            W[key] = gathered[name].reshape(-1, gathered[name].shape[-1])
    return W


def _grad_blocks(gw):
    parts = {}
    if "qkv" in gw:
        parts["w_in"] = _join_w_in(gw["qkv"], gw["rest"])
    for name, key in _COL.items():
        if key in gw:
            parts[name] = gw[key]
    for name, key in _ROW.items():
        if key in gw:
            parts[name] = gw[key].reshape(N_DEV, -1, gw[key].shape[-1])
    return parts


def kernel(x, mem, norm_mix_g, w_in, b_forget, pool_w, pool_scale, sgu_norm_g, sgu_w, sgu_b, w_branch_a, w_branch_b, w_branch_c, b_gate, w_out, norm_xattn_g, norm_mem_g, w_xq, w_xkv, w_xo, norm_ffn_g, w_ff1, w_ff2, final_norm_g, loss_target, m_norm_mix_g, m_w_in, m_b_forget, m_pool_w, m_pool_scale, m_sgu_norm_g, m_sgu_w, m_sgu_b, m_w_branch_a, m_w_branch_b, m_w_branch_c, m_b_gate, m_w_out, m_norm_xattn_g, m_norm_mem_g, m_w_xq, m_w_xkv, m_w_xo, m_norm_ffn_g, m_w_ff1, m_w_ff2, m_final_norm_g, v_norm_mix_g, v_w_in, v_b_forget, v_pool_w, v_pool_scale, v_sgu_norm_g, v_sgu_w, v_sgu_b, v_w_branch_a, v_w_branch_b, v_w_branch_c, v_b_gate, v_w_out, v_norm_xattn_g, v_norm_mem_g, v_w_xq, v_w_xkv, v_w_xo, v_norm_ffn_g, v_w_ff1, v_w_ff2, v_final_norm_g):
    names = ["norm_mix_g", "w_in", "b_forget", "pool_w", "pool_scale", "sgu_norm_g", "sgu_w", "sgu_b", "w_branch_a", "w_branch_b",
             "w_branch_c", "b_gate", "w_out", "norm_xattn_g", "norm_mem_g", "w_xq", "w_xkv", "w_xo", "norm_ffn_g", "w_ff1", "w_ff2",
             "final_norm_g"]
    w = dict(zip(names, [norm_mix_g, w_in, b_forget, pool_w, pool_scale, sgu_norm_g, sgu_w, sgu_b, w_branch_a, w_branch_b, w_branch_c,
                         b_gate, w_out, norm_xattn_g, norm_mem_g, w_xq, w_xkv, w_xo, norm_ffn_g, w_ff1, w_ff2, final_norm_g]))
    m = dict(zip(names, [m_norm_mix_g, m_w_in, m_b_forget, m_pool_w, m_pool_scale, m_sgu_norm_g, m_sgu_w, m_sgu_b, m_w_branch_a,
                         m_w_branch_b, m_w_branch_c, m_b_gate, m_w_out, m_norm_xattn_g, m_norm_mem_g, m_w_xq, m_w_xkv, m_w_xo,
                         m_norm_ffn_g, m_w_ff1, m_w_ff2, m_final_norm_g]))
    v = dict(zip(names, [v_norm_mix_g, v_w_in, v_b_forget, v_pool_w, v_pool_scale, v_sgu_norm_g, v_sgu_w, v_sgu_b, v_w_branch_a,
                         v_w_branch_b, v_w_branch_c, v_b_gate, v_w_out, v_norm_xattn_g, v_norm_mem_g, v_w_xq, v_w_xkv, v_w_xo,
                         v_norm_ffn_g, v_w_ff1, v_w_ff2, v_final_norm_g]))

    sp = {k: w[k] for k in _SMALL}
    shards = [{k: w[k][l].astype(BF16) for k in _BIG} for l in range(DEPTH)]
    for sh in shards:
        sh["w_in"] = jnp.pad(sh["w_in"], ((0, 0), (0, _SHARD_IN_PAD - _SHARD_IN)))
    me = _dev_index(*_position())

    def gather_out(l, keys, name, after=None):
        srcs = [shards[l][k] for k in keys]
        lands = [_own_block_placed(a, jax.ShapeDtypeStruct((N_DEV, *a.shape), a.dtype)) for a in srcs]
        state, token = _split_start(_plan_gather_out, srcs, lands, after=after, name=name + "_out_start")
        return (keys, name, state), token

    def gather_pass(job, value):
        keys, name, state = job
        lands = _split_wait(_plan_gather_out, state, value, name=name + "_out_wait")
        state, token = _split_start(_plan_gather_pass, [], lands, name=name + "_pass_start")
        return (keys, name, state), token, lands[0]

    def gather_end(job, value):
        keys, name, state = job
        return _layer_weights(dict(zip(keys, _split_wait(_plan_gather_pass, state, value, name=name + "_pass_wait"))))

    jobs = {}

    def source(l, point, value):
        if (l, point) == (0, "begin"):
            first = _all_gather([shards[0][k] for k in _FIRST], name="gather_l0_first")
            jobs["l0"], token = gather_out(0, _LATER, "gather_l0", after=first[0])
            return _layer_weights(dict(zip(_FIRST, first))), token
        if (l, point) == (0, "attended"):
            jobs["l0"], _, arrived = gather_pass(jobs["l0"], value)
            jobs["l1_first"], token = gather_out(1, _FIRST, "gather_l1_first", after=arrived)
            jobs["l1"], jobs["token"] = gather_out(1, _LATER, "gather_l1", after=token)
            return {}, None
        if (l, point) == (0, "mixed"):
            return gather_end(jobs.pop("l0"), value), jobs.pop("token")
        if (l, point) == (0, "expanded"):
            jobs["l1_first"], token, _ = gather_pass(jobs["l1_first"], value)
            return {}, token
        if (l, point) == (1, "begin"):
            W = gather_end(jobs.pop("l1_first"), value)
            jobs["l1"], token, _ = gather_pass(jobs["l1"], value)
            return W, token
        if (l, point) == (1, "mixed"):
            return gather_end(jobs.pop("l1"), value), None
        return {}, None

    received = [{} for _ in range(DEPTH)]
    travelling = []

    def grads_done(l, gw):
        blocks = _grad_blocks(gw)
        keys = [k for k in _BIG if k in blocks]
        parts = [blocks[k] for k in keys]
        group = f"exchange_grads_l{l}_" + ("in" if "w_in" in blocks else "merge" if "w_out" in blocks else "mlp")
        lands = [_own_block_placed(lax.dynamic_index_in_dim(p, me, 0, keepdims=False), p) for p in parts]
        state, token = _split_start(_plan_exchange, parts, lands, name=group + "_start")
        travelling.append((l, keys, state, group + "_wait"))
        return token

    loss, dx, small = _local_step(x[0], mem[0], loss_target[0], sp, source, grads_done)
    grads, deltas, new_m, new_v = {}, {}, {}, {}
    like = [loss] + [w[k] for k in _SMALL]
    packed = _pack([loss] + [small[k] for k in _SMALL])
    eighths = packed.reshape(N_DEV, -1, _PACK_LANES)
    own = lambda a: _own_block_placed(lax.dynamic_index_in_dim(a, me, 0, keepdims=False) if a.ndim == 3 else a, eighths)
    scatter, done = _split_start(_plan_exchange, [eighths], [own(eighths)], after=dx, name="small_grads_scatter_start")

    def reduce_small(after):
        mine = _sum_blocks(_split_wait(_plan_exchange, scatter, after, name="small_grads_scatter_wait")[0], name="small_grads_sum")
        return _split_start(_plan_broadcast, [mine], [own(mine)], name="small_grads_gather_start")

    def update_small(state, after):
        total = _split_wait(_plan_broadcast, state, after, name="small_grads_gather_wait")[0].reshape(packed.shape)
        loss_sum, *g_small = _unpack(total, like)
        rows = lambda d: [_as_rows(d[k]) for k in _SMALL]
        outs = _adamw_small([_as_rows(g) for g in g_small], rows(w), rows(m), rows(v), name="adamw_small")
        grads.update(zip(_SMALL, g_small))
        for dst, vals in zip((deltas, new_m, new_v), outs):
            dst.update({k: a.reshape(w[k].shape) for k, a in zip(_SMALL, vals)})
        return loss_sum[0, 0], outs[0][0]

    groups = list(dict.fromkeys(tuple(keys) for _, keys, _, _ in travelling))
    for n_done, group_keys in enumerate(groups):
        if n_done == 1:
            gather, _ = reduce_small(done)
        if n_done == len(groups) - 1:
            loss, done = update_small(gather, done)
        for l, keys, state, wait_name in travelling:
            if tuple(keys) == group_keys:
                received[l].update(zip(keys, _split_wait(_plan_exchange, state, done, name=wait_name)))
        for k in group_keys:
            outs = _adamw_sharded([received[l][k] for l in range(DEPTH)], w[k], m[k], v[k], name="adamw_" + k)
            grads[k], deltas[k], new_m[k], new_v[k] = outs
        done = grads[group_keys[-1]]

    return (loss, dx[None], *[grads[k] for k in names], *[deltas[k] for k in names], *[new_m[k] for k in names],
            *[new_v[k] for k in names])
```

```python
import functools
import math

import jax
import jax.numpy as jnp
from jax import lax
from jax.experimental import pallas as pl
from jax.experimental.pallas import tpu as pltpu

F32 = jnp.float32
BF16 = jnp.bfloat16
MESH = pl.DeviceIdType.MESH

N_DEV = 8
D = 1024
DEPTH = 2
EPS = 1e-6
NEG = -1e30
POOL_W = 256
FOX_H = 8
FOX_DH = 64
FOX_W = 512
SGU_W = 256
SGU_CHUNK = 128
XH = 4
XDH = 256
N_IN = 5384
R_OFF_Q, R_OFF_F, R_OFF_C = 256, 1792, 1800
QKV_W = 3 * FOX_W
OFF_A, OFF_F, OFF_C, OFF_G, REST_W = 0, 256, 512, 1024, 4096
F_LANES = 128

ADAM_LR = 0.001
ADAM_B1 = 0.9
ADAM_B2 = 0.999
ADAM_EPS = 1e-08
ADAM_WD = 0.01
ADAM_STEP = 10

VMEM_LIMIT = 56 * 1024 * 1024


def _tile(n, pref):
    t = min(n, pref)
    while n % t:
        t -= 128
    assert t > 0, (n, pref)
    return t


def _params(sem=None):
    return pltpu.CompilerParams(dimension_semantics=sem, vmem_limit_bytes=VMEM_LIMIT)


def _dot(a, b, ca, cb):
    return lax.dot_general(a, b, (((ca,), (cb,)), ((), ())), preferred_element_type=F32)


def _sigmoid(z):
    return 1.0 / (1.0 + jnp.exp(-z))


_GELU_K = math.sqrt(2.0 / math.pi)
_GELU_C = 0.044715


def _gelu(x):
    return 0.5 * x * (1.0 + jnp.tanh(_GELU_K * (x + _GELU_C * x * x * x)))


def _gelu_grad(x):
    t = jnp.tanh(_GELU_K * (x + _GELU_C * x * x * x))
    return 0.5 * (1.0 + t) + 0.5 * x * (1.0 - t * t) * _GELU_K * (1.0 + 3.0 * _GELU_C * x * x)


def _rows(shape):
    return lax.broadcasted_iota(jnp.int32, shape, 0)


def _lanes(shape):
    return lax.broadcasted_iota(jnp.int32, shape, 1)


class _Gathered:
    def __init__(self, arr):
        self.arr = arr
        self.shape = (arr.shape[1], N_DEV * arr.shape[2])


_TOKEN = (8, 128)


def _mm(a, b, *, ta=False, tb=False, extras=(), row_extras=(), epilogue=None, out_dtypes=(F32,), row_outs=0, shard_out=False, after=None,
        tm=None, tn=512, tk=None, name):
    a_parts = list(a) if isinstance(a, (list, tuple)) else [a]
    b_parts = list(b) if isinstance(b, (list, tuple)) else [b]
    gathered = isinstance(b, _Gathered)
    assert (len(a_parts) == 1 or not ta) and (len(b_parts) == 1 or not tb) and min(len(a_parts), len(b_parts)) == 1
    a0, b0 = a_parts[0], b_parts[0]
    M, K = (a0.shape[1], a0.shape[0]) if ta else (a0.shape[0], a0.shape[1] * len(a_parts))
    N, Kb = b0.shape if tb else (b0.shape[1] * len(b_parts), b0.shape[0])
    assert Kb == K, (a0.shape, b0.shape, ta, tb)
    if gathered:
        if tb:
            tk = b.arr.shape[2]
        else:
            tn = b.arr.shape[2]
    if len(a_parts) > 1:
        tk = a0.shape[1]
    if shard_out:
        tn = N // N_DEV
    tm = _tile(M, tm or (1024 if ta else 2048))
    tn = _tile(b0.shape[1] if len(b_parts) > 1 else N, tn)
    per_piece = b0.shape[1] // tn
    size = lambda dt: jnp.dtype(dt).itemsize
    row_bytes = len(a_parts) * tm * size(a0.dtype) + len(b_parts) * tn * size(b.arr.dtype if gathered else b0.dtype)
    tile_bytes = tm * tn * (sum(size(e.dtype) for e in extras) + sum(map(size, out_dtypes)))

    def vmem_bytes(k_tile):
        return 2 * (k_tile * row_bytes + tile_bytes) + tm * tn * 4 * (K > k_tile)

    if tk is None:
        tk = next(c for c in (_tile(K, 2048), _tile(K, 1024), _tile(K, 512), _tile(K, 256)) if vmem_bytes(c) <= VMEM_LIMIT - (4 << 20))
    tk = _tile(K, tk)
    nk = K // tk
    ca, cb = (0 if ta else 1), (1 if tb else 0)
    n_a, n_b, n_ex, n_out = len(a_parts), len(b_parts), len(extras) + len(row_extras), len(out_dtypes)
    tokens = [] if after is None else [after]
    n_in = n_a + n_b + n_ex + len(tokens)
    if epilogue is None:
        epilogue = lambda acc: (acc,)

    def body(*refs):
        a_refs, b_refs = refs[:n_a], refs[n_a:n_a + n_b]
        ex_refs = refs[n_a + n_b:n_a + n_b + n_ex]
        o_refs = refs[n_in:n_in + n_out]
        j, k = pl.program_id(1), pl.program_id(2)

        def finish(acc):
            vals = epilogue(acc, *[e[...] for e in ex_refs])
            for o_ref, val in zip(o_refs[:n_out - row_outs], vals):
                o_ref[...] = val.astype(o_ref.dtype)
            for o_ref, val in zip(o_refs[n_out - row_outs:], vals[n_out - row_outs:]):
                first = pl.program_id(0) == 0
                o_ref[...] = jnp.where(first, val, o_ref[...] + val)

        def step(a_ref, b_ref):
            part = _dot(a_ref[...].astype(BF16), b_ref[...].astype(BF16), ca, cb)
            if nk == 1:
                finish(part)
            else:
                acc_ref = refs[-1]

                @pl.when(k == 0)
                def _():
                    acc_ref[...] = part

                @pl.when(k > 0)
                def _():
                    acc_ref[...] += part

                @pl.when(k == nk - 1)
                def _():
                    finish(acc_ref[...])

        if n_a > 1:
            for p in range(n_a):
                pl.when(k == p)(functools.partial(step, a_refs[p], b_refs[0]))
        elif n_b > 1:
            for p in range(n_b):
                pl.when(j // per_piece == p)(functools.partial(step, a_refs[0], b_refs[p]))
        else:
            step(a_refs[0], b_refs[0])

    if n_a > 1:
        a_specs = [pl.BlockSpec((tm, tk), lambda i, j, k: (i, 0))] * n_a
    else:
        a_specs = [pl.BlockSpec((tk, tm), lambda i, j, k: (k, i)) if ta else pl.BlockSpec((tm, tk), lambda i, j, k: (i, k))]
    if gathered:
        b_arrs = [b.arr]
        b_specs = [pl.BlockSpec((None, tn, tk), lambda i, j, k: (k, j, 0)) if tb else pl.BlockSpec((None, tk, tn), lambda i, j, k: (j, k, 0))]
    elif n_b > 1:
        b_arrs = b_parts
        b_specs = [pl.BlockSpec((tk, tn), functools.partial(lambda p, i, j, k: (k, jnp.clip(j - p * per_piece, 0, per_piece - 1)), p))
                   for p in range(n_b)]
    else:
        b_arrs = b_parts
        b_specs = [pl.BlockSpec((tn, tk), lambda i, j, k: (j, k)) if tb else pl.BlockSpec((tk, tn), lambda i, j, k: (k, j))]
    tile = pl.BlockSpec((tm, tn), lambda i, j, k: (i, j))
    if shard_out:
        out_specs = [pl.BlockSpec((None, tm, tn), lambda i, j, k: (j, i, 0))] * n_out
        out_shape = [jax.ShapeDtypeStruct((N_DEV, M, tn), dt) for dt in out_dtypes]
    else:
        assert row_outs == 0 or tn == N
        out_specs = [tile] * (n_out - row_outs) + [pl.BlockSpec((1, tn), lambda i, j, k: (0, j))] * row_outs
        out_shape = [jax.ShapeDtypeStruct((1, N) if t >= n_out - row_outs else (M, N), dt) for t, dt in enumerate(out_dtypes)]
    assert vmem_bytes(tk) <= VMEM_LIMIT - (4 << 20), (name, vmem_bytes(tk))
    outs = pl.pallas_call(
        body,
        name=name,
        grid=(M // tm, N // tn, nk),
        in_specs=a_specs + b_specs + [tile] * len(extras) + [pl.BlockSpec((1, tn), lambda i, j, k: (0, j))] * len(row_extras)
        + [pl.BlockSpec(_TOKEN, lambda i, j, k: (0, 0))] * len(tokens),
        out_specs=out_specs,
        out_shape=out_shape,
        scratch_shapes=[pltpu.VMEM((tm, tn), F32)] if nk > 1 else [],
        compiler_params=_params(("arbitrary",) * 3 if row_outs else ("parallel", "parallel", "arbitrary")),
    )(*a_parts, *b_arrs, *extras, *row_extras, *tokens)
    return outs[0] if n_out == 1 else outs


def _add(acc, res):
    return (acc + res,)


def _norm_grad(dh, x, dres, g):
    r = lax.rsqrt(jnp.mean(x * x, axis=-1, keepdims=True) + EPS)
    xn = x * r
    dxn = dh * g
    return r * (dxn - xn * jnp.mean(dxn * xn, axis=-1, keepdims=True)) + dres, jnp.sum(dh * xn, axis=0, keepdims=True)


def _add_norm_grad(acc, more, x, dres, g):
    return _norm_grad(acc + more, x, dres, g)


def _add_norm(acc, res, g):
    x = acc + res
    return x, x * lax.rsqrt(jnp.mean(x * x, axis=-1, keepdims=True) + EPS) * g


def _rms_fwd(x, g, *, after=None, name):
    R, C = x.shape
    tm = _tile(R, 256)
    tokens = [] if after is None else [after]

    def body(x_ref, g_ref, *rest):
        xv = x_ref[...]
        r = lax.rsqrt(jnp.mean(xv * xv, axis=-1, keepdims=True) + EPS)
        rest[-1][...] = (xv * r * g_ref[...]).astype(BF16)

    return pl.pallas_call(
        body,
        name=name,
        grid=(R // tm,),
        in_specs=[pl.BlockSpec((tm, C), lambda i: (i, 0)), pl.BlockSpec((1, C), lambda i: (0, 0))]
        + [pl.BlockSpec(_TOKEN, lambda i: (0, 0))] * len(tokens),
        out_specs=pl.BlockSpec((tm, C), lambda i: (i, 0)),
        out_shape=jax.ShapeDtypeStruct((R, C), BF16),
        compiler_params=_params(("parallel",)),
    )(x, g.reshape(1, C), *tokens)


def _rms_bwd(x, g, dh, dres, *, name):
    R, C = x.shape
    tm = _tile(R, 256)

    def body(x_ref, g_ref, dh_ref, dres_ref, dx_ref, dg_ref):
        xv = x_ref[...]
        r = lax.rsqrt(jnp.mean(xv * xv, axis=-1, keepdims=True) + EPS)
        xn = xv * r
        dh_v = dh_ref[...].astype(F32)
        dxn = dh_v * g_ref[...]
        dx_ref[...] = r * (dxn - xn * jnp.mean(dxn * xn, axis=-1, keepdims=True)) + dres_ref[...]
        part = jnp.sum(dh_v * xn, axis=0, keepdims=True)

        @pl.when(pl.program_id(0) == 0)
        def _():
            dg_ref[...] = part

        @pl.when(pl.program_id(0) > 0)
        def _():
            dg_ref[...] += part

    row = pl.BlockSpec((tm, C), lambda i: (i, 0))
    vec = pl.BlockSpec((1, C), lambda i: (0, 0))
    dx, dg = pl.pallas_call(
        body,
        name=name,
        grid=(R // tm,),
        in_specs=[row, vec, row, row],
        out_specs=[row, vec],
        out_shape=[jax.ShapeDtypeStruct((R, C), F32), jax.ShapeDtypeStruct((1, C), F32)],
        compiler_params=_params(("arbitrary",)),
    )(x, g.reshape(1, C), dh, dres)
    return dx, dg.reshape(C)


def _final_loss(x, g, target, *, name):
    R, C = x.shape
    tm = _tile(R, 256)

    def body(x_ref, g_ref, t_ref, loss_ref, dx_ref, dg_ref):
        xv = x_ref[...]
        r = lax.rsqrt(jnp.mean(xv * xv, axis=-1, keepdims=True) + EPS)
        xn = xv * r
        gv = g_ref[...]
        err = xn * gv - t_ref[...]
        lpart = (0.5 / C) * jnp.sum(jnp.sum(err * err, axis=1, keepdims=True), axis=0, keepdims=True)
        dy = err * (1.0 / C)
        dxn = dy * gv
        dx_ref[...] = r * (dxn - xn * jnp.mean(dxn * xn, axis=-1, keepdims=True))
        gpart = jnp.sum(dy * xn, axis=0, keepdims=True)

        @pl.when(pl.program_id(0) == 0)
        def _():
            loss_ref[...] = lpart
            dg_ref[...] = gpart

        @pl.when(pl.program_id(0) > 0)
        def _():
            loss_ref[...] += lpart
            dg_ref[...] += gpart

    row = pl.BlockSpec((tm, C), lambda i: (i, 0))
    vec = pl.BlockSpec((1, C), lambda i: (0, 0))
    loss, dx, dg = pl.pallas_call(
        body,
        name=name,
        grid=(R // tm,),
        in_specs=[row, vec, row],
        out_specs=[pl.BlockSpec((1, 1), lambda i: (0, 0)), row, vec],
        out_shape=[jax.ShapeDtypeStruct((1, 1), F32), jax.ShapeDtypeStruct((R, C), F32), jax.ShapeDtypeStruct((1, C), F32)],
        compiler_params=_params(("arbitrary",)),
    )(x, g.reshape(1, C), target)
    return loss, dx, dg.reshape(C)


def _pool_select(lane, vals):
    out = vals[3]
    for gi in (2, 1, 0):
        out = jnp.where(lane < 64 * (gi + 1), vals[gi], out)
    return out


def _pool_diff(a):
    row, lane = _rows(a.shape), _lanes(a.shape)

    def down(v, k):
        return jnp.where(row >= k, pltpu.roll(v, k, 0), 0.0)

    s2 = a + down(a, 1)
    s4 = s2 + down(s2, 2)
    s8 = s4 + down(s4, 4)
    s16 = s8 + down(s8, 8)
    wsum = _pool_select(lane, (s2, s4, s8, s16))
    win = _pool_select(lane, (2, 4, 8, 16))
    cnt = jnp.minimum(row + 1, win).astype(F32)
    return wsum / cnt - a, cnt


def _pool_diff_t(dd, cnt):
    S = dd.shape[0]
    row, lane = _rows(dd.shape), _lanes(dd.shape)

    def up(v, k):
        return jnp.where(row < S - k, pltpu.roll(v, S - k, 0), 0.0)

    e = dd / cnt
    s2 = e + up(e, 1)
    s4 = s2 + up(s2, 2)
    s8 = s4 + up(s4, 4)
    s16 = s8 + up(s8, 8)
    return _pool_select(lane, (s2, s4, s8, s16)) - dd


def _pool_fwd(rest, wbd, scale, *, name):
    S = rest.shape[0]

    def body(a_ref, w_ref, s_ref, o_ref):
        d, _ = _pool_diff(a_ref[...])
        yp = _dot(d.astype(BF16), w_ref[...], 1, 0)
        o_ref[...] = (yp * s_ref[...]).astype(BF16)

    return pl.pallas_call(
        body,
        name=name,
        grid=(1,),
        in_specs=[
            pl.BlockSpec((S, POOL_W), lambda i: (0, OFF_A // POOL_W)),
            pl.BlockSpec((POOL_W, POOL_W), lambda i: (0, 0)),
            pl.BlockSpec((1, POOL_W), lambda i: (0, 0)),
        ],
        out_specs=pl.BlockSpec((S, POOL_W), lambda i: (0, 0)),
        out_shape=jax.ShapeDtypeStruct((S, POOL_W), BF16),
        compiler_params=_params(("arbitrary",)),
    )(rest, wbd, scale.reshape(1, POOL_W))


def _pool_bwd(rest, wbd, wbd_t, scale, dpa, *, name):
    S = rest.shape[0]

    def body(a_ref, w_ref, wt_ref, s_ref, dpa_ref, da_ref, dw_ref, ds_ref):
        d, cnt = _pool_diff(a_ref[...])
        db = d.astype(BF16)
        yp = _dot(db, w_ref[...], 1, 0)
        dpa_v = dpa_ref[...]
        ds_ref[...] = jnp.sum(dpa_v * yp, axis=0, keepdims=True)
        dyp = (dpa_v * s_ref[...]).astype(BF16)
        dw_ref[...] = _dot(db, dyp, 0, 0)
        dd = _dot(dyp, wt_ref[...], 1, 0)
        da_ref[...] = _pool_diff_t(dd, cnt).astype(BF16)

    full = pl.BlockSpec((S, POOL_W), lambda i: (0, 0))
    sq = pl.BlockSpec((POOL_W, POOL_W), lambda i: (0, 0))
    vec = pl.BlockSpec((1, POOL_W), lambda i: (0, 0))
    return pl.pallas_call(
        body,
        name=name,
        grid=(1,),
        in_specs=[pl.BlockSpec((S, POOL_W), lambda i: (0, OFF_A // POOL_W)), sq, sq, vec, full],
        out_specs=[full, sq, vec],
        out_shape=[
            jax.ShapeDtypeStruct((S, POOL_W), BF16),
            jax.ShapeDtypeStruct((POOL_W, POOL_W), F32),
            jax.ShapeDtypeStruct((1, POOL_W), F32),
        ],
        compiler_params=_params(("arbitrary",)),
    )(rest, wbd, wbd_t, scale.reshape(1, POOL_W), dpa)


def _log_sigmoid(z):
    return jnp.minimum(z, 0.0) - jnp.log(1.0 + jnp.exp(-jnp.abs(z)))


_F_SPEC_COL = OFF_F // F_LANES


def _fox_prep(rest, bpad, *, name):
    S = rest.shape[0]

    def body(f_ref, b_ref, o_ref, ot_ref):
        acc = _log_sigmoid(f_ref[...] + b_ref[...])
        row = _rows(acc.shape)
        k = 1
        while k < S:
            acc = acc + jnp.where(row >= k, pltpu.roll(acc, k, 0), 0.0)
            k *= 2
        o_ref[...] = acc
        ot_ref[...] = acc.T

    return pl.pallas_call(
        body,
        name=name,
        grid=(1,),
        in_specs=[pl.BlockSpec((S, F_LANES), lambda i: (0, _F_SPEC_COL)), pl.BlockSpec((1, F_LANES), lambda i: (0, 0))],
        out_specs=[pl.BlockSpec((S, F_LANES), lambda i: (0, 0)), pl.BlockSpec((F_LANES, S), lambda i: (0, 0))],
        out_shape=[jax.ShapeDtypeStruct((S, F_LANES), F32), jax.ShapeDtypeStruct((F_LANES, S), F32)],
        compiler_params=_params(("arbitrary",)),
    )(rest, bpad)


def _fox_post(rest, bpad, dcum, *, name):
    S = rest.shape[0]

    def body(f_ref, b_ref, d_ref, df_ref, db_ref):
        acc = d_ref[...]
        row = _rows(acc.shape)
        k = 1
        while k < S:
            acc = acc + jnp.where(row < S - k, pltpu.roll(acc, S - k, 0), 0.0)
            k *= 2
        df = acc * (1.0 - _sigmoid(f_ref[...] + b_ref[...]))
        df_ref[...] = df.astype(BF16)
        db_ref[...] = jnp.sum(df, axis=0, keepdims=True)

    full = pl.BlockSpec((S, F_LANES), lambda i: (0, 0))
    vec = pl.BlockSpec((1, F_LANES), lambda i: (0, 0))
    return pl.pallas_call(
        body,
        name=name,
        grid=(1,),
        in_specs=[pl.BlockSpec((S, F_LANES), lambda i: (0, _F_SPEC_COL)), vec, full],
        out_specs=[full, vec],
        out_shape=[jax.ShapeDtypeStruct((S, F_LANES), BF16), jax.ShapeDtypeStruct((1, F_LANES), F32)],
        compiler_params=_params(("arbitrary",)),
    )(rest, bpad, dcum)


_FOX_SCALE = FOX_DH ** -0.5
_PAIRS = FOX_H // 2


def _scaled(v):
    return (v.astype(F32) * _FOX_SCALE).astype(BF16)


def _diag_mask(s):
    return jnp.where(_rows(s.shape) >= _lanes(s.shape), s, NEG)


def _fox_fwd(qkv, cum, fk3, *, name):
    S = qkv.shape[0]
    nk, t = fk3.shape[1:]

    def body(q_ref, k_ref, v_ref, cum_ref, fk_ref, o_ref, lse_ref):
        i = pl.program_id(0)
        lane = _lanes((t, 128))
        lo = lane < FOX_DH
        cumv = cum_ref[...]
        qm, fq = [], []
        for h in range(FOX_H):
            qs = _scaled(q_ref[:, 128 * (h // 2):128 * (h // 2 + 1)])
            zero = jnp.zeros_like(qs)
            qm.append(jnp.where(lo, qs, zero) if h % 2 == 0 else jnp.where(lo, zero, qs))
            fq.append(jnp.broadcast_to(cumv[:, h:h + 1], (t, 128)))

        def tile(j, state, masked):
            m, acc, lsum = (list(part) for part in state)
            k0 = pl.multiple_of(j * t, t)
            for hp in range(_PAIRS):
                cols = slice(128 * hp, 128 * (hp + 1))
                kb = k_ref[pl.ds(k0, t), cols]
                vb = v_ref[pl.ds(k0, t), cols]
                one = jnp.ones_like(vb)
                alphas, pvs = [], []
                for h in (2 * hp, 2 * hp + 1):
                    s = _dot(qm[h], kb, 1, 1) + jnp.concatenate([fq[h]] * (t // 128), axis=1) - fk_ref[h, pl.ds(j, 1), :]
                    if masked:
                        s = _diag_mask(s)
                    m_new = jnp.maximum(m[h], jnp.max(s, axis=-1, keepdims=True))
                    p = jnp.exp(s - m_new)
                    alphas.append(jnp.exp(m[h] - m_new))
                    m[h] = m_new
                    pvs.append(_dot(p.astype(BF16), jnp.where(lo, vb, one) if h % 2 == 0 else jnp.where(lo, one, vb), 1, 0))
                acc[hp] = jnp.where(lo, alphas[0], alphas[1]) * acc[hp] + jnp.where(lo, pvs[0], pvs[1])
                lsum[hp] = jnp.where(lo, alphas[1], alphas[0]) * lsum[hp] + jnp.where(lo, pvs[1], pvs[0])
            return tuple(m), tuple(acc), tuple(lsum)

        zeros = (jnp.zeros((t, 128), F32),) * _PAIRS
        state = lax.fori_loop(0, i, functools.partial(tile, masked=False), ((jnp.full((t, 1), NEG, F32),) * FOX_H, zeros, zeros))
        m, acc, lsum = tile(i, state, True)
        for hp in range(_PAIRS):
            o_ref[:, 128 * hp:128 * (hp + 1)] = acc[hp] / pltpu.roll(lsum[hp], FOX_DH, 1)
            lse = [m[2 * hp] + jnp.log(lsum[hp][:, FOX_DH:FOX_DH + 1]), m[2 * hp + 1] + jnp.log(lsum[hp][:, 0:1])]
            lse_ref[hp] = jnp.where(lane == 0, lse[0], jnp.where(lane == 1, lse[1], 0.0))

    whole = lambda col: pl.BlockSpec((S, FOX_W), lambda i: (0, col))
    return pl.pallas_call(
        body,
        name=name,
        grid=(S // t,),
        in_specs=[
            pl.BlockSpec((t, FOX_W), lambda i: (i, 0)), whole(1), whole(2),
            pl.BlockSpec((t, F_LANES), lambda i: (i, 0)),
            pl.BlockSpec((FOX_H, nk, t), lambda i: (0, 0, 0)),
        ],
        out_specs=[pl.BlockSpec((t, FOX_W), lambda i: (i, 0)), pl.BlockSpec((_PAIRS, t, 128), lambda i: (0, i, 0))],
        out_shape=[jax.ShapeDtypeStruct((S, FOX_W), F32), jax.ShapeDtypeStruct((_PAIRS, S, 128), F32)],
        compiler_params=_params(("arbitrary",)),
    )(qkv, qkv, qkv, cum, fk3)


def _fox_bwd(qkv, cum, fk3, o, do, lse, *, name):
    S = qkv.shape[0]
    nk, t = fk3.shape[1:]
    q_at, k_at, v_at = 0, FOX_W, 2 * FOX_W

    def body(qkv_ref, cum_ref, fk_ref, o_ref, do_ref, lse_ref, dq_ref, dk_ref, dv_ref, dfq_ref, dfk_ref,
             qs_sc, ks_sc, bias_sc, delta_sc, dq_sc):
        lane = _lanes((t, 128))
        lo = lane < FOX_DH
        mine = lambda h: lo if h % 2 == 0 else jnp.logical_not(lo)

        def by_head(tile, values):
            for h, val in enumerate(values):
                tile = jnp.where(lane == h, val, tile)
            return tile

        def prep(i, carry):
            r = pl.ds(pl.multiple_of(i * t, t), t)
            qs_sc[r, :] = _scaled(qkv_ref[r, q_at:q_at + FOX_W])
            ks_sc[r, :] = _scaled(qkv_ref[r, k_at:k_at + FOX_W])
            cum_t = cum_ref[r, :]
            for hp in range(_PAIRS):
                cols = slice(128 * hp, 128 * (hp + 1))
                prod = do_ref[r, cols].astype(F32) * o_ref[r, cols]
                for h in (2 * hp, 2 * hp + 1):
                    delta = jnp.sum(jnp.where(mine(h), prod, 0.0), axis=-1, keepdims=True)
                    delta_sc[h, r, :] = jnp.broadcast_to(delta, (t, 128))
                    bias_sc[h, r, :] = jnp.broadcast_to(cum_t[:, h:h + 1] - lse_ref[hp, r, h % 2:h % 2 + 1], (t, 128))
            dfq_ref[r, :] = jnp.zeros((t, 128), F32)
            dq_sc[r, :] = jnp.zeros((t, FOX_W), F32)
            return carry

        lax.fori_loop(0, nk, prep, 0)

        def kv_tile(j, carry):
            kr = pl.ds(pl.multiple_of(j * t, t), t)

            def q_tile(i, acc, masked):
                dk, dv, dfk = list(acc[:_PAIRS]), list(acc[_PAIRS:2 * _PAIRS]), list(acc[2 * _PAIRS:])
                qr = pl.ds(pl.multiple_of(i * t, t), t)
                dq_old, dfq_old = dq_sc[qr, :], dfq_ref[qr, :]
                wide = lambda a: jnp.concatenate([a] * (t // 128), axis=1)
                row_sums, dq_new = [], []
                for hp in range(_PAIRS):
                    cols = slice(128 * hp, 128 * (hp + 1))
                    kb = qkv_ref[kr, k_at + 128 * hp:k_at + 128 * (hp + 1)]
                    vb = qkv_ref[kr, v_at + 128 * hp:v_at + 128 * (hp + 1)]
                    ksb, qsb, dob = ks_sc[kr, cols], qs_sc[qr, cols], do_ref[qr, cols]
                    zero = jnp.zeros_like(qsb)
                    dq_t = jnp.zeros((t, 128), F32)
                    for h in (2 * hp, 2 * hp + 1):
                        qe, doe, ke = (jnp.where(mine(h), a, zero) for a in (qsb, dob, ksb))
                        s = _dot(qe, kb, 1, 1) + wide(bias_sc[h, qr, :]) - fk_ref[h, pl.ds(j, 1), :]
                        if masked:
                            s = _diag_mask(s)
                        p = jnp.exp(s)
                        dv[hp] = dv[hp] + _dot(p.astype(BF16), doe, 0, 0)
                        dp = _dot(doe, vb, 1, 1)
                        ds = p * (dp - wide(delta_sc[h, qr, :]))
                        dsb = ds.astype(BF16)
                        dk[hp] = dk[hp] + _dot(dsb, qe, 0, 0)
                        dq_t = dq_t + _dot(dsb, ke, 1, 0)
                        row_sums.append(jnp.sum(ds, axis=-1, keepdims=True))
                        dfk[h] = dfk[h] - jnp.sum(ds, axis=0, keepdims=True)
                    dq_new.append(dq_old[:, cols] + dq_t)
                for hp in range(_PAIRS):
                    dq_sc[qr, 128 * hp:128 * (hp + 1)] = dq_new[hp]
                dfq_ref[qr, :] = dfq_old + by_head(jnp.zeros((t, 128), F32), row_sums)
                return (*dk, *dv, *dfk)

            init = tuple([jnp.zeros((t, 128), F32)] * (2 * _PAIRS) + [jnp.zeros((1, t), F32)] * FOX_H)
            acc = q_tile(j, init, True)
            acc = lax.fori_loop(j + 1, nk, functools.partial(q_tile, masked=False), acc)
            for hp in range(_PAIRS):
                cols = slice(128 * hp, 128 * (hp + 1))
                dk_ref[kr, cols] = acc[hp].astype(BF16)
                dv_ref[kr, cols] = acc[_PAIRS + hp].astype(BF16)
            for h in range(FOX_H):
                dfk_ref[h, pl.ds(j, 1), :] = acc[2 * _PAIRS + h]
            return carry

        lax.fori_loop(0, nk, kv_tile, 0)
        dq_ref[...] = dq_sc[...].astype(BF16)

    vm = pl.BlockSpec(memory_space=pltpu.VMEM)
    big = jax.ShapeDtypeStruct((S, FOX_W), BF16)
    return pl.pallas_call(
        body,
        name=name,
        in_specs=[vm] * 6,
        out_specs=[vm] * 5,
        out_shape=[big, big, big, jax.ShapeDtypeStruct((S, 128), F32), jax.ShapeDtypeStruct((FOX_H, nk, t), F32)],
        scratch_shapes=[pltpu.VMEM((S, FOX_W), BF16), pltpu.VMEM((S, FOX_W), BF16), pltpu.VMEM((FOX_H, S, 128), F32),
                        pltpu.VMEM((FOX_H, S, 128), F32), pltpu.VMEM((S, FOX_W), F32)],
        compiler_params=pltpu.CompilerParams(vmem_limit_bytes=VMEM_LIMIT),
    )(qkv, cum, fk3, o, do, lse)


def _group_mask(lane, gi):
    return (lane >= 64 * gi) & (lane < 64 * (gi + 1))


_U_COL = OFF_C // SGU_W


def _sgu_fwd(rest, gn, wm, bias, *, name):
    S = rest.shape[0]
    ts = _tile(S, 512)
    nc = ts // SGU_CHUNK

    def body(u_ref, v_ref, g_ref, w_ref, b_ref, o_ref):
        zv = _gelu(v_ref[...])
        vn = zv * lax.rsqrt(jnp.mean(zv * zv, axis=-1, keepdims=True) + EPS) * g_ref[...]
        lane = _lanes((SGU_CHUNK, SGU_W))
        for c in range(nc):
            rows = slice(c * SGU_CHUNK, (c + 1) * SGU_CHUNK)
            vcb = vn[rows].astype(BF16)
            mixed = b_ref[...]
            for gi in range(4):
                mixed = mixed + jnp.where(_group_mask(lane, gi), _dot(w_ref[gi], vcb, 1, 0), 0.0)
            o_ref[rows, :] = (_gelu(u_ref[rows, :]) * mixed).astype(BF16)

    return pl.pallas_call(
        body,
        name=name,
        grid=(S // ts,),
        in_specs=[
            pl.BlockSpec((ts, SGU_W), lambda i: (i, _U_COL)),
            pl.BlockSpec((ts, SGU_W), lambda i: (i, _U_COL + 1)),
            pl.BlockSpec((1, SGU_W), lambda i: (0, 0)),
            pl.BlockSpec((4, SGU_CHUNK, SGU_CHUNK), lambda i: (0, 0, 0)),
            pl.BlockSpec((SGU_CHUNK, SGU_W), lambda i: (0, 0)),
        ],
        out_specs=pl.BlockSpec((ts, SGU_W), lambda i: (i, 0)),
        out_shape=jax.ShapeDtypeStruct((S, SGU_W), BF16),
        compiler_params=_params(("parallel",)),
    )(rest, rest, gn.reshape(1, SGU_W), wm, bias)


def _sgu_bwd(rest, gn, wm, wm_t, bias, dsg, *, name):
    S = rest.shape[0]
    ts = _tile(S, 512)
    nc = ts // SGU_CHUNK

    def body(u_ref, v_ref, g_ref, w_ref, wt_ref, b_ref, dsg_ref, dc_ref, dw_ref, db_ref, dg_ref):
        first = pl.program_id(0) == 0

        @pl.when(first)
        def _():
            dw_ref[...] = jnp.zeros_like(dw_ref)
            db_ref[...] = jnp.zeros_like(db_ref)
            dg_ref[...] = jnp.zeros_like(dg_ref)

        gv = g_ref[...]
        lane = _lanes((SGU_CHUNK, SGU_W))
        for c in range(nc):
            rows = slice(c * SGU_CHUNK, (c + 1) * SGU_CHUNK)
            vpre = v_ref[rows, :]
            upre = u_ref[rows, :]
            zv = _gelu(vpre)
            r = lax.rsqrt(jnp.mean(zv * zv, axis=-1, keepdims=True) + EPS)
            zn = zv * r
            vcb = (zn * gv).astype(BF16)
            mixed = b_ref[...]
            for gi in range(4):
                mixed = mixed + jnp.where(_group_mask(lane, gi), _dot(w_ref[gi], vcb, 1, 0), 0.0)
            zu = _gelu(upre)
            dsg_v = dsg_ref[rows, :]
            dc_ref[rows, :SGU_W] = (dsg_v * mixed * _gelu_grad(upre)).astype(BF16)
            dmixed = dsg_v * zu
            db_ref[...] += dmixed
            dvn = jnp.zeros((SGU_CHUNK, SGU_W), F32)
            for gi in range(4):
                dmg = jnp.where(_group_mask(lane, gi), dmixed, 0.0).astype(BF16)
                dw_ref[gi] += _dot(dmg, vcb, 1, 1)
                dvn = dvn + _dot(wt_ref[gi], dmg, 1, 0)
            dg_ref[...] += jnp.sum(dvn * zn, axis=0, keepdims=True)
            dzn = dvn * gv
            dzv = r * (dzn - zn * jnp.mean(dzn * zn, axis=-1, keepdims=True))
            dc_ref[rows, SGU_W:] = (dzv * _gelu_grad(vpre)).astype(BF16)

    blk = pl.BlockSpec((ts, SGU_W), lambda i: (i, 0))
    vec = pl.BlockSpec((1, SGU_W), lambda i: (0, 0))
    w3 = pl.BlockSpec((4, SGU_CHUNK, SGU_CHUNK), lambda i: (0, 0, 0))
    bsp = pl.BlockSpec((SGU_CHUNK, SGU_W), lambda i: (0, 0))
    return pl.pallas_call(
        body,
        name=name,
        grid=(S // ts,),
        in_specs=[
            pl.BlockSpec((ts, SGU_W), lambda i: (i, _U_COL)),
            pl.BlockSpec((ts, SGU_W), lambda i: (i, _U_COL + 1)),
            vec, w3, w3, bsp, blk,
        ],
        out_specs=[pl.BlockSpec((ts, 2 * SGU_W), lambda i: (i, 0)), w3, bsp, vec],
        out_shape=[
            jax.ShapeDtypeStruct((S, 2 * SGU_W), BF16),
            jax.ShapeDtypeStruct((4, SGU_CHUNK, SGU_CHUNK), F32),
            jax.ShapeDtypeStruct((SGU_CHUNK, SGU_W), F32),
            jax.ShapeDtypeStruct((1, SGU_W), F32),
        ],
        compiler_params=_params(("arbitrary",)),
    )(rest, rest, gn.reshape(1, SGU_W), wm, wm_t, bias, dsg)


_GT = 512
_G0 = OFF_G // _GT


def _gate_specs(tm, col_of):
    specs = [pl.BlockSpec((tm, _GT), functools.partial(lambda k, *ids: (col_of(*ids)[0], _G0 + 2 * k + col_of(*ids)[1]), k)) for k in range(3)]
    specs += [pl.BlockSpec((1, _GT), functools.partial(lambda k, *ids: (0, 2 * k + col_of(*ids)[1]), k)) for k in range(3)]
    return specs


def _merge_fwd(rest, bg, ya, yb, yc, *, name):
    S = rest.shape[0]
    tm = _tile(S, 512)

    def body(g1, g2, g3, b1, b2, b3, ya_ref, yb_ref, yc_ref, o_ref):
        acc = _sigmoid(g1[...] + b1[...]) * ya_ref[...]
        acc = acc + _sigmoid(g2[...] + b2[...]) * yb_ref[...]
        acc = acc + _sigmoid(g3[...] + b3[...]) * yc_ref[...]
        o_ref[...] = acc.astype(BF16)

    blk = pl.BlockSpec((tm, _GT), lambda i, j: (i, j))
    return pl.pallas_call(
        body,
        name=name,
        grid=(S // tm, D // _GT),
        in_specs=_gate_specs(tm, lambda i, j: (i, j)) + [blk, blk, blk],
        out_specs=blk,
        out_shape=jax.ShapeDtypeStruct((S, D), BF16),
        compiler_params=_params(("parallel", "parallel")),
    )(rest, rest, rest, bg, bg, bg, ya, yb, yc)


def _merge_bwd(rest, bg, ya, yb, yc, dm, *, name):
    S = rest.shape[0]
    tm = _tile(S, 512)

    def body(g1, g2, g3, b1, b2, b3, ya_ref, yb_ref, yc_ref, dm_ref, dya, dyb, dyc, dg1, dg2, dg3, db1, db2, db3):
        first = pl.program_id(1) == 0
        dmv = dm_ref[...]
        for g_ref, b_ref, y_ref, dy_ref, dg_ref, db_ref in (
            (g1, b1, ya_ref, dya, dg1, db1), (g2, b2, yb_ref, dyb, dg2, db2), (g3, b3, yc_ref, dyc, dg3, db3)):
            gate = _sigmoid(g_ref[...] + b_ref[...])
            dy_ref[...] = (dmv * gate).astype(BF16)
            dpre = dmv * y_ref[...] * gate * (1.0 - gate)
            dg_ref[...] = dpre.astype(BF16)
            part = jnp.sum(dpre, axis=0, keepdims=True)

            @pl.when(first)
            def _():
                db_ref[...] = part

            @pl.when(jnp.logical_not(first))
            def _():
                db_ref[...] += part

    blk = pl.BlockSpec((tm, _GT), lambda j, i: (i, j))
    vec = pl.BlockSpec((1, _GT), lambda j, i: (0, j))
    big = jax.ShapeDtypeStruct((S, D), BF16)
    small = jax.ShapeDtypeStruct((1, D), F32)
    return pl.pallas_call(
        body,
        name=name,
        grid=(D // _GT, S // tm),
        in_specs=_gate_specs(tm, lambda j, i: (i, j)) + [blk, blk, blk, blk],
        out_specs=[blk] * 6 + [vec] * 3,
        out_shape=[big] * 6 + [small] * 3,
        compiler_params=_params(("parallel", "arbitrary")),
    )(rest, rest, rest, bg, bg, bg, ya, yb, yc, dm)


_X_SCALE = XDH ** -0.5


def _xattn_fwd(xq, kv, *, name):
    S = xq.shape[0]
    M = kv.shape[0]
    tq = _tile(S, 512)

    def body(q_ref, k_ref, v_ref, o_ref):
        s = _dot(q_ref[...], k_ref[...], 1, 1) * _X_SCALE
        e = jnp.exp(s - jnp.max(s, axis=-1, keepdims=True))
        p = e / jnp.sum(e, axis=-1, keepdims=True)
        o_ref[...] = _dot(p.astype(BF16), v_ref[...], 1, 0).astype(BF16)

    return pl.pallas_call(
        body,
        name=name,
        grid=(S // tq, XH),
        in_specs=[
            pl.BlockSpec((tq, XDH), lambda i, h: (i, h)),
            pl.BlockSpec((M, XDH), lambda i, h: (0, h)),
            pl.BlockSpec((M, XDH), lambda i, h: (0, XH + h)),
        ],
        out_specs=pl.BlockSpec((tq, XDH), lambda i, h: (i, h)),
        out_shape=jax.ShapeDtypeStruct((S, D), BF16),
        compiler_params=_params(("parallel", "parallel")),
    )(xq, kv, kv)


def _xattn_bwd(xq, kv, do, *, name):
    S = xq.shape[0]
    M = kv.shape[0]
    tq = _tile(S, 512)

    def body(q_ref, k_ref, v_ref, do_ref, dq_ref, dk_ref, dv_ref):
        qb = q_ref[...]
        kb = k_ref[...]
        dob = do_ref[...]
        s = _dot(qb, kb, 1, 1) * _X_SCALE
        e = jnp.exp(s - jnp.max(s, axis=-1, keepdims=True))
        p = e / jnp.sum(e, axis=-1, keepdims=True)
        dp = _dot(dob, v_ref[...], 1, 1)
        ds = (p * (dp - jnp.sum(p * dp, axis=-1, keepdims=True)) * _X_SCALE).astype(BF16)
        dq_ref[...] = _dot(ds, kb, 1, 0).astype(BF16)
        dk_part = _dot(ds, qb, 0, 0)
        dv_part = _dot(p.astype(BF16), dob, 0, 0)

        @pl.when(pl.program_id(1) == 0)
        def _():
            dk_ref[...] = dk_part
            dv_ref[...] = dv_part

        @pl.when(pl.program_id(1) > 0)
        def _():
            dk_ref[...] += dk_part
            dv_ref[...] += dv_part

    qspec = pl.BlockSpec((tq, XDH), lambda h, i: (i, h))
    kspec = pl.BlockSpec((M, XDH), lambda h, i: (0, h))
    dxq, dxk, dxv = pl.pallas_call(
        body,
        name=name,
        grid=(XH, S // tq),
        in_specs=[qspec, kspec, pl.BlockSpec((M, XDH), lambda h, i: (0, XH + h)), qspec],
        out_specs=[qspec, kspec, kspec],
        out_shape=[jax.ShapeDtypeStruct((S, D), BF16), jax.ShapeDtypeStruct((M, D), F32), jax.ShapeDtypeStruct((M, D), F32)],
        compiler_params=_params(("parallel", "arbitrary")),
    )(xq, kv, kv, do)
    return dxq, jnp.concatenate([dxk, dxv], axis=1)


def _adam_math(w, g, m, v):
    m = ADAM_B1 * m + (1.0 - ADAM_B1) * g
    v = ADAM_B2 * v + (1.0 - ADAM_B2) * (g * g)
    m_hat = m / (1.0 - ADAM_B1 ** ADAM_STEP)
    v_hat = v / (1.0 - ADAM_B2 ** ADAM_STEP)
    delta = -ADAM_LR * (m_hat / (jnp.sqrt(v_hat) + ADAM_EPS) + ADAM_WD * w)
    return delta, m, v


def _adamw_sharded(parts, w, m, v, *, name):
    _, R, C = w.shape
    Cp = parts[0].shape[2]
    tm = _tile(R, 256)
    nr = R // tm

    def body(p0_ref, p1_ref, w_ref, m_ref, v_ref, g_ref, d_ref, mo_ref, vo_ref):
        def update(p_ref):
            g = p_ref[0][:, :C].astype(F32)
            for dev in range(1, N_DEV):
                g = g + p_ref[dev][:, :C].astype(F32)
            delta, mn, vn = _adam_math(w_ref[...], g, m_ref[...], v_ref[...])
            g_ref[...] = g
            d_ref[...] = delta
            mo_ref[...] = mn
            vo_ref[...] = vn

        @pl.when(pl.program_id(0) == 0)
        def _():
            update(p0_ref)

        @pl.when(pl.program_id(0) == 1)
        def _():
            update(p1_ref)

    p0 = pl.BlockSpec((N_DEV, tm, Cp), lambda l, i: (0, i * (1 - l) + (nr - 1) * l, 0))
    p1 = pl.BlockSpec((N_DEV, tm, Cp), lambda l, i: (0, i * l, 0))
    blk = pl.BlockSpec((None, tm, C), lambda l, i: (l, i, 0))
    sds = jax.ShapeDtypeStruct(w.shape, F32)
    return pl.pallas_call(
        body,
        name=name,
        grid=(DEPTH, nr),
        in_specs=[p0, p1, blk, blk, blk],
        out_specs=[blk] * 4,
        out_shape=[sds] * 4,
        compiler_params=_params(("arbitrary", "arbitrary")),
    )(parts[0], parts[1], w, m, v)


def _adamw_small(g, w, m, v, *, name):
    n = len(g)

    def body(*refs):
        g_refs, w_refs, m_refs, v_refs = (refs[k * n:(k + 1) * n] for k in range(4))
        d_out, m_out, v_out = (refs[(4 + k) * n:(5 + k) * n] for k in range(3))
        for t in range(n):
            delta, mn, vn = _adam_math(w_refs[t][...], g_refs[t][...], m_refs[t][...], v_refs[t][...])
            d_out[t][...] = delta
            m_out[t][...] = mn
            v_out[t][...] = vn

    vm = pl.BlockSpec(memory_space=pltpu.VMEM)
    shapes = [jax.ShapeDtypeStruct(a.shape, F32) for a in w]
    outs = pl.pallas_call(
        body,
        name=name,
        in_specs=[vm] * (4 * n),
        out_specs=[vm] * (3 * n),
        out_shape=shapes * 3,
        compiler_params=pltpu.CompilerParams(vmem_limit_bytes=VMEM_LIMIT),
    )(*g, *w, *m, *v)
    return outs[:n], outs[n:2 * n], outs[2 * n:]


def _position():
    return lax.axis_index("x"), lax.axis_index("y"), lax.axis_index("c")


def _dev_index(px, py, pc):
    return 4 * px + 2 * py + pc


_ANY = pl.BlockSpec(memory_space=pl.ANY)


def _peers(x, y, c):
    out = []
    for mask in range(1, N_DEV):
        fx, fy, fc = (mask >> 2) & 1, (mask >> 1) & 1, mask & 1
        out.append((1 - x if fx else x, 1 - y if fy else y, 1 - c if fc else c))
    return out


_HBM = pl.BlockSpec(memory_space=pltpu.HBM)
_SEM = pl.BlockSpec(memory_space=pltpu.SEMAPHORE)


def _own_block_placed(block, like):
    x, y, c = _position()
    return lax.dynamic_update_index_in_dim(lax.empty(like.shape, like.dtype), block, _dev_index(x, y, c), 0)


_COPY_BYTES = 256 << 10
_MAX_PIECES = 8


def _pieces(blocks):
    out = []
    for t, b in enumerate(blocks):
        R, C = b.shape[-2:]
        n = max(1, min(_MAX_PIECES, R * C * jnp.dtype(b.dtype).itemsize // _COPY_BYTES))
        while n > 1 and R % (16 * n):
            n -= 1
        out += [(t, pl.ds(j * (R // n), R // n) if n > 1 else None) for j in range(n)]
    return out


def _cut(block, rows):
    return block if rows is None else block.at[rows]


def _copies(per_piece):
    def mark(fn):
        fn.per_piece = per_piece
        return fn
    return mark


@_copies(N_DEV - 1)
def _plan_exchange(srcs, lands, send_sems, recv_sems, arrivals):
    x, y, c = _position()
    me = _dev_index(x, y, c)
    out = []
    for k, peer in enumerate(_peers(x, y, c)):
        p = _dev_index(*peer)
        for i, (t, rows) in enumerate(_pieces(lands)):
            sems = dict(send_sem=send_sems.at[7 * i + k], recv_sem=recv_sems.at[7 * i + k], device_id=peer, device_id_type=MESH)
            src, dst = (lands[t].at[p], lands[t].at[p]) if arrivals else (srcs[t].at[p], lands[t].at[me])
            out.append(pltpu.make_async_remote_copy(src_ref=_cut(src, rows), dst_ref=_cut(dst, rows), **sems))
    return out


@_copies(N_DEV - 1)
def _plan_broadcast(srcs, lands, send_sems, recv_sems, arrivals):
    x, y, c = _position()
    me = _dev_index(x, y, c)
    out = []
    for k, peer in enumerate(_peers(x, y, c)):
        p = _dev_index(*peer)
        for i, (t, rows) in enumerate(_pieces(lands)):
            sems = dict(send_sem=send_sems.at[7 * i + k], recv_sem=recv_sems.at[7 * i + k], device_id=peer, device_id_type=MESH)
            src, dst = (lands[t].at[p], lands[t].at[p]) if arrivals else (srcs[t], lands[t].at[me])
            out.append(pltpu.make_async_remote_copy(src_ref=_cut(src, rows), dst_ref=_cut(dst, rows), **sems))
    return out


@_copies(4)
def _plan_gather_out(srcs, lands, send_sems, recv_sems, arrivals):
    x, y, c = _position()
    me = _dev_index(x, y, c)
    out = []
    for k, peer in enumerate([(x, y, 1 - c), (1 - x, y, c), (x, 1 - y, c), (1 - x, 1 - y, c)]):
        p = _dev_index(*peer)
        for i, (t, rows) in enumerate(_pieces(lands)):
            sems = dict(send_sem=send_sems.at[4 * i + k], recv_sem=recv_sems.at[4 * i + k], device_id=peer, device_id_type=MESH)
            src, dst = (lands[t].at[p], lands[t].at[p]) if arrivals else (srcs[t], lands[t].at[me])
            out.append(pltpu.make_async_remote_copy(src_ref=_cut(src, rows), dst_ref=_cut(dst, rows), **sems))
    return out


@_copies(3)
def _plan_gather_pass(srcs, lands, send_sems, recv_sems, arrivals):
    x, y, c = _position()
    sibling = (x, y, 1 - c)
    out = []
    for k, chip in enumerate([(1 - x, y), (x, 1 - y), (1 - x, 1 - y)]):
        p = _dev_index(*chip, 1 - c) if arrivals else _dev_index(*chip, c)
        for i, (t, rows) in enumerate(_pieces(lands)):
            sems = dict(send_sem=send_sems.at[3 * i + k], recv_sem=recv_sems.at[3 * i + k], device_id=sibling, device_id_type=MESH)
            block = _cut(lands[t].at[p], rows)
            out.append(pltpu.make_async_remote_copy(src_ref=block, dst_ref=block, **sems))
    return out


def _split_start(plan, srcs, lands, *, after=None, name):
    n_src, n = len(srcs), len(srcs) + len(lands)
    n_sem = plan.per_piece * len(_pieces(lands))
    order = [] if after is None else [after]

    def body(*refs):
        send_sems, recv_sems = refs[n + len(order):n + len(order) + 2]
        token = refs[-1]
        for cp in plan(refs[:n_src], refs[n_src:n], send_sems, recv_sems, arrivals=False):
            cp.start()
        token[...] = jnp.zeros_like(token)

    hbm = lambda a: pltpu.HBM(a.shape, a.dtype)
    outs = pl.pallas_call(
        body,
        name=name,
        in_specs=[_HBM] * n + [_ANY] * len(order),
        out_specs=[_SEM, _SEM] + [_HBM] * n + [pl.BlockSpec(memory_space=pltpu.VMEM)],
        out_shape=[pltpu.SemaphoreType.DMA((n_sem,)), pltpu.SemaphoreType.DMA((n_sem,))] + [hbm(a) for a in (*srcs, *lands)]
        + [jax.ShapeDtypeStruct(_TOKEN, F32)],
        input_output_aliases={i: 2 + i for i in range(n)},
        compiler_params=pltpu.CompilerParams(has_side_effects=pltpu.SideEffectType.DATAFLOW_SIDE_EFFECTING),
    )(*[pltpu.with_memory_space_constraint(a, pltpu.HBM) for a in (*srcs, *lands)], *order)
    return (outs[0], outs[1], outs[2:2 + n_src], outs[2 + n_src:2 + n]), outs[-1]


def _split_wait(plan, state, after, *, name):
    send_sems, recv_sems, srcs, lands = state
    n_src, n = len(srcs), len(srcs) + len(lands)

    def body(*refs):
        send_refs, recv_refs = refs[n:n + 2]
        for cp in plan(refs[:n_src], refs[n_src:n], send_refs, recv_refs, arrivals=False):
            cp.wait_send()
        for cp in plan(refs[:n_src], refs[n_src:n], send_refs, recv_refs, arrivals=True):
            cp.wait_recv()

    hbm = lambda a: pltpu.HBM(a.shape, a.dtype)
    outs = pl.pallas_call(
        body,
        name=name,
        in_specs=[_HBM] * n + [_SEM, _SEM, _ANY],
        out_specs=[_HBM] * n,
        out_shape=[hbm(a) for a in (*srcs, *lands)],
        input_output_aliases={i: i for i in range(n)},
        compiler_params=pltpu.CompilerParams(has_side_effects=pltpu.SideEffectType.DATAFLOW_SIDE_EFFECTING),
    )(*srcs, *lands, send_sems, recv_sems, after)
    return outs[n_src:]


def _sum_blocks(blocks, *, name):
    _, R, C = blocks.shape
    tm = next(R // n for n in (4, 3, 2, 1) if R % (8 * n) == 0)

    def body(b_ref, o_ref):
        g = b_ref[0]
        for dev in range(1, N_DEV):
            g = g + b_ref[dev]
        o_ref[...] = g

    return pl.pallas_call(
        body,
        name=name,
        grid=(R // tm,),
        in_specs=[pl.BlockSpec((N_DEV, tm, C), lambda i: (0, i, 0))],
        out_specs=pl.BlockSpec((tm, C), lambda i: (i, 0)),
        out_shape=jax.ShapeDtypeStruct((R, C), F32),
        compiler_params=_params(("parallel",)),
    )(blocks)


def _block_diag(w):
    out = jnp.zeros((POOL_W, POOL_W), w.dtype)
    for gi in range(4):
        out = out.at[64 * gi:64 * (gi + 1), 64 * gi:64 * (gi + 1)].set(w[gi])
    return out


def _layer_consts(sp, l):
    causal = jnp.tril(jnp.ones((SGU_CHUNK, SGU_CHUNK), F32))
    wm = (sp["sgu_w"][l] * causal[None]).astype(BF16)
    wbd = _block_diag(sp["pool_w"][l]).astype(BF16)
    return dict(
        wbd=wbd, wbd_t=wbd.T, wm=wm, wm_t=wm.transpose(0, 2, 1),
        sgu_bias=jnp.repeat(sp["sgu_b"][l].T, 64, axis=1),
        bpad=jnp.pad(sp["b_forget"][l], (0, F_LANES - FOX_H)).reshape(1, F_LANES),
        bg=sp["b_gate"][l].reshape(1, 3 * D),
    )


def _relu2(acc):
    return acc, jnp.square(jnp.maximum(acc, 0.0))


def _relu2_grad(acc, z):
    return (acc * 2.0 * jnp.maximum(z, 0.0),)


def _layer_fwd(l, x, h, mem, source, sp):
    S = x.shape[0]
    t = _tile(S, 256)
    c = _layer_consts(sp, l)
    n = f"l{l}_"
    W, after = source(l, "begin", x)
    if h is None:
        h, after = _rms_fwd(x, sp["norm_mix_g"][l], after=after, name=n + "norm_mix"), None
    hm = _rms_fwd(mem, sp["norm_mem_g"][l], name=n + "norm_mem")
    more, token = source(l, "normed", hm)
    W.update(more)
    qkv = _mm(h, W["qkv"], out_dtypes=(BF16,), after=after if token is None else token, name=n + "qkv")
    rest = _mm(h, W["rest"], name=n + "rest")
    pa = _pool_fwd(rest, c["wbd"], sp["pool_scale"][l], name=n + "pool")
    cum, cum_t = _fox_prep(rest, c["bpad"], name=n + "fox_prep")
    fk3 = cum_t[:FOX_H].reshape(FOX_H, S // t, t)
    o, lse = _fox_fwd(qkv, cum, fk3, name=n + "fox")
    more, _ = source(l, "attended", o)
    W.update(more)
    sg = _sgu_fwd(rest, sp["sgu_norm_g"][l], c["wm"], c["sgu_bias"], name=n + "sgu")
    more, after = source(l, "mixed", sg)
    W.update(more)
    ya = _mm(pa, W["ba"], out_dtypes=(BF16,), after=after, name=n + "branch_a")
    yb = _mm(o, W["bb"], out_dtypes=(BF16,), name=n + "branch_b")
    yc = _mm(sg, W["bc"], out_dtypes=(BF16,), name=n + "branch_c")
    merged = _merge_fwd(rest, c["bg"], ya, yb, yc, name=n + "merge")
    whole_rows = dict(epilogue=_add_norm, out_dtypes=(F32, BF16), tm=1024, tn=D)
    x1, hx = _mm(merged, W["out"], extras=(x,), row_extras=(sp["norm_xattn_g"][l].reshape(1, D),), name=n + "out", **whole_rows)
    xq = _mm(hx, W["xq"], out_dtypes=(BF16,), name=n + "xq")
    kv = _mm(hm, W["xkv"], out_dtypes=(BF16,), name=n + "xkv")
    o2 = _xattn_fwd(xq, kv, name=n + "xattn")
    x2, hf = _mm(o2, W["xo"], extras=(x1,), row_extras=(sp["norm_ffn_g"][l].reshape(1, D),), name=n + "xo", **whole_rows)
    z, act = _mm(hf, W["ff1"], epilogue=_relu2, out_dtypes=(BF16, BF16), name=n + "ff1")
    _, after = source(l, "expanded", act)
    if l + 1 < DEPTH:
        x3, h_next = _mm(act, W["ff2"], extras=(x2,), row_extras=(sp["norm_mix_g"][l + 1].reshape(1, D),), after=after, name=n + "ff2",
                         **whole_rows)
    else:
        x3, h_next = _mm(act, W["ff2"], extras=(x2,), epilogue=_add, after=after, name=n + "ff2"), None
    saved = dict(x=x, h=h, qkv=qkv, rest=rest, pa=pa, cum=cum, fk3=fk3, o=o, lse=lse, sg=sg, ya=ya, yb=yb, yc=yc,
                 merged=merged, x1=x1, hx=hx, hm=hm, xq=xq, kv=kv, o2=o2, x2=x2, hf=hf, z=z, act=act, c=c)
    return x3, h_next, saved, W


def _layer_bwd(l, dx3, sv, mem, W, sp, grads_done):
    S = dx3.shape[0]
    c = sv["c"]
    n = f"l{l}b_"
    bf = dict(out_dtypes=(BF16,))
    gw, gs = {}, {}
    gw["ff2"] = _mm(sv["act"], dx3, ta=True, name=n + "dw_ff2", **bf)
    dz = _mm(dx3, W["ff2"], tb=True, extras=(sv["z"],), epilogue=_relu2_grad, name=n + "dz", **bf)
    gw["ff1"] = _mm(sv["hf"], dz, ta=True, shard_out=True, name=n + "dw_ff1", **bf)
    whole_rows = dict(epilogue=_norm_grad, out_dtypes=(F32, F32), row_outs=1, tm=1024, tn=D)
    gain = lambda key: (sp[key][l].reshape(1, D),)
    dx2, dg = _mm(dz, W["ff1"], tb=True, extras=(sv["x2"], dx3), row_extras=gain("norm_ffn_g"), name=n + "dhf", **whole_rows)
    gs["norm_ffn_g"] = dg.reshape(D)
    gw["xo"] = _mm(sv["o2"], dx2, ta=True, name=n + "dw_xo", **bf)
    do2 = _mm(dx2, W["xo"], tb=True, name=n + "do2", **bf)
    dxq, dkv = _xattn_bwd(sv["xq"], sv["kv"], do2, name=n + "dxattn")
    gw["xq"] = _mm(sv["hx"], dxq, ta=True, name=n + "dw_xq", **bf)
    gw["xkv"] = _mm(sv["hm"], dkv, ta=True, shard_out=True, name=n + "dw_xkv", **bf)
    dhm = _mm(dkv, W["xkv"], tb=True, name=n + "dhm")
    _, gs["norm_mem_g"] = _rms_bwd(mem, sp["norm_mem_g"][l], dhm, jnp.zeros_like(mem), name=n + "dnorm_mem")
    dx1, dg = _mm(dxq, W["xq"], tb=True, extras=(sv["x1"], dx2), row_extras=gain("norm_xattn_g"), name=n + "dhx", **whole_rows)
    gs["norm_xattn_g"] = dg.reshape(D)
    after, gw = grads_done(l, gw), {}
    gw["out"] = _mm(sv["merged"], dx1, ta=True, name=n + "dw_out", **bf)
    dm = _mm(dx1, W["out"], tb=True, after=after, name=n + "dmerged")
    dya, dyb, dyc, dg1, dg2, dg3, db1, db2, db3 = _merge_bwd(sv["rest"], c["bg"], sv["ya"], sv["yb"], sv["yc"], dm, name=n + "dmerge")
    gs["b_gate"] = jnp.concatenate([db1, db2, db3], axis=1).reshape(3 * D)
    gw["ba"] = _mm(sv["pa"], dya, ta=True, shard_out=True, name=n + "dw_ba", **bf)
    gw["bb"] = _mm(sv["o"], dyb, ta=True, shard_out=True, name=n + "dw_bb", **bf)
    gw["bc"] = _mm(sv["sg"], dyc, ta=True, shard_out=True, name=n + "dw_bc", **bf)
    after, gw = grads_done(l, gw), {}
    dpa = _mm(dya, W["ba"], tb=True, name=n + "dpa")
    do = _mm(dyb, W["bb"], tb=True, after=after, name=n + "do", **bf)
    dsg = _mm(dyc, W["bc"], tb=True, name=n + "dsg")
    da, dwbd, dscale = _pool_bwd(sv["rest"], c["wbd"], c["wbd_t"], sp["pool_scale"][l], dpa, name=n + "dpool")
    gs["pool_w"] = jnp.stack([dwbd[64 * gi:64 * (gi + 1), 64 * gi:64 * (gi + 1)] for gi in range(4)])
    gs["pool_scale"] = dscale.reshape(POOL_W)
    dq, dk, dv, dfq, dfk = _fox_bwd(sv["qkv"], sv["cum"], sv["fk3"], sv["o"], do, sv["lse"], name=n + "dfox")
    dcum = dfq + jnp.pad(dfk.reshape(FOX_H, S).T, ((0, 0), (0, F_LANES - FOX_H)))
    df, dbf = _fox_post(sv["rest"], c["bpad"], dcum, name=n + "dfox_post")
    gs["b_forget"] = dbf[0, :FOX_H]
    dc, dwm, dbias, dgn = _sgu_bwd(sv["rest"], sp["sgu_norm_g"][l], c["wm"], c["wm_t"], c["sgu_bias"], dsg, name=n + "dsgu")
    gs["sgu_w"] = dwm * jnp.tril(jnp.ones((SGU_CHUNK, SGU_CHUNK), F32))[None]
    gs["sgu_b"] = dbias.reshape(SGU_CHUNK, 4, 64).sum(axis=2).T
    gs["sgu_norm_g"] = dgn.reshape(SGU_W)
    dqkv = [dq, dk, dv]
    drest = [jnp.concatenate([da, df, jnp.zeros((S, OFF_C - OFF_F - F_LANES), BF16), dc], axis=1), dg1, dg2, dg3]
    gw["qkv"] = _mm(sv["h"], dqkv, ta=True, name=n + "dw_qkv", **bf)
    gw["rest"] = _mm(sv["h"], drest, ta=True, name=n + "dw_rest", **bf)
    after = grads_done(l, gw)
    dh = _mm(dqkv, W["qkv"], tb=True, after=after, name=n + "dh_qkv")
    dx, dg = _mm(drest, W["rest"], tb=True, extras=(dh, sv["x"], dx1), row_extras=gain("norm_mix_g"), name=n + "dh",
                 **{**whole_rows, "epilogue": _add_norm_grad, "tm": 512})
    gs["norm_mix_g"] = dg.reshape(D)
    return dx, gs


def _local_step(x, mem, target, sp, source, grads_done):
    saved, Ws, h = [], [], None
    for l in range(DEPTH):
        x, h, sv, W = _layer_fwd(l, x, h, mem, source, sp)
        saved.append(sv)
        Ws.append(W)
    loss, dx, dgf = _final_loss(x, sp["final_norm_g"], target, name="final_loss")
    gss = [None] * DEPTH
    for l in reversed(range(DEPTH)):
        dx, gss[l] = _layer_bwd(l, dx, saved[l], mem, Ws[l], sp, grads_done)
    small = {k: jnp.stack([gss[l][k] for l in range(DEPTH)]) for k in gss[0]}
    small["final_norm_g"] = dgf
    return loss, dx, small


_SMALL = ["norm_mix_g", "b_forget", "pool_w", "pool_scale", "sgu_norm_g", "sgu_w", "sgu_b", "b_gate", "norm_xattn_g",
          "norm_mem_g", "norm_ffn_g", "final_norm_g"]
_COL = {"w_branch_a": "ba", "w_branch_b": "bb", "w_branch_c": "bc", "w_xkv": "xkv", "w_ff1": "ff1"}
_ROW = {"w_out": "out", "w_xq": "xq", "w_xo": "xo", "w_ff2": "ff2"}
_BIG = ["w_in", "w_branch_a", "w_branch_b", "w_branch_c", "w_out", "w_xq", "w_xkv", "w_xo", "w_ff1", "w_ff2"]
_PACK_LANES = 128


def _as_rows(a):
    return a.reshape(-1, a.shape[-1])


def _pack(tensors):
    rows = []
    for a in tensors:
        flat = a.reshape(-1)
        flat = jnp.pad(flat, (0, (-flat.shape[0]) % (8 * _PACK_LANES)))
        rows.append(flat.reshape(-1, _PACK_LANES))
    n_rows = sum(r.shape[0] for r in rows)
    rows.append(jnp.zeros(((-n_rows) % (8 * N_DEV), _PACK_LANES), F32))
    return jnp.concatenate(rows, axis=0)


def _unpack(packed, like):
    out, r = [], 0
    for a in like:
        size = math.prod(a.shape)
        nr = 8 * (-(-size // (8 * _PACK_LANES)))
        out.append(packed[r:r + nr].reshape(-1)[:size].reshape(a.shape))
        r += nr
    return out


_SHARD_IN = N_IN // N_DEV
_SHARD_IN_PAD = -(-_SHARD_IN // 128) * 128


def _columns(pieces, start, stop):
    out, at = [], 0
    for p in pieces:
        lo, hi = max(start, at), min(stop, at + p.shape[1])
        if lo < hi:
            out.append(p[:, lo - at:hi - at])
        at += p.shape[1]
    return out


def _split_w_in(blocks):
    K = blocks[0].shape[0]
    pad = jnp.zeros((K, OFF_C - OFF_F - FOX_H), blocks[0].dtype)
    cols = functools.partial(_columns, blocks)
    rest = jnp.concatenate(cols(0, R_OFF_Q) + cols(R_OFF_F, R_OFF_C) + [pad] + cols(R_OFF_C, N_IN), axis=1)
    return jnp.concatenate(cols(R_OFF_Q, R_OFF_F), axis=1), rest


def _join_w_in(qkv, rest):
    in_order = [rest[:, :R_OFF_Q], qkv, rest[:, OFF_F:OFF_F + FOX_H], rest[:, OFF_C:]]
    pad = jnp.zeros((qkv.shape[0], _SHARD_IN_PAD - _SHARD_IN), qkv.dtype)
    return jnp.stack([jnp.concatenate(_columns(in_order, _SHARD_IN * d, _SHARD_IN * (d + 1)) + [pad], axis=1) for d in range(N_DEV)])


_FIRST = ["w_in"]
_LATER = [k for k in _BIG if k not in _FIRST]


def _layer_weights(gathered):
    W = {}
    if "w_in" in gathered:
        W.update(zip(("qkv", "rest"), _split_w_in([gathered["w_in"][d][:, :_SHARD_IN] for d in range(N_DEV)])))
    for name, key in _COL.items():
        if name in gathered:
            W[key] = _Gathered(gathered[name])
    for name, key in _ROW.items():
        if name in gathered:
            W[key] = gathered[name].reshape(-1, gathered[name].shape[-1])
    return W


def _grad_blocks(gw):
    parts = {}
    if "qkv" in gw:
        parts["w_in"] = _join_w_in(gw["qkv"], gw["rest"])
    for name, key in _COL.items():
        if key in gw:
            parts[name] = gw[key]
    for name, key in _ROW.items():
        if key in gw:
            parts[name] = gw[key].reshape(N_DEV, -1, gw[key].shape[-1])
    return parts


def kernel(x, mem, norm_mix_g, w_in, b_forget, pool_w, pool_scale, sgu_norm_g, sgu_w, sgu_b, w_branch_a, w_branch_b, w_branch_c, b_gate, w_out, norm_xattn_g, norm_mem_g, w_xq, w_xkv, w_xo, norm_ffn_g, w_ff1, w_ff2, final_norm_g, loss_target, m_norm_mix_g, m_w_in, m_b_forget, m_pool_w, m_pool_scale, m_sgu_norm_g, m_sgu_w, m_sgu_b, m_w_branch_a, m_w_branch_b, m_w_branch_c, m_b_gate, m_w_out, m_norm_xattn_g, m_norm_mem_g, m_w_xq, m_w_xkv, m_w_xo, m_norm_ffn_g, m_w_ff1, m_w_ff2, m_final_norm_g, v_norm_mix_g, v_w_in, v_b_forget, v_pool_w, v_pool_scale, v_sgu_norm_g, v_sgu_w, v_sgu_b, v_w_branch_a, v_w_branch_b, v_w_branch_c, v_b_gate, v_w_out, v_norm_xattn_g, v_norm_mem_g, v_w_xq, v_w_xkv, v_w_xo, v_norm_ffn_g, v_w_ff1, v_w_ff2, v_final_norm_g):
    names = ["norm_mix_g", "w_in", "b_forget", "pool_w", "pool_scale", "sgu_norm_g", "sgu_w", "sgu_b", "w_branch_a", "w_branch_b",
             "w_branch_c", "b_gate", "w_out", "norm_xattn_g", "norm_mem_g", "w_xq", "w_xkv", "w_xo", "norm_ffn_g", "w_ff1", "w_ff2",
             "final_norm_g"]
    w = dict(zip(names, [norm_mix_g, w_in, b_forget, pool_w, pool_scale, sgu_norm_g, sgu_w, sgu_b, w_branch_a, w_branch_b, w_branch_c,
                         b_gate, w_out, norm_xattn_g, norm_mem_g, w_xq, w_xkv, w_xo, norm_ffn_g, w_ff1, w_ff2, final_norm_g]))
    m = dict(zip(names, [m_norm_mix_g, m_w_in, m_b_forget, m_pool_w, m_pool_scale, m_sgu_norm_g, m_sgu_w, m_sgu_b, m_w_branch_a,
                         m_w_branch_b, m_w_branch_c, m_b_gate, m_w_out, m_norm_xattn_g, m_norm_mem_g, m_w_xq, m_w_xkv, m_w_xo,
                         m_norm_ffn_g, m_w_ff1, m_w_ff2, m_final_norm_g]))
    v = dict(zip(names, [v_norm_mix_g, v_w_in, v_b_forget, v_pool_w, v_pool_scale, v_sgu_norm_g, v_sgu_w, v_sgu_b, v_w_branch_a,
                         v_w_branch_b, v_w_branch_c, v_b_gate, v_w_out, v_norm_xattn_g, v_norm_mem_g, v_w_xq, v_w_xkv, v_w_xo,
                         v_norm_ffn_g, v_w_ff1, v_w_ff2, v_final_norm_g]))

    sp = {k: w[k] for k in _SMALL}
    shards = [{k: w[k][l].astype(BF16) for k in _BIG} for l in range(DEPTH)]
    for sh in shards:
        sh["w_in"] = jnp.pad(sh["w_in"], ((0, 0), (0, _SHARD_IN_PAD - _SHARD_IN)))
    me = _dev_index(*_position())

    def gather_out(l, keys, name, after=None):
        srcs = [shards[l][k] for k in keys]
        lands = [_own_block_placed(a, jax.ShapeDtypeStruct((N_DEV, *a.shape), a.dtype)) for a in srcs]
        state, token = _split_start(_plan_gather_out, srcs, lands, after=after, name=name + "_out_start")
        return (keys, name, state), token

    def gather_pass(job, value):
        keys, name, state = job
        lands = _split_wait(_plan_gather_out, state, value, name=name + "_out_wait")
        state, token = _split_start(_plan_gather_pass, [], lands, name=name + "_pass_start")
        return (keys, name, state), token, lands[0]

    def gather_end(job, value):
        keys, name, state = job
        return _layer_weights(dict(zip(keys, _split_wait(_plan_gather_pass, state, value, name=name + "_pass_wait"))))

    jobs = {}

    def source(l, point, value):
        if (l, point) == (0, "begin"):
            jobs["l0_first"], token = gather_out(0, _FIRST, "gather_l0_first")
            return {}, token
        if (l, point) == (0, "normed"):
            jobs["l0_first"], token, arrived = gather_pass(jobs["l0_first"], value)
            jobs["l0"], _ = gather_out(0, _LATER, "gather_l0", after=arrived)
            return gather_end(jobs.pop("l0_first"), token), None
        if (l, point) == (0, "attended"):
            jobs["l0"], _, arrived = gather_pass(jobs["l0"], value)
            jobs["l1_first"], token = gather_out(1, _FIRST, "gather_l1_first", after=arrived)
            jobs["l1"], jobs["token"] = gather_out(1, _LATER, "gather_l1", after=token)
            return {}, None
        if (l, point) == (0, "mixed"):
            return gather_end(jobs.pop("l0"), value), jobs.pop("token")
        if (l, point) == (0, "expanded"):
            jobs["l1_first"], token, _ = gather_pass(jobs["l1_first"], value)
            return {}, token
        if (l, point) == (1, "begin"):
            W = gather_end(jobs.pop("l1_first"), value)
            jobs["l1"], token, _ = gather_pass(jobs["l1"], value)
            return W, token
        if (l, point) == (1, "mixed"):
            return gather_end(jobs.pop("l1"), value), None
        return {}, None

    received = [{} for _ in range(DEPTH)]
    travelling = []

    def grads_done(l, gw):
        blocks = _grad_blocks(gw)
        keys = [k for k in _BIG if k in blocks]
        parts = [blocks[k] for k in keys]
        group = f"exchange_grads_l{l}_" + ("in" if "w_in" in blocks else "merge" if "w_out" in blocks else "mlp")
        lands = [_own_block_placed(lax.dynamic_index_in_dim(p, me, 0, keepdims=False), p) for p in parts]
        state, token = _split_start(_plan_exchange, parts, lands, name=group + "_start")
        travelling.append((l, keys, state, group + "_wait"))
        return token

    loss, dx, small = _local_step(x[0], mem[0], loss_target[0], sp, source, grads_done)
    grads, deltas, new_m, new_v = {}, {}, {}, {}
    like = [loss] + [w[k] for k in _SMALL]
    packed = _pack([loss] + [small[k] for k in _SMALL])
    eighths = packed.reshape(N_DEV, -1, _PACK_LANES)
    own = lambda a: _own_block_placed(lax.dynamic_index_in_dim(a, me, 0, keepdims=False) if a.ndim == 3 else a, eighths)
    scatter, done = _split_start(_plan_exchange, [eighths], [own(eighths)], after=dx, name="small_grads_scatter_start")

    def reduce_small(after):
        mine = _sum_blocks(_split_wait(_plan_exchange, scatter, after, name="small_grads_scatter_wait")[0], name="small_grads_sum")
        return _split_start(_plan_broadcast, [mine], [own(mine)], name="small_grads_gather_start")

    def update_small(state, after):
        total = _split_wait(_plan_broadcast, state, after, name="small_grads_gather_wait")[0].reshape(packed.shape)
        loss_sum, *g_small = _unpack(total, like)
        rows = lambda d: [_as_rows(d[k]) for k in _SMALL]
        outs = _adamw_small([_as_rows(g) for g in g_small], rows(w), rows(m), rows(v), name="adamw_small")
        grads.update(zip(_SMALL, g_small))
        for dst, vals in zip((deltas, new_m, new_v), outs):
            dst.update({k: a.reshape(w[k].shape) for k, a in zip(_SMALL, vals)})
        return loss_sum[0, 0], outs[0][0]

    groups = list(dict.fromkeys(tuple(keys) for _, keys, _, _ in travelling))
    for n_done, group_keys in enumerate(groups):
        if n_done == 1:
            gather, _ = reduce_small(done)
        if n_done == len(groups) - 1:
            loss, done = update_small(gather, done)
        for l, keys, state, wait_name in travelling:
            if tuple(keys) == group_keys:
                received[l].update(zip(keys, _split_wait(_plan_exchange, state, done, name=wait_name)))
        for k in group_keys:
            outs = _adamw_sharded([received[l][k] for l in range(DEPTH)], w[k], m[k], v[k], name="adamw_" + k)
            grads[k], deltas[k], new_m[k], new_v[k] = outs
        done = grads[group_keys[-1]]

    return (loss, dx[None], *[grads[k] for k in names], *[deltas[k] for k in names], *[new_m[k] for k in names],
            *[new_v[k] for k in names])
```

```python
import functools
import math

import jax
import jax.numpy as jnp
from jax import lax
from jax.experimental import pallas as pl
from jax.experimental.pallas import tpu as pltpu

F32 = jnp.float32
BF16 = jnp.bfloat16
MESH = pl.DeviceIdType.MESH

N_DEV = 8
D = 1024
DEPTH = 2
EPS = 1e-6
NEG = -1e30
POOL_W = 256
FOX_H = 8
FOX_DH = 64
FOX_W = 512
SGU_W = 256
SGU_CHUNK = 128
XH = 4
XDH = 256
N_IN = 5384
R_OFF_Q, R_OFF_F, R_OFF_C = 256, 1792, 1800
QKV_W = 3 * FOX_W
OFF_A, OFF_F, OFF_C, OFF_G, REST_W = 0, 256, 512, 1024, 4096
F_LANES = 128

ADAM_LR = 0.001
ADAM_B1 = 0.9
ADAM_B2 = 0.999
ADAM_EPS = 1e-08
ADAM_WD = 0.01
ADAM_STEP = 10

VMEM_LIMIT = 56 * 1024 * 1024


def _tile(n, pref):
    t = min(n, pref)
    while n % t:
        t -= 128
    assert t > 0, (n, pref)
    return t


def _params(sem=None):
    return pltpu.CompilerParams(dimension_semantics=sem, vmem_limit_bytes=VMEM_LIMIT)


def _dot(a, b, ca, cb):
    return lax.dot_general(a, b, (((ca,), (cb,)), ((), ())), preferred_element_type=F32)


def _sigmoid(z):
    return 1.0 / (1.0 + jnp.exp(-z))


_GELU_K = math.sqrt(2.0 / math.pi)
_GELU_C = 0.044715


def _gelu(x):
    return 0.5 * x * (1.0 + jnp.tanh(_GELU_K * (x + _GELU_C * x * x * x)))


def _gelu_grad(x):
    t = jnp.tanh(_GELU_K * (x + _GELU_C * x * x * x))
    return 0.5 * (1.0 + t) + 0.5 * x * (1.0 - t * t) * _GELU_K * (1.0 + 3.0 * _GELU_C * x * x)


def _rows(shape):
    return lax.broadcasted_iota(jnp.int32, shape, 0)


def _lanes(shape):
    return lax.broadcasted_iota(jnp.int32, shape, 1)


class _Gathered:
    def __init__(self, arr):
        self.arr = arr
        self.shape = (arr.shape[1], N_DEV * arr.shape[2])


_TOKEN = (8, 128)


def _mm(a, b, *, ta=False, tb=False, extras=(), row_extras=(), epilogue=None, out_dtypes=(F32,), row_outs=0, shard_out=False, after=None,
        tm=None, tn=512, tk=None, name):
    a_parts = list(a) if isinstance(a, (list, tuple)) else [a]
    b_parts = list(b) if isinstance(b, (list, tuple)) else [b]
    gathered = isinstance(b, _Gathered)
    assert (len(a_parts) == 1 or not ta) and (len(b_parts) == 1 or not tb) and min(len(a_parts), len(b_parts)) == 1
    a0, b0 = a_parts[0], b_parts[0]
    M, K = (a0.shape[1], a0.shape[0]) if ta else (a0.shape[0], a0.shape[1] * len(a_parts))
    N, Kb = b0.shape if tb else (b0.shape[1] * len(b_parts), b0.shape[0])
    assert Kb == K, (a0.shape, b0.shape, ta, tb)
    if gathered:
        if tb:
            tk = b.arr.shape[2]
        else:
            tn = b.arr.shape[2]
    if len(a_parts) > 1:
        tk = a0.shape[1]
    if shard_out:
        tn = N // N_DEV
    tm = _tile(M, tm or (1024 if ta else 2048))
    tn = _tile(b0.shape[1] if len(b_parts) > 1 else N, tn)
    per_piece = b0.shape[1] // tn
    size = lambda dt: jnp.dtype(dt).itemsize
    row_bytes = len(a_parts) * tm * size(a0.dtype) + len(b_parts) * tn * size(b.arr.dtype if gathered else b0.dtype)
    tile_bytes = tm * tn * (sum(size(e.dtype) for e in extras) + sum(map(size, out_dtypes)))

    def vmem_bytes(k_tile):
        return 2 * (k_tile * row_bytes + tile_bytes) + tm * tn * 4 * (K > k_tile)

    if tk is None:
        tk = next(c for c in (_tile(K, 2048), _tile(K, 1024), _tile(K, 512), _tile(K, 256)) if vmem_bytes(c) <= VMEM_LIMIT - (4 << 20))
    tk = _tile(K, tk)
    nk = K // tk
    ca, cb = (0 if ta else 1), (1 if tb else 0)
    n_a, n_b, n_ex, n_out = len(a_parts), len(b_parts), len(extras) + len(row_extras), len(out_dtypes)
    tokens = [] if after is None else [after]
    n_in = n_a + n_b + n_ex + len(tokens)
    if epilogue is None:
        epilogue = lambda acc: (acc,)

    def body(*refs):
        a_refs, b_refs = refs[:n_a], refs[n_a:n_a + n_b]
        ex_refs = refs[n_a + n_b:n_a + n_b + n_ex]
        o_refs = refs[n_in:n_in + n_out]
        j, k = pl.program_id(1), pl.program_id(2)

        def finish(acc):
            vals = epilogue(acc, *[e[...] for e in ex_refs])
            for o_ref, val in zip(o_refs[:n_out - row_outs], vals):
                o_ref[...] = val.astype(o_ref.dtype)
            for o_ref, val in zip(o_refs[n_out - row_outs:], vals[n_out - row_outs:]):
                first = pl.program_id(0) == 0
                o_ref[...] = jnp.where(first, val, o_ref[...] + val)

        def step(a_ref, b_ref):
            part = _dot(a_ref[...].astype(BF16), b_ref[...].astype(BF16), ca, cb)
            if nk == 1:
                finish(part)
            else:
                acc_ref = refs[-1]

                @pl.when(k == 0)
                def _():
                    acc_ref[...] = part

                @pl.when(k > 0)
                def _():
                    acc_ref[...] += part

                @pl.when(k == nk - 1)
                def _():
                    finish(acc_ref[...])

        if n_a > 1:
            for p in range(n_a):
                pl.when(k == p)(functools.partial(step, a_refs[p], b_refs[0]))
        elif n_b > 1:
            for p in range(n_b):
                pl.when(j // per_piece == p)(functools.partial(step, a_refs[0], b_refs[p]))
        else:
            step(a_refs[0], b_refs[0])

    if n_a > 1:
        a_specs = [pl.BlockSpec((tm, tk), lambda i, j, k: (i, 0))] * n_a
    else:
        a_specs = [pl.BlockSpec((tk, tm), lambda i, j, k: (k, i)) if ta else pl.BlockSpec((tm, tk), lambda i, j, k: (i, k))]
    if gathered:
        b_arrs = [b.arr]
        b_specs = [pl.BlockSpec((None, tn, tk), lambda i, j, k: (k, j, 0)) if tb else pl.BlockSpec((None, tk, tn), lambda i, j, k: (j, k, 0))]
    elif n_b > 1:
        b_arrs = b_parts
        b_specs = [pl.BlockSpec((tk, tn), functools.partial(lambda p, i, j, k: (k, jnp.clip(j - p * per_piece, 0, per_piece - 1)), p))
                   for p in range(n_b)]
    else:
        b_arrs = b_parts
        b_specs = [pl.BlockSpec((tn, tk), lambda i, j, k: (j, k)) if tb else pl.BlockSpec((tk, tn), lambda i, j, k: (k, j))]
    tile = pl.BlockSpec((tm, tn), lambda i, j, k: (i, j))
    if shard_out:
        out_specs = [pl.BlockSpec((None, tm, tn), lambda i, j, k: (j, i, 0))] * n_out
        out_shape = [jax.ShapeDtypeStruct((N_DEV, M, tn), dt) for dt in out_dtypes]
    else:
        assert row_outs == 0 or tn == N
        out_specs = [tile] * (n_out - row_outs) + [pl.BlockSpec((1, tn), lambda i, j, k: (0, j))] * row_outs
        out_shape = [jax.ShapeDtypeStruct((1, N) if t >= n_out - row_outs else (M, N), dt) for t, dt in enumerate(out_dtypes)]
    assert vmem_bytes(tk) <= VMEM_LIMIT - (4 << 20), (name, vmem_bytes(tk))
    outs = pl.pallas_call(
        body,
        name=name,
        grid=(M // tm, N // tn, nk),
        in_specs=a_specs + b_specs + [tile] * len(extras) + [pl.BlockSpec((1, tn), lambda i, j, k: (0, j))] * len(row_extras)
        + [pl.BlockSpec(_TOKEN, lambda i, j, k: (0, 0))] * len(tokens),
        out_specs=out_specs,
        out_shape=out_shape,
        scratch_shapes=[pltpu.VMEM((tm, tn), F32)] if nk > 1 else [],
        compiler_params=_params(("arbitrary",) * 3 if row_outs else ("parallel", "parallel", "arbitrary")),
    )(*a_parts, *b_arrs, *extras, *row_extras, *tokens)
    return outs[0] if n_out == 1 else outs


def _add(acc, res):
    return (acc + res,)


def _norm_grad(dh, x, dres, g):
    r = lax.rsqrt(jnp.mean(x * x, axis=-1, keepdims=True) + EPS)
    xn = x * r
    dxn = dh * g
    return r * (dxn - xn * jnp.mean(dxn * xn, axis=-1, keepdims=True)) + dres, jnp.sum(dh * xn, axis=0, keepdims=True)


def _add_norm_grad(acc, more, x, dres, g):
    return _norm_grad(acc + more, x, dres, g)


def _add_norm(acc, res, g):
    x = acc + res
    return x, x * lax.rsqrt(jnp.mean(x * x, axis=-1, keepdims=True) + EPS) * g


def _rms_fwd(x, g, *, after=None, name):
    R, C = x.shape
    tm = _tile(R, 256)
    tokens = [] if after is None else [after]

    def body(x_ref, g_ref, *rest):
        xv = x_ref[...]
        r = lax.rsqrt(jnp.mean(xv * xv, axis=-1, keepdims=True) + EPS)
        rest[-1][...] = (xv * r * g_ref[...]).astype(BF16)

    return pl.pallas_call(
        body,
        name=name,
        grid=(R // tm,),
        in_specs=[pl.BlockSpec((tm, C), lambda i: (i, 0)), pl.BlockSpec((1, C), lambda i: (0, 0))]
        + [pl.BlockSpec(_TOKEN, lambda i: (0, 0))] * len(tokens),
        out_specs=pl.BlockSpec((tm, C), lambda i: (i, 0)),
        out_shape=jax.ShapeDtypeStruct((R, C), BF16),
        compiler_params=_params(("parallel",)),
    )(x, g.reshape(1, C), *tokens)


def _rms_bwd(x, g, dh, dres, *, name):
    R, C = x.shape
    tm = _tile(R, 256)

    def body(x_ref, g_ref, dh_ref, dres_ref, dx_ref, dg_ref):
        xv = x_ref[...]
        r = lax.rsqrt(jnp.mean(xv * xv, axis=-1, keepdims=True) + EPS)
        xn = xv * r
        dh_v = dh_ref[...].astype(F32)
        dxn = dh_v * g_ref[...]
        dx_ref[...] = r * (dxn - xn * jnp.mean(dxn * xn, axis=-1, keepdims=True)) + dres_ref[...]
        part = jnp.sum(dh_v * xn, axis=0, keepdims=True)

        @pl.when(pl.program_id(0) == 0)
        def _():
            dg_ref[...] = part

        @pl.when(pl.program_id(0) > 0)
        def _():
            dg_ref[...] += part

    row = pl.BlockSpec((tm, C), lambda i: (i, 0))
    vec = pl.BlockSpec((1, C), lambda i: (0, 0))
    dx, dg = pl.pallas_call(
        body,
        name=name,
        grid=(R // tm,),
        in_specs=[row, vec, row, row],
        out_specs=[row, vec],
        out_shape=[jax.ShapeDtypeStruct((R, C), F32), jax.ShapeDtypeStruct((1, C), F32)],
        compiler_params=_params(("arbitrary",)),
    )(x, g.reshape(1, C), dh, dres)
    return dx, dg.reshape(C)


def _final_loss(x, g, target, *, name):
    R, C = x.shape
    tm = _tile(R, 256)

    def body(x_ref, g_ref, t_ref, loss_ref, dx_ref, dg_ref):
        xv = x_ref[...]
        r = lax.rsqrt(jnp.mean(xv * xv, axis=-1, keepdims=True) + EPS)
        xn = xv * r
        gv = g_ref[...]
        err = xn * gv - t_ref[...]
        lpart = (0.5 / C) * jnp.sum(jnp.sum(err * err, axis=1, keepdims=True), axis=0, keepdims=True)
        dy = err * (1.0 / C)
        dxn = dy * gv
        dx_ref[...] = r * (dxn - xn * jnp.mean(dxn * xn, axis=-1, keepdims=True))
        gpart = jnp.sum(dy * xn, axis=0, keepdims=True)

        @pl.when(pl.program_id(0) == 0)
        def _():
            loss_ref[...] = lpart
            dg_ref[...] = gpart

        @pl.when(pl.program_id(0) > 0)
        def _():
            loss_ref[...] += lpart
            dg_ref[...] += gpart

    row = pl.BlockSpec((tm, C), lambda i: (i, 0))
    vec = pl.BlockSpec((1, C), lambda i: (0, 0))
    loss, dx, dg = pl.pallas_call(
        body,
        name=name,
        grid=(R // tm,),
        in_specs=[row, vec, row],
        out_specs=[pl.BlockSpec((1, 1), lambda i: (0, 0)), row, vec],
        out_shape=[jax.ShapeDtypeStruct((1, 1), F32), jax.ShapeDtypeStruct((R, C), F32), jax.ShapeDtypeStruct((1, C), F32)],
        compiler_params=_params(("arbitrary",)),
    )(x, g.reshape(1, C), target)
    return loss, dx, dg.reshape(C)


def _pool_select(lane, vals):
    out = vals[3]
    for gi in (2, 1, 0):
        out = jnp.where(lane < 64 * (gi + 1), vals[gi], out)
    return out


def _pool_diff(a):
    row, lane = _rows(a.shape), _lanes(a.shape)

    def down(v, k):
        return jnp.where(row >= k, pltpu.roll(v, k, 0), 0.0)

    s2 = a + down(a, 1)
    s4 = s2 + down(s2, 2)
    s8 = s4 + down(s4, 4)
    s16 = s8 + down(s8, 8)
    wsum = _pool_select(lane, (s2, s4, s8, s16))
    win = _pool_select(lane, (2, 4, 8, 16))
    cnt = jnp.minimum(row + 1, win).astype(F32)
    return wsum / cnt - a, cnt


def _pool_diff_t(dd, cnt):
    S = dd.shape[0]
    row, lane = _rows(dd.shape), _lanes(dd.shape)

    def up(v, k):
        return jnp.where(row < S - k, pltpu.roll(v, S - k, 0), 0.0)

    e = dd / cnt
    s2 = e + up(e, 1)
    s4 = s2 + up(s2, 2)
    s8 = s4 + up(s4, 4)
    s16 = s8 + up(s8, 8)
    return _pool_select(lane, (s2, s4, s8, s16)) - dd


def _pool_fwd(rest, wbd, scale, *, name):
    S = rest.shape[0]

    def body(a_ref, w_ref, s_ref, o_ref):
        d, _ = _pool_diff(a_ref[...])
        yp = _dot(d.astype(BF16), w_ref[...], 1, 0)
        o_ref[...] = (yp * s_ref[...]).astype(BF16)

    return pl.pallas_call(
        body,
        name=name,
        grid=(1,),
        in_specs=[
            pl.BlockSpec((S, POOL_W), lambda i: (0, OFF_A // POOL_W)),
            pl.BlockSpec((POOL_W, POOL_W), lambda i: (0, 0)),
            pl.BlockSpec((1, POOL_W), lambda i: (0, 0)),
        ],
        out_specs=pl.BlockSpec((S, POOL_W), lambda i: (0, 0)),
        out_shape=jax.ShapeDtypeStruct((S, POOL_W), BF16),
        compiler_params=_params(("arbitrary",)),
    )(rest, wbd, scale.reshape(1, POOL_W))


def _pool_bwd(rest, wbd, wbd_t, scale, dpa, *, name):
    S = rest.shape[0]

    def body(a_ref, w_ref, wt_ref, s_ref, dpa_ref, da_ref, dw_ref, ds_ref):
        d, cnt = _pool_diff(a_ref[...])
        db = d.astype(BF16)
        yp = _dot(db, w_ref[...], 1, 0)
        dpa_v = dpa_ref[...]
        ds_ref[...] = jnp.sum(dpa_v * yp, axis=0, keepdims=True)
        dyp = (dpa_v * s_ref[...]).astype(BF16)
        dw_ref[...] = _dot(db, dyp, 0, 0)
        dd = _dot(dyp, wt_ref[...], 1, 0)
        da_ref[...] = _pool_diff_t(dd, cnt).astype(BF16)

    full = pl.BlockSpec((S, POOL_W), lambda i: (0, 0))
    sq = pl.BlockSpec((POOL_W, POOL_W), lambda i: (0, 0))
    vec = pl.BlockSpec((1, POOL_W), lambda i: (0, 0))
    return pl.pallas_call(
        body,
        name=name,
        grid=(1,),
        in_specs=[pl.BlockSpec((S, POOL_W), lambda i: (0, OFF_A // POOL_W)), sq, sq, vec, full],
        out_specs=[full, sq, vec],
        out_shape=[
            jax.ShapeDtypeStruct((S, POOL_W), BF16),
            jax.ShapeDtypeStruct((POOL_W, POOL_W), F32),
            jax.ShapeDtypeStruct((1, POOL_W), F32),
        ],
        compiler_params=_params(("arbitrary",)),
    )(rest, wbd, wbd_t, scale.reshape(1, POOL_W), dpa)


def _log_sigmoid(z):
    return jnp.minimum(z, 0.0) - jnp.log(1.0 + jnp.exp(-jnp.abs(z)))


_F_SPEC_COL = OFF_F // F_LANES


def _fox_prep(rest, bpad, *, name):
    S = rest.shape[0]

    def body(f_ref, b_ref, o_ref, ot_ref):
        acc = _log_sigmoid(f_ref[...] + b_ref[...])
        row = _rows(acc.shape)
        k = 1
        while k < S:
            acc = acc + jnp.where(row >= k, pltpu.roll(acc, k, 0), 0.0)
            k *= 2
        o_ref[...] = acc
        ot_ref[...] = acc.T

    return pl.pallas_call(
        body,
        name=name,
        grid=(1,),
        in_specs=[pl.BlockSpec((S, F_LANES), lambda i: (0, _F_SPEC_COL)), pl.BlockSpec((1, F_LANES), lambda i: (0, 0))],
        out_specs=[pl.BlockSpec((S, F_LANES), lambda i: (0, 0)), pl.BlockSpec((F_LANES, S), lambda i: (0, 0))],
        out_shape=[jax.ShapeDtypeStruct((S, F_LANES), F32), jax.ShapeDtypeStruct((F_LANES, S), F32)],
        compiler_params=_params(("arbitrary",)),
    )(rest, bpad)


def _fox_post(rest, bpad, dcum, *, name):
    S = rest.shape[0]

    def body(f_ref, b_ref, d_ref, df_ref, db_ref):
        acc = d_ref[...]
        row = _rows(acc.shape)
        k = 1
        while k < S:
            acc = acc + jnp.where(row < S - k, pltpu.roll(acc, S - k, 0), 0.0)
            k *= 2
        df = acc * (1.0 - _sigmoid(f_ref[...] + b_ref[...]))
        df_ref[...] = df.astype(BF16)
        db_ref[...] = jnp.sum(df, axis=0, keepdims=True)

    full = pl.BlockSpec((S, F_LANES), lambda i: (0, 0))
    vec = pl.BlockSpec((1, F_LANES), lambda i: (0, 0))
    return pl.pallas_call(
        body,
        name=name,
        grid=(1,),
        in_specs=[pl.BlockSpec((S, F_LANES), lambda i: (0, _F_SPEC_COL)), vec, full],
        out_specs=[full, vec],
        out_shape=[jax.ShapeDtypeStruct((S, F_LANES), BF16), jax.ShapeDtypeStruct((1, F_LANES), F32)],
        compiler_params=_params(("arbitrary",)),
    )(rest, bpad, dcum)


_FOX_SCALE = FOX_DH ** -0.5
_PAIRS = FOX_H // 2


def _scaled(v):
    return (v.astype(F32) * _FOX_SCALE).astype(BF16)


def _diag_mask(s):
    return jnp.where(_rows(s.shape) >= _lanes(s.shape), s, NEG)


def _fox_fwd(qkv, cum, fk3, *, name):
    S = qkv.shape[0]
    nk, t = fk3.shape[1:]

    def body(q_ref, k_ref, v_ref, cum_ref, fk_ref, o_ref, lse_ref):
        i = pl.program_id(0)
        lane = _lanes((t, 128))
        lo = lane < FOX_DH
        cumv = cum_ref[...]
        qm, fq = [], []
        for h in range(FOX_H):
            qs = _scaled(q_ref[:, 128 * (h // 2):128 * (h // 2 + 1)])
            zero = jnp.zeros_like(qs)
            qm.append(jnp.where(lo, qs, zero) if h % 2 == 0 else jnp.where(lo, zero, qs))
            fq.append(jnp.broadcast_to(cumv[:, h:h + 1], (t, 128)))

        def tile(j, state, masked):
            m, acc, lsum = (list(part) for part in state)
            k0 = pl.multiple_of(j * t, t)
            for hp in range(_PAIRS):
                cols = slice(128 * hp, 128 * (hp + 1))
                kb = k_ref[pl.ds(k0, t), cols]
                vb = v_ref[pl.ds(k0, t), cols]
                one = jnp.ones_like(vb)
                alphas, pvs = [], []
                for h in (2 * hp, 2 * hp + 1):
                    s = _dot(qm[h], kb, 1, 1) + jnp.concatenate([fq[h]] * (t // 128), axis=1) - fk_ref[h, pl.ds(j, 1), :]
                    if masked:
                        s = _diag_mask(s)
                    m_new = jnp.maximum(m[h], jnp.max(s, axis=-1, keepdims=True))
                    p = jnp.exp(s - m_new)
                    alphas.append(jnp.exp(m[h] - m_new))
                    m[h] = m_new
                    pvs.append(_dot(p.astype(BF16), jnp.where(lo, vb, one) if h % 2 == 0 else jnp.where(lo, one, vb), 1, 0))
                acc[hp] = jnp.where(lo, alphas[0], alphas[1]) * acc[hp] + jnp.where(lo, pvs[0], pvs[1])
                lsum[hp] = jnp.where(lo, alphas[1], alphas[0]) * lsum[hp] + jnp.where(lo, pvs[1], pvs[0])
            return tuple(m), tuple(acc), tuple(lsum)

        zeros = (jnp.zeros((t, 128), F32),) * _PAIRS
        state = lax.fori_loop(0, i, functools.partial(tile, masked=False), ((jnp.full((t, 1), NEG, F32),) * FOX_H, zeros, zeros))
        m, acc, lsum = tile(i, state, True)
        for hp in range(_PAIRS):
            o_ref[:, 128 * hp:128 * (hp + 1)] = acc[hp] / pltpu.roll(lsum[hp], FOX_DH, 1)
            lse = [m[2 * hp] + jnp.log(lsum[hp][:, FOX_DH:FOX_DH + 1]), m[2 * hp + 1] + jnp.log(lsum[hp][:, 0:1])]
            lse_ref[hp] = jnp.where(lane == 0, lse[0], jnp.where(lane == 1, lse[1], 0.0))

    whole = lambda col: pl.BlockSpec((S, FOX_W), lambda i: (0, col))
    return pl.pallas_call(
        body,
        name=name,
        grid=(S // t,),
        in_specs=[
            pl.BlockSpec((t, FOX_W), lambda i: (i, 0)), whole(1), whole(2),
            pl.BlockSpec((t, F_LANES), lambda i: (i, 0)),
            pl.BlockSpec((FOX_H, nk, t), lambda i: (0, 0, 0)),
        ],
        out_specs=[pl.BlockSpec((t, FOX_W), lambda i: (i, 0)), pl.BlockSpec((_PAIRS, t, 128), lambda i: (0, i, 0))],
        out_shape=[jax.ShapeDtypeStruct((S, FOX_W), F32), jax.ShapeDtypeStruct((_PAIRS, S, 128), F32)],
        compiler_params=_params(("arbitrary",)),
    )(qkv, qkv, qkv, cum, fk3)


def _fox_bwd(qkv, cum, fk3, o, do, lse, *, name):
    S = qkv.shape[0]
    nk, t = fk3.shape[1:]
    q_at, k_at, v_at = 0, FOX_W, 2 * FOX_W

    def body(qkv_ref, cum_ref, fk_ref, o_ref, do_ref, lse_ref, dq_ref, dk_ref, dv_ref, dfq_ref, dfk_ref,
             qs_sc, ks_sc, bias_sc, delta_sc, dq_sc):
        lane = _lanes((t, 128))
        lo = lane < FOX_DH
        mine = lambda h: lo if h % 2 == 0 else jnp.logical_not(lo)

        def by_head(tile, values):
            for h, val in enumerate(values):
                tile = jnp.where(lane == h, val, tile)
            return tile

        def prep(i, carry):
            r = pl.ds(pl.multiple_of(i * t, t), t)
            qs_sc[r, :] = _scaled(qkv_ref[r, q_at:q_at + FOX_W])
            ks_sc[r, :] = _scaled(qkv_ref[r, k_at:k_at + FOX_W])
            cum_t = cum_ref[r, :]
            for hp in range(_PAIRS):
                cols = slice(128 * hp, 128 * (hp + 1))
                prod = do_ref[r, cols].astype(F32) * o_ref[r, cols]
                for h in (2 * hp, 2 * hp + 1):
                    delta = jnp.sum(jnp.where(mine(h), prod, 0.0), axis=-1, keepdims=True)
                    delta_sc[h, r, :] = jnp.broadcast_to(delta, (t, 128))
                    bias_sc[h, r, :] = jnp.broadcast_to(cum_t[:, h:h + 1] - lse_ref[hp, r, h % 2:h % 2 + 1], (t, 128))
            dfq_ref[r, :] = jnp.zeros((t, 128), F32)
            dq_sc[r, :] = jnp.zeros((t, FOX_W), F32)
            return carry

        lax.fori_loop(0, nk, prep, 0)

        def kv_tile(j, carry):
            kr = pl.ds(pl.multiple_of(j * t, t), t)

            def q_tile(i, acc, masked):
                dk, dv, dfk = list(acc[:_PAIRS]), list(acc[_PAIRS:2 * _PAIRS]), list(acc[2 * _PAIRS:])
                qr = pl.ds(pl.multiple_of(i * t, t), t)
                dq_old, dfq_old = dq_sc[qr, :], dfq_ref[qr, :]
                wide = lambda a: jnp.concatenate([a] * (t // 128), axis=1)
                row_sums, dq_new = [], []
                for hp in range(_PAIRS):
                    cols = slice(128 * hp, 128 * (hp + 1))
                    kb = qkv_ref[kr, k_at + 128 * hp:k_at + 128 * (hp + 1)]
                    vb = qkv_ref[kr, v_at + 128 * hp:v_at + 128 * (hp + 1)]
                    ksb, qsb, dob = ks_sc[kr, cols], qs_sc[qr, cols], do_ref[qr, cols]
                    zero = jnp.zeros_like(qsb)
                    dq_t = jnp.zeros((t, 128), F32)
                    for h in (2 * hp, 2 * hp + 1):
                        qe, doe, ke = (jnp.where(mine(h), a, zero) for a in (qsb, dob, ksb))
                        s = _dot(qe, kb, 1, 1) + wide(bias_sc[h, qr, :]) - fk_ref[h, pl.ds(j, 1), :]
                        if masked:
                            s = _diag_mask(s)
                        p = jnp.exp(s)
                        dv[hp] = dv[hp] + _dot(p.astype(BF16), doe, 0, 0)
                        dp = _dot(doe, vb, 1, 1)
                        ds = p * (dp - wide(delta_sc[h, qr, :]))
                        dsb = ds.astype(BF16)
                        dk[hp] = dk[hp] + _dot(dsb, qe, 0, 0)
                        dq_t = dq_t + _dot(dsb, ke, 1, 0)
                        row_sums.append(jnp.sum(ds, axis=-1, keepdims=True))
                        dfk[h] = dfk[h] - jnp.sum(ds, axis=0, keepdims=True)
                    dq_new.append(dq_old[:, cols] + dq_t)
                for hp in range(_PAIRS):
                    dq_sc[qr, 128 * hp:128 * (hp + 1)] = dq_new[hp]
                dfq_ref[qr, :] = dfq_old + by_head(jnp.zeros((t, 128), F32), row_sums)
                return (*dk, *dv, *dfk)

            init = tuple([jnp.zeros((t, 128), F32)] * (2 * _PAIRS) + [jnp.zeros((1, t), F32)] * FOX_H)
            acc = q_tile(j, init, True)
            acc = lax.fori_loop(j + 1, nk, functools.partial(q_tile, masked=False), acc)
            for hp in range(_PAIRS):
                cols = slice(128 * hp, 128 * (hp + 1))
                dk_ref[kr, cols] = acc[hp].astype(BF16)
                dv_ref[kr, cols] = acc[_PAIRS + hp].astype(BF16)
            for h in range(FOX_H):
                dfk_ref[h, pl.ds(j, 1), :] = acc[2 * _PAIRS + h]
            return carry

        lax.fori_loop(0, nk, kv_tile, 0)
        dq_ref[...] = dq_sc[...].astype(BF16)

    vm = pl.BlockSpec(memory_space=pltpu.VMEM)
    big = jax.ShapeDtypeStruct((S, FOX_W), BF16)
    return pl.pallas_call(
        body,
        name=name,
        in_specs=[vm] * 6,
        out_specs=[vm] * 5,
        out_shape=[big, big, big, jax.ShapeDtypeStruct((S, 128), F32), jax.ShapeDtypeStruct((FOX_H, nk, t), F32)],
        scratch_shapes=[pltpu.VMEM((S, FOX_W), BF16), pltpu.VMEM((S, FOX_W), BF16), pltpu.VMEM((FOX_H, S, 128), F32),
                        pltpu.VMEM((FOX_H, S, 128), F32), pltpu.VMEM((S, FOX_W), F32)],
        compiler_params=pltpu.CompilerParams(vmem_limit_bytes=VMEM_LIMIT),
    )(qkv, cum, fk3, o, do, lse)


def _group_mask(lane, gi):
    return (lane >= 64 * gi) & (lane < 64 * (gi + 1))


_U_COL = OFF_C // SGU_W


def _sgu_fwd(rest, gn, wm, bias, *, name):
    S = rest.shape[0]
    ts = _tile(S, 512)
    nc = ts // SGU_CHUNK

    def body(u_ref, v_ref, g_ref, w_ref, b_ref, o_ref):
        zv = _gelu(v_ref[...])
        vn = zv * lax.rsqrt(jnp.mean(zv * zv, axis=-1, keepdims=True) + EPS) * g_ref[...]
        lane = _lanes((SGU_CHUNK, SGU_W))
        for c in range(nc):
            rows = slice(c * SGU_CHUNK, (c + 1) * SGU_CHUNK)
            vcb = vn[rows].astype(BF16)
            mixed = b_ref[...]
            for gi in range(4):
                mixed = mixed + jnp.where(_group_mask(lane, gi), _dot(w_ref[gi], vcb, 1, 0), 0.0)
            o_ref[rows, :] = (_gelu(u_ref[rows, :]) * mixed).astype(BF16)

    return pl.pallas_call(
        body,
        name=name,
        grid=(S // ts,),
        in_specs=[
            pl.BlockSpec((ts, SGU_W), lambda i: (i, _U_COL)),
            pl.BlockSpec((ts, SGU_W), lambda i: (i, _U_COL + 1)),
            pl.BlockSpec((1, SGU_W), lambda i: (0, 0)),
            pl.BlockSpec((4, SGU_CHUNK, SGU_CHUNK), lambda i: (0, 0, 0)),
            pl.BlockSpec((SGU_CHUNK, SGU_W), lambda i: (0, 0)),
        ],
        out_specs=pl.BlockSpec((ts, SGU_W), lambda i: (i, 0)),
        out_shape=jax.ShapeDtypeStruct((S, SGU_W), BF16),
        compiler_params=_params(("parallel",)),
    )(rest, rest, gn.reshape(1, SGU_W), wm, bias)


def _sgu_bwd(rest, gn, wm, wm_t, bias, dsg, *, name):
    S = rest.shape[0]
    ts = _tile(S, 512)
    nc = ts // SGU_CHUNK

    def body(u_ref, v_ref, g_ref, w_ref, wt_ref, b_ref, dsg_ref, dc_ref, dw_ref, db_ref, dg_ref):
        first = pl.program_id(0) == 0

        @pl.when(first)
        def _():
            dw_ref[...] = jnp.zeros_like(dw_ref)
            db_ref[...] = jnp.zeros_like(db_ref)
            dg_ref[...] = jnp.zeros_like(dg_ref)

        gv = g_ref[...]
        lane = _lanes((SGU_CHUNK, SGU_W))
        for c in range(nc):
            rows = slice(c * SGU_CHUNK, (c + 1) * SGU_CHUNK)
            vpre = v_ref[rows, :]
            upre = u_ref[rows, :]
            zv = _gelu(vpre)
            r = lax.rsqrt(jnp.mean(zv * zv, axis=-1, keepdims=True) + EPS)
            zn = zv * r
            vcb = (zn * gv).astype(BF16)
            mixed = b_ref[...]
            for gi in range(4):
                mixed = mixed + jnp.where(_group_mask(lane, gi), _dot(w_ref[gi], vcb, 1, 0), 0.0)
            zu = _gelu(upre)
            dsg_v = dsg_ref[rows, :]
            dc_ref[rows, :SGU_W] = (dsg_v * mixed * _gelu_grad(upre)).astype(BF16)
            dmixed = dsg_v * zu
            db_ref[...] += dmixed
            dvn = jnp.zeros((SGU_CHUNK, SGU_W), F32)
            for gi in range(4):
                dmg = jnp.where(_group_mask(lane, gi), dmixed, 0.0).astype(BF16)
                dw_ref[gi] += _dot(dmg, vcb, 1, 1)
                dvn = dvn + _dot(wt_ref[gi], dmg, 1, 0)
            dg_ref[...] += jnp.sum(dvn * zn, axis=0, keepdims=True)
            dzn = dvn * gv
            dzv = r * (dzn - zn * jnp.mean(dzn * zn, axis=-1, keepdims=True))
            dc_ref[rows, SGU_W:] = (dzv * _gelu_grad(vpre)).astype(BF16)

    blk = pl.BlockSpec((ts, SGU_W), lambda i: (i, 0))
    vec = pl.BlockSpec((1, SGU_W), lambda i: (0, 0))
    w3 = pl.BlockSpec((4, SGU_CHUNK, SGU_CHUNK), lambda i: (0, 0, 0))
    bsp = pl.BlockSpec((SGU_CHUNK, SGU_W), lambda i: (0, 0))
    return pl.pallas_call(
        body,
        name=name,
        grid=(S // ts,),
        in_specs=[
            pl.BlockSpec((ts, SGU_W), lambda i: (i, _U_COL)),
            pl.BlockSpec((ts, SGU_W), lambda i: (i, _U_COL + 1)),
            vec, w3, w3, bsp, blk,
        ],
        out_specs=[pl.BlockSpec((ts, 2 * SGU_W), lambda i: (i, 0)), w3, bsp, vec],
        out_shape=[
            jax.ShapeDtypeStruct((S, 2 * SGU_W), BF16),
            jax.ShapeDtypeStruct((4, SGU_CHUNK, SGU_CHUNK), F32),
            jax.ShapeDtypeStruct((SGU_CHUNK, SGU_W), F32),
            jax.ShapeDtypeStruct((1, SGU_W), F32),
        ],
        compiler_params=_params(("arbitrary",)),
    )(rest, rest, gn.reshape(1, SGU_W), wm, wm_t, bias, dsg)


_GT = 512
_G0 = OFF_G // _GT


def _gate_specs(tm, col_of):
    specs = [pl.BlockSpec((tm, _GT), functools.partial(lambda k, *ids: (col_of(*ids)[0], _G0 + 2 * k + col_of(*ids)[1]), k)) for k in range(3)]
    specs += [pl.BlockSpec((1, _GT), functools.partial(lambda k, *ids: (0, 2 * k + col_of(*ids)[1]), k)) for k in range(3)]
    return specs


def _merge_fwd(rest, bg, ya, yb, yc, *, name):
    S = rest.shape[0]
    tm = _tile(S, 512)

    def body(g1, g2, g3, b1, b2, b3, ya_ref, yb_ref, yc_ref, o_ref):
        acc = _sigmoid(g1[...] + b1[...]) * ya_ref[...]
        acc = acc + _sigmoid(g2[...] + b2[...]) * yb_ref[...]
        acc = acc + _sigmoid(g3[...] + b3[...]) * yc_ref[...]
        o_ref[...] = acc.astype(BF16)

    blk = pl.BlockSpec((tm, _GT), lambda i, j: (i, j))
    return pl.pallas_call(
        body,
        name=name,
        grid=(S // tm, D // _GT),
        in_specs=_gate_specs(tm, lambda i, j: (i, j)) + [blk, blk, blk],
        out_specs=blk,
        out_shape=jax.ShapeDtypeStruct((S, D), BF16),
        compiler_params=_params(("parallel", "parallel")),
    )(rest, rest, rest, bg, bg, bg, ya, yb, yc)


def _merge_bwd(rest, bg, ya, yb, yc, dm, *, name):
    S = rest.shape[0]
    tm = _tile(S, 512)

    def body(g1, g2, g3, b1, b2, b3, ya_ref, yb_ref, yc_ref, dm_ref, dya, dyb, dyc, dg1, dg2, dg3, db1, db2, db3):
        first = pl.program_id(1) == 0
        dmv = dm_ref[...]
        for g_ref, b_ref, y_ref, dy_ref, dg_ref, db_ref in (
            (g1, b1, ya_ref, dya, dg1, db1), (g2, b2, yb_ref, dyb, dg2, db2), (g3, b3, yc_ref, dyc, dg3, db3)):
            gate = _sigmoid(g_ref[...] + b_ref[...])
            dy_ref[...] = (dmv * gate).astype(BF16)
            dpre = dmv * y_ref[...] * gate * (1.0 - gate)
            dg_ref[...] = dpre.astype(BF16)
            part = jnp.sum(dpre, axis=0, keepdims=True)

            @pl.when(first)
            def _():
                db_ref[...] = part

            @pl.when(jnp.logical_not(first))
            def _():
                db_ref[...] += part

    blk = pl.BlockSpec((tm, _GT), lambda j, i: (i, j))
    vec = pl.BlockSpec((1, _GT), lambda j, i: (0, j))
    big = jax.ShapeDtypeStruct((S, D), BF16)
    small = jax.ShapeDtypeStruct((1, D), F32)
    return pl.pallas_call(
        body,
        name=name,
        grid=(D // _GT, S // tm),
        in_specs=_gate_specs(tm, lambda j, i: (i, j)) + [blk, blk, blk, blk],
        out_specs=[blk] * 6 + [vec] * 3,
        out_shape=[big] * 6 + [small] * 3,
        compiler_params=_params(("parallel", "arbitrary")),
    )(rest, rest, rest, bg, bg, bg, ya, yb, yc, dm)


_X_SCALE = XDH ** -0.5


def _xattn_fwd(xq, kv, *, name):
    S = xq.shape[0]
    M = kv.shape[0]
    tq = _tile(S, 512)

    def body(q_ref, k_ref, v_ref, o_ref):
        s = _dot(q_ref[...], k_ref[...], 1, 1) * _X_SCALE
        e = jnp.exp(s - jnp.max(s, axis=-1, keepdims=True))
        p = e / jnp.sum(e, axis=-1, keepdims=True)
        o_ref[...] = _dot(p.astype(BF16), v_ref[...], 1, 0).astype(BF16)

    return pl.pallas_call(
        body,
        name=name,
        grid=(S // tq, XH),
        in_specs=[
            pl.BlockSpec((tq, XDH), lambda i, h: (i, h)),
            pl.BlockSpec((M, XDH), lambda i, h: (0, h)),
            pl.BlockSpec((M, XDH), lambda i, h: (0, XH + h)),
        ],
        out_specs=pl.BlockSpec((tq, XDH), lambda i, h: (i, h)),
        out_shape=jax.ShapeDtypeStruct((S, D), BF16),
        compiler_params=_params(("parallel", "parallel")),
    )(xq, kv, kv)


def _xattn_bwd(xq, kv, do, *, name):
    S = xq.shape[0]
    M = kv.shape[0]
    tq = _tile(S, 512)

    def body(q_ref, k_ref, v_ref, do_ref, dq_ref, dk_ref, dv_ref):
        qb = q_ref[...]
        kb = k_ref[...]
        dob = do_ref[...]
        s = _dot(qb, kb, 1, 1) * _X_SCALE
        e = jnp.exp(s - jnp.max(s, axis=-1, keepdims=True))
        p = e / jnp.sum(e, axis=-1, keepdims=True)
        dp = _dot(dob, v_ref[...], 1, 1)
        ds = (p * (dp - jnp.sum(p * dp, axis=-1, keepdims=True)) * _X_SCALE).astype(BF16)
        dq_ref[...] = _dot(ds, kb, 1, 0).astype(BF16)
        dk_part = _dot(ds, qb, 0, 0)
        dv_part = _dot(p.astype(BF16), dob, 0, 0)

        @pl.when(pl.program_id(1) == 0)
        def _():
            dk_ref[...] = dk_part
            dv_ref[...] = dv_part

        @pl.when(pl.program_id(1) > 0)
        def _():
            dk_ref[...] += dk_part
            dv_ref[...] += dv_part

    qspec = pl.BlockSpec((tq, XDH), lambda h, i: (i, h))
    kspec = pl.BlockSpec((M, XDH), lambda h, i: (0, h))
    dxq, dxk, dxv = pl.pallas_call(
        body,
        name=name,
        grid=(XH, S // tq),
        in_specs=[qspec, kspec, pl.BlockSpec((M, XDH), lambda h, i: (0, XH + h)), qspec],
        out_specs=[qspec, kspec, kspec],
        out_shape=[jax.ShapeDtypeStruct((S, D), BF16), jax.ShapeDtypeStruct((M, D), F32), jax.ShapeDtypeStruct((M, D), F32)],
        compiler_params=_params(("parallel", "arbitrary")),
    )(xq, kv, kv, do)
    return dxq, jnp.concatenate([dxk, dxv], axis=1)


def _adam_math(w, g, m, v):
    m = ADAM_B1 * m + (1.0 - ADAM_B1) * g
    v = ADAM_B2 * v + (1.0 - ADAM_B2) * (g * g)
    m_hat = m / (1.0 - ADAM_B1 ** ADAM_STEP)
    v_hat = v / (1.0 - ADAM_B2 ** ADAM_STEP)
    delta = -ADAM_LR * (m_hat / (jnp.sqrt(v_hat) + ADAM_EPS) + ADAM_WD * w)
    return delta, m, v


def _adamw_sharded(parts, w, m, v, *, name):
    _, R, C = w.shape
    Cp = parts[0].shape[2]
    tm = _tile(R, 256)
    nr = R // tm

    def body(p0_ref, p1_ref, w_ref, m_ref, v_ref, g_ref, d_ref, mo_ref, vo_ref):
        def update(p_ref):
            g = p_ref[0][:, :C].astype(F32)
            for dev in range(1, N_DEV):
                g = g + p_ref[dev][:, :C].astype(F32)
            delta, mn, vn = _adam_math(w_ref[...], g, m_ref[...], v_ref[...])
            g_ref[...] = g
            d_ref[...] = delta
            mo_ref[...] = mn
            vo_ref[...] = vn

        @pl.when(pl.program_id(0) == 0)
        def _():
            update(p0_ref)

        @pl.when(pl.program_id(0) == 1)
        def _():
            update(p1_ref)

    p0 = pl.BlockSpec((N_DEV, tm, Cp), lambda l, i: (0, i * (1 - l) + (nr - 1) * l, 0))
    p1 = pl.BlockSpec((N_DEV, tm, Cp), lambda l, i: (0, i * l, 0))
    blk = pl.BlockSpec((None, tm, C), lambda l, i: (l, i, 0))
    sds = jax.ShapeDtypeStruct(w.shape, F32)
    return pl.pallas_call(
        body,
        name=name,
        grid=(DEPTH, nr),
        in_specs=[p0, p1, blk, blk, blk],
        out_specs=[blk] * 4,
        out_shape=[sds] * 4,
        compiler_params=_params(("arbitrary", "arbitrary")),
    )(parts[0], parts[1], w, m, v)


def _adamw_small(g, w, m, v, *, name):
    n = len(g)

    def body(*refs):
        g_refs, w_refs, m_refs, v_refs = (refs[k * n:(k + 1) * n] for k in range(4))
        d_out, m_out, v_out = (refs[(4 + k) * n:(5 + k) * n] for k in range(3))
        for t in range(n):
            delta, mn, vn = _adam_math(w_refs[t][...], g_refs[t][...], m_refs[t][...], v_refs[t][...])
            d_out[t][...] = delta
            m_out[t][...] = mn
            v_out[t][...] = vn

    vm = pl.BlockSpec(memory_space=pltpu.VMEM)
    shapes = [jax.ShapeDtypeStruct(a.shape, F32) for a in w]
    outs = pl.pallas_call(
        body,
        name=name,
        in_specs=[vm] * (4 * n),
        out_specs=[vm] * (3 * n),
        out_shape=shapes * 3,
        compiler_params=pltpu.CompilerParams(vmem_limit_bytes=VMEM_LIMIT),
    )(*g, *w, *m, *v)
    return outs[:n], outs[n:2 * n], outs[2 * n:]


def _position():
    return lax.axis_index("x"), lax.axis_index("y"), lax.axis_index("c")


def _dev_index(px, py, pc):
    return 4 * px + 2 * py + pc


_ANY = pl.BlockSpec(memory_space=pl.ANY)


def _peers(x, y, c):
    out = []
    for mask in range(1, N_DEV):
        fx, fy, fc = (mask >> 2) & 1, (mask >> 1) & 1, mask & 1
        out.append((1 - x if fx else x, 1 - y if fy else y, 1 - c if fc else c))
    return out


_HBM = pl.BlockSpec(memory_space=pltpu.HBM)
_SEM = pl.BlockSpec(memory_space=pltpu.SEMAPHORE)


def _own_block_placed(block, like):
    x, y, c = _position()
    return lax.dynamic_update_index_in_dim(lax.empty(like.shape, like.dtype), block, _dev_index(x, y, c), 0)


_COPY_BYTES = 256 << 10
_MAX_PIECES = 8


def _pieces(blocks):
    out = []
    for t, b in enumerate(blocks):
        R, C = b.shape[-2:]
        n = max(1, min(_MAX_PIECES, R * C * jnp.dtype(b.dtype).itemsize // _COPY_BYTES))
        while n > 1 and R % (16 * n):
            n -= 1
        out += [(t, pl.ds(j * (R // n), R // n) if n > 1 else None) for j in range(n)]
    return out


def _cut(block, rows):
    return block if rows is None else block.at[rows]


def _copies(per_piece):
    def mark(fn):
        fn.per_piece = per_piece
        return fn
    return mark


@_copies(N_DEV - 1)
def _plan_exchange(srcs, lands, send_sems, recv_sems, arrivals):
    x, y, c = _position()
    me = _dev_index(x, y, c)
    out = []
    for k, peer in enumerate(_peers(x, y, c)):
        p = _dev_index(*peer)
        for i, (t, rows) in enumerate(_pieces(lands)):
            sems = dict(send_sem=send_sems.at[7 * i + k], recv_sem=recv_sems.at[7 * i + k], device_id=peer, device_id_type=MESH)
            src, dst = (lands[t].at[p], lands[t].at[p]) if arrivals else (srcs[t].at[p], lands[t].at[me])
            out.append(pltpu.make_async_remote_copy(src_ref=_cut(src, rows), dst_ref=_cut(dst, rows), **sems))
    return out


@_copies(N_DEV - 1)
def _plan_broadcast(srcs, lands, send_sems, recv_sems, arrivals):
    x, y, c = _position()
    me = _dev_index(x, y, c)
    out = []
    for k, peer in enumerate(_peers(x, y, c)):
        p = _dev_index(*peer)
        for i, (t, rows) in enumerate(_pieces(lands)):
            sems = dict(send_sem=send_sems.at[7 * i + k], recv_sem=recv_sems.at[7 * i + k], device_id=peer, device_id_type=MESH)
            src, dst = (lands[t].at[p], lands[t].at[p]) if arrivals else (srcs[t], lands[t].at[me])
            out.append(pltpu.make_async_remote_copy(src_ref=_cut(src, rows), dst_ref=_cut(dst, rows), **sems))
    return out


@_copies(4)
def _plan_gather_out(srcs, lands, send_sems, recv_sems, arrivals):
    x, y, c = _position()
    me = _dev_index(x, y, c)
    out = []
    for k, peer in enumerate([(x, y, 1 - c), (1 - x, y, c), (x, 1 - y, c), (1 - x, 1 - y, c)]):
        p = _dev_index(*peer)
        for i, (t, rows) in enumerate(_pieces(lands)):
            sems = dict(send_sem=send_sems.at[4 * i + k], recv_sem=recv_sems.at[4 * i + k], device_id=peer, device_id_type=MESH)
            src, dst = (lands[t].at[p], lands[t].at[p]) if arrivals else (srcs[t], lands[t].at[me])
            out.append(pltpu.make_async_remote_copy(src_ref=_cut(src, rows), dst_ref=_cut(dst, rows), **sems))
    return out


@_copies(3)
def _plan_gather_pass(srcs, lands, send_sems, recv_sems, arrivals):
    x, y, c = _position()
    sibling = (x, y, 1 - c)
    out = []
    for k, chip in enumerate([(1 - x, y), (x, 1 - y), (1 - x, 1 - y)]):
        p = _dev_index(*chip, 1 - c) if arrivals else _dev_index(*chip, c)
        for i, (t, rows) in enumerate(_pieces(lands)):
            sems = dict(send_sem=send_sems.at[3 * i + k], recv_sem=recv_sems.at[3 * i + k], device_id=sibling, device_id_type=MESH)
            block = _cut(lands[t].at[p], rows)
            out.append(pltpu.make_async_remote_copy(src_ref=block, dst_ref=block, **sems))
    return out


def _split_start(plan, srcs, lands, *, after=None, name):
    n_src, n = len(srcs), len(srcs) + len(lands)
    n_sem = plan.per_piece * len(_pieces(lands))
    order = [] if after is None else [after]

    def body(*refs):
        send_sems, recv_sems = refs[n + len(order):n + len(order) + 2]
        token = refs[-1]
        for cp in plan(refs[:n_src], refs[n_src:n], send_sems, recv_sems, arrivals=False):
            cp.start()
        token[...] = jnp.zeros_like(token)

    hbm = lambda a: pltpu.HBM(a.shape, a.dtype)
    outs = pl.pallas_call(
        body,
        name=name,
        in_specs=[_HBM] * n + [_ANY] * len(order),
        out_specs=[_SEM, _SEM] + [_HBM] * n + [pl.BlockSpec(memory_space=pltpu.VMEM)],
        out_shape=[pltpu.SemaphoreType.DMA((n_sem,)), pltpu.SemaphoreType.DMA((n_sem,))] + [hbm(a) for a in (*srcs, *lands)]
        + [jax.ShapeDtypeStruct(_TOKEN, F32)],
        input_output_aliases={i: 2 + i for i in range(n)},
        compiler_params=pltpu.CompilerParams(has_side_effects=pltpu.SideEffectType.DATAFLOW_SIDE_EFFECTING),
    )(*[pltpu.with_memory_space_constraint(a, pltpu.HBM) for a in (*srcs, *lands)], *order)
    return (outs[0], outs[1], outs[2:2 + n_src], outs[2 + n_src:2 + n]), outs[-1]


def _split_wait(plan, state, after, *, name):
    send_sems, recv_sems, srcs, lands = state
    n_src, n = len(srcs), len(srcs) + len(lands)

    def body(*refs):
        send_refs, recv_refs = refs[n:n + 2]
        for cp in plan(refs[:n_src], refs[n_src:n], send_refs, recv_refs, arrivals=False):
            cp.wait_send()
        for cp in plan(refs[:n_src], refs[n_src:n], send_refs, recv_refs, arrivals=True):
            cp.wait_recv()

    hbm = lambda a: pltpu.HBM(a.shape, a.dtype)
    outs = pl.pallas_call(
        body,
        name=name,
        in_specs=[_HBM] * n + [_SEM, _SEM, _ANY],
        out_specs=[_HBM] * n,
        out_shape=[hbm(a) for a in (*srcs, *lands)],
        input_output_aliases={i: i for i in range(n)},
        compiler_params=pltpu.CompilerParams(has_side_effects=pltpu.SideEffectType.DATAFLOW_SIDE_EFFECTING),
    )(*srcs, *lands, send_sems, recv_sems, after)
    return outs[n_src:]


def _sum_blocks(blocks, *, name):
    _, R, C = blocks.shape
    tm = next(R // n for n in (4, 3, 2, 1) if R % (8 * n) == 0)

    def body(b_ref, o_ref):
        g = b_ref[0]
        for dev in range(1, N_DEV):
            g = g + b_ref[dev]
        o_ref[...] = g

    return pl.pallas_call(
        body,
        name=name,
        grid=(R // tm,),
        in_specs=[pl.BlockSpec((N_DEV, tm, C), lambda i: (0, i, 0))],
        out_specs=pl.BlockSpec((tm, C), lambda i: (i, 0)),
        out_shape=jax.ShapeDtypeStruct((R, C), F32),
        compiler_params=_params(("parallel",)),
    )(blocks)


def _block_diag(w):
    out = jnp.zeros((POOL_W, POOL_W), w.dtype)
    for gi in range(4):
        out = out.at[64 * gi:64 * (gi + 1), 64 * gi:64 * (gi + 1)].set(w[gi])
    return out


def _layer_consts(sp, l):
    causal = jnp.tril(jnp.ones((SGU_CHUNK, SGU_CHUNK), F32))
    wm = (sp["sgu_w"][l] * causal[None]).astype(BF16)
    wbd = _block_diag(sp["pool_w"][l]).astype(BF16)
    return dict(
        wbd=wbd, wbd_t=wbd.T, wm=wm, wm_t=wm.transpose(0, 2, 1),
        sgu_bias=jnp.repeat(sp["sgu_b"][l].T, 64, axis=1),
        bpad=jnp.pad(sp["b_forget"][l], (0, F_LANES - FOX_H)).reshape(1, F_LANES),
        bg=sp["b_gate"][l].reshape(1, 3 * D),
    )


def _relu2(acc):
    return acc, jnp.square(jnp.maximum(acc, 0.0))


def _relu2_grad(acc, z):
    return (acc * 2.0 * jnp.maximum(z, 0.0),)


def _layer_fwd(l, x, h, mem, source, sp):
    S = x.shape[0]
    t = _tile(S, 256)
    c = _layer_consts(sp, l)
    n = f"l{l}_"
    W, after = source(l, "begin", x)
    if h is None:
        h, after = _rms_fwd(x, sp["norm_mix_g"][l], after=after, name=n + "norm_mix"), None
    hm = _rms_fwd(mem, sp["norm_mem_g"][l], name=n + "norm_mem")
    more, token = source(l, "normed", hm)
    W.update(more)
    qkv = _mm(h, W["qkv"], out_dtypes=(BF16,), after=after if token is None else token, name=n + "qkv")
    rest = _mm(h, W["rest"], name=n + "rest")
    pa = _pool_fwd(rest, c["wbd"], sp["pool_scale"][l], name=n + "pool")
    cum, cum_t = _fox_prep(rest, c["bpad"], name=n + "fox_prep")
    fk3 = cum_t[:FOX_H].reshape(FOX_H, S // t, t)
    o, lse = _fox_fwd(qkv, cum, fk3, name=n + "fox")
    more, _ = source(l, "attended", o)
    W.update(more)
    sg = _sgu_fwd(rest, sp["sgu_norm_g"][l], c["wm"], c["sgu_bias"], name=n + "sgu")
    more, after = source(l, "mixed", sg)
    W.update(more)
    ya = _mm(pa, W["ba"], out_dtypes=(BF16,), after=after, name=n + "branch_a")
    yb = _mm(o, W["bb"], out_dtypes=(BF16,), name=n + "branch_b")
    yc = _mm(sg, W["bc"], out_dtypes=(BF16,), name=n + "branch_c")
    merged = _merge_fwd(rest, c["bg"], ya, yb, yc, name=n + "merge")
    whole_rows = dict(epilogue=_add_norm, out_dtypes=(F32, BF16), tm=1024, tn=D)
    x1, hx = _mm(merged, W["out"], extras=(x,), row_extras=(sp["norm_xattn_g"][l].reshape(1, D),), name=n + "out", **whole_rows)
    xq = _mm(hx, W["xq"], out_dtypes=(BF16,), name=n + "xq")
    kv = _mm(hm, W["xkv"], out_dtypes=(BF16,), name=n + "xkv")
    o2 = _xattn_fwd(xq, kv, name=n + "xattn")
    x2, hf = _mm(o2, W["xo"], extras=(x1,), row_extras=(sp["norm_ffn_g"][l].reshape(1, D),), name=n + "xo", **whole_rows)
    z, act = _mm(hf, W["ff1"], epilogue=_relu2, out_dtypes=(BF16, BF16), name=n + "ff1")
    _, after = source(l, "expanded", act)
    if l + 1 < DEPTH:
        x3, h_next = _mm(act, W["ff2"], extras=(x2,), row_extras=(sp["norm_mix_g"][l + 1].reshape(1, D),), after=after, name=n + "ff2",
                         **whole_rows)
    else:
        x3, h_next = _mm(act, W["ff2"], extras=(x2,), epilogue=_add, after=after, name=n + "ff2"), None
    saved = dict(x=x, h=h, qkv=qkv, rest=rest, pa=pa, cum=cum, fk3=fk3, o=o, lse=lse, sg=sg, ya=ya, yb=yb, yc=yc,
                 merged=merged, x1=x1, hx=hx, hm=hm, xq=xq, kv=kv, o2=o2, x2=x2, hf=hf, z=z, act=act, c=c)
    return x3, h_next, saved, W


def _layer_bwd(l, dx3, sv, mem, W, sp, grads_done):
    S = dx3.shape[0]
    c = sv["c"]
    n = f"l{l}b_"
    bf = dict(out_dtypes=(BF16,))
    gw, gs = {}, {}
    gw["ff2"] = _mm(sv["act"], dx3, ta=True, name=n + "dw_ff2", **bf)
    dz = _mm(dx3, W["ff2"], tb=True, extras=(sv["z"],), epilogue=_relu2_grad, name=n + "dz", **bf)
    gw["ff1"] = _mm(sv["hf"], dz, ta=True, shard_out=True, name=n + "dw_ff1", **bf)
    whole_rows = dict(epilogue=_norm_grad, out_dtypes=(F32, F32), row_outs=1, tm=1024, tn=D)
    gain = lambda key: (sp[key][l].reshape(1, D),)
    dx2, dg = _mm(dz, W["ff1"], tb=True, extras=(sv["x2"], dx3), row_extras=gain("norm_ffn_g"), name=n + "dhf", **whole_rows)
    gs["norm_ffn_g"] = dg.reshape(D)
    gw["xo"] = _mm(sv["o2"], dx2, ta=True, name=n + "dw_xo", **bf)
    do2 = _mm(dx2, W["xo"], tb=True, name=n + "do2", **bf)
    dxq, dkv = _xattn_bwd(sv["xq"], sv["kv"], do2, name=n + "dxattn")
    gw["xq"] = _mm(sv["hx"], dxq, ta=True, name=n + "dw_xq", **bf)
    gw["xkv"] = _mm(sv["hm"], dkv, ta=True, shard_out=True, name=n + "dw_xkv", **bf)
    dhm = _mm(dkv, W["xkv"], tb=True, name=n + "dhm")
    _, gs["norm_mem_g"] = _rms_bwd(mem, sp["norm_mem_g"][l], dhm, jnp.zeros_like(mem), name=n + "dnorm_mem")
    dx1, dg = _mm(dxq, W["xq"], tb=True, extras=(sv["x1"], dx2), row_extras=gain("norm_xattn_g"), name=n + "dhx", **whole_rows)
    gs["norm_xattn_g"] = dg.reshape(D)
    after, gw = grads_done(l, gw), {}
    gw["out"] = _mm(sv["merged"], dx1, ta=True, name=n + "dw_out", **bf)
    dm = _mm(dx1, W["out"], tb=True, after=after, name=n + "dmerged")
    dya, dyb, dyc, dg1, dg2, dg3, db1, db2, db3 = _merge_bwd(sv["rest"], c["bg"], sv["ya"], sv["yb"], sv["yc"], dm, name=n + "dmerge")
    gs["b_gate"] = jnp.concatenate([db1, db2, db3], axis=1).reshape(3 * D)
    gw["ba"] = _mm(sv["pa"], dya, ta=True, shard_out=True, name=n + "dw_ba", **bf)
    gw["bb"] = _mm(sv["o"], dyb, ta=True, shard_out=True, name=n + "dw_bb", **bf)
    gw["bc"] = _mm(sv["sg"], dyc, ta=True, shard_out=True, name=n + "dw_bc", **bf)
    after, gw = grads_done(l, gw), {}
    dpa = _mm(dya, W["ba"], tb=True, name=n + "dpa")
    do = _mm(dyb, W["bb"], tb=True, after=after, name=n + "do", **bf)
    dsg = _mm(dyc, W["bc"], tb=True, name=n + "dsg")
    da, dwbd, dscale = _pool_bwd(sv["rest"], c["wbd"], c["wbd_t"], sp["pool_scale"][l], dpa, name=n + "dpool")
    gs["pool_w"] = jnp.stack([dwbd[64 * gi:64 * (gi + 1), 64 * gi:64 * (gi + 1)] for gi in range(4)])
    gs["pool_scale"] = dscale.reshape(POOL_W)
    dq, dk, dv, dfq, dfk = _fox_bwd(sv["qkv"], sv["cum"], sv["fk3"], sv["o"], do, sv["lse"], name=n + "dfox")
    dcum = dfq + jnp.pad(dfk.reshape(FOX_H, S).T, ((0, 0), (0, F_LANES - FOX_H)))
    df, dbf = _fox_post(sv["rest"], c["bpad"], dcum, name=n + "dfox_post")
    gs["b_forget"] = dbf[0, :FOX_H]
    dc, dwm, dbias, dgn = _sgu_bwd(sv["rest"], sp["sgu_norm_g"][l], c["wm"], c["wm_t"], c["sgu_bias"], dsg, name=n + "dsgu")
    gs["sgu_w"] = dwm * jnp.tril(jnp.ones((SGU_CHUNK, SGU_CHUNK), F32))[None]
    gs["sgu_b"] = dbias.reshape(SGU_CHUNK, 4, 64).sum(axis=2).T
    gs["sgu_norm_g"] = dgn.reshape(SGU_W)
    dqkv = [dq, dk, dv]
    drest = [jnp.concatenate([da, df, jnp.zeros((S, OFF_C - OFF_F - F_LANES), BF16), dc], axis=1), dg1, dg2, dg3]
    gw["qkv"] = _mm(sv["h"], dqkv, ta=True, name=n + "dw_qkv", **bf)
    gw["rest"] = _mm(sv["h"], drest, ta=True, name=n + "dw_rest", **bf)
    after = grads_done(l, gw)
    dh = _mm(dqkv, W["qkv"], tb=True, after=after, name=n + "dh_qkv")
    dx, dg = _mm(drest, W["rest"], tb=True, extras=(dh, sv["x"], dx1), row_extras=gain("norm_mix_g"), name=n + "dh",
                 **{**whole_rows, "epilogue": _add_norm_grad, "tm": 512})
    gs["norm_mix_g"] = dg.reshape(D)
    return dx, gs


def _local_step(x, mem, target, sp, source, grads_done):
    saved, Ws, h = [], [], None
    for l in range(DEPTH):
        x, h, sv, W = _layer_fwd(l, x, h, mem, source, sp)
        saved.append(sv)
        Ws.append(W)
    loss, dx, dgf = _final_loss(x, sp["final_norm_g"], target, name="final_loss")
    gss = [None] * DEPTH
    for l in reversed(range(DEPTH)):
        dx, gss[l] = _layer_bwd(l, dx, saved[l], mem, Ws[l], sp, grads_done)
    small = {k: jnp.stack([gss[l][k] for l in range(DEPTH)]) for k in gss[0]}
    small["final_norm_g"] = dgf
    return loss, dx, small


_SMALL = ["norm_mix_g", "b_forget", "pool_w", "pool_scale", "sgu_norm_g", "sgu_w", "sgu_b", "b_gate", "norm_xattn_g",
          "norm_mem_g", "norm_ffn_g", "final_norm_g"]
_COL = {"w_branch_a": "ba", "w_branch_b": "bb", "w_branch_c": "bc", "w_xkv": "xkv", "w_ff1": "ff1"}
_ROW = {"w_out": "out", "w_xq": "xq", "w_xo": "xo", "w_ff2": "ff2"}
_BIG = ["w_in", "w_branch_a", "w_branch_b", "w_branch_c", "w_out", "w_xq", "w_xkv", "w_xo", "w_ff1", "w_ff2"]
_PACK_LANES = 128


def _as_rows(a):
    return a.reshape(-1, a.shape[-1])


def _pack(tensors):
    rows = []
    for a in tensors:
        flat = a.reshape(-1)
        flat = jnp.pad(flat, (0, (-flat.shape[0]) % (8 * _PACK_LANES)))
        rows.append(flat.reshape(-1, _PACK_LANES))
    n_rows = sum(r.shape[0] for r in rows)
    rows.append(jnp.zeros(((-n_rows) % (8 * N_DEV), _PACK_LANES), F32))
    return jnp.concatenate(rows, axis=0)


def _unpack(packed, like):
    out, r = [], 0
    for a in like:
        size = math.prod(a.shape)
        nr = 8 * (-(-size // (8 * _PACK_LANES)))
        out.append(packed[r:r + nr].reshape(-1)[:size].reshape(a.shape))
        r += nr
    return out


_SHARD_IN = N_IN // N_DEV
_SHARD_IN_PAD = -(-_SHARD_IN // 128) * 128


def _columns(pieces, start, stop):
    out, at = [], 0
    for p in pieces:
        lo, hi = max(start, at), min(stop, at + p.shape[1])
        if lo < hi:
            out.append(p[:, lo - at:hi - at])
        at += p.shape[1]
    return out


def _split_w_in(blocks):
    K = blocks[0].shape[0]
    pad = jnp.zeros((K, OFF_C - OFF_F - FOX_H), blocks[0].dtype)
    cols = functools.partial(_columns, blocks)
    rest = jnp.concatenate(cols(0, R_OFF_Q) + cols(R_OFF_F, R_OFF_C) + [pad] + cols(R_OFF_C, N_IN), axis=1)
    return jnp.concatenate(cols(R_OFF_Q, R_OFF_F), axis=1), rest


def _join_w_in(qkv, rest):
    in_order = [rest[:, :R_OFF_Q], qkv, rest[:, OFF_F:OFF_F + FOX_H], rest[:, OFF_C:]]
    pad = jnp.zeros((qkv.shape[0], _SHARD_IN_PAD - _SHARD_IN), qkv.dtype)
    return jnp.stack([jnp.concatenate(_columns(in_order, _SHARD_IN * d, _SHARD_IN * (d + 1)) + [pad], axis=1) for d in range(N_DEV)])


_FIRST = ["w_in"]
_LATER = [k for k in _BIG if k not in _FIRST]


def _layer_weights(gathered):
    W = {}
    if "w_in" in gathered:
        W.update(zip(("qkv", "rest"), _split_w_in([gathered["w_in"][d][:, :_SHARD_IN] for d in range(N_DEV)])))
    for name, key in _COL.items():
        if name in gathered:
            W[key] = _Gathered(gathered[name])
    if "xkv" in W:
        W["xkv"] = W["xkv"].arr.transpose(1, 0, 2).reshape(D, -1)
    for name, key in _ROW.items():
        if name in gathered:
            W[key] = gathered[name].reshape(-1, gathered[name].shape[-1])
    return W


def _grad_blocks(gw):
    parts = {}
    if "qkv" in gw:
        parts["w_in"] = _join_w_in(gw["qkv"], gw["rest"])
    for name, key in _COL.items():
        if key in gw:
            parts[name] = gw[key]
    for name, key in _ROW.items():
        if key in gw:
            parts[name] = gw[key].reshape(N_DEV, -1, gw[key].shape[-1])
    return parts


def kernel(x, mem, norm_mix_g, w_in, b_forget, pool_w, pool_scale, sgu_norm_g, sgu_w, sgu_b, w_branch_a, w_branch_b, w_branch_c, b_gate, w_out, norm_xattn_g, norm_mem_g, w_xq, w_xkv, w_xo, norm_ffn_g, w_ff1, w_ff2, final_norm_g, loss_target, m_norm_mix_g, m_w_in, m_b_forget, m_pool_w, m_pool_scale, m_sgu_norm_g, m_sgu_w, m_sgu_b, m_w_branch_a, m_w_branch_b, m_w_branch_c, m_b_gate, m_w_out, m_norm_xattn_g, m_norm_mem_g, m_w_xq, m_w_xkv, m_w_xo, m_norm_ffn_g, m_w_ff1, m_w_ff2, m_final_norm_g, v_norm_mix_g, v_w_in, v_b_forget, v_pool_w, v_pool_scale, v_sgu_norm_g, v_sgu_w, v_sgu_b, v_w_branch_a, v_w_branch_b, v_w_branch_c, v_b_gate, v_w_out, v_norm_xattn_g, v_norm_mem_g, v_w_xq, v_w_xkv, v_w_xo, v_norm_ffn_g, v_w_ff1, v_w_ff2, v_final_norm_g):
    names = ["norm_mix_g", "w_in", "b_forget", "pool_w", "pool_scale", "sgu_norm_g", "sgu_w", "sgu_b", "w_branch_a", "w_branch_b",
             "w_branch_c", "b_gate", "w_out", "norm_xattn_g", "norm_mem_g", "w_xq", "w_xkv", "w_xo", "norm_ffn_g", "w_ff1", "w_ff2",
             "final_norm_g"]
    w = dict(zip(names, [norm_mix_g, w_in, b_forget, pool_w, pool_scale, sgu_norm_g, sgu_w, sgu_b, w_branch_a, w_branch_b, w_branch_c,
                         b_gate, w_out, norm_xattn_g, norm_mem_g, w_xq, w_xkv, w_xo, norm_ffn_g, w_ff1, w_ff2, final_norm_g]))
    m = dict(zip(names, [m_norm_mix_g, m_w_in, m_b_forget, m_pool_w, m_pool_scale, m_sgu_norm_g, m_sgu_w, m_sgu_b, m_w_branch_a,
                         m_w_branch_b, m_w_branch_c, m_b_gate, m_w_out, m_norm_xattn_g, m_norm_mem_g, m_w_xq, m_w_xkv, m_w_xo,
                         m_norm_ffn_g, m_w_ff1, m_w_ff2, m_final_norm_g]))
    v = dict(zip(names, [v_norm_mix_g, v_w_in, v_b_forget, v_pool_w, v_pool_scale, v_sgu_norm_g, v_sgu_w, v_sgu_b, v_w_branch_a,
                         v_w_branch_b, v_w_branch_c, v_b_gate, v_w_out, v_norm_xattn_g, v_norm_mem_g, v_w_xq, v_w_xkv, v_w_xo,
                         v_norm_ffn_g, v_w_ff1, v_w_ff2, v_final_norm_g]))

    sp = {k: w[k] for k in _SMALL}
    shards = [{k: w[k][l].astype(BF16) for k in _BIG} for l in range(DEPTH)]
    for sh in shards:
        sh["w_in"] = jnp.pad(sh["w_in"], ((0, 0), (0, _SHARD_IN_PAD - _SHARD_IN)))
    me = _dev_index(*_position())

    def gather_out(l, keys, name, after=None):
        srcs = [shards[l][k] for k in keys]
        lands = [_own_block_placed(a, jax.ShapeDtypeStruct((N_DEV, *a.shape), a.dtype)) for a in srcs]
        state, token = _split_start(_plan_gather_out, srcs, lands, after=after, name=name + "_out_start")
        return (keys, name, state), token

    def gather_pass(job, value):
        keys, name, state = job
        lands = _split_wait(_plan_gather_out, state, value, name=name + "_out_wait")
        state, token = _split_start(_plan_gather_pass, [], lands, name=name + "_pass_start")
        return (keys, name, state), token, lands[0]

    def gather_end(job, value):
        keys, name, state = job
        return _layer_weights(dict(zip(keys, _split_wait(_plan_gather_pass, state, value, name=name + "_pass_wait"))))

    jobs = {}

    def source(l, point, value):
        if (l, point) == (0, "begin"):
            jobs["l0_first"], token = gather_out(0, _FIRST, "gather_l0_first")
            return {}, token
        if (l, point) == (0, "normed"):
            jobs["l0_first"], token, arrived = gather_pass(jobs["l0_first"], value)
            jobs["l0"], _ = gather_out(0, _LATER, "gather_l0", after=arrived)
            return gather_end(jobs.pop("l0_first"), token), None
        if (l, point) == (0, "attended"):
            jobs["l0"], _, arrived = gather_pass(jobs["l0"], value)
            jobs["l1_first"], token = gather_out(1, _FIRST, "gather_l1_first", after=arrived)
            jobs["l1"], jobs["token"] = gather_out(1, _LATER, "gather_l1", after=token)
            return {}, None
        if (l, point) == (0, "mixed"):
            return gather_end(jobs.pop("l0"), value), jobs.pop("token")
        if (l, point) == (0, "expanded"):
            jobs["l1_first"], token, _ = gather_pass(jobs["l1_first"], value)
            return {}, token
        if (l, point) == (1, "begin"):
            W = gather_end(jobs.pop("l1_first"), value)
            jobs["l1"], token, _ = gather_pass(jobs["l1"], value)
            return W, token
        if (l, point) == (1, "mixed"):
            return gather_end(jobs.pop("l1"), value), None
        return {}, None

    received = [{} for _ in range(DEPTH)]
    travelling = []

    def grads_done(l, gw):
        blocks = _grad_blocks(gw)
        keys = [k for k in _BIG if k in blocks]
        parts = [blocks[k] for k in keys]
        group = f"exchange_grads_l{l}_" + ("in" if "w_in" in blocks else "merge" if "w_out" in blocks else "mlp")
        lands = [_own_block_placed(lax.dynamic_index_in_dim(p, me, 0, keepdims=False), p) for p in parts]
        state, token = _split_start(_plan_exchange, parts, lands, name=group + "_start")
        travelling.append((l, keys, state, group + "_wait"))
        return token

    loss, dx, small = _local_step(x[0], mem[0], loss_target[0], sp, source, grads_done)
    grads, deltas, new_m, new_v = {}, {}, {}, {}
    like = [loss] + [w[k] for k in _SMALL]
    packed = _pack([loss] + [small[k] for k in _SMALL])
    eighths = packed.reshape(N_DEV, -1, _PACK_LANES)
    own = lambda a: _own_block_placed(lax.dynamic_index_in_dim(a, me, 0, keepdims=False) if a.ndim == 3 else a, eighths)
    scatter, done = _split_start(_plan_exchange, [eighths], [own(eighths)], after=dx, name="small_grads_scatter_start")

    def reduce_small(after):
        mine = _sum_blocks(_split_wait(_plan_exchange, scatter, after, name="small_grads_scatter_wait")[0], name="small_grads_sum")
        return _split_start(_plan_broadcast, [mine], [own(mine)], name="small_grads_gather_start")

    def update_small(state, after):
        total = _split_wait(_plan_broadcast, state, after, name="small_grads_gather_wait")[0].reshape(packed.shape)
        loss_sum, *g_small = _unpack(total, like)
        rows = lambda d: [_as_rows(d[k]) for k in _SMALL]
        outs = _adamw_small([_as_rows(g) for g in g_small], rows(w), rows(m), rows(v), name="adamw_small")
        grads.update(zip(_SMALL, g_small))
        for dst, vals in zip((deltas, new_m, new_v), outs):
            dst.update({k: a.reshape(w[k].shape) for k, a in zip(_SMALL, vals)})
        return loss_sum[0, 0], outs[0][0]

    groups = list(dict.fromkeys(tuple(keys) for _, keys, _, _ in travelling))
    for n_done, group_keys in enumerate(groups):
        if n_done == 1:
            gather, _ = reduce_small(done)
        if n_done == len(groups) - 1:
            loss, done = update_small(gather, done)
        for l, keys, state, wait_name in travelling:
            if tuple(keys) == group_keys:
                received[l].update(zip(keys, _split_wait(_plan_exchange, state, done, name=wait_name)))
        for k in group_keys:
            outs = _adamw_sharded([received[l][k] for l in range(DEPTH)], w[k], m[k], v[k], name="adamw_" + k)
            grads[k], deltas[k], new_m[k], new_v[k] = outs
        done = grads[group_keys[-1]]

    return (loss, dx[None], *[grads[k] for k in names], *[deltas[k] for k in names], *[new_m[k] for k in names],
            *[new_v[k] for k in names])
```

```python
import functools
import math

import jax
import jax.numpy as jnp
from jax import lax
from jax.experimental import pallas as pl
from jax.experimental.pallas import tpu as pltpu

F32 = jnp.float32
BF16 = jnp.bfloat16
MESH = pl.DeviceIdType.MESH

N_DEV = 8
D = 1024
DEPTH = 2
EPS = 1e-6
NEG = -1e30
POOL_W = 256
FOX_H = 8
FOX_DH = 64
FOX_W = 512
SGU_W = 256
SGU_CHUNK = 128
XH = 4
XDH = 256
N_IN = 5384
R_OFF_Q, R_OFF_F, R_OFF_C = 256, 1792, 1800
QKV_W = 3 * FOX_W
OFF_A, OFF_F, OFF_C, OFF_G, REST_W = 0, 256, 512, 1024, 4096
F_LANES = 128

ADAM_LR = 0.001
ADAM_B1 = 0.9
ADAM_B2 = 0.999
ADAM_EPS = 1e-08
ADAM_WD = 0.01
ADAM_STEP = 10

VMEM_LIMIT = 56 * 1024 * 1024


def _tile(n, pref):
    t = min(n, pref)
    while n % t:
        t -= 128
    assert t > 0, (n, pref)
    return t


def _params(sem=None):
    return pltpu.CompilerParams(dimension_semantics=sem, vmem_limit_bytes=VMEM_LIMIT)


def _dot(a, b, ca, cb):
    return lax.dot_general(a, b, (((ca,), (cb,)), ((), ())), preferred_element_type=F32)


def _sigmoid(z):
    return 1.0 / (1.0 + jnp.exp(-z))


_GELU_K = math.sqrt(2.0 / math.pi)
_GELU_C = 0.044715


def _gelu(x):
    return 0.5 * x * (1.0 + jnp.tanh(_GELU_K * (x + _GELU_C * x * x * x)))


def _gelu_grad(x):
    t = jnp.tanh(_GELU_K * (x + _GELU_C * x * x * x))
    return 0.5 * (1.0 + t) + 0.5 * x * (1.0 - t * t) * _GELU_K * (1.0 + 3.0 * _GELU_C * x * x)


def _rows(shape):
    return lax.broadcasted_iota(jnp.int32, shape, 0)


def _lanes(shape):
    return lax.broadcasted_iota(jnp.int32, shape, 1)


class _Gathered:
    def __init__(self, arr):
        self.arr = arr
        self.shape = (arr.shape[1], N_DEV * arr.shape[2])


_TOKEN = (8, 128)


def _mm(a, b, *, ta=False, tb=False, extras=(), row_extras=(), epilogue=None, out_dtypes=(F32,), row_outs=0, shard_out=False, after=None,
        tm=None, tn=512, tk=None, name):
    a_parts = list(a) if isinstance(a, (list, tuple)) else [a]
    b_parts = list(b) if isinstance(b, (list, tuple)) else [b]
    gathered = isinstance(b, _Gathered)
    assert (len(a_parts) == 1 or not ta) and (len(b_parts) == 1 or not tb) and min(len(a_parts), len(b_parts)) == 1
    a0, b0 = a_parts[0], b_parts[0]
    M, K = (a0.shape[1], a0.shape[0]) if ta else (a0.shape[0], a0.shape[1] * len(a_parts))
    N, Kb = b0.shape if tb else (b0.shape[1] * len(b_parts), b0.shape[0])
    assert Kb == K, (a0.shape, b0.shape, ta, tb)
    if gathered:
        if tb:
            tk = b.arr.shape[2]
        else:
            tn = b.arr.shape[2]
    if len(a_parts) > 1:
        tk = a0.shape[1]
    if shard_out:
        tn = N // N_DEV
    tm = _tile(M, tm or (1024 if ta else 2048))
    tn = _tile(b0.shape[1] if len(b_parts) > 1 else N, tn)
    per_piece = b0.shape[1] // tn
    size = lambda dt: jnp.dtype(dt).itemsize
    row_bytes = len(a_parts) * tm * size(a0.dtype) + len(b_parts) * tn * size(b.arr.dtype if gathered else b0.dtype)
    tile_bytes = tm * tn * (sum(size(e.dtype) for e in extras) + sum(map(size, out_dtypes)))

    def vmem_bytes(k_tile):
        return 2 * (k_tile * row_bytes + tile_bytes) + tm * tn * 4 * (K > k_tile)

    if tk is None:
        tk = next(c for c in (_tile(K, 2048), _tile(K, 1024), _tile(K, 512), _tile(K, 256)) if vmem_bytes(c) <= VMEM_LIMIT - (4 << 20))
    tk = _tile(K, tk)
    nk = K // tk
    ca, cb = (0 if ta else 1), (1 if tb else 0)
    n_a, n_b, n_ex, n_out = len(a_parts), len(b_parts), len(extras) + len(row_extras), len(out_dtypes)
    tokens = [] if after is None else [after]
    n_in = n_a + n_b + n_ex + len(tokens)
    if epilogue is None:
        epilogue = lambda acc: (acc,)

    def body(*refs):
        a_refs, b_refs = refs[:n_a], refs[n_a:n_a + n_b]
        ex_refs = refs[n_a + n_b:n_a + n_b + n_ex]
        o_refs = refs[n_in:n_in + n_out]
        j, k = pl.program_id(1), pl.program_id(2)

        def finish(acc):
            vals = epilogue(acc, *[e[...] for e in ex_refs])
            for o_ref, val in zip(o_refs[:n_out - row_outs], vals):
                o_ref[...] = val.astype(o_ref.dtype)
            for o_ref, val in zip(o_refs[n_out - row_outs:], vals[n_out - row_outs:]):
                first = pl.program_id(0) == 0
                o_ref[...] = jnp.where(first, val, o_ref[...] + val)

        def step(a_ref, b_ref):
            part = _dot(a_ref[...].astype(BF16), b_ref[...].astype(BF16), ca, cb)
            if nk == 1:
                finish(part)
            else:
                acc_ref = refs[-1]

                @pl.when(k == 0)
                def _():
                    acc_ref[...] = part

                @pl.when(k > 0)
                def _():
                    acc_ref[...] += part

                @pl.when(k == nk - 1)
                def _():
                    finish(acc_ref[...])

        if n_a > 1:
            for p in range(n_a):
                pl.when(k == p)(functools.partial(step, a_refs[p], b_refs[0]))
        elif n_b > 1:
            for p in range(n_b):
                pl.when(j // per_piece == p)(functools.partial(step, a_refs[0], b_refs[p]))
        else:
            step(a_refs[0], b_refs[0])

    if n_a > 1:
        a_specs = [pl.BlockSpec((tm, tk), lambda i, j, k: (i, 0))] * n_a
    else:
        a_specs = [pl.BlockSpec((tk, tm), lambda i, j, k: (k, i)) if ta else pl.BlockSpec((tm, tk), lambda i, j, k: (i, k))]
    if gathered:
        b_arrs = [b.arr]
        b_specs = [pl.BlockSpec((None, tn, tk), lambda i, j, k: (k, j, 0)) if tb else pl.BlockSpec((None, tk, tn), lambda i, j, k: (j, k, 0))]
    elif n_b > 1:
        b_arrs = b_parts
        b_specs = [pl.BlockSpec((tk, tn), functools.partial(lambda p, i, j, k: (k, jnp.clip(j - p * per_piece, 0, per_piece - 1)), p))
                   for p in range(n_b)]
    else:
        b_arrs = b_parts
        b_specs = [pl.BlockSpec((tn, tk), lambda i, j, k: (j, k)) if tb else pl.BlockSpec((tk, tn), lambda i, j, k: (k, j))]
    tile = pl.BlockSpec((tm, tn), lambda i, j, k: (i, j))
    if shard_out:
        out_specs = [pl.BlockSpec((None, tm, tn), lambda i, j, k: (j, i, 0))] * n_out
        out_shape = [jax.ShapeDtypeStruct((N_DEV, M, tn), dt) for dt in out_dtypes]
    else:
        assert row_outs == 0 or tn == N
        out_specs = [tile] * (n_out - row_outs) + [pl.BlockSpec((1, tn), lambda i, j, k: (0, j))] * row_outs
        out_shape = [jax.ShapeDtypeStruct((1, N) if t >= n_out - row_outs else (M, N), dt) for t, dt in enumerate(out_dtypes)]
    assert vmem_bytes(tk) <= VMEM_LIMIT - (4 << 20), (name, vmem_bytes(tk))
    outs = pl.pallas_call(
        body,
        name=name,
        grid=(M // tm, N // tn, nk),
        in_specs=a_specs + b_specs + [tile] * len(extras) + [pl.BlockSpec((1, tn), lambda i, j, k: (0, j))] * len(row_extras)
        + [pl.BlockSpec(_TOKEN, lambda i, j, k: (0, 0))] * len(tokens),
        out_specs=out_specs,
        out_shape=out_shape,
        scratch_shapes=[pltpu.VMEM((tm, tn), F32)] if nk > 1 else [],
        compiler_params=_params(("arbitrary",) * 3 if row_outs else ("parallel", "parallel", "arbitrary")),
    )(*a_parts, *b_arrs, *extras, *row_extras, *tokens)
    return outs[0] if n_out == 1 else outs


def _add(acc, res):
    return (acc + res,)


def _norm_grad(dh, x, dres, g):
    r = lax.rsqrt(jnp.mean(x * x, axis=-1, keepdims=True) + EPS)
    xn = x * r
    dxn = dh * g
    return r * (dxn - xn * jnp.mean(dxn * xn, axis=-1, keepdims=True)) + dres, jnp.sum(dh * xn, axis=0, keepdims=True)


def _add_norm_grad(acc, more, x, dres, g):
    return _norm_grad(acc + more, x, dres, g)


def _add_norm(acc, res, g):
    x = acc + res
    return x, x * lax.rsqrt(jnp.mean(x * x, axis=-1, keepdims=True) + EPS) * g


def _rms_fwd(x, g, *, after=None, name):
    R, C = x.shape
    tm = _tile(R, 256)
    tokens = [] if after is None else [after]

    def body(x_ref, g_ref, *rest):
        xv = x_ref[...]
        r = lax.rsqrt(jnp.mean(xv * xv, axis=-1, keepdims=True) + EPS)
        rest[-1][...] = (xv * r * g_ref[...]).astype(BF16)

    return pl.pallas_call(
        body,
        name=name,
        grid=(R // tm,),
        in_specs=[pl.BlockSpec((tm, C), lambda i: (i, 0)), pl.BlockSpec((1, C), lambda i: (0, 0))]
        + [pl.BlockSpec(_TOKEN, lambda i: (0, 0))] * len(tokens),
        out_specs=pl.BlockSpec((tm, C), lambda i: (i, 0)),
        out_shape=jax.ShapeDtypeStruct((R, C), BF16),
        compiler_params=_params(("parallel",)),
    )(x, g.reshape(1, C), *tokens)


def _rms_bwd(x, g, dh, dres, *, name):
    R, C = x.shape
    tm = _tile(R, 256)

    def body(x_ref, g_ref, dh_ref, dres_ref, dx_ref, dg_ref):
        xv = x_ref[...]
        r = lax.rsqrt(jnp.mean(xv * xv, axis=-1, keepdims=True) + EPS)
        xn = xv * r
        dh_v = dh_ref[...].astype(F32)
        dxn = dh_v * g_ref[...]
        dx_ref[...] = r * (dxn - xn * jnp.mean(dxn * xn, axis=-1, keepdims=True)) + dres_ref[...]
        part = jnp.sum(dh_v * xn, axis=0, keepdims=True)

        @pl.when(pl.program_id(0) == 0)
        def _():
            dg_ref[...] = part

        @pl.when(pl.program_id(0) > 0)
        def _():
            dg_ref[...] += part

    row = pl.BlockSpec((tm, C), lambda i: (i, 0))
    vec = pl.BlockSpec((1, C), lambda i: (0, 0))
    dx, dg = pl.pallas_call(
        body,
        name=name,
        grid=(R // tm,),
        in_specs=[row, vec, row, row],
        out_specs=[row, vec],
        out_shape=[jax.ShapeDtypeStruct((R, C), F32), jax.ShapeDtypeStruct((1, C), F32)],
        compiler_params=_params(("arbitrary",)),
    )(x, g.reshape(1, C), dh, dres)
    return dx, dg.reshape(C)


def _final_loss(x, g, target, *, name):
    R, C = x.shape
    tm = _tile(R, 256)

    def body(x_ref, g_ref, t_ref, loss_ref, dx_ref, dg_ref):
        xv = x_ref[...]
        r = lax.rsqrt(jnp.mean(xv * xv, axis=-1, keepdims=True) + EPS)
        xn = xv * r
        gv = g_ref[...]
        err = xn * gv - t_ref[...]
        lpart = (0.5 / C) * jnp.sum(jnp.sum(err * err, axis=1, keepdims=True), axis=0, keepdims=True)
        dy = err * (1.0 / C)
        dxn = dy * gv
        dx_ref[...] = r * (dxn - xn * jnp.mean(dxn * xn, axis=-1, keepdims=True))
        gpart = jnp.sum(dy * xn, axis=0, keepdims=True)

        @pl.when(pl.program_id(0) == 0)
        def _():
            loss_ref[...] = lpart
            dg_ref[...] = gpart

        @pl.when(pl.program_id(0) > 0)
        def _():
            loss_ref[...] += lpart
            dg_ref[...] += gpart

    row = pl.BlockSpec((tm, C), lambda i: (i, 0))
    vec = pl.BlockSpec((1, C), lambda i: (0, 0))
    loss, dx, dg = pl.pallas_call(
        body,
        name=name,
        grid=(R // tm,),
        in_specs=[row, vec, row],
        out_specs=[pl.BlockSpec((1, 1), lambda i: (0, 0)), row, vec],
        out_shape=[jax.ShapeDtypeStruct((1, 1), F32), jax.ShapeDtypeStruct((R, C), F32), jax.ShapeDtypeStruct((1, C), F32)],
        compiler_params=_params(("arbitrary",)),
    )(x, g.reshape(1, C), target)
    return loss, dx, dg.reshape(C)


def _pool_select(lane, vals):
    out = vals[3]
    for gi in (2, 1, 0):
        out = jnp.where(lane < 64 * (gi + 1), vals[gi], out)
    return out


def _pool_diff(a):
    row, lane = _rows(a.shape), _lanes(a.shape)

    def down(v, k):
        return jnp.where(row >= k, pltpu.roll(v, k, 0), 0.0)

    s2 = a + down(a, 1)
    s4 = s2 + down(s2, 2)
    s8 = s4 + down(s4, 4)
    s16 = s8 + down(s8, 8)
    wsum = _pool_select(lane, (s2, s4, s8, s16))
    win = _pool_select(lane, (2, 4, 8, 16))
    cnt = jnp.minimum(row + 1, win).astype(F32)
    return wsum / cnt - a, cnt


def _pool_diff_t(dd, cnt):
    S = dd.shape[0]
    row, lane = _rows(dd.shape), _lanes(dd.shape)

    def up(v, k):
        return jnp.where(row < S - k, pltpu.roll(v, S - k, 0), 0.0)

    e = dd / cnt
    s2 = e + up(e, 1)
    s4 = s2 + up(s2, 2)
    s8 = s4 + up(s4, 4)
    s16 = s8 + up(s8, 8)
    return _pool_select(lane, (s2, s4, s8, s16)) - dd


def _pool_fwd(rest, wbd, scale, *, name):
    S = rest.shape[0]

    def body(a_ref, w_ref, s_ref, o_ref):
        d, _ = _pool_diff(a_ref[...])
        yp = _dot(d.astype(BF16), w_ref[...], 1, 0)
        o_ref[...] = (yp * s_ref[...]).astype(BF16)

    return pl.pallas_call(
        body,
        name=name,
        grid=(1,),
        in_specs=[
            pl.BlockSpec((S, POOL_W), lambda i: (0, OFF_A // POOL_W)),
            pl.BlockSpec((POOL_W, POOL_W), lambda i: (0, 0)),
            pl.BlockSpec((1, POOL_W), lambda i: (0, 0)),
        ],
        out_specs=pl.BlockSpec((S, POOL_W), lambda i: (0, 0)),
        out_shape=jax.ShapeDtypeStruct((S, POOL_W), BF16),
        compiler_params=_params(("arbitrary",)),
    )(rest, wbd, scale.reshape(1, POOL_W))


def _pool_bwd(rest, wbd, wbd_t, scale, dpa, *, name):
    S = rest.shape[0]

    def body(a_ref, w_ref, wt_ref, s_ref, dpa_ref, da_ref, dw_ref, ds_ref):
        d, cnt = _pool_diff(a_ref[...])
        db = d.astype(BF16)
        yp = _dot(db, w_ref[...], 1, 0)
        dpa_v = dpa_ref[...]
        ds_ref[...] = jnp.sum(dpa_v * yp, axis=0, keepdims=True)
        dyp = (dpa_v * s_ref[...]).astype(BF16)
        dw_ref[...] = _dot(db, dyp, 0, 0)
        dd = _dot(dyp, wt_ref[...], 1, 0)
        da_ref[...] = _pool_diff_t(dd, cnt).astype(BF16)

    full = pl.BlockSpec((S, POOL_W), lambda i: (0, 0))
    sq = pl.BlockSpec((POOL_W, POOL_W), lambda i: (0, 0))
    vec = pl.BlockSpec((1, POOL_W), lambda i: (0, 0))
    return pl.pallas_call(
        body,
        name=name,
        grid=(1,),
        in_specs=[pl.BlockSpec((S, POOL_W), lambda i: (0, OFF_A // POOL_W)), sq, sq, vec, full],
        out_specs=[full, sq, vec],
        out_shape=[
            jax.ShapeDtypeStruct((S, POOL_W), BF16),
            jax.ShapeDtypeStruct((POOL_W, POOL_W), F32),
            jax.ShapeDtypeStruct((1, POOL_W), F32),
        ],
        compiler_params=_params(("arbitrary",)),
    )(rest, wbd, wbd_t, scale.reshape(1, POOL_W), dpa)


def _log_sigmoid(z):
    return jnp.minimum(z, 0.0) - jnp.log(1.0 + jnp.exp(-jnp.abs(z)))


_F_SPEC_COL = OFF_F // F_LANES


def _fox_prep(rest, bpad, *, name):
    S = rest.shape[0]

    def body(f_ref, b_ref, o_ref, ot_ref):
        acc = _log_sigmoid(f_ref[...] + b_ref[...])
        row = _rows(acc.shape)
        k = 1
        while k < S:
            acc = acc + jnp.where(row >= k, pltpu.roll(acc, k, 0), 0.0)
            k *= 2
        o_ref[...] = acc
        ot_ref[...] = acc.T

    return pl.pallas_call(
        body,
        name=name,
        grid=(1,),
        in_specs=[pl.BlockSpec((S, F_LANES), lambda i: (0, _F_SPEC_COL)), pl.BlockSpec((1, F_LANES), lambda i: (0, 0))],
        out_specs=[pl.BlockSpec((S, F_LANES), lambda i: (0, 0)), pl.BlockSpec((F_LANES, S), lambda i: (0, 0))],
        out_shape=[jax.ShapeDtypeStruct((S, F_LANES), F32), jax.ShapeDtypeStruct((F_LANES, S), F32)],
        compiler_params=_params(("arbitrary",)),
    )(rest, bpad)


def _fox_post(rest, bpad, dcum, *, name):
    S = rest.shape[0]

    def body(f_ref, b_ref, d_ref, df_ref, db_ref):
        acc = d_ref[...]
        row = _rows(acc.shape)
        k = 1
        while k < S:
            acc = acc + jnp.where(row < S - k, pltpu.roll(acc, S - k, 0), 0.0)
            k *= 2
        df = acc * (1.0 - _sigmoid(f_ref[...] + b_ref[...]))
        df_ref[...] = df.astype(BF16)
        db_ref[...] = jnp.sum(df, axis=0, keepdims=True)

    full = pl.BlockSpec((S, F_LANES), lambda i: (0, 0))
    vec = pl.BlockSpec((1, F_LANES), lambda i: (0, 0))
    return pl.pallas_call(
        body,
        name=name,
        grid=(1,),
        in_specs=[pl.BlockSpec((S, F_LANES), lambda i: (0, _F_SPEC_COL)), vec, full],
        out_specs=[full, vec],
        out_shape=[jax.ShapeDtypeStruct((S, F_LANES), BF16), jax.ShapeDtypeStruct((1, F_LANES), F32)],
        compiler_params=_params(("arbitrary",)),
    )(rest, bpad, dcum)


_FOX_SCALE = FOX_DH ** -0.5
_PAIRS = FOX_H // 2


def _scaled(v):
    return (v.astype(F32) * _FOX_SCALE).astype(BF16)


def _diag_mask(s):
    return jnp.where(_rows(s.shape) >= _lanes(s.shape), s, NEG)


def _fox_fwd(qkv, cum, fk3, *, name):
    S = qkv.shape[0]
    nk, t = fk3.shape[1:]

    def body(q_ref, k_ref, v_ref, cum_ref, fk_ref, o_ref, lse_ref):
        i = pl.program_id(0)
        lane = _lanes((t, 128))
        lo = lane < FOX_DH
        cumv = cum_ref[...]
        qm, fq = [], []
        for h in range(FOX_H):
            qs = _scaled(q_ref[:, 128 * (h // 2):128 * (h // 2 + 1)])
            zero = jnp.zeros_like(qs)
            qm.append(jnp.where(lo, qs, zero) if h % 2 == 0 else jnp.where(lo, zero, qs))
            fq.append(jnp.broadcast_to(cumv[:, h:h + 1], (t, 128)))

        def tile(j, state, masked):
            m, acc, lsum = (list(part) for part in state)
            k0 = pl.multiple_of(j * t, t)
            for hp in range(_PAIRS):
                cols = slice(128 * hp, 128 * (hp + 1))
                kb = k_ref[pl.ds(k0, t), cols]
                vb = v_ref[pl.ds(k0, t), cols]
                one = jnp.ones_like(vb)
                alphas, pvs = [], []
                for h in (2 * hp, 2 * hp + 1):
                    s = _dot(qm[h], kb, 1, 1) + jnp.concatenate([fq[h]] * (t // 128), axis=1) - fk_ref[h, pl.ds(j, 1), :]
                    if masked:
                        s = _diag_mask(s)
                    m_new = jnp.maximum(m[h], jnp.max(s, axis=-1, keepdims=True))
                    p = jnp.exp(s - m_new)
                    alphas.append(jnp.exp(m[h] - m_new))
                    m[h] = m_new
                    pvs.append(_dot(p.astype(BF16), jnp.where(lo, vb, one) if h % 2 == 0 else jnp.where(lo, one, vb), 1, 0))
                acc[hp] = jnp.where(lo, alphas[0], alphas[1]) * acc[hp] + jnp.where(lo, pvs[0], pvs[1])
                lsum[hp] = jnp.where(lo, alphas[1], alphas[0]) * lsum[hp] + jnp.where(lo, pvs[1], pvs[0])
            return tuple(m), tuple(acc), tuple(lsum)

        zeros = (jnp.zeros((t, 128), F32),) * _PAIRS
        state = lax.fori_loop(0, i, functools.partial(tile, masked=False), ((jnp.full((t, 1), NEG, F32),) * FOX_H, zeros, zeros))
        m, acc, lsum = tile(i, state, True)
        for hp in range(_PAIRS):
            o_ref[:, 128 * hp:128 * (hp + 1)] = acc[hp] / pltpu.roll(lsum[hp], FOX_DH, 1)
            lse = [m[2 * hp] + jnp.log(lsum[hp][:, FOX_DH:FOX_DH + 1]), m[2 * hp + 1] + jnp.log(lsum[hp][:, 0:1])]
            lse_ref[hp] = jnp.where(lane == 0, lse[0], jnp.where(lane == 1, lse[1], 0.0))

    whole = lambda col: pl.BlockSpec((S, FOX_W), lambda i: (0, col))
    return pl.pallas_call(
        body,
        name=name,
        grid=(S // t,),
        in_specs=[
            pl.BlockSpec((t, FOX_W), lambda i: (i, 0)), whole(1), whole(2),
            pl.BlockSpec((t, F_LANES), lambda i: (i, 0)),
            pl.BlockSpec((FOX_H, nk, t), lambda i: (0, 0, 0)),
        ],
        out_specs=[pl.BlockSpec((t, FOX_W), lambda i: (i, 0)), pl.BlockSpec((_PAIRS, t, 128), lambda i: (0, i, 0))],
        out_shape=[jax.ShapeDtypeStruct((S, FOX_W), F32), jax.ShapeDtypeStruct((_PAIRS, S, 128), F32)],
        compiler_params=_params(("arbitrary",)),
    )(qkv, qkv, qkv, cum, fk3)


def _fox_bwd(qkv, cum, fk3, o, do, lse, *, name):
    S = qkv.shape[0]
    nk, t = fk3.shape[1:]
    q_at, k_at, v_at = 0, FOX_W, 2 * FOX_W

    def body(qkv_ref, cum_ref, fk_ref, o_ref, do_ref, lse_ref, dq_ref, dk_ref, dv_ref, dfq_ref, dfk_ref,
             qs_sc, ks_sc, bias_sc, delta_sc, dq_sc):
        lane = _lanes((t, 128))
        lo = lane < FOX_DH
        mine = lambda h: lo if h % 2 == 0 else jnp.logical_not(lo)

        def by_head(tile, values):
            for h, val in enumerate(values):
                tile = jnp.where(lane == h, val, tile)
            return tile

        def prep(i, carry):
            r = pl.ds(pl.multiple_of(i * t, t), t)
            qs_sc[r, :] = _scaled(qkv_ref[r, q_at:q_at + FOX_W])
            ks_sc[r, :] = _scaled(qkv_ref[r, k_at:k_at + FOX_W])
            cum_t = cum_ref[r, :]
            for hp in range(_PAIRS):
                cols = slice(128 * hp, 128 * (hp + 1))
                prod = do_ref[r, cols].astype(F32) * o_ref[r, cols]
                for h in (2 * hp, 2 * hp + 1):
                    delta = jnp.sum(jnp.where(mine(h), prod, 0.0), axis=-1, keepdims=True)
                    delta_sc[h, r, :] = jnp.broadcast_to(delta, (t, 128))
                    bias_sc[h, r, :] = jnp.broadcast_to(cum_t[:, h:h + 1] - lse_ref[hp, r, h % 2:h % 2 + 1], (t, 128))
            dfq_ref[r, :] = jnp.zeros((t, 128), F32)
            dq_sc[r, :] = jnp.zeros((t, FOX_W), F32)
            return carry

        lax.fori_loop(0, nk, prep, 0)

        def kv_tile(j, carry):
            kr = pl.ds(pl.multiple_of(j * t, t), t)

            def q_tile(i, acc, masked):
                dk, dv, dfk = list(acc[:_PAIRS]), list(acc[_PAIRS:2 * _PAIRS]), list(acc[2 * _PAIRS:])
                qr = pl.ds(pl.multiple_of(i * t, t), t)
                dq_old, dfq_old = dq_sc[qr, :], dfq_ref[qr, :]
                wide = lambda a: jnp.concatenate([a] * (t // 128), axis=1)
                row_sums, dq_new = [], []
                for hp in range(_PAIRS):
                    cols = slice(128 * hp, 128 * (hp + 1))
                    kb = qkv_ref[kr, k_at + 128 * hp:k_at + 128 * (hp + 1)]
                    vb = qkv_ref[kr, v_at + 128 * hp:v_at + 128 * (hp + 1)]
                    ksb, qsb, dob = ks_sc[kr, cols], qs_sc[qr, cols], do_ref[qr, cols]
                    zero = jnp.zeros_like(qsb)
                    dq_t = jnp.zeros((t, 128), F32)
                    for h in (2 * hp, 2 * hp + 1):
                        qe, doe, ke = (jnp.where(mine(h), a, zero) for a in (qsb, dob, ksb))
                        s = _dot(qe, kb, 1, 1) + wide(bias_sc[h, qr, :]) - fk_ref[h, pl.ds(j, 1), :]
                        if masked:
                            s = _diag_mask(s)
                        p = jnp.exp(s)
                        dv[hp] = dv[hp] + _dot(p.astype(BF16), doe, 0, 0)
                        dp = _dot(doe, vb, 1, 1)
                        ds = p * (dp - wide(delta_sc[h, qr, :]))
                        dsb = ds.astype(BF16)
                        dk[hp] = dk[hp] + _dot(dsb, qe, 0, 0)
                        dq_t = dq_t + _dot(dsb, ke, 1, 0)
                        row_sums.append(jnp.sum(ds, axis=-1, keepdims=True))
                        dfk[h] = dfk[h] - jnp.sum(ds, axis=0, keepdims=True)
                    dq_new.append(dq_old[:, cols] + dq_t)
                for hp in range(_PAIRS):
                    dq_sc[qr, 128 * hp:128 * (hp + 1)] = dq_new[hp]
                dfq_ref[qr, :] = dfq_old + by_head(jnp.zeros((t, 128), F32), row_sums)
                return (*dk, *dv, *dfk)

            init = tuple([jnp.zeros((t, 128), F32)] * (2 * _PAIRS) + [jnp.zeros((1, t), F32)] * FOX_H)
            acc = q_tile(j, init, True)
            acc = lax.fori_loop(j + 1, nk, functools.partial(q_tile, masked=False), acc)
            for hp in range(_PAIRS):
                cols = slice(128 * hp, 128 * (hp + 1))
                dk_ref[kr, cols] = acc[hp].astype(BF16)
                dv_ref[kr, cols] = acc[_PAIRS + hp].astype(BF16)
            for h in range(FOX_H):
                dfk_ref[h, pl.ds(j, 1), :] = acc[2 * _PAIRS + h]
            return carry

        lax.fori_loop(0, nk, kv_tile, 0)
        dq_ref[...] = dq_sc[...].astype(BF16)

    vm = pl.BlockSpec(memory_space=pltpu.VMEM)
    big = jax.ShapeDtypeStruct((S, FOX_W), BF16)
    return pl.pallas_call(
        body,
        name=name,
        in_specs=[vm] * 6,
        out_specs=[vm] * 5,
        out_shape=[big, big, big, jax.ShapeDtypeStruct((S, 128), F32), jax.ShapeDtypeStruct((FOX_H, nk, t), F32)],
        scratch_shapes=[pltpu.VMEM((S, FOX_W), BF16), pltpu.VMEM((S, FOX_W), BF16), pltpu.VMEM((FOX_H, S, 128), F32),
                        pltpu.VMEM((FOX_H, S, 128), F32), pltpu.VMEM((S, FOX_W), F32)],
        compiler_params=pltpu.CompilerParams(vmem_limit_bytes=VMEM_LIMIT),
    )(qkv, cum, fk3, o, do, lse)


def _group_mask(lane, gi):
    return (lane >= 64 * gi) & (lane < 64 * (gi + 1))


_U_COL = OFF_C // SGU_W


def _sgu_fwd(rest, gn, wm, bias, *, name):
    S = rest.shape[0]
    ts = _tile(S, 512)
    nc = ts // SGU_CHUNK

    def body(u_ref, v_ref, g_ref, w_ref, b_ref, o_ref):
        zv = _gelu(v_ref[...])
        vn = zv * lax.rsqrt(jnp.mean(zv * zv, axis=-1, keepdims=True) + EPS) * g_ref[...]
        lane = _lanes((SGU_CHUNK, SGU_W))
        for c in range(nc):
            rows = slice(c * SGU_CHUNK, (c + 1) * SGU_CHUNK)
            vcb = vn[rows].astype(BF16)
            mixed = b_ref[...]
            for gi in range(4):
                mixed = mixed + jnp.where(_group_mask(lane, gi), _dot(w_ref[gi], vcb, 1, 0), 0.0)
            o_ref[rows, :] = (_gelu(u_ref[rows, :]) * mixed).astype(BF16)

    return pl.pallas_call(
        body,
        name=name,
        grid=(S // ts,),
        in_specs=[
            pl.BlockSpec((ts, SGU_W), lambda i: (i, _U_COL)),
            pl.BlockSpec((ts, SGU_W), lambda i: (i, _U_COL + 1)),
            pl.BlockSpec((1, SGU_W), lambda i: (0, 0)),
            pl.BlockSpec((4, SGU_CHUNK, SGU_CHUNK), lambda i: (0, 0, 0)),
            pl.BlockSpec((SGU_CHUNK, SGU_W), lambda i: (0, 0)),
        ],
        out_specs=pl.BlockSpec((ts, SGU_W), lambda i: (i, 0)),
        out_shape=jax.ShapeDtypeStruct((S, SGU_W), BF16),
        compiler_params=_params(("parallel",)),
    )(rest, rest, gn.reshape(1, SGU_W), wm, bias)


def _sgu_bwd(rest, gn, wm, wm_t, bias, dsg, *, name):
    S = rest.shape[0]
    ts = _tile(S, 512)
    nc = ts // SGU_CHUNK

    def body(u_ref, v_ref, g_ref, w_ref, wt_ref, b_ref, dsg_ref, dc_ref, dw_ref, db_ref, dg_ref):
        first = pl.program_id(0) == 0

        @pl.when(first)
        def _():
            dw_ref[...] = jnp.zeros_like(dw_ref)
            db_ref[...] = jnp.zeros_like(db_ref)
            dg_ref[...] = jnp.zeros_like(dg_ref)

        gv = g_ref[...]
        lane = _lanes((SGU_CHUNK, SGU_W))
        for c in range(nc):
            rows = slice(c * SGU_CHUNK, (c + 1) * SGU_CHUNK)
            vpre = v_ref[rows, :]
            upre = u_ref[rows, :]
            zv = _gelu(vpre)
            r = lax.rsqrt(jnp.mean(zv * zv, axis=-1, keepdims=True) + EPS)
            zn = zv * r
            vcb = (zn * gv).astype(BF16)
            mixed = b_ref[...]
            for gi in range(4):
                mixed = mixed + jnp.where(_group_mask(lane, gi), _dot(w_ref[gi], vcb, 1, 0), 0.0)
            zu = _gelu(upre)
            dsg_v = dsg_ref[rows, :]
            dc_ref[rows, :SGU_W] = (dsg_v * mixed * _gelu_grad(upre)).astype(BF16)
            dmixed = dsg_v * zu
            db_ref[...] += dmixed
            dvn = jnp.zeros((SGU_CHUNK, SGU_W), F32)
            for gi in range(4):
                dmg = jnp.where(_group_mask(lane, gi), dmixed, 0.0).astype(BF16)
                dw_ref[gi] += _dot(dmg, vcb, 1, 1)
                dvn = dvn + _dot(wt_ref[gi], dmg, 1, 0)
            dg_ref[...] += jnp.sum(dvn * zn, axis=0, keepdims=True)
            dzn = dvn * gv
            dzv = r * (dzn - zn * jnp.mean(dzn * zn, axis=-1, keepdims=True))
            dc_ref[rows, SGU_W:] = (dzv * _gelu_grad(vpre)).astype(BF16)

    blk = pl.BlockSpec((ts, SGU_W), lambda i: (i, 0))
    vec = pl.BlockSpec((1, SGU_W), lambda i: (0, 0))
    w3 = pl.BlockSpec((4, SGU_CHUNK, SGU_CHUNK), lambda i: (0, 0, 0))
    bsp = pl.BlockSpec((SGU_CHUNK, SGU_W), lambda i: (0, 0))
    return pl.pallas_call(
        body,
        name=name,
        grid=(S // ts,),
        in_specs=[
            pl.BlockSpec((ts, SGU_W), lambda i: (i, _U_COL)),
            pl.BlockSpec((ts, SGU_W), lambda i: (i, _U_COL + 1)),
            vec, w3, w3, bsp, blk,
        ],
        out_specs=[pl.BlockSpec((ts, 2 * SGU_W), lambda i: (i, 0)), w3, bsp, vec],
        out_shape=[
            jax.ShapeDtypeStruct((S, 2 * SGU_W), BF16),
            jax.ShapeDtypeStruct((4, SGU_CHUNK, SGU_CHUNK), F32),
            jax.ShapeDtypeStruct((SGU_CHUNK, SGU_W), F32),
            jax.ShapeDtypeStruct((1, SGU_W), F32),
        ],
        compiler_params=_params(("arbitrary",)),
    )(rest, rest, gn.reshape(1, SGU_W), wm, wm_t, bias, dsg)


_GT = 512
_G0 = OFF_G // _GT


def _gate_specs(tm, col_of):
    specs = [pl.BlockSpec((tm, _GT), functools.partial(lambda k, *ids: (col_of(*ids)[0], _G0 + 2 * k + col_of(*ids)[1]), k)) for k in range(3)]
    specs += [pl.BlockSpec((1, _GT), functools.partial(lambda k, *ids: (0, 2 * k + col_of(*ids)[1]), k)) for k in range(3)]
    return specs


def _merge_fwd(rest, bg, ya, yb, yc, *, name):
    S = rest.shape[0]
    tm = _tile(S, 512)

    def body(g1, g2, g3, b1, b2, b3, ya_ref, yb_ref, yc_ref, o_ref):
        acc = _sigmoid(g1[...] + b1[...]) * ya_ref[...]
        acc = acc + _sigmoid(g2[...] + b2[...]) * yb_ref[...]
        acc = acc + _sigmoid(g3[...] + b3[...]) * yc_ref[...]
        o_ref[...] = acc.astype(BF16)

    blk = pl.BlockSpec((tm, _GT), lambda i, j: (i, j))
    return pl.pallas_call(
        body,
        name=name,
        grid=(S // tm, D // _GT),
        in_specs=_gate_specs(tm, lambda i, j: (i, j)) + [blk, blk, blk],
        out_specs=blk,
        out_shape=jax.ShapeDtypeStruct((S, D), BF16),
        compiler_params=_params(("parallel", "parallel")),
    )(rest, rest, rest, bg, bg, bg, ya, yb, yc)


def _merge_bwd(rest, bg, ya, yb, yc, dm, *, name):
    S = rest.shape[0]
    tm = _tile(S, 512)

    def body(g1, g2, g3, b1, b2, b3, ya_ref, yb_ref, yc_ref, dm_ref, dya, dyb, dyc, dg1, dg2, dg3, db1, db2, db3):
        first = pl.program_id(1) == 0
        dmv = dm_ref[...]
        for g_ref, b_ref, y_ref, dy_ref, dg_ref, db_ref in (
            (g1, b1, ya_ref, dya, dg1, db1), (g2, b2, yb_ref, dyb, dg2, db2), (g3, b3, yc_ref, dyc, dg3, db3)):
            gate = _sigmoid(g_ref[...] + b_ref[...])
            dy_ref[...] = (dmv * gate).astype(BF16)
            dpre = dmv * y_ref[...] * gate * (1.0 - gate)
            dg_ref[...] = dpre.astype(BF16)
            part = jnp.sum(dpre, axis=0, keepdims=True)

            @pl.when(first)
            def _():
                db_ref[...] = part

            @pl.when(jnp.logical_not(first))
            def _():
                db_ref[...] += part

    blk = pl.BlockSpec((tm, _GT), lambda j, i: (i, j))
    vec = pl.BlockSpec((1, _GT), lambda j, i: (0, j))
    big = jax.ShapeDtypeStruct((S, D), BF16)
    small = jax.ShapeDtypeStruct((1, D), F32)
    return pl.pallas_call(
        body,
        name=name,
        grid=(D // _GT, S // tm),
        in_specs=_gate_specs(tm, lambda j, i: (i, j)) + [blk, blk, blk, blk],
        out_specs=[blk] * 6 + [vec] * 3,
        out_shape=[big] * 6 + [small] * 3,
        compiler_params=_params(("parallel", "arbitrary")),
    )(rest, rest, rest, bg, bg, bg, ya, yb, yc, dm)


_X_SCALE = XDH ** -0.5


def _xattn_fwd(xq, kv, *, name):
    S = xq.shape[0]
    M = kv.shape[0]
    tq = _tile(S, 512)

    def body(q_ref, k_ref, v_ref, o_ref):
        s = _dot(q_ref[...], k_ref[...], 1, 1) * _X_SCALE
        e = jnp.exp(s - jnp.max(s, axis=-1, keepdims=True))
        p = e / jnp.sum(e, axis=-1, keepdims=True)
        o_ref[...] = _dot(p.astype(BF16), v_ref[...], 1, 0).astype(BF16)

    return pl.pallas_call(
        body,
        name=name,
        grid=(S // tq, XH),
        in_specs=[
            pl.BlockSpec((tq, XDH), lambda i, h: (i, h)),
            pl.BlockSpec((M, XDH), lambda i, h: (0, h)),
            pl.BlockSpec((M, XDH), lambda i, h: (0, XH + h)),
        ],
        out_specs=pl.BlockSpec((tq, XDH), lambda i, h: (i, h)),
        out_shape=jax.ShapeDtypeStruct((S, D), BF16),
        compiler_params=_params(("parallel", "parallel")),
    )(xq, kv, kv)


def _xattn_bwd(xq, kv, do, *, name):
    S = xq.shape[0]
    M = kv.shape[0]
    tq = _tile(S, 512)

    def body(q_ref, k_ref, v_ref, do_ref, dq_ref, dk_ref, dv_ref):
        qb = q_ref[...]
        kb = k_ref[...]
        dob = do_ref[...]
        s = _dot(qb, kb, 1, 1) * _X_SCALE
        e = jnp.exp(s - jnp.max(s, axis=-1, keepdims=True))
        p = e / jnp.sum(e, axis=-1, keepdims=True)
        dp = _dot(dob, v_ref[...], 1, 1)
        ds = (p * (dp - jnp.sum(p * dp, axis=-1, keepdims=True)) * _X_SCALE).astype(BF16)
        dq_ref[...] = _dot(ds, kb, 1, 0).astype(BF16)
        dk_part = _dot(ds, qb, 0, 0)
        dv_part = _dot(p.astype(BF16), dob, 0, 0)

        @pl.when(pl.program_id(1) == 0)
        def _():
            dk_ref[...] = dk_part
            dv_ref[...] = dv_part

        @pl.when(pl.program_id(1) > 0)
        def _():
            dk_ref[...] += dk_part
            dv_ref[...] += dv_part

    qspec = pl.BlockSpec((tq, XDH), lambda h, i: (i, h))
    kspec = pl.BlockSpec((M, XDH), lambda h, i: (0, h))
    dxq, dxk, dxv = pl.pallas_call(
        body,
        name=name,
        grid=(XH, S // tq),
        in_specs=[qspec, kspec, pl.BlockSpec((M, XDH), lambda h, i: (0, XH + h)), qspec],
        out_specs=[qspec, kspec, kspec],
        out_shape=[jax.ShapeDtypeStruct((S, D), BF16), jax.ShapeDtypeStruct((M, D), F32), jax.ShapeDtypeStruct((M, D), F32)],
        compiler_params=_params(("parallel", "arbitrary")),
    )(xq, kv, kv, do)
    return dxq, jnp.concatenate([dxk, dxv], axis=1)


def _adam_math(w, g, m, v):
    m = ADAM_B1 * m + (1.0 - ADAM_B1) * g
    v = ADAM_B2 * v + (1.0 - ADAM_B2) * (g * g)
    m_hat = m / (1.0 - ADAM_B1 ** ADAM_STEP)
    v_hat = v / (1.0 - ADAM_B2 ** ADAM_STEP)
    delta = -ADAM_LR * (m_hat / (jnp.sqrt(v_hat) + ADAM_EPS) + ADAM_WD * w)
    return delta, m, v


def _adamw_sharded(parts, w, m, v, *, name):
    _, R, C = w.shape
    tm = _tile(R, 256)
    nr = R // tm

    def body(p0_ref, p1_ref, w_ref, m_ref, v_ref, g_ref, d_ref, mo_ref, vo_ref):
        def update(p_ref):
            g = p_ref[0].astype(F32)
            for dev in range(1, N_DEV):
                g = g + p_ref[dev].astype(F32)
            delta, mn, vn = _adam_math(w_ref[...], g, m_ref[...], v_ref[...])
            g_ref[...] = g
            d_ref[...] = delta
            mo_ref[...] = mn
            vo_ref[...] = vn

        @pl.when(pl.program_id(0) == 0)
        def _():
            update(p0_ref)

        @pl.when(pl.program_id(0) == 1)
        def _():
            update(p1_ref)

    p0 = pl.BlockSpec((N_DEV, tm, C), lambda l, i: (0, i * (1 - l) + (nr - 1) * l, 0))
    p1 = pl.BlockSpec((N_DEV, tm, C), lambda l, i: (0, i * l, 0))
    blk = pl.BlockSpec((None, tm, C), lambda l, i: (l, i, 0))
    sds = jax.ShapeDtypeStruct(w.shape, F32)
    return pl.pallas_call(
        body,
        name=name,
        grid=(DEPTH, nr),
        in_specs=[p0, p1, blk, blk, blk],
        out_specs=[blk] * 4,
        out_shape=[sds] * 4,
        compiler_params=_params(("arbitrary", "arbitrary")),
    )(parts[0], parts[1], w, m, v)


def _adamw_small(g, w, m, v, *, name):
    n = len(g)

    def body(*refs):
        g_refs, w_refs, m_refs, v_refs = (refs[k * n:(k + 1) * n] for k in range(4))
        d_out, m_out, v_out = (refs[(4 + k) * n:(5 + k) * n] for k in range(3))
        for t in range(n):
            delta, mn, vn = _adam_math(w_refs[t][...], g_refs[t][...], m_refs[t][...], v_refs[t][...])
            d_out[t][...] = delta
            m_out[t][...] = mn
            v_out[t][...] = vn

    vm = pl.BlockSpec(memory_space=pltpu.VMEM)
    shapes = [jax.ShapeDtypeStruct(a.shape, F32) for a in w]
    outs = pl.pallas_call(
        body,
        name=name,
        in_specs=[vm] * (4 * n),
        out_specs=[vm] * (3 * n),
        out_shape=shapes * 3,
        compiler_params=pltpu.CompilerParams(vmem_limit_bytes=VMEM_LIMIT),
    )(*g, *w, *m, *v)
    return outs[:n], outs[n:2 * n], outs[2 * n:]


def _position():
    return lax.axis_index("x"), lax.axis_index("y"), lax.axis_index("c")


def _dev_index(px, py, pc):
    return 4 * px + 2 * py + pc


_ANY = pl.BlockSpec(memory_space=pl.ANY)


def _peers(x, y, c):
    out = []
    for mask in range(1, N_DEV):
        fx, fy, fc = (mask >> 2) & 1, (mask >> 1) & 1, mask & 1
        out.append((1 - x if fx else x, 1 - y if fy else y, 1 - c if fc else c))
    return out


_HBM = pl.BlockSpec(memory_space=pltpu.HBM)
_SEM = pl.BlockSpec(memory_space=pltpu.SEMAPHORE)


def _own_block_placed(block, like):
    x, y, c = _position()
    return lax.dynamic_update_index_in_dim(lax.empty(like.shape, like.dtype), block, _dev_index(x, y, c), 0)


_COPY_BYTES = 256 << 10
_MAX_PIECES = 8


def _pieces(blocks):
    out = []
    for t, b in enumerate(blocks):
        R, C = b.shape[-2:]
        n = max(1, min(_MAX_PIECES, R * C * jnp.dtype(b.dtype).itemsize // _COPY_BYTES))
        while n > 1 and R % (16 * n):
            n -= 1
        out += [(t, pl.ds(j * (R // n), R // n) if n > 1 else None) for j in range(n)]
    return out


def _cut(block, rows):
    return block if rows is None else block.at[rows]


def _copies(per_piece):
    def mark(fn):
        fn.per_piece = per_piece
        return fn
    return mark


@_copies(N_DEV - 1)
def _plan_exchange(srcs, lands, send_sems, recv_sems, arrivals):
    x, y, c = _position()
    me = _dev_index(x, y, c)
    out = []
    for k, peer in enumerate(_peers(x, y, c)):
        p = _dev_index(*peer)
        for i, (t, rows) in enumerate(_pieces(lands)):
            sems = dict(send_sem=send_sems.at[7 * i + k], recv_sem=recv_sems.at[7 * i + k], device_id=peer, device_id_type=MESH)
            src, dst = (lands[t].at[p], lands[t].at[p]) if arrivals else (srcs[t].at[p], lands[t].at[me])
            out.append(pltpu.make_async_remote_copy(src_ref=_cut(src, rows), dst_ref=_cut(dst, rows), **sems))
    return out


@_copies(N_DEV - 1)
def _plan_broadcast(srcs, lands, send_sems, recv_sems, arrivals):
    x, y, c = _position()
    me = _dev_index(x, y, c)
    out = []
    for k, peer in enumerate(_peers(x, y, c)):
        p = _dev_index(*peer)
        for i, (t, rows) in enumerate(_pieces(lands)):
            sems = dict(send_sem=send_sems.at[7 * i + k], recv_sem=recv_sems.at[7 * i + k], device_id=peer, device_id_type=MESH)
            src, dst = (lands[t].at[p], lands[t].at[p]) if arrivals else (srcs[t], lands[t].at[me])
            out.append(pltpu.make_async_remote_copy(src_ref=_cut(src, rows), dst_ref=_cut(dst, rows), **sems))
    return out


@_copies(4)
def _plan_gather_out(srcs, lands, send_sems, recv_sems, arrivals):
    x, y, c = _position()
    me = _dev_index(x, y, c)
    out = []
    for k, peer in enumerate([(x, y, 1 - c), (1 - x, y, c), (x, 1 - y, c), (1 - x, 1 - y, c)]):
        p = _dev_index(*peer)
        for i, (t, rows) in enumerate(_pieces(lands)):
            sems = dict(send_sem=send_sems.at[4 * i + k], recv_sem=recv_sems.at[4 * i + k], device_id=peer, device_id_type=MESH)
            src, dst = (lands[t].at[p], lands[t].at[p]) if arrivals else (srcs[t], lands[t].at[me])
            out.append(pltpu.make_async_remote_copy(src_ref=_cut(src, rows), dst_ref=_cut(dst, rows), **sems))
    return out


@_copies(3)
def _plan_gather_pass(srcs, lands, send_sems, recv_sems, arrivals):
    x, y, c = _position()
    sibling = (x, y, 1 - c)
    out = []
    for k, chip in enumerate([(1 - x, y), (x, 1 - y), (1 - x, 1 - y)]):
        p = _dev_index(*chip, 1 - c) if arrivals else _dev_index(*chip, c)
        for i, (t, rows) in enumerate(_pieces(lands)):
            sems = dict(send_sem=send_sems.at[3 * i + k], recv_sem=recv_sems.at[3 * i + k], device_id=sibling, device_id_type=MESH)
            block = _cut(lands[t].at[p], rows)
            out.append(pltpu.make_async_remote_copy(src_ref=block, dst_ref=block, **sems))
    return out


def _split_start(plan, srcs, lands, *, after=None, name):
    n_src, n = len(srcs), len(srcs) + len(lands)
    n_sem = plan.per_piece * len(_pieces(lands))
    order = [] if after is None else [after]

    def body(*refs):
        send_sems, recv_sems = refs[n + len(order):n + len(order) + 2]
        token = refs[-1]
        for cp in plan(refs[:n_src], refs[n_src:n], send_sems, recv_sems, arrivals=False):
            cp.start()
        token[...] = jnp.zeros_like(token)

    hbm = lambda a: pltpu.HBM(a.shape, a.dtype)
    outs = pl.pallas_call(
        body,
        name=name,
        in_specs=[_HBM] * n + [_ANY] * len(order),
        out_specs=[_SEM, _SEM] + [_HBM] * n + [pl.BlockSpec(memory_space=pltpu.VMEM)],
        out_shape=[pltpu.SemaphoreType.DMA((n_sem,)), pltpu.SemaphoreType.DMA((n_sem,))] + [hbm(a) for a in (*srcs, *lands)]
        + [jax.ShapeDtypeStruct(_TOKEN, F32)],
        input_output_aliases={i: 2 + i for i in range(n)},
        compiler_params=pltpu.CompilerParams(has_side_effects=pltpu.SideEffectType.DATAFLOW_SIDE_EFFECTING),
    )(*[pltpu.with_memory_space_constraint(a, pltpu.HBM) for a in (*srcs, *lands)], *order)
    return (outs[0], outs[1], outs[2:2 + n_src], outs[2 + n_src:2 + n]), outs[-1]


def _split_wait(plan, state, after, *, name):
    send_sems, recv_sems, srcs, lands = state
    n_src, n = len(srcs), len(srcs) + len(lands)

    def body(*refs):
        send_refs, recv_refs = refs[n:n + 2]
        for cp in plan(refs[:n_src], refs[n_src:n], send_refs, recv_refs, arrivals=False):
            cp.wait_send()
        for cp in plan(refs[:n_src], refs[n_src:n], send_refs, recv_refs, arrivals=True):
            cp.wait_recv()

    hbm = lambda a: pltpu.HBM(a.shape, a.dtype)
    outs = pl.pallas_call(
        body,
        name=name,
        in_specs=[_HBM] * n + [_SEM, _SEM, _ANY],
        out_specs=[_HBM] * n,
        out_shape=[hbm(a) for a in (*srcs, *lands)],
        input_output_aliases={i: i for i in range(n)},
        compiler_params=pltpu.CompilerParams(has_side_effects=pltpu.SideEffectType.DATAFLOW_SIDE_EFFECTING),
    )(*srcs, *lands, send_sems, recv_sems, after)
    return outs[n_src:]


def _sum_blocks(blocks, *, name):
    _, R, C = blocks.shape
    tm = next(R // n for n in (4, 3, 2, 1) if R % (8 * n) == 0)

    def body(b_ref, o_ref):
        g = b_ref[0]
        for dev in range(1, N_DEV):
            g = g + b_ref[dev]
        o_ref[...] = g

    return pl.pallas_call(
        body,
        name=name,
        grid=(R // tm,),
        in_specs=[pl.BlockSpec((N_DEV, tm, C), lambda i: (0, i, 0))],
        out_specs=pl.BlockSpec((tm, C), lambda i: (i, 0)),
        out_shape=jax.ShapeDtypeStruct((R, C), F32),
        compiler_params=_params(("parallel",)),
    )(blocks)


def _block_diag(w):
    out = jnp.zeros((POOL_W, POOL_W), w.dtype)
    for gi in range(4):
        out = out.at[64 * gi:64 * (gi + 1), 64 * gi:64 * (gi + 1)].set(w[gi])
    return out


def _layer_consts(sp, l):
    causal = jnp.tril(jnp.ones((SGU_CHUNK, SGU_CHUNK), F32))
    wm = (sp["sgu_w"][l] * causal[None]).astype(BF16)
    wbd = _block_diag(sp["pool_w"][l]).astype(BF16)
    return dict(
        wbd=wbd, wbd_t=wbd.T, wm=wm, wm_t=wm.transpose(0, 2, 1),
        sgu_bias=jnp.repeat(sp["sgu_b"][l].T, 64, axis=1),
        bpad=jnp.pad(sp["b_forget"][l], (0, F_LANES - FOX_H)).reshape(1, F_LANES),
        bg=sp["b_gate"][l].reshape(1, 3 * D),
    )


def _relu2(acc):
    return acc, jnp.square(jnp.maximum(acc, 0.0))


def _relu2_grad(acc, z):
    return (acc * 2.0 * jnp.maximum(z, 0.0),)


def _layer_fwd(l, x, h, mem, source, sp):
    S = x.shape[0]
    t = _tile(S, 256)
    c = _layer_consts(sp, l)
    n = f"l{l}_"
    W, after = source(l, "begin", x)
    if h is None:
        h, after = _rms_fwd(x, sp["norm_mix_g"][l], after=after, name=n + "norm_mix"), None
    hm = _rms_fwd(mem, sp["norm_mem_g"][l], name=n + "norm_mem")
    more, token = source(l, "normed", hm)
    W.update(more)
    qkv = _mm(h, W["qkv"], out_dtypes=(BF16,), after=after if token is None else token, name=n + "qkv")
    rest = _mm(h, W["rest"], name=n + "rest")
    pa = _pool_fwd(rest, c["wbd"], sp["pool_scale"][l], name=n + "pool")
    cum, cum_t = _fox_prep(rest, c["bpad"], name=n + "fox_prep")
    fk3 = cum_t[:FOX_H].reshape(FOX_H, S // t, t)
    o, lse = _fox_fwd(qkv, cum, fk3, name=n + "fox")
    more, _ = source(l, "attended", o)
    W.update(more)
    sg = _sgu_fwd(rest, sp["sgu_norm_g"][l], c["wm"], c["sgu_bias"], name=n + "sgu")
    more, after = source(l, "mixed", sg)
    W.update(more)
    ya = _mm(pa, W["ba"], out_dtypes=(BF16,), after=after, name=n + "branch_a")
    yb = _mm(o, W["bb"], out_dtypes=(BF16,), name=n + "branch_b")
    yc = _mm(sg, W["bc"], out_dtypes=(BF16,), name=n + "branch_c")
    merged = _merge_fwd(rest, c["bg"], ya, yb, yc, name=n + "merge")
    whole_rows = dict(epilogue=_add_norm, out_dtypes=(F32, BF16), tm=1024, tn=D)
    x1, hx = _mm(merged, W["out"], extras=(x,), row_extras=(sp["norm_xattn_g"][l].reshape(1, D),), name=n + "out", **whole_rows)
    xq = _mm(hx, W["xq"], out_dtypes=(BF16,), name=n + "xq")
    kv = _mm(hm, W["xkv"], out_dtypes=(BF16,), name=n + "xkv")
    o2 = _xattn_fwd(xq, kv, name=n + "xattn")
    x2, hf = _mm(o2, W["xo"], extras=(x1,), row_extras=(sp["norm_ffn_g"][l].reshape(1, D),), name=n + "xo", **whole_rows)
    z, act = _mm(hf, W["ff1"], epilogue=_relu2, out_dtypes=(BF16, BF16), name=n + "ff1")
    _, after = source(l, "expanded", act)
    if l + 1 < DEPTH:
        x3, h_next = _mm(act, W["ff2"], extras=(x2,), row_extras=(sp["norm_mix_g"][l + 1].reshape(1, D),), after=after, name=n + "ff2",
                         **whole_rows)
    else:
        x3, h_next = _mm(act, W["ff2"], extras=(x2,), epilogue=_add, after=after, name=n + "ff2"), None
    saved = dict(x=x, h=h, qkv=qkv, rest=rest, pa=pa, cum=cum, fk3=fk3, o=o, lse=lse, sg=sg, ya=ya, yb=yb, yc=yc,
                 merged=merged, x1=x1, hx=hx, hm=hm, xq=xq, kv=kv, o2=o2, x2=x2, hf=hf, z=z, act=act, c=c)
    return x3, h_next, saved, W


def _layer_bwd(l, dx3, sv, mem, W, sp, grads_done):
    S = dx3.shape[0]
    c = sv["c"]
    n = f"l{l}b_"
    bf = dict(out_dtypes=(BF16,))
    gw, gs = {}, {}
    gw["ff2"] = _mm(sv["act"], dx3, ta=True, name=n + "dw_ff2", **bf)
    dz = _mm(dx3, W["ff2"], tb=True, extras=(sv["z"],), epilogue=_relu2_grad, name=n + "dz", **bf)
    gw["ff1"] = _mm(sv["hf"], dz, ta=True, shard_out=True, name=n + "dw_ff1", **bf)
    whole_rows = dict(epilogue=_norm_grad, out_dtypes=(F32, F32), row_outs=1, tm=1024, tn=D)
    gain = lambda key: (sp[key][l].reshape(1, D),)
    dx2, dg = _mm(dz, W["ff1"], tb=True, extras=(sv["x2"], dx3), row_extras=gain("norm_ffn_g"), name=n + "dhf", **whole_rows)
    gs["norm_ffn_g"] = dg.reshape(D)
    gw["xo"] = _mm(sv["o2"], dx2, ta=True, name=n + "dw_xo", **bf)
    do2 = _mm(dx2, W["xo"], tb=True, name=n + "do2", **bf)
    dxq, dkv = _xattn_bwd(sv["xq"], sv["kv"], do2, name=n + "dxattn")
    gw["xq"] = _mm(sv["hx"], dxq, ta=True, name=n + "dw_xq", **bf)
    gw["xkv"] = _mm(sv["hm"], dkv, ta=True, shard_out=True, name=n + "dw_xkv", **bf)
    dhm = _mm(dkv, W["xkv"], tb=True, name=n + "dhm")
    _, gs["norm_mem_g"] = _rms_bwd(mem, sp["norm_mem_g"][l], dhm, jnp.zeros_like(mem), name=n + "dnorm_mem")
    dx1, dg = _mm(dxq, W["xq"], tb=True, extras=(sv["x1"], dx2), row_extras=gain("norm_xattn_g"), name=n + "dhx", **whole_rows)
    gs["norm_xattn_g"] = dg.reshape(D)
    after, gw = grads_done(l, gw), {}
    gw["out"] = _mm(sv["merged"], dx1, ta=True, name=n + "dw_out", **bf)
    dm = _mm(dx1, W["out"], tb=True, after=after, name=n + "dmerged")
    dya, dyb, dyc, dg1, dg2, dg3, db1, db2, db3 = _merge_bwd(sv["rest"], c["bg"], sv["ya"], sv["yb"], sv["yc"], dm, name=n + "dmerge")
    gs["b_gate"] = jnp.concatenate([db1, db2, db3], axis=1).reshape(3 * D)
    gw["ba"] = _mm(sv["pa"], dya, ta=True, shard_out=True, name=n + "dw_ba", **bf)
    gw["bb"] = _mm(sv["o"], dyb, ta=True, shard_out=True, name=n + "dw_bb", **bf)
    gw["bc"] = _mm(sv["sg"], dyc, ta=True, shard_out=True, name=n + "dw_bc", **bf)
    after, gw = grads_done(l, gw), {}
    dpa = _mm(dya, W["ba"], tb=True, name=n + "dpa")
    do = _mm(dyb, W["bb"], tb=True, after=after, name=n + "do", **bf)
    dsg = _mm(dyc, W["bc"], tb=True, name=n + "dsg")
    da, dwbd, dscale = _pool_bwd(sv["rest"], c["wbd"], c["wbd_t"], sp["pool_scale"][l], dpa, name=n + "dpool")
    gs["pool_w"] = jnp.stack([dwbd[64 * gi:64 * (gi + 1), 64 * gi:64 * (gi + 1)] for gi in range(4)])
    gs["pool_scale"] = dscale.reshape(POOL_W)
    dq, dk, dv, dfq, dfk = _fox_bwd(sv["qkv"], sv["cum"], sv["fk3"], sv["o"], do, sv["lse"], name=n + "dfox")
    dcum = dfq + jnp.pad(dfk.reshape(FOX_H, S).T, ((0, 0), (0, F_LANES - FOX_H)))
    df, dbf = _fox_post(sv["rest"], c["bpad"], dcum, name=n + "dfox_post")
    gs["b_forget"] = dbf[0, :FOX_H]
    dc, dwm, dbias, dgn = _sgu_bwd(sv["rest"], sp["sgu_norm_g"][l], c["wm"], c["wm_t"], c["sgu_bias"], dsg, name=n + "dsgu")
    gs["sgu_w"] = dwm * jnp.tril(jnp.ones((SGU_CHUNK, SGU_CHUNK), F32))[None]
    gs["sgu_b"] = dbias.reshape(SGU_CHUNK, 4, 64).sum(axis=2).T
    gs["sgu_norm_g"] = dgn.reshape(SGU_W)
    dqkv = [dq, dk, dv]
    drest = [jnp.concatenate([da, df, jnp.zeros((S, OFF_C - OFF_F - F_LANES), BF16), dc], axis=1), dg1, dg2, dg3]
    gw["qkv"] = _mm(sv["h"], dqkv, ta=True, name=n + "dw_qkv", **bf)
    gw["rest"] = _mm(sv["h"], drest, ta=True, name=n + "dw_rest", **bf)
    after = grads_done(l, gw)
    dh = _mm(dqkv, W["qkv"], tb=True, after=after, name=n + "dh_qkv")
    dx, dg = _mm(drest, W["rest"], tb=True, extras=(dh, sv["x"], dx1), row_extras=gain("norm_mix_g"), name=n + "dh",
                 **{**whole_rows, "epilogue": _add_norm_grad, "tm": 512})
    gs["norm_mix_g"] = dg.reshape(D)
    return dx, gs


def _local_step(x, mem, target, sp, source, grads_done):
    saved, Ws, h = [], [], None
    for l in range(DEPTH):
        x, h, sv, W = _layer_fwd(l, x, h, mem, source, sp)
        saved.append(sv)
        Ws.append(W)
    loss, dx, dgf = _final_loss(x, sp["final_norm_g"], target, name="final_loss")
    gss = [None] * DEPTH
    for l in reversed(range(DEPTH)):
        dx, gss[l] = _layer_bwd(l, dx, saved[l], mem, Ws[l], sp, grads_done)
    small = {k: jnp.stack([gss[l][k] for l in range(DEPTH)]) for k in gss[0]}
    small["final_norm_g"] = dgf
    return loss, dx, small


_SMALL = ["norm_mix_g", "b_forget", "pool_w", "pool_scale", "sgu_norm_g", "sgu_w", "sgu_b", "b_gate", "norm_xattn_g",
          "norm_mem_g", "norm_ffn_g", "final_norm_g"]
_COL = {"w_branch_a": "ba", "w_branch_b": "bb", "w_branch_c": "bc", "w_xkv": "xkv", "w_ff1": "ff1"}
_ROW = {"w_out": "out", "w_xq": "xq", "w_xo": "xo", "w_ff2": "ff2"}
_BIG = ["w_in", "w_branch_a", "w_branch_b", "w_branch_c", "w_out", "w_xq", "w_xkv", "w_xo", "w_ff1", "w_ff2"]
_PACK_LANES = 128


def _as_rows(a):
    return a.reshape(-1, a.shape[-1])


def _pack(tensors):
    rows = []
    for a in tensors:
        flat = a.reshape(-1)
        flat = jnp.pad(flat, (0, (-flat.shape[0]) % (8 * _PACK_LANES)))
        rows.append(flat.reshape(-1, _PACK_LANES))
    n_rows = sum(r.shape[0] for r in rows)
    rows.append(jnp.zeros(((-n_rows) % (8 * N_DEV), _PACK_LANES), F32))
    return jnp.concatenate(rows, axis=0)


def _unpack(packed, like):
    out, r = [], 0
    for a in like:
        size = math.prod(a.shape)
        nr = 8 * (-(-size // (8 * _PACK_LANES)))
        out.append(packed[r:r + nr].reshape(-1)[:size].reshape(a.shape))
        r += nr
    return out


_SHARD_IN = N_IN // N_DEV


def _columns(pieces, start, stop):
    out, at = [], 0
    for p in pieces:
        lo, hi = max(start, at), min(stop, at + p.shape[1])
        if lo < hi:
            out.append(p[:, lo - at:hi - at])
        at += p.shape[1]
    return out


def _split_w_in(blocks):
    K = blocks[0].shape[0]
    pad = jnp.zeros((K, OFF_C - OFF_F - FOX_H), blocks[0].dtype)
    cols = functools.partial(_columns, blocks)
    rest = jnp.concatenate(cols(0, R_OFF_Q) + cols(R_OFF_F, R_OFF_C) + [pad] + cols(R_OFF_C, N_IN), axis=1)
    return jnp.concatenate(cols(R_OFF_Q, R_OFF_F), axis=1), rest


def _join_w_in(qkv, rest):
    in_order = [rest[:, :R_OFF_Q], qkv, rest[:, OFF_F:OFF_F + FOX_H], rest[:, OFF_C:]]
    return jnp.stack([jnp.concatenate(_columns(in_order, _SHARD_IN * d, _SHARD_IN * (d + 1)), axis=1) for d in range(N_DEV)])


_FIRST = ["w_in"]
_LATER = [k for k in _BIG if k not in _FIRST]


def _layer_weights(gathered):
    W = {}
    if "w_in" in gathered:
        W.update(zip(("qkv", "rest"), _split_w_in([gathered["w_in"][d] for d in range(N_DEV)])))
    for name, key in _COL.items():
        if name in gathered:
            W[key] = _Gathered(gathered[name])
    if "xkv" in W:
        W["xkv"] = W["xkv"].arr.transpose(1, 0, 2).reshape(D, -1)
    for name, key in _ROW.items():
        if name in gathered:
            W[key] = gathered[name].reshape(-1, gathered[name].shape[-1])
    return W


def _grad_blocks(gw):
    parts = {}
    if "qkv" in gw:
        parts["w_in"] = _join_w_in(gw["qkv"], gw["rest"])
    for name, key in _COL.items():
        if key in gw:
            parts[name] = gw[key]
    for name, key in _ROW.items():
        if key in gw:
            parts[name] = gw[key].reshape(N_DEV, -1, gw[key].shape[-1])
    return parts


def kernel(x, mem, norm_mix_g, w_in, b_forget, pool_w, pool_scale, sgu_norm_g, sgu_w, sgu_b, w_branch_a, w_branch_b, w_branch_c, b_gate, w_out, norm_xattn_g, norm_mem_g, w_xq, w_xkv, w_xo, norm_ffn_g, w_ff1, w_ff2, final_norm_g, loss_target, m_norm_mix_g, m_w_in, m_b_forget, m_pool_w, m_pool_scale, m_sgu_norm_g, m_sgu_w, m_sgu_b, m_w_branch_a, m_w_branch_b, m_w_branch_c, m_b_gate, m_w_out, m_norm_xattn_g, m_norm_mem_g, m_w_xq, m_w_xkv, m_w_xo, m_norm_ffn_g, m_w_ff1, m_w_ff2, m_final_norm_g, v_norm_mix_g, v_w_in, v_b_forget, v_pool_w, v_pool_scale, v_sgu_norm_g, v_sgu_w, v_sgu_b, v_w_branch_a, v_w_branch_b, v_w_branch_c, v_b_gate, v_w_out, v_norm_xattn_g, v_norm_mem_g, v_w_xq, v_w_xkv, v_w_xo, v_norm_ffn_g, v_w_ff1, v_w_ff2, v_final_norm_g):
    names = ["norm_mix_g", "w_in", "b_forget", "pool_w", "pool_scale", "sgu_norm_g", "sgu_w", "sgu_b", "w_branch_a", "w_branch_b",
             "w_branch_c", "b_gate", "w_out", "norm_xattn_g", "norm_mem_g", "w_xq", "w_xkv", "w_xo", "norm_ffn_g", "w_ff1", "w_ff2",
             "final_norm_g"]
    w = dict(zip(names, [norm_mix_g, w_in, b_forget, pool_w, pool_scale, sgu_norm_g, sgu_w, sgu_b, w_branch_a, w_branch_b, w_branch_c,
                         b_gate, w_out, norm_xattn_g, norm_mem_g, w_xq, w_xkv, w_xo, norm_ffn_g, w_ff1, w_ff2, final_norm_g]))
    m = dict(zip(names, [m_norm_mix_g, m_w_in, m_b_forget, m_pool_w, m_pool_scale, m_sgu_norm_g, m_sgu_w, m_sgu_b, m_w_branch_a,
                         m_w_branch_b, m_w_branch_c, m_b_gate, m_w_out, m_norm_xattn_g, m_norm_mem_g, m_w_xq, m_w_xkv, m_w_xo,
                         m_norm_ffn_g, m_w_ff1, m_w_ff2, m_final_norm_g]))
    v = dict(zip(names, [v_norm_mix_g, v_w_in, v_b_forget, v_pool_w, v_pool_scale, v_sgu_norm_g, v_sgu_w, v_sgu_b, v_w_branch_a,
                         v_w_branch_b, v_w_branch_c, v_b_gate, v_w_out, v_norm_xattn_g, v_norm_mem_g, v_w_xq, v_w_xkv, v_w_xo,
                         v_norm_ffn_g, v_w_ff1, v_w_ff2, v_final_norm_g]))

    sp = {k: w[k] for k in _SMALL}
    shards = [{k: w[k][l].astype(BF16) for k in _BIG} for l in range(DEPTH)]
    me = _dev_index(*_position())

    def gather_out(l, keys, name, after=None):
        srcs = [shards[l][k] for k in keys]
        lands = [_own_block_placed(a, jax.ShapeDtypeStruct((N_DEV, *a.shape), a.dtype)) for a in srcs]
        state, token = _split_start(_plan_gather_out, srcs, lands, after=after, name=name + "_out_start")
        return (keys, name, state), token

    def gather_pass(job, value):
        keys, name, state = job
        lands = _split_wait(_plan_gather_out, state, value, name=name + "_out_wait")
        state, token = _split_start(_plan_gather_pass, [], lands, name=name + "_pass_start")
        return (keys, name, state), token, lands[0]

    def gather_end(job, value):
        keys, name, state = job
        return _layer_weights(dict(zip(keys, _split_wait(_plan_gather_pass, state, value, name=name + "_pass_wait"))))

    jobs = {}

    def source(l, point, value):
        if (l, point) == (0, "begin"):
            jobs["l0_first"], token = gather_out(0, _FIRST, "gather_l0_first")
            return {}, token
        if (l, point) == (0, "normed"):
            jobs["l0_first"], token, arrived = gather_pass(jobs["l0_first"], value)
            jobs["l0"], _ = gather_out(0, _LATER, "gather_l0", after=arrived)
            return gather_end(jobs.pop("l0_first"), token), None
        if (l, point) == (0, "attended"):
            jobs["l0"], _, arrived = gather_pass(jobs["l0"], value)
            jobs["l1_first"], token = gather_out(1, _FIRST, "gather_l1_first", after=arrived)
            jobs["l1"], jobs["token"] = gather_out(1, _LATER, "gather_l1", after=token)
            return {}, None
        if (l, point) == (0, "mixed"):
            return gather_end(jobs.pop("l0"), value), jobs.pop("token")
        if (l, point) == (0, "expanded"):
            jobs["l1_first"], token, _ = gather_pass(jobs["l1_first"], value)
            return {}, token
        if (l, point) == (1, "begin"):
            W = gather_end(jobs.pop("l1_first"), value)
            jobs["l1"], token, _ = gather_pass(jobs["l1"], value)
            return W, token
        if (l, point) == (1, "mixed"):
            return gather_end(jobs.pop("l1"), value), None
        return {}, None

    received = [{} for _ in range(DEPTH)]
    travelling = []

    def grads_done(l, gw):
        blocks = _grad_blocks(gw)
        keys = [k for k in _BIG if k in blocks]
        parts = [blocks[k] for k in keys]
        group = f"exchange_grads_l{l}_" + ("in" if "w_in" in blocks else "merge" if "w_out" in blocks else "mlp")
        lands = [_own_block_placed(lax.dynamic_index_in_dim(p, me, 0, keepdims=False), p) for p in parts]
        state, token = _split_start(_plan_exchange, parts, lands, name=group + "_start")
        travelling.append((l, keys, state, group + "_wait"))
        return token

    loss, dx, small = _local_step(x[0], mem[0], loss_target[0], sp, source, grads_done)
    grads, deltas, new_m, new_v = {}, {}, {}, {}
    like = [loss] + [w[k] for k in _SMALL]
    packed = _pack([loss] + [small[k] for k in _SMALL])
    eighths = packed.reshape(N_DEV, -1, _PACK_LANES)
    own = lambda a: _own_block_placed(lax.dynamic_index_in_dim(a, me, 0, keepdims=False) if a.ndim == 3 else a, eighths)
    scatter, done = _split_start(_plan_exchange, [eighths], [own(eighths)], after=dx, name="small_grads_scatter_start")

    def reduce_small(after):
        mine = _sum_blocks(_split_wait(_plan_exchange, scatter, after, name="small_grads_scatter_wait")[0], name="small_grads_sum")
        return _split_start(_plan_broadcast, [mine], [own(mine)], name="small_grads_gather_start")

    def update_small(state, after):
        total = _split_wait(_plan_broadcast, state, after, name="small_grads_gather_wait")[0].reshape(packed.shape)
        loss_sum, *g_small = _unpack(total, like)
        rows = lambda d: [_as_rows(d[k]) for k in _SMALL]
        outs = _adamw_small([_as_rows(g) for g in g_small], rows(w), rows(m), rows(v), name="adamw_small")
        grads.update(zip(_SMALL, g_small))
        for dst, vals in zip((deltas, new_m, new_v), outs):
            dst.update({k: a.reshape(w[k].shape) for k, a in zip(_SMALL, vals)})
        return loss_sum[0, 0], outs[0][0]

    groups = list(dict.fromkeys(tuple(keys) for _, keys, _, _ in travelling))
    for n_done, group_keys in enumerate(groups):
        if n_done == 1:
            gather, _ = reduce_small(done)
        if n_done == len(groups) - 1:
            loss, done = update_small(gather, done)
        for l, keys, state, wait_name in travelling:
            if tuple(keys) == group_keys:
                received[l].update(zip(keys, _split_wait(_plan_exchange, state, done, name=wait_name)))
        for k in group_keys:
            outs = _adamw_sharded([received[l][k] for l in range(DEPTH)], w[k], m[k], v[k], name="adamw_" + k)
            grads[k], deltas[k], new_m[k], new_v[k] = outs
        done = grads[group_keys[-1]]

    return (loss, dx[None], *[grads[k] for k in names], *[deltas[k] for k in names], *[new_m[k] for k in names],
            *[new_v[k] for k in names])
```

```python
import functools
import math

import jax
import jax.numpy as jnp
from jax import lax
from jax.experimental import pallas as pl
from jax.experimental.pallas import tpu as pltpu

F32 = jnp.float32
BF16 = jnp.bfloat16
MESH = pl.DeviceIdType.MESH

N_DEV = 8
D = 1024
DEPTH = 2
EPS = 1e-6
NEG = -1e30
POOL_W = 256
FOX_H = 8
FOX_DH = 64
FOX_W = 512
SGU_W = 256
SGU_CHUNK = 128
XH = 4
XDH = 256
N_IN = 5384
R_OFF_Q, R_OFF_F, R_OFF_C = 256, 1792, 1800
QKV_W = 3 * FOX_W
OFF_A, OFF_F, OFF_C, OFF_G, REST_W = 0, 256, 512, 1024, 4096
F_LANES = 128

ADAM_LR = 0.001
ADAM_B1 = 0.9
ADAM_B2 = 0.999
ADAM_EPS = 1e-08
ADAM_WD = 0.01
ADAM_STEP = 10

VMEM_LIMIT = 56 * 1024 * 1024


def _tile(n, pref):
    t = min(n, pref)
    while n % t:
        t -= 128
    assert t > 0, (n, pref)
    return t


def _params(sem=None):
    return pltpu.CompilerParams(dimension_semantics=sem, vmem_limit_bytes=VMEM_LIMIT)


def _dot(a, b, ca, cb):
    return lax.dot_general(a, b, (((ca,), (cb,)), ((), ())), preferred_element_type=F32)


def _sigmoid(z):
    return 1.0 / (1.0 + jnp.exp(-z))


_GELU_K = math.sqrt(2.0 / math.pi)
_GELU_C = 0.044715


def _gelu(x):
    return 0.5 * x * (1.0 + jnp.tanh(_GELU_K * (x + _GELU_C * x * x * x)))


def _gelu_grad(x):
    t = jnp.tanh(_GELU_K * (x + _GELU_C * x * x * x))
    return 0.5 * (1.0 + t) + 0.5 * x * (1.0 - t * t) * _GELU_K * (1.0 + 3.0 * _GELU_C * x * x)


def _rows(shape):
    return lax.broadcasted_iota(jnp.int32, shape, 0)


def _lanes(shape):
    return lax.broadcasted_iota(jnp.int32, shape, 1)


class _Gathered:
    def __init__(self, arr):
        self.arr = arr
        self.shape = (arr.shape[1], N_DEV * arr.shape[2])


_TOKEN = (8, 128)


def _mm(a, b, *, ta=False, tb=False, extras=(), row_extras=(), epilogue=None, out_dtypes=(F32,), row_outs=0, shard_out=False, after=None,
        tm=None, tn=512, tk=None, name):
    a_parts = list(a) if isinstance(a, (list, tuple)) else [a]
    b_parts = list(b) if isinstance(b, (list, tuple)) else [b]
    gathered = isinstance(b, _Gathered)
    assert (len(a_parts) == 1 or not ta) and (len(b_parts) == 1 or not tb) and min(len(a_parts), len(b_parts)) == 1
    a0, b0 = a_parts[0], b_parts[0]
    M, K = (a0.shape[1], a0.shape[0]) if ta else (a0.shape[0], sum(p.shape[1] for p in a_parts))
    N, Kb = b0.shape if tb else (b0.shape[1] * len(b_parts), b0.shape[0])
    assert Kb == K, (a0.shape, b0.shape, ta, tb)
    if gathered:
        if tb:
            tk = b.arr.shape[2]
        else:
            tn = b.arr.shape[2]
    if len(a_parts) > 1:
        tk = math.gcd(*[p.shape[1] for p in a_parts])
    if shard_out:
        tn = N // N_DEV
    tm = _tile(M, tm or (1024 if ta else 2048))
    tn = _tile(b0.shape[1] if len(b_parts) > 1 else N, tn)
    per_piece = b0.shape[1] // tn
    size = lambda dt: jnp.dtype(dt).itemsize
    row_bytes = len(a_parts) * tm * size(a0.dtype) + len(b_parts) * tn * size(b.arr.dtype if gathered else b0.dtype)
    tile_bytes = tm * tn * (sum(size(e.dtype) for e in extras) + sum(map(size, out_dtypes)))

    def vmem_bytes(k_tile):
        return 2 * (k_tile * row_bytes + tile_bytes) + tm * tn * 4 * (K > k_tile)

    if tk is None:
        tk = next(c for c in (_tile(K, 2048), _tile(K, 1024), _tile(K, 512), _tile(K, 256)) if vmem_bytes(c) <= VMEM_LIMIT - (4 << 20))
    tk = _tile(K, tk)
    nk = K // tk
    ca, cb = (0 if ta else 1), (1 if tb else 0)
    n_a, n_b, n_ex, n_out = len(a_parts), len(b_parts), len(extras) + len(row_extras), len(out_dtypes)
    tokens = [] if after is None else [after]
    n_in = n_a + n_b + n_ex + len(tokens)
    if epilogue is None:
        epilogue = lambda acc: (acc,)

    def body(*refs):
        a_refs, b_refs = refs[:n_a], refs[n_a:n_a + n_b]
        ex_refs = refs[n_a + n_b:n_a + n_b + n_ex]
        o_refs = refs[n_in:n_in + n_out]
        j, k = pl.program_id(1), pl.program_id(2)

        def finish(acc):
            vals = epilogue(acc, *[e[...] for e in ex_refs])
            for o_ref, val in zip(o_refs[:n_out - row_outs], vals):
                o_ref[...] = val.astype(o_ref.dtype)
            for o_ref, val in zip(o_refs[n_out - row_outs:], vals[n_out - row_outs:]):
                first = pl.program_id(0) == 0
                o_ref[...] = jnp.where(first, val, o_ref[...] + val)

        def step(a_ref, b_ref):
            part = _dot(a_ref[...].astype(BF16), b_ref[...].astype(BF16), ca, cb)
            if nk == 1:
                finish(part)
            else:
                acc_ref = refs[-1]

                @pl.when(k == 0)
                def _():
                    acc_ref[...] = part

                @pl.when(k > 0)
                def _():
                    acc_ref[...] += part

                @pl.when(k == nk - 1)
                def _():
                    finish(acc_ref[...])

        if n_a > 1:
            for p in range(n_a):
                pl.when((k >= a_first[p]) & (k < a_first[p + 1]))(functools.partial(step, a_refs[p], b_refs[0]))
        elif n_b > 1:
            for p in range(n_b):
                pl.when(j // per_piece == p)(functools.partial(step, a_refs[0], b_refs[p]))
        else:
            step(a_refs[0], b_refs[0])

    if n_a > 1:
        a_first = [sum(p.shape[1] for p in a_parts[:q]) // tk for q in range(n_a + 1)]
        a_specs = [pl.BlockSpec((tm, tk), functools.partial(
            lambda p, i, j, k: (i, jnp.clip(k - a_first[p], 0, a_first[p + 1] - a_first[p] - 1)), p)) for p in range(n_a)]
    else:
        a_specs = [pl.BlockSpec((tk, tm), lambda i, j, k: (k, i)) if ta else pl.BlockSpec((tm, tk), lambda i, j, k: (i, k))]
    if gathered:
        b_arrs = [b.arr]
        b_specs = [pl.BlockSpec((None, tn, tk), lambda i, j, k: (k, j, 0)) if tb else pl.BlockSpec((None, tk, tn), lambda i, j, k: (j, k, 0))]
    elif n_b > 1:
        b_arrs = b_parts
        b_specs = [pl.BlockSpec((tk, tn), functools.partial(lambda p, i, j, k: (k, jnp.clip(j - p * per_piece, 0, per_piece - 1)), p))
                   for p in range(n_b)]
    else:
        b_arrs = b_parts
        b_specs = [pl.BlockSpec((tn, tk), lambda i, j, k: (j, k)) if tb else pl.BlockSpec((tk, tn), lambda i, j, k: (k, j))]
    tile = pl.BlockSpec((tm, tn), lambda i, j, k: (i, j))
    if shard_out:
        out_specs = [pl.BlockSpec((None, tm, tn), lambda i, j, k: (j, i, 0))] * n_out
        out_shape = [jax.ShapeDtypeStruct((N_DEV, M, tn), dt) for dt in out_dtypes]
    else:
        assert row_outs == 0 or tn == N
        out_specs = [tile] * (n_out - row_outs) + [pl.BlockSpec((1, tn), lambda i, j, k: (0, j))] * row_outs
        out_shape = [jax.ShapeDtypeStruct((1, N) if t >= n_out - row_outs else (M, N), dt) for t, dt in enumerate(out_dtypes)]
    assert vmem_bytes(tk) <= VMEM_LIMIT - (4 << 20), (name, vmem_bytes(tk))
    outs = pl.pallas_call(
        body,
        name=name,
        grid=(M // tm, N // tn, nk),
        in_specs=a_specs + b_specs + [tile] * len(extras) + [pl.BlockSpec((1, tn), lambda i, j, k: (0, j))] * len(row_extras)
        + [pl.BlockSpec(_TOKEN, lambda i, j, k: (0, 0))] * len(tokens),
        out_specs=out_specs,
        out_shape=out_shape,
        scratch_shapes=[pltpu.VMEM((tm, tn), F32)] if nk > 1 else [],
        compiler_params=_params(("arbitrary",) * 3 if row_outs else ("parallel", "parallel", "arbitrary")),
    )(*a_parts, *b_arrs, *extras, *row_extras, *tokens)
    return outs[0] if n_out == 1 else outs


def _add(acc, res):
    return (acc + res,)


def _norm_grad(dh, x, dres, g):
    r = lax.rsqrt(jnp.mean(x * x, axis=-1, keepdims=True) + EPS)
    xn = x * r
    dxn = dh * g
    return r * (dxn - xn * jnp.mean(dxn * xn, axis=-1, keepdims=True)) + dres, jnp.sum(dh * xn, axis=0, keepdims=True)


def _add_norm(acc, res, g):
    x = acc + res
    return x, x * lax.rsqrt(jnp.mean(x * x, axis=-1, keepdims=True) + EPS) * g


def _rms_fwd(x, g, *, after=None, name):
    R, C = x.shape
    tm = _tile(R, 256)
    tokens = [] if after is None else [after]

    def body(x_ref, g_ref, *rest):
        xv = x_ref[...]
        r = lax.rsqrt(jnp.mean(xv * xv, axis=-1, keepdims=True) + EPS)
        rest[-1][...] = (xv * r * g_ref[...]).astype(BF16)

    return pl.pallas_call(
        body,
        name=name,
        grid=(R // tm,),
        in_specs=[pl.BlockSpec((tm, C), lambda i: (i, 0)), pl.BlockSpec((1, C), lambda i: (0, 0))]
        + [pl.BlockSpec(_TOKEN, lambda i: (0, 0))] * len(tokens),
        out_specs=pl.BlockSpec((tm, C), lambda i: (i, 0)),
        out_shape=jax.ShapeDtypeStruct((R, C), BF16),
        compiler_params=_params(("parallel",)),
    )(x, g.reshape(1, C), *tokens)


def _rms_bwd(x, g, dh, dres, *, name):
    R, C = x.shape
    tm = _tile(R, 256)

    def body(x_ref, g_ref, dh_ref, dres_ref, dx_ref, dg_ref):
        xv = x_ref[...]
        r = lax.rsqrt(jnp.mean(xv * xv, axis=-1, keepdims=True) + EPS)
        xn = xv * r
        dh_v = dh_ref[...].astype(F32)
        dxn = dh_v * g_ref[...]
        dx_ref[...] = r * (dxn - xn * jnp.mean(dxn * xn, axis=-1, keepdims=True)) + dres_ref[...]
        part = jnp.sum(dh_v * xn, axis=0, keepdims=True)

        @pl.when(pl.program_id(0) == 0)
        def _():
            dg_ref[...] = part

        @pl.when(pl.program_id(0) > 0)
        def _():
            dg_ref[...] += part

    row = pl.BlockSpec((tm, C), lambda i: (i, 0))
    vec = pl.BlockSpec((1, C), lambda i: (0, 0))
    dx, dg = pl.pallas_call(
        body,
        name=name,
        grid=(R // tm,),
        in_specs=[row, vec, row, row],
        out_specs=[row, vec],
        out_shape=[jax.ShapeDtypeStruct((R, C), F32), jax.ShapeDtypeStruct((1, C), F32)],
        compiler_params=_params(("arbitrary",)),
    )(x, g.reshape(1, C), dh, dres)
    return dx, dg.reshape(C)


def _final_loss(x, g, target, *, name):
    R, C = x.shape
    tm = _tile(R, 256)

    def body(x_ref, g_ref, t_ref, loss_ref, dx_ref, dg_ref):
        xv = x_ref[...]
        r = lax.rsqrt(jnp.mean(xv * xv, axis=-1, keepdims=True) + EPS)
        xn = xv * r
        gv = g_ref[...]
        err = xn * gv - t_ref[...]
        lpart = (0.5 / C) * jnp.sum(jnp.sum(err * err, axis=1, keepdims=True), axis=0, keepdims=True)
        dy = err * (1.0 / C)
        dxn = dy * gv
        dx_ref[...] = r * (dxn - xn * jnp.mean(dxn * xn, axis=-1, keepdims=True))
        gpart = jnp.sum(dy * xn, axis=0, keepdims=True)

        @pl.when(pl.program_id(0) == 0)
        def _():
            loss_ref[...] = lpart
            dg_ref[...] = gpart

        @pl.when(pl.program_id(0) > 0)
        def _():
            loss_ref[...] += lpart
            dg_ref[...] += gpart

    row = pl.BlockSpec((tm, C), lambda i: (i, 0))
    vec = pl.BlockSpec((1, C), lambda i: (0, 0))
    loss, dx, dg = pl.pallas_call(
        body,
        name=name,
        grid=(R // tm,),
        in_specs=[row, vec, row],
        out_specs=[pl.BlockSpec((1, 1), lambda i: (0, 0)), row, vec],
        out_shape=[jax.ShapeDtypeStruct((1, 1), F32), jax.ShapeDtypeStruct((R, C), F32), jax.ShapeDtypeStruct((1, C), F32)],
        compiler_params=_params(("arbitrary",)),
    )(x, g.reshape(1, C), target)
    return loss, dx, dg.reshape(C)


def _pool_select(lane, vals):
    out = vals[3]
    for gi in (2, 1, 0):
        out = jnp.where(lane < 64 * (gi + 1), vals[gi], out)
    return out


def _pool_diff(a):
    row, lane = _rows(a.shape), _lanes(a.shape)

    def down(v, k):
        return jnp.where(row >= k, pltpu.roll(v, k, 0), 0.0)

    s2 = a + down(a, 1)
    s4 = s2 + down(s2, 2)
    s8 = s4 + down(s4, 4)
    s16 = s8 + down(s8, 8)
    wsum = _pool_select(lane, (s2, s4, s8, s16))
    win = _pool_select(lane, (2, 4, 8, 16))
    cnt = jnp.minimum(row + 1, win).astype(F32)
    return wsum / cnt - a, cnt


def _pool_diff_t(dd, cnt):
    S = dd.shape[0]
    row, lane = _rows(dd.shape), _lanes(dd.shape)

    def up(v, k):
        return jnp.where(row < S - k, pltpu.roll(v, S - k, 0), 0.0)

    e = dd / cnt
    s2 = e + up(e, 1)
    s4 = s2 + up(s2, 2)
    s8 = s4 + up(s4, 4)
    s16 = s8 + up(s8, 8)
    return _pool_select(lane, (s2, s4, s8, s16)) - dd


def _pool_fwd(rest, wbd, scale, *, name):
    S = rest.shape[0]

    def body(a_ref, w_ref, s_ref, o_ref):
        d, _ = _pool_diff(a_ref[...])
        yp = _dot(d.astype(BF16), w_ref[...], 1, 0)
        o_ref[...] = (yp * s_ref[...]).astype(BF16)

    return pl.pallas_call(
        body,
        name=name,
        grid=(1,),
        in_specs=[
            pl.BlockSpec((S, POOL_W), lambda i: (0, OFF_A // POOL_W)),
            pl.BlockSpec((POOL_W, POOL_W), lambda i: (0, 0)),
            pl.BlockSpec((1, POOL_W), lambda i: (0, 0)),
        ],
        out_specs=pl.BlockSpec((S, POOL_W), lambda i: (0, 0)),
        out_shape=jax.ShapeDtypeStruct((S, POOL_W), BF16),
        compiler_params=_params(("arbitrary",)),
    )(rest, wbd, scale.reshape(1, POOL_W))


def _pool_bwd(rest, wbd, wbd_t, scale, dpa, *, name):
    S = rest.shape[0]

    def body(a_ref, w_ref, wt_ref, s_ref, dpa_ref, da_ref, dw_ref, ds_ref):
        d, cnt = _pool_diff(a_ref[...])
        db = d.astype(BF16)
        yp = _dot(db, w_ref[...], 1, 0)
        dpa_v = dpa_ref[...]
        ds_ref[...] = jnp.sum(dpa_v * yp, axis=0, keepdims=True)
        dyp = (dpa_v * s_ref[...]).astype(BF16)
        dw_ref[...] = _dot(db, dyp, 0, 0)
        dd = _dot(dyp, wt_ref[...], 1, 0)
        da_ref[...] = _pool_diff_t(dd, cnt).astype(BF16)

    full = pl.BlockSpec((S, POOL_W), lambda i: (0, 0))
    sq = pl.BlockSpec((POOL_W, POOL_W), lambda i: (0, 0))
    vec = pl.BlockSpec((1, POOL_W), lambda i: (0, 0))
    return pl.pallas_call(
        body,
        name=name,
        grid=(1,),
        in_specs=[pl.BlockSpec((S, POOL_W), lambda i: (0, OFF_A // POOL_W)), sq, sq, vec, full],
        out_specs=[full, sq, vec],
        out_shape=[
            jax.ShapeDtypeStruct((S, POOL_W), BF16),
            jax.ShapeDtypeStruct((POOL_W, POOL_W), F32),
            jax.ShapeDtypeStruct((1, POOL_W), F32),
        ],
        compiler_params=_params(("arbitrary",)),
    )(rest, wbd, wbd_t, scale.reshape(1, POOL_W), dpa)


def _log_sigmoid(z):
    return jnp.minimum(z, 0.0) - jnp.log(1.0 + jnp.exp(-jnp.abs(z)))


_F_SPEC_COL = OFF_F // F_LANES


def _fox_prep(rest, bpad, *, name):
    S = rest.shape[0]

    def body(f_ref, b_ref, o_ref, ot_ref):
        acc = _log_sigmoid(f_ref[...] + b_ref[...])
        row = _rows(acc.shape)
        k = 1
        while k < S:
            acc = acc + jnp.where(row >= k, pltpu.roll(acc, k, 0), 0.0)
            k *= 2
        o_ref[...] = acc
        ot_ref[...] = acc.T

    return pl.pallas_call(
        body,
        name=name,
        grid=(1,),
        in_specs=[pl.BlockSpec((S, F_LANES), lambda i: (0, _F_SPEC_COL)), pl.BlockSpec((1, F_LANES), lambda i: (0, 0))],
        out_specs=[pl.BlockSpec((S, F_LANES), lambda i: (0, 0)), pl.BlockSpec((F_LANES, S), lambda i: (0, 0))],
        out_shape=[jax.ShapeDtypeStruct((S, F_LANES), F32), jax.ShapeDtypeStruct((F_LANES, S), F32)],
        compiler_params=_params(("arbitrary",)),
    )(rest, bpad)


def _fox_post(rest, bpad, dcum, *, name):
    S = rest.shape[0]

    def body(f_ref, b_ref, d_ref, df_ref, db_ref):
        acc = d_ref[...]
        row = _rows(acc.shape)
        k = 1
        while k < S:
            acc = acc + jnp.where(row < S - k, pltpu.roll(acc, S - k, 0), 0.0)
            k *= 2
        df = acc * (1.0 - _sigmoid(f_ref[...] + b_ref[...]))
        df_ref[...] = df.astype(BF16)
        db_ref[...] = jnp.sum(df, axis=0, keepdims=True)

    full = pl.BlockSpec((S, F_LANES), lambda i: (0, 0))
    vec = pl.BlockSpec((1, F_LANES), lambda i: (0, 0))
    return pl.pallas_call(
        body,
        name=name,
        grid=(1,),
        in_specs=[pl.BlockSpec((S, F_LANES), lambda i: (0, _F_SPEC_COL)), vec, full],
        out_specs=[full, vec],
        out_shape=[jax.ShapeDtypeStruct((S, F_LANES), BF16), jax.ShapeDtypeStruct((1, F_LANES), F32)],
        compiler_params=_params(("arbitrary",)),
    )(rest, bpad, dcum)


_FOX_SCALE = FOX_DH ** -0.5
_PAIRS = FOX_H // 2


def _scaled(v):
    return (v.astype(F32) * _FOX_SCALE).astype(BF16)


def _diag_mask(s):
    return jnp.where(_rows(s.shape) >= _lanes(s.shape), s, NEG)


def _fox_fwd(qkv, cum, fk3, *, name):
    S = qkv.shape[0]
    nk, t = fk3.shape[1:]

    def body(q_ref, k_ref, v_ref, cum_ref, fk_ref, o_ref, lse_ref):
        i = pl.program_id(0)
        lane = _lanes((t, 128))
        lo = lane < FOX_DH
        cumv = cum_ref[...]
        qm, fq = [], []
        for h in range(FOX_H):
            qs = _scaled(q_ref[:, 128 * (h // 2):128 * (h // 2 + 1)])
            zero = jnp.zeros_like(qs)
            qm.append(jnp.where(lo, qs, zero) if h % 2 == 0 else jnp.where(lo, zero, qs))
            fq.append(jnp.broadcast_to(cumv[:, h:h + 1], (t, 128)))

        def tile(j, state, masked):
            m, acc, lsum = (list(part) for part in state)
            k0 = pl.multiple_of(j * t, t)
            for hp in range(_PAIRS):
                cols = slice(128 * hp, 128 * (hp + 1))
                kb = k_ref[pl.ds(k0, t), cols]
                vb = v_ref[pl.ds(k0, t), cols]
                one = jnp.ones_like(vb)
                alphas, pvs = [], []
                for h in (2 * hp, 2 * hp + 1):
                    s = _dot(qm[h], kb, 1, 1) + jnp.concatenate([fq[h]] * (t // 128), axis=1) - fk_ref[h, pl.ds(j, 1), :]
                    if masked:
                        s = _diag_mask(s)
                    m_new = jnp.maximum(m[h], jnp.max(s, axis=-1, keepdims=True))
                    p = jnp.exp(s - m_new)
                    alphas.append(jnp.exp(m[h] - m_new))
                    m[h] = m_new
                    pvs.append(_dot(p.astype(BF16), jnp.where(lo, vb, one) if h % 2 == 0 else jnp.where(lo, one, vb), 1, 0))
                acc[hp] = jnp.where(lo, alphas[0], alphas[1]) * acc[hp] + jnp.where(lo, pvs[0], pvs[1])
                lsum[hp] = jnp.where(lo, alphas[1], alphas[0]) * lsum[hp] + jnp.where(lo, pvs[1], pvs[0])
            return tuple(m), tuple(acc), tuple(lsum)

        zeros = (jnp.zeros((t, 128), F32),) * _PAIRS
        state = lax.fori_loop(0, i, functools.partial(tile, masked=False), ((jnp.full((t, 1), NEG, F32),) * FOX_H, zeros, zeros))
        m, acc, lsum = tile(i, state, True)
        for hp in range(_PAIRS):
            o_ref[:, 128 * hp:128 * (hp + 1)] = acc[hp] / pltpu.roll(lsum[hp], FOX_DH, 1)
            lse = [m[2 * hp] + jnp.log(lsum[hp][:, FOX_DH:FOX_DH + 1]), m[2 * hp + 1] + jnp.log(lsum[hp][:, 0:1])]
            lse_ref[hp] = jnp.where(lane == 0, lse[0], jnp.where(lane == 1, lse[1], 0.0))

    whole = lambda col: pl.BlockSpec((S, FOX_W), lambda i: (0, col))
    return pl.pallas_call(
        body,
        name=name,
        grid=(S // t,),
        in_specs=[
            pl.BlockSpec((t, FOX_W), lambda i: (i, 0)), whole(1), whole(2),
            pl.BlockSpec((t, F_LANES), lambda i: (i, 0)),
            pl.BlockSpec((FOX_H, nk, t), lambda i: (0, 0, 0)),
        ],
        out_specs=[pl.BlockSpec((t, FOX_W), lambda i: (i, 0)), pl.BlockSpec((_PAIRS, t, 128), lambda i: (0, i, 0))],
        out_shape=[jax.ShapeDtypeStruct((S, FOX_W), F32), jax.ShapeDtypeStruct((_PAIRS, S, 128), F32)],
        compiler_params=_params(("arbitrary",)),
    )(qkv, qkv, qkv, cum, fk3)


def _fox_bwd(qkv, cum, fk3, o, do, lse, *, name):
    S = qkv.shape[0]
    nk, t = fk3.shape[1:]
    q_at, k_at, v_at = 0, FOX_W, 2 * FOX_W

    def body(qkv_ref, cum_ref, fk_ref, o_ref, do_ref, lse_ref, dq_ref, dk_ref, dv_ref, dfq_ref, dfk_ref,
             qs_sc, ks_sc, bias_sc, delta_sc, dq_sc):
        lane = _lanes((t, 128))
        lo = lane < FOX_DH
        mine = lambda h: lo if h % 2 == 0 else jnp.logical_not(lo)

        def by_head(tile, values):
            for h, val in enumerate(values):
                tile = jnp.where(lane == h, val, tile)
            return tile

        def prep(i, carry):
            r = pl.ds(pl.multiple_of(i * t, t), t)
            qs_sc[r, :] = _scaled(qkv_ref[r, q_at:q_at + FOX_W])
            ks_sc[r, :] = _scaled(qkv_ref[r, k_at:k_at + FOX_W])
            cum_t = cum_ref[r, :]
            for hp in range(_PAIRS):
                cols = slice(128 * hp, 128 * (hp + 1))
                prod = do_ref[r, cols].astype(F32) * o_ref[r, cols]
                for h in (2 * hp, 2 * hp + 1):
                    delta = jnp.sum(jnp.where(mine(h), prod, 0.0), axis=-1, keepdims=True)
                    delta_sc[h, r, :] = jnp.broadcast_to(delta, (t, 128))
                    bias_sc[h, r, :] = jnp.broadcast_to(cum_t[:, h:h + 1] - lse_ref[hp, r, h % 2:h % 2 + 1], (t, 128))
            dfq_ref[r, :] = jnp.zeros((t, 128), F32)
            dq_sc[r, :] = jnp.zeros((t, FOX_W), F32)
            return carry

        lax.fori_loop(0, nk, prep, 0)

        def kv_tile(j, carry):
            kr = pl.ds(pl.multiple_of(j * t, t), t)

            def q_tile(i, acc, masked):
                dk, dv, dfk = list(acc[:_PAIRS]), list(acc[_PAIRS:2 * _PAIRS]), list(acc[2 * _PAIRS:])
                qr = pl.ds(pl.multiple_of(i * t, t), t)
                dq_old, dfq_old = dq_sc[qr, :], dfq_ref[qr, :]
                wide = lambda a: jnp.concatenate([a] * (t // 128), axis=1)
                row_sums, dq_new = [], []
                for hp in range(_PAIRS):
                    cols = slice(128 * hp, 128 * (hp + 1))
                    kb = qkv_ref[kr, k_at + 128 * hp:k_at + 128 * (hp + 1)]
                    vb = qkv_ref[kr, v_at + 128 * hp:v_at + 128 * (hp + 1)]
                    ksb, qsb, dob = ks_sc[kr, cols], qs_sc[qr, cols], do_ref[qr, cols]
                    zero = jnp.zeros_like(qsb)
                    dq_t = jnp.zeros((t, 128), F32)
                    for h in (2 * hp, 2 * hp + 1):
                        qe, doe, ke = (jnp.where(mine(h), a, zero) for a in (qsb, dob, ksb))
                        s = _dot(qe, kb, 1, 1) + wide(bias_sc[h, qr, :]) - fk_ref[h, pl.ds(j, 1), :]
                        if masked:
                            s = _diag_mask(s)
                        p = jnp.exp(s)
                        dv[hp] = dv[hp] + _dot(p.astype(BF16), doe, 0, 0)
                        dp = _dot(doe, vb, 1, 1)
                        ds = p * (dp - wide(delta_sc[h, qr, :]))
                        dsb = ds.astype(BF16)
                        dk[hp] = dk[hp] + _dot(dsb, qe, 0, 0)
                        dq_t = dq_t + _dot(dsb, ke, 1, 0)
                        row_sums.append(jnp.sum(ds, axis=-1, keepdims=True))
                        dfk[h] = dfk[h] - jnp.sum(ds, axis=0, keepdims=True)
                    dq_new.append(dq_old[:, cols] + dq_t)
                for hp in range(_PAIRS):
                    dq_sc[qr, 128 * hp:128 * (hp + 1)] = dq_new[hp]
                dfq_ref[qr, :] = dfq_old + by_head(jnp.zeros((t, 128), F32), row_sums)
                return (*dk, *dv, *dfk)

            init = tuple([jnp.zeros((t, 128), F32)] * (2 * _PAIRS) + [jnp.zeros((1, t), F32)] * FOX_H)
            acc = q_tile(j, init, True)
            acc = lax.fori_loop(j + 1, nk, functools.partial(q_tile, masked=False), acc)
            for hp in range(_PAIRS):
                cols = slice(128 * hp, 128 * (hp + 1))
                dk_ref[kr, cols] = acc[hp].astype(BF16)
                dv_ref[kr, cols] = acc[_PAIRS + hp].astype(BF16)
            for h in range(FOX_H):
                dfk_ref[h, pl.ds(j, 1), :] = acc[2 * _PAIRS + h]
            return carry

        lax.fori_loop(0, nk, kv_tile, 0)
        dq_ref[...] = dq_sc[...].astype(BF16)

    vm = pl.BlockSpec(memory_space=pltpu.VMEM)
    big = jax.ShapeDtypeStruct((S, FOX_W), BF16)
    return pl.pallas_call(
        body,
        name=name,
        in_specs=[vm] * 6,
        out_specs=[vm] * 5,
        out_shape=[big, big, big, jax.ShapeDtypeStruct((S, 128), F32), jax.ShapeDtypeStruct((FOX_H, nk, t), F32)],
        scratch_shapes=[pltpu.VMEM((S, FOX_W), BF16), pltpu.VMEM((S, FOX_W), BF16), pltpu.VMEM((FOX_H, S, 128), F32),
                        pltpu.VMEM((FOX_H, S, 128), F32), pltpu.VMEM((S, FOX_W), F32)],
        compiler_params=pltpu.CompilerParams(vmem_limit_bytes=VMEM_LIMIT),
    )(qkv, cum, fk3, o, do, lse)


def _group_mask(lane, gi):
    return (lane >= 64 * gi) & (lane < 64 * (gi + 1))


_U_COL = OFF_C // SGU_W


def _sgu_fwd(rest, gn, wm, bias, *, name):
    S = rest.shape[0]
    ts = _tile(S, 512)
    nc = ts // SGU_CHUNK

    def body(u_ref, v_ref, g_ref, w_ref, b_ref, o_ref):
        zv = _gelu(v_ref[...])
        vn = zv * lax.rsqrt(jnp.mean(zv * zv, axis=-1, keepdims=True) + EPS) * g_ref[...]
        lane = _lanes((SGU_CHUNK, SGU_W))
        for c in range(nc):
            rows = slice(c * SGU_CHUNK, (c + 1) * SGU_CHUNK)
            vcb = vn[rows].astype(BF16)
            mixed = b_ref[...]
            for gi in range(4):
                mixed = mixed + jnp.where(_group_mask(lane, gi), _dot(w_ref[gi], vcb, 1, 0), 0.0)
            o_ref[rows, :] = (_gelu(u_ref[rows, :]) * mixed).astype(BF16)

    return pl.pallas_call(
        body,
        name=name,
        grid=(S // ts,),
        in_specs=[
            pl.BlockSpec((ts, SGU_W), lambda i: (i, _U_COL)),
            pl.BlockSpec((ts, SGU_W), lambda i: (i, _U_COL + 1)),
            pl.BlockSpec((1, SGU_W), lambda i: (0, 0)),
            pl.BlockSpec((4, SGU_CHUNK, SGU_CHUNK), lambda i: (0, 0, 0)),
            pl.BlockSpec((SGU_CHUNK, SGU_W), lambda i: (0, 0)),
        ],
        out_specs=pl.BlockSpec((ts, SGU_W), lambda i: (i, 0)),
        out_shape=jax.ShapeDtypeStruct((S, SGU_W), BF16),
        compiler_params=_params(("parallel",)),
    )(rest, rest, gn.reshape(1, SGU_W), wm, bias)


def _sgu_bwd(rest, gn, wm, wm_t, bias, dsg, *, name):
    S = rest.shape[0]
    ts = _tile(S, 512)
    nc = ts // SGU_CHUNK

    def body(u_ref, v_ref, g_ref, w_ref, wt_ref, b_ref, dsg_ref, dc_ref, dw_ref, db_ref, dg_ref):
        first = pl.program_id(0) == 0

        @pl.when(first)
        def _():
            dw_ref[...] = jnp.zeros_like(dw_ref)
            db_ref[...] = jnp.zeros_like(db_ref)
            dg_ref[...] = jnp.zeros_like(dg_ref)

        gv = g_ref[...]
        lane = _lanes((SGU_CHUNK, SGU_W))
        for c in range(nc):
            rows = slice(c * SGU_CHUNK, (c + 1) * SGU_CHUNK)
            vpre = v_ref[rows, :]
            upre = u_ref[rows, :]
            zv = _gelu(vpre)
            r = lax.rsqrt(jnp.mean(zv * zv, axis=-1, keepdims=True) + EPS)
            zn = zv * r
            vcb = (zn * gv).astype(BF16)
            mixed = b_ref[...]
            for gi in range(4):
                mixed = mixed + jnp.where(_group_mask(lane, gi), _dot(w_ref[gi], vcb, 1, 0), 0.0)
            zu = _gelu(upre)
            dsg_v = dsg_ref[rows, :]
            dc_ref[rows, :SGU_W] = (dsg_v * mixed * _gelu_grad(upre)).astype(BF16)
            dmixed = dsg_v * zu
            db_ref[...] += dmixed
            dvn = jnp.zeros((SGU_CHUNK, SGU_W), F32)
            for gi in range(4):
                dmg = jnp.where(_group_mask(lane, gi), dmixed, 0.0).astype(BF16)
                dw_ref[gi] += _dot(dmg, vcb, 1, 1)
                dvn = dvn + _dot(wt_ref[gi], dmg, 1, 0)
            dg_ref[...] += jnp.sum(dvn * zn, axis=0, keepdims=True)
            dzn = dvn * gv
            dzv = r * (dzn - zn * jnp.mean(dzn * zn, axis=-1, keepdims=True))
            dc_ref[rows, SGU_W:] = (dzv * _gelu_grad(vpre)).astype(BF16)

    blk = pl.BlockSpec((ts, SGU_W), lambda i: (i, 0))
    vec = pl.BlockSpec((1, SGU_W), lambda i: (0, 0))
    w3 = pl.BlockSpec((4, SGU_CHUNK, SGU_CHUNK), lambda i: (0, 0, 0))
    bsp = pl.BlockSpec((SGU_CHUNK, SGU_W), lambda i: (0, 0))
    return pl.pallas_call(
        body,
        name=name,
        grid=(S // ts,),
        in_specs=[
            pl.BlockSpec((ts, SGU_W), lambda i: (i, _U_COL)),
            pl.BlockSpec((ts, SGU_W), lambda i: (i, _U_COL + 1)),
            vec, w3, w3, bsp, blk,
        ],
        out_specs=[pl.BlockSpec((ts, 2 * SGU_W), lambda i: (i, 0)), w3, bsp, vec],
        out_shape=[
            jax.ShapeDtypeStruct((S, 2 * SGU_W), BF16),
            jax.ShapeDtypeStruct((4, SGU_CHUNK, SGU_CHUNK), F32),
            jax.ShapeDtypeStruct((SGU_CHUNK, SGU_W), F32),
            jax.ShapeDtypeStruct((1, SGU_W), F32),
        ],
        compiler_params=_params(("arbitrary",)),
    )(rest, rest, gn.reshape(1, SGU_W), wm, wm_t, bias, dsg)


_GT = 512
_G0 = OFF_G // _GT


def _gate_specs(tm, col_of):
    specs = [pl.BlockSpec((tm, _GT), functools.partial(lambda k, *ids: (col_of(*ids)[0], _G0 + 2 * k + col_of(*ids)[1]), k)) for k in range(3)]
    specs += [pl.BlockSpec((1, _GT), functools.partial(lambda k, *ids: (0, 2 * k + col_of(*ids)[1]), k)) for k in range(3)]
    return specs


def _merge_fwd(rest, bg, ya, yb, yc, *, name):
    S = rest.shape[0]
    tm = _tile(S, 512)

    def body(g1, g2, g3, b1, b2, b3, ya_ref, yb_ref, yc_ref, o_ref):
        acc = _sigmoid(g1[...] + b1[...]) * ya_ref[...]
        acc = acc + _sigmoid(g2[...] + b2[...]) * yb_ref[...]
        acc = acc + _sigmoid(g3[...] + b3[...]) * yc_ref[...]
        o_ref[...] = acc.astype(BF16)

    blk = pl.BlockSpec((tm, _GT), lambda i, j: (i, j))
    return pl.pallas_call(
        body,
        name=name,
        grid=(S // tm, D // _GT),
        in_specs=_gate_specs(tm, lambda i, j: (i, j)) + [blk, blk, blk],
        out_specs=blk,
        out_shape=jax.ShapeDtypeStruct((S, D), BF16),
        compiler_params=_params(("parallel", "parallel")),
    )(rest, rest, rest, bg, bg, bg, ya, yb, yc)


def _merge_bwd(rest, bg, ya, yb, yc, dm, *, name):
    S = rest.shape[0]
    tm = _tile(S, 512)

    def body(g1, g2, g3, b1, b2, b3, ya_ref, yb_ref, yc_ref, dm_ref, dya, dyb, dyc, dg1, dg2, dg3, db1, db2, db3):
        first = pl.program_id(1) == 0
        dmv = dm_ref[...]
        for g_ref, b_ref, y_ref, dy_ref, dg_ref, db_ref in (
            (g1, b1, ya_ref, dya, dg1, db1), (g2, b2, yb_ref, dyb, dg2, db2), (g3, b3, yc_ref, dyc, dg3, db3)):
            gate = _sigmoid(g_ref[...] + b_ref[...])
            dy_ref[...] = (dmv * gate).astype(BF16)
            dpre = dmv * y_ref[...] * gate * (1.0 - gate)
            dg_ref[...] = dpre.astype(BF16)
            part = jnp.sum(dpre, axis=0, keepdims=True)

            @pl.when(first)
            def _():
                db_ref[...] = part

            @pl.when(jnp.logical_not(first))
            def _():
                db_ref[...] += part

    blk = pl.BlockSpec((tm, _GT), lambda j, i: (i, j))
    vec = pl.BlockSpec((1, _GT), lambda j, i: (0, j))
    big = jax.ShapeDtypeStruct((S, D), BF16)
    small = jax.ShapeDtypeStruct((1, D), F32)
    return pl.pallas_call(
        body,
        name=name,
        grid=(D // _GT, S // tm),
        in_specs=_gate_specs(tm, lambda j, i: (i, j)) + [blk, blk, blk, blk],
        out_specs=[blk] * 6 + [vec] * 3,
        out_shape=[big] * 6 + [small] * 3,
        compiler_params=_params(("parallel", "arbitrary")),
    )(rest, rest, rest, bg, bg, bg, ya, yb, yc, dm)


_X_SCALE = XDH ** -0.5


def _xattn_fwd(xq, kv, *, name):
    S = xq.shape[0]
    M = kv.shape[0]
    tq = _tile(S, 512)

    def body(q_ref, k_ref, v_ref, o_ref):
        s = _dot(q_ref[...], k_ref[...], 1, 1) * _X_SCALE
        e = jnp.exp(s - jnp.max(s, axis=-1, keepdims=True))
        p = e / jnp.sum(e, axis=-1, keepdims=True)
        o_ref[...] = _dot(p.astype(BF16), v_ref[...], 1, 0).astype(BF16)

    return pl.pallas_call(
        body,
        name=name,
        grid=(S // tq, XH),
        in_specs=[
            pl.BlockSpec((tq, XDH), lambda i, h: (i, h)),
            pl.BlockSpec((M, XDH), lambda i, h: (0, h)),
            pl.BlockSpec((M, XDH), lambda i, h: (0, XH + h)),
        ],
        out_specs=pl.BlockSpec((tq, XDH), lambda i, h: (i, h)),
        out_shape=jax.ShapeDtypeStruct((S, D), BF16),
        compiler_params=_params(("parallel", "parallel")),
    )(xq, kv, kv)


def _xattn_bwd(xq, kv, do, *, name):
    S = xq.shape[0]
    M = kv.shape[0]
    tq = _tile(S, 512)

    def body(q_ref, k_ref, v_ref, do_ref, dq_ref, dk_ref, dv_ref):
        qb = q_ref[...]
        kb = k_ref[...]
        dob = do_ref[...]
        s = _dot(qb, kb, 1, 1) * _X_SCALE
        e = jnp.exp(s - jnp.max(s, axis=-1, keepdims=True))
        p = e / jnp.sum(e, axis=-1, keepdims=True)
        dp = _dot(dob, v_ref[...], 1, 1)
        ds = (p * (dp - jnp.sum(p * dp, axis=-1, keepdims=True)) * _X_SCALE).astype(BF16)
        dq_ref[...] = _dot(ds, kb, 1, 0).astype(BF16)
        dk_part = _dot(ds, qb, 0, 0)
        dv_part = _dot(p.astype(BF16), dob, 0, 0)

        @pl.when(pl.program_id(1) == 0)
        def _():
            dk_ref[...] = dk_part
            dv_ref[...] = dv_part

        @pl.when(pl.program_id(1) > 0)
        def _():
            dk_ref[...] += dk_part
            dv_ref[...] += dv_part

    qspec = pl.BlockSpec((tq, XDH), lambda h, i: (i, h))
    kspec = pl.BlockSpec((M, XDH), lambda h, i: (0, h))
    dxq, dxk, dxv = pl.pallas_call(
        body,
        name=name,
        grid=(XH, S // tq),
        in_specs=[qspec, kspec, pl.BlockSpec((M, XDH), lambda h, i: (0, XH + h)), qspec],
        out_specs=[qspec, kspec, kspec],
        out_shape=[jax.ShapeDtypeStruct((S, D), BF16), jax.ShapeDtypeStruct((M, D), F32), jax.ShapeDtypeStruct((M, D), F32)],
        compiler_params=_params(("parallel", "arbitrary")),
    )(xq, kv, kv, do)
    return dxq, jnp.concatenate([dxk, dxv], axis=1)


def _adam_math(w, g, m, v):
    m = ADAM_B1 * m + (1.0 - ADAM_B1) * g
    v = ADAM_B2 * v + (1.0 - ADAM_B2) * (g * g)
    m_hat = m / (1.0 - ADAM_B1 ** ADAM_STEP)
    v_hat = v / (1.0 - ADAM_B2 ** ADAM_STEP)
    delta = -ADAM_LR * (m_hat / (jnp.sqrt(v_hat) + ADAM_EPS) + ADAM_WD * w)
    return delta, m, v


def _adamw_sharded(parts, w, m, v, *, name):
    _, R, C = w.shape
    tm = _tile(R, 256)
    nr = R // tm

    def body(p0_ref, p1_ref, w_ref, m_ref, v_ref, g_ref, d_ref, mo_ref, vo_ref):
        def update(p_ref):
            g = p_ref[0].astype(F32)
            for dev in range(1, N_DEV):
                g = g + p_ref[dev].astype(F32)
            delta, mn, vn = _adam_math(w_ref[...], g, m_ref[...], v_ref[...])
            g_ref[...] = g
            d_ref[...] = delta
            mo_ref[...] = mn
            vo_ref[...] = vn

        @pl.when(pl.program_id(0) == 0)
        def _():
            update(p0_ref)

        @pl.when(pl.program_id(0) == 1)
        def _():
            update(p1_ref)

    p0 = pl.BlockSpec((N_DEV, tm, C), lambda l, i: (0, i * (1 - l) + (nr - 1) * l, 0))
    p1 = pl.BlockSpec((N_DEV, tm, C), lambda l, i: (0, i * l, 0))
    blk = pl.BlockSpec((None, tm, C), lambda l, i: (l, i, 0))
    sds = jax.ShapeDtypeStruct(w.shape, F32)
    return pl.pallas_call(
        body,
        name=name,
        grid=(DEPTH, nr),
        in_specs=[p0, p1, blk, blk, blk],
        out_specs=[blk] * 4,
        out_shape=[sds] * 4,
        compiler_params=_params(("arbitrary", "arbitrary")),
    )(parts[0], parts[1], w, m, v)


def _adamw_small(g, w, m, v, *, name):
    n = len(g)

    def body(*refs):
        g_refs, w_refs, m_refs, v_refs = (refs[k * n:(k + 1) * n] for k in range(4))
        d_out, m_out, v_out = (refs[(4 + k) * n:(5 + k) * n] for k in range(3))
        for t in range(n):
            delta, mn, vn = _adam_math(w_refs[t][...], g_refs[t][...], m_refs[t][...], v_refs[t][...])
            d_out[t][...] = delta
            m_out[t][...] = mn
            v_out[t][...] = vn

    vm = pl.BlockSpec(memory_space=pltpu.VMEM)
    shapes = [jax.ShapeDtypeStruct(a.shape, F32) for a in w]
    outs = pl.pallas_call(
        body,
        name=name,
        in_specs=[vm] * (4 * n),
        out_specs=[vm] * (3 * n),
        out_shape=shapes * 3,
        compiler_params=pltpu.CompilerParams(vmem_limit_bytes=VMEM_LIMIT),
    )(*g, *w, *m, *v)
    return outs[:n], outs[n:2 * n], outs[2 * n:]


def _position():
    return lax.axis_index("x"), lax.axis_index("y"), lax.axis_index("c")


def _dev_index(px, py, pc):
    return 4 * px + 2 * py + pc


_ANY = pl.BlockSpec(memory_space=pl.ANY)


def _peers(x, y, c):
    out = []
    for mask in range(1, N_DEV):
        fx, fy, fc = (mask >> 2) & 1, (mask >> 1) & 1, mask & 1
        out.append((1 - x if fx else x, 1 - y if fy else y, 1 - c if fc else c))
    return out


_HBM = pl.BlockSpec(memory_space=pltpu.HBM)
_SEM = pl.BlockSpec(memory_space=pltpu.SEMAPHORE)


def _own_block_placed(block, like):
    x, y, c = _position()
    return lax.dynamic_update_index_in_dim(lax.empty(like.shape, like.dtype), block, _dev_index(x, y, c), 0)


_COPY_BYTES = 256 << 10
_MAX_PIECES = 8


def _pieces(blocks):
    out = []
    for t, b in enumerate(blocks):
        R, C = b.shape[-2:]
        n = max(1, min(_MAX_PIECES, R * C * jnp.dtype(b.dtype).itemsize // _COPY_BYTES))
        while n > 1 and R % (16 * n):
            n -= 1
        out += [(t, pl.ds(j * (R // n), R // n) if n > 1 else None) for j in range(n)]
    return out


def _cut(block, rows):
    return block if rows is None else block.at[rows]


def _copies(per_piece):
    def mark(fn):
        fn.per_piece = per_piece
        return fn
    return mark


@_copies(N_DEV - 1)
def _plan_exchange(srcs, lands, send_sems, recv_sems, arrivals):
    x, y, c = _position()
    me = _dev_index(x, y, c)
    out = []
    for k, peer in enumerate(_peers(x, y, c)):
        p = _dev_index(*peer)
        for i, (t, rows) in enumerate(_pieces(lands)):
            sems = dict(send_sem=send_sems.at[7 * i + k], recv_sem=recv_sems.at[7 * i + k], device_id=peer, device_id_type=MESH)
            src, dst = (lands[t].at[p], lands[t].at[p]) if arrivals else (srcs[t].at[p], lands[t].at[me])
            out.append(pltpu.make_async_remote_copy(src_ref=_cut(src, rows), dst_ref=_cut(dst, rows), **sems))
    return out


@_copies(N_DEV - 1)
def _plan_broadcast(srcs, lands, send_sems, recv_sems, arrivals):
    x, y, c = _position()
    me = _dev_index(x, y, c)
    out = []
    for k, peer in enumerate(_peers(x, y, c)):
        p = _dev_index(*peer)
        for i, (t, rows) in enumerate(_pieces(lands)):
            sems = dict(send_sem=send_sems.at[7 * i + k], recv_sem=recv_sems.at[7 * i + k], device_id=peer, device_id_type=MESH)
            src, dst = (lands[t].at[p], lands[t].at[p]) if arrivals else (srcs[t], lands[t].at[me])
            out.append(pltpu.make_async_remote_copy(src_ref=_cut(src, rows), dst_ref=_cut(dst, rows), **sems))
    return out


@_copies(4)
def _plan_gather_out(srcs, lands, send_sems, recv_sems, arrivals):
    x, y, c = _position()
    me = _dev_index(x, y, c)
    out = []
    for k, peer in enumerate([(x, y, 1 - c), (1 - x, y, c), (x, 1 - y, c), (1 - x, 1 - y, c)]):
        p = _dev_index(*peer)
        for i, (t, rows) in enumerate(_pieces(lands)):
            sems = dict(send_sem=send_sems.at[4 * i + k], recv_sem=recv_sems.at[4 * i + k], device_id=peer, device_id_type=MESH)
            src, dst = (lands[t].at[p], lands[t].at[p]) if arrivals else (srcs[t], lands[t].at[me])
            out.append(pltpu.make_async_remote_copy(src_ref=_cut(src, rows), dst_ref=_cut(dst, rows), **sems))
    return out


@_copies(3)
def _plan_gather_pass(srcs, lands, send_sems, recv_sems, arrivals):
    x, y, c = _position()
    sibling = (x, y, 1 - c)
    out = []
    for k, chip in enumerate([(1 - x, y), (x, 1 - y), (1 - x, 1 - y)]):
        p = _dev_index(*chip, 1 - c) if arrivals else _dev_index(*chip, c)
        for i, (t, rows) in enumerate(_pieces(lands)):
            sems = dict(send_sem=send_sems.at[3 * i + k], recv_sem=recv_sems.at[3 * i + k], device_id=sibling, device_id_type=MESH)
            block = _cut(lands[t].at[p], rows)
            out.append(pltpu.make_async_remote_copy(src_ref=block, dst_ref=block, **sems))
    return out


def _split_start(plan, srcs, lands, *, after=None, name):
    n_src, n = len(srcs), len(srcs) + len(lands)
    n_sem = plan.per_piece * len(_pieces(lands))
    order = [] if after is None else [after]

    def body(*refs):
        send_sems, recv_sems = refs[n + len(order):n + len(order) + 2]
        token = refs[-1]
        for cp in plan(refs[:n_src], refs[n_src:n], send_sems, recv_sems, arrivals=False):
            cp.start()
        token[...] = jnp.zeros_like(token)

    hbm = lambda a: pltpu.HBM(a.shape, a.dtype)
    outs = pl.pallas_call(
        body,
        name=name,
        in_specs=[_HBM] * n + [_ANY] * len(order),
        out_specs=[_SEM, _SEM] + [_HBM] * n + [pl.BlockSpec(memory_space=pltpu.VMEM)],
        out_shape=[pltpu.SemaphoreType.DMA((n_sem,)), pltpu.SemaphoreType.DMA((n_sem,))] + [hbm(a) for a in (*srcs, *lands)]
        + [jax.ShapeDtypeStruct(_TOKEN, F32)],
        input_output_aliases={i: 2 + i for i in range(n)},
        compiler_params=pltpu.CompilerParams(has_side_effects=pltpu.SideEffectType.DATAFLOW_SIDE_EFFECTING),
    )(*[pltpu.with_memory_space_constraint(a, pltpu.HBM) for a in (*srcs, *lands)], *order)
    return (outs[0], outs[1], outs[2:2 + n_src], outs[2 + n_src:2 + n]), outs[-1]


def _split_wait(plan, state, after, *, name):
    send_sems, recv_sems, srcs, lands = state
    n_src, n = len(srcs), len(srcs) + len(lands)

    def body(*refs):
        send_refs, recv_refs = refs[n:n + 2]
        for cp in plan(refs[:n_src], refs[n_src:n], send_refs, recv_refs, arrivals=False):
            cp.wait_send()
        for cp in plan(refs[:n_src], refs[n_src:n], send_refs, recv_refs, arrivals=True):
            cp.wait_recv()

    hbm = lambda a: pltpu.HBM(a.shape, a.dtype)
    outs = pl.pallas_call(
        body,
        name=name,
        in_specs=[_HBM] * n + [_SEM, _SEM, _ANY],
        out_specs=[_HBM] * n,
        out_shape=[hbm(a) for a in (*srcs, *lands)],
        input_output_aliases={i: i for i in range(n)},
        compiler_params=pltpu.CompilerParams(has_side_effects=pltpu.SideEffectType.DATAFLOW_SIDE_EFFECTING),
    )(*srcs, *lands, send_sems, recv_sems, after)
    return outs[n_src:]


def _sum_blocks(blocks, *, name):
    _, R, C = blocks.shape
    tm = next(R // n for n in (4, 3, 2, 1) if R % (8 * n) == 0)

    def body(b_ref, o_ref):
        g = b_ref[0]
        for dev in range(1, N_DEV):
            g = g + b_ref[dev]
        o_ref[...] = g

    return pl.pallas_call(
        body,
        name=name,
        grid=(R // tm,),
        in_specs=[pl.BlockSpec((N_DEV, tm, C), lambda i: (0, i, 0))],
        out_specs=pl.BlockSpec((tm, C), lambda i: (i, 0)),
        out_shape=jax.ShapeDtypeStruct((R, C), F32),
        compiler_params=_params(("parallel",)),
    )(blocks)


def _block_diag(w):
    out = jnp.zeros((POOL_W, POOL_W), w.dtype)
    for gi in range(4):
        out = out.at[64 * gi:64 * (gi + 1), 64 * gi:64 * (gi + 1)].set(w[gi])
    return out


def _layer_consts(sp, l):
    causal = jnp.tril(jnp.ones((SGU_CHUNK, SGU_CHUNK), F32))
    wm = (sp["sgu_w"][l] * causal[None]).astype(BF16)
    wbd = _block_diag(sp["pool_w"][l]).astype(BF16)
    return dict(
        wbd=wbd, wbd_t=wbd.T, wm=wm, wm_t=wm.transpose(0, 2, 1),
        sgu_bias=jnp.repeat(sp["sgu_b"][l].T, 64, axis=1),
        bpad=jnp.pad(sp["b_forget"][l], (0, F_LANES - FOX_H)).reshape(1, F_LANES),
        bg=sp["b_gate"][l].reshape(1, 3 * D),
    )


def _relu2(acc):
    return acc, jnp.square(jnp.maximum(acc, 0.0))


def _relu2_grad(acc, z):
    return (acc * 2.0 * jnp.maximum(z, 0.0),)


def _layer_fwd(l, x, h, mem, source, sp):
    S = x.shape[0]
    t = _tile(S, 256)
    c = _layer_consts(sp, l)
    n = f"l{l}_"
    W, after = source(l, "begin", x)
    if h is None:
        h, after = _rms_fwd(x, sp["norm_mix_g"][l], after=after, name=n + "norm_mix"), None
    hm = _rms_fwd(mem, sp["norm_mem_g"][l], name=n + "norm_mem")
    more, token = source(l, "normed", hm)
    W.update(more)
    qkv = _mm(h, W["qkv"], out_dtypes=(BF16,), after=after if token is None else token, name=n + "qkv")
    rest = _mm(h, W["rest"], name=n + "rest")
    pa = _pool_fwd(rest, c["wbd"], sp["pool_scale"][l], name=n + "pool")
    cum, cum_t = _fox_prep(rest, c["bpad"], name=n + "fox_prep")
    fk3 = cum_t[:FOX_H].reshape(FOX_H, S // t, t)
    o, lse = _fox_fwd(qkv, cum, fk3, name=n + "fox")
    more, _ = source(l, "attended", o)
    W.update(more)
    sg = _sgu_fwd(rest, sp["sgu_norm_g"][l], c["wm"], c["sgu_bias"], name=n + "sgu")
    more, after = source(l, "mixed", sg)
    W.update(more)
    ya = _mm(pa, W["ba"], out_dtypes=(BF16,), after=after, name=n + "branch_a")
    yb = _mm(o, W["bb"], out_dtypes=(BF16,), name=n + "branch_b")
    yc = _mm(sg, W["bc"], out_dtypes=(BF16,), name=n + "branch_c")
    merged = _merge_fwd(rest, c["bg"], ya, yb, yc, name=n + "merge")
    whole_rows = dict(epilogue=_add_norm, out_dtypes=(F32, BF16), tm=1024, tn=D)
    x1, hx = _mm(merged, W["out"], extras=(x,), row_extras=(sp["norm_xattn_g"][l].reshape(1, D),), name=n + "out", **whole_rows)
    xq = _mm(hx, W["xq"], out_dtypes=(BF16,), name=n + "xq")
    kv = _mm(hm, W["xkv"], out_dtypes=(BF16,), name=n + "xkv")
    o2 = _xattn_fwd(xq, kv, name=n + "xattn")
    x2, hf = _mm(o2, W["xo"], extras=(x1,), row_extras=(sp["norm_ffn_g"][l].reshape(1, D),), name=n + "xo", **whole_rows)
    z, act = _mm(hf, W["ff1"], epilogue=_relu2, out_dtypes=(BF16, BF16), name=n + "ff1")
    _, after = source(l, "expanded", act)
    if l + 1 < DEPTH:
        x3, h_next = _mm(act, W["ff2"], extras=(x2,), row_extras=(sp["norm_mix_g"][l + 1].reshape(1, D),), after=after, name=n + "ff2",
                         **whole_rows)
    else:
        x3, h_next = _mm(act, W["ff2"], extras=(x2,), epilogue=_add, after=after, name=n + "ff2"), None
    saved = dict(x=x, h=h, qkv=qkv, rest=rest, pa=pa, cum=cum, fk3=fk3, o=o, lse=lse, sg=sg, ya=ya, yb=yb, yc=yc,
                 merged=merged, x1=x1, hx=hx, hm=hm, xq=xq, kv=kv, o2=o2, x2=x2, hf=hf, z=z, act=act, c=c)
    return x3, h_next, saved, W


def _layer_bwd(l, dx3, sv, mem, W, sp, grads_done):
    S = dx3.shape[0]
    c = sv["c"]
    n = f"l{l}b_"
    bf = dict(out_dtypes=(BF16,))
    gw, gs = {}, {}
    gw["ff2"] = _mm(sv["act"], dx3, ta=True, name=n + "dw_ff2", **bf)
    dz = _mm(dx3, W["ff2"], tb=True, extras=(sv["z"],), epilogue=_relu2_grad, name=n + "dz", **bf)
    gw["ff1"] = _mm(sv["hf"], dz, ta=True, shard_out=True, name=n + "dw_ff1", **bf)
    whole_rows = dict(epilogue=_norm_grad, out_dtypes=(F32, F32), row_outs=1, tm=1024, tn=D)
    gain = lambda key: (sp[key][l].reshape(1, D),)
    dx2, dg = _mm(dz, W["ff1"], tb=True, extras=(sv["x2"], dx3), row_extras=gain("norm_ffn_g"), name=n + "dhf", **whole_rows)
    gs["norm_ffn_g"] = dg.reshape(D)
    gw["xo"] = _mm(sv["o2"], dx2, ta=True, name=n + "dw_xo", **bf)
    do2 = _mm(dx2, W["xo"], tb=True, name=n + "do2", **bf)
    dxq, dkv = _xattn_bwd(sv["xq"], sv["kv"], do2, name=n + "dxattn")
    gw["xq"] = _mm(sv["hx"], dxq, ta=True, name=n + "dw_xq", **bf)
    gw["xkv"] = _mm(sv["hm"], dkv, ta=True, shard_out=True, name=n + "dw_xkv", **bf)
    dhm = _mm(dkv, W["xkv"], tb=True, name=n + "dhm")
    _, gs["norm_mem_g"] = _rms_bwd(mem, sp["norm_mem_g"][l], dhm, jnp.zeros_like(mem), name=n + "dnorm_mem")
    dx1, dg = _mm(dxq, W["xq"], tb=True, extras=(sv["x1"], dx2), row_extras=gain("norm_xattn_g"), name=n + "dhx", **whole_rows)
    gs["norm_xattn_g"] = dg.reshape(D)
    after, gw = grads_done(l, gw), {}
    gw["out"] = _mm(sv["merged"], dx1, ta=True, name=n + "dw_out", **bf)
    dm = _mm(dx1, W["out"], tb=True, after=after, name=n + "dmerged")
    dya, dyb, dyc, dg1, dg2, dg3, db1, db2, db3 = _merge_bwd(sv["rest"], c["bg"], sv["ya"], sv["yb"], sv["yc"], dm, name=n + "dmerge")
    gs["b_gate"] = jnp.concatenate([db1, db2, db3], axis=1).reshape(3 * D)
    gw["ba"] = _mm(sv["pa"], dya, ta=True, shard_out=True, name=n + "dw_ba", **bf)
    gw["bb"] = _mm(sv["o"], dyb, ta=True, shard_out=True, name=n + "dw_bb", **bf)
    gw["bc"] = _mm(sv["sg"], dyc, ta=True, shard_out=True, name=n + "dw_bc", **bf)
    after, gw = grads_done(l, gw), {}
    dpa = _mm(dya, W["ba"], tb=True, name=n + "dpa")
    do = _mm(dyb, W["bb"], tb=True, after=after, name=n + "do", **bf)
    dsg = _mm(dyc, W["bc"], tb=True, name=n + "dsg")
    da, dwbd, dscale = _pool_bwd(sv["rest"], c["wbd"], c["wbd_t"], sp["pool_scale"][l], dpa, name=n + "dpool")
    gs["pool_w"] = jnp.stack([dwbd[64 * gi:64 * (gi + 1), 64 * gi:64 * (gi + 1)] for gi in range(4)])
    gs["pool_scale"] = dscale.reshape(POOL_W)
    dq, dk, dv, dfq, dfk = _fox_bwd(sv["qkv"], sv["cum"], sv["fk3"], sv["o"], do, sv["lse"], name=n + "dfox")
    dcum = dfq + jnp.pad(dfk.reshape(FOX_H, S).T, ((0, 0), (0, F_LANES - FOX_H)))
    df, dbf = _fox_post(sv["rest"], c["bpad"], dcum, name=n + "dfox_post")
    gs["b_forget"] = dbf[0, :FOX_H]
    dc, dwm, dbias, dgn = _sgu_bwd(sv["rest"], sp["sgu_norm_g"][l], c["wm"], c["wm_t"], c["sgu_bias"], dsg, name=n + "dsgu")
    gs["sgu_w"] = dwm * jnp.tril(jnp.ones((SGU_CHUNK, SGU_CHUNK), F32))[None]
    gs["sgu_b"] = dbias.reshape(SGU_CHUNK, 4, 64).sum(axis=2).T
    gs["sgu_norm_g"] = dgn.reshape(SGU_W)
    dqkv = [dq, dk, dv]
    drest = [jnp.concatenate([da, df, jnp.zeros((S, OFF_C - OFF_F - F_LANES), BF16), dc], axis=1), dg1, dg2, dg3]
    gw["qkv"] = _mm(sv["h"], dqkv, ta=True, name=n + "dw_qkv", **bf)
    gw["rest"] = _mm(sv["h"], drest, ta=True, name=n + "dw_rest", **bf)
    after = grads_done(l, gw)
    dx, dg = _mm(dqkv + drest, W["in"], tb=True, extras=(sv["x"], dx1), row_extras=gain("norm_mix_g"), after=after, name=n + "dh",
                 **whole_rows)
    gs["norm_mix_g"] = dg.reshape(D)
    return dx, gs


def _local_step(x, mem, target, sp, source, grads_done):
    saved, Ws, h = [], [], None
    for l in range(DEPTH):
        x, h, sv, W = _layer_fwd(l, x, h, mem, source, sp)
        saved.append(sv)
        Ws.append(W)
    loss, dx, dgf = _final_loss(x, sp["final_norm_g"], target, name="final_loss")
    gss = [None] * DEPTH
    for l in reversed(range(DEPTH)):
        dx, gss[l] = _layer_bwd(l, dx, saved[l], mem, Ws[l], sp, grads_done)
    small = {k: jnp.stack([gss[l][k] for l in range(DEPTH)]) for k in gss[0]}
    small["final_norm_g"] = dgf
    return loss, dx, small


_SMALL = ["norm_mix_g", "b_forget", "pool_w", "pool_scale", "sgu_norm_g", "sgu_w", "sgu_b", "b_gate", "norm_xattn_g",
          "norm_mem_g", "norm_ffn_g", "final_norm_g"]
_COL = {"w_branch_a": "ba", "w_branch_b": "bb", "w_branch_c": "bc", "w_xkv": "xkv", "w_ff1": "ff1"}
_ROW = {"w_out": "out", "w_xq": "xq", "w_xo": "xo", "w_ff2": "ff2"}
_BIG = ["w_in", "w_branch_a", "w_branch_b", "w_branch_c", "w_out", "w_xq", "w_xkv", "w_xo", "w_ff1", "w_ff2"]
_PACK_LANES = 128


def _as_rows(a):
    return a.reshape(-1, a.shape[-1])


def _pack(tensors):
    rows = []
    for a in tensors:
        flat = a.reshape(-1)
        flat = jnp.pad(flat, (0, (-flat.shape[0]) % (8 * _PACK_LANES)))
        rows.append(flat.reshape(-1, _PACK_LANES))
    n_rows = sum(r.shape[0] for r in rows)
    rows.append(jnp.zeros(((-n_rows) % (8 * N_DEV), _PACK_LANES), F32))
    return jnp.concatenate(rows, axis=0)


def _unpack(packed, like):
    out, r = [], 0
    for a in like:
        size = math.prod(a.shape)
        nr = 8 * (-(-size // (8 * _PACK_LANES)))
        out.append(packed[r:r + nr].reshape(-1)[:size].reshape(a.shape))
        r += nr
    return out


_SHARD_IN = N_IN // N_DEV


def _columns(pieces, start, stop):
    out, at = [], 0
    for p in pieces:
        lo, hi = max(start, at), min(stop, at + p.shape[1])
        if lo < hi:
            out.append(p[:, lo - at:hi - at])
        at += p.shape[1]
    return out


def _split_w_in(blocks):
    K = blocks[0].shape[0]
    pad = jnp.zeros((K, OFF_C - OFF_F - FOX_H), blocks[0].dtype)
    cols = functools.partial(_columns, blocks)
    rest = jnp.concatenate(cols(0, R_OFF_Q) + cols(R_OFF_F, R_OFF_C) + [pad] + cols(R_OFF_C, N_IN), axis=1)
    qkv = jnp.concatenate(cols(R_OFF_Q, R_OFF_F), axis=1)
    return qkv, rest, jnp.concatenate([qkv, rest], axis=1)


def _join_w_in(qkv, rest):
    in_order = [rest[:, :R_OFF_Q], qkv, rest[:, OFF_F:OFF_F + FOX_H], rest[:, OFF_C:]]
    return jnp.stack([jnp.concatenate(_columns(in_order, _SHARD_IN * d, _SHARD_IN * (d + 1)), axis=1) for d in range(N_DEV)])


_FIRST = ["w_in"]
_LATER = [k for k in _BIG if k not in _FIRST]


def _layer_weights(gathered):
    W = {}
    if "w_in" in gathered:
        W.update(zip(("qkv", "rest", "in"), _split_w_in([gathered["w_in"][d] for d in range(N_DEV)])))
    for name, key in _COL.items():
        if name in gathered:
            W[key] = _Gathered(gathered[name])
    if "xkv" in W:
        W["xkv"] = W["xkv"].arr.transpose(1, 0, 2).reshape(D, -1)
    for name, key in _ROW.items():
        if name in gathered:
            W[key] = gathered[name].reshape(-1, gathered[name].shape[-1])
    return W


def _grad_blocks(gw):
    parts = {}
    if "qkv" in gw:
        parts["w_in"] = _join_w_in(gw["qkv"], gw["rest"])
    for name, key in _COL.items():
        if key in gw:
            parts[name] = gw[key]
    for name, key in _ROW.items():
        if key in gw:
            parts[name] = gw[key].reshape(N_DEV, -1, gw[key].shape[-1])
    return parts


def kernel(x, mem, norm_mix_g, w_in, b_forget, pool_w, pool_scale, sgu_norm_g, sgu_w, sgu_b, w_branch_a, w_branch_b, w_branch_c, b_gate, w_out, norm_xattn_g, norm_mem_g, w_xq, w_xkv, w_xo, norm_ffn_g, w_ff1, w_ff2, final_norm_g, loss_target, m_norm_mix_g, m_w_in, m_b_forget, m_pool_w, m_pool_scale, m_sgu_norm_g, m_sgu_w, m_sgu_b, m_w_branch_a, m_w_branch_b, m_w_branch_c, m_b_gate, m_w_out, m_norm_xattn_g, m_norm_mem_g, m_w_xq, m_w_xkv, m_w_xo, m_norm_ffn_g, m_w_ff1, m_w_ff2, m_final_norm_g, v_norm_mix_g, v_w_in, v_b_forget, v_pool_w, v_pool_scale, v_sgu_norm_g, v_sgu_w, v_sgu_b, v_w_branch_a, v_w_branch_b, v_w_branch_c, v_b_gate, v_w_out, v_norm_xattn_g, v_norm_mem_g, v_w_xq, v_w_xkv, v_w_xo, v_norm_ffn_g, v_w_ff1, v_w_ff2, v_final_norm_g):
    names = ["norm_mix_g", "w_in", "b_forget", "pool_w", "pool_scale", "sgu_norm_g", "sgu_w", "sgu_b", "w_branch_a", "w_branch_b",
             "w_branch_c", "b_gate", "w_out", "norm_xattn_g", "norm_mem_g", "w_xq", "w_xkv", "w_xo", "norm_ffn_g", "w_ff1", "w_ff2",
             "final_norm_g"]
    w = dict(zip(names, [norm_mix_g, w_in, b_forget, pool_w, pool_scale, sgu_norm_g, sgu_w, sgu_b, w_branch_a, w_branch_b, w_branch_c,
                         b_gate, w_out, norm_xattn_g, norm_mem_g, w_xq, w_xkv, w_xo, norm_ffn_g, w_ff1, w_ff2, final_norm_g]))
    m = dict(zip(names, [m_norm_mix_g, m_w_in, m_b_forget, m_pool_w, m_pool_scale, m_sgu_norm_g, m_sgu_w, m_sgu_b, m_w_branch_a,
                         m_w_branch_b, m_w_branch_c, m_b_gate, m_w_out, m_norm_xattn_g, m_norm_mem_g, m_w_xq, m_w_xkv, m_w_xo,
                         m_norm_ffn_g, m_w_ff1, m_w_ff2, m_final_norm_g]))
    v = dict(zip(names, [v_norm_mix_g, v_w_in, v_b_forget, v_pool_w, v_pool_scale, v_sgu_norm_g, v_sgu_w, v_sgu_b, v_w_branch_a,
                         v_w_branch_b, v_w_branch_c, v_b_gate, v_w_out, v_norm_xattn_g, v_norm_mem_g, v_w_xq, v_w_xkv, v_w_xo,
                         v_norm_ffn_g, v_w_ff1, v_w_ff2, v_final_norm_g]))

    sp = {k: w[k] for k in _SMALL}
    shards = [{k: w[k][l].astype(BF16) for k in _BIG} for l in range(DEPTH)]
    me = _dev_index(*_position())

    def gather_out(l, keys, name, after=None):
        srcs = [shards[l][k] for k in keys]
        lands = [_own_block_placed(a, jax.ShapeDtypeStruct((N_DEV, *a.shape), a.dtype)) for a in srcs]
        state, token = _split_start(_plan_gather_out, srcs, lands, after=after, name=name + "_out_start")
        return (keys, name, state), token

    def gather_pass(job, value):
        keys, name, state = job
        lands = _split_wait(_plan_gather_out, state, value, name=name + "_out_wait")
        state, token = _split_start(_plan_gather_pass, [], lands, name=name + "_pass_start")
        return (keys, name, state), token, lands[0]

    def gather_end(job, value):
        keys, name, state = job
        return _layer_weights(dict(zip(keys, _split_wait(_plan_gather_pass, state, value, name=name + "_pass_wait"))))

    jobs = {}

    def source(l, point, value):
        if (l, point) == (0, "begin"):
            jobs["l0_first"], token = gather_out(0, _FIRST, "gather_l0_first")
            return {}, token
        if (l, point) == (0, "normed"):
            jobs["l0_first"], token, arrived = gather_pass(jobs["l0_first"], value)
            jobs["l0"], _ = gather_out(0, _LATER, "gather_l0", after=arrived)
            return gather_end(jobs.pop("l0_first"), token), None
        if (l, point) == (0, "attended"):
            jobs["l0"], _, arrived = gather_pass(jobs["l0"], value)
            jobs["l1_first"], token = gather_out(1, _FIRST, "gather_l1_first", after=arrived)
            jobs["l1"], jobs["token"] = gather_out(1, _LATER, "gather_l1", after=token)
            return {}, None
        if (l, point) == (0, "mixed"):
            return gather_end(jobs.pop("l0"), value), jobs.pop("token")
        if (l, point) == (0, "expanded"):
            jobs["l1_first"], token, _ = gather_pass(jobs["l1_first"], value)
            return {}, token
        if (l, point) == (1, "begin"):
            W = gather_end(jobs.pop("l1_first"), value)
            jobs["l1"], token, _ = gather_pass(jobs["l1"], value)
            return W, token
        if (l, point) == (1, "mixed"):
            return gather_end(jobs.pop("l1"), value), None
        return {}, None

    received = [{} for _ in range(DEPTH)]
    travelling = []

    def grads_done(l, gw):
        blocks = _grad_blocks(gw)
        keys = [k for k in _BIG if k in blocks]
        parts = [blocks[k] for k in keys]
        group = f"exchange_grads_l{l}_" + ("in" if "w_in" in blocks else "merge" if "w_out" in blocks else "mlp")
        lands = [_own_block_placed(lax.dynamic_index_in_dim(p, me, 0, keepdims=False), p) for p in parts]
        state, token = _split_start(_plan_exchange, parts, lands, name=group + "_start")
        travelling.append((l, keys, state, group + "_wait"))
        return token

    loss, dx, small = _local_step(x[0], mem[0], loss_target[0], sp, source, grads_done)
    grads, deltas, new_m, new_v = {}, {}, {}, {}
    like = [loss] + [w[k] for k in _SMALL]
    packed = _pack([loss] + [small[k] for k in _SMALL])
    eighths = packed.reshape(N_DEV, -1, _PACK_LANES)
    own = lambda a: _own_block_placed(lax.dynamic_index_in_dim(a, me, 0, keepdims=False) if a.ndim == 3 else a, eighths)
    scatter, done = _split_start(_plan_exchange, [eighths], [own(eighths)], after=dx, name="small_grads_scatter_start")

    def reduce_small(after):
        mine = _sum_blocks(_split_wait(_plan_exchange, scatter, after, name="small_grads_scatter_wait")[0], name="small_grads_sum")
        return _split_start(_plan_broadcast, [mine], [own(mine)], name="small_grads_gather_start")

    def update_small(state, after):
        total = _split_wait(_plan_broadcast, state, after, name="small_grads_gather_wait")[0].reshape(packed.shape)
        loss_sum, *g_small = _unpack(total, like)
        rows = lambda d: [_as_rows(d[k]) for k in _SMALL]
        outs = _adamw_small([_as_rows(g) for g in g_small], rows(w), rows(m), rows(v), name="adamw_small")
        grads.update(zip(_SMALL, g_small))
        for dst, vals in zip((deltas, new_m, new_v), outs):
            dst.update({k: a.reshape(w[k].shape) for k, a in zip(_SMALL, vals)})
        return loss_sum[0, 0], outs[0][0]

    groups = list(dict.fromkeys(tuple(keys) for _, keys, _, _ in travelling))
    for n_done, group_keys in enumerate(groups):
        if n_done == 1:
            gather, _ = reduce_small(done)
        if n_done == len(groups) - 1:
            loss, done = update_small(gather, done)
        for l, keys, state, wait_name in travelling:
            if tuple(keys) == group_keys:
                received[l].update(zip(keys, _split_wait(_plan_exchange, state, done, name=wait_name)))
        for k in group_keys:
            outs = _adamw_sharded([received[l][k] for l in range(DEPTH)], w[k], m[k], v[k], name="adamw_" + k)
            grads[k], deltas[k], new_m[k], new_v[k] = outs
        done = grads[group_keys[-1]]

    return (loss, dx[None], *[grads[k] for k in names], *[deltas[k] for k in names], *[new_m[k] for k in names],
            *[new_v[k] for k in names])
```

```python
import functools
import math

import jax
import jax.numpy as jnp
from jax import lax
from jax.experimental import pallas as pl
from jax.experimental.pallas import tpu as pltpu

F32 = jnp.float32
BF16 = jnp.bfloat16
MESH = pl.DeviceIdType.MESH

N_DEV = 8
D = 1024
DEPTH = 2
EPS = 1e-6
NEG = -1e30
POOL_W = 256
FOX_H = 8
FOX_DH = 64
FOX_W = 512
SGU_W = 256
SGU_CHUNK = 128
XH = 4
XDH = 256
N_IN = 5384
R_OFF_Q, R_OFF_F, R_OFF_C = 256, 1792, 1800
QKV_W = 3 * FOX_W
OFF_A, OFF_F, OFF_C, OFF_G, REST_W = 0, 256, 512, 1024, 4096
F_LANES = 128

ADAM_LR = 0.001
ADAM_B1 = 0.9
ADAM_B2 = 0.999
ADAM_EPS = 1e-08
ADAM_WD = 0.01
ADAM_STEP = 10

VMEM_LIMIT = 56 * 1024 * 1024


def _tile(n, pref):
    t = min(n, pref)
    while n % t:
        t -= 128
    assert t > 0, (n, pref)
    return t


def _params(sem=None):
    return pltpu.CompilerParams(dimension_semantics=sem, vmem_limit_bytes=VMEM_LIMIT)


def _dot(a, b, ca, cb):
    return lax.dot_general(a, b, (((ca,), (cb,)), ((), ())), preferred_element_type=F32)


def _sigmoid(z):
    return 1.0 / (1.0 + jnp.exp(-z))


_GELU_K = math.sqrt(2.0 / math.pi)
_GELU_C = 0.044715


def _gelu(x):
    return 0.5 * x * (1.0 + jnp.tanh(_GELU_K * (x + _GELU_C * x * x * x)))


def _gelu_grad(x):
    t = jnp.tanh(_GELU_K * (x + _GELU_C * x * x * x))
    return 0.5 * (1.0 + t) + 0.5 * x * (1.0 - t * t) * _GELU_K * (1.0 + 3.0 * _GELU_C * x * x)


def _rows(shape):
    return lax.broadcasted_iota(jnp.int32, shape, 0)


def _lanes(shape):
    return lax.broadcasted_iota(jnp.int32, shape, 1)


class _Gathered:
    def __init__(self, arr):
        self.arr = arr
        self.shape = (arr.shape[1], N_DEV * arr.shape[2])


_TOKEN = (8, 128)


def _mm(a, b, *, ta=False, tb=False, extras=(), row_extras=(), epilogue=None, out_dtypes=(F32,), row_outs=0, shard_out=False, after=None,
        tm=None, tn=512, tk=None, name):
    a_parts = list(a) if isinstance(a, (list, tuple)) else [a]
    b_parts = list(b) if isinstance(b, (list, tuple)) else [b]
    gathered = isinstance(b, _Gathered)
    assert (len(a_parts) == 1 or not ta) and (len(b_parts) == 1 or not tb) and min(len(a_parts), len(b_parts)) == 1
    a0, b0 = a_parts[0], b_parts[0]
    M, K = (a0.shape[1], a0.shape[0]) if ta else (a0.shape[0], sum(p.shape[1] for p in a_parts))
    N, Kb = b0.shape if tb else (b0.shape[1] * len(b_parts), b0.shape[0])
    assert Kb == K, (a0.shape, b0.shape, ta, tb)
    if gathered:
        if tb:
            tk = b.arr.shape[2]
        else:
            tn = b.arr.shape[2]
    if len(a_parts) > 1:
        tk = math.gcd(*[p.shape[1] for p in a_parts])
    if shard_out:
        tn = N // N_DEV
    tm = _tile(M, tm or (1024 if ta else 2048))
    tn = _tile(b0.shape[1] if len(b_parts) > 1 else N, tn)
    per_piece = b0.shape[1] // tn
    size = lambda dt: jnp.dtype(dt).itemsize
    row_bytes = len(a_parts) * tm * size(a0.dtype) + len(b_parts) * tn * size(b.arr.dtype if gathered else b0.dtype)
    tile_bytes = tm * tn * (sum(size(e.dtype) for e in extras) + sum(map(size, out_dtypes)))

    def vmem_bytes(k_tile):
        return 2 * (k_tile * row_bytes + tile_bytes) + tm * tn * 4 * (K > k_tile)

    if tk is None:
        tk = next(c for c in (_tile(K, 2048), _tile(K, 1024), _tile(K, 512), _tile(K, 256)) if vmem_bytes(c) <= VMEM_LIMIT - (4 << 20))
    tk = _tile(K, tk)
    nk = K // tk
    ca, cb = (0 if ta else 1), (1 if tb else 0)
    n_a, n_b, n_ex, n_out = len(a_parts), len(b_parts), len(extras) + len(row_extras), len(out_dtypes)
    tokens = [] if after is None else [after]
    n_in = n_a + n_b + n_ex + len(tokens)
    if epilogue is None:
        epilogue = lambda acc: (acc,)

    def body(*refs):
        a_refs, b_refs = refs[:n_a], refs[n_a:n_a + n_b]
        ex_refs = refs[n_a + n_b:n_a + n_b + n_ex]
        o_refs = refs[n_in:n_in + n_out]
        j, k = pl.program_id(1), pl.program_id(2)

        def finish(acc):
            vals = epilogue(acc, *[e[...] for e in ex_refs])
            for o_ref, val in zip(o_refs[:n_out - row_outs], vals):
                o_ref[...] = val.astype(o_ref.dtype)
            for o_ref, val in zip(o_refs[n_out - row_outs:], vals[n_out - row_outs:]):
                first = pl.program_id(0) == 0
                o_ref[...] = jnp.where(first, val, o_ref[...] + val)

        def step(a_ref, b_ref):
            part = _dot(a_ref[...].astype(BF16), b_ref[...].astype(BF16), ca, cb)
            if nk == 1:
                finish(part)
            else:
                acc_ref = refs[-1]

                @pl.when(k == 0)
                def _():
                    acc_ref[...] = part

                @pl.when(k > 0)
                def _():
                    acc_ref[...] += part

                @pl.when(k == nk - 1)
                def _():
                    finish(acc_ref[...])

        if n_a > 1:
            for p in range(n_a):
                pl.when((k >= a_first[p]) & (k < a_first[p + 1]))(functools.partial(step, a_refs[p], b_refs[0]))
        elif n_b > 1:
            for p in range(n_b):
                pl.when(j // per_piece == p)(functools.partial(step, a_refs[0], b_refs[p]))
        else:
            step(a_refs[0], b_refs[0])

    if n_a > 1:
        a_first = [sum(p.shape[1] for p in a_parts[:q]) // tk for q in range(n_a + 1)]
        a_specs = [pl.BlockSpec((tm, tk), functools.partial(
            lambda p, i, j, k: (i, jnp.clip(k - a_first[p], 0, a_first[p + 1] - a_first[p] - 1)), p)) for p in range(n_a)]
    else:
        a_specs = [pl.BlockSpec((tk, tm), lambda i, j, k: (k, i)) if ta else pl.BlockSpec((tm, tk), lambda i, j, k: (i, k))]
    if gathered:
        b_arrs = [b.arr]
        b_specs = [pl.BlockSpec((None, tn, tk), lambda i, j, k: (k, j, 0)) if tb else pl.BlockSpec((None, tk, tn), lambda i, j, k: (j, k, 0))]
    elif n_b > 1:
        b_arrs = b_parts
        b_specs = [pl.BlockSpec((tk, tn), functools.partial(lambda p, i, j, k: (k, jnp.clip(j - p * per_piece, 0, per_piece - 1)), p))
                   for p in range(n_b)]
    else:
        b_arrs = b_parts
        b_specs = [pl.BlockSpec((tn, tk), lambda i, j, k: (j, k)) if tb else pl.BlockSpec((tk, tn), lambda i, j, k: (k, j))]
    tile = pl.BlockSpec((tm, tn), lambda i, j, k: (i, j))
    if shard_out:
        out_specs = [pl.BlockSpec((None, tm, tn), lambda i, j, k: (j, i, 0))] * n_out
        out_shape = [jax.ShapeDtypeStruct((N_DEV, M, tn), dt) for dt in out_dtypes]
    else:
        assert row_outs == 0 or tn == N
        out_specs = [tile] * (n_out - row_outs) + [pl.BlockSpec((1, tn), lambda i, j, k: (0, j))] * row_outs
        out_shape = [jax.ShapeDtypeStruct((1, N) if t >= n_out - row_outs else (M, N), dt) for t, dt in enumerate(out_dtypes)]
    assert vmem_bytes(tk) <= VMEM_LIMIT - (4 << 20), (name, vmem_bytes(tk))
    outs = pl.pallas_call(
        body,
        name=name,
        grid=(M // tm, N // tn, nk),
        in_specs=a_specs + b_specs + [tile] * len(extras) + [pl.BlockSpec((1, tn), lambda i, j, k: (0, j))] * len(row_extras)
        + [pl.BlockSpec(_TOKEN, lambda i, j, k: (0, 0))] * len(tokens),
        out_specs=out_specs,
        out_shape=out_shape,
        scratch_shapes=[pltpu.VMEM((tm, tn), F32)] if nk > 1 else [],
        compiler_params=_params(("arbitrary",) * 3 if row_outs else ("parallel", "parallel", "arbitrary")),
    )(*a_parts, *b_arrs, *extras, *row_extras, *tokens)
    return outs[0] if n_out == 1 else outs


def _add(acc, res):
    return (acc + res,)


def _norm_grad(dh, x, dres, g):
    r = lax.rsqrt(jnp.mean(x * x, axis=-1, keepdims=True) + EPS)
    xn = x * r
    dxn = dh * g
    return r * (dxn - xn * jnp.mean(dxn * xn, axis=-1, keepdims=True)) + dres, jnp.sum(dh * xn, axis=0, keepdims=True)


def _add_norm(acc, res, g):
    x = acc + res
    return x, x * lax.rsqrt(jnp.mean(x * x, axis=-1, keepdims=True) + EPS) * g


def _rms_fwd(x, g, *, after=None, name):
    R, C = x.shape
    tm = _tile(R, 256)
    tokens = [] if after is None else [after]

    def body(x_ref, g_ref, *rest):
        xv = x_ref[...]
        r = lax.rsqrt(jnp.mean(xv * xv, axis=-1, keepdims=True) + EPS)
        rest[-1][...] = (xv * r * g_ref[...]).astype(BF16)

    return pl.pallas_call(
        body,
        name=name,
        grid=(R // tm,),
        in_specs=[pl.BlockSpec((tm, C), lambda i: (i, 0)), pl.BlockSpec((1, C), lambda i: (0, 0))]
        + [pl.BlockSpec(_TOKEN, lambda i: (0, 0))] * len(tokens),
        out_specs=pl.BlockSpec((tm, C), lambda i: (i, 0)),
        out_shape=jax.ShapeDtypeStruct((R, C), BF16),
        compiler_params=_params(("parallel",)),
    )(x, g.reshape(1, C), *tokens)


def _rms_bwd(x, g, dh, dres, *, name):
    R, C = x.shape
    tm = _tile(R, 256)

    def body(x_ref, g_ref, dh_ref, dres_ref, dx_ref, dg_ref):
        xv = x_ref[...]
        r = lax.rsqrt(jnp.mean(xv * xv, axis=-1, keepdims=True) + EPS)
        xn = xv * r
        dh_v = dh_ref[...].astype(F32)
        dxn = dh_v * g_ref[...]
        dx_ref[...] = r * (dxn - xn * jnp.mean(dxn * xn, axis=-1, keepdims=True)) + dres_ref[...]
        part = jnp.sum(dh_v * xn, axis=0, keepdims=True)

        @pl.when(pl.program_id(0) == 0)
        def _():
            dg_ref[...] = part

        @pl.when(pl.program_id(0) > 0)
        def _():
            dg_ref[...] += part

    row = pl.BlockSpec((tm, C), lambda i: (i, 0))
    vec = pl.BlockSpec((1, C), lambda i: (0, 0))
    dx, dg = pl.pallas_call(
        body,
        name=name,
        grid=(R // tm,),
        in_specs=[row, vec, row, row],
        out_specs=[row, vec],
        out_shape=[jax.ShapeDtypeStruct((R, C), F32), jax.ShapeDtypeStruct((1, C), F32)],
        compiler_params=_params(("arbitrary",)),
    )(x, g.reshape(1, C), dh, dres)
    return dx, dg.reshape(C)


def _final_loss(x, g, target, *, name):
    R, C = x.shape
    tm = _tile(R, 256)

    def body(x_ref, g_ref, t_ref, loss_ref, dx_ref, dg_ref):
        xv = x_ref[...]
        r = lax.rsqrt(jnp.mean(xv * xv, axis=-1, keepdims=True) + EPS)
        xn = xv * r
        gv = g_ref[...]
        err = xn * gv - t_ref[...]
        lpart = (0.5 / C) * jnp.sum(jnp.sum(err * err, axis=1, keepdims=True), axis=0, keepdims=True)
        dy = err * (1.0 / C)
        dxn = dy * gv
        dx_ref[...] = r * (dxn - xn * jnp.mean(dxn * xn, axis=-1, keepdims=True))
        gpart = jnp.sum(dy * xn, axis=0, keepdims=True)

        @pl.when(pl.program_id(0) == 0)
        def _():
            loss_ref[...] = lpart
            dg_ref[...] = gpart

        @pl.when(pl.program_id(0) > 0)
        def _():
            loss_ref[...] += lpart
            dg_ref[...] += gpart

    row = pl.BlockSpec((tm, C), lambda i: (i, 0))
    vec = pl.BlockSpec((1, C), lambda i: (0, 0))
    loss, dx, dg = pl.pallas_call(
        body,
        name=name,
        grid=(R // tm,),
        in_specs=[row, vec, row],
        out_specs=[pl.BlockSpec((1, 1), lambda i: (0, 0)), row, vec],
        out_shape=[jax.ShapeDtypeStruct((1, 1), F32), jax.ShapeDtypeStruct((R, C), F32), jax.ShapeDtypeStruct((1, C), F32)],
        compiler_params=_params(("arbitrary",)),
    )(x, g.reshape(1, C), target)
    return loss, dx, dg.reshape(C)


def _pool_select(lane, vals):
    out = vals[3]
    for gi in (2, 1, 0):
        out = jnp.where(lane < 64 * (gi + 1), vals[gi], out)
    return out


def _pool_diff(a):
    row, lane = _rows(a.shape), _lanes(a.shape)

    def down(v, k):
        return jnp.where(row >= k, pltpu.roll(v, k, 0), 0.0)

    s2 = a + down(a, 1)
    s4 = s2 + down(s2, 2)
    s8 = s4 + down(s4, 4)
    s16 = s8 + down(s8, 8)
    wsum = _pool_select(lane, (s2, s4, s8, s16))
    win = _pool_select(lane, (2, 4, 8, 16))
    cnt = jnp.minimum(row + 1, win).astype(F32)
    return wsum / cnt - a, cnt


def _pool_diff_t(dd, cnt):
    S = dd.shape[0]
    row, lane = _rows(dd.shape), _lanes(dd.shape)

    def up(v, k):
        return jnp.where(row < S - k, pltpu.roll(v, S - k, 0), 0.0)

    e = dd / cnt
    s2 = e + up(e, 1)
    s4 = s2 + up(s2, 2)
    s8 = s4 + up(s4, 4)
    s16 = s8 + up(s8, 8)
    return _pool_select(lane, (s2, s4, s8, s16)) - dd


def _pool_fwd(rest, wbd, scale, *, name):
    S = rest.shape[0]

    def body(a_ref, w_ref, s_ref, o_ref):
        d, _ = _pool_diff(a_ref[...])
        yp = _dot(d.astype(BF16), w_ref[...], 1, 0)
        o_ref[...] = (yp * s_ref[...]).astype(BF16)

    return pl.pallas_call(
        body,
        name=name,
        grid=(1,),
        in_specs=[
            pl.BlockSpec((S, POOL_W), lambda i: (0, OFF_A // POOL_W)),
            pl.BlockSpec((POOL_W, POOL_W), lambda i: (0, 0)),
            pl.BlockSpec((1, POOL_W), lambda i: (0, 0)),
        ],
        out_specs=pl.BlockSpec((S, POOL_W), lambda i: (0, 0)),
        out_shape=jax.ShapeDtypeStruct((S, POOL_W), BF16),
        compiler_params=_params(("arbitrary",)),
    )(rest, wbd, scale.reshape(1, POOL_W))


def _pool_bwd(rest, wbd, wbd_t, scale, dpa, *, name):
    S = rest.shape[0]

    def body(a_ref, w_ref, wt_ref, s_ref, dpa_ref, da_ref, dw_ref, ds_ref):
        d, cnt = _pool_diff(a_ref[...])
        db = d.astype(BF16)
        yp = _dot(db, w_ref[...], 1, 0)
        dpa_v = dpa_ref[...]
        ds_ref[...] = jnp.sum(dpa_v * yp, axis=0, keepdims=True)
        dyp = (dpa_v * s_ref[...]).astype(BF16)
        dw_ref[...] = _dot(db, dyp, 0, 0)
        dd = _dot(dyp, wt_ref[...], 1, 0)
        da_ref[...] = _pool_diff_t(dd, cnt).astype(BF16)

    full = pl.BlockSpec((S, POOL_W), lambda i: (0, 0))
    sq = pl.BlockSpec((POOL_W, POOL_W), lambda i: (0, 0))
    vec = pl.BlockSpec((1, POOL_W), lambda i: (0, 0))
    return pl.pallas_call(
        body,
        name=name,
        grid=(1,),
        in_specs=[pl.BlockSpec((S, POOL_W), lambda i: (0, OFF_A // POOL_W)), sq, sq, vec, full],
        out_specs=[full, sq, vec],
        out_shape=[
            jax.ShapeDtypeStruct((S, POOL_W), BF16),
            jax.ShapeDtypeStruct((POOL_W, POOL_W), F32),
            jax.ShapeDtypeStruct((1, POOL_W), F32),
        ],
        compiler_params=_params(("arbitrary",)),
    )(rest, wbd, wbd_t, scale.reshape(1, POOL_W), dpa)


def _log_sigmoid(z):
    return jnp.minimum(z, 0.0) - jnp.log(1.0 + jnp.exp(-jnp.abs(z)))


_F_SPEC_COL = OFF_F // F_LANES


def _fox_prep(rest, bpad, *, name):
    S = rest.shape[0]

    def body(f_ref, b_ref, o_ref, ot_ref):
        acc = _log_sigmoid(f_ref[...] + b_ref[...])
        row = _rows(acc.shape)
        k = 1
        while k < S:
            acc = acc + jnp.where(row >= k, pltpu.roll(acc, k, 0), 0.0)
            k *= 2
        o_ref[...] = acc
        ot_ref[...] = acc.T

    return pl.pallas_call(
        body,
        name=name,
        grid=(1,),
        in_specs=[pl.BlockSpec((S, F_LANES), lambda i: (0, _F_SPEC_COL)), pl.BlockSpec((1, F_LANES), lambda i: (0, 0))],
        out_specs=[pl.BlockSpec((S, F_LANES), lambda i: (0, 0)), pl.BlockSpec((F_LANES, S), lambda i: (0, 0))],
        out_shape=[jax.ShapeDtypeStruct((S, F_LANES), F32), jax.ShapeDtypeStruct((F_LANES, S), F32)],
        compiler_params=_params(("arbitrary",)),
    )(rest, bpad)


def _fox_post(rest, bpad, dcum, *, name):
    S = rest.shape[0]

    def body(f_ref, b_ref, d_ref, df_ref, db_ref):
        acc = d_ref[...]
        row = _rows(acc.shape)
        k = 1
        while k < S:
            acc = acc + jnp.where(row < S - k, pltpu.roll(acc, S - k, 0), 0.0)
            k *= 2
        df = acc * (1.0 - _sigmoid(f_ref[...] + b_ref[...]))
        df_ref[...] = df.astype(BF16)
        db_ref[...] = jnp.sum(df, axis=0, keepdims=True)

    full = pl.BlockSpec((S, F_LANES), lambda i: (0, 0))
    vec = pl.BlockSpec((1, F_LANES), lambda i: (0, 0))
    return pl.pallas_call(
        body,
        name=name,
        grid=(1,),
        in_specs=[pl.BlockSpec((S, F_LANES), lambda i: (0, _F_SPEC_COL)), vec, full],
        out_specs=[full, vec],
        out_shape=[jax.ShapeDtypeStruct((S, F_LANES), BF16), jax.ShapeDtypeStruct((1, F_LANES), F32)],
        compiler_params=_params(("arbitrary",)),
    )(rest, bpad, dcum)


_FOX_SCALE = FOX_DH ** -0.5
_PAIRS = FOX_H // 2


def _scaled(v):
    return (v.astype(F32) * _FOX_SCALE).astype(BF16)


def _diag_mask(s):
    return jnp.where(_rows(s.shape) >= _lanes(s.shape), s, NEG)


def _fox_fwd(qkv, cum, fk3, *, name):
    S = qkv.shape[0]
    nk, t = fk3.shape[1:]

    def body(q_ref, k_ref, v_ref, cum_ref, fk_ref, o_ref, lse_ref):
        i = pl.program_id(0)
        lane = _lanes((t, 128))
        lo = lane < FOX_DH
        cumv = cum_ref[...]
        qm, fq = [], []
        for h in range(FOX_H):
            qs = _scaled(q_ref[:, 128 * (h // 2):128 * (h // 2 + 1)])
            zero = jnp.zeros_like(qs)
            qm.append(jnp.where(lo, qs, zero) if h % 2 == 0 else jnp.where(lo, zero, qs))
            fq.append(jnp.broadcast_to(cumv[:, h:h + 1], (t, 128)))

        def tile(j, state, masked):
            m, acc, lsum = (list(part) for part in state)
            k0 = pl.multiple_of(j * t, t)
            for hp in range(_PAIRS):
                cols = slice(128 * hp, 128 * (hp + 1))
                kb = k_ref[pl.ds(k0, t), cols]
                vb = v_ref[pl.ds(k0, t), cols]
                one = jnp.ones_like(vb)
                alphas, pvs = [], []
                for h in (2 * hp, 2 * hp + 1):
                    s = _dot(qm[h], kb, 1, 1) + jnp.concatenate([fq[h]] * (t // 128), axis=1) - fk_ref[h, pl.ds(j, 1), :]
                    if masked:
                        s = _diag_mask(s)
                    m_new = jnp.maximum(m[h], jnp.max(s, axis=-1, keepdims=True))
                    p = jnp.exp(s - m_new)
                    alphas.append(jnp.exp(m[h] - m_new))
                    m[h] = m_new
                    pvs.append(_dot(p.astype(BF16), jnp.where(lo, vb, one) if h % 2 == 0 else jnp.where(lo, one, vb), 1, 0))
                acc[hp] = jnp.where(lo, alphas[0], alphas[1]) * acc[hp] + jnp.where(lo, pvs[0], pvs[1])
                lsum[hp] = jnp.where(lo, alphas[1], alphas[0]) * lsum[hp] + jnp.where(lo, pvs[1], pvs[0])
            return tuple(m), tuple(acc), tuple(lsum)

        zeros = (jnp.zeros((t, 128), F32),) * _PAIRS
        state = lax.fori_loop(0, i, functools.partial(tile, masked=False), ((jnp.full((t, 1), NEG, F32),) * FOX_H, zeros, zeros))
        m, acc, lsum = tile(i, state, True)
        for hp in range(_PAIRS):
            o_ref[:, 128 * hp:128 * (hp + 1)] = acc[hp] / pltpu.roll(lsum[hp], FOX_DH, 1)
            lse = [m[2 * hp] + jnp.log(lsum[hp][:, FOX_DH:FOX_DH + 1]), m[2 * hp + 1] + jnp.log(lsum[hp][:, 0:1])]
            lse_ref[hp] = jnp.where(lane == 0, lse[0], jnp.where(lane == 1, lse[1], 0.0))

    whole = lambda col: pl.BlockSpec((S, FOX_W), lambda i: (0, col))
    return pl.pallas_call(
        body,
        name=name,
        grid=(S // t,),
        in_specs=[
            pl.BlockSpec((t, FOX_W), lambda i: (i, 0)), whole(1), whole(2),
            pl.BlockSpec((t, F_LANES), lambda i: (i, 0)),
            pl.BlockSpec((FOX_H, nk, t), lambda i: (0, 0, 0)),
        ],
        out_specs=[pl.BlockSpec((t, FOX_W), lambda i: (i, 0)), pl.BlockSpec((_PAIRS, t, 128), lambda i: (0, i, 0))],
        out_shape=[jax.ShapeDtypeStruct((S, FOX_W), F32), jax.ShapeDtypeStruct((_PAIRS, S, 128), F32)],
        compiler_params=_params(("arbitrary",)),
    )(qkv, qkv, qkv, cum, fk3)


def _fox_bwd(qkv, cum, fk3, o, do, lse, *, name):
    S = qkv.shape[0]
    nk, t = fk3.shape[1:]
    q_at, k_at, v_at = 0, FOX_W, 2 * FOX_W

    def body(qkv_ref, cum_ref, fk_ref, o_ref, do_ref, lse_ref, dq_ref, dk_ref, dv_ref, dfq_ref, dfk_ref,
             qs_sc, ks_sc, bias_sc, delta_sc, dq_sc):
        lane = _lanes((t, 128))
        lo = lane < FOX_DH
        mine = lambda h: lo if h % 2 == 0 else jnp.logical_not(lo)

        def by_head(tile, values):
            for h, val in enumerate(values):
                tile = jnp.where(lane == h, val, tile)
            return tile

        def prep(i, carry):
            r = pl.ds(pl.multiple_of(i * t, t), t)
            qs_sc[r, :] = _scaled(qkv_ref[r, q_at:q_at + FOX_W])
            ks_sc[r, :] = _scaled(qkv_ref[r, k_at:k_at + FOX_W])
            cum_t = cum_ref[r, :]
            for hp in range(_PAIRS):
                cols = slice(128 * hp, 128 * (hp + 1))
                prod = do_ref[r, cols].astype(F32) * o_ref[r, cols]
                for h in (2 * hp, 2 * hp + 1):
                    delta = jnp.sum(jnp.where(mine(h), prod, 0.0), axis=-1, keepdims=True)
                    delta_sc[h, r, :] = jnp.broadcast_to(delta, (t, 128))
                    bias_sc[h, r, :] = jnp.broadcast_to(cum_t[:, h:h + 1] - lse_ref[hp, r, h % 2:h % 2 + 1], (t, 128))
            dfq_ref[r, :] = jnp.zeros((t, 128), F32)
            dq_sc[r, :] = jnp.zeros((t, FOX_W), F32)
            return carry

        lax.fori_loop(0, nk, prep, 0)

        def kv_tile(j, carry):
            kr = pl.ds(pl.multiple_of(j * t, t), t)

            def q_tile(i, acc, masked):
                dk, dv, dfk = list(acc[:_PAIRS]), list(acc[_PAIRS:2 * _PAIRS]), list(acc[2 * _PAIRS:])
                qr = pl.ds(pl.multiple_of(i * t, t), t)
                dq_old, dfq_old = dq_sc[qr, :], dfq_ref[qr, :]
                wide = lambda a: jnp.concatenate([a] * (t // 128), axis=1)
                row_sums, dq_new = [], []
                for hp in range(_PAIRS):
                    cols = slice(128 * hp, 128 * (hp + 1))
                    kb = qkv_ref[kr, k_at + 128 * hp:k_at + 128 * (hp + 1)]
                    vb = qkv_ref[kr, v_at + 128 * hp:v_at + 128 * (hp + 1)]
                    ksb, qsb, dob = ks_sc[kr, cols], qs_sc[qr, cols], do_ref[qr, cols]
                    zero = jnp.zeros_like(qsb)
                    dq_t = jnp.zeros((t, 128), F32)
                    for h in (2 * hp, 2 * hp + 1):
                        qe, doe, ke = (jnp.where(mine(h), a, zero) for a in (qsb, dob, ksb))
                        s = _dot(qe, kb, 1, 1) + wide(bias_sc[h, qr, :]) - fk_ref[h, pl.ds(j, 1), :]
                        if masked:
                            s = _diag_mask(s)
                        p = jnp.exp(s)
                        dv[hp] = dv[hp] + _dot(p.astype(BF16), doe, 0, 0)
                        dp = _dot(doe, vb, 1, 1)
                        ds = p * (dp - wide(delta_sc[h, qr, :]))
                        dsb = ds.astype(BF16)
                        dk[hp] = dk[hp] + _dot(dsb, qe, 0, 0)
                        dq_t = dq_t + _dot(dsb, ke, 1, 0)
                        row_sums.append(jnp.sum(ds, axis=-1, keepdims=True))
                        dfk[h] = dfk[h] - jnp.sum(ds, axis=0, keepdims=True)
                    dq_new.append(dq_old[:, cols] + dq_t)
                for hp in range(_PAIRS):
                    dq_sc[qr, 128 * hp:128 * (hp + 1)] = dq_new[hp]
                dfq_ref[qr, :] = dfq_old + by_head(jnp.zeros((t, 128), F32), row_sums)
                return (*dk, *dv, *dfk)

            init = tuple([jnp.zeros((t, 128), F32)] * (2 * _PAIRS) + [jnp.zeros((1, t), F32)] * FOX_H)
            acc = q_tile(j, init, True)
            acc = lax.fori_loop(j + 1, nk, functools.partial(q_tile, masked=False), acc)
            for hp in range(_PAIRS):
                cols = slice(128 * hp, 128 * (hp + 1))
                dk_ref[kr, cols] = acc[hp].astype(BF16)
                dv_ref[kr, cols] = acc[_PAIRS + hp].astype(BF16)
            for h in range(FOX_H):
                dfk_ref[h, pl.ds(j, 1), :] = acc[2 * _PAIRS + h]
            return carry

        lax.fori_loop(0, nk, kv_tile, 0)
        dq_ref[...] = dq_sc[...].astype(BF16)

    vm = pl.BlockSpec(memory_space=pltpu.VMEM)
    big = jax.ShapeDtypeStruct((S, FOX_W), BF16)
    return pl.pallas_call(
        body,
        name=name,
        in_specs=[vm] * 6,
        out_specs=[vm] * 5,
        out_shape=[big, big, big, jax.ShapeDtypeStruct((S, 128), F32), jax.ShapeDtypeStruct((FOX_H, nk, t), F32)],
        scratch_shapes=[pltpu.VMEM((S, FOX_W), BF16), pltpu.VMEM((S, FOX_W), BF16), pltpu.VMEM((FOX_H, S, 128), F32),
                        pltpu.VMEM((FOX_H, S, 128), F32), pltpu.VMEM((S, FOX_W), F32)],
        compiler_params=pltpu.CompilerParams(vmem_limit_bytes=VMEM_LIMIT),
    )(qkv, cum, fk3, o, do, lse)


def _group_mask(lane, gi):
    return (lane >= 64 * gi) & (lane < 64 * (gi + 1))


_U_COL = OFF_C // SGU_W


def _sgu_fwd(rest, gn, wm, bias, *, name):
    S = rest.shape[0]
    ts = _tile(S, 512)
    nc = ts // SGU_CHUNK

    def body(u_ref, v_ref, g_ref, w_ref, b_ref, o_ref):
        zv = _gelu(v_ref[...])
        vn = zv * lax.rsqrt(jnp.mean(zv * zv, axis=-1, keepdims=True) + EPS) * g_ref[...]
        lane = _lanes((SGU_CHUNK, SGU_W))
        for c in range(nc):
            rows = slice(c * SGU_CHUNK, (c + 1) * SGU_CHUNK)
            vcb = vn[rows].astype(BF16)
            mixed = b_ref[...]
            for gi in range(4):
                mixed = mixed + jnp.where(_group_mask(lane, gi), _dot(w_ref[gi], vcb, 1, 0), 0.0)
            o_ref[rows, :] = (_gelu(u_ref[rows, :]) * mixed).astype(BF16)

    return pl.pallas_call(
        body,
        name=name,
        grid=(S // ts,),
        in_specs=[
            pl.BlockSpec((ts, SGU_W), lambda i: (i, _U_COL)),
            pl.BlockSpec((ts, SGU_W), lambda i: (i, _U_COL + 1)),
            pl.BlockSpec((1, SGU_W), lambda i: (0, 0)),
            pl.BlockSpec((4, SGU_CHUNK, SGU_CHUNK), lambda i: (0, 0, 0)),
            pl.BlockSpec((SGU_CHUNK, SGU_W), lambda i: (0, 0)),
        ],
        out_specs=pl.BlockSpec((ts, SGU_W), lambda i: (i, 0)),
        out_shape=jax.ShapeDtypeStruct((S, SGU_W), BF16),
        compiler_params=_params(("parallel",)),
    )(rest, rest, gn.reshape(1, SGU_W), wm, bias)


def _sgu_bwd(rest, gn, wm, wm_t, bias, dsg, *, name):
    S = rest.shape[0]
    ts = _tile(S, 512)
    nc = ts // SGU_CHUNK

    def body(u_ref, v_ref, g_ref, w_ref, wt_ref, b_ref, dsg_ref, dc_ref, dw_ref, db_ref, dg_ref):
        first = pl.program_id(0) == 0

        @pl.when(first)
        def _():
            dw_ref[...] = jnp.zeros_like(dw_ref)
            db_ref[...] = jnp.zeros_like(db_ref)
            dg_ref[...] = jnp.zeros_like(dg_ref)

        gv = g_ref[...]
        lane = _lanes((SGU_CHUNK, SGU_W))
        for c in range(nc):
            rows = slice(c * SGU_CHUNK, (c + 1) * SGU_CHUNK)
            vpre = v_ref[rows, :]
            upre = u_ref[rows, :]
            zv = _gelu(vpre)
            r = lax.rsqrt(jnp.mean(zv * zv, axis=-1, keepdims=True) + EPS)
            zn = zv * r
            vcb = (zn * gv).astype(BF16)
            mixed = b_ref[...]
            for gi in range(4):
                mixed = mixed + jnp.where(_group_mask(lane, gi), _dot(w_ref[gi], vcb, 1, 0), 0.0)
            zu = _gelu(upre)
            dsg_v = dsg_ref[rows, :]
            dc_ref[rows, :SGU_W] = (dsg_v * mixed * _gelu_grad(upre)).astype(BF16)
            dmixed = dsg_v * zu
            db_ref[...] += dmixed
            dvn = jnp.zeros((SGU_CHUNK, SGU_W), F32)
            for gi in range(4):
                dmg = jnp.where(_group_mask(lane, gi), dmixed, 0.0).astype(BF16)
                dw_ref[gi] += _dot(dmg, vcb, 1, 1)
                dvn = dvn + _dot(wt_ref[gi], dmg, 1, 0)
            dg_ref[...] += jnp.sum(dvn * zn, axis=0, keepdims=True)
            dzn = dvn * gv
            dzv = r * (dzn - zn * jnp.mean(dzn * zn, axis=-1, keepdims=True))
            dc_ref[rows, SGU_W:] = (dzv * _gelu_grad(vpre)).astype(BF16)

    blk = pl.BlockSpec((ts, SGU_W), lambda i: (i, 0))
    vec = pl.BlockSpec((1, SGU_W), lambda i: (0, 0))
    w3 = pl.BlockSpec((4, SGU_CHUNK, SGU_CHUNK), lambda i: (0, 0, 0))
    bsp = pl.BlockSpec((SGU_CHUNK, SGU_W), lambda i: (0, 0))
    return pl.pallas_call(
        body,
        name=name,
        grid=(S // ts,),
        in_specs=[
            pl.BlockSpec((ts, SGU_W), lambda i: (i, _U_COL)),
            pl.BlockSpec((ts, SGU_W), lambda i: (i, _U_COL + 1)),
            vec, w3, w3, bsp, blk,
        ],
        out_specs=[pl.BlockSpec((ts, 2 * SGU_W), lambda i: (i, 0)), w3, bsp, vec],
        out_shape=[
            jax.ShapeDtypeStruct((S, 2 * SGU_W), BF16),
            jax.ShapeDtypeStruct((4, SGU_CHUNK, SGU_CHUNK), F32),
            jax.ShapeDtypeStruct((SGU_CHUNK, SGU_W), F32),
            jax.ShapeDtypeStruct((1, SGU_W), F32),
        ],
        compiler_params=_params(("arbitrary",)),
    )(rest, rest, gn.reshape(1, SGU_W), wm, wm_t, bias, dsg)


_GT = 512
_G0 = OFF_G // _GT


def _gate_specs(tm, col_of):
    specs = [pl.BlockSpec((tm, _GT), functools.partial(lambda k, *ids: (col_of(*ids)[0], _G0 + 2 * k + col_of(*ids)[1]), k)) for k in range(3)]
    specs += [pl.BlockSpec((1, _GT), functools.partial(lambda k, *ids: (0, 2 * k + col_of(*ids)[1]), k)) for k in range(3)]
    return specs


def _merge_fwd(rest, bg, ya, yb, yc, *, name):
    S = rest.shape[0]
    tm = _tile(S, 512)

    def body(g1, g2, g3, b1, b2, b3, ya_ref, yb_ref, yc_ref, o_ref):
        acc = _sigmoid(g1[...] + b1[...]) * ya_ref[...]
        acc = acc + _sigmoid(g2[...] + b2[...]) * yb_ref[...]
        acc = acc + _sigmoid(g3[...] + b3[...]) * yc_ref[...]
        o_ref[...] = acc.astype(BF16)

    blk = pl.BlockSpec((tm, _GT), lambda i, j: (i, j))
    return pl.pallas_call(
        body,
        name=name,
        grid=(S // tm, D // _GT),
        in_specs=_gate_specs(tm, lambda i, j: (i, j)) + [blk, blk, blk],
        out_specs=blk,
        out_shape=jax.ShapeDtypeStruct((S, D), BF16),
        compiler_params=_params(("parallel", "parallel")),
    )(rest, rest, rest, bg, bg, bg, ya, yb, yc)


def _merge_bwd(rest, bg, ya, yb, yc, dm, *, name):
    S = rest.shape[0]
    tm = _tile(S, 512)

    def body(g1, g2, g3, b1, b2, b3, ya_ref, yb_ref, yc_ref, dm_ref, dya, dyb, dyc, dg1, dg2, dg3, db1, db2, db3):
        first = pl.program_id(1) == 0
        dmv = dm_ref[...]
        for g_ref, b_ref, y_ref, dy_ref, dg_ref, db_ref in (
            (g1, b1, ya_ref, dya, dg1, db1), (g2, b2, yb_ref, dyb, dg2, db2), (g3, b3, yc_ref, dyc, dg3, db3)):
            gate = _sigmoid(g_ref[...] + b_ref[...])
            dy_ref[...] = (dmv * gate).astype(BF16)
            dpre = dmv * y_ref[...] * gate * (1.0 - gate)
            dg_ref[...] = dpre.astype(BF16)
            part = jnp.sum(dpre, axis=0, keepdims=True)

            @pl.when(first)
            def _():
                db_ref[...] = part

            @pl.when(jnp.logical_not(first))
            def _():
                db_ref[...] += part

    blk = pl.BlockSpec((tm, _GT), lambda j, i: (i, j))
    vec = pl.BlockSpec((1, _GT), lambda j, i: (0, j))
    big = jax.ShapeDtypeStruct((S, D), BF16)
    small = jax.ShapeDtypeStruct((1, D), F32)
    return pl.pallas_call(
        body,
        name=name,
        grid=(D // _GT, S // tm),
        in_specs=_gate_specs(tm, lambda j, i: (i, j)) + [blk, blk, blk, blk],
        out_specs=[blk] * 6 + [vec] * 3,
        out_shape=[big] * 6 + [small] * 3,
        compiler_params=_params(("parallel", "arbitrary")),
    )(rest, rest, rest, bg, bg, bg, ya, yb, yc, dm)


_X_SCALE = XDH ** -0.5


def _xattn_fwd(xq, kv, *, name):
    S = xq.shape[0]
    M = kv.shape[0]
    tq = _tile(S, 512)

    def body(q_ref, k_ref, v_ref, o_ref):
        s = _dot(q_ref[...], k_ref[...], 1, 1) * _X_SCALE
        e = jnp.exp(s - jnp.max(s, axis=-1, keepdims=True))
        p = e / jnp.sum(e, axis=-1, keepdims=True)
        o_ref[...] = _dot(p.astype(BF16), v_ref[...], 1, 0).astype(BF16)

    return pl.pallas_call(
        body,
        name=name,
        grid=(S // tq, XH),
        in_specs=[
            pl.BlockSpec((tq, XDH), lambda i, h: (i, h)),
            pl.BlockSpec((M, XDH), lambda i, h: (0, h)),
            pl.BlockSpec((M, XDH), lambda i, h: (0, XH + h)),
        ],
        out_specs=pl.BlockSpec((tq, XDH), lambda i, h: (i, h)),
        out_shape=jax.ShapeDtypeStruct((S, D), BF16),
        compiler_params=_params(("parallel", "parallel")),
    )(xq, kv, kv)


def _xattn_bwd(xq, kv, do, *, name):
    S = xq.shape[0]
    M = kv.shape[0]
    tq = _tile(S, 512)

    def body(q_ref, k_ref, v_ref, do_ref, dq_ref, dk_ref, dv_ref):
        qb = q_ref[...]
        kb = k_ref[...]
        dob = do_ref[...]
        s = _dot(qb, kb, 1, 1) * _X_SCALE
        e = jnp.exp(s - jnp.max(s, axis=-1, keepdims=True))
        p = e / jnp.sum(e, axis=-1, keepdims=True)
        dp = _dot(dob, v_ref[...], 1, 1)
        ds = (p * (dp - jnp.sum(p * dp, axis=-1, keepdims=True)) * _X_SCALE).astype(BF16)
        dq_ref[...] = _dot(ds, kb, 1, 0).astype(BF16)
        dk_part = _dot(ds, qb, 0, 0)
        dv_part = _dot(p.astype(BF16), dob, 0, 0)

        @pl.when(pl.program_id(1) == 0)
        def _():
            dk_ref[...] = dk_part
            dv_ref[...] = dv_part

        @pl.when(pl.program_id(1) > 0)
        def _():
            dk_ref[...] += dk_part
            dv_ref[...] += dv_part

    qspec = pl.BlockSpec((tq, XDH), lambda h, i: (i, h))
    kspec = pl.BlockSpec((M, XDH), lambda h, i: (0, h))
    dxq, dxk, dxv = pl.pallas_call(
        body,
        name=name,
        grid=(XH, S // tq),
        in_specs=[qspec, kspec, pl.BlockSpec((M, XDH), lambda h, i: (0, XH + h)), qspec],
        out_specs=[qspec, kspec, kspec],
        out_shape=[jax.ShapeDtypeStruct((S, D), BF16), jax.ShapeDtypeStruct((M, D), F32), jax.ShapeDtypeStruct((M, D), F32)],
        compiler_params=_params(("parallel", "arbitrary")),
    )(xq, kv, kv, do)
    return dxq, jnp.concatenate([dxk, dxv], axis=1)


def _adam_math(w, g, m, v):
    m = ADAM_B1 * m + (1.0 - ADAM_B1) * g
    v = ADAM_B2 * v + (1.0 - ADAM_B2) * (g * g)
    m_hat = m / (1.0 - ADAM_B1 ** ADAM_STEP)
    v_hat = v / (1.0 - ADAM_B2 ** ADAM_STEP)
    delta = -ADAM_LR * (m_hat / (jnp.sqrt(v_hat) + ADAM_EPS) + ADAM_WD * w)
    return delta, m, v


def _adamw_sharded(parts, w, m, v, *, name):
    _, R, C = w.shape
    tm = _tile(R, 256)
    nr = R // tm

    def body(p0_ref, p1_ref, w_ref, m_ref, v_ref, g_ref, d_ref, mo_ref, vo_ref):
        def update(p_ref):
            g = p_ref[0].astype(F32)
            for dev in range(1, N_DEV):
                g = g + p_ref[dev].astype(F32)
            delta, mn, vn = _adam_math(w_ref[...], g, m_ref[...], v_ref[...])
            g_ref[...] = g
            d_ref[...] = delta
            mo_ref[...] = mn
            vo_ref[...] = vn

        @pl.when(pl.program_id(0) == 0)
        def _():
            update(p0_ref)

        @pl.when(pl.program_id(0) == 1)
        def _():
            update(p1_ref)

    p0 = pl.BlockSpec((N_DEV, tm, C), lambda l, i: (0, i * (1 - l) + (nr - 1) * l, 0))
    p1 = pl.BlockSpec((N_DEV, tm, C), lambda l, i: (0, i * l, 0))
    blk = pl.BlockSpec((None, tm, C), lambda l, i: (l, i, 0))
    sds = jax.ShapeDtypeStruct(w.shape, F32)
    return pl.pallas_call(
        body,
        name=name,
        grid=(DEPTH, nr),
        in_specs=[p0, p1, blk, blk, blk],
        out_specs=[blk] * 4,
        out_shape=[sds] * 4,
        compiler_params=_params(("arbitrary", "arbitrary")),
    )(parts[0], parts[1], w, m, v)


def _adamw_small(g, w, m, v, *, name):
    n = len(g)

    def body(*refs):
        g_refs, w_refs, m_refs, v_refs = (refs[k * n:(k + 1) * n] for k in range(4))
        d_out, m_out, v_out = (refs[(4 + k) * n:(5 + k) * n] for k in range(3))
        for t in range(n):
            delta, mn, vn = _adam_math(w_refs[t][...], g_refs[t][...], m_refs[t][...], v_refs[t][...])
            d_out[t][...] = delta
            m_out[t][...] = mn
            v_out[t][...] = vn

    vm = pl.BlockSpec(memory_space=pltpu.VMEM)
    shapes = [jax.ShapeDtypeStruct(a.shape, F32) for a in w]
    outs = pl.pallas_call(
        body,
        name=name,
        in_specs=[vm] * (4 * n),
        out_specs=[vm] * (3 * n),
        out_shape=shapes * 3,
        compiler_params=pltpu.CompilerParams(vmem_limit_bytes=VMEM_LIMIT),
    )(*g, *w, *m, *v)
    return outs[:n], outs[n:2 * n], outs[2 * n:]


def _position():
    return lax.axis_index("x"), lax.axis_index("y"), lax.axis_index("c")


def _dev_index(px, py, pc):
    return 4 * px + 2 * py + pc


_ANY = pl.BlockSpec(memory_space=pl.ANY)


def _peers(x, y, c):
    out = []
    for mask in range(1, N_DEV):
        fx, fy, fc = (mask >> 2) & 1, (mask >> 1) & 1, mask & 1
        out.append((1 - x if fx else x, 1 - y if fy else y, 1 - c if fc else c))
    return out


_HBM = pl.BlockSpec(memory_space=pltpu.HBM)
_SEM = pl.BlockSpec(memory_space=pltpu.SEMAPHORE)


def _own_block_placed(block, like):
    x, y, c = _position()
    return lax.dynamic_update_index_in_dim(lax.empty(like.shape, like.dtype), block, _dev_index(x, y, c), 0)


_COPY_BYTES = 256 << 10
_MAX_PIECES = 8


def _pieces(blocks):
    out = []
    for t, b in enumerate(blocks):
        R, C = b.shape[-2:]
        n = max(1, min(_MAX_PIECES, R * C * jnp.dtype(b.dtype).itemsize // _COPY_BYTES))
        while n > 1 and R % (16 * n):
            n -= 1
        out += [(t, pl.ds(j * (R // n), R // n) if n > 1 else None) for j in range(n)]
    return out


def _cut(block, rows):
    return block if rows is None else block.at[rows]


def _copies(per_piece):
    def mark(fn):
        fn.per_piece = per_piece
        return fn
    return mark


@_copies(N_DEV - 1)
def _plan_exchange(srcs, lands, send_sems, recv_sems, arrivals):
    x, y, c = _position()
    me = _dev_index(x, y, c)
    out = []
    for k, peer in enumerate(_peers(x, y, c)):
        p = _dev_index(*peer)
        for i, (t, rows) in enumerate(_pieces(lands)):
            sems = dict(send_sem=send_sems.at[7 * i + k], recv_sem=recv_sems.at[7 * i + k], device_id=peer, device_id_type=MESH)
            src, dst = (lands[t].at[p], lands[t].at[p]) if arrivals else (srcs[t].at[p], lands[t].at[me])
            out.append(pltpu.make_async_remote_copy(src_ref=_cut(src, rows), dst_ref=_cut(dst, rows), **sems))
    return out


@_copies(N_DEV - 1)
def _plan_broadcast(srcs, lands, send_sems, recv_sems, arrivals):
    x, y, c = _position()
    me = _dev_index(x, y, c)
    out = []
    for k, peer in enumerate(_peers(x, y, c)):
        p = _dev_index(*peer)
        for i, (t, rows) in enumerate(_pieces(lands)):
            sems = dict(send_sem=send_sems.at[7 * i + k], recv_sem=recv_sems.at[7 * i + k], device_id=peer, device_id_type=MESH)
            src, dst = (lands[t].at[p], lands[t].at[p]) if arrivals else (srcs[t], lands[t].at[me])
            out.append(pltpu.make_async_remote_copy(src_ref=_cut(src, rows), dst_ref=_cut(dst, rows), **sems))
    return out


@_copies(4)
def _plan_gather_out(srcs, lands, send_sems, recv_sems, arrivals):
    x, y, c = _position()
    me = _dev_index(x, y, c)
    out = []
    for k, peer in enumerate([(x, y, 1 - c), (1 - x, y, c), (x, 1 - y, c), (1 - x, 1 - y, c)]):
        p = _dev_index(*peer)
        for i, (t, rows) in enumerate(_pieces(lands)):
            sems = dict(send_sem=send_sems.at[4 * i + k], recv_sem=recv_sems.at[4 * i + k], device_id=peer, device_id_type=MESH)
            src, dst = (lands[t].at[p], lands[t].at[p]) if arrivals else (srcs[t], lands[t].at[me])
            out.append(pltpu.make_async_remote_copy(src_ref=_cut(src, rows), dst_ref=_cut(dst, rows), **sems))
    return out


@_copies(3)
def _plan_gather_pass(srcs, lands, send_sems, recv_sems, arrivals):
    x, y, c = _position()
    sibling = (x, y, 1 - c)
    out = []
    for k, chip in enumerate([(1 - x, y), (x, 1 - y), (1 - x, 1 - y)]):
        p = _dev_index(*chip, 1 - c) if arrivals else _dev_index(*chip, c)
        for i, (t, rows) in enumerate(_pieces(lands)):
            sems = dict(send_sem=send_sems.at[3 * i + k], recv_sem=recv_sems.at[3 * i + k], device_id=sibling, device_id_type=MESH)
            block = _cut(lands[t].at[p], rows)
            out.append(pltpu.make_async_remote_copy(src_ref=block, dst_ref=block, **sems))
    return out


def _split_start(plan, srcs, lands, *, after=None, name):
    n_src, n = len(srcs), len(srcs) + len(lands)
    n_sem = plan.per_piece * len(_pieces(lands))
    order = [] if after is None else [after]

    def body(*refs):
        send_sems, recv_sems = refs[n + len(order):n + len(order) + 2]
        token = refs[-1]
        for cp in plan(refs[:n_src], refs[n_src:n], send_sems, recv_sems, arrivals=False):
            cp.start()
        token[...] = jnp.zeros_like(token)

    hbm = lambda a: pltpu.HBM(a.shape, a.dtype)
    outs = pl.pallas_call(
        body,
        name=name,
        in_specs=[_HBM] * n + [_ANY] * len(order),
        out_specs=[_SEM, _SEM] + [_HBM] * n + [pl.BlockSpec(memory_space=pltpu.VMEM)],
        out_shape=[pltpu.SemaphoreType.DMA((n_sem,)), pltpu.SemaphoreType.DMA((n_sem,))] + [hbm(a) for a in (*srcs, *lands)]
        + [jax.ShapeDtypeStruct(_TOKEN, F32)],
        input_output_aliases={i: 2 + i for i in range(n)},
        compiler_params=pltpu.CompilerParams(has_side_effects=pltpu.SideEffectType.DATAFLOW_SIDE_EFFECTING),
    )(*[pltpu.with_memory_space_constraint(a, pltpu.HBM) for a in (*srcs, *lands)], *order)
    return (outs[0], outs[1], outs[2:2 + n_src], outs[2 + n_src:2 + n]), outs[-1]


def _split_wait(plan, state, after, *, name):
    send_sems, recv_sems, srcs, lands = state
    n_src, n = len(srcs), len(srcs) + len(lands)

    def body(*refs):
        send_refs, recv_refs = refs[n:n + 2]
        for cp in plan(refs[:n_src], refs[n_src:n], send_refs, recv_refs, arrivals=False):
            cp.wait_send()
        for cp in plan(refs[:n_src], refs[n_src:n], send_refs, recv_refs, arrivals=True):
            cp.wait_recv()

    hbm = lambda a: pltpu.HBM(a.shape, a.dtype)
    outs = pl.pallas_call(
        body,
        name=name,
        in_specs=[_HBM] * n + [_SEM, _SEM, _ANY],
        out_specs=[_HBM] * n,
        out_shape=[hbm(a) for a in (*srcs, *lands)],
        input_output_aliases={i: i for i in range(n)},
        compiler_params=pltpu.CompilerParams(has_side_effects=pltpu.SideEffectType.DATAFLOW_SIDE_EFFECTING),
    )(*srcs, *lands, send_sems, recv_sems, after)
    return outs[n_src:]


def _sum_blocks(blocks, *, name):
    _, R, C = blocks.shape
    tm = next(R // n for n in (4, 3, 2, 1) if R % (8 * n) == 0)

    def body(b_ref, o_ref):
        g = b_ref[0]
        for dev in range(1, N_DEV):
            g = g + b_ref[dev]
        o_ref[...] = g

    return pl.pallas_call(
        body,
        name=name,
        grid=(R // tm,),
        in_specs=[pl.BlockSpec((N_DEV, tm, C), lambda i: (0, i, 0))],
        out_specs=pl.BlockSpec((tm, C), lambda i: (i, 0)),
        out_shape=jax.ShapeDtypeStruct((R, C), F32),
        compiler_params=_params(("parallel",)),
    )(blocks)


def _block_diag(w):
    out = jnp.zeros((POOL_W, POOL_W), w.dtype)
    for gi in range(4):
        out = out.at[64 * gi:64 * (gi + 1), 64 * gi:64 * (gi + 1)].set(w[gi])
    return out


def _layer_consts(sp, l):
    causal = jnp.tril(jnp.ones((SGU_CHUNK, SGU_CHUNK), F32))
    wm = (sp["sgu_w"][l] * causal[None]).astype(BF16)
    wbd = _block_diag(sp["pool_w"][l]).astype(BF16)
    return dict(
        wbd=wbd, wbd_t=wbd.T, wm=wm, wm_t=wm.transpose(0, 2, 1),
        sgu_bias=jnp.repeat(sp["sgu_b"][l].T, 64, axis=1),
        bpad=jnp.pad(sp["b_forget"][l], (0, F_LANES - FOX_H)).reshape(1, F_LANES),
        bg=sp["b_gate"][l].reshape(1, 3 * D),
    )


def _relu2(acc):
    return acc, jnp.square(jnp.maximum(acc, 0.0))


def _relu2_grad(acc, z):
    return (acc * 2.0 * jnp.maximum(z, 0.0),)


def _layer_fwd(l, x, h, mem, source, sp):
    S = x.shape[0]
    t = _tile(S, 256)
    c = _layer_consts(sp, l)
    n = f"l{l}_"
    W, after = source(l, "begin", x)
    if h is None:
        h, after = _rms_fwd(x, sp["norm_mix_g"][l], after=after, name=n + "norm_mix"), None
    hm = _rms_fwd(mem, sp["norm_mem_g"][l], name=n + "norm_mem")
    more, token = source(l, "normed", hm)
    W.update(more)
    qkv = _mm(h, W["qkv"], out_dtypes=(BF16,), after=after if token is None else token, name=n + "qkv")
    rest = _mm(h, W["rest"], name=n + "rest")
    pa = _pool_fwd(rest, c["wbd"], sp["pool_scale"][l], name=n + "pool")
    cum, cum_t = _fox_prep(rest, c["bpad"], name=n + "fox_prep")
    fk3 = cum_t[:FOX_H].reshape(FOX_H, S // t, t)
    o, lse = _fox_fwd(qkv, cum, fk3, name=n + "fox")
    more, _ = source(l, "attended", o)
    W.update(more)
    sg = _sgu_fwd(rest, sp["sgu_norm_g"][l], c["wm"], c["sgu_bias"], name=n + "sgu")
    more, after = source(l, "mixed", sg)
    W.update(more)
    ya = _mm(pa, W["ba"], out_dtypes=(BF16,), after=after, name=n + "branch_a")
    yb = _mm(o, W["bb"], out_dtypes=(BF16,), name=n + "branch_b")
    yc = _mm(sg, W["bc"], out_dtypes=(BF16,), name=n + "branch_c")
    merged = _merge_fwd(rest, c["bg"], ya, yb, yc, name=n + "merge")
    whole_rows = dict(epilogue=_add_norm, out_dtypes=(F32, BF16), tm=1024, tn=D)
    x1, hx = _mm(merged, W["out"], extras=(x,), row_extras=(sp["norm_xattn_g"][l].reshape(1, D),), name=n + "out", **whole_rows)
    xq = _mm(hx, W["xq"], out_dtypes=(BF16,), name=n + "xq")
    kv = _mm(hm, W["xkv"], out_dtypes=(BF16,), name=n + "xkv")
    o2 = _xattn_fwd(xq, kv, name=n + "xattn")
    x2, hf = _mm(o2, W["xo"], extras=(x1,), row_extras=(sp["norm_ffn_g"][l].reshape(1, D),), name=n + "xo", **whole_rows)
    z, act = _mm(hf, W["ff1"], epilogue=_relu2, out_dtypes=(BF16, BF16), name=n + "ff1")
    _, after = source(l, "expanded", act)
    if l + 1 < DEPTH:
        x3, h_next = _mm(act, W["ff2"], extras=(x2,), row_extras=(sp["norm_mix_g"][l + 1].reshape(1, D),), after=after, name=n + "ff2",
                         **whole_rows)
    else:
        x3, h_next = _mm(act, W["ff2"], extras=(x2,), epilogue=_add, after=after, name=n + "ff2"), None
    saved = dict(x=x, h=h, qkv=qkv, rest=rest, pa=pa, cum=cum, fk3=fk3, o=o, lse=lse, sg=sg, ya=ya, yb=yb, yc=yc,
                 merged=merged, x1=x1, hx=hx, hm=hm, xq=xq, kv=kv, o2=o2, x2=x2, hf=hf, z=z, act=act, c=c)
    return x3, h_next, saved, W


def _layer_bwd(l, dx3, sv, mem, W, sp, grads_done):
    S = dx3.shape[0]
    c = sv["c"]
    n = f"l{l}b_"
    bf = dict(out_dtypes=(BF16,))
    gw, gs = {}, {}
    gw["ff2"] = _mm(sv["act"], dx3, ta=True, name=n + "dw_ff2", **bf)
    dz = _mm(dx3, W["ff2"], tb=True, extras=(sv["z"],), epilogue=_relu2_grad, name=n + "dz", **bf)
    gw["ff1"] = _mm(sv["hf"], dz, ta=True, shard_out=True, name=n + "dw_ff1", **bf)
    whole_rows = dict(epilogue=_norm_grad, out_dtypes=(F32, F32), row_outs=1, tm=1024, tn=D)
    gain = lambda key: (sp[key][l].reshape(1, D),)
    dx2, dg = _mm(dz, W["ff1_t"], extras=(sv["x2"], dx3), row_extras=gain("norm_ffn_g"), name=n + "dhf", **whole_rows)
    gs["norm_ffn_g"] = dg.reshape(D)
    gw["xo"] = _mm(sv["o2"], dx2, ta=True, name=n + "dw_xo", **bf)
    do2 = _mm(dx2, W["xo"], tb=True, name=n + "do2", **bf)
    dxq, dkv = _xattn_bwd(sv["xq"], sv["kv"], do2, name=n + "dxattn")
    gw["xq"] = _mm(sv["hx"], dxq, ta=True, name=n + "dw_xq", **bf)
    gw["xkv"] = _mm(sv["hm"], dkv, ta=True, shard_out=True, name=n + "dw_xkv", **bf)
    dhm = _mm(dkv, W["xkv"], tb=True, name=n + "dhm")
    _, gs["norm_mem_g"] = _rms_bwd(mem, sp["norm_mem_g"][l], dhm, jnp.zeros_like(mem), name=n + "dnorm_mem")
    dx1, dg = _mm(dxq, W["xq"], tb=True, extras=(sv["x1"], dx2), row_extras=gain("norm_xattn_g"), name=n + "dhx", **whole_rows)
    gs["norm_xattn_g"] = dg.reshape(D)
    after, gw = grads_done(l, gw), {}
    gw["out"] = _mm(sv["merged"], dx1, ta=True, name=n + "dw_out", **bf)
    dm = _mm(dx1, W["out"], tb=True, after=after, name=n + "dmerged")
    dya, dyb, dyc, dg1, dg2, dg3, db1, db2, db3 = _merge_bwd(sv["rest"], c["bg"], sv["ya"], sv["yb"], sv["yc"], dm, name=n + "dmerge")
    gs["b_gate"] = jnp.concatenate([db1, db2, db3], axis=1).reshape(3 * D)
    gw["ba"] = _mm(sv["pa"], dya, ta=True, shard_out=True, name=n + "dw_ba", **bf)
    gw["bb"] = _mm(sv["o"], dyb, ta=True, shard_out=True, name=n + "dw_bb", **bf)
    gw["bc"] = _mm(sv["sg"], dyc, ta=True, shard_out=True, name=n + "dw_bc", **bf)
    after, gw = grads_done(l, gw), {}
    dpa = _mm(dya, W["ba"], tb=True, name=n + "dpa")
    do = _mm(dyb, W["bb"], tb=True, after=after, name=n + "do", **bf)
    dsg = _mm(dyc, W["bc"], tb=True, name=n + "dsg")
    da, dwbd, dscale = _pool_bwd(sv["rest"], c["wbd"], c["wbd_t"], sp["pool_scale"][l], dpa, name=n + "dpool")
    gs["pool_w"] = jnp.stack([dwbd[64 * gi:64 * (gi + 1), 64 * gi:64 * (gi + 1)] for gi in range(4)])
    gs["pool_scale"] = dscale.reshape(POOL_W)
    dq, dk, dv, dfq, dfk = _fox_bwd(sv["qkv"], sv["cum"], sv["fk3"], sv["o"], do, sv["lse"], name=n + "dfox")
    dcum = dfq + jnp.pad(dfk.reshape(FOX_H, S).T, ((0, 0), (0, F_LANES - FOX_H)))
    df, dbf = _fox_post(sv["rest"], c["bpad"], dcum, name=n + "dfox_post")
    gs["b_forget"] = dbf[0, :FOX_H]
    dc, dwm, dbias, dgn = _sgu_bwd(sv["rest"], sp["sgu_norm_g"][l], c["wm"], c["wm_t"], c["sgu_bias"], dsg, name=n + "dsgu")
    gs["sgu_w"] = dwm * jnp.tril(jnp.ones((SGU_CHUNK, SGU_CHUNK), F32))[None]
    gs["sgu_b"] = dbias.reshape(SGU_CHUNK, 4, 64).sum(axis=2).T
    gs["sgu_norm_g"] = dgn.reshape(SGU_W)
    dqkv = [dq, dk, dv]
    drest = [jnp.concatenate([da, df, jnp.zeros((S, OFF_C - OFF_F - F_LANES), BF16), dc], axis=1), dg1, dg2, dg3]
    gw["qkv"] = _mm(sv["h"], dqkv, ta=True, name=n + "dw_qkv", **bf)
    gw["rest"] = _mm(sv["h"], drest, ta=True, name=n + "dw_rest", **bf)
    after = grads_done(l, gw)
    dx, dg = _mm(dqkv + drest, W["in_t"], extras=(sv["x"], dx1), row_extras=gain("norm_mix_g"), after=after, name=n + "dh",
                 **whole_rows)
    gs["norm_mix_g"] = dg.reshape(D)
    return dx, gs


def _local_step(x, mem, target, sp, source, grads_done):
    saved, Ws, h = [], [], None
    for l in range(DEPTH):
        x, h, sv, W = _layer_fwd(l, x, h, mem, source, sp)
        saved.append(sv)
        Ws.append(W)
    loss, dx, dgf = _final_loss(x, sp["final_norm_g"], target, name="final_loss")
    gss = [None] * DEPTH
    for l in reversed(range(DEPTH)):
        dx, gss[l] = _layer_bwd(l, dx, saved[l], mem, Ws[l], sp, grads_done)
    small = {k: jnp.stack([gss[l][k] for l in range(DEPTH)]) for k in gss[0]}
    small["final_norm_g"] = dgf
    return loss, dx, small


_SMALL = ["norm_mix_g", "b_forget", "pool_w", "pool_scale", "sgu_norm_g", "sgu_w", "sgu_b", "b_gate", "norm_xattn_g",
          "norm_mem_g", "norm_ffn_g", "final_norm_g"]
_COL = {"w_branch_a": "ba", "w_branch_b": "bb", "w_branch_c": "bc", "w_xkv": "xkv", "w_ff1": "ff1"}
_ROW = {"w_out": "out", "w_xq": "xq", "w_xo": "xo", "w_ff2": "ff2"}
_BIG = ["w_in", "w_branch_a", "w_branch_b", "w_branch_c", "w_out", "w_xq", "w_xkv", "w_xo", "w_ff1", "w_ff2"]
_PACK_LANES = 128


def _as_rows(a):
    return a.reshape(-1, a.shape[-1])


def _pack(tensors):
    rows = []
    for a in tensors:
        flat = a.reshape(-1)
        flat = jnp.pad(flat, (0, (-flat.shape[0]) % (8 * _PACK_LANES)))
        rows.append(flat.reshape(-1, _PACK_LANES))
    n_rows = sum(r.shape[0] for r in rows)
    rows.append(jnp.zeros(((-n_rows) % (8 * N_DEV), _PACK_LANES), F32))
    return jnp.concatenate(rows, axis=0)


def _unpack(packed, like):
    out, r = [], 0
    for a in like:
        size = math.prod(a.shape)
        nr = 8 * (-(-size // (8 * _PACK_LANES)))
        out.append(packed[r:r + nr].reshape(-1)[:size].reshape(a.shape))
        r += nr
    return out


_SHARD_IN = N_IN // N_DEV


def _columns(pieces, start, stop):
    out, at = [], 0
    for p in pieces:
        lo, hi = max(start, at), min(stop, at + p.shape[1])
        if lo < hi:
            out.append(p[:, lo - at:hi - at])
        at += p.shape[1]
    return out


def _split_w_in(blocks):
    K = blocks[0].shape[0]
    pad = jnp.zeros((K, OFF_C - OFF_F - FOX_H), blocks[0].dtype)
    cols = functools.partial(_columns, blocks)
    rest = jnp.concatenate(cols(0, R_OFF_Q) + cols(R_OFF_F, R_OFF_C) + [pad] + cols(R_OFF_C, N_IN), axis=1)
    qkv = jnp.concatenate(cols(R_OFF_Q, R_OFF_F), axis=1)
    return qkv, rest, jnp.concatenate([qkv.T, rest.T], axis=0)


def _join_w_in(qkv, rest):
    in_order = [rest[:, :R_OFF_Q], qkv, rest[:, OFF_F:OFF_F + FOX_H], rest[:, OFF_C:]]
    return jnp.stack([jnp.concatenate(_columns(in_order, _SHARD_IN * d, _SHARD_IN * (d + 1)), axis=1) for d in range(N_DEV)])


_FIRST = ["w_in"]
_LATER = [k for k in _BIG if k not in _FIRST]


def _layer_weights(gathered):
    W = {}
    if "w_in" in gathered:
        W.update(zip(("qkv", "rest", "in_t"), _split_w_in([gathered["w_in"][d] for d in range(N_DEV)])))
    for name, key in _COL.items():
        if name in gathered:
            W[key] = _Gathered(gathered[name])
    if "ff1" in W:
        W["ff1_t"] = W["ff1"].arr.transpose(0, 2, 1).reshape(-1, D)
    if "xkv" in W:
        W["xkv"] = W["xkv"].arr.transpose(1, 0, 2).reshape(D, -1)
    for name, key in _ROW.items():
        if name in gathered:
            W[key] = gathered[name].reshape(-1, gathered[name].shape[-1])
    return W


def _grad_blocks(gw):
    parts = {}
    if "qkv" in gw:
        parts["w_in"] = _join_w_in(gw["qkv"], gw["rest"])
    for name, key in _COL.items():
        if key in gw:
            parts[name] = gw[key]
    for name, key in _ROW.items():
        if key in gw:
            parts[name] = gw[key].reshape(N_DEV, -1, gw[key].shape[-1])
    return parts


def kernel(x, mem, norm_mix_g, w_in, b_forget, pool_w, pool_scale, sgu_norm_g, sgu_w, sgu_b, w_branch_a, w_branch_b, w_branch_c, b_gate, w_out, norm_xattn_g, norm_mem_g, w_xq, w_xkv, w_xo, norm_ffn_g, w_ff1, w_ff2, final_norm_g, loss_target, m_norm_mix_g, m_w_in, m_b_forget, m_pool_w, m_pool_scale, m_sgu_norm_g, m_sgu_w, m_sgu_b, m_w_branch_a, m_w_branch_b, m_w_branch_c, m_b_gate, m_w_out, m_norm_xattn_g, m_norm_mem_g, m_w_xq, m_w_xkv, m_w_xo, m_norm_ffn_g, m_w_ff1, m_w_ff2, m_final_norm_g, v_norm_mix_g, v_w_in, v_b_forget, v_pool_w, v_pool_scale, v_sgu_norm_g, v_sgu_w, v_sgu_b, v_w_branch_a, v_w_branch_b, v_w_branch_c, v_b_gate, v_w_out, v_norm_xattn_g, v_norm_mem_g, v_w_xq, v_w_xkv, v_w_xo, v_norm_ffn_g, v_w_ff1, v_w_ff2, v_final_norm_g):
    names = ["norm_mix_g", "w_in", "b_forget", "pool_w", "pool_scale", "sgu_norm_g", "sgu_w", "sgu_b", "w_branch_a", "w_branch_b",
             "w_branch_c", "b_gate", "w_out", "norm_xattn_g", "norm_mem_g", "w_xq", "w_xkv", "w_xo", "norm_ffn_g", "w_ff1", "w_ff2",
             "final_norm_g"]
    w = dict(zip(names, [norm_mix_g, w_in, b_forget, pool_w, pool_scale, sgu_norm_g, sgu_w, sgu_b, w_branch_a, w_branch_b, w_branch_c,
                         b_gate, w_out, norm_xattn_g, norm_mem_g, w_xq, w_xkv, w_xo, norm_ffn_g, w_ff1, w_ff2, final_norm_g]))
    m = dict(zip(names, [m_norm_mix_g, m_w_in, m_b_forget, m_pool_w, m_pool_scale, m_sgu_norm_g, m_sgu_w, m_sgu_b, m_w_branch_a,
                         m_w_branch_b, m_w_branch_c, m_b_gate, m_w_out, m_norm_xattn_g, m_norm_mem_g, m_w_xq, m_w_xkv, m_w_xo,
                         m_norm_ffn_g, m_w_ff1, m_w_ff2, m_final_norm_g]))
    v = dict(zip(names, [v_norm_mix_g, v_w_in, v_b_forget, v_pool_w, v_pool_scale, v_sgu_norm_g, v_sgu_w, v_sgu_b, v_w_branch_a,
                         v_w_branch_b, v_w_branch_c, v_b_gate, v_w_out, v_norm_xattn_g, v_norm_mem_g, v_w_xq, v_w_xkv, v_w_xo,
                         v_norm_ffn_g, v_w_ff1, v_w_ff2, v_final_norm_g]))

    sp = {k: w[k] for k in _SMALL}
    shards = [{k: w[k][l].astype(BF16) for k in _BIG} for l in range(DEPTH)]
    me = _dev_index(*_position())

    def gather_out(l, keys, name, after=None):
        srcs = [shards[l][k] for k in keys]
        lands = [_own_block_placed(a, jax.ShapeDtypeStruct((N_DEV, *a.shape), a.dtype)) for a in srcs]
        state, token = _split_start(_plan_gather_out, srcs, lands, after=after, name=name + "_out_start")
        return (keys, name, state), token

    def gather_pass(job, value):
        keys, name, state = job
        lands = _split_wait(_plan_gather_out, state, value, name=name + "_out_wait")
        state, token = _split_start(_plan_gather_pass, [], lands, name=name + "_pass_start")
        return (keys, name, state), token, lands[0]

    def gather_end(job, value):
        keys, name, state = job
        return _layer_weights(dict(zip(keys, _split_wait(_plan_gather_pass, state, value, name=name + "_pass_wait"))))

    jobs = {}

    def source(l, point, value):
        if (l, point) == (0, "begin"):
            jobs["l0_first"], token = gather_out(0, _FIRST, "gather_l0_first")
            return {}, token
        if (l, point) == (0, "normed"):
            jobs["l0_first"], token, arrived = gather_pass(jobs["l0_first"], value)
            jobs["l0"], _ = gather_out(0, _LATER, "gather_l0", after=arrived)
            return gather_end(jobs.pop("l0_first"), token), None
        if (l, point) == (0, "attended"):
            jobs["l0"], _, arrived = gather_pass(jobs["l0"], value)
            jobs["l1_first"], token = gather_out(1, _FIRST, "gather_l1_first", after=arrived)
            jobs["l1"], jobs["token"] = gather_out(1, _LATER, "gather_l1", after=token)
            return {}, None
        if (l, point) == (0, "mixed"):
            return gather_end(jobs.pop("l0"), value), jobs.pop("token")
        if (l, point) == (0, "expanded"):
            jobs["l1_first"], token, _ = gather_pass(jobs["l1_first"], value)
            return {}, token
        if (l, point) == (1, "begin"):
            W = gather_end(jobs.pop("l1_first"), value)
            jobs["l1"], token, _ = gather_pass(jobs["l1"], value)
            return W, token
        if (l, point) == (1, "mixed"):
            return gather_end(jobs.pop("l1"), value), None
        return {}, None

    received = [{} for _ in range(DEPTH)]
    travelling = []

    def grads_done(l, gw):
        blocks = _grad_blocks(gw)
        keys = [k for k in _BIG if k in blocks]
        parts = [blocks[k] for k in keys]
        group = f"exchange_grads_l{l}_" + ("in" if "w_in" in blocks else "merge" if "w_out" in blocks else "mlp")
        lands = [_own_block_placed(lax.dynamic_index_in_dim(p, me, 0, keepdims=False), p) for p in parts]
        state, token = _split_start(_plan_exchange, parts, lands, name=group + "_start")
        travelling.append((l, keys, state, group + "_wait"))
        return token

    loss, dx, small = _local_step(x[0], mem[0], loss_target[0], sp, source, grads_done)
    grads, deltas, new_m, new_v = {}, {}, {}, {}
    like = [loss] + [w[k] for k in _SMALL]
    packed = _pack([loss] + [small[k] for k in _SMALL])
    eighths = packed.reshape(N_DEV, -1, _PACK_LANES)
    own = lambda a: _own_block_placed(lax.dynamic_index_in_dim(a, me, 0, keepdims=False) if a.ndim == 3 else a, eighths)
    scatter, done = _split_start(_plan_exchange, [eighths], [own(eighths)], after=dx, name="small_grads_scatter_start")

    def reduce_small(after):
        mine = _sum_blocks(_split_wait(_plan_exchange, scatter, after, name="small_grads_scatter_wait")[0], name="small_grads_sum")
        return _split_start(_plan_broadcast, [mine], [own(mine)], name="small_grads_gather_start")

    def update_small(state, after):
        total = _split_wait(_plan_broadcast, state, after, name="small_grads_gather_wait")[0].reshape(packed.shape)
        loss_sum, *g_small = _unpack(total, like)
        rows = lambda d: [_as_rows(d[k]) for k in _SMALL]
        outs = _adamw_small([_as_rows(g) for g in g_small], rows(w), rows(m), rows(v), name="adamw_small")
        grads.update(zip(_SMALL, g_small))
        for dst, vals in zip((deltas, new_m, new_v), outs):
            dst.update({k: a.reshape(w[k].shape) for k, a in zip(_SMALL, vals)})
        return loss_sum[0, 0], outs[0][0]

    groups = list(dict.fromkeys(tuple(keys) for _, keys, _, _ in travelling))
    for n_done, group_keys in enumerate(groups):
        if n_done == 1:
            gather, _ = reduce_small(done)
        if n_done == len(groups) - 1:
            loss, done = update_small(gather, done)
        for l, keys, state, wait_name in travelling:
            if tuple(keys) == group_keys:
                received[l].update(zip(keys, _split_wait(_plan_exchange, state, done, name=wait_name)))
        for k in group_keys:
            outs = _adamw_sharded([received[l][k] for l in range(DEPTH)], w[k], m[k], v[k], name="adamw_" + k)
            grads[k], deltas[k], new_m[k], new_v[k] = outs
        done = grads[group_keys[-1]]

    return (loss, dx[None], *[grads[k] for k in names], *[deltas[k] for k in names], *[new_m[k] for k in names],
            *[new_v[k] for k in names])
```

```python
import functools
import math

import jax
import jax.numpy as jnp
from jax import lax
from jax.experimental import pallas as pl
from jax.experimental.pallas import tpu as pltpu

F32 = jnp.float32
BF16 = jnp.bfloat16
MESH = pl.DeviceIdType.MESH

N_DEV = 8
D = 1024
DEPTH = 2
EPS = 1e-6
NEG = -1e30
POOL_W = 256
FOX_H = 8
FOX_DH = 64
FOX_W = 512
SGU_W = 256
SGU_CHUNK = 128
XH = 4
XDH = 256
N_IN = 5384
R_OFF_Q, R_OFF_F, R_OFF_C = 256, 1792, 1800
QKV_W = 3 * FOX_W
OFF_A, OFF_F, OFF_C, OFF_G, REST_W = 0, 256, 512, 1024, 4096
F_LANES = 128

ADAM_LR = 0.001
ADAM_B1 = 0.9
ADAM_B2 = 0.999
ADAM_EPS = 1e-08
ADAM_WD = 0.01
ADAM_STEP = 10

VMEM_LIMIT = 56 * 1024 * 1024


def _tile(n, pref):
    t = min(n, pref)
    while n % t:
        t -= 128
    assert t > 0, (n, pref)
    return t


def _params(sem=None):
    return pltpu.CompilerParams(dimension_semantics=sem, vmem_limit_bytes=VMEM_LIMIT)


def _dot(a, b, ca, cb):
    return lax.dot_general(a, b, (((ca,), (cb,)), ((), ())), preferred_element_type=F32)


def _sigmoid(z):
    return 1.0 / (1.0 + jnp.exp(-z))


_GELU_K = math.sqrt(2.0 / math.pi)
_GELU_C = 0.044715


def _gelu(x):
    return 0.5 * x * (1.0 + jnp.tanh(_GELU_K * (x + _GELU_C * x * x * x)))


def _gelu_grad(x):
    t = jnp.tanh(_GELU_K * (x + _GELU_C * x * x * x))
    return 0.5 * (1.0 + t) + 0.5 * x * (1.0 - t * t) * _GELU_K * (1.0 + 3.0 * _GELU_C * x * x)


def _rows(shape):
    return lax.broadcasted_iota(jnp.int32, shape, 0)


def _lanes(shape):
    return lax.broadcasted_iota(jnp.int32, shape, 1)


class _Gathered:
    def __init__(self, arr):
        self.arr = arr
        self.shape = (arr.shape[1], N_DEV * arr.shape[2])


_TOKEN = (8, 128)


def _mm(a, b, *, ta=False, tb=False, extras=(), row_extras=(), epilogue=None, out_dtypes=(F32,), row_outs=0, shard_out=False, after=None,
        tm=None, tn=512, tk=None, name):
    a_parts = list(a) if isinstance(a, (list, tuple)) else [a]
    b_parts = list(b) if isinstance(b, (list, tuple)) else [b]
    gathered = isinstance(b, _Gathered)
    assert (len(a_parts) == 1 or not ta) and (len(b_parts) == 1 or not tb) and min(len(a_parts), len(b_parts)) == 1
    a0, b0 = a_parts[0], b_parts[0]
    M, K = (a0.shape[1], a0.shape[0]) if ta else (a0.shape[0], a0.shape[1] * len(a_parts))
    N, Kb = b0.shape if tb else (b0.shape[1] * len(b_parts), b0.shape[0])
    assert Kb == K, (a0.shape, b0.shape, ta, tb)
    if gathered:
        if tb:
            tk = b.arr.shape[2]
        else:
            tn = b.arr.shape[2]
    if len(a_parts) > 1:
        tk = a0.shape[1]
    if shard_out:
        tn = N // N_DEV
    tm = _tile(M, tm or (1024 if ta else 2048))
    tn = _tile(b0.shape[1] if len(b_parts) > 1 else N, tn)
    per_piece = b0.shape[1] // tn
    size = lambda dt: jnp.dtype(dt).itemsize
    row_bytes = len(a_parts) * tm * size(a0.dtype) + len(b_parts) * tn * size(b.arr.dtype if gathered else b0.dtype)
    tile_bytes = tm * tn * (sum(size(e.dtype) for e in extras) + sum(map(size, out_dtypes)))

    def vmem_bytes(k_tile):
        return 2 * (k_tile * row_bytes + tile_bytes) + tm * tn * 4 * (K > k_tile)

    if tk is None:
        tk = next(c for c in (_tile(K, 2048), _tile(K, 1024), _tile(K, 512), _tile(K, 256)) if vmem_bytes(c) <= VMEM_LIMIT - (4 << 20))
    tk = _tile(K, tk)
    nk = K // tk
    ca, cb = (0 if ta else 1), (1 if tb else 0)
    n_a, n_b, n_ex, n_out = len(a_parts), len(b_parts), len(extras) + len(row_extras), len(out_dtypes)
    tokens = [] if after is None else [after]
    n_in = n_a + n_b + n_ex + len(tokens)
    if epilogue is None:
        epilogue = lambda acc: (acc,)

    def body(*refs):
        a_refs, b_refs = refs[:n_a], refs[n_a:n_a + n_b]
        ex_refs = refs[n_a + n_b:n_a + n_b + n_ex]
        o_refs = refs[n_in:n_in + n_out]
        j, k = pl.program_id(1), pl.program_id(2)

        def finish(acc):
            vals = epilogue(acc, *[e[...] for e in ex_refs])
            for o_ref, val in zip(o_refs[:n_out - row_outs], vals):
                o_ref[...] = val.astype(o_ref.dtype)
            for o_ref, val in zip(o_refs[n_out - row_outs:], vals[n_out - row_outs:]):
                first = pl.program_id(0) == 0
                o_ref[...] = jnp.where(first, val, o_ref[...] + val)

        def step(a_ref, b_ref):
            part = _dot(a_ref[...].astype(BF16), b_ref[...].astype(BF16), ca, cb)
            if nk == 1:
                finish(part)
            else:
                acc_ref = refs[-1]

                @pl.when(k == 0)
                def _():
                    acc_ref[...] = part

                @pl.when(k > 0)
                def _():
                    acc_ref[...] += part

                @pl.when(k == nk - 1)
                def _():
                    finish(acc_ref[...])

        if n_a > 1:
            for p in range(n_a):
                pl.when(k == p)(functools.partial(step, a_refs[p], b_refs[0]))
        elif n_b > 1:
            for p in range(n_b):
                pl.when(j // per_piece == p)(functools.partial(step, a_refs[0], b_refs[p]))
        else:
            step(a_refs[0], b_refs[0])

    if n_a > 1:
        a_specs = [pl.BlockSpec((tm, tk), lambda i, j, k: (i, 0))] * n_a
    else:
        a_specs = [pl.BlockSpec((tk, tm), lambda i, j, k: (k, i)) if ta else pl.BlockSpec((tm, tk), lambda i, j, k: (i, k))]
    if gathered:
        b_arrs = [b.arr]
        b_specs = [pl.BlockSpec((None, tn, tk), lambda i, j, k: (k, j, 0)) if tb else pl.BlockSpec((None, tk, tn), lambda i, j, k: (j, k, 0))]
    elif n_b > 1:
        b_arrs = b_parts
        b_specs = [pl.BlockSpec((tk, tn), functools.partial(lambda p, i, j, k: (k, jnp.clip(j - p * per_piece, 0, per_piece - 1)), p))
                   for p in range(n_b)]
    else:
        b_arrs = b_parts
        b_specs = [pl.BlockSpec((tn, tk), lambda i, j, k: (j, k)) if tb else pl.BlockSpec((tk, tn), lambda i, j, k: (k, j))]
    tile = pl.BlockSpec((tm, tn), lambda i, j, k: (i, j))
    if shard_out:
        out_specs = [pl.BlockSpec((None, tm, tn), lambda i, j, k: (j, i, 0))] * n_out
        out_shape = [jax.ShapeDtypeStruct((N_DEV, M, tn), dt) for dt in out_dtypes]
    else:
        assert row_outs == 0 or tn == N
        out_specs = [tile] * (n_out - row_outs) + [pl.BlockSpec((1, tn), lambda i, j, k: (0, j))] * row_outs
        out_shape = [jax.ShapeDtypeStruct((1, N) if t >= n_out - row_outs else (M, N), dt) for t, dt in enumerate(out_dtypes)]
    assert vmem_bytes(tk) <= VMEM_LIMIT - (4 << 20), (name, vmem_bytes(tk))
    outs = pl.pallas_call(
        body,
        name=name,
        grid=(M // tm, N // tn, nk),
        in_specs=a_specs + b_specs + [tile] * len(extras) + [pl.BlockSpec((1, tn), lambda i, j, k: (0, j))] * len(row_extras)
        + [pl.BlockSpec(_TOKEN, lambda i, j, k: (0, 0))] * len(tokens),
        out_specs=out_specs,
        out_shape=out_shape,
        scratch_shapes=[pltpu.VMEM((tm, tn), F32)] if nk > 1 else [],
        compiler_params=_params(("arbitrary",) * 3 if row_outs else ("parallel", "parallel", "arbitrary")),
    )(*a_parts, *b_arrs, *extras, *row_extras, *tokens)
    return outs[0] if n_out == 1 else outs


def _add(acc, res):
    return (acc + res,)


def _norm_grad(dh, x, dres, g):
    r = lax.rsqrt(jnp.mean(x * x, axis=-1, keepdims=True) + EPS)
    xn = x * r
    dxn = dh * g
    dx = r * (dxn - xn * jnp.mean(dxn * xn, axis=-1, keepdims=True)) + dres
    return dx, dx, jnp.sum(dh * xn, axis=0, keepdims=True)


def _add_norm_grad(acc, more, x, dres, g):
    return _norm_grad(acc + more, x, dres, g)


def _add_norm(acc, res, g):
    x = acc + res
    return x, x * lax.rsqrt(jnp.mean(x * x, axis=-1, keepdims=True) + EPS) * g


def _rms_fwd(x, g, *, after=None, name):
    R, C = x.shape
    tm = _tile(R, 256)
    tokens = [] if after is None else [after]

    def body(x_ref, g_ref, *rest):
        xv = x_ref[...]
        r = lax.rsqrt(jnp.mean(xv * xv, axis=-1, keepdims=True) + EPS)
        rest[-1][...] = (xv * r * g_ref[...]).astype(BF16)

    return pl.pallas_call(
        body,
        name=name,
        grid=(R // tm,),
        in_specs=[pl.BlockSpec((tm, C), lambda i: (i, 0)), pl.BlockSpec((1, C), lambda i: (0, 0))]
        + [pl.BlockSpec(_TOKEN, lambda i: (0, 0))] * len(tokens),
        out_specs=pl.BlockSpec((tm, C), lambda i: (i, 0)),
        out_shape=jax.ShapeDtypeStruct((R, C), BF16),
        compiler_params=_params(("parallel",)),
    )(x, g.reshape(1, C), *tokens)


def _rms_bwd(x, g, dh, dres, *, name):
    R, C = x.shape
    tm = _tile(R, 256)

    def body(x_ref, g_ref, dh_ref, dres_ref, dx_ref, dg_ref):
        xv = x_ref[...]
        r = lax.rsqrt(jnp.mean(xv * xv, axis=-1, keepdims=True) + EPS)
        xn = xv * r
        dh_v = dh_ref[...].astype(F32)
        dxn = dh_v * g_ref[...]
        dx_ref[...] = r * (dxn - xn * jnp.mean(dxn * xn, axis=-1, keepdims=True)) + dres_ref[...]
        part = jnp.sum(dh_v * xn, axis=0, keepdims=True)

        @pl.when(pl.program_id(0) == 0)
        def _():
            dg_ref[...] = part

        @pl.when(pl.program_id(0) > 0)
        def _():
            dg_ref[...] += part

    row = pl.BlockSpec((tm, C), lambda i: (i, 0))
    vec = pl.BlockSpec((1, C), lambda i: (0, 0))
    dx, dg = pl.pallas_call(
        body,
        name=name,
        grid=(R // tm,),
        in_specs=[row, vec, row, row],
        out_specs=[row, vec],
        out_shape=[jax.ShapeDtypeStruct((R, C), F32), jax.ShapeDtypeStruct((1, C), F32)],
        compiler_params=_params(("arbitrary",)),
    )(x, g.reshape(1, C), dh, dres)
    return dx, dg.reshape(C)


def _final_loss(x, g, target, *, name):
    R, C = x.shape
    tm = _tile(R, 256)

    def body(x_ref, g_ref, t_ref, loss_ref, dx_ref, dg_ref):
        xv = x_ref[...]
        r = lax.rsqrt(jnp.mean(xv * xv, axis=-1, keepdims=True) + EPS)
        xn = xv * r
        gv = g_ref[...]
        err = xn * gv - t_ref[...]
        lpart = (0.5 / C) * jnp.sum(jnp.sum(err * err, axis=1, keepdims=True), axis=0, keepdims=True)
        dy = err * (1.0 / C)
        dxn = dy * gv
        dx_ref[...] = r * (dxn - xn * jnp.mean(dxn * xn, axis=-1, keepdims=True))
        gpart = jnp.sum(dy * xn, axis=0, keepdims=True)

        @pl.when(pl.program_id(0) == 0)
        def _():
            loss_ref[...] = lpart
            dg_ref[...] = gpart

        @pl.when(pl.program_id(0) > 0)
        def _():
            loss_ref[...] += lpart
            dg_ref[...] += gpart

    row = pl.BlockSpec((tm, C), lambda i: (i, 0))
    vec = pl.BlockSpec((1, C), lambda i: (0, 0))
    loss, dx, dg = pl.pallas_call(
        body,
        name=name,
        grid=(R // tm,),
        in_specs=[row, vec, row],
        out_specs=[pl.BlockSpec((1, 1), lambda i: (0, 0)), row, vec],
        out_shape=[jax.ShapeDtypeStruct((1, 1), F32), jax.ShapeDtypeStruct((R, C), F32), jax.ShapeDtypeStruct((1, C), F32)],
        compiler_params=_params(("arbitrary",)),
    )(x, g.reshape(1, C), target)
    return loss, dx, dg.reshape(C)


def _pool_select(lane, vals):
    out = vals[3]
    for gi in (2, 1, 0):
        out = jnp.where(lane < 64 * (gi + 1), vals[gi], out)
    return out


def _pool_diff(a):
    row, lane = _rows(a.shape), _lanes(a.shape)

    def down(v, k):
        return jnp.where(row >= k, pltpu.roll(v, k, 0), 0.0)

    s2 = a + down(a, 1)
    s4 = s2 + down(s2, 2)
    s8 = s4 + down(s4, 4)
    s16 = s8 + down(s8, 8)
    wsum = _pool_select(lane, (s2, s4, s8, s16))
    win = _pool_select(lane, (2, 4, 8, 16))
    cnt = jnp.minimum(row + 1, win).astype(F32)
    return wsum / cnt - a, cnt


def _pool_diff_t(dd, cnt):
    S = dd.shape[0]
    row, lane = _rows(dd.shape), _lanes(dd.shape)

    def up(v, k):
        return jnp.where(row < S - k, pltpu.roll(v, S - k, 0), 0.0)

    e = dd / cnt
    s2 = e + up(e, 1)
    s4 = s2 + up(s2, 2)
    s8 = s4 + up(s4, 4)
    s16 = s8 + up(s8, 8)
    return _pool_select(lane, (s2, s4, s8, s16)) - dd


def _pool_fwd(rest, wbd, scale, *, name):
    S = rest.shape[0]

    def body(a_ref, w_ref, s_ref, o_ref):
        d, _ = _pool_diff(a_ref[...])
        yp = _dot(d.astype(BF16), w_ref[...], 1, 0)
        o_ref[...] = (yp * s_ref[...]).astype(BF16)

    return pl.pallas_call(
        body,
        name=name,
        grid=(1,),
        in_specs=[
            pl.BlockSpec((S, POOL_W), lambda i: (0, OFF_A // POOL_W)),
            pl.BlockSpec((POOL_W, POOL_W), lambda i: (0, 0)),
            pl.BlockSpec((1, POOL_W), lambda i: (0, 0)),
        ],
        out_specs=pl.BlockSpec((S, POOL_W), lambda i: (0, 0)),
        out_shape=jax.ShapeDtypeStruct((S, POOL_W), BF16),
        compiler_params=_params(("arbitrary",)),
    )(rest, wbd, scale.reshape(1, POOL_W))


def _pool_bwd(rest, wbd, wbd_t, scale, dpa, *, name):
    S = rest.shape[0]

    def body(a_ref, w_ref, wt_ref, s_ref, dpa_ref, da_ref, dw_ref, ds_ref):
        d, cnt = _pool_diff(a_ref[...])
        db = d.astype(BF16)
        yp = _dot(db, w_ref[...], 1, 0)
        dpa_v = dpa_ref[...]
        ds_ref[...] = jnp.sum(dpa_v * yp, axis=0, keepdims=True)
        dyp = (dpa_v * s_ref[...]).astype(BF16)
        dw_ref[...] = _dot(db, dyp, 0, 0)
        dd = _dot(dyp, wt_ref[...], 1, 0)
        da_ref[...] = _pool_diff_t(dd, cnt).astype(BF16)

    full = pl.BlockSpec((S, POOL_W), lambda i: (0, 0))
    sq = pl.BlockSpec((POOL_W, POOL_W), lambda i: (0, 0))
    vec = pl.BlockSpec((1, POOL_W), lambda i: (0, 0))
    return pl.pallas_call(
        body,
        name=name,
        grid=(1,),
        in_specs=[pl.BlockSpec((S, POOL_W), lambda i: (0, OFF_A // POOL_W)), sq, sq, vec, full],
        out_specs=[full, sq, vec],
        out_shape=[
            jax.ShapeDtypeStruct((S, POOL_W), BF16),
            jax.ShapeDtypeStruct((POOL_W, POOL_W), F32),
            jax.ShapeDtypeStruct((1, POOL_W), F32),
        ],
        compiler_params=_params(("arbitrary",)),
    )(rest, wbd, wbd_t, scale.reshape(1, POOL_W), dpa)


def _log_sigmoid(z):
    return jnp.minimum(z, 0.0) - jnp.log(1.0 + jnp.exp(-jnp.abs(z)))


_F_SPEC_COL = OFF_F // F_LANES


def _fox_prep(rest, bpad, *, name):
    S = rest.shape[0]

    def body(f_ref, b_ref, o_ref, ot_ref):
        acc = _log_sigmoid(f_ref[...] + b_ref[...])
        row = _rows(acc.shape)
        k = 1
        while k < S:
            acc = acc + jnp.where(row >= k, pltpu.roll(acc, k, 0), 0.0)
            k *= 2
        o_ref[...] = acc
        ot_ref[...] = acc.T

    return pl.pallas_call(
        body,
        name=name,
        grid=(1,),
        in_specs=[pl.BlockSpec((S, F_LANES), lambda i: (0, _F_SPEC_COL)), pl.BlockSpec((1, F_LANES), lambda i: (0, 0))],
        out_specs=[pl.BlockSpec((S, F_LANES), lambda i: (0, 0)), pl.BlockSpec((F_LANES, S), lambda i: (0, 0))],
        out_shape=[jax.ShapeDtypeStruct((S, F_LANES), F32), jax.ShapeDtypeStruct((F_LANES, S), F32)],
        compiler_params=_params(("arbitrary",)),
    )(rest, bpad)


def _fox_post(rest, bpad, dcum, *, name):
    S = rest.shape[0]

    def body(f_ref, b_ref, d_ref, df_ref, db_ref):
        acc = d_ref[...]
        row = _rows(acc.shape)
        k = 1
        while k < S:
            acc = acc + jnp.where(row < S - k, pltpu.roll(acc, S - k, 0), 0.0)
            k *= 2
        df = acc * (1.0 - _sigmoid(f_ref[...] + b_ref[...]))
        df_ref[...] = df.astype(BF16)
        db_ref[...] = jnp.sum(df, axis=0, keepdims=True)

    full = pl.BlockSpec((S, F_LANES), lambda i: (0, 0))
    vec = pl.BlockSpec((1, F_LANES), lambda i: (0, 0))
    return pl.pallas_call(
        body,
        name=name,
        grid=(1,),
        in_specs=[pl.BlockSpec((S, F_LANES), lambda i: (0, _F_SPEC_COL)), vec, full],
        out_specs=[full, vec],
        out_shape=[jax.ShapeDtypeStruct((S, F_LANES), BF16), jax.ShapeDtypeStruct((1, F_LANES), F32)],
        compiler_params=_params(("arbitrary",)),
    )(rest, bpad, dcum)


_FOX_SCALE = FOX_DH ** -0.5
_PAIRS = FOX_H // 2


def _scaled(v):
    return (v.astype(F32) * _FOX_SCALE).astype(BF16)


def _diag_mask(s):
    return jnp.where(_rows(s.shape) >= _lanes(s.shape), s, NEG)


def _fox_fwd(qkv, cum, fk3, *, name):
    S = qkv.shape[0]
    nk, t = fk3.shape[1:]

    def body(q_ref, k_ref, v_ref, cum_ref, fk_ref, o_ref, lse_ref):
        i = pl.program_id(0)
        lane = _lanes((t, 128))
        lo = lane < FOX_DH
        cumv = cum_ref[...]
        qm, fq = [], []
        for h in range(FOX_H):
            qs = _scaled(q_ref[:, 128 * (h // 2):128 * (h // 2 + 1)])
            zero = jnp.zeros_like(qs)
            qm.append(jnp.where(lo, qs, zero) if h % 2 == 0 else jnp.where(lo, zero, qs))
            fq.append(jnp.broadcast_to(cumv[:, h:h + 1], (t, 128)))

        def tile(j, state, masked):
            m, acc, lsum = (list(part) for part in state)
            k0 = pl.multiple_of(j * t, t)
            for hp in range(_PAIRS):
                cols = slice(128 * hp, 128 * (hp + 1))
                kb = k_ref[pl.ds(k0, t), cols]
                vb = v_ref[pl.ds(k0, t), cols]
                one = jnp.ones_like(vb)
                alphas, pvs = [], []
                for h in (2 * hp, 2 * hp + 1):
                    s = _dot(qm[h], kb, 1, 1) + jnp.concatenate([fq[h]] * (t // 128), axis=1) - fk_ref[h, pl.ds(j, 1), :]
                    if masked:
                        s = _diag_mask(s)
                    m_new = jnp.maximum(m[h], jnp.max(s, axis=-1, keepdims=True))
                    p = jnp.exp(s - m_new)
                    alphas.append(jnp.exp(m[h] - m_new))
                    m[h] = m_new
                    pvs.append(_dot(p.astype(BF16), jnp.where(lo, vb, one) if h % 2 == 0 else jnp.where(lo, one, vb), 1, 0))
                acc[hp] = jnp.where(lo, alphas[0], alphas[1]) * acc[hp] + jnp.where(lo, pvs[0], pvs[1])
                lsum[hp] = jnp.where(lo, alphas[1], alphas[0]) * lsum[hp] + jnp.where(lo, pvs[1], pvs[0])
            return tuple(m), tuple(acc), tuple(lsum)

        zeros = (jnp.zeros((t, 128), F32),) * _PAIRS
        state = lax.fori_loop(0, i, functools.partial(tile, masked=False), ((jnp.full((t, 1), NEG, F32),) * FOX_H, zeros, zeros))
        m, acc, lsum = tile(i, state, True)
        for hp in range(_PAIRS):
            o_ref[:, 128 * hp:128 * (hp + 1)] = acc[hp] / pltpu.roll(lsum[hp], FOX_DH, 1)
            lse = [m[2 * hp] + jnp.log(lsum[hp][:, FOX_DH:FOX_DH + 1]), m[2 * hp + 1] + jnp.log(lsum[hp][:, 0:1])]
            lse_ref[hp] = jnp.where(lane == 0, lse[0], jnp.where(lane == 1, lse[1], 0.0))

    whole = lambda col: pl.BlockSpec((S, FOX_W), lambda i: (0, col))
    return pl.pallas_call(
        body,
        name=name,
        grid=(S // t,),
        in_specs=[
            pl.BlockSpec((t, FOX_W), lambda i: (i, 0)), whole(1), whole(2),
            pl.BlockSpec((t, F_LANES), lambda i: (i, 0)),
            pl.BlockSpec((FOX_H, nk, t), lambda i: (0, 0, 0)),
        ],
        out_specs=[pl.BlockSpec((t, FOX_W), lambda i: (i, 0)), pl.BlockSpec((_PAIRS, t, 128), lambda i: (0, i, 0))],
        out_shape=[jax.ShapeDtypeStruct((S, FOX_W), F32), jax.ShapeDtypeStruct((_PAIRS, S, 128), F32)],
        compiler_params=_params(("arbitrary",)),
    )(qkv, qkv, qkv, cum, fk3)


def _fox_bwd(qkv, cum, fk3, o, do, lse, *, name):
    S = qkv.shape[0]
    nk, t = fk3.shape[1:]
    q_at, k_at, v_at = 0, FOX_W, 2 * FOX_W

    def body(qkv_ref, cum_ref, fk_ref, o_ref, do_ref, lse_ref, dq_ref, dk_ref, dv_ref, dfq_ref, dfk_ref,
             qs_sc, ks_sc, bias_sc, delta_sc, dq_sc):
        lane = _lanes((t, 128))
        lo = lane < FOX_DH
        mine = lambda h: lo if h % 2 == 0 else jnp.logical_not(lo)

        def by_head(tile, values):
            for h, val in enumerate(values):
                tile = jnp.where(lane == h, val, tile)
            return tile

        def prep(i, carry):
            r = pl.ds(pl.multiple_of(i * t, t), t)
            qs_sc[r, :] = _scaled(qkv_ref[r, q_at:q_at + FOX_W])
            ks_sc[r, :] = _scaled(qkv_ref[r, k_at:k_at + FOX_W])
            cum_t = cum_ref[r, :]
            for hp in range(_PAIRS):
                cols = slice(128 * hp, 128 * (hp + 1))
                prod = do_ref[r, cols].astype(F32) * o_ref[r, cols]
                for h in (2 * hp, 2 * hp + 1):
                    delta = jnp.sum(jnp.where(mine(h), prod, 0.0), axis=-1, keepdims=True)
                    delta_sc[h, r, :] = jnp.broadcast_to(delta, (t, 128))
                    bias_sc[h, r, :] = jnp.broadcast_to(cum_t[:, h:h + 1] - lse_ref[hp, r, h % 2:h % 2 + 1], (t, 128))
            dfq_ref[r, :] = jnp.zeros((t, 128), F32)
            dq_sc[r, :] = jnp.zeros((t, FOX_W), F32)
            return carry

        lax.fori_loop(0, nk, prep, 0)

        def kv_tile(j, carry):
            kr = pl.ds(pl.multiple_of(j * t, t), t)

            def q_tile(i, acc, masked):
                dk, dv, dfk = list(acc[:_PAIRS]), list(acc[_PAIRS:2 * _PAIRS]), list(acc[2 * _PAIRS:])
                qr = pl.ds(pl.multiple_of(i * t, t), t)
                dq_old, dfq_old = dq_sc[qr, :], dfq_ref[qr, :]
                wide = lambda a: jnp.concatenate([a] * (t // 128), axis=1)
                row_sums, dq_new = [], []
                for hp in range(_PAIRS):
                    cols = slice(128 * hp, 128 * (hp + 1))
                    kb = qkv_ref[kr, k_at + 128 * hp:k_at + 128 * (hp + 1)]
                    vb = qkv_ref[kr, v_at + 128 * hp:v_at + 128 * (hp + 1)]
                    ksb, qsb, dob = ks_sc[kr, cols], qs_sc[qr, cols], do_ref[qr, cols]
                    zero = jnp.zeros_like(qsb)
                    dq_t = jnp.zeros((t, 128), F32)
                    for h in (2 * hp, 2 * hp + 1):
                        qe, doe, ke = (jnp.where(mine(h), a, zero) for a in (qsb, dob, ksb))
                        s = _dot(qe, kb, 1, 1) + wide(bias_sc[h, qr, :]) - fk_ref[h, pl.ds(j, 1), :]
                        if masked:
                            s = _diag_mask(s)
                        p = jnp.exp(s)
                        dv[hp] = dv[hp] + _dot(p.astype(BF16), doe, 0, 0)
                        dp = _dot(doe, vb, 1, 1)
                        ds = p * (dp - wide(delta_sc[h, qr, :]))
                        dsb = ds.astype(BF16)
                        dk[hp] = dk[hp] + _dot(dsb, qe, 0, 0)
                        dq_t = dq_t + _dot(dsb, ke, 1, 0)
                        row_sums.append(jnp.sum(ds, axis=-1, keepdims=True))
                        dfk[h] = dfk[h] - jnp.sum(ds, axis=0, keepdims=True)
                    dq_new.append(dq_old[:, cols] + dq_t)
                for hp in range(_PAIRS):
                    dq_sc[qr, 128 * hp:128 * (hp + 1)] = dq_new[hp]
                dfq_ref[qr, :] = dfq_old + by_head(jnp.zeros((t, 128), F32), row_sums)
                return (*dk, *dv, *dfk)

            init = tuple([jnp.zeros((t, 128), F32)] * (2 * _PAIRS) + [jnp.zeros((1, t), F32)] * FOX_H)
            acc = q_tile(j, init, True)
            acc = lax.fori_loop(j + 1, nk, functools.partial(q_tile, masked=False), acc)
            for hp in range(_PAIRS):
                cols = slice(128 * hp, 128 * (hp + 1))
                dk_ref[kr, cols] = acc[hp].astype(BF16)
                dv_ref[kr, cols] = acc[_PAIRS + hp].astype(BF16)
            for h in range(FOX_H):
                dfk_ref[h, pl.ds(j, 1), :] = acc[2 * _PAIRS + h]
            return carry

        lax.fori_loop(0, nk, kv_tile, 0)
        dq_ref[...] = dq_sc[...].astype(BF16)

    vm = pl.BlockSpec(memory_space=pltpu.VMEM)
    big = jax.ShapeDtypeStruct((S, FOX_W), BF16)
    return pl.pallas_call(
        body,
        name=name,
        in_specs=[vm] * 6,
        out_specs=[vm] * 5,
        out_shape=[big, big, big, jax.ShapeDtypeStruct((S, 128), F32), jax.ShapeDtypeStruct((FOX_H, nk, t), F32)],
        scratch_shapes=[pltpu.VMEM((S, FOX_W), BF16), pltpu.VMEM((S, FOX_W), BF16), pltpu.VMEM((FOX_H, S, 128), F32),
                        pltpu.VMEM((FOX_H, S, 128), F32), pltpu.VMEM((S, FOX_W), F32)],
        compiler_params=pltpu.CompilerParams(vmem_limit_bytes=VMEM_LIMIT),
    )(qkv, cum, fk3, o, do, lse)


def _group_mask(lane, gi):
    return (lane >= 64 * gi) & (lane < 64 * (gi + 1))


_U_COL = OFF_C // SGU_W


def _sgu_fwd(rest, gn, wm, bias, *, name):
    S = rest.shape[0]
    ts = _tile(S, 512)
    nc = ts // SGU_CHUNK

    def body(u_ref, v_ref, g_ref, w_ref, b_ref, o_ref):
        zv = _gelu(v_ref[...])
        vn = zv * lax.rsqrt(jnp.mean(zv * zv, axis=-1, keepdims=True) + EPS) * g_ref[...]
        lane = _lanes((SGU_CHUNK, SGU_W))
        for c in range(nc):
            rows = slice(c * SGU_CHUNK, (c + 1) * SGU_CHUNK)
            vcb = vn[rows].astype(BF16)
            mixed = b_ref[...]
            for gi in range(4):
                mixed = mixed + jnp.where(_group_mask(lane, gi), _dot(w_ref[gi], vcb, 1, 0), 0.0)
            o_ref[rows, :] = (_gelu(u_ref[rows, :]) * mixed).astype(BF16)

    return pl.pallas_call(
        body,
        name=name,
        grid=(S // ts,),
        in_specs=[
            pl.BlockSpec((ts, SGU_W), lambda i: (i, _U_COL)),
            pl.BlockSpec((ts, SGU_W), lambda i: (i, _U_COL + 1)),
            pl.BlockSpec((1, SGU_W), lambda i: (0, 0)),
            pl.BlockSpec((4, SGU_CHUNK, SGU_CHUNK), lambda i: (0, 0, 0)),
            pl.BlockSpec((SGU_CHUNK, SGU_W), lambda i: (0, 0)),
        ],
        out_specs=pl.BlockSpec((ts, SGU_W), lambda i: (i, 0)),
        out_shape=jax.ShapeDtypeStruct((S, SGU_W), BF16),
        compiler_params=_params(("parallel",)),
    )(rest, rest, gn.reshape(1, SGU_W), wm, bias)


def _sgu_bwd(rest, gn, wm, wm_t, bias, dsg, *, name):
    S = rest.shape[0]
    ts = _tile(S, 512)
    nc = ts // SGU_CHUNK

    def body(u_ref, v_ref, g_ref, w_ref, wt_ref, b_ref, dsg_ref, dc_ref, dw_ref, db_ref, dg_ref):
        first = pl.program_id(0) == 0

        @pl.when(first)
        def _():
            dw_ref[...] = jnp.zeros_like(dw_ref)
            db_ref[...] = jnp.zeros_like(db_ref)
            dg_ref[...] = jnp.zeros_like(dg_ref)

        gv = g_ref[...]
        lane = _lanes((SGU_CHUNK, SGU_W))
        for c in range(nc):
            rows = slice(c * SGU_CHUNK, (c + 1) * SGU_CHUNK)
            vpre = v_ref[rows, :]
            upre = u_ref[rows, :]
            zv = _gelu(vpre)
            r = lax.rsqrt(jnp.mean(zv * zv, axis=-1, keepdims=True) + EPS)
            zn = zv * r
            vcb = (zn * gv).astype(BF16)
            mixed = b_ref[...]
            for gi in range(4):
                mixed = mixed + jnp.where(_group_mask(lane, gi), _dot(w_ref[gi], vcb, 1, 0), 0.0)
            zu = _gelu(upre)
            dsg_v = dsg_ref[rows, :]
            dc_ref[rows, :SGU_W] = (dsg_v * mixed * _gelu_grad(upre)).astype(BF16)
            dmixed = dsg_v * zu
            db_ref[...] += dmixed
            dvn = jnp.zeros((SGU_CHUNK, SGU_W), F32)
            for gi in range(4):
                dmg = jnp.where(_group_mask(lane, gi), dmixed, 0.0).astype(BF16)
                dw_ref[gi] += _dot(dmg, vcb, 1, 1)
                dvn = dvn + _dot(wt_ref[gi], dmg, 1, 0)
            dg_ref[...] += jnp.sum(dvn * zn, axis=0, keepdims=True)
            dzn = dvn * gv
            dzv = r * (dzn - zn * jnp.mean(dzn * zn, axis=-1, keepdims=True))
            dc_ref[rows, SGU_W:] = (dzv * _gelu_grad(vpre)).astype(BF16)

    blk = pl.BlockSpec((ts, SGU_W), lambda i: (i, 0))
    vec = pl.BlockSpec((1, SGU_W), lambda i: (0, 0))
    w3 = pl.BlockSpec((4, SGU_CHUNK, SGU_CHUNK), lambda i: (0, 0, 0))
    bsp = pl.BlockSpec((SGU_CHUNK, SGU_W), lambda i: (0, 0))
    return pl.pallas_call(
        body,
        name=name,
        grid=(S // ts,),
        in_specs=[
            pl.BlockSpec((ts, SGU_W), lambda i: (i, _U_COL)),
            pl.BlockSpec((ts, SGU_W), lambda i: (i, _U_COL + 1)),
            vec, w3, w3, bsp, blk,
        ],
        out_specs=[pl.BlockSpec((ts, 2 * SGU_W), lambda i: (i, 0)), w3, bsp, vec],
        out_shape=[
            jax.ShapeDtypeStruct((S, 2 * SGU_W), BF16),
            jax.ShapeDtypeStruct((4, SGU_CHUNK, SGU_CHUNK), F32),
            jax.ShapeDtypeStruct((SGU_CHUNK, SGU_W), F32),
            jax.ShapeDtypeStruct((1, SGU_W), F32),
        ],
        compiler_params=_params(("arbitrary",)),
    )(rest, rest, gn.reshape(1, SGU_W), wm, wm_t, bias, dsg)


_GT = 512
_G0 = OFF_G // _GT


def _gate_specs(tm, col_of):
    specs = [pl.BlockSpec((tm, _GT), functools.partial(lambda k, *ids: (col_of(*ids)[0], _G0 + 2 * k + col_of(*ids)[1]), k)) for k in range(3)]
    specs += [pl.BlockSpec((1, _GT), functools.partial(lambda k, *ids: (0, 2 * k + col_of(*ids)[1]), k)) for k in range(3)]
    return specs


def _merge_fwd(rest, bg, ya, yb, yc, *, name):
    S = rest.shape[0]
    tm = _tile(S, 512)

    def body(g1, g2, g3, b1, b2, b3, ya_ref, yb_ref, yc_ref, o_ref):
        acc = _sigmoid(g1[...] + b1[...]) * ya_ref[...]
        acc = acc + _sigmoid(g2[...] + b2[...]) * yb_ref[...]
        acc = acc + _sigmoid(g3[...] + b3[...]) * yc_ref[...]
        o_ref[...] = acc.astype(BF16)

    blk = pl.BlockSpec((tm, _GT), lambda i, j: (i, j))
    return pl.pallas_call(
        body,
        name=name,
        grid=(S // tm, D // _GT),
        in_specs=_gate_specs(tm, lambda i, j: (i, j)) + [blk, blk, blk],
        out_specs=blk,
        out_shape=jax.ShapeDtypeStruct((S, D), BF16),
        compiler_params=_params(("parallel", "parallel")),
    )(rest, rest, rest, bg, bg, bg, ya, yb, yc)


def _merge_bwd(rest, bg, ya, yb, yc, dm, *, name):
    S = rest.shape[0]
    tm = _tile(S, 512)

    def body(g1, g2, g3, b1, b2, b3, ya_ref, yb_ref, yc_ref, dm_ref, dya, dyb, dyc, dg1, dg2, dg3, db1, db2, db3):
        first = pl.program_id(1) == 0
        dmv = dm_ref[...]
        for g_ref, b_ref, y_ref, dy_ref, dg_ref, db_ref in (
            (g1, b1, ya_ref, dya, dg1, db1), (g2, b2, yb_ref, dyb, dg2, db2), (g3, b3, yc_ref, dyc, dg3, db3)):
            gate = _sigmoid(g_ref[...] + b_ref[...])
            dy_ref[...] = (dmv * gate).astype(BF16)
            dpre = dmv * y_ref[...] * gate * (1.0 - gate)
            dg_ref[...] = dpre.astype(BF16)
            part = jnp.sum(dpre, axis=0, keepdims=True)

            @pl.when(first)
            def _():
                db_ref[...] = part

            @pl.when(jnp.logical_not(first))
            def _():
                db_ref[...] += part

    blk = pl.BlockSpec((tm, _GT), lambda j, i: (i, j))
    vec = pl.BlockSpec((1, _GT), lambda j, i: (0, j))
    big = jax.ShapeDtypeStruct((S, D), BF16)
    small = jax.ShapeDtypeStruct((1, D), F32)
    return pl.pallas_call(
        body,
        name=name,
        grid=(D // _GT, S // tm),
        in_specs=_gate_specs(tm, lambda j, i: (i, j)) + [blk, blk, blk, blk],
        out_specs=[blk] * 6 + [vec] * 3,
        out_shape=[big] * 6 + [small] * 3,
        compiler_params=_params(("parallel", "arbitrary")),
    )(rest, rest, rest, bg, bg, bg, ya, yb, yc, dm)


_X_SCALE = XDH ** -0.5


def _xattn_fwd(xq, kv, *, name):
    S = xq.shape[0]
    M = kv.shape[0]
    tq = _tile(S, 512)

    def body(q_ref, k_ref, v_ref, o_ref):
        s = _dot(q_ref[...], k_ref[...], 1, 1) * _X_SCALE
        e = jnp.exp(s - jnp.max(s, axis=-1, keepdims=True))
        p = e / jnp.sum(e, axis=-1, keepdims=True)
        o_ref[...] = _dot(p.astype(BF16), v_ref[...], 1, 0).astype(BF16)

    return pl.pallas_call(
        body,
        name=name,
        grid=(S // tq, XH),
        in_specs=[
            pl.BlockSpec((tq, XDH), lambda i, h: (i, h)),
            pl.BlockSpec((M, XDH), lambda i, h: (0, h)),
            pl.BlockSpec((M, XDH), lambda i, h: (0, XH + h)),
        ],
        out_specs=pl.BlockSpec((tq, XDH), lambda i, h: (i, h)),
        out_shape=jax.ShapeDtypeStruct((S, D), BF16),
        compiler_params=_params(("parallel", "parallel")),
    )(xq, kv, kv)


def _xattn_bwd(xq, kv, do, *, name):
    S = xq.shape[0]
    M = kv.shape[0]
    tq = _tile(S, 512)

    def body(q_ref, k_ref, v_ref, do_ref, dq_ref, dk_ref, dv_ref):
        qb = q_ref[...]
        kb = k_ref[...]
        dob = do_ref[...]
        s = _dot(qb, kb, 1, 1) * _X_SCALE
        e = jnp.exp(s - jnp.max(s, axis=-1, keepdims=True))
        p = e / jnp.sum(e, axis=-1, keepdims=True)
        dp = _dot(dob, v_ref[...], 1, 1)
        ds = (p * (dp - jnp.sum(p * dp, axis=-1, keepdims=True)) * _X_SCALE).astype(BF16)
        dq_ref[...] = _dot(ds, kb, 1, 0).astype(BF16)
        dk_part = _dot(ds, qb, 0, 0)
        dv_part = _dot(p.astype(BF16), dob, 0, 0)

        @pl.when(pl.program_id(1) == 0)
        def _():
            dk_ref[...] = dk_part
            dv_ref[...] = dv_part

        @pl.when(pl.program_id(1) > 0)
        def _():
            dk_ref[...] += dk_part
            dv_ref[...] += dv_part

    qspec = pl.BlockSpec((tq, XDH), lambda h, i: (i, h))
    kspec = pl.BlockSpec((M, XDH), lambda h, i: (0, h))
    dxq, dxk, dxv = pl.pallas_call(
        body,
        name=name,
        grid=(XH, S // tq),
        in_specs=[qspec, kspec, pl.BlockSpec((M, XDH), lambda h, i: (0, XH + h)), qspec],
        out_specs=[qspec, kspec, kspec],
        out_shape=[jax.ShapeDtypeStruct((S, D), BF16), jax.ShapeDtypeStruct((M, D), F32), jax.ShapeDtypeStruct((M, D), F32)],
        compiler_params=_params(("parallel", "arbitrary")),
    )(xq, kv, kv, do)
    return dxq, jnp.concatenate([dxk, dxv], axis=1)


def _adam_math(w, g, m, v):
    m = ADAM_B1 * m + (1.0 - ADAM_B1) * g
    v = ADAM_B2 * v + (1.0 - ADAM_B2) * (g * g)
    m_hat = m / (1.0 - ADAM_B1 ** ADAM_STEP)
    v_hat = v / (1.0 - ADAM_B2 ** ADAM_STEP)
    delta = -ADAM_LR * (m_hat / (jnp.sqrt(v_hat) + ADAM_EPS) + ADAM_WD * w)
    return delta, m, v


def _adamw_sharded(parts, w, m, v, *, name):
    _, R, C = w.shape
    Cp = parts[0].shape[2]
    tm = _tile(R, 256)
    nr = R // tm

    def body(p0_ref, p1_ref, w_ref, m_ref, v_ref, g_ref, d_ref, mo_ref, vo_ref):
        def update(p_ref):
            g = p_ref[0][:, :C].astype(F32)
            for dev in range(1, N_DEV):
                g = g + p_ref[dev][:, :C].astype(F32)
            delta, mn, vn = _adam_math(w_ref[...], g, m_ref[...], v_ref[...])
            g_ref[...] = g
            d_ref[...] = delta
            mo_ref[...] = mn
            vo_ref[...] = vn

        @pl.when(pl.program_id(0) == 0)
        def _():
            update(p0_ref)

        @pl.when(pl.program_id(0) == 1)
        def _():
            update(p1_ref)

    p0 = pl.BlockSpec((N_DEV, tm, Cp), lambda l, i: (0, i * (1 - l) + (nr - 1) * l, 0))
    p1 = pl.BlockSpec((N_DEV, tm, Cp), lambda l, i: (0, i * l, 0))
    blk = pl.BlockSpec((None, tm, C), lambda l, i: (l, i, 0))
    sds = jax.ShapeDtypeStruct(w.shape, F32)
    return pl.pallas_call(
        body,
        name=name,
        grid=(DEPTH, nr),
        in_specs=[p0, p1, blk, blk, blk],
        out_specs=[blk] * 4,
        out_shape=[sds] * 4,
        compiler_params=_params(("arbitrary", "arbitrary")),
    )(parts[0], parts[1], w, m, v)


def _adamw_small(g, w, m, v, *, name):
    n = len(g)

    def body(*refs):
        g_refs, w_refs, m_refs, v_refs = (refs[k * n:(k + 1) * n] for k in range(4))
        d_out, m_out, v_out = (refs[(4 + k) * n:(5 + k) * n] for k in range(3))
        for t in range(n):
            delta, mn, vn = _adam_math(w_refs[t][...], g_refs[t][...], m_refs[t][...], v_refs[t][...])
            d_out[t][...] = delta
            m_out[t][...] = mn
            v_out[t][...] = vn

    vm = pl.BlockSpec(memory_space=pltpu.VMEM)
    shapes = [jax.ShapeDtypeStruct(a.shape, F32) for a in w]
    outs = pl.pallas_call(
        body,
        name=name,
        in_specs=[vm] * (4 * n),
        out_specs=[vm] * (3 * n),
        out_shape=shapes * 3,
        compiler_params=pltpu.CompilerParams(vmem_limit_bytes=VMEM_LIMIT),
    )(*g, *w, *m, *v)
    return outs[:n], outs[n:2 * n], outs[2 * n:]


def _position():
    return lax.axis_index("x"), lax.axis_index("y"), lax.axis_index("c")


def _dev_index(px, py, pc):
    return 4 * px + 2 * py + pc


_ANY = pl.BlockSpec(memory_space=pl.ANY)


def _peers(x, y, c):
    out = []
    for mask in range(1, N_DEV):
        fx, fy, fc = (mask >> 2) & 1, (mask >> 1) & 1, mask & 1
        out.append((1 - x if fx else x, 1 - y if fy else y, 1 - c if fc else c))
    return out


_HBM = pl.BlockSpec(memory_space=pltpu.HBM)
_SEM = pl.BlockSpec(memory_space=pltpu.SEMAPHORE)


def _own_block_placed(block, like):
    x, y, c = _position()
    return lax.dynamic_update_index_in_dim(lax.empty(like.shape, like.dtype), block, _dev_index(x, y, c), 0)


_COPY_BYTES = 256 << 10
_MAX_PIECES = 8


def _pieces(blocks):
    out = []
    for t, b in enumerate(blocks):
        R, C = b.shape[-2:]
        n = max(1, min(_MAX_PIECES, R * C * jnp.dtype(b.dtype).itemsize // _COPY_BYTES))
        while n > 1 and R % (16 * n):
            n -= 1
        out += [(t, pl.ds(j * (R // n), R // n) if n > 1 else None) for j in range(n)]
    return out


def _cut(block, rows):
    return block if rows is None else block.at[rows]


def _copies(per_piece):
    def mark(fn):
        fn.per_piece = per_piece
        return fn
    return mark


@_copies(N_DEV - 1)
def _plan_exchange(srcs, lands, send_sems, recv_sems, arrivals):
    x, y, c = _position()
    me = _dev_index(x, y, c)
    out = []
    for k, peer in enumerate(_peers(x, y, c)):
        p = _dev_index(*peer)
        for i, (t, rows) in enumerate(_pieces(lands)):
            sems = dict(send_sem=send_sems.at[7 * i + k], recv_sem=recv_sems.at[7 * i + k], device_id=peer, device_id_type=MESH)
            src, dst = (lands[t].at[p], lands[t].at[p]) if arrivals else (srcs[t].at[p], lands[t].at[me])
            out.append(pltpu.make_async_remote_copy(src_ref=_cut(src, rows), dst_ref=_cut(dst, rows), **sems))
    return out


@_copies(N_DEV - 1)
def _plan_broadcast(srcs, lands, send_sems, recv_sems, arrivals):
    x, y, c = _position()
    me = _dev_index(x, y, c)
    out = []
    for k, peer in enumerate(_peers(x, y, c)):
        p = _dev_index(*peer)
        for i, (t, rows) in enumerate(_pieces(lands)):
            sems = dict(send_sem=send_sems.at[7 * i + k], recv_sem=recv_sems.at[7 * i + k], device_id=peer, device_id_type=MESH)
            src, dst = (lands[t].at[p], lands[t].at[p]) if arrivals else (srcs[t], lands[t].at[me])
            out.append(pltpu.make_async_remote_copy(src_ref=_cut(src, rows), dst_ref=_cut(dst, rows), **sems))
    return out


@_copies(4)
def _plan_gather_out(srcs, lands, send_sems, recv_sems, arrivals):
    x, y, c = _position()
    me = _dev_index(x, y, c)
    out = []
    for k, peer in enumerate([(x, y, 1 - c), (1 - x, y, c), (x, 1 - y, c), (1 - x, 1 - y, c)]):
        p = _dev_index(*peer)
        for i, (t, rows) in enumerate(_pieces(lands)):
            sems = dict(send_sem=send_sems.at[4 * i + k], recv_sem=recv_sems.at[4 * i + k], device_id=peer, device_id_type=MESH)
            src, dst = (lands[t].at[p], lands[t].at[p]) if arrivals else (srcs[t], lands[t].at[me])
            out.append(pltpu.make_async_remote_copy(src_ref=_cut(src, rows), dst_ref=_cut(dst, rows), **sems))
    return out


@_copies(3)
def _plan_gather_pass(srcs, lands, send_sems, recv_sems, arrivals):
    x, y, c = _position()
    sibling = (x, y, 1 - c)
    out = []
    for k, chip in enumerate([(1 - x, y), (x, 1 - y), (1 - x, 1 - y)]):
        p = _dev_index(*chip, 1 - c) if arrivals else _dev_index(*chip, c)
        for i, (t, rows) in enumerate(_pieces(lands)):
            sems = dict(send_sem=send_sems.at[3 * i + k], recv_sem=recv_sems.at[3 * i + k], device_id=sibling, device_id_type=MESH)
            block = _cut(lands[t].at[p], rows)
            out.append(pltpu.make_async_remote_copy(src_ref=block, dst_ref=block, **sems))
    return out


def _split_start(plan, srcs, lands, *, after=None, name):
    n_src, n = len(srcs), len(srcs) + len(lands)
    n_sem = plan.per_piece * len(_pieces(lands))
    order = [] if after is None else [after]

    def body(*refs):
        send_sems, recv_sems = refs[n + len(order):n + len(order) + 2]
        token = refs[-1]
        for cp in plan(refs[:n_src], refs[n_src:n], send_sems, recv_sems, arrivals=False):
            cp.start()
        token[...] = jnp.zeros_like(token)

    hbm = lambda a: pltpu.HBM(a.shape, a.dtype)
    outs = pl.pallas_call(
        body,
        name=name,
        in_specs=[_HBM] * n + [_ANY] * len(order),
        out_specs=[_SEM, _SEM] + [_HBM] * n + [pl.BlockSpec(memory_space=pltpu.VMEM)],
        out_shape=[pltpu.SemaphoreType.DMA((n_sem,)), pltpu.SemaphoreType.DMA((n_sem,))] + [hbm(a) for a in (*srcs, *lands)]
        + [jax.ShapeDtypeStruct(_TOKEN, F32)],
        input_output_aliases={i: 2 + i for i in range(n)},
        compiler_params=pltpu.CompilerParams(has_side_effects=pltpu.SideEffectType.DATAFLOW_SIDE_EFFECTING),
    )(*[pltpu.with_memory_space_constraint(a, pltpu.HBM) for a in (*srcs, *lands)], *order)
    return (outs[0], outs[1], outs[2:2 + n_src], outs[2 + n_src:2 + n]), outs[-1]


def _split_wait(plan, state, after, *, name):
    send_sems, recv_sems, srcs, lands = state
    n_src, n = len(srcs), len(srcs) + len(lands)

    def body(*refs):
        send_refs, recv_refs = refs[n:n + 2]
        for cp in plan(refs[:n_src], refs[n_src:n], send_refs, recv_refs, arrivals=False):
            cp.wait_send()
        for cp in plan(refs[:n_src], refs[n_src:n], send_refs, recv_refs, arrivals=True):
            cp.wait_recv()

    hbm = lambda a: pltpu.HBM(a.shape, a.dtype)
    outs = pl.pallas_call(
        body,
        name=name,
        in_specs=[_HBM] * n + [_SEM, _SEM, _ANY],
        out_specs=[_HBM] * n,
        out_shape=[hbm(a) for a in (*srcs, *lands)],
        input_output_aliases={i: i for i in range(n)},
        compiler_params=pltpu.CompilerParams(has_side_effects=pltpu.SideEffectType.DATAFLOW_SIDE_EFFECTING),
    )(*srcs, *lands, send_sems, recv_sems, after)
    return outs[n_src:]


def _sum_blocks(blocks, *, name):
    _, R, C = blocks.shape
    tm = next(R // n for n in (4, 3, 2, 1) if R % (8 * n) == 0)

    def body(b_ref, o_ref):
        g = b_ref[0]
        for dev in range(1, N_DEV):
            g = g + b_ref[dev]
        o_ref[...] = g

    return pl.pallas_call(
        body,
        name=name,
        grid=(R // tm,),
        in_specs=[pl.BlockSpec((N_DEV, tm, C), lambda i: (0, i, 0))],
        out_specs=pl.BlockSpec((tm, C), lambda i: (i, 0)),
        out_shape=jax.ShapeDtypeStruct((R, C), F32),
        compiler_params=_params(("parallel",)),
    )(blocks)


def _block_diag(w):
    out = jnp.zeros((POOL_W, POOL_W), w.dtype)
    for gi in range(4):
        out = out.at[64 * gi:64 * (gi + 1), 64 * gi:64 * (gi + 1)].set(w[gi])
    return out


def _layer_consts(sp, l):
    causal = jnp.tril(jnp.ones((SGU_CHUNK, SGU_CHUNK), F32))
    wm = (sp["sgu_w"][l] * causal[None]).astype(BF16)
    wbd = _block_diag(sp["pool_w"][l]).astype(BF16)
    return dict(
        wbd=wbd, wbd_t=wbd.T, wm=wm, wm_t=wm.transpose(0, 2, 1),
        sgu_bias=jnp.repeat(sp["sgu_b"][l].T, 64, axis=1),
        bpad=jnp.pad(sp["b_forget"][l], (0, F_LANES - FOX_H)).reshape(1, F_LANES),
        bg=sp["b_gate"][l].reshape(1, 3 * D),
    )


def _relu2(acc):
    return acc, jnp.square(jnp.maximum(acc, 0.0))


def _relu2_grad(acc, z):
    return (acc * 2.0 * jnp.maximum(z, 0.0),)


def _layer_fwd(l, x, h, mem, source, sp):
    S = x.shape[0]
    t = _tile(S, 256)
    c = _layer_consts(sp, l)
    n = f"l{l}_"
    W, after = source(l, "begin", x)
    if h is None:
        h, after = _rms_fwd(x, sp["norm_mix_g"][l], after=after, name=n + "norm_mix"), None
    hm = _rms_fwd(mem, sp["norm_mem_g"][l], name=n + "norm_mem")
    more, token = source(l, "normed", hm)
    W.update(more)
    qkv = _mm(h, W["qkv"], out_dtypes=(BF16,), after=after if token is None else token, name=n + "qkv")
    rest = _mm(h, W["rest"], name=n + "rest")
    pa = _pool_fwd(rest, c["wbd"], sp["pool_scale"][l], name=n + "pool")
    cum, cum_t = _fox_prep(rest, c["bpad"], name=n + "fox_prep")
    fk3 = cum_t[:FOX_H].reshape(FOX_H, S // t, t)
    o, lse = _fox_fwd(qkv, cum, fk3, name=n + "fox")
    more, _ = source(l, "attended", o)
    W.update(more)
    sg = _sgu_fwd(rest, sp["sgu_norm_g"][l], c["wm"], c["sgu_bias"], name=n + "sgu")
    more, after = source(l, "mixed", sg)
    W.update(more)
    ya = _mm(pa, W["ba"], out_dtypes=(BF16,), after=after, name=n + "branch_a")
    yb = _mm(o, W["bb"], out_dtypes=(BF16,), name=n + "branch_b")
    yc = _mm(sg, W["bc"], out_dtypes=(BF16,), name=n + "branch_c")
    merged = _merge_fwd(rest, c["bg"], ya, yb, yc, name=n + "merge")
    whole_rows = dict(epilogue=_add_norm, out_dtypes=(F32, BF16), tm=1024, tn=D)
    x1, hx = _mm(merged, W["out"], extras=(x,), row_extras=(sp["norm_xattn_g"][l].reshape(1, D),), name=n + "out", **whole_rows)
    xq = _mm(hx, W["xq"], out_dtypes=(BF16,), name=n + "xq")
    kv = _mm(hm, W["xkv"], out_dtypes=(BF16,), name=n + "xkv")
    o2 = _xattn_fwd(xq, kv, name=n + "xattn")
    x2, hf = _mm(o2, W["xo"], extras=(x1,), row_extras=(sp["norm_ffn_g"][l].reshape(1, D),), name=n + "xo", **whole_rows)
    z, act = _mm(hf, W["ff1"], epilogue=_relu2, out_dtypes=(BF16, BF16), name=n + "ff1")
    _, after = source(l, "expanded", act)
    if l + 1 < DEPTH:
        x3, h_next = _mm(act, W["ff2"], extras=(x2,), row_extras=(sp["norm_mix_g"][l + 1].reshape(1, D),), after=after, name=n + "ff2",
                         **whole_rows)
    else:
        x3, h_next = _mm(act, W["ff2"], extras=(x2,), epilogue=_add, after=after, name=n + "ff2"), None
    saved = dict(x=x, h=h, qkv=qkv, rest=rest, pa=pa, cum=cum, fk3=fk3, o=o, lse=lse, sg=sg, ya=ya, yb=yb, yc=yc,
                 merged=merged, x1=x1, hx=hx, hm=hm, xq=xq, kv=kv, o2=o2, x2=x2, hf=hf, z=z, act=act, c=c)
    return x3, h_next, saved, W


def _layer_bwd(l, dx3, dx3_bf, sv, mem, W, sp, grads_done):
    S = dx3.shape[0]
    c = sv["c"]
    n = f"l{l}b_"
    bf = dict(out_dtypes=(BF16,))
    gw, gs = {}, {}
    gw["ff2"] = _mm(sv["act"], dx3_bf, ta=True, name=n + "dw_ff2", **bf)
    dz = _mm(dx3_bf, W["ff2"], tb=True, extras=(sv["z"],), epilogue=_relu2_grad, name=n + "dz", **bf)
    gw["ff1"] = _mm(sv["hf"], dz, ta=True, shard_out=True, name=n + "dw_ff1", **bf)
    whole_rows = dict(epilogue=_norm_grad, out_dtypes=(F32, BF16, F32), row_outs=1, tm=1024, tn=D)
    gain = lambda key: (sp[key][l].reshape(1, D),)
    dx2, dx2_bf, dg = _mm(dz, W["ff1"], tb=True, extras=(sv["x2"], dx3), row_extras=gain("norm_ffn_g"), name=n + "dhf", **whole_rows)
    gs["norm_ffn_g"] = dg.reshape(D)
    gw["xo"] = _mm(sv["o2"], dx2_bf, ta=True, name=n + "dw_xo", **bf)
    do2 = _mm(dx2_bf, W["xo"], tb=True, name=n + "do2", **bf)
    dxq, dkv = _xattn_bwd(sv["xq"], sv["kv"], do2, name=n + "dxattn")
    gw["xq"] = _mm(sv["hx"], dxq, ta=True, name=n + "dw_xq", **bf)
    gw["xkv"] = _mm(sv["hm"], dkv, ta=True, shard_out=True, name=n + "dw_xkv", **bf)
    dhm = _mm(dkv, W["xkv"], tb=True, name=n + "dhm")
    _, gs["norm_mem_g"] = _rms_bwd(mem, sp["norm_mem_g"][l], dhm, jnp.zeros_like(mem), name=n + "dnorm_mem")
    dx1, dx1_bf, dg = _mm(dxq, W["xq"], tb=True, extras=(sv["x1"], dx2), row_extras=gain("norm_xattn_g"), name=n + "dhx", **whole_rows)
    gs["norm_xattn_g"] = dg.reshape(D)
    after, gw = grads_done(l, gw), {}
    gw["out"] = _mm(sv["merged"], dx1_bf, ta=True, name=n + "dw_out", **bf)
    dm = _mm(dx1_bf, W["out"], tb=True, after=after, name=n + "dmerged")
    dya, dyb, dyc, dg1, dg2, dg3, db1, db2, db3 = _merge_bwd(sv["rest"], c["bg"], sv["ya"], sv["yb"], sv["yc"], dm, name=n + "dmerge")
    gs["b_gate"] = jnp.concatenate([db1, db2, db3], axis=1).reshape(3 * D)
    gw["ba"] = _mm(sv["pa"], dya, ta=True, shard_out=True, name=n + "dw_ba", **bf)
    gw["bb"] = _mm(sv["o"], dyb, ta=True, shard_out=True, name=n + "dw_bb", **bf)
    gw["bc"] = _mm(sv["sg"], dyc, ta=True, shard_out=True, name=n + "dw_bc", **bf)
    after, gw = grads_done(l, gw), {}
    dpa = _mm(dya, W["ba"], tb=True, name=n + "dpa")
    do = _mm(dyb, W["bb"], tb=True, after=after, name=n + "do", **bf)
    dsg = _mm(dyc, W["bc"], tb=True, name=n + "dsg")
    da, dwbd, dscale = _pool_bwd(sv["rest"], c["wbd"], c["wbd_t"], sp["pool_scale"][l], dpa, name=n + "dpool")
    gs["pool_w"] = jnp.stack([dwbd[64 * gi:64 * (gi + 1), 64 * gi:64 * (gi + 1)] for gi in range(4)])
    gs["pool_scale"] = dscale.reshape(POOL_W)
    dq, dk, dv, dfq, dfk = _fox_bwd(sv["qkv"], sv["cum"], sv["fk3"], sv["o"], do, sv["lse"], name=n + "dfox")
    dcum = dfq + jnp.pad(dfk.reshape(FOX_H, S).T, ((0, 0), (0, F_LANES - FOX_H)))
    df, dbf = _fox_post(sv["rest"], c["bpad"], dcum, name=n + "dfox_post")
    gs["b_forget"] = dbf[0, :FOX_H]
    dc, dwm, dbias, dgn = _sgu_bwd(sv["rest"], sp["sgu_norm_g"][l], c["wm"], c["wm_t"], c["sgu_bias"], dsg, name=n + "dsgu")
    gs["sgu_w"] = dwm * jnp.tril(jnp.ones((SGU_CHUNK, SGU_CHUNK), F32))[None]
    gs["sgu_b"] = dbias.reshape(SGU_CHUNK, 4, 64).sum(axis=2).T
    gs["sgu_norm_g"] = dgn.reshape(SGU_W)
    dqkv = [dq, dk, dv]
    drest = [jnp.concatenate([da, df, jnp.zeros((S, OFF_C - OFF_F - F_LANES), BF16), dc], axis=1), dg1, dg2, dg3]
    gw["qkv"] = _mm(sv["h"], dqkv, ta=True, name=n + "dw_qkv", **bf)
    gw["rest"] = _mm(sv["h"], drest, ta=True, name=n + "dw_rest", **bf)
    after = grads_done(l, gw)
    dh = _mm(dqkv, W["qkv"], tb=True, after=after, name=n + "dh_qkv")
    dx, dx_bf, dg = _mm(drest, W["rest"], tb=True, extras=(dh, sv["x"], dx1), row_extras=gain("norm_mix_g"), name=n + "dh",
                        **{**whole_rows, "epilogue": _add_norm_grad, "tm": 512})
    gs["norm_mix_g"] = dg.reshape(D)
    return dx, dx_bf, gs


def _local_step(x, mem, target, sp, source, grads_done):
    saved, Ws, h = [], [], None
    for l in range(DEPTH):
        x, h, sv, W = _layer_fwd(l, x, h, mem, source, sp)
        saved.append(sv)
        Ws.append(W)
    loss, dx, dgf = _final_loss(x, sp["final_norm_g"], target, name="final_loss")
    gss = [None] * DEPTH
    dx_bf = dx
    for l in reversed(range(DEPTH)):
        dx, dx_bf, gss[l] = _layer_bwd(l, dx, dx_bf, saved[l], mem, Ws[l], sp, grads_done)
    small = {k: jnp.stack([gss[l][k] for l in range(DEPTH)]) for k in gss[0]}
    small["final_norm_g"] = dgf
    return loss, dx, small


_SMALL = ["norm_mix_g", "b_forget", "pool_w", "pool_scale", "sgu_norm_g", "sgu_w", "sgu_b", "b_gate", "norm_xattn_g",
          "norm_mem_g", "norm_ffn_g", "final_norm_g"]
_COL = {"w_branch_a": "ba", "w_branch_b": "bb", "w_branch_c": "bc", "w_xkv": "xkv", "w_ff1": "ff1"}
_ROW = {"w_out": "out", "w_xq": "xq", "w_xo": "xo", "w_ff2": "ff2"}
_BIG = ["w_in", "w_branch_a", "w_branch_b", "w_branch_c", "w_out", "w_xq", "w_xkv", "w_xo", "w_ff1", "w_ff2"]
_PACK_LANES = 128


def _as_rows(a):
    return a.reshape(-1, a.shape[-1])


def _pack(tensors):
    rows = []
    for a in tensors:
        flat = a.reshape(-1)
        flat = jnp.pad(flat, (0, (-flat.shape[0]) % (8 * _PACK_LANES)))
        rows.append(flat.reshape(-1, _PACK_LANES))
    n_rows = sum(r.shape[0] for r in rows)
    rows.append(jnp.zeros(((-n_rows) % (8 * N_DEV), _PACK_LANES), F32))
    return jnp.concatenate(rows, axis=0)


def _unpack(packed, like):
    out, r = [], 0
    for a in like:
        size = math.prod(a.shape)
        nr = 8 * (-(-size // (8 * _PACK_LANES)))
        out.append(packed[r:r + nr].reshape(-1)[:size].reshape(a.shape))
        r += nr
    return out


_SHARD_IN = N_IN // N_DEV
_SHARD_IN_PAD = -(-_SHARD_IN // 128) * 128


def _columns(pieces, start, stop):
    out, at = [], 0
    for p in pieces:
        lo, hi = max(start, at), min(stop, at + p.shape[1])
        if lo < hi:
            out.append(p[:, lo - at:hi - at])
        at += p.shape[1]
    return out


def _split_w_in(blocks):
    K = blocks[0].shape[0]
    pad = jnp.zeros((K, OFF_C - OFF_F - FOX_H), blocks[0].dtype)
    cols = functools.partial(_columns, blocks)
    rest = jnp.concatenate(cols(0, R_OFF_Q) + cols(R_OFF_F, R_OFF_C) + [pad] + cols(R_OFF_C, N_IN), axis=1)
    return jnp.concatenate(cols(R_OFF_Q, R_OFF_F), axis=1), rest


def _join_w_in(qkv, rest):
    in_order = [rest[:, :R_OFF_Q], qkv, rest[:, OFF_F:OFF_F + FOX_H], rest[:, OFF_C:]]
    pad = jnp.zeros((qkv.shape[0], _SHARD_IN_PAD - _SHARD_IN), qkv.dtype)
    return jnp.stack([jnp.concatenate(_columns(in_order, _SHARD_IN * d, _SHARD_IN * (d + 1)) + [pad], axis=1) for d in range(N_DEV)])


_FIRST = ["w_in"]
_LATER = [k for k in _BIG if k not in _FIRST]


def _layer_weights(gathered):
    W = {}
    if "w_in" in gathered:
        W.update(zip(("qkv", "rest"), _split_w_in([gathered["w_in"][d][:, :_SHARD_IN] for d in range(N_DEV)])))
    for name, key in _COL.items():
        if name in gathered:
            W[key] = _Gathered(gathered[name])
    for name, key in _ROW.items():
        if name in gathered:
            W[key] = gathered[name].reshape(-1, gathered[name].shape[-1])
    return W


def _grad_blocks(gw):
    parts = {}
    if "qkv" in gw:
        parts["w_in"] = _join_w_in(gw["qkv"], gw["rest"])
    for name, key in _COL.items():
        if key in gw:
            parts[name] = gw[key]
    for name, key in _ROW.items():
        if key in gw:
            parts[name] = gw[key].reshape(N_DEV, -1, gw[key].shape[-1])
    return parts


def kernel(x, mem, norm_mix_g, w_in, b_forget, pool_w, pool_scale, sgu_norm_g, sgu_w, sgu_b, w_branch_a, w_branch_b, w_branch_c, b_gate, w_out, norm_xattn_g, norm_mem_g, w_xq, w_xkv, w_xo, norm_ffn_g, w_ff1, w_ff2, final_norm_g, loss_target, m_norm_mix_g, m_w_in, m_b_forget, m_pool_w, m_pool_scale, m_sgu_norm_g, m_sgu_w, m_sgu_b, m_w_branch_a, m_w_branch_b, m_w_branch_c, m_b_gate, m_w_out, m_norm_xattn_g, m_norm_mem_g, m_w_xq, m_w_xkv, m_w_xo, m_norm_ffn_g, m_w_ff1, m_w_ff2, m_final_norm_g, v_norm_mix_g, v_w_in, v_b_forget, v_pool_w, v_pool_scale, v_sgu_norm_g, v_sgu_w, v_sgu_b, v_w_branch_a, v_w_branch_b, v_w_branch_c, v_b_gate, v_w_out, v_norm_xattn_g, v_norm_mem_g, v_w_xq, v_w_xkv, v_w_xo, v_norm_ffn_g, v_w_ff1, v_w_ff2, v_final_norm_g):
    names = ["norm_mix_g", "w_in", "b_forget", "pool_w", "pool_scale", "sgu_norm_g", "sgu_w", "sgu_b", "w_branch_a", "w_branch_b",
             "w_branch_c", "b_gate", "w_out", "norm_xattn_g", "norm_mem_g", "w_xq", "w_xkv", "w_xo", "norm_ffn_g", "w_ff1", "w_ff2",
             "final_norm_g"]
    w = dict(zip(names, [norm_mix_g, w_in, b_forget, pool_w, pool_scale, sgu_norm_g, sgu_w, sgu_b, w_branch_a, w_branch_b, w_branch_c,
                         b_gate, w_out, norm_xattn_g, norm_mem_g, w_xq, w_xkv, w_xo, norm_ffn_g, w_ff1, w_ff2, final_norm_g]))
    m = dict(zip(names, [m_norm_mix_g, m_w_in, m_b_forget, m_pool_w, m_pool_scale, m_sgu_norm_g, m_sgu_w, m_sgu_b, m_w_branch_a,
                         m_w_branch_b, m_w_branch_c, m_b_gate, m_w_out, m_norm_xattn_g, m_norm_mem_g, m_w_xq, m_w_xkv, m_w_xo,
                         m_norm_ffn_g, m_w_ff1, m_w_ff2, m_final_norm_g]))
    v = dict(zip(names, [v_norm_mix_g, v_w_in, v_b_forget, v_pool_w, v_pool_scale, v_sgu_norm_g, v_sgu_w, v_sgu_b, v_w_branch_a,
                         v_w_branch_b, v_w_branch_c, v_b_gate, v_w_out, v_norm_xattn_g, v_norm_mem_g, v_w_xq, v_w_xkv, v_w_xo,
                         v_norm_ffn_g, v_w_ff1, v_w_ff2, v_final_norm_g]))

    sp = {k: w[k] for k in _SMALL}
    shards = [{k: w[k][l].astype(BF16) for k in _BIG} for l in range(DEPTH)]
    for sh in shards:
        sh["w_in"] = jnp.pad(sh["w_in"], ((0, 0), (0, _SHARD_IN_PAD - _SHARD_IN)))
    me = _dev_index(*_position())

    def gather_out(l, keys, name, after=None):
        srcs = [shards[l][k] for k in keys]
        lands = [_own_block_placed(a, jax.ShapeDtypeStruct((N_DEV, *a.shape), a.dtype)) for a in srcs]
        state, token = _split_start(_plan_gather_out, srcs, lands, after=after, name=name + "_out_start")
        return (keys, name, state), token

    def gather_pass(job, value):
        keys, name, state = job
        lands = _split_wait(_plan_gather_out, state, value, name=name + "_out_wait")
        state, token = _split_start(_plan_gather_pass, [], lands, name=name + "_pass_start")
        return (keys, name, state), token, lands[0]

    def gather_end(job, value):
        keys, name, state = job
        return _layer_weights(dict(zip(keys, _split_wait(_plan_gather_pass, state, value, name=name + "_pass_wait"))))

    jobs = {}

    def source(l, point, value):
        if (l, point) == (0, "begin"):
            jobs["l0_first"], token = gather_out(0, _FIRST, "gather_l0_first")
            return {}, token
        if (l, point) == (0, "normed"):
            jobs["l0_first"], token, arrived = gather_pass(jobs["l0_first"], value)
            jobs["l0"], _ = gather_out(0, _LATER, "gather_l0", after=arrived)
            return gather_end(jobs.pop("l0_first"), token), None
        if (l, point) == (0, "attended"):
            jobs["l0"], _, arrived = gather_pass(jobs["l0"], value)
            jobs["l1_first"], token = gather_out(1, _FIRST, "gather_l1_first", after=arrived)
            jobs["l1"], jobs["token"] = gather_out(1, _LATER, "gather_l1", after=token)
            return {}, None
        if (l, point) == (0, "mixed"):
            return gather_end(jobs.pop("l0"), value), jobs.pop("token")
        if (l, point) == (0, "expanded"):
            jobs["l1_first"], token, _ = gather_pass(jobs["l1_first"], value)
            return {}, token
        if (l, point) == (1, "begin"):
            W = gather_end(jobs.pop("l1_first"), value)
            jobs["l1"], token, _ = gather_pass(jobs["l1"], value)
            return W, token
        if (l, point) == (1, "mixed"):
            return gather_end(jobs.pop("l1"), value), None
        return {}, None

    received = [{} for _ in range(DEPTH)]
    travelling = []

    def grads_done(l, gw):
        blocks = _grad_blocks(gw)
        keys = [k for k in _BIG if k in blocks]
        parts = [blocks[k] for k in keys]
        group = f"exchange_grads_l{l}_" + ("in" if "w_in" in blocks else "merge" if "w_out" in blocks else "mlp")
        lands = [_own_block_placed(lax.dynamic_index_in_dim(p, me, 0, keepdims=False), p) for p in parts]
        state, token = _split_start(_plan_exchange, parts, lands, name=group + "_start")
        travelling.append((l, keys, state, group + "_wait"))
        return token

    loss, dx, small = _local_step(x[0], mem[0], loss_target[0], sp, source, grads_done)
    grads, deltas, new_m, new_v = {}, {}, {}, {}
    like = [loss] + [w[k] for k in _SMALL]
    packed = _pack([loss] + [small[k] for k in _SMALL])
    eighths = packed.reshape(N_DEV, -1, _PACK_LANES)
    own = lambda a: _own_block_placed(lax.dynamic_index_in_dim(a, me, 0, keepdims=False) if a.ndim == 3 else a, eighths)
    scatter, done = _split_start(_plan_exchange, [eighths], [own(eighths)], after=dx, name="small_grads_scatter_start")

    def reduce_small(after):
        mine = _sum_blocks(_split_wait(_plan_exchange, scatter, after, name="small_grads_scatter_wait")[0], name="small_grads_sum")
        return _split_start(_plan_broadcast, [mine], [own(mine)], name="small_grads_gather_start")

    def update_small(state, after):
        total = _split_wait(_plan_broadcast, state, after, name="small_grads_gather_wait")[0].reshape(packed.shape)
        loss_sum, *g_small = _unpack(total, like)
        rows = lambda d: [_as_rows(d[k]) for k in _SMALL]
        outs = _adamw_small([_as_rows(g) for g in g_small], rows(w), rows(m), rows(v), name="adamw_small")
        grads.update(zip(_SMALL, g_small))
        for dst, vals in zip((deltas, new_m, new_v), outs):
            dst.update({k: a.reshape(w[k].shape) for k, a in zip(_SMALL, vals)})
        return loss_sum[0, 0], outs[0][0]

    groups = list(dict.fromkeys(tuple(keys) for _, keys, _, _ in travelling))
    for n_done, group_keys in enumerate(groups):
        if n_done == 1:
            gather, _ = reduce_small(done)
        if n_done == len(groups) - 1:
            loss, done = update_small(gather, done)
        for l, keys, state, wait_name in travelling:
            if tuple(keys) == group_keys:
                received[l].update(zip(keys, _split_wait(_plan_exchange, state, done, name=wait_name)))
        for k in group_keys:
            outs = _adamw_sharded([received[l][k] for l in range(DEPTH)], w[k], m[k], v[k], name="adamw_" + k)
            grads[k], deltas[k], new_m[k], new_v[k] = outs
        done = grads[group_keys[-1]]

    return (loss, dx[None], *[grads[k] for k in names], *[deltas[k] for k in names], *[new_m[k] for k in names],
            *[new_v[k] for k in names])
```

```python
import functools
import math

import jax
import jax.numpy as jnp
from jax import lax
from jax.experimental import pallas as pl
from jax.experimental.pallas import tpu as pltpu

F32 = jnp.float32
BF16 = jnp.bfloat16
MESH = pl.DeviceIdType.MESH

N_DEV = 8
D = 1024
DEPTH = 2
EPS = 1e-6
NEG = -1e30
POOL_W = 256
FOX_H = 8
FOX_DH = 64
FOX_W = 512
SGU_W = 256
SGU_CHUNK = 128
XH = 4
XDH = 256
N_IN = 5384
R_OFF_Q, R_OFF_F, R_OFF_C = 256, 1792, 1800
QKV_W = 3 * FOX_W
OFF_A, OFF_F, OFF_C, OFF_G, REST_W = 0, 256, 512, 1024, 4096
F_LANES = 128

ADAM_LR = 0.001
ADAM_B1 = 0.9
ADAM_B2 = 0.999
ADAM_EPS = 1e-08
ADAM_WD = 0.01
ADAM_STEP = 10

VMEM_LIMIT = 56 * 1024 * 1024


def _tile(n, pref):
    t = min(n, pref)
    while n % t:
        t -= 128
    assert t > 0, (n, pref)
    return t


def _params(sem=None):
    return pltpu.CompilerParams(dimension_semantics=sem, vmem_limit_bytes=VMEM_LIMIT)


def _dot(a, b, ca, cb):
    return lax.dot_general(a, b, (((ca,), (cb,)), ((), ())), preferred_element_type=F32)


def _sigmoid(z):
    return 1.0 / (1.0 + jnp.exp(-z))


_GELU_K = math.sqrt(2.0 / math.pi)
_GELU_C = 0.044715


def _gelu(x):
    return 0.5 * x * (1.0 + jnp.tanh(_GELU_K * (x + _GELU_C * x * x * x)))


def _gelu_grad(x):
    t = jnp.tanh(_GELU_K * (x + _GELU_C * x * x * x))
    return 0.5 * (1.0 + t) + 0.5 * x * (1.0 - t * t) * _GELU_K * (1.0 + 3.0 * _GELU_C * x * x)


def _rows(shape):
    return lax.broadcasted_iota(jnp.int32, shape, 0)


def _lanes(shape):
    return lax.broadcasted_iota(jnp.int32, shape, 1)


class _Gathered:
    def __init__(self, arr):
        self.arr = arr
        self.shape = (arr.shape[1], N_DEV * arr.shape[2])


_TOKEN = (8, 128)


def _mm(a, b, *, ta=False, tb=False, extras=(), row_extras=(), epilogue=None, out_dtypes=(F32,), row_outs=0, shard_out=False, after=None,
        tm=None, tn=512, tk=None, name):
    a_parts = list(a) if isinstance(a, (list, tuple)) else [a]
    b_parts = list(b) if isinstance(b, (list, tuple)) else [b]
    gathered = isinstance(b, _Gathered)
    assert (len(a_parts) == 1 or not ta) and (len(b_parts) == 1 or not tb) and min(len(a_parts), len(b_parts)) == 1
    a0, b0 = a_parts[0], b_parts[0]
    M, K = (a0.shape[1], a0.shape[0]) if ta else (a0.shape[0], a0.shape[1] * len(a_parts))
    N, Kb = b0.shape if tb else (b0.shape[1] * len(b_parts), b0.shape[0])
    assert Kb == K, (a0.shape, b0.shape, ta, tb)
    if gathered:
        if tb:
            tk = b.arr.shape[2]
        else:
            tn = b.arr.shape[2]
    if len(a_parts) > 1:
        tk = a0.shape[1]
    if shard_out:
        tn = N // N_DEV
    tm = _tile(M, tm or (1024 if ta else 2048))
    tn = _tile(b0.shape[1] if len(b_parts) > 1 else N, tn)
    per_piece = b0.shape[1] // tn
    size = lambda dt: jnp.dtype(dt).itemsize
    row_bytes = len(a_parts) * tm * size(a0.dtype) + len(b_parts) * tn * size(b.arr.dtype if gathered else b0.dtype)
    tile_bytes = tm * tn * (sum(size(e.dtype) for e in extras) + sum(map(size, out_dtypes)))

    def vmem_bytes(k_tile):
        return 2 * (k_tile * row_bytes + tile_bytes) + tm * tn * 4 * (K > k_tile)

    if tk is None:
        tk = next(c for c in (_tile(K, 2048), _tile(K, 1024), _tile(K, 512), _tile(K, 256)) if vmem_bytes(c) <= VMEM_LIMIT - (4 << 20))
    tk = _tile(K, tk)
    nk = K // tk
    ca, cb = (0 if ta else 1), (1 if tb else 0)
    n_a, n_b, n_ex, n_out = len(a_parts), len(b_parts), len(extras) + len(row_extras), len(out_dtypes)
    tokens = [] if after is None else [after]
    n_in = n_a + n_b + n_ex + len(tokens)
    if epilogue is None:
        epilogue = lambda acc: (acc,)

    def body(*refs):
        a_refs, b_refs = refs[:n_a], refs[n_a:n_a + n_b]
        ex_refs = refs[n_a + n_b:n_a + n_b + n_ex]
        o_refs = refs[n_in:n_in + n_out]
        j, k = pl.program_id(1), pl.program_id(2)

        def finish(acc):
            vals = epilogue(acc, *[e[...] for e in ex_refs])
            for o_ref, val in zip(o_refs[:n_out - row_outs], vals):
                o_ref[...] = val.astype(o_ref.dtype)
            for o_ref, val in zip(o_refs[n_out - row_outs:], vals[n_out - row_outs:]):
                first = pl.program_id(0) == 0
                o_ref[...] = jnp.where(first, val, o_ref[...] + val)

        def step(a_ref, b_ref):
            part = _dot(a_ref[...].astype(BF16), b_ref[...].astype(BF16), ca, cb)
            if nk == 1:
                finish(part)
            else:
                acc_ref = refs[-1]

                @pl.when(k == 0)
                def _():
                    acc_ref[...] = part

                @pl.when(k > 0)
                def _():
                    acc_ref[...] += part

                @pl.when(k == nk - 1)
                def _():
                    finish(acc_ref[...])

        if n_a > 1:
            for p in range(n_a):
                pl.when(k == p)(functools.partial(step, a_refs[p], b_refs[0]))
        elif n_b > 1:
            for p in range(n_b):
                pl.when(j // per_piece == p)(functools.partial(step, a_refs[0], b_refs[p]))
        else:
            step(a_refs[0], b_refs[0])

    if n_a > 1:
        a_specs = [pl.BlockSpec((tm, tk), lambda i, j, k: (i, 0))] * n_a
    else:
        a_specs = [pl.BlockSpec((tk, tm), lambda i, j, k: (k, i)) if ta else pl.BlockSpec((tm, tk), lambda i, j, k: (i, k))]
    if gathered:
        b_arrs = [b.arr]
        b_specs = [pl.BlockSpec((None, tn, tk), lambda i, j, k: (k, j, 0)) if tb else pl.BlockSpec((None, tk, tn), lambda i, j, k: (j, k, 0))]
    elif n_b > 1:
        b_arrs = b_parts
        b_specs = [pl.BlockSpec((tk, tn), functools.partial(lambda p, i, j, k: (k, jnp.clip(j - p * per_piece, 0, per_piece - 1)), p))
                   for p in range(n_b)]
    else:
        b_arrs = b_parts
        b_specs = [pl.BlockSpec((tn, tk), lambda i, j, k: (j, k)) if tb else pl.BlockSpec((tk, tn), lambda i, j, k: (k, j))]
    tile = pl.BlockSpec((tm, tn), lambda i, j, k: (i, j))
    if shard_out:
        out_specs = [pl.BlockSpec((None, tm, tn), lambda i, j, k: (j, i, 0))] * n_out
        out_shape = [jax.ShapeDtypeStruct((N_DEV, M, tn), dt) for dt in out_dtypes]
    else:
        assert row_outs == 0 or tn == N
        out_specs = [tile] * (n_out - row_outs) + [pl.BlockSpec((1, tn), lambda i, j, k: (0, j))] * row_outs
        out_shape = [jax.ShapeDtypeStruct((1, N) if t >= n_out - row_outs else (M, N), dt) for t, dt in enumerate(out_dtypes)]
    assert vmem_bytes(tk) <= VMEM_LIMIT - (4 << 20), (name, vmem_bytes(tk))
    outs = pl.pallas_call(
        body,
        name=name,
        grid=(M // tm, N // tn, nk),
        in_specs=a_specs + b_specs + [tile] * len(extras) + [pl.BlockSpec((1, tn), lambda i, j, k: (0, j))] * len(row_extras)
        + [pl.BlockSpec(_TOKEN, lambda i, j, k: (0, 0))] * len(tokens),
        out_specs=out_specs,
        out_shape=out_shape,
        scratch_shapes=[pltpu.VMEM((tm, tn), F32)] if nk > 1 else [],
        compiler_params=_params(("arbitrary",) * 3 if row_outs else ("parallel", "parallel", "arbitrary")),
    )(*a_parts, *b_arrs, *extras, *row_extras, *tokens)
    return outs[0] if n_out == 1 else outs


def _add(acc, res):
    return (acc + res,)


def _norm_grad(dh, x, dres, g):
    r = lax.rsqrt(jnp.mean(x * x, axis=-1, keepdims=True) + EPS)
    xn = x * r
    dxn = dh * g
    dx = r * (dxn - xn * jnp.mean(dxn * xn, axis=-1, keepdims=True)) + dres
    return dx, dx, jnp.sum(dh * xn, axis=0, keepdims=True)


def _add_norm_grad(acc, more, x, dres, g):
    return _norm_grad(acc + more, x, dres, g)


def _add_norm(acc, res, g):
    x = acc + res
    return x, x * lax.rsqrt(jnp.mean(x * x, axis=-1, keepdims=True) + EPS) * g


def _rms_fwd(x, g, *, after=None, name):
    R, C = x.shape
    tm = _tile(R, 256)
    tokens = [] if after is None else [after]

    def body(x_ref, g_ref, *rest):
        xv = x_ref[...]
        r = lax.rsqrt(jnp.mean(xv * xv, axis=-1, keepdims=True) + EPS)
        rest[-1][...] = (xv * r * g_ref[...]).astype(BF16)

    return pl.pallas_call(
        body,
        name=name,
        grid=(R // tm,),
        in_specs=[pl.BlockSpec((tm, C), lambda i: (i, 0)), pl.BlockSpec((1, C), lambda i: (0, 0))]
        + [pl.BlockSpec(_TOKEN, lambda i: (0, 0))] * len(tokens),
        out_specs=pl.BlockSpec((tm, C), lambda i: (i, 0)),
        out_shape=jax.ShapeDtypeStruct((R, C), BF16),
        compiler_params=_params(("parallel",)),
    )(x, g.reshape(1, C), *tokens)


def _rms_bwd(x, g, dh, dres, *, name):
    R, C = x.shape
    tm = _tile(R, 256)

    def body(x_ref, g_ref, dh_ref, dres_ref, dx_ref, dg_ref):
        xv = x_ref[...]
        r = lax.rsqrt(jnp.mean(xv * xv, axis=-1, keepdims=True) + EPS)
        xn = xv * r
        dh_v = dh_ref[...].astype(F32)
        dxn = dh_v * g_ref[...]
        dx_ref[...] = r * (dxn - xn * jnp.mean(dxn * xn, axis=-1, keepdims=True)) + dres_ref[...]
        part = jnp.sum(dh_v * xn, axis=0, keepdims=True)

        @pl.when(pl.program_id(0) == 0)
        def _():
            dg_ref[...] = part

        @pl.when(pl.program_id(0) > 0)
        def _():
            dg_ref[...] += part

    row = pl.BlockSpec((tm, C), lambda i: (i, 0))
    vec = pl.BlockSpec((1, C), lambda i: (0, 0))
    dx, dg = pl.pallas_call(
        body,
        name=name,
        grid=(R // tm,),
        in_specs=[row, vec, row, row],
        out_specs=[row, vec],
        out_shape=[jax.ShapeDtypeStruct((R, C), F32), jax.ShapeDtypeStruct((1, C), F32)],
        compiler_params=_params(("arbitrary",)),
    )(x, g.reshape(1, C), dh, dres)
    return dx, dg.reshape(C)


def _final_loss(x, g, target, *, name):
    R, C = x.shape
    tm = _tile(R, 256)

    def body(x_ref, g_ref, t_ref, loss_ref, dx_ref, dx_bf_ref, dg_ref):
        xv = x_ref[...]
        r = lax.rsqrt(jnp.mean(xv * xv, axis=-1, keepdims=True) + EPS)
        xn = xv * r
        gv = g_ref[...]
        err = xn * gv - t_ref[...]
        lpart = (0.5 / C) * jnp.sum(jnp.sum(err * err, axis=1, keepdims=True), axis=0, keepdims=True)
        dy = err * (1.0 / C)
        dxn = dy * gv
        dx = r * (dxn - xn * jnp.mean(dxn * xn, axis=-1, keepdims=True))
        dx_ref[...] = dx
        dx_bf_ref[...] = dx.astype(BF16)
        gpart = jnp.sum(dy * xn, axis=0, keepdims=True)

        @pl.when(pl.program_id(0) == 0)
        def _():
            loss_ref[...] = lpart
            dg_ref[...] = gpart

        @pl.when(pl.program_id(0) > 0)
        def _():
            loss_ref[...] += lpart
            dg_ref[...] += gpart

    row = pl.BlockSpec((tm, C), lambda i: (i, 0))
    vec = pl.BlockSpec((1, C), lambda i: (0, 0))
    loss, dx, dx_bf, dg = pl.pallas_call(
        body,
        name=name,
        grid=(R // tm,),
        in_specs=[row, vec, row],
        out_specs=[pl.BlockSpec((1, 1), lambda i: (0, 0)), row, row, vec],
        out_shape=[jax.ShapeDtypeStruct((1, 1), F32), jax.ShapeDtypeStruct((R, C), F32), jax.ShapeDtypeStruct((R, C), BF16),
                   jax.ShapeDtypeStruct((1, C), F32)],
        compiler_params=_params(("arbitrary",)),
    )(x, g.reshape(1, C), target)
    return loss, dx, dx_bf, dg.reshape(C)


def _pool_select(lane, vals):
    out = vals[3]
    for gi in (2, 1, 0):
        out = jnp.where(lane < 64 * (gi + 1), vals[gi], out)
    return out


def _pool_diff(a):
    row, lane = _rows(a.shape), _lanes(a.shape)

    def down(v, k):
        return jnp.where(row >= k, pltpu.roll(v, k, 0), 0.0)

    s2 = a + down(a, 1)
    s4 = s2 + down(s2, 2)
    s8 = s4 + down(s4, 4)
    s16 = s8 + down(s8, 8)
    wsum = _pool_select(lane, (s2, s4, s8, s16))
    win = _pool_select(lane, (2, 4, 8, 16))
    cnt = jnp.minimum(row + 1, win).astype(F32)
    return wsum / cnt - a, cnt


def _pool_diff_t(dd, cnt):
    S = dd.shape[0]
    row, lane = _rows(dd.shape), _lanes(dd.shape)

    def up(v, k):
        return jnp.where(row < S - k, pltpu.roll(v, S - k, 0), 0.0)

    e = dd / cnt
    s2 = e + up(e, 1)
    s4 = s2 + up(s2, 2)
    s8 = s4 + up(s4, 4)
    s16 = s8 + up(s8, 8)
    return _pool_select(lane, (s2, s4, s8, s16)) - dd


def _pool_fwd(rest, wbd, scale, *, name):
    S = rest.shape[0]

    def body(a_ref, w_ref, s_ref, o_ref):
        d, _ = _pool_diff(a_ref[...])
        yp = _dot(d.astype(BF16), w_ref[...], 1, 0)
        o_ref[...] = (yp * s_ref[...]).astype(BF16)

    return pl.pallas_call(
        body,
        name=name,
        grid=(1,),
        in_specs=[
            pl.BlockSpec((S, POOL_W), lambda i: (0, OFF_A // POOL_W)),
            pl.BlockSpec((POOL_W, POOL_W), lambda i: (0, 0)),
            pl.BlockSpec((1, POOL_W), lambda i: (0, 0)),
        ],
        out_specs=pl.BlockSpec((S, POOL_W), lambda i: (0, 0)),
        out_shape=jax.ShapeDtypeStruct((S, POOL_W), BF16),
        compiler_params=_params(("arbitrary",)),
    )(rest, wbd, scale.reshape(1, POOL_W))


def _pool_bwd(rest, wbd, wbd_t, scale, dpa, *, name):
    S = rest.shape[0]

    def body(a_ref, w_ref, wt_ref, s_ref, dpa_ref, da_ref, dw_ref, ds_ref):
        d, cnt = _pool_diff(a_ref[...])
        db = d.astype(BF16)
        yp = _dot(db, w_ref[...], 1, 0)
        dpa_v = dpa_ref[...]
        ds_ref[...] = jnp.sum(dpa_v * yp, axis=0, keepdims=True)
        dyp = (dpa_v * s_ref[...]).astype(BF16)
        dw_ref[...] = _dot(db, dyp, 0, 0)
        dd = _dot(dyp, wt_ref[...], 1, 0)
        da_ref[...] = _pool_diff_t(dd, cnt).astype(BF16)

    full = pl.BlockSpec((S, POOL_W), lambda i: (0, 0))
    sq = pl.BlockSpec((POOL_W, POOL_W), lambda i: (0, 0))
    vec = pl.BlockSpec((1, POOL_W), lambda i: (0, 0))
    return pl.pallas_call(
        body,
        name=name,
        grid=(1,),
        in_specs=[pl.BlockSpec((S, POOL_W), lambda i: (0, OFF_A // POOL_W)), sq, sq, vec, full],
        out_specs=[full, sq, vec],
        out_shape=[
            jax.ShapeDtypeStruct((S, POOL_W), BF16),
            jax.ShapeDtypeStruct((POOL_W, POOL_W), F32),
            jax.ShapeDtypeStruct((1, POOL_W), F32),
        ],
        compiler_params=_params(("arbitrary",)),
    )(rest, wbd, wbd_t, scale.reshape(1, POOL_W), dpa)


def _log_sigmoid(z):
    return jnp.minimum(z, 0.0) - jnp.log(1.0 + jnp.exp(-jnp.abs(z)))


_F_SPEC_COL = OFF_F // F_LANES


def _fox_prep(rest, bpad, *, name):
    S = rest.shape[0]

    def body(f_ref, b_ref, o_ref, ot_ref):
        acc = _log_sigmoid(f_ref[...] + b_ref[...])
        row = _rows(acc.shape)
        k = 1
        while k < S:
            acc = acc + jnp.where(row >= k, pltpu.roll(acc, k, 0), 0.0)
            k *= 2
        o_ref[...] = acc
        ot_ref[...] = acc.T

    return pl.pallas_call(
        body,
        name=name,
        grid=(1,),
        in_specs=[pl.BlockSpec((S, F_LANES), lambda i: (0, _F_SPEC_COL)), pl.BlockSpec((1, F_LANES), lambda i: (0, 0))],
        out_specs=[pl.BlockSpec((S, F_LANES), lambda i: (0, 0)), pl.BlockSpec((F_LANES, S), lambda i: (0, 0))],
        out_shape=[jax.ShapeDtypeStruct((S, F_LANES), F32), jax.ShapeDtypeStruct((F_LANES, S), F32)],
        compiler_params=_params(("arbitrary",)),
    )(rest, bpad)


def _fox_post(rest, bpad, dcum, *, name):
    S = rest.shape[0]

    def body(f_ref, b_ref, d_ref, df_ref, db_ref):
        acc = d_ref[...]
        row = _rows(acc.shape)
        k = 1
        while k < S:
            acc = acc + jnp.where(row < S - k, pltpu.roll(acc, S - k, 0), 0.0)
            k *= 2
        df = acc * (1.0 - _sigmoid(f_ref[...] + b_ref[...]))
        df_ref[...] = df.astype(BF16)
        db_ref[...] = jnp.sum(df, axis=0, keepdims=True)

    full = pl.BlockSpec((S, F_LANES), lambda i: (0, 0))
    vec = pl.BlockSpec((1, F_LANES), lambda i: (0, 0))
    return pl.pallas_call(
        body,
        name=name,
        grid=(1,),
        in_specs=[pl.BlockSpec((S, F_LANES), lambda i: (0, _F_SPEC_COL)), vec, full],
        out_specs=[full, vec],
        out_shape=[jax.ShapeDtypeStruct((S, F_LANES), BF16), jax.ShapeDtypeStruct((1, F_LANES), F32)],
        compiler_params=_params(("arbitrary",)),
    )(rest, bpad, dcum)


_FOX_SCALE = FOX_DH ** -0.5
_PAIRS = FOX_H // 2


def _scaled(v):
    return (v.astype(F32) * _FOX_SCALE).astype(BF16)


def _diag_mask(s):
    return jnp.where(_rows(s.shape) >= _lanes(s.shape), s, NEG)


def _fox_fwd(qkv, cum, fk3, *, name):
    S = qkv.shape[0]
    nk, t = fk3.shape[1:]

    def body(q_ref, k_ref, v_ref, cum_ref, fk_ref, o_ref, lse_ref):
        i = pl.program_id(0)
        lane = _lanes((t, 128))
        lo = lane < FOX_DH
        cumv = cum_ref[...]
        qm, fq = [], []
        for h in range(FOX_H):
            qs = _scaled(q_ref[:, 128 * (h // 2):128 * (h // 2 + 1)])
            zero = jnp.zeros_like(qs)
            qm.append(jnp.where(lo, qs, zero) if h % 2 == 0 else jnp.where(lo, zero, qs))
            fq.append(jnp.broadcast_to(cumv[:, h:h + 1], (t, 128)))

        def tile(j, state, masked):
            m, acc, lsum = (list(part) for part in state)
            k0 = pl.multiple_of(j * t, t)
            for hp in range(_PAIRS):
                cols = slice(128 * hp, 128 * (hp + 1))
                kb = k_ref[pl.ds(k0, t), cols]
                vb = v_ref[pl.ds(k0, t), cols]
                one = jnp.ones_like(vb)
                alphas, pvs = [], []
                for h in (2 * hp, 2 * hp + 1):
                    s = _dot(qm[h], kb, 1, 1) + jnp.concatenate([fq[h]] * (t // 128), axis=1) - fk_ref[h, pl.ds(j, 1), :]
                    if masked:
                        s = _diag_mask(s)
                    m_new = jnp.maximum(m[h], jnp.max(s, axis=-1, keepdims=True))
                    p = jnp.exp(s - m_new)
                    alphas.append(jnp.exp(m[h] - m_new))
                    m[h] = m_new
                    pvs.append(_dot(p.astype(BF16), jnp.where(lo, vb, one) if h % 2 == 0 else jnp.where(lo, one, vb), 1, 0))
                acc[hp] = jnp.where(lo, alphas[0], alphas[1]) * acc[hp] + jnp.where(lo, pvs[0], pvs[1])
                lsum[hp] = jnp.where(lo, alphas[1], alphas[0]) * lsum[hp] + jnp.where(lo, pvs[1], pvs[0])
            return tuple(m), tuple(acc), tuple(lsum)

        zeros = (jnp.zeros((t, 128), F32),) * _PAIRS
        state = lax.fori_loop(0, i, functools.partial(tile, masked=False), ((jnp.full((t, 1), NEG, F32),) * FOX_H, zeros, zeros))
        m, acc, lsum = tile(i, state, True)
        for hp in range(_PAIRS):
            o_ref[:, 128 * hp:128 * (hp + 1)] = acc[hp] / pltpu.roll(lsum[hp], FOX_DH, 1)
            lse = [m[2 * hp] + jnp.log(lsum[hp][:, FOX_DH:FOX_DH + 1]), m[2 * hp + 1] + jnp.log(lsum[hp][:, 0:1])]
            lse_ref[hp] = jnp.where(lane == 0, lse[0], jnp.where(lane == 1, lse[1], 0.0))

    whole = lambda col: pl.BlockSpec((S, FOX_W), lambda i: (0, col))
    return pl.pallas_call(
        body,
        name=name,
        grid=(S // t,),
        in_specs=[
            pl.BlockSpec((t, FOX_W), lambda i: (i, 0)), whole(1), whole(2),
            pl.BlockSpec((t, F_LANES), lambda i: (i, 0)),
            pl.BlockSpec((FOX_H, nk, t), lambda i: (0, 0, 0)),
        ],
        out_specs=[pl.BlockSpec((t, FOX_W), lambda i: (i, 0)), pl.BlockSpec((_PAIRS, t, 128), lambda i: (0, i, 0))],
        out_shape=[jax.ShapeDtypeStruct((S, FOX_W), F32), jax.ShapeDtypeStruct((_PAIRS, S, 128), F32)],
        compiler_params=_params(("arbitrary",)),
    )(qkv, qkv, qkv, cum, fk3)


def _fox_bwd(qkv, cum, fk3, o, do, lse, *, name):
    S = qkv.shape[0]
    nk, t = fk3.shape[1:]
    q_at, k_at, v_at = 0, FOX_W, 2 * FOX_W

    def body(qkv_ref, cum_ref, fk_ref, o_ref, do_ref, lse_ref, dq_ref, dk_ref, dv_ref, dfq_ref, dfk_ref,
             qs_sc, ks_sc, bias_sc, delta_sc, dq_sc):
        lane = _lanes((t, 128))
        lo = lane < FOX_DH
        mine = lambda h: lo if h % 2 == 0 else jnp.logical_not(lo)

        def by_head(tile, values):
            for h, val in enumerate(values):
                tile = jnp.where(lane == h, val, tile)
            return tile

        def prep(i, carry):
            r = pl.ds(pl.multiple_of(i * t, t), t)
            qs_sc[r, :] = _scaled(qkv_ref[r, q_at:q_at + FOX_W])
            ks_sc[r, :] = _scaled(qkv_ref[r, k_at:k_at + FOX_W])
            cum_t = cum_ref[r, :]
            for hp in range(_PAIRS):
                cols = slice(128 * hp, 128 * (hp + 1))
                prod = do_ref[r, cols].astype(F32) * o_ref[r, cols]
                for h in (2 * hp, 2 * hp + 1):
                    delta = jnp.sum(jnp.where(mine(h), prod, 0.0), axis=-1, keepdims=True)
                    delta_sc[h, r, :] = jnp.broadcast_to(delta, (t, 128))
                    bias_sc[h, r, :] = jnp.broadcast_to(cum_t[:, h:h + 1] - lse_ref[hp, r, h % 2:h % 2 + 1], (t, 128))
            dfq_ref[r, :] = jnp.zeros((t, 128), F32)
            dq_sc[r, :] = jnp.zeros((t, FOX_W), F32)
            return carry

        lax.fori_loop(0, nk, prep, 0)

        def kv_tile(j, carry):
            kr = pl.ds(pl.multiple_of(j * t, t), t)

            def q_tile(i, acc, masked):
                dk, dv, dfk = list(acc[:_PAIRS]), list(acc[_PAIRS:2 * _PAIRS]), list(acc[2 * _PAIRS:])
                qr = pl.ds(pl.multiple_of(i * t, t), t)
                dq_old, dfq_old = dq_sc[qr, :], dfq_ref[qr, :]
                wide = lambda a: jnp.concatenate([a] * (t // 128), axis=1)
                row_sums, dq_new = [], []
                for hp in range(_PAIRS):
                    cols = slice(128 * hp, 128 * (hp + 1))
                    kb = qkv_ref[kr, k_at + 128 * hp:k_at + 128 * (hp + 1)]
                    vb = qkv_ref[kr, v_at + 128 * hp:v_at + 128 * (hp + 1)]
                    ksb, qsb, dob = ks_sc[kr, cols], qs_sc[qr, cols], do_ref[qr, cols]
                    zero = jnp.zeros_like(qsb)
                    dq_t = jnp.zeros((t, 128), F32)
                    for h in (2 * hp, 2 * hp + 1):
                        qe, doe, ke = (jnp.where(mine(h), a, zero) for a in (qsb, dob, ksb))
                        s = _dot(qe, kb, 1, 1) + wide(bias_sc[h, qr, :]) - fk_ref[h, pl.ds(j, 1), :]
                        if masked:
                            s = _diag_mask(s)
                        p = jnp.exp(s)
                        dv[hp] = dv[hp] + _dot(p.astype(BF16), doe, 0, 0)
                        dp = _dot(doe, vb, 1, 1)
                        ds = p * (dp - wide(delta_sc[h, qr, :]))
                        dsb = ds.astype(BF16)
                        dk[hp] = dk[hp] + _dot(dsb, qe, 0, 0)
                        dq_t = dq_t + _dot(dsb, ke, 1, 0)
                        row_sums.append(jnp.sum(ds, axis=-1, keepdims=True))
                        dfk[h] = dfk[h] - jnp.sum(ds, axis=0, keepdims=True)
                    dq_new.append(dq_old[:, cols] + dq_t)
                for hp in range(_PAIRS):
                    dq_sc[qr, 128 * hp:128 * (hp + 1)] = dq_new[hp]
                dfq_ref[qr, :] = dfq_old + by_head(jnp.zeros((t, 128), F32), row_sums)
                return (*dk, *dv, *dfk)

            init = tuple([jnp.zeros((t, 128), F32)] * (2 * _PAIRS) + [jnp.zeros((1, t), F32)] * FOX_H)
            acc = q_tile(j, init, True)
            acc = lax.fori_loop(j + 1, nk, functools.partial(q_tile, masked=False), acc)
            for hp in range(_PAIRS):
                cols = slice(128 * hp, 128 * (hp + 1))
                dk_ref[kr, cols] = acc[hp].astype(BF16)
                dv_ref[kr, cols] = acc[_PAIRS + hp].astype(BF16)
            for h in range(FOX_H):
                dfk_ref[h, pl.ds(j, 1), :] = acc[2 * _PAIRS + h]
            return carry

        lax.fori_loop(0, nk, kv_tile, 0)
        dq_ref[...] = dq_sc[...].astype(BF16)

    vm = pl.BlockSpec(memory_space=pltpu.VMEM)
    big = jax.ShapeDtypeStruct((S, FOX_W), BF16)
    return pl.pallas_call(
        body,
        name=name,
        in_specs=[vm] * 6,
        out_specs=[vm] * 5,
        out_shape=[big, big, big, jax.ShapeDtypeStruct((S, 128), F32), jax.ShapeDtypeStruct((FOX_H, nk, t), F32)],
        scratch_shapes=[pltpu.VMEM((S, FOX_W), BF16), pltpu.VMEM((S, FOX_W), BF16), pltpu.VMEM((FOX_H, S, 128), F32),
                        pltpu.VMEM((FOX_H, S, 128), F32), pltpu.VMEM((S, FOX_W), F32)],
        compiler_params=pltpu.CompilerParams(vmem_limit_bytes=VMEM_LIMIT),
    )(qkv, cum, fk3, o, do, lse)


def _group_mask(lane, gi):
    return (lane >= 64 * gi) & (lane < 64 * (gi + 1))


_U_COL = OFF_C // SGU_W


def _sgu_fwd(rest, gn, wm, bias, *, name):
    S = rest.shape[0]
    ts = _tile(S, 512)
    nc = ts // SGU_CHUNK

    def body(u_ref, v_ref, g_ref, w_ref, b_ref, o_ref):
        zv = _gelu(v_ref[...])
        vn = zv * lax.rsqrt(jnp.mean(zv * zv, axis=-1, keepdims=True) + EPS) * g_ref[...]
        lane = _lanes((SGU_CHUNK, SGU_W))
        for c in range(nc):
            rows = slice(c * SGU_CHUNK, (c + 1) * SGU_CHUNK)
            vcb = vn[rows].astype(BF16)
            mixed = b_ref[...]
            for gi in range(4):
                mixed = mixed + jnp.where(_group_mask(lane, gi), _dot(w_ref[gi], vcb, 1, 0), 0.0)
            o_ref[rows, :] = (_gelu(u_ref[rows, :]) * mixed).astype(BF16)

    return pl.pallas_call(
        body,
        name=name,
        grid=(S // ts,),
        in_specs=[
            pl.BlockSpec((ts, SGU_W), lambda i: (i, _U_COL)),
            pl.BlockSpec((ts, SGU_W), lambda i: (i, _U_COL + 1)),
            pl.BlockSpec((1, SGU_W), lambda i: (0, 0)),
            pl.BlockSpec((4, SGU_CHUNK, SGU_CHUNK), lambda i: (0, 0, 0)),
            pl.BlockSpec((SGU_CHUNK, SGU_W), lambda i: (0, 0)),
        ],
        out_specs=pl.BlockSpec((ts, SGU_W), lambda i: (i, 0)),
        out_shape=jax.ShapeDtypeStruct((S, SGU_W), BF16),
        compiler_params=_params(("parallel",)),
    )(rest, rest, gn.reshape(1, SGU_W), wm, bias)


def _sgu_bwd(rest, gn, wm, wm_t, bias, dsg, *, name):
    S = rest.shape[0]
    ts = _tile(S, 512)
    nc = ts // SGU_CHUNK

    def body(u_ref, v_ref, g_ref, w_ref, wt_ref, b_ref, dsg_ref, dc_ref, dw_ref, db_ref, dg_ref):
        first = pl.program_id(0) == 0

        @pl.when(first)
        def _():
            dw_ref[...] = jnp.zeros_like(dw_ref)
            db_ref[...] = jnp.zeros_like(db_ref)
            dg_ref[...] = jnp.zeros_like(dg_ref)

        gv = g_ref[...]
        lane = _lanes((SGU_CHUNK, SGU_W))
        for c in range(nc):
            rows = slice(c * SGU_CHUNK, (c + 1) * SGU_CHUNK)
            vpre = v_ref[rows, :]
            upre = u_ref[rows, :]
            zv = _gelu(vpre)
            r = lax.rsqrt(jnp.mean(zv * zv, axis=-1, keepdims=True) + EPS)
            zn = zv * r
            vcb = (zn * gv).astype(BF16)
            mixed = b_ref[...]
            for gi in range(4):
                mixed = mixed + jnp.where(_group_mask(lane, gi), _dot(w_ref[gi], vcb, 1, 0), 0.0)
            zu = _gelu(upre)
            dsg_v = dsg_ref[rows, :]
            dc_ref[rows, :SGU_W] = (dsg_v * mixed * _gelu_grad(upre)).astype(BF16)
            dmixed = dsg_v * zu
            db_ref[...] += dmixed
            dvn = jnp.zeros((SGU_CHUNK, SGU_W), F32)
            for gi in range(4):
                dmg = jnp.where(_group_mask(lane, gi), dmixed, 0.0).astype(BF16)
                dw_ref[gi] += _dot(dmg, vcb, 1, 1)
                dvn = dvn + _dot(wt_ref[gi], dmg, 1, 0)
            dg_ref[...] += jnp.sum(dvn * zn, axis=0, keepdims=True)
            dzn = dvn * gv
            dzv = r * (dzn - zn * jnp.mean(dzn * zn, axis=-1, keepdims=True))
            dc_ref[rows, SGU_W:] = (dzv * _gelu_grad(vpre)).astype(BF16)

    blk = pl.BlockSpec((ts, SGU_W), lambda i: (i, 0))
    vec = pl.BlockSpec((1, SGU_W), lambda i: (0, 0))
    w3 = pl.BlockSpec((4, SGU_CHUNK, SGU_CHUNK), lambda i: (0, 0, 0))
    bsp = pl.BlockSpec((SGU_CHUNK, SGU_W), lambda i: (0, 0))
    return pl.pallas_call(
        body,
        name=name,
        grid=(S // ts,),
        in_specs=[
            pl.BlockSpec((ts, SGU_W), lambda i: (i, _U_COL)),
            pl.BlockSpec((ts, SGU_W), lambda i: (i, _U_COL + 1)),
            vec, w3, w3, bsp, blk,
        ],
        out_specs=[pl.BlockSpec((ts, 2 * SGU_W), lambda i: (i, 0)), w3, bsp, vec],
        out_shape=[
            jax.ShapeDtypeStruct((S, 2 * SGU_W), BF16),
            jax.ShapeDtypeStruct((4, SGU_CHUNK, SGU_CHUNK), F32),
            jax.ShapeDtypeStruct((SGU_CHUNK, SGU_W), F32),
            jax.ShapeDtypeStruct((1, SGU_W), F32),
        ],
        compiler_params=_params(("arbitrary",)),
    )(rest, rest, gn.reshape(1, SGU_W), wm, wm_t, bias, dsg)


_GT = 512
_G0 = OFF_G // _GT


def _gate_specs(tm, col_of):
    specs = [pl.BlockSpec((tm, _GT), functools.partial(lambda k, *ids: (col_of(*ids)[0], _G0 + 2 * k + col_of(*ids)[1]), k)) for k in range(3)]
    specs += [pl.BlockSpec((1, _GT), functools.partial(lambda k, *ids: (0, 2 * k + col_of(*ids)[1]), k)) for k in range(3)]
    return specs


def _merge_fwd(rest, bg, ya, yb, yc, *, name):
    S = rest.shape[0]
    tm = _tile(S, 512)

    def body(g1, g2, g3, b1, b2, b3, ya_ref, yb_ref, yc_ref, o_ref):
        acc = _sigmoid(g1[...] + b1[...]) * ya_ref[...]
        acc = acc + _sigmoid(g2[...] + b2[...]) * yb_ref[...]
        acc = acc + _sigmoid(g3[...] + b3[...]) * yc_ref[...]
        o_ref[...] = acc.astype(BF16)

    blk = pl.BlockSpec((tm, _GT), lambda i, j: (i, j))
    return pl.pallas_call(
        body,
        name=name,
        grid=(S // tm, D // _GT),
        in_specs=_gate_specs(tm, lambda i, j: (i, j)) + [blk, blk, blk],
        out_specs=blk,
        out_shape=jax.ShapeDtypeStruct((S, D), BF16),
        compiler_params=_params(("parallel", "parallel")),
    )(rest, rest, rest, bg, bg, bg, ya, yb, yc)


def _merge_bwd(rest, bg, ya, yb, yc, dm, *, name):
    S = rest.shape[0]
    tm = _tile(S, 512)

    def body(g1, g2, g3, b1, b2, b3, ya_ref, yb_ref, yc_ref, dm_ref, dya, dyb, dyc, dg1, dg2, dg3, db1, db2, db3):
        first = pl.program_id(1) == 0
        dmv = dm_ref[...]
        for g_ref, b_ref, y_ref, dy_ref, dg_ref, db_ref in (
            (g1, b1, ya_ref, dya, dg1, db1), (g2, b2, yb_ref, dyb, dg2, db2), (g3, b3, yc_ref, dyc, dg3, db3)):
            gate = _sigmoid(g_ref[...] + b_ref[...])
            dy_ref[...] = (dmv * gate).astype(BF16)
            dpre = dmv * y_ref[...] * gate * (1.0 - gate)
            dg_ref[...] = dpre.astype(BF16)
            part = jnp.sum(dpre, axis=0, keepdims=True)

            @pl.when(first)
            def _():
                db_ref[...] = part

            @pl.when(jnp.logical_not(first))
            def _():
                db_ref[...] += part

    blk = pl.BlockSpec((tm, _GT), lambda j, i: (i, j))
    vec = pl.BlockSpec((1, _GT), lambda j, i: (0, j))
    big = jax.ShapeDtypeStruct((S, D), BF16)
    small = jax.ShapeDtypeStruct((1, D), F32)
    return pl.pallas_call(
        body,
        name=name,
        grid=(D // _GT, S // tm),
        in_specs=_gate_specs(tm, lambda j, i: (i, j)) + [blk, blk, blk, blk],
        out_specs=[blk] * 6 + [vec] * 3,
        out_shape=[big] * 6 + [small] * 3,
        compiler_params=_params(("parallel", "arbitrary")),
    )(rest, rest, rest, bg, bg, bg, ya, yb, yc, dm)


_X_SCALE = XDH ** -0.5


def _xattn_fwd(xq, kv, *, name):
    S = xq.shape[0]
    M = kv.shape[0]
    tq = _tile(S, 512)

    def body(q_ref, k_ref, v_ref, o_ref):
        s = _dot(q_ref[...], k_ref[...], 1, 1) * _X_SCALE
        e = jnp.exp(s - jnp.max(s, axis=-1, keepdims=True))
        p = e / jnp.sum(e, axis=-1, keepdims=True)
        o_ref[...] = _dot(p.astype(BF16), v_ref[...], 1, 0).astype(BF16)

    return pl.pallas_call(
        body,
        name=name,
        grid=(S // tq, XH),
        in_specs=[
            pl.BlockSpec((tq, XDH), lambda i, h: (i, h)),
            pl.BlockSpec((M, XDH), lambda i, h: (0, h)),
            pl.BlockSpec((M, XDH), lambda i, h: (0, XH + h)),
        ],
        out_specs=pl.BlockSpec((tq, XDH), lambda i, h: (i, h)),
        out_shape=jax.ShapeDtypeStruct((S, D), BF16),
        compiler_params=_params(("parallel", "parallel")),
    )(xq, kv, kv)


def _xattn_bwd(xq, kv, do, *, name):
    S = xq.shape[0]
    M = kv.shape[0]
    tq = _tile(S, 512)

    def body(q_ref, k_ref, v_ref, do_ref, dq_ref, dk_ref, dv_ref):
        qb = q_ref[...]
        kb = k_ref[...]
        dob = do_ref[...]
        s = _dot(qb, kb, 1, 1) * _X_SCALE
        e = jnp.exp(s - jnp.max(s, axis=-1, keepdims=True))
        p = e / jnp.sum(e, axis=-1, keepdims=True)
        dp = _dot(dob, v_ref[...], 1, 1)
        ds = (p * (dp - jnp.sum(p * dp, axis=-1, keepdims=True)) * _X_SCALE).astype(BF16)
        dq_ref[...] = _dot(ds, kb, 1, 0).astype(BF16)
        dk_part = _dot(ds, qb, 0, 0)
        dv_part = _dot(p.astype(BF16), dob, 0, 0)

        @pl.when(pl.program_id(1) == 0)
        def _():
            dk_ref[...] = dk_part
            dv_ref[...] = dv_part

        @pl.when(pl.program_id(1) > 0)
        def _():
            dk_ref[...] += dk_part
            dv_ref[...] += dv_part

    qspec = pl.BlockSpec((tq, XDH), lambda h, i: (i, h))
    kspec = pl.BlockSpec((M, XDH), lambda h, i: (0, h))
    dxq, dxk, dxv = pl.pallas_call(
        body,
        name=name,
        grid=(XH, S // tq),
        in_specs=[qspec, kspec, pl.BlockSpec((M, XDH), lambda h, i: (0, XH + h)), qspec],
        out_specs=[qspec, kspec, kspec],
        out_shape=[jax.ShapeDtypeStruct((S, D), BF16), jax.ShapeDtypeStruct((M, D), F32), jax.ShapeDtypeStruct((M, D), F32)],
        compiler_params=_params(("parallel", "arbitrary")),
    )(xq, kv, kv, do)
    return dxq, jnp.concatenate([dxk, dxv], axis=1)


def _adam_math(w, g, m, v):
    m = ADAM_B1 * m + (1.0 - ADAM_B1) * g
    v = ADAM_B2 * v + (1.0 - ADAM_B2) * (g * g)
    m_hat = m / (1.0 - ADAM_B1 ** ADAM_STEP)
    v_hat = v / (1.0 - ADAM_B2 ** ADAM_STEP)
    delta = -ADAM_LR * (m_hat / (jnp.sqrt(v_hat) + ADAM_EPS) + ADAM_WD * w)
    return delta, m, v


def _adamw_sharded(parts, w, m, v, *, name):
    _, R, C = w.shape
    Cp = parts[0].shape[2]
    tm = _tile(R, 256)
    nr = R // tm

    def body(p0_ref, p1_ref, w_ref, m_ref, v_ref, g_ref, d_ref, mo_ref, vo_ref):
        def update(p_ref):
            g = p_ref[0][:, :C].astype(F32)
            for dev in range(1, N_DEV):
                g = g + p_ref[dev][:, :C].astype(F32)
            delta, mn, vn = _adam_math(w_ref[...], g, m_ref[...], v_ref[...])
            g_ref[...] = g
            d_ref[...] = delta
            mo_ref[...] = mn
            vo_ref[...] = vn

        @pl.when(pl.program_id(0) == 0)
        def _():
            update(p0_ref)

        @pl.when(pl.program_id(0) == 1)
        def _():
            update(p1_ref)

    p0 = pl.BlockSpec((N_DEV, tm, Cp), lambda l, i: (0, i * (1 - l) + (nr - 1) * l, 0))
    p1 = pl.BlockSpec((N_DEV, tm, Cp), lambda l, i: (0, i * l, 0))
    blk = pl.BlockSpec((None, tm, C), lambda l, i: (l, i, 0))
    sds = jax.ShapeDtypeStruct(w.shape, F32)
    return pl.pallas_call(
        body,
        name=name,
        grid=(DEPTH, nr),
        in_specs=[p0, p1, blk, blk, blk],
        out_specs=[blk] * 4,
        out_shape=[sds] * 4,
        compiler_params=_params(("arbitrary", "arbitrary")),
    )(parts[0], parts[1], w, m, v)


def _adamw_small(g, w, m, v, *, name):
    n = len(g)

    def body(*refs):
        g_refs, w_refs, m_refs, v_refs = (refs[k * n:(k + 1) * n] for k in range(4))
        d_out, m_out, v_out = (refs[(4 + k) * n:(5 + k) * n] for k in range(3))
        for t in range(n):
            delta, mn, vn = _adam_math(w_refs[t][...], g_refs[t][...], m_refs[t][...], v_refs[t][...])
            d_out[t][...] = delta
            m_out[t][...] = mn
            v_out[t][...] = vn

    vm = pl.BlockSpec(memory_space=pltpu.VMEM)
    shapes = [jax.ShapeDtypeStruct(a.shape, F32) for a in w]
    outs = pl.pallas_call(
        body,
        name=name,
        in_specs=[vm] * (4 * n),
        out_specs=[vm] * (3 * n),
        out_shape=shapes * 3,
        compiler_params=pltpu.CompilerParams(vmem_limit_bytes=VMEM_LIMIT),
    )(*g, *w, *m, *v)
    return outs[:n], outs[n:2 * n], outs[2 * n:]


def _position():
    return lax.axis_index("x"), lax.axis_index("y"), lax.axis_index("c")


def _dev_index(px, py, pc):
    return 4 * px + 2 * py + pc


_ANY = pl.BlockSpec(memory_space=pl.ANY)


def _peers(x, y, c):
    out = []
    for mask in range(1, N_DEV):
        fx, fy, fc = (mask >> 2) & 1, (mask >> 1) & 1, mask & 1
        out.append((1 - x if fx else x, 1 - y if fy else y, 1 - c if fc else c))
    return out


_HBM = pl.BlockSpec(memory_space=pltpu.HBM)
_SEM = pl.BlockSpec(memory_space=pltpu.SEMAPHORE)


def _own_block_placed(block, like):
    x, y, c = _position()
    return lax.dynamic_update_index_in_dim(lax.empty(like.shape, like.dtype), block, _dev_index(x, y, c), 0)


_COPY_BYTES = 256 << 10
_MAX_PIECES = 8


def _pieces(blocks):
    out = []
    for t, b in enumerate(blocks):
        R, C = b.shape[-2:]
        n = max(1, min(_MAX_PIECES, R * C * jnp.dtype(b.dtype).itemsize // _COPY_BYTES))
        while n > 1 and R % (16 * n):
            n -= 1
        out += [(t, pl.ds(j * (R // n), R // n) if n > 1 else None) for j in range(n)]
    return out


def _cut(block, rows):
    return block if rows is None else block.at[rows]


def _copies(per_piece):
    def mark(fn):
        fn.per_piece = per_piece
        return fn
    return mark


@_copies(N_DEV - 1)
def _plan_exchange(srcs, lands, send_sems, recv_sems, arrivals):
    x, y, c = _position()
    me = _dev_index(x, y, c)
    out = []
    for k, peer in enumerate(_peers(x, y, c)):
        p = _dev_index(*peer)
        for i, (t, rows) in enumerate(_pieces(lands)):
            sems = dict(send_sem=send_sems.at[7 * i + k], recv_sem=recv_sems.at[7 * i + k], device_id=peer, device_id_type=MESH)
            src, dst = (lands[t].at[p], lands[t].at[p]) if arrivals else (srcs[t].at[p], lands[t].at[me])
            out.append(pltpu.make_async_remote_copy(src_ref=_cut(src, rows), dst_ref=_cut(dst, rows), **sems))
    return out


@_copies(N_DEV - 1)
def _plan_broadcast(srcs, lands, send_sems, recv_sems, arrivals):
    x, y, c = _position()
    me = _dev_index(x, y, c)
    out = []
    for k, peer in enumerate(_peers(x, y, c)):
        p = _dev_index(*peer)
        for i, (t, rows) in enumerate(_pieces(lands)):
            sems = dict(send_sem=send_sems.at[7 * i + k], recv_sem=recv_sems.at[7 * i + k], device_id=peer, device_id_type=MESH)
            src, dst = (lands[t].at[p], lands[t].at[p]) if arrivals else (srcs[t], lands[t].at[me])
            out.append(pltpu.make_async_remote_copy(src_ref=_cut(src, rows), dst_ref=_cut(dst, rows), **sems))
    return out


@_copies(4)
def _plan_gather_out(srcs, lands, send_sems, recv_sems, arrivals):
    x, y, c = _position()
    me = _dev_index(x, y, c)
    out = []
    for k, peer in enumerate([(x, y, 1 - c), (1 - x, y, c), (x, 1 - y, c), (1 - x, 1 - y, c)]):
        p = _dev_index(*peer)
        for i, (t, rows) in enumerate(_pieces(lands)):
            sems = dict(send_sem=send_sems.at[4 * i + k], recv_sem=recv_sems.at[4 * i + k], device_id=peer, device_id_type=MESH)
            src, dst = (lands[t].at[p], lands[t].at[p]) if arrivals else (srcs[t], lands[t].at[me])
            out.append(pltpu.make_async_remote_copy(src_ref=_cut(src, rows), dst_ref=_cut(dst, rows), **sems))
    return out


@_copies(3)
def _plan_gather_pass(srcs, lands, send_sems, recv_sems, arrivals):
    x, y, c = _position()
    sibling = (x, y, 1 - c)
    out = []
    for k, chip in enumerate([(1 - x, y), (x, 1 - y), (1 - x, 1 - y)]):
        p = _dev_index(*chip, 1 - c) if arrivals else _dev_index(*chip, c)
        for i, (t, rows) in enumerate(_pieces(lands)):
            sems = dict(send_sem=send_sems.at[3 * i + k], recv_sem=recv_sems.at[3 * i + k], device_id=sibling, device_id_type=MESH)
            block = _cut(lands[t].at[p], rows)
            out.append(pltpu.make_async_remote_copy(src_ref=block, dst_ref=block, **sems))
    return out


def _split_start(plan, srcs, lands, *, after=None, name):
    n_src, n = len(srcs), len(srcs) + len(lands)
    n_sem = plan.per_piece * len(_pieces(lands))
    order = [] if after is None else [after]

    def body(*refs):
        send_sems, recv_sems = refs[n + len(order):n + len(order) + 2]
        token = refs[-1]
        for cp in plan(refs[:n_src], refs[n_src:n], send_sems, recv_sems, arrivals=False):
            cp.start()
        token[...] = jnp.zeros_like(token)

    hbm = lambda a: pltpu.HBM(a.shape, a.dtype)
    outs = pl.pallas_call(
        body,
        name=name,
        in_specs=[_HBM] * n + [_ANY] * len(order),
        out_specs=[_SEM, _SEM] + [_HBM] * n + [pl.BlockSpec(memory_space=pltpu.VMEM)],
        out_shape=[pltpu.SemaphoreType.DMA((n_sem,)), pltpu.SemaphoreType.DMA((n_sem,))] + [hbm(a) for a in (*srcs, *lands)]
        + [jax.ShapeDtypeStruct(_TOKEN, F32)],
        input_output_aliases={i: 2 + i for i in range(n)},
        compiler_params=pltpu.CompilerParams(has_side_effects=pltpu.SideEffectType.DATAFLOW_SIDE_EFFECTING),
    )(*[pltpu.with_memory_space_constraint(a, pltpu.HBM) for a in (*srcs, *lands)], *order)
    return (outs[0], outs[1], outs[2:2 + n_src], outs[2 + n_src:2 + n]), outs[-1]


def _split_wait(plan, state, after, *, name):
    send_sems, recv_sems, srcs, lands = state
    n_src, n = len(srcs), len(srcs) + len(lands)

    def body(*refs):
        send_refs, recv_refs = refs[n:n + 2]
        for cp in plan(refs[:n_src], refs[n_src:n], send_refs, recv_refs, arrivals=False):
            cp.wait_send()
        for cp in plan(refs[:n_src], refs[n_src:n], send_refs, recv_refs, arrivals=True):
            cp.wait_recv()

    hbm = lambda a: pltpu.HBM(a.shape, a.dtype)
    outs = pl.pallas_call(
        body,
        name=name,
        in_specs=[_HBM] * n + [_SEM, _SEM, _ANY],
        out_specs=[_HBM] * n,
        out_shape=[hbm(a) for a in (*srcs, *lands)],
        input_output_aliases={i: i for i in range(n)},
        compiler_params=pltpu.CompilerParams(has_side_effects=pltpu.SideEffectType.DATAFLOW_SIDE_EFFECTING),
    )(*srcs, *lands, send_sems, recv_sems, after)
    return outs[n_src:]


def _sum_blocks(blocks, *, name):
    _, R, C = blocks.shape
    tm = next(R // n for n in (4, 3, 2, 1) if R % (8 * n) == 0)

    def body(b_ref, o_ref):
        g = b_ref[0]
        for dev in range(1, N_DEV):
            g = g + b_ref[dev]
        o_ref[...] = g

    return pl.pallas_call(
        body,
        name=name,
        grid=(R // tm,),
        in_specs=[pl.BlockSpec((N_DEV, tm, C), lambda i: (0, i, 0))],
        out_specs=pl.BlockSpec((tm, C), lambda i: (i, 0)),
        out_shape=jax.ShapeDtypeStruct((R, C), F32),
        compiler_params=_params(("parallel",)),
    )(blocks)


def _block_diag(w):
    out = jnp.zeros((POOL_W, POOL_W), w.dtype)
    for gi in range(4):
        out = out.at[64 * gi:64 * (gi + 1), 64 * gi:64 * (gi + 1)].set(w[gi])
    return out


def _layer_consts(sp, l):
    causal = jnp.tril(jnp.ones((SGU_CHUNK, SGU_CHUNK), F32))
    wm = (sp["sgu_w"][l] * causal[None]).astype(BF16)
    wbd = _block_diag(sp["pool_w"][l]).astype(BF16)
    return dict(
        wbd=wbd, wbd_t=wbd.T, wm=wm, wm_t=wm.transpose(0, 2, 1),
        sgu_bias=jnp.repeat(sp["sgu_b"][l].T, 64, axis=1),
        bpad=jnp.pad(sp["b_forget"][l], (0, F_LANES - FOX_H)).reshape(1, F_LANES),
        bg=sp["b_gate"][l].reshape(1, 3 * D),
    )


def _relu2(acc):
    return acc, jnp.square(jnp.maximum(acc, 0.0))


def _relu2_grad(acc, z):
    return (acc * 2.0 * jnp.maximum(z, 0.0),)


def _layer_fwd(l, x, h, mem, source, sp):
    S = x.shape[0]
    t = _tile(S, 256)
    c = _layer_consts(sp, l)
    n = f"l{l}_"
    W, after = source(l, "begin", x)
    if h is None:
        h, after = _rms_fwd(x, sp["norm_mix_g"][l], after=after, name=n + "norm_mix"), None
    hm = _rms_fwd(mem, sp["norm_mem_g"][l], name=n + "norm_mem")
    more, token = source(l, "normed", hm)
    W.update(more)
    qkv = _mm(h, W["qkv"], out_dtypes=(BF16,), after=after if token is None else token, name=n + "qkv")
    rest = _mm(h, W["rest"], name=n + "rest")
    pa = _pool_fwd(rest, c["wbd"], sp["pool_scale"][l], name=n + "pool")
    cum, cum_t = _fox_prep(rest, c["bpad"], name=n + "fox_prep")
    fk3 = cum_t[:FOX_H].reshape(FOX_H, S // t, t)
    o, lse = _fox_fwd(qkv, cum, fk3, name=n + "fox")
    more, _ = source(l, "attended", o)
    W.update(more)
    sg = _sgu_fwd(rest, sp["sgu_norm_g"][l], c["wm"], c["sgu_bias"], name=n + "sgu")
    more, after = source(l, "mixed", sg)
    W.update(more)
    ya = _mm(pa, W["ba"], out_dtypes=(BF16,), after=after, name=n + "branch_a")
    yb = _mm(o, W["bb"], out_dtypes=(BF16,), name=n + "branch_b")
    yc = _mm(sg, W["bc"], out_dtypes=(BF16,), name=n + "branch_c")
    merged = _merge_fwd(rest, c["bg"], ya, yb, yc, name=n + "merge")
    whole_rows = dict(epilogue=_add_norm, out_dtypes=(F32, BF16), tm=1024, tn=D)
    x1, hx = _mm(merged, W["out"], extras=(x,), row_extras=(sp["norm_xattn_g"][l].reshape(1, D),), name=n + "out", **whole_rows)
    xq = _mm(hx, W["xq"], out_dtypes=(BF16,), name=n + "xq")
    kv = _mm(hm, W["xkv"], out_dtypes=(BF16,), name=n + "xkv")
    o2 = _xattn_fwd(xq, kv, name=n + "xattn")
    x2, hf = _mm(o2, W["xo"], extras=(x1,), row_extras=(sp["norm_ffn_g"][l].reshape(1, D),), name=n + "xo", **whole_rows)
    z, act = _mm(hf, W["ff1"], epilogue=_relu2, out_dtypes=(BF16, BF16), name=n + "ff1")
    _, after = source(l, "expanded", act)
    if l + 1 < DEPTH:
        x3, h_next = _mm(act, W["ff2"], extras=(x2,), row_extras=(sp["norm_mix_g"][l + 1].reshape(1, D),), after=after, name=n + "ff2",
                         **whole_rows)
    else:
        x3, h_next = _mm(act, W["ff2"], extras=(x2,), epilogue=_add, after=after, name=n + "ff2"), None
    saved = dict(x=x, h=h, qkv=qkv, rest=rest, pa=pa, cum=cum, fk3=fk3, o=o, lse=lse, sg=sg, ya=ya, yb=yb, yc=yc,
                 merged=merged, x1=x1, hx=hx, hm=hm, xq=xq, kv=kv, o2=o2, x2=x2, hf=hf, z=z, act=act, c=c)
    return x3, h_next, saved, W


def _layer_bwd(l, dx3, dx3_bf, sv, mem, W, sp, grads_done):
    S = dx3.shape[0]
    c = sv["c"]
    n = f"l{l}b_"
    bf = dict(out_dtypes=(BF16,))
    gw, gs = {}, {}
    gw["ff2"] = _mm(sv["act"], dx3_bf, ta=True, name=n + "dw_ff2", **bf)
    dz = _mm(dx3_bf, W["ff2"], tb=True, extras=(sv["z"],), epilogue=_relu2_grad, name=n + "dz", **bf)
    gw["ff1"] = _mm(sv["hf"], dz, ta=True, shard_out=True, name=n + "dw_ff1", **bf)
    whole_rows = dict(epilogue=_norm_grad, out_dtypes=(F32, BF16, F32), row_outs=1, tm=1024, tn=D)
    gain = lambda key: (sp[key][l].reshape(1, D),)
    dx2, dx2_bf, dg = _mm(dz, W["ff1"], tb=True, extras=(sv["x2"], dx3), row_extras=gain("norm_ffn_g"), name=n + "dhf", **whole_rows)
    gs["norm_ffn_g"] = dg.reshape(D)
    gw["xo"] = _mm(sv["o2"], dx2_bf, ta=True, name=n + "dw_xo", **bf)
    do2 = _mm(dx2_bf, W["xo"], tb=True, name=n + "do2", **bf)
    dxq, dkv = _xattn_bwd(sv["xq"], sv["kv"], do2, name=n + "dxattn")
    gw["xq"] = _mm(sv["hx"], dxq, ta=True, name=n + "dw_xq", **bf)
    gw["xkv"] = _mm(sv["hm"], dkv, ta=True, shard_out=True, name=n + "dw_xkv", **bf)
    dhm = _mm(dkv, W["xkv"], tb=True, name=n + "dhm")
    _, gs["norm_mem_g"] = _rms_bwd(mem, sp["norm_mem_g"][l], dhm, jnp.zeros_like(mem), name=n + "dnorm_mem")
    dx1, dx1_bf, dg = _mm(dxq, W["xq"], tb=True, extras=(sv["x1"], dx2), row_extras=gain("norm_xattn_g"), name=n + "dhx", **whole_rows)
    gs["norm_xattn_g"] = dg.reshape(D)
    after, gw = grads_done(l, gw), {}
    gw["out"] = _mm(sv["merged"], dx1_bf, ta=True, name=n + "dw_out", **bf)
    dm = _mm(dx1_bf, W["out"], tb=True, after=after, name=n + "dmerged")
    dya, dyb, dyc, dg1, dg2, dg3, db1, db2, db3 = _merge_bwd(sv["rest"], c["bg"], sv["ya"], sv["yb"], sv["yc"], dm, name=n + "dmerge")
    gs["b_gate"] = jnp.concatenate([db1, db2, db3], axis=1).reshape(3 * D)
    gw["ba"] = _mm(sv["pa"], dya, ta=True, shard_out=True, name=n + "dw_ba", **bf)
    gw["bb"] = _mm(sv["o"], dyb, ta=True, shard_out=True, name=n + "dw_bb", **bf)
    gw["bc"] = _mm(sv["sg"], dyc, ta=True, shard_out=True, name=n + "dw_bc", **bf)
    after, gw = grads_done(l, gw), {}
    dpa = _mm(dya, W["ba"], tb=True, name=n + "dpa")
    do = _mm(dyb, W["bb"], tb=True, after=after, name=n + "do", **bf)
    dsg = _mm(dyc, W["bc"], tb=True, name=n + "dsg")
    da, dwbd, dscale = _pool_bwd(sv["rest"], c["wbd"], c["wbd_t"], sp["pool_scale"][l], dpa, name=n + "dpool")
    gs["pool_w"] = jnp.stack([dwbd[64 * gi:64 * (gi + 1), 64 * gi:64 * (gi + 1)] for gi in range(4)])
    gs["pool_scale"] = dscale.reshape(POOL_W)
    dq, dk, dv, dfq, dfk = _fox_bwd(sv["qkv"], sv["cum"], sv["fk3"], sv["o"], do, sv["lse"], name=n + "dfox")
    dcum = dfq + jnp.pad(dfk.reshape(FOX_H, S).T, ((0, 0), (0, F_LANES - FOX_H)))
    df, dbf = _fox_post(sv["rest"], c["bpad"], dcum, name=n + "dfox_post")
    gs["b_forget"] = dbf[0, :FOX_H]
    dc, dwm, dbias, dgn = _sgu_bwd(sv["rest"], sp["sgu_norm_g"][l], c["wm"], c["wm_t"], c["sgu_bias"], dsg, name=n + "dsgu")
    gs["sgu_w"] = dwm * jnp.tril(jnp.ones((SGU_CHUNK, SGU_CHUNK), F32))[None]
    gs["sgu_b"] = dbias.reshape(SGU_CHUNK, 4, 64).sum(axis=2).T
    gs["sgu_norm_g"] = dgn.reshape(SGU_W)
    dqkv = [dq, dk, dv]
    drest = [jnp.concatenate([da, df, jnp.zeros((S, OFF_C - OFF_F - F_LANES), BF16), dc], axis=1), dg1, dg2, dg3]
    gw["qkv"] = _mm(sv["h"], dqkv, ta=True, name=n + "dw_qkv", **bf)
    gw["rest"] = _mm(sv["h"], drest, ta=True, name=n + "dw_rest", **bf)
    after = grads_done(l, gw)
    dh = _mm(dqkv, W["qkv"], tb=True, after=after, name=n + "dh_qkv")
    dx, dx_bf, dg = _mm(drest, W["rest"], tb=True, extras=(dh, sv["x"], dx1), row_extras=gain("norm_mix_g"), name=n + "dh",
                        **{**whole_rows, "epilogue": _add_norm_grad, "tm": 512})
    gs["norm_mix_g"] = dg.reshape(D)
    return dx, dx_bf, gs


def _local_step(x, mem, target, sp, source, grads_done):
    saved, Ws, h = [], [], None
    for l in range(DEPTH):
        x, h, sv, W = _layer_fwd(l, x, h, mem, source, sp)
        saved.append(sv)
        Ws.append(W)
    loss, dx, dx_bf, dgf = _final_loss(x, sp["final_norm_g"], target, name="final_loss")
    gss = [None] * DEPTH
    for l in reversed(range(DEPTH)):
        dx, dx_bf, gss[l] = _layer_bwd(l, dx, dx_bf, saved[l], mem, Ws[l], sp, grads_done)
    small = {k: jnp.stack([gss[l][k] for l in range(DEPTH)]) for k in gss[0]}
    small["final_norm_g"] = dgf
    return loss, dx, small


_SMALL = ["norm_mix_g", "b_forget", "pool_w", "pool_scale", "sgu_norm_g", "sgu_w", "sgu_b", "b_gate", "norm_xattn_g",
          "norm_mem_g", "norm_ffn_g", "final_norm_g"]
_COL = {"w_branch_a": "ba", "w_branch_b": "bb", "w_branch_c": "bc", "w_xkv": "xkv", "w_ff1": "ff1"}
_ROW = {"w_out": "out", "w_xq": "xq", "w_xo": "xo", "w_ff2": "ff2"}
_BIG = ["w_in", "w_branch_a", "w_branch_b", "w_branch_c", "w_out", "w_xq", "w_xkv", "w_xo", "w_ff1", "w_ff2"]
_PACK_LANES = 128


def _as_rows(a):
    return a.reshape(-1, a.shape[-1])


def _pack(tensors):
    rows = []
    for a in tensors:
        flat = a.reshape(-1)
        flat = jnp.pad(flat, (0, (-flat.shape[0]) % (8 * _PACK_LANES)))
        rows.append(flat.reshape(-1, _PACK_LANES))
    n_rows = sum(r.shape[0] for r in rows)
    rows.append(jnp.zeros(((-n_rows) % (8 * N_DEV), _PACK_LANES), F32))
    return jnp.concatenate(rows, axis=0)


def _unpack(packed, like):
    out, r = [], 0
    for a in like:
        size = math.prod(a.shape)
        nr = 8 * (-(-size // (8 * _PACK_LANES)))
        out.append(packed[r:r + nr].reshape(-1)[:size].reshape(a.shape))
        r += nr
    return out


_SHARD_IN = N_IN // N_DEV
_SHARD_IN_PAD = -(-_SHARD_IN // 128) * 128


def _columns(pieces, start, stop):
    out, at = [], 0
    for p in pieces:
        lo, hi = max(start, at), min(stop, at + p.shape[1])
        if lo < hi:
            out.append(p[:, lo - at:hi - at])
        at += p.shape[1]
    return out


def _split_w_in(blocks):
    K = blocks[0].shape[0]
    pad = jnp.zeros((K, OFF_C - OFF_F - FOX_H), blocks[0].dtype)
    cols = functools.partial(_columns, blocks)
    rest = jnp.concatenate(cols(0, R_OFF_Q) + cols(R_OFF_F, R_OFF_C) + [pad] + cols(R_OFF_C, N_IN), axis=1)
    return jnp.concatenate(cols(R_OFF_Q, R_OFF_F), axis=1), rest


def _join_w_in(qkv, rest):
    in_order = [rest[:, :R_OFF_Q], qkv, rest[:, OFF_F:OFF_F + FOX_H], rest[:, OFF_C:]]
    pad = jnp.zeros((qkv.shape[0], _SHARD_IN_PAD - _SHARD_IN), qkv.dtype)
    return jnp.stack([jnp.concatenate(_columns(in_order, _SHARD_IN * d, _SHARD_IN * (d + 1)) + [pad], axis=1) for d in range(N_DEV)])


_FIRST = ["w_in"]
_LATER = [k for k in _BIG if k not in _FIRST]


def _layer_weights(gathered):
    W = {}
    if "w_in" in gathered:
        W.update(zip(("qkv", "rest"), _split_w_in([gathered["w_in"][d][:, :_SHARD_IN] for d in range(N_DEV)])))
    for name, key in _COL.items():
        if name in gathered:
            W[key] = _Gathered(gathered[name])
    for name, key in _ROW.items():
        if name in gathered:
            W[key] = gathered[name].reshape(-1, gathered[name].shape[-1])
    return W


def _grad_blocks(gw):
    parts = {}
    if "qkv" in gw:
        parts["w_in"] = _join_w_in(gw["qkv"], gw["rest"])
    for name, key in _COL.items():
        if key in gw:
            parts[name] = gw[key]
    for name, key in _ROW.items():
        if key in gw:
            parts[name] = gw[key].reshape(N_DEV, -1, gw[key].shape[-1])
    return parts


def kernel(x, mem, norm_mix_g, w_in, b_forget, pool_w, pool_scale, sgu_norm_g, sgu_w, sgu_b, w_branch_a, w_branch_b, w_branch_c, b_gate, w_out, norm_xattn_g, norm_mem_g, w_xq, w_xkv, w_xo, norm_ffn_g, w_ff1, w_ff2, final_norm_g, loss_target, m_norm_mix_g, m_w_in, m_b_forget, m_pool_w, m_pool_scale, m_sgu_norm_g, m_sgu_w, m_sgu_b, m_w_branch_a, m_w_branch_b, m_w_branch_c, m_b_gate, m_w_out, m_norm_xattn_g, m_norm_mem_g, m_w_xq, m_w_xkv, m_w_xo, m_norm_ffn_g, m_w_ff1, m_w_ff2, m_final_norm_g, v_norm_mix_g, v_w_in, v_b_forget, v_pool_w, v_pool_scale, v_sgu_norm_g, v_sgu_w, v_sgu_b, v_w_branch_a, v_w_branch_b, v_w_branch_c, v_b_gate, v_w_out, v_norm_xattn_g, v_norm_mem_g, v_w_xq, v_w_xkv, v_w_xo, v_norm_ffn_g, v_w_ff1, v_w_ff2, v_final_norm_g):
    names = ["norm_mix_g", "w_in", "b_forget", "pool_w", "pool_scale", "sgu_norm_g", "sgu_w", "sgu_b", "w_branch_a", "w_branch_b",
             "w_branch_c", "b_gate", "w_out", "norm_xattn_g", "norm_mem_g", "w_xq", "w_xkv", "w_xo", "norm_ffn_g", "w_ff1", "w_ff2",
             "final_norm_g"]
    w = dict(zip(names, [norm_mix_g, w_in, b_forget, pool_w, pool_scale, sgu_norm_g, sgu_w, sgu_b, w_branch_a, w_branch_b, w_branch_c,
                         b_gate, w_out, norm_xattn_g, norm_mem_g, w_xq, w_xkv, w_xo, norm_ffn_g, w_ff1, w_ff2, final_norm_g]))
    m = dict(zip(names, [m_norm_mix_g, m_w_in, m_b_forget, m_pool_w, m_pool_scale, m_sgu_norm_g, m_sgu_w, m_sgu_b, m_w_branch_a,
                         m_w_branch_b, m_w_branch_c, m_b_gate, m_w_out, m_norm_xattn_g, m_norm_mem_g, m_w_xq, m_w_xkv, m_w_xo,
                         m_norm_ffn_g, m_w_ff1, m_w_ff2, m_final_norm_g]))
    v = dict(zip(names, [v_norm_mix_g, v_w_in, v_b_forget, v_pool_w, v_pool_scale, v_sgu_norm_g, v_sgu_w, v_sgu_b, v_w_branch_a,
                         v_w_branch_b, v_w_branch_c, v_b_gate, v_w_out, v_norm_xattn_g, v_norm_mem_g, v_w_xq, v_w_xkv, v_w_xo,
                         v_norm_ffn_g, v_w_ff1, v_w_ff2, v_final_norm_g]))

    sp = {k: w[k] for k in _SMALL}
    shards = [{k: w[k][l].astype(BF16) for k in _BIG} for l in range(DEPTH)]
    for sh in shards:
        sh["w_in"] = jnp.pad(sh["w_in"], ((0, 0), (0, _SHARD_IN_PAD - _SHARD_IN)))
    me = _dev_index(*_position())

    def gather_out(l, keys, name, after=None):
        srcs = [shards[l][k] for k in keys]
        lands = [_own_block_placed(a, jax.ShapeDtypeStruct((N_DEV, *a.shape), a.dtype)) for a in srcs]
        state, token = _split_start(_plan_gather_out, srcs, lands, after=after, name=name + "_out_start")
        return (keys, name, state), token

    def gather_pass(job, value):
        keys, name, state = job
        lands = _split_wait(_plan_gather_out, state, value, name=name + "_out_wait")
        state, token = _split_start(_plan_gather_pass, [], lands, name=name + "_pass_start")
        return (keys, name, state), token, lands[0]

    def gather_end(job, value):
        keys, name, state = job
        return _layer_weights(dict(zip(keys, _split_wait(_plan_gather_pass, state, value, name=name + "_pass_wait"))))

    jobs = {}

    def source(l, point, value):
        if (l, point) == (0, "begin"):
            jobs["l0_first"], token = gather_out(0, _FIRST, "gather_l0_first")
            return {}, token
        if (l, point) == (0, "normed"):
            jobs["l0_first"], token, arrived = gather_pass(jobs["l0_first"], value)
            jobs["l0"], _ = gather_out(0, _LATER, "gather_l0", after=arrived)
            return gather_end(jobs.pop("l0_first"), token), None
        if (l, point) == (0, "attended"):
            jobs["l0"], _, arrived = gather_pass(jobs["l0"], value)
            jobs["l1_first"], token = gather_out(1, _FIRST, "gather_l1_first", after=arrived)
            jobs["l1"], jobs["token"] = gather_out(1, _LATER, "gather_l1", after=token)
            return {}, None
        if (l, point) == (0, "mixed"):
            return gather_end(jobs.pop("l0"), value), jobs.pop("token")
        if (l, point) == (0, "expanded"):
            jobs["l1_first"], token, _ = gather_pass(jobs["l1_first"], value)
            return {}, token
        if (l, point) == (1, "begin"):
            W = gather_end(jobs.pop("l1_first"), value)
            jobs["l1"], token, _ = gather_pass(jobs["l1"], value)
            return W, token
        if (l, point) == (1, "mixed"):
            return gather_end(jobs.pop("l1"), value), None
        return {}, None

    received = [{} for _ in range(DEPTH)]
    travelling = []

    def grads_done(l, gw):
        blocks = _grad_blocks(gw)
        keys = [k for k in _BIG if k in blocks]
        parts = [blocks[k] for k in keys]
        group = f"exchange_grads_l{l}_" + ("in" if "w_in" in blocks else "merge" if "w_out" in blocks else "mlp")
        lands = [_own_block_placed(lax.dynamic_index_in_dim(p, me, 0, keepdims=False), p) for p in parts]
        state, token = _split_start(_plan_exchange, parts, lands, name=group + "_start")
        travelling.append((l, keys, state, group + "_wait"))
        return token

    loss, dx, small = _local_step(x[0], mem[0], loss_target[0], sp, source, grads_done)
    grads, deltas, new_m, new_v = {}, {}, {}, {}
    like = [loss] + [w[k] for k in _SMALL]
    packed = _pack([loss] + [small[k] for k in _SMALL])
    eighths = packed.reshape(N_DEV, -1, _PACK_LANES)
    own = lambda a: _own_block_placed(lax.dynamic_index_in_dim(a, me, 0, keepdims=False) if a.ndim == 3 else a, eighths)
    scatter, done = _split_start(_plan_exchange, [eighths], [own(eighths)], after=dx, name="small_grads_scatter_start")

    def reduce_small(after):
        mine = _sum_blocks(_split_wait(_plan_exchange, scatter, after, name="small_grads_scatter_wait")[0], name="small_grads_sum")
        return _split_start(_plan_broadcast, [mine], [own(mine)], name="small_grads_gather_start")

    def update_small(state, after):
        total = _split_wait(_plan_broadcast, state, after, name="small_grads_gather_wait")[0].reshape(packed.shape)
        loss_sum, *g_small = _unpack(total, like)
        rows = lambda d: [_as_rows(d[k]) for k in _SMALL]
        outs = _adamw_small([_as_rows(g) for g in g_small], rows(w), rows(m), rows(v), name="adamw_small")
        grads.update(zip(_SMALL, g_small))
        for dst, vals in zip((deltas, new_m, new_v), outs):
            dst.update({k: a.reshape(w[k].shape) for k, a in zip(_SMALL, vals)})
        return loss_sum[0, 0], outs[0][0]

    groups = list(dict.fromkeys(tuple(keys) for _, keys, _, _ in travelling))
    for n_done, group_keys in enumerate(groups):
        if n_done == 1:
            gather, _ = reduce_small(done)
        if n_done == len(groups) - 1:
            loss, done = update_small(gather, done)
        for l, keys, state, wait_name in travelling:
            if tuple(keys) == group_keys:
                received[l].update(zip(keys, _split_wait(_plan_exchange, state, done, name=wait_name)))
        for k in group_keys:
            outs = _adamw_sharded([received[l][k] for l in range(DEPTH)], w[k], m[k], v[k], name="adamw_" + k)
            grads[k], deltas[k], new_m[k], new_v[k] = outs
        done = grads[group_keys[-1]]

    return (loss, dx[None], *[grads[k] for k in names], *[deltas[k] for k in names], *[new_m[k] for k in names],
            *[new_v[k] for k in names])
```

```python
import functools
import math

import jax
import jax.numpy as jnp
from jax import lax
from jax.experimental import pallas as pl
from jax.experimental.pallas import tpu as pltpu

F32 = jnp.float32
BF16 = jnp.bfloat16
MESH = pl.DeviceIdType.MESH

N_DEV = 8
D = 1024
DEPTH = 2
EPS = 1e-6
NEG = -1e30
POOL_W = 256
FOX_H = 8
FOX_DH = 64
FOX_W = 512
SGU_W = 256
SGU_CHUNK = 128
XH = 4
XDH = 256
N_IN = 5384
R_OFF_Q, R_OFF_F, R_OFF_C = 256, 1792, 1800
QKV_W = 3 * FOX_W
OFF_A, OFF_F, OFF_C, OFF_G, REST_W = 0, 256, 512, 1024, 4096
F_LANES = 128

ADAM_LR = 0.001
ADAM_B1 = 0.9
ADAM_B2 = 0.999
ADAM_EPS = 1e-08
ADAM_WD = 0.01
ADAM_STEP = 10

VMEM_LIMIT = 56 * 1024 * 1024


def _tile(n, pref):
    t = min(n, pref)
    while n % t:
        t -= 128
    assert t > 0, (n, pref)
    return t


def _params(sem=None):
    return pltpu.CompilerParams(dimension_semantics=sem, vmem_limit_bytes=VMEM_LIMIT)


def _dot(a, b, ca, cb):
    return lax.dot_general(a, b, (((ca,), (cb,)), ((), ())), preferred_element_type=F32)


def _sigmoid(z):
    return 1.0 / (1.0 + jnp.exp(-z))


_GELU_K = math.sqrt(2.0 / math.pi)
_GELU_C = 0.044715


def _gelu(x):
    return 0.5 * x * (1.0 + jnp.tanh(_GELU_K * (x + _GELU_C * x * x * x)))


def _gelu_grad(x):
    t = jnp.tanh(_GELU_K * (x + _GELU_C * x * x * x))
    return 0.5 * (1.0 + t) + 0.5 * x * (1.0 - t * t) * _GELU_K * (1.0 + 3.0 * _GELU_C * x * x)


def _rows(shape):
    return lax.broadcasted_iota(jnp.int32, shape, 0)


def _lanes(shape):
    return lax.broadcasted_iota(jnp.int32, shape, 1)


class _Gathered:
    def __init__(self, arr):
        self.arr = arr
        self.shape = (arr.shape[1], N_DEV * arr.shape[2])


_TOKEN = (8, 128)


def _mm(a, b, *, ta=False, tb=False, extras=(), row_extras=(), epilogue=None, out_dtypes=(F32,), row_outs=0, shard_out=False, after=None,
        tm=None, tn=512, tk=None, name):
    a_parts = list(a) if isinstance(a, (list, tuple)) else [a]
    b_parts = list(b) if isinstance(b, (list, tuple)) else [b]
    gathered = isinstance(b, _Gathered)
    assert (len(a_parts) == 1 or not ta) and (len(b_parts) == 1 or not tb) and min(len(a_parts), len(b_parts)) == 1
    a0, b0 = a_parts[0], b_parts[0]
    M, K = (a0.shape[1], a0.shape[0]) if ta else (a0.shape[0], a0.shape[1] * len(a_parts))
    N, Kb = b0.shape if tb else (b0.shape[1] * len(b_parts), b0.shape[0])
    assert Kb == K, (a0.shape, b0.shape, ta, tb)
    if gathered:
        if tb:
            tk = b.arr.shape[2]
        else:
            tn = b.arr.shape[2]
    if len(a_parts) > 1:
        tk = a0.shape[1]
    if shard_out:
        tn = N // N_DEV
    tm = _tile(M, tm or (1024 if ta else 2048))
    tn = _tile(b0.shape[1] if len(b_parts) > 1 else N, tn)
    per_piece = b0.shape[1] // tn
    size = lambda dt: jnp.dtype(dt).itemsize
    row_bytes = len(a_parts) * tm * size(a0.dtype) + len(b_parts) * tn * size(b.arr.dtype if gathered else b0.dtype)
    tile_bytes = tm * tn * (sum(size(e.dtype) for e in extras) + sum(map(size, out_dtypes)))

    def vmem_bytes(k_tile):
        return 2 * (k_tile * row_bytes + tile_bytes) + tm * tn * 4 * (K > k_tile)

    if tk is None:
        tk = next(c for c in (_tile(K, 2048), _tile(K, 1024), _tile(K, 512), _tile(K, 256)) if vmem_bytes(c) <= VMEM_LIMIT - (4 << 20))
    tk = _tile(K, tk)
    nk = K // tk
    ca, cb = (0 if ta else 1), (1 if tb else 0)
    n_a, n_b, n_ex, n_out = len(a_parts), len(b_parts), len(extras) + len(row_extras), len(out_dtypes)
    tokens = [] if after is None else [after]
    n_in = n_a + n_b + n_ex + len(tokens)
    if epilogue is None:
        epilogue = lambda acc: (acc,)

    def body(*refs):
        a_refs, b_refs = refs[:n_a], refs[n_a:n_a + n_b]
        ex_refs = refs[n_a + n_b:n_a + n_b + n_ex]
        o_refs = refs[n_in:n_in + n_out]
        j, k = pl.program_id(1), pl.program_id(2)

        def finish(acc):
            vals = epilogue(acc, *[e[...] for e in ex_refs])
            for o_ref, val in zip(o_refs[:n_out - row_outs], vals):
                o_ref[...] = val.astype(o_ref.dtype)
            for o_ref, val in zip(o_refs[n_out - row_outs:], vals[n_out - row_outs:]):
                first = pl.program_id(0) == 0
                o_ref[...] = jnp.where(first, val, o_ref[...] + val)

        def step(a_ref, b_ref):
            part = _dot(a_ref[...].astype(BF16), b_ref[...].astype(BF16), ca, cb)
            if nk == 1:
                finish(part)
            else:
                acc_ref = refs[-1]

                @pl.when(k == 0)
                def _():
                    acc_ref[...] = part

                @pl.when(k > 0)
                def _():
                    acc_ref[...] += part

                @pl.when(k == nk - 1)
                def _():
                    finish(acc_ref[...])

        if n_a > 1:
            for p in range(n_a):
                pl.when(k == p)(functools.partial(step, a_refs[p], b_refs[0]))
        elif n_b > 1:
            for p in range(n_b):
                pl.when(j // per_piece == p)(functools.partial(step, a_refs[0], b_refs[p]))
        else:
            step(a_refs[0], b_refs[0])

    if n_a > 1:
        a_specs = [pl.BlockSpec((tm, tk), lambda i, j, k: (i, 0))] * n_a
    else:
        a_specs = [pl.BlockSpec((tk, tm), lambda i, j, k: (k, i)) if ta else pl.BlockSpec((tm, tk), lambda i, j, k: (i, k))]
    if gathered:
        b_arrs = [b.arr]
        b_specs = [pl.BlockSpec((None, tn, tk), lambda i, j, k: (k, j, 0)) if tb else pl.BlockSpec((None, tk, tn), lambda i, j, k: (j, k, 0))]
    elif n_b > 1:
        b_arrs = b_parts
        b_specs = [pl.BlockSpec((tk, tn), functools.partial(lambda p, i, j, k: (k, jnp.clip(j - p * per_piece, 0, per_piece - 1)), p))
                   for p in range(n_b)]
    else:
        b_arrs = b_parts
        b_specs = [pl.BlockSpec((tn, tk), lambda i, j, k: (j, k)) if tb else pl.BlockSpec((tk, tn), lambda i, j, k: (k, j))]
    tile = pl.BlockSpec((tm, tn), lambda i, j, k: (i, j))
    if shard_out:
        out_specs = [pl.BlockSpec((None, tm, tn), lambda i, j, k: (j, i, 0))] * n_out
        out_shape = [jax.ShapeDtypeStruct((N_DEV, M, tn), dt) for dt in out_dtypes]
    else:
        assert row_outs == 0 or tn == N
        out_specs = [tile] * (n_out - row_outs) + [pl.BlockSpec((1, tn), lambda i, j, k: (0, j))] * row_outs
        out_shape = [jax.ShapeDtypeStruct((1, N) if t >= n_out - row_outs else (M, N), dt) for t, dt in enumerate(out_dtypes)]
    assert vmem_bytes(tk) <= VMEM_LIMIT - (4 << 20), (name, vmem_bytes(tk))
    outs = pl.pallas_call(
        body,
        name=name,
        grid=(M // tm, N // tn, nk),
        in_specs=a_specs + b_specs + [tile] * len(extras) + [pl.BlockSpec((1, tn), lambda i, j, k: (0, j))] * len(row_extras)
        + [pl.BlockSpec(_TOKEN, lambda i, j, k: (0, 0))] * len(tokens),
        out_specs=out_specs,
        out_shape=out_shape,
        scratch_shapes=[pltpu.VMEM((tm, tn), F32)] if nk > 1 else [],
        compiler_params=_params(("arbitrary",) * 3 if row_outs else ("parallel", "parallel", "arbitrary")),
    )(*a_parts, *b_arrs, *extras, *row_extras, *tokens)
    return outs[0] if n_out == 1 else outs


def _add(acc, res):
    return (acc + res,)


def _norm_grad(dh, x, dres, g):
    r = lax.rsqrt(jnp.mean(x * x, axis=-1, keepdims=True) + EPS)
    xn = x * r
    dxn = dh * g
    dx = r * (dxn - xn * jnp.mean(dxn * xn, axis=-1, keepdims=True)) + dres
    return dx, dx, jnp.sum(dh * xn, axis=0, keepdims=True)


def _add_norm_grad(acc, more, x, dres, g):
    return _norm_grad(acc + more, x, dres, g)


def _add_norm(acc, res, g):
    x = acc + res
    return x, x * lax.rsqrt(jnp.mean(x * x, axis=-1, keepdims=True) + EPS) * g


def _rms_fwd(x, g, *, after=None, name):
    R, C = x.shape
    tm = _tile(R, 256)
    tokens = [] if after is None else [after]

    def body(x_ref, g_ref, *rest):
        xv = x_ref[...]
        r = lax.rsqrt(jnp.mean(xv * xv, axis=-1, keepdims=True) + EPS)
        rest[-1][...] = (xv * r * g_ref[...]).astype(BF16)

    return pl.pallas_call(
        body,
        name=name,
        grid=(R // tm,),
        in_specs=[pl.BlockSpec((tm, C), lambda i: (i, 0)), pl.BlockSpec((1, C), lambda i: (0, 0))]
        + [pl.BlockSpec(_TOKEN, lambda i: (0, 0))] * len(tokens),
        out_specs=pl.BlockSpec((tm, C), lambda i: (i, 0)),
        out_shape=jax.ShapeDtypeStruct((R, C), BF16),
        compiler_params=_params(("parallel",)),
    )(x, g.reshape(1, C), *tokens)


def _rms_bwd(x, g, dh, dres, *, name):
    R, C = x.shape
    tm = _tile(R, 256)

    def body(x_ref, g_ref, dh_ref, dres_ref, dx_ref, dg_ref):
        xv = x_ref[...]
        r = lax.rsqrt(jnp.mean(xv * xv, axis=-1, keepdims=True) + EPS)
        xn = xv * r
        dh_v = dh_ref[...].astype(F32)
        dxn = dh_v * g_ref[...]
        dx_ref[...] = r * (dxn - xn * jnp.mean(dxn * xn, axis=-1, keepdims=True)) + dres_ref[...]
        part = jnp.sum(dh_v * xn, axis=0, keepdims=True)

        @pl.when(pl.program_id(0) == 0)
        def _():
            dg_ref[...] = part

        @pl.when(pl.program_id(0) > 0)
        def _():
            dg_ref[...] += part

    row = pl.BlockSpec((tm, C), lambda i: (i, 0))
    vec = pl.BlockSpec((1, C), lambda i: (0, 0))
    dx, dg = pl.pallas_call(
        body,
        name=name,
        grid=(R // tm,),
        in_specs=[row, vec, row, row],
        out_specs=[row, vec],
        out_shape=[jax.ShapeDtypeStruct((R, C), F32), jax.ShapeDtypeStruct((1, C), F32)],
        compiler_params=_params(("arbitrary",)),
    )(x, g.reshape(1, C), dh, dres)
    return dx, dg.reshape(C)


def _final_loss(x, g, target, *, name):
    R, C = x.shape
    tm = _tile(R, 256)

    def body(x_ref, g_ref, t_ref, loss_ref, dx_ref, dx_bf_ref, dg_ref):
        xv = x_ref[...]
        r = lax.rsqrt(jnp.mean(xv * xv, axis=-1, keepdims=True) + EPS)
        xn = xv * r
        gv = g_ref[...]
        err = xn * gv - t_ref[...]
        lpart = (0.5 / C) * jnp.sum(jnp.sum(err * err, axis=1, keepdims=True), axis=0, keepdims=True)
        dy = err * (1.0 / C)
        dxn = dy * gv
        dx = r * (dxn - xn * jnp.mean(dxn * xn, axis=-1, keepdims=True))
        dx_ref[...] = dx
        dx_bf_ref[...] = dx.astype(BF16)
        gpart = jnp.sum(dy * xn, axis=0, keepdims=True)

        @pl.when(pl.program_id(0) == 0)
        def _():
            loss_ref[...] = lpart
            dg_ref[...] = gpart

        @pl.when(pl.program_id(0) > 0)
        def _():
            loss_ref[...] += lpart
            dg_ref[...] += gpart

    row = pl.BlockSpec((tm, C), lambda i: (i, 0))
    vec = pl.BlockSpec((1, C), lambda i: (0, 0))
    loss, dx, dx_bf, dg = pl.pallas_call(
        body,
        name=name,
        grid=(R // tm,),
        in_specs=[row, vec, row],
        out_specs=[pl.BlockSpec((1, 1), lambda i: (0, 0)), row, row, vec],
        out_shape=[jax.ShapeDtypeStruct((1, 1), F32), jax.ShapeDtypeStruct((R, C), F32), jax.ShapeDtypeStruct((R, C), BF16),
                   jax.ShapeDtypeStruct((1, C), F32)],
        compiler_params=_params(("arbitrary",)),
    )(x, g.reshape(1, C), target)
    return loss, dx, dx_bf, dg.reshape(C)


def _pool_select(lane, vals):
    out = vals[3]
    for gi in (2, 1, 0):
        out = jnp.where(lane < 64 * (gi + 1), vals[gi], out)
    return out


def _pool_diff(a):
    row, lane = _rows(a.shape), _lanes(a.shape)

    def down(v, k):
        return jnp.where(row >= k, pltpu.roll(v, k, 0), 0.0)

    s2 = a + down(a, 1)
    s4 = s2 + down(s2, 2)
    s8 = s4 + down(s4, 4)
    s16 = s8 + down(s8, 8)
    wsum = _pool_select(lane, (s2, s4, s8, s16))
    win = _pool_select(lane, (2, 4, 8, 16))
    cnt = jnp.minimum(row + 1, win).astype(F32)
    return wsum / cnt - a, cnt


def _pool_diff_t(dd, cnt):
    S = dd.shape[0]
    row, lane = _rows(dd.shape), _lanes(dd.shape)

    def up(v, k):
        return jnp.where(row < S - k, pltpu.roll(v, S - k, 0), 0.0)

    e = dd / cnt
    s2 = e + up(e, 1)
    s4 = s2 + up(s2, 2)
    s8 = s4 + up(s4, 4)
    s16 = s8 + up(s8, 8)
    return _pool_select(lane, (s2, s4, s8, s16)) - dd


def _pool_fwd(rest, wbd, scale, *, name):
    S = rest.shape[0]

    def body(a_ref, w_ref, s_ref, o_ref):
        d, _ = _pool_diff(a_ref[...])
        yp = _dot(d.astype(BF16), w_ref[...], 1, 0)
        o_ref[...] = (yp * s_ref[...]).astype(BF16)

    return pl.pallas_call(
        body,
        name=name,
        grid=(1,),
        in_specs=[
            pl.BlockSpec((S, POOL_W), lambda i: (0, OFF_A // POOL_W)),
            pl.BlockSpec((POOL_W, POOL_W), lambda i: (0, 0)),
            pl.BlockSpec((1, POOL_W), lambda i: (0, 0)),
        ],
        out_specs=pl.BlockSpec((S, POOL_W), lambda i: (0, 0)),
        out_shape=jax.ShapeDtypeStruct((S, POOL_W), BF16),
        compiler_params=_params(("arbitrary",)),
    )(rest, wbd, scale.reshape(1, POOL_W))


def _pool_bwd(rest, wbd, wbd_t, scale, dpa, *, name):
    S = rest.shape[0]

    def body(a_ref, w_ref, wt_ref, s_ref, dpa_ref, da_ref, dw_ref, ds_ref):
        d, cnt = _pool_diff(a_ref[...])
        db = d.astype(BF16)
        yp = _dot(db, w_ref[...], 1, 0)
        dpa_v = dpa_ref[...]
        ds_ref[...] = jnp.sum(dpa_v * yp, axis=0, keepdims=True)
        dyp = (dpa_v * s_ref[...]).astype(BF16)
        dw_ref[...] = _dot(db, dyp, 0, 0)
        dd = _dot(dyp, wt_ref[...], 1, 0)
        da_ref[...] = _pool_diff_t(dd, cnt).astype(BF16)

    full = pl.BlockSpec((S, POOL_W), lambda i: (0, 0))
    sq = pl.BlockSpec((POOL_W, POOL_W), lambda i: (0, 0))
    vec = pl.BlockSpec((1, POOL_W), lambda i: (0, 0))
    return pl.pallas_call(
        body,
        name=name,
        grid=(1,),
        in_specs=[pl.BlockSpec((S, POOL_W), lambda i: (0, OFF_A // POOL_W)), sq, sq, vec, full],
        out_specs=[full, sq, vec],
        out_shape=[
            jax.ShapeDtypeStruct((S, POOL_W), BF16),
            jax.ShapeDtypeStruct((POOL_W, POOL_W), F32),
            jax.ShapeDtypeStruct((1, POOL_W), F32),
        ],
        compiler_params=_params(("arbitrary",)),
    )(rest, wbd, wbd_t, scale.reshape(1, POOL_W), dpa)


def _log_sigmoid(z):
    return jnp.minimum(z, 0.0) - jnp.log(1.0 + jnp.exp(-jnp.abs(z)))


_F_SPEC_COL = OFF_F // F_LANES


def _fox_prep(rest, bpad, *, name):
    S = rest.shape[0]

    def body(f_ref, b_ref, o_ref, ot_ref):
        acc = _log_sigmoid(f_ref[...] + b_ref[...])
        row = _rows(acc.shape)
        k = 1
        while k < S:
            acc = acc + jnp.where(row >= k, pltpu.roll(acc, k, 0), 0.0)
            k *= 2
        o_ref[...] = acc
        ot_ref[...] = acc.T

    return pl.pallas_call(
        body,
        name=name,
        grid=(1,),
        in_specs=[pl.BlockSpec((S, F_LANES), lambda i: (0, _F_SPEC_COL)), pl.BlockSpec((1, F_LANES), lambda i: (0, 0))],
        out_specs=[pl.BlockSpec((S, F_LANES), lambda i: (0, 0)), pl.BlockSpec((F_LANES, S), lambda i: (0, 0))],
        out_shape=[jax.ShapeDtypeStruct((S, F_LANES), F32), jax.ShapeDtypeStruct((F_LANES, S), F32)],
        compiler_params=_params(("arbitrary",)),
    )(rest, bpad)


def _fox_post(rest, bpad, dcum, *, name):
    S = rest.shape[0]

    def body(f_ref, b_ref, d_ref, df_ref, db_ref):
        acc = d_ref[...]
        row = _rows(acc.shape)
        k = 1
        while k < S:
            acc = acc + jnp.where(row < S - k, pltpu.roll(acc, S - k, 0), 0.0)
            k *= 2
        df = acc * (1.0 - _sigmoid(f_ref[...] + b_ref[...]))
        df_ref[...] = df.astype(BF16)
        db_ref[...] = jnp.sum(df, axis=0, keepdims=True)

    full = pl.BlockSpec((S, F_LANES), lambda i: (0, 0))
    vec = pl.BlockSpec((1, F_LANES), lambda i: (0, 0))
    return pl.pallas_call(
        body,
        name=name,
        grid=(1,),
        in_specs=[pl.BlockSpec((S, F_LANES), lambda i: (0, _F_SPEC_COL)), vec, full],
        out_specs=[full, vec],
        out_shape=[jax.ShapeDtypeStruct((S, F_LANES), BF16), jax.ShapeDtypeStruct((1, F_LANES), F32)],
        compiler_params=_params(("arbitrary",)),
    )(rest, bpad, dcum)


_FOX_SCALE = FOX_DH ** -0.5
_PAIRS = FOX_H // 2


def _scaled(v):
    return (v.astype(F32) * _FOX_SCALE).astype(BF16)


def _diag_mask(s):
    return jnp.where(_rows(s.shape) >= _lanes(s.shape), s, NEG)


def _fox_fwd(qkv, cum, fk3, *, name):
    S = qkv.shape[0]
    nk, t = fk3.shape[1:]

    def body(q_ref, k_ref, v_ref, cum_ref, fk_ref, o_ref, lse_ref):
        i = pl.program_id(0)
        lane = _lanes((t, 128))
        lo = lane < FOX_DH
        cumv = cum_ref[...]
        qm, fq = [], []
        for h in range(FOX_H):
            qs = _scaled(q_ref[:, 128 * (h // 2):128 * (h // 2 + 1)])
            zero = jnp.zeros_like(qs)
            qm.append(jnp.where(lo, qs, zero) if h % 2 == 0 else jnp.where(lo, zero, qs))
            fq.append(jnp.broadcast_to(cumv[:, h:h + 1], (t, 128)))

        def tile(j, state, masked):
            m, acc, lsum = (list(part) for part in state)
            k0 = pl.multiple_of(j * t, t)
            for hp in range(_PAIRS):
                cols = slice(128 * hp, 128 * (hp + 1))
                kb = k_ref[pl.ds(k0, t), cols]
                vb = v_ref[pl.ds(k0, t), cols]
                one = jnp.ones_like(vb)
                alphas, pvs = [], []
                for h in (2 * hp, 2 * hp + 1):
                    s = _dot(qm[h], kb, 1, 1) + jnp.concatenate([fq[h]] * (t // 128), axis=1) - fk_ref[h, pl.ds(j, 1), :]
                    if masked:
                        s = _diag_mask(s)
                    m_new = jnp.maximum(m[h], jnp.max(s, axis=-1, keepdims=True))
                    p = jnp.exp(s - m_new)
                    alphas.append(jnp.exp(m[h] - m_new))
                    m[h] = m_new
                    pvs.append(_dot(p.astype(BF16), jnp.where(lo, vb, one) if h % 2 == 0 else jnp.where(lo, one, vb), 1, 0))
                acc[hp] = jnp.where(lo, alphas[0], alphas[1]) * acc[hp] + jnp.where(lo, pvs[0], pvs[1])
                lsum[hp] = jnp.where(lo, alphas[1], alphas[0]) * lsum[hp] + jnp.where(lo, pvs[1], pvs[0])
            return tuple(m), tuple(acc), tuple(lsum)

        zeros = (jnp.zeros((t, 128), F32),) * _PAIRS
        state = lax.fori_loop(0, i, functools.partial(tile, masked=False), ((jnp.full((t, 1), NEG, F32),) * FOX_H, zeros, zeros))
        m, acc, lsum = tile(i, state, True)
        for hp in range(_PAIRS):
            o_ref[:, 128 * hp:128 * (hp + 1)] = acc[hp] / pltpu.roll(lsum[hp], FOX_DH, 1)
            lse = [m[2 * hp] + jnp.log(lsum[hp][:, FOX_DH:FOX_DH + 1]), m[2 * hp + 1] + jnp.log(lsum[hp][:, 0:1])]
            lse_ref[hp] = jnp.where(lane == 0, lse[0], jnp.where(lane == 1, lse[1], 0.0))

    whole = lambda col: pl.BlockSpec((S, FOX_W), lambda i: (0, col))
    return pl.pallas_call(
        body,
        name=name,
        grid=(S // t,),
        in_specs=[
            pl.BlockSpec((t, FOX_W), lambda i: (i, 0)), whole(1), whole(2),
            pl.BlockSpec((t, F_LANES), lambda i: (i, 0)),
            pl.BlockSpec((FOX_H, nk, t), lambda i: (0, 0, 0)),
        ],
        out_specs=[pl.BlockSpec((t, FOX_W), lambda i: (i, 0)), pl.BlockSpec((_PAIRS, t, 128), lambda i: (0, i, 0))],
        out_shape=[jax.ShapeDtypeStruct((S, FOX_W), F32), jax.ShapeDtypeStruct((_PAIRS, S, 128), F32)],
        compiler_params=_params(("arbitrary",)),
    )(qkv, qkv, qkv, cum, fk3)


def _fox_bwd(qkv, cum, fk3, o, do, lse, *, name):
    S = qkv.shape[0]
    nk, t = fk3.shape[1:]
    q_at, k_at, v_at = 0, FOX_W, 2 * FOX_W

    def body(qkv_ref, cum_ref, fk_ref, o_ref, do_ref, lse_ref, dq_ref, dk_ref, dv_ref, dfq_ref, dfk_ref,
             qs_sc, ks_sc, bias_sc, delta_sc, dq_sc):
        lane = _lanes((t, 128))
        lo = lane < FOX_DH
        mine = lambda h: lo if h % 2 == 0 else jnp.logical_not(lo)

        def by_head(tile, values):
            for h, val in enumerate(values):
                tile = jnp.where(lane == h, val, tile)
            return tile

        def prep(i, carry):
            r = pl.ds(pl.multiple_of(i * t, t), t)
            qs_sc[r, :] = _scaled(qkv_ref[r, q_at:q_at + FOX_W])
            ks_sc[r, :] = _scaled(qkv_ref[r, k_at:k_at + FOX_W])
            cum_t = cum_ref[r, :]
            for hp in range(_PAIRS):
                cols = slice(128 * hp, 128 * (hp + 1))
                prod = do_ref[r, cols].astype(F32) * o_ref[r, cols]
                for h in (2 * hp, 2 * hp + 1):
                    delta = jnp.sum(jnp.where(mine(h), prod, 0.0), axis=-1, keepdims=True)
                    delta_sc[h, r, :] = jnp.broadcast_to(delta, (t, 128))
                    bias_sc[h, r, :] = jnp.broadcast_to(cum_t[:, h:h + 1] - lse_ref[hp, r, h % 2:h % 2 + 1], (t, 128))
            dfq_ref[r, :] = jnp.zeros((t, 128), F32)
            dq_sc[r, :] = jnp.zeros((t, FOX_W), F32)
            return carry

        lax.fori_loop(0, nk, prep, 0)

        def kv_tile(j, carry):
            kr = pl.ds(pl.multiple_of(j * t, t), t)

            def q_tile(i, acc, masked):
                dk, dv, dfk = list(acc[:_PAIRS]), list(acc[_PAIRS:2 * _PAIRS]), list(acc[2 * _PAIRS:])
                qr = pl.ds(pl.multiple_of(i * t, t), t)
                dq_old, dfq_old = dq_sc[qr, :], dfq_ref[qr, :]
                wide = lambda a: jnp.concatenate([a] * (t // 128), axis=1)
                row_sums, dq_new = [], []
                for hp in range(_PAIRS):
                    cols = slice(128 * hp, 128 * (hp + 1))
                    kb = qkv_ref[kr, k_at + 128 * hp:k_at + 128 * (hp + 1)]
                    vb = qkv_ref[kr, v_at + 128 * hp:v_at + 128 * (hp + 1)]
                    ksb, qsb, dob = ks_sc[kr, cols], qs_sc[qr, cols], do_ref[qr, cols]
                    zero = jnp.zeros_like(qsb)
                    dq_t = jnp.zeros((t, 128), F32)
                    for h in (2 * hp, 2 * hp + 1):
                        qe, doe, ke = (jnp.where(mine(h), a, zero) for a in (qsb, dob, ksb))
                        s = _dot(qe, kb, 1, 1) + wide(bias_sc[h, qr, :]) - fk_ref[h, pl.ds(j, 1), :]
                        if masked:
                            s = _diag_mask(s)
                        p = jnp.exp(s)
                        dv[hp] = dv[hp] + _dot(p.astype(BF16), doe, 0, 0)
                        dp = _dot(doe, vb, 1, 1)
                        ds = p * (dp - wide(delta_sc[h, qr, :]))
                        dsb = ds.astype(BF16)
                        dk[hp] = dk[hp] + _dot(dsb, qe, 0, 0)
                        dq_t = dq_t + _dot(dsb, ke, 1, 0)
                        row_sums.append(jnp.sum(ds, axis=-1, keepdims=True))
                        dfk[h] = dfk[h] - jnp.sum(ds, axis=0, keepdims=True)
                    dq_new.append(dq_old[:, cols] + dq_t)
                for hp in range(_PAIRS):
                    dq_sc[qr, 128 * hp:128 * (hp + 1)] = dq_new[hp]
                dfq_ref[qr, :] = dfq_old + by_head(jnp.zeros((t, 128), F32), row_sums)
                return (*dk, *dv, *dfk)

            init = tuple([jnp.zeros((t, 128), F32)] * (2 * _PAIRS) + [jnp.zeros((1, t), F32)] * FOX_H)
            acc = q_tile(j, init, True)
            acc = lax.fori_loop(j + 1, nk, functools.partial(q_tile, masked=False), acc)
            for hp in range(_PAIRS):
                cols = slice(128 * hp, 128 * (hp + 1))
                dk_ref[kr, cols] = acc[hp].astype(BF16)
                dv_ref[kr, cols] = acc[_PAIRS + hp].astype(BF16)
            for h in range(FOX_H):
                dfk_ref[h, pl.ds(j, 1), :] = acc[2 * _PAIRS + h]
            return carry

        lax.fori_loop(0, nk, kv_tile, 0)
        dq_ref[...] = dq_sc[...].astype(BF16)

    vm = pl.BlockSpec(memory_space=pltpu.VMEM)
    big = jax.ShapeDtypeStruct((S, FOX_W), BF16)
    return pl.pallas_call(
        body,
        name=name,
        in_specs=[vm] * 6,
        out_specs=[vm] * 5,
        out_shape=[big, big, big, jax.ShapeDtypeStruct((S, 128), F32), jax.ShapeDtypeStruct((FOX_H, nk, t), F32)],
        scratch_shapes=[pltpu.VMEM((S, FOX_W), BF16), pltpu.VMEM((S, FOX_W), BF16), pltpu.VMEM((FOX_H, S, 128), F32),
                        pltpu.VMEM((FOX_H, S, 128), F32), pltpu.VMEM((S, FOX_W), F32)],
        compiler_params=pltpu.CompilerParams(vmem_limit_bytes=VMEM_LIMIT),
    )(qkv, cum, fk3, o, do, lse)


def _group_mask(lane, gi):
    return (lane >= 64 * gi) & (lane < 64 * (gi + 1))


_U_COL = OFF_C // SGU_W


def _sgu_fwd(rest, gn, wm, bias, *, name):
    S = rest.shape[0]
    ts = _tile(S, 512)
    nc = ts // SGU_CHUNK

    def body(u_ref, v_ref, g_ref, w_ref, b_ref, o_ref):
        zv = _gelu(v_ref[...])
        vn = zv * lax.rsqrt(jnp.mean(zv * zv, axis=-1, keepdims=True) + EPS) * g_ref[...]
        lane = _lanes((SGU_CHUNK, SGU_W))
        for c in range(nc):
            rows = slice(c * SGU_CHUNK, (c + 1) * SGU_CHUNK)
            vcb = vn[rows].astype(BF16)
            mixed = b_ref[...]
            for gi in range(4):
                mixed = mixed + jnp.where(_group_mask(lane, gi), _dot(w_ref[gi], vcb, 1, 0), 0.0)
            o_ref[rows, :] = (_gelu(u_ref[rows, :]) * mixed).astype(BF16)

    return pl.pallas_call(
        body,
        name=name,
        grid=(S // ts,),
        in_specs=[
            pl.BlockSpec((ts, SGU_W), lambda i: (i, _U_COL)),
            pl.BlockSpec((ts, SGU_W), lambda i: (i, _U_COL + 1)),
            pl.BlockSpec((1, SGU_W), lambda i: (0, 0)),
            pl.BlockSpec((4, SGU_CHUNK, SGU_CHUNK), lambda i: (0, 0, 0)),
            pl.BlockSpec((SGU_CHUNK, SGU_W), lambda i: (0, 0)),
        ],
        out_specs=pl.BlockSpec((ts, SGU_W), lambda i: (i, 0)),
        out_shape=jax.ShapeDtypeStruct((S, SGU_W), BF16),
        compiler_params=_params(("parallel",)),
    )(rest, rest, gn.reshape(1, SGU_W), wm, bias)


def _sgu_bwd(rest, gn, wm, wm_t, bias, dsg, *, name):
    S = rest.shape[0]
    ts = _tile(S, 512)
    nc = ts // SGU_CHUNK

    def body(u_ref, v_ref, g_ref, w_ref, wt_ref, b_ref, dsg_ref, dc_ref, dw_ref, db_ref, dg_ref):
        first = pl.program_id(0) == 0

        @pl.when(first)
        def _():
            dw_ref[...] = jnp.zeros_like(dw_ref)
            db_ref[...] = jnp.zeros_like(db_ref)
            dg_ref[...] = jnp.zeros_like(dg_ref)

        gv = g_ref[...]
        lane = _lanes((SGU_CHUNK, SGU_W))
        for c in range(nc):
            rows = slice(c * SGU_CHUNK, (c + 1) * SGU_CHUNK)
            vpre = v_ref[rows, :]
            upre = u_ref[rows, :]
            zv = _gelu(vpre)
            r = lax.rsqrt(jnp.mean(zv * zv, axis=-1, keepdims=True) + EPS)
            zn = zv * r
            vcb = (zn * gv).astype(BF16)
            mixed = b_ref[...]
            for gi in range(4):
                mixed = mixed + jnp.where(_group_mask(lane, gi), _dot(w_ref[gi], vcb, 1, 0), 0.0)
            zu = _gelu(upre)
            dsg_v = dsg_ref[rows, :]
            dc_ref[rows, :SGU_W] = (dsg_v * mixed * _gelu_grad(upre)).astype(BF16)
            dmixed = dsg_v * zu
            db_ref[...] += dmixed
            dvn = jnp.zeros((SGU_CHUNK, SGU_W), F32)
            for gi in range(4):
                dmg = jnp.where(_group_mask(lane, gi), dmixed, 0.0).astype(BF16)
                dw_ref[gi] += _dot(dmg, vcb, 1, 1)
                dvn = dvn + _dot(wt_ref[gi], dmg, 1, 0)
            dg_ref[...] += jnp.sum(dvn * zn, axis=0, keepdims=True)
            dzn = dvn * gv
            dzv = r * (dzn - zn * jnp.mean(dzn * zn, axis=-1, keepdims=True))
            dc_ref[rows, SGU_W:] = (dzv * _gelu_grad(vpre)).astype(BF16)

    blk = pl.BlockSpec((ts, SGU_W), lambda i: (i, 0))
    vec = pl.BlockSpec((1, SGU_W), lambda i: (0, 0))
    w3 = pl.BlockSpec((4, SGU_CHUNK, SGU_CHUNK), lambda i: (0, 0, 0))
    bsp = pl.BlockSpec((SGU_CHUNK, SGU_W), lambda i: (0, 0))
    return pl.pallas_call(
        body,
        name=name,
        grid=(S // ts,),
        in_specs=[
            pl.BlockSpec((ts, SGU_W), lambda i: (i, _U_COL)),
            pl.BlockSpec((ts, SGU_W), lambda i: (i, _U_COL + 1)),
            vec, w3, w3, bsp, blk,
        ],
        out_specs=[pl.BlockSpec((ts, 2 * SGU_W), lambda i: (i, 0)), w3, bsp, vec],
        out_shape=[
            jax.ShapeDtypeStruct((S, 2 * SGU_W), BF16),
            jax.ShapeDtypeStruct((4, SGU_CHUNK, SGU_CHUNK), F32),
            jax.ShapeDtypeStruct((SGU_CHUNK, SGU_W), F32),
            jax.ShapeDtypeStruct((1, SGU_W), F32),
        ],
        compiler_params=_params(("arbitrary",)),
    )(rest, rest, gn.reshape(1, SGU_W), wm, wm_t, bias, dsg)


_GT = 512
_G0 = OFF_G // _GT


def _gate_specs(tm, col_of):
    specs = [pl.BlockSpec((tm, _GT), functools.partial(lambda k, *ids: (col_of(*ids)[0], _G0 + 2 * k + col_of(*ids)[1]), k)) for k in range(3)]
    specs += [pl.BlockSpec((1, _GT), functools.partial(lambda k, *ids: (0, 2 * k + col_of(*ids)[1]), k)) for k in range(3)]
    return specs


def _merge_fwd(rest, bg, ya, yb, yc, *, name):
    S = rest.shape[0]
    tm = _tile(S, 512)

    def body(g1, g2, g3, b1, b2, b3, ya_ref, yb_ref, yc_ref, o_ref):
        acc = _sigmoid(g1[...] + b1[...]) * ya_ref[...]
        acc = acc + _sigmoid(g2[...] + b2[...]) * yb_ref[...]
        acc = acc + _sigmoid(g3[...] + b3[...]) * yc_ref[...]
        o_ref[...] = acc.astype(BF16)

    blk = pl.BlockSpec((tm, _GT), lambda i, j: (i, j))
    return pl.pallas_call(
        body,
        name=name,
        grid=(S // tm, D // _GT),
        in_specs=_gate_specs(tm, lambda i, j: (i, j)) + [blk, blk, blk],
        out_specs=blk,
        out_shape=jax.ShapeDtypeStruct((S, D), BF16),
        compiler_params=_params(("parallel", "parallel")),
    )(rest, rest, rest, bg, bg, bg, ya, yb, yc)


def _merge_bwd(rest, bg, ya, yb, yc, dm, *, name):
    S = rest.shape[0]
    tm = _tile(S, 512)

    def body(g1, g2, g3, b1, b2, b3, ya_ref, yb_ref, yc_ref, dm_ref, dya, dyb, dyc, dg1, dg2, dg3, db1, db2, db3):
        first = pl.program_id(1) == 0
        dmv = dm_ref[...]
        for g_ref, b_ref, y_ref, dy_ref, dg_ref, db_ref in (
            (g1, b1, ya_ref, dya, dg1, db1), (g2, b2, yb_ref, dyb, dg2, db2), (g3, b3, yc_ref, dyc, dg3, db3)):
            gate = _sigmoid(g_ref[...] + b_ref[...])
            dy_ref[...] = (dmv * gate).astype(BF16)
            dpre = dmv * y_ref[...] * gate * (1.0 - gate)
            dg_ref[...] = dpre.astype(BF16)
            part = jnp.sum(dpre, axis=0, keepdims=True)

            @pl.when(first)
            def _():
                db_ref[...] = part

            @pl.when(jnp.logical_not(first))
            def _():
                db_ref[...] += part

    blk = pl.BlockSpec((tm, _GT), lambda j, i: (i, j))
    vec = pl.BlockSpec((1, _GT), lambda j, i: (0, j))
    big = jax.ShapeDtypeStruct((S, D), BF16)
    small = jax.ShapeDtypeStruct((1, D), F32)
    return pl.pallas_call(
        body,
        name=name,
        grid=(D // _GT, S // tm),
        in_specs=_gate_specs(tm, lambda j, i: (i, j)) + [blk, blk, blk, blk],
        out_specs=[blk] * 6 + [vec] * 3,
        out_shape=[big] * 6 + [small] * 3,
        compiler_params=_params(("parallel", "arbitrary")),
    )(rest, rest, rest, bg, bg, bg, ya, yb, yc, dm)


_X_SCALE = XDH ** -0.5


def _xattn_fwd(xq, kv, *, name):
    S = xq.shape[0]
    M = kv.shape[0]
    tq = _tile(S, 512)

    def body(q_ref, k_ref, v_ref, o_ref):
        s = _dot(q_ref[...], k_ref[...], 1, 1) * _X_SCALE
        e = jnp.exp(s - jnp.max(s, axis=-1, keepdims=True))
        p = e / jnp.sum(e, axis=-1, keepdims=True)
        o_ref[...] = _dot(p.astype(BF16), v_ref[...], 1, 0).astype(BF16)

    return pl.pallas_call(
        body,
        name=name,
        grid=(S // tq, XH),
        in_specs=[
            pl.BlockSpec((tq, XDH), lambda i, h: (i, h)),
            pl.BlockSpec((M, XDH), lambda i, h: (0, h)),
            pl.BlockSpec((M, XDH), lambda i, h: (0, XH + h)),
        ],
        out_specs=pl.BlockSpec((tq, XDH), lambda i, h: (i, h)),
        out_shape=jax.ShapeDtypeStruct((S, D), BF16),
        compiler_params=_params(("parallel", "parallel")),
    )(xq, kv, kv)


def _xattn_bwd(xq, kv, do, *, name):
    S = xq.shape[0]
    M = kv.shape[0]
    tq = _tile(S, 512)

    def body(q_ref, k_ref, v_ref, do_ref, dq_ref, dk_ref, dv_ref):
        qb = q_ref[...]
        kb = k_ref[...]
        dob = do_ref[...]
        s = _dot(qb, kb, 1, 1) * _X_SCALE
        e = jnp.exp(s - jnp.max(s, axis=-1, keepdims=True))
        p = e / jnp.sum(e, axis=-1, keepdims=True)
        dp = _dot(dob, v_ref[...], 1, 1)
        ds = (p * (dp - jnp.sum(p * dp, axis=-1, keepdims=True)) * _X_SCALE).astype(BF16)
        dq_ref[...] = _dot(ds, kb, 1, 0).astype(BF16)
        dk_part = _dot(ds, qb, 0, 0)
        dv_part = _dot(p.astype(BF16), dob, 0, 0)

        @pl.when(pl.program_id(1) == 0)
        def _():
            dk_ref[...] = dk_part
            dv_ref[...] = dv_part

        @pl.when(pl.program_id(1) > 0)
        def _():
            dk_ref[...] += dk_part
            dv_ref[...] += dv_part

    qspec = pl.BlockSpec((tq, XDH), lambda h, i: (i, h))
    kspec = pl.BlockSpec((M, XDH), lambda h, i: (0, h))
    dxq, dxk, dxv = pl.pallas_call(
        body,
        name=name,
        grid=(XH, S // tq),
        in_specs=[qspec, kspec, pl.BlockSpec((M, XDH), lambda h, i: (0, XH + h)), qspec],
        out_specs=[qspec, kspec, kspec],
        out_shape=[jax.ShapeDtypeStruct((S, D), BF16), jax.ShapeDtypeStruct((M, D), F32), jax.ShapeDtypeStruct((M, D), F32)],
        compiler_params=_params(("parallel", "arbitrary")),
    )(xq, kv, kv, do)
    return dxq, jnp.concatenate([dxk, dxv], axis=1)


def _adam_math(w, g, m, v):
    m = ADAM_B1 * m + (1.0 - ADAM_B1) * g
    v = ADAM_B2 * v + (1.0 - ADAM_B2) * (g * g)
    m_hat = m / (1.0 - ADAM_B1 ** ADAM_STEP)
    v_hat = v / (1.0 - ADAM_B2 ** ADAM_STEP)
    delta = -ADAM_LR * (m_hat / (jnp.sqrt(v_hat) + ADAM_EPS) + ADAM_WD * w)
    return delta, m, v


def _adamw_sharded(parts, w, m, v, *, name):
    _, R, C = w.shape
    Cp = parts[0].shape[2]
    tm = _tile(R, 256)
    nr = R // tm

    def body(p0_ref, p1_ref, w_ref, m_ref, v_ref, g_ref, d_ref, mo_ref, vo_ref):
        def update(p_ref):
            g = p_ref[0][:, :C].astype(F32)
            for dev in range(1, N_DEV):
                g = g + p_ref[dev][:, :C].astype(F32)
            delta, mn, vn = _adam_math(w_ref[...], g, m_ref[...], v_ref[...])
            g_ref[...] = g
            d_ref[...] = delta
            mo_ref[...] = mn
            vo_ref[...] = vn

        @pl.when(pl.program_id(0) == 0)
        def _():
            update(p0_ref)

        @pl.when(pl.program_id(0) == 1)
        def _():
            update(p1_ref)

    p0 = pl.BlockSpec((N_DEV, tm, Cp), lambda l, i: (0, i * (1 - l) + (nr - 1) * l, 0))
    p1 = pl.BlockSpec((N_DEV, tm, Cp), lambda l, i: (0, i * l, 0))
    blk = pl.BlockSpec((None, tm, C), lambda l, i: (l, i, 0))
    sds = jax.ShapeDtypeStruct(w.shape, F32)
    return pl.pallas_call(
        body,
        name=name,
        grid=(DEPTH, nr),
        in_specs=[p0, p1, blk, blk, blk],
        out_specs=[blk] * 4,
        out_shape=[sds] * 4,
        compiler_params=_params(("arbitrary", "arbitrary")),
    )(parts[0], parts[1], w, m, v)


def _adamw_small(g, w, m, v, *, name):
    n = len(g)

    def body(*refs):
        g_refs, w_refs, m_refs, v_refs = (refs[k * n:(k + 1) * n] for k in range(4))
        d_out, m_out, v_out = (refs[(4 + k) * n:(5 + k) * n] for k in range(3))
        for t in range(n):
            delta, mn, vn = _adam_math(w_refs[t][...], g_refs[t][...], m_refs[t][...], v_refs[t][...])
            d_out[t][...] = delta
            m_out[t][...] = mn
            v_out[t][...] = vn

    vm = pl.BlockSpec(memory_space=pltpu.VMEM)
    shapes = [jax.ShapeDtypeStruct(a.shape, F32) for a in w]
    outs = pl.pallas_call(
        body,
        name=name,
        in_specs=[vm] * (4 * n),
        out_specs=[vm] * (3 * n),
        out_shape=shapes * 3,
        compiler_params=pltpu.CompilerParams(vmem_limit_bytes=VMEM_LIMIT),
    )(*g, *w, *m, *v)
    return outs[:n], outs[n:2 * n], outs[2 * n:]


def _position():
    return lax.axis_index("x"), lax.axis_index("y"), lax.axis_index("c")


def _dev_index(px, py, pc):
    return 4 * px + 2 * py + pc


_ANY = pl.BlockSpec(memory_space=pl.ANY)


def _peers(x, y, c):
    out = []
    for mask in range(1, N_DEV):
        fx, fy, fc = (mask >> 2) & 1, (mask >> 1) & 1, mask & 1
        out.append((1 - x if fx else x, 1 - y if fy else y, 1 - c if fc else c))
    return out


_HBM = pl.BlockSpec(memory_space=pltpu.HBM)
_SEM = pl.BlockSpec(memory_space=pltpu.SEMAPHORE)


def _own_block_placed(block, like):
    x, y, c = _position()
    return lax.dynamic_update_index_in_dim(lax.empty(like.shape, like.dtype), block, _dev_index(x, y, c), 0)


_COPY_BYTES = 256 << 10
_MAX_PIECES = 8


def _pieces(blocks):
    out = []
    for t, b in enumerate(blocks):
        R, C = b.shape[-2:]
        n = max(1, min(_MAX_PIECES, R * C * jnp.dtype(b.dtype).itemsize // _COPY_BYTES))
        while n > 1 and R % (16 * n):
            n -= 1
        out += [(t, pl.ds(j * (R // n), R // n) if n > 1 else None) for j in range(n)]
    return out


def _cut(block, rows):
    return block if rows is None else block.at[rows]


def _copies(per_piece):
    def mark(fn):
        fn.per_piece = per_piece
        return fn
    return mark


@_copies(N_DEV - 1)
def _plan_exchange(srcs, lands, send_sems, recv_sems, arrivals):
    x, y, c = _position()
    me = _dev_index(x, y, c)
    out = []
    for k, peer in enumerate(_peers(x, y, c)):
        p = _dev_index(*peer)
        for i, (t, rows) in enumerate(_pieces(lands)):
            sems = dict(send_sem=send_sems.at[7 * i + k], recv_sem=recv_sems.at[7 * i + k], device_id=peer, device_id_type=MESH)
            src, dst = (lands[t].at[p], lands[t].at[p]) if arrivals else (srcs[t].at[p], lands[t].at[me])
            out.append(pltpu.make_async_remote_copy(src_ref=_cut(src, rows), dst_ref=_cut(dst, rows), **sems))
    return out


@_copies(N_DEV - 1)
def _plan_broadcast(srcs, lands, send_sems, recv_sems, arrivals):
    x, y, c = _position()
    me = _dev_index(x, y, c)
    out = []
    for k, peer in enumerate(_peers(x, y, c)):
        p = _dev_index(*peer)
        for i, (t, rows) in enumerate(_pieces(lands)):
            sems = dict(send_sem=send_sems.at[7 * i + k], recv_sem=recv_sems.at[7 * i + k], device_id=peer, device_id_type=MESH)
            src, dst = (lands[t].at[p], lands[t].at[p]) if arrivals else (srcs[t], lands[t].at[me])
            out.append(pltpu.make_async_remote_copy(src_ref=_cut(src, rows), dst_ref=_cut(dst, rows), **sems))
    return out


@_copies(4)
def _plan_gather_out(srcs, lands, send_sems, recv_sems, arrivals):
    x, y, c = _position()
    me = _dev_index(x, y, c)
    out = []
    for k, peer in enumerate([(x, y, 1 - c), (1 - x, y, c), (x, 1 - y, c), (1 - x, 1 - y, c)]):
        p = _dev_index(*peer)
        for i, (t, rows) in enumerate(_pieces(lands)):
            sems = dict(send_sem=send_sems.at[4 * i + k], recv_sem=recv_sems.at[4 * i + k], device_id=peer, device_id_type=MESH)
            src, dst = (lands[t].at[p], lands[t].at[p]) if arrivals else (srcs[t], lands[t].at[me])
            out.append(pltpu.make_async_remote_copy(src_ref=_cut(src, rows), dst_ref=_cut(dst, rows), **sems))
    return out


@_copies(3)
def _plan_gather_pass(srcs, lands, send_sems, recv_sems, arrivals):
    x, y, c = _position()
    sibling = (x, y, 1 - c)
    out = []
    for k, chip in enumerate([(1 - x, y), (x, 1 - y), (1 - x, 1 - y)]):
        p = _dev_index(*chip, 1 - c) if arrivals else _dev_index(*chip, c)
        for i, (t, rows) in enumerate(_pieces(lands)):
            sems = dict(send_sem=send_sems.at[3 * i + k], recv_sem=recv_sems.at[3 * i + k], device_id=sibling, device_id_type=MESH)
            block = _cut(lands[t].at[p], rows)
            out.append(pltpu.make_async_remote_copy(src_ref=block, dst_ref=block, **sems))
    return out


def _split_start(plan, srcs, lands, *, after=None, name):
    n_src, n = len(srcs), len(srcs) + len(lands)
    n_sem = plan.per_piece * len(_pieces(lands))
    order = [] if after is None else [after]

    def body(*refs):
        send_sems, recv_sems = refs[n + len(order):n + len(order) + 2]
        token = refs[-1]
        for cp in plan(refs[:n_src], refs[n_src:n], send_sems, recv_sems, arrivals=False):
            cp.start()
        token[...] = jnp.zeros_like(token)

    hbm = lambda a: pltpu.HBM(a.shape, a.dtype)
    outs = pl.pallas_call(
        body,
        name=name,
        in_specs=[_HBM] * n + [_ANY] * len(order),
        out_specs=[_SEM, _SEM] + [_HBM] * n + [pl.BlockSpec(memory_space=pltpu.VMEM)],
        out_shape=[pltpu.SemaphoreType.DMA((n_sem,)), pltpu.SemaphoreType.DMA((n_sem,))] + [hbm(a) for a in (*srcs, *lands)]
        + [jax.ShapeDtypeStruct(_TOKEN, F32)],
        input_output_aliases={i: 2 + i for i in range(n)},
        compiler_params=pltpu.CompilerParams(has_side_effects=pltpu.SideEffectType.DATAFLOW_SIDE_EFFECTING),
    )(*[pltpu.with_memory_space_constraint(a, pltpu.HBM) for a in (*srcs, *lands)], *order)
    return (outs[0], outs[1], outs[2:2 + n_src], outs[2 + n_src:2 + n]), outs[-1]


def _split_wait(plan, state, after, *, name):
    send_sems, recv_sems, srcs, lands = state
    n_src, n = len(srcs), len(srcs) + len(lands)

    def body(*refs):
        send_refs, recv_refs = refs[n:n + 2]
        for cp in plan(refs[:n_src], refs[n_src:n], send_refs, recv_refs, arrivals=False):
            cp.wait_send()
        for cp in plan(refs[:n_src], refs[n_src:n], send_refs, recv_refs, arrivals=True):
            cp.wait_recv()

    hbm = lambda a: pltpu.HBM(a.shape, a.dtype)
    outs = pl.pallas_call(
        body,
        name=name,
        in_specs=[_HBM] * n + [_SEM, _SEM, _ANY],
        out_specs=[_HBM] * n,
        out_shape=[hbm(a) for a in (*srcs, *lands)],
        input_output_aliases={i: i for i in range(n)},
        compiler_params=pltpu.CompilerParams(has_side_effects=pltpu.SideEffectType.DATAFLOW_SIDE_EFFECTING),
    )(*srcs, *lands, send_sems, recv_sems, after)
    return outs[n_src:]


def _sum_blocks(blocks, *, name):
    _, R, C = blocks.shape
    tm = next(R // n for n in (4, 3, 2, 1) if R % (8 * n) == 0)

    def body(b_ref, o_ref):
        g = b_ref[0]
        for dev in range(1, N_DEV):
            g = g + b_ref[dev]
        o_ref[...] = g

    return pl.pallas_call(
        body,
        name=name,
        grid=(R // tm,),
        in_specs=[pl.BlockSpec((N_DEV, tm, C), lambda i: (0, i, 0))],
        out_specs=pl.BlockSpec((tm, C), lambda i: (i, 0)),
        out_shape=jax.ShapeDtypeStruct((R, C), F32),
        compiler_params=_params(("parallel",)),
    )(blocks)


def _block_diag(w):
    out = jnp.zeros((POOL_W, POOL_W), w.dtype)
    for gi in range(4):
        out = out.at[64 * gi:64 * (gi + 1), 64 * gi:64 * (gi + 1)].set(w[gi])
    return out


def _layer_consts(sp, l):
    causal = jnp.tril(jnp.ones((SGU_CHUNK, SGU_CHUNK), F32))
    wm = (sp["sgu_w"][l] * causal[None]).astype(BF16)
    wbd = _block_diag(sp["pool_w"][l]).astype(BF16)
    return dict(
        wbd=wbd, wbd_t=wbd.T, wm=wm, wm_t=wm.transpose(0, 2, 1),
        sgu_bias=jnp.repeat(sp["sgu_b"][l].T, 64, axis=1),
        bpad=jnp.pad(sp["b_forget"][l], (0, F_LANES - FOX_H)).reshape(1, F_LANES),
        bg=sp["b_gate"][l].reshape(1, 3 * D),
    )


def _relu2(acc):
    return acc, jnp.square(jnp.maximum(acc, 0.0))


def _relu2_grad(acc, z):
    return (acc * 2.0 * jnp.maximum(z, 0.0),)


def _layer_fwd(l, x, h, mem, source, sp):
    S = x.shape[0]
    t = _tile(S, 256)
    c = _layer_consts(sp, l)
    n = f"l{l}_"
    W, after = source(l, "begin", x)
    if h is None:
        h, after = _rms_fwd(x, sp["norm_mix_g"][l], after=after, name=n + "norm_mix"), None
    hm = _rms_fwd(mem, sp["norm_mem_g"][l], name=n + "norm_mem")
    more, token = source(l, "normed", hm)
    W.update(more)
    qkv = _mm(h, W["qkv"], out_dtypes=(BF16,), after=after if token is None else token, name=n + "qkv")
    rest = _mm(h, W["rest"], name=n + "rest")
    pa = _pool_fwd(rest, c["wbd"], sp["pool_scale"][l], name=n + "pool")
    cum, cum_t = _fox_prep(rest, c["bpad"], name=n + "fox_prep")
    fk3 = cum_t[:FOX_H].reshape(FOX_H, S // t, t)
    o, lse = _fox_fwd(qkv, cum, fk3, name=n + "fox")
    more, _ = source(l, "attended", o)
    W.update(more)
    sg = _sgu_fwd(rest, sp["sgu_norm_g"][l], c["wm"], c["sgu_bias"], name=n + "sgu")
    more, after = source(l, "mixed", sg)
    W.update(more)
    ya = _mm(pa, W["ba"], out_dtypes=(BF16,), after=after, name=n + "branch_a")
    yb = _mm(o, W["bb"], out_dtypes=(BF16,), name=n + "branch_b")
    yc = _mm(sg, W["bc"], out_dtypes=(BF16,), name=n + "branch_c")
    merged = _merge_fwd(rest, c["bg"], ya, yb, yc, name=n + "merge")
    whole_rows = dict(epilogue=_add_norm, out_dtypes=(F32, BF16), tm=1024, tn=D)
    x1, hx = _mm(merged, W["out"], extras=(x,), row_extras=(sp["norm_xattn_g"][l].reshape(1, D),), name=n + "out", **whole_rows)
    xq = _mm(hx, W["xq"], out_dtypes=(BF16,), name=n + "xq")
    kv = _mm(hm, W["xkv"], out_dtypes=(BF16,), name=n + "xkv")
    o2 = _xattn_fwd(xq, kv, name=n + "xattn")
    x2, hf = _mm(o2, W["xo"], extras=(x1,), row_extras=(sp["norm_ffn_g"][l].reshape(1, D),), name=n + "xo", **whole_rows)
    z, act = _mm(hf, W["ff1"], epilogue=_relu2, out_dtypes=(BF16, BF16), name=n + "ff1")
    _, after = source(l, "expanded", act)
    if l + 1 < DEPTH:
        x3, h_next = _mm(act, W["ff2"], extras=(x2,), row_extras=(sp["norm_mix_g"][l + 1].reshape(1, D),), after=after, name=n + "ff2",
                         **whole_rows)
    else:
        x3, h_next = _mm(act, W["ff2"], extras=(x2,), epilogue=_add, after=after, name=n + "ff2"), None
    saved = dict(x=x, h=h, qkv=qkv, rest=rest, pa=pa, cum=cum, fk3=fk3, o=o, lse=lse, sg=sg, ya=ya, yb=yb, yc=yc,
                 merged=merged, x1=x1, hx=hx, hm=hm, xq=xq, kv=kv, o2=o2, x2=x2, hf=hf, z=z, act=act, c=c)
    return x3, h_next, saved, W


def _layer_bwd(l, dx3, dx3_bf, sv, mem, W, sp, grads_done):
    S = dx3.shape[0]
    c = sv["c"]
    n = f"l{l}b_"
    bf = dict(out_dtypes=(BF16,))
    gw, gs = {}, {}
    gw["ff2"] = _mm(sv["act"], dx3_bf, ta=True, name=n + "dw_ff2", **bf)
    dz = _mm(dx3_bf, W["ff2"], tb=True, extras=(sv["z"],), epilogue=_relu2_grad, name=n + "dz", **bf)
    gw["ff1"] = _mm(sv["hf"], dz, ta=True, shard_out=True, name=n + "dw_ff1", **bf)
    whole_rows = dict(epilogue=_norm_grad, out_dtypes=(F32, BF16, F32), row_outs=1, tm=1024, tn=D)
    gain = lambda key: (sp[key][l].reshape(1, D),)
    dx2, dx2_bf, dg = _mm(dz, W["ff1"], tb=True, extras=(sv["x2"], dx3), row_extras=gain("norm_ffn_g"), name=n + "dhf", **whole_rows)
    gs["norm_ffn_g"] = dg.reshape(D)
    gw["xo"] = _mm(sv["o2"], dx2_bf, ta=True, name=n + "dw_xo", **bf)
    do2 = _mm(dx2_bf, W["xo"], tb=True, name=n + "do2", **bf)
    dxq, dkv = _xattn_bwd(sv["xq"], sv["kv"], do2, name=n + "dxattn")
    gw["xq"] = _mm(sv["hx"], dxq, ta=True, name=n + "dw_xq", **bf)
    gw["xkv"] = _mm(sv["hm"], dkv, ta=True, shard_out=True, name=n + "dw_xkv", **bf)
    dhm = _mm(dkv, W["xkv"], tb=True, name=n + "dhm")
    _, gs["norm_mem_g"] = _rms_bwd(mem, sp["norm_mem_g"][l], dhm, jnp.zeros_like(mem), name=n + "dnorm_mem")
    dx1, dx1_bf, dg = _mm(dxq, W["xq"], tb=True, extras=(sv["x1"], dx2), row_extras=gain("norm_xattn_g"), name=n + "dhx", **whole_rows)
    gs["norm_xattn_g"] = dg.reshape(D)
    after, gw = grads_done(l, gw), {}
    gw["out"] = _mm(sv["merged"], dx1_bf, ta=True, name=n + "dw_out", **bf)
    dm = _mm(dx1_bf, W["out"], tb=True, after=after, name=n + "dmerged")
    dya, dyb, dyc, dg1, dg2, dg3, db1, db2, db3 = _merge_bwd(sv["rest"], c["bg"], sv["ya"], sv["yb"], sv["yc"], dm, name=n + "dmerge")
    gs["b_gate"] = jnp.concatenate([db1, db2, db3], axis=1).reshape(3 * D)
    gw["ba"] = _mm(sv["pa"], dya, ta=True, shard_out=True, name=n + "dw_ba", **bf)
    gw["bb"] = _mm(sv["o"], dyb, ta=True, shard_out=True, name=n + "dw_bb", **bf)
    gw["bc"] = _mm(sv["sg"], dyc, ta=True, shard_out=True, name=n + "dw_bc", **bf)
    after, gw = grads_done(l, gw), {}
    dpa = _mm(dya, W["ba"], tb=True, name=n + "dpa")
    do = _mm(dyb, W["bb"], tb=True, after=after, name=n + "do", **bf)
    dsg = _mm(dyc, W["bc"], tb=True, name=n + "dsg")
    da, dwbd, dscale = _pool_bwd(sv["rest"], c["wbd"], c["wbd_t"], sp["pool_scale"][l], dpa, name=n + "dpool")
    gs["pool_w"] = jnp.stack([dwbd[64 * gi:64 * (gi + 1), 64 * gi:64 * (gi + 1)] for gi in range(4)])
    gs["pool_scale"] = dscale.reshape(POOL_W)
    dq, dk, dv, dfq, dfk = _fox_bwd(sv["qkv"], sv["cum"], sv["fk3"], sv["o"], do, sv["lse"], name=n + "dfox")
    dcum = dfq + jnp.pad(dfk.reshape(FOX_H, S).T, ((0, 0), (0, F_LANES - FOX_H)))
    df, dbf = _fox_post(sv["rest"], c["bpad"], dcum, name=n + "dfox_post")
    gs["b_forget"] = dbf[0, :FOX_H]
    dc, dwm, dbias, dgn = _sgu_bwd(sv["rest"], sp["sgu_norm_g"][l], c["wm"], c["wm_t"], c["sgu_bias"], dsg, name=n + "dsgu")
    gs["sgu_w"] = dwm * jnp.tril(jnp.ones((SGU_CHUNK, SGU_CHUNK), F32))[None]
    gs["sgu_b"] = dbias.reshape(SGU_CHUNK, 4, 64).sum(axis=2).T
    gs["sgu_norm_g"] = dgn.reshape(SGU_W)
    dqkv = [dq, dk, dv]
    drest = [jnp.concatenate([da, df, jnp.zeros((S, OFF_C - OFF_F - F_LANES), BF16), dc], axis=1), dg1, dg2, dg3]
    gw["qkv"] = _mm(sv["h"], dqkv, ta=True, name=n + "dw_qkv", **bf)
    gw["rest"] = _mm(sv["h"], drest, ta=True, name=n + "dw_rest", **bf)
    after = grads_done(l, gw)
    dh = _mm(dqkv, W["qkv"], tb=True, after=after, name=n + "dh_qkv")
    dx, dx_bf, dg = _mm(drest, W["rest"], tb=True, extras=(dh, sv["x"], dx1), row_extras=gain("norm_mix_g"), name=n + "dh",
                        **{**whole_rows, "epilogue": _add_norm_grad, "tm": 512})
    gs["norm_mix_g"] = dg.reshape(D)
    return dx, dx_bf, gs


def _local_step(x, mem, target, sp, source, grads_done):
    saved, Ws, h = [], [], None
    for l in range(DEPTH):
        x, h, sv, W = _layer_fwd(l, x, h, mem, source, sp)
        saved.append(sv)
        Ws.append(W)
    loss, dx, dx_bf, dgf = _final_loss(x, sp["final_norm_g"], target, name="final_loss")
    gss = [None] * DEPTH
    for l in reversed(range(DEPTH)):
        dx, dx_bf, gss[l] = _layer_bwd(l, dx, dx_bf, saved[l], mem, Ws[l], sp, grads_done)
    small = {k: jnp.stack([gss[l][k] for l in range(DEPTH)]) for k in gss[0]}
    small["final_norm_g"] = dgf
    return loss, dx, small


_SMALL = ["norm_mix_g", "b_forget", "pool_w", "pool_scale", "sgu_norm_g", "sgu_w", "sgu_b", "b_gate", "norm_xattn_g",
          "norm_mem_g", "norm_ffn_g", "final_norm_g"]
_COL = {"w_branch_a": "ba", "w_branch_b": "bb", "w_branch_c": "bc", "w_xkv": "xkv", "w_ff1": "ff1"}
_ROW = {"w_out": "out", "w_xq": "xq", "w_xo": "xo", "w_ff2": "ff2"}
_BIG = ["w_in", "w_branch_a", "w_branch_b", "w_branch_c", "w_out", "w_xq", "w_xkv", "w_xo", "w_ff1", "w_ff2"]
_PACK_LANES = 128


def _as_rows(a):
    return a.reshape(-1, a.shape[-1])


def _pack(tensors):
    rows = []
    for a in tensors:
        flat = a.reshape(-1)
        flat = jnp.pad(flat, (0, (-flat.shape[0]) % (8 * _PACK_LANES)))
        rows.append(flat.reshape(-1, _PACK_LANES))
    n_rows = sum(r.shape[0] for r in rows)
    rows.append(jnp.zeros(((-n_rows) % (8 * N_DEV), _PACK_LANES), F32))
    return jnp.concatenate(rows, axis=0)


def _unpack(packed, like):
    out, r = [], 0
    for a in like:
        size = math.prod(a.shape)
        nr = 8 * (-(-size // (8 * _PACK_LANES)))
        out.append(packed[r:r + nr].reshape(-1)[:size].reshape(a.shape))
        r += nr
    return out


_SHARD_IN = N_IN // N_DEV
_SHARD_IN_PAD = -(-_SHARD_IN // 128) * 128


def _columns(pieces, start, stop):
    out, at = [], 0
    for p in pieces:
        lo, hi = max(start, at), min(stop, at + p.shape[1])
        if lo < hi:
            out.append(p[:, lo - at:hi - at])
        at += p.shape[1]
    return out


def _split_w_in(blocks):
    K = blocks[0].shape[0]
    pad = jnp.zeros((K, OFF_C - OFF_F - FOX_H), blocks[0].dtype)
    cols = functools.partial(_columns, blocks)
    rest = jnp.concatenate(cols(0, R_OFF_Q) + cols(R_OFF_F, R_OFF_C) + [pad] + cols(R_OFF_C, N_IN), axis=1)
    return jnp.concatenate(cols(R_OFF_Q, R_OFF_F), axis=1), rest


def _join_w_in(qkv, rest):
    in_order = [rest[:, :R_OFF_Q], qkv, rest[:, OFF_F:OFF_F + FOX_H], rest[:, OFF_C:]]
    pad = jnp.zeros((qkv.shape[0], _SHARD_IN_PAD - _SHARD_IN), qkv.dtype)
    return jnp.stack([jnp.concatenate(_columns(in_order, _SHARD_IN * d, _SHARD_IN * (d + 1)) + [pad], axis=1) for d in range(N_DEV)])


_FIRST = ["w_in"]
_LATER = [k for k in _BIG if k not in _FIRST]


def _layer_weights(gathered):
    W = {}
    if "w_in" in gathered:
        W.update(zip(("qkv", "rest"), _split_w_in([gathered["w_in"][d][:, :_SHARD_IN] for d in range(N_DEV)])))
    for name, key in _COL.items():
        if name in gathered:
            W[key] = _Gathered(gathered[name])
    if "xkv" in W:
        W["xkv"] = W["xkv"].arr.transpose(1, 0, 2).reshape(D, -1)
    for name, key in _ROW.items():
        if name in gathered:
            W[key] = gathered[name].reshape(-1, gathered[name].shape[-1])
    return W


def _grad_blocks(gw):
    parts = {}
    if "qkv" in gw:
        parts["w_in"] = _join_w_in(gw["qkv"], gw["rest"])
    for name, key in _COL.items():
        if key in gw:
            parts[name] = gw[key]
    for name, key in _ROW.items():
        if key in gw:
            parts[name] = gw[key].reshape(N_DEV, -1, gw[key].shape[-1])
    return parts


def kernel(x, mem, norm_mix_g, w_in, b_forget, pool_w, pool_scale, sgu_norm_g, sgu_w, sgu_b, w_branch_a, w_branch_b, w_branch_c, b_gate, w_out, norm_xattn_g, norm_mem_g, w_xq, w_xkv, w_xo, norm_ffn_g, w_ff1, w_ff2, final_norm_g, loss_target, m_norm_mix_g, m_w_in, m_b_forget, m_pool_w, m_pool_scale, m_sgu_norm_g, m_sgu_w, m_sgu_b, m_w_branch_a, m_w_branch_b, m_w_branch_c, m_b_gate, m_w_out, m_norm_xattn_g, m_norm_mem_g, m_w_xq, m_w_xkv, m_w_xo, m_norm_ffn_g, m_w_ff1, m_w_ff2, m_final_norm_g, v_norm_mix_g, v_w_in, v_b_forget, v_pool_w, v_pool_scale, v_sgu_norm_g, v_sgu_w, v_sgu_b, v_w_branch_a, v_w_branch_b, v_w_branch_c, v_b_gate, v_w_out, v_norm_xattn_g, v_norm_mem_g, v_w_xq, v_w_xkv, v_w_xo, v_norm_ffn_g, v_w_ff1, v_w_ff2, v_final_norm_g):
    names = ["norm_mix_g", "w_in", "b_forget", "pool_w", "pool_scale", "sgu_norm_g", "sgu_w", "sgu_b", "w_branch_a", "w_branch_b",
             "w_branch_c", "b_gate", "w_out", "norm_xattn_g", "norm_mem_g", "w_xq", "w_xkv", "w_xo", "norm_ffn_g", "w_ff1", "w_ff2",
             "final_norm_g"]
    w = dict(zip(names, [norm_mix_g, w_in, b_forget, pool_w, pool_scale, sgu_norm_g, sgu_w, sgu_b, w_branch_a, w_branch_b, w_branch_c,
                         b_gate, w_out, norm_xattn_g, norm_mem_g, w_xq, w_xkv, w_xo, norm_ffn_g, w_ff1, w_ff2, final_norm_g]))
    m = dict(zip(names, [m_norm_mix_g, m_w_in, m_b_forget, m_pool_w, m_pool_scale, m_sgu_norm_g, m_sgu_w, m_sgu_b, m_w_branch_a,
                         m_w_branch_b, m_w_branch_c, m_b_gate, m_w_out, m_norm_xattn_g, m_norm_mem_g, m_w_xq, m_w_xkv, m_w_xo,
                         m_norm_ffn_g, m_w_ff1, m_w_ff2, m_final_norm_g]))
    v = dict(zip(names, [v_norm_mix_g, v_w_in, v_b_forget, v_pool_w, v_pool_scale, v_sgu_norm_g, v_sgu_w, v_sgu_b, v_w_branch_a,
                         v_w_branch_b, v_w_branch_c, v_b_gate, v_w_out, v_norm_xattn_g, v_norm_mem_g, v_w_xq, v_w_xkv, v_w_xo,
                         v_norm_ffn_g, v_w_ff1, v_w_ff2, v_final_norm_g]))

    sp = {k: w[k] for k in _SMALL}
    shards = [{k: w[k][l].astype(BF16) for k in _BIG} for l in range(DEPTH)]
    for sh in shards:
        sh["w_in"] = jnp.pad(sh["w_in"], ((0, 0), (0, _SHARD_IN_PAD - _SHARD_IN)))
    me = _dev_index(*_position())

    def gather_out(l, keys, name, after=None):
        srcs = [shards[l][k] for k in keys]
        lands = [_own_block_placed(a, jax.ShapeDtypeStruct((N_DEV, *a.shape), a.dtype)) for a in srcs]
        state, token = _split_start(_plan_gather_out, srcs, lands, after=after, name=name + "_out_start")
        return (keys, name, state), token

    def gather_pass(job, value):
        keys, name, state = job
        lands = _split_wait(_plan_gather_out, state, value, name=name + "_out_wait")
        state, token = _split_start(_plan_gather_pass, [], lands, name=name + "_pass_start")
        return (keys, name, state), token, lands[0]

    def gather_end(job, value):
        keys, name, state = job
        return _layer_weights(dict(zip(keys, _split_wait(_plan_gather_pass, state, value, name=name + "_pass_wait"))))

    jobs = {}

    def source(l, point, value):
        if (l, point) == (0, "begin"):
            jobs["l0_first"], token = gather_out(0, _FIRST, "gather_l0_first")
            return {}, token
        if (l, point) == (0, "normed"):
            jobs["l0_first"], token, arrived = gather_pass(jobs["l0_first"], value)
            jobs["l0"], _ = gather_out(0, _LATER, "gather_l0", after=arrived)
            return gather_end(jobs.pop("l0_first"), token), None
        if (l, point) == (0, "attended"):
            jobs["l0"], _, arrived = gather_pass(jobs["l0"], value)
            jobs["l1_first"], token = gather_out(1, _FIRST, "gather_l1_first", after=arrived)
            jobs["l1"], jobs["token"] = gather_out(1, _LATER, "gather_l1", after=token)
            return {}, None
        if (l, point) == (0, "mixed"):
            return gather_end(jobs.pop("l0"), value), jobs.pop("token")
        if (l, point) == (0, "expanded"):
            jobs["l1_first"], token, _ = gather_pass(jobs["l1_first"], value)
            return {}, token
        if (l, point) == (1, "begin"):
            W = gather_end(jobs.pop("l1_first"), value)
            jobs["l1"], token, _ = gather_pass(jobs["l1"], value)
            return W, token
        if (l, point) == (1, "mixed"):
            return gather_end(jobs.pop("l1"), value), None
        return {}, None

    received = [{} for _ in range(DEPTH)]
    travelling = []

    def grads_done(l, gw):
        blocks = _grad_blocks(gw)
        keys = [k for k in _BIG if k in blocks]
        parts = [blocks[k] for k in keys]
        group = f"exchange_grads_l{l}_" + ("in" if "w_in" in blocks else "merge" if "w_out" in blocks else "mlp")
        lands = [_own_block_placed(lax.dynamic_index_in_dim(p, me, 0, keepdims=False), p) for p in parts]
        state, token = _split_start(_plan_exchange, parts, lands, name=group + "_start")
        travelling.append((l, keys, state, group + "_wait"))
        return token

    loss, dx, small = _local_step(x[0], mem[0], loss_target[0], sp, source, grads_done)
    grads, deltas, new_m, new_v = {}, {}, {}, {}
    like = [loss] + [w[k] for k in _SMALL]
    packed = _pack([loss] + [small[k] for k in _SMALL])
    eighths = packed.reshape(N_DEV, -1, _PACK_LANES)
    own = lambda a: _own_block_placed(lax.dynamic_index_in_dim(a, me, 0, keepdims=False) if a.ndim == 3 else a, eighths)
    scatter, done = _split_start(_plan_exchange, [eighths], [own(eighths)], after=dx, name="small_grads_scatter_start")

    def reduce_small(after):
        mine = _sum_blocks(_split_wait(_plan_exchange, scatter, after, name="small_grads_scatter_wait")[0], name="small_grads_sum")
        return _split_start(_plan_broadcast, [mine], [own(mine)], name="small_grads_gather_start")

    def update_small(state, after):
        total = _split_wait(_plan_broadcast, state, after, name="small_grads_gather_wait")[0].reshape(packed.shape)
        loss_sum, *g_small = _unpack(total, like)
        rows = lambda d: [_as_rows(d[k]) for k in _SMALL]
        outs = _adamw_small([_as_rows(g) for g in g_small], rows(w), rows(m), rows(v), name="adamw_small")
        grads.update(zip(_SMALL, g_small))
        for dst, vals in zip((deltas, new_m, new_v), outs):
            dst.update({k: a.reshape(w[k].shape) for k, a in zip(_SMALL, vals)})
        return loss_sum[0, 0], outs[0][0]

    groups = list(dict.fromkeys(tuple(keys) for _, keys, _, _ in travelling))
    for n_done, group_keys in enumerate(groups):
        if n_done == 1:
            gather, _ = reduce_small(done)
        if n_done == len(groups) - 1:
            loss, done = update_small(gather, done)
        for l, keys, state, wait_name in travelling:
            if tuple(keys) == group_keys:
                received[l].update(zip(keys, _split_wait(_plan_exchange, state, done, name=wait_name)))
        for k in group_keys:
            outs = _adamw_sharded([received[l][k] for l in range(DEPTH)], w[k], m[k], v[k], name="adamw_" + k)
            grads[k], deltas[k], new_m[k], new_v[k] = outs
        done = grads[group_keys[-1]]

    return (loss, dx[None], *[grads[k] for k in names], *[deltas[k] for k in names], *[new_m[k] for k in names],
            *[new_v[k] for k in names])
```
